```python
import jax, jax.numpy as jnp
from jax import lax
import numpy as np

D_MODEL = 1024
BATCH = 8
SEQ = 4096
DEPTH = 1

N_META = 16
SWA_HEADS = 8
SWA_KV_HEADS = 2
SWA_HEAD_DIM = 64
SWA_WIDTH = SWA_HEADS * SWA_HEAD_DIM
SWA_KV_WIDTH = SWA_KV_HEADS * SWA_HEAD_DIM
WINDOW = 128
GLA_HEADS = 4
GLA_WIDTH = D_MODEL - SWA_WIDTH
GLA_V_DIM = GLA_WIDTH // GLA_HEADS
GLA_K_DIM = GLA_V_DIM // 2
GLA_K_WIDTH = GLA_HEADS * GLA_K_DIM
GLA_GATE_RANK = 16
GLA_TAU = 16.0
GLA_CHUNK = 64
D_FF = ((8 * D_MODEL + 3 * 256 - 1) // (3 * 256)) * 256
LN_EPS = 1e-5
RMS_EPS = 1e-6
ALPHA = (2.0 * DEPTH) ** 0.25
BETA = (8.0 * DEPTH) ** -0.25
IN_SPLITS = (SWA_WIDTH, SWA_KV_WIDTH, SWA_KV_WIDTH,
             GLA_K_WIDTH, GLA_K_WIDTH, GLA_WIDTH,
             GLA_WIDTH, GLA_GATE_RANK)
D_IN = int(sum(IN_SPLITS))
NEG_INF = -1e30

kernel_name = "hymba_gla_swa_sink_alibi_deepnorm"


def layer_norm(x, g, b):
    xf = x.astype(jnp.float32)
    mu = jnp.mean(xf, axis=-1, keepdims=True)
    var = jnp.mean(jnp.square(xf - mu), axis=-1, keepdims=True)
    y = (xf - mu) * lax.rsqrt(var + LN_EPS)
    return (y * g.astype(jnp.float32) + b.astype(jnp.float32)).astype(x.dtype)


def rms_norm(x, g):
    xf = x.astype(jnp.float32)
    y = xf * lax.rsqrt(jnp.mean(jnp.square(xf), axis=-1, keepdims=True) + RMS_EPS)
    return (y * g.astype(jnp.float32)).astype(x.dtype)


def alibi_slopes(n_heads):
    return jnp.asarray(2.0 ** (-8.0 * (np.arange(n_heads) + 1) / n_heads), dtype=jnp.float32)


def sink_softmax(scores, sink):
    sink = jnp.broadcast_to(sink, scores.shape[:-1] + (1,)).astype(jnp.float32)
    p = jax.nn.softmax(jnp.concatenate([scores, sink], axis=-1), axis=-1)
    return p[..., :-1]


def sliding_window_gqa(q, k, v, sinks):
    B, L, Hq, dh = q.shape
    Hkv = k.shape[2]
    G = Hq // Hkv
    S = L - N_META
    nb = S // WINDOW
    scale = dh ** -0.5
    slopes = alibi_slopes(Hq).reshape(Hkv, G)
    sink = sinks.astype(jnp.float32).reshape(Hkv, G)
    q = q.reshape(B, L, Hkv, G, dh)
    qm, qr = q[:, :N_META], q[:, N_META:]
    km, kr = k[:, :N_META], k[:, N_META:]
    vm, vr = v[:, :N_META], v[:, N_META:]

    qb = qr.reshape(B, nb, WINDOW, Hkv, G, dh)
    kb = kr.reshape(B, nb, WINDOW, Hkv, dh)
    vb = vr.reshape(B, nb, WINDOW, Hkv, dh)
    pad = ((0, 0), (1, 0), (0, 0), (0, 0), (0, 0))
    kband = jnp.concatenate([jnp.pad(kb, pad)[:, :-1], kb], axis=2)
    vband = jnp.concatenate([jnp.pad(vb, pad)[:, :-1], vb], axis=2)
    s_band = jnp.einsum('bnikgd,bnjkd->bkgnij', qb, kband).astype(jnp.float32) * scale
    s_meta = jnp.einsum('bnikgd,bmkd->bkgnim', qb, km).astype(jnp.float32) * scale
    qi = jnp.arange(nb)[:, None] * WINDOW + jnp.arange(WINDOW)[None, :]
    kj = jnp.arange(nb)[:, None] * WINDOW - WINDOW + jnp.arange(2 * WINDOW)[None, :]
    dist_band = qi[:, :, None] - kj[:, None, :]
    valid = (dist_band >= 0) & (dist_band < WINDOW) & (kj[:, None, :] >= 0)
    dist_meta = (qi + N_META)[:, :, None] - jnp.arange(N_META)[None, None, :]
    sl = slopes[:, :, None, None, None]
    s_band = jnp.where(valid, s_band - sl * dist_band.astype(jnp.float32), NEG_INF)
    s_meta = s_meta - sl * dist_meta.astype(jnp.float32)
    p = sink_softmax(jnp.concatenate([s_meta, s_band], axis=-1), sink[None, :, :, None, None, None])
    p = p.astype(v.dtype)
    o_real = (jnp.einsum('bkgnim,bmkd->bnikgd', p[..., :N_META], vm)
              + jnp.einsum('bkgnij,bnjkd->bnikgd', p[..., N_META:], vband))
    o_real = o_real.reshape(B, S, Hq * dh)

    s_mm = jnp.einsum('bikgd,bjkd->bkgij', qm, km).astype(jnp.float32) * scale
    dist_mm = jnp.arange(N_META)[:, None] - jnp.arange(N_META)[None, :]
    s_mm = jnp.where(dist_mm >= 0, s_mm - slopes[:, :, None, None] * dist_mm.astype(jnp.float32), NEG_INF)
    p_mm = sink_softmax(s_mm, sink[None, :, :, None, None]).astype(v.dtype)
    o_meta = jnp.einsum('bkgij,bjkd->bikgd', p_mm, vm).reshape(B, N_META, Hq * dh)
    return jnp.concatenate([o_meta, o_real], axis=1)


def chunked_gla(q, k, v, log_g):
    B, L, H, dk = q.shape
    dv = v.shape[-1]
    C = GLA_CHUNK
    pad_front = (-L) % C
    padw = ((0, 0), (pad_front, 0), (0, 0), (0, 0))
    q = jnp.pad(q * (dk ** -0.5), padw)
    k = jnp.pad(k, padw)
    v = jnp.pad(v, padw)
    log_g = jnp.pad(log_g, padw)
    Lp = L + pad_front
    n = Lp // C

    def to_chunks(t):
        return t.reshape(B, n, C, H, t.shape[-1]).transpose(1, 0, 3, 2, 4)

    qc, kc, vc = to_chunks(q), to_chunks(k), to_chunks(v)
    bc = jnp.cumsum(to_chunks(log_g).astype(jnp.float32), axis=3)
    causal = jnp.tril(jnp.ones((C, C), dtype=bool))[..., None]

    def step(state, inp):
        qx, kx, vx, bx = inp
        o_inter = jnp.einsum('bhcd,bhde->bhce', qx * jnp.exp(bx), state)
        diff = bx[:, :, :, None, :] - bx[:, :, None, :, :]
        decay = jnp.exp(jnp.where(causal, diff, NEG_INF))
        A = jnp.einsum('bhid,bhjd,bhijd->bhij', qx.astype(jnp.float32), kx.astype(jnp.float32), decay)
        o_intra = jnp.einsum('bhij,bhje->bhie', A, vx.astype(jnp.float32))
        b_last = bx[:, :, -1:, :]
        state = (jnp.exp(b_last[:, :, 0, :])[..., None] * state
                 + jnp.einsum('bhjd,bhje->bhde', kx * jnp.exp(b_last - bx), vx.astype(jnp.float32)))
        return state, o_inter + o_intra

    s0 = jnp.zeros((B, H, dk, dv), dtype=jnp.float32)
    _, o = lax.scan(step, s0, (qc, kc, vc, bc))
    o = o.transpose(1, 0, 3, 2, 4).reshape(B, Lp, H, dv)[:, pad_front:]
    return o.astype(v.dtype)


def hybrid_mixer(h, w_in, b_in, w_gate_lr2, b_gate_lr2, sinks, gla_norm_g, w_out):
    B, L, _ = h.shape
    proj = jnp.einsum('bld,de->ble', h, w_in) + b_in
    cuts = tuple(int(c) for c in np.cumsum(IN_SPLITS)[:-1])
    q_s, k_s, v_s, q_g, k_g, v_g, r_g, g_lr = jnp.split(proj, cuts, axis=-1)
    o_s = sliding_window_gqa(q_s.reshape(B, L, SWA_HEADS, SWA_HEAD_DIM),
                             k_s.reshape(B, L, SWA_KV_HEADS, SWA_HEAD_DIM),
                             v_s.reshape(B, L, SWA_KV_HEADS, SWA_HEAD_DIM), sinks)
    gate_logit = jnp.einsum('blr,rk->blk', g_lr, w_gate_lr2) + b_gate_lr2
    log_g = jax.nn.log_sigmoid(gate_logit.astype(jnp.float32)) / GLA_TAU
    o_g = chunked_gla(q_g.reshape(B, L, GLA_HEADS, GLA_K_DIM),
                      k_g.reshape(B, L, GLA_HEADS, GLA_K_DIM),
                      v_g.reshape(B, L, GLA_HEADS, GLA_V_DIM),
                      log_g.reshape(B, L, GLA_HEADS, GLA_K_DIM))
    o_g = rms_norm(o_g, gla_norm_g).reshape(B, L, GLA_WIDTH) * jax.nn.silu(r_g)
    o = jnp.concatenate([o_s, o_g], axis=-1)
    return jnp.einsum('ble,ed->bld', o, w_out)


def swiglu(h, w_gate, w_up, w_down):
    a = jax.nn.silu(jnp.einsum('bld,df->blf', h, w_gate)) * jnp.einsum('bld,df->blf', h, w_up)
    return jnp.einsum('blf,fd->bld', a, w_down)


def _fwd_setup_inputs(seed: int = 0) -> dict:
    key = jax.random.key(seed)
    ks = jax.random.split(key, 20)
    f32 = jnp.float32
    nrm = lambda k, shape, s: jax.random.normal(k, shape, dtype=f32) * s
    col_scale = np.ones((D_IN,), dtype=np.float32)
    off = np.concatenate([[0], np.cumsum(IN_SPLITS)])
    col_scale[off[2]:off[3]] = BETA
    col_scale[off[5]:off[6]] = BETA
    return {
        "x": nrm(ks[0], (BATCH, SEQ, D_MODEL), 1.0),
        "meta_tokens": nrm(ks[1], (N_META, D_MODEL), 1.0),
        "ln_in_g": 1.0 + nrm(ks[2], (D_MODEL,), 0.02),
        "ln_in_b": nrm(ks[3], (D_MODEL,), 0.02),
        "w_in": nrm(ks[4], (DEPTH, D_MODEL, D_IN), D_MODEL ** -0.5) * jnp.asarray(col_scale),
        "b_in": nrm(ks[5], (DEPTH, D_IN), 0.02),
        "w_gate_lr2": nrm(ks[6], (DEPTH, GLA_GATE_RANK, GLA_K_WIDTH), GLA_GATE_RANK ** -0.5),
        "b_gate_lr2": nrm(ks[7], (DEPTH, GLA_K_WIDTH), 0.1),
        "attn_sinks": nrm(ks[8], (DEPTH, SWA_HEADS), 0.5),
        "gla_norm_g": 1.0 + nrm(ks[9], (DEPTH, GLA_V_DIM), 0.02),
        "w_out": nrm(ks[10], (DEPTH, D_MODEL, D_MODEL), BETA * D_MODEL ** -0.5),
        "ln1_g": 1.0 + nrm(ks[11], (DEPTH, D_MODEL), 0.02),
        "ln1_b": nrm(ks[12], (DEPTH, D_MODEL), 0.02),
        "w_ffn_gate": nrm(ks[13], (DEPTH, D_MODEL, D_FF), D_MODEL ** -0.5),
        "w_ffn_up": nrm(ks[14], (DEPTH, D_MODEL, D_FF), D_MODEL ** -0.5),
        "w_ffn_down": nrm(ks[15], (DEPTH, D_FF, D_MODEL), BETA * D_FF ** -0.5),
        "ln2_g": 1.0 + nrm(ks[16], (DEPTH, D_MODEL), 0.02),
        "ln2_b": nrm(ks[17], (DEPTH, D_MODEL), 0.02),
    }


def _fwd_reference(x, meta_tokens, ln_in_g, ln_in_b, w_in, b_in, w_gate_lr2, b_gate_lr2,
              attn_sinks, gla_norm_g, w_out, ln1_g, ln1_b, w_ffn_gate, w_ffn_up,
              w_ffn_down, ln2_g, ln2_b):
    B = x.shape[0]
    meta = jnp.broadcast_to(meta_tokens[None].astype(x.dtype), (B, N_META, x.shape[-1]))
    h = layer_norm(jnp.concatenate([meta, x], axis=1), ln_in_g, ln_in_b)
    for l in range(DEPTH):
        mix = hybrid_mixer(h, w_in[l], b_in[l], w_gate_lr2[l], b_gate_lr2[l],
                           attn_sinks[l], gla_norm_g[l], w_out[l])
        h = layer_norm(ALPHA * h + mix, ln1_g[l], ln1_b[l])
        ffn = swiglu(h, w_ffn_gate[l], w_ffn_up[l], w_ffn_down[l])
        h = layer_norm(ALPHA * h + ffn, ln2_g[l], ln2_b[l])
    return h[:, N_META:]


import jax as _jax
import jax.numpy as _jnp

TWIN_FORMAT = 'train_step'
FWD_PARAMS = ['x', 'meta_tokens', 'ln_in_g', 'ln_in_b', 'w_in', 'b_in', 'w_gate_lr2', 'b_gate_lr2', 'attn_sinks', 'gla_norm_g', 'w_out', 'ln1_g', 'ln1_b', 'w_ffn_gate', 'w_ffn_up', 'w_ffn_down', 'ln2_g', 'ln2_b']
TWIN_WEIGHTS = ['meta_tokens', 'ln_in_g', 'ln_in_b', 'w_in', 'b_in', 'w_gate_lr2', 'b_gate_lr2', 'attn_sinks', 'gla_norm_g', 'w_out', 'ln1_g', 'ln1_b', 'w_ffn_gate', 'w_ffn_up', 'w_ffn_down', 'ln2_g', 'ln2_b']
TWIN_DIFF_INPUT = 'x'
TWIN_INPUTS = ['x', 'meta_tokens', 'ln_in_g', 'ln_in_b', 'w_in', 'b_in', 'w_gate_lr2', 'b_gate_lr2', 'attn_sinks', 'gla_norm_g', 'w_out', 'ln1_g', 'ln1_b', 'w_ffn_gate', 'w_ffn_up', 'w_ffn_down', 'ln2_g', 'ln2_b', 'loss_target', 'm_meta_tokens', 'm_ln_in_g', 'm_ln_in_b', 'm_w_in', 'm_b_in', 'm_w_gate_lr2', 'm_b_gate_lr2', 'm_attn_sinks', 'm_gla_norm_g', 'm_w_out', 'm_ln1_g', 'm_ln1_b', 'm_w_ffn_gate', 'm_w_ffn_up', 'm_w_ffn_down', 'm_ln2_g', 'm_ln2_b', 'v_meta_tokens', 'v_ln_in_g', 'v_ln_in_b', 'v_w_in', 'v_b_in', 'v_w_gate_lr2', 'v_b_gate_lr2', 'v_attn_sinks', 'v_gla_norm_g', 'v_w_out', 'v_ln1_g', 'v_ln1_b', 'v_w_ffn_gate', 'v_w_ffn_up', 'v_w_ffn_down', 'v_ln2_g', 'v_ln2_b']
TWIN_OUTPUTS = ['loss', 'grad_x', 'grad_meta_tokens', 'grad_ln_in_g', 'grad_ln_in_b', 'grad_w_in', 'grad_b_in', 'grad_w_gate_lr2', 'grad_b_gate_lr2', 'grad_attn_sinks', 'grad_gla_norm_g', 'grad_w_out', 'grad_ln1_g', 'grad_ln1_b', 'grad_w_ffn_gate', 'grad_w_ffn_up', 'grad_w_ffn_down', 'grad_ln2_g', 'grad_ln2_b', 'delta_meta_tokens', 'delta_ln_in_g', 'delta_ln_in_b', 'delta_w_in', 'delta_b_in', 'delta_w_gate_lr2', 'delta_b_gate_lr2', 'delta_attn_sinks', 'delta_gla_norm_g', 'delta_w_out', 'delta_ln1_g', 'delta_ln1_b', 'delta_w_ffn_gate', 'delta_w_ffn_up', 'delta_w_ffn_down', 'delta_ln2_g', 'delta_ln2_b', 'new_m_meta_tokens', 'new_m_ln_in_g', 'new_m_ln_in_b', 'new_m_w_in', 'new_m_b_in', 'new_m_w_gate_lr2', 'new_m_b_gate_lr2', 'new_m_attn_sinks', 'new_m_gla_norm_g', 'new_m_w_out', 'new_m_ln1_g', 'new_m_ln1_b', 'new_m_w_ffn_gate', 'new_m_w_ffn_up', 'new_m_w_ffn_down', 'new_m_ln2_g', 'new_m_ln2_b', 'new_v_meta_tokens', 'new_v_ln_in_g', 'new_v_ln_in_b', 'new_v_w_in', 'new_v_b_in', 'new_v_w_gate_lr2', 'new_v_b_gate_lr2', 'new_v_attn_sinks', 'new_v_gla_norm_g', 'new_v_w_out', 'new_v_ln1_g', 'new_v_ln1_b', 'new_v_w_ffn_gate', 'new_v_w_ffn_up', 'new_v_w_ffn_down', 'new_v_ln2_g', 'new_v_ln2_b']
TWIN_LEAF_KINDS = {'loss': 'loss', 'grad_x': 'grad_x', 'grad_meta_tokens': 'grad_w', 'grad_ln_in_g': 'grad_w', 'grad_ln_in_b': 'grad_w', 'grad_w_in': 'grad_w', 'grad_b_in': 'grad_w', 'grad_w_gate_lr2': 'grad_w', 'grad_b_gate_lr2': 'grad_w', 'grad_attn_sinks': 'grad_w', 'grad_gla_norm_g': 'grad_w', 'grad_w_out': 'grad_w', 'grad_ln1_g': 'grad_w', 'grad_ln1_b': 'grad_w', 'grad_w_ffn_gate': 'grad_w', 'grad_w_ffn_up': 'grad_w', 'grad_w_ffn_down': 'grad_w', 'grad_ln2_g': 'grad_w', 'grad_ln2_b': 'grad_w', 'delta_meta_tokens': 'delta_w', 'delta_ln_in_g': 'delta_w', 'delta_ln_in_b': 'delta_w', 'delta_w_in': 'delta_w', 'delta_b_in': 'delta_w', 'delta_w_gate_lr2': 'delta_w', 'delta_b_gate_lr2': 'delta_w', 'delta_attn_sinks': 'delta_w', 'delta_gla_norm_g': 'delta_w', 'delta_w_out': 'delta_w', 'delta_ln1_g': 'delta_w', 'delta_ln1_b': 'delta_w', 'delta_w_ffn_gate': 'delta_w', 'delta_w_ffn_up': 'delta_w', 'delta_w_ffn_down': 'delta_w', 'delta_ln2_g': 'delta_w', 'delta_ln2_b': 'delta_w', 'new_m_meta_tokens': 'new_m', 'new_m_ln_in_g': 'new_m', 'new_m_ln_in_b': 'new_m', 'new_m_w_in': 'new_m', 'new_m_b_in': 'new_m', 'new_m_w_gate_lr2': 'new_m', 'new_m_b_gate_lr2': 'new_m', 'new_m_attn_sinks': 'new_m', 'new_m_gla_norm_g': 'new_m', 'new_m_w_out': 'new_m', 'new_m_ln1_g': 'new_m', 'new_m_ln1_b': 'new_m', 'new_m_w_ffn_gate': 'new_m', 'new_m_w_ffn_up': 'new_m', 'new_m_w_ffn_down': 'new_m', 'new_m_ln2_g': 'new_m', 'new_m_ln2_b': 'new_m', 'new_v_meta_tokens': 'new_v', 'new_v_ln_in_g': 'new_v', 'new_v_ln_in_b': 'new_v', 'new_v_w_in': 'new_v', 'new_v_b_in': 'new_v', 'new_v_w_gate_lr2': 'new_v', 'new_v_b_gate_lr2': 'new_v', 'new_v_attn_sinks': 'new_v', 'new_v_gla_norm_g': 'new_v', 'new_v_w_out': 'new_v', 'new_v_ln1_g': 'new_v', 'new_v_ln1_b': 'new_v', 'new_v_w_ffn_gate': 'new_v', 'new_v_w_ffn_up': 'new_v', 'new_v_w_ffn_down': 'new_v', 'new_v_ln2_g': 'new_v', 'new_v_ln2_b': 'new_v'}


def _forward(args):
    return _fwd_reference(*[args[k] for k in FWD_PARAMS])


def _output_shape():
    out = _jax.eval_shape(lambda: _forward(_fwd_setup_inputs(0)))
    return out.shape, out.dtype

N_MICROBATCH = 1
ADAM_LR = 0.001
ADAM_B1 = 0.9
ADAM_B2 = 0.999
ADAM_EPS = 1e-08
ADAM_WD = 0.01
ADAM_STEP = 10
PER_EXAMPLE_BATCH_AXIS = {'x': 0, 'loss_target': 0}
SHARED_INPUTS = []
_WEIGHT_DTYPES = {'meta_tokens': _jnp.float32, 'ln_in_g': _jnp.float32, 'ln_in_b': _jnp.float32, 'w_in': _jnp.float32, 'b_in': _jnp.float32, 'w_gate_lr2': _jnp.float32, 'b_gate_lr2': _jnp.float32, 'attn_sinks': _jnp.float32, 'gla_norm_g': _jnp.float32, 'w_out': _jnp.float32, 'ln1_g': _jnp.float32, 'ln1_b': _jnp.float32, 'w_ffn_gate': _jnp.float32, 'w_ffn_up': _jnp.float32, 'w_ffn_down': _jnp.float32, 'ln2_g': _jnp.float32, 'ln2_b': _jnp.float32}
MOMENT_SCALE = {'meta_tokens': 2.617691e-03, 'ln_in_g': 9.203065e-01, 'ln_in_b': 4.165126e-01, 'w_in': 6.401284e-02, 'b_in': 1.009843e-01, 'w_gate_lr2': 9.392546e-03, 'b_gate_lr2': 4.241789e-02, 'attn_sinks': 9.539700e-03, 'gla_norm_g': 1.080883e-01, 'w_out': 6.722886e-02, 'ln1_g': 9.840298e-01, 'ln1_b': 4.233700e-01, 'w_ffn_gate': 3.157439e-02, 'w_ffn_up': 3.059579e-02, 'w_ffn_down': 8.511619e-02, 'ln2_g': 3.205194e+01, 'ln2_b': 7.988131e-01}


def _to_microbatches(a, axis):
    t = _jnp.moveaxis(a, axis, 0)
    t = t.reshape((N_MICROBATCH, t.shape[0] // N_MICROBATCH) + t.shape[1:])
    return _jnp.moveaxis(t, 1, axis + 1)


def setup_inputs(seed: int = 0) -> dict:
    inp = _fwd_setup_inputs(seed)
    key = _jax.random.fold_in(_jax.random.key(seed), 7919)
    shape, _ = _output_shape()
    out = dict(inp)
    out["loss_target"] = _jax.random.normal(_jax.random.fold_in(key, 0), shape, _jnp.float32)
    for i, name in enumerate(TWIN_WEIGHTS):
        w = inp[name].astype(_jnp.float32)
        if MOMENT_SCALE is None:
            s = _jnp.sqrt(_jnp.mean(_jnp.square(w)) + 1e-30)
        else:
            s = MOMENT_SCALE[name]
        km, kv = _jax.random.split(_jax.random.fold_in(key, i + 1))
        out[name] = w
        out["m_" + name] = s * _jax.random.normal(km, w.shape, _jnp.float32)
        out["v_" + name] = (s * s) * _jax.random.uniform(kv, w.shape, _jnp.float32, 0.5, 1.5)
    if N_MICROBATCH > 1:
        for name, axis in PER_EXAMPLE_BATCH_AXIS.items():
            out[name] = _to_microbatches(out[name], axis)
    return {'x': out['x'], 'meta_tokens': out['meta_tokens'], 'ln_in_g': out['ln_in_g'], 'ln_in_b': out['ln_in_b'], 'w_in': out['w_in'], 'b_in': out['b_in'], 'w_gate_lr2': out['w_gate_lr2'], 'b_gate_lr2': out['b_gate_lr2'], 'attn_sinks': out['attn_sinks'], 'gla_norm_g': out['gla_norm_g'], 'w_out': out['w_out'], 'ln1_g': out['ln1_g'], 'ln1_b': out['ln1_b'], 'w_ffn_gate': out['w_ffn_gate'], 'w_ffn_up': out['w_ffn_up'], 'w_ffn_down': out['w_ffn_down'], 'ln2_g': out['ln2_g'], 'ln2_b': out['ln2_b'], 'loss_target': out['loss_target'], 'm_meta_tokens': out['m_meta_tokens'], 'm_ln_in_g': out['m_ln_in_g'], 'm_ln_in_b': out['m_ln_in_b'], 'm_w_in': out['m_w_in'], 'm_b_in': out['m_b_in'], 'm_w_gate_lr2': out['m_w_gate_lr2'], 'm_b_gate_lr2': out['m_b_gate_lr2'], 'm_attn_sinks': out['m_attn_sinks'], 'm_gla_norm_g': out['m_gla_norm_g'], 'm_w_out': out['m_w_out'], 'm_ln1_g': out['m_ln1_g'], 'm_ln1_b': out['m_ln1_b'], 'm_w_ffn_gate': out['m_w_ffn_gate'], 'm_w_ffn_up': out['m_w_ffn_up'], 'm_w_ffn_down': out['m_w_ffn_down'], 'm_ln2_g': out['m_ln2_g'], 'm_ln2_b': out['m_ln2_b'], 'v_meta_tokens': out['v_meta_tokens'], 'v_ln_in_g': out['v_ln_in_g'], 'v_ln_in_b': out['v_ln_in_b'], 'v_w_in': out['v_w_in'], 'v_b_in': out['v_b_in'], 'v_w_gate_lr2': out['v_w_gate_lr2'], 'v_b_gate_lr2': out['v_b_gate_lr2'], 'v_attn_sinks': out['v_attn_sinks'], 'v_gla_norm_g': out['v_gla_norm_g'], 'v_w_out': out['v_w_out'], 'v_ln1_g': out['v_ln1_g'], 'v_ln1_b': out['v_ln1_b'], 'v_w_ffn_gate': out['v_w_ffn_gate'], 'v_w_ffn_up': out['v_w_ffn_up'], 'v_w_ffn_down': out['v_w_ffn_down'], 'v_ln2_g': out['v_ln2_g'], 'v_ln2_b': out['v_ln2_b']}


def _loss(weights, diff, rest, loss_target):
    with _jax.named_scope("forward"):
        args = {**rest, TWIN_DIFF_INPUT: diff, **{k: w.astype(_WEIGHT_DTYPES[k]) for k, w in weights.items()}}
        y = _forward(args)
    with _jax.named_scope("loss_head"):
        err = _jnp.square(y.astype(_jnp.float32) - loss_target)
        return 0.5 * _jnp.sum(_jnp.mean(err, axis=-1)) if err.ndim else 0.5 * err


def _adamw(w, g, m, v):
    m = ADAM_B1 * m + (1.0 - ADAM_B1) * g
    v = ADAM_B2 * v + (1.0 - ADAM_B2) * _jnp.square(g)
    m_hat = m / (1.0 - ADAM_B1 ** ADAM_STEP)
    v_hat = v / (1.0 - ADAM_B2 ** ADAM_STEP)
    delta = -ADAM_LR * (m_hat / (_jnp.sqrt(v_hat) + ADAM_EPS) + ADAM_WD * w)
    return delta, m, v


def reference(x, meta_tokens, ln_in_g, ln_in_b, w_in, b_in, w_gate_lr2, b_gate_lr2, attn_sinks, gla_norm_g, w_out, ln1_g, ln1_b, w_ffn_gate, w_ffn_up, w_ffn_down, ln2_g, ln2_b, loss_target, m_meta_tokens, m_ln_in_g, m_ln_in_b, m_w_in, m_b_in, m_w_gate_lr2, m_b_gate_lr2, m_attn_sinks, m_gla_norm_g, m_w_out, m_ln1_g, m_ln1_b, m_w_ffn_gate, m_w_ffn_up, m_w_ffn_down, m_ln2_g, m_ln2_b, v_meta_tokens, v_ln_in_g, v_ln_in_b, v_w_in, v_b_in, v_w_gate_lr2, v_b_gate_lr2, v_attn_sinks, v_gla_norm_g, v_w_out, v_ln1_g, v_ln1_b, v_w_ffn_gate, v_w_ffn_up, v_w_ffn_down, v_ln2_g, v_ln2_b):
    given = dict(x=x, meta_tokens=meta_tokens, ln_in_g=ln_in_g, ln_in_b=ln_in_b, w_in=w_in, b_in=b_in, w_gate_lr2=w_gate_lr2, b_gate_lr2=b_gate_lr2, attn_sinks=attn_sinks, gla_norm_g=gla_norm_g, w_out=w_out, ln1_g=ln1_g, ln1_b=ln1_b, w_ffn_gate=w_ffn_gate, w_ffn_up=w_ffn_up, w_ffn_down=w_ffn_down, ln2_g=ln2_g, ln2_b=ln2_b, loss_target=loss_target, m_meta_tokens=m_meta_tokens, m_ln_in_g=m_ln_in_g, m_ln_in_b=m_ln_in_b, m_w_in=m_w_in, m_b_in=m_b_in, m_w_gate_lr2=m_w_gate_lr2, m_b_gate_lr2=m_b_gate_lr2, m_attn_sinks=m_attn_sinks, m_gla_norm_g=m_gla_norm_g, m_w_out=m_w_out, m_ln1_g=m_ln1_g, m_ln1_b=m_ln1_b, m_w_ffn_gate=m_w_ffn_gate, m_w_ffn_up=m_w_ffn_up, m_w_ffn_down=m_w_ffn_down, m_ln2_g=m_ln2_g, m_ln2_b=m_ln2_b, v_meta_tokens=v_meta_tokens, v_ln_in_g=v_ln_in_g, v_ln_in_b=v_ln_in_b, v_w_in=v_w_in, v_b_in=v_b_in, v_w_gate_lr2=v_w_gate_lr2, v_b_gate_lr2=v_b_gate_lr2, v_attn_sinks=v_attn_sinks, v_gla_norm_g=v_gla_norm_g, v_w_out=v_w_out, v_ln1_g=v_ln1_g, v_ln1_b=v_ln1_b, v_w_ffn_gate=v_w_ffn_gate, v_w_ffn_up=v_w_ffn_up, v_w_ffn_down=v_w_ffn_down, v_ln2_g=v_ln2_g, v_ln2_b=v_ln2_b)
    weights = {n: given[n] for n in TWIN_WEIGHTS}
    shared = {n: given[n] for n in SHARED_INPUTS}
    per_example = {n: given[n] for n in ['x']}
    grad_fn = _jax.value_and_grad(_loss, argnums=(0, 1))

    def one_microbatch(ex, loss_target):
        ex = dict(ex)
        diff = ex.pop(TWIN_DIFF_INPUT)
        return grad_fn(weights, diff, {**shared, **ex}, loss_target)

    if N_MICROBATCH == 1:
        loss, (grad_w, grad_x) = one_microbatch(per_example, given["loss_target"])
    else:
        def body(carry, xs):
            loss_sum, grad_sum = carry
            l_k, (gw_k, gx_k) = one_microbatch(xs[0], xs[1])
            with _jax.named_scope("update"):
                return (loss_sum + l_k, _jax.tree.map(_jnp.add, grad_sum, gw_k)), gx_k

        init = (_jnp.zeros((), _jnp.float32), _jax.tree.map(_jnp.zeros_like, weights))
        (loss, grad_w), grad_x = _jax.lax.scan(body, init, (per_example, given["loss_target"]))
    with _jax.named_scope("update"):
        delta_w, new_m, new_v = {}, {}, {}
        for n in TWIN_WEIGHTS:
            delta_w[n], new_m[n], new_v[n] = _adamw(weights[n], grad_w[n], given["m_" + n], given["v_" + n])
    return (loss, grad_x, *[grad_w[n] for n in TWIN_WEIGHTS], *[delta_w[n] for n in TWIN_WEIGHTS],
            *[new_m[n] for n in TWIN_WEIGHTS], *[new_v[n] for n in TWIN_WEIGHTS])
```

```python
import functools

import jax
import jax.numpy as jnp
from jax import lax
from jax.experimental import pallas as pl
from jax.experimental.pallas import tpu as pltpu

F32 = jnp.float32
BF16 = jnp.bfloat16
MESH = pl.DeviceIdType.MESH

D = 1024
SEQ = 4096
N_META = 16
SWA_HEADS, SWA_KV_HEADS, DH = 8, 2, 64
WINDOW = 128
GLA_HEADS, DK, DV = 4, 64, 128
GLA_TAU = 16.0
CH = 64
D_FF = 2816
D_IN = 2320
LN_EPS = 1e-5
RMS_EPS = 1e-6
ALPHA = 2.0 ** 0.25
NEG = -1e30
ADAM_LR, ADAM_B1, ADAM_B2, ADAM_EPS, ADAM_WD, ADAM_STEP = 0.001, 0.9, 0.999, 1e-8, 0.01, 10
O_QS, O_KS, O_VS, O_QG, O_KG, O_VG, O_RG, O_LR = 0, 512, 640, 768, 1024, 1280, 1792, 2304

LANE = 128
BLK = WINDOW
D_IN_P = D_IN + LANE - 16
META_OFF = CH - N_META
N_CHIPS = 4
SHARD_ROWS = dict(w_in=D_IN // N_CHIPS, w_out=D // N_CHIPS, w_g=D_FF // N_CHIPS, w_u=D_FF // N_CHIPS,
                  w_d=D_FF // N_CHIPS)
SMALL_ROWS = 32
PACK_ROWS = 3072
HALF = PACK_ROWS // 2
VMEM_CAP_MB = 64


def _lp():
    return SEQ + BLK


def _row_tile(cap):
    lp = _lp()
    return max(t for t in range(16, cap + 1, 16) if lp % t == 0)


def _params(vmem_mb, **kw):
    assert vmem_mb <= VMEM_CAP_MB - 6
    return pltpu.CompilerParams(vmem_limit_bytes=vmem_mb << 20, **kw)


def _seq(n=1):
    return ("arbitrary",) * n


def _const(shape):
    return pl.BlockSpec(shape, lambda *_: (0,) * len(shape), pipeline_mode=pl.Buffered(1))


def _acc(shape):
    return pl.BlockSpec(shape, lambda *_: (0,) * len(shape))


def _rows(tm, width):
    return pl.BlockSpec((tm, width), lambda i: (i, 0))


def _dot(a, b):
    return jnp.dot(a.astype(BF16), b.astype(BF16), preferred_element_type=F32)


def _dot_nt(a, b):
    return lax.dot_general(a.astype(BF16), b.astype(BF16), (((1,), (1,)), ((), ())), preferred_element_type=F32)


def _dot_tn(a, b):
    return lax.dot_general(a.astype(BF16), b.astype(BF16), (((0,), (0,)), ((), ())), preferred_element_type=F32)


def _dot_exact(a, b):
    return jnp.dot(a, b, precision=lax.Precision.HIGHEST, preferred_element_type=F32)


def _ln_stats(x):
    mu = jnp.mean(x, axis=-1, keepdims=True)
    xc = x - mu
    rstd = lax.rsqrt(jnp.mean(xc * xc, axis=-1, keepdims=True) + LN_EPS)
    return xc * rstd, rstd


def _ln_bwd(dy, xhat, rstd, g):
    dxh = dy * g
    return rstd * (dxh - jnp.mean(dxh, axis=-1, keepdims=True) - xhat * jnp.mean(dxh * xhat, axis=-1, keepdims=True))


def _sigmoid(x):
    return 1.0 / (1.0 + jnp.exp(-x))


def _iota(shape, dim):
    return lax.broadcasted_iota(jnp.int32, shape, dim)


def _ln_in_fwd(x, meta_ext, g, b):
    nb = SEQ // BLK

    def body(x_ref, m_ref, g_ref, b_ref, h_ref):
        i = pl.program_id(0)
        xin = jnp.where(i < nb, x_ref[...], m_ref[...])
        xhat, _ = _ln_stats(xin)
        h_ref[...] = xhat * g_ref[...] + b_ref[...]

    return pl.pallas_call(
        body, name="ln_in_fwd", grid=(nb + 1,),
        in_specs=[pl.BlockSpec((BLK, D), lambda i: (jnp.minimum(i, nb - 1), 0)),
                  _const((BLK, D)), _const((1, D)), _const((1, D))],
        out_specs=_rows(BLK, D),
        out_shape=jax.ShapeDtypeStruct((_lp(), D), F32),
        compiler_params=_params(16, dimension_semantics=_seq()),
    )(x, meta_ext, g, b)


def _in_proj(h0, w_in_t, b_in_p, wg2_p, bg2):
    tm = _row_tile(384)
    lp = _lp()
    widths = (512, 128, 128, 256, 256, 512, 512, 128)
    offs = (O_QS, O_KS, O_VS, O_QG, O_KG, O_VG, O_RG, O_LR)

    def body(h_ref, w_ref, b_ref, wg2_ref, bg2_ref, *outs):
        proj = _dot_nt(h_ref[...], w_ref[...]) + b_ref[...]
        for o_ref, off, wd in zip(outs[:8], offs, widths):
            o_ref[...] = proj[:, off:off + wd]
        outs[8][...] = _dot(proj[:, O_LR:O_LR + LANE], wg2_ref[...]) + bg2_ref[...]

    return pl.pallas_call(
        body, name="in_proj", grid=(lp // tm,),
        in_specs=[_rows(tm, D), _const((D_IN_P, D)), _const((1, D_IN_P)), _const((LANE, 256)), _const((1, 256))],
        out_specs=[_rows(tm, w) for w in widths] + [_rows(tm, 256)],
        out_shape=[jax.ShapeDtypeStruct((lp, w), F32) for w in widths] + [jax.ShapeDtypeStruct((lp, 256), F32)],
        compiler_params=_params(40, dimension_semantics=_seq()),
    )(h0, w_in_t, b_in_p, wg2_p, bg2)


def _swa_masks(n):
    nb = SEQ // BLK
    is_meta = n == nb
    ri = _iota((BLK, BLK), 0)
    cj = _iota((BLK, BLK), 1)
    meta_col = ((cj >= META_OFF) & (cj < CH)).astype(jnp.int32)
    meta_q = meta_col * ((cj <= ri) & (ri < CH)).astype(jnp.int32)
    valid_m = jnp.where(is_meta, meta_q, meta_col) > 0
    dist_m = jnp.where(is_meta, ri - cj, n * BLK + ri + CH - cj).astype(F32)
    valid_p = jnp.where((n >= 1) & (n < nb), (cj > ri).astype(jnp.int32), 0) > 0
    dist_p = (ri + BLK - cj).astype(F32)
    valid_c = jnp.where(n < nb, (cj <= ri).astype(jnp.int32), 0) > 0
    dist_c = (ri - cj).astype(F32)
    return (dist_m, dist_p, dist_c), (valid_m, valid_p, valid_c)


def _swa_probs(q_h, ks, sink, slope, dists, valids):
    s = [jnp.where(v, _dot_nt(q_h, k) * (DH ** -0.5) - slope * d, NEG) for k, d, v in zip(ks, dists, valids)]
    m = jnp.maximum(jnp.maximum(jnp.max(s[0], axis=-1, keepdims=True), jnp.max(s[1], axis=-1, keepdims=True)),
                    jnp.maximum(jnp.max(s[2], axis=-1, keepdims=True), sink))
    e = [jnp.exp(x - m) for x in s]
    e_sink = jnp.exp(sink - m)
    inv = 1.0 / (jnp.sum(e[0], axis=-1, keepdims=True) + jnp.sum(e[1], axis=-1, keepdims=True)
                 + jnp.sum(e[2], axis=-1, keepdims=True) + e_sink)
    return e[0] * inv, e[1] * inv, e[2] * inv, e_sink * inv


def _swa_kv_specs(width):
    nb = SEQ // BLK
    return [pl.BlockSpec((BLK, width), lambda n: (nb, 0)),
            pl.BlockSpec((BLK, width), lambda n: (jnp.clip(n - 1, 0, nb - 1), 0)),
            pl.BlockSpec((BLK, width), lambda n: (jnp.minimum(n, nb), 0))]


def _swa_fwd(sinks, qs, ks, vs):
    nb = SEQ // BLK
    g = SWA_HEADS // SWA_KV_HEADS

    def body(sink_ref, q_ref, km_ref, kp_ref, kc_ref, vm_ref, vp_ref, vc_ref, o_ref):
        dists, valids = _swa_masks(pl.program_id(0))
        for h in range(SWA_HEADS):
            kv = slice((h // g) * DH, (h // g + 1) * DH)
            hs = slice(h * DH, (h + 1) * DH)
            p_m, p_p, p_c, _ = _swa_probs(q_ref[:, hs], (km_ref[:, kv], kp_ref[:, kv], kc_ref[:, kv]),
                                          sink_ref[h], 2.0 ** -(h + 1), dists, valids)
            o_ref[:, hs] = _dot(p_m, vm_ref[:, kv]) + _dot(p_p, vp_ref[:, kv]) + _dot(p_c, vc_ref[:, kv])

    kvw = SWA_KV_HEADS * DH
    return pl.pallas_call(
        body, name="swa_fwd", grid=(nb + 1,),
        in_specs=[pl.BlockSpec(memory_space=pltpu.SMEM), _rows(BLK, SWA_HEADS * DH)] + _swa_kv_specs(kvw) + _swa_kv_specs(kvw),
        out_specs=_rows(BLK, SWA_HEADS * DH),
        out_shape=jax.ShapeDtypeStruct((_lp(), SWA_HEADS * DH), F32),
        compiler_params=_params(16, dimension_semantics=_seq()),
    )(sinks, qs, ks, ks, ks, vs, vs, vs)


def _gla_block(t):
    nc = SEQ // CH
    return jnp.where(t == 0, nc, jnp.where(t == nc + 1, nc + 1, t - 1))


def _gla_rowmask(t):
    nc = SEQ // CH
    ri = _iota((CH, 1), 0)
    m = jnp.where(t == 0, (ri >= META_OFF).astype(jnp.int32), jnp.where(t == nc + 1, 0, 1))
    return (m > 0).astype(F32) + jnp.zeros((CH, 1), F32)


def _gla_decay(z, rmask):
    log_g = (jnp.minimum(z, 0.0) - jnp.log1p(jnp.exp(-jnp.abs(z)))) * (rmask / GLA_TAU)
    tril = (_iota((CH, CH), 0) >= _iota((CH, CH), 1)).astype(F32)
    return _dot_exact(tril, log_g), jnp.sum(log_g, axis=0, keepdims=True)


def _gla_fwd(qg, kg, vg, z):
    nc = SEQ // CH
    steps = nc + 2
    kw, vw = GLA_HEADS * DK, GLA_HEADS * DV

    def body(q_ref, k_ref, v_ref, z_ref, o_ref, st_ref, st):
        t = pl.program_id(0)

        @pl.when(t == 0)
        def _():
            st[...] = jnp.zeros_like(st)

        st_prev = st[...]
        st_ref[0] = st_prev
        rmask = _gla_rowmask(t)
        b, b_last = _gla_decay(z_ref[...], rmask)
        q = q_ref[...] * (rmask * DK ** -0.5)
        k = k_ref[...] * rmask
        v = v_ref[...] * rmask
        qe = q * jnp.exp(b)
        ke = k * jnp.exp(-b)
        kd = k * jnp.exp(b_last - b)
        e_last = jnp.exp(b_last)
        causal = _iota((CH, CH), 0) >= _iota((CH, CH), 1)
        for h in range(GLA_HEADS):
            ks, vs_ = slice(h * DK, (h + 1) * DK), slice(h * DV, (h + 1) * DV)
            a = jnp.where(causal, _dot_nt(qe[:, ks], ke[:, ks]), 0.0)
            o_ref[:, vs_] = _dot(a, v[:, vs_]) + _dot_nt(qe[:, ks], st_prev[:, ks])
            st[:, ks] = st_prev[:, ks] * e_last[:, ks] + _dot_tn(v[:, vs_], kd[:, ks])

    blk = lambda w: pl.BlockSpec((CH, w), lambda t: (_gla_block(t), 0))
    return pl.pallas_call(
        body, name="gla_fwd", grid=(steps,),
        in_specs=[blk(kw), blk(kw), blk(vw), blk(kw)],
        out_specs=[blk(vw), pl.BlockSpec((1, DV, kw), lambda t: (t, 0, 0))],
        out_shape=[jax.ShapeDtypeStruct((_lp(), vw), F32), jax.ShapeDtypeStruct((steps, DV, kw), F32)],
        scratch_shapes=[pltpu.VMEM((DV, kw), F32)],
        compiler_params=_params(16, dimension_semantics=_seq()),
    )(qg, kg, vg, z)


def _post_mix(o_s, o_gla, r_g, h0, gn4, w_out, g1, b1):
    tm = _row_tile(384)
    lp = _lp()

    def body(os_ref, og_ref, r_ref, h0_ref, gn_ref, w_ref, g_ref, b_ref, o_ref, pre_ref, h1_ref):
        o_ref[:, 0:512] = os_ref[...].astype(BF16)
        for h in range(GLA_HEADS):
            hs = slice(h * DV, (h + 1) * DV)
            xg = og_ref[:, hs]
            n = xg * lax.rsqrt(jnp.mean(xg * xg, axis=-1, keepdims=True) + RMS_EPS) * gn_ref[...]
            r = r_ref[:, hs]
            o_ref[:, 512 + h * DV:512 + (h + 1) * DV] = (n * (r * _sigmoid(r))).astype(BF16)
        pre = ALPHA * h0_ref[...] + _dot(o_ref[...], w_ref[...])
        pre_ref[...] = pre
        xhat, _ = _ln_stats(pre)
        h1_ref[...] = xhat * g_ref[...] + b_ref[...]

    return pl.pallas_call(
        body, name="post_mix", grid=(lp // tm,),
        in_specs=[_rows(tm, 512), _rows(tm, 512), _rows(tm, 512), _rows(tm, D), _const((1, DV)), _const((D, D)),
                  _const((1, D)), _const((1, D))],
        out_specs=[_rows(tm, D), _rows(tm, D), _rows(tm, D)],
        out_shape=[jax.ShapeDtypeStruct((lp, D), BF16), jax.ShapeDtypeStruct((lp, D), F32),
                   jax.ShapeDtypeStruct((lp, D), F32)],
        compiler_params=_params(32, dimension_semantics=_seq()),
    )(o_s, o_gla, r_g, h0, gn4, w_out, g1, b1)


def _ffn_fwd(h1, wg_t, wu_t, wd):
    tm = _row_tile(192)
    lp = _lp()

    def body(h_ref, wg_ref, wu_ref, wd_ref, g_ref, u_ref, pre_ref):
        h = h_ref[...]
        g = _dot_nt(h, wg_ref[...])
        u = _dot_nt(h, wu_ref[...])
        g_ref[...] = g
        u_ref[...] = u
        pre_ref[...] = ALPHA * h + _dot(g * _sigmoid(g) * u, wd_ref[...])

    return pl.pallas_call(
        body, name="ffn_fwd", grid=(lp // tm,),
        in_specs=[_rows(tm, D), _const((D_FF, D)), _const((D_FF, D)), _const((D_FF, D))],
        out_specs=[_rows(tm, D_FF), _rows(tm, D_FF), _rows(tm, D)],
        out_shape=[jax.ShapeDtypeStruct((lp, D_FF), F32), jax.ShapeDtypeStruct((lp, D_FF), F32),
                   jax.ShapeDtypeStruct((lp, D), F32)],
        compiler_params=_params(48, dimension_semantics=_seq()),
    )(h1, wg_t, wu_t, wd)


def _ln2_loss_bwd(pre2, target, g2, b2):
    nb = SEQ // BLK

    def body(p_ref, t_ref, g_ref, b_ref, dp_ref, loss_ref, dg_ref, db_ref, acc):
        i = pl.program_id(0)

        @pl.when(i == 0)
        def _():
            acc[...] = jnp.zeros_like(acc)
            dg_ref[...] = jnp.zeros_like(dg_ref)
            db_ref[...] = jnp.zeros_like(db_ref)

        real = jnp.where(i < nb, 1.0, 0.0)
        xhat, rstd = _ln_stats(p_ref[...])
        diff = (xhat * g_ref[...] + b_ref[...] - t_ref[...]) * real
        acc[...] += jnp.sum(diff * diff, axis=0, keepdims=True)
        dy = diff * (1.0 / D)
        dp_ref[...] = _ln_bwd(dy, xhat, rstd, g_ref[...])
        dg_ref[...] += jnp.sum(dy * xhat, axis=0, keepdims=True)
        db_ref[...] += jnp.sum(dy, axis=0, keepdims=True)

        @pl.when(i == nb)
        def _():
            loss_ref[...] = jnp.zeros_like(loss_ref) + (0.5 / D) * jnp.sum(acc[...], axis=1, keepdims=True)

    return pl.pallas_call(
        body, name="ln2_loss_bwd", grid=(nb + 1,),
        in_specs=[_rows(BLK, D), pl.BlockSpec((BLK, D), lambda i: (jnp.minimum(i, nb - 1), 0)), _const((1, D)),
                  _const((1, D))],
        out_specs=[_rows(BLK, D), _acc((1, LANE)), _acc((1, D)), _acc((1, D))],
        out_shape=[jax.ShapeDtypeStruct((_lp(), D), F32), jax.ShapeDtypeStruct((1, LANE), F32),
                   jax.ShapeDtypeStruct((1, D), F32), jax.ShapeDtypeStruct((1, D), F32)],
        scratch_shapes=[pltpu.VMEM((1, D), F32)],
        compiler_params=_params(16, dimension_semantics=_seq()),
    )(pre2, target, g2, b2)


def _ffn_bwd(dpre2, g, u, pre1, wg_t, wu_t, wd, g1):
    tm = _row_tile(192)
    lp = _lp()

    def body(dp_ref, g_ref, u_ref, p1_ref, wg_ref, wu_ref, wd_ref, g1_ref, a_ref, dg_ref, du_ref, dp1_ref,
             dg1_ref, db1_ref):
        @pl.when(pl.program_id(0) == 0)
        def _():
            dg1_ref[...] = jnp.zeros_like(dg1_ref)
            db1_ref[...] = jnp.zeros_like(db1_ref)

        dp = dp_ref[...]
        gg, uu = g_ref[...], u_ref[...]
        sg = _sigmoid(gg)
        silu = gg * sg
        da = _dot_nt(dp, wd_ref[...])
        a_ref[...] = (silu * uu).astype(BF16)
        dgate = (da * uu * (sg * (1.0 + gg * (1.0 - sg)))).astype(BF16)
        dup = (da * silu).astype(BF16)
        dg_ref[...] = dgate
        du_ref[...] = dup
        dh1 = ALPHA * dp + _dot(dgate, wg_ref[...]) + _dot(dup, wu_ref[...])
        xhat, rstd = _ln_stats(p1_ref[...])
        dp1_ref[...] = _ln_bwd(dh1, xhat, rstd, g1_ref[...])
        dg1_ref[...] += jnp.sum(dh1 * xhat, axis=0, keepdims=True)
        db1_ref[...] += jnp.sum(dh1, axis=0, keepdims=True)

    return pl.pallas_call(
        body, name="ffn_bwd", grid=(lp // tm,),
        in_specs=[_rows(tm, D), _rows(tm, D_FF), _rows(tm, D_FF), _rows(tm, D), _const((D_FF, D)), _const((D_FF, D)),
                  _const((D_FF, D)), _const((1, D))],
        out_specs=[_rows(tm, D_FF), _rows(tm, D_FF), _rows(tm, D_FF), _rows(tm, D), _acc((1, D)), _acc((1, D))],
        out_shape=[jax.ShapeDtypeStruct((lp, D_FF), BF16)] * 3
        + [jax.ShapeDtypeStruct((lp, D), F32), jax.ShapeDtypeStruct((1, D), F32), jax.ShapeDtypeStruct((1, D), F32)],
        compiler_params=_params(52, dimension_semantics=_seq()),
    )(dpre2, g, u, pre1, wg_t, wu_t, wd, g1)


def _atb(a, b, name):
    lp = _lp()
    tm = _row_tile(384)
    n, w = a.shape[1], b.shape[1]
    bw = 512 if n * w * 4 > (4 << 20) else w

    def body(a_ref, b_ref, o_ref):
        @pl.when(pl.program_id(1) == 0)
        def _():
            o_ref[...] = jnp.zeros_like(o_ref)

        o_ref[...] += _dot_tn(a_ref[...], b_ref[...])

    return pl.pallas_call(
        body, name=name, grid=(w // bw, lp // tm),
        in_specs=[pl.BlockSpec((tm, n), lambda j, k: (k, 0)), pl.BlockSpec((tm, bw), lambda j, k: (k, j))],
        out_specs=pl.BlockSpec((n, bw), lambda j, k: (0, j)),
        out_shape=jax.ShapeDtypeStruct((n, w), F32),
        compiler_params=_params(48, dimension_semantics=_seq(2)),
    )(a, b)


def _out_bwd(dpre1, w_out, o_gla, r_g, gn4):
    tm = _row_tile(384)
    lp = _lp()

    def body(dp_ref, w_ref, og_ref, r_ref, gn_ref, dos_ref, dog_ref, dr_ref, dgn_ref):
        @pl.when(pl.program_id(0) == 0)
        def _():
            dgn_ref[...] = jnp.zeros_like(dgn_ref)

        do = _dot_nt(dp_ref[...], w_ref[...])
        dos_ref[...] = do[:, 0:512]
        gn = gn_ref[...]
        for h in range(GLA_HEADS):
            hs = slice(h * DV, (h + 1) * DV)
            xg = og_ref[:, hs]
            rstd = lax.rsqrt(jnp.mean(xg * xg, axis=-1, keepdims=True) + RMS_EPS)
            nx = xg * rstd
            r = r_ref[:, hs]
            sr = _sigmoid(r)
            d_o = do[:, 512 + h * DV:512 + (h + 1) * DV]
            dr_ref[:, hs] = d_o * (nx * gn) * (sr * (1.0 + r * (1.0 - sr)))
            dn = d_o * (r * sr)
            dgn_ref[...] += jnp.sum(dn * nx, axis=0, keepdims=True)
            dnx = dn * gn
            dog_ref[:, hs] = rstd * (dnx - nx * jnp.mean(dnx * nx, axis=-1, keepdims=True))

    return pl.pallas_call(
        body, name="out_bwd", grid=(lp // tm,),
        in_specs=[_rows(tm, D), _const((D, D)), _rows(tm, 512), _rows(tm, 512), _const((1, DV))],
        out_specs=[_rows(tm, 512), _rows(tm, 512), _rows(tm, 512), _acc((1, DV))],
        out_shape=[jax.ShapeDtypeStruct((lp, 512), F32)] * 3 + [jax.ShapeDtypeStruct((1, DV), F32)],
        compiler_params=_params(32, dimension_semantics=_seq()),
    )(dpre1, w_out, o_gla, r_g, gn4)


def _gla_bwd(qg, kg, vg, z, do_gla, st_all):
    nc = SEQ // CH
    steps = nc + 2
    kw, vw = GLA_HEADS * DK, GLA_HEADS * DV

    def body(q_ref, k_ref, v_ref, z_ref, do_ref, st_ref, dq_ref, dk_ref, dv_ref, dz_ref, dst):
        t = steps - 1 - pl.program_id(0)

        @pl.when(pl.program_id(0) == 0)
        def _():
            dst[...] = jnp.zeros_like(dst)

        rmask = _gla_rowmask(t)
        zz = z_ref[...]
        b, b_last = _gla_decay(zz, rmask)
        e_b, e_nb, e_kd, e_last = jnp.exp(b), jnp.exp(-b), jnp.exp(b_last - b), jnp.exp(b_last)
        q = q_ref[...] * (rmask * DK ** -0.5)
        k = k_ref[...] * rmask
        v = v_ref[...] * rmask
        qe, ke, kd = q * e_b, k * e_nb, k * e_kd
        st_prev = st_ref[0]
        dst_new = dst[...]
        causal = _iota((CH, CH), 0) >= _iota((CH, CH), 1)
        dqe_parts, dke_parts, dkd_parts = [], [], []
        for h in range(GLA_HEADS):
            ks, vs_ = slice(h * DK, (h + 1) * DK), slice(h * DV, (h + 1) * DV)
            d_o = do_ref[:, vs_]
            a = jnp.where(causal, _dot_nt(qe[:, ks], ke[:, ks]), 0.0)
            da = jnp.where(causal, _dot_nt(d_o, v[:, vs_]), 0.0)
            dqe_parts.append(_dot(d_o, st_prev[:, ks]) + _dot(da, ke[:, ks]))
            dke_parts.append(_dot_tn(da, qe[:, ks]))
            dkd_parts.append(_dot(v[:, vs_], dst_new[:, ks]))
            dv_ref[:, vs_] = _dot_tn(a, d_o) + _dot_nt(kd[:, ks], dst_new[:, ks])
            dst[:, ks] = dst_new[:, ks] * e_last[:, ks] + _dot_tn(d_o, qe[:, ks])
        dqe = jnp.concatenate(dqe_parts, axis=1)
        dke = jnp.concatenate(dke_parts, axis=1)
        dkd = jnp.concatenate(dkd_parts, axis=1)
        dq_ref[...] = dqe * e_b * (rmask * DK ** -0.5)
        dk_ref[...] = (dke * e_nb + dkd * e_kd) * rmask
        dkd_kd = dkd * kd
        db = dqe * qe - dke * ke - dkd_kd
        db_last = jnp.sum(dst_new * st_prev, axis=0, keepdims=True) * e_last + jnp.sum(dkd_kd, axis=0, keepdims=True)
        triu = (_iota((CH, CH), 0) <= _iota((CH, CH), 1)).astype(F32)
        dlog_g = _dot_exact(triu, db) + db_last
        dz_ref[...] = dlog_g * (rmask / GLA_TAU) * _sigmoid(-zz)

    blk = lambda w: pl.BlockSpec((CH, w), lambda s: (_gla_block(steps - 1 - s), 0))
    return pl.pallas_call(
        body, name="gla_bwd", grid=(steps,),
        in_specs=[blk(kw), blk(kw), blk(vw), blk(kw), blk(vw), pl.BlockSpec((1, DV, kw), lambda s: (steps - 1 - s, 0, 0))],
        out_specs=[blk(kw), blk(kw), blk(vw), blk(kw)],
        out_shape=[jax.ShapeDtypeStruct((_lp(), kw), F32), jax.ShapeDtypeStruct((_lp(), kw), F32),
                   jax.ShapeDtypeStruct((_lp(), vw), F32), jax.ShapeDtypeStruct((_lp(), kw), F32)],
        scratch_shapes=[pltpu.VMEM((DV, kw), F32)],
        compiler_params=_params(16, dimension_semantics=_seq()),
    )(qg, kg, vg, z, do_gla, st_all)


def _swa_bwd(sinks, qs, ks, vs, do_s):
    nb = SEQ // BLK
    g = SWA_HEADS // SWA_KV_HEADS
    kvw = SWA_KV_HEADS * DH
    scale = DH ** -0.5

    def body(sink_ref, q_ref, km_ref, kp_ref, kc_ref, vm_ref, vp_ref, vc_ref, do_ref,
             dq_ref, dk_ref, dv_ref, dsink_ref, carry_k, carry_v, meta_k, meta_v):
        n = pl.program_id(0)

        @pl.when(n == 0)
        def _():
            for r in (carry_k, carry_v, meta_k, meta_v):
                r[...] = jnp.zeros_like(r)
            dsink_ref[...] = jnp.zeros_like(dsink_ref)

        @pl.when(n <= nb)
        def _():
            dists, valids = _swa_masks(n)
            lane = _iota((1, LANE), 1)
            dk_parts = [[], [], []]
            dv_parts = [[], [], []]
            for kvh in range(SWA_KV_HEADS):
                kv = slice(kvh * DH, (kvh + 1) * DH)
                kb = (km_ref[:, kv], kp_ref[:, kv], kc_ref[:, kv])
                vb = (vm_ref[:, kv], vp_ref[:, kv], vc_ref[:, kv])
                dk_acc = [jnp.zeros((BLK, DH), F32) for _ in range(3)]
                dv_acc = [jnp.zeros((BLK, DH), F32) for _ in range(3)]
                for gi in range(g):
                    h = kvh * g + gi
                    hs = slice(h * DH, (h + 1) * DH)
                    q_h, do_h = q_ref[:, hs], do_ref[:, hs]
                    *p, p_sink = _swa_probs(q_h, kb, sink_ref[h], 2.0 ** -(h + 1), dists, valids)
                    dp = [_dot_nt(do_h, vx) for vx in vb]
                    delta = sum(jnp.sum(px * dx, axis=-1, keepdims=True) for px, dx in zip(p, dp))
                    ds = [px * (dx - delta) for px, dx in zip(p, dp)]
                    dq_ref[:, hs] = scale * sum(_dot(dsx, kx) for dsx, kx in zip(ds, kb))
                    for j in range(3):
                        dk_acc[j] = dk_acc[j] + scale * _dot_tn(ds[j], q_h)
                        dv_acc[j] = dv_acc[j] + _dot_tn(p[j], do_h)
                    dsink_ref[...] += jnp.where(lane == h, -jnp.sum(p_sink * delta, axis=0, keepdims=True), 0.0)
                for j in range(3):
                    dk_parts[j].append(dk_acc[j])
                    dv_parts[j].append(dv_acc[j])
            dk3 = [jnp.concatenate(x, axis=1) for x in dk_parts]
            dv3 = [jnp.concatenate(x, axis=1) for x in dv_parts]
            meta_k[...] += dk3[0]
            meta_v[...] += dv3[0]
            dk_ref[...] = carry_k[...] + dk3[1]
            dv_ref[...] = carry_v[...] + dv3[1]
            carry_k[...] = dk3[2]
            carry_v[...] = dv3[2]

        @pl.when(n == nb + 1)
        def _():
            dk_ref[...] = meta_k[...]
            dv_ref[...] = meta_v[...]

    kv_out = pl.BlockSpec((BLK, kvw), lambda n: (jnp.where(n == nb + 1, nb, jnp.clip(n - 1, 0, nb - 1)), 0))
    qblk = pl.BlockSpec((BLK, SWA_HEADS * DH), lambda n: (jnp.minimum(n, nb), 0))
    return pl.pallas_call(
        body, name="swa_bwd", grid=(nb + 2,),
        in_specs=[pl.BlockSpec(memory_space=pltpu.SMEM), qblk] + _swa_kv_specs(kvw) + _swa_kv_specs(kvw) + [qblk],
        out_specs=[qblk, kv_out, kv_out, _acc((1, LANE))],
        out_shape=[jax.ShapeDtypeStruct((_lp(), SWA_HEADS * DH), F32), jax.ShapeDtypeStruct((_lp(), kvw), F32),
                   jax.ShapeDtypeStruct((_lp(), kvw), F32), jax.ShapeDtypeStruct((1, LANE), F32)],
        scratch_shapes=[pltpu.VMEM((BLK, kvw), F32)] * 4,
        compiler_params=_params(16, dimension_semantics=_seq()),
    )(sinks, qs, ks, ks, ks, vs, vs, vs, do_s)


def _in_bwd(dqs, dks, dvs, dqg, dkg, dvg, drg, dz, dpre1, w_in_t, wg2_p):
    tm = _row_tile(384)
    lp = _lp()
    widths = (512, 128, 128, 256, 256, 512, 512)
    offs = (O_QS, O_KS, O_VS, O_QG, O_KG, O_VG, O_RG)

    def body(*refs):
        parts, (dz_ref, dp1_ref, w_ref, wg2_ref, dproj_ref, dh0_ref, dbin_ref, dbg_ref) = refs[:7], refs[7:]

        @pl.when(pl.program_id(0) == 0)
        def _():
            dbin_ref[...] = jnp.zeros_like(dbin_ref)
            dbg_ref[...] = jnp.zeros_like(dbg_ref)

        for p_ref, off, wd in zip(parts, offs, widths):
            val = p_ref[...]
            dproj_ref[:, off:off + wd] = val.astype(BF16)
            dbin_ref[:, off:off + wd] += jnp.sum(val, axis=0, keepdims=True)
        dz = dz_ref[...]
        dlr = _dot_nt(dz, wg2_ref[...])
        dproj_ref[:, O_LR:O_LR + LANE] = dlr.astype(BF16)
        dbin_ref[:, O_LR:O_LR + LANE] += jnp.sum(dlr, axis=0, keepdims=True)
        dbg_ref[...] += jnp.sum(dz, axis=0, keepdims=True)
        dh0_ref[...] = ALPHA * dp1_ref[...] + _dot(dproj_ref[...], w_ref[...])

    return pl.pallas_call(
        body, name="in_bwd", grid=(lp // tm,),
        in_specs=[_rows(tm, w) for w in widths] + [_rows(tm, 256), _rows(tm, D), _const((D_IN_P, D)), _const((LANE, 256))],
        out_specs=[_rows(tm, D_IN_P), _rows(tm, D), _acc((1, D_IN_P)), _acc((1, 256))],
        out_shape=[jax.ShapeDtypeStruct((lp, D_IN_P), BF16), jax.ShapeDtypeStruct((lp, D), F32),
                   jax.ShapeDtypeStruct((1, D_IN_P), F32), jax.ShapeDtypeStruct((1, 256), F32)],
        compiler_params=_params(40, dimension_semantics=_seq()),
    )(dqs, dks, dvs, dqg, dkg, dvg, drg, dz, dpre1, w_in_t, wg2_p)


def _ln_in_bwd(x, meta_ext, dh0, g):
    nb = SEQ // BLK

    def body(x_ref, m_ref, dh_ref, g_ref, dx_ref, dm_ref, dg_ref, db_ref):
        i = pl.program_id(0)

        @pl.when(i == 0)
        def _():
            dg_ref[...] = jnp.zeros_like(dg_ref)
            db_ref[...] = jnp.zeros_like(db_ref)

        xin = jnp.where(i < nb, x_ref[...], m_ref[...])
        xhat, rstd = _ln_stats(xin)
        dh = dh_ref[...]
        dxin = _ln_bwd(dh, xhat, rstd, g_ref[...])
        dg_ref[...] += jnp.sum(dh * xhat, axis=0, keepdims=True)
        db_ref[...] += jnp.sum(dh, axis=0, keepdims=True)

        @pl.when(i < nb)
        def _():
            dx_ref[...] = dxin

        @pl.when(i == nb)
        def _():
            dm_ref[...] = dxin

    xblk = pl.BlockSpec((BLK, D), lambda i: (jnp.minimum(i, nb - 1), 0))
    return pl.pallas_call(
        body, name="ln_in_bwd", grid=(nb + 1,),
        in_specs=[xblk, _const((BLK, D)), _rows(BLK, D), _const((1, D))],
        out_specs=[xblk, _acc((BLK, D)), _acc((1, D)), _acc((1, D))],
        out_shape=[jax.ShapeDtypeStruct((SEQ, D), F32), jax.ShapeDtypeStruct((BLK, D), F32),
                   jax.ShapeDtypeStruct((1, D), F32), jax.ShapeDtypeStruct((1, D), F32)],
        compiler_params=_params(16, dimension_semantics=_seq()),
    )(x, meta_ext, dh0, g)


def _local_step(x, target, meta_full, ln_in_g, ln_in_b, w_in_t, b_in, wg2, bg2, sinks, gn, w_out, g1, b1,
                wg_t, wu_t, wd, g2, b2):
    row = lambda v: v.reshape(1, -1).astype(F32)
    meta_ext = jnp.pad(meta_full, ((META_OFF, BLK - CH), (0, 0)))
    b_in_p = jnp.pad(row(b_in), ((0, 0), (0, D_IN_P - D_IN)))
    wg2_p = jnp.pad(wg2, ((0, LANE - wg2.shape[0]), (0, 0))).astype(BF16)
    gn4 = row(gn)
    sinks = sinks.reshape(-1).astype(F32)

    h0 = _ln_in_fwd(x, meta_ext, row(ln_in_g), row(ln_in_b))
    qs, ks, vs, qg, kg, vg, rg, glr, z = _in_proj(h0, w_in_t, b_in_p, wg2_p, row(bg2))
    o_s = _swa_fwd(sinks, qs, ks, vs)
    o_gla, st_all = _gla_fwd(qg, kg, vg, z)
    o, pre1, h1 = _post_mix(o_s, o_gla, rg, h0, gn4, w_out, row(g1), row(b1))
    g, u, pre2 = _ffn_fwd(h1, wg_t, wu_t, wd)

    dpre2, loss, dg2, db2 = _ln2_loss_bwd(pre2, target, row(g2), row(b2))
    a, dgate, dup, dpre1, dg1, db1 = _ffn_bwd(dpre2, g, u, pre1, wg_t, wu_t, wd, row(g1))
    dwd = _atb(a, dpre2, "dw_down")
    dwg_t = _atb(dgate, h1, "dw_gate")
    dwu_t = _atb(dup, h1, "dw_up")
    dw_out = _atb(o, dpre1, "dw_out")
    do_s, do_gla, drg, dgn = _out_bwd(dpre1, w_out, o_gla, rg, gn4)
    dqg, dkg, dvg, dz = _gla_bwd(qg, kg, vg, z, do_gla, st_all)
    dqs, dks, dvs, dsinks = _swa_bwd(sinks, qs, ks, vs, do_s)
    dproj, dh0, db_in_p, dbg2 = _in_bwd(dqs, dks, dvs, dqg, dkg, dvg, drg, dz, dpre1, w_in_t, wg2_p)
    dw_in_t = _atb(dproj, h0, "dw_in")
    dwg2_p = _atb(glr, dz, "dw_gate_lr2")
    dx, dmeta_blk, dg_in, db_in_ln = _ln_in_bwd(x, meta_ext, dh0, row(ln_in_g))

    grads = dict(
        w_in=dw_in_t[:D_IN], w_out=dw_out, w_g=dwg_t, w_u=dwu_t, w_d=dwd,
        meta=dmeta_blk[META_OFF:CH], ln_in_g=dg_in, ln_in_b=db_in_ln, ln1_g=dg1, ln1_b=db1, ln2_g=dg2, ln2_b=db2,
        b_in=db_in_p[:, :D_IN], wg2=dwg2_p[:wg2.shape[0]], bg2=dbg2, sinks=dsinks[:, :SWA_HEADS], gn=dgn)
    return loss[0, 0], dx, grads


HBM = pl.BlockSpec(memory_space=pltpu.HBM)


def _place():
    return lax.axis_index("x"), lax.axis_index("y"), lax.axis_index("c")


def _other_chips(x, y):
    return [(1 - x, y), (x, 1 - y), (1 - x, 1 - y)]


def _gather_shards(shards):
    n = len(shards)

    def body(*refs):
        ins, outs = refs[:n], refs[n:2 * n]
        send_sems, recv_sems, local_sems = refs[2 * n:]
        x, y, c = _place()
        mine = 2 * x + y
        chips = _other_chips(x, y)
        started = []
        for a in range(n):
            loc = pltpu.make_async_copy(ins[a], outs[a].at[mine], local_sems.at[a])
            loc.start()
            started.append(loc)
        sends = []
        for a in range(n):
            for j, (px, py) in enumerate(chips):
                cp = pltpu.make_async_remote_copy(ins[a], outs[a].at[mine], send_sems.at[3 * a + j],
                                                  recv_sems.at[3 * a + j], device_id=(px, py, c), device_id_type=MESH)
                cp.start()
                sends.append(cp)
        for a in range(n):
            for j, (px, py) in enumerate(chips):
                pltpu.make_async_remote_copy(ins[a], outs[a].at[2 * px + py], send_sems.at[3 * a + j],
                                             recv_sems.at[3 * a + j], device_id=(px, py, c),
                                             device_id_type=MESH).wait_recv()
        for cp in sends:
            cp.wait_send()
        for loc in started:
            loc.wait()

    return pl.pallas_call(
        body, name="gather_shards",
        in_specs=[HBM] * n, out_specs=[HBM] * n,
        out_shape=[jax.ShapeDtypeStruct((N_CHIPS,) + s.shape, s.dtype) for s in shards],
        scratch_shapes=[pltpu.SemaphoreType.DMA((3 * n,)), pltpu.SemaphoreType.DMA((3 * n,)),
                        pltpu.SemaphoreType.DMA((n,))],
        compiler_params=pltpu.CompilerParams(has_side_effects=True),
    )(*shards)


def _sibling_exchange(pack):
    def body(g_ref, out_ref, send_sems, recv_sems):
        x, y, c = _place()
        theirs = pl.multiple_of((1 - c) * HALF, 8)
        sends = []
        for s in range(N_CHIPS):
            cp = pltpu.make_async_remote_copy(g_ref.at[s, pl.ds(theirs, HALF)], out_ref.at[s], send_sems.at[s],
                                              recv_sems.at[s], device_id=(x, y, 1 - c), device_id_type=MESH)
            cp.start()
            sends.append(cp)
        for cp in sends:
            cp.wait_recv()
        for cp in sends:
            cp.wait_send()

    return pl.pallas_call(
        body, name="sibling_exchange", in_specs=[HBM], out_specs=HBM,
        out_shape=jax.ShapeDtypeStruct((N_CHIPS, HALF, D), F32),
        scratch_shapes=[pltpu.SemaphoreType.DMA((N_CHIPS,)), pltpu.SemaphoreType.DMA((N_CHIPS,))],
        compiler_params=pltpu.CompilerParams(has_side_effects=True),
    )(pack)


def _add_halves(core, pack, recv):
    tr = 512
    per = HALF // tr

    def body(c_ref, a_ref, b_ref, o_ref):
        o_ref[...] = a_ref[...] + b_ref[...]

    return pl.pallas_call(
        body, name="add_halves",
        grid_spec=pltpu.PrefetchScalarGridSpec(
            num_scalar_prefetch=1, grid=(N_CHIPS, per),
            in_specs=[pl.BlockSpec((1, tr, D), lambda s, i, c: (s, c[0] * per + i, 0)),
                      pl.BlockSpec((1, tr, D), lambda s, i, c: (s, i, 0))],
            out_specs=pl.BlockSpec((1, tr, D), lambda s, i, c: (s, i, 0))),
        out_shape=jax.ShapeDtypeStruct((N_CHIPS, HALF, D), F32),
        compiler_params=_params(32, dimension_semantics=_seq(2)),
    )(core, pack, recv)


def _chip_scatter(part):
    def body(t_ref, out_ref, send_sems, recv_sems, local_sem):
        x, y, c = _place()
        mine = 2 * x + y
        chips = _other_chips(x, y)
        loc = pltpu.make_async_copy(t_ref.at[mine], out_ref.at[mine], local_sem)
        loc.start()
        sends = []
        for j, (px, py) in enumerate(chips):
            cp = pltpu.make_async_remote_copy(t_ref.at[2 * px + py], out_ref.at[mine], send_sems.at[j], recv_sems.at[j],
                                              device_id=(px, py, c), device_id_type=MESH)
            cp.start()
            sends.append(cp)
        for j, (px, py) in enumerate(chips):
            pltpu.make_async_remote_copy(t_ref.at[mine], out_ref.at[2 * px + py], send_sems.at[j], recv_sems.at[j],
                                         device_id=(px, py, c), device_id_type=MESH).wait_recv()
        for cp in sends:
            cp.wait_send()
        loc.wait()

    return pl.pallas_call(
        body, name="chip_scatter", in_specs=[HBM], out_specs=HBM,
        out_shape=jax.ShapeDtypeStruct((N_CHIPS, HALF, D), F32),
        scratch_shapes=[pltpu.SemaphoreType.DMA((3,)), pltpu.SemaphoreType.DMA((3,)), pltpu.SemaphoreType.DMA],
        compiler_params=pltpu.CompilerParams(has_side_effects=True),
    )(part)


def _sum_chips(parts):
    tr = 512

    def body(p_ref, o_ref):
        o_ref[...] = ((p_ref[0] + p_ref[1]) + p_ref[2]) + p_ref[3]

    return pl.pallas_call(
        body, name="sum_chips", grid=(HALF // tr,),
        in_specs=[pl.BlockSpec((N_CHIPS, tr, D), lambda i: (0, i, 0))],
        out_specs=pl.BlockSpec((tr, D), lambda i: (i, 0)),
        out_shape=jax.ShapeDtypeStruct((HALF, D), F32),
        compiler_params=_params(32, dimension_semantics=_seq()),
    )(parts)


def _join_halves(half):
    def body(h_ref, out_ref, send_sem, recv_sem, local_sem):
        x, y, c = _place()
        rows = pl.ds(pl.multiple_of(c * HALF, 8), HALF)
        loc = pltpu.make_async_copy(h_ref, out_ref.at[rows], local_sem)
        loc.start()
        cp = pltpu.make_async_remote_copy(h_ref, out_ref.at[rows], send_sem, recv_sem, device_id=(x, y, 1 - c),
                                          device_id_type=MESH)
        cp.start()
        other = pl.ds(pl.multiple_of((1 - c) * HALF, 8), HALF)
        pltpu.make_async_remote_copy(h_ref, out_ref.at[other], send_sem, recv_sem, device_id=(x, y, 1 - c),
                                     device_id_type=MESH).wait_recv()
        cp.wait_send()
        loc.wait()

    return pl.pallas_call(
        body, name="join_halves", in_specs=[HBM], out_specs=HBM,
        out_shape=jax.ShapeDtypeStruct((PACK_ROWS, D), F32),
        scratch_shapes=[pltpu.SemaphoreType.DMA, pltpu.SemaphoreType.DMA, pltpu.SemaphoreType.DMA],
        compiler_params=pltpu.CompilerParams(has_side_effects=True),
    )(half)


def _reduce_pack(pack):
    core = lax.axis_index("c").astype(jnp.int32).reshape(1)
    chip_part = _add_halves(core, pack, _sibling_exchange(pack))
    return _join_halves(_sum_chips(_chip_scatter(chip_part)))


def _adamw(w, g, m, v, name):
    rows, cols = w.shape
    tr = max(t for t in range(8, 257, 8) if rows % t == 0) if rows % 8 == 0 else rows

    def body(w_ref, g_ref, m_ref, v_ref, d_ref, nm_ref, nv_ref):
        gg = g_ref[...]
        nm = ADAM_B1 * m_ref[...] + (1.0 - ADAM_B1) * gg
        nv = ADAM_B2 * v_ref[...] + (1.0 - ADAM_B2) * (gg * gg)
        m_hat = nm / (1.0 - ADAM_B1 ** ADAM_STEP)
        v_hat = nv / (1.0 - ADAM_B2 ** ADAM_STEP)
        d_ref[...] = -ADAM_LR * (m_hat / (jnp.sqrt(v_hat) + ADAM_EPS) + ADAM_WD * w_ref[...])
        nm_ref[...] = nm
        nv_ref[...] = nv

    blk = pl.BlockSpec((tr, cols), lambda i: (i, 0))
    return pl.pallas_call(
        body, name=name, grid=(rows // tr,),
        in_specs=[blk] * 4, out_specs=[blk] * 3,
        out_shape=[jax.ShapeDtypeStruct(w.shape, F32)] * 3,
        compiler_params=_params(32, dimension_semantics=_seq()),
    )(w, g, m, v)


def _flat_rows(v, rows):
    flat = v.reshape(-1).astype(F32)
    return jnp.pad(flat, (0, rows * D - flat.shape[0])).reshape(rows, D)


def _small_pack(gr):
    tail = jnp.concatenate([gr["bg2"].reshape(-1), gr["sinks"].reshape(-1), gr["gn"].reshape(-1)])
    rows = [gr["meta"].reshape(N_META, D)] + [gr[k].reshape(1, D) for k in
                                              ("ln_in_g", "ln_in_b", "ln1_g", "ln1_b", "ln2_g", "ln2_b")]
    rows += [_flat_rows(gr["b_in"], 3), _flat_rows(gr["wg2"], 4), _flat_rows(tail, 1)]
    packed = jnp.concatenate(rows, axis=0)
    return jnp.pad(packed, ((0, SMALL_ROWS - packed.shape[0]), (0, 0)))


def _small_unpack(p):
    flat = lambda r0, n, size: p[r0:r0 + n].reshape(-1)[:size]
    tail = p[29]
    return dict(meta=p[0:N_META], ln_in_g=p[16], ln_in_b=p[17], ln1_g=p[18], ln1_b=p[19], ln2_g=p[20], ln2_b=p[21],
                b_in=flat(22, 3, D_IN), wg2=flat(25, 4, 16 * 256).reshape(16, 256), bg2=tail[0:256],
                sinks=tail[256:256 + SWA_HEADS], gn=tail[256 + SWA_HEADS:256 + SWA_HEADS + DV])


BIG = ("w_in", "w_out", "w_g", "w_u", "w_d")


def kernel(x, meta_tokens, ln_in_g, ln_in_b, w_in, b_in, w_gate_lr2, b_gate_lr2, attn_sinks, gla_norm_g, w_out, ln1_g, ln1_b, w_ffn_gate, w_ffn_up, w_ffn_down, ln2_g, ln2_b, loss_target, m_meta_tokens, m_ln_in_g, m_ln_in_b, m_w_in, m_b_in, m_w_gate_lr2, m_b_gate_lr2, m_attn_sinks, m_gla_norm_g, m_w_out, m_ln1_g, m_ln1_b, m_w_ffn_gate, m_w_ffn_up, m_w_ffn_down, m_ln2_g, m_ln2_b, v_meta_tokens, v_ln_in_g, v_ln_in_b, v_w_in, v_b_in, v_w_gate_lr2, v_b_gate_lr2, v_attn_sinks, v_gla_norm_g, v_w_out, v_ln1_g, v_ln1_b, v_w_ffn_gate, v_w_ffn_up, v_w_ffn_down, v_ln2_g, v_ln2_b):
    chip = 2 * lax.axis_index("x") + lax.axis_index("y")

    shards = [w_in[0].T.astype(BF16), w_out[0].astype(BF16), w_ffn_gate[0].T.astype(BF16),
              w_ffn_up[0].T.astype(BF16), w_ffn_down[0].astype(BF16), meta_tokens, w_gate_lr2[0]]
    g_in, g_out, g_g, g_u, g_d, g_meta, g_wg2 = _gather_shards(shards)
    w_in_t = jnp.pad(g_in.reshape(D_IN, D), ((0, D_IN_P - D_IN), (0, 0)))
    meta_full = jnp.concatenate([g_meta[s] for s in range(N_CHIPS)], axis=1)
    wg2_full = jnp.concatenate([g_wg2[s] for s in range(N_CHIPS)], axis=1)

    loss_part, dx, gr = _local_step(
        x[0], loss_target[0], meta_full, ln_in_g, ln_in_b, w_in_t, b_in[0], wg2_full, b_gate_lr2[0], attn_sinks[0],
        gla_norm_g[0], g_out.reshape(D, D), ln1_g[0], ln1_b[0], g_g.reshape(D_FF, D), g_u.reshape(D_FF, D),
        g_d.reshape(D_FF, D), ln2_g[0], ln2_b[0])
    loss = lax.psum(loss_part, ("x", "y", "c"))

    small = _small_pack(gr)
    used = sum(SHARD_ROWS[k] for k in BIG) + SMALL_ROWS
    slabs = []
    for s in range(N_CHIPS):
        parts = [gr[k][s * SHARD_ROWS[k]:(s + 1) * SHARD_ROWS[k]] for k in BIG]
        slabs.append(jnp.concatenate(parts + [small, jnp.zeros((PACK_ROWS - used, D), F32)], axis=0))
    red = _reduce_pack(jnp.stack(slabs))

    big_g, r0 = {}, 0
    for k in BIG:
        big_g[k] = red[r0:r0 + SHARD_ROWS[k]]
        r0 += SHARD_ROWS[k]
    sg = _small_unpack(red[r0:r0 + SMALL_ROWS])
    col = lambda a, width: lax.dynamic_slice_in_dim(a, chip * width, width, axis=1)
    grads = dict(
        meta_tokens=col(sg["meta"], D // N_CHIPS), ln_in_g=sg["ln_in_g"], ln_in_b=sg["ln_in_b"],
        w_in=big_g["w_in"].T[None], b_in=sg["b_in"][None], w_gate_lr2=col(sg["wg2"], 256 // N_CHIPS)[None],
        b_gate_lr2=sg["bg2"][None], attn_sinks=sg["sinks"][None], gla_norm_g=sg["gn"][None],
        w_out=big_g["w_out"][None], ln1_g=sg["ln1_g"][None], ln1_b=sg["ln1_b"][None],
        w_ffn_gate=big_g["w_g"].T[None], w_ffn_up=big_g["w_u"].T[None], w_ffn_down=big_g["w_d"][None],
        ln2_g=sg["ln2_g"][None], ln2_b=sg["ln2_b"][None])
    weights = dict(meta_tokens=meta_tokens, ln_in_g=ln_in_g, ln_in_b=ln_in_b, w_in=w_in, b_in=b_in,
                   w_gate_lr2=w_gate_lr2, b_gate_lr2=b_gate_lr2, attn_sinks=attn_sinks, gla_norm_g=gla_norm_g,
                   w_out=w_out, ln1_g=ln1_g, ln1_b=ln1_b, w_ffn_gate=w_ffn_gate, w_ffn_up=w_ffn_up,
                   w_ffn_down=w_ffn_down, ln2_g=ln2_g, ln2_b=ln2_b)
    m_in = dict(meta_tokens=m_meta_tokens, ln_in_g=m_ln_in_g, ln_in_b=m_ln_in_b, w_in=m_w_in, b_in=m_b_in,
                w_gate_lr2=m_w_gate_lr2, b_gate_lr2=m_b_gate_lr2, attn_sinks=m_attn_sinks, gla_norm_g=m_gla_norm_g,
                w_out=m_w_out, ln1_g=m_ln1_g, ln1_b=m_ln1_b, w_ffn_gate=m_w_ffn_gate, w_ffn_up=m_w_ffn_up,
                w_ffn_down=m_w_ffn_down, ln2_g=m_ln2_g, ln2_b=m_ln2_b)
    v_in = dict(meta_tokens=v_meta_tokens, ln_in_g=v_ln_in_g, ln_in_b=v_ln_in_b, w_in=v_w_in, b_in=v_b_in,
                w_gate_lr2=v_w_gate_lr2, b_gate_lr2=v_b_gate_lr2, attn_sinks=v_attn_sinks, gla_norm_g=v_gla_norm_g,
                w_out=v_w_out, ln1_g=v_ln1_g, ln1_b=v_ln1_b, w_ffn_gate=v_w_ffn_gate, w_ffn_up=v_w_ffn_up,
                w_ffn_down=v_w_ffn_down, ln2_g=v_ln2_g, ln2_b=v_ln2_b)
    names = list(weights)
    big_names = ("w_in", "w_out", "w_ffn_gate", "w_ffn_up", "w_ffn_down")

    delta, new_m, new_v = {}, {}, {}
    for k in big_names:
        two_d = lambda a: a.reshape(a.shape[-2], a.shape[-1])
        d_, m_, v_ = _adamw(two_d(weights[k]), two_d(grads[k]), two_d(m_in[k]), two_d(v_in[k]), "adamw_" + k)
        delta[k], new_m[k], new_v[k] = (t.reshape(weights[k].shape) for t in (d_, m_, v_))
    small_names = [k for k in names if k not in big_names]
    sizes = [weights[k].size for k in small_names]
    rows_small = -(-sum(sizes) // D)
    rows_small += -rows_small % 8
    cat = lambda src: _flat_rows(jnp.concatenate([src[k].reshape(-1) for k in small_names]), rows_small)
    d_, m_, v_ = _adamw(cat(weights), cat(grads), cat(m_in), cat(v_in), "adamw_small")
    off = 0
    for k, n in zip(small_names, sizes):
        for dst, src in ((delta, d_), (new_m, m_), (new_v, v_)):
            dst[k] = src.reshape(-1)[off:off + n].reshape(weights[k].shape)
        off += n
    grads = {k: grads[k].reshape(weights[k].shape) for k in names}

    return (loss, dx[None], *[grads[k] for k in names], *[delta[k] for k in names], *[new_m[k] for k in names],
            *[new_v[k] for k in names])
```

```python
import functools

import jax
import jax.numpy as jnp
from jax import lax
from jax.experimental import pallas as pl
from jax.experimental.pallas import tpu as pltpu

F32 = jnp.float32
BF16 = jnp.bfloat16
MESH = pl.DeviceIdType.MESH

D = 1024
SEQ = 4096
N_META = 16
SWA_HEADS, SWA_KV_HEADS, DH = 8, 2, 64
WINDOW = 128
GLA_HEADS, DK, DV = 4, 64, 128
GLA_TAU = 16.0
CH = 64
D_FF = 2816
D_IN = 2320
LN_EPS = 1e-5
RMS_EPS = 1e-6
ALPHA = 2.0 ** 0.25
NEG = -1e30
ADAM_LR, ADAM_B1, ADAM_B2, ADAM_EPS, ADAM_WD, ADAM_STEP = 0.001, 0.9, 0.999, 1e-8, 0.01, 10
O_QS, O_KS, O_VS, O_QG, O_KG, O_VG, O_RG, O_LR = 0, 512, 640, 768, 1024, 1280, 1792, 2304

LANE = 128
BLK = WINDOW
D_IN_P = D_IN + LANE - 16
META_OFF = CH - N_META
N_CHIPS = 4
SHARD_ROWS = dict(w_in=D_IN // N_CHIPS, w_out=D // N_CHIPS, w_g=D_FF // N_CHIPS, w_u=D_FF // N_CHIPS,
                  w_d=D_FF // N_CHIPS)
SMALL_ROWS = 32
BF16_ROWS = 16
W_IN_WIN = -(-SHARD_ROWS["w_in"] // (2 * BF16_ROWS)) * 2 * BF16_ROWS
VMEM_CAP_MB = 64


def _lp():
    return SEQ + BLK


def _row_tile(cap):
    lp = _lp()
    return max(t for t in range(16, cap + 1, 16) if lp % t == 0)


def _params(vmem_mb, **kw):
    assert vmem_mb <= VMEM_CAP_MB - 6
    return pltpu.CompilerParams(vmem_limit_bytes=vmem_mb << 20, **kw)


def _seq(n=1):
    return ("arbitrary",) * n


def _const(shape):
    return pl.BlockSpec(shape, lambda *_: (0,) * len(shape), pipeline_mode=pl.Buffered(1))


def _acc(shape):
    return pl.BlockSpec(shape, lambda *_: (0,) * len(shape))


def _rows(tm, width):
    return pl.BlockSpec((tm, width), lambda i: (i, 0))


def _dot(a, b):
    return jnp.dot(a.astype(BF16), b.astype(BF16), preferred_element_type=F32)


def _dot_nt(a, b):
    return lax.dot_general(a.astype(BF16), b.astype(BF16), (((1,), (1,)), ((), ())), preferred_element_type=F32)


def _dot_tn(a, b):
    return lax.dot_general(a.astype(BF16), b.astype(BF16), (((0,), (0,)), ((), ())), preferred_element_type=F32)


def _dot_exact(a, b):
    return jnp.dot(a, b, precision=lax.Precision.HIGHEST, preferred_element_type=F32)


def _ln_stats(x):
    mu = jnp.mean(x, axis=-1, keepdims=True)
    xc = x - mu
    rstd = lax.rsqrt(jnp.mean(xc * xc, axis=-1, keepdims=True) + LN_EPS)
    return xc * rstd, rstd


def _ln_bwd(dy, xhat, rstd, g):
    dxh = dy * g
    return rstd * (dxh - jnp.mean(dxh, axis=-1, keepdims=True) - xhat * jnp.mean(dxh * xhat, axis=-1, keepdims=True))


def _sigmoid(x):
    return 1.0 / (1.0 + jnp.exp(-x))


def _iota(shape, dim):
    return lax.broadcasted_iota(jnp.int32, shape, dim)


def _hbm(*arrays):
    return tuple(pltpu.with_memory_space_constraint(a, pltpu.HBM) for a in arrays)


def _ln_in_fwd(x, meta_ext, g, b):
    nb = SEQ // BLK

    def body(x_ref, m_ref, g_ref, b_ref, h_ref):
        i = pl.program_id(0)
        xin = jnp.where(i < nb, x_ref[...], m_ref[...])
        xhat, _ = _ln_stats(xin)
        h_ref[...] = xhat * g_ref[...] + b_ref[...]

    return pl.pallas_call(
        body, name="ln_in_fwd", grid=(nb + 1,),
        in_specs=[pl.BlockSpec((BLK, D), lambda i: (jnp.minimum(i, nb - 1), 0)),
                  _const((BLK, D)), _const((1, D)), _const((1, D))],
        out_specs=_rows(BLK, D),
        out_shape=pltpu.HBM((_lp(), D), F32),
        compiler_params=_params(16, dimension_semantics=_seq()),
    )(*_hbm(x, meta_ext, g, b))


def _in_proj(h0, w_in_t, b_in_p, wg2_p, bg2):
    tm = _row_tile(384)
    lp = _lp()
    widths = (512, 128, 128, 256, 256, 512, 512, 128)
    offs = (O_QS, O_KS, O_VS, O_QG, O_KG, O_VG, O_RG, O_LR)

    def body(h_ref, w_ref, b_ref, wg2_ref, bg2_ref, *outs):
        proj = _dot_nt(h_ref[...], w_ref[...]) + b_ref[...]
        for o_ref, off, wd in zip(outs[:8], offs, widths):
            o_ref[...] = proj[:, off:off + wd]
        outs[8][...] = _dot(proj[:, O_LR:O_LR + LANE], wg2_ref[...]) + bg2_ref[...]

    return pl.pallas_call(
        body, name="in_proj", grid=(lp // tm,),
        in_specs=[_rows(tm, D), _const((D_IN_P, D)), _const((1, D_IN_P)), _const((LANE, 256)), _const((1, 256))],
        out_specs=[_rows(tm, w) for w in widths] + [_rows(tm, 256)],
        out_shape=[pltpu.HBM((lp, w), F32) for w in widths] + [pltpu.HBM((lp, 256), F32)],
        compiler_params=_params(40, dimension_semantics=_seq()),
    )(*_hbm(h0, w_in_t, b_in_p, wg2_p, bg2))


def _swa_masks(n):
    nb = SEQ // BLK
    is_meta = n == nb
    ri = _iota((BLK, BLK), 0)
    cj = _iota((BLK, BLK), 1)
    meta_col = ((cj >= META_OFF) & (cj < CH)).astype(jnp.int32)
    meta_q = meta_col * ((cj <= ri) & (ri < CH)).astype(jnp.int32)
    valid_m = jnp.where(is_meta, meta_q, meta_col) > 0
    dist_m = jnp.where(is_meta, ri - cj, n * BLK + ri + CH - cj).astype(F32)
    valid_p = jnp.where((n >= 1) & (n < nb), (cj > ri).astype(jnp.int32), 0) > 0
    dist_p = (ri + BLK - cj).astype(F32)
    valid_c = jnp.where(n < nb, (cj <= ri).astype(jnp.int32), 0) > 0
    dist_c = (ri - cj).astype(F32)
    return (dist_m, dist_p, dist_c), (valid_m, valid_p, valid_c)


def _swa_probs(q_h, ks, sink, slope, dists, valids):
    s = [jnp.where(v, _dot_nt(q_h, k) * (DH ** -0.5) - slope * d, NEG) for k, d, v in zip(ks, dists, valids)]
    m = jnp.maximum(jnp.maximum(jnp.max(s[0], axis=-1, keepdims=True), jnp.max(s[1], axis=-1, keepdims=True)),
                    jnp.maximum(jnp.max(s[2], axis=-1, keepdims=True), sink))
    e = [jnp.exp(x - m) for x in s]
    e_sink = jnp.exp(sink - m)
    inv = 1.0 / (jnp.sum(e[0], axis=-1, keepdims=True) + jnp.sum(e[1], axis=-1, keepdims=True)
                 + jnp.sum(e[2], axis=-1, keepdims=True) + e_sink)
    return e[0] * inv, e[1] * inv, e[2] * inv, e_sink * inv


def _swa_kv_specs(width):
    nb = SEQ // BLK
    return [pl.BlockSpec((BLK, width), lambda n: (nb, 0)),
            pl.BlockSpec((BLK, width), lambda n: (jnp.clip(n - 1, 0, nb - 1), 0)),
            pl.BlockSpec((BLK, width), lambda n: (jnp.minimum(n, nb), 0))]


def _swa_fwd(sinks, qs, ks, vs):
    nb = SEQ // BLK
    g = SWA_HEADS // SWA_KV_HEADS

    def body(sink_ref, q_ref, km_ref, kp_ref, kc_ref, vm_ref, vp_ref, vc_ref, o_ref):
        dists, valids = _swa_masks(pl.program_id(0))
        for h in range(SWA_HEADS):
            kv = slice((h // g) * DH, (h // g + 1) * DH)
            hs = slice(h * DH, (h + 1) * DH)
            p_m, p_p, p_c, _ = _swa_probs(q_ref[:, hs], (km_ref[:, kv], kp_ref[:, kv], kc_ref[:, kv]),
                                          sink_ref[h], 2.0 ** -(h + 1), dists, valids)
            o_ref[:, hs] = _dot(p_m, vm_ref[:, kv]) + _dot(p_p, vp_ref[:, kv]) + _dot(p_c, vc_ref[:, kv])

    kvw = SWA_KV_HEADS * DH
    return pl.pallas_call(
        body, name="swa_fwd", grid=(nb + 1,),
        in_specs=[pl.BlockSpec(memory_space=pltpu.SMEM), _rows(BLK, SWA_HEADS * DH)] + _swa_kv_specs(kvw) + _swa_kv_specs(kvw),
        out_specs=_rows(BLK, SWA_HEADS * DH),
        out_shape=pltpu.HBM((_lp(), SWA_HEADS * DH), F32),
        compiler_params=_params(16, dimension_semantics=_seq()),
    )(sinks, *_hbm(qs, ks, ks, ks, vs, vs, vs))


def _gla_block(t):
    nc = SEQ // CH
    return jnp.where(t == 0, nc, jnp.where(t == nc + 1, nc + 1, t - 1))


def _gla_rowmask(t):
    nc = SEQ // CH
    ri = _iota((CH, 1), 0)
    m = jnp.where(t == 0, (ri >= META_OFF).astype(jnp.int32), jnp.where(t == nc + 1, 0, 1))
    return (m > 0).astype(F32) + jnp.zeros((CH, 1), F32)


def _gla_decay(z, rmask):
    log_g = (jnp.minimum(z, 0.0) - jnp.log1p(jnp.exp(-jnp.abs(z)))) * (rmask / GLA_TAU)
    tril = (_iota((CH, CH), 0) >= _iota((CH, CH), 1)).astype(F32)
    return _dot_exact(tril, log_g), jnp.sum(log_g, axis=0, keepdims=True)


def _gla_fwd(qg, kg, vg, z):
    nc = SEQ // CH
    steps = nc + 2
    kw, vw = GLA_HEADS * DK, GLA_HEADS * DV

    def body(q_ref, k_ref, v_ref, z_ref, o_ref, st_ref, st):
        t = pl.program_id(0)

        @pl.when(t == 0)
        def _():
            st[...] = jnp.zeros_like(st)

        st_prev = st[...]
        st_ref[0] = st_prev
        rmask = _gla_rowmask(t)
        b, b_last = _gla_decay(z_ref[...], rmask)
        q = q_ref[...] * (rmask * DK ** -0.5)
        k = k_ref[...] * rmask
        v = v_ref[...] * rmask
        qe = q * jnp.exp(b)
        ke = k * jnp.exp(-b)
        kd = k * jnp.exp(b_last - b)
        e_last = jnp.exp(b_last)
        causal = _iota((CH, CH), 0) >= _iota((CH, CH), 1)
        for h in range(GLA_HEADS):
            ks, vs_ = slice(h * DK, (h + 1) * DK), slice(h * DV, (h + 1) * DV)
            a = jnp.where(causal, _dot_nt(qe[:, ks], ke[:, ks]), 0.0)
            o_ref[:, vs_] = _dot(a, v[:, vs_]) + _dot_nt(qe[:, ks], st_prev[:, ks])
            st[:, ks] = st_prev[:, ks] * e_last[:, ks] + _dot_tn(v[:, vs_], kd[:, ks])

    blk = lambda w: pl.BlockSpec((CH, w), lambda t: (_gla_block(t), 0))
    return pl.pallas_call(
        body, name="gla_fwd", grid=(steps,),
        in_specs=[blk(kw), blk(kw), blk(vw), blk(kw)],
        out_specs=[blk(vw), pl.BlockSpec((1, DV, kw), lambda t: (t, 0, 0))],
        out_shape=[pltpu.HBM((_lp(), vw), F32), pltpu.HBM((steps, DV, kw), F32)],
        scratch_shapes=[pltpu.VMEM((DV, kw), F32)],
        compiler_params=_params(16, dimension_semantics=_seq()),
    )(*_hbm(qg, kg, vg, z))


def _post_mix(o_s, o_gla, r_g, h0, gn4, w_out, g1, b1):
    tm = _row_tile(384)
    lp = _lp()

    def body(os_ref, og_ref, r_ref, h0_ref, gn_ref, w_ref, g_ref, b_ref, o_ref, pre_ref, h1_ref):
        o_ref[:, 0:512] = os_ref[...].astype(BF16)
        for h in range(GLA_HEADS):
            hs = slice(h * DV, (h + 1) * DV)
            xg = og_ref[:, hs]
            n = xg * lax.rsqrt(jnp.mean(xg * xg, axis=-1, keepdims=True) + RMS_EPS) * gn_ref[...]
            r = r_ref[:, hs]
            o_ref[:, 512 + h * DV:512 + (h + 1) * DV] = (n * (r * _sigmoid(r))).astype(BF16)
        pre = ALPHA * h0_ref[...] + _dot(o_ref[...], w_ref[...])
        pre_ref[...] = pre
        xhat, _ = _ln_stats(pre)
        h1_ref[...] = xhat * g_ref[...] + b_ref[...]

    return pl.pallas_call(
        body, name="post_mix", grid=(lp // tm,),
        in_specs=[_rows(tm, 512), _rows(tm, 512), _rows(tm, 512), _rows(tm, D), _const((1, DV)), _const((D, D)),
                  _const((1, D)), _const((1, D))],
        out_specs=[_rows(tm, D), _rows(tm, D), _rows(tm, D)],
        out_shape=[pltpu.HBM((lp, D), BF16), pltpu.HBM((lp, D), F32),
                   pltpu.HBM((lp, D), F32)],
        compiler_params=_params(32, dimension_semantics=_seq()),
    )(*_hbm(o_s, o_gla, r_g, h0, gn4, w_out, g1, b1))


def _ffn_fwd(h1, wg_t, wu_t, wd):
    tm = _row_tile(192)
    lp = _lp()

    def body(h_ref, wg_ref, wu_ref, wd_ref, g_ref, u_ref, pre_ref):
        h = h_ref[...]
        g = _dot_nt(h, wg_ref[...])
        u = _dot_nt(h, wu_ref[...])
        g_ref[...] = g
        u_ref[...] = u
        pre_ref[...] = ALPHA * h + _dot(g * _sigmoid(g) * u, wd_ref[...])

    return pl.pallas_call(
        body, name="ffn_fwd", grid=(lp // tm,),
        in_specs=[_rows(tm, D), _const((D_FF, D)), _const((D_FF, D)), _const((D_FF, D))],
        out_specs=[_rows(tm, D_FF), _rows(tm, D_FF), _rows(tm, D)],
        out_shape=[pltpu.HBM((lp, D_FF), F32), pltpu.HBM((lp, D_FF), F32),
                   pltpu.HBM((lp, D), F32)],
        compiler_params=_params(48, dimension_semantics=_seq()),
    )(*_hbm(h1, wg_t, wu_t, wd))


def _ln2_loss_bwd(pre2, target, g2, b2):
    nb = SEQ // BLK

    def body(p_ref, t_ref, g_ref, b_ref, dp_ref, loss_ref, dg_ref, db_ref, acc):
        i = pl.program_id(0)

        @pl.when(i == 0)
        def _():
            acc[...] = jnp.zeros_like(acc)
            dg_ref[...] = jnp.zeros_like(dg_ref)
            db_ref[...] = jnp.zeros_like(db_ref)

        real = jnp.where(i < nb, 1.0, 0.0)
        xhat, rstd = _ln_stats(p_ref[...])
        diff = (xhat * g_ref[...] + b_ref[...] - t_ref[...]) * real
        acc[...] += jnp.sum(diff * diff, axis=0, keepdims=True)
        dy = diff * (1.0 / D)
        dp_ref[...] = _ln_bwd(dy, xhat, rstd, g_ref[...])
        dg_ref[...] += jnp.sum(dy * xhat, axis=0, keepdims=True)
        db_ref[...] += jnp.sum(dy, axis=0, keepdims=True)

        @pl.when(i == nb)
        def _():
            loss_ref[...] = jnp.zeros_like(loss_ref) + (0.5 / D) * jnp.sum(acc[...], axis=1, keepdims=True)

    return pl.pallas_call(
        body, name="ln2_loss_bwd", grid=(nb + 1,),
        in_specs=[_rows(BLK, D), pl.BlockSpec((BLK, D), lambda i: (jnp.minimum(i, nb - 1), 0)), _const((1, D)),
                  _const((1, D))],
        out_specs=[_rows(BLK, D), _acc((1, LANE)), _acc((1, D)), _acc((1, D))],
        out_shape=[pltpu.HBM((_lp(), D), F32), pltpu.HBM((1, LANE), F32),
                   pltpu.HBM((1, D), F32), pltpu.HBM((1, D), F32)],
        scratch_shapes=[pltpu.VMEM((1, D), F32)],
        compiler_params=_params(16, dimension_semantics=_seq()),
    )(*_hbm(pre2, target, g2, b2))


def _ffn_bwd(dpre2, g, u, pre1, wg_t, wu_t, wd, g1):
    tm = _row_tile(192)
    lp = _lp()

    def body(dp_ref, g_ref, u_ref, p1_ref, wg_ref, wu_ref, wd_ref, g1_ref, a_ref, dg_ref, du_ref, dp1_ref,
             dg1_ref, db1_ref):
        @pl.when(pl.program_id(0) == 0)
        def _():
            dg1_ref[...] = jnp.zeros_like(dg1_ref)
            db1_ref[...] = jnp.zeros_like(db1_ref)

        dp = dp_ref[...]
        gg, uu = g_ref[...], u_ref[...]
        sg = _sigmoid(gg)
        silu = gg * sg
        da = _dot_nt(dp, wd_ref[...])
        a_ref[...] = (silu * uu).astype(BF16)
        dgate = (da * uu * (sg * (1.0 + gg * (1.0 - sg)))).astype(BF16)
        dup = (da * silu).astype(BF16)
        dg_ref[...] = dgate
        du_ref[...] = dup
        dh1 = ALPHA * dp + _dot(dgate, wg_ref[...]) + _dot(dup, wu_ref[...])
        xhat, rstd = _ln_stats(p1_ref[...])
        dp1_ref[...] = _ln_bwd(dh1, xhat, rstd, g1_ref[...])
        dg1_ref[...] += jnp.sum(dh1 * xhat, axis=0, keepdims=True)
        db1_ref[...] += jnp.sum(dh1, axis=0, keepdims=True)

    return pl.pallas_call(
        body, name="ffn_bwd", grid=(lp // tm,),
        in_specs=[_rows(tm, D), _rows(tm, D_FF), _rows(tm, D_FF), _rows(tm, D), _const((D_FF, D)), _const((D_FF, D)),
                  _const((D_FF, D)), _const((1, D))],
        out_specs=[_rows(tm, D_FF), _rows(tm, D_FF), _rows(tm, D_FF), _rows(tm, D), _acc((1, D)), _acc((1, D))],
        out_shape=[pltpu.HBM((lp, D_FF), BF16)] * 3
        + [pltpu.HBM((lp, D), F32), pltpu.HBM((1, D), F32), pltpu.HBM((1, D), F32)],
        compiler_params=_params(52, dimension_semantics=_seq()),
    )(*_hbm(dpre2, g, u, pre1, wg_t, wu_t, wd, g1))


def _atb(a, b, name):
    lp = _lp()
    tm = _row_tile(384)
    n, w = a.shape[1], b.shape[1]
    bw = 512 if n * w * 4 > (4 << 20) else w

    def body(a_ref, b_ref, o_ref):
        @pl.when(pl.program_id(1) == 0)
        def _():
            o_ref[...] = jnp.zeros_like(o_ref)

        o_ref[...] += _dot_tn(a_ref[...], b_ref[...])

    return pl.pallas_call(
        body, name=name, grid=(w // bw, lp // tm),
        in_specs=[pl.BlockSpec((tm, n), lambda j, k: (k, 0)), pl.BlockSpec((tm, bw), lambda j, k: (k, j))],
        out_specs=pl.BlockSpec((n, bw), lambda j, k: (0, j)),
        out_shape=pltpu.HBM((n, w), F32),
        compiler_params=_params(48, dimension_semantics=_seq(2)),
    )(*_hbm(a, b))


def _out_bwd(dpre1, w_out, o_gla, r_g, gn4):
    tm = _row_tile(384)
    lp = _lp()

    def body(dp_ref, w_ref, og_ref, r_ref, gn_ref, dos_ref, dog_ref, dr_ref, dgn_ref):
        @pl.when(pl.program_id(0) == 0)
        def _():
            dgn_ref[...] = jnp.zeros_like(dgn_ref)

        do = _dot_nt(dp_ref[...], w_ref[...])
        dos_ref[...] = do[:, 0:512]
        gn = gn_ref[...]
        for h in range(GLA_HEADS):
            hs = slice(h * DV, (h + 1) * DV)
            xg = og_ref[:, hs]
            rstd = lax.rsqrt(jnp.mean(xg * xg, axis=-1, keepdims=True) + RMS_EPS)
            nx = xg * rstd
            r = r_ref[:, hs]
            sr = _sigmoid(r)
            d_o = do[:, 512 + h * DV:512 + (h + 1) * DV]
            dr_ref[:, hs] = d_o * (nx * gn) * (sr * (1.0 + r * (1.0 - sr)))
            dn = d_o * (r * sr)
            dgn_ref[...] += jnp.sum(dn * nx, axis=0, keepdims=True)
            dnx = dn * gn
            dog_ref[:, hs] = rstd * (dnx - nx * jnp.mean(dnx * nx, axis=-1, keepdims=True))

    return pl.pallas_call(
        body, name="out_bwd", grid=(lp // tm,),
        in_specs=[_rows(tm, D), _const((D, D)), _rows(tm, 512), _rows(tm, 512), _const((1, DV))],
        out_specs=[_rows(tm, 512), _rows(tm, 512), _rows(tm, 512), _acc((1, DV))],
        out_shape=[pltpu.HBM((lp, 512), F32)] * 3 + [pltpu.HBM((1, DV), F32)],
        compiler_params=_params(32, dimension_semantics=_seq()),
    )(*_hbm(dpre1, w_out, o_gla, r_g, gn4))


def _gla_bwd(qg, kg, vg, z, do_gla, st_all):
    nc = SEQ // CH
    steps = nc + 2
    kw, vw = GLA_HEADS * DK, GLA_HEADS * DV

    def body(q_ref, k_ref, v_ref, z_ref, do_ref, st_ref, dq_ref, dk_ref, dv_ref, dz_ref, dst):
        t = steps - 1 - pl.program_id(0)

        @pl.when(pl.program_id(0) == 0)
        def _():
            dst[...] = jnp.zeros_like(dst)

        rmask = _gla_rowmask(t)
        zz = z_ref[...]
        b, b_last = _gla_decay(zz, rmask)
        e_b, e_nb, e_kd, e_last = jnp.exp(b), jnp.exp(-b), jnp.exp(b_last - b), jnp.exp(b_last)
        q = q_ref[...] * (rmask * DK ** -0.5)
        k = k_ref[...] * rmask
        v = v_ref[...] * rmask
        qe, ke, kd = q * e_b, k * e_nb, k * e_kd
        st_prev = st_ref[0]
        dst_new = dst[...]
        causal = _iota((CH, CH), 0) >= _iota((CH, CH), 1)
        dqe_parts, dke_parts, dkd_parts = [], [], []
        for h in range(GLA_HEADS):
            ks, vs_ = slice(h * DK, (h + 1) * DK), slice(h * DV, (h + 1) * DV)
            d_o = do_ref[:, vs_]
            a = jnp.where(causal, _dot_nt(qe[:, ks], ke[:, ks]), 0.0)
            da = jnp.where(causal, _dot_nt(d_o, v[:, vs_]), 0.0)
            dqe_parts.append(_dot(d_o, st_prev[:, ks]) + _dot(da, ke[:, ks]))
            dke_parts.append(_dot_tn(da, qe[:, ks]))
            dkd_parts.append(_dot(v[:, vs_], dst_new[:, ks]))
            dv_ref[:, vs_] = _dot_tn(a, d_o) + _dot_nt(kd[:, ks], dst_new[:, ks])
            dst[:, ks] = dst_new[:, ks] * e_last[:, ks] + _dot_tn(d_o, qe[:, ks])
        dqe = jnp.concatenate(dqe_parts, axis=1)
        dke = jnp.concatenate(dke_parts, axis=1)
        dkd = jnp.concatenate(dkd_parts, axis=1)
        dq_ref[...] = dqe * e_b * (rmask * DK ** -0.5)
        dk_ref[...] = (dke * e_nb + dkd * e_kd) * rmask
        dkd_kd = dkd * kd
        db = dqe * qe - dke * ke - dkd_kd
        db_last = jnp.sum(dst_new * st_prev, axis=0, keepdims=True) * e_last + jnp.sum(dkd_kd, axis=0, keepdims=True)
        triu = (_iota((CH, CH), 0) <= _iota((CH, CH), 1)).astype(F32)
        dlog_g = _dot_exact(triu, db) + db_last
        dz_ref[...] = dlog_g * (rmask / GLA_TAU) * _sigmoid(-zz)

    blk = lambda w: pl.BlockSpec((CH, w), lambda s: (_gla_block(steps - 1 - s), 0))
    return pl.pallas_call(
        body, name="gla_bwd", grid=(steps,),
        in_specs=[blk(kw), blk(kw), blk(vw), blk(kw), blk(vw), pl.BlockSpec((1, DV, kw), lambda s: (steps - 1 - s, 0, 0))],
        out_specs=[blk(kw), blk(kw), blk(vw), blk(kw)],
        out_shape=[pltpu.HBM((_lp(), kw), F32), pltpu.HBM((_lp(), kw), F32),
                   pltpu.HBM((_lp(), vw), F32), pltpu.HBM((_lp(), kw), F32)],
        scratch_shapes=[pltpu.VMEM((DV, kw), F32)],
        compiler_params=_params(16, dimension_semantics=_seq()),
    )(*_hbm(qg, kg, vg, z, do_gla, st_all))


def _swa_bwd(sinks, qs, ks, vs, do_s):
    nb = SEQ // BLK
    g = SWA_HEADS // SWA_KV_HEADS
    kvw = SWA_KV_HEADS * DH
    scale = DH ** -0.5

    def body(sink_ref, q_ref, km_ref, kp_ref, kc_ref, vm_ref, vp_ref, vc_ref, do_ref,
             dq_ref, dk_ref, dv_ref, dsink_ref, carry_k, carry_v, meta_k, meta_v):
        n = pl.program_id(0)

        @pl.when(n == 0)
        def _():
            for r in (carry_k, carry_v, meta_k, meta_v):
                r[...] = jnp.zeros_like(r)
            dsink_ref[...] = jnp.zeros_like(dsink_ref)

        @pl.when(n <= nb)
        def _():
            dists, valids = _swa_masks(n)
            lane = _iota((1, LANE), 1)
            dk_parts = [[], [], []]
            dv_parts = [[], [], []]
            for kvh in range(SWA_KV_HEADS):
                kv = slice(kvh * DH, (kvh + 1) * DH)
                kb = (km_ref[:, kv], kp_ref[:, kv], kc_ref[:, kv])
                vb = (vm_ref[:, kv], vp_ref[:, kv], vc_ref[:, kv])
                dk_acc = [jnp.zeros((BLK, DH), F32) for _ in range(3)]
                dv_acc = [jnp.zeros((BLK, DH), F32) for _ in range(3)]
                for gi in range(g):
                    h = kvh * g + gi
                    hs = slice(h * DH, (h + 1) * DH)
                    q_h, do_h = q_ref[:, hs], do_ref[:, hs]
                    *p, p_sink = _swa_probs(q_h, kb, sink_ref[h], 2.0 ** -(h + 1), dists, valids)
                    dp = [_dot_nt(do_h, vx) for vx in vb]
                    delta = sum(jnp.sum(px * dx, axis=-1, keepdims=True) for px, dx in zip(p, dp))
                    ds = [px * (dx - delta) for px, dx in zip(p, dp)]
                    dq_ref[:, hs] = scale * sum(_dot(dsx, kx) for dsx, kx in zip(ds, kb))
                    for j in range(3):
                        dk_acc[j] = dk_acc[j] + scale * _dot_tn(ds[j], q_h)
                        dv_acc[j] = dv_acc[j] + _dot_tn(p[j], do_h)
                    dsink_ref[...] += jnp.where(lane == h, -jnp.sum(p_sink * delta, axis=0, keepdims=True), 0.0)
                for j in range(3):
                    dk_parts[j].append(dk_acc[j])
                    dv_parts[j].append(dv_acc[j])
            dk3 = [jnp.concatenate(x, axis=1) for x in dk_parts]
            dv3 = [jnp.concatenate(x, axis=1) for x in dv_parts]
            meta_k[...] += dk3[0]
            meta_v[...] += dv3[0]
            dk_ref[...] = carry_k[...] + dk3[1]
            dv_ref[...] = carry_v[...] + dv3[1]
            carry_k[...] = dk3[2]
            carry_v[...] = dv3[2]

        @pl.when(n == nb + 1)
        def _():
            dk_ref[...] = meta_k[...]
            dv_ref[...] = meta_v[...]

    kv_out = pl.BlockSpec((BLK, kvw), lambda n: (jnp.where(n == nb + 1, nb, jnp.clip(n - 1, 0, nb - 1)), 0))
    qblk = pl.BlockSpec((BLK, SWA_HEADS * DH), lambda n: (jnp.minimum(n, nb), 0))
    return pl.pallas_call(
        body, name="swa_bwd", grid=(nb + 2,),
        in_specs=[pl.BlockSpec(memory_space=pltpu.SMEM), qblk] + _swa_kv_specs(kvw) + _swa_kv_specs(kvw) + [qblk],
        out_specs=[qblk, kv_out, kv_out, _acc((1, LANE))],
        out_shape=[pltpu.HBM((_lp(), SWA_HEADS * DH), F32), pltpu.HBM((_lp(), kvw), F32),
                   pltpu.HBM((_lp(), kvw), F32), pltpu.HBM((1, LANE), F32)],
        scratch_shapes=[pltpu.VMEM((BLK, kvw), F32)] * 4,
        compiler_params=_params(16, dimension_semantics=_seq()),
    )(sinks, *_hbm(qs, ks, ks, ks, vs, vs, vs, do_s))


def _in_bwd(dqs, dks, dvs, dqg, dkg, dvg, drg, dz, dpre1, w_in_t, wg2_p):
    tm = _row_tile(384)
    lp = _lp()
    widths = (512, 128, 128, 256, 256, 512, 512)
    offs = (O_QS, O_KS, O_VS, O_QG, O_KG, O_VG, O_RG)

    def body(*refs):
        parts, (dz_ref, dp1_ref, w_ref, wg2_ref, dproj_ref, dh0_ref, dbin_ref, dbg_ref) = refs[:7], refs[7:]

        @pl.when(pl.program_id(0) == 0)
        def _():
            dbin_ref[...] = jnp.zeros_like(dbin_ref)
            dbg_ref[...] = jnp.zeros_like(dbg_ref)

        for p_ref, off, wd in zip(parts, offs, widths):
            val = p_ref[...]
            dproj_ref[:, off:off + wd] = val.astype(BF16)
            dbin_ref[:, off:off + wd] += jnp.sum(val, axis=0, keepdims=True)
        dz = dz_ref[...]
        dlr = _dot_nt(dz, wg2_ref[...])
        dproj_ref[:, O_LR:O_LR + LANE] = dlr.astype(BF16)
        dbin_ref[:, O_LR:O_LR + LANE] += jnp.sum(dlr, axis=0, keepdims=True)
        dbg_ref[...] += jnp.sum(dz, axis=0, keepdims=True)
        dh0_ref[...] = ALPHA * dp1_ref[...] + _dot(dproj_ref[...], w_ref[...])

    return pl.pallas_call(
        body, name="in_bwd", grid=(lp // tm,),
        in_specs=[_rows(tm, w) for w in widths] + [_rows(tm, 256), _rows(tm, D), _const((D_IN_P, D)), _const((LANE, 256))],
        out_specs=[_rows(tm, D_IN_P), _rows(tm, D), _acc((1, D_IN_P)), _acc((1, 256))],
        out_shape=[pltpu.HBM((lp, D_IN_P), BF16), pltpu.HBM((lp, D), F32),
                   pltpu.HBM((1, D_IN_P), F32), pltpu.HBM((1, 256), F32)],
        compiler_params=_params(40, dimension_semantics=_seq()),
    )(*_hbm(dqs, dks, dvs, dqg, dkg, dvg, drg, dz, dpre1, w_in_t, wg2_p))


def _ln_in_bwd(x, meta_ext, dh0, g):
    nb = SEQ // BLK

    def body(x_ref, m_ref, dh_ref, g_ref, dx_ref, dm_ref, dg_ref, db_ref):
        i = pl.program_id(0)

        @pl.when(i == 0)
        def _():
            dg_ref[...] = jnp.zeros_like(dg_ref)
            db_ref[...] = jnp.zeros_like(db_ref)

        xin = jnp.where(i < nb, x_ref[...], m_ref[...])
        xhat, rstd = _ln_stats(xin)
        dh = dh_ref[...]
        dxin = _ln_bwd(dh, xhat, rstd, g_ref[...])
        dg_ref[...] += jnp.sum(dh * xhat, axis=0, keepdims=True)
        db_ref[...] += jnp.sum(dh, axis=0, keepdims=True)

        @pl.when(i < nb)
        def _():
            dx_ref[...] = dxin

        @pl.when(i == nb)
        def _():
            dm_ref[...] = dxin

    xblk = pl.BlockSpec((BLK, D), lambda i: (jnp.minimum(i, nb - 1), 0))
    return pl.pallas_call(
        body, name="ln_in_bwd", grid=(nb + 1,),
        in_specs=[xblk, _const((BLK, D)), _rows(BLK, D), _const((1, D))],
        out_specs=[xblk, _acc((BLK, D)), _acc((1, D)), _acc((1, D))],
        out_shape=[pltpu.HBM((SEQ, D), F32), pltpu.HBM((BLK, D), F32),
                   pltpu.HBM((1, D), F32), pltpu.HBM((1, D), F32)],
        compiler_params=_params(16, dimension_semantics=_seq()),
    )(*_hbm(x, meta_ext, dh0, g))


def _local_step(x, target, meta_full, ln_in_g, ln_in_b, w_in_t, b_in, wg2, bg2, sinks, gn, w_out, g1, b1,
                wg_t, wu_t, wd, g2, b2):
    row = lambda v: v.reshape(1, -1).astype(F32)
    meta_ext = jnp.pad(meta_full, ((META_OFF, BLK - CH), (0, 0)))
    b_in_p = jnp.pad(row(b_in), ((0, 0), (0, D_IN_P - D_IN)))
    wg2_p = jnp.pad(wg2, ((0, LANE - wg2.shape[0]), (0, 0))).astype(BF16)
    gn4 = row(gn)
    sinks = sinks.reshape(-1).astype(F32)

    h0 = _ln_in_fwd(x, meta_ext, row(ln_in_g), row(ln_in_b))
    qs, ks, vs, qg, kg, vg, rg, glr, z = _in_proj(h0, w_in_t, b_in_p, wg2_p, row(bg2))
    o_s = _swa_fwd(sinks, qs, ks, vs)
    o_gla, st_all = _gla_fwd(qg, kg, vg, z)
    o, pre1, h1 = _post_mix(o_s, o_gla, rg, h0, gn4, w_out, row(g1), row(b1))
    g, u, pre2 = _ffn_fwd(h1, wg_t, wu_t, wd)

    dpre2, loss, dg2, db2 = _ln2_loss_bwd(pre2, target, row(g2), row(b2))
    a, dgate, dup, dpre1, dg1, db1 = _ffn_bwd(dpre2, g, u, pre1, wg_t, wu_t, wd, row(g1))
    dwd = _atb(a, dpre2, "dw_down")
    dwg_t = _atb(dgate, h1, "dw_gate")
    dwu_t = _atb(dup, h1, "dw_up")
    dw_out = _atb(o, dpre1, "dw_out")
    do_s, do_gla, drg, dgn = _out_bwd(dpre1, w_out, o_gla, rg, gn4)
    dqg, dkg, dvg, dz = _gla_bwd(qg, kg, vg, z, do_gla, st_all)
    dqs, dks, dvs, dsinks = _swa_bwd(sinks, qs, ks, vs, do_s)
    dproj, dh0, db_in_p, dbg2 = _in_bwd(dqs, dks, dvs, dqg, dkg, dvg, drg, dz, dpre1, w_in_t, wg2_p)
    dw_in_t = _atb(dproj, h0, "dw_in")
    dwg2_p = _atb(glr, dz, "dw_gate_lr2")
    dx, dmeta_blk, dg_in, db_in_ln = _ln_in_bwd(x, meta_ext, dh0, row(ln_in_g))

    grads = dict(
        w_in=dw_in_t, w_out=dw_out, w_g=dwg_t, w_u=dwu_t, w_d=dwd,
        meta=dmeta_blk[META_OFF:CH], ln_in_g=dg_in, ln_in_b=db_in_ln, ln1_g=dg1, ln1_b=db1, ln2_g=dg2, ln2_b=db2,
        b_in=db_in_p[:, :D_IN], wg2=dwg2_p[:wg2.shape[0]], bg2=dbg2, sinks=dsinks[:, :SWA_HEADS], gn=dgn)
    return loss[0, 0], dx, grads


HBM = pl.BlockSpec(memory_space=pltpu.HBM)


def _place():
    return lax.axis_index("x"), lax.axis_index("y"), lax.axis_index("c")


def _other_chips(x, y):
    return [(1 - x, y), (x, 1 - y), (1 - x, 1 - y)]


def _dma_sems(n):
    return pltpu.SemaphoreType.DMA((n,))


def _comm_params():
    return pltpu.CompilerParams(has_side_effects=True)


def _gather_halves(shards):
    n = len(shards)

    def body(*refs):
        ins, outs = refs[:n], refs[n:2 * n]
        ici_send, ici_recv, d2d_send, d2d_recv, local_sems = refs[2 * n:]
        x, y, c = _place()
        mine = 2 * x + y
        chips = _other_chips(x, y)

        def ici(a, j, src_chip):
            px, py = chips[j]
            return pltpu.make_async_remote_copy(ins[a].at[c], outs[a].at[src_chip, c], ici_send.at[3 * a + j],
                                                ici_recv.at[3 * a + j], device_id=(px, py, c), device_id_type=MESH)

        def d2d(a, j, half):
            px, py = chips[j]
            blk = outs[a].at[2 * px + py, half]
            return pltpu.make_async_remote_copy(blk, blk, d2d_send.at[3 * a + j], d2d_recv.at[3 * a + j],
                                                device_id=(x, y, 1 - c), device_id_type=MESH)

        local = [pltpu.make_async_copy(ins[a], outs[a].at[mine], local_sems.at[a]) for a in range(n)]
        for cp in local:
            cp.start()
        sends = [ici(a, j, mine) for a in range(n) for j in range(3)]
        for cp in sends:
            cp.start()
        passed = []
        for a in range(n):
            for j, (px, py) in enumerate(chips):
                ici(a, j, 2 * px + py).wait_recv()
                fwd = d2d(a, j, c)
                fwd.start()
                passed.append(fwd)
        for a in range(n):
            for j in range(3):
                d2d(a, j, 1 - c).wait_recv()
        for cp in sends + passed:
            cp.wait_send()
        for cp in local:
            cp.wait()

    return pl.pallas_call(
        body, name="gather_halves",
        in_specs=[HBM] * n, out_specs=[HBM] * n,
        out_shape=[pltpu.HBM((N_CHIPS,) + s.shape, s.dtype) for s in shards],
        scratch_shapes=[_dma_sems(3 * n)] * 4 + [_dma_sems(n)],
        compiler_params=_comm_params(),
    )(*_hbm(*shards))


def _sibling_exchange(grads):
    n = len(grads)

    def body(*refs):
        ins, outs = refs[:n], refs[n:2 * n]
        send_sems, recv_sems = refs[2 * n:]
        x, y, c = _place()
        copies = []
        for a in range(n):
            for s in range(N_CHIPS):
                cp = pltpu.make_async_remote_copy(ins[a].at[s, 1 - c], outs[a].at[s], send_sems.at[N_CHIPS * a + s],
                                                  recv_sems.at[N_CHIPS * a + s], device_id=(x, y, 1 - c),
                                                  device_id_type=MESH)
                cp.start()
                copies.append(cp)
        for cp in copies:
            cp.wait_recv()
        for cp in copies:
            cp.wait_send()

    return pl.pallas_call(
        body, name="sibling_exchange", in_specs=[HBM] * n, out_specs=[HBM] * n,
        out_shape=[pltpu.HBM((N_CHIPS, g.shape[2], D), F32) for g in grads],
        scratch_shapes=[_dma_sems(N_CHIPS * n)] * 2,
        compiler_params=_comm_params(),
    )(*_hbm(*grads))


def _add_halves(core, grad, recv, dtype, name):
    h = grad.shape[2]

    def body(c_ref, a_ref, b_ref, o_ref):
        o_ref[...] = (a_ref[0] + b_ref[...]).astype(dtype)

    return pl.pallas_call(
        body, name=name,
        grid_spec=pltpu.PrefetchScalarGridSpec(
            num_scalar_prefetch=1, grid=(N_CHIPS,),
            in_specs=[pl.BlockSpec((1, 1, h, D), lambda s, c: (s, c[0], 0, 0)),
                      pl.BlockSpec((1, h, D), lambda s, c: (s, 0, 0))],
            out_specs=pl.BlockSpec((1, h, D), lambda s, c: (s, 0, 0))),
        out_shape=pltpu.HBM((N_CHIPS, h, D), dtype),
        compiler_params=_params(16, dimension_semantics=_seq()),
    )(core, *_hbm(grad, recv))


def _chip_scatter(parts):
    n = len(parts)

    def body(*refs):
        ins, outs = refs[:n], refs[n:2 * n]
        send_sems, recv_sems, local_sems = refs[2 * n:]
        x, y, c = _place()
        mine = 2 * x + y
        chips = _other_chips(x, y)
        local = [pltpu.make_async_copy(ins[a].at[mine], outs[a].at[mine], local_sems.at[a]) for a in range(n)]
        for cp in local:
            cp.start()
        sends = []
        for a in range(n):
            for j, (px, py) in enumerate(chips):
                cp = pltpu.make_async_remote_copy(ins[a].at[2 * px + py], outs[a].at[mine], send_sems.at[3 * a + j],
                                                  recv_sems.at[3 * a + j], device_id=(px, py, c), device_id_type=MESH)
                cp.start()
                sends.append(cp)
        for a in range(n):
            for j, (px, py) in enumerate(chips):
                pltpu.make_async_remote_copy(ins[a].at[mine], outs[a].at[2 * px + py], send_sems.at[3 * a + j],
                                             recv_sems.at[3 * a + j], device_id=(px, py, c),
                                             device_id_type=MESH).wait_recv()
        for cp in sends:
            cp.wait_send()
        for cp in local:
            cp.wait()

    return pl.pallas_call(
        body, name="chip_scatter", in_specs=[HBM] * n, out_specs=[HBM] * n,
        out_shape=[pltpu.HBM(p.shape, p.dtype) for p in parts],
        scratch_shapes=[_dma_sems(3 * n)] * 2 + [_dma_sems(n)],
        compiler_params=_comm_params(),
    )(*_hbm(*parts))


def _sum_chips(parts, name):
    h = parts.shape[1]

    def body(p_ref, o_ref):
        o_ref[...] = ((p_ref[0].astype(F32) + p_ref[1].astype(F32)) + p_ref[2].astype(F32)) + p_ref[3].astype(F32)

    return pl.pallas_call(
        body, name=name, grid=(1,),
        in_specs=[pl.BlockSpec((N_CHIPS, h, D), lambda i: (0, 0, 0))],
        out_specs=pl.BlockSpec((h, D), lambda i: (0, 0)),
        out_shape=pltpu.HBM((h, D), F32),
        compiler_params=_params(16, dimension_semantics=_seq()),
    )(*_hbm(parts))


def _join_halves(halves):
    n = len(halves)

    def body(*refs):
        ins, outs = refs[:n], refs[n:2 * n]
        send_sems, recv_sems, local_sems = refs[2 * n:]
        x, y, c = _place()
        local, sends = [], []
        for a in range(n):
            loc = pltpu.make_async_copy(ins[a], outs[a].at[c], local_sems.at[a])
            loc.start()
            local.append(loc)
            cp = pltpu.make_async_remote_copy(ins[a], outs[a].at[c], send_sems.at[a], recv_sems.at[a],
                                              device_id=(x, y, 1 - c), device_id_type=MESH)
            cp.start()
            sends.append(cp)
        for a in range(n):
            pltpu.make_async_remote_copy(ins[a], outs[a].at[1 - c], send_sems.at[a], recv_sems.at[a],
                                         device_id=(x, y, 1 - c), device_id_type=MESH).wait_recv()
        for cp in sends:
            cp.wait_send()
        for cp in local:
            cp.wait()

    return pl.pallas_call(
        body, name="join_halves", in_specs=[HBM] * n, out_specs=[HBM] * n,
        out_shape=[pltpu.HBM((2,) + h.shape, F32) for h in halves],
        scratch_shapes=[_dma_sems(n)] * 3,
        compiler_params=_comm_params(),
    )(*_hbm(*halves))


def _reduce_scatter(grads, wire_dtypes, names):
    core = lax.axis_index("c").astype(jnp.int32).reshape(1)
    recv = _sibling_exchange(grads)
    parts = [_add_halves(core, g, r, dt, "add_halves_" + nm) for g, r, dt, nm in zip(grads, recv, wire_dtypes, names)]
    halves = [_sum_chips(p, "sum_chips_" + nm) for p, nm in zip(_chip_scatter(parts), names)]
    return [f.reshape(2 * f.shape[1], D) for f in _join_halves(halves)]


def _adamw(w, g, m, v, name):
    rows, cols = w.shape
    tr = max(t for t in range(8, 257, 8) if rows % t == 0) if rows % 8 == 0 else rows

    def body(w_ref, g_ref, m_ref, v_ref, d_ref, nm_ref, nv_ref):
        gg = g_ref[...]
        nm = ADAM_B1 * m_ref[...] + (1.0 - ADAM_B1) * gg
        nv = ADAM_B2 * v_ref[...] + (1.0 - ADAM_B2) * (gg * gg)
        m_hat = nm / (1.0 - ADAM_B1 ** ADAM_STEP)
        v_hat = nv / (1.0 - ADAM_B2 ** ADAM_STEP)
        d_ref[...] = -ADAM_LR * (m_hat / (jnp.sqrt(v_hat) + ADAM_EPS) + ADAM_WD * w_ref[...])
        nm_ref[...] = nm
        nv_ref[...] = nv

    blk = pl.BlockSpec((tr, cols), lambda i: (i, 0))
    return pl.pallas_call(
        body, name=name, grid=(rows // tr,),
        in_specs=[blk] * 4, out_specs=[blk] * 3,
        out_shape=[pltpu.HBM(w.shape, F32)] * 3,
        compiler_params=_params(32, dimension_semantics=_seq()),
    )(*_hbm(w, g, m, v))


def _flat_rows(v, rows):
    flat = v.reshape(-1).astype(F32)
    return jnp.pad(flat, (0, rows * D - flat.shape[0])).reshape(rows, D)


def _small_pack(gr):
    tail = jnp.concatenate([gr["bg2"].reshape(-1), gr["sinks"].reshape(-1), gr["gn"].reshape(-1)])
    rows = [gr["meta"].reshape(N_META, D)] + [gr[k].reshape(1, D) for k in
                                              ("ln_in_g", "ln_in_b", "ln1_g", "ln1_b", "ln2_g", "ln2_b")]
    rows += [_flat_rows(gr["b_in"], 3), _flat_rows(gr["wg2"], 4), _flat_rows(tail, 1)]
    packed = jnp.concatenate(rows, axis=0)
    return jnp.pad(packed, ((0, SMALL_ROWS - packed.shape[0]), (0, 0)))


def _small_unpack(p):
    flat = lambda r0, n, size: p[r0:r0 + n].reshape(-1)[:size]
    tail = p[29]
    return dict(meta=p[0:N_META], ln_in_g=p[16], ln_in_b=p[17], ln1_g=p[18], ln1_b=p[19], ln2_g=p[20], ln2_b=p[21],
                b_in=flat(22, 3, D_IN), wg2=flat(25, 4, 16 * 256).reshape(16, 256), bg2=tail[0:256],
                sinks=tail[256:256 + SWA_HEADS], gn=tail[256 + SWA_HEADS:256 + SWA_HEADS + DV])


BIG = ("w_in", "w_out", "w_g", "w_u", "w_d")


def kernel(x, meta_tokens, ln_in_g, ln_in_b, w_in, b_in, w_gate_lr2, b_gate_lr2, attn_sinks, gla_norm_g, w_out, ln1_g, ln1_b, w_ffn_gate, w_ffn_up, w_ffn_down, ln2_g, ln2_b, loss_target, m_meta_tokens, m_ln_in_g, m_ln_in_b, m_w_in, m_b_in, m_w_gate_lr2, m_b_gate_lr2, m_attn_sinks, m_gla_norm_g, m_w_out, m_ln1_g, m_ln1_b, m_w_ffn_gate, m_w_ffn_up, m_w_ffn_down, m_ln2_g, m_ln2_b, v_meta_tokens, v_ln_in_g, v_ln_in_b, v_w_in, v_b_in, v_w_gate_lr2, v_b_gate_lr2, v_attn_sinks, v_gla_norm_g, v_w_out, v_ln1_g, v_ln1_b, v_w_ffn_gate, v_w_ffn_up, v_w_ffn_down, v_ln2_g, v_ln2_b):
    chip = 2 * lax.axis_index("x") + lax.axis_index("y")

    halves = lambda a: a.reshape(2, a.shape[0] // 2, a.shape[1])
    r_in = SHARD_ROWS["w_in"]
    shards = [jnp.pad(w_in[0].T.astype(BF16), ((0, W_IN_WIN - r_in), (0, 0))), w_out[0].astype(BF16),
              w_ffn_gate[0].T.astype(BF16), w_ffn_up[0].T.astype(BF16), w_ffn_down[0].astype(BF16), meta_tokens,
              w_gate_lr2[0]]
    g_in, g_out, g_g, g_u, g_d, g_meta, g_wg2 = _gather_halves([halves(a) for a in shards])
    w_in_t = jnp.pad(g_in.reshape(N_CHIPS, W_IN_WIN, D)[:, :r_in].reshape(D_IN, D), ((0, D_IN_P - D_IN), (0, 0)))
    meta_full = jnp.concatenate([g_meta[s].reshape(N_META, -1) for s in range(N_CHIPS)], axis=1)
    wg2_full = jnp.concatenate([g_wg2[s].reshape(w_gate_lr2.shape[1], -1) for s in range(N_CHIPS)], axis=1)

    loss_part, dx, gr = _local_step(
        x[0], loss_target[0], meta_full, ln_in_g, ln_in_b, w_in_t, b_in[0], wg2_full, b_gate_lr2[0], attn_sinks[0],
        gla_norm_g[0], g_out.reshape(D, D), ln1_g[0], ln1_b[0], g_g.reshape(D_FF, D), g_u.reshape(D_FF, D),
        g_d.reshape(D_FF, D), ln2_g[0], ln2_b[0])
    loss = lax.psum(loss_part, ("x", "y", "c"))

    win_start = [s * r_in // BF16_ROWS * BF16_ROWS for s in range(N_CHIPS)]
    to_send = [jnp.stack([gr["w_in"][st:st + W_IN_WIN] for st in win_start])] + [gr[k] for k in BIG[1:]]
    to_send.append(jnp.broadcast_to(_small_pack(gr), (N_CHIPS, SMALL_ROWS, D)))
    to_send = [a.reshape(N_CHIPS, 2, -1, D) for a in to_send]
    red = _reduce_scatter(to_send, [BF16] * len(BIG) + [F32], list(BIG) + ["small"])

    big_g = dict(zip(BIG, red))
    big_g["w_in"] = lax.dynamic_slice_in_dim(red[0], chip * (r_in % BF16_ROWS), r_in, axis=0)
    sg = _small_unpack(red[-1])
    col = lambda a, width: lax.dynamic_slice_in_dim(a, chip * width, width, axis=1)
    grads = dict(
        meta_tokens=col(sg["meta"], D // N_CHIPS), ln_in_g=sg["ln_in_g"], ln_in_b=sg["ln_in_b"],
        w_in=big_g["w_in"].T[None], b_in=sg["b_in"][None], w_gate_lr2=col(sg["wg2"], 256 // N_CHIPS)[None],
        b_gate_lr2=sg["bg2"][None], attn_sinks=sg["sinks"][None], gla_norm_g=sg["gn"][None],
        w_out=big_g["w_out"][None], ln1_g=sg["ln1_g"][None], ln1_b=sg["ln1_b"][None],
        w_ffn_gate=big_g["w_g"].T[None], w_ffn_up=big_g["w_u"].T[None], w_ffn_down=big_g["w_d"][None],
        ln2_g=sg["ln2_g"][None], ln2_b=sg["ln2_b"][None])
    weights = dict(meta_tokens=meta_tokens, ln_in_g=ln_in_g, ln_in_b=ln_in_b, w_in=w_in, b_in=b_in,
                   w_gate_lr2=w_gate_lr2, b_gate_lr2=b_gate_lr2, attn_sinks=attn_sinks, gla_norm_g=gla_norm_g,
                   w_out=w_out, ln1_g=ln1_g, ln1_b=ln1_b, w_ffn_gate=w_ffn_gate, w_ffn_up=w_ffn_up,
                   w_ffn_down=w_ffn_down, ln2_g=ln2_g, ln2_b=ln2_b)
    m_in = dict(meta_tokens=m_meta_tokens, ln_in_g=m_ln_in_g, ln_in_b=m_ln_in_b, w_in=m_w_in, b_in=m_b_in,
                w_gate_lr2=m_w_gate_lr2, b_gate_lr2=m_b_gate_lr2, attn_sinks=m_attn_sinks, gla_norm_g=m_gla_norm_g,
                w_out=m_w_out, ln1_g=m_ln1_g, ln1_b=m_ln1_b, w_ffn_gate=m_w_ffn_gate, w_ffn_up=m_w_ffn_up,
                w_ffn_down=m_w_ffn_down, ln2_g=m_ln2_g, ln2_b=m_ln2_b)
    v_in = dict(meta_tokens=v_meta_tokens, ln_in_g=v_ln_in_g, ln_in_b=v_ln_in_b, w_in=v_w_in, b_in=v_b_in,
                w_gate_lr2=v_w_gate_lr2, b_gate_lr2=v_b_gate_lr2, attn_sinks=v_attn_sinks, gla_norm_g=v_gla_norm_g,
                w_out=v_w_out, ln1_g=v_ln1_g, ln1_b=v_ln1_b, w_ffn_gate=v_w_ffn_gate, w_ffn_up=v_w_ffn_up,
                w_ffn_down=v_w_ffn_down, ln2_g=v_ln2_g, ln2_b=v_ln2_b)
    names = list(weights)
    big_names = ("w_in", "w_out", "w_ffn_gate", "w_ffn_up", "w_ffn_down")

    delta, new_m, new_v = {}, {}, {}
    for k in big_names:
        two_d = lambda a: a.reshape(a.shape[-2], a.shape[-1])
        d_, m_, v_ = _adamw(two_d(weights[k]), two_d(grads[k]), two_d(m_in[k]), two_d(v_in[k]), "adamw_" + k)
        delta[k], new_m[k], new_v[k] = (t.reshape(weights[k].shape) for t in (d_, m_, v_))
    small_names = [k for k in names if k not in big_names]
    sizes = [weights[k].size for k in small_names]
    rows_small = -(-sum(sizes) // D)
    rows_small += -rows_small % 8
    cat = lambda src: _flat_rows(jnp.concatenate([src[k].reshape(-1) for k in small_names]), rows_small)
    d_, m_, v_ = _adamw(cat(weights), cat(grads), cat(m_in), cat(v_in), "adamw_small")
    off = 0
    for k, n in zip(small_names, sizes):
        for dst, src in ((delta, d_), (new_m, m_), (new_v, v_)):
            dst[k] = src.reshape(-1)[off:off + n].reshape(weights[k].shape)
        off += n
    grads = {k: grads[k].reshape(weights[k].shape) for k in names}

    return (loss, dx[None], *[grads[k] for k in names], *[delta[k] for k in names], *[new_m[k] for k in names],
            *[new_v[k] for k in names])
```

```python
import functools

import jax
import jax.numpy as jnp
from jax import lax
from jax.experimental import pallas as pl
from jax.experimental.pallas import tpu as pltpu

F32 = jnp.float32
BF16 = jnp.bfloat16
MESH = pl.DeviceIdType.MESH

D = 1024
SEQ = 4096
N_META = 16
SWA_HEADS, SWA_KV_HEADS, DH = 8, 2, 64
WINDOW = 128
GLA_HEADS, DK, DV = 4, 64, 128
GLA_TAU = 16.0
CH = 64
D_FF = 2816
D_IN = 2320
LN_EPS = 1e-5
RMS_EPS = 1e-6
ALPHA = 2.0 ** 0.25
NEG = -1e30
ADAM_LR, ADAM_B1, ADAM_B2, ADAM_EPS, ADAM_WD, ADAM_STEP = 0.001, 0.9, 0.999, 1e-8, 0.01, 10
O_QS, O_KS, O_VS, O_QG, O_KG, O_VG, O_RG, O_LR = 0, 512, 640, 768, 1024, 1280, 1792, 2304

LANE = 128
BLK = WINDOW
D_IN_P = D_IN + LANE - 16
META_OFF = CH - N_META
N_CHIPS = 4
SHARD_ROWS = dict(w_in=D_IN // N_CHIPS, w_out=D // N_CHIPS, w_g=D_FF // N_CHIPS, w_u=D_FF // N_CHIPS,
                  w_d=D_FF // N_CHIPS)
SMALL_ROWS = 32
BF16_ROWS = 16
W_IN_WIN = -(-SHARD_ROWS["w_in"] // (2 * BF16_ROWS)) * 2 * BF16_ROWS
VMEM_CAP_MB = 64


def _lp():
    return SEQ + BLK


def _row_tile(cap):
    lp = _lp()
    return max(t for t in range(16, cap + 1, 16) if lp % t == 0)


def _params(vmem_mb, **kw):
    assert vmem_mb <= VMEM_CAP_MB - 6
    return pltpu.CompilerParams(vmem_limit_bytes=vmem_mb << 20, **kw)


def _seq(n=1):
    return ("arbitrary",) * n


def _const(shape):
    return pl.BlockSpec(shape, lambda *_: (0,) * len(shape), pipeline_mode=pl.Buffered(1))


def _acc(shape):
    return pl.BlockSpec(shape, lambda *_: (0,) * len(shape))


def _rows(tm, width):
    return pl.BlockSpec((tm, width), lambda i: (i, 0))


def _dot(a, b):
    return jnp.dot(a.astype(BF16), b.astype(BF16), preferred_element_type=F32)


def _dot_nt(a, b):
    return lax.dot_general(a.astype(BF16), b.astype(BF16), (((1,), (1,)), ((), ())), preferred_element_type=F32)


def _dot_tn(a, b):
    return lax.dot_general(a.astype(BF16), b.astype(BF16), (((0,), (0,)), ((), ())), preferred_element_type=F32)


def _dot_exact(a, b):
    return jnp.dot(a, b, precision=lax.Precision.HIGHEST, preferred_element_type=F32)


def _ln_stats(x):
    mu = jnp.mean(x, axis=-1, keepdims=True)
    xc = x - mu
    rstd = lax.rsqrt(jnp.mean(xc * xc, axis=-1, keepdims=True) + LN_EPS)
    return xc * rstd, rstd


def _ln_bwd(dy, xhat, rstd, g):
    dxh = dy * g
    return rstd * (dxh - jnp.mean(dxh, axis=-1, keepdims=True) - xhat * jnp.mean(dxh * xhat, axis=-1, keepdims=True))


def _sigmoid(x):
    return 1.0 / (1.0 + jnp.exp(-x))


def _iota(shape, dim):
    return lax.broadcasted_iota(jnp.int32, shape, dim)


def _hbm(*arrays):
    return tuple(pltpu.with_memory_space_constraint(a, pltpu.HBM) for a in arrays)


def _ln_in_fwd(x, meta_ext, g, b):
    nb = SEQ // BLK

    def body(x_ref, m_ref, g_ref, b_ref, h_ref):
        i = pl.program_id(0)
        xin = jnp.where(i < nb, x_ref[...], m_ref[...])
        xhat, _ = _ln_stats(xin)
        h_ref[...] = xhat * g_ref[...] + b_ref[...]

    return pl.pallas_call(
        body, name="ln_in_fwd", grid=(nb + 1,),
        in_specs=[pl.BlockSpec((BLK, D), lambda i: (jnp.minimum(i, nb - 1), 0)),
                  _const((BLK, D)), _const((1, D)), _const((1, D))],
        out_specs=_rows(BLK, D),
        out_shape=pltpu.HBM((_lp(), D), F32),
        compiler_params=_params(16, dimension_semantics=_seq()),
    )(*_hbm(x, meta_ext, g, b))


def _in_proj(h0, w_in_t, b_in_p, wg2_p, bg2):
    tm = _row_tile(384)
    lp = _lp()
    widths = (512, 128, 128, 256, 256, 512, 512, 128)
    offs = (O_QS, O_KS, O_VS, O_QG, O_KG, O_VG, O_RG, O_LR)

    def body(h_ref, w_ref, b_ref, wg2_ref, bg2_ref, *outs):
        proj = _dot_nt(h_ref[...], w_ref[...]) + b_ref[...]
        for o_ref, off, wd in zip(outs[:8], offs, widths):
            o_ref[...] = proj[:, off:off + wd]
        outs[8][...] = _dot(proj[:, O_LR:O_LR + LANE], wg2_ref[...]) + bg2_ref[...]

    return pl.pallas_call(
        body, name="in_proj", grid=(lp // tm,),
        in_specs=[_rows(tm, D), _const((D_IN_P, D)), _const((1, D_IN_P)), _const((LANE, 256)), _const((1, 256))],
        out_specs=[_rows(tm, w) for w in widths] + [_rows(tm, 256)],
        out_shape=[pltpu.HBM((lp, w), F32) for w in widths] + [pltpu.HBM((lp, 256), F32)],
        compiler_params=_params(40, dimension_semantics=_seq()),
    )(*_hbm(h0, w_in_t, b_in_p, wg2_p, bg2))


def _swa_masks(n):
    nb = SEQ // BLK
    is_meta = n == nb
    ri = _iota((BLK, BLK), 0)
    cj = _iota((BLK, BLK), 1)
    meta_col = ((cj >= META_OFF) & (cj < CH)).astype(jnp.int32)
    meta_q = meta_col * ((cj <= ri) & (ri < CH)).astype(jnp.int32)
    valid_m = jnp.where(is_meta, meta_q, meta_col) > 0
    dist_m = jnp.where(is_meta, ri - cj, n * BLK + ri + CH - cj).astype(F32)
    valid_p = jnp.where((n >= 1) & (n < nb), (cj > ri).astype(jnp.int32), 0) > 0
    dist_p = (ri + BLK - cj).astype(F32)
    valid_c = jnp.where(n < nb, (cj <= ri).astype(jnp.int32), 0) > 0
    dist_c = (ri - cj).astype(F32)
    return (dist_m, dist_p, dist_c), (valid_m, valid_p, valid_c)


def _swa_probs(q_h, ks, sink, slope, dists, valids):
    s = [jnp.where(v, _dot_nt(q_h, k) * (DH ** -0.5) - slope * d, NEG) for k, d, v in zip(ks, dists, valids)]
    m = jnp.maximum(jnp.maximum(jnp.max(s[0], axis=-1, keepdims=True), jnp.max(s[1], axis=-1, keepdims=True)),
                    jnp.maximum(jnp.max(s[2], axis=-1, keepdims=True), sink))
    e = [jnp.exp(x - m) for x in s]
    e_sink = jnp.exp(sink - m)
    inv = 1.0 / (jnp.sum(e[0], axis=-1, keepdims=True) + jnp.sum(e[1], axis=-1, keepdims=True)
                 + jnp.sum(e[2], axis=-1, keepdims=True) + e_sink)
    return e[0] * inv, e[1] * inv, e[2] * inv, e_sink * inv


def _swa_kv_specs(width):
    nb = SEQ // BLK
    return [pl.BlockSpec((BLK, width), lambda n: (nb, 0)),
            pl.BlockSpec((BLK, width), lambda n: (jnp.clip(n - 1, 0, nb - 1), 0)),
            pl.BlockSpec((BLK, width), lambda n: (jnp.minimum(n, nb), 0))]


def _swa_fwd(sinks, qs, ks, vs):
    nb = SEQ // BLK
    g = SWA_HEADS // SWA_KV_HEADS

    def body(sink_ref, q_ref, km_ref, kp_ref, kc_ref, vm_ref, vp_ref, vc_ref, o_ref):
        dists, valids = _swa_masks(pl.program_id(0))
        for h in range(SWA_HEADS):
            kv = slice((h // g) * DH, (h // g + 1) * DH)
            hs = slice(h * DH, (h + 1) * DH)
            p_m, p_p, p_c, _ = _swa_probs(q_ref[:, hs], (km_ref[:, kv], kp_ref[:, kv], kc_ref[:, kv]),
                                          sink_ref[h], 2.0 ** -(h + 1), dists, valids)
            o_ref[:, hs] = _dot(p_m, vm_ref[:, kv]) + _dot(p_p, vp_ref[:, kv]) + _dot(p_c, vc_ref[:, kv])

    kvw = SWA_KV_HEADS * DH
    return pl.pallas_call(
        body, name="swa_fwd", grid=(nb + 1,),
        in_specs=[pl.BlockSpec(memory_space=pltpu.SMEM), _rows(BLK, SWA_HEADS * DH)] + _swa_kv_specs(kvw) + _swa_kv_specs(kvw),
        out_specs=_rows(BLK, SWA_HEADS * DH),
        out_shape=pltpu.HBM((_lp(), SWA_HEADS * DH), F32),
        compiler_params=_params(16, dimension_semantics=_seq()),
    )(sinks, *_hbm(qs, ks, ks, ks, vs, vs, vs))


def _gla_block(t):
    nc = SEQ // CH
    return jnp.where(t == 0, nc, jnp.where(t == nc + 1, nc + 1, t - 1))


def _gla_rowmask(t):
    nc = SEQ // CH
    ri = _iota((CH, 1), 0)
    m = jnp.where(t == 0, (ri >= META_OFF).astype(jnp.int32), jnp.where(t == nc + 1, 0, 1))
    return (m > 0).astype(F32) + jnp.zeros((CH, 1), F32)


def _gla_decay(z, rmask):
    log_g = (jnp.minimum(z, 0.0) - jnp.log1p(jnp.exp(-jnp.abs(z)))) * (rmask / GLA_TAU)
    tril = (_iota((CH, CH), 0) >= _iota((CH, CH), 1)).astype(F32)
    return _dot_exact(tril, log_g), jnp.sum(log_g, axis=0, keepdims=True)


def _gla_fwd(qg, kg, vg, z):
    nc = SEQ // CH
    steps = nc + 2
    kw, vw = GLA_HEADS * DK, GLA_HEADS * DV

    def body(q_ref, k_ref, v_ref, z_ref, o_ref, st_ref, st):
        t = pl.program_id(0)

        @pl.when(t == 0)
        def _():
            st[...] = jnp.zeros_like(st)

        st_prev = st[...]
        st_ref[0] = st_prev
        rmask = _gla_rowmask(t)
        b, b_last = _gla_decay(z_ref[...], rmask)
        q = q_ref[...] * (rmask * DK ** -0.5)
        k = k_ref[...] * rmask
        v = v_ref[...] * rmask
        qe = q * jnp.exp(b)
        ke = k * jnp.exp(-b)
        kd = k * jnp.exp(b_last - b)
        e_last = jnp.exp(b_last)
        causal = _iota((CH, CH), 0) >= _iota((CH, CH), 1)
        for h in range(GLA_HEADS):
            ks, vs_ = slice(h * DK, (h + 1) * DK), slice(h * DV, (h + 1) * DV)
            a = jnp.where(causal, _dot_nt(qe[:, ks], ke[:, ks]), 0.0)
            o_ref[:, vs_] = _dot(a, v[:, vs_]) + _dot_nt(qe[:, ks], st_prev[:, ks])
            st[:, ks] = st_prev[:, ks] * e_last[:, ks] + _dot_tn(v[:, vs_], kd[:, ks])

    blk = lambda w: pl.BlockSpec((CH, w), lambda t: (_gla_block(t), 0))
    return pl.pallas_call(
        body, name="gla_fwd", grid=(steps,),
        in_specs=[blk(kw), blk(kw), blk(vw), blk(kw)],
        out_specs=[blk(vw), pl.BlockSpec((1, DV, kw), lambda t: (t, 0, 0))],
        out_shape=[pltpu.HBM((_lp(), vw), F32), pltpu.HBM((steps, DV, kw), F32)],
        scratch_shapes=[pltpu.VMEM((DV, kw), F32)],
        compiler_params=_params(16, dimension_semantics=_seq()),
    )(*_hbm(qg, kg, vg, z))


def _post_mix(o_s, o_gla, r_g, h0, gn4, w_out, g1, b1):
    tm = _row_tile(384)
    lp = _lp()

    def body(os_ref, og_ref, r_ref, h0_ref, gn_ref, w_ref, g_ref, b_ref, o_ref, pre_ref, h1_ref):
        o_ref[:, 0:512] = os_ref[...].astype(BF16)
        for h in range(GLA_HEADS):
            hs = slice(h * DV, (h + 1) * DV)
            xg = og_ref[:, hs]
            n = xg * lax.rsqrt(jnp.mean(xg * xg, axis=-1, keepdims=True) + RMS_EPS) * gn_ref[...]
            r = r_ref[:, hs]
            o_ref[:, 512 + h * DV:512 + (h + 1) * DV] = (n * (r * _sigmoid(r))).astype(BF16)
        pre = ALPHA * h0_ref[...] + _dot(o_ref[...], w_ref[...])
        pre_ref[...] = pre
        xhat, _ = _ln_stats(pre)
        h1_ref[...] = xhat * g_ref[...] + b_ref[...]

    return pl.pallas_call(
        body, name="post_mix", grid=(lp // tm,),
        in_specs=[_rows(tm, 512), _rows(tm, 512), _rows(tm, 512), _rows(tm, D), _const((1, DV)), _const((D, D)),
                  _const((1, D)), _const((1, D))],
        out_specs=[_rows(tm, D), _rows(tm, D), _rows(tm, D)],
        out_shape=[pltpu.HBM((lp, D), BF16), pltpu.HBM((lp, D), F32),
                   pltpu.HBM((lp, D), F32)],
        compiler_params=_params(32, dimension_semantics=_seq()),
    )(*_hbm(o_s, o_gla, r_g, h0, gn4, w_out, g1, b1))


def _ffn_fwd(h1, wg_t, wu_t, wd):
    tm = _row_tile(192)
    lp = _lp()

    def body(h_ref, wg_ref, wu_ref, wd_ref, g_ref, u_ref, pre_ref):
        h = h_ref[...]
        g = _dot_nt(h, wg_ref[...])
        u = _dot_nt(h, wu_ref[...])
        g_ref[...] = g
        u_ref[...] = u
        pre_ref[...] = ALPHA * h + _dot(g * _sigmoid(g) * u, wd_ref[...])

    return pl.pallas_call(
        body, name="ffn_fwd", grid=(lp // tm,),
        in_specs=[_rows(tm, D), _const((D_FF, D)), _const((D_FF, D)), _const((D_FF, D))],
        out_specs=[_rows(tm, D_FF), _rows(tm, D_FF), _rows(tm, D)],
        out_shape=[pltpu.HBM((lp, D_FF), F32), pltpu.HBM((lp, D_FF), F32),
                   pltpu.HBM((lp, D), F32)],
        compiler_params=_params(48, dimension_semantics=_seq()),
    )(*_hbm(h1, wg_t, wu_t, wd))


def _ln2_loss_bwd(pre2, target, g2, b2):
    nb = SEQ // BLK

    def body(p_ref, t_ref, g_ref, b_ref, dp_ref, loss_ref, dg_ref, db_ref, acc):
        i = pl.program_id(0)

        @pl.when(i == 0)
        def _():
            acc[...] = jnp.zeros_like(acc)
            dg_ref[...] = jnp.zeros_like(dg_ref)
            db_ref[...] = jnp.zeros_like(db_ref)

        real = jnp.where(i < nb, 1.0, 0.0)
        xhat, rstd = _ln_stats(p_ref[...])
        diff = (xhat * g_ref[...] + b_ref[...] - t_ref[...]) * real
        acc[...] += jnp.sum(diff * diff, axis=0, keepdims=True)
        dy = diff * (1.0 / D)
        dp_ref[...] = _ln_bwd(dy, xhat, rstd, g_ref[...])
        dg_ref[...] += jnp.sum(dy * xhat, axis=0, keepdims=True)
        db_ref[...] += jnp.sum(dy, axis=0, keepdims=True)

        @pl.when(i == nb)
        def _():
            loss_ref[...] = jnp.zeros_like(loss_ref) + (0.5 / D) * jnp.sum(acc[...], axis=1, keepdims=True)

    return pl.pallas_call(
        body, name="ln2_loss_bwd", grid=(nb + 1,),
        in_specs=[_rows(BLK, D), pl.BlockSpec((BLK, D), lambda i: (jnp.minimum(i, nb - 1), 0)), _const((1, D)),
                  _const((1, D))],
        out_specs=[_rows(BLK, D), _acc((1, LANE)), _acc((1, D)), _acc((1, D))],
        out_shape=[pltpu.HBM((_lp(), D), F32), pltpu.HBM((1, LANE), F32),
                   pltpu.HBM((1, D), F32), pltpu.HBM((1, D), F32)],
        scratch_shapes=[pltpu.VMEM((1, D), F32)],
        compiler_params=_params(16, dimension_semantics=_seq()),
    )(*_hbm(pre2, target, g2, b2))


def _ffn_bwd(dpre2, g, u, pre1, wg_t, wu_t, wd, g1):
    tm = _row_tile(192)
    lp = _lp()

    def body(dp_ref, g_ref, u_ref, p1_ref, wg_ref, wu_ref, wd_ref, g1_ref, a_ref, dg_ref, du_ref, dp1_ref,
             dg1_ref, db1_ref):
        @pl.when(pl.program_id(0) == 0)
        def _():
            dg1_ref[...] = jnp.zeros_like(dg1_ref)
            db1_ref[...] = jnp.zeros_like(db1_ref)

        dp = dp_ref[...]
        gg, uu = g_ref[...], u_ref[...]
        sg = _sigmoid(gg)
        silu = gg * sg
        da = _dot_nt(dp, wd_ref[...])
        a_ref[...] = (silu * uu).astype(BF16)
        dgate = (da * uu * (sg * (1.0 + gg * (1.0 - sg)))).astype(BF16)
        dup = (da * silu).astype(BF16)
        dg_ref[...] = dgate
        du_ref[...] = dup
        dh1 = ALPHA * dp + _dot(dgate, wg_ref[...]) + _dot(dup, wu_ref[...])
        xhat, rstd = _ln_stats(p1_ref[...])
        dp1_ref[...] = _ln_bwd(dh1, xhat, rstd, g1_ref[...])
        dg1_ref[...] += jnp.sum(dh1 * xhat, axis=0, keepdims=True)
        db1_ref[...] += jnp.sum(dh1, axis=0, keepdims=True)

    return pl.pallas_call(
        body, name="ffn_bwd", grid=(lp // tm,),
        in_specs=[_rows(tm, D), _rows(tm, D_FF), _rows(tm, D_FF), _rows(tm, D), _const((D_FF, D)), _const((D_FF, D)),
                  _const((D_FF, D)), _const((1, D))],
        out_specs=[_rows(tm, D_FF), _rows(tm, D_FF), _rows(tm, D_FF), _rows(tm, D), _acc((1, D)), _acc((1, D))],
        out_shape=[pltpu.HBM((lp, D_FF), BF16)] * 3
        + [pltpu.HBM((lp, D), F32), pltpu.HBM((1, D), F32), pltpu.HBM((1, D), F32)],
        compiler_params=_params(52, dimension_semantics=_seq()),
    )(*_hbm(dpre2, g, u, pre1, wg_t, wu_t, wd, g1))


def _atb(a, b, name):
    lp = _lp()
    tm = _row_tile(384)
    n, w = a.shape[1], b.shape[1]
    bw = 512 if n * w * 4 > (4 << 20) else w

    def body(a_ref, b_ref, o_ref):
        @pl.when(pl.program_id(1) == 0)
        def _():
            o_ref[...] = jnp.zeros_like(o_ref)

        o_ref[...] += _dot_tn(a_ref[...], b_ref[...])

    return pl.pallas_call(
        body, name=name, grid=(w // bw, lp // tm),
        in_specs=[pl.BlockSpec((tm, n), lambda j, k: (k, 0)), pl.BlockSpec((tm, bw), lambda j, k: (k, j))],
        out_specs=pl.BlockSpec((n, bw), lambda j, k: (0, j)),
        out_shape=pltpu.HBM((n, w), F32),
        compiler_params=_params(48, dimension_semantics=_seq(2)),
    )(*_hbm(a, b))


def _out_bwd(dpre1, w_out, o_gla, r_g, gn4):
    tm = _row_tile(384)
    lp = _lp()

    def body(dp_ref, w_ref, og_ref, r_ref, gn_ref, dos_ref, dog_ref, dr_ref, dgn_ref):
        @pl.when(pl.program_id(0) == 0)
        def _():
            dgn_ref[...] = jnp.zeros_like(dgn_ref)

        do = _dot_nt(dp_ref[...], w_ref[...])
        dos_ref[...] = do[:, 0:512]
        gn = gn_ref[...]
        for h in range(GLA_HEADS):
            hs = slice(h * DV, (h + 1) * DV)
            xg = og_ref[:, hs]
            rstd = lax.rsqrt(jnp.mean(xg * xg, axis=-1, keepdims=True) + RMS_EPS)
            nx = xg * rstd
            r = r_ref[:, hs]
            sr = _sigmoid(r)
            d_o = do[:, 512 + h * DV:512 + (h + 1) * DV]
            dr_ref[:, hs] = d_o * (nx * gn) * (sr * (1.0 + r * (1.0 - sr)))
            dn = d_o * (r * sr)
            dgn_ref[...] += jnp.sum(dn * nx, axis=0, keepdims=True)
            dnx = dn * gn
            dog_ref[:, hs] = rstd * (dnx - nx * jnp.mean(dnx * nx, axis=-1, keepdims=True))

    return pl.pallas_call(
        body, name="out_bwd", grid=(lp // tm,),
        in_specs=[_rows(tm, D), _const((D, D)), _rows(tm, 512), _rows(tm, 512), _const((1, DV))],
        out_specs=[_rows(tm, 512), _rows(tm, 512), _rows(tm, 512), _acc((1, DV))],
        out_shape=[pltpu.HBM((lp, 512), F32)] * 3 + [pltpu.HBM((1, DV), F32)],
        compiler_params=_params(32, dimension_semantics=_seq()),
    )(*_hbm(dpre1, w_out, o_gla, r_g, gn4))


def _gla_bwd(qg, kg, vg, z, do_gla, st_all):
    nc = SEQ // CH
    steps = nc + 2
    kw, vw = GLA_HEADS * DK, GLA_HEADS * DV

    def body(q_ref, k_ref, v_ref, z_ref, do_ref, st_ref, dq_ref, dk_ref, dv_ref, dz_ref, dst):
        t = steps - 1 - pl.program_id(0)

        @pl.when(pl.program_id(0) == 0)
        def _():
            dst[...] = jnp.zeros_like(dst)

        rmask = _gla_rowmask(t)
        zz = z_ref[...]
        b, b_last = _gla_decay(zz, rmask)
        e_b, e_nb, e_kd, e_last = jnp.exp(b), jnp.exp(-b), jnp.exp(b_last - b), jnp.exp(b_last)
        q = q_ref[...] * (rmask * DK ** -0.5)
        k = k_ref[...] * rmask
        v = v_ref[...] * rmask
        qe, ke, kd = q * e_b, k * e_nb, k * e_kd
        st_prev = st_ref[0]
        dst_new = dst[...]
        causal = _iota((CH, CH), 0) >= _iota((CH, CH), 1)
        dqe_parts, dke_parts, dkd_parts = [], [], []
        for h in range(GLA_HEADS):
            ks, vs_ = slice(h * DK, (h + 1) * DK), slice(h * DV, (h + 1) * DV)
            d_o = do_ref[:, vs_]
            a = jnp.where(causal, _dot_nt(qe[:, ks], ke[:, ks]), 0.0)
            da = jnp.where(causal, _dot_nt(d_o, v[:, vs_]), 0.0)
            dqe_parts.append(_dot(d_o, st_prev[:, ks]) + _dot(da, ke[:, ks]))
            dke_parts.append(_dot_tn(da, qe[:, ks]))
            dkd_parts.append(_dot(v[:, vs_], dst_new[:, ks]))
            dv_ref[:, vs_] = _dot_tn(a, d_o) + _dot_nt(kd[:, ks], dst_new[:, ks])
            dst[:, ks] = dst_new[:, ks] * e_last[:, ks] + _dot_tn(d_o, qe[:, ks])
        dqe = jnp.concatenate(dqe_parts, axis=1)
        dke = jnp.concatenate(dke_parts, axis=1)
        dkd = jnp.concatenate(dkd_parts, axis=1)
        dq_ref[...] = dqe * e_b * (rmask * DK ** -0.5)
        dk_ref[...] = (dke * e_nb + dkd * e_kd) * rmask
        dkd_kd = dkd * kd
        db = dqe * qe - dke * ke - dkd_kd
        db_last = jnp.sum(dst_new * st_prev, axis=0, keepdims=True) * e_last + jnp.sum(dkd_kd, axis=0, keepdims=True)
        triu = (_iota((CH, CH), 0) <= _iota((CH, CH), 1)).astype(F32)
        dlog_g = _dot_exact(triu, db) + db_last
        dz_ref[...] = dlog_g * (rmask / GLA_TAU) * _sigmoid(-zz)

    blk = lambda w: pl.BlockSpec((CH, w), lambda s: (_gla_block(steps - 1 - s), 0))
    return pl.pallas_call(
        body, name="gla_bwd", grid=(steps,),
        in_specs=[blk(kw), blk(kw), blk(vw), blk(kw), blk(vw), pl.BlockSpec((1, DV, kw), lambda s: (steps - 1 - s, 0, 0))],
        out_specs=[blk(kw), blk(kw), blk(vw), blk(kw)],
        out_shape=[pltpu.HBM((_lp(), kw), F32), pltpu.HBM((_lp(), kw), F32),
                   pltpu.HBM((_lp(), vw), F32), pltpu.HBM((_lp(), kw), F32)],
        scratch_shapes=[pltpu.VMEM((DV, kw), F32)],
        compiler_params=_params(16, dimension_semantics=_seq()),
    )(*_hbm(qg, kg, vg, z, do_gla, st_all))


def _swa_bwd(sinks, qs, ks, vs, do_s):
    nb = SEQ // BLK
    g = SWA_HEADS // SWA_KV_HEADS
    kvw = SWA_KV_HEADS * DH
    scale = DH ** -0.5

    def body(sink_ref, q_ref, km_ref, kp_ref, kc_ref, vm_ref, vp_ref, vc_ref, do_ref,
             dq_ref, dk_ref, dv_ref, dsink_ref, carry_k, carry_v, meta_k, meta_v):
        n = pl.program_id(0)

        @pl.when(n == 0)
        def _():
            for r in (carry_k, carry_v, meta_k, meta_v):
                r[...] = jnp.zeros_like(r)
            dsink_ref[...] = jnp.zeros_like(dsink_ref)

        @pl.when(n <= nb)
        def _():
            dists, valids = _swa_masks(n)
            lane = _iota((1, LANE), 1)
            dk_parts = [[], [], []]
            dv_parts = [[], [], []]
            for kvh in range(SWA_KV_HEADS):
                kv = slice(kvh * DH, (kvh + 1) * DH)
                kb = (km_ref[:, kv], kp_ref[:, kv], kc_ref[:, kv])
                vb = (vm_ref[:, kv], vp_ref[:, kv], vc_ref[:, kv])
                dk_acc = [jnp.zeros((BLK, DH), F32) for _ in range(3)]
                dv_acc = [jnp.zeros((BLK, DH), F32) for _ in range(3)]
                for gi in range(g):
                    h = kvh * g + gi
                    hs = slice(h * DH, (h + 1) * DH)
                    q_h, do_h = q_ref[:, hs], do_ref[:, hs]
                    *p, p_sink = _swa_probs(q_h, kb, sink_ref[h], 2.0 ** -(h + 1), dists, valids)
                    dp = [_dot_nt(do_h, vx) for vx in vb]
                    delta = sum(jnp.sum(px * dx, axis=-1, keepdims=True) for px, dx in zip(p, dp))
                    ds = [px * (dx - delta) for px, dx in zip(p, dp)]
                    dq_ref[:, hs] = scale * sum(_dot(dsx, kx) for dsx, kx in zip(ds, kb))
                    for j in range(3):
                        dk_acc[j] = dk_acc[j] + scale * _dot_tn(ds[j], q_h)
                        dv_acc[j] = dv_acc[j] + _dot_tn(p[j], do_h)
                    dsink_ref[...] += jnp.where(lane == h, -jnp.sum(p_sink * delta, axis=0, keepdims=True), 0.0)
                for j in range(3):
                    dk_parts[j].append(dk_acc[j])
                    dv_parts[j].append(dv_acc[j])
            dk3 = [jnp.concatenate(x, axis=1) for x in dk_parts]
            dv3 = [jnp.concatenate(x, axis=1) for x in dv_parts]
            meta_k[...] += dk3[0]
            meta_v[...] += dv3[0]
            dk_ref[...] = carry_k[...] + dk3[1]
            dv_ref[...] = carry_v[...] + dv3[1]
            carry_k[...] = dk3[2]
            carry_v[...] = dv3[2]

        @pl.when(n == nb + 1)
        def _():
            dk_ref[...] = meta_k[...]
            dv_ref[...] = meta_v[...]

    kv_out = pl.BlockSpec((BLK, kvw), lambda n: (jnp.where(n == nb + 1, nb, jnp.clip(n - 1, 0, nb - 1)), 0))
    qblk = pl.BlockSpec((BLK, SWA_HEADS * DH), lambda n: (jnp.minimum(n, nb), 0))
    return pl.pallas_call(
        body, name="swa_bwd", grid=(nb + 2,),
        in_specs=[pl.BlockSpec(memory_space=pltpu.SMEM), qblk] + _swa_kv_specs(kvw) + _swa_kv_specs(kvw) + [qblk],
        out_specs=[qblk, kv_out, kv_out, _acc((1, LANE))],
        out_shape=[pltpu.HBM((_lp(), SWA_HEADS * DH), F32), pltpu.HBM((_lp(), kvw), F32),
                   pltpu.HBM((_lp(), kvw), F32), pltpu.HBM((1, LANE), F32)],
        scratch_shapes=[pltpu.VMEM((BLK, kvw), F32)] * 4,
        compiler_params=_params(16, dimension_semantics=_seq()),
    )(sinks, *_hbm(qs, ks, ks, ks, vs, vs, vs, do_s))


def _in_bwd(dqs, dks, dvs, dqg, dkg, dvg, drg, dz, dpre1, w_in_t, wg2_p):
    tm = _row_tile(384)
    lp = _lp()
    widths = (512, 128, 128, 256, 256, 512, 512)
    offs = (O_QS, O_KS, O_VS, O_QG, O_KG, O_VG, O_RG)

    def body(*refs):
        parts, (dz_ref, dp1_ref, w_ref, wg2_ref, dproj_ref, dh0_ref, dbin_ref, dbg_ref) = refs[:7], refs[7:]

        @pl.when(pl.program_id(0) == 0)
        def _():
            dbin_ref[...] = jnp.zeros_like(dbin_ref)
            dbg_ref[...] = jnp.zeros_like(dbg_ref)

        for p_ref, off, wd in zip(parts, offs, widths):
            val = p_ref[...]
            dproj_ref[:, off:off + wd] = val.astype(BF16)
            dbin_ref[:, off:off + wd] += jnp.sum(val, axis=0, keepdims=True)
        dz = dz_ref[...]
        dlr = _dot_nt(dz, wg2_ref[...])
        dproj_ref[:, O_LR:O_LR + LANE] = dlr.astype(BF16)
        dbin_ref[:, O_LR:O_LR + LANE] += jnp.sum(dlr, axis=0, keepdims=True)
        dbg_ref[...] += jnp.sum(dz, axis=0, keepdims=True)
        dh0_ref[...] = ALPHA * dp1_ref[...] + _dot(dproj_ref[...], w_ref[...])

    return pl.pallas_call(
        body, name="in_bwd", grid=(lp // tm,),
        in_specs=[_rows(tm, w) for w in widths] + [_rows(tm, 256), _rows(tm, D), _const((D_IN_P, D)), _const((LANE, 256))],
        out_specs=[_rows(tm, D_IN_P), _rows(tm, D), _acc((1, D_IN_P)), _acc((1, 256))],
        out_shape=[pltpu.HBM((lp, D_IN_P), BF16), pltpu.HBM((lp, D), F32),
                   pltpu.HBM((1, D_IN_P), F32), pltpu.HBM((1, 256), F32)],
        compiler_params=_params(40, dimension_semantics=_seq()),
    )(*_hbm(dqs, dks, dvs, dqg, dkg, dvg, drg, dz, dpre1, w_in_t, wg2_p))


def _ln_in_bwd(x, meta_ext, dh0, g):
    nb = SEQ // BLK

    def body(x_ref, m_ref, dh_ref, g_ref, dx_ref, dm_ref, dg_ref, db_ref):
        i = pl.program_id(0)

        @pl.when(i == 0)
        def _():
            dg_ref[...] = jnp.zeros_like(dg_ref)
            db_ref[...] = jnp.zeros_like(db_ref)

        xin = jnp.where(i < nb, x_ref[...], m_ref[...])
        xhat, rstd = _ln_stats(xin)
        dh = dh_ref[...]
        dxin = _ln_bwd(dh, xhat, rstd, g_ref[...])
        dg_ref[...] += jnp.sum(dh * xhat, axis=0, keepdims=True)
        db_ref[...] += jnp.sum(dh, axis=0, keepdims=True)

        @pl.when(i < nb)
        def _():
            dx_ref[...] = dxin

        @pl.when(i == nb)
        def _():
            dm_ref[...] = dxin

    xblk = pl.BlockSpec((BLK, D), lambda i: (jnp.minimum(i, nb - 1), 0))
    return pl.pallas_call(
        body, name="ln_in_bwd", grid=(nb + 1,),
        in_specs=[xblk, _const((BLK, D)), _rows(BLK, D), _const((1, D))],
        out_specs=[xblk, _acc((BLK, D)), _acc((1, D)), _acc((1, D))],
        out_shape=[pltpu.HBM((SEQ, D), F32), pltpu.HBM((BLK, D), F32),
                   pltpu.HBM((1, D), F32), pltpu.HBM((1, D), F32)],
        compiler_params=_params(16, dimension_semantics=_seq()),
    )(*_hbm(x, meta_ext, dh0, g))


def _local_step(x, target, meta_full, ln_in_g, ln_in_b, w_in_t, b_in, wg2, bg2, sinks, gn, w_out, g1, b1,
                wg_t, wu_t, wd, g2, b2):
    row = lambda v: v.reshape(1, -1).astype(F32)
    meta_ext = jnp.pad(meta_full, ((META_OFF, BLK - CH), (0, 0)))
    b_in_p = jnp.pad(row(b_in), ((0, 0), (0, D_IN_P - D_IN)))
    wg2_p = jnp.pad(wg2, ((0, LANE - wg2.shape[0]), (0, 0))).astype(BF16)
    gn4 = row(gn)
    sinks = sinks.reshape(-1).astype(F32)

    h0 = _ln_in_fwd(x, meta_ext, row(ln_in_g), row(ln_in_b))
    qs, ks, vs, qg, kg, vg, rg, glr, z = _in_proj(h0, w_in_t, b_in_p, wg2_p, row(bg2))
    o_s = _swa_fwd(sinks, qs, ks, vs)
    o_gla, st_all = _gla_fwd(qg, kg, vg, z)
    o, pre1, h1 = _post_mix(o_s, o_gla, rg, h0, gn4, w_out, row(g1), row(b1))
    g, u, pre2 = _ffn_fwd(h1, wg_t, wu_t, wd)

    dpre2, loss, dg2, db2 = _ln2_loss_bwd(pre2, target, row(g2), row(b2))
    a, dgate, dup, dpre1, dg1, db1 = _ffn_bwd(dpre2, g, u, pre1, wg_t, wu_t, wd, row(g1))
    dwd = _atb(a, dpre2, "dw_down")
    dwg_t = _atb(dgate, h1, "dw_gate")
    dwu_t = _atb(dup, h1, "dw_up")
    dw_out = _atb(o, dpre1, "dw_out")
    do_s, do_gla, drg, dgn = _out_bwd(dpre1, w_out, o_gla, rg, gn4)
    dqg, dkg, dvg, dz = _gla_bwd(qg, kg, vg, z, do_gla, st_all)
    dqs, dks, dvs, dsinks = _swa_bwd(sinks, qs, ks, vs, do_s)
    dproj, dh0, db_in_p, dbg2 = _in_bwd(dqs, dks, dvs, dqg, dkg, dvg, drg, dz, dpre1, w_in_t, wg2_p)
    dw_in_t = _atb(dproj, h0, "dw_in")
    dwg2_p = _atb(glr, dz, "dw_gate_lr2")
    dx, dmeta_blk, dg_in, db_in_ln = _ln_in_bwd(x, meta_ext, dh0, row(ln_in_g))

    grads = dict(
        w_in=dw_in_t, w_out=dw_out, w_g=dwg_t, w_u=dwu_t, w_d=dwd,
        meta=dmeta_blk[META_OFF:CH], ln_in_g=dg_in, ln_in_b=db_in_ln, ln1_g=dg1, ln1_b=db1, ln2_g=dg2, ln2_b=db2,
        b_in=db_in_p[:, :D_IN], wg2=dwg2_p[:wg2.shape[0]], bg2=dbg2, sinks=dsinks[:, :SWA_HEADS], gn=dgn)
    return loss[0, 0], dx, grads


HBM = pl.BlockSpec(memory_space=pltpu.HBM)


def _place():
    return lax.axis_index("x"), lax.axis_index("y"), lax.axis_index("c")


def _other_chips(x, y):
    return [(1 - x, y), (x, 1 - y), (1 - x, 1 - y)]


def _dma_sems(n):
    return pltpu.SemaphoreType.DMA((n,))


def _comm_params():
    return pltpu.CompilerParams(has_side_effects=True)


def _gather_halves(shards):
    n = len(shards)

    def body(*refs):
        ins, outs = refs[:n], refs[n:2 * n]
        ici_send, ici_recv, d2d_send, d2d_recv = refs[2 * n:]
        x, y, c = _place()
        mine = 2 * x + y
        chips = _other_chips(x, y)

        def ici(a, j, src_chip):
            px, py = chips[j]
            return pltpu.make_async_remote_copy(ins[a].at[c], outs[a].at[src_chip, c], ici_send.at[3 * a + j],
                                                ici_recv.at[3 * a + j], device_id=(px, py, c), device_id_type=MESH)

        def d2d(a, j, half):
            px, py = chips[j]
            blk = outs[a].at[2 * px + py, half]
            return pltpu.make_async_remote_copy(blk, blk, d2d_send.at[3 * a + j], d2d_recv.at[3 * a + j],
                                                device_id=(x, y, 1 - c), device_id_type=MESH)

        sends = [ici(a, j, mine) for a in range(n) for j in range(3)]
        for cp in sends:
            cp.start()
        passed = []
        for a in range(n):
            for j, (px, py) in enumerate(chips):
                ici(a, j, 2 * px + py).wait_recv()
                fwd = d2d(a, j, c)
                fwd.start()
                passed.append(fwd)
        for a in range(n):
            for j in range(3):
                d2d(a, j, 1 - c).wait_recv()
        for cp in sends + passed:
            cp.wait_send()

    gathered = pl.pallas_call(
        body, name="gather_halves",
        in_specs=[HBM] * n, out_specs=[HBM] * n,
        out_shape=[pltpu.HBM((N_CHIPS,) + s.shape, s.dtype) for s in shards],
        scratch_shapes=[_dma_sems(3 * n)] * 4,
        compiler_params=_comm_params(),
    )(*_hbm(*shards))
    mine = 2 * lax.axis_index("x") + lax.axis_index("y")
    return [lax.dynamic_update_index_in_dim(g, s, mine, axis=0) for g, s in zip(gathered, shards)]


def _sibling_exchange(grads):
    n = len(grads)

    def body(*refs):
        ins, outs = refs[:n], refs[n:2 * n]
        send_sems, recv_sems = refs[2 * n:]
        x, y, c = _place()
        copies = []
        for a in range(n):
            for s in range(N_CHIPS):
                cp = pltpu.make_async_remote_copy(ins[a].at[s, 1 - c], outs[a].at[s], send_sems.at[N_CHIPS * a + s],
                                                  recv_sems.at[N_CHIPS * a + s], device_id=(x, y, 1 - c),
                                                  device_id_type=MESH)
                cp.start()
                copies.append(cp)
        for cp in copies:
            cp.wait_recv()
        for cp in copies:
            cp.wait_send()

    return pl.pallas_call(
        body, name="sibling_exchange", in_specs=[HBM] * n, out_specs=[HBM] * n,
        out_shape=[pltpu.HBM((N_CHIPS, g.shape[2], D), F32) for g in grads],
        scratch_shapes=[_dma_sems(N_CHIPS * n)] * 2,
        compiler_params=_comm_params(),
    )(*_hbm(*grads))


def _add_halves(core, grad, recv, dtype, name):
    h = grad.shape[2]

    def body(c_ref, a_ref, b_ref, o_ref):
        o_ref[...] = (a_ref[0] + b_ref[...]).astype(dtype)

    return pl.pallas_call(
        body, name=name,
        grid_spec=pltpu.PrefetchScalarGridSpec(
            num_scalar_prefetch=1, grid=(N_CHIPS,),
            in_specs=[pl.BlockSpec((1, 1, h, D), lambda s, c: (s, c[0], 0, 0)),
                      pl.BlockSpec((1, h, D), lambda s, c: (s, 0, 0))],
            out_specs=pl.BlockSpec((1, h, D), lambda s, c: (s, 0, 0))),
        out_shape=pltpu.HBM((N_CHIPS, h, D), dtype),
        compiler_params=_params(16, dimension_semantics=_seq()),
    )(core, *_hbm(grad, recv))


def _chip_scatter(parts, with_own):
    n = len(parts)

    def body(*refs):
        ins, outs = refs[:n], refs[n:2 * n]
        send_sems, recv_sems, local_sems = refs[2 * n:]
        x, y, c = _place()
        mine = 2 * x + y
        chips = _other_chips(x, y)
        local = [pltpu.make_async_copy(ins[a].at[mine], outs[a].at[mine], local_sems.at[a]) for a in range(n)
                 if with_own[a]]
        for cp in local:
            cp.start()
        sends = []
        for a in range(n):
            for j, (px, py) in enumerate(chips):
                cp = pltpu.make_async_remote_copy(ins[a].at[2 * px + py], outs[a].at[mine], send_sems.at[3 * a + j],
                                                  recv_sems.at[3 * a + j], device_id=(px, py, c), device_id_type=MESH)
                cp.start()
                sends.append(cp)
        for a in range(n):
            for j, (px, py) in enumerate(chips):
                pltpu.make_async_remote_copy(ins[a].at[mine], outs[a].at[2 * px + py], send_sems.at[3 * a + j],
                                             recv_sems.at[3 * a + j], device_id=(px, py, c),
                                             device_id_type=MESH).wait_recv()
        for cp in sends:
            cp.wait_send()
        for cp in local:
            cp.wait()

    return pl.pallas_call(
        body, name="chip_scatter", in_specs=[HBM] * n, out_specs=[HBM] * n,
        out_shape=[pltpu.HBM(p.shape, p.dtype) for p in parts],
        scratch_shapes=[_dma_sems(3 * n)] * 2 + [_dma_sems(n)],
        compiler_params=_comm_params(),
    )(*_hbm(*parts))


def _sum_chips(slots, first, rest, name):
    h = first.shape[1]

    def body(i_ref, a_ref, b_ref, c_ref, d_ref, o_ref):
        o_ref[...] = ((a_ref[...].astype(F32) + b_ref[...].astype(F32)) + c_ref[...].astype(F32)) + d_ref[...].astype(F32)

    slab = lambda k: pl.BlockSpec((1, h, D), lambda i, ix: (ix[k], 0, 0))
    return pl.pallas_call(
        body, name=name,
        grid_spec=pltpu.PrefetchScalarGridSpec(num_scalar_prefetch=1, grid=(1,),
                                               in_specs=[slab(0), slab(1), slab(2), slab(3)], out_specs=slab(4)),
        out_shape=pltpu.HBM((2, h, D), F32),
        compiler_params=_params(16, dimension_semantics=_seq()),
    )(slots, *_hbm(first, rest, rest, rest))


def _join_halves(halves):
    n = len(halves)

    def body(*refs):
        outs = refs[n:2 * n]
        send_sems, recv_sems = refs[2 * n:]
        x, y, c = _place()

        def copy(a, slab):
            return pltpu.make_async_remote_copy(outs[a].at[slab], outs[a].at[slab], send_sems.at[a], recv_sems.at[a],
                                                device_id=(x, y, 1 - c), device_id_type=MESH)

        for a in range(n):
            copy(a, c).start()
        for a in range(n):
            copy(a, 1 - c).wait_recv()
        for a in range(n):
            copy(a, c).wait_send()

    return pl.pallas_call(
        body, name="join_halves", in_specs=[HBM] * n, out_specs=[HBM] * n,
        out_shape=[pltpu.HBM(h.shape, F32) for h in halves],
        input_output_aliases={a: a for a in range(n)},
        scratch_shapes=[_dma_sems(n)] * 2,
        compiler_params=_comm_params(),
    )(*_hbm(*halves))


def _reduce_scatter(grads, wire_dtypes, same_order, names):
    x, y, c = _place()
    core = c.astype(jnp.int32).reshape(1)
    others = [2 * px + py for px, py in _other_chips(x, y)]
    own_first = jnp.stack([2 * x + y] + others + [c]).astype(jnp.int32)
    chip_order = jnp.stack([0 * c, 0 * c + 1, 0 * c + 2, 0 * c + 3, c]).astype(jnp.int32)
    recv = _sibling_exchange(grads)
    parts = [_add_halves(core, g, r, dt, "add_halves_" + nm) for g, r, dt, nm in zip(grads, recv, wire_dtypes, names)]
    got = _chip_scatter(parts, same_order)
    halves = [_sum_chips(chip_order, q, q, "sum_chips_" + nm) if fixed else _sum_chips(own_first, p, q, "sum_chips_" + nm)
              for p, q, fixed, nm in zip(parts, got, same_order, names)]
    return [f.reshape(2 * f.shape[1], D) for f in _join_halves(halves)]


def _adamw(w, g, m, v, name):
    rows, cols = w.shape
    tr = max(t for t in range(8, 257, 8) if rows % t == 0) if rows % 8 == 0 else rows

    def body(w_ref, g_ref, m_ref, v_ref, d_ref, nm_ref, nv_ref):
        gg = g_ref[...]
        nm = ADAM_B1 * m_ref[...] + (1.0 - ADAM_B1) * gg
        nv = ADAM_B2 * v_ref[...] + (1.0 - ADAM_B2) * (gg * gg)
        m_hat = nm / (1.0 - ADAM_B1 ** ADAM_STEP)
        v_hat = nv / (1.0 - ADAM_B2 ** ADAM_STEP)
        d_ref[...] = -ADAM_LR * (m_hat / (jnp.sqrt(v_hat) + ADAM_EPS) + ADAM_WD * w_ref[...])
        nm_ref[...] = nm
        nv_ref[...] = nv

    blk = pl.BlockSpec((tr, cols), lambda i: (i, 0))
    return pl.pallas_call(
        body, name=name, grid=(rows // tr,),
        in_specs=[blk] * 4, out_specs=[blk] * 3,
        out_shape=[pltpu.HBM(w.shape, F32)] * 3,
        compiler_params=_params(32, dimension_semantics=_seq()),
    )(*_hbm(w, g, m, v))


def _flat_rows(v, rows):
    flat = v.reshape(-1).astype(F32)
    return jnp.pad(flat, (0, rows * D - flat.shape[0])).reshape(rows, D)


def _small_pack(gr):
    tail = jnp.concatenate([gr["bg2"].reshape(-1), gr["sinks"].reshape(-1), gr["gn"].reshape(-1)])
    rows = [gr["meta"].reshape(N_META, D)] + [gr[k].reshape(1, D) for k in
                                              ("ln_in_g", "ln_in_b", "ln1_g", "ln1_b", "ln2_g", "ln2_b")]
    rows += [_flat_rows(gr["b_in"], 3), _flat_rows(gr["wg2"], 4), _flat_rows(tail, 1)]
    packed = jnp.concatenate(rows, axis=0)
    return jnp.pad(packed, ((0, SMALL_ROWS - packed.shape[0]), (0, 0)))


def _small_unpack(p):
    flat = lambda r0, n, size: p[r0:r0 + n].reshape(-1)[:size]
    tail = p[29]
    return dict(meta=p[0:N_META], ln_in_g=p[16], ln_in_b=p[17], ln1_g=p[18], ln1_b=p[19], ln2_g=p[20], ln2_b=p[21],
                b_in=flat(22, 3, D_IN), wg2=flat(25, 4, 16 * 256).reshape(16, 256), bg2=tail[0:256],
                sinks=tail[256:256 + SWA_HEADS], gn=tail[256 + SWA_HEADS:256 + SWA_HEADS + DV])


BIG = ("w_in", "w_out", "w_g", "w_u", "w_d")


def kernel(x, meta_tokens, ln_in_g, ln_in_b, w_in, b_in, w_gate_lr2, b_gate_lr2, attn_sinks, gla_norm_g, w_out, ln1_g, ln1_b, w_ffn_gate, w_ffn_up, w_ffn_down, ln2_g, ln2_b, loss_target, m_meta_tokens, m_ln_in_g, m_ln_in_b, m_w_in, m_b_in, m_w_gate_lr2, m_b_gate_lr2, m_attn_sinks, m_gla_norm_g, m_w_out, m_ln1_g, m_ln1_b, m_w_ffn_gate, m_w_ffn_up, m_w_ffn_down, m_ln2_g, m_ln2_b, v_meta_tokens, v_ln_in_g, v_ln_in_b, v_w_in, v_b_in, v_w_gate_lr2, v_b_gate_lr2, v_attn_sinks, v_gla_norm_g, v_w_out, v_ln1_g, v_ln1_b, v_w_ffn_gate, v_w_ffn_up, v_w_ffn_down, v_ln2_g, v_ln2_b):
    chip = 2 * lax.axis_index("x") + lax.axis_index("y")

    halves = lambda a: a.reshape(2, a.shape[0] // 2, a.shape[1])
    r_in = SHARD_ROWS["w_in"]
    shards = [jnp.pad(w_in[0].T.astype(BF16), ((0, W_IN_WIN - r_in), (0, 0))), w_out[0].astype(BF16),
              w_ffn_gate[0].T.astype(BF16), w_ffn_up[0].T.astype(BF16), w_ffn_down[0].astype(BF16), meta_tokens,
              w_gate_lr2[0]]
    g_in, g_out, g_g, g_u, g_d, g_meta, g_wg2 = _gather_halves([halves(a) for a in shards])
    w_in_t = jnp.pad(g_in.reshape(N_CHIPS, W_IN_WIN, D)[:, :r_in].reshape(D_IN, D), ((0, D_IN_P - D_IN), (0, 0)))
    meta_full = jnp.concatenate([g_meta[s].reshape(N_META, -1) for s in range(N_CHIPS)], axis=1)
    wg2_full = jnp.concatenate([g_wg2[s].reshape(w_gate_lr2.shape[1], -1) for s in range(N_CHIPS)], axis=1)

    loss_part, dx, gr = _local_step(
        x[0], loss_target[0], meta_full, ln_in_g, ln_in_b, w_in_t, b_in[0], wg2_full, b_gate_lr2[0], attn_sinks[0],
        gla_norm_g[0], g_out.reshape(D, D), ln1_g[0], ln1_b[0], g_g.reshape(D_FF, D), g_u.reshape(D_FF, D),
        g_d.reshape(D_FF, D), ln2_g[0], ln2_b[0])
    loss = lax.psum(loss_part, ("x", "y", "c"))

    win_start = [s * r_in // BF16_ROWS * BF16_ROWS for s in range(N_CHIPS)]
    to_send = [jnp.stack([gr["w_in"][st:st + W_IN_WIN] for st in win_start])] + [gr[k] for k in BIG[1:]]
    to_send.append(jnp.broadcast_to(_small_pack(gr), (N_CHIPS, SMALL_ROWS, D)))
    to_send = [a.reshape(N_CHIPS, 2, -1, D) for a in to_send]
    red = _reduce_scatter(to_send, [BF16] * len(BIG) + [F32], [False] * len(BIG) + [True], list(BIG) + ["small"])

    big_g = dict(zip(BIG, red))
    big_g["w_in"] = lax.dynamic_slice_in_dim(red[0], chip * (r_in % BF16_ROWS), r_in, axis=0)
    sg = _small_unpack(red[-1])
    col = lambda a, width: lax.dynamic_slice_in_dim(a, chip * width, width, axis=1)
    grads = dict(
        meta_tokens=col(sg["meta"], D // N_CHIPS), ln_in_g=sg["ln_in_g"], ln_in_b=sg["ln_in_b"],
        w_in=big_g["w_in"].T[None], b_in=sg["b_in"][None], w_gate_lr2=col(sg["wg2"], 256 // N_CHIPS)[None],
        b_gate_lr2=sg["bg2"][None], attn_sinks=sg["sinks"][None], gla_norm_g=sg["gn"][None],
        w_out=big_g["w_out"][None], ln1_g=sg["ln1_g"][None], ln1_b=sg["ln1_b"][None],
        w_ffn_gate=big_g["w_g"].T[None], w_ffn_up=big_g["w_u"].T[None], w_ffn_down=big_g["w_d"][None],
        ln2_g=sg["ln2_g"][None], ln2_b=sg["ln2_b"][None])
    weights = dict(meta_tokens=meta_tokens, ln_in_g=ln_in_g, ln_in_b=ln_in_b, w_in=w_in, b_in=b_in,
                   w_gate_lr2=w_gate_lr2, b_gate_lr2=b_gate_lr2, attn_sinks=attn_sinks, gla_norm_g=gla_norm_g,
                   w_out=w_out, ln1_g=ln1_g, ln1_b=ln1_b, w_ffn_gate=w_ffn_gate, w_ffn_up=w_ffn_up,
                   w_ffn_down=w_ffn_down, ln2_g=ln2_g, ln2_b=ln2_b)
    m_in = dict(meta_tokens=m_meta_tokens, ln_in_g=m_ln_in_g, ln_in_b=m_ln_in_b, w_in=m_w_in, b_in=m_b_in,
                w_gate_lr2=m_w_gate_lr2, b_gate_lr2=m_b_gate_lr2, attn_sinks=m_attn_sinks, gla_norm_g=m_gla_norm_g,
                w_out=m_w_out, ln1_g=m_ln1_g, ln1_b=m_ln1_b, w_ffn_gate=m_w_ffn_gate, w_ffn_up=m_w_ffn_up,
                w_ffn_down=m_w_ffn_down, ln2_g=m_ln2_g, ln2_b=m_ln2_b)
    v_in = dict(meta_tokens=v_meta_tokens, ln_in_g=v_ln_in_g, ln_in_b=v_ln_in_b, w_in=v_w_in, b_in=v_b_in,
                w_gate_lr2=v_w_gate_lr2, b_gate_lr2=v_b_gate_lr2, attn_sinks=v_attn_sinks, gla_norm_g=v_gla_norm_g,
                w_out=v_w_out, ln1_g=v_ln1_g, ln1_b=v_ln1_b, w_ffn_gate=v_w_ffn_gate, w_ffn_up=v_w_ffn_up,
                w_ffn_down=v_w_ffn_down, ln2_g=v_ln2_g, ln2_b=v_ln2_b)
    names = list(weights)
    big_names = ("w_in", "w_out", "w_ffn_gate", "w_ffn_up", "w_ffn_down")

    delta, new_m, new_v = {}, {}, {}
    for k in big_names:
        two_d = lambda a: a.reshape(a.shape[-2], a.shape[-1])
        d_, m_, v_ = _adamw(two_d(weights[k]), two_d(grads[k]), two_d(m_in[k]), two_d(v_in[k]), "adamw_" + k)
        delta[k], new_m[k], new_v[k] = (t.reshape(weights[k].shape) for t in (d_, m_, v_))
    small_names = [k for k in names if k not in big_names]
    sizes = [weights[k].size for k in small_names]
    rows_small = -(-sum(sizes) // D)
    rows_small += -rows_small % 8
    cat = lambda src: _flat_rows(jnp.concatenate([src[k].reshape(-1) for k in small_names]), rows_small)
    d_, m_, v_ = _adamw(cat(weights), cat(grads), cat(m_in), cat(v_in), "adamw_small")
    off = 0
    for k, n in zip(small_names, sizes):
        for dst, src in ((delta, d_), (new_m, m_), (new_v, v_)):
            dst[k] = src.reshape(-1)[off:off + n].reshape(weights[k].shape)
        off += n
    grads = {k: grads[k].reshape(weights[k].shape) for k in names}

    return (loss, dx[None], *[grads[k] for k in names], *[delta[k] for k in names], *[new_m[k] for k in names],
            *[new_v[k] for k in names])
```

```python
import functools

import jax
import jax.numpy as jnp
from jax import lax
from jax.experimental import pallas as pl
from jax.experimental.pallas import tpu as pltpu

F32 = jnp.float32
BF16 = jnp.bfloat16
MESH = pl.DeviceIdType.MESH

D = 1024
SEQ = 4096
N_META = 16
SWA_HEADS, SWA_KV_HEADS, DH = 8, 2, 64
WINDOW = 128
GLA_HEADS, DK, DV = 4, 64, 128
GLA_TAU = 16.0
CH = 64
D_FF = 2816
D_IN = 2320
LN_EPS = 1e-5
RMS_EPS = 1e-6
ALPHA = 2.0 ** 0.25
NEG = -1e30
ADAM_LR, ADAM_B1, ADAM_B2, ADAM_EPS, ADAM_WD, ADAM_STEP = 0.001, 0.9, 0.999, 1e-8, 0.01, 10
O_QS, O_KS, O_VS, O_QG, O_KG, O_VG, O_RG, O_LR = 0, 512, 640, 768, 1024, 1280, 1792, 2304

LANE = 128
BLK = WINDOW
D_IN_P = D_IN + LANE - 16
META_OFF = CH - N_META
HEAD_POS = (0, 4, 1, 5, 2, 6, 3, 7)
HEAD_INV = tuple(HEAD_POS.index(h) for h in range(SWA_HEADS))
N_CHIPS = 4
SHARD_ROWS = dict(w_in=D_IN // N_CHIPS, w_out=D // N_CHIPS, w_g=D_FF // N_CHIPS, w_u=D_FF // N_CHIPS,
                  w_d=D_FF // N_CHIPS)
SMALL_ROWS = 32
BF16_ROWS = 16
W_IN_WIN = -(-SHARD_ROWS["w_in"] // (2 * BF16_ROWS)) * 2 * BF16_ROWS
VMEM_CAP_MB = 64


def _lp():
    return SEQ + BLK


def _row_tile(cap):
    lp = _lp()
    return max(t for t in range(16, cap + 1, 16) if lp % t == 0)


def _params(vmem_mb, **kw):
    assert vmem_mb <= VMEM_CAP_MB - 6
    return pltpu.CompilerParams(vmem_limit_bytes=vmem_mb << 20, **kw)


def _seq(n=1):
    return ("arbitrary",) * n


def _const(shape):
    return pl.BlockSpec(shape, lambda *_: (0,) * len(shape), pipeline_mode=pl.Buffered(1))


def _acc(shape):
    return pl.BlockSpec(shape, lambda *_: (0,) * len(shape))


def _rows(tm, width):
    return pl.BlockSpec((tm, width), lambda i: (i, 0))


def _dot(a, b):
    return jnp.dot(a.astype(BF16), b.astype(BF16), preferred_element_type=F32)


def _dot_nt(a, b):
    return lax.dot_general(a.astype(BF16), b.astype(BF16), (((1,), (1,)), ((), ())), preferred_element_type=F32)


def _dot_tn(a, b):
    return lax.dot_general(a.astype(BF16), b.astype(BF16), (((0,), (0,)), ((), ())), preferred_element_type=F32)


def _dot_exact(a, b):
    return jnp.dot(a, b, precision=lax.Precision.HIGHEST, preferred_element_type=F32)


def _ln_stats(x):
    mu = jnp.mean(x, axis=-1, keepdims=True)
    xc = x - mu
    rstd = lax.rsqrt(jnp.mean(xc * xc, axis=-1, keepdims=True) + LN_EPS)
    return xc * rstd, rstd


def _ln_bwd(dy, xhat, rstd, g):
    dxh = dy * g
    return rstd * (dxh - jnp.mean(dxh, axis=-1, keepdims=True) - xhat * jnp.mean(dxh * xhat, axis=-1, keepdims=True))


def _sigmoid(x):
    return 1.0 / (1.0 + jnp.exp(-x))


def _iota(shape, dim):
    return lax.broadcasted_iota(jnp.int32, shape, dim)


def _hbm(*arrays):
    return tuple(pltpu.with_memory_space_constraint(a, pltpu.HBM) for a in arrays)


def _ln_in_fwd(x, meta_ext, g, b):
    nb = SEQ // BLK

    def body(x_ref, m_ref, g_ref, b_ref, h_ref):
        i = pl.program_id(0)
        xin = jnp.where(i < nb, x_ref[...], m_ref[...])
        xhat, _ = _ln_stats(xin)
        h_ref[...] = xhat * g_ref[...] + b_ref[...]

    return pl.pallas_call(
        body, name="ln_in_fwd", grid=(nb + 1,),
        in_specs=[pl.BlockSpec((BLK, D), lambda i: (jnp.minimum(i, nb - 1), 0)),
                  _const((BLK, D)), _const((1, D)), _const((1, D))],
        out_specs=_rows(BLK, D),
        out_shape=pltpu.HBM((_lp(), D), F32),
        compiler_params=_params(16, dimension_semantics=_seq()),
    )(*_hbm(x, meta_ext, g, b))


def _in_proj(h0, w_in_t, b_in_p, wg2_p, bg2):
    tm = _row_tile(384)
    lp = _lp()
    widths = (512, 128, 128, 256, 256, 512, 512, 128)
    offs = (O_QS, O_KS, O_VS, O_QG, O_KG, O_VG, O_RG, O_LR)

    def body(h_ref, w_ref, b_ref, wg2_ref, bg2_ref, *outs):
        proj = _dot_nt(h_ref[...], w_ref[...]) + b_ref[...]
        for o_ref, off, wd in zip(outs[:8], offs, widths):
            o_ref[...] = proj[:, off:off + wd]
        outs[8][...] = _dot(proj[:, O_LR:O_LR + LANE], wg2_ref[...]) + bg2_ref[...]

    return pl.pallas_call(
        body, name="in_proj", grid=(lp // tm,),
        in_specs=[_rows(tm, D), _const((D_IN_P, D)), _const((1, D_IN_P)), _const((LANE, 256)), _const((1, 256))],
        out_specs=[_rows(tm, w) for w in widths] + [_rows(tm, 256)],
        out_shape=[pltpu.HBM((lp, w), F32) for w in widths] + [pltpu.HBM((lp, 256), F32)],
        compiler_params=_params(40, dimension_semantics=_seq()),
    )(*_hbm(h0, w_in_t, b_in_p, wg2_p, bg2))


def _swa_masks(n):
    nb = SEQ // BLK
    is_meta = n == nb
    ri = _iota((BLK, BLK), 0)
    cj = _iota((BLK, BLK), 1)
    meta_col = ((cj >= META_OFF) & (cj < CH)).astype(jnp.int32)
    meta_q = meta_col * ((cj <= ri) & (ri < CH)).astype(jnp.int32)
    valid_m = jnp.where(is_meta, meta_q, meta_col) > 0
    dist_m = jnp.where(is_meta, ri - cj, n * BLK + ri + CH - cj).astype(F32)
    valid_p = jnp.where((n >= 1) & (n < nb), (cj > ri).astype(jnp.int32), 0) > 0
    dist_p = (ri + BLK - cj).astype(F32)
    valid_c = jnp.where(n < nb, (cj <= ri).astype(jnp.int32), 0) > 0
    dist_c = (ri - cj).astype(F32)
    return (dist_m, dist_p, dist_c), (valid_m, valid_p, valid_c)


def _swa_bias(n):
    dists, valids = _swa_masks(n)
    return (jnp.concatenate([-d for d in dists], axis=1),
            jnp.concatenate([jnp.where(v, 0.0, NEG) for v in valids], axis=1))


def _swa_half(ref, pos, scale=1.0):
    col = ref[:, (pos // 2) * LANE:(pos // 2 + 1) * LANE]
    lane = _iota((BLK, LANE), 1)
    mine = lane < DH if pos % 2 == 0 else lane >= DH
    return jnp.where(mine, col * scale, 0.0).astype(BF16)


def _swa_merge(even, odd):
    return jnp.where(_iota((BLK, LANE), 1) < DH, even, odd)


def _swa_softmax(t, sink):
    m = jnp.maximum(jnp.max(t, axis=-1, keepdims=True), sink)
    e = jnp.exp(t - m)
    e_sink = jnp.exp(sink - m)
    inv = 1.0 / (jnp.sum(e, axis=-1, keepdims=True) + e_sink)
    return e * inv, e_sink * inv


def _swa_kv_specs(width):
    nb = SEQ // BLK
    return [pl.BlockSpec((BLK, width), lambda n: (nb, 0)),
            pl.BlockSpec((BLK, width), lambda n: (jnp.clip(n - 1, 0, nb - 1), 0)),
            pl.BlockSpec((BLK, width), lambda n: (jnp.minimum(n, nb), 0))]


def _swa_fwd(sinks, qs, ks, vs):
    nb = SEQ // BLK
    heads = range(SWA_HEADS)

    def body(sink_ref, q_ref, km_ref, kp_ref, kc_ref, vm_ref, vp_ref, vc_ref, o_ref):
        negdist, maskbias = _swa_bias(pl.program_id(0))
        k_all = jnp.concatenate([km_ref[...], kp_ref[...], kc_ref[...]], axis=0).astype(BF16)
        v_all = jnp.concatenate([vm_ref[...], vp_ref[...], vc_ref[...]], axis=0).astype(BF16)
        q = [_swa_half(q_ref, pos, DH ** -0.5) for pos in heads]
        t = [_dot_nt(q[pos], k_all) + (2.0 ** -(HEAD_POS[pos] + 1) * negdist + maskbias) for pos in heads]
        p = [_swa_softmax(t[pos], sink_ref[HEAD_POS[pos]])[0].astype(BF16) for pos in heads]
        o = [_dot(p[pos], v_all) for pos in heads]
        for col in range(SWA_HEADS // 2):
            o_ref[:, col * LANE:(col + 1) * LANE] = _swa_merge(o[2 * col], o[2 * col + 1])

    kvw = SWA_KV_HEADS * DH
    return pl.pallas_call(
        body, name="swa_fwd", grid=(nb + 1,),
        in_specs=[pl.BlockSpec(memory_space=pltpu.SMEM), _rows(BLK, SWA_HEADS * DH)] + _swa_kv_specs(kvw) + _swa_kv_specs(kvw),
        out_specs=_rows(BLK, SWA_HEADS * DH),
        out_shape=pltpu.HBM((_lp(), SWA_HEADS * DH), F32),
        compiler_params=_params(16, dimension_semantics=_seq()),
    )(sinks, *_hbm(qs, ks, ks, ks, vs, vs, vs))


def _gla_block(t):
    nc = SEQ // CH
    return jnp.where(t == 0, nc, jnp.where(t == nc + 1, nc + 1, t - 1))


def _gla_rowmask(t):
    nc = SEQ // CH
    ri = _iota((CH, 1), 0)
    m = jnp.where(t == 0, (ri >= META_OFF).astype(jnp.int32), jnp.where(t == nc + 1, 0, 1))
    return (m > 0).astype(F32) + jnp.zeros((CH, 1), F32)


def _gla_decay(z, rmask):
    log_g = (jnp.minimum(z, 0.0) - jnp.log1p(jnp.exp(-jnp.abs(z)))) * (rmask / GLA_TAU)
    tril = (_iota((CH, CH), 0) >= _iota((CH, CH), 1)).astype(F32)
    return _dot_exact(tril, log_g), jnp.sum(log_g, axis=0, keepdims=True)


def _gla_fwd(qg, kg, vg, z):
    nc = SEQ // CH
    steps = nc + 2
    kw, vw = GLA_HEADS * DK, GLA_HEADS * DV

    def body(q_ref, k_ref, v_ref, z_ref, o_ref, st_ref, st):
        t = pl.program_id(0)

        @pl.when(t == 0)
        def _():
            st[...] = jnp.zeros_like(st)

        st_prev = st[...]
        st_ref[0] = st_prev
        rmask = _gla_rowmask(t)
        b, b_last = _gla_decay(z_ref[...], rmask)
        q = q_ref[...] * (rmask * DK ** -0.5)
        k = k_ref[...] * rmask
        v = v_ref[...] * rmask
        qe = q * jnp.exp(b)
        ke = k * jnp.exp(-b)
        kd = k * jnp.exp(b_last - b)
        e_last = jnp.exp(b_last)
        causal = _iota((CH, CH), 0) >= _iota((CH, CH), 1)
        for h in range(GLA_HEADS):
            ks, vs_ = slice(h * DK, (h + 1) * DK), slice(h * DV, (h + 1) * DV)
            a = jnp.where(causal, _dot_nt(qe[:, ks], ke[:, ks]), 0.0)
            o_ref[:, vs_] = _dot(a, v[:, vs_]) + _dot_nt(qe[:, ks], st_prev[:, ks])
            st[:, ks] = st_prev[:, ks] * e_last[:, ks] + _dot_tn(v[:, vs_], kd[:, ks])

    blk = lambda w: pl.BlockSpec((CH, w), lambda t: (_gla_block(t), 0))
    return pl.pallas_call(
        body, name="gla_fwd", grid=(steps,),
        in_specs=[blk(kw), blk(kw), blk(vw), blk(kw)],
        out_specs=[blk(vw), pl.BlockSpec((1, DV, kw), lambda t: (t, 0, 0))],
        out_shape=[pltpu.HBM((_lp(), vw), F32), pltpu.HBM((steps, DV, kw), F32)],
        scratch_shapes=[pltpu.VMEM((DV, kw), F32)],
        compiler_params=_params(16, dimension_semantics=_seq()),
    )(*_hbm(qg, kg, vg, z))


def _post_mix(o_s, o_gla, r_g, h0, gn4, w_out, g1, b1):
    tm = _row_tile(384)
    lp = _lp()

    def body(os_ref, og_ref, r_ref, h0_ref, gn_ref, w_ref, g_ref, b_ref, o_ref, pre_ref, h1_ref):
        o_ref[:, 0:512] = os_ref[...].astype(BF16)
        for h in range(GLA_HEADS):
            hs = slice(h * DV, (h + 1) * DV)
            xg = og_ref[:, hs]
            n = xg * lax.rsqrt(jnp.mean(xg * xg, axis=-1, keepdims=True) + RMS_EPS) * gn_ref[...]
            r = r_ref[:, hs]
            o_ref[:, 512 + h * DV:512 + (h + 1) * DV] = (n * (r * _sigmoid(r))).astype(BF16)
        pre = ALPHA * h0_ref[...] + _dot(o_ref[...], w_ref[...])
        pre_ref[...] = pre
        xhat, _ = _ln_stats(pre)
        h1_ref[...] = xhat * g_ref[...] + b_ref[...]

    return pl.pallas_call(
        body, name="post_mix", grid=(lp // tm,),
        in_specs=[_rows(tm, 512), _rows(tm, 512), _rows(tm, 512), _rows(tm, D), _const((1, DV)), _const((D, D)),
                  _const((1, D)), _const((1, D))],
        out_specs=[_rows(tm, D), _rows(tm, D), _rows(tm, D)],
        out_shape=[pltpu.HBM((lp, D), BF16), pltpu.HBM((lp, D), F32),
                   pltpu.HBM((lp, D), F32)],
        compiler_params=_params(32, dimension_semantics=_seq()),
    )(*_hbm(o_s, o_gla, r_g, h0, gn4, w_out, g1, b1))


def _ffn_fwd(h1, wg_t, wu_t, wd):
    tm = _row_tile(192)
    lp = _lp()

    def body(h_ref, wg_ref, wu_ref, wd_ref, g_ref, u_ref, pre_ref):
        h = h_ref[...]
        g = _dot_nt(h, wg_ref[...])
        u = _dot_nt(h, wu_ref[...])
        g_ref[...] = g
        u_ref[...] = u
        pre_ref[...] = ALPHA * h + _dot(g * _sigmoid(g) * u, wd_ref[...])

    return pl.pallas_call(
        body, name="ffn_fwd", grid=(lp // tm,),
        in_specs=[_rows(tm, D), _const((D_FF, D)), _const((D_FF, D)), _const((D_FF, D))],
        out_specs=[_rows(tm, D_FF), _rows(tm, D_FF), _rows(tm, D)],
        out_shape=[pltpu.HBM((lp, D_FF), F32), pltpu.HBM((lp, D_FF), F32),
                   pltpu.HBM((lp, D), F32)],
        compiler_params=_params(48, dimension_semantics=_seq()),
    )(*_hbm(h1, wg_t, wu_t, wd))


def _ln2_loss_bwd(pre2, target, g2, b2):
    nb = SEQ // BLK

    def body(p_ref, t_ref, g_ref, b_ref, dp_ref, loss_ref, dg_ref, db_ref, acc):
        i = pl.program_id(0)

        @pl.when(i == 0)
        def _():
            acc[...] = jnp.zeros_like(acc)
            dg_ref[...] = jnp.zeros_like(dg_ref)
            db_ref[...] = jnp.zeros_like(db_ref)

        real = jnp.where(i < nb, 1.0, 0.0)
        xhat, rstd = _ln_stats(p_ref[...])
        diff = (xhat * g_ref[...] + b_ref[...] - t_ref[...]) * real
        acc[...] += jnp.sum(diff * diff, axis=0, keepdims=True)
        dy = diff * (1.0 / D)
        dp_ref[...] = _ln_bwd(dy, xhat, rstd, g_ref[...])
        dg_ref[...] += jnp.sum(dy * xhat, axis=0, keepdims=True)
        db_ref[...] += jnp.sum(dy, axis=0, keepdims=True)

        @pl.when(i == nb)
        def _():
            loss_ref[...] = jnp.zeros_like(loss_ref) + (0.5 / D) * jnp.sum(acc[...], axis=1, keepdims=True)

    return pl.pallas_call(
        body, name="ln2_loss_bwd", grid=(nb + 1,),
        in_specs=[_rows(BLK, D), pl.BlockSpec((BLK, D), lambda i: (jnp.minimum(i, nb - 1), 0)), _const((1, D)),
                  _const((1, D))],
        out_specs=[_rows(BLK, D), _acc((1, LANE)), _acc((1, D)), _acc((1, D))],
        out_shape=[pltpu.HBM((_lp(), D), F32), pltpu.HBM((1, LANE), F32),
                   pltpu.HBM((1, D), F32), pltpu.HBM((1, D), F32)],
        scratch_shapes=[pltpu.VMEM((1, D), F32)],
        compiler_params=_params(16, dimension_semantics=_seq()),
    )(*_hbm(pre2, target, g2, b2))


def _ffn_bwd(dpre2, g, u, pre1, wg_t, wu_t, wd, g1):
    tm = _row_tile(192)
    lp = _lp()

    def body(dp_ref, g_ref, u_ref, p1_ref, wg_ref, wu_ref, wd_ref, g1_ref, a_ref, dg_ref, du_ref, dp1_ref,
             dg1_ref, db1_ref):
        @pl.when(pl.program_id(0) == 0)
        def _():
            dg1_ref[...] = jnp.zeros_like(dg1_ref)
            db1_ref[...] = jnp.zeros_like(db1_ref)

        dp = dp_ref[...]
        gg, uu = g_ref[...], u_ref[...]
        sg = _sigmoid(gg)
        silu = gg * sg
        da = _dot_nt(dp, wd_ref[...])
        a_ref[...] = (silu * uu).astype(BF16)
        dgate = (da * uu * (sg * (1.0 + gg * (1.0 - sg)))).astype(BF16)
        dup = (da * silu).astype(BF16)
        dg_ref[...] = dgate
        du_ref[...] = dup
        dh1 = ALPHA * dp + _dot(dgate, wg_ref[...]) + _dot(dup, wu_ref[...])
        xhat, rstd = _ln_stats(p1_ref[...])
        dp1_ref[...] = _ln_bwd(dh1, xhat, rstd, g1_ref[...])
        dg1_ref[...] += jnp.sum(dh1 * xhat, axis=0, keepdims=True)
        db1_ref[...] += jnp.sum(dh1, axis=0, keepdims=True)

    return pl.pallas_call(
        body, name="ffn_bwd", grid=(lp // tm,),
        in_specs=[_rows(tm, D), _rows(tm, D_FF), _rows(tm, D_FF), _rows(tm, D), _const((D_FF, D)), _const((D_FF, D)),
                  _const((D_FF, D)), _const((1, D))],
        out_specs=[_rows(tm, D_FF), _rows(tm, D_FF), _rows(tm, D_FF), _rows(tm, D), _acc((1, D)), _acc((1, D))],
        out_shape=[pltpu.HBM((lp, D_FF), BF16)] * 3
        + [pltpu.HBM((lp, D), F32), pltpu.HBM((1, D), F32), pltpu.HBM((1, D), F32)],
        compiler_params=_params(52, dimension_semantics=_seq()),
    )(*_hbm(dpre2, g, u, pre1, wg_t, wu_t, wd, g1))


def _atb(a, b, name):
    lp = _lp()
    tm = _row_tile(384)
    n, w = a.shape[1], b.shape[1]
    bw = 512 if n * w * 4 > (4 << 20) else w

    def body(a_ref, b_ref, o_ref):
        @pl.when(pl.program_id(1) == 0)
        def _():
            o_ref[...] = jnp.zeros_like(o_ref)

        o_ref[...] += _dot_tn(a_ref[...], b_ref[...])

    return pl.pallas_call(
        body, name=name, grid=(w // bw, lp // tm),
        in_specs=[pl.BlockSpec((tm, n), lambda j, k: (k, 0)), pl.BlockSpec((tm, bw), lambda j, k: (k, j))],
        out_specs=pl.BlockSpec((n, bw), lambda j, k: (0, j)),
        out_shape=pltpu.HBM((n, w), F32),
        compiler_params=_params(48, dimension_semantics=_seq(2)),
    )(*_hbm(a, b))


def _out_bwd(dpre1, w_out, o_gla, r_g, gn4):
    tm = _row_tile(384)
    lp = _lp()

    def body(dp_ref, w_ref, og_ref, r_ref, gn_ref, dos_ref, dog_ref, dr_ref, dgn_ref):
        @pl.when(pl.program_id(0) == 0)
        def _():
            dgn_ref[...] = jnp.zeros_like(dgn_ref)

        do = _dot_nt(dp_ref[...], w_ref[...])
        dos_ref[...] = do[:, 0:512]
        gn = gn_ref[...]
        for h in range(GLA_HEADS):
            hs = slice(h * DV, (h + 1) * DV)
            xg = og_ref[:, hs]
            rstd = lax.rsqrt(jnp.mean(xg * xg, axis=-1, keepdims=True) + RMS_EPS)
            nx = xg * rstd
            r = r_ref[:, hs]
            sr = _sigmoid(r)
            d_o = do[:, 512 + h * DV:512 + (h + 1) * DV]
            dr_ref[:, hs] = d_o * (nx * gn) * (sr * (1.0 + r * (1.0 - sr)))
            dn = d_o * (r * sr)
            dgn_ref[...] += jnp.sum(dn * nx, axis=0, keepdims=True)
            dnx = dn * gn
            dog_ref[:, hs] = rstd * (dnx - nx * jnp.mean(dnx * nx, axis=-1, keepdims=True))

    return pl.pallas_call(
        body, name="out_bwd", grid=(lp // tm,),
        in_specs=[_rows(tm, D), _const((D, D)), _rows(tm, 512), _rows(tm, 512), _const((1, DV))],
        out_specs=[_rows(tm, 512), _rows(tm, 512), _rows(tm, 512), _acc((1, DV))],
        out_shape=[pltpu.HBM((lp, 512), F32)] * 3 + [pltpu.HBM((1, DV), F32)],
        compiler_params=_params(32, dimension_semantics=_seq()),
    )(*_hbm(dpre1, w_out, o_gla, r_g, gn4))


def _gla_bwd(qg, kg, vg, z, do_gla, st_all):
    nc = SEQ // CH
    steps = nc + 2
    kw, vw = GLA_HEADS * DK, GLA_HEADS * DV

    def body(q_ref, k_ref, v_ref, z_ref, do_ref, st_ref, dq_ref, dk_ref, dv_ref, dz_ref, dst):
        t = steps - 1 - pl.program_id(0)

        @pl.when(pl.program_id(0) == 0)
        def _():
            dst[...] = jnp.zeros_like(dst)

        rmask = _gla_rowmask(t)
        zz = z_ref[...]
        b, b_last = _gla_decay(zz, rmask)
        e_b, e_nb, e_kd, e_last = jnp.exp(b), jnp.exp(-b), jnp.exp(b_last - b), jnp.exp(b_last)
        q = q_ref[...] * (rmask * DK ** -0.5)
        k = k_ref[...] * rmask
        v = v_ref[...] * rmask
        qe, ke, kd = q * e_b, k * e_nb, k * e_kd
        st_prev = st_ref[0]
        dst_new = dst[...]
        causal = _iota((CH, CH), 0) >= _iota((CH, CH), 1)
        dqe_parts, dke_parts, dkd_parts = [], [], []
        for h in range(GLA_HEADS):
            ks, vs_ = slice(h * DK, (h + 1) * DK), slice(h * DV, (h + 1) * DV)
            d_o = do_ref[:, vs_]
            a = jnp.where(causal, _dot_nt(qe[:, ks], ke[:, ks]), 0.0)
            da = jnp.where(causal, _dot_nt(d_o, v[:, vs_]), 0.0)
            dqe_parts.append(_dot(d_o, st_prev[:, ks]) + _dot(da, ke[:, ks]))
            dke_parts.append(_dot_tn(da, qe[:, ks]))
            dkd_parts.append(_dot(v[:, vs_], dst_new[:, ks]))
            dv_ref[:, vs_] = _dot_tn(a, d_o) + _dot_nt(kd[:, ks], dst_new[:, ks])
            dst[:, ks] = dst_new[:, ks] * e_last[:, ks] + _dot_tn(d_o, qe[:, ks])
        dqe = jnp.concatenate(dqe_parts, axis=1)
        dke = jnp.concatenate(dke_parts, axis=1)
        dkd = jnp.concatenate(dkd_parts, axis=1)
        dq_ref[...] = dqe * e_b * (rmask * DK ** -0.5)
        dk_ref[...] = (dke * e_nb + dkd * e_kd) * rmask
        dkd_kd = dkd * kd
        db = dqe * qe - dke * ke - dkd_kd
        db_last = jnp.sum(dst_new * st_prev, axis=0, keepdims=True) * e_last + jnp.sum(dkd_kd, axis=0, keepdims=True)
        triu = (_iota((CH, CH), 0) <= _iota((CH, CH), 1)).astype(F32)
        dlog_g = _dot_exact(triu, db) + db_last
        dz_ref[...] = dlog_g * (rmask / GLA_TAU) * _sigmoid(-zz)

    blk = lambda w: pl.BlockSpec((CH, w), lambda s: (_gla_block(steps - 1 - s), 0))
    return pl.pallas_call(
        body, name="gla_bwd", grid=(steps,),
        in_specs=[blk(kw), blk(kw), blk(vw), blk(kw), blk(vw), pl.BlockSpec((1, DV, kw), lambda s: (steps - 1 - s, 0, 0))],
        out_specs=[blk(kw), blk(kw), blk(vw), blk(kw)],
        out_shape=[pltpu.HBM((_lp(), kw), F32), pltpu.HBM((_lp(), kw), F32),
                   pltpu.HBM((_lp(), vw), F32), pltpu.HBM((_lp(), kw), F32)],
        scratch_shapes=[pltpu.VMEM((DV, kw), F32)],
        compiler_params=_params(16, dimension_semantics=_seq()),
    )(*_hbm(qg, kg, vg, z, do_gla, st_all))


def _swa_bwd(sinks, qs, ks, vs, do_s):
    nb = SEQ // BLK
    kvw = SWA_KV_HEADS * DH
    scale = DH ** -0.5
    heads = range(SWA_HEADS)

    def body(sink_ref, q_ref, km_ref, kp_ref, kc_ref, vm_ref, vp_ref, vc_ref, do_ref,
             dq_ref, dk_ref, dv_ref, dsink_ref, carry_k, carry_v, meta_k, meta_v):
        n = pl.program_id(0)

        @pl.when(n == 0)
        def _():
            for r in (carry_k, carry_v, meta_k, meta_v):
                r[...] = jnp.zeros_like(r)
            dsink_ref[...] = jnp.zeros_like(dsink_ref)

        @pl.when(n <= nb)
        def _():
            negdist, maskbias = _swa_bias(n)
            lane = _iota((1, LANE), 1)
            k_all = jnp.concatenate([km_ref[...], kp_ref[...], kc_ref[...]], axis=0).astype(BF16)
            v_all = jnp.concatenate([vm_ref[...], vp_ref[...], vc_ref[...]], axis=0).astype(BF16)
            q = [_swa_half(q_ref, pos, scale) for pos in heads]
            d_o = [_swa_half(do_ref, pos) for pos in heads]
            t = [_dot_nt(q[pos], k_all) + (2.0 ** -(HEAD_POS[pos] + 1) * negdist + maskbias) for pos in heads]
            dp = [_dot_nt(d_o[pos], v_all) for pos in heads]
            soft = [_swa_softmax(t[pos], sink_ref[HEAD_POS[pos]]) for pos in heads]
            p = [s[0] for s in soft]
            delta = [jnp.sum(p[pos] * dp[pos], axis=-1, keepdims=True) for pos in heads]
            ds = [(p[pos] * (dp[pos] - delta[pos])).astype(BF16) for pos in heads]
            dq = [_dot(ds[pos], k_all) for pos in heads]
            for col in range(SWA_HEADS // 2):
                dq_ref[:, col * LANE:(col + 1) * LANE] = scale * _swa_merge(dq[2 * col], dq[2 * col + 1])
            dsink = jnp.zeros((1, LANE), F32)
            for pos in heads:
                dsink = dsink + jnp.where(lane == HEAD_POS[pos],
                                          -jnp.sum(soft[pos][1] * delta[pos], axis=0, keepdims=True), 0.0)
            dsink_ref[...] += dsink
            dk3 = _dot_tn(jnp.concatenate(q, axis=0), jnp.concatenate(ds, axis=0)).T
            dv3 = _dot_tn(jnp.concatenate(d_o, axis=0), jnp.concatenate([x.astype(BF16) for x in p], axis=0)).T
            meta_k[...] += dk3[0:BLK]
            meta_v[...] += dv3[0:BLK]
            dk_ref[...] = carry_k[...] + dk3[BLK:2 * BLK]
            dv_ref[...] = carry_v[...] + dv3[BLK:2 * BLK]
            carry_k[...] = dk3[2 * BLK:3 * BLK]
            carry_v[...] = dv3[2 * BLK:3 * BLK]

        @pl.when(n == nb + 1)
        def _():
            dk_ref[...] = meta_k[...]
            dv_ref[...] = meta_v[...]

    kv_out = pl.BlockSpec((BLK, kvw), lambda n: (jnp.where(n == nb + 1, nb, jnp.clip(n - 1, 0, nb - 1)), 0))
    qblk = pl.BlockSpec((BLK, SWA_HEADS * DH), lambda n: (jnp.minimum(n, nb), 0))
    return pl.pallas_call(
        body, name="swa_bwd", grid=(nb + 2,),
        in_specs=[pl.BlockSpec(memory_space=pltpu.SMEM), qblk] + _swa_kv_specs(kvw) + _swa_kv_specs(kvw) + [qblk],
        out_specs=[qblk, kv_out, kv_out, _acc((1, LANE))],
        out_shape=[pltpu.HBM((_lp(), SWA_HEADS * DH), F32), pltpu.HBM((_lp(), kvw), F32),
                   pltpu.HBM((_lp(), kvw), F32), pltpu.HBM((1, LANE), F32)],
        scratch_shapes=[pltpu.VMEM((BLK, kvw), F32)] * 4,
        compiler_params=_params(16, dimension_semantics=_seq()),
    )(sinks, *_hbm(qs, ks, ks, ks, vs, vs, vs, do_s))


def _in_bwd(dqs, dks, dvs, dqg, dkg, dvg, drg, dz, dpre1, w_in_t, wg2_p):
    tm = _row_tile(384)
    lp = _lp()
    widths = (512, 128, 128, 256, 256, 512, 512)
    offs = (O_QS, O_KS, O_VS, O_QG, O_KG, O_VG, O_RG)

    def body(*refs):
        parts, (dz_ref, dp1_ref, w_ref, wg2_ref, dproj_ref, dh0_ref, dbin_ref, dbg_ref) = refs[:7], refs[7:]

        @pl.when(pl.program_id(0) == 0)
        def _():
            dbin_ref[...] = jnp.zeros_like(dbin_ref)
            dbg_ref[...] = jnp.zeros_like(dbg_ref)

        for p_ref, off, wd in zip(parts, offs, widths):
            val = p_ref[...]
            dproj_ref[:, off:off + wd] = val.astype(BF16)
            dbin_ref[:, off:off + wd] += jnp.sum(val, axis=0, keepdims=True)
        dz = dz_ref[...]
        dlr = _dot_nt(dz, wg2_ref[...])
        dproj_ref[:, O_LR:O_LR + LANE] = dlr.astype(BF16)
        dbin_ref[:, O_LR:O_LR + LANE] += jnp.sum(dlr, axis=0, keepdims=True)
        dbg_ref[...] += jnp.sum(dz, axis=0, keepdims=True)
        dh0_ref[...] = ALPHA * dp1_ref[...] + _dot(dproj_ref[...], w_ref[...])

    return pl.pallas_call(
        body, name="in_bwd", grid=(lp // tm,),
        in_specs=[_rows(tm, w) for w in widths] + [_rows(tm, 256), _rows(tm, D), _const((D_IN_P, D)), _const((LANE, 256))],
        out_specs=[_rows(tm, D_IN_P), _rows(tm, D), _acc((1, D_IN_P)), _acc((1, 256))],
        out_shape=[pltpu.HBM((lp, D_IN_P), BF16), pltpu.HBM((lp, D), F32),
                   pltpu.HBM((1, D_IN_P), F32), pltpu.HBM((1, 256), F32)],
        compiler_params=_params(40, dimension_semantics=_seq()),
    )(*_hbm(dqs, dks, dvs, dqg, dkg, dvg, drg, dz, dpre1, w_in_t, wg2_p))


def _ln_in_bwd(x, meta_ext, dh0, g):
    nb = SEQ // BLK

    def body(x_ref, m_ref, dh_ref, g_ref, dx_ref, dm_ref, dg_ref, db_ref):
        i = pl.program_id(0)

        @pl.when(i == 0)
        def _():
            dg_ref[...] = jnp.zeros_like(dg_ref)
            db_ref[...] = jnp.zeros_like(db_ref)

        xin = jnp.where(i < nb, x_ref[...], m_ref[...])
        xhat, rstd = _ln_stats(xin)
        dh = dh_ref[...]
        dxin = _ln_bwd(dh, xhat, rstd, g_ref[...])
        dg_ref[...] += jnp.sum(dh * xhat, axis=0, keepdims=True)
        db_ref[...] += jnp.sum(dh, axis=0, keepdims=True)

        @pl.when(i < nb)
        def _():
            dx_ref[...] = dxin

        @pl.when(i == nb)
        def _():
            dm_ref[...] = dxin

    xblk = pl.BlockSpec((BLK, D), lambda i: (jnp.minimum(i, nb - 1), 0))
    return pl.pallas_call(
        body, name="ln_in_bwd", grid=(nb + 1,),
        in_specs=[xblk, _const((BLK, D)), _rows(BLK, D), _const((1, D))],
        out_specs=[xblk, _acc((BLK, D)), _acc((1, D)), _acc((1, D))],
        out_shape=[pltpu.HBM((SEQ, D), F32), pltpu.HBM((BLK, D), F32),
                   pltpu.HBM((1, D), F32), pltpu.HBM((1, D), F32)],
        compiler_params=_params(16, dimension_semantics=_seq()),
    )(*_hbm(x, meta_ext, dh0, g))


def _local_step(x, target, meta_full, ln_in_g, ln_in_b, w_in_t, b_in, wg2, bg2, sinks, gn, w_out, g1, b1,
                wg_t, wu_t, wd, g2, b2):
    row = lambda v: v.reshape(1, -1).astype(F32)
    meta_ext = jnp.pad(meta_full, ((META_OFF, BLK - CH), (0, 0)))
    w_in_t, w_out = _head_order(w_in_t, HEAD_POS, 0), _head_order(w_out, HEAD_POS, 0)
    b_in_p = jnp.pad(_head_order(row(b_in), HEAD_POS, 1), ((0, 0), (0, D_IN_P - D_IN)))
    wg2_p = jnp.pad(wg2, ((0, LANE - wg2.shape[0]), (0, 0))).astype(BF16)
    gn4 = row(gn)
    sinks = sinks.reshape(-1).astype(F32)

    h0 = _ln_in_fwd(x, meta_ext, row(ln_in_g), row(ln_in_b))
    qs, ks, vs, qg, kg, vg, rg, glr, z = _in_proj(h0, w_in_t, b_in_p, wg2_p, row(bg2))
    o_s = _swa_fwd(sinks, qs, ks, vs)
    o_gla, st_all = _gla_fwd(qg, kg, vg, z)
    o, pre1, h1 = _post_mix(o_s, o_gla, rg, h0, gn4, w_out, row(g1), row(b1))
    g, u, pre2 = _ffn_fwd(h1, wg_t, wu_t, wd)

    dpre2, loss, dg2, db2 = _ln2_loss_bwd(pre2, target, row(g2), row(b2))
    a, dgate, dup, dpre1, dg1, db1 = _ffn_bwd(dpre2, g, u, pre1, wg_t, wu_t, wd, row(g1))
    dwd = _atb(a, dpre2, "dw_down")
    dwg_t = _atb(dgate, h1, "dw_gate")
    dwu_t = _atb(dup, h1, "dw_up")
    dw_out = _atb(o, dpre1, "dw_out")
    do_s, do_gla, drg, dgn = _out_bwd(dpre1, w_out, o_gla, rg, gn4)
    dqg, dkg, dvg, dz = _gla_bwd(qg, kg, vg, z, do_gla, st_all)
    dqs, dks, dvs, dsinks = _swa_bwd(sinks, qs, ks, vs, do_s)
    dproj, dh0, db_in_p, dbg2 = _in_bwd(dqs, dks, dvs, dqg, dkg, dvg, drg, dz, dpre1, w_in_t, wg2_p)
    dw_in_t = _atb(dproj, h0, "dw_in")
    dwg2_p = _atb(glr, dz, "dw_gate_lr2")
    dx, dmeta_blk, dg_in, db_in_ln = _ln_in_bwd(x, meta_ext, dh0, row(ln_in_g))

    grads = dict(
        w_in=_head_order(dw_in_t, HEAD_INV, 0), w_out=_head_order(dw_out, HEAD_INV, 0), w_g=dwg_t, w_u=dwu_t, w_d=dwd,
        meta=dmeta_blk[META_OFF:CH], ln_in_g=dg_in, ln_in_b=db_in_ln, ln1_g=dg1, ln1_b=db1, ln2_g=dg2, ln2_b=db2,
        b_in=_head_order(db_in_p[:, :D_IN], HEAD_INV, 1), wg2=dwg2_p[:wg2.shape[0]], bg2=dbg2,
        sinks=dsinks[:, :SWA_HEADS], gn=dgn)
    return loss[0, 0], dx, grads


def _head_order(a, order, axis):
    take = lambda i: lax.slice_in_dim(a, i * DH, (i + 1) * DH, axis=axis)
    rest = lax.slice_in_dim(a, SWA_HEADS * DH, a.shape[axis], axis=axis)
    return jnp.concatenate([take(i) for i in order] + [rest], axis=axis)


HBM = pl.BlockSpec(memory_space=pltpu.HBM)


def _place():
    return lax.axis_index("x"), lax.axis_index("y"), lax.axis_index("c")


def _other_chips(x, y):
    return [(1 - x, y), (x, 1 - y), (1 - x, 1 - y)]


def _dma_sems(n):
    return pltpu.SemaphoreType.DMA((n,))


def _comm_params():
    return pltpu.CompilerParams(has_side_effects=True)


def _gather_halves(shards):
    n = len(shards)

    def body(*refs):
        ins, outs = refs[:n], refs[n:2 * n]
        ici_send, ici_recv, d2d_send, d2d_recv = refs[2 * n:]
        x, y, c = _place()
        mine = 2 * x + y
        chips = _other_chips(x, y)

        def ici(a, j, src_chip):
            px, py = chips[j]
            return pltpu.make_async_remote_copy(ins[a].at[c], outs[a].at[src_chip, c], ici_send.at[3 * a + j],
                                                ici_recv.at[3 * a + j], device_id=(px, py, c), device_id_type=MESH)

        def d2d(a, j, half):
            px, py = chips[j]
            blk = outs[a].at[2 * px + py, half]
            return pltpu.make_async_remote_copy(blk, blk, d2d_send.at[3 * a + j], d2d_recv.at[3 * a + j],
                                                device_id=(x, y, 1 - c), device_id_type=MESH)

        sends = [ici(a, j, mine) for a in range(n) for j in range(3)]
        for cp in sends:
            cp.start()
        passed = []
        for a in range(n):
            for j, (px, py) in enumerate(chips):
                ici(a, j, 2 * px + py).wait_recv()
                fwd = d2d(a, j, c)
                fwd.start()
                passed.append(fwd)
        for a in range(n):
            for j in range(3):
                d2d(a, j, 1 - c).wait_recv()
        for cp in sends + passed:
            cp.wait_send()

    gathered = pl.pallas_call(
        body, name="gather_halves",
        in_specs=[HBM] * n, out_specs=[HBM] * n,
        out_shape=[pltpu.HBM((N_CHIPS,) + s.shape, s.dtype) for s in shards],
        scratch_shapes=[_dma_sems(3 * n)] * 4,
        compiler_params=_comm_params(),
    )(*_hbm(*shards))
    mine = 2 * lax.axis_index("x") + lax.axis_index("y")
    return [lax.dynamic_update_index_in_dim(g, s, mine, axis=0) for g, s in zip(gathered, shards)]


def _sibling_exchange(grads):
    n = len(grads)

    def body(*refs):
        ins, outs = refs[:n], refs[n:2 * n]
        send_sems, recv_sems = refs[2 * n:]
        x, y, c = _place()
        copies = []
        for a in range(n):
            for s in range(N_CHIPS):
                cp = pltpu.make_async_remote_copy(ins[a].at[s, 1 - c], outs[a].at[s], send_sems.at[N_CHIPS * a + s],
                                                  recv_sems.at[N_CHIPS * a + s], device_id=(x, y, 1 - c),
                                                  device_id_type=MESH)
                cp.start()
                copies.append(cp)
        for cp in copies:
            cp.wait_recv()
        for cp in copies:
            cp.wait_send()

    return pl.pallas_call(
        body, name="sibling_exchange", in_specs=[HBM] * n, out_specs=[HBM] * n,
        out_shape=[pltpu.HBM((N_CHIPS, g.shape[2], D), F32) for g in grads],
        scratch_shapes=[_dma_sems(N_CHIPS * n)] * 2,
        compiler_params=_comm_params(),
    )(*_hbm(*grads))


def _add_halves(core, grad, recv, dtype, name):
    h = grad.shape[2]

    def body(c_ref, a_ref, b_ref, o_ref):
        o_ref[...] = (a_ref[0] + b_ref[...]).astype(dtype)

    return pl.pallas_call(
        body, name=name,
        grid_spec=pltpu.PrefetchScalarGridSpec(
            num_scalar_prefetch=1, grid=(N_CHIPS,),
            in_specs=[pl.BlockSpec((1, 1, h, D), lambda s, c: (s, c[0], 0, 0)),
                      pl.BlockSpec((1, h, D), lambda s, c: (s, 0, 0))],
            out_specs=pl.BlockSpec((1, h, D), lambda s, c: (s, 0, 0))),
        out_shape=pltpu.HBM((N_CHIPS, h, D), dtype),
        compiler_params=_params(16, dimension_semantics=_seq()),
    )(core, *_hbm(grad, recv))


def _chip_scatter(parts, with_own):
    n = len(parts)

    def body(*refs):
        ins, outs = refs[:n], refs[n:2 * n]
        send_sems, recv_sems, local_sems = refs[2 * n:]
        x, y, c = _place()
        mine = 2 * x + y
        chips = _other_chips(x, y)
        local = [pltpu.make_async_copy(ins[a].at[mine], outs[a].at[mine], local_sems.at[a]) for a in range(n)
                 if with_own[a]]
        for cp in local:
            cp.start()
        sends = []
        for a in range(n):
            for j, (px, py) in enumerate(chips):
                cp = pltpu.make_async_remote_copy(ins[a].at[2 * px + py], outs[a].at[mine], send_sems.at[3 * a + j],
                                                  recv_sems.at[3 * a + j], device_id=(px, py, c), device_id_type=MESH)
                cp.start()
                sends.append(cp)
        for a in range(n):
            for j, (px, py) in enumerate(chips):
                pltpu.make_async_remote_copy(ins[a].at[mine], outs[a].at[2 * px + py], send_sems.at[3 * a + j],
                                             recv_sems.at[3 * a + j], device_id=(px, py, c),
                                             device_id_type=MESH).wait_recv()
        for cp in sends:
            cp.wait_send()
        for cp in local:
            cp.wait()

    return pl.pallas_call(
        body, name="chip_scatter", in_specs=[HBM] * n, out_specs=[HBM] * n,
        out_shape=[pltpu.HBM(p.shape, p.dtype) for p in parts],
        scratch_shapes=[_dma_sems(3 * n)] * 2 + [_dma_sems(n)],
        compiler_params=_comm_params(),
    )(*_hbm(*parts))


def _sum_chips(slots, first, rest, name):
    h = first.shape[1]

    def body(i_ref, a_ref, b_ref, c_ref, d_ref, o_ref):
        o_ref[...] = ((a_ref[...].astype(F32) + b_ref[...].astype(F32)) + c_ref[...].astype(F32)) + d_ref[...].astype(F32)

    slab = lambda k: pl.BlockSpec((1, h, D), lambda i, ix: (ix[k], 0, 0))
    return pl.pallas_call(
        body, name=name,
        grid_spec=pltpu.PrefetchScalarGridSpec(num_scalar_prefetch=1, grid=(1,),
                                               in_specs=[slab(0), slab(1), slab(2), slab(3)], out_specs=slab(4)),
        out_shape=pltpu.HBM((2, h, D), F32),
        compiler_params=_params(16, dimension_semantics=_seq()),
    )(slots, *_hbm(first, rest, rest, rest))


def _join_halves(halves):
    n = len(halves)

    def body(*refs):
        outs = refs[n:2 * n]
        send_sems, recv_sems = refs[2 * n:]
        x, y, c = _place()

        def copy(a, slab):
            return pltpu.make_async_remote_copy(outs[a].at[slab], outs[a].at[slab], send_sems.at[a], recv_sems.at[a],
                                                device_id=(x, y, 1 - c), device_id_type=MESH)

        for a in range(n):
            copy(a, c).start()
        for a in range(n):
            copy(a, 1 - c).wait_recv()
        for a in range(n):
            copy(a, c).wait_send()

    return pl.pallas_call(
        body, name="join_halves", in_specs=[HBM] * n, out_specs=[HBM] * n,
        out_shape=[pltpu.HBM(h.shape, F32) for h in halves],
        input_output_aliases={a: a for a in range(n)},
        scratch_shapes=[_dma_sems(n)] * 2,
        compiler_params=_comm_params(),
    )(*_hbm(*halves))


def _reduce_scatter(grads, wire_dtypes, same_order, names):
    x, y, c = _place()
    core = c.astype(jnp.int32).reshape(1)
    others = [2 * px + py for px, py in _other_chips(x, y)]
    own_first = jnp.stack([2 * x + y] + others + [c]).astype(jnp.int32)
    chip_order = jnp.stack([0 * c, 0 * c + 1, 0 * c + 2, 0 * c + 3, c]).astype(jnp.int32)
    recv = _sibling_exchange(grads)
    parts = [_add_halves(core, g, r, dt, "add_halves_" + nm) for g, r, dt, nm in zip(grads, recv, wire_dtypes, names)]
    got = _chip_scatter(parts, same_order)
    halves = [_sum_chips(chip_order, q, q, "sum_chips_" + nm) if fixed else _sum_chips(own_first, p, q, "sum_chips_" + nm)
              for p, q, fixed, nm in zip(parts, got, same_order, names)]
    return [f.reshape(2 * f.shape[1], D) for f in _join_halves(halves)]


def _adamw(w, g, m, v, name):
    rows, cols = w.shape
    tr = max(t for t in range(8, 257, 8) if rows % t == 0) if rows % 8 == 0 else rows

    def body(w_ref, g_ref, m_ref, v_ref, d_ref, nm_ref, nv_ref):
        gg = g_ref[...]
        nm = ADAM_B1 * m_ref[...] + (1.0 - ADAM_B1) * gg
        nv = ADAM_B2 * v_ref[...] + (1.0 - ADAM_B2) * (gg * gg)
        m_hat = nm / (1.0 - ADAM_B1 ** ADAM_STEP)
        v_hat = nv / (1.0 - ADAM_B2 ** ADAM_STEP)
        d_ref[...] = -ADAM_LR * (m_hat / (jnp.sqrt(v_hat) + ADAM_EPS) + ADAM_WD * w_ref[...])
        nm_ref[...] = nm
        nv_ref[...] = nv

    blk = pl.BlockSpec((tr, cols), lambda i: (i, 0))
    return pl.pallas_call(
        body, name=name, grid=(rows // tr,),
        in_specs=[blk] * 4, out_specs=[blk] * 3,
        out_shape=[pltpu.HBM(w.shape, F32)] * 3,
        compiler_params=_params(32, dimension_semantics=_seq()),
    )(*_hbm(w, g, m, v))


def _flat_rows(v, rows):
    flat = v.reshape(-1).astype(F32)
    return jnp.pad(flat, (0, rows * D - flat.shape[0])).reshape(rows, D)


def _small_pack(gr):
    tail = jnp.concatenate([gr["bg2"].reshape(-1), gr["sinks"].reshape(-1), gr["gn"].reshape(-1)])
    rows = [gr["meta"].reshape(N_META, D)] + [gr[k].reshape(1, D) for k in
                                              ("ln_in_g", "ln_in_b", "ln1_g", "ln1_b", "ln2_g", "ln2_b")]
    rows += [_flat_rows(gr["b_in"], 3), _flat_rows(gr["wg2"], 4), _flat_rows(tail, 1)]
    packed = jnp.concatenate(rows, axis=0)
    return jnp.pad(packed, ((0, SMALL_ROWS - packed.shape[0]), (0, 0)))


def _small_unpack(p):
    flat = lambda r0, n, size: p[r0:r0 + n].reshape(-1)[:size]
    tail = p[29]
    return dict(meta=p[0:N_META], ln_in_g=p[16], ln_in_b=p[17], ln1_g=p[18], ln1_b=p[19], ln2_g=p[20], ln2_b=p[21],
                b_in=flat(22, 3, D_IN), wg2=flat(25, 4, 16 * 256).reshape(16, 256), bg2=tail[0:256],
                sinks=tail[256:256 + SWA_HEADS], gn=tail[256 + SWA_HEADS:256 + SWA_HEADS + DV])


BIG = ("w_in", "w_out", "w_g", "w_u", "w_d")


def kernel(x, meta_tokens, ln_in_g, ln_in_b, w_in, b_in, w_gate_lr2, b_gate_lr2, attn_sinks, gla_norm_g, w_out, ln1_g, ln1_b, w_ffn_gate, w_ffn_up, w_ffn_down, ln2_g, ln2_b, loss_target, m_meta_tokens, m_ln_in_g, m_ln_in_b, m_w_in, m_b_in, m_w_gate_lr2, m_b_gate_lr2, m_attn_sinks, m_gla_norm_g, m_w_out, m_ln1_g, m_ln1_b, m_w_ffn_gate, m_w_ffn_up, m_w_ffn_down, m_ln2_g, m_ln2_b, v_meta_tokens, v_ln_in_g, v_ln_in_b, v_w_in, v_b_in, v_w_gate_lr2, v_b_gate_lr2, v_attn_sinks, v_gla_norm_g, v_w_out, v_ln1_g, v_ln1_b, v_w_ffn_gate, v_w_ffn_up, v_w_ffn_down, v_ln2_g, v_ln2_b):
    chip = 2 * lax.axis_index("x") + lax.axis_index("y")

    halves = lambda a: a.reshape(2, a.shape[0] // 2, a.shape[1])
    r_in = SHARD_ROWS["w_in"]
    shards = [jnp.pad(w_in[0].T.astype(BF16), ((0, W_IN_WIN - r_in), (0, 0))), w_out[0].astype(BF16),
              w_ffn_gate[0].T.astype(BF16), w_ffn_up[0].T.astype(BF16), w_ffn_down[0].astype(BF16), meta_tokens,
              w_gate_lr2[0]]
    g_in, g_out, g_g, g_u, g_d, g_meta, g_wg2 = _gather_halves([halves(a) for a in shards])
    w_in_t = jnp.pad(g_in.reshape(N_CHIPS, W_IN_WIN, D)[:, :r_in].reshape(D_IN, D), ((0, D_IN_P - D_IN), (0, 0)))
    meta_full = jnp.concatenate([g_meta[s].reshape(N_META, -1) for s in range(N_CHIPS)], axis=1)
    wg2_full = jnp.concatenate([g_wg2[s].reshape(w_gate_lr2.shape[1], -1) for s in range(N_CHIPS)], axis=1)

    loss_part, dx, gr = _local_step(
        x[0], loss_target[0], meta_full, ln_in_g, ln_in_b, w_in_t, b_in[0], wg2_full, b_gate_lr2[0], attn_sinks[0],
        gla_norm_g[0], g_out.reshape(D, D), ln1_g[0], ln1_b[0], g_g.reshape(D_FF, D), g_u.reshape(D_FF, D),
        g_d.reshape(D_FF, D), ln2_g[0], ln2_b[0])
    loss = lax.psum(loss_part, ("x", "y", "c"))

    win_start = [s * r_in // BF16_ROWS * BF16_ROWS for s in range(N_CHIPS)]
    to_send = [jnp.stack([gr["w_in"][st:st + W_IN_WIN] for st in win_start])] + [gr[k] for k in BIG[1:]]
    to_send.append(jnp.broadcast_to(_small_pack(gr), (N_CHIPS, SMALL_ROWS, D)))
    to_send = [a.reshape(N_CHIPS, 2, -1, D) for a in to_send]
    red = _reduce_scatter(to_send, [BF16] * len(BIG) + [F32], [False] * len(BIG) + [True], list(BIG) + ["small"])

    big_g = dict(zip(BIG, red))
    big_g["w_in"] = lax.dynamic_slice_in_dim(red[0], chip * (r_in % BF16_ROWS), r_in, axis=0)
    sg = _small_unpack(red[-1])
    col = lambda a, width: lax.dynamic_slice_in_dim(a, chip * width, width, axis=1)
    grads = dict(
        meta_tokens=col(sg["meta"], D // N_CHIPS), ln_in_g=sg["ln_in_g"], ln_in_b=sg["ln_in_b"],
        w_in=big_g["w_in"].T[None], b_in=sg["b_in"][None], w_gate_lr2=col(sg["wg2"], 256 // N_CHIPS)[None],
        b_gate_lr2=sg["bg2"][None], attn_sinks=sg["sinks"][None], gla_norm_g=sg["gn"][None],
        w_out=big_g["w_out"][None], ln1_g=sg["ln1_g"][None], ln1_b=sg["ln1_b"][None],
        w_ffn_gate=big_g["w_g"].T[None], w_ffn_up=big_g["w_u"].T[None], w_ffn_down=big_g["w_d"][None],
        ln2_g=sg["ln2_g"][None], ln2_b=sg["ln2_b"][None])
    weights = dict(meta_tokens=meta_tokens, ln_in_g=ln_in_g, ln_in_b=ln_in_b, w_in=w_in, b_in=b_in,
                   w_gate_lr2=w_gate_lr2, b_gate_lr2=b_gate_lr2, attn_sinks=attn_sinks, gla_norm_g=gla_norm_g,
                   w_out=w_out, ln1_g=ln1_g, ln1_b=ln1_b, w_ffn_gate=w_ffn_gate, w_ffn_up=w_ffn_up,
                   w_ffn_down=w_ffn_down, ln2_g=ln2_g, ln2_b=ln2_b)
    m_in = dict(meta_tokens=m_meta_tokens, ln_in_g=m_ln_in_g, ln_in_b=m_ln_in_b, w_in=m_w_in, b_in=m_b_in,
                w_gate_lr2=m_w_gate_lr2, b_gate_lr2=m_b_gate_lr2, attn_sinks=m_attn_sinks, gla_norm_g=m_gla_norm_g,
                w_out=m_w_out, ln1_g=m_ln1_g, ln1_b=m_ln1_b, w_ffn_gate=m_w_ffn_gate, w_ffn_up=m_w_ffn_up,
                w_ffn_down=m_w_ffn_down, ln2_g=m_ln2_g, ln2_b=m_ln2_b)
    v_in = dict(meta_tokens=v_meta_tokens, ln_in_g=v_ln_in_g, ln_in_b=v_ln_in_b, w_in=v_w_in, b_in=v_b_in,
                w_gate_lr2=v_w_gate_lr2, b_gate_lr2=v_b_gate_lr2, attn_sinks=v_attn_sinks, gla_norm_g=v_gla_norm_g,
                w_out=v_w_out, ln1_g=v_ln1_g, ln1_b=v_ln1_b, w_ffn_gate=v_w_ffn_gate, w_ffn_up=v_w_ffn_up,
                w_ffn_down=v_w_ffn_down, ln2_g=v_ln2_g, ln2_b=v_ln2_b)
    names = list(weights)
    big_names = ("w_in", "w_out", "w_ffn_gate", "w_ffn_up", "w_ffn_down")

    delta, new_m, new_v = {}, {}, {}
    for k in big_names:
        two_d = lambda a: a.reshape(a.shape[-2], a.shape[-1])
        d_, m_, v_ = _adamw(two_d(weights[k]), two_d(grads[k]), two_d(m_in[k]), two_d(v_in[k]), "adamw_" + k)
        delta[k], new_m[k], new_v[k] = (t.reshape(weights[k].shape) for t in (d_, m_, v_))
    small_names = [k for k in names if k not in big_names]
    sizes = [weights[k].size for k in small_names]
    rows_small = -(-sum(sizes) // D)
    rows_small += -rows_small % 8
    cat = lambda src: _flat_rows(jnp.concatenate([src[k].reshape(-1) for k in small_names]), rows_small)
    d_, m_, v_ = _adamw(cat(weights), cat(grads), cat(m_in), cat(v_in), "adamw_small")
    off = 0
    for k, n in zip(small_names, sizes):
        for dst, src in ((delta, d_), (new_m, m_), (new_v, v_)):
            dst[k] = src.reshape(-1)[off:off + n].reshape(weights[k].shape)
        off += n
    grads = {k: grads[k].reshape(weights[k].shape) for k in names}

    return (loss, dx[None], *[grads[k] for k in names], *[delta[k] for k in names], *[new_m[k] for k in names],
            *[new_v[k] for k in names])
```

```python
import functools

import jax
import jax.numpy as jnp
from jax import lax
from jax.experimental import pallas as pl
from jax.experimental.pallas import tpu as pltpu

F32 = jnp.float32
BF16 = jnp.bfloat16
MESH = pl.DeviceIdType.MESH

D = 1024
SEQ = 4096
N_META = 16
SWA_HEADS, SWA_KV_HEADS, DH = 8, 2, 64
WINDOW = 128
GLA_HEADS, DK, DV = 4, 64, 128
GLA_TAU = 16.0
CH = 64
D_FF = 2816
D_IN = 2320
LN_EPS = 1e-5
RMS_EPS = 1e-6
ALPHA = 2.0 ** 0.25
NEG = -1e30
ADAM_LR, ADAM_B1, ADAM_B2, ADAM_EPS, ADAM_WD, ADAM_STEP = 0.001, 0.9, 0.999, 1e-8, 0.01, 10
O_QS, O_KS, O_VS, O_QG, O_KG, O_VG, O_RG, O_LR = 0, 512, 640, 768, 1024, 1280, 1792, 2304

LANE = 128
BLK = WINDOW
D_IN_P = D_IN + LANE - 16
META_OFF = CH - N_META
HEAD_POS = (0, 4, 1, 5, 2, 6, 3, 7)
N_CHIPS = 4
SHARD_ROWS = dict(w_in=D_IN // N_CHIPS, w_out=D // N_CHIPS, w_g=D_FF // N_CHIPS, w_u=D_FF // N_CHIPS,
                  w_d=D_FF // N_CHIPS)
SMALL_ROWS = 32
BF16_ROWS = 16
W_IN_WIN = -(-SHARD_ROWS["w_in"] // (2 * BF16_ROWS)) * 2 * BF16_ROWS
VMEM_CAP_MB = 64


def _lp():
    return SEQ + BLK


def _row_tile(cap):
    lp = _lp()
    return max(t for t in range(16, cap + 1, 16) if lp % t == 0)


def _params(vmem_mb, **kw):
    assert vmem_mb <= VMEM_CAP_MB - 6
    return pltpu.CompilerParams(vmem_limit_bytes=vmem_mb << 20, **kw)


def _seq(n=1):
    return ("arbitrary",) * n


def _const(shape):
    return pl.BlockSpec(shape, lambda *_: (0,) * len(shape), pipeline_mode=pl.Buffered(1))


def _acc(shape):
    return pl.BlockSpec(shape, lambda *_: (0,) * len(shape))


def _rows(tm, width):
    return pl.BlockSpec((tm, width), lambda i: (i, 0))


def _dot(a, b):
    return jnp.dot(a.astype(BF16), b.astype(BF16), preferred_element_type=F32)


def _dot_nt(a, b):
    return lax.dot_general(a.astype(BF16), b.astype(BF16), (((1,), (1,)), ((), ())), preferred_element_type=F32)


def _dot_tn(a, b):
    return lax.dot_general(a.astype(BF16), b.astype(BF16), (((0,), (0,)), ((), ())), preferred_element_type=F32)


def _dot_exact(a, b):
    return jnp.dot(a, b, precision=lax.Precision.HIGHEST, preferred_element_type=F32)


def _ln_stats(x):
    mu = jnp.mean(x, axis=-1, keepdims=True)
    xc = x - mu
    rstd = lax.rsqrt(jnp.mean(xc * xc, axis=-1, keepdims=True) + LN_EPS)
    return xc * rstd, rstd


def _ln_bwd(dy, xhat, rstd, g):
    dxh = dy * g
    return rstd * (dxh - jnp.mean(dxh, axis=-1, keepdims=True) - xhat * jnp.mean(dxh * xhat, axis=-1, keepdims=True))


def _sigmoid(x):
    return 1.0 / (1.0 + jnp.exp(-x))


def _iota(shape, dim):
    return lax.broadcasted_iota(jnp.int32, shape, dim)


def _hbm(*arrays):
    return tuple(pltpu.with_memory_space_constraint(a, pltpu.HBM) for a in arrays)


def _ln_in_fwd(x, meta_ext, g, b):
    nb = SEQ // BLK

    def body(x_ref, m_ref, g_ref, b_ref, h_ref):
        i = pl.program_id(0)
        xin = jnp.where(i < nb, x_ref[...], m_ref[...])
        xhat, _ = _ln_stats(xin)
        h_ref[...] = xhat * g_ref[...] + b_ref[...]

    return pl.pallas_call(
        body, name="ln_in_fwd", grid=(nb + 1,),
        in_specs=[pl.BlockSpec((BLK, D), lambda i: (jnp.minimum(i, nb - 1), 0)),
                  _const((BLK, D)), _const((1, D)), _const((1, D))],
        out_specs=_rows(BLK, D),
        out_shape=pltpu.HBM((_lp(), D), F32),
        compiler_params=_params(16, dimension_semantics=_seq()),
    )(*_hbm(x, meta_ext, g, b))


def _in_proj(h0, w_in_t, b_in_p, wg2_p, bg2):
    tm = _row_tile(384)
    lp = _lp()
    widths = (512, 128, 128, 256, 256, 512, 512, 128)
    offs = (O_QS, O_KS, O_VS, O_QG, O_KG, O_VG, O_RG, O_LR)

    def body(h_ref, w_ref, b_ref, wg2_ref, bg2_ref, *outs):
        proj = _dot_nt(h_ref[...], w_ref[...]) + b_ref[...]
        for pos, h in enumerate(HEAD_POS):
            outs[0][:, pos * DH:(pos + 1) * DH] = proj[:, O_QS + h * DH:O_QS + (h + 1) * DH]
        for o_ref, off, wd in zip(outs[1:8], offs[1:], widths[1:]):
            o_ref[...] = proj[:, off:off + wd]
        outs[8][...] = _dot(proj[:, O_LR:O_LR + LANE], wg2_ref[...]) + bg2_ref[...]

    return pl.pallas_call(
        body, name="in_proj", grid=(lp // tm,),
        in_specs=[_rows(tm, D), _const((D_IN_P, D)), _const((1, D_IN_P)), _const((LANE, 256)), _const((1, 256))],
        out_specs=[_rows(tm, w) for w in widths] + [_rows(tm, 256)],
        out_shape=[pltpu.HBM((lp, w), F32) for w in widths] + [pltpu.HBM((lp, 256), F32)],
        compiler_params=_params(40, dimension_semantics=_seq()),
    )(*_hbm(h0, w_in_t, b_in_p, wg2_p, bg2))


def _swa_masks(n):
    nb = SEQ // BLK
    is_meta = n == nb
    ri = _iota((BLK, BLK), 0)
    cj = _iota((BLK, BLK), 1)
    meta_col = ((cj >= META_OFF) & (cj < CH)).astype(jnp.int32)
    meta_q = meta_col * ((cj <= ri) & (ri < CH)).astype(jnp.int32)
    valid_m = jnp.where(is_meta, meta_q, meta_col) > 0
    dist_m = jnp.where(is_meta, ri - cj, n * BLK + ri + CH - cj).astype(F32)
    valid_p = jnp.where((n >= 1) & (n < nb), (cj > ri).astype(jnp.int32), 0) > 0
    dist_p = (ri + BLK - cj).astype(F32)
    valid_c = jnp.where(n < nb, (cj <= ri).astype(jnp.int32), 0) > 0
    dist_c = (ri - cj).astype(F32)
    return (dist_m, dist_p, dist_c), (valid_m, valid_p, valid_c)


def _swa_bias(n):
    dists, valids = _swa_masks(n)
    return (jnp.concatenate([-d for d in dists], axis=1),
            jnp.concatenate([jnp.where(v, 0.0, NEG) for v in valids], axis=1))


def _swa_half(ref, pos, scale=1.0):
    col = ref[:, (pos // 2) * LANE:(pos // 2 + 1) * LANE]
    lane = _iota((BLK, LANE), 1)
    mine = lane < DH if pos % 2 == 0 else lane >= DH
    return jnp.where(mine, col * scale, 0.0).astype(BF16)


def _swa_merge(even, odd):
    return jnp.where(_iota((BLK, LANE), 1) < DH, even, odd)


def _swa_softmax(t, sink):
    m = jnp.maximum(jnp.max(t, axis=-1, keepdims=True), sink)
    e = jnp.exp(t - m)
    e_sink = jnp.exp(sink - m)
    inv = 1.0 / (jnp.sum(e, axis=-1, keepdims=True) + e_sink)
    return e * inv, e_sink * inv


def _swa_kv_specs(width):
    nb = SEQ // BLK
    return [pl.BlockSpec((BLK, width), lambda n: (nb, 0)),
            pl.BlockSpec((BLK, width), lambda n: (jnp.clip(n - 1, 0, nb - 1), 0)),
            pl.BlockSpec((BLK, width), lambda n: (jnp.minimum(n, nb), 0))]


def _swa_fwd(sinks, qs, ks, vs):
    nb = SEQ // BLK
    heads = range(SWA_HEADS)

    def body(sink_ref, q_ref, km_ref, kp_ref, kc_ref, vm_ref, vp_ref, vc_ref, o_ref):
        negdist, maskbias = _swa_bias(pl.program_id(0))
        k_all = jnp.concatenate([km_ref[...], kp_ref[...], kc_ref[...]], axis=0).astype(BF16)
        v_all = jnp.concatenate([vm_ref[...], vp_ref[...], vc_ref[...]], axis=0).astype(BF16)
        q = [_swa_half(q_ref, pos, DH ** -0.5) for pos in heads]
        t = [_dot_nt(q[pos], k_all) + (2.0 ** -(HEAD_POS[pos] + 1) * negdist + maskbias) for pos in heads]
        p = [_swa_softmax(t[pos], sink_ref[HEAD_POS[pos]])[0].astype(BF16) for pos in heads]
        o = [_dot(p[pos], v_all) for pos in heads]
        for col in range(SWA_HEADS // 2):
            o_ref[:, col * LANE:(col + 1) * LANE] = _swa_merge(o[2 * col], o[2 * col + 1])

    kvw = SWA_KV_HEADS * DH
    return pl.pallas_call(
        body, name="swa_fwd", grid=(nb + 1,),
        in_specs=[pl.BlockSpec(memory_space=pltpu.SMEM), _rows(BLK, SWA_HEADS * DH)] + _swa_kv_specs(kvw) + _swa_kv_specs(kvw),
        out_specs=_rows(BLK, SWA_HEADS * DH),
        out_shape=pltpu.HBM((_lp(), SWA_HEADS * DH), F32),
        compiler_params=_params(16, dimension_semantics=_seq()),
    )(sinks, *_hbm(qs, ks, ks, ks, vs, vs, vs))


def _gla_block(t):
    nc = SEQ // CH
    return jnp.where(t == 0, nc, jnp.where(t == nc + 1, nc + 1, t - 1))


def _gla_rowmask(t):
    nc = SEQ // CH
    ri = _iota((CH, 1), 0)
    m = jnp.where(t == 0, (ri >= META_OFF).astype(jnp.int32), jnp.where(t == nc + 1, 0, 1))
    return (m > 0).astype(F32) + jnp.zeros((CH, 1), F32)


def _gla_decay(z, rmask):
    log_g = (jnp.minimum(z, 0.0) - jnp.log1p(jnp.exp(-jnp.abs(z)))) * (rmask / GLA_TAU)
    tril = (_iota((CH, CH), 0) >= _iota((CH, CH), 1)).astype(F32)
    return _dot_exact(tril, log_g), jnp.sum(log_g, axis=0, keepdims=True)


def _gla_fwd(qg, kg, vg, z):
    nc = SEQ // CH
    steps = nc + 2
    kw, vw = GLA_HEADS * DK, GLA_HEADS * DV

    def body(q_ref, k_ref, v_ref, z_ref, o_ref, st_ref, st):
        t = pl.program_id(0)

        @pl.when(t == 0)
        def _():
            st[...] = jnp.zeros_like(st)

        st_prev = st[...]
        st_ref[0] = st_prev
        rmask = _gla_rowmask(t)
        b, b_last = _gla_decay(z_ref[...], rmask)
        q = q_ref[...] * (rmask * DK ** -0.5)
        k = k_ref[...] * rmask
        v = v_ref[...] * rmask
        qe = q * jnp.exp(b)
        ke = k * jnp.exp(-b)
        kd = k * jnp.exp(b_last - b)
        e_last = jnp.exp(b_last)
        causal = _iota((CH, CH), 0) >= _iota((CH, CH), 1)
        for h in range(GLA_HEADS):
            ks, vs_ = slice(h * DK, (h + 1) * DK), slice(h * DV, (h + 1) * DV)
            a = jnp.where(causal, _dot_nt(qe[:, ks], ke[:, ks]), 0.0)
            o_ref[:, vs_] = _dot(a, v[:, vs_]) + _dot_nt(qe[:, ks], st_prev[:, ks])
            st[:, ks] = st_prev[:, ks] * e_last[:, ks] + _dot_tn(v[:, vs_], kd[:, ks])

    blk = lambda w: pl.BlockSpec((CH, w), lambda t: (_gla_block(t), 0))
    return pl.pallas_call(
        body, name="gla_fwd", grid=(steps,),
        in_specs=[blk(kw), blk(kw), blk(vw), blk(kw)],
        out_specs=[blk(vw), pl.BlockSpec((1, DV, kw), lambda t: (t, 0, 0))],
        out_shape=[pltpu.HBM((_lp(), vw), F32), pltpu.HBM((steps, DV, kw), F32)],
        scratch_shapes=[pltpu.VMEM((DV, kw), F32)],
        compiler_params=_params(16, dimension_semantics=_seq()),
    )(*_hbm(qg, kg, vg, z))


def _post_mix(o_s, o_gla, r_g, h0, gn4, w_out, g1, b1):
    tm = _row_tile(384)
    lp = _lp()

    def body(os_ref, og_ref, r_ref, h0_ref, gn_ref, w_ref, g_ref, b_ref, o_ref, pre_ref, h1_ref):
        for pos, h in enumerate(HEAD_POS):
            o_ref[:, h * DH:(h + 1) * DH] = os_ref[:, pos * DH:(pos + 1) * DH].astype(BF16)
        for h in range(GLA_HEADS):
            hs = slice(h * DV, (h + 1) * DV)
            xg = og_ref[:, hs]
            n = xg * lax.rsqrt(jnp.mean(xg * xg, axis=-1, keepdims=True) + RMS_EPS) * gn_ref[...]
            r = r_ref[:, hs]
            o_ref[:, 512 + h * DV:512 + (h + 1) * DV] = (n * (r * _sigmoid(r))).astype(BF16)
        pre = ALPHA * h0_ref[...] + _dot(o_ref[...], w_ref[...])
        pre_ref[...] = pre
        xhat, _ = _ln_stats(pre)
        h1_ref[...] = xhat * g_ref[...] + b_ref[...]

    return pl.pallas_call(
        body, name="post_mix", grid=(lp // tm,),
        in_specs=[_rows(tm, 512), _rows(tm, 512), _rows(tm, 512), _rows(tm, D), _const((1, DV)), _const((D, D)),
                  _const((1, D)), _const((1, D))],
        out_specs=[_rows(tm, D), _rows(tm, D), _rows(tm, D)],
        out_shape=[pltpu.HBM((lp, D), BF16), pltpu.HBM((lp, D), F32),
                   pltpu.HBM((lp, D), F32)],
        compiler_params=_params(32, dimension_semantics=_seq()),
    )(*_hbm(o_s, o_gla, r_g, h0, gn4, w_out, g1, b1))


def _ffn_fwd(h1, wg_t, wu_t, wd):
    tm = _row_tile(192)
    lp = _lp()

    def body(h_ref, wg_ref, wu_ref, wd_ref, g_ref, u_ref, pre_ref):
        h = h_ref[...]
        g = _dot_nt(h, wg_ref[...])
        u = _dot_nt(h, wu_ref[...])
        g_ref[...] = g
        u_ref[...] = u
        pre_ref[...] = ALPHA * h + _dot(g * _sigmoid(g) * u, wd_ref[...])

    return pl.pallas_call(
        body, name="ffn_fwd", grid=(lp // tm,),
        in_specs=[_rows(tm, D), _const((D_FF, D)), _const((D_FF, D)), _const((D_FF, D))],
        out_specs=[_rows(tm, D_FF), _rows(tm, D_FF), _rows(tm, D)],
        out_shape=[pltpu.HBM((lp, D_FF), F32), pltpu.HBM((lp, D_FF), F32),
                   pltpu.HBM((lp, D), F32)],
        compiler_params=_params(48, dimension_semantics=_seq()),
    )(*_hbm(h1, wg_t, wu_t, wd))


def _ln2_loss_bwd(pre2, target, g2, b2):
    nb = SEQ // BLK

    def body(p_ref, t_ref, g_ref, b_ref, dp_ref, loss_ref, dg_ref, db_ref, acc):
        i = pl.program_id(0)

        @pl.when(i == 0)
        def _():
            acc[...] = jnp.zeros_like(acc)
            dg_ref[...] = jnp.zeros_like(dg_ref)
            db_ref[...] = jnp.zeros_like(db_ref)

        real = jnp.where(i < nb, 1.0, 0.0)
        xhat, rstd = _ln_stats(p_ref[...])
        diff = (xhat * g_ref[...] + b_ref[...] - t_ref[...]) * real
        acc[...] += jnp.sum(diff * diff, axis=0, keepdims=True)
        dy = diff * (1.0 / D)
        dp_ref[...] = _ln_bwd(dy, xhat, rstd, g_ref[...])
        dg_ref[...] += jnp.sum(dy * xhat, axis=0, keepdims=True)
        db_ref[...] += jnp.sum(dy, axis=0, keepdims=True)

        @pl.when(i == nb)
        def _():
            loss_ref[...] = jnp.zeros_like(loss_ref) + (0.5 / D) * jnp.sum(acc[...], axis=1, keepdims=True)

    return pl.pallas_call(
        body, name="ln2_loss_bwd", grid=(nb + 1,),
        in_specs=[_rows(BLK, D), pl.BlockSpec((BLK, D), lambda i: (jnp.minimum(i, nb - 1), 0)), _const((1, D)),
                  _const((1, D))],
        out_specs=[_rows(BLK, D), _acc((1, LANE)), _acc((1, D)), _acc((1, D))],
        out_shape=[pltpu.HBM((_lp(), D), F32), pltpu.HBM((1, LANE), F32),
                   pltpu.HBM((1, D), F32), pltpu.HBM((1, D), F32)],
        scratch_shapes=[pltpu.VMEM((1, D), F32)],
        compiler_params=_params(16, dimension_semantics=_seq()),
    )(*_hbm(pre2, target, g2, b2))


def _ffn_bwd(dpre2, g, u, pre1, wg_t, wu_t, wd, g1):
    tm = _row_tile(192)
    lp = _lp()

    def body(dp_ref, g_ref, u_ref, p1_ref, wg_ref, wu_ref, wd_ref, g1_ref, a_ref, dg_ref, du_ref, dp1_ref,
             dg1_ref, db1_ref):
        @pl.when(pl.program_id(0) == 0)
        def _():
            dg1_ref[...] = jnp.zeros_like(dg1_ref)
            db1_ref[...] = jnp.zeros_like(db1_ref)

        dp = dp_ref[...]
        gg, uu = g_ref[...], u_ref[...]
        sg = _sigmoid(gg)
        silu = gg * sg
        da = _dot_nt(dp, wd_ref[...])
        a_ref[...] = (silu * uu).astype(BF16)
        dgate = (da * uu * (sg * (1.0 + gg * (1.0 - sg)))).astype(BF16)
        dup = (da * silu).astype(BF16)
        dg_ref[...] = dgate
        du_ref[...] = dup
        dh1 = ALPHA * dp + _dot(dgate, wg_ref[...]) + _dot(dup, wu_ref[...])
        xhat, rstd = _ln_stats(p1_ref[...])
        dp1_ref[...] = _ln_bwd(dh1, xhat, rstd, g1_ref[...])
        dg1_ref[...] += jnp.sum(dh1 * xhat, axis=0, keepdims=True)
        db1_ref[...] += jnp.sum(dh1, axis=0, keepdims=True)

    return pl.pallas_call(
        body, name="ffn_bwd", grid=(lp // tm,),
        in_specs=[_rows(tm, D), _rows(tm, D_FF), _rows(tm, D_FF), _rows(tm, D), _const((D_FF, D)), _const((D_FF, D)),
                  _const((D_FF, D)), _const((1, D))],
        out_specs=[_rows(tm, D_FF), _rows(tm, D_FF), _rows(tm, D_FF), _rows(tm, D), _acc((1, D)), _acc((1, D))],
        out_shape=[pltpu.HBM((lp, D_FF), BF16)] * 3
        + [pltpu.HBM((lp, D), F32), pltpu.HBM((1, D), F32), pltpu.HBM((1, D), F32)],
        compiler_params=_params(52, dimension_semantics=_seq()),
    )(*_hbm(dpre2, g, u, pre1, wg_t, wu_t, wd, g1))


def _atb(a, b, name):
    lp = _lp()
    tm = _row_tile(384)
    n, w = a.shape[1], b.shape[1]
    bw = 512 if n * w * 4 > (4 << 20) else w

    def body(a_ref, b_ref, o_ref):
        @pl.when(pl.program_id(1) == 0)
        def _():
            o_ref[...] = jnp.zeros_like(o_ref)

        o_ref[...] += _dot_tn(a_ref[...], b_ref[...])

    return pl.pallas_call(
        body, name=name, grid=(w // bw, lp // tm),
        in_specs=[pl.BlockSpec((tm, n), lambda j, k: (k, 0)), pl.BlockSpec((tm, bw), lambda j, k: (k, j))],
        out_specs=pl.BlockSpec((n, bw), lambda j, k: (0, j)),
        out_shape=pltpu.HBM((n, w), F32),
        compiler_params=_params(48, dimension_semantics=_seq(2)),
    )(*_hbm(a, b))


def _out_bwd(dpre1, w_out, o_gla, r_g, gn4):
    tm = _row_tile(384)
    lp = _lp()

    def body(dp_ref, w_ref, og_ref, r_ref, gn_ref, dos_ref, dog_ref, dr_ref, dgn_ref):
        @pl.when(pl.program_id(0) == 0)
        def _():
            dgn_ref[...] = jnp.zeros_like(dgn_ref)

        do = _dot_nt(dp_ref[...], w_ref[...])
        for pos, h in enumerate(HEAD_POS):
            dos_ref[:, pos * DH:(pos + 1) * DH] = do[:, h * DH:(h + 1) * DH]
        gn = gn_ref[...]
        for h in range(GLA_HEADS):
            hs = slice(h * DV, (h + 1) * DV)
            xg = og_ref[:, hs]
            rstd = lax.rsqrt(jnp.mean(xg * xg, axis=-1, keepdims=True) + RMS_EPS)
            nx = xg * rstd
            r = r_ref[:, hs]
            sr = _sigmoid(r)
            d_o = do[:, 512 + h * DV:512 + (h + 1) * DV]
            dr_ref[:, hs] = d_o * (nx * gn) * (sr * (1.0 + r * (1.0 - sr)))
            dn = d_o * (r * sr)
            dgn_ref[...] += jnp.sum(dn * nx, axis=0, keepdims=True)
            dnx = dn * gn
            dog_ref[:, hs] = rstd * (dnx - nx * jnp.mean(dnx * nx, axis=-1, keepdims=True))

    return pl.pallas_call(
        body, name="out_bwd", grid=(lp // tm,),
        in_specs=[_rows(tm, D), _const((D, D)), _rows(tm, 512), _rows(tm, 512), _const((1, DV))],
        out_specs=[_rows(tm, 512), _rows(tm, 512), _rows(tm, 512), _acc((1, DV))],
        out_shape=[pltpu.HBM((lp, 512), F32)] * 3 + [pltpu.HBM((1, DV), F32)],
        compiler_params=_params(32, dimension_semantics=_seq()),
    )(*_hbm(dpre1, w_out, o_gla, r_g, gn4))


def _gla_bwd(qg, kg, vg, z, do_gla, st_all):
    nc = SEQ // CH
    steps = nc + 2
    kw, vw = GLA_HEADS * DK, GLA_HEADS * DV

    def body(q_ref, k_ref, v_ref, z_ref, do_ref, st_ref, dq_ref, dk_ref, dv_ref, dz_ref, dst):
        t = steps - 1 - pl.program_id(0)

        @pl.when(pl.program_id(0) == 0)
        def _():
            dst[...] = jnp.zeros_like(dst)

        rmask = _gla_rowmask(t)
        zz = z_ref[...]
        b, b_last = _gla_decay(zz, rmask)
        e_b, e_nb, e_kd, e_last = jnp.exp(b), jnp.exp(-b), jnp.exp(b_last - b), jnp.exp(b_last)
        q = q_ref[...] * (rmask * DK ** -0.5)
        k = k_ref[...] * rmask
        v = v_ref[...] * rmask
        qe, ke, kd = q * e_b, k * e_nb, k * e_kd
        st_prev = st_ref[0]
        dst_new = dst[...]
        causal = _iota((CH, CH), 0) >= _iota((CH, CH), 1)
        dqe_parts, dke_parts, dkd_parts = [], [], []
        for h in range(GLA_HEADS):
            ks, vs_ = slice(h * DK, (h + 1) * DK), slice(h * DV, (h + 1) * DV)
            d_o = do_ref[:, vs_]
            a = jnp.where(causal, _dot_nt(qe[:, ks], ke[:, ks]), 0.0)
            da = jnp.where(causal, _dot_nt(d_o, v[:, vs_]), 0.0)
            dqe_parts.append(_dot(d_o, st_prev[:, ks]) + _dot(da, ke[:, ks]))
            dke_parts.append(_dot_tn(da, qe[:, ks]))
            dkd_parts.append(_dot(v[:, vs_], dst_new[:, ks]))
            dv_ref[:, vs_] = _dot_tn(a, d_o) + _dot_nt(kd[:, ks], dst_new[:, ks])
            dst[:, ks] = dst_new[:, ks] * e_last[:, ks] + _dot_tn(d_o, qe[:, ks])
        dqe = jnp.concatenate(dqe_parts, axis=1)
        dke = jnp.concatenate(dke_parts, axis=1)
        dkd = jnp.concatenate(dkd_parts, axis=1)
        dq_ref[...] = dqe * e_b * (rmask * DK ** -0.5)
        dk_ref[...] = (dke * e_nb + dkd * e_kd) * rmask
        dkd_kd = dkd * kd
        db = dqe * qe - dke * ke - dkd_kd
        db_last = jnp.sum(dst_new * st_prev, axis=0, keepdims=True) * e_last + jnp.sum(dkd_kd, axis=0, keepdims=True)
        triu = (_iota((CH, CH), 0) <= _iota((CH, CH), 1)).astype(F32)
        dlog_g = _dot_exact(triu, db) + db_last
        dz_ref[...] = dlog_g * (rmask / GLA_TAU) * _sigmoid(-zz)

    blk = lambda w: pl.BlockSpec((CH, w), lambda s: (_gla_block(steps - 1 - s), 0))
    return pl.pallas_call(
        body, name="gla_bwd", grid=(steps,),
        in_specs=[blk(kw), blk(kw), blk(vw), blk(kw), blk(vw), pl.BlockSpec((1, DV, kw), lambda s: (steps - 1 - s, 0, 0))],
        out_specs=[blk(kw), blk(kw), blk(vw), blk(kw)],
        out_shape=[pltpu.HBM((_lp(), kw), F32), pltpu.HBM((_lp(), kw), F32),
                   pltpu.HBM((_lp(), vw), F32), pltpu.HBM((_lp(), kw), F32)],
        scratch_shapes=[pltpu.VMEM((DV, kw), F32)],
        compiler_params=_params(16, dimension_semantics=_seq()),
    )(*_hbm(qg, kg, vg, z, do_gla, st_all))


def _swa_bwd(sinks, qs, ks, vs, do_s):
    nb = SEQ // BLK
    kvw = SWA_KV_HEADS * DH
    scale = DH ** -0.5
    heads = range(SWA_HEADS)

    def body(sink_ref, q_ref, km_ref, kp_ref, kc_ref, vm_ref, vp_ref, vc_ref, do_ref,
             dq_ref, dk_ref, dv_ref, dsink_ref, carry_k, carry_v, meta_k, meta_v):
        n = pl.program_id(0)

        @pl.when(n == 0)
        def _():
            for r in (carry_k, carry_v, meta_k, meta_v):
                r[...] = jnp.zeros_like(r)
            dsink_ref[...] = jnp.zeros_like(dsink_ref)

        @pl.when(n <= nb)
        def _():
            negdist, maskbias = _swa_bias(n)
            lane = _iota((1, LANE), 1)
            k_all = jnp.concatenate([km_ref[...], kp_ref[...], kc_ref[...]], axis=0).astype(BF16)
            v_all = jnp.concatenate([vm_ref[...], vp_ref[...], vc_ref[...]], axis=0).astype(BF16)
            q = [_swa_half(q_ref, pos, scale) for pos in heads]
            d_o = [_swa_half(do_ref, pos) for pos in heads]
            t = [_dot_nt(q[pos], k_all) + (2.0 ** -(HEAD_POS[pos] + 1) * negdist + maskbias) for pos in heads]
            dp = [_dot_nt(d_o[pos], v_all) for pos in heads]
            soft = [_swa_softmax(t[pos], sink_ref[HEAD_POS[pos]]) for pos in heads]
            p = [s[0] for s in soft]
            delta = [jnp.sum(p[pos] * dp[pos], axis=-1, keepdims=True) for pos in heads]
            ds = [(p[pos] * (dp[pos] - delta[pos])).astype(BF16) for pos in heads]
            dq = [_dot(ds[pos], k_all) for pos in heads]
            for col in range(SWA_HEADS // 2):
                dq_ref[:, col * LANE:(col + 1) * LANE] = scale * _swa_merge(dq[2 * col], dq[2 * col + 1])
            dsink = jnp.zeros((1, LANE), F32)
            for pos in heads:
                dsink = dsink + jnp.where(lane == HEAD_POS[pos],
                                          -jnp.sum(soft[pos][1] * delta[pos], axis=0, keepdims=True), 0.0)
            dsink_ref[...] += dsink
            dk3 = _dot_tn(jnp.concatenate(q, axis=0), jnp.concatenate(ds, axis=0)).T
            dv3 = _dot_tn(jnp.concatenate(d_o, axis=0), jnp.concatenate([x.astype(BF16) for x in p], axis=0)).T
            meta_k[...] += dk3[0:BLK]
            meta_v[...] += dv3[0:BLK]
            dk_ref[...] = carry_k[...] + dk3[BLK:2 * BLK]
            dv_ref[...] = carry_v[...] + dv3[BLK:2 * BLK]
            carry_k[...] = dk3[2 * BLK:3 * BLK]
            carry_v[...] = dv3[2 * BLK:3 * BLK]

        @pl.when(n == nb + 1)
        def _():
            dk_ref[...] = meta_k[...]
            dv_ref[...] = meta_v[...]

    kv_out = pl.BlockSpec((BLK, kvw), lambda n: (jnp.where(n == nb + 1, nb, jnp.clip(n - 1, 0, nb - 1)), 0))
    qblk = pl.BlockSpec((BLK, SWA_HEADS * DH), lambda n: (jnp.minimum(n, nb), 0))
    return pl.pallas_call(
        body, name="swa_bwd", grid=(nb + 2,),
        in_specs=[pl.BlockSpec(memory_space=pltpu.SMEM), qblk] + _swa_kv_specs(kvw) + _swa_kv_specs(kvw) + [qblk],
        out_specs=[qblk, kv_out, kv_out, _acc((1, LANE))],
        out_shape=[pltpu.HBM((_lp(), SWA_HEADS * DH), F32), pltpu.HBM((_lp(), kvw), F32),
                   pltpu.HBM((_lp(), kvw), F32), pltpu.HBM((1, LANE), F32)],
        scratch_shapes=[pltpu.VMEM((BLK, kvw), F32)] * 4,
        compiler_params=_params(16, dimension_semantics=_seq()),
    )(sinks, *_hbm(qs, ks, ks, ks, vs, vs, vs, do_s))


def _in_bwd(dqs, dks, dvs, dqg, dkg, dvg, drg, dz, dpre1, w_in_t, wg2_p):
    tm = _row_tile(384)
    lp = _lp()
    widths = (512, 128, 128, 256, 256, 512, 512)
    offs = (O_QS, O_KS, O_VS, O_QG, O_KG, O_VG, O_RG)

    def body(*refs):
        parts, (dz_ref, dp1_ref, w_ref, wg2_ref, dproj_ref, dh0_ref, dbin_ref, dbg_ref) = refs[:7], refs[7:]

        @pl.when(pl.program_id(0) == 0)
        def _():
            dbin_ref[...] = jnp.zeros_like(dbin_ref)
            dbg_ref[...] = jnp.zeros_like(dbg_ref)

        for pos, h in enumerate(HEAD_POS):
            val = parts[0][:, pos * DH:(pos + 1) * DH]
            dproj_ref[:, O_QS + h * DH:O_QS + (h + 1) * DH] = val.astype(BF16)
            dbin_ref[:, O_QS + h * DH:O_QS + (h + 1) * DH] += jnp.sum(val, axis=0, keepdims=True)
        for p_ref, off, wd in zip(parts[1:], offs[1:], widths[1:]):
            val = p_ref[...]
            dproj_ref[:, off:off + wd] = val.astype(BF16)
            dbin_ref[:, off:off + wd] += jnp.sum(val, axis=0, keepdims=True)
        dz = dz_ref[...]
        dlr = _dot_nt(dz, wg2_ref[...])
        dproj_ref[:, O_LR:O_LR + LANE] = dlr.astype(BF16)
        dbin_ref[:, O_LR:O_LR + LANE] += jnp.sum(dlr, axis=0, keepdims=True)
        dbg_ref[...] += jnp.sum(dz, axis=0, keepdims=True)
        dh0_ref[...] = ALPHA * dp1_ref[...] + _dot(dproj_ref[...], w_ref[...])

    return pl.pallas_call(
        body, name="in_bwd", grid=(lp // tm,),
        in_specs=[_rows(tm, w) for w in widths] + [_rows(tm, 256), _rows(tm, D), _const((D_IN_P, D)), _const((LANE, 256))],
        out_specs=[_rows(tm, D_IN_P), _rows(tm, D), _acc((1, D_IN_P)), _acc((1, 256))],
        out_shape=[pltpu.HBM((lp, D_IN_P), BF16), pltpu.HBM((lp, D), F32),
                   pltpu.HBM((1, D_IN_P), F32), pltpu.HBM((1, 256), F32)],
        compiler_params=_params(40, dimension_semantics=_seq()),
    )(*_hbm(dqs, dks, dvs, dqg, dkg, dvg, drg, dz, dpre1, w_in_t, wg2_p))


def _ln_in_bwd(x, meta_ext, dh0, g):
    nb = SEQ // BLK

    def body(x_ref, m_ref, dh_ref, g_ref, dx_ref, dm_ref, dg_ref, db_ref):
        i = pl.program_id(0)

        @pl.when(i == 0)
        def _():
            dg_ref[...] = jnp.zeros_like(dg_ref)
            db_ref[...] = jnp.zeros_like(db_ref)

        xin = jnp.where(i < nb, x_ref[...], m_ref[...])
        xhat, rstd = _ln_stats(xin)
        dh = dh_ref[...]
        dxin = _ln_bwd(dh, xhat, rstd, g_ref[...])
        dg_ref[...] += jnp.sum(dh * xhat, axis=0, keepdims=True)
        db_ref[...] += jnp.sum(dh, axis=0, keepdims=True)

        @pl.when(i < nb)
        def _():
            dx_ref[...] = dxin

        @pl.when(i == nb)
        def _():
            dm_ref[...] = dxin

    xblk = pl.BlockSpec((BLK, D), lambda i: (jnp.minimum(i, nb - 1), 0))
    return pl.pallas_call(
        body, name="ln_in_bwd", grid=(nb + 1,),
        in_specs=[xblk, _const((BLK, D)), _rows(BLK, D), _const((1, D))],
        out_specs=[xblk, _acc((BLK, D)), _acc((1, D)), _acc((1, D))],
        out_shape=[pltpu.HBM((SEQ, D), F32), pltpu.HBM((BLK, D), F32),
                   pltpu.HBM((1, D), F32), pltpu.HBM((1, D), F32)],
        compiler_params=_params(16, dimension_semantics=_seq()),
    )(*_hbm(x, meta_ext, dh0, g))


def _local_step(x, target, meta_full, ln_in_g, ln_in_b, w_in_t, b_in, wg2, bg2, sinks, gn, w_out, g1, b1,
                wg_t, wu_t, wd, g2, b2):
    row = lambda v: v.reshape(1, -1).astype(F32)
    meta_ext = jnp.pad(meta_full, ((META_OFF, BLK - CH), (0, 0)))
    b_in_p = jnp.pad(row(b_in), ((0, 0), (0, D_IN_P - D_IN)))
    wg2_p = jnp.pad(wg2, ((0, LANE - wg2.shape[0]), (0, 0))).astype(BF16)
    gn4 = row(gn)
    sinks = sinks.reshape(-1).astype(F32)

    h0 = _ln_in_fwd(x, meta_ext, row(ln_in_g), row(ln_in_b))
    qs, ks, vs, qg, kg, vg, rg, glr, z = _in_proj(h0, w_in_t, b_in_p, wg2_p, row(bg2))
    o_s = _swa_fwd(sinks, qs, ks, vs)
    o_gla, st_all = _gla_fwd(qg, kg, vg, z)
    o, pre1, h1 = _post_mix(o_s, o_gla, rg, h0, gn4, w_out, row(g1), row(b1))
    g, u, pre2 = _ffn_fwd(h1, wg_t, wu_t, wd)

    dpre2, loss, dg2, db2 = _ln2_loss_bwd(pre2, target, row(g2), row(b2))
    a, dgate, dup, dpre1, dg1, db1 = _ffn_bwd(dpre2, g, u, pre1, wg_t, wu_t, wd, row(g1))
    dwd = _atb(a, dpre2, "dw_down")
    dwg_t = _atb(dgate, h1, "dw_gate")
    dwu_t = _atb(dup, h1, "dw_up")
    dw_out = _atb(o, dpre1, "dw_out")
    do_s, do_gla, drg, dgn = _out_bwd(dpre1, w_out, o_gla, rg, gn4)
    dqg, dkg, dvg, dz = _gla_bwd(qg, kg, vg, z, do_gla, st_all)
    dqs, dks, dvs, dsinks = _swa_bwd(sinks, qs, ks, vs, do_s)
    dproj, dh0, db_in_p, dbg2 = _in_bwd(dqs, dks, dvs, dqg, dkg, dvg, drg, dz, dpre1, w_in_t, wg2_p)
    dw_in_t = _atb(dproj, h0, "dw_in")
    dwg2_p = _atb(glr, dz, "dw_gate_lr2")
    dx, dmeta_blk, dg_in, db_in_ln = _ln_in_bwd(x, meta_ext, dh0, row(ln_in_g))

    grads = dict(
        w_in=dw_in_t, w_out=dw_out, w_g=dwg_t, w_u=dwu_t, w_d=dwd,
        meta=dmeta_blk[META_OFF:CH], ln_in_g=dg_in, ln_in_b=db_in_ln, ln1_g=dg1, ln1_b=db1, ln2_g=dg2, ln2_b=db2,
        b_in=db_in_p[:, :D_IN], wg2=dwg2_p[:wg2.shape[0]], bg2=dbg2, sinks=dsinks[:, :SWA_HEADS], gn=dgn)
    return loss[0, 0], dx, grads


HBM = pl.BlockSpec(memory_space=pltpu.HBM)


def _place():
    return lax.axis_index("x"), lax.axis_index("y"), lax.axis_index("c")


def _other_chips(x, y):
    return [(1 - x, y), (x, 1 - y), (1 - x, 1 - y)]


def _dma_sems(n):
    return pltpu.SemaphoreType.DMA((n,))


def _comm_params():
    return pltpu.CompilerParams(has_side_effects=True)


def _gather_halves(shards):
    n = len(shards)

    def body(*refs):
        ins, outs = refs[:n], refs[n:2 * n]
        ici_send, ici_recv, d2d_send, d2d_recv = refs[2 * n:]
        x, y, c = _place()
        mine = 2 * x + y
        chips = _other_chips(x, y)

        def ici(a, j, src_chip):
            px, py = chips[j]
            return pltpu.make_async_remote_copy(ins[a].at[c], outs[a].at[src_chip, c], ici_send.at[3 * a + j],
                                                ici_recv.at[3 * a + j], device_id=(px, py, c), device_id_type=MESH)

        def d2d(a, j, half):
            px, py = chips[j]
            blk = outs[a].at[2 * px + py, half]
            return pltpu.make_async_remote_copy(blk, blk, d2d_send.at[3 * a + j], d2d_recv.at[3 * a + j],
                                                device_id=(x, y, 1 - c), device_id_type=MESH)

        sends = [ici(a, j, mine) for a in range(n) for j in range(3)]
        for cp in sends:
            cp.start()
        passed = []
        for a in range(n):
            for j, (px, py) in enumerate(chips):
                ici(a, j, 2 * px + py).wait_recv()
                fwd = d2d(a, j, c)
                fwd.start()
                passed.append(fwd)
        for a in range(n):
            for j in range(3):
                d2d(a, j, 1 - c).wait_recv()
        for cp in sends + passed:
            cp.wait_send()

    gathered = pl.pallas_call(
        body, name="gather_halves",
        in_specs=[HBM] * n, out_specs=[HBM] * n,
        out_shape=[pltpu.HBM((N_CHIPS,) + s.shape, s.dtype) for s in shards],
        scratch_shapes=[_dma_sems(3 * n)] * 4,
        compiler_params=_comm_params(),
    )(*_hbm(*shards))
    mine = 2 * lax.axis_index("x") + lax.axis_index("y")
    return [lax.dynamic_update_index_in_dim(g, s, mine, axis=0) for g, s in zip(gathered, shards)]


def _sibling_exchange(grads):
    n = len(grads)

    def body(*refs):
        ins, outs = refs[:n], refs[n:2 * n]
        send_sems, recv_sems = refs[2 * n:]
        x, y, c = _place()
        copies = []
        for a in range(n):
            for s in range(N_CHIPS):
                cp = pltpu.make_async_remote_copy(ins[a].at[s, 1 - c], outs[a].at[s], send_sems.at[N_CHIPS * a + s],
                                                  recv_sems.at[N_CHIPS * a + s], device_id=(x, y, 1 - c),
                                                  device_id_type=MESH)
                cp.start()
                copies.append(cp)
        for cp in copies:
            cp.wait_recv()
        for cp in copies:
            cp.wait_send()

    return pl.pallas_call(
        body, name="sibling_exchange", in_specs=[HBM] * n, out_specs=[HBM] * n,
        out_shape=[pltpu.HBM((N_CHIPS, g.shape[2], D), F32) for g in grads],
        scratch_shapes=[_dma_sems(N_CHIPS * n)] * 2,
        compiler_params=_comm_params(),
    )(*_hbm(*grads))


def _add_halves(core, grad, recv, dtype, name):
    h = grad.shape[2]

    def body(c_ref, a_ref, b_ref, o_ref):
        o_ref[...] = (a_ref[0] + b_ref[...]).astype(dtype)

    return pl.pallas_call(
        body, name=name,
        grid_spec=pltpu.PrefetchScalarGridSpec(
            num_scalar_prefetch=1, grid=(N_CHIPS,),
            in_specs=[pl.BlockSpec((1, 1, h, D), lambda s, c: (s, c[0], 0, 0)),
                      pl.BlockSpec((1, h, D), lambda s, c: (s, 0, 0))],
            out_specs=pl.BlockSpec((1, h, D), lambda s, c: (s, 0, 0))),
        out_shape=pltpu.HBM((N_CHIPS, h, D), dtype),
        compiler_params=_params(16, dimension_semantics=_seq()),
    )(core, *_hbm(grad, recv))


def _chip_scatter(parts, with_own):
    n = len(parts)

    def body(*refs):
        ins, outs = refs[:n], refs[n:2 * n]
        send_sems, recv_sems, local_sems = refs[2 * n:]
        x, y, c = _place()
        mine = 2 * x + y
        chips = _other_chips(x, y)
        local = [pltpu.make_async_copy(ins[a].at[mine], outs[a].at[mine], local_sems.at[a]) for a in range(n)
                 if with_own[a]]
        for cp in local:
            cp.start()
        sends = []
        for a in range(n):
            for j, (px, py) in enumerate(chips):
                cp = pltpu.make_async_remote_copy(ins[a].at[2 * px + py], outs[a].at[mine], send_sems.at[3 * a + j],
                                                  recv_sems.at[3 * a + j], device_id=(px, py, c), device_id_type=MESH)
                cp.start()
                sends.append(cp)
        for a in range(n):
            for j, (px, py) in enumerate(chips):
                pltpu.make_async_remote_copy(ins[a].at[mine], outs[a].at[2 * px + py], send_sems.at[3 * a + j],
                                             recv_sems.at[3 * a + j], device_id=(px, py, c),
                                             device_id_type=MESH).wait_recv()
        for cp in sends:
            cp.wait_send()
        for cp in local:
            cp.wait()

    return pl.pallas_call(
        body, name="chip_scatter", in_specs=[HBM] * n, out_specs=[HBM] * n,
        out_shape=[pltpu.HBM(p.shape, p.dtype) for p in parts],
        scratch_shapes=[_dma_sems(3 * n)] * 2 + [_dma_sems(n)],
        compiler_params=_comm_params(),
    )(*_hbm(*parts))


def _sum_chips(slots, first, rest, name):
    h = first.shape[1]

    def body(i_ref, a_ref, b_ref, c_ref, d_ref, o_ref):
        o_ref[...] = ((a_ref[...].astype(F32) + b_ref[...].astype(F32)) + c_ref[...].astype(F32)) + d_ref[...].astype(F32)

    slab = lambda k: pl.BlockSpec((1, h, D), lambda i, ix: (ix[k], 0, 0))
    return pl.pallas_call(
        body, name=name,
        grid_spec=pltpu.PrefetchScalarGridSpec(num_scalar_prefetch=1, grid=(1,),
                                               in_specs=[slab(0), slab(1), slab(2), slab(3)], out_specs=slab(4)),
        out_shape=pltpu.HBM((2, h, D), F32),
        compiler_params=_params(16, dimension_semantics=_seq()),
    )(slots, *_hbm(first, rest, rest, rest))


def _join_halves(halves):
    n = len(halves)

    def body(*refs):
        outs = refs[n:2 * n]
        send_sems, recv_sems = refs[2 * n:]
        x, y, c = _place()

        def copy(a, slab):
            return pltpu.make_async_remote_copy(outs[a].at[slab], outs[a].at[slab], send_sems.at[a], recv_sems.at[a],
                                                device_id=(x, y, 1 - c), device_id_type=MESH)

        for a in range(n):
            copy(a, c).start()
        for a in range(n):
            copy(a, 1 - c).wait_recv()
        for a in range(n):
            copy(a, c).wait_send()

    return pl.pallas_call(
        body, name="join_halves", in_specs=[HBM] * n, out_specs=[HBM] * n,
        out_shape=[pltpu.HBM(h.shape, F32) for h in halves],
        input_output_aliases={a: a for a in range(n)},
        scratch_shapes=[_dma_sems(n)] * 2,
        compiler_params=_comm_params(),
    )(*_hbm(*halves))


def _reduce_scatter(grads, wire_dtypes, same_order, names):
    x, y, c = _place()
    core = c.astype(jnp.int32).reshape(1)
    others = [2 * px + py for px, py in _other_chips(x, y)]
    own_first = jnp.stack([2 * x + y] + others + [c]).astype(jnp.int32)
    chip_order = jnp.stack([0 * c, 0 * c + 1, 0 * c + 2, 0 * c + 3, c]).astype(jnp.int32)
    recv = _sibling_exchange(grads)
    parts = [_add_halves(core, g, r, dt, "add_halves_" + nm) for g, r, dt, nm in zip(grads, recv, wire_dtypes, names)]
    got = _chip_scatter(parts, same_order)
    halves = [_sum_chips(chip_order, q, q, "sum_chips_" + nm) if fixed else _sum_chips(own_first, p, q, "sum_chips_" + nm)
              for p, q, fixed, nm in zip(parts, got, same_order, names)]
    return [f.reshape(2 * f.shape[1], D) for f in _join_halves(halves)]


def _adamw(w, g, m, v, name):
    rows, cols = w.shape
    if rows % 8 == 0:
        tr = max(t for t in range(8, 257, 8) if rows % t == 0)
        grid, blk = (rows // tr,), pl.BlockSpec((tr, cols), lambda i: (i, 0))
    else:
        grid, blk = (cols // 256,), pl.BlockSpec((rows, 256), lambda i: (0, i))

    def body(w_ref, g_ref, m_ref, v_ref, d_ref, nm_ref, nv_ref):
        gg = g_ref[...]
        nm = ADAM_B1 * m_ref[...] + (1.0 - ADAM_B1) * gg
        nv = ADAM_B2 * v_ref[...] + (1.0 - ADAM_B2) * (gg * gg)
        m_hat = nm / (1.0 - ADAM_B1 ** ADAM_STEP)
        v_hat = nv / (1.0 - ADAM_B2 ** ADAM_STEP)
        d_ref[...] = -ADAM_LR * (m_hat / (jnp.sqrt(v_hat) + ADAM_EPS) + ADAM_WD * w_ref[...])
        nm_ref[...] = nm
        nv_ref[...] = nv

    return pl.pallas_call(
        body, name=name, grid=grid,
        in_specs=[blk] * 4, out_specs=[blk] * 3,
        out_shape=[pltpu.HBM(w.shape, F32)] * 3,
        compiler_params=_params(32, dimension_semantics=_seq()),
    )(*_hbm(w, g, m, v))


def _flat_rows(v, rows):
    flat = v.reshape(-1).astype(F32)
    return jnp.pad(flat, (0, rows * D - flat.shape[0])).reshape(rows, D)


def _small_pack(gr):
    tail = jnp.concatenate([gr["bg2"].reshape(-1), gr["sinks"].reshape(-1), gr["gn"].reshape(-1)])
    rows = [gr["meta"].reshape(N_META, D)] + [gr[k].reshape(1, D) for k in
                                              ("ln_in_g", "ln_in_b", "ln1_g", "ln1_b", "ln2_g", "ln2_b")]
    rows += [_flat_rows(gr["b_in"], 3), _flat_rows(gr["wg2"], 4), _flat_rows(tail, 1)]
    packed = jnp.concatenate(rows, axis=0)
    return jnp.pad(packed, ((0, SMALL_ROWS - packed.shape[0]), (0, 0)))


def _small_unpack(p):
    flat = lambda r0, n, size: p[r0:r0 + n].reshape(-1)[:size]
    tail = p[29]
    return dict(meta=p[0:N_META], ln_in_g=p[16], ln_in_b=p[17], ln1_g=p[18], ln1_b=p[19], ln2_g=p[20], ln2_b=p[21],
                b_in=flat(22, 3, D_IN), wg2=flat(25, 4, 16 * 256).reshape(16, 256), bg2=tail[0:256],
                sinks=tail[256:256 + SWA_HEADS], gn=tail[256 + SWA_HEADS:256 + SWA_HEADS + DV])


BIG = ("w_in", "w_out", "w_g", "w_u", "w_d")


def kernel(x, meta_tokens, ln_in_g, ln_in_b, w_in, b_in, w_gate_lr2, b_gate_lr2, attn_sinks, gla_norm_g, w_out, ln1_g, ln1_b, w_ffn_gate, w_ffn_up, w_ffn_down, ln2_g, ln2_b, loss_target, m_meta_tokens, m_ln_in_g, m_ln_in_b, m_w_in, m_b_in, m_w_gate_lr2, m_b_gate_lr2, m_attn_sinks, m_gla_norm_g, m_w_out, m_ln1_g, m_ln1_b, m_w_ffn_gate, m_w_ffn_up, m_w_ffn_down, m_ln2_g, m_ln2_b, v_meta_tokens, v_ln_in_g, v_ln_in_b, v_w_in, v_b_in, v_w_gate_lr2, v_b_gate_lr2, v_attn_sinks, v_gla_norm_g, v_w_out, v_ln1_g, v_ln1_b, v_w_ffn_gate, v_w_ffn_up, v_w_ffn_down, v_ln2_g, v_ln2_b):
    chip = 2 * lax.axis_index("x") + lax.axis_index("y")

    halves = lambda a: a.reshape(2, a.shape[0] // 2, a.shape[1])
    r_in = SHARD_ROWS["w_in"]
    shards = [jnp.pad(w_in[0].T.astype(BF16), ((0, W_IN_WIN - r_in), (0, 0))), w_out[0].astype(BF16),
              w_ffn_gate[0].T.astype(BF16), w_ffn_up[0].T.astype(BF16), w_ffn_down[0].astype(BF16), meta_tokens,
              w_gate_lr2[0]]
    g_in, g_out, g_g, g_u, g_d, g_meta, g_wg2 = _gather_halves([halves(a) for a in shards])
    w_in_t = jnp.pad(g_in.reshape(N_CHIPS, W_IN_WIN, D)[:, :r_in].reshape(D_IN, D), ((0, D_IN_P - D_IN), (0, 0)))
    meta_full = jnp.concatenate([g_meta[s].reshape(N_META, -1) for s in range(N_CHIPS)], axis=1)
    wg2_full = jnp.concatenate([g_wg2[s].reshape(w_gate_lr2.shape[1], -1) for s in range(N_CHIPS)], axis=1)

    loss_part, dx, gr = _local_step(
        x[0], loss_target[0], meta_full, ln_in_g, ln_in_b, w_in_t, b_in[0], wg2_full, b_gate_lr2[0], attn_sinks[0],
        gla_norm_g[0], g_out.reshape(D, D), ln1_g[0], ln1_b[0], g_g.reshape(D_FF, D), g_u.reshape(D_FF, D),
        g_d.reshape(D_FF, D), ln2_g[0], ln2_b[0])
    loss = lax.psum(loss_part, ("x", "y", "c"))

    win_start = [s * r_in // BF16_ROWS * BF16_ROWS for s in range(N_CHIPS)]
    to_send = [jnp.stack([gr["w_in"][st:st + W_IN_WIN] for st in win_start])] + [gr[k] for k in BIG[1:]]
    to_send.append(jnp.broadcast_to(_small_pack(gr), (N_CHIPS, SMALL_ROWS, D)))
    to_send = [a.reshape(N_CHIPS, 2, -1, D) for a in to_send]
    red = _reduce_scatter(to_send, [BF16] * len(BIG) + [F32], [False] * len(BIG) + [True], list(BIG) + ["small"])

    big_g = dict(zip(BIG, red))
    big_g["w_in"] = lax.dynamic_slice_in_dim(red[0], chip * (r_in % BF16_ROWS), r_in, axis=0)
    sg = _small_unpack(red[-1])
    col = lambda a, width: lax.dynamic_slice_in_dim(a, chip * width, width, axis=1)
    grads = dict(
        meta_tokens=col(sg["meta"], D // N_CHIPS), ln_in_g=sg["ln_in_g"], ln_in_b=sg["ln_in_b"],
        w_in=big_g["w_in"].T[None], b_in=sg["b_in"][None], w_gate_lr2=col(sg["wg2"], 256 // N_CHIPS)[None],
        b_gate_lr2=sg["bg2"][None], attn_sinks=sg["sinks"][None], gla_norm_g=sg["gn"][None],
        w_out=big_g["w_out"][None], ln1_g=sg["ln1_g"][None], ln1_b=sg["ln1_b"][None],
        w_ffn_gate=big_g["w_g"].T[None], w_ffn_up=big_g["w_u"].T[None], w_ffn_down=big_g["w_d"][None],
        ln2_g=sg["ln2_g"][None], ln2_b=sg["ln2_b"][None])
    weights = dict(meta_tokens=meta_tokens, ln_in_g=ln_in_g, ln_in_b=ln_in_b, w_in=w_in, b_in=b_in,
                   w_gate_lr2=w_gate_lr2, b_gate_lr2=b_gate_lr2, attn_sinks=attn_sinks, gla_norm_g=gla_norm_g,
                   w_out=w_out, ln1_g=ln1_g, ln1_b=ln1_b, w_ffn_gate=w_ffn_gate, w_ffn_up=w_ffn_up,
                   w_ffn_down=w_ffn_down, ln2_g=ln2_g, ln2_b=ln2_b)
    m_in = dict(meta_tokens=m_meta_tokens, ln_in_g=m_ln_in_g, ln_in_b=m_ln_in_b, w_in=m_w_in, b_in=m_b_in,
                w_gate_lr2=m_w_gate_lr2, b_gate_lr2=m_b_gate_lr2, attn_sinks=m_attn_sinks, gla_norm_g=m_gla_norm_g,
                w_out=m_w_out, ln1_g=m_ln1_g, ln1_b=m_ln1_b, w_ffn_gate=m_w_ffn_gate, w_ffn_up=m_w_ffn_up,
                w_ffn_down=m_w_ffn_down, ln2_g=m_ln2_g, ln2_b=m_ln2_b)
    v_in = dict(meta_tokens=v_meta_tokens, ln_in_g=v_ln_in_g, ln_in_b=v_ln_in_b, w_in=v_w_in, b_in=v_b_in,
                w_gate_lr2=v_w_gate_lr2, b_gate_lr2=v_b_gate_lr2, attn_sinks=v_attn_sinks, gla_norm_g=v_gla_norm_g,
                w_out=v_w_out, ln1_g=v_ln1_g, ln1_b=v_ln1_b, w_ffn_gate=v_w_ffn_gate, w_ffn_up=v_w_ffn_up,
                w_ffn_down=v_w_ffn_down, ln2_g=v_ln2_g, ln2_b=v_ln2_b)
    names = list(weights)
    big_names = ("w_in", "w_out", "w_ffn_gate", "w_ffn_up", "w_ffn_down")

    delta, new_m, new_v = {}, {}, {}
    for k, kk in zip(big_names, BIG):
        flip = (lambda a: a.T) if kk in ("w_in", "w_g", "w_u") else (lambda a: a)
        d_, m_, v_ = _adamw(flip(weights[k][0]), big_g[kk], flip(m_in[k][0]), flip(v_in[k][0]), "adamw_" + k)
        delta[k], new_m[k], new_v[k] = (flip(t)[None] for t in (d_, m_, v_))
    small_names = [k for k in names if k not in big_names]
    sizes = [weights[k].size for k in small_names]
    rows_small = -(-sum(sizes) // D)
    rows_small += -rows_small % 8
    cat = lambda src: _flat_rows(jnp.concatenate([src[k].reshape(-1) for k in small_names]), rows_small)
    d_, m_, v_ = _adamw(cat(weights), cat(grads), cat(m_in), cat(v_in), "adamw_small")
    off = 0
    for k, n in zip(small_names, sizes):
        for dst, src in ((delta, d_), (new_m, m_), (new_v, v_)):
            dst[k] = src.reshape(-1)[off:off + n].reshape(weights[k].shape)
        off += n
    grads = {k: grads[k].reshape(weights[k].shape) for k in names}

    return (loss, dx[None], *[grads[k] for k in names], *[delta[k] for k in names], *[new_m[k] for k in names],
            *[new_v[k] for k in names])
```

```python
import functools

import jax
import jax.numpy as jnp
from jax import lax
from jax.experimental import pallas as pl
from jax.experimental.pallas import tpu as pltpu

F32 = jnp.float32
BF16 = jnp.bfloat16
MESH = pl.DeviceIdType.MESH

D = 1024
SEQ = 4096
N_META = 16
SWA_HEADS, SWA_KV_HEADS, DH = 8, 2, 64
WINDOW = 128
GLA_HEADS, DK, DV = 4, 64, 128
GLA_TAU = 16.0
CH = 64
D_FF = 2816
D_IN = 2320
LN_EPS = 1e-5
RMS_EPS = 1e-6
ALPHA = 2.0 ** 0.25
NEG = -1e30
ADAM_LR, ADAM_B1, ADAM_B2, ADAM_EPS, ADAM_WD, ADAM_STEP = 0.001, 0.9, 0.999, 1e-8, 0.01, 10
O_QS, O_KS, O_VS, O_QG, O_KG, O_VG, O_RG, O_LR = 0, 512, 640, 768, 1024, 1280, 1792, 2304

LANE = 128
BLK = WINDOW
D_IN_P = D_IN + LANE - 16
META_OFF = CH - N_META
HEAD_POS = (0, 4, 1, 5, 2, 6, 3, 7)
TOKEN = (8, LANE)
N_CHIPS = 4
SHARD_ROWS = dict(w_in=D_IN // N_CHIPS, w_out=D // N_CHIPS, w_g=D_FF // N_CHIPS, w_u=D_FF // N_CHIPS,
                  w_d=D_FF // N_CHIPS)
SMALL_ROWS = 32
BF16_ROWS = 16
W_IN_WIN = -(-SHARD_ROWS["w_in"] // (2 * BF16_ROWS)) * 2 * BF16_ROWS
VMEM_CAP_MB = 64


def _lp():
    return SEQ + BLK


def _row_tile(cap):
    lp = _lp()
    return max(t for t in range(16, cap + 1, 16) if lp % t == 0)


def _params(vmem_mb, **kw):
    assert vmem_mb <= VMEM_CAP_MB - 6
    return pltpu.CompilerParams(vmem_limit_bytes=vmem_mb << 20, **kw)


def _seq(n=1):
    return ("arbitrary",) * n


def _const(shape):
    return pl.BlockSpec(shape, lambda *_: (0,) * len(shape), pipeline_mode=pl.Buffered(1))


def _acc(shape):
    return pl.BlockSpec(shape, lambda *_: (0,) * len(shape))


def _rows(tm, width):
    return pl.BlockSpec((tm, width), lambda i: (i, 0))


def _dot(a, b):
    return jnp.dot(a.astype(BF16), b.astype(BF16), preferred_element_type=F32)


def _dot_nt(a, b):
    return lax.dot_general(a.astype(BF16), b.astype(BF16), (((1,), (1,)), ((), ())), preferred_element_type=F32)


def _dot_tn(a, b):
    return lax.dot_general(a.astype(BF16), b.astype(BF16), (((0,), (0,)), ((), ())), preferred_element_type=F32)


def _dot_exact(a, b):
    return jnp.dot(a, b, precision=lax.Precision.HIGHEST, preferred_element_type=F32)


def _ln_stats(x):
    mu = jnp.mean(x, axis=-1, keepdims=True)
    xc = x - mu
    rstd = lax.rsqrt(jnp.mean(xc * xc, axis=-1, keepdims=True) + LN_EPS)
    return xc * rstd, rstd


def _ln_bwd(dy, xhat, rstd, g):
    dxh = dy * g
    return rstd * (dxh - jnp.mean(dxh, axis=-1, keepdims=True) - xhat * jnp.mean(dxh * xhat, axis=-1, keepdims=True))


def _sigmoid(x):
    return 1.0 / (1.0 + jnp.exp(-x))


def _iota(shape, dim):
    return lax.broadcasted_iota(jnp.int32, shape, dim)


def _hbm(*arrays):
    return tuple(pltpu.with_memory_space_constraint(a, pltpu.HBM) for a in arrays)


def _ln_in_fwd(x, meta_ext, g, b):
    nb = SEQ // BLK

    def body(x_ref, m_ref, g_ref, b_ref, h_ref):
        i = pl.program_id(0)
        xin = jnp.where(i < nb, x_ref[...], m_ref[...])
        xhat, _ = _ln_stats(xin)
        h_ref[...] = xhat * g_ref[...] + b_ref[...]

    return pl.pallas_call(
        body, name="ln_in_fwd", grid=(nb + 1,),
        in_specs=[pl.BlockSpec((BLK, D), lambda i: (jnp.minimum(i, nb - 1), 0)),
                  _const((BLK, D)), _const((1, D)), _const((1, D))],
        out_specs=_rows(BLK, D),
        out_shape=pltpu.HBM((_lp(), D), F32),
        compiler_params=_params(16, dimension_semantics=_seq()),
    )(*_hbm(x, meta_ext, g, b))


def _in_proj(h0, w_in_t, b_in_p, wg2_p, bg2, token):
    tm = _row_tile(384)
    lp = _lp()
    widths = (512, 128, 128, 256, 256, 512, 512, 128)
    offs = (O_QS, O_KS, O_VS, O_QG, O_KG, O_VG, O_RG, O_LR)

    def body(h_ref, w_ref, b_ref, wg2_ref, bg2_ref, token_ref, *outs):
        proj = _dot_nt(h_ref[...], w_ref[...]) + b_ref[...]
        for pos, h in enumerate(HEAD_POS):
            outs[0][:, pos * DH:(pos + 1) * DH] = proj[:, O_QS + h * DH:O_QS + (h + 1) * DH]
        for o_ref, off, wd in zip(outs[1:8], offs[1:], widths[1:]):
            o_ref[...] = proj[:, off:off + wd]
        outs[8][...] = _dot(proj[:, O_LR:O_LR + LANE], wg2_ref[...]) + bg2_ref[...]

    return pl.pallas_call(
        body, name="in_proj", grid=(lp // tm,),
        in_specs=[_rows(tm, D), _const((D_IN_P, D)), _const((1, D_IN_P)), _const((LANE, 256)), _const((1, 256)),
                  _const(TOKEN)],
        out_specs=[_rows(tm, w) for w in widths] + [_rows(tm, 256)],
        out_shape=[pltpu.HBM((lp, w), F32) for w in widths] + [pltpu.HBM((lp, 256), F32)],
        compiler_params=_params(40, dimension_semantics=_seq()),
    )(*_hbm(h0, w_in_t, b_in_p, wg2_p, bg2), token)


def _swa_masks(n):
    nb = SEQ // BLK
    is_meta = n == nb
    ri = _iota((BLK, BLK), 0)
    cj = _iota((BLK, BLK), 1)
    meta_col = ((cj >= META_OFF) & (cj < CH)).astype(jnp.int32)
    meta_q = meta_col * ((cj <= ri) & (ri < CH)).astype(jnp.int32)
    valid_m = jnp.where(is_meta, meta_q, meta_col) > 0
    dist_m = jnp.where(is_meta, ri - cj, n * BLK + ri + CH - cj).astype(F32)
    valid_p = jnp.where((n >= 1) & (n < nb), (cj > ri).astype(jnp.int32), 0) > 0
    dist_p = (ri + BLK - cj).astype(F32)
    valid_c = jnp.where(n < nb, (cj <= ri).astype(jnp.int32), 0) > 0
    dist_c = (ri - cj).astype(F32)
    return (dist_m, dist_p, dist_c), (valid_m, valid_p, valid_c)


def _swa_bias(n):
    dists, valids = _swa_masks(n)
    return (jnp.concatenate([-d for d in dists], axis=1),
            jnp.concatenate([jnp.where(v, 0.0, NEG) for v in valids], axis=1))


def _swa_half(ref, pos, scale=1.0):
    col = ref[:, (pos // 2) * LANE:(pos // 2 + 1) * LANE]
    lane = _iota((BLK, LANE), 1)
    mine = lane < DH if pos % 2 == 0 else lane >= DH
    return jnp.where(mine, col * scale, 0.0).astype(BF16)


def _swa_merge(even, odd):
    return jnp.where(_iota((BLK, LANE), 1) < DH, even, odd)


def _swa_softmax(t, sink):
    m = jnp.maximum(jnp.max(t, axis=-1, keepdims=True), sink)
    e = jnp.exp(t - m)
    e_sink = jnp.exp(sink - m)
    inv = 1.0 / (jnp.sum(e, axis=-1, keepdims=True) + e_sink)
    return e * inv, e_sink * inv


def _swa_kv_specs(width):
    nb = SEQ // BLK
    return [pl.BlockSpec((BLK, width), lambda n: (nb, 0)),
            pl.BlockSpec((BLK, width), lambda n: (jnp.clip(n - 1, 0, nb - 1), 0)),
            pl.BlockSpec((BLK, width), lambda n: (jnp.minimum(n, nb), 0))]


def _swa_fwd(sinks, qs, ks, vs):
    nb = SEQ // BLK
    heads = range(SWA_HEADS)

    def body(sink_ref, q_ref, km_ref, kp_ref, kc_ref, vm_ref, vp_ref, vc_ref, o_ref):
        negdist, maskbias = _swa_bias(pl.program_id(0))
        k_all = jnp.concatenate([km_ref[...], kp_ref[...], kc_ref[...]], axis=0).astype(BF16)
        v_all = jnp.concatenate([vm_ref[...], vp_ref[...], vc_ref[...]], axis=0).astype(BF16)
        q = [_swa_half(q_ref, pos, DH ** -0.5) for pos in heads]
        t = [_dot_nt(q[pos], k_all) + (2.0 ** -(HEAD_POS[pos] + 1) * negdist + maskbias) for pos in heads]
        p = [_swa_softmax(t[pos], sink_ref[HEAD_POS[pos]])[0].astype(BF16) for pos in heads]
        o = [_dot(p[pos], v_all) for pos in heads]
        for col in range(SWA_HEADS // 2):
            o_ref[:, col * LANE:(col + 1) * LANE] = _swa_merge(o[2 * col], o[2 * col + 1])

    kvw = SWA_KV_HEADS * DH
    return pl.pallas_call(
        body, name="swa_fwd", grid=(nb + 1,),
        in_specs=[pl.BlockSpec(memory_space=pltpu.SMEM), _rows(BLK, SWA_HEADS * DH)] + _swa_kv_specs(kvw) + _swa_kv_specs(kvw),
        out_specs=_rows(BLK, SWA_HEADS * DH),
        out_shape=pltpu.HBM((_lp(), SWA_HEADS * DH), F32),
        compiler_params=_params(16, dimension_semantics=_seq()),
    )(sinks, *_hbm(qs, ks, ks, ks, vs, vs, vs))


def _gla_block(t):
    nc = SEQ // CH
    return jnp.where(t == 0, nc, jnp.where(t == nc + 1, nc + 1, t - 1))


def _gla_rowmask(t):
    nc = SEQ // CH
    ri = _iota((CH, 1), 0)
    m = jnp.where(t == 0, (ri >= META_OFF).astype(jnp.int32), jnp.where(t == nc + 1, 0, 1))
    return (m > 0).astype(F32) + jnp.zeros((CH, 1), F32)


def _gla_decay(z, rmask):
    log_g = (jnp.minimum(z, 0.0) - jnp.log1p(jnp.exp(-jnp.abs(z)))) * (rmask / GLA_TAU)
    tril = (_iota((CH, CH), 0) >= _iota((CH, CH), 1)).astype(F32)
    return _dot_exact(tril, log_g), jnp.sum(log_g, axis=0, keepdims=True)


def _gla_fwd(qg, kg, vg, z):
    nc = SEQ // CH
    steps = nc + 2
    kw, vw = GLA_HEADS * DK, GLA_HEADS * DV

    def body(q_ref, k_ref, v_ref, z_ref, o_ref, st_ref, st):
        t = pl.program_id(0)

        @pl.when(t == 0)
        def _():
            st[...] = jnp.zeros_like(st)

        st_prev = st[...]
        st_ref[0] = st_prev
        rmask = _gla_rowmask(t)
        b, b_last = _gla_decay(z_ref[...], rmask)
        q = q_ref[...] * (rmask * DK ** -0.5)
        k = k_ref[...] * rmask
        v = v_ref[...] * rmask
        qe = q * jnp.exp(b)
        ke = k * jnp.exp(-b)
        kd = k * jnp.exp(b_last - b)
        e_last = jnp.exp(b_last)
        causal = _iota((CH, CH), 0) >= _iota((CH, CH), 1)
        for h in range(GLA_HEADS):
            ks, vs_ = slice(h * DK, (h + 1) * DK), slice(h * DV, (h + 1) * DV)
            a = jnp.where(causal, _dot_nt(qe[:, ks], ke[:, ks]), 0.0)
            o_ref[:, vs_] = _dot(a, v[:, vs_]) + _dot_nt(qe[:, ks], st_prev[:, ks])
            st[:, ks] = st_prev[:, ks] * e_last[:, ks] + _dot_tn(v[:, vs_], kd[:, ks])

    blk = lambda w: pl.BlockSpec((CH, w), lambda t: (_gla_block(t), 0))
    return pl.pallas_call(
        body, name="gla_fwd", grid=(steps,),
        in_specs=[blk(kw), blk(kw), blk(vw), blk(kw)],
        out_specs=[blk(vw), pl.BlockSpec((1, DV, kw), lambda t: (t, 0, 0))],
        out_shape=[pltpu.HBM((_lp(), vw), F32), pltpu.HBM((steps, DV, kw), F32)],
        scratch_shapes=[pltpu.VMEM((DV, kw), F32)],
        compiler_params=_params(16, dimension_semantics=_seq()),
    )(*_hbm(qg, kg, vg, z))


def _post_mix(o_s, o_gla, r_g, h0, gn4, w_out, g1, b1):
    tm = _row_tile(384)
    lp = _lp()

    def body(os_ref, og_ref, r_ref, h0_ref, gn_ref, w_ref, g_ref, b_ref, o_ref, pre_ref, h1_ref):
        for pos, h in enumerate(HEAD_POS):
            o_ref[:, h * DH:(h + 1) * DH] = os_ref[:, pos * DH:(pos + 1) * DH].astype(BF16)
        for h in range(GLA_HEADS):
            hs = slice(h * DV, (h + 1) * DV)
            xg = og_ref[:, hs]
            n = xg * lax.rsqrt(jnp.mean(xg * xg, axis=-1, keepdims=True) + RMS_EPS) * gn_ref[...]
            r = r_ref[:, hs]
            o_ref[:, 512 + h * DV:512 + (h + 1) * DV] = (n * (r * _sigmoid(r))).astype(BF16)
        pre = ALPHA * h0_ref[...] + _dot(o_ref[...], w_ref[...])
        pre_ref[...] = pre
        xhat, _ = _ln_stats(pre)
        h1_ref[...] = xhat * g_ref[...] + b_ref[...]

    return pl.pallas_call(
        body, name="post_mix", grid=(lp // tm,),
        in_specs=[_rows(tm, 512), _rows(tm, 512), _rows(tm, 512), _rows(tm, D), _const((1, DV)), _const((D, D)),
                  _const((1, D)), _const((1, D))],
        out_specs=[_rows(tm, D), _rows(tm, D), _rows(tm, D)],
        out_shape=[pltpu.HBM((lp, D), BF16), pltpu.HBM((lp, D), F32),
                   pltpu.HBM((lp, D), F32)],
        compiler_params=_params(32, dimension_semantics=_seq()),
    )(*_hbm(o_s, o_gla, r_g, h0, gn4, w_out, g1, b1))


def _ffn_fwd(h1, wg_t, wu_t, wd):
    tm = _row_tile(192)
    lp = _lp()

    def body(h_ref, wg_ref, wu_ref, wd_ref, g_ref, u_ref, pre_ref):
        h = h_ref[...]
        g = _dot_nt(h, wg_ref[...])
        u = _dot_nt(h, wu_ref[...])
        g_ref[...] = g
        u_ref[...] = u
        pre_ref[...] = ALPHA * h + _dot(g * _sigmoid(g) * u, wd_ref[...])

    return pl.pallas_call(
        body, name="ffn_fwd", grid=(lp // tm,),
        in_specs=[_rows(tm, D), _const((D_FF, D)), _const((D_FF, D)), _const((D_FF, D))],
        out_specs=[_rows(tm, D_FF), _rows(tm, D_FF), _rows(tm, D)],
        out_shape=[pltpu.HBM((lp, D_FF), F32), pltpu.HBM((lp, D_FF), F32),
                   pltpu.HBM((lp, D), F32)],
        compiler_params=_params(48, dimension_semantics=_seq()),
    )(*_hbm(h1, wg_t, wu_t, wd))


def _ln2_loss_bwd(pre2, target, g2, b2):
    nb = SEQ // BLK

    def body(p_ref, t_ref, g_ref, b_ref, dp_ref, loss_ref, dg_ref, db_ref, acc):
        i = pl.program_id(0)

        @pl.when(i == 0)
        def _():
            acc[...] = jnp.zeros_like(acc)
            dg_ref[...] = jnp.zeros_like(dg_ref)
            db_ref[...] = jnp.zeros_like(db_ref)

        real = jnp.where(i < nb, 1.0, 0.0)
        xhat, rstd = _ln_stats(p_ref[...])
        diff = (xhat * g_ref[...] + b_ref[...] - t_ref[...]) * real
        acc[...] += jnp.sum(diff * diff, axis=0, keepdims=True)
        dy = diff * (1.0 / D)
        dp_ref[...] = _ln_bwd(dy, xhat, rstd, g_ref[...])
        dg_ref[...] += jnp.sum(dy * xhat, axis=0, keepdims=True)
        db_ref[...] += jnp.sum(dy, axis=0, keepdims=True)

        @pl.when(i == nb)
        def _():
            loss_ref[...] = jnp.zeros_like(loss_ref) + (0.5 / D) * jnp.sum(acc[...], axis=1, keepdims=True)

    return pl.pallas_call(
        body, name="ln2_loss_bwd", grid=(nb + 1,),
        in_specs=[_rows(BLK, D), pl.BlockSpec((BLK, D), lambda i: (jnp.minimum(i, nb - 1), 0)), _const((1, D)),
                  _const((1, D))],
        out_specs=[_rows(BLK, D), _acc((1, LANE)), _acc((1, D)), _acc((1, D))],
        out_shape=[pltpu.HBM((_lp(), D), F32), pltpu.HBM((1, LANE), F32),
                   pltpu.HBM((1, D), F32), pltpu.HBM((1, D), F32)],
        scratch_shapes=[pltpu.VMEM((1, D), F32)],
        compiler_params=_params(16, dimension_semantics=_seq()),
    )(*_hbm(pre2, target, g2, b2))


def _ffn_bwd(dpre2, g, u, pre1, wg_t, wu_t, wd, g1):
    tm = _row_tile(192)
    lp = _lp()

    def body(dp_ref, g_ref, u_ref, p1_ref, wg_ref, wu_ref, wd_ref, g1_ref, a_ref, dg_ref, du_ref, dp1_ref,
             dg1_ref, db1_ref):
        @pl.when(pl.program_id(0) == 0)
        def _():
            dg1_ref[...] = jnp.zeros_like(dg1_ref)
            db1_ref[...] = jnp.zeros_like(db1_ref)

        dp = dp_ref[...]
        gg, uu = g_ref[...], u_ref[...]
        sg = _sigmoid(gg)
        silu = gg * sg
        da = _dot_nt(dp, wd_ref[...])
        a_ref[...] = (silu * uu).astype(BF16)
        dgate = (da * uu * (sg * (1.0 + gg * (1.0 - sg)))).astype(BF16)
        dup = (da * silu).astype(BF16)
        dg_ref[...] = dgate
        du_ref[...] = dup
        dh1 = ALPHA * dp + _dot(dgate, wg_ref[...]) + _dot(dup, wu_ref[...])
        xhat, rstd = _ln_stats(p1_ref[...])
        dp1_ref[...] = _ln_bwd(dh1, xhat, rstd, g1_ref[...])
        dg1_ref[...] += jnp.sum(dh1 * xhat, axis=0, keepdims=True)
        db1_ref[...] += jnp.sum(dh1, axis=0, keepdims=True)

    return pl.pallas_call(
        body, name="ffn_bwd", grid=(lp // tm,),
        in_specs=[_rows(tm, D), _rows(tm, D_FF), _rows(tm, D_FF), _rows(tm, D), _const((D_FF, D)), _const((D_FF, D)),
                  _const((D_FF, D)), _const((1, D))],
        out_specs=[_rows(tm, D_FF), _rows(tm, D_FF), _rows(tm, D_FF), _rows(tm, D), _acc((1, D)), _acc((1, D))],
        out_shape=[pltpu.HBM((lp, D_FF), BF16)] * 3
        + [pltpu.HBM((lp, D), F32), pltpu.HBM((1, D), F32), pltpu.HBM((1, D), F32)],
        compiler_params=_params(52, dimension_semantics=_seq()),
    )(*_hbm(dpre2, g, u, pre1, wg_t, wu_t, wd, g1))


def _atb(a, b, name):
    lp = _lp()
    tm = _row_tile(384)
    n, w = a.shape[1], b.shape[1]
    bw = 512 if n * w * 4 > (4 << 20) else w

    def body(a_ref, b_ref, o_ref):
        @pl.when(pl.program_id(1) == 0)
        def _():
            o_ref[...] = jnp.zeros_like(o_ref)

        o_ref[...] += _dot_tn(a_ref[...], b_ref[...])

    return pl.pallas_call(
        body, name=name, grid=(w // bw, lp // tm),
        in_specs=[pl.BlockSpec((tm, n), lambda j, k: (k, 0)), pl.BlockSpec((tm, bw), lambda j, k: (k, j))],
        out_specs=pl.BlockSpec((n, bw), lambda j, k: (0, j)),
        out_shape=pltpu.HBM((n, w), F32),
        compiler_params=_params(48, dimension_semantics=_seq(2)),
    )(*_hbm(a, b))


def _out_bwd(dpre1, w_out, o_gla, r_g, gn4, token):
    tm = _row_tile(384)
    lp = _lp()

    def body(dp_ref, w_ref, og_ref, r_ref, gn_ref, token_ref, dos_ref, dog_ref, dr_ref, dgn_ref):
        @pl.when(pl.program_id(0) == 0)
        def _():
            dgn_ref[...] = jnp.zeros_like(dgn_ref)

        do = _dot_nt(dp_ref[...], w_ref[...])
        for pos, h in enumerate(HEAD_POS):
            dos_ref[:, pos * DH:(pos + 1) * DH] = do[:, h * DH:(h + 1) * DH]
        gn = gn_ref[...]
        for h in range(GLA_HEADS):
            hs = slice(h * DV, (h + 1) * DV)
            xg = og_ref[:, hs]
            rstd = lax.rsqrt(jnp.mean(xg * xg, axis=-1, keepdims=True) + RMS_EPS)
            nx = xg * rstd
            r = r_ref[:, hs]
            sr = _sigmoid(r)
            d_o = do[:, 512 + h * DV:512 + (h + 1) * DV]
            dr_ref[:, hs] = d_o * (nx * gn) * (sr * (1.0 + r * (1.0 - sr)))
            dn = d_o * (r * sr)
            dgn_ref[...] += jnp.sum(dn * nx, axis=0, keepdims=True)
            dnx = dn * gn
            dog_ref[:, hs] = rstd * (dnx - nx * jnp.mean(dnx * nx, axis=-1, keepdims=True))

    return pl.pallas_call(
        body, name="out_bwd", grid=(lp // tm,),
        in_specs=[_rows(tm, D), _const((D, D)), _rows(tm, 512), _rows(tm, 512), _const((1, DV)), _const(TOKEN)],
        out_specs=[_rows(tm, 512), _rows(tm, 512), _rows(tm, 512), _acc((1, DV))],
        out_shape=[pltpu.HBM((lp, 512), F32)] * 3 + [pltpu.HBM((1, DV), F32)],
        compiler_params=_params(32, dimension_semantics=_seq()),
    )(*_hbm(dpre1, w_out, o_gla, r_g, gn4), token)


def _gla_bwd(qg, kg, vg, z, do_gla, st_all):
    nc = SEQ // CH
    steps = nc + 2
    kw, vw = GLA_HEADS * DK, GLA_HEADS * DV

    def body(q_ref, k_ref, v_ref, z_ref, do_ref, st_ref, dq_ref, dk_ref, dv_ref, dz_ref, dst):
        t = steps - 1 - pl.program_id(0)

        @pl.when(pl.program_id(0) == 0)
        def _():
            dst[...] = jnp.zeros_like(dst)

        rmask = _gla_rowmask(t)
        zz = z_ref[...]
        b, b_last = _gla_decay(zz, rmask)
        e_b, e_nb, e_kd, e_last = jnp.exp(b), jnp.exp(-b), jnp.exp(b_last - b), jnp.exp(b_last)
        q = q_ref[...] * (rmask * DK ** -0.5)
        k = k_ref[...] * rmask
        v = v_ref[...] * rmask
        qe, ke, kd = q * e_b, k * e_nb, k * e_kd
        st_prev = st_ref[0]
        dst_new = dst[...]
        causal = _iota((CH, CH), 0) >= _iota((CH, CH), 1)
        dqe_parts, dke_parts, dkd_parts = [], [], []
        for h in range(GLA_HEADS):
            ks, vs_ = slice(h * DK, (h + 1) * DK), slice(h * DV, (h + 1) * DV)
            d_o = do_ref[:, vs_]
            a = jnp.where(causal, _dot_nt(qe[:, ks], ke[:, ks]), 0.0)
            da = jnp.where(causal, _dot_nt(d_o, v[:, vs_]), 0.0)
            dqe_parts.append(_dot(d_o, st_prev[:, ks]) + _dot(da, ke[:, ks]))
            dke_parts.append(_dot_tn(da, qe[:, ks]))
            dkd_parts.append(_dot(v[:, vs_], dst_new[:, ks]))
            dv_ref[:, vs_] = _dot_tn(a, d_o) + _dot_nt(kd[:, ks], dst_new[:, ks])
            dst[:, ks] = dst_new[:, ks] * e_last[:, ks] + _dot_tn(d_o, qe[:, ks])
        dqe = jnp.concatenate(dqe_parts, axis=1)
        dke = jnp.concatenate(dke_parts, axis=1)
        dkd = jnp.concatenate(dkd_parts, axis=1)
        dq_ref[...] = dqe * e_b * (rmask * DK ** -0.5)
        dk_ref[...] = (dke * e_nb + dkd * e_kd) * rmask
        dkd_kd = dkd * kd
        db = dqe * qe - dke * ke - dkd_kd
        db_last = jnp.sum(dst_new * st_prev, axis=0, keepdims=True) * e_last + jnp.sum(dkd_kd, axis=0, keepdims=True)
        triu = (_iota((CH, CH), 0) <= _iota((CH, CH), 1)).astype(F32)
        dlog_g = _dot_exact(triu, db) + db_last
        dz_ref[...] = dlog_g * (rmask / GLA_TAU) * _sigmoid(-zz)

    blk = lambda w: pl.BlockSpec((CH, w), lambda s: (_gla_block(steps - 1 - s), 0))
    return pl.pallas_call(
        body, name="gla_bwd", grid=(steps,),
        in_specs=[blk(kw), blk(kw), blk(vw), blk(kw), blk(vw), pl.BlockSpec((1, DV, kw), lambda s: (steps - 1 - s, 0, 0))],
        out_specs=[blk(kw), blk(kw), blk(vw), blk(kw)],
        out_shape=[pltpu.HBM((_lp(), kw), F32), pltpu.HBM((_lp(), kw), F32),
                   pltpu.HBM((_lp(), vw), F32), pltpu.HBM((_lp(), kw), F32)],
        scratch_shapes=[pltpu.VMEM((DV, kw), F32)],
        compiler_params=_params(16, dimension_semantics=_seq()),
    )(*_hbm(qg, kg, vg, z, do_gla, st_all))


def _swa_bwd(sinks, qs, ks, vs, do_s):
    nb = SEQ // BLK
    kvw = SWA_KV_HEADS * DH
    scale = DH ** -0.5
    heads = range(SWA_HEADS)

    def body(sink_ref, q_ref, km_ref, kp_ref, kc_ref, vm_ref, vp_ref, vc_ref, do_ref,
             dq_ref, dk_ref, dv_ref, dsink_ref, carry_k, carry_v, meta_k, meta_v):
        n = pl.program_id(0)

        @pl.when(n == 0)
        def _():
            for r in (carry_k, carry_v, meta_k, meta_v):
                r[...] = jnp.zeros_like(r)
            dsink_ref[...] = jnp.zeros_like(dsink_ref)

        @pl.when(n <= nb)
        def _():
            negdist, maskbias = _swa_bias(n)
            lane = _iota((1, LANE), 1)
            k_all = jnp.concatenate([km_ref[...], kp_ref[...], kc_ref[...]], axis=0).astype(BF16)
            v_all = jnp.concatenate([vm_ref[...], vp_ref[...], vc_ref[...]], axis=0).astype(BF16)
            q = [_swa_half(q_ref, pos, scale) for pos in heads]
            d_o = [_swa_half(do_ref, pos) for pos in heads]
            t = [_dot_nt(q[pos], k_all) + (2.0 ** -(HEAD_POS[pos] + 1) * negdist + maskbias) for pos in heads]
            dp = [_dot_nt(d_o[pos], v_all) for pos in heads]
            soft = [_swa_softmax(t[pos], sink_ref[HEAD_POS[pos]]) for pos in heads]
            p = [s[0] for s in soft]
            delta = [jnp.sum(p[pos] * dp[pos], axis=-1, keepdims=True) for pos in heads]
            ds = [(p[pos] * (dp[pos] - delta[pos])).astype(BF16) for pos in heads]
            dq = [_dot(ds[pos], k_all) for pos in heads]
            for col in range(SWA_HEADS // 2):
                dq_ref[:, col * LANE:(col + 1) * LANE] = scale * _swa_merge(dq[2 * col], dq[2 * col + 1])
            dsink = jnp.zeros((1, LANE), F32)
            for pos in heads:
                dsink = dsink + jnp.where(lane == HEAD_POS[pos],
                                          -jnp.sum(soft[pos][1] * delta[pos], axis=0, keepdims=True), 0.0)
            dsink_ref[...] += dsink
            dk3 = _dot_tn(jnp.concatenate(q, axis=0), jnp.concatenate(ds, axis=0)).T
            dv3 = _dot_tn(jnp.concatenate(d_o, axis=0), jnp.concatenate([x.astype(BF16) for x in p], axis=0)).T
            meta_k[...] += dk3[0:BLK]
            meta_v[...] += dv3[0:BLK]
            dk_ref[...] = carry_k[...] + dk3[BLK:2 * BLK]
            dv_ref[...] = carry_v[...] + dv3[BLK:2 * BLK]
            carry_k[...] = dk3[2 * BLK:3 * BLK]
            carry_v[...] = dv3[2 * BLK:3 * BLK]

        @pl.when(n == nb + 1)
        def _():
            dk_ref[...] = meta_k[...]
            dv_ref[...] = meta_v[...]

    kv_out = pl.BlockSpec((BLK, kvw), lambda n: (jnp.where(n == nb + 1, nb, jnp.clip(n - 1, 0, nb - 1)), 0))
    qblk = pl.BlockSpec((BLK, SWA_HEADS * DH), lambda n: (jnp.minimum(n, nb), 0))
    return pl.pallas_call(
        body, name="swa_bwd", grid=(nb + 2,),
        in_specs=[pl.BlockSpec(memory_space=pltpu.SMEM), qblk] + _swa_kv_specs(kvw) + _swa_kv_specs(kvw) + [qblk],
        out_specs=[qblk, kv_out, kv_out, _acc((1, LANE))],
        out_shape=[pltpu.HBM((_lp(), SWA_HEADS * DH), F32), pltpu.HBM((_lp(), kvw), F32),
                   pltpu.HBM((_lp(), kvw), F32), pltpu.HBM((1, LANE), F32)],
        scratch_shapes=[pltpu.VMEM((BLK, kvw), F32)] * 4,
        compiler_params=_params(16, dimension_semantics=_seq()),
    )(sinks, *_hbm(qs, ks, ks, ks, vs, vs, vs, do_s))


def _in_bwd(dqs, dks, dvs, dqg, dkg, dvg, drg, dz, dpre1, w_in_t, wg2_p):
    tm = _row_tile(384)
    lp = _lp()
    widths = (512, 128, 128, 256, 256, 512, 512)
    offs = (O_QS, O_KS, O_VS, O_QG, O_KG, O_VG, O_RG)

    def body(*refs):
        parts, (dz_ref, dp1_ref, w_ref, wg2_ref, dproj_ref, dh0_ref, dbin_ref, dbg_ref) = refs[:7], refs[7:]

        @pl.when(pl.program_id(0) == 0)
        def _():
            dbin_ref[...] = jnp.zeros_like(dbin_ref)
            dbg_ref[...] = jnp.zeros_like(dbg_ref)

        for pos, h in enumerate(HEAD_POS):
            val = parts[0][:, pos * DH:(pos + 1) * DH]
            dproj_ref[:, O_QS + h * DH:O_QS + (h + 1) * DH] = val.astype(BF16)
            dbin_ref[:, O_QS + h * DH:O_QS + (h + 1) * DH] += jnp.sum(val, axis=0, keepdims=True)
        for p_ref, off, wd in zip(parts[1:], offs[1:], widths[1:]):
            val = p_ref[...]
            dproj_ref[:, off:off + wd] = val.astype(BF16)
            dbin_ref[:, off:off + wd] += jnp.sum(val, axis=0, keepdims=True)
        dz = dz_ref[...]
        dlr = _dot_nt(dz, wg2_ref[...])
        dproj_ref[:, O_LR:O_LR + LANE] = dlr.astype(BF16)
        dbin_ref[:, O_LR:O_LR + LANE] += jnp.sum(dlr, axis=0, keepdims=True)
        dbg_ref[...] += jnp.sum(dz, axis=0, keepdims=True)
        dh0_ref[...] = ALPHA * dp1_ref[...] + _dot(dproj_ref[...], w_ref[...])

    return pl.pallas_call(
        body, name="in_bwd", grid=(lp // tm,),
        in_specs=[_rows(tm, w) for w in widths] + [_rows(tm, 256), _rows(tm, D), _const((D_IN_P, D)), _const((LANE, 256))],
        out_specs=[_rows(tm, D_IN_P), _rows(tm, D), _acc((1, D_IN_P)), _acc((1, 256))],
        out_shape=[pltpu.HBM((lp, D_IN_P), BF16), pltpu.HBM((lp, D), F32),
                   pltpu.HBM((1, D_IN_P), F32), pltpu.HBM((1, 256), F32)],
        compiler_params=_params(40, dimension_semantics=_seq()),
    )(*_hbm(dqs, dks, dvs, dqg, dkg, dvg, drg, dz, dpre1, w_in_t, wg2_p))


def _ln_in_bwd(x, meta_ext, dh0, g):
    nb = SEQ // BLK

    def body(x_ref, m_ref, dh_ref, g_ref, dx_ref, dm_ref, dg_ref, db_ref):
        i = pl.program_id(0)

        @pl.when(i == 0)
        def _():
            dg_ref[...] = jnp.zeros_like(dg_ref)
            db_ref[...] = jnp.zeros_like(db_ref)

        xin = jnp.where(i < nb, x_ref[...], m_ref[...])
        xhat, rstd = _ln_stats(xin)
        dh = dh_ref[...]
        dxin = _ln_bwd(dh, xhat, rstd, g_ref[...])
        dg_ref[...] += jnp.sum(dh * xhat, axis=0, keepdims=True)
        db_ref[...] += jnp.sum(dh, axis=0, keepdims=True)

        @pl.when(i < nb)
        def _():
            dx_ref[...] = dxin

        @pl.when(i == nb)
        def _():
            dm_ref[...] = dxin

    xblk = pl.BlockSpec((BLK, D), lambda i: (jnp.minimum(i, nb - 1), 0))
    return pl.pallas_call(
        body, name="ln_in_bwd", grid=(nb + 1,),
        in_specs=[xblk, _const((BLK, D)), _rows(BLK, D), _const((1, D))],
        out_specs=[xblk, _acc((BLK, D)), _acc((1, D)), _acc((1, D))],
        out_shape=[pltpu.HBM((SEQ, D), F32), pltpu.HBM((BLK, D), F32),
                   pltpu.HBM((1, D), F32), pltpu.HBM((1, D), F32)],
        compiler_params=_params(16, dimension_semantics=_seq()),
    )(*_hbm(x, meta_ext, dh0, g))


def _local_step(x, target, meta_full, ln_in_g, ln_in_b, w_in_t, b_in, wg2, bg2, sinks, gn, g1, b1, g2, b2,
                token, fetch_rest, ship_ffn):
    row = lambda v: v.reshape(1, -1).astype(F32)
    meta_ext = jnp.pad(meta_full, ((META_OFF, BLK - CH), (0, 0)))
    b_in_p = jnp.pad(row(b_in), ((0, 0), (0, D_IN_P - D_IN)))
    wg2_p = jnp.pad(wg2, ((0, LANE - wg2.shape[0]), (0, 0))).astype(BF16)
    gn4 = row(gn)
    sinks = sinks.reshape(-1).astype(F32)

    h0 = _ln_in_fwd(x, meta_ext, row(ln_in_g), row(ln_in_b))
    qs, ks, vs, qg, kg, vg, rg, glr, z = _in_proj(h0, w_in_t, b_in_p, wg2_p, row(bg2), token)
    o_s = _swa_fwd(sinks, qs, ks, vs)
    o_gla, st_all = _gla_fwd(qg, kg, vg, z)
    w_out, wg_t, wu_t, wd = fetch_rest([o_s, o_gla])
    o, pre1, h1 = _post_mix(o_s, o_gla, rg, h0, gn4, w_out, row(g1), row(b1))
    g, u, pre2 = _ffn_fwd(h1, wg_t, wu_t, wd)

    dpre2, loss, dg2, db2 = _ln2_loss_bwd(pre2, target, row(g2), row(b2))
    a, dgate, dup, dpre1, dg1, db1 = _ffn_bwd(dpre2, g, u, pre1, wg_t, wu_t, wd, row(g1))
    dwd = _atb(a, dpre2, "dw_down")
    dwg_t = _atb(dgate, h1, "dw_gate")
    dwu_t = _atb(dup, h1, "dw_up")
    dw_out = _atb(o, dpre1, "dw_out")
    token = ship_ffn(dict(w_out=dw_out, w_g=dwg_t, w_u=dwu_t, w_d=dwd))
    do_s, do_gla, drg, dgn = _out_bwd(dpre1, w_out, o_gla, rg, gn4, token)
    dqg, dkg, dvg, dz = _gla_bwd(qg, kg, vg, z, do_gla, st_all)
    dqs, dks, dvs, dsinks = _swa_bwd(sinks, qs, ks, vs, do_s)
    dproj, dh0, db_in_p, dbg2 = _in_bwd(dqs, dks, dvs, dqg, dkg, dvg, drg, dz, dpre1, w_in_t, wg2_p)
    dw_in_t = _atb(dproj, h0, "dw_in")
    dwg2_p = _atb(glr, dz, "dw_gate_lr2")
    dx, dmeta_blk, dg_in, db_in_ln = _ln_in_bwd(x, meta_ext, dh0, row(ln_in_g))

    grads = dict(
        w_in=dw_in_t,
        meta=dmeta_blk[META_OFF:CH], ln_in_g=dg_in, ln_in_b=db_in_ln, ln1_g=dg1, ln1_b=db1, ln2_g=dg2, ln2_b=db2,
        b_in=db_in_p[:, :D_IN], wg2=dwg2_p[:wg2.shape[0]], bg2=dbg2, sinks=dsinks[:, :SWA_HEADS], gn=dgn)
    return loss[0, 0], dx, grads


HBM = pl.BlockSpec(memory_space=pltpu.HBM)


def _place():
    return lax.axis_index("x"), lax.axis_index("y"), lax.axis_index("c")


def _other_chips(x, y):
    return [(1 - x, y), (x, 1 - y), (1 - x, 1 - y)]


def _dma_sems(n):
    return pltpu.SemaphoreType.DMA((n,))


def _comm_params():
    return pltpu.CompilerParams(has_side_effects=True)


def _gather_halves(shards):
    n = len(shards)

    def body(*refs):
        ins, outs = refs[:n], refs[n:2 * n]
        ici_send, ici_recv, d2d_send, d2d_recv = refs[2 * n:]
        x, y, c = _place()
        mine = 2 * x + y
        chips = _other_chips(x, y)

        def ici(a, j, src_chip):
            px, py = chips[j]
            return pltpu.make_async_remote_copy(ins[a].at[c], outs[a].at[src_chip, c], ici_send.at[3 * a + j],
                                                ici_recv.at[3 * a + j], device_id=(px, py, c), device_id_type=MESH)

        def d2d(a, j, half):
            px, py = chips[j]
            blk = outs[a].at[2 * px + py, half]
            return pltpu.make_async_remote_copy(blk, blk, d2d_send.at[3 * a + j], d2d_recv.at[3 * a + j],
                                                device_id=(x, y, 1 - c), device_id_type=MESH)

        sends = [ici(a, j, mine) for a in range(n) for j in range(3)]
        for cp in sends:
            cp.start()
        passed = []
        for a in range(n):
            for j, (px, py) in enumerate(chips):
                ici(a, j, 2 * px + py).wait_recv()
                fwd = d2d(a, j, c)
                fwd.start()
                passed.append(fwd)
        for a in range(n):
            for j in range(3):
                d2d(a, j, 1 - c).wait_recv()
        for cp in sends + passed:
            cp.wait_send()

    gathered = pl.pallas_call(
        body, name="gather_halves",
        in_specs=[HBM] * n, out_specs=[HBM] * n,
        out_shape=[pltpu.HBM((N_CHIPS,) + s.shape, s.dtype) for s in shards],
        scratch_shapes=[_dma_sems(3 * n)] * 4,
        compiler_params=_comm_params(),
    )(*_hbm(*shards))
    mine = 2 * lax.axis_index("x") + lax.axis_index("y")
    return [lax.dynamic_update_index_in_dim(g, s, mine, axis=0) for g, s in zip(gathered, shards)]


SEM = pl.BlockSpec(memory_space=pltpu.SEMAPHORE)


def _ici_copies(kind, srcs, lands, send_sems, recv_sems):
    x, y, c = _place()
    mine = 2 * x + y
    to_start, to_wait = [], []
    for a in range(len(srcs)):
        for j, (px, py) in enumerate(_other_chips(x, y)):
            peer = 2 * px + py
            if kind == "gather":
                src, there, here = srcs[a].at[c], lands[a].at[mine, c], lands[a].at[peer, c]
            else:
                src, there, here = srcs[a].at[peer], lands[a].at[mine], lands[a].at[peer]
            for dst, out in ((there, to_start), (here, to_wait)):
                out.append(pltpu.make_async_remote_copy(src, dst, send_sems.at[3 * a + j], recv_sems.at[3 * a + j],
                                                        device_id=(px, py, c), device_id_type=MESH))
    return to_start, to_wait


def _split_params():
    return pltpu.CompilerParams(has_side_effects=pltpu.SideEffectType.DATAFLOW_SIDE_EFFECTING)


def _ici_start(kind, srcs, land_shapes, name):
    n = len(srcs)
    lands = [pltpu.with_memory_space_constraint(lax.empty(s, a.dtype), pltpu.HBM) for s, a in zip(land_shapes, srcs)]

    def body(*refs):
        to_start, _ = _ici_copies(kind, refs[:n], refs[n:2 * n], refs[2 * n], refs[2 * n + 1])
        for cp in to_start:
            cp.start()
        refs[-1][...] = jnp.zeros(TOKEN, F32)

    outs = pl.pallas_call(
        body, name=name, in_specs=[HBM] * (2 * n),
        out_specs=[SEM, SEM] + [HBM] * (2 * n) + [pl.BlockSpec(memory_space=pltpu.VMEM)],
        out_shape=[_dma_sems(3 * n)] * 2 + [pltpu.HBM(a.shape, a.dtype) for a in list(srcs) + lands]
        + [jax.ShapeDtypeStruct(TOKEN, F32)],
        input_output_aliases={i: 2 + i for i in range(2 * n)},
        compiler_params=_split_params(),
    )(*_hbm(*srcs), *lands)
    return outs[:-1], outs[-1]


def _ici_wait(kind, handle, after, name):
    n = (len(handle) - 2) // 2

    def body(*refs):
        _, to_wait = _ici_copies(kind, refs[:n], refs[n:2 * n], refs[2 * n], refs[2 * n + 1])
        for cp in to_wait:
            cp.wait_send()
            cp.wait_recv()

    outs = pl.pallas_call(
        body, name=name, in_specs=[HBM] * (2 * n) + [SEM, SEM] + [pl.BlockSpec(memory_space=pl.ANY)] * len(after),
        out_specs=[HBM] * (2 * n), out_shape=[pltpu.HBM(a.shape, a.dtype) for a in handle[2:]],
        input_output_aliases={i: i for i in range(2 * n)},
        compiler_params=_split_params(),
    )(*handle[2:], handle[0], handle[1], *after)
    return list(outs[n:])


def _sibling_forward(lands):
    n = len(lands)

    def body(*refs):
        outs = refs[n:2 * n]
        send_sems, recv_sems = refs[2 * n:]
        x, y, c = _place()

        def copy(a, j, half):
            px, py = _other_chips(x, y)[j]
            blk = outs[a].at[2 * px + py, half]
            return pltpu.make_async_remote_copy(blk, blk, send_sems.at[3 * a + j], recv_sems.at[3 * a + j],
                                                device_id=(x, y, 1 - c), device_id_type=MESH)

        pairs = [(a, j) for a in range(n) for j in range(3)]
        for a, j in pairs:
            copy(a, j, c).start()
        for a, j in pairs:
            copy(a, j, 1 - c).wait_recv()
        for a, j in pairs:
            copy(a, j, c).wait_send()

    return pl.pallas_call(
        body, name="sibling_forward", in_specs=[HBM] * n, out_specs=[HBM] * n,
        out_shape=[pltpu.HBM(a.shape, a.dtype) for a in lands],
        input_output_aliases={a: a for a in range(n)},
        scratch_shapes=[_dma_sems(3 * n)] * 2,
        compiler_params=_comm_params(),
    )(*_hbm(*lands))


def _sibling_exchange(grads):
    n = len(grads)

    def body(*refs):
        ins, outs = refs[:n], refs[n:2 * n]
        send_sems, recv_sems = refs[2 * n:]
        x, y, c = _place()
        copies = []
        for a in range(n):
            for s in range(N_CHIPS):
                cp = pltpu.make_async_remote_copy(ins[a].at[s, 1 - c], outs[a].at[s], send_sems.at[N_CHIPS * a + s],
                                                  recv_sems.at[N_CHIPS * a + s], device_id=(x, y, 1 - c),
                                                  device_id_type=MESH)
                cp.start()
                copies.append(cp)
        for cp in copies:
            cp.wait_recv()
        for cp in copies:
            cp.wait_send()

    return pl.pallas_call(
        body, name="sibling_exchange", in_specs=[HBM] * n, out_specs=[HBM] * n,
        out_shape=[pltpu.HBM((N_CHIPS, g.shape[2], D), F32) for g in grads],
        scratch_shapes=[_dma_sems(N_CHIPS * n)] * 2,
        compiler_params=_comm_params(),
    )(*_hbm(*grads))


def _add_halves(core, grad, recv, dtype, name):
    h = grad.shape[2]

    def body(c_ref, a_ref, b_ref, o_ref):
        o_ref[...] = (a_ref[0] + b_ref[...]).astype(dtype)

    return pl.pallas_call(
        body, name=name,
        grid_spec=pltpu.PrefetchScalarGridSpec(
            num_scalar_prefetch=1, grid=(N_CHIPS,),
            in_specs=[pl.BlockSpec((1, 1, h, D), lambda s, c: (s, c[0], 0, 0)),
                      pl.BlockSpec((1, h, D), lambda s, c: (s, 0, 0))],
            out_specs=pl.BlockSpec((1, h, D), lambda s, c: (s, 0, 0))),
        out_shape=pltpu.HBM((N_CHIPS, h, D), dtype),
        compiler_params=_params(16, dimension_semantics=_seq()),
    )(core, *_hbm(grad, recv))


def _chip_scatter(parts, with_own):
    n = len(parts)

    def body(*refs):
        ins, outs = refs[:n], refs[n:2 * n]
        send_sems, recv_sems, local_sems = refs[2 * n:]
        x, y, c = _place()
        mine = 2 * x + y
        chips = _other_chips(x, y)
        local = [pltpu.make_async_copy(ins[a].at[mine], outs[a].at[mine], local_sems.at[a]) for a in range(n)
                 if with_own[a]]
        for cp in local:
            cp.start()
        sends = []
        for a in range(n):
            for j, (px, py) in enumerate(chips):
                cp = pltpu.make_async_remote_copy(ins[a].at[2 * px + py], outs[a].at[mine], send_sems.at[3 * a + j],
                                                  recv_sems.at[3 * a + j], device_id=(px, py, c), device_id_type=MESH)
                cp.start()
                sends.append(cp)
        for a in range(n):
            for j, (px, py) in enumerate(chips):
                pltpu.make_async_remote_copy(ins[a].at[mine], outs[a].at[2 * px + py], send_sems.at[3 * a + j],
                                             recv_sems.at[3 * a + j], device_id=(px, py, c),
                                             device_id_type=MESH).wait_recv()
        for cp in sends:
            cp.wait_send()
        for cp in local:
            cp.wait()

    return pl.pallas_call(
        body, name="chip_scatter", in_specs=[HBM] * n, out_specs=[HBM] * n,
        out_shape=[pltpu.HBM(p.shape, p.dtype) for p in parts],
        scratch_shapes=[_dma_sems(3 * n)] * 2 + [_dma_sems(n)],
        compiler_params=_comm_params(),
    )(*_hbm(*parts))


def _sum_chips(slots, first, rest, name):
    h = first.shape[1]

    def body(i_ref, a_ref, b_ref, c_ref, d_ref, o_ref):
        o_ref[...] = ((a_ref[...].astype(F32) + b_ref[...].astype(F32)) + c_ref[...].astype(F32)) + d_ref[...].astype(F32)

    slab = lambda k: pl.BlockSpec((1, h, D), lambda i, ix: (ix[k], 0, 0))
    return pl.pallas_call(
        body, name=name,
        grid_spec=pltpu.PrefetchScalarGridSpec(num_scalar_prefetch=1, grid=(1,),
                                               in_specs=[slab(0), slab(1), slab(2), slab(3)], out_specs=slab(4)),
        out_shape=pltpu.HBM((2, h, D), F32),
        compiler_params=_params(16, dimension_semantics=_seq()),
    )(slots, *_hbm(first, rest, rest, rest))


def _join_halves(halves):
    n = len(halves)

    def body(*refs):
        outs = refs[n:2 * n]
        send_sems, recv_sems = refs[2 * n:]
        x, y, c = _place()

        def copy(a, slab):
            return pltpu.make_async_remote_copy(outs[a].at[slab], outs[a].at[slab], send_sems.at[a], recv_sems.at[a],
                                                device_id=(x, y, 1 - c), device_id_type=MESH)

        for a in range(n):
            copy(a, c).start()
        for a in range(n):
            copy(a, 1 - c).wait_recv()
        for a in range(n):
            copy(a, c).wait_send()

    return pl.pallas_call(
        body, name="join_halves", in_specs=[HBM] * n, out_specs=[HBM] * n,
        out_shape=[pltpu.HBM(h.shape, F32) for h in halves],
        input_output_aliases={a: a for a in range(n)},
        scratch_shapes=[_dma_sems(n)] * 2,
        compiler_params=_comm_params(),
    )(*_hbm(*halves))


def _chip_partials(grads, wire_dtypes, names):
    core = lax.axis_index("c").astype(jnp.int32).reshape(1)
    recv = _sibling_exchange(grads)
    return [_add_halves(core, g, r, dt, "add_halves_" + nm) for g, r, dt, nm in zip(grads, recv, wire_dtypes, names)]


def _finish_reduce(parts, got, same_order, names):
    x, y, c = _place()
    others = [2 * px + py for px, py in _other_chips(x, y)]
    own_first = jnp.stack([2 * x + y] + others + [c]).astype(jnp.int32)
    chip_order = jnp.stack([0 * c, 0 * c + 1, 0 * c + 2, 0 * c + 3, c]).astype(jnp.int32)
    halves = [_sum_chips(chip_order, q, q, "sum_chips_" + nm) if fixed else _sum_chips(own_first, p, q, "sum_chips_" + nm)
              for p, q, fixed, nm in zip(parts, got, same_order, names)]
    return [f.reshape(2 * f.shape[1], D) for f in _join_halves(halves)]


def _adamw(w, g, m, v, name):
    rows, cols = w.shape
    if rows % 8 == 0:
        tr = max(t for t in range(8, 257, 8) if rows % t == 0)
        grid, blk = (rows // tr,), pl.BlockSpec((tr, cols), lambda i: (i, 0))
    else:
        grid, blk = (cols // 256,), pl.BlockSpec((rows, 256), lambda i: (0, i))

    def body(w_ref, g_ref, m_ref, v_ref, d_ref, nm_ref, nv_ref):
        gg = g_ref[...]
        nm = ADAM_B1 * m_ref[...] + (1.0 - ADAM_B1) * gg
        nv = ADAM_B2 * v_ref[...] + (1.0 - ADAM_B2) * (gg * gg)
        m_hat = nm / (1.0 - ADAM_B1 ** ADAM_STEP)
        v_hat = nv / (1.0 - ADAM_B2 ** ADAM_STEP)
        d_ref[...] = -ADAM_LR * (m_hat / (jnp.sqrt(v_hat) + ADAM_EPS) + ADAM_WD * w_ref[...])
        nm_ref[...] = nm
        nv_ref[...] = nv

    return pl.pallas_call(
        body, name=name, grid=grid,
        in_specs=[blk] * 4, out_specs=[blk] * 3,
        out_shape=[pltpu.HBM(w.shape, F32)] * 3,
        compiler_params=_params(32, dimension_semantics=_seq()),
    )(*_hbm(w, g, m, v))


def _flat_rows(v, rows):
    flat = v.reshape(-1).astype(F32)
    return jnp.pad(flat, (0, rows * D - flat.shape[0])).reshape(rows, D)


def _small_pack(gr):
    tail = jnp.concatenate([gr["bg2"].reshape(-1), gr["sinks"].reshape(-1), gr["gn"].reshape(-1)])
    rows = [gr["meta"].reshape(N_META, D)] + [gr[k].reshape(1, D) for k in
                                              ("ln_in_g", "ln_in_b", "ln1_g", "ln1_b", "ln2_g", "ln2_b")]
    rows += [_flat_rows(gr["b_in"], 3), _flat_rows(gr["wg2"], 4), _flat_rows(tail, 1)]
    packed = jnp.concatenate(rows, axis=0)
    return jnp.pad(packed, ((0, SMALL_ROWS - packed.shape[0]), (0, 0)))


def _small_unpack(p):
    flat = lambda r0, n, size: p[r0:r0 + n].reshape(-1)[:size]
    tail = p[29]
    return dict(meta=p[0:N_META], ln_in_g=p[16], ln_in_b=p[17], ln1_g=p[18], ln1_b=p[19], ln2_g=p[20], ln2_b=p[21],
                b_in=flat(22, 3, D_IN), wg2=flat(25, 4, 16 * 256).reshape(16, 256), bg2=tail[0:256],
                sinks=tail[256:256 + SWA_HEADS], gn=tail[256 + SWA_HEADS:256 + SWA_HEADS + DV])


BIG = ("w_in", "w_out", "w_g", "w_u", "w_d")


def kernel(x, meta_tokens, ln_in_g, ln_in_b, w_in, b_in, w_gate_lr2, b_gate_lr2, attn_sinks, gla_norm_g, w_out, ln1_g, ln1_b, w_ffn_gate, w_ffn_up, w_ffn_down, ln2_g, ln2_b, loss_target, m_meta_tokens, m_ln_in_g, m_ln_in_b, m_w_in, m_b_in, m_w_gate_lr2, m_b_gate_lr2, m_attn_sinks, m_gla_norm_g, m_w_out, m_ln1_g, m_ln1_b, m_w_ffn_gate, m_w_ffn_up, m_w_ffn_down, m_ln2_g, m_ln2_b, v_meta_tokens, v_ln_in_g, v_ln_in_b, v_w_in, v_b_in, v_w_gate_lr2, v_b_gate_lr2, v_attn_sinks, v_gla_norm_g, v_w_out, v_ln1_g, v_ln1_b, v_w_ffn_gate, v_w_ffn_up, v_w_ffn_down, v_ln2_g, v_ln2_b):
    chip = 2 * lax.axis_index("x") + lax.axis_index("y")

    halves = lambda a: a.reshape(2, a.shape[0] // 2, a.shape[1])
    r_in = SHARD_ROWS["w_in"]
    first = [jnp.pad(w_in[0].T.astype(BF16), ((0, W_IN_WIN - r_in), (0, 0))), meta_tokens, w_gate_lr2[0]]
    g_in, g_meta, g_wg2 = _gather_halves([halves(a) for a in first])
    rest = [halves(a) for a in (w_out[0].astype(BF16), w_ffn_gate[0].T.astype(BF16), w_ffn_up[0].T.astype(BF16),
                                w_ffn_down[0].astype(BF16))]
    rest_handle, token = _ici_start("gather", rest, [(N_CHIPS,) + a.shape for a in rest], "gather_rest_start")
    w_in_t = jnp.pad(g_in.reshape(N_CHIPS, W_IN_WIN, D)[:, :r_in].reshape(D_IN, D), ((0, D_IN_P - D_IN), (0, 0)))
    meta_full = jnp.concatenate([g_meta[s].reshape(N_META, -1) for s in range(N_CHIPS)], axis=1)
    wg2_full = jnp.concatenate([g_wg2[s].reshape(w_gate_lr2.shape[1], -1) for s in range(N_CHIPS)], axis=1)

    def fetch_rest(after):
        lands = _sibling_forward(_ici_wait("gather", rest_handle, after, "gather_rest_wait"))
        return [lax.dynamic_update_index_in_dim(g, s, chip, axis=0).reshape(-1, D) for g, s in zip(lands, rest)]

    ffn_names = list(BIG[1:])
    sent = {}

    def ship_ffn(g):
        sent["parts"] = _chip_partials([g[k].reshape(N_CHIPS, 2, -1, D) for k in ffn_names], [BF16] * 4, ffn_names)
        sent["handle"], ffn_token = _ici_start("scatter", sent["parts"], [p.shape for p in sent["parts"]],
                                               "scatter_ffn_start")
        return ffn_token

    loss_part, dx, gr = _local_step(
        x[0], loss_target[0], meta_full, ln_in_g, ln_in_b, w_in_t, b_in[0], wg2_full, b_gate_lr2[0], attn_sinks[0],
        gla_norm_g[0], ln1_g[0], ln1_b[0], ln2_g[0], ln2_b[0], token, fetch_rest, ship_ffn)
    loss = lax.psum(loss_part, ("x", "y", "c"))
    ffn_got = _ici_wait("scatter", sent["handle"], [dx], "scatter_ffn_wait")

    win_start = [s * r_in // BF16_ROWS * BF16_ROWS for s in range(N_CHIPS)]
    last = [jnp.stack([gr["w_in"][st:st + W_IN_WIN] for st in win_start]),
            jnp.broadcast_to(_small_pack(gr), (N_CHIPS, SMALL_ROWS, D))]
    last_parts = _chip_partials([a.reshape(N_CHIPS, 2, -1, D) for a in last], [BF16, F32], ["w_in", "small"])
    last_got = _chip_scatter(last_parts, [False, True])
    red = _finish_reduce([last_parts[0]] + sent["parts"] + [last_parts[1]], [last_got[0]] + ffn_got + [last_got[1]],
                         [False] * len(BIG) + [True], list(BIG) + ["small"])

    big_g = dict(zip(BIG, red))
    big_g["w_in"] = lax.dynamic_slice_in_dim(red[0], chip * (r_in % BF16_ROWS), r_in, axis=0)
    sg = _small_unpack(red[-1])
    col = lambda a, width: lax.dynamic_slice_in_dim(a, chip * width, width, axis=1)
    grads = dict(
        meta_tokens=col(sg["meta"], D // N_CHIPS), ln_in_g=sg["ln_in_g"], ln_in_b=sg["ln_in_b"],
        w_in=big_g["w_in"].T[None], b_in=sg["b_in"][None], w_gate_lr2=col(sg["wg2"], 256 // N_CHIPS)[None],
        b_gate_lr2=sg["bg2"][None], attn_sinks=sg["sinks"][None], gla_norm_g=sg["gn"][None],
        w_out=big_g["w_out"][None], ln1_g=sg["ln1_g"][None], ln1_b=sg["ln1_b"][None],
        w_ffn_gate=big_g["w_g"].T[None], w_ffn_up=big_g["w_u"].T[None], w_ffn_down=big_g["w_d"][None],
        ln2_g=sg["ln2_g"][None], ln2_b=sg["ln2_b"][None])
    weights = dict(meta_tokens=meta_tokens, ln_in_g=ln_in_g, ln_in_b=ln_in_b, w_in=w_in, b_in=b_in,
                   w_gate_lr2=w_gate_lr2, b_gate_lr2=b_gate_lr2, attn_sinks=attn_sinks, gla_norm_g=gla_norm_g,
                   w_out=w_out, ln1_g=ln1_g, ln1_b=ln1_b, w_ffn_gate=w_ffn_gate, w_ffn_up=w_ffn_up,
                   w_ffn_down=w_ffn_down, ln2_g=ln2_g, ln2_b=ln2_b)
    m_in = dict(meta_tokens=m_meta_tokens, ln_in_g=m_ln_in_g, ln_in_b=m_ln_in_b, w_in=m_w_in, b_in=m_b_in,
                w_gate_lr2=m_w_gate_lr2, b_gate_lr2=m_b_gate_lr2, attn_sinks=m_attn_sinks, gla_norm_g=m_gla_norm_g,
                w_out=m_w_out, ln1_g=m_ln1_g, ln1_b=m_ln1_b, w_ffn_gate=m_w_ffn_gate, w_ffn_up=m_w_ffn_up,
                w_ffn_down=m_w_ffn_down, ln2_g=m_ln2_g, ln2_b=m_ln2_b)
    v_in = dict(meta_tokens=v_meta_tokens, ln_in_g=v_ln_in_g, ln_in_b=v_ln_in_b, w_in=v_w_in, b_in=v_b_in,
                w_gate_lr2=v_w_gate_lr2, b_gate_lr2=v_b_gate_lr2, attn_sinks=v_attn_sinks, gla_norm_g=v_gla_norm_g,
                w_out=v_w_out, ln1_g=v_ln1_g, ln1_b=v_ln1_b, w_ffn_gate=v_w_ffn_gate, w_ffn_up=v_w_ffn_up,
                w_ffn_down=v_w_ffn_down, ln2_g=v_ln2_g, ln2_b=v_ln2_b)
    names = list(weights)
    big_names = ("w_in", "w_out", "w_ffn_gate", "w_ffn_up", "w_ffn_down")

    delta, new_m, new_v = {}, {}, {}
    for k, kk in zip(big_names, BIG):
        flip = (lambda a: a.T) if kk in ("w_in", "w_g", "w_u") else (lambda a: a)
        d_, m_, v_ = _adamw(flip(weights[k][0]), big_g[kk], flip(m_in[k][0]), flip(v_in[k][0]), "adamw_" + k)
        delta[k], new_m[k], new_v[k] = (flip(t)[None] for t in (d_, m_, v_))
    small_names = [k for k in names if k not in big_names]
    sizes = [weights[k].size for k in small_names]
    rows_small = -(-sum(sizes) // D)
    rows_small += -rows_small % 8
    cat = lambda src: _flat_rows(jnp.concatenate([src[k].reshape(-1) for k in small_names]), rows_small)
    d_, m_, v_ = _adamw(cat(weights), cat(grads), cat(m_in), cat(v_in), "adamw_small")
    off = 0
    for k, n in zip(small_names, sizes):
        for dst, src in ((delta, d_), (new_m, m_), (new_v, v_)):
            dst[k] = src.reshape(-1)[off:off + n].reshape(weights[k].shape)
        off += n
    grads = {k: grads[k].reshape(weights[k].shape) for k in names}

    return (loss, dx[None], *[grads[k] for k in names], *[delta[k] for k in names], *[new_m[k] for k in names],
            *[new_v[k] for k in names])
```

```python
import functools

import jax
import jax.numpy as jnp
from jax import lax
from jax.experimental import pallas as pl
from jax.experimental.pallas import tpu as pltpu

F32 = jnp.float32
BF16 = jnp.bfloat16
MESH = pl.DeviceIdType.MESH

D = 1024
SEQ = 4096
N_META = 16
SWA_HEADS, SWA_KV_HEADS, DH = 8, 2, 64
WINDOW = 128
GLA_HEADS, DK, DV = 4, 64, 128
GLA_TAU = 16.0
CH = 64
D_FF = 2816
D_IN = 2320
LN_EPS = 1e-5
RMS_EPS = 1e-6
ALPHA = 2.0 ** 0.25
NEG = -1e30
ADAM_LR, ADAM_B1, ADAM_B2, ADAM_EPS, ADAM_WD, ADAM_STEP = 0.001, 0.9, 0.999, 1e-8, 0.01, 10
O_QS, O_KS, O_VS, O_QG, O_KG, O_VG, O_RG, O_LR = 0, 512, 640, 768, 1024, 1280, 1792, 2304

LANE = 128
BLK = WINDOW
D_IN_P = D_IN + LANE - 16
META_OFF = CH - N_META
HEAD_POS = (0, 4, 1, 5, 2, 6, 3, 7)
TOKEN = (8, LANE)
N_CHIPS = 4
SHARD_ROWS = dict(w_in=D_IN // N_CHIPS, w_out=D // N_CHIPS, w_g=D_FF // N_CHIPS, w_u=D_FF // N_CHIPS,
                  w_d=D_FF // N_CHIPS)
SMALL_ROWS = 32
BF16_ROWS = 16
W_IN_WIN = -(-SHARD_ROWS["w_in"] // (2 * BF16_ROWS)) * 2 * BF16_ROWS
VMEM_CAP_MB = 64


def _lp():
    return SEQ + BLK


def _row_tile(cap):
    lp = _lp()
    return max(t for t in range(16, cap + 1, 16) if lp % t == 0)


def _params(vmem_mb, **kw):
    assert vmem_mb <= VMEM_CAP_MB - 6
    return pltpu.CompilerParams(vmem_limit_bytes=vmem_mb << 20, **kw)


def _seq(n=1):
    return ("arbitrary",) * n


def _const(shape):
    return pl.BlockSpec(shape, lambda *_: (0,) * len(shape), pipeline_mode=pl.Buffered(1))


def _acc(shape):
    return pl.BlockSpec(shape, lambda *_: (0,) * len(shape))


def _rows(tm, width):
    return pl.BlockSpec((tm, width), lambda i: (i, 0))


def _dot(a, b):
    return jnp.dot(a.astype(BF16), b.astype(BF16), preferred_element_type=F32)


def _dot_nt(a, b):
    return lax.dot_general(a.astype(BF16), b.astype(BF16), (((1,), (1,)), ((), ())), preferred_element_type=F32)


def _dot_tn(a, b):
    return lax.dot_general(a.astype(BF16), b.astype(BF16), (((0,), (0,)), ((), ())), preferred_element_type=F32)


def _dot_exact(a, b):
    return jnp.dot(a, b, precision=lax.Precision.HIGHEST, preferred_element_type=F32)


def _ln_stats(x):
    mu = jnp.mean(x, axis=-1, keepdims=True)
    xc = x - mu
    rstd = lax.rsqrt(jnp.mean(xc * xc, axis=-1, keepdims=True) + LN_EPS)
    return xc * rstd, rstd


def _ln_bwd(dy, xhat, rstd, g):
    dxh = dy * g
    return rstd * (dxh - jnp.mean(dxh, axis=-1, keepdims=True) - xhat * jnp.mean(dxh * xhat, axis=-1, keepdims=True))


def _sigmoid(x):
    return 1.0 / (1.0 + jnp.exp(-x))


def _iota(shape, dim):
    return lax.broadcasted_iota(jnp.int32, shape, dim)


def _hbm(*arrays):
    return tuple(pltpu.with_memory_space_constraint(a, pltpu.HBM) for a in arrays)


def _ln_in_fwd(x, meta_ext, g, b):
    nb = SEQ // BLK

    def body(x_ref, m_ref, g_ref, b_ref, h_ref):
        i = pl.program_id(0)
        xin = jnp.where(i < nb, x_ref[...], m_ref[...])
        xhat, _ = _ln_stats(xin)
        h_ref[...] = xhat * g_ref[...] + b_ref[...]

    return pl.pallas_call(
        body, name="ln_in_fwd", grid=(nb + 1,),
        in_specs=[pl.BlockSpec((BLK, D), lambda i: (jnp.minimum(i, nb - 1), 0)),
                  _const((BLK, D)), _const((1, D)), _const((1, D))],
        out_specs=_rows(BLK, D),
        out_shape=pltpu.HBM((_lp(), D), F32),
        compiler_params=_params(16, dimension_semantics=_seq()),
    )(*_hbm(x, meta_ext, g, b))


def _in_proj(h0, w_in_t, b_in_p, wg2_p, bg2, token):
    tm = _row_tile(384)
    lp = _lp()
    widths = (512, 128, 128, 256, 256, 512, 512, 128)
    offs = (O_QS, O_KS, O_VS, O_QG, O_KG, O_VG, O_RG, O_LR)

    def body(h_ref, w_ref, b_ref, wg2_ref, bg2_ref, token_ref, *outs):
        proj = _dot_nt(h_ref[...], w_ref[...]) + b_ref[...]
        for pos, h in enumerate(HEAD_POS):
            outs[0][:, pos * DH:(pos + 1) * DH] = proj[:, O_QS + h * DH:O_QS + (h + 1) * DH]
        for o_ref, off, wd in zip(outs[1:8], offs[1:], widths[1:]):
            o_ref[...] = proj[:, off:off + wd]
        outs[8][...] = _dot(proj[:, O_LR:O_LR + LANE], wg2_ref[...]) + bg2_ref[...]

    return pl.pallas_call(
        body, name="in_proj", grid=(lp // tm,),
        in_specs=[_rows(tm, D), _const((D_IN_P, D)), _const((1, D_IN_P)), _const((LANE, 256)), _const((1, 256)),
                  _const(TOKEN)],
        out_specs=[_rows(tm, w) for w in widths] + [_rows(tm, 256)],
        out_shape=[pltpu.HBM((lp, w), F32) for w in widths] + [pltpu.HBM((lp, 256), F32)],
        compiler_params=_params(40, dimension_semantics=_seq()),
    )(*_hbm(h0, w_in_t, b_in_p, wg2_p, bg2), token)


def _swa_masks(n):
    nb = SEQ // BLK
    is_meta = n == nb
    ri = _iota((BLK, BLK), 0)
    cj = _iota((BLK, BLK), 1)
    meta_col = ((cj >= META_OFF) & (cj < CH)).astype(jnp.int32)
    meta_q = meta_col * ((cj <= ri) & (ri < CH)).astype(jnp.int32)
    valid_m = jnp.where(is_meta, meta_q, meta_col) > 0
    dist_m = jnp.where(is_meta, ri - cj, n * BLK + ri + CH - cj).astype(F32)
    valid_p = jnp.where((n >= 1) & (n < nb), (cj > ri).astype(jnp.int32), 0) > 0
    dist_p = (ri + BLK - cj).astype(F32)
    valid_c = jnp.where(n < nb, (cj <= ri).astype(jnp.int32), 0) > 0
    dist_c = (ri - cj).astype(F32)
    return (dist_m, dist_p, dist_c), (valid_m, valid_p, valid_c)


def _swa_bias(n):
    dists, valids = _swa_masks(n)
    return (jnp.concatenate([-d for d in dists], axis=1),
            jnp.concatenate([jnp.where(v, 0.0, NEG) for v in valids], axis=1))


def _swa_half(ref, pos, scale=1.0):
    col = ref[:, (pos // 2) * LANE:(pos // 2 + 1) * LANE]
    lane = _iota((BLK, LANE), 1)
    mine = lane < DH if pos % 2 == 0 else lane >= DH
    return jnp.where(mine, col * scale, 0.0).astype(BF16)


def _swa_merge(even, odd):
    return jnp.where(_iota((BLK, LANE), 1) < DH, even, odd)


def _swa_softmax(t, sink):
    m = jnp.maximum(jnp.max(t, axis=-1, keepdims=True), sink)
    e = jnp.exp(t - m)
    e_sink = jnp.exp(sink - m)
    inv = 1.0 / (jnp.sum(e, axis=-1, keepdims=True) + e_sink)
    return e * inv, e_sink * inv


def _swa_kv_specs(width):
    nb = SEQ // BLK
    return [pl.BlockSpec((BLK, width), lambda n: (nb, 0)),
            pl.BlockSpec((BLK, width), lambda n: (jnp.clip(n - 1, 0, nb - 1), 0)),
            pl.BlockSpec((BLK, width), lambda n: (jnp.minimum(n, nb), 0))]


def _swa_fwd(sinks, qs, ks, vs):
    nb = SEQ // BLK
    heads = range(SWA_HEADS)

    def body(sink_ref, q_ref, km_ref, kp_ref, kc_ref, vm_ref, vp_ref, vc_ref, o_ref):
        negdist, maskbias = _swa_bias(pl.program_id(0))
        k_all = jnp.concatenate([km_ref[...], kp_ref[...], kc_ref[...]], axis=0).astype(BF16)
        v_all = jnp.concatenate([vm_ref[...], vp_ref[...], vc_ref[...]], axis=0).astype(BF16)
        q = [_swa_half(q_ref, pos, DH ** -0.5) for pos in heads]
        t = [_dot_nt(q[pos], k_all) + (2.0 ** -(HEAD_POS[pos] + 1) * negdist + maskbias) for pos in heads]
        p = [_swa_softmax(t[pos], sink_ref[HEAD_POS[pos]])[0].astype(BF16) for pos in heads]
        o = [_dot(p[pos], v_all) for pos in heads]
        for col in range(SWA_HEADS // 2):
            o_ref[:, col * LANE:(col + 1) * LANE] = _swa_merge(o[2 * col], o[2 * col + 1])

    kvw = SWA_KV_HEADS * DH
    return pl.pallas_call(
        body, name="swa_fwd", grid=(nb + 1,),
        in_specs=[pl.BlockSpec(memory_space=pltpu.SMEM), _rows(BLK, SWA_HEADS * DH)] + _swa_kv_specs(kvw) + _swa_kv_specs(kvw),
        out_specs=_rows(BLK, SWA_HEADS * DH),
        out_shape=pltpu.HBM((_lp(), SWA_HEADS * DH), F32),
        compiler_params=_params(16, dimension_semantics=_seq()),
    )(sinks, *_hbm(qs, ks, ks, ks, vs, vs, vs))


GLA_PER_STEP = BLK // CH


def _gla_block(s):
    nb = SEQ // BLK
    return jnp.where(s == 0, nb, s - 1)


def _gla_rowmask(s):
    ri = _iota((BLK, 1), 0)
    m = jnp.where(s == 0, ((ri >= META_OFF) & (ri < CH)).astype(jnp.int32), 1)
    return (m > 0).astype(F32) + jnp.zeros((BLK, 1), F32)


def _gla_chunk_masks():
    r, c = _iota((BLK, BLK), 0), _iota((BLK, BLK), 1)
    same = ((r < CH) & (c < CH)) | ((r >= CH) & (c >= CH))
    return same & (r >= c), same & (r <= c), same


def _gla_decay(z, rmask):
    log_g = (jnp.minimum(z, 0.0) - jnp.log1p(jnp.exp(-jnp.abs(z)))) * (rmask / GLA_TAU)
    lower, _, same = _gla_chunk_masks()
    return _dot_exact(lower.astype(F32), log_g), _dot_exact(same.astype(F32), log_g)


def _gla_slices(c, h):
    return slice(c * CH, (c + 1) * CH), slice(h * DK, (h + 1) * DK), slice(h * DV, (h + 1) * DV)


def _gla_fwd(qg, kg, vg, z):
    steps = SEQ // BLK + 1
    kw, vw = GLA_HEADS * DK, GLA_HEADS * DV
    pairs = [(c, h) for c in range(GLA_PER_STEP) for h in range(GLA_HEADS)]

    def body(q_ref, k_ref, v_ref, z_ref, o_ref, st_ref, st):
        s = pl.program_id(0)

        @pl.when(s == 0)
        def _():
            st[...] = jnp.zeros_like(st)

        rmask = _gla_rowmask(s)
        b, b_last = _gla_decay(z_ref[...], rmask)
        q = q_ref[...] * (rmask * DK ** -0.5)
        k = k_ref[...] * rmask
        v = v_ref[...] * rmask
        qe = q * jnp.exp(b)
        ke = k * jnp.exp(-b)
        kd = k * jnp.exp(b_last - b)
        e_last = jnp.exp(b_last)
        causal = _iota((CH, CH), 0) >= _iota((CH, CH), 1)
        a, upd, intra = {}, {}, {}
        for c, h in pairs:
            rows, ks, vs_ = _gla_slices(c, h)
            a[c, h] = jnp.where(causal, _dot_nt(qe[rows, ks], ke[rows, ks]), 0.0)
            upd[c, h] = _dot_tn(v[rows, vs_], kd[rows, ks])
        for c, h in pairs:
            rows, ks, vs_ = _gla_slices(c, h)
            intra[c, h] = _dot(a[c, h], v[rows, vs_])
        state = st[...]
        for c in range(GLA_PER_STEP):
            st_ref[0, c] = state
            for h in range(GLA_HEADS):
                rows, ks, vs_ = _gla_slices(c, h)
                o_ref[rows, vs_] = intra[c, h] + _dot_nt(qe[rows, ks], state[:, ks])
            state = state * e_last[c * CH:c * CH + 1] + jnp.concatenate([upd[c, h] for h in range(GLA_HEADS)], axis=1)
        st[...] = state

    blk = lambda w: pl.BlockSpec((BLK, w), lambda s: (_gla_block(s), 0))
    return pl.pallas_call(
        body, name="gla_fwd", grid=(steps,),
        in_specs=[blk(kw), blk(kw), blk(vw), blk(kw)],
        out_specs=[blk(vw), pl.BlockSpec((1, GLA_PER_STEP, DV, kw), lambda s: (s, 0, 0, 0))],
        out_shape=[pltpu.HBM((_lp(), vw), F32), pltpu.HBM((steps, GLA_PER_STEP, DV, kw), F32)],
        scratch_shapes=[pltpu.VMEM((DV, kw), F32)],
        compiler_params=_params(16, dimension_semantics=_seq()),
    )(*_hbm(qg, kg, vg, z))


def _post_mix(o_s, o_gla, r_g, h0, gn4, w_out, g1, b1):
    tm = _row_tile(384)
    lp = _lp()

    def body(os_ref, og_ref, r_ref, h0_ref, gn_ref, w_ref, g_ref, b_ref, o_ref, pre_ref, h1_ref):
        for pos, h in enumerate(HEAD_POS):
            o_ref[:, h * DH:(h + 1) * DH] = os_ref[:, pos * DH:(pos + 1) * DH].astype(BF16)
        for h in range(GLA_HEADS):
            hs = slice(h * DV, (h + 1) * DV)
            xg = og_ref[:, hs]
            n = xg * lax.rsqrt(jnp.mean(xg * xg, axis=-1, keepdims=True) + RMS_EPS) * gn_ref[...]
            r = r_ref[:, hs]
            o_ref[:, 512 + h * DV:512 + (h + 1) * DV] = (n * (r * _sigmoid(r))).astype(BF16)
        pre = ALPHA * h0_ref[...] + _dot(o_ref[...], w_ref[...])
        pre_ref[...] = pre
        xhat, _ = _ln_stats(pre)
        h1_ref[...] = xhat * g_ref[...] + b_ref[...]

    return pl.pallas_call(
        body, name="post_mix", grid=(lp // tm,),
        in_specs=[_rows(tm, 512), _rows(tm, 512), _rows(tm, 512), _rows(tm, D), _const((1, DV)), _const((D, D)),
                  _const((1, D)), _const((1, D))],
        out_specs=[_rows(tm, D), _rows(tm, D), _rows(tm, D)],
        out_shape=[pltpu.HBM((lp, D), BF16), pltpu.HBM((lp, D), F32),
                   pltpu.HBM((lp, D), F32)],
        compiler_params=_params(32, dimension_semantics=_seq()),
    )(*_hbm(o_s, o_gla, r_g, h0, gn4, w_out, g1, b1))


def _ffn_fwd(h1, wg_t, wu_t, wd):
    tm = _row_tile(192)
    lp = _lp()

    def body(h_ref, wg_ref, wu_ref, wd_ref, g_ref, u_ref, pre_ref):
        h = h_ref[...]
        g = _dot_nt(h, wg_ref[...])
        u = _dot_nt(h, wu_ref[...])
        g_ref[...] = g
        u_ref[...] = u
        pre_ref[...] = ALPHA * h + _dot(g * _sigmoid(g) * u, wd_ref[...])

    return pl.pallas_call(
        body, name="ffn_fwd", grid=(lp // tm,),
        in_specs=[_rows(tm, D), _const((D_FF, D)), _const((D_FF, D)), _const((D_FF, D))],
        out_specs=[_rows(tm, D_FF), _rows(tm, D_FF), _rows(tm, D)],
        out_shape=[pltpu.HBM((lp, D_FF), F32), pltpu.HBM((lp, D_FF), F32),
                   pltpu.HBM((lp, D), F32)],
        compiler_params=_params(48, dimension_semantics=_seq()),
    )(*_hbm(h1, wg_t, wu_t, wd))


def _ln2_loss_bwd(pre2, target, g2, b2):
    nb = SEQ // BLK

    def body(p_ref, t_ref, g_ref, b_ref, dp_ref, loss_ref, dg_ref, db_ref, acc):
        i = pl.program_id(0)

        @pl.when(i == 0)
        def _():
            acc[...] = jnp.zeros_like(acc)
            dg_ref[...] = jnp.zeros_like(dg_ref)
            db_ref[...] = jnp.zeros_like(db_ref)

        real = jnp.where(i < nb, 1.0, 0.0)
        xhat, rstd = _ln_stats(p_ref[...])
        diff = (xhat * g_ref[...] + b_ref[...] - t_ref[...]) * real
        acc[...] += jnp.sum(diff * diff, axis=0, keepdims=True)
        dy = diff * (1.0 / D)
        dp_ref[...] = _ln_bwd(dy, xhat, rstd, g_ref[...])
        dg_ref[...] += jnp.sum(dy * xhat, axis=0, keepdims=True)
        db_ref[...] += jnp.sum(dy, axis=0, keepdims=True)

        @pl.when(i == nb)
        def _():
            loss_ref[...] = jnp.zeros_like(loss_ref) + (0.5 / D) * jnp.sum(acc[...], axis=1, keepdims=True)

    return pl.pallas_call(
        body, name="ln2_loss_bwd", grid=(nb + 1,),
        in_specs=[_rows(BLK, D), pl.BlockSpec((BLK, D), lambda i: (jnp.minimum(i, nb - 1), 0)), _const((1, D)),
                  _const((1, D))],
        out_specs=[_rows(BLK, D), _acc((1, LANE)), _acc((1, D)), _acc((1, D))],
        out_shape=[pltpu.HBM((_lp(), D), F32), pltpu.HBM((1, LANE), F32),
                   pltpu.HBM((1, D), F32), pltpu.HBM((1, D), F32)],
        scratch_shapes=[pltpu.VMEM((1, D), F32)],
        compiler_params=_params(16, dimension_semantics=_seq()),
    )(*_hbm(pre2, target, g2, b2))


def _ffn_bwd(dpre2, g, u, pre1, wg_t, wu_t, wd, g1):
    tm = _row_tile(192)
    lp = _lp()

    def body(dp_ref, g_ref, u_ref, p1_ref, wg_ref, wu_ref, wd_ref, g1_ref, a_ref, dg_ref, du_ref, dp1_ref,
             dg1_ref, db1_ref):
        @pl.when(pl.program_id(0) == 0)
        def _():
            dg1_ref[...] = jnp.zeros_like(dg1_ref)
            db1_ref[...] = jnp.zeros_like(db1_ref)

        dp = dp_ref[...]
        gg, uu = g_ref[...], u_ref[...]
        sg = _sigmoid(gg)
        silu = gg * sg
        da = _dot_nt(dp, wd_ref[...])
        a_ref[...] = (silu * uu).astype(BF16)
        dgate = (da * uu * (sg * (1.0 + gg * (1.0 - sg)))).astype(BF16)
        dup = (da * silu).astype(BF16)
        dg_ref[...] = dgate
        du_ref[...] = dup
        dh1 = ALPHA * dp + _dot(dgate, wg_ref[...]) + _dot(dup, wu_ref[...])
        xhat, rstd = _ln_stats(p1_ref[...])
        dp1_ref[...] = _ln_bwd(dh1, xhat, rstd, g1_ref[...])
        dg1_ref[...] += jnp.sum(dh1 * xhat, axis=0, keepdims=True)
        db1_ref[...] += jnp.sum(dh1, axis=0, keepdims=True)

    return pl.pallas_call(
        body, name="ffn_bwd", grid=(lp // tm,),
        in_specs=[_rows(tm, D), _rows(tm, D_FF), _rows(tm, D_FF), _rows(tm, D), _const((D_FF, D)), _const((D_FF, D)),
                  _const((D_FF, D)), _const((1, D))],
        out_specs=[_rows(tm, D_FF), _rows(tm, D_FF), _rows(tm, D_FF), _rows(tm, D), _acc((1, D)), _acc((1, D))],
        out_shape=[pltpu.HBM((lp, D_FF), BF16)] * 3
        + [pltpu.HBM((lp, D), F32), pltpu.HBM((1, D), F32), pltpu.HBM((1, D), F32)],
        compiler_params=_params(52, dimension_semantics=_seq()),
    )(*_hbm(dpre2, g, u, pre1, wg_t, wu_t, wd, g1))


def _atb(a, b, name):
    lp = _lp()
    tm = _row_tile(384)
    n, w = a.shape[1], b.shape[1]
    bw = 512 if n * w * 4 > (4 << 20) else w

    def body(a_ref, b_ref, o_ref):
        @pl.when(pl.program_id(1) == 0)
        def _():
            o_ref[...] = jnp.zeros_like(o_ref)

        o_ref[...] += _dot_tn(a_ref[...], b_ref[...])

    return pl.pallas_call(
        body, name=name, grid=(w // bw, lp // tm),
        in_specs=[pl.BlockSpec((tm, n), lambda j, k: (k, 0)), pl.BlockSpec((tm, bw), lambda j, k: (k, j))],
        out_specs=pl.BlockSpec((n, bw), lambda j, k: (0, j)),
        out_shape=pltpu.HBM((n, w), F32),
        compiler_params=_params(48, dimension_semantics=_seq(2)),
    )(*_hbm(a, b))


def _out_bwd(dpre1, w_out, o_gla, r_g, gn4, token):
    tm = _row_tile(384)
    lp = _lp()

    def body(dp_ref, w_ref, og_ref, r_ref, gn_ref, token_ref, dos_ref, dog_ref, dr_ref, dgn_ref):
        @pl.when(pl.program_id(0) == 0)
        def _():
            dgn_ref[...] = jnp.zeros_like(dgn_ref)

        do = _dot_nt(dp_ref[...], w_ref[...])
        for pos, h in enumerate(HEAD_POS):
            dos_ref[:, pos * DH:(pos + 1) * DH] = do[:, h * DH:(h + 1) * DH]
        gn = gn_ref[...]
        for h in range(GLA_HEADS):
            hs = slice(h * DV, (h + 1) * DV)
            xg = og_ref[:, hs]
            rstd = lax.rsqrt(jnp.mean(xg * xg, axis=-1, keepdims=True) + RMS_EPS)
            nx = xg * rstd
            r = r_ref[:, hs]
            sr = _sigmoid(r)
            d_o = do[:, 512 + h * DV:512 + (h + 1) * DV]
            dr_ref[:, hs] = d_o * (nx * gn) * (sr * (1.0 + r * (1.0 - sr)))
            dn = d_o * (r * sr)
            dgn_ref[...] += jnp.sum(dn * nx, axis=0, keepdims=True)
            dnx = dn * gn
            dog_ref[:, hs] = rstd * (dnx - nx * jnp.mean(dnx * nx, axis=-1, keepdims=True))

    return pl.pallas_call(
        body, name="out_bwd", grid=(lp // tm,),
        in_specs=[_rows(tm, D), _const((D, D)), _rows(tm, 512), _rows(tm, 512), _const((1, DV)), _const(TOKEN)],
        out_specs=[_rows(tm, 512), _rows(tm, 512), _rows(tm, 512), _acc((1, DV))],
        out_shape=[pltpu.HBM((lp, 512), F32)] * 3 + [pltpu.HBM((1, DV), F32)],
        compiler_params=_params(32, dimension_semantics=_seq()),
    )(*_hbm(dpre1, w_out, o_gla, r_g, gn4), token)


def _gla_bwd(qg, kg, vg, z, do_gla, st_all):
    steps = SEQ // BLK + 1
    kw, vw = GLA_HEADS * DK, GLA_HEADS * DV
    pairs = [(c, h) for c in range(GLA_PER_STEP) for h in range(GLA_HEADS)]
    heads = range(GLA_HEADS)

    def body(q_ref, k_ref, v_ref, z_ref, do_ref, st_ref, dq_ref, dk_ref, dv_ref, dz_ref, dst):
        @pl.when(pl.program_id(0) == 0)
        def _():
            dst[...] = jnp.zeros_like(dst)

        rmask = _gla_rowmask(steps - 1 - pl.program_id(0))
        zz = z_ref[...]
        b, b_last = _gla_decay(zz, rmask)
        e_b, e_nb, e_kd, e_last = jnp.exp(b), jnp.exp(-b), jnp.exp(b_last - b), jnp.exp(b_last)
        q = q_ref[...] * (rmask * DK ** -0.5)
        k = k_ref[...] * rmask
        v = v_ref[...] * rmask
        qe, ke, kd = q * e_b, k * e_nb, k * e_kd
        d_o = do_ref[...]
        causal = _iota((CH, CH), 0) >= _iota((CH, CH), 1)
        a, da, dqe, dke, dv_intra, carry = {}, {}, {}, {}, {}, {}
        for c, h in pairs:
            rows, ks, vs_ = _gla_slices(c, h)
            a[c, h] = jnp.where(causal, _dot_nt(qe[rows, ks], ke[rows, ks]), 0.0)
            da[c, h] = jnp.where(causal, _dot_nt(d_o[rows, vs_], v[rows, vs_]), 0.0)
            carry[c, h] = _dot_tn(d_o[rows, vs_], qe[rows, ks])
        for c, h in pairs:
            rows, ks, vs_ = _gla_slices(c, h)
            dqe[c, h] = _dot(d_o[rows, vs_], st_ref[0, c][:, ks]) + _dot(da[c, h], ke[rows, ks])
            dke[c, h] = _dot_tn(da[c, h], qe[rows, ks])
            dv_intra[c, h] = _dot_tn(a[c, h], d_o[rows, vs_])
        dstate = dst[...]
        dkd, db_decay = {}, {}
        for c in reversed(range(GLA_PER_STEP)):
            for h in heads:
                rows, ks, vs_ = _gla_slices(c, h)
                dkd[c, h] = _dot(v[rows, vs_], dstate[:, ks])
                dv_ref[rows, vs_] = dv_intra[c, h] + _dot_nt(kd[rows, ks], dstate[:, ks])
            chunk_last = e_last[c * CH:c * CH + 1]
            db_decay[c] = jnp.sum(dstate * st_ref[0, c], axis=0, keepdims=True) * chunk_last
            dstate = dstate * chunk_last + jnp.concatenate([carry[c, h] for h in heads], axis=1)
        dst[...] = dstate
        rows_of = lambda parts: jnp.concatenate(
            [jnp.concatenate([parts[c, h] for h in heads], axis=1) for c in range(GLA_PER_STEP)], axis=0)
        dqe_all, dke_all, dkd_all = rows_of(dqe), rows_of(dke), rows_of(dkd)
        dq_ref[...] = dqe_all * e_b * (rmask * DK ** -0.5)
        dk_ref[...] = (dke_all * e_nb + dkd_all * e_kd) * rmask
        dkd_kd = dkd_all * kd
        db = dqe_all * qe - dke_all * ke - dkd_kd
        _, upper, same = _gla_chunk_masks()
        decay_rows = jnp.concatenate([jnp.broadcast_to(db_decay[c], (CH, kw)) for c in range(GLA_PER_STEP)], axis=0)
        dlog_g = _dot_exact(upper.astype(F32), db) + _dot_exact(same.astype(F32), dkd_kd) + decay_rows
        dz_ref[...] = dlog_g * (rmask / GLA_TAU) * _sigmoid(-zz)

    blk = lambda w: pl.BlockSpec((BLK, w), lambda s: (_gla_block(steps - 1 - s), 0))
    return pl.pallas_call(
        body, name="gla_bwd", grid=(steps,),
        in_specs=[blk(kw), blk(kw), blk(vw), blk(kw), blk(vw),
                  pl.BlockSpec((1, GLA_PER_STEP, DV, kw), lambda s: (steps - 1 - s, 0, 0, 0))],
        out_specs=[blk(kw), blk(kw), blk(vw), blk(kw)],
        out_shape=[pltpu.HBM((_lp(), kw), F32), pltpu.HBM((_lp(), kw), F32),
                   pltpu.HBM((_lp(), vw), F32), pltpu.HBM((_lp(), kw), F32)],
        scratch_shapes=[pltpu.VMEM((DV, kw), F32)],
        compiler_params=_params(16, dimension_semantics=_seq()),
    )(*_hbm(qg, kg, vg, z, do_gla, st_all))


def _swa_bwd(sinks, qs, ks, vs, do_s):
    nb = SEQ // BLK
    kvw = SWA_KV_HEADS * DH
    scale = DH ** -0.5
    heads = range(SWA_HEADS)

    def body(sink_ref, q_ref, km_ref, kp_ref, kc_ref, vm_ref, vp_ref, vc_ref, do_ref,
             dq_ref, dk_ref, dv_ref, dsink_ref, carry_k, carry_v, meta_k, meta_v):
        n = pl.program_id(0)

        @pl.when(n == 0)
        def _():
            for r in (carry_k, carry_v, meta_k, meta_v):
                r[...] = jnp.zeros_like(r)
            dsink_ref[...] = jnp.zeros_like(dsink_ref)

        @pl.when(n <= nb)
        def _():
            negdist, maskbias = _swa_bias(n)
            lane = _iota((1, LANE), 1)
            k_all = jnp.concatenate([km_ref[...], kp_ref[...], kc_ref[...]], axis=0).astype(BF16)
            v_all = jnp.concatenate([vm_ref[...], vp_ref[...], vc_ref[...]], axis=0).astype(BF16)
            q = [_swa_half(q_ref, pos, scale) for pos in heads]
            d_o = [_swa_half(do_ref, pos) for pos in heads]
            t = [_dot_nt(q[pos], k_all) + (2.0 ** -(HEAD_POS[pos] + 1) * negdist + maskbias) for pos in heads]
            dp = [_dot_nt(d_o[pos], v_all) for pos in heads]
            soft = [_swa_softmax(t[pos], sink_ref[HEAD_POS[pos]]) for pos in heads]
            p = [s[0] for s in soft]
            delta = [jnp.sum(p[pos] * dp[pos], axis=-1, keepdims=True) for pos in heads]
            ds = [(p[pos] * (dp[pos] - delta[pos])).astype(BF16) for pos in heads]
            dq = [_dot(ds[pos], k_all) for pos in heads]
            for col in range(SWA_HEADS // 2):
                dq_ref[:, col * LANE:(col + 1) * LANE] = scale * _swa_merge(dq[2 * col], dq[2 * col + 1])
            dsink = jnp.zeros((1, LANE), F32)
            for pos in heads:
                dsink = dsink + jnp.where(lane == HEAD_POS[pos],
                                          -jnp.sum(soft[pos][1] * delta[pos], axis=0, keepdims=True), 0.0)
            dsink_ref[...] += dsink
            dk3 = _dot_tn(jnp.concatenate(q, axis=0), jnp.concatenate(ds, axis=0)).T
            dv3 = _dot_tn(jnp.concatenate(d_o, axis=0), jnp.concatenate([x.astype(BF16) for x in p], axis=0)).T
            meta_k[...] += dk3[0:BLK]
            meta_v[...] += dv3[0:BLK]
            dk_ref[...] = carry_k[...] + dk3[BLK:2 * BLK]
            dv_ref[...] = carry_v[...] + dv3[BLK:2 * BLK]
            carry_k[...] = dk3[2 * BLK:3 * BLK]
            carry_v[...] = dv3[2 * BLK:3 * BLK]

        @pl.when(n == nb + 1)
        def _():
            dk_ref[...] = meta_k[...]
            dv_ref[...] = meta_v[...]

    kv_out = pl.BlockSpec((BLK, kvw), lambda n: (jnp.where(n == nb + 1, nb, jnp.clip(n - 1, 0, nb - 1)), 0))
    qblk = pl.BlockSpec((BLK, SWA_HEADS * DH), lambda n: (jnp.minimum(n, nb), 0))
    return pl.pallas_call(
        body, name="swa_bwd", grid=(nb + 2,),
        in_specs=[pl.BlockSpec(memory_space=pltpu.SMEM), qblk] + _swa_kv_specs(kvw) + _swa_kv_specs(kvw) + [qblk],
        out_specs=[qblk, kv_out, kv_out, _acc((1, LANE))],
        out_shape=[pltpu.HBM((_lp(), SWA_HEADS * DH), F32), pltpu.HBM((_lp(), kvw), F32),
                   pltpu.HBM((_lp(), kvw), F32), pltpu.HBM((1, LANE), F32)],
        scratch_shapes=[pltpu.VMEM((BLK, kvw), F32)] * 4,
        compiler_params=_params(16, dimension_semantics=_seq()),
    )(sinks, *_hbm(qs, ks, ks, ks, vs, vs, vs, do_s))


def _in_bwd(dqs, dks, dvs, dqg, dkg, dvg, drg, dz, dpre1, w_in_t, wg2_p):
    tm = _row_tile(384)
    lp = _lp()
    widths = (512, 128, 128, 256, 256, 512, 512)
    offs = (O_QS, O_KS, O_VS, O_QG, O_KG, O_VG, O_RG)

    def body(*refs):
        parts, (dz_ref, dp1_ref, w_ref, wg2_ref, dproj_ref, dh0_ref, dbin_ref, dbg_ref) = refs[:7], refs[7:]

        @pl.when(pl.program_id(0) == 0)
        def _():
            dbin_ref[...] = jnp.zeros_like(dbin_ref)
            dbg_ref[...] = jnp.zeros_like(dbg_ref)

        for pos, h in enumerate(HEAD_POS):
            val = parts[0][:, pos * DH:(pos + 1) * DH]
            dproj_ref[:, O_QS + h * DH:O_QS + (h + 1) * DH] = val.astype(BF16)
            dbin_ref[:, O_QS + h * DH:O_QS + (h + 1) * DH] += jnp.sum(val, axis=0, keepdims=True)
        for p_ref, off, wd in zip(parts[1:], offs[1:], widths[1:]):
            val = p_ref[...]
            dproj_ref[:, off:off + wd] = val.astype(BF16)
            dbin_ref[:, off:off + wd] += jnp.sum(val, axis=0, keepdims=True)
        dz = dz_ref[...]
        dlr = _dot_nt(dz, wg2_ref[...])
        dproj_ref[:, O_LR:O_LR + LANE] = dlr.astype(BF16)
        dbin_ref[:, O_LR:O_LR + LANE] += jnp.sum(dlr, axis=0, keepdims=True)
        dbg_ref[...] += jnp.sum(dz, axis=0, keepdims=True)
        dh0_ref[...] = ALPHA * dp1_ref[...] + _dot(dproj_ref[...], w_ref[...])

    return pl.pallas_call(
        body, name="in_bwd", grid=(lp // tm,),
        in_specs=[_rows(tm, w) for w in widths] + [_rows(tm, 256), _rows(tm, D), _const((D_IN_P, D)), _const((LANE, 256))],
        out_specs=[_rows(tm, D_IN_P), _rows(tm, D), _acc((1, D_IN_P)), _acc((1, 256))],
        out_shape=[pltpu.HBM((lp, D_IN_P), BF16), pltpu.HBM((lp, D), F32),
                   pltpu.HBM((1, D_IN_P), F32), pltpu.HBM((1, 256), F32)],
        compiler_params=_params(40, dimension_semantics=_seq()),
    )(*_hbm(dqs, dks, dvs, dqg, dkg, dvg, drg, dz, dpre1, w_in_t, wg2_p))


def _ln_in_bwd(x, meta_ext, dh0, g):
    nb = SEQ // BLK

    def body(x_ref, m_ref, dh_ref, g_ref, dx_ref, dm_ref, dg_ref, db_ref):
        i = pl.program_id(0)

        @pl.when(i == 0)
        def _():
            dg_ref[...] = jnp.zeros_like(dg_ref)
            db_ref[...] = jnp.zeros_like(db_ref)

        xin = jnp.where(i < nb, x_ref[...], m_ref[...])
        xhat, rstd = _ln_stats(xin)
        dh = dh_ref[...]
        dxin = _ln_bwd(dh, xhat, rstd, g_ref[...])
        dg_ref[...] += jnp.sum(dh * xhat, axis=0, keepdims=True)
        db_ref[...] += jnp.sum(dh, axis=0, keepdims=True)

        @pl.when(i < nb)
        def _():
            dx_ref[...] = dxin

        @pl.when(i == nb)
        def _():
            dm_ref[...] = dxin

    xblk = pl.BlockSpec((BLK, D), lambda i: (jnp.minimum(i, nb - 1), 0))
    return pl.pallas_call(
        body, name="ln_in_bwd", grid=(nb + 1,),
        in_specs=[xblk, _const((BLK, D)), _rows(BLK, D), _const((1, D))],
        out_specs=[xblk, _acc((BLK, D)), _acc((1, D)), _acc((1, D))],
        out_shape=[pltpu.HBM((SEQ, D), F32), pltpu.HBM((BLK, D), F32),
                   pltpu.HBM((1, D), F32), pltpu.HBM((1, D), F32)],
        compiler_params=_params(16, dimension_semantics=_seq()),
    )(*_hbm(x, meta_ext, dh0, g))


def _local_step(x, target, meta_full, ln_in_g, ln_in_b, w_in_t, b_in, wg2, bg2, sinks, gn, g1, b1, g2, b2,
                token, fetch_rest, ship_ffn):
    row = lambda v: v.reshape(1, -1).astype(F32)
    meta_ext = jnp.pad(meta_full, ((META_OFF, BLK - CH), (0, 0)))
    b_in_p = jnp.pad(row(b_in), ((0, 0), (0, D_IN_P - D_IN)))
    wg2_p = jnp.pad(wg2, ((0, LANE - wg2.shape[0]), (0, 0))).astype(BF16)
    gn4 = row(gn)
    sinks = sinks.reshape(-1).astype(F32)

    h0 = _ln_in_fwd(x, meta_ext, row(ln_in_g), row(ln_in_b))
    qs, ks, vs, qg, kg, vg, rg, glr, z = _in_proj(h0, w_in_t, b_in_p, wg2_p, row(bg2), token)
    o_s = _swa_fwd(sinks, qs, ks, vs)
    o_gla, st_all = _gla_fwd(qg, kg, vg, z)
    w_out, wg_t, wu_t, wd = fetch_rest([o_s, o_gla])
    o, pre1, h1 = _post_mix(o_s, o_gla, rg, h0, gn4, w_out, row(g1), row(b1))
    g, u, pre2 = _ffn_fwd(h1, wg_t, wu_t, wd)

    dpre2, loss, dg2, db2 = _ln2_loss_bwd(pre2, target, row(g2), row(b2))
    a, dgate, dup, dpre1, dg1, db1 = _ffn_bwd(dpre2, g, u, pre1, wg_t, wu_t, wd, row(g1))
    dwd = _atb(a, dpre2, "dw_down")
    dwg_t = _atb(dgate, h1, "dw_gate")
    dwu_t = _atb(dup, h1, "dw_up")
    dw_out = _atb(o, dpre1, "dw_out")
    token = ship_ffn(dict(w_out=dw_out, w_g=dwg_t, w_u=dwu_t, w_d=dwd))
    do_s, do_gla, drg, dgn = _out_bwd(dpre1, w_out, o_gla, rg, gn4, token)
    dqg, dkg, dvg, dz = _gla_bwd(qg, kg, vg, z, do_gla, st_all)
    dqs, dks, dvs, dsinks = _swa_bwd(sinks, qs, ks, vs, do_s)
    dproj, dh0, db_in_p, dbg2 = _in_bwd(dqs, dks, dvs, dqg, dkg, dvg, drg, dz, dpre1, w_in_t, wg2_p)
    dw_in_t = _atb(dproj, h0, "dw_in")
    dwg2_p = _atb(glr, dz, "dw_gate_lr2")
    dx, dmeta_blk, dg_in, db_in_ln = _ln_in_bwd(x, meta_ext, dh0, row(ln_in_g))

    grads = dict(
        w_in=dw_in_t,
        meta=dmeta_blk[META_OFF:CH], ln_in_g=dg_in, ln_in_b=db_in_ln, ln1_g=dg1, ln1_b=db1, ln2_g=dg2, ln2_b=db2,
        b_in=db_in_p[:, :D_IN], wg2=dwg2_p[:wg2.shape[0]], bg2=dbg2, sinks=dsinks[:, :SWA_HEADS], gn=dgn)
    return loss[0, 0], dx, grads


HBM = pl.BlockSpec(memory_space=pltpu.HBM)


def _place():
    return lax.axis_index("x"), lax.axis_index("y"), lax.axis_index("c")


def _other_chips(x, y):
    return [(1 - x, y), (x, 1 - y), (1 - x, 1 - y)]


def _dma_sems(n):
    return pltpu.SemaphoreType.DMA((n,))


def _comm_params():
    return pltpu.CompilerParams(has_side_effects=True)


def _gather_halves(shards):
    n = len(shards)

    def body(*refs):
        ins, outs = refs[:n], refs[n:2 * n]
        ici_send, ici_recv, d2d_send, d2d_recv = refs[2 * n:]
        x, y, c = _place()
        mine = 2 * x + y
        chips = _other_chips(x, y)

        def ici(a, j, src_chip):
            px, py = chips[j]
            return pltpu.make_async_remote_copy(ins[a].at[c], outs[a].at[src_chip, c], ici_send.at[3 * a + j],
                                                ici_recv.at[3 * a + j], device_id=(px, py, c), device_id_type=MESH)

        def d2d(a, j, half):
            px, py = chips[j]
            blk = outs[a].at[2 * px + py, half]
            return pltpu.make_async_remote_copy(blk, blk, d2d_send.at[3 * a + j], d2d_recv.at[3 * a + j],
                                                device_id=(x, y, 1 - c), device_id_type=MESH)

        sends = [ici(a, j, mine) for a in range(n) for j in range(3)]
        for cp in sends:
            cp.start()
        passed = []
        for a in range(n):
            for j, (px, py) in enumerate(chips):
                ici(a, j, 2 * px + py).wait_recv()
                fwd = d2d(a, j, c)
                fwd.start()
                passed.append(fwd)
        for a in range(n):
            for j in range(3):
                d2d(a, j, 1 - c).wait_recv()
        for cp in sends + passed:
            cp.wait_send()

    gathered = pl.pallas_call(
        body, name="gather_halves",
        in_specs=[HBM] * n, out_specs=[HBM] * n,
        out_shape=[pltpu.HBM((N_CHIPS,) + s.shape, s.dtype) for s in shards],
        scratch_shapes=[_dma_sems(3 * n)] * 4,
        compiler_params=_comm_params(),
    )(*_hbm(*shards))
    mine = 2 * lax.axis_index("x") + lax.axis_index("y")
    return [lax.dynamic_update_index_in_dim(g, s, mine, axis=0) for g, s in zip(gathered, shards)]


SEM = pl.BlockSpec(memory_space=pltpu.SEMAPHORE)


def _ici_copies(kind, srcs, lands, send_sems, recv_sems):
    x, y, c = _place()
    mine = 2 * x + y
    to_start, to_wait = [], []
    for a in range(len(srcs)):
        for j, (px, py) in enumerate(_other_chips(x, y)):
            peer = 2 * px + py
            if kind == "gather":
                src, there, here = srcs[a].at[c], lands[a].at[mine, c], lands[a].at[peer, c]
            else:
                src, there, here = srcs[a].at[peer], lands[a].at[mine], lands[a].at[peer]
            for dst, out in ((there, to_start), (here, to_wait)):
                out.append(pltpu.make_async_remote_copy(src, dst, send_sems.at[3 * a + j], recv_sems.at[3 * a + j],
                                                        device_id=(px, py, c), device_id_type=MESH))
    return to_start, to_wait


def _split_params():
    return pltpu.CompilerParams(has_side_effects=pltpu.SideEffectType.DATAFLOW_SIDE_EFFECTING)


def _ici_start(kind, srcs, land_shapes, name):
    n = len(srcs)
    lands = [pltpu.with_memory_space_constraint(lax.empty(s, a.dtype), pltpu.HBM) for s, a in zip(land_shapes, srcs)]

    def body(*refs):
        to_start, _ = _ici_copies(kind, refs[:n], refs[n:2 * n], refs[2 * n], refs[2 * n + 1])
        for cp in to_start:
            cp.start()
        refs[-1][...] = jnp.zeros(TOKEN, F32)

    outs = pl.pallas_call(
        body, name=name, in_specs=[HBM] * (2 * n),
        out_specs=[SEM, SEM] + [HBM] * (2 * n) + [pl.BlockSpec(memory_space=pltpu.VMEM)],
        out_shape=[_dma_sems(3 * n)] * 2 + [pltpu.HBM(a.shape, a.dtype) for a in list(srcs) + lands]
        + [jax.ShapeDtypeStruct(TOKEN, F32)],
        input_output_aliases={i: 2 + i for i in range(2 * n)},
        compiler_params=_split_params(),
    )(*_hbm(*srcs), *lands)
    return outs[:-1], outs[-1]


def _ici_wait(kind, handle, after, name):
    n = (len(handle) - 2) // 2

    def body(*refs):
        _, to_wait = _ici_copies(kind, refs[:n], refs[n:2 * n], refs[2 * n], refs[2 * n + 1])
        for cp in to_wait:
            cp.wait_send()
            cp.wait_recv()

    outs = pl.pallas_call(
        body, name=name, in_specs=[HBM] * (2 * n) + [SEM, SEM] + [pl.BlockSpec(memory_space=pl.ANY)] * len(after),
        out_specs=[HBM] * (2 * n), out_shape=[pltpu.HBM(a.shape, a.dtype) for a in handle[2:]],
        input_output_aliases={i: i for i in range(2 * n)},
        compiler_params=_split_params(),
    )(*handle[2:], handle[0], handle[1], *after)
    return list(outs[n:])


def _sibling_forward(lands):
    n = len(lands)

    def body(*refs):
        outs = refs[n:2 * n]
        send_sems, recv_sems = refs[2 * n:]
        x, y, c = _place()

        def copy(a, j, half):
            px, py = _other_chips(x, y)[j]
            blk = outs[a].at[2 * px + py, half]
            return pltpu.make_async_remote_copy(blk, blk, send_sems.at[3 * a + j], recv_sems.at[3 * a + j],
                                                device_id=(x, y, 1 - c), device_id_type=MESH)

        pairs = [(a, j) for a in range(n) for j in range(3)]
        for a, j in pairs:
            copy(a, j, c).start()
        for a, j in pairs:
            copy(a, j, 1 - c).wait_recv()
        for a, j in pairs:
            copy(a, j, c).wait_send()

    return pl.pallas_call(
        body, name="sibling_forward", in_specs=[HBM] * n, out_specs=[HBM] * n,
        out_shape=[pltpu.HBM(a.shape, a.dtype) for a in lands],
        input_output_aliases={a: a for a in range(n)},
        scratch_shapes=[_dma_sems(3 * n)] * 2,
        compiler_params=_comm_params(),
    )(*_hbm(*lands))


def _sibling_exchange(grads):
    n = len(grads)

    def body(*refs):
        ins, outs = refs[:n], refs[n:2 * n]
        send_sems, recv_sems = refs[2 * n:]
        x, y, c = _place()
        copies = []
        for a in range(n):
            for s in range(N_CHIPS):
                cp = pltpu.make_async_remote_copy(ins[a].at[s, 1 - c], outs[a].at[s], send_sems.at[N_CHIPS * a + s],
                                                  recv_sems.at[N_CHIPS * a + s], device_id=(x, y, 1 - c),
                                                  device_id_type=MESH)
                cp.start()
                copies.append(cp)
        for cp in copies:
            cp.wait_recv()
        for cp in copies:
            cp.wait_send()

    return pl.pallas_call(
        body, name="sibling_exchange", in_specs=[HBM] * n, out_specs=[HBM] * n,
        out_shape=[pltpu.HBM((N_CHIPS, g.shape[2], D), F32) for g in grads],
        scratch_shapes=[_dma_sems(N_CHIPS * n)] * 2,
        compiler_params=_comm_params(),
    )(*_hbm(*grads))


def _add_halves(core, grad, recv, dtype, name):
    h = grad.shape[2]

    def body(c_ref, a_ref, b_ref, o_ref):
        o_ref[...] = (a_ref[0] + b_ref[...]).astype(dtype)

    return pl.pallas_call(
        body, name=name,
        grid_spec=pltpu.PrefetchScalarGridSpec(
            num_scalar_prefetch=1, grid=(N_CHIPS,),
            in_specs=[pl.BlockSpec((1, 1, h, D), lambda s, c: (s, c[0], 0, 0)),
                      pl.BlockSpec((1, h, D), lambda s, c: (s, 0, 0))],
            out_specs=pl.BlockSpec((1, h, D), lambda s, c: (s, 0, 0))),
        out_shape=pltpu.HBM((N_CHIPS, h, D), dtype),
        compiler_params=_params(16, dimension_semantics=_seq()),
    )(core, *_hbm(grad, recv))


def _chip_scatter(parts, with_own):
    n = len(parts)

    def body(*refs):
        ins, outs = refs[:n], refs[n:2 * n]
        send_sems, recv_sems, local_sems = refs[2 * n:]
        x, y, c = _place()
        mine = 2 * x + y
        chips = _other_chips(x, y)
        local = [pltpu.make_async_copy(ins[a].at[mine], outs[a].at[mine], local_sems.at[a]) for a in range(n)
                 if with_own[a]]
        for cp in local:
            cp.start()
        sends = []
        for a in range(n):
            for j, (px, py) in enumerate(chips):
                cp = pltpu.make_async_remote_copy(ins[a].at[2 * px + py], outs[a].at[mine], send_sems.at[3 * a + j],
                                                  recv_sems.at[3 * a + j], device_id=(px, py, c), device_id_type=MESH)
                cp.start()
                sends.append(cp)
        for a in range(n):
            for j, (px, py) in enumerate(chips):
                pltpu.make_async_remote_copy(ins[a].at[mine], outs[a].at[2 * px + py], send_sems.at[3 * a + j],
                                             recv_sems.at[3 * a + j], device_id=(px, py, c),
                                             device_id_type=MESH).wait_recv()
        for cp in sends:
            cp.wait_send()
        for cp in local:
            cp.wait()

    return pl.pallas_call(
        body, name="chip_scatter", in_specs=[HBM] * n, out_specs=[HBM] * n,
        out_shape=[pltpu.HBM(p.shape, p.dtype) for p in parts],
        scratch_shapes=[_dma_sems(3 * n)] * 2 + [_dma_sems(n)],
        compiler_params=_comm_params(),
    )(*_hbm(*parts))


def _sum_chips(slots, first, rest, name):
    h = first.shape[1]

    def body(i_ref, a_ref, b_ref, c_ref, d_ref, o_ref):
        o_ref[...] = ((a_ref[...].astype(F32) + b_ref[...].astype(F32)) + c_ref[...].astype(F32)) + d_ref[...].astype(F32)

    slab = lambda k: pl.BlockSpec((1, h, D), lambda i, ix: (ix[k], 0, 0))
    return pl.pallas_call(
        body, name=name,
        grid_spec=pltpu.PrefetchScalarGridSpec(num_scalar_prefetch=1, grid=(1,),
                                               in_specs=[slab(0), slab(1), slab(2), slab(3)], out_specs=slab(4)),
        out_shape=pltpu.HBM((2, h, D), F32),
        compiler_params=_params(16, dimension_semantics=_seq()),
    )(slots, *_hbm(first, rest, rest, rest))


def _join_halves(halves):
    n = len(halves)

    def body(*refs):
        outs = refs[n:2 * n]
        send_sems, recv_sems = refs[2 * n:]
        x, y, c = _place()

        def copy(a, slab):
            return pltpu.make_async_remote_copy(outs[a].at[slab], outs[a].at[slab], send_sems.at[a], recv_sems.at[a],
                                                device_id=(x, y, 1 - c), device_id_type=MESH)

        for a in range(n):
            copy(a, c).start()
        for a in range(n):
            copy(a, 1 - c).wait_recv()
        for a in range(n):
            copy(a, c).wait_send()

    return pl.pallas_call(
        body, name="join_halves", in_specs=[HBM] * n, out_specs=[HBM] * n,
        out_shape=[pltpu.HBM(h.shape, F32) for h in halves],
        input_output_aliases={a: a for a in range(n)},
        scratch_shapes=[_dma_sems(n)] * 2,
        compiler_params=_comm_params(),
    )(*_hbm(*halves))


def _chip_partials(grads, wire_dtypes, names):
    core = lax.axis_index("c").astype(jnp.int32).reshape(1)
    recv = _sibling_exchange(grads)
    return [_add_halves(core, g, r, dt, "add_halves_" + nm) for g, r, dt, nm in zip(grads, recv, wire_dtypes, names)]


def _finish_reduce(parts, got, same_order, names):
    x, y, c = _place()
    others = [2 * px + py for px, py in _other_chips(x, y)]
    own_first = jnp.stack([2 * x + y] + others + [c]).astype(jnp.int32)
    chip_order = jnp.stack([0 * c, 0 * c + 1, 0 * c + 2, 0 * c + 3, c]).astype(jnp.int32)
    halves = [_sum_chips(chip_order, q, q, "sum_chips_" + nm) if fixed else _sum_chips(own_first, p, q, "sum_chips_" + nm)
              for p, q, fixed, nm in zip(parts, got, same_order, names)]
    return [f.reshape(2 * f.shape[1], D) for f in _join_halves(halves)]


def _adamw(w, g, m, v, name):
    rows, cols = w.shape
    if rows % 8 == 0:
        tr = max(t for t in range(8, 257, 8) if rows % t == 0)
        grid, blk = (rows // tr,), pl.BlockSpec((tr, cols), lambda i: (i, 0))
    else:
        grid, blk = (cols // 256,), pl.BlockSpec((rows, 256), lambda i: (0, i))

    def body(w_ref, g_ref, m_ref, v_ref, d_ref, nm_ref, nv_ref):
        gg = g_ref[...]
        nm = ADAM_B1 * m_ref[...] + (1.0 - ADAM_B1) * gg
        nv = ADAM_B2 * v_ref[...] + (1.0 - ADAM_B2) * (gg * gg)
        m_hat = nm / (1.0 - ADAM_B1 ** ADAM_STEP)
        v_hat = nv / (1.0 - ADAM_B2 ** ADAM_STEP)
        d_ref[...] = -ADAM_LR * (m_hat / (jnp.sqrt(v_hat) + ADAM_EPS) + ADAM_WD * w_ref[...])
        nm_ref[...] = nm
        nv_ref[...] = nv

    return pl.pallas_call(
        body, name=name, grid=grid,
        in_specs=[blk] * 4, out_specs=[blk] * 3,
        out_shape=[pltpu.HBM(w.shape, F32)] * 3,
        compiler_params=_params(32, dimension_semantics=_seq()),
    )(*_hbm(w, g, m, v))


def _flat_rows(v, rows):
    flat = v.reshape(-1).astype(F32)
    return jnp.pad(flat, (0, rows * D - flat.shape[0])).reshape(rows, D)


def _small_pack(gr):
    tail = jnp.concatenate([gr["bg2"].reshape(-1), gr["sinks"].reshape(-1), gr["gn"].reshape(-1)])
    rows = [gr["meta"].reshape(N_META, D)] + [gr[k].reshape(1, D) for k in
                                              ("ln_in_g", "ln_in_b", "ln1_g", "ln1_b", "ln2_g", "ln2_b")]
    rows += [_flat_rows(gr["b_in"], 3), _flat_rows(gr["wg2"], 4), _flat_rows(tail, 1)]
    packed = jnp.concatenate(rows, axis=0)
    return jnp.pad(packed, ((0, SMALL_ROWS - packed.shape[0]), (0, 0)))


def _small_unpack(p):
    flat = lambda r0, n, size: p[r0:r0 + n].reshape(-1)[:size]
    tail = p[29]
    return dict(meta=p[0:N_META], ln_in_g=p[16], ln_in_b=p[17], ln1_g=p[18], ln1_b=p[19], ln2_g=p[20], ln2_b=p[21],
                b_in=flat(22, 3, D_IN), wg2=flat(25, 4, 16 * 256).reshape(16, 256), bg2=tail[0:256],
                sinks=tail[256:256 + SWA_HEADS], gn=tail[256 + SWA_HEADS:256 + SWA_HEADS + DV])


BIG = ("w_in", "w_out", "w_g", "w_u", "w_d")


def kernel(x, meta_tokens, ln_in_g, ln_in_b, w_in, b_in, w_gate_lr2, b_gate_lr2, attn_sinks, gla_norm_g, w_out, ln1_g, ln1_b, w_ffn_gate, w_ffn_up, w_ffn_down, ln2_g, ln2_b, loss_target, m_meta_tokens, m_ln_in_g, m_ln_in_b, m_w_in, m_b_in, m_w_gate_lr2, m_b_gate_lr2, m_attn_sinks, m_gla_norm_g, m_w_out, m_ln1_g, m_ln1_b, m_w_ffn_gate, m_w_ffn_up, m_w_ffn_down, m_ln2_g, m_ln2_b, v_meta_tokens, v_ln_in_g, v_ln_in_b, v_w_in, v_b_in, v_w_gate_lr2, v_b_gate_lr2, v_attn_sinks, v_gla_norm_g, v_w_out, v_ln1_g, v_ln1_b, v_w_ffn_gate, v_w_ffn_up, v_w_ffn_down, v_ln2_g, v_ln2_b):
    chip = 2 * lax.axis_index("x") + lax.axis_index("y")

    halves = lambda a: a.reshape(2, a.shape[0] // 2, a.shape[1])
    r_in = SHARD_ROWS["w_in"]
    first = [jnp.pad(w_in[0].T.astype(BF16), ((0, W_IN_WIN - r_in), (0, 0))), meta_tokens, w_gate_lr2[0]]
    g_in, g_meta, g_wg2 = _gather_halves([halves(a) for a in first])
    rest = [halves(a) for a in (w_out[0].astype(BF16), w_ffn_gate[0].T.astype(BF16), w_ffn_up[0].T.astype(BF16),
                                w_ffn_down[0].astype(BF16))]
    rest_handle, token = _ici_start("gather", rest, [(N_CHIPS,) + a.shape for a in rest], "gather_rest_start")
    w_in_t = jnp.pad(g_in.reshape(N_CHIPS, W_IN_WIN, D)[:, :r_in].reshape(D_IN, D), ((0, D_IN_P - D_IN), (0, 0)))
    meta_full = jnp.concatenate([g_meta[s].reshape(N_META, -1) for s in range(N_CHIPS)], axis=1)
    wg2_full = jnp.concatenate([g_wg2[s].reshape(w_gate_lr2.shape[1], -1) for s in range(N_CHIPS)], axis=1)

    def fetch_rest(after):
        lands = _sibling_forward(_ici_wait("gather", rest_handle, after, "gather_rest_wait"))
        return [lax.dynamic_update_index_in_dim(g, s, chip, axis=0).reshape(-1, D) for g, s in zip(lands, rest)]

    ffn_names = list(BIG[1:])
    sent = {}

    def ship_ffn(g):
        sent["parts"] = _chip_partials([g[k].reshape(N_CHIPS, 2, -1, D) for k in ffn_names], [BF16] * 4, ffn_names)
        sent["handle"], ffn_token = _ici_start("scatter", sent["parts"], [p.shape for p in sent["parts"]],
                                               "scatter_ffn_start")
        return ffn_token

    loss_part, dx, gr = _local_step(
        x[0], loss_target[0], meta_full, ln_in_g, ln_in_b, w_in_t, b_in[0], wg2_full, b_gate_lr2[0], attn_sinks[0],
        gla_norm_g[0], ln1_g[0], ln1_b[0], ln2_g[0], ln2_b[0], token, fetch_rest, ship_ffn)
    loss = lax.psum(loss_part, ("x", "y", "c"))
    ffn_got = _ici_wait("scatter", sent["handle"], [dx], "scatter_ffn_wait")

    win_start = [s * r_in // BF16_ROWS * BF16_ROWS for s in range(N_CHIPS)]
    last = [jnp.stack([gr["w_in"][st:st + W_IN_WIN] for st in win_start]),
            jnp.broadcast_to(_small_pack(gr), (N_CHIPS, SMALL_ROWS, D))]
    last_parts = _chip_partials([a.reshape(N_CHIPS, 2, -1, D) for a in last], [BF16, F32], ["w_in", "small"])
    last_got = _chip_scatter(last_parts, [False, True])
    red = _finish_reduce([last_parts[0]] + sent["parts"] + [last_parts[1]], [last_got[0]] + ffn_got + [last_got[1]],
                         [False] * len(BIG) + [True], list(BIG) + ["small"])

    big_g = dict(zip(BIG, red))
    big_g["w_in"] = lax.dynamic_slice_in_dim(red[0], chip * (r_in % BF16_ROWS), r_in, axis=0)
    sg = _small_unpack(red[-1])
    col = lambda a, width: lax.dynamic_slice_in_dim(a, chip * width, width, axis=1)
    grads = dict(
        meta_tokens=col(sg["meta"], D // N_CHIPS), ln_in_g=sg["ln_in_g"], ln_in_b=sg["ln_in_b"],
        w_in=big_g["w_in"].T[None], b_in=sg["b_in"][None], w_gate_lr2=col(sg["wg2"], 256 // N_CHIPS)[None],
        b_gate_lr2=sg["bg2"][None], attn_sinks=sg["sinks"][None], gla_norm_g=sg["gn"][None],
        w_out=big_g["w_out"][None], ln1_g=sg["ln1_g"][None], ln1_b=sg["ln1_b"][None],
        w_ffn_gate=big_g["w_g"].T[None], w_ffn_up=big_g["w_u"].T[None], w_ffn_down=big_g["w_d"][None],
        ln2_g=sg["ln2_g"][None], ln2_b=sg["ln2_b"][None])
    weights = dict(meta_tokens=meta_tokens, ln_in_g=ln_in_g, ln_in_b=ln_in_b, w_in=w_in, b_in=b_in,
                   w_gate_lr2=w_gate_lr2, b_gate_lr2=b_gate_lr2, attn_sinks=attn_sinks, gla_norm_g=gla_norm_g,
                   w_out=w_out, ln1_g=ln1_g, ln1_b=ln1_b, w_ffn_gate=w_ffn_gate, w_ffn_up=w_ffn_up,
                   w_ffn_down=w_ffn_down, ln2_g=ln2_g, ln2_b=ln2_b)
    m_in = dict(meta_tokens=m_meta_tokens, ln_in_g=m_ln_in_g, ln_in_b=m_ln_in_b, w_in=m_w_in, b_in=m_b_in,
                w_gate_lr2=m_w_gate_lr2, b_gate_lr2=m_b_gate_lr2, attn_sinks=m_attn_sinks, gla_norm_g=m_gla_norm_g,
                w_out=m_w_out, ln1_g=m_ln1_g, ln1_b=m_ln1_b, w_ffn_gate=m_w_ffn_gate, w_ffn_up=m_w_ffn_up,
                w_ffn_down=m_w_ffn_down, ln2_g=m_ln2_g, ln2_b=m_ln2_b)
    v_in = dict(meta_tokens=v_meta_tokens, ln_in_g=v_ln_in_g, ln_in_b=v_ln_in_b, w_in=v_w_in, b_in=v_b_in,
                w_gate_lr2=v_w_gate_lr2, b_gate_lr2=v_b_gate_lr2, attn_sinks=v_attn_sinks, gla_norm_g=v_gla_norm_g,
                w_out=v_w_out, ln1_g=v_ln1_g, ln1_b=v_ln1_b, w_ffn_gate=v_w_ffn_gate, w_ffn_up=v_w_ffn_up,
                w_ffn_down=v_w_ffn_down, ln2_g=v_ln2_g, ln2_b=v_ln2_b)
    names = list(weights)
    big_names = ("w_in", "w_out", "w_ffn_gate", "w_ffn_up", "w_ffn_down")

    delta, new_m, new_v = {}, {}, {}
    for k, kk in zip(big_names, BIG):
        flip = (lambda a: a.T) if kk in ("w_in", "w_g", "w_u") else (lambda a: a)
        d_, m_, v_ = _adamw(flip(weights[k][0]), big_g[kk], flip(m_in[k][0]), flip(v_in[k][0]), "adamw_" + k)
        delta[k], new_m[k], new_v[k] = (flip(t)[None] for t in (d_, m_, v_))
    small_names = [k for k in names if k not in big_names]
    sizes = [weights[k].size for k in small_names]
    rows_small = -(-sum(sizes) // D)
    rows_small += -rows_small % 8
    cat = lambda src: _flat_rows(jnp.concatenate([src[k].reshape(-1) for k in small_names]), rows_small)
    d_, m_, v_ = _adamw(cat(weights), cat(grads), cat(m_in), cat(v_in), "adamw_small")
    off = 0
    for k, n in zip(small_names, sizes):
        for dst, src in ((delta, d_), (new_m, m_), (new_v, v_)):
            dst[k] = src.reshape(-1)[off:off + n].reshape(weights[k].shape)
        off += n
    grads = {k: grads[k].reshape(weights[k].shape) for k in names}

    return (loss, dx[None], *[grads[k] for k in names], *[delta[k] for k in names], *[new_m[k] for k in names],
            *[new_v[k] for k in names])
```

```python
import functools

import jax
import jax.numpy as jnp
from jax import lax
from jax.experimental import pallas as pl
from jax.experimental.pallas import tpu as pltpu

F32 = jnp.float32
BF16 = jnp.bfloat16
MESH = pl.DeviceIdType.MESH

D = 1024
SEQ = 4096
N_META = 16
SWA_HEADS, SWA_KV_HEADS, DH = 8, 2, 64
WINDOW = 128
GLA_HEADS, DK, DV = 4, 64, 128
GLA_TAU = 16.0
CH = 64
D_FF = 2816
D_IN = 2320
LN_EPS = 1e-5
RMS_EPS = 1e-6
ALPHA = 2.0 ** 0.25
NEG = -1e30
ADAM_LR, ADAM_B1, ADAM_B2, ADAM_EPS, ADAM_WD, ADAM_STEP = 0.001, 0.9, 0.999, 1e-8, 0.01, 10
O_QS, O_KS, O_VS, O_QG, O_KG, O_VG, O_RG, O_LR = 0, 512, 640, 768, 1024, 1280, 1792, 2304

LANE = 128
BLK = WINDOW
D_IN_P = D_IN + LANE - 16
META_OFF = CH - N_META
HEAD_POS = (0, 4, 1, 5, 2, 6, 3, 7)
TOKEN = (8, LANE)
N_CHIPS = 4
SHARD_ROWS = dict(w_in=D_IN // N_CHIPS, w_out=D // N_CHIPS, w_g=D_FF // N_CHIPS, w_u=D_FF // N_CHIPS,
                  w_d=D_FF // N_CHIPS)
SMALL_ROWS = 32
BF16_ROWS = 16
W_IN_WIN = -(-SHARD_ROWS["w_in"] // (2 * BF16_ROWS)) * 2 * BF16_ROWS
VMEM_CAP_MB = 64


def _lp():
    return SEQ + BLK


def _row_tile(cap):
    lp = _lp()
    return max(t for t in range(16, cap + 1, 16) if lp % t == 0)


def _params(vmem_mb, **kw):
    assert vmem_mb <= VMEM_CAP_MB - 6
    return pltpu.CompilerParams(vmem_limit_bytes=vmem_mb << 20, **kw)


def _seq(n=1):
    return ("arbitrary",) * n


def _const(shape):
    return pl.BlockSpec(shape, lambda *_: (0,) * len(shape), pipeline_mode=pl.Buffered(1))


def _acc(shape):
    return pl.BlockSpec(shape, lambda *_: (0,) * len(shape))


def _rows(tm, width):
    return pl.BlockSpec((tm, width), lambda i: (i, 0))


def _dot(a, b):
    return jnp.dot(a.astype(BF16), b.astype(BF16), preferred_element_type=F32)


def _dot_nt(a, b):
    return lax.dot_general(a.astype(BF16), b.astype(BF16), (((1,), (1,)), ((), ())), preferred_element_type=F32)


def _dot_tn(a, b):
    return lax.dot_general(a.astype(BF16), b.astype(BF16), (((0,), (0,)), ((), ())), preferred_element_type=F32)


def _dot_exact(a, b):
    return jnp.dot(a, b, precision=lax.Precision.HIGHEST, preferred_element_type=F32)


def _ln_stats(x):
    mu = jnp.mean(x, axis=-1, keepdims=True)
    xc = x - mu
    rstd = lax.rsqrt(jnp.mean(xc * xc, axis=-1, keepdims=True) + LN_EPS)
    return xc * rstd, rstd


def _ln_bwd(dy, xhat, rstd, g):
    dxh = dy * g
    return rstd * (dxh - jnp.mean(dxh, axis=-1, keepdims=True) - xhat * jnp.mean(dxh * xhat, axis=-1, keepdims=True))


def _sigmoid(x):
    return 1.0 / (1.0 + jnp.exp(-x))


def _iota(shape, dim):
    return lax.broadcasted_iota(jnp.int32, shape, dim)


def _hbm(*arrays):
    return tuple(pltpu.with_memory_space_constraint(a, pltpu.HBM) for a in arrays)


def _ln_in_fwd(x, meta_ext, g, b, token):
    nb = SEQ // BLK

    def body(x_ref, m_ref, g_ref, b_ref, token_ref, h_ref):
        i = pl.program_id(0)
        xin = jnp.where(i < nb, x_ref[...], m_ref[...])
        xhat, _ = _ln_stats(xin)
        h_ref[...] = xhat * g_ref[...] + b_ref[...]

    return pl.pallas_call(
        body, name="ln_in_fwd", grid=(nb + 1,),
        in_specs=[pl.BlockSpec((BLK, D), lambda i: (jnp.minimum(i, nb - 1), 0)),
                  _const((BLK, D)), _const((1, D)), _const((1, D)), _const(TOKEN)],
        out_specs=_rows(BLK, D),
        out_shape=pltpu.HBM((_lp(), D), F32),
        compiler_params=_params(16, dimension_semantics=_seq()),
    )(*_hbm(x, meta_ext, g, b), token)


def _in_proj(h0, w_in_t, b_in_p, wg2_p, bg2):
    tm = _row_tile(384)
    lp = _lp()
    widths = (512, 128, 128, 256, 256, 512, 512, 128)
    offs = (O_QS, O_KS, O_VS, O_QG, O_KG, O_VG, O_RG, O_LR)

    def body(h_ref, w_ref, b_ref, wg2_ref, bg2_ref, *outs):
        proj = _dot_nt(h_ref[...], w_ref[...]) + b_ref[...]
        for pos, h in enumerate(HEAD_POS):
            outs[0][:, pos * DH:(pos + 1) * DH] = proj[:, O_QS + h * DH:O_QS + (h + 1) * DH]
        for o_ref, off, wd in zip(outs[1:8], offs[1:], widths[1:]):
            o_ref[...] = proj[:, off:off + wd]
        outs[8][...] = _dot(proj[:, O_LR:O_LR + LANE], wg2_ref[...]) + bg2_ref[...]

    return pl.pallas_call(
        body, name="in_proj", grid=(lp // tm,),
        in_specs=[_rows(tm, D), _const((D_IN_P, D)), _const((1, D_IN_P)), _const((LANE, 256)), _const((1, 256))],
        out_specs=[_rows(tm, w) for w in widths] + [_rows(tm, 256)],
        out_shape=[pltpu.HBM((lp, w), F32) for w in widths] + [pltpu.HBM((lp, 256), F32)],
        compiler_params=_params(40, dimension_semantics=_seq()),
    )(*_hbm(h0, w_in_t, b_in_p, wg2_p, bg2))


def _swa_masks(n):
    nb = SEQ // BLK
    is_meta = n == nb
    ri = _iota((BLK, BLK), 0)
    cj = _iota((BLK, BLK), 1)
    meta_col = ((cj >= META_OFF) & (cj < CH)).astype(jnp.int32)
    meta_q = meta_col * ((cj <= ri) & (ri < CH)).astype(jnp.int32)
    valid_m = jnp.where(is_meta, meta_q, meta_col) > 0
    dist_m = jnp.where(is_meta, ri - cj, n * BLK + ri + CH - cj).astype(F32)
    valid_p = jnp.where((n >= 1) & (n < nb), (cj > ri).astype(jnp.int32), 0) > 0
    dist_p = (ri + BLK - cj).astype(F32)
    valid_c = jnp.where(n < nb, (cj <= ri).astype(jnp.int32), 0) > 0
    dist_c = (ri - cj).astype(F32)
    return (dist_m, dist_p, dist_c), (valid_m, valid_p, valid_c)


def _swa_bias(n):
    dists, valids = _swa_masks(n)
    return (jnp.concatenate([-d for d in dists], axis=1),
            jnp.concatenate([jnp.where(v, 0.0, NEG) for v in valids], axis=1))


def _swa_half(ref, pos, scale=1.0):
    col = ref[:, (pos // 2) * LANE:(pos // 2 + 1) * LANE]
    lane = _iota((BLK, LANE), 1)
    mine = lane < DH if pos % 2 == 0 else lane >= DH
    return jnp.where(mine, col * scale, 0.0).astype(BF16)


def _swa_merge(even, odd):
    return jnp.where(_iota((BLK, LANE), 1) < DH, even, odd)


def _swa_softmax(t, sink):
    m = jnp.maximum(jnp.max(t, axis=-1, keepdims=True), sink)
    e = jnp.exp(t - m)
    e_sink = jnp.exp(sink - m)
    inv = 1.0 / (jnp.sum(e, axis=-1, keepdims=True) + e_sink)
    return e * inv, e_sink * inv


def _swa_kv_specs(width):
    nb = SEQ // BLK
    return [pl.BlockSpec((BLK, width), lambda n: (nb, 0)),
            pl.BlockSpec((BLK, width), lambda n: (jnp.clip(n - 1, 0, nb - 1), 0)),
            pl.BlockSpec((BLK, width), lambda n: (jnp.minimum(n, nb), 0))]


def _swa_fwd(sinks, qs, ks, vs):
    nb = SEQ // BLK
    heads = range(SWA_HEADS)

    def body(sink_ref, q_ref, km_ref, kp_ref, kc_ref, vm_ref, vp_ref, vc_ref, o_ref):
        negdist, maskbias = _swa_bias(pl.program_id(0))
        k_all = jnp.concatenate([km_ref[...], kp_ref[...], kc_ref[...]], axis=0).astype(BF16)
        v_all = jnp.concatenate([vm_ref[...], vp_ref[...], vc_ref[...]], axis=0).astype(BF16)
        q = [_swa_half(q_ref, pos, DH ** -0.5) for pos in heads]
        t = [_dot_nt(q[pos], k_all) + (2.0 ** -(HEAD_POS[pos] + 1) * negdist + maskbias) for pos in heads]
        p = [_swa_softmax(t[pos], sink_ref[HEAD_POS[pos]])[0].astype(BF16) for pos in heads]
        o = [_dot(p[pos], v_all) for pos in heads]
        for col in range(SWA_HEADS // 2):
            o_ref[:, col * LANE:(col + 1) * LANE] = _swa_merge(o[2 * col], o[2 * col + 1])

    kvw = SWA_KV_HEADS * DH
    return pl.pallas_call(
        body, name="swa_fwd", grid=(nb + 1,),
        in_specs=[pl.BlockSpec(memory_space=pltpu.SMEM), _rows(BLK, SWA_HEADS * DH)] + _swa_kv_specs(kvw) + _swa_kv_specs(kvw),
        out_specs=_rows(BLK, SWA_HEADS * DH),
        out_shape=pltpu.HBM((_lp(), SWA_HEADS * DH), F32),
        compiler_params=_params(16, dimension_semantics=_seq()),
    )(sinks, *_hbm(qs, ks, ks, ks, vs, vs, vs))


GLA_PER_STEP = BLK // CH


def _gla_block(s):
    nb = SEQ // BLK
    return jnp.where(s == 0, nb, s - 1)


def _gla_rowmask(s):
    ri = _iota((BLK, 1), 0)
    m = jnp.where(s == 0, ((ri >= META_OFF) & (ri < CH)).astype(jnp.int32), 1)
    return (m > 0).astype(F32) + jnp.zeros((BLK, 1), F32)


def _gla_chunk_masks():
    r, c = _iota((BLK, BLK), 0), _iota((BLK, BLK), 1)
    same = ((r < CH) & (c < CH)) | ((r >= CH) & (c >= CH))
    return same & (r >= c), same & (r <= c), same


def _gla_decay(z, rmask):
    log_g = (jnp.minimum(z, 0.0) - jnp.log1p(jnp.exp(-jnp.abs(z)))) * (rmask / GLA_TAU)
    lower, _, same = _gla_chunk_masks()
    return _dot_exact(lower.astype(F32), log_g), _dot_exact(same.astype(F32), log_g)


def _gla_slices(c, h):
    return slice(c * CH, (c + 1) * CH), slice(h * DK, (h + 1) * DK), slice(h * DV, (h + 1) * DV)


def _gla_fwd(qg, kg, vg, z):
    steps = SEQ // BLK + 1
    kw, vw = GLA_HEADS * DK, GLA_HEADS * DV
    pairs = [(c, h) for c in range(GLA_PER_STEP) for h in range(GLA_HEADS)]

    def body(q_ref, k_ref, v_ref, z_ref, o_ref, st_ref, st):
        s = pl.program_id(0)

        @pl.when(s == 0)
        def _():
            st[...] = jnp.zeros_like(st)

        rmask = _gla_rowmask(s)
        b, b_last = _gla_decay(z_ref[...], rmask)
        q = q_ref[...] * (rmask * DK ** -0.5)
        k = k_ref[...] * rmask
        v = v_ref[...] * rmask
        qe = q * jnp.exp(b)
        ke = k * jnp.exp(-b)
        kd = k * jnp.exp(b_last - b)
        e_last = jnp.exp(b_last)
        causal = _iota((CH, CH), 0) >= _iota((CH, CH), 1)
        a, upd, intra = {}, {}, {}
        for c, h in pairs:
            rows, ks, vs_ = _gla_slices(c, h)
            a[c, h] = jnp.where(causal, _dot_nt(qe[rows, ks], ke[rows, ks]), 0.0)
            upd[c, h] = _dot_tn(v[rows, vs_], kd[rows, ks])
        for c, h in pairs:
            rows, ks, vs_ = _gla_slices(c, h)
            intra[c, h] = _dot(a[c, h], v[rows, vs_])
        state = st[...]
        for c in range(GLA_PER_STEP):
            st_ref[0, c] = state
            for h in range(GLA_HEADS):
                rows, ks, vs_ = _gla_slices(c, h)
                o_ref[rows, vs_] = intra[c, h] + _dot_nt(qe[rows, ks], state[:, ks])
            state = state * e_last[c * CH:c * CH + 1] + jnp.concatenate([upd[c, h] for h in range(GLA_HEADS)], axis=1)
        st[...] = state

    blk = lambda w: pl.BlockSpec((BLK, w), lambda s: (_gla_block(s), 0))
    return pl.pallas_call(
        body, name="gla_fwd", grid=(steps,),
        in_specs=[blk(kw), blk(kw), blk(vw), blk(kw)],
        out_specs=[blk(vw), pl.BlockSpec((1, GLA_PER_STEP, DV, kw), lambda s: (s, 0, 0, 0))],
        out_shape=[pltpu.HBM((_lp(), vw), F32), pltpu.HBM((steps, GLA_PER_STEP, DV, kw), F32)],
        scratch_shapes=[pltpu.VMEM((DV, kw), F32)],
        compiler_params=_params(16, dimension_semantics=_seq()),
    )(*_hbm(qg, kg, vg, z))


def _post_mix(o_s, o_gla, r_g, h0, gn4, w_out, g1, b1):
    tm = _row_tile(384)
    lp = _lp()

    def body(os_ref, og_ref, r_ref, h0_ref, gn_ref, w_ref, g_ref, b_ref, o_ref, pre_ref, h1_ref):
        for pos, h in enumerate(HEAD_POS):
            o_ref[:, h * DH:(h + 1) * DH] = os_ref[:, pos * DH:(pos + 1) * DH].astype(BF16)
        for h in range(GLA_HEADS):
            hs = slice(h * DV, (h + 1) * DV)
            xg = og_ref[:, hs]
            n = xg * lax.rsqrt(jnp.mean(xg * xg, axis=-1, keepdims=True) + RMS_EPS) * gn_ref[...]
            r = r_ref[:, hs]
            o_ref[:, 512 + h * DV:512 + (h + 1) * DV] = (n * (r * _sigmoid(r))).astype(BF16)
        pre = ALPHA * h0_ref[...] + _dot(o_ref[...], w_ref[...])
        pre_ref[...] = pre
        xhat, _ = _ln_stats(pre)
        h1_ref[...] = xhat * g_ref[...] + b_ref[...]

    return pl.pallas_call(
        body, name="post_mix", grid=(lp // tm,),
        in_specs=[_rows(tm, 512), _rows(tm, 512), _rows(tm, 512), _rows(tm, D), _const((1, DV)), _const((D, D)),
                  _const((1, D)), _const((1, D))],
        out_specs=[_rows(tm, D), _rows(tm, D), _rows(tm, D)],
        out_shape=[pltpu.HBM((lp, D), BF16), pltpu.HBM((lp, D), F32),
                   pltpu.HBM((lp, D), F32)],
        compiler_params=_params(32, dimension_semantics=_seq()),
    )(*_hbm(o_s, o_gla, r_g, h0, gn4, w_out, g1, b1))


def _ffn_fwd(h1, wg_t, wu_t, wd):
    tm = _row_tile(384)
    lp = _lp()
    half = D_FF // 2

    def body(h_ref, wg_ref, wu_ref, wd_ref, g_ref, u_ref, pre_ref):
        h = h_ref[...]
        hb = h.astype(BF16)
        pre = ALPHA * h
        for j in range(2):
            cols = slice(j * half, (j + 1) * half)
            g = _dot_nt(hb, wg_ref[cols, :])
            u = _dot_nt(hb, wu_ref[cols, :])
            g_ref[:, cols] = g
            u_ref[:, cols] = u
            pre = pre + _dot(g * _sigmoid(g) * u, wd_ref[cols, :])
        pre_ref[...] = pre

    return pl.pallas_call(
        body, name="ffn_fwd", grid=(lp // tm,),
        in_specs=[_rows(tm, D), _const((D_FF, D)), _const((D_FF, D)), _const((D_FF, D))],
        out_specs=[_rows(tm, D_FF), _rows(tm, D_FF), _rows(tm, D)],
        out_shape=[pltpu.HBM((lp, D_FF), F32), pltpu.HBM((lp, D_FF), F32),
                   pltpu.HBM((lp, D), F32)],
        compiler_params=_params(56, dimension_semantics=_seq()),
    )(*_hbm(h1, wg_t, wu_t, wd))


def _ln2_loss_bwd(pre2, target, g2, b2):
    nb = SEQ // BLK

    def body(p_ref, t_ref, g_ref, b_ref, dp_ref, loss_ref, dg_ref, db_ref, acc):
        i = pl.program_id(0)

        @pl.when(i == 0)
        def _():
            acc[...] = jnp.zeros_like(acc)
            dg_ref[...] = jnp.zeros_like(dg_ref)
            db_ref[...] = jnp.zeros_like(db_ref)

        real = jnp.where(i < nb, 1.0, 0.0)
        xhat, rstd = _ln_stats(p_ref[...])
        diff = (xhat * g_ref[...] + b_ref[...] - t_ref[...]) * real
        acc[...] += jnp.sum(diff * diff, axis=0, keepdims=True)
        dy = diff * (1.0 / D)
        dp_ref[...] = _ln_bwd(dy, xhat, rstd, g_ref[...])
        dg_ref[...] += jnp.sum(dy * xhat, axis=0, keepdims=True)
        db_ref[...] += jnp.sum(dy, axis=0, keepdims=True)

        @pl.when(i == nb)
        def _():
            loss_ref[...] = jnp.zeros_like(loss_ref) + (0.5 / D) * jnp.sum(acc[...], axis=1, keepdims=True)

    return pl.pallas_call(
        body, name="ln2_loss_bwd", grid=(nb + 1,),
        in_specs=[_rows(BLK, D), pl.BlockSpec((BLK, D), lambda i: (jnp.minimum(i, nb - 1), 0)), _const((1, D)),
                  _const((1, D))],
        out_specs=[_rows(BLK, D), _acc((1, LANE)), _acc((1, D)), _acc((1, D))],
        out_shape=[pltpu.HBM((_lp(), D), F32), pltpu.HBM((1, LANE), F32),
                   pltpu.HBM((1, D), F32), pltpu.HBM((1, D), F32)],
        scratch_shapes=[pltpu.VMEM((1, D), F32)],
        compiler_params=_params(16, dimension_semantics=_seq()),
    )(*_hbm(pre2, target, g2, b2))


def _ffn_bwd(dpre2, g, u, pre1, wg_t, wu_t, wd, g1):
    tm = _row_tile(192)
    lp = _lp()

    def body(dp_ref, g_ref, u_ref, p1_ref, wg_ref, wu_ref, wd_ref, g1_ref, a_ref, dg_ref, du_ref, dp1_ref,
             dg1_ref, db1_ref):
        @pl.when(pl.program_id(0) == 0)
        def _():
            dg1_ref[...] = jnp.zeros_like(dg1_ref)
            db1_ref[...] = jnp.zeros_like(db1_ref)

        dp = dp_ref[...]
        gg, uu = g_ref[...], u_ref[...]
        sg = _sigmoid(gg)
        silu = gg * sg
        da = _dot_nt(dp, wd_ref[...])
        a_ref[...] = (silu * uu).astype(BF16)
        dgate = (da * uu * (sg * (1.0 + gg * (1.0 - sg)))).astype(BF16)
        dup = (da * silu).astype(BF16)
        dg_ref[...] = dgate
        du_ref[...] = dup
        dh1 = ALPHA * dp + _dot(dgate, wg_ref[...]) + _dot(dup, wu_ref[...])
        xhat, rstd = _ln_stats(p1_ref[...])
        dp1_ref[...] = _ln_bwd(dh1, xhat, rstd, g1_ref[...])
        dg1_ref[...] += jnp.sum(dh1 * xhat, axis=0, keepdims=True)
        db1_ref[...] += jnp.sum(dh1, axis=0, keepdims=True)

    return pl.pallas_call(
        body, name="ffn_bwd", grid=(lp // tm,),
        in_specs=[_rows(tm, D), _rows(tm, D_FF), _rows(tm, D_FF), _rows(tm, D), _const((D_FF, D)), _const((D_FF, D)),
                  _const((D_FF, D)), _const((1, D))],
        out_specs=[_rows(tm, D_FF), _rows(tm, D_FF), _rows(tm, D_FF), _rows(tm, D), _acc((1, D)), _acc((1, D))],
        out_shape=[pltpu.HBM((lp, D_FF), BF16)] * 3
        + [pltpu.HBM((lp, D), F32), pltpu.HBM((1, D), F32), pltpu.HBM((1, D), F32)],
        compiler_params=_params(52, dimension_semantics=_seq()),
    )(*_hbm(dpre2, g, u, pre1, wg_t, wu_t, wd, g1))


def _atb(a, b, name):
    lp = _lp()
    tm = _row_tile(1408)
    n, w = a.shape[1], b.shape[1]
    bw = 512 if n * w * 4 > (4 << 20) else w

    def body(a_ref, b_ref, o_ref):
        @pl.when(pl.program_id(1) == 0)
        def _():
            o_ref[...] = jnp.zeros_like(o_ref)

        o_ref[...] += _dot_tn(a_ref[...], b_ref[...])

    return pl.pallas_call(
        body, name=name, grid=(w // bw, lp // tm),
        in_specs=[pl.BlockSpec((tm, n), lambda j, k: (k, 0)), pl.BlockSpec((tm, bw), lambda j, k: (k, j))],
        out_specs=pl.BlockSpec((n, bw), lambda j, k: (0, j)),
        out_shape=pltpu.HBM((n, w), F32),
        compiler_params=_params(48, dimension_semantics=_seq(2)),
    )(*_hbm(a, b))


def _out_bwd(dpre1, w_out, o_gla, r_g, gn4, token):
    tm = _row_tile(384)
    lp = _lp()

    def body(dp_ref, w_ref, og_ref, r_ref, gn_ref, token_ref, dos_ref, dog_ref, dr_ref, dgn_ref):
        @pl.when(pl.program_id(0) == 0)
        def _():
            dgn_ref[...] = jnp.zeros_like(dgn_ref)

        do = _dot_nt(dp_ref[...], w_ref[...])
        for pos, h in enumerate(HEAD_POS):
            dos_ref[:, pos * DH:(pos + 1) * DH] = do[:, h * DH:(h + 1) * DH]
        gn = gn_ref[...]
        for h in range(GLA_HEADS):
            hs = slice(h * DV, (h + 1) * DV)
            xg = og_ref[:, hs]
            rstd = lax.rsqrt(jnp.mean(xg * xg, axis=-1, keepdims=True) + RMS_EPS)
            nx = xg * rstd
            r = r_ref[:, hs]
            sr = _sigmoid(r)
            d_o = do[:, 512 + h * DV:512 + (h + 1) * DV]
            dr_ref[:, hs] = d_o * (nx * gn) * (sr * (1.0 + r * (1.0 - sr)))
            dn = d_o * (r * sr)
            dgn_ref[...] += jnp.sum(dn * nx, axis=0, keepdims=True)
            dnx = dn * gn
            dog_ref[:, hs] = rstd * (dnx - nx * jnp.mean(dnx * nx, axis=-1, keepdims=True))

    return pl.pallas_call(
        body, name="out_bwd", grid=(lp // tm,),
        in_specs=[_rows(tm, D), _const((D, D)), _rows(tm, 512), _rows(tm, 512), _const((1, DV)), _const(TOKEN)],
        out_specs=[_rows(tm, 512), _rows(tm, 512), _rows(tm, 512), _acc((1, DV))],
        out_shape=[pltpu.HBM((lp, 512), F32)] * 3 + [pltpu.HBM((1, DV), F32)],
        compiler_params=_params(32, dimension_semantics=_seq()),
    )(*_hbm(dpre1, w_out, o_gla, r_g, gn4), token)


def _gla_bwd(qg, kg, vg, z, do_gla, st_all):
    steps = SEQ // BLK + 1
    kw, vw = GLA_HEADS * DK, GLA_HEADS * DV
    pairs = [(c, h) for c in range(GLA_PER_STEP) for h in range(GLA_HEADS)]
    heads = range(GLA_HEADS)

    def body(q_ref, k_ref, v_ref, z_ref, do_ref, st_ref, dq_ref, dk_ref, dv_ref, dz_ref, dst):
        @pl.when(pl.program_id(0) == 0)
        def _():
            dst[...] = jnp.zeros_like(dst)

        rmask = _gla_rowmask(steps - 1 - pl.program_id(0))
        zz = z_ref[...]
        b, b_last = _gla_decay(zz, rmask)
        e_b, e_nb, e_kd, e_last = jnp.exp(b), jnp.exp(-b), jnp.exp(b_last - b), jnp.exp(b_last)
        q = q_ref[...] * (rmask * DK ** -0.5)
        k = k_ref[...] * rmask
        v = v_ref[...] * rmask
        qe, ke, kd = q * e_b, k * e_nb, k * e_kd
        d_o = do_ref[...]
        causal = _iota((CH, CH), 0) >= _iota((CH, CH), 1)
        a, da, dqe, dke, dv_intra, carry = {}, {}, {}, {}, {}, {}
        for c, h in pairs:
            rows, ks, vs_ = _gla_slices(c, h)
            a[c, h] = jnp.where(causal, _dot_nt(qe[rows, ks], ke[rows, ks]), 0.0)
            da[c, h] = jnp.where(causal, _dot_nt(d_o[rows, vs_], v[rows, vs_]), 0.0)
            carry[c, h] = _dot_tn(d_o[rows, vs_], qe[rows, ks])
        for c, h in pairs:
            rows, ks, vs_ = _gla_slices(c, h)
            dqe[c, h] = _dot(d_o[rows, vs_], st_ref[0, c][:, ks]) + _dot(da[c, h], ke[rows, ks])
            dke[c, h] = _dot_tn(da[c, h], qe[rows, ks])
            dv_intra[c, h] = _dot_tn(a[c, h], d_o[rows, vs_])
        dstate = dst[...]
        dkd, db_decay = {}, {}
        for c in reversed(range(GLA_PER_STEP)):
            for h in heads:
                rows, ks, vs_ = _gla_slices(c, h)
                dkd[c, h] = _dot(v[rows, vs_], dstate[:, ks])
                dv_ref[rows, vs_] = dv_intra[c, h] + _dot_nt(kd[rows, ks], dstate[:, ks])
            chunk_last = e_last[c * CH:c * CH + 1]
            db_decay[c] = jnp.sum(dstate * st_ref[0, c], axis=0, keepdims=True) * chunk_last
            dstate = dstate * chunk_last + jnp.concatenate([carry[c, h] for h in heads], axis=1)
        dst[...] = dstate
        rows_of = lambda parts: jnp.concatenate(
            [jnp.concatenate([parts[c, h] for h in heads], axis=1) for c in range(GLA_PER_STEP)], axis=0)
        dqe_all, dke_all, dkd_all = rows_of(dqe), rows_of(dke), rows_of(dkd)
        dq_ref[...] = dqe_all * e_b * (rmask * DK ** -0.5)
        dk_ref[...] = (dke_all * e_nb + dkd_all * e_kd) * rmask
        dkd_kd = dkd_all * kd
        db = dqe_all * qe - dke_all * ke - dkd_kd
        _, upper, same = _gla_chunk_masks()
        decay_rows = jnp.concatenate([jnp.broadcast_to(db_decay[c], (CH, kw)) for c in range(GLA_PER_STEP)], axis=0)
        dlog_g = _dot_exact(upper.astype(F32), db) + _dot_exact(same.astype(F32), dkd_kd) + decay_rows
        dz_ref[...] = dlog_g * (rmask / GLA_TAU) * _sigmoid(-zz)

    blk = lambda w: pl.BlockSpec((BLK, w), lambda s: (_gla_block(steps - 1 - s), 0))
    return pl.pallas_call(
        body, name="gla_bwd", grid=(steps,),
        in_specs=[blk(kw), blk(kw), blk(vw), blk(kw), blk(vw),
                  pl.BlockSpec((1, GLA_PER_STEP, DV, kw), lambda s: (steps - 1 - s, 0, 0, 0))],
        out_specs=[blk(kw), blk(kw), blk(vw), blk(kw)],
        out_shape=[pltpu.HBM((_lp(), kw), F32), pltpu.HBM((_lp(), kw), F32),
                   pltpu.HBM((_lp(), vw), F32), pltpu.HBM((_lp(), kw), F32)],
        scratch_shapes=[pltpu.VMEM((DV, kw), F32)],
        compiler_params=_params(16, dimension_semantics=_seq()),
    )(*_hbm(qg, kg, vg, z, do_gla, st_all))


def _swa_bwd(sinks, qs, ks, vs, do_s):
    nb = SEQ // BLK
    kvw = SWA_KV_HEADS * DH
    scale = DH ** -0.5
    heads = range(SWA_HEADS)

    def body(sink_ref, q_ref, km_ref, kp_ref, kc_ref, vm_ref, vp_ref, vc_ref, do_ref,
             dq_ref, dk_ref, dv_ref, dsink_ref, carry_k, carry_v, meta_k, meta_v):
        n = pl.program_id(0)

        @pl.when(n == 0)
        def _():
            for r in (carry_k, carry_v, meta_k, meta_v):
                r[...] = jnp.zeros_like(r)
            dsink_ref[...] = jnp.zeros_like(dsink_ref)

        @pl.when(n <= nb)
        def _():
            negdist, maskbias = _swa_bias(n)
            lane = _iota((1, LANE), 1)
            k_all = jnp.concatenate([km_ref[...], kp_ref[...], kc_ref[...]], axis=0).astype(BF16)
            v_all = jnp.concatenate([vm_ref[...], vp_ref[...], vc_ref[...]], axis=0).astype(BF16)
            q = [_swa_half(q_ref, pos, scale) for pos in heads]
            d_o = [_swa_half(do_ref, pos) for pos in heads]
            t = [_dot_nt(q[pos], k_all) + (2.0 ** -(HEAD_POS[pos] + 1) * negdist + maskbias) for pos in heads]
            dp = [_dot_nt(d_o[pos], v_all) for pos in heads]
            soft = [_swa_softmax(t[pos], sink_ref[HEAD_POS[pos]]) for pos in heads]
            p = [s[0] for s in soft]
            delta = [jnp.sum(p[pos] * dp[pos], axis=-1, keepdims=True) for pos in heads]
            ds = [(p[pos] * (dp[pos] - delta[pos])).astype(BF16) for pos in heads]
            dq = [_dot(ds[pos], k_all) for pos in heads]
            for col in range(SWA_HEADS // 2):
                dq_ref[:, col * LANE:(col + 1) * LANE] = scale * _swa_merge(dq[2 * col], dq[2 * col + 1])
            dsink = jnp.zeros((1, LANE), F32)
            for pos in heads:
                dsink = dsink + jnp.where(lane == HEAD_POS[pos],
                                          -jnp.sum(soft[pos][1] * delta[pos], axis=0, keepdims=True), 0.0)
            dsink_ref[...] += dsink
            dk3 = _dot_tn(jnp.concatenate(q, axis=0), jnp.concatenate(ds, axis=0)).T
            dv3 = _dot_tn(jnp.concatenate(d_o, axis=0), jnp.concatenate([x.astype(BF16) for x in p], axis=0)).T
            meta_k[...] += dk3[0:BLK]
            meta_v[...] += dv3[0:BLK]
            dk_ref[...] = carry_k[...] + dk3[BLK:2 * BLK]
            dv_ref[...] = carry_v[...] + dv3[BLK:2 * BLK]
            carry_k[...] = dk3[2 * BLK:3 * BLK]
            carry_v[...] = dv3[2 * BLK:3 * BLK]

        @pl.when(n == nb + 1)
        def _():
            dk_ref[...] = meta_k[...]
            dv_ref[...] = meta_v[...]

    kv_out = pl.BlockSpec((BLK, kvw), lambda n: (jnp.where(n == nb + 1, nb, jnp.clip(n - 1, 0, nb - 1)), 0))
    qblk = pl.BlockSpec((BLK, SWA_HEADS * DH), lambda n: (jnp.minimum(n, nb), 0))
    return pl.pallas_call(
        body, name="swa_bwd", grid=(nb + 2,),
        in_specs=[pl.BlockSpec(memory_space=pltpu.SMEM), qblk] + _swa_kv_specs(kvw) + _swa_kv_specs(kvw) + [qblk],
        out_specs=[qblk, kv_out, kv_out, _acc((1, LANE))],
        out_shape=[pltpu.HBM((_lp(), SWA_HEADS * DH), F32), pltpu.HBM((_lp(), kvw), F32),
                   pltpu.HBM((_lp(), kvw), F32), pltpu.HBM((1, LANE), F32)],
        scratch_shapes=[pltpu.VMEM((BLK, kvw), F32)] * 4,
        compiler_params=_params(16, dimension_semantics=_seq()),
    )(sinks, *_hbm(qs, ks, ks, ks, vs, vs, vs, do_s))


def _in_bwd(dqs, dks, dvs, dqg, dkg, dvg, drg, dz, dpre1, w_in_t, wg2_p):
    tm = _row_tile(384)
    lp = _lp()
    widths = (512, 128, 128, 256, 256, 512, 512)
    offs = (O_QS, O_KS, O_VS, O_QG, O_KG, O_VG, O_RG)

    def body(*refs):
        parts, (dz_ref, dp1_ref, w_ref, wg2_ref, dproj_ref, dh0_ref, dbin_ref, dbg_ref) = refs[:7], refs[7:]

        @pl.when(pl.program_id(0) == 0)
        def _():
            dbin_ref[...] = jnp.zeros_like(dbin_ref)
            dbg_ref[...] = jnp.zeros_like(dbg_ref)

        for pos, h in enumerate(HEAD_POS):
            val = parts[0][:, pos * DH:(pos + 1) * DH]
            dproj_ref[:, O_QS + h * DH:O_QS + (h + 1) * DH] = val.astype(BF16)
            dbin_ref[:, O_QS + h * DH:O_QS + (h + 1) * DH] += jnp.sum(val, axis=0, keepdims=True)
        for p_ref, off, wd in zip(parts[1:], offs[1:], widths[1:]):
            val = p_ref[...]
            dproj_ref[:, off:off + wd] = val.astype(BF16)
            dbin_ref[:, off:off + wd] += jnp.sum(val, axis=0, keepdims=True)
        dz = dz_ref[...]
        dlr = _dot_nt(dz, wg2_ref[...])
        dproj_ref[:, O_LR:O_LR + LANE] = dlr.astype(BF16)
        dbin_ref[:, O_LR:O_LR + LANE] += jnp.sum(dlr, axis=0, keepdims=True)
        dbg_ref[...] += jnp.sum(dz, axis=0, keepdims=True)
        dh0_ref[...] = ALPHA * dp1_ref[...] + _dot(dproj_ref[...], w_ref[...])

    return pl.pallas_call(
        body, name="in_bwd", grid=(lp // tm,),
        in_specs=[_rows(tm, w) for w in widths] + [_rows(tm, 256), _rows(tm, D), _const((D_IN_P, D)), _const((LANE, 256))],
        out_specs=[_rows(tm, D_IN_P), _rows(tm, D), _acc((1, D_IN_P)), _acc((1, 256))],
        out_shape=[pltpu.HBM((lp, D_IN_P), BF16), pltpu.HBM((lp, D), F32),
                   pltpu.HBM((1, D_IN_P), F32), pltpu.HBM((1, 256), F32)],
        compiler_params=_params(40, dimension_semantics=_seq()),
    )(*_hbm(dqs, dks, dvs, dqg, dkg, dvg, drg, dz, dpre1, w_in_t, wg2_p))


def _ln_in_bwd(x, meta_ext, dh0, g, token):
    nb = SEQ // BLK

    def body(x_ref, m_ref, dh_ref, g_ref, token_ref, dx_ref, dm_ref, dg_ref, db_ref):
        i = pl.program_id(0)

        @pl.when(i == 0)
        def _():
            dg_ref[...] = jnp.zeros_like(dg_ref)
            db_ref[...] = jnp.zeros_like(db_ref)

        xin = jnp.where(i < nb, x_ref[...], m_ref[...])
        xhat, rstd = _ln_stats(xin)
        dh = dh_ref[...]
        dxin = _ln_bwd(dh, xhat, rstd, g_ref[...])
        dg_ref[...] += jnp.sum(dh * xhat, axis=0, keepdims=True)
        db_ref[...] += jnp.sum(dh, axis=0, keepdims=True)

        @pl.when(i < nb)
        def _():
            dx_ref[...] = dxin

        @pl.when(i == nb)
        def _():
            dm_ref[...] = dxin

    xblk = pl.BlockSpec((BLK, D), lambda i: (jnp.minimum(i, nb - 1), 0))
    return pl.pallas_call(
        body, name="ln_in_bwd", grid=(nb + 1,),
        in_specs=[xblk, _const((BLK, D)), _rows(BLK, D), _const((1, D)), _const(TOKEN)],
        out_specs=[xblk, _acc((BLK, D)), _acc((1, D)), _acc((1, D))],
        out_shape=[pltpu.HBM((SEQ, D), F32), pltpu.HBM((BLK, D), F32),
                   pltpu.HBM((1, D), F32), pltpu.HBM((1, D), F32)],
        compiler_params=_params(16, dimension_semantics=_seq()),
    )(*_hbm(x, meta_ext, dh0, g), token)


def _local_step(x, target, meta_full, ln_in_g, ln_in_b, b_in, wg2, bg2, sinks, gn, g1, b1, g2, b2,
                token, fetch_w_in, fetch_rest, ship_ffn, ship_w_in):
    row = lambda v: v.reshape(1, -1).astype(F32)
    meta_ext = jnp.pad(meta_full, ((META_OFF, BLK - CH), (0, 0)))
    b_in_p = jnp.pad(row(b_in), ((0, 0), (0, D_IN_P - D_IN)))
    wg2_p = jnp.pad(wg2, ((0, LANE - wg2.shape[0]), (0, 0))).astype(BF16)
    gn4 = row(gn)
    sinks = sinks.reshape(-1).astype(F32)

    h0 = _ln_in_fwd(x, meta_ext, row(ln_in_g), row(ln_in_b), token)
    w_in_t = fetch_w_in([h0])
    qs, ks, vs, qg, kg, vg, rg, glr, z = _in_proj(h0, w_in_t, b_in_p, wg2_p, row(bg2))
    o_s = _swa_fwd(sinks, qs, ks, vs)
    o_gla, st_all = _gla_fwd(qg, kg, vg, z)
    w_out, wg_t, wu_t, wd = fetch_rest([o_s, o_gla])
    o, pre1, h1 = _post_mix(o_s, o_gla, rg, h0, gn4, w_out, row(g1), row(b1))
    g, u, pre2 = _ffn_fwd(h1, wg_t, wu_t, wd)

    dpre2, loss, dg2, db2 = _ln2_loss_bwd(pre2, target, row(g2), row(b2))
    a, dgate, dup, dpre1, dg1, db1 = _ffn_bwd(dpre2, g, u, pre1, wg_t, wu_t, wd, row(g1))
    dwd = _atb(a, dpre2, "dw_down")
    dwg_t = _atb(dgate, h1, "dw_gate")
    dwu_t = _atb(dup, h1, "dw_up")
    dw_out = _atb(o, dpre1, "dw_out")
    token = ship_ffn(dict(w_out=dw_out, w_g=dwg_t, w_u=dwu_t, w_d=dwd))
    do_s, do_gla, drg, dgn = _out_bwd(dpre1, w_out, o_gla, rg, gn4, token)
    dqg, dkg, dvg, dz = _gla_bwd(qg, kg, vg, z, do_gla, st_all)
    dqs, dks, dvs, dsinks = _swa_bwd(sinks, qs, ks, vs, do_s)
    dproj, dh0, db_in_p, dbg2 = _in_bwd(dqs, dks, dvs, dqg, dkg, dvg, drg, dz, dpre1, w_in_t, wg2_p)
    token = ship_w_in(_atb(dproj, h0, "dw_in"))
    dwg2_p = _atb(glr, dz, "dw_gate_lr2")
    dx, dmeta_blk, dg_in, db_in_ln = _ln_in_bwd(x, meta_ext, dh0, row(ln_in_g), token)

    grads = dict(
        meta=dmeta_blk[META_OFF:CH], ln_in_g=dg_in, ln_in_b=db_in_ln, ln1_g=dg1, ln1_b=db1, ln2_g=dg2, ln2_b=db2,
        b_in=db_in_p[:, :D_IN], wg2=dwg2_p[:wg2.shape[0]], bg2=dbg2, sinks=dsinks[:, :SWA_HEADS], gn=dgn)
    return loss[0, 0], dx, grads


HBM = pl.BlockSpec(memory_space=pltpu.HBM)


def _place():
    return lax.axis_index("x"), lax.axis_index("y"), lax.axis_index("c")


def _other_chips(x, y):
    return [(1 - x, y), (x, 1 - y), (1 - x, 1 - y)]


def _dma_sems(n):
    return pltpu.SemaphoreType.DMA((n,))


def _comm_params():
    return pltpu.CompilerParams(has_side_effects=True)


def _gather_halves(shards):
    n = len(shards)

    def body(*refs):
        ins, outs = refs[:n], refs[n:2 * n]
        ici_send, ici_recv, d2d_send, d2d_recv = refs[2 * n:]
        x, y, c = _place()
        mine = 2 * x + y
        chips = _other_chips(x, y)

        def ici(a, j, src_chip):
            px, py = chips[j]
            return pltpu.make_async_remote_copy(ins[a].at[c], outs[a].at[src_chip, c], ici_send.at[3 * a + j],
                                                ici_recv.at[3 * a + j], device_id=(px, py, c), device_id_type=MESH)

        def d2d(a, j, half):
            px, py = chips[j]
            blk = outs[a].at[2 * px + py, half]
            return pltpu.make_async_remote_copy(blk, blk, d2d_send.at[3 * a + j], d2d_recv.at[3 * a + j],
                                                device_id=(x, y, 1 - c), device_id_type=MESH)

        sends = [ici(a, j, mine) for a in range(n) for j in range(3)]
        for cp in sends:
            cp.start()
        passed = []
        for a in range(n):
            for j, (px, py) in enumerate(chips):
                ici(a, j, 2 * px + py).wait_recv()
                fwd = d2d(a, j, c)
                fwd.start()
                passed.append(fwd)
        for a in range(n):
            for j in range(3):
                d2d(a, j, 1 - c).wait_recv()
        for cp in sends + passed:
            cp.wait_send()

    gathered = pl.pallas_call(
        body, name="gather_halves",
        in_specs=[HBM] * n, out_specs=[HBM] * n,
        out_shape=[pltpu.HBM((N_CHIPS,) + s.shape, s.dtype) for s in shards],
        scratch_shapes=[_dma_sems(3 * n)] * 4,
        compiler_params=_comm_params(),
    )(*_hbm(*shards))
    mine = 2 * lax.axis_index("x") + lax.axis_index("y")
    return [lax.dynamic_update_index_in_dim(g, s, mine, axis=0) for g, s in zip(gathered, shards)]


SEM = pl.BlockSpec(memory_space=pltpu.SEMAPHORE)


def _ici_copies(kind, srcs, lands, send_sems, recv_sems):
    x, y, c = _place()
    mine = 2 * x + y
    to_start, to_wait = [], []
    for a in range(len(srcs)):
        for j, (px, py) in enumerate(_other_chips(x, y)):
            peer = 2 * px + py
            if kind == "gather":
                src, there, here = srcs[a].at[c], lands[a].at[mine, c], lands[a].at[peer, c]
            else:
                src, there, here = srcs[a].at[peer], lands[a].at[mine], lands[a].at[peer]
            for dst, out in ((there, to_start), (here, to_wait)):
                out.append(pltpu.make_async_remote_copy(src, dst, send_sems.at[3 * a + j], recv_sems.at[3 * a + j],
                                                        device_id=(px, py, c), device_id_type=MESH))
    return to_start, to_wait


def _split_params():
    return pltpu.CompilerParams(has_side_effects=pltpu.SideEffectType.DATAFLOW_SIDE_EFFECTING)


def _ici_start(kind, srcs, land_shapes, after, name):
    n = len(srcs)
    lands = [pltpu.with_memory_space_constraint(lax.empty(s, a.dtype), pltpu.HBM) for s, a in zip(land_shapes, srcs)]

    def body(*refs):
        outs = refs[2 * n + len(after):]
        to_start, _ = _ici_copies(kind, refs[:n], refs[n:2 * n], outs[0], outs[1])
        for cp in to_start:
            cp.start()
        outs[-1][...] = jnp.zeros(TOKEN, F32)

    outs = pl.pallas_call(
        body, name=name, in_specs=[HBM] * (2 * n) + [pl.BlockSpec(memory_space=pl.ANY)] * len(after),
        out_specs=[SEM, SEM] + [HBM] * (2 * n) + [pl.BlockSpec(memory_space=pltpu.VMEM)],
        out_shape=[_dma_sems(3 * n)] * 2 + [pltpu.HBM(a.shape, a.dtype) for a in list(srcs) + lands]
        + [jax.ShapeDtypeStruct(TOKEN, F32)],
        input_output_aliases={i: 2 + i for i in range(2 * n)},
        compiler_params=_split_params(),
    )(*_hbm(*srcs), *lands, *after)
    return outs[:-1], outs[-1]


def _ici_wait(kind, handle, after, name):
    n = (len(handle) - 2) // 2

    def body(*refs):
        _, to_wait = _ici_copies(kind, refs[:n], refs[n:2 * n], refs[2 * n], refs[2 * n + 1])
        for cp in to_wait:
            cp.wait_send()
            cp.wait_recv()

    outs = pl.pallas_call(
        body, name=name, in_specs=[HBM] * (2 * n) + [SEM, SEM] + [pl.BlockSpec(memory_space=pl.ANY)] * len(after),
        out_specs=[HBM] * (2 * n), out_shape=[pltpu.HBM(a.shape, a.dtype) for a in handle[2:]],
        input_output_aliases={i: i for i in range(2 * n)},
        compiler_params=_split_params(),
    )(*handle[2:], handle[0], handle[1], *after)
    return list(outs[n:])


def _sibling_forward(lands, name):
    n = len(lands)

    def body(*refs):
        outs = refs[n:2 * n]
        send_sems, recv_sems = refs[2 * n:]
        x, y, c = _place()

        def copy(a, j, half):
            px, py = _other_chips(x, y)[j]
            blk = outs[a].at[2 * px + py, half]
            return pltpu.make_async_remote_copy(blk, blk, send_sems.at[3 * a + j], recv_sems.at[3 * a + j],
                                                device_id=(x, y, 1 - c), device_id_type=MESH)

        pairs = [(a, j) for a in range(n) for j in range(3)]
        for a, j in pairs:
            copy(a, j, c).start()
        for a, j in pairs:
            copy(a, j, 1 - c).wait_recv()
        for a, j in pairs:
            copy(a, j, c).wait_send()

    return pl.pallas_call(
        body, name=name, in_specs=[HBM] * n, out_specs=[HBM] * n,
        out_shape=[pltpu.HBM(a.shape, a.dtype) for a in lands],
        input_output_aliases={a: a for a in range(n)},
        scratch_shapes=[_dma_sems(3 * n)] * 2,
        compiler_params=_comm_params(),
    )(*_hbm(*lands))


def _sibling_exchange(grads, name):
    n = len(grads)

    def body(*refs):
        ins, outs = refs[:n], refs[n:2 * n]
        send_sems, recv_sems = refs[2 * n:]
        x, y, c = _place()
        copies = []
        for a in range(n):
            for s in range(N_CHIPS):
                cp = pltpu.make_async_remote_copy(ins[a].at[s, 1 - c], outs[a].at[s], send_sems.at[N_CHIPS * a + s],
                                                  recv_sems.at[N_CHIPS * a + s], device_id=(x, y, 1 - c),
                                                  device_id_type=MESH)
                cp.start()
                copies.append(cp)
        for cp in copies:
            cp.wait_recv()
        for cp in copies:
            cp.wait_send()

    return pl.pallas_call(
        body, name=name, in_specs=[HBM] * n, out_specs=[HBM] * n,
        out_shape=[pltpu.HBM((N_CHIPS, g.shape[2], D), F32) for g in grads],
        scratch_shapes=[_dma_sems(N_CHIPS * n)] * 2,
        compiler_params=_comm_params(),
    )(*_hbm(*grads))


def _add_halves(core, grad, recv, dtype, name):
    h = grad.shape[2]

    def body(c_ref, a_ref, b_ref, o_ref):
        o_ref[...] = (a_ref[0] + b_ref[...]).astype(dtype)

    return pl.pallas_call(
        body, name=name,
        grid_spec=pltpu.PrefetchScalarGridSpec(
            num_scalar_prefetch=1, grid=(N_CHIPS,),
            in_specs=[pl.BlockSpec((1, 1, h, D), lambda s, c: (s, c[0], 0, 0)),
                      pl.BlockSpec((1, h, D), lambda s, c: (s, 0, 0))],
            out_specs=pl.BlockSpec((1, h, D), lambda s, c: (s, 0, 0))),
        out_shape=pltpu.HBM((N_CHIPS, h, D), dtype),
        compiler_params=_params(16, dimension_semantics=_seq()),
    )(core, *_hbm(grad, recv))


def _chip_scatter(parts, with_own):
    n = len(parts)

    def body(*refs):
        ins, outs = refs[:n], refs[n:2 * n]
        send_sems, recv_sems, local_sems = refs[2 * n:]
        x, y, c = _place()
        mine = 2 * x + y
        chips = _other_chips(x, y)
        local = [pltpu.make_async_copy(ins[a].at[mine], outs[a].at[mine], local_sems.at[a]) for a in range(n)
                 if with_own[a]]
        for cp in local:
            cp.start()
        sends = []
        for a in range(n):
            for j, (px, py) in enumerate(chips):
                cp = pltpu.make_async_remote_copy(ins[a].at[2 * px + py], outs[a].at[mine], send_sems.at[3 * a + j],
                                                  recv_sems.at[3 * a + j], device_id=(px, py, c), device_id_type=MESH)
                cp.start()
                sends.append(cp)
        for a in range(n):
            for j, (px, py) in enumerate(chips):
                pltpu.make_async_remote_copy(ins[a].at[mine], outs[a].at[2 * px + py], send_sems.at[3 * a + j],
                                             recv_sems.at[3 * a + j], device_id=(px, py, c),
                                             device_id_type=MESH).wait_recv()
        for cp in sends:
            cp.wait_send()
        for cp in local:
            cp.wait()

    return pl.pallas_call(
        body, name="chip_scatter", in_specs=[HBM] * n, out_specs=[HBM] * n,
        out_shape=[pltpu.HBM(p.shape, p.dtype) for p in parts],
        scratch_shapes=[_dma_sems(3 * n)] * 2 + [_dma_sems(n)],
        compiler_params=_comm_params(),
    )(*_hbm(*parts))


def _sum_chips(slots, first, rest, name):
    h = first.shape[1]

    def body(i_ref, a_ref, b_ref, c_ref, d_ref, o_ref):
        o_ref[...] = ((a_ref[...].astype(F32) + b_ref[...].astype(F32)) + c_ref[...].astype(F32)) + d_ref[...].astype(F32)

    slab = lambda k: pl.BlockSpec((1, h, D), lambda i, ix: (ix[k], 0, 0))
    return pl.pallas_call(
        body, name=name,
        grid_spec=pltpu.PrefetchScalarGridSpec(num_scalar_prefetch=1, grid=(1,),
                                               in_specs=[slab(0), slab(1), slab(2), slab(3)], out_specs=slab(4)),
        out_shape=pltpu.HBM((2, h, D), F32),
        compiler_params=_params(16, dimension_semantics=_seq()),
    )(slots, *_hbm(first, rest, rest, rest))


def _join_halves(halves):
    n = len(halves)

    def body(*refs):
        outs = refs[n:2 * n]
        send_sems, recv_sems = refs[2 * n:]
        x, y, c = _place()

        def copy(a, slab):
            return pltpu.make_async_remote_copy(outs[a].at[slab], outs[a].at[slab], send_sems.at[a], recv_sems.at[a],
                                                device_id=(x, y, 1 - c), device_id_type=MESH)

        for a in range(n):
            copy(a, c).start()
        for a in range(n):
            copy(a, 1 - c).wait_recv()
        for a in range(n):
            copy(a, c).wait_send()

    return pl.pallas_call(
        body, name="join_halves", in_specs=[HBM] * n, out_specs=[HBM] * n,
        out_shape=[pltpu.HBM(h.shape, F32) for h in halves],
        input_output_aliases={a: a for a in range(n)},
        scratch_shapes=[_dma_sems(n)] * 2,
        compiler_params=_comm_params(),
    )(*_hbm(*halves))


def _chip_partials(grads, wire_dtypes, names):
    core = lax.axis_index("c").astype(jnp.int32).reshape(1)
    recv = _sibling_exchange(grads, "sibling_exchange_" + names[0])
    return [_add_halves(core, g, r, dt, "add_halves_" + nm) for g, r, dt, nm in zip(grads, recv, wire_dtypes, names)]


def _finish_reduce(parts, got, same_order, names):
    x, y, c = _place()
    others = [2 * px + py for px, py in _other_chips(x, y)]
    own_first = jnp.stack([2 * x + y] + others + [c]).astype(jnp.int32)
    chip_order = jnp.stack([0 * c, 0 * c + 1, 0 * c + 2, 0 * c + 3, c]).astype(jnp.int32)
    halves = [_sum_chips(chip_order, q, q, "sum_chips_" + nm) if fixed else _sum_chips(own_first, p, q, "sum_chips_" + nm)
              for p, q, fixed, nm in zip(parts, got, same_order, names)]
    return [f.reshape(2 * f.shape[1], D) for f in _join_halves(halves)]


def _adamw(w, g, m, v, name):
    rows, cols = w.shape
    if rows % 8 == 0:
        tr = max(t for t in range(8, 257, 8) if rows % t == 0)
        grid, blk = (rows // tr,), pl.BlockSpec((tr, cols), lambda i: (i, 0))
    else:
        grid, blk = (cols // 256,), pl.BlockSpec((rows, 256), lambda i: (0, i))

    def body(w_ref, g_ref, m_ref, v_ref, d_ref, nm_ref, nv_ref):
        gg = g_ref[...]
        nm = ADAM_B1 * m_ref[...] + (1.0 - ADAM_B1) * gg
        nv = ADAM_B2 * v_ref[...] + (1.0 - ADAM_B2) * (gg * gg)
        m_hat = nm / (1.0 - ADAM_B1 ** ADAM_STEP)
        v_hat = nv / (1.0 - ADAM_B2 ** ADAM_STEP)
        d_ref[...] = -ADAM_LR * (m_hat / (jnp.sqrt(v_hat) + ADAM_EPS) + ADAM_WD * w_ref[...])
        nm_ref[...] = nm
        nv_ref[...] = nv

    return pl.pallas_call(
        body, name=name, grid=grid,
        in_specs=[blk] * 4, out_specs=[blk] * 3,
        out_shape=[pltpu.HBM(w.shape, F32)] * 3,
        compiler_params=_params(32, dimension_semantics=_seq()),
    )(*_hbm(w, g, m, v))


def _flat_rows(v, rows):
    flat = v.reshape(-1).astype(F32)
    return jnp.pad(flat, (0, rows * D - flat.shape[0])).reshape(rows, D)


def _small_pack(gr):
    tail = jnp.concatenate([gr["bg2"].reshape(-1), gr["sinks"].reshape(-1), gr["gn"].reshape(-1)])
    rows = [gr["meta"].reshape(N_META, D)] + [gr[k].reshape(1, D) for k in
                                              ("ln_in_g", "ln_in_b", "ln1_g", "ln1_b", "ln2_g", "ln2_b")]
    rows += [_flat_rows(gr["b_in"], 3), _flat_rows(gr["wg2"], 4), _flat_rows(tail, 1)]
    packed = jnp.concatenate(rows, axis=0)
    return jnp.pad(packed, ((0, SMALL_ROWS - packed.shape[0]), (0, 0)))


def _small_unpack(p):
    flat = lambda r0, n, size: p[r0:r0 + n].reshape(-1)[:size]
    tail = p[29]
    return dict(meta=p[0:N_META], ln_in_g=p[16], ln_in_b=p[17], ln1_g=p[18], ln1_b=p[19], ln2_g=p[20], ln2_b=p[21],
                b_in=flat(22, 3, D_IN), wg2=flat(25, 4, 16 * 256).reshape(16, 256), bg2=tail[0:256],
                sinks=tail[256:256 + SWA_HEADS], gn=tail[256 + SWA_HEADS:256 + SWA_HEADS + DV])


BIG = ("w_in", "w_out", "w_g", "w_u", "w_d")


def kernel(x, meta_tokens, ln_in_g, ln_in_b, w_in, b_in, w_gate_lr2, b_gate_lr2, attn_sinks, gla_norm_g, w_out, ln1_g, ln1_b, w_ffn_gate, w_ffn_up, w_ffn_down, ln2_g, ln2_b, loss_target, m_meta_tokens, m_ln_in_g, m_ln_in_b, m_w_in, m_b_in, m_w_gate_lr2, m_b_gate_lr2, m_attn_sinks, m_gla_norm_g, m_w_out, m_ln1_g, m_ln1_b, m_w_ffn_gate, m_w_ffn_up, m_w_ffn_down, m_ln2_g, m_ln2_b, v_meta_tokens, v_ln_in_g, v_ln_in_b, v_w_in, v_b_in, v_w_gate_lr2, v_b_gate_lr2, v_attn_sinks, v_gla_norm_g, v_w_out, v_ln1_g, v_ln1_b, v_w_ffn_gate, v_w_ffn_up, v_w_ffn_down, v_ln2_g, v_ln2_b):
    chip = 2 * lax.axis_index("x") + lax.axis_index("y")

    halves = lambda a: a.reshape(2, a.shape[0] // 2, a.shape[1])
    r_in = SHARD_ROWS["w_in"]
    first = [halves(jnp.pad(w_in[0].T.astype(BF16), ((0, W_IN_WIN - r_in), (0, 0))))]
    rest = [halves(a) for a in (w_out[0].astype(BF16), w_ffn_gate[0].T.astype(BF16), w_ffn_up[0].T.astype(BF16),
                                w_ffn_down[0].astype(BF16))]
    lands = lambda arrs: [(N_CHIPS,) + a.shape for a in arrs]
    g_meta, g_wg2 = _gather_halves([halves(meta_tokens), halves(w_gate_lr2[0])])
    first_handle, first_token = _ici_start("gather", first, lands(first), [g_meta], "gather_w_in_start")
    rest_handle, token = _ici_start("gather", rest, lands(rest), [first_token], "gather_rest_start")
    meta_full = jnp.concatenate([g_meta[s].reshape(N_META, -1) for s in range(N_CHIPS)], axis=1)
    wg2_full = jnp.concatenate([g_wg2[s].reshape(w_gate_lr2.shape[1], -1) for s in range(N_CHIPS)], axis=1)

    def fetch(handle, shards, after, name):
        got = _sibling_forward(_ici_wait("gather", handle, after, name + "_wait"), name + "_forward")
        return [lax.dynamic_update_index_in_dim(g, s, chip, axis=0) for g, s in zip(got, shards)]

    def fetch_w_in(after):
        g_in, = fetch(first_handle, first, after, "gather_w_in")
        return jnp.pad(g_in.reshape(N_CHIPS, W_IN_WIN, D)[:, :r_in].reshape(D_IN, D), ((0, D_IN_P - D_IN), (0, 0)))

    def fetch_rest(after):
        return [g.reshape(-1, D) for g in fetch(rest_handle, rest, after, "gather_rest")]

    sent = {}

    def ship(key, grads, names):
        parts = _chip_partials([g.reshape(N_CHIPS, 2, -1, D) for g in grads], [BF16] * len(grads), names)
        handle, ship_token = _ici_start("scatter", parts, [p.shape for p in parts], [], "scatter_" + key + "_start")
        sent[key] = (parts, handle)
        return ship_token

    def ship_ffn(g):
        return ship("ffn", [g[k] for k in BIG[1:]], list(BIG[1:]))

    def ship_w_in(dw_in_t):
        win_start = [s * r_in // BF16_ROWS * BF16_ROWS for s in range(N_CHIPS)]
        return ship("w_in", [jnp.stack([dw_in_t[st:st + W_IN_WIN] for st in win_start])], ["w_in"])

    loss_part, dx, gr = _local_step(
        x[0], loss_target[0], meta_full, ln_in_g, ln_in_b, b_in[0], wg2_full, b_gate_lr2[0], attn_sinks[0],
        gla_norm_g[0], ln1_g[0], ln1_b[0], ln2_g[0], ln2_b[0], token, fetch_w_in, fetch_rest, ship_ffn, ship_w_in)
    loss = lax.psum(loss_part, ("x", "y", "c"))
    ffn_got = _ici_wait("scatter", sent["ffn"][1], [dx], "scatter_ffn_wait")
    w_in_got = _ici_wait("scatter", sent["w_in"][1], [dx], "scatter_w_in_wait")

    small = jnp.broadcast_to(_small_pack(gr), (N_CHIPS, SMALL_ROWS, D)).reshape(N_CHIPS, 2, -1, D)
    small_parts = _chip_partials([small], [F32], ["small"])
    small_got = list(_chip_scatter(small_parts, [True]))
    red = _finish_reduce(sent["w_in"][0] + sent["ffn"][0] + small_parts, w_in_got + ffn_got + small_got,
                         [False] * len(BIG) + [True], list(BIG) + ["small"])

    big_g = dict(zip(BIG, red))
    big_g["w_in"] = lax.dynamic_slice_in_dim(red[0], chip * (r_in % BF16_ROWS), r_in, axis=0)
    sg = _small_unpack(red[-1])
    col = lambda a, width: lax.dynamic_slice_in_dim(a, chip * width, width, axis=1)
    grads = dict(
        meta_tokens=col(sg["meta"], D // N_CHIPS), ln_in_g=sg["ln_in_g"], ln_in_b=sg["ln_in_b"],
        w_in=big_g["w_in"].T[None], b_in=sg["b_in"][None], w_gate_lr2=col(sg["wg2"], 256 // N_CHIPS)[None],
        b_gate_lr2=sg["bg2"][None], attn_sinks=sg["sinks"][None], gla_norm_g=sg["gn"][None],
        w_out=big_g["w_out"][None], ln1_g=sg["ln1_g"][None], ln1_b=sg["ln1_b"][None],
        w_ffn_gate=big_g["w_g"].T[None], w_ffn_up=big_g["w_u"].T[None], w_ffn_down=big_g["w_d"][None],
        ln2_g=sg["ln2_g"][None], ln2_b=sg["ln2_b"][None])
    weights = dict(meta_tokens=meta_tokens, ln_in_g=ln_in_g, ln_in_b=ln_in_b, w_in=w_in, b_in=b_in,
                   w_gate_lr2=w_gate_lr2, b_gate_lr2=b_gate_lr2, attn_sinks=attn_sinks, gla_norm_g=gla_norm_g,
                   w_out=w_out, ln1_g=ln1_g, ln1_b=ln1_b, w_ffn_gate=w_ffn_gate, w_ffn_up=w_ffn_up,
                   w_ffn_down=w_ffn_down, ln2_g=ln2_g, ln2_b=ln2_b)
    m_in = dict(meta_tokens=m_meta_tokens, ln_in_g=m_ln_in_g, ln_in_b=m_ln_in_b, w_in=m_w_in, b_in=m_b_in,
                w_gate_lr2=m_w_gate_lr2, b_gate_lr2=m_b_gate_lr2, attn_sinks=m_attn_sinks, gla_norm_g=m_gla_norm_g,
                w_out=m_w_out, ln1_g=m_ln1_g, ln1_b=m_ln1_b, w_ffn_gate=m_w_ffn_gate, w_ffn_up=m_w_ffn_up,
                w_ffn_down=m_w_ffn_down, ln2_g=m_ln2_g, ln2_b=m_ln2_b)
    v_in = dict(meta_tokens=v_meta_tokens, ln_in_g=v_ln_in_g, ln_in_b=v_ln_in_b, w_in=v_w_in, b_in=v_b_in,
                w_gate_lr2=v_w_gate_lr2, b_gate_lr2=v_b_gate_lr2, attn_sinks=v_attn_sinks, gla_norm_g=v_gla_norm_g,
                w_out=v_w_out, ln1_g=v_ln1_g, ln1_b=v_ln1_b, w_ffn_gate=v_w_ffn_gate, w_ffn_up=v_w_ffn_up,
                w_ffn_down=v_w_ffn_down, ln2_g=v_ln2_g, ln2_b=v_ln2_b)
    names = list(weights)
    big_names = ("w_in", "w_out", "w_ffn_gate", "w_ffn_up", "w_ffn_down")

    delta, new_m, new_v = {}, {}, {}
    for k, kk in zip(big_names, BIG):
        flip = (lambda a: a.T) if kk in ("w_in", "w_g", "w_u") else (lambda a: a)
        d_, m_, v_ = _adamw(flip(weights[k][0]), big_g[kk], flip(m_in[k][0]), flip(v_in[k][0]), "adamw_" + k)
        delta[k], new_m[k], new_v[k] = (flip(t)[None] for t in (d_, m_, v_))
    small_names = [k for k in names if k not in big_names]
    sizes = [weights[k].size for k in small_names]
    rows_small = -(-sum(sizes) // D)
    rows_small += -rows_small % 8
    cat = lambda src: _flat_rows(jnp.concatenate([src[k].reshape(-1) for k in small_names]), rows_small)
    d_, m_, v_ = _adamw(cat(weights), cat(grads), cat(m_in), cat(v_in), "adamw_small")
    off = 0
    for k, n in zip(small_names, sizes):
        for dst, src in ((delta, d_), (new_m, m_), (new_v, v_)):
            dst[k] = src.reshape(-1)[off:off + n].reshape(weights[k].shape)
        off += n
    grads = {k: grads[k].reshape(weights[k].shape) for k in names}

    return (loss, dx[None], *[grads[k] for k in names], *[delta[k] for k in names], *[new_m[k] for k in names],
            *[new_v[k] for k in names])
```

```python
import functools

import jax
import jax.numpy as jnp
from jax import lax
from jax.experimental import pallas as pl
from jax.experimental.pallas import tpu as pltpu

F32 = jnp.float32
BF16 = jnp.bfloat16
MESH = pl.DeviceIdType.MESH

D = 1024
SEQ = 4096
N_META = 16
SWA_HEADS, SWA_KV_HEADS, DH = 8, 2, 64
WINDOW = 128
GLA_HEADS, DK, DV = 4, 64, 128
GLA_TAU = 16.0
CH = 64
D_FF = 2816
D_IN = 2320
LN_EPS = 1e-5
RMS_EPS = 1e-6
ALPHA = 2.0 ** 0.25
NEG = -1e30
ADAM_LR, ADAM_B1, ADAM_B2, ADAM_EPS, ADAM_WD, ADAM_STEP = 0.001, 0.9, 0.999, 1e-8, 0.01, 10
O_QS, O_KS, O_VS, O_QG, O_KG, O_VG, O_RG, O_LR = 0, 512, 640, 768, 1024, 1280, 1792, 2304

LANE = 128
BLK = WINDOW
D_IN_P = D_IN + LANE - 16
META_OFF = CH - N_META
HEAD_POS = (0, 4, 1, 5, 2, 6, 3, 7)
LN_ROWS = 512
TOKEN = (8, LANE)
N_CHIPS = 4
SHARD_ROWS = dict(w_in=D_IN // N_CHIPS, w_out=D // N_CHIPS, w_g=D_FF // N_CHIPS, w_u=D_FF // N_CHIPS,
                  w_d=D_FF // N_CHIPS)
SMALL_ROWS = 32
BF16_ROWS = 16
W_IN_WIN = -(-SHARD_ROWS["w_in"] // (2 * BF16_ROWS)) * 2 * BF16_ROWS
VMEM_CAP_MB = 64


def _lp():
    return SEQ + BLK


def _row_tile(cap):
    lp = _lp()
    return max(t for t in range(16, cap + 1, 16) if lp % t == 0)


def _params(vmem_mb, **kw):
    assert vmem_mb <= VMEM_CAP_MB - 6
    return pltpu.CompilerParams(vmem_limit_bytes=vmem_mb << 20, **kw)


def _seq(n=1):
    return ("arbitrary",) * n


def _const(shape):
    return pl.BlockSpec(shape, lambda *_: (0,) * len(shape), pipeline_mode=pl.Buffered(1))


def _acc(shape):
    return pl.BlockSpec(shape, lambda *_: (0,) * len(shape))


def _rows(tm, width):
    return pl.BlockSpec((tm, width), lambda i: (i, 0))


def _dot(a, b):
    return jnp.dot(a.astype(BF16), b.astype(BF16), preferred_element_type=F32)


def _dot_nt(a, b):
    return lax.dot_general(a.astype(BF16), b.astype(BF16), (((1,), (1,)), ((), ())), preferred_element_type=F32)


def _dot_tn(a, b):
    return lax.dot_general(a.astype(BF16), b.astype(BF16), (((0,), (0,)), ((), ())), preferred_element_type=F32)


def _dot_exact(a, b):
    return jnp.dot(a, b, precision=lax.Precision.HIGHEST, preferred_element_type=F32)


def _ln_stats(x):
    mu = jnp.mean(x, axis=-1, keepdims=True)
    xc = x - mu
    rstd = lax.rsqrt(jnp.mean(xc * xc, axis=-1, keepdims=True) + LN_EPS)
    return xc * rstd, rstd


def _ln_bwd(dy, xhat, rstd, g):
    dxh = dy * g
    return rstd * (dxh - jnp.mean(dxh, axis=-1, keepdims=True) - xhat * jnp.mean(dxh * xhat, axis=-1, keepdims=True))


def _sigmoid(x):
    return 1.0 / (1.0 + jnp.exp(-x))


def _iota(shape, dim):
    return lax.broadcasted_iota(jnp.int32, shape, dim)


def _hbm(*arrays):
    return tuple(pltpu.with_memory_space_constraint(a, pltpu.HBM) for a in arrays)


def _ln_in_fwd(x, meta_ext, g, b, token):
    tr = min(LN_ROWS, SEQ)

    def ln(x_ref, g_ref, b_ref, h_ref):
        xhat, _ = _ln_stats(x_ref[...])
        h_ref[...] = xhat * g_ref[...] + b_ref[...]

    def body(x_ref, g_ref, b_ref, token_ref, h_ref):
        ln(x_ref, g_ref, b_ref, h_ref)

    h_real = pl.pallas_call(
        body, name="ln_in_fwd", grid=(SEQ // tr,),
        in_specs=[_rows(tr, D), _const((1, D)), _const((1, D)), _const(TOKEN)],
        out_specs=_rows(tr, D),
        out_shape=pltpu.HBM((_lp(), D), F32),
        compiler_params=_params(32, dimension_semantics=_seq()),
    )(*_hbm(x, g, b), token)

    def meta_body(m_ref, g_ref, b_ref, real_ref, h_ref):
        ln(m_ref, g_ref, b_ref, h_ref)

    return pl.pallas_call(
        meta_body, name="ln_in_fwd_meta", grid=(1,),
        in_specs=[_const((BLK, D)), _const((1, D)), _const((1, D)), pl.BlockSpec(memory_space=pl.ANY)],
        out_specs=pl.BlockSpec((BLK, D), lambda i: (SEQ // BLK, 0)),
        out_shape=pltpu.HBM((_lp(), D), F32),
        input_output_aliases={3: 0},
        compiler_params=_params(16, dimension_semantics=_seq()),
    )(*_hbm(meta_ext, g, b, h_real))


def _in_proj(h0, w_in_t, b_in_p, wg2_p, bg2):
    tm = _row_tile(384)
    lp = _lp()
    widths = (512, 128, 128, 256, 256, 512, 512, 128)
    offs = (O_QS, O_KS, O_VS, O_QG, O_KG, O_VG, O_RG, O_LR)

    def body(h_ref, w_ref, b_ref, wg2_ref, bg2_ref, *outs):
        proj = _dot_nt(h_ref[...], w_ref[...]) + b_ref[...]
        for pos, h in enumerate(HEAD_POS):
            outs[0][:, pos * DH:(pos + 1) * DH] = proj[:, O_QS + h * DH:O_QS + (h + 1) * DH]
        for o_ref, off, wd in zip(outs[1:8], offs[1:], widths[1:]):
            o_ref[...] = proj[:, off:off + wd]
        outs[8][...] = _dot(proj[:, O_LR:O_LR + LANE], wg2_ref[...]) + bg2_ref[...]

    return pl.pallas_call(
        body, name="in_proj", grid=(lp // tm,),
        in_specs=[_rows(tm, D), _const((D_IN_P, D)), _const((1, D_IN_P)), _const((LANE, 256)), _const((1, 256))],
        out_specs=[_rows(tm, w) for w in widths] + [_rows(tm, 256)],
        out_shape=[pltpu.HBM((lp, w), F32) for w in widths] + [pltpu.HBM((lp, 256), F32)],
        compiler_params=_params(40, dimension_semantics=_seq()),
    )(*_hbm(h0, w_in_t, b_in_p, wg2_p, bg2))


def _swa_masks(n):
    nb = SEQ // BLK
    is_meta = n == nb
    ri = _iota((BLK, BLK), 0)
    cj = _iota((BLK, BLK), 1)
    meta_col = ((cj >= META_OFF) & (cj < CH)).astype(jnp.int32)
    meta_q = meta_col * ((cj <= ri) & (ri < CH)).astype(jnp.int32)
    valid_m = jnp.where(is_meta, meta_q, meta_col) > 0
    dist_m = jnp.where(is_meta, ri - cj, n * BLK + ri + CH - cj).astype(F32)
    valid_p = jnp.where((n >= 1) & (n < nb), (cj > ri).astype(jnp.int32), 0) > 0
    dist_p = (ri + BLK - cj).astype(F32)
    valid_c = jnp.where(n < nb, (cj <= ri).astype(jnp.int32), 0) > 0
    dist_c = (ri - cj).astype(F32)
    return (dist_m, dist_p, dist_c), (valid_m, valid_p, valid_c)


def _swa_bias(n):
    dists, valids = _swa_masks(n)
    return (jnp.concatenate([-d for d in dists], axis=1),
            jnp.concatenate([jnp.where(v, 0.0, NEG) for v in valids], axis=1))


def _swa_half(ref, pos, scale=1.0):
    col = ref[:, (pos // 2) * LANE:(pos // 2 + 1) * LANE]
    lane = _iota((BLK, LANE), 1)
    mine = lane < DH if pos % 2 == 0 else lane >= DH
    return jnp.where(mine, col * scale, 0.0).astype(BF16)


def _swa_merge(even, odd):
    return jnp.where(_iota((BLK, LANE), 1) < DH, even, odd)


def _swa_softmax(t, sink):
    m = jnp.maximum(jnp.max(t, axis=-1, keepdims=True), sink)
    e = jnp.exp(t - m)
    e_sink = jnp.exp(sink - m)
    inv = 1.0 / (jnp.sum(e, axis=-1, keepdims=True) + e_sink)
    return e * inv, e_sink * inv


def _swa_kv_specs(width):
    nb = SEQ // BLK
    return [pl.BlockSpec((BLK, width), lambda n: (nb, 0)),
            pl.BlockSpec((BLK, width), lambda n: (jnp.clip(n - 1, 0, nb - 1), 0)),
            pl.BlockSpec((BLK, width), lambda n: (jnp.minimum(n, nb), 0))]


def _swa_fwd(sinks, qs, ks, vs):
    nb = SEQ // BLK
    heads = range(SWA_HEADS)

    def body(sink_ref, q_ref, km_ref, kp_ref, kc_ref, vm_ref, vp_ref, vc_ref, o_ref):
        negdist, maskbias = _swa_bias(pl.program_id(0))
        k_all = jnp.concatenate([km_ref[...], kp_ref[...], kc_ref[...]], axis=0).astype(BF16)
        v_all = jnp.concatenate([vm_ref[...], vp_ref[...], vc_ref[...]], axis=0).astype(BF16)
        q = [_swa_half(q_ref, pos, DH ** -0.5) for pos in heads]
        t = [_dot_nt(q[pos], k_all) + (2.0 ** -(HEAD_POS[pos] + 1) * negdist + maskbias) for pos in heads]
        p = [_swa_softmax(t[pos], sink_ref[HEAD_POS[pos]])[0].astype(BF16) for pos in heads]
        o = [_dot(p[pos], v_all) for pos in heads]
        for col in range(SWA_HEADS // 2):
            o_ref[:, col * LANE:(col + 1) * LANE] = _swa_merge(o[2 * col], o[2 * col + 1])

    kvw = SWA_KV_HEADS * DH
    return pl.pallas_call(
        body, name="swa_fwd", grid=(nb + 1,),
        in_specs=[pl.BlockSpec(memory_space=pltpu.SMEM), _rows(BLK, SWA_HEADS * DH)] + _swa_kv_specs(kvw) + _swa_kv_specs(kvw),
        out_specs=_rows(BLK, SWA_HEADS * DH),
        out_shape=pltpu.HBM((_lp(), SWA_HEADS * DH), F32),
        compiler_params=_params(16, dimension_semantics=_seq()),
    )(sinks, *_hbm(qs, ks, ks, ks, vs, vs, vs))


GLA_PER_STEP = BLK // CH


def _gla_block(s):
    nb = SEQ // BLK
    return jnp.where(s == 0, nb, s - 1)


def _gla_rowmask(s):
    ri = _iota((BLK, 1), 0)
    m = jnp.where(s == 0, ((ri >= META_OFF) & (ri < CH)).astype(jnp.int32), 1)
    return (m > 0).astype(F32) + jnp.zeros((BLK, 1), F32)


def _gla_chunk_masks():
    r, c = _iota((BLK, BLK), 0), _iota((BLK, BLK), 1)
    same = ((r < CH) & (c < CH)) | ((r >= CH) & (c >= CH))
    return same & (r >= c), same & (r <= c), same


def _gla_decay(z, rmask):
    log_g = (jnp.minimum(z, 0.0) - jnp.log1p(jnp.exp(-jnp.abs(z)))) * (rmask / GLA_TAU)
    lower, _, same = _gla_chunk_masks()
    return _dot_exact(lower.astype(F32), log_g), _dot_exact(same.astype(F32), log_g)


def _gla_slices(c, h):
    return slice(c * CH, (c + 1) * CH), slice(h * DK, (h + 1) * DK), slice(h * DV, (h + 1) * DV)


def _gla_fwd(qg, kg, vg, z):
    steps = SEQ // BLK + 1
    kw, vw = GLA_HEADS * DK, GLA_HEADS * DV
    pairs = [(c, h) for c in range(GLA_PER_STEP) for h in range(GLA_HEADS)]

    def body(q_ref, k_ref, v_ref, z_ref, o_ref, st_ref, st):
        s = pl.program_id(0)

        @pl.when(s == 0)
        def _():
            st[...] = jnp.zeros_like(st)

        rmask = _gla_rowmask(s)
        b, b_last = _gla_decay(z_ref[...], rmask)
        q = q_ref[...] * (rmask * DK ** -0.5)
        k = k_ref[...] * rmask
        v = v_ref[...] * rmask
        qe = q * jnp.exp(b)
        ke = k * jnp.exp(-b)
        kd = k * jnp.exp(b_last - b)
        e_last = jnp.exp(b_last)
        causal = _iota((CH, CH), 0) >= _iota((CH, CH), 1)
        a, upd, intra = {}, {}, {}
        for c, h in pairs:
            rows, ks, vs_ = _gla_slices(c, h)
            a[c, h] = jnp.where(causal, _dot_nt(qe[rows, ks], ke[rows, ks]), 0.0)
            upd[c, h] = _dot_tn(v[rows, vs_], kd[rows, ks])
        for c, h in pairs:
            rows, ks, vs_ = _gla_slices(c, h)
            intra[c, h] = _dot(a[c, h], v[rows, vs_])
        state = st[...]
        for c in range(GLA_PER_STEP):
            st_ref[0, c] = state
            for h in range(GLA_HEADS):
                rows, ks, vs_ = _gla_slices(c, h)
                o_ref[rows, vs_] = intra[c, h] + _dot_nt(qe[rows, ks], state[:, ks])
            state = state * e_last[c * CH:c * CH + 1] + jnp.concatenate([upd[c, h] for h in range(GLA_HEADS)], axis=1)
        st[...] = state

    blk = lambda w: pl.BlockSpec((BLK, w), lambda s: (_gla_block(s), 0))
    return pl.pallas_call(
        body, name="gla_fwd", grid=(steps,),
        in_specs=[blk(kw), blk(kw), blk(vw), blk(kw)],
        out_specs=[blk(vw), pl.BlockSpec((1, GLA_PER_STEP, DV, kw), lambda s: (s, 0, 0, 0))],
        out_shape=[pltpu.HBM((_lp(), vw), F32), pltpu.HBM((steps, GLA_PER_STEP, DV, kw), F32)],
        scratch_shapes=[pltpu.VMEM((DV, kw), F32)],
        compiler_params=_params(16, dimension_semantics=_seq()),
    )(*_hbm(qg, kg, vg, z))


def _post_mix(o_s, o_gla, r_g, h0, gn4, w_out, g1, b1):
    tm = _row_tile(384)
    lp = _lp()

    def body(os_ref, og_ref, r_ref, h0_ref, gn_ref, w_ref, g_ref, b_ref, o_ref, pre_ref, h1_ref):
        for pos, h in enumerate(HEAD_POS):
            o_ref[:, h * DH:(h + 1) * DH] = os_ref[:, pos * DH:(pos + 1) * DH].astype(BF16)
        for h in range(GLA_HEADS):
            hs = slice(h * DV, (h + 1) * DV)
            xg = og_ref[:, hs]
            n = xg * lax.rsqrt(jnp.mean(xg * xg, axis=-1, keepdims=True) + RMS_EPS) * gn_ref[...]
            r = r_ref[:, hs]
            o_ref[:, 512 + h * DV:512 + (h + 1) * DV] = (n * (r * _sigmoid(r))).astype(BF16)
        pre = ALPHA * h0_ref[...] + _dot(o_ref[...], w_ref[...])
        pre_ref[...] = pre
        xhat, _ = _ln_stats(pre)
        h1_ref[...] = xhat * g_ref[...] + b_ref[...]

    return pl.pallas_call(
        body, name="post_mix", grid=(lp // tm,),
        in_specs=[_rows(tm, 512), _rows(tm, 512), _rows(tm, 512), _rows(tm, D), _const((1, DV)), _const((D, D)),
                  _const((1, D)), _const((1, D))],
        out_specs=[_rows(tm, D), _rows(tm, D), _rows(tm, D)],
        out_shape=[pltpu.HBM((lp, D), BF16), pltpu.HBM((lp, D), F32),
                   pltpu.HBM((lp, D), F32)],
        compiler_params=_params(32, dimension_semantics=_seq()),
    )(*_hbm(o_s, o_gla, r_g, h0, gn4, w_out, g1, b1))


def _ffn_fwd_loss(h1, wg_t, wu_t, wd, target, g2, b2):
    tm = _row_tile(384)
    lp = _lp()
    steps = lp // tm
    half = D_FF // 2

    def body(h_ref, wg_ref, wu_ref, wd_ref, t_ref, g2_ref, b2_ref, g_ref, u_ref, dp_ref, loss_ref, dg_ref, db_ref, acc):
        i = pl.program_id(0)

        @pl.when(i == 0)
        def _():
            acc[...] = jnp.zeros_like(acc)
            dg_ref[...] = jnp.zeros_like(dg_ref)
            db_ref[...] = jnp.zeros_like(db_ref)

        h = h_ref[...]
        hb = h.astype(BF16)
        pre = ALPHA * h
        for j in range(2):
            cols = slice(j * half, (j + 1) * half)
            g = _dot_nt(hb, wg_ref[cols, :])
            u = _dot_nt(hb, wu_ref[cols, :])
            g_ref[:, cols] = g
            u_ref[:, cols] = u
            pre = pre + _dot(g * _sigmoid(g) * u, wd_ref[cols, :])
        xhat, rstd = _ln_stats(pre)
        real = i * tm + _iota((tm, 1), 0) < SEQ
        diff = jnp.where(real, xhat * g2_ref[...] + b2_ref[...] - t_ref[...], 0.0)
        acc[...] += jnp.sum(diff * diff, axis=0, keepdims=True)
        dy = diff * (1.0 / D)
        dp_ref[...] = _ln_bwd(dy, xhat, rstd, g2_ref[...])
        dg_ref[...] += jnp.sum(dy * xhat, axis=0, keepdims=True)
        db_ref[...] += jnp.sum(dy, axis=0, keepdims=True)

        @pl.when(i == steps - 1)
        def _():
            loss_ref[...] = jnp.zeros_like(loss_ref) + (0.5 / D) * jnp.sum(acc[...], axis=1, keepdims=True)

    return pl.pallas_call(
        body, name="ffn_fwd_loss", grid=(steps,),
        in_specs=[_rows(tm, D), _const((D_FF, D)), _const((D_FF, D)), _const((D_FF, D)), _rows(tm, D), _const((1, D)),
                  _const((1, D))],
        out_specs=[_rows(tm, D_FF), _rows(tm, D_FF), _rows(tm, D), _acc((1, LANE)), _acc((1, D)), _acc((1, D))],
        out_shape=[pltpu.HBM((lp, D_FF), F32), pltpu.HBM((lp, D_FF), F32), pltpu.HBM((lp, D), F32),
                   pltpu.HBM((1, LANE), F32), pltpu.HBM((1, D), F32), pltpu.HBM((1, D), F32)],
        scratch_shapes=[pltpu.VMEM((1, D), F32)],
        compiler_params=_params(56, dimension_semantics=_seq()),
    )(*_hbm(h1, wg_t, wu_t, wd, target, g2, b2))


def _ffn_bwd(dpre2, g, u, pre1, wg_t, wu_t, wd, g1):
    tm = _row_tile(192)
    lp = _lp()

    def body(dp_ref, g_ref, u_ref, p1_ref, wg_ref, wu_ref, wd_ref, g1_ref, a_ref, dg_ref, du_ref, dp1_ref,
             dg1_ref, db1_ref):
        @pl.when(pl.program_id(0) == 0)
        def _():
            dg1_ref[...] = jnp.zeros_like(dg1_ref)
            db1_ref[...] = jnp.zeros_like(db1_ref)

        dp = dp_ref[...]
        gg, uu = g_ref[...], u_ref[...]
        sg = _sigmoid(gg)
        silu = gg * sg
        da = _dot_nt(dp, wd_ref[...])
        a_ref[...] = (silu * uu).astype(BF16)
        dgate = (da * uu * (sg * (1.0 + gg * (1.0 - sg)))).astype(BF16)
        dup = (da * silu).astype(BF16)
        dg_ref[...] = dgate
        du_ref[...] = dup
        dh1 = ALPHA * dp + _dot(dgate, wg_ref[...]) + _dot(dup, wu_ref[...])
        xhat, rstd = _ln_stats(p1_ref[...])
        dp1_ref[...] = _ln_bwd(dh1, xhat, rstd, g1_ref[...])
        dg1_ref[...] += jnp.sum(dh1 * xhat, axis=0, keepdims=True)
        db1_ref[...] += jnp.sum(dh1, axis=0, keepdims=True)

    return pl.pallas_call(
        body, name="ffn_bwd", grid=(lp // tm,),
        in_specs=[_rows(tm, D), _rows(tm, D_FF), _rows(tm, D_FF), _rows(tm, D), _const((D_FF, D)), _const((D_FF, D)),
                  _const((D_FF, D)), _const((1, D))],
        out_specs=[_rows(tm, D_FF), _rows(tm, D_FF), _rows(tm, D_FF), _rows(tm, D), _acc((1, D)), _acc((1, D))],
        out_shape=[pltpu.HBM((lp, D_FF), BF16)] * 3
        + [pltpu.HBM((lp, D), F32), pltpu.HBM((1, D), F32), pltpu.HBM((1, D), F32)],
        compiler_params=_params(52, dimension_semantics=_seq()),
    )(*_hbm(dpre2, g, u, pre1, wg_t, wu_t, wd, g1))


def _atb(a, b, name):
    lp = _lp()
    tm = _row_tile(1408)
    n, w = a.shape[1], b.shape[1]
    bw = 512 if n * w * 4 > (4 << 20) else w

    def body(a_ref, b_ref, o_ref):
        @pl.when(pl.program_id(1) == 0)
        def _():
            o_ref[...] = jnp.zeros_like(o_ref)

        o_ref[...] += _dot_tn(a_ref[...], b_ref[...])

    return pl.pallas_call(
        body, name=name, grid=(w // bw, lp // tm),
        in_specs=[pl.BlockSpec((tm, n), lambda j, k: (k, 0)), pl.BlockSpec((tm, bw), lambda j, k: (k, j))],
        out_specs=pl.BlockSpec((n, bw), lambda j, k: (0, j)),
        out_shape=pltpu.HBM((n, w), F32),
        compiler_params=_params(48, dimension_semantics=_seq(2)),
    )(*_hbm(a, b))


def _out_bwd(dpre1, w_out, o_gla, r_g, gn4, token):
    tm = _row_tile(384)
    lp = _lp()

    def body(dp_ref, w_ref, og_ref, r_ref, gn_ref, token_ref, dos_ref, dog_ref, dr_ref, dgn_ref):
        @pl.when(pl.program_id(0) == 0)
        def _():
            dgn_ref[...] = jnp.zeros_like(dgn_ref)

        do = _dot_nt(dp_ref[...], w_ref[...])
        for pos, h in enumerate(HEAD_POS):
            dos_ref[:, pos * DH:(pos + 1) * DH] = do[:, h * DH:(h + 1) * DH]
        gn = gn_ref[...]
        for h in range(GLA_HEADS):
            hs = slice(h * DV, (h + 1) * DV)
            xg = og_ref[:, hs]
            rstd = lax.rsqrt(jnp.mean(xg * xg, axis=-1, keepdims=True) + RMS_EPS)
            nx = xg * rstd
            r = r_ref[:, hs]
            sr = _sigmoid(r)
            d_o = do[:, 512 + h * DV:512 + (h + 1) * DV]
            dr_ref[:, hs] = d_o * (nx * gn) * (sr * (1.0 + r * (1.0 - sr)))
            dn = d_o * (r * sr)
            dgn_ref[...] += jnp.sum(dn * nx, axis=0, keepdims=True)
            dnx = dn * gn
            dog_ref[:, hs] = rstd * (dnx - nx * jnp.mean(dnx * nx, axis=-1, keepdims=True))

    return pl.pallas_call(
        body, name="out_bwd", grid=(lp // tm,),
        in_specs=[_rows(tm, D), _const((D, D)), _rows(tm, 512), _rows(tm, 512), _const((1, DV)), _const(TOKEN)],
        out_specs=[_rows(tm, 512), _rows(tm, 512), _rows(tm, 512), _acc((1, DV))],
        out_shape=[pltpu.HBM((lp, 512), F32)] * 3 + [pltpu.HBM((1, DV), F32)],
        compiler_params=_params(32, dimension_semantics=_seq()),
    )(*_hbm(dpre1, w_out, o_gla, r_g, gn4), token)


def _gla_bwd(qg, kg, vg, z, do_gla, st_all):
    steps = SEQ // BLK + 1
    kw, vw = GLA_HEADS * DK, GLA_HEADS * DV
    pairs = [(c, h) for c in range(GLA_PER_STEP) for h in range(GLA_HEADS)]
    heads = range(GLA_HEADS)

    def body(q_ref, k_ref, v_ref, z_ref, do_ref, st_ref, dq_ref, dk_ref, dv_ref, dz_ref, dst):
        @pl.when(pl.program_id(0) == 0)
        def _():
            dst[...] = jnp.zeros_like(dst)

        rmask = _gla_rowmask(steps - 1 - pl.program_id(0))
        zz = z_ref[...]
        b, b_last = _gla_decay(zz, rmask)
        e_b, e_nb, e_kd, e_last = jnp.exp(b), jnp.exp(-b), jnp.exp(b_last - b), jnp.exp(b_last)
        q = q_ref[...] * (rmask * DK ** -0.5)
        k = k_ref[...] * rmask
        v = v_ref[...] * rmask
        qe, ke, kd = q * e_b, k * e_nb, k * e_kd
        d_o = do_ref[...]
        causal = _iota((CH, CH), 0) >= _iota((CH, CH), 1)
        a, da, dqe, dke, dv_intra, carry = {}, {}, {}, {}, {}, {}
        for c, h in pairs:
            rows, ks, vs_ = _gla_slices(c, h)
            a[c, h] = jnp.where(causal, _dot_nt(qe[rows, ks], ke[rows, ks]), 0.0)
            da[c, h] = jnp.where(causal, _dot_nt(d_o[rows, vs_], v[rows, vs_]), 0.0)
            carry[c, h] = _dot_tn(d_o[rows, vs_], qe[rows, ks])
        for c, h in pairs:
            rows, ks, vs_ = _gla_slices(c, h)
            dqe[c, h] = _dot(d_o[rows, vs_], st_ref[0, c][:, ks]) + _dot(da[c, h], ke[rows, ks])
            dke[c, h] = _dot_tn(da[c, h], qe[rows, ks])
            dv_intra[c, h] = _dot_tn(a[c, h], d_o[rows, vs_])
        dstate = dst[...]
        dkd, db_decay = {}, {}
        for c in reversed(range(GLA_PER_STEP)):
            for h in heads:
                rows, ks, vs_ = _gla_slices(c, h)
                dkd[c, h] = _dot(v[rows, vs_], dstate[:, ks])
                dv_ref[rows, vs_] = dv_intra[c, h] + _dot_nt(kd[rows, ks], dstate[:, ks])
            chunk_last = e_last[c * CH:c * CH + 1]
            db_decay[c] = jnp.sum(dstate * st_ref[0, c], axis=0, keepdims=True) * chunk_last
            dstate = dstate * chunk_last + jnp.concatenate([carry[c, h] for h in heads], axis=1)
        dst[...] = dstate
        rows_of = lambda parts: jnp.concatenate(
            [jnp.concatenate([parts[c, h] for h in heads], axis=1) for c in range(GLA_PER_STEP)], axis=0)
        dqe_all, dke_all, dkd_all = rows_of(dqe), rows_of(dke), rows_of(dkd)
        dq_ref[...] = dqe_all * e_b * (rmask * DK ** -0.5)
        dk_ref[...] = (dke_all * e_nb + dkd_all * e_kd) * rmask
        dkd_kd = dkd_all * kd
        db = dqe_all * qe - dke_all * ke - dkd_kd
        _, upper, same = _gla_chunk_masks()
        decay_rows = jnp.concatenate([jnp.broadcast_to(db_decay[c], (CH, kw)) for c in range(GLA_PER_STEP)], axis=0)
        dlog_g = _dot_exact(upper.astype(F32), db) + _dot_exact(same.astype(F32), dkd_kd) + decay_rows
        dz_ref[...] = dlog_g * (rmask / GLA_TAU) * _sigmoid(-zz)

    blk = lambda w: pl.BlockSpec((BLK, w), lambda s: (_gla_block(steps - 1 - s), 0))
    return pl.pallas_call(
        body, name="gla_bwd", grid=(steps,),
        in_specs=[blk(kw), blk(kw), blk(vw), blk(kw), blk(vw),
                  pl.BlockSpec((1, GLA_PER_STEP, DV, kw), lambda s: (steps - 1 - s, 0, 0, 0))],
        out_specs=[blk(kw), blk(kw), blk(vw), blk(kw)],
        out_shape=[pltpu.HBM((_lp(), kw), F32), pltpu.HBM((_lp(), kw), F32),
                   pltpu.HBM((_lp(), vw), F32), pltpu.HBM((_lp(), kw), F32)],
        scratch_shapes=[pltpu.VMEM((DV, kw), F32)],
        compiler_params=_params(16, dimension_semantics=_seq()),
    )(*_hbm(qg, kg, vg, z, do_gla, st_all))


def _swa_bwd(sinks, qs, ks, vs, do_s):
    nb = SEQ // BLK
    kvw = SWA_KV_HEADS * DH
    scale = DH ** -0.5
    heads = range(SWA_HEADS)

    def body(sink_ref, q_ref, km_ref, kp_ref, kc_ref, vm_ref, vp_ref, vc_ref, do_ref,
             dq_ref, dk_ref, dv_ref, dsink_ref, carry_k, carry_v, meta_k, meta_v):
        n = pl.program_id(0)

        @pl.when(n == 0)
        def _():
            for r in (carry_k, carry_v, meta_k, meta_v):
                r[...] = jnp.zeros_like(r)
            dsink_ref[...] = jnp.zeros_like(dsink_ref)

        @pl.when(n <= nb)
        def _():
            negdist, maskbias = _swa_bias(n)
            lane = _iota((1, LANE), 1)
            k_all = jnp.concatenate([km_ref[...], kp_ref[...], kc_ref[...]], axis=0).astype(BF16)
            v_all = jnp.concatenate([vm_ref[...], vp_ref[...], vc_ref[...]], axis=0).astype(BF16)
            q = [_swa_half(q_ref, pos, scale) for pos in heads]
            d_o = [_swa_half(do_ref, pos) for pos in heads]
            t = [_dot_nt(q[pos], k_all) + (2.0 ** -(HEAD_POS[pos] + 1) * negdist + maskbias) for pos in heads]
            dp = [_dot_nt(d_o[pos], v_all) for pos in heads]
            soft = [_swa_softmax(t[pos], sink_ref[HEAD_POS[pos]]) for pos in heads]
            p = [s[0] for s in soft]
            delta = [jnp.sum(p[pos] * dp[pos], axis=-1, keepdims=True) for pos in heads]
            ds = [(p[pos] * (dp[pos] - delta[pos])).astype(BF16) for pos in heads]
            dq = [_dot(ds[pos], k_all) for pos in heads]
            for col in range(SWA_HEADS // 2):
                dq_ref[:, col * LANE:(col + 1) * LANE] = scale * _swa_merge(dq[2 * col], dq[2 * col + 1])
            dsink = jnp.zeros((1, LANE), F32)
            for pos in heads:
                dsink = dsink + jnp.where(lane == HEAD_POS[pos],
                                          -jnp.sum(soft[pos][1] * delta[pos], axis=0, keepdims=True), 0.0)
            dsink_ref[...] += dsink
            dk3 = _dot_tn(jnp.concatenate(q, axis=0), jnp.concatenate(ds, axis=0)).T
            dv3 = _dot_tn(jnp.concatenate(d_o, axis=0), jnp.concatenate([x.astype(BF16) for x in p], axis=0)).T
            meta_k[...] += dk3[0:BLK]
            meta_v[...] += dv3[0:BLK]
            dk_ref[...] = carry_k[...] + dk3[BLK:2 * BLK]
            dv_ref[...] = carry_v[...] + dv3[BLK:2 * BLK]
            carry_k[...] = dk3[2 * BLK:3 * BLK]
            carry_v[...] = dv3[2 * BLK:3 * BLK]

        @pl.when(n == nb + 1)
        def _():
            dk_ref[...] = meta_k[...]
            dv_ref[...] = meta_v[...]

    kv_out = pl.BlockSpec((BLK, kvw), lambda n: (jnp.where(n == nb + 1, nb, jnp.clip(n - 1, 0, nb - 1)), 0))
    qblk = pl.BlockSpec((BLK, SWA_HEADS * DH), lambda n: (jnp.minimum(n, nb), 0))
    return pl.pallas_call(
        body, name="swa_bwd", grid=(nb + 2,),
        in_specs=[pl.BlockSpec(memory_space=pltpu.SMEM), qblk] + _swa_kv_specs(kvw) + _swa_kv_specs(kvw) + [qblk],
        out_specs=[qblk, kv_out, kv_out, _acc((1, LANE))],
        out_shape=[pltpu.HBM((_lp(), SWA_HEADS * DH), F32), pltpu.HBM((_lp(), kvw), F32),
                   pltpu.HBM((_lp(), kvw), F32), pltpu.HBM((1, LANE), F32)],
        scratch_shapes=[pltpu.VMEM((BLK, kvw), F32)] * 4,
        compiler_params=_params(16, dimension_semantics=_seq()),
    )(sinks, *_hbm(qs, ks, ks, ks, vs, vs, vs, do_s))


def _in_bwd(dqs, dks, dvs, dqg, dkg, dvg, drg, dz, dpre1, w_in_t, wg2_p):
    tm = _row_tile(384)
    lp = _lp()
    widths = (512, 128, 128, 256, 256, 512, 512)
    offs = (O_QS, O_KS, O_VS, O_QG, O_KG, O_VG, O_RG)

    def body(*refs):
        parts, (dz_ref, dp1_ref, w_ref, wg2_ref, dproj_ref, dh0_ref, dbin_ref, dbg_ref) = refs[:7], refs[7:]

        @pl.when(pl.program_id(0) == 0)
        def _():
            dbin_ref[...] = jnp.zeros_like(dbin_ref)
            dbg_ref[...] = jnp.zeros_like(dbg_ref)

        for pos, h in enumerate(HEAD_POS):
            val = parts[0][:, pos * DH:(pos + 1) * DH]
            dproj_ref[:, O_QS + h * DH:O_QS + (h + 1) * DH] = val.astype(BF16)
            dbin_ref[:, O_QS + h * DH:O_QS + (h + 1) * DH] += jnp.sum(val, axis=0, keepdims=True)
        for p_ref, off, wd in zip(parts[1:], offs[1:], widths[1:]):
            val = p_ref[...]
            dproj_ref[:, off:off + wd] = val.astype(BF16)
            dbin_ref[:, off:off + wd] += jnp.sum(val, axis=0, keepdims=True)
        dz = dz_ref[...]
        dlr = _dot_nt(dz, wg2_ref[...])
        dproj_ref[:, O_LR:O_LR + LANE] = dlr.astype(BF16)
        dbin_ref[:, O_LR:O_LR + LANE] += jnp.sum(dlr, axis=0, keepdims=True)
        dbg_ref[...] += jnp.sum(dz, axis=0, keepdims=True)
        dh0_ref[...] = ALPHA * dp1_ref[...] + _dot(dproj_ref[...], w_ref[...])

    return pl.pallas_call(
        body, name="in_bwd", grid=(lp // tm,),
        in_specs=[_rows(tm, w) for w in widths] + [_rows(tm, 256), _rows(tm, D), _const((D_IN_P, D)), _const((LANE, 256))],
        out_specs=[_rows(tm, D_IN_P), _rows(tm, D), _acc((1, D_IN_P)), _acc((1, 256))],
        out_shape=[pltpu.HBM((lp, D_IN_P), BF16), pltpu.HBM((lp, D), F32),
                   pltpu.HBM((1, D_IN_P), F32), pltpu.HBM((1, 256), F32)],
        compiler_params=_params(40, dimension_semantics=_seq()),
    )(*_hbm(dqs, dks, dvs, dqg, dkg, dvg, drg, dz, dpre1, w_in_t, wg2_p))


def _ln_in_bwd(x, meta_ext, dh0, g, token):
    tr = min(LN_ROWS, SEQ)

    def ln_bwd(x_ref, dh_ref, g_ref, dx_ref, dg_ref, db_ref):
        @pl.when(pl.program_id(0) == 0)
        def _():
            dg_ref[...] = jnp.zeros_like(dg_ref)
            db_ref[...] = jnp.zeros_like(db_ref)

        xhat, rstd = _ln_stats(x_ref[...])
        dh = dh_ref[...]
        dx_ref[...] = _ln_bwd(dh, xhat, rstd, g_ref[...])
        dg_ref[...] += jnp.sum(dh * xhat, axis=0, keepdims=True)
        db_ref[...] += jnp.sum(dh, axis=0, keepdims=True)

    def body(x_ref, dh_ref, g_ref, token_ref, dx_ref, dg_ref, db_ref):
        ln_bwd(x_ref, dh_ref, g_ref, dx_ref, dg_ref, db_ref)

    def meta_body(m_ref, dh_ref, g_ref, dm_ref, dg_ref, db_ref):
        ln_bwd(m_ref, dh_ref, g_ref, dm_ref, dg_ref, db_ref)

    sums = [pltpu.HBM((1, D), F32), pltpu.HBM((1, D), F32)]
    dx, dg, db = pl.pallas_call(
        body, name="ln_in_bwd", grid=(SEQ // tr,),
        in_specs=[_rows(tr, D), _rows(tr, D), _const((1, D)), _const(TOKEN)],
        out_specs=[_rows(tr, D), _acc((1, D)), _acc((1, D))],
        out_shape=[pltpu.HBM((SEQ, D), F32)] + sums,
        compiler_params=_params(32, dimension_semantics=_seq()),
    )(*_hbm(x, dh0, g), token)
    dm, dg_m, db_m = pl.pallas_call(
        meta_body, name="ln_in_bwd_meta", grid=(1,),
        in_specs=[_const((BLK, D)), pl.BlockSpec((BLK, D), lambda i: (SEQ // BLK, 0)), _const((1, D))],
        out_specs=[_acc((BLK, D)), _acc((1, D)), _acc((1, D))],
        out_shape=[pltpu.HBM((BLK, D), F32)] + sums,
        compiler_params=_params(16, dimension_semantics=_seq()),
    )(*_hbm(meta_ext, dh0, g))
    return dx, dm, dg + dg_m, db + db_m


def _local_step(x, target, meta_full, ln_in_g, ln_in_b, b_in, wg2, bg2, sinks, gn, g1, b1, g2, b2,
                token, fetch_w_in, fetch_rest, ship_ffn, ship_w_in):
    row = lambda v: v.reshape(1, -1).astype(F32)
    meta_ext = jnp.pad(meta_full, ((META_OFF, BLK - CH), (0, 0)))
    b_in_p = jnp.pad(row(b_in), ((0, 0), (0, D_IN_P - D_IN)))
    wg2_p = jnp.pad(wg2, ((0, LANE - wg2.shape[0]), (0, 0))).astype(BF16)
    gn4 = row(gn)
    sinks = sinks.reshape(-1).astype(F32)

    h0 = _ln_in_fwd(x, meta_ext, row(ln_in_g), row(ln_in_b), token)
    w_in_t = fetch_w_in([h0])
    qs, ks, vs, qg, kg, vg, rg, glr, z = _in_proj(h0, w_in_t, b_in_p, wg2_p, row(bg2))
    o_s = _swa_fwd(sinks, qs, ks, vs)
    o_gla, st_all = _gla_fwd(qg, kg, vg, z)
    w_out, wg_t, wu_t, wd = fetch_rest([o_s, o_gla])
    o, pre1, h1 = _post_mix(o_s, o_gla, rg, h0, gn4, w_out, row(g1), row(b1))
    target_p = jnp.pad(target, ((0, BLK), (0, 0)))
    g, u, dpre2, loss, dg2, db2 = _ffn_fwd_loss(h1, wg_t, wu_t, wd, target_p, row(g2), row(b2))
    a, dgate, dup, dpre1, dg1, db1 = _ffn_bwd(dpre2, g, u, pre1, wg_t, wu_t, wd, row(g1))
    dwd = _atb(a, dpre2, "dw_down")
    dwg_t = _atb(dgate, h1, "dw_gate")
    dwu_t = _atb(dup, h1, "dw_up")
    dw_out = _atb(o, dpre1, "dw_out")
    token = ship_ffn(dict(w_out=dw_out, w_g=dwg_t, w_u=dwu_t, w_d=dwd))
    do_s, do_gla, drg, dgn = _out_bwd(dpre1, w_out, o_gla, rg, gn4, token)
    dqg, dkg, dvg, dz = _gla_bwd(qg, kg, vg, z, do_gla, st_all)
    dqs, dks, dvs, dsinks = _swa_bwd(sinks, qs, ks, vs, do_s)
    dproj, dh0, db_in_p, dbg2 = _in_bwd(dqs, dks, dvs, dqg, dkg, dvg, drg, dz, dpre1, w_in_t, wg2_p)
    token = ship_w_in(_atb(dproj, h0, "dw_in"))
    dwg2_p = _atb(glr, dz, "dw_gate_lr2")
    dx, dmeta_blk, dg_in, db_in_ln = _ln_in_bwd(x, meta_ext, dh0, row(ln_in_g), token)

    grads = dict(
        meta=dmeta_blk[META_OFF:CH], ln_in_g=dg_in, ln_in_b=db_in_ln, ln1_g=dg1, ln1_b=db1, ln2_g=dg2, ln2_b=db2,
        b_in=db_in_p[:, :D_IN], wg2=dwg2_p[:wg2.shape[0]], bg2=dbg2, sinks=dsinks[:, :SWA_HEADS], gn=dgn)
    return loss[0, 0], dx, grads


HBM = pl.BlockSpec(memory_space=pltpu.HBM)


def _place():
    return lax.axis_index("x"), lax.axis_index("y"), lax.axis_index("c")


def _other_chips(x, y):
    return [(1 - x, y), (x, 1 - y), (1 - x, 1 - y)]


def _dma_sems(n):
    return pltpu.SemaphoreType.DMA((n,))


def _comm_params():
    return pltpu.CompilerParams(has_side_effects=True)


def _gather_halves(shards):
    n = len(shards)

    def body(*refs):
        ins, outs = refs[:n], refs[n:2 * n]
        ici_send, ici_recv, d2d_send, d2d_recv = refs[2 * n:]
        x, y, c = _place()
        mine = 2 * x + y
        chips = _other_chips(x, y)

        def ici(a, j, src_chip):
            px, py = chips[j]
            return pltpu.make_async_remote_copy(ins[a].at[c], outs[a].at[src_chip, c], ici_send.at[3 * a + j],
                                                ici_recv.at[3 * a + j], device_id=(px, py, c), device_id_type=MESH)

        def d2d(a, j, half):
            px, py = chips[j]
            blk = outs[a].at[2 * px + py, half]
            return pltpu.make_async_remote_copy(blk, blk, d2d_send.at[3 * a + j], d2d_recv.at[3 * a + j],
                                                device_id=(x, y, 1 - c), device_id_type=MESH)

        sends = [ici(a, j, mine) for a in range(n) for j in range(3)]
        for cp in sends:
            cp.start()
        passed = []
        for a in range(n):
            for j, (px, py) in enumerate(chips):
                ici(a, j, 2 * px + py).wait_recv()
                fwd = d2d(a, j, c)
                fwd.start()
                passed.append(fwd)
        for a in range(n):
            for j in range(3):
                d2d(a, j, 1 - c).wait_recv()
        for cp in sends + passed:
            cp.wait_send()

    gathered = pl.pallas_call(
        body, name="gather_halves",
        in_specs=[HBM] * n, out_specs=[HBM] * n,
        out_shape=[pltpu.HBM((N_CHIPS,) + s.shape, s.dtype) for s in shards],
        scratch_shapes=[_dma_sems(3 * n)] * 4,
        compiler_params=_comm_params(),
    )(*_hbm(*shards))
    mine = 2 * lax.axis_index("x") + lax.axis_index("y")
    return [lax.dynamic_update_index_in_dim(g, s, mine, axis=0) for g, s in zip(gathered, shards)]


SEM = pl.BlockSpec(memory_space=pltpu.SEMAPHORE)


def _ici_copies(kind, srcs, lands, send_sems, recv_sems):
    x, y, c = _place()
    mine = 2 * x + y
    to_start, to_wait = [], []
    for a in range(len(srcs)):
        for j, (px, py) in enumerate(_other_chips(x, y)):
            peer = 2 * px + py
            if kind == "gather":
                src, there, here = srcs[a].at[c], lands[a].at[mine, c], lands[a].at[peer, c]
            else:
                src, there, here = srcs[a].at[peer], lands[a].at[mine], lands[a].at[peer]
            for dst, out in ((there, to_start), (here, to_wait)):
                out.append(pltpu.make_async_remote_copy(src, dst, send_sems.at[3 * a + j], recv_sems.at[3 * a + j],
                                                        device_id=(px, py, c), device_id_type=MESH))
    return to_start, to_wait


def _split_params():
    return pltpu.CompilerParams(has_side_effects=pltpu.SideEffectType.DATAFLOW_SIDE_EFFECTING)


def _ici_start(kind, srcs, land_shapes, after, name):
    n = len(srcs)
    lands = [pltpu.with_memory_space_constraint(lax.empty(s, a.dtype), pltpu.HBM) for s, a in zip(land_shapes, srcs)]

    def body(*refs):
        outs = refs[2 * n + len(after):]
        to_start, _ = _ici_copies(kind, refs[:n], refs[n:2 * n], outs[0], outs[1])
        for cp in to_start:
            cp.start()
        outs[-1][...] = jnp.zeros(TOKEN, F32)

    outs = pl.pallas_call(
        body, name=name, in_specs=[HBM] * (2 * n) + [pl.BlockSpec(memory_space=pl.ANY)] * len(after),
        out_specs=[SEM, SEM] + [HBM] * (2 * n) + [pl.BlockSpec(memory_space=pltpu.VMEM)],
        out_shape=[_dma_sems(3 * n)] * 2 + [pltpu.HBM(a.shape, a.dtype) for a in list(srcs) + lands]
        + [jax.ShapeDtypeStruct(TOKEN, F32)],
        input_output_aliases={i: 2 + i for i in range(2 * n)},
        compiler_params=_split_params(),
    )(*_hbm(*srcs), *lands, *after)
    return outs[:-1], outs[-1]


def _ici_wait(kind, handle, after, name):
    n = (len(handle) - 2) // 2

    def body(*refs):
        _, to_wait = _ici_copies(kind, refs[:n], refs[n:2 * n], refs[2 * n], refs[2 * n + 1])
        for cp in to_wait:
            cp.wait_send()
            cp.wait_recv()

    outs = pl.pallas_call(
        body, name=name, in_specs=[HBM] * (2 * n) + [SEM, SEM] + [pl.BlockSpec(memory_space=pl.ANY)] * len(after),
        out_specs=[HBM] * (2 * n), out_shape=[pltpu.HBM(a.shape, a.dtype) for a in handle[2:]],
        input_output_aliases={i: i for i in range(2 * n)},
        compiler_params=_split_params(),
    )(*handle[2:], handle[0], handle[1], *after)
    return list(outs[n:])


def _sibling_forward(lands, name):
    n = len(lands)

    def body(*refs):
        outs = refs[n:2 * n]
        send_sems, recv_sems = refs[2 * n:]
        x, y, c = _place()

        def copy(a, j, half):
            px, py = _other_chips(x, y)[j]
            blk = outs[a].at[2 * px + py, half]
            return pltpu.make_async_remote_copy(blk, blk, send_sems.at[3 * a + j], recv_sems.at[3 * a + j],
                                                device_id=(x, y, 1 - c), device_id_type=MESH)

        pairs = [(a, j) for a in range(n) for j in range(3)]
        for a, j in pairs:
            copy(a, j, c).start()
        for a, j in pairs:
            copy(a, j, 1 - c).wait_recv()
        for a, j in pairs:
            copy(a, j, c).wait_send()

    return pl.pallas_call(
        body, name=name, in_specs=[HBM] * n, out_specs=[HBM] * n,
        out_shape=[pltpu.HBM(a.shape, a.dtype) for a in lands],
        input_output_aliases={a: a for a in range(n)},
        scratch_shapes=[_dma_sems(3 * n)] * 2,
        compiler_params=_comm_params(),
    )(*_hbm(*lands))


def _sibling_exchange(grads, name):
    n = len(grads)

    def body(*refs):
        ins, outs = refs[:n], refs[n:2 * n]
        send_sems, recv_sems = refs[2 * n:]
        x, y, c = _place()
        copies = []
        for a in range(n):
            for s in range(N_CHIPS):
                cp = pltpu.make_async_remote_copy(ins[a].at[s, 1 - c], outs[a].at[s], send_sems.at[N_CHIPS * a + s],
                                                  recv_sems.at[N_CHIPS * a + s], device_id=(x, y, 1 - c),
                                                  device_id_type=MESH)
                cp.start()
                copies.append(cp)
        for cp in copies:
            cp.wait_recv()
        for cp in copies:
            cp.wait_send()

    return pl.pallas_call(
        body, name=name, in_specs=[HBM] * n, out_specs=[HBM] * n,
        out_shape=[pltpu.HBM((N_CHIPS, g.shape[2], D), F32) for g in grads],
        scratch_shapes=[_dma_sems(N_CHIPS * n)] * 2,
        compiler_params=_comm_params(),
    )(*_hbm(*grads))


def _add_halves(core, grad, recv, dtype, name):
    h = grad.shape[2]

    def body(c_ref, a_ref, b_ref, o_ref):
        o_ref[...] = (a_ref[0] + b_ref[...]).astype(dtype)

    return pl.pallas_call(
        body, name=name,
        grid_spec=pltpu.PrefetchScalarGridSpec(
            num_scalar_prefetch=1, grid=(N_CHIPS,),
            in_specs=[pl.BlockSpec((1, 1, h, D), lambda s, c: (s, c[0], 0, 0)),
                      pl.BlockSpec((1, h, D), lambda s, c: (s, 0, 0))],
            out_specs=pl.BlockSpec((1, h, D), lambda s, c: (s, 0, 0))),
        out_shape=pltpu.HBM((N_CHIPS, h, D), dtype),
        compiler_params=_params(16, dimension_semantics=_seq()),
    )(core, *_hbm(grad, recv))


def _chip_scatter(parts, with_own):
    n = len(parts)

    def body(*refs):
        ins, outs = refs[:n], refs[n:2 * n]
        send_sems, recv_sems, local_sems = refs[2 * n:]
        x, y, c = _place()
        mine = 2 * x + y
        chips = _other_chips(x, y)
        local = [pltpu.make_async_copy(ins[a].at[mine], outs[a].at[mine], local_sems.at[a]) for a in range(n)
                 if with_own[a]]
        for cp in local:
            cp.start()
        sends = []
        for a in range(n):
            for j, (px, py) in enumerate(chips):
                cp = pltpu.make_async_remote_copy(ins[a].at[2 * px + py], outs[a].at[mine], send_sems.at[3 * a + j],
                                                  recv_sems.at[3 * a + j], device_id=(px, py, c), device_id_type=MESH)
                cp.start()
                sends.append(cp)
        for a in range(n):
            for j, (px, py) in enumerate(chips):
                pltpu.make_async_remote_copy(ins[a].at[mine], outs[a].at[2 * px + py], send_sems.at[3 * a + j],
                                             recv_sems.at[3 * a + j], device_id=(px, py, c),
                                             device_id_type=MESH).wait_recv()
        for cp in sends:
            cp.wait_send()
        for cp in local:
            cp.wait()

    return pl.pallas_call(
        body, name="chip_scatter", in_specs=[HBM] * n, out_specs=[HBM] * n,
        out_shape=[pltpu.HBM(p.shape, p.dtype) for p in parts],
        scratch_shapes=[_dma_sems(3 * n)] * 2 + [_dma_sems(n)],
        compiler_params=_comm_params(),
    )(*_hbm(*parts))


def _sum_chips(slots, first, rest, name):
    h = first.shape[1]

    def body(i_ref, a_ref, b_ref, c_ref, d_ref, o_ref):
        o_ref[...] = ((a_ref[...].astype(F32) + b_ref[...].astype(F32)) + c_ref[...].astype(F32)) + d_ref[...].astype(F32)

    slab = lambda k: pl.BlockSpec((1, h, D), lambda i, ix: (ix[k], 0, 0))
    return pl.pallas_call(
        body, name=name,
        grid_spec=pltpu.PrefetchScalarGridSpec(num_scalar_prefetch=1, grid=(1,),
                                               in_specs=[slab(0), slab(1), slab(2), slab(3)], out_specs=slab(4)),
        out_shape=pltpu.HBM((2, h, D), F32),
        compiler_params=_params(16, dimension_semantics=_seq()),
    )(slots, *_hbm(first, rest, rest, rest))


def _join_halves(halves):
    n = len(halves)

    def body(*refs):
        outs = refs[n:2 * n]
        send_sems, recv_sems = refs[2 * n:]
        x, y, c = _place()

        def copy(a, slab):
            return pltpu.make_async_remote_copy(outs[a].at[slab], outs[a].at[slab], send_sems.at[a], recv_sems.at[a],
                                                device_id=(x, y, 1 - c), device_id_type=MESH)

        for a in range(n):
            copy(a, c).start()
        for a in range(n):
            copy(a, 1 - c).wait_recv()
        for a in range(n):
            copy(a, c).wait_send()

    return pl.pallas_call(
        body, name="join_halves", in_specs=[HBM] * n, out_specs=[HBM] * n,
        out_shape=[pltpu.HBM(h.shape, F32) for h in halves],
        input_output_aliases={a: a for a in range(n)},
        scratch_shapes=[_dma_sems(n)] * 2,
        compiler_params=_comm_params(),
    )(*_hbm(*halves))


def _chip_partials(grads, wire_dtypes, names):
    core = lax.axis_index("c").astype(jnp.int32).reshape(1)
    recv = _sibling_exchange(grads, "sibling_exchange_" + names[0])
    return [_add_halves(core, g, r, dt, "add_halves_" + nm) for g, r, dt, nm in zip(grads, recv, wire_dtypes, names)]


def _finish_reduce(parts, got, same_order, names):
    x, y, c = _place()
    others = [2 * px + py for px, py in _other_chips(x, y)]
    own_first = jnp.stack([2 * x + y] + others + [c]).astype(jnp.int32)
    chip_order = jnp.stack([0 * c, 0 * c + 1, 0 * c + 2, 0 * c + 3, c]).astype(jnp.int32)
    halves = [_sum_chips(chip_order, q, q, "sum_chips_" + nm) if fixed else _sum_chips(own_first, p, q, "sum_chips_" + nm)
              for p, q, fixed, nm in zip(parts, got, same_order, names)]
    return [f.reshape(2 * f.shape[1], D) for f in _join_halves(halves)]


def _adamw(w, g, m, v, name):
    rows, cols = w.shape
    if rows % 8 == 0:
        tr = max(t for t in range(8, 257, 8) if rows % t == 0)
        grid, blk = (rows // tr,), pl.BlockSpec((tr, cols), lambda i: (i, 0))
    else:
        grid, blk = (cols // 256,), pl.BlockSpec((rows, 256), lambda i: (0, i))

    def body(w_ref, g_ref, m_ref, v_ref, d_ref, nm_ref, nv_ref):
        gg = g_ref[...]
        nm = ADAM_B1 * m_ref[...] + (1.0 - ADAM_B1) * gg
        nv = ADAM_B2 * v_ref[...] + (1.0 - ADAM_B2) * (gg * gg)
        m_hat = nm / (1.0 - ADAM_B1 ** ADAM_STEP)
        v_hat = nv / (1.0 - ADAM_B2 ** ADAM_STEP)
        d_ref[...] = -ADAM_LR * (m_hat / (jnp.sqrt(v_hat) + ADAM_EPS) + ADAM_WD * w_ref[...])
        nm_ref[...] = nm
        nv_ref[...] = nv

    return pl.pallas_call(
        body, name=name, grid=grid,
        in_specs=[blk] * 4, out_specs=[blk] * 3,
        out_shape=[pltpu.HBM(w.shape, F32)] * 3,
        compiler_params=_params(32, dimension_semantics=_seq()),
    )(*_hbm(w, g, m, v))


def _flat_rows(v, rows):
    flat = v.reshape(-1).astype(F32)
    return jnp.pad(flat, (0, rows * D - flat.shape[0])).reshape(rows, D)


def _small_pack(gr):
    tail = jnp.concatenate([gr["bg2"].reshape(-1), gr["sinks"].reshape(-1), gr["gn"].reshape(-1), gr["loss"].reshape(-1)])
    rows = [gr["meta"].reshape(N_META, D)] + [gr[k].reshape(1, D) for k in
                                              ("ln_in_g", "ln_in_b", "ln1_g", "ln1_b", "ln2_g", "ln2_b")]
    rows += [_flat_rows(gr["b_in"], 3), _flat_rows(gr["wg2"], 4), _flat_rows(tail, 1)]
    packed = jnp.concatenate(rows, axis=0)
    return jnp.pad(packed, ((0, SMALL_ROWS - packed.shape[0]), (0, 0)))


def _small_unpack(p):
    flat = lambda r0, n, size: p[r0:r0 + n].reshape(-1)[:size]
    tail = p[29]
    return dict(meta=p[0:N_META], ln_in_g=p[16], ln_in_b=p[17], ln1_g=p[18], ln1_b=p[19], ln2_g=p[20], ln2_b=p[21],
                b_in=flat(22, 3, D_IN), wg2=flat(25, 4, 16 * 256).reshape(16, 256), bg2=tail[0:256],
                sinks=tail[256:256 + SWA_HEADS], gn=tail[256 + SWA_HEADS:256 + SWA_HEADS + DV],
                loss=tail[256 + SWA_HEADS + DV])


BIG = ("w_in", "w_out", "w_g", "w_u", "w_d")


def kernel(x, meta_tokens, ln_in_g, ln_in_b, w_in, b_in, w_gate_lr2, b_gate_lr2, attn_sinks, gla_norm_g, w_out, ln1_g, ln1_b, w_ffn_gate, w_ffn_up, w_ffn_down, ln2_g, ln2_b, loss_target, m_meta_tokens, m_ln_in_g, m_ln_in_b, m_w_in, m_b_in, m_w_gate_lr2, m_b_gate_lr2, m_attn_sinks, m_gla_norm_g, m_w_out, m_ln1_g, m_ln1_b, m_w_ffn_gate, m_w_ffn_up, m_w_ffn_down, m_ln2_g, m_ln2_b, v_meta_tokens, v_ln_in_g, v_ln_in_b, v_w_in, v_b_in, v_w_gate_lr2, v_b_gate_lr2, v_attn_sinks, v_gla_norm_g, v_w_out, v_ln1_g, v_ln1_b, v_w_ffn_gate, v_w_ffn_up, v_w_ffn_down, v_ln2_g, v_ln2_b):
    chip = 2 * lax.axis_index("x") + lax.axis_index("y")

    halves = lambda a: a.reshape(2, a.shape[0] // 2, a.shape[1])
    r_in = SHARD_ROWS["w_in"]
    first = [halves(jnp.pad(w_in[0].T.astype(BF16), ((0, W_IN_WIN - r_in), (0, 0))))]
    rest = [halves(a) for a in (w_out[0].astype(BF16), w_ffn_gate[0].T.astype(BF16), w_ffn_up[0].T.astype(BF16),
                                w_ffn_down[0].astype(BF16))]
    lands = lambda arrs: [(N_CHIPS,) + a.shape for a in arrs]
    g_meta, g_wg2 = _gather_halves([halves(meta_tokens), halves(w_gate_lr2[0])])
    first_handle, first_token = _ici_start("gather", first, lands(first), [g_meta], "gather_w_in_start")
    rest_handle, token = _ici_start("gather", rest, lands(rest), [first_token], "gather_rest_start")
    meta_full = jnp.concatenate([g_meta[s].reshape(N_META, -1) for s in range(N_CHIPS)], axis=1)
    wg2_full = jnp.concatenate([g_wg2[s].reshape(w_gate_lr2.shape[1], -1) for s in range(N_CHIPS)], axis=1)

    def fetch(handle, shards, after, name):
        got = _sibling_forward(_ici_wait("gather", handle, after, name + "_wait"), name + "_forward")
        return [lax.dynamic_update_index_in_dim(g, s, chip, axis=0) for g, s in zip(got, shards)]

    def fetch_w_in(after):
        g_in, = fetch(first_handle, first, after, "gather_w_in")
        return jnp.pad(g_in.reshape(N_CHIPS, W_IN_WIN, D)[:, :r_in].reshape(D_IN, D), ((0, D_IN_P - D_IN), (0, 0)))

    def fetch_rest(after):
        return [g.reshape(-1, D) for g in fetch(rest_handle, rest, after, "gather_rest")]

    sent = {}

    def ship(key, grads, names):
        parts = _chip_partials([g.reshape(N_CHIPS, 2, -1, D) for g in grads], [BF16] * len(grads), names)
        handle, ship_token = _ici_start("scatter", parts, [p.shape for p in parts], [], "scatter_" + key + "_start")
        sent[key] = (parts, handle)
        return ship_token

    def ship_ffn(g):
        return ship("ffn", [g[k] for k in BIG[1:]], list(BIG[1:]))

    def ship_w_in(dw_in_t):
        win_start = [s * r_in // BF16_ROWS * BF16_ROWS for s in range(N_CHIPS)]
        return ship("w_in", [jnp.stack([dw_in_t[st:st + W_IN_WIN] for st in win_start])], ["w_in"])

    loss_part, dx, gr = _local_step(
        x[0], loss_target[0], meta_full, ln_in_g, ln_in_b, b_in[0], wg2_full, b_gate_lr2[0], attn_sinks[0],
        gla_norm_g[0], ln1_g[0], ln1_b[0], ln2_g[0], ln2_b[0], token, fetch_w_in, fetch_rest, ship_ffn, ship_w_in)
    ffn_got = _ici_wait("scatter", sent["ffn"][1], [dx], "scatter_ffn_wait")
    w_in_got = _ici_wait("scatter", sent["w_in"][1], [dx], "scatter_w_in_wait")

    gr["loss"] = loss_part
    small = jnp.broadcast_to(_small_pack(gr), (N_CHIPS, SMALL_ROWS, D)).reshape(N_CHIPS, 2, -1, D)
    small_parts = _chip_partials([small], [F32], ["small"])
    small_got = list(_chip_scatter(small_parts, [True]))
    red = _finish_reduce(sent["w_in"][0] + sent["ffn"][0] + small_parts, w_in_got + ffn_got + small_got,
                         [False] * len(BIG) + [True], list(BIG) + ["small"])

    big_g = dict(zip(BIG, red))
    big_g["w_in"] = lax.dynamic_slice_in_dim(red[0], chip * (r_in % BF16_ROWS), r_in, axis=0)
    sg = _small_unpack(red[-1])
    col = lambda a, width: lax.dynamic_slice_in_dim(a, chip * width, width, axis=1)
    grads = dict(
        meta_tokens=col(sg["meta"], D // N_CHIPS), ln_in_g=sg["ln_in_g"], ln_in_b=sg["ln_in_b"],
        w_in=big_g["w_in"].T[None], b_in=sg["b_in"][None], w_gate_lr2=col(sg["wg2"], 256 // N_CHIPS)[None],
        b_gate_lr2=sg["bg2"][None], attn_sinks=sg["sinks"][None], gla_norm_g=sg["gn"][None],
        w_out=big_g["w_out"][None], ln1_g=sg["ln1_g"][None], ln1_b=sg["ln1_b"][None],
        w_ffn_gate=big_g["w_g"].T[None], w_ffn_up=big_g["w_u"].T[None], w_ffn_down=big_g["w_d"][None],
        ln2_g=sg["ln2_g"][None], ln2_b=sg["ln2_b"][None])
    weights = dict(meta_tokens=meta_tokens, ln_in_g=ln_in_g, ln_in_b=ln_in_b, w_in=w_in, b_in=b_in,
                   w_gate_lr2=w_gate_lr2, b_gate_lr2=b_gate_lr2, attn_sinks=attn_sinks, gla_norm_g=gla_norm_g,
                   w_out=w_out, ln1_g=ln1_g, ln1_b=ln1_b, w_ffn_gate=w_ffn_gate, w_ffn_up=w_ffn_up,
                   w_ffn_down=w_ffn_down, ln2_g=ln2_g, ln2_b=ln2_b)
    m_in = dict(meta_tokens=m_meta_tokens, ln_in_g=m_ln_in_g, ln_in_b=m_ln_in_b, w_in=m_w_in, b_in=m_b_in,
                w_gate_lr2=m_w_gate_lr2, b_gate_lr2=m_b_gate_lr2, attn_sinks=m_attn_sinks, gla_norm_g=m_gla_norm_g,
                w_out=m_w_out, ln1_g=m_ln1_g, ln1_b=m_ln1_b, w_ffn_gate=m_w_ffn_gate, w_ffn_up=m_w_ffn_up,
                w_ffn_down=m_w_ffn_down, ln2_g=m_ln2_g, ln2_b=m_ln2_b)
    v_in = dict(meta_tokens=v_meta_tokens, ln_in_g=v_ln_in_g, ln_in_b=v_ln_in_b, w_in=v_w_in, b_in=v_b_in,
                w_gate_lr2=v_w_gate_lr2, b_gate_lr2=v_b_gate_lr2, attn_sinks=v_attn_sinks, gla_norm_g=v_gla_norm_g,
                w_out=v_w_out, ln1_g=v_ln1_g, ln1_b=v_ln1_b, w_ffn_gate=v_w_ffn_gate, w_ffn_up=v_w_ffn_up,
                w_ffn_down=v_w_ffn_down, ln2_g=v_ln2_g, ln2_b=v_ln2_b)
    names = list(weights)
    big_names = ("w_in", "w_out", "w_ffn_gate", "w_ffn_up", "w_ffn_down")

    delta, new_m, new_v = {}, {}, {}
    for k, kk in zip(big_names, BIG):
        flip = (lambda a: a.T) if kk in ("w_in", "w_g", "w_u") else (lambda a: a)
        d_, m_, v_ = _adamw(flip(weights[k][0]), big_g[kk], flip(m_in[k][0]), flip(v_in[k][0]), "adamw_" + k)
        delta[k], new_m[k], new_v[k] = (flip(t)[None] for t in (d_, m_, v_))
    small_names = [k for k in names if k not in big_names]
    sizes = [weights[k].size for k in small_names]
    rows_small = -(-sum(sizes) // D)
    rows_small += -rows_small % 8
    cat = lambda src: _flat_rows(jnp.concatenate([src[k].reshape(-1) for k in small_names]), rows_small)
    d_, m_, v_ = _adamw(cat(weights), cat(grads), cat(m_in), cat(v_in), "adamw_small")
    off = 0
    for k, n in zip(small_names, sizes):
        for dst, src in ((delta, d_), (new_m, m_), (new_v, v_)):
            dst[k] = src.reshape(-1)[off:off + n].reshape(weights[k].shape)
        off += n
    grads = {k: grads[k].reshape(weights[k].shape) for k in names}

    return (sg["loss"], dx[None], *[grads[k] for k in names], *[delta[k] for k in names], *[new_m[k] for k in names],
            *[new_v[k] for k in names])
```

```python
import functools

import jax
import jax.numpy as jnp
from jax import lax
from jax.experimental import pallas as pl
from jax.experimental.pallas import tpu as pltpu

F32 = jnp.float32
BF16 = jnp.bfloat16
MESH = pl.DeviceIdType.MESH

D = 1024
SEQ = 4096
N_META = 16
SWA_HEADS, SWA_KV_HEADS, DH = 8, 2, 64
WINDOW = 128
GLA_HEADS, DK, DV = 4, 64, 128
GLA_TAU = 16.0
CH = 64
D_FF = 2816
D_IN = 2320
LN_EPS = 1e-5
RMS_EPS = 1e-6
ALPHA = 2.0 ** 0.25
NEG = -1e30
ADAM_LR, ADAM_B1, ADAM_B2, ADAM_EPS, ADAM_WD, ADAM_STEP = 0.001, 0.9, 0.999, 1e-8, 0.01, 10
O_QS, O_KS, O_VS, O_QG, O_KG, O_VG, O_RG, O_LR = 0, 512, 640, 768, 1024, 1280, 1792, 2304

LANE = 128
BLK = WINDOW
D_IN_P = D_IN + LANE - 16
META_OFF = CH - N_META
HEAD_POS = (0, 4, 1, 5, 2, 6, 3, 7)
LN_ROWS = 512
TOKEN = (8, LANE)
N_CHIPS = 4
SHARD_ROWS = dict(w_in=D_IN // N_CHIPS, w_out=D // N_CHIPS, w_g=D_FF // N_CHIPS, w_u=D_FF // N_CHIPS,
                  w_d=D_FF // N_CHIPS)
SMALL_ROWS = 32
BF16_ROWS = 16
W_IN_WIN = -(-SHARD_ROWS["w_in"] // (2 * BF16_ROWS)) * 2 * BF16_ROWS
VMEM_CAP_MB = 64


def _lp():
    return SEQ + BLK


def _row_tile(cap):
    lp = _lp()
    return max(t for t in range(16, cap + 1, 16) if lp % t == 0)


def _params(vmem_mb, **kw):
    assert vmem_mb <= VMEM_CAP_MB - 6
    return pltpu.CompilerParams(vmem_limit_bytes=vmem_mb << 20, **kw)


def _seq(n=1):
    return ("arbitrary",) * n


def _const(shape):
    return pl.BlockSpec(shape, lambda *_: (0,) * len(shape), pipeline_mode=pl.Buffered(1))


def _acc(shape):
    return pl.BlockSpec(shape, lambda *_: (0,) * len(shape))


def _rows(tm, width):
    return pl.BlockSpec((tm, width), lambda i: (i, 0))


def _dot(a, b):
    return jnp.dot(a.astype(BF16), b.astype(BF16), preferred_element_type=F32)


def _dot_nt(a, b):
    return lax.dot_general(a.astype(BF16), b.astype(BF16), (((1,), (1,)), ((), ())), preferred_element_type=F32)


def _dot_tn(a, b):
    return lax.dot_general(a.astype(BF16), b.astype(BF16), (((0,), (0,)), ((), ())), preferred_element_type=F32)


def _dot_exact(a, b):
    return jnp.dot(a, b, precision=lax.Precision.HIGHEST, preferred_element_type=F32)


def _ln_stats(x):
    mu = jnp.mean(x, axis=-1, keepdims=True)
    xc = x - mu
    rstd = lax.rsqrt(jnp.mean(xc * xc, axis=-1, keepdims=True) + LN_EPS)
    return xc * rstd, rstd


def _ln_bwd(dy, xhat, rstd, g):
    dxh = dy * g
    return rstd * (dxh - jnp.mean(dxh, axis=-1, keepdims=True) - xhat * jnp.mean(dxh * xhat, axis=-1, keepdims=True))


def _sigmoid(x):
    return 1.0 / (1.0 + jnp.exp(-x))


def _iota(shape, dim):
    return lax.broadcasted_iota(jnp.int32, shape, dim)


def _hbm(*arrays):
    return tuple(pltpu.with_memory_space_constraint(a, pltpu.HBM) for a in arrays)


def _ln_in_fwd(x, meta_ext, g, b, token):
    tr = min(LN_ROWS, SEQ)

    def ln(x_ref, g_ref, b_ref, h_ref):
        xhat, _ = _ln_stats(x_ref[...])
        h_ref[...] = xhat * g_ref[...] + b_ref[...]

    def body(x_ref, g_ref, b_ref, token_ref, h_ref):
        ln(x_ref, g_ref, b_ref, h_ref)

    h_real = pl.pallas_call(
        body, name="ln_in_fwd", grid=(SEQ // tr,),
        in_specs=[_rows(tr, D), _const((1, D)), _const((1, D)), _const(TOKEN)],
        out_specs=_rows(tr, D),
        out_shape=pltpu.HBM((_lp(), D), F32),
        compiler_params=_params(32, dimension_semantics=_seq()),
    )(*_hbm(x, g, b), token)

    def meta_body(m_ref, g_ref, b_ref, real_ref, h_ref):
        ln(m_ref, g_ref, b_ref, h_ref)

    return pl.pallas_call(
        meta_body, name="ln_in_fwd_meta", grid=(1,),
        in_specs=[_const((BLK, D)), _const((1, D)), _const((1, D)), pl.BlockSpec(memory_space=pl.ANY)],
        out_specs=pl.BlockSpec((BLK, D), lambda i: (SEQ // BLK, 0)),
        out_shape=pltpu.HBM((_lp(), D), F32),
        input_output_aliases={3: 0},
        compiler_params=_params(16, dimension_semantics=_seq()),
    )(*_hbm(meta_ext, g, b, h_real))


def _in_proj(h0, w_in_t, b_in_p, wg2_p, bg2):
    tm = _row_tile(384)
    lp = _lp()
    widths = (512, 128, 128, 256, 256, 512, 512, 128)
    offs = (O_QS, O_KS, O_VS, O_QG, O_KG, O_VG, O_RG, O_LR)

    def body(h_ref, w_ref, b_ref, wg2_ref, bg2_ref, *outs):
        proj = _dot_nt(h_ref[...], w_ref[...]) + b_ref[...]
        for pos, h in enumerate(HEAD_POS):
            outs[0][:, pos * DH:(pos + 1) * DH] = proj[:, O_QS + h * DH:O_QS + (h + 1) * DH]
        for o_ref, off, wd in zip(outs[1:8], offs[1:], widths[1:]):
            o_ref[...] = proj[:, off:off + wd]
        outs[8][...] = _dot(proj[:, O_LR:O_LR + LANE], wg2_ref[...]) + bg2_ref[...]

    return pl.pallas_call(
        body, name="in_proj", grid=(lp // tm,),
        in_specs=[_rows(tm, D), _const((D_IN_P, D)), _const((1, D_IN_P)), _const((LANE, 256)), _const((1, 256))],
        out_specs=[_rows(tm, w) for w in widths] + [_rows(tm, 256)],
        out_shape=[pltpu.HBM((lp, w), F32) for w in widths] + [pltpu.HBM((lp, 256), F32)],
        compiler_params=_params(40, dimension_semantics=_seq()),
    )(*_hbm(h0, w_in_t, b_in_p, wg2_p, bg2))


def _swa_masks(n):
    nb = SEQ // BLK
    is_meta = n == nb
    ri = _iota((BLK, BLK), 0)
    cj = _iota((BLK, BLK), 1)
    meta_col = ((cj >= META_OFF) & (cj < CH)).astype(jnp.int32)
    meta_q = meta_col * ((cj <= ri) & (ri < CH)).astype(jnp.int32)
    valid_m = jnp.where(is_meta, meta_q, meta_col) > 0
    dist_m = jnp.where(is_meta, ri - cj, n * BLK + ri + CH - cj).astype(F32)
    valid_p = jnp.where((n >= 1) & (n < nb), (cj > ri).astype(jnp.int32), 0) > 0
    dist_p = (ri + BLK - cj).astype(F32)
    valid_c = jnp.where(n < nb, (cj <= ri).astype(jnp.int32), 0) > 0
    dist_c = (ri - cj).astype(F32)
    return (dist_m, dist_p, dist_c), (valid_m, valid_p, valid_c)


def _swa_bias(n):
    dists, valids = _swa_masks(n)
    return (jnp.concatenate([-d for d in dists], axis=1),
            jnp.concatenate([jnp.where(v, 0.0, NEG) for v in valids], axis=1))


def _swa_half(ref, pos, scale=1.0):
    col = ref[:, (pos // 2) * LANE:(pos // 2 + 1) * LANE]
    lane = _iota((BLK, LANE), 1)
    mine = lane < DH if pos % 2 == 0 else lane >= DH
    return jnp.where(mine, col * scale, 0.0).astype(BF16)


def _swa_merge(even, odd):
    return jnp.where(_iota((BLK, LANE), 1) < DH, even, odd)


def _swa_softmax(t, sink):
    m = jnp.maximum(jnp.max(t, axis=-1, keepdims=True), sink)
    e = jnp.exp(t - m)
    e_sink = jnp.exp(sink - m)
    inv = 1.0 / (jnp.sum(e, axis=-1, keepdims=True) + e_sink)
    return e * inv, e_sink * inv


def _swa_kv_specs(width):
    nb = SEQ // BLK
    return [pl.BlockSpec((BLK, width), lambda n: (nb, 0)),
            pl.BlockSpec((BLK, width), lambda n: (jnp.clip(n - 1, 0, nb - 1), 0)),
            pl.BlockSpec((BLK, width), lambda n: (jnp.minimum(n, nb), 0))]


def _swa_fwd(sinks, qs, ks, vs):
    nb = SEQ // BLK
    heads = range(SWA_HEADS)

    def body(sink_ref, q_ref, km_ref, kp_ref, kc_ref, vm_ref, vp_ref, vc_ref, o_ref):
        negdist, maskbias = _swa_bias(pl.program_id(0))
        k_all = jnp.concatenate([km_ref[...], kp_ref[...], kc_ref[...]], axis=0).astype(BF16)
        v_all = jnp.concatenate([vm_ref[...], vp_ref[...], vc_ref[...]], axis=0).astype(BF16)
        q = [_swa_half(q_ref, pos, DH ** -0.5) for pos in heads]
        t = [_dot_nt(q[pos], k_all) + (2.0 ** -(HEAD_POS[pos] + 1) * negdist + maskbias) for pos in heads]
        p = [_swa_softmax(t[pos], sink_ref[HEAD_POS[pos]])[0].astype(BF16) for pos in heads]
        o = [_dot(p[pos], v_all) for pos in heads]
        for col in range(SWA_HEADS // 2):
            o_ref[:, col * LANE:(col + 1) * LANE] = _swa_merge(o[2 * col], o[2 * col + 1])

    kvw = SWA_KV_HEADS * DH
    return pl.pallas_call(
        body, name="swa_fwd", grid=(nb + 1,),
        in_specs=[pl.BlockSpec(memory_space=pltpu.SMEM), _rows(BLK, SWA_HEADS * DH)] + _swa_kv_specs(kvw) + _swa_kv_specs(kvw),
        out_specs=_rows(BLK, SWA_HEADS * DH),
        out_shape=pltpu.HBM((_lp(), SWA_HEADS * DH), F32),
        compiler_params=_params(16, dimension_semantics=_seq()),
    )(sinks, *_hbm(qs, ks, ks, ks, vs, vs, vs))


GLA_PER_STEP = BLK // CH


def _gla_block(s):
    nb = SEQ // BLK
    return jnp.where(s == 0, nb, s - 1)


def _gla_rowmask(s):
    ri = _iota((BLK, 1), 0)
    m = jnp.where(s == 0, ((ri >= META_OFF) & (ri < CH)).astype(jnp.int32), 1)
    return (m > 0).astype(F32) + jnp.zeros((BLK, 1), F32)


def _gla_chunk_masks():
    r, c = _iota((BLK, BLK), 0), _iota((BLK, BLK), 1)
    same = ((r < CH) & (c < CH)) | ((r >= CH) & (c >= CH))
    return same & (r >= c), same & (r <= c), same


def _gla_decay(z, rmask):
    log_g = (jnp.minimum(z, 0.0) - jnp.log1p(jnp.exp(-jnp.abs(z)))) * (rmask / GLA_TAU)
    lower, _, same = _gla_chunk_masks()
    return _dot_exact(lower.astype(F32), log_g), _dot_exact(same.astype(F32), log_g)


def _gla_slices(c, h):
    return slice(c * CH, (c + 1) * CH), slice(h * DK, (h + 1) * DK), slice(h * DV, (h + 1) * DV)


def _gla_fwd(qg, kg, vg, z):
    steps = SEQ // BLK + 1
    kw, vw = GLA_HEADS * DK, GLA_HEADS * DV
    pairs = [(c, h) for c in range(GLA_PER_STEP) for h in range(GLA_HEADS)]

    def body(q_ref, k_ref, v_ref, z_ref, o_ref, st_ref, st):
        s = pl.program_id(0)

        @pl.when(s == 0)
        def _():
            st[...] = jnp.zeros_like(st)

        rmask = _gla_rowmask(s)
        b, b_last = _gla_decay(z_ref[...], rmask)
        q = q_ref[...] * (rmask * DK ** -0.5)
        k = k_ref[...] * rmask
        v = v_ref[...] * rmask
        qe = q * jnp.exp(b)
        ke = k * jnp.exp(-b)
        kd = k * jnp.exp(b_last - b)
        e_last = jnp.exp(b_last)
        causal = _iota((CH, CH), 0) >= _iota((CH, CH), 1)
        a, upd, intra = {}, {}, {}
        for c, h in pairs:
            rows, ks, vs_ = _gla_slices(c, h)
            a[c, h] = jnp.where(causal, _dot_nt(qe[rows, ks], ke[rows, ks]), 0.0)
            upd[c, h] = _dot_tn(v[rows, vs_], kd[rows, ks])
        for c, h in pairs:
            rows, ks, vs_ = _gla_slices(c, h)
            intra[c, h] = _dot(a[c, h], v[rows, vs_])
        state = st[...]
        for c in range(GLA_PER_STEP):
            st_ref[0, c] = state
            for h in range(GLA_HEADS):
                rows, ks, vs_ = _gla_slices(c, h)
                o_ref[rows, vs_] = intra[c, h] + _dot_nt(qe[rows, ks], state[:, ks])
            state = state * e_last[c * CH:c * CH + 1] + jnp.concatenate([upd[c, h] for h in range(GLA_HEADS)], axis=1)
        st[...] = state

    blk = lambda w: pl.BlockSpec((BLK, w), lambda s: (_gla_block(s), 0))
    return pl.pallas_call(
        body, name="gla_fwd", grid=(steps,),
        in_specs=[blk(kw), blk(kw), blk(vw), blk(kw)],
        out_specs=[blk(vw), pl.BlockSpec((1, GLA_PER_STEP, DV, kw), lambda s: (s, 0, 0, 0))],
        out_shape=[pltpu.HBM((_lp(), vw), F32), pltpu.HBM((steps, GLA_PER_STEP, DV, kw), F32)],
        scratch_shapes=[pltpu.VMEM((DV, kw), F32)],
        compiler_params=_params(16, dimension_semantics=_seq()),
    )(*_hbm(qg, kg, vg, z))


def _post_mix(o_s, o_gla, r_g, h0, gn4, w_out, g1, b1):
    tm = _row_tile(384)
    lp = _lp()

    def body(os_ref, og_ref, r_ref, h0_ref, gn_ref, w_ref, g_ref, b_ref, o_ref, pre_ref, h1_ref):
        for pos, h in enumerate(HEAD_POS):
            o_ref[:, h * DH:(h + 1) * DH] = os_ref[:, pos * DH:(pos + 1) * DH].astype(BF16)
        for h in range(GLA_HEADS):
            hs = slice(h * DV, (h + 1) * DV)
            xg = og_ref[:, hs]
            n = xg * lax.rsqrt(jnp.mean(xg * xg, axis=-1, keepdims=True) + RMS_EPS) * gn_ref[...]
            r = r_ref[:, hs]
            o_ref[:, 512 + h * DV:512 + (h + 1) * DV] = (n * (r * _sigmoid(r))).astype(BF16)
        pre = ALPHA * h0_ref[...] + _dot(o_ref[...], w_ref[...])
        pre_ref[...] = pre
        xhat, _ = _ln_stats(pre)
        h1_ref[...] = xhat * g_ref[...] + b_ref[...]

    return pl.pallas_call(
        body, name="post_mix", grid=(lp // tm,),
        in_specs=[_rows(tm, 512), _rows(tm, 512), _rows(tm, 512), _rows(tm, D), _const((1, DV)), _const((D, D)),
                  _const((1, D)), _const((1, D))],
        out_specs=[_rows(tm, D), _rows(tm, D), _rows(tm, D)],
        out_shape=[pltpu.HBM((lp, D), BF16), pltpu.HBM((lp, D), F32),
                   pltpu.HBM((lp, D), F32)],
        compiler_params=_params(32, dimension_semantics=_seq()),
    )(*_hbm(o_s, o_gla, r_g, h0, gn4, w_out, g1, b1))


def _ffn_fwd_loss_bwd(h1, wg_t, wu_t, wd, target, g2, b2):
    lp = _lp()
    tm = max(t for t in range(BLK, 384 + 1, BLK) if lp % t == 0)
    steps = lp // tm
    last_blk = SEQ // BLK - 1
    half = D_FF // 2
    n_t = tm // BLK

    def body(*refs):
        h_ref, wg_ref, wu_ref, wd_ref = refs[:4]
        t_refs = refs[4:4 + n_t]
        g2_ref, b2_ref, a_ref, dgate_ref, dup_ref, dp_ref, loss_ref, dg_ref, db_ref, g_s, u_s, acc = refs[4 + n_t:]
        i = pl.program_id(0)

        @pl.when(i == 0)
        def _():
            acc[...] = jnp.zeros_like(acc)
            dg_ref[...] = jnp.zeros_like(dg_ref)
            db_ref[...] = jnp.zeros_like(db_ref)

        h = h_ref[...]
        hb = h.astype(BF16)
        pre = ALPHA * h
        for j in range(2):
            cols = slice(j * half, (j + 1) * half)
            g = _dot_nt(hb, wg_ref[cols, :])
            u = _dot_nt(hb, wu_ref[cols, :])
            g_s[:, cols] = g
            u_s[:, cols] = u
            pre = pre + _dot(g * _sigmoid(g) * u, wd_ref[cols, :])
        xhat, rstd = _ln_stats(pre)
        real = i * tm + _iota((tm, 1), 0) < SEQ
        target_rows = jnp.concatenate([t[...] for t in t_refs], axis=0)
        diff = jnp.where(real, xhat * g2_ref[...] + b2_ref[...] - target_rows, 0.0)
        acc[...] += jnp.sum(diff * diff, axis=0, keepdims=True)
        dy = diff * (1.0 / D)
        dpre = _ln_bwd(dy, xhat, rstd, g2_ref[...])
        dp_ref[...] = dpre
        dg_ref[...] += jnp.sum(dy * xhat, axis=0, keepdims=True)
        db_ref[...] += jnp.sum(dy, axis=0, keepdims=True)
        dpb = dpre.astype(BF16)
        for j in range(2):
            cols = slice(j * half, (j + 1) * half)
            g, u = g_s[:, cols], u_s[:, cols]
            sg = _sigmoid(g)
            silu = g * sg
            da = _dot_nt(dpb, wd_ref[cols, :])
            a_ref[:, cols] = (silu * u).astype(BF16)
            dgate_ref[:, cols] = (da * u * (sg * (1.0 + g * (1.0 - sg)))).astype(BF16)
            dup_ref[:, cols] = (da * silu).astype(BF16)

        @pl.when(i == steps - 1)
        def _():
            loss_ref[...] = jnp.zeros_like(loss_ref) + (0.5 / D) * jnp.sum(acc[...], axis=1, keepdims=True)

    t_spec = lambda k: pl.BlockSpec((BLK, D), lambda i: (jnp.minimum(i * n_t + k, last_blk), 0))
    return pl.pallas_call(
        body, name="ffn_fwd_loss_bwd", grid=(steps,),
        in_specs=[_rows(tm, D), _const((D_FF, D)), _const((D_FF, D)), _const((D_FF, D))]
        + [t_spec(k) for k in range(n_t)] + [_const((1, D)), _const((1, D))],
        out_specs=[_rows(tm, D_FF), _rows(tm, D_FF), _rows(tm, D_FF), _rows(tm, D), _acc((1, LANE)), _acc((1, D)),
                   _acc((1, D))],
        out_shape=[pltpu.HBM((lp, D_FF), BF16)] * 3 + [pltpu.HBM((lp, D), F32), pltpu.HBM((1, LANE), F32),
                                                         pltpu.HBM((1, D), F32), pltpu.HBM((1, D), F32)],
        scratch_shapes=[pltpu.VMEM((tm, D_FF), F32), pltpu.VMEM((tm, D_FF), F32), pltpu.VMEM((1, D), F32)],
        compiler_params=_params(58, dimension_semantics=_seq()),
    )(*_hbm(h1, wg_t, wu_t, wd, *[target] * n_t, g2, b2))


def _ffn_bwd_input(dpre2, dgate, dup, pre1, wg_t, wu_t, g1):
    tm = _row_tile(384)
    lp = _lp()

    def body(dp_ref, dg_ref, du_ref, p1_ref, wg_ref, wu_ref, g1_ref, dp1_ref, dg1_ref, db1_ref):
        @pl.when(pl.program_id(0) == 0)
        def _():
            dg1_ref[...] = jnp.zeros_like(dg1_ref)
            db1_ref[...] = jnp.zeros_like(db1_ref)

        dh1 = ALPHA * dp_ref[...] + _dot(dg_ref[...], wg_ref[...]) + _dot(du_ref[...], wu_ref[...])
        xhat, rstd = _ln_stats(p1_ref[...])
        dp1_ref[...] = _ln_bwd(dh1, xhat, rstd, g1_ref[...])
        dg1_ref[...] += jnp.sum(dh1 * xhat, axis=0, keepdims=True)
        db1_ref[...] += jnp.sum(dh1, axis=0, keepdims=True)

    return pl.pallas_call(
        body, name="ffn_bwd_input", grid=(lp // tm,),
        in_specs=[_rows(tm, D), _rows(tm, D_FF), _rows(tm, D_FF), _rows(tm, D), _const((D_FF, D)), _const((D_FF, D)),
                  _const((1, D))],
        out_specs=[_rows(tm, D), _acc((1, D)), _acc((1, D))],
        out_shape=[pltpu.HBM((lp, D), F32), pltpu.HBM((1, D), F32), pltpu.HBM((1, D), F32)],
        compiler_params=_params(40, dimension_semantics=_seq()),
    )(*_hbm(dpre2, dgate, dup, pre1, wg_t, wu_t, g1))


def _atb(a, b, name):
    lp = _lp()
    tm = _row_tile(1408)
    n, w = a.shape[1], b.shape[1]
    bw = 512 if n * w * 4 > (4 << 20) else w

    def body(a_ref, b_ref, o_ref):
        @pl.when(pl.program_id(1) == 0)
        def _():
            o_ref[...] = jnp.zeros_like(o_ref)

        o_ref[...] += _dot_tn(a_ref[...], b_ref[...])

    return pl.pallas_call(
        body, name=name, grid=(w // bw, lp // tm),
        in_specs=[pl.BlockSpec((tm, n), lambda j, k: (k, 0)), pl.BlockSpec((tm, bw), lambda j, k: (k, j))],
        out_specs=pl.BlockSpec((n, bw), lambda j, k: (0, j)),
        out_shape=pltpu.HBM((n, w), F32),
        compiler_params=_params(48, dimension_semantics=_seq(2)),
    )(*_hbm(a, b))


def _out_bwd(dpre1, w_out, o_gla, r_g, gn4, token):
    tm = _row_tile(384)
    lp = _lp()

    def body(dp_ref, w_ref, og_ref, r_ref, gn_ref, token_ref, dos_ref, dog_ref, dr_ref, dgn_ref):
        @pl.when(pl.program_id(0) == 0)
        def _():
            dgn_ref[...] = jnp.zeros_like(dgn_ref)

        do = _dot_nt(dp_ref[...], w_ref[...])
        for pos, h in enumerate(HEAD_POS):
            dos_ref[:, pos * DH:(pos + 1) * DH] = do[:, h * DH:(h + 1) * DH]
        gn = gn_ref[...]
        for h in range(GLA_HEADS):
            hs = slice(h * DV, (h + 1) * DV)
            xg = og_ref[:, hs]
            rstd = lax.rsqrt(jnp.mean(xg * xg, axis=-1, keepdims=True) + RMS_EPS)
            nx = xg * rstd
            r = r_ref[:, hs]
            sr = _sigmoid(r)
            d_o = do[:, 512 + h * DV:512 + (h + 1) * DV]
            dr_ref[:, hs] = d_o * (nx * gn) * (sr * (1.0 + r * (1.0 - sr)))
            dn = d_o * (r * sr)
            dgn_ref[...] += jnp.sum(dn * nx, axis=0, keepdims=True)
            dnx = dn * gn
            dog_ref[:, hs] = rstd * (dnx - nx * jnp.mean(dnx * nx, axis=-1, keepdims=True))

    return pl.pallas_call(
        body, name="out_bwd", grid=(lp // tm,),
        in_specs=[_rows(tm, D), _const((D, D)), _rows(tm, 512), _rows(tm, 512), _const((1, DV)), _const(TOKEN)],
        out_specs=[_rows(tm, 512), _rows(tm, 512), _rows(tm, 512), _acc((1, DV))],
        out_shape=[pltpu.HBM((lp, 512), F32)] * 3 + [pltpu.HBM((1, DV), F32)],
        compiler_params=_params(32, dimension_semantics=_seq()),
    )(*_hbm(dpre1, w_out, o_gla, r_g, gn4), token)


def _gla_bwd(qg, kg, vg, z, do_gla, st_all):
    steps = SEQ // BLK + 1
    kw, vw = GLA_HEADS * DK, GLA_HEADS * DV
    pairs = [(c, h) for c in range(GLA_PER_STEP) for h in range(GLA_HEADS)]
    heads = range(GLA_HEADS)

    def body(q_ref, k_ref, v_ref, z_ref, do_ref, st_ref, dq_ref, dk_ref, dv_ref, dz_ref, dst):
        @pl.when(pl.program_id(0) == 0)
        def _():
            dst[...] = jnp.zeros_like(dst)

        rmask = _gla_rowmask(steps - 1 - pl.program_id(0))
        zz = z_ref[...]
        b, b_last = _gla_decay(zz, rmask)
        e_b, e_nb, e_kd, e_last = jnp.exp(b), jnp.exp(-b), jnp.exp(b_last - b), jnp.exp(b_last)
        q = q_ref[...] * (rmask * DK ** -0.5)
        k = k_ref[...] * rmask
        v = v_ref[...] * rmask
        qe, ke, kd = q * e_b, k * e_nb, k * e_kd
        d_o = do_ref[...]
        causal = _iota((CH, CH), 0) >= _iota((CH, CH), 1)
        a, da, dqe, dke, dv_intra, carry = {}, {}, {}, {}, {}, {}
        for c, h in pairs:
            rows, ks, vs_ = _gla_slices(c, h)
            a[c, h] = jnp.where(causal, _dot_nt(qe[rows, ks], ke[rows, ks]), 0.0)
            da[c, h] = jnp.where(causal, _dot_nt(d_o[rows, vs_], v[rows, vs_]), 0.0)
            carry[c, h] = _dot_tn(d_o[rows, vs_], qe[rows, ks])
        for c, h in pairs:
            rows, ks, vs_ = _gla_slices(c, h)
            dqe[c, h] = _dot(d_o[rows, vs_], st_ref[0, c][:, ks]) + _dot(da[c, h], ke[rows, ks])
            dke[c, h] = _dot_tn(da[c, h], qe[rows, ks])
            dv_intra[c, h] = _dot_tn(a[c, h], d_o[rows, vs_])
        dstate = dst[...]
        dkd, db_decay = {}, {}
        for c in reversed(range(GLA_PER_STEP)):
            for h in heads:
                rows, ks, vs_ = _gla_slices(c, h)
                dkd[c, h] = _dot(v[rows, vs_], dstate[:, ks])
                dv_ref[rows, vs_] = dv_intra[c, h] + _dot_nt(kd[rows, ks], dstate[:, ks])
            chunk_last = e_last[c * CH:c * CH + 1]
            db_decay[c] = jnp.sum(dstate * st_ref[0, c], axis=0, keepdims=True) * chunk_last
            dstate = dstate * chunk_last + jnp.concatenate([carry[c, h] for h in heads], axis=1)
        dst[...] = dstate
        rows_of = lambda parts: jnp.concatenate(
            [jnp.concatenate([parts[c, h] for h in heads], axis=1) for c in range(GLA_PER_STEP)], axis=0)
        dqe_all, dke_all, dkd_all = rows_of(dqe), rows_of(dke), rows_of(dkd)
        dq_ref[...] = dqe_all * e_b * (rmask * DK ** -0.5)
        dk_ref[...] = (dke_all * e_nb + dkd_all * e_kd) * rmask
        dkd_kd = dkd_all * kd
        db = dqe_all * qe - dke_all * ke - dkd_kd
        _, upper, same = _gla_chunk_masks()
        decay_rows = jnp.concatenate([jnp.broadcast_to(db_decay[c], (CH, kw)) for c in range(GLA_PER_STEP)], axis=0)
        dlog_g = _dot_exact(upper.astype(F32), db) + _dot_exact(same.astype(F32), dkd_kd) + decay_rows
        dz_ref[...] = dlog_g * (rmask / GLA_TAU) * _sigmoid(-zz)

    blk = lambda w: pl.BlockSpec((BLK, w), lambda s: (_gla_block(steps - 1 - s), 0))
    return pl.pallas_call(
        body, name="gla_bwd", grid=(steps,),
        in_specs=[blk(kw), blk(kw), blk(vw), blk(kw), blk(vw),
                  pl.BlockSpec((1, GLA_PER_STEP, DV, kw), lambda s: (steps - 1 - s, 0, 0, 0))],
        out_specs=[blk(kw), blk(kw), blk(vw), blk(kw)],
        out_shape=[pltpu.HBM((_lp(), kw), F32), pltpu.HBM((_lp(), kw), F32),
                   pltpu.HBM((_lp(), vw), F32), pltpu.HBM((_lp(), kw), F32)],
        scratch_shapes=[pltpu.VMEM((DV, kw), F32)],
        compiler_params=_params(16, dimension_semantics=_seq()),
    )(*_hbm(qg, kg, vg, z, do_gla, st_all))


def _swa_bwd(sinks, qs, ks, vs, do_s):
    nb = SEQ // BLK
    kvw = SWA_KV_HEADS * DH
    scale = DH ** -0.5
    heads = range(SWA_HEADS)

    def body(sink_ref, q_ref, km_ref, kp_ref, kc_ref, vm_ref, vp_ref, vc_ref, do_ref,
             dq_ref, dk_ref, dv_ref, dsink_ref, carry_k, carry_v, meta_k, meta_v):
        n = pl.program_id(0)

        @pl.when(n == 0)
        def _():
            for r in (carry_k, carry_v, meta_k, meta_v):
                r[...] = jnp.zeros_like(r)
            dsink_ref[...] = jnp.zeros_like(dsink_ref)

        @pl.when(n <= nb)
        def _():
            negdist, maskbias = _swa_bias(n)
            lane = _iota((1, LANE), 1)
            k_all = jnp.concatenate([km_ref[...], kp_ref[...], kc_ref[...]], axis=0).astype(BF16)
            v_all = jnp.concatenate([vm_ref[...], vp_ref[...], vc_ref[...]], axis=0).astype(BF16)
            q = [_swa_half(q_ref, pos, scale) for pos in heads]
            d_o = [_swa_half(do_ref, pos) for pos in heads]
            t = [_dot_nt(q[pos], k_all) + (2.0 ** -(HEAD_POS[pos] + 1) * negdist + maskbias) for pos in heads]
            dp = [_dot_nt(d_o[pos], v_all) for pos in heads]
            soft = [_swa_softmax(t[pos], sink_ref[HEAD_POS[pos]]) for pos in heads]
            p = [s[0] for s in soft]
            delta = [jnp.sum(p[pos] * dp[pos], axis=-1, keepdims=True) for pos in heads]
            ds = [(p[pos] * (dp[pos] - delta[pos])).astype(BF16) for pos in heads]
            dq = [_dot(ds[pos], k_all) for pos in heads]
            for col in range(SWA_HEADS // 2):
                dq_ref[:, col * LANE:(col + 1) * LANE] = scale * _swa_merge(dq[2 * col], dq[2 * col + 1])
            dsink = jnp.zeros((1, LANE), F32)
            for pos in heads:
                dsink = dsink + jnp.where(lane == HEAD_POS[pos],
                                          -jnp.sum(soft[pos][1] * delta[pos], axis=0, keepdims=True), 0.0)
            dsink_ref[...] += dsink
            dk3 = _dot_tn(jnp.concatenate(q, axis=0), jnp.concatenate(ds, axis=0)).T
            dv3 = _dot_tn(jnp.concatenate(d_o, axis=0), jnp.concatenate([x.astype(BF16) for x in p], axis=0)).T
            meta_k[...] += dk3[0:BLK]
            meta_v[...] += dv3[0:BLK]
            dk_ref[...] = carry_k[...] + dk3[BLK:2 * BLK]
            dv_ref[...] = carry_v[...] + dv3[BLK:2 * BLK]
            carry_k[...] = dk3[2 * BLK:3 * BLK]
            carry_v[...] = dv3[2 * BLK:3 * BLK]

        @pl.when(n == nb + 1)
        def _():
            dk_ref[...] = meta_k[...]
            dv_ref[...] = meta_v[...]

    kv_out = pl.BlockSpec((BLK, kvw), lambda n: (jnp.where(n == nb + 1, nb, jnp.clip(n - 1, 0, nb - 1)), 0))
    qblk = pl.BlockSpec((BLK, SWA_HEADS * DH), lambda n: (jnp.minimum(n, nb), 0))
    return pl.pallas_call(
        body, name="swa_bwd", grid=(nb + 2,),
        in_specs=[pl.BlockSpec(memory_space=pltpu.SMEM), qblk] + _swa_kv_specs(kvw) + _swa_kv_specs(kvw) + [qblk],
        out_specs=[qblk, kv_out, kv_out, _acc((1, LANE))],
        out_shape=[pltpu.HBM((_lp(), SWA_HEADS * DH), F32), pltpu.HBM((_lp(), kvw), F32),
                   pltpu.HBM((_lp(), kvw), F32), pltpu.HBM((1, LANE), F32)],
        scratch_shapes=[pltpu.VMEM((BLK, kvw), F32)] * 4,
        compiler_params=_params(16, dimension_semantics=_seq()),
    )(sinks, *_hbm(qs, ks, ks, ks, vs, vs, vs, do_s))


def _in_bwd(dqs, dks, dvs, dqg, dkg, dvg, drg, dz, dpre1, w_in_t, wg2_p):
    tm = _row_tile(384)
    lp = _lp()
    widths = (512, 128, 128, 256, 256, 512, 512)
    offs = (O_QS, O_KS, O_VS, O_QG, O_KG, O_VG, O_RG)

    def body(*refs):
        parts, (dz_ref, dp1_ref, w_ref, wg2_ref, dproj_ref, dh0_ref, dbin_ref, dbg_ref) = refs[:7], refs[7:]

        @pl.when(pl.program_id(0) == 0)
        def _():
            dbin_ref[...] = jnp.zeros_like(dbin_ref)
            dbg_ref[...] = jnp.zeros_like(dbg_ref)

        for pos, h in enumerate(HEAD_POS):
            val = parts[0][:, pos * DH:(pos + 1) * DH]
            dproj_ref[:, O_QS + h * DH:O_QS + (h + 1) * DH] = val.astype(BF16)
            dbin_ref[:, O_QS + h * DH:O_QS + (h + 1) * DH] += jnp.sum(val, axis=0, keepdims=True)
        for p_ref, off, wd in zip(parts[1:], offs[1:], widths[1:]):
            val = p_ref[...]
            dproj_ref[:, off:off + wd] = val.astype(BF16)
            dbin_ref[:, off:off + wd] += jnp.sum(val, axis=0, keepdims=True)
        dz = dz_ref[...]
        dlr = _dot_nt(dz, wg2_ref[...])
        dproj_ref[:, O_LR:O_LR + LANE] = dlr.astype(BF16)
        dbin_ref[:, O_LR:O_LR + LANE] += jnp.sum(dlr, axis=0, keepdims=True)
        dbg_ref[...] += jnp.sum(dz, axis=0, keepdims=True)
        dh0_ref[...] = ALPHA * dp1_ref[...] + _dot(dproj_ref[...], w_ref[...])

    return pl.pallas_call(
        body, name="in_bwd", grid=(lp // tm,),
        in_specs=[_rows(tm, w) for w in widths] + [_rows(tm, 256), _rows(tm, D), _const((D_IN_P, D)), _const((LANE, 256))],
        out_specs=[_rows(tm, D_IN_P), _rows(tm, D), _acc((1, D_IN_P)), _acc((1, 256))],
        out_shape=[pltpu.HBM((lp, D_IN_P), BF16), pltpu.HBM((lp, D), F32),
                   pltpu.HBM((1, D_IN_P), F32), pltpu.HBM((1, 256), F32)],
        compiler_params=_params(40, dimension_semantics=_seq()),
    )(*_hbm(dqs, dks, dvs, dqg, dkg, dvg, drg, dz, dpre1, w_in_t, wg2_p))


def _ln_in_bwd(x, meta_ext, dh0, g, token):
    tr = min(LN_ROWS, SEQ)

    def ln_bwd(x_ref, dh_ref, g_ref, dx_ref, dg_ref, db_ref):
        @pl.when(pl.program_id(0) == 0)
        def _():
            dg_ref[...] = jnp.zeros_like(dg_ref)
            db_ref[...] = jnp.zeros_like(db_ref)

        xhat, rstd = _ln_stats(x_ref[...])
        dh = dh_ref[...]
        dx_ref[...] = _ln_bwd(dh, xhat, rstd, g_ref[...])
        dg_ref[...] += jnp.sum(dh * xhat, axis=0, keepdims=True)
        db_ref[...] += jnp.sum(dh, axis=0, keepdims=True)

    def body(x_ref, dh_ref, g_ref, token_ref, dx_ref, dg_ref, db_ref):
        ln_bwd(x_ref, dh_ref, g_ref, dx_ref, dg_ref, db_ref)

    def meta_body(m_ref, dh_ref, g_ref, dm_ref, dg_ref, db_ref):
        ln_bwd(m_ref, dh_ref, g_ref, dm_ref, dg_ref, db_ref)

    sums = [pltpu.HBM((1, D), F32), pltpu.HBM((1, D), F32)]
    dx, dg, db = pl.pallas_call(
        body, name="ln_in_bwd", grid=(SEQ // tr,),
        in_specs=[_rows(tr, D), _rows(tr, D), _const((1, D)), _const(TOKEN)],
        out_specs=[_rows(tr, D), _acc((1, D)), _acc((1, D))],
        out_shape=[pltpu.HBM((SEQ, D), F32)] + sums,
        compiler_params=_params(32, dimension_semantics=_seq()),
    )(*_hbm(x, dh0, g), token)
    dm, dg_m, db_m = pl.pallas_call(
        meta_body, name="ln_in_bwd_meta", grid=(1,),
        in_specs=[_const((BLK, D)), pl.BlockSpec((BLK, D), lambda i: (SEQ // BLK, 0)), _const((1, D))],
        out_specs=[_acc((BLK, D)), _acc((1, D)), _acc((1, D))],
        out_shape=[pltpu.HBM((BLK, D), F32)] + sums,
        compiler_params=_params(16, dimension_semantics=_seq()),
    )(*_hbm(meta_ext, dh0, g))
    return dx, dm, dg + dg_m, db + db_m


def _local_step(x, target, meta_full, ln_in_g, ln_in_b, b_in, wg2, bg2, sinks, gn, g1, b1, g2, b2,
                token, fetch_w_in, fetch_rest, ship_ffn, ship_w_in):
    row = lambda v: v.reshape(1, -1).astype(F32)
    meta_ext = jnp.pad(meta_full, ((META_OFF, BLK - CH), (0, 0)))
    b_in_p = jnp.pad(row(b_in), ((0, 0), (0, D_IN_P - D_IN)))
    wg2_p = jnp.pad(wg2, ((0, LANE - wg2.shape[0]), (0, 0))).astype(BF16)
    gn4 = row(gn)
    sinks = sinks.reshape(-1).astype(F32)

    h0 = _ln_in_fwd(x, meta_ext, row(ln_in_g), row(ln_in_b), token)
    w_in_t = fetch_w_in([h0])
    qs, ks, vs, qg, kg, vg, rg, glr, z = _in_proj(h0, w_in_t, b_in_p, wg2_p, row(bg2))
    o_s = _swa_fwd(sinks, qs, ks, vs)
    o_gla, st_all = _gla_fwd(qg, kg, vg, z)
    w_out, wg_t, wu_t, wd = fetch_rest([o_s, o_gla])
    o, pre1, h1 = _post_mix(o_s, o_gla, rg, h0, gn4, w_out, row(g1), row(b1))
    a, dgate, dup, dpre2, loss, dg2, db2 = _ffn_fwd_loss_bwd(h1, wg_t, wu_t, wd, target, row(g2), row(b2))
    dpre1, dg1, db1 = _ffn_bwd_input(dpre2, dgate, dup, pre1, wg_t, wu_t, row(g1))
    dwd = _atb(a, dpre2, "dw_down")
    dwg_t = _atb(dgate, h1, "dw_gate")
    dwu_t = _atb(dup, h1, "dw_up")
    dw_out = _atb(o, dpre1, "dw_out")
    token = ship_ffn(dict(w_out=dw_out, w_g=dwg_t, w_u=dwu_t, w_d=dwd))
    do_s, do_gla, drg, dgn = _out_bwd(dpre1, w_out, o_gla, rg, gn4, token)
    dqg, dkg, dvg, dz = _gla_bwd(qg, kg, vg, z, do_gla, st_all)
    dqs, dks, dvs, dsinks = _swa_bwd(sinks, qs, ks, vs, do_s)
    dproj, dh0, db_in_p, dbg2 = _in_bwd(dqs, dks, dvs, dqg, dkg, dvg, drg, dz, dpre1, w_in_t, wg2_p)
    token = ship_w_in(_atb(dproj, h0, "dw_in"))
    dwg2_p = _atb(glr, dz, "dw_gate_lr2")
    dx, dmeta_blk, dg_in, db_in_ln = _ln_in_bwd(x, meta_ext, dh0, row(ln_in_g), token)

    grads = dict(
        meta=dmeta_blk[META_OFF:CH], ln_in_g=dg_in, ln_in_b=db_in_ln, ln1_g=dg1, ln1_b=db1, ln2_g=dg2, ln2_b=db2,
        b_in=db_in_p[:, :D_IN], wg2=dwg2_p[:wg2.shape[0]], bg2=dbg2, sinks=dsinks[:, :SWA_HEADS], gn=dgn)
    return loss[0, 0], dx, grads


HBM = pl.BlockSpec(memory_space=pltpu.HBM)


def _place():
    return lax.axis_index("x"), lax.axis_index("y"), lax.axis_index("c")


def _other_chips(x, y):
    return [(1 - x, y), (x, 1 - y), (1 - x, 1 - y)]


def _dma_sems(n):
    return pltpu.SemaphoreType.DMA((n,))


def _comm_params():
    return pltpu.CompilerParams(has_side_effects=True)


def _gather_halves(shards):
    n = len(shards)

    def body(*refs):
        ins, outs = refs[:n], refs[n:2 * n]
        ici_send, ici_recv, d2d_send, d2d_recv = refs[2 * n:]
        x, y, c = _place()
        mine = 2 * x + y
        chips = _other_chips(x, y)

        def ici(a, j, src_chip):
            px, py = chips[j]
            return pltpu.make_async_remote_copy(ins[a].at[c], outs[a].at[src_chip, c], ici_send.at[3 * a + j],
                                                ici_recv.at[3 * a + j], device_id=(px, py, c), device_id_type=MESH)

        def d2d(a, j, half):
            px, py = chips[j]
            blk = outs[a].at[2 * px + py, half]
            return pltpu.make_async_remote_copy(blk, blk, d2d_send.at[3 * a + j], d2d_recv.at[3 * a + j],
                                                device_id=(x, y, 1 - c), device_id_type=MESH)

        sends = [ici(a, j, mine) for a in range(n) for j in range(3)]
        for cp in sends:
            cp.start()
        passed = []
        for a in range(n):
            for j, (px, py) in enumerate(chips):
                ici(a, j, 2 * px + py).wait_recv()
                fwd = d2d(a, j, c)
                fwd.start()
                passed.append(fwd)
        for a in range(n):
            for j in range(3):
                d2d(a, j, 1 - c).wait_recv()
        for cp in sends + passed:
            cp.wait_send()

    gathered = pl.pallas_call(
        body, name="gather_halves",
        in_specs=[HBM] * n, out_specs=[HBM] * n,
        out_shape=[pltpu.HBM((N_CHIPS,) + s.shape, s.dtype) for s in shards],
        scratch_shapes=[_dma_sems(3 * n)] * 4,
        compiler_params=_comm_params(),
    )(*_hbm(*shards))
    mine = 2 * lax.axis_index("x") + lax.axis_index("y")
    return [lax.dynamic_update_index_in_dim(g, s, mine, axis=0) for g, s in zip(gathered, shards)]


SEM = pl.BlockSpec(memory_space=pltpu.SEMAPHORE)


def _ici_copies(kind, srcs, lands, send_sems, recv_sems):
    x, y, c = _place()
    mine = 2 * x + y
    to_start, to_wait = [], []
    for a in range(len(srcs)):
        for j, (px, py) in enumerate(_other_chips(x, y)):
            peer = 2 * px + py
            if kind == "gather":
                src, there, here = srcs[a].at[c], lands[a].at[mine, c], lands[a].at[peer, c]
            else:
                src, there, here = srcs[a].at[peer], lands[a].at[mine], lands[a].at[peer]
            for dst, out in ((there, to_start), (here, to_wait)):
                out.append(pltpu.make_async_remote_copy(src, dst, send_sems.at[3 * a + j], recv_sems.at[3 * a + j],
                                                        device_id=(px, py, c), device_id_type=MESH))
    return to_start, to_wait


def _split_params():
    return pltpu.CompilerParams(has_side_effects=pltpu.SideEffectType.DATAFLOW_SIDE_EFFECTING)


def _ici_start(kind, srcs, land_shapes, after, name):
    n = len(srcs)
    lands = [pltpu.with_memory_space_constraint(lax.empty(s, a.dtype), pltpu.HBM) for s, a in zip(land_shapes, srcs)]

    def body(*refs):
        outs = refs[2 * n + len(after):]
        to_start, _ = _ici_copies(kind, refs[:n], refs[n:2 * n], outs[0], outs[1])
        for cp in to_start:
            cp.start()
        outs[-1][...] = jnp.zeros(TOKEN, F32)

    outs = pl.pallas_call(
        body, name=name, in_specs=[HBM] * (2 * n) + [pl.BlockSpec(memory_space=pl.ANY)] * len(after),
        out_specs=[SEM, SEM] + [HBM] * (2 * n) + [pl.BlockSpec(memory_space=pltpu.VMEM)],
        out_shape=[_dma_sems(3 * n)] * 2 + [pltpu.HBM(a.shape, a.dtype) for a in list(srcs) + lands]
        + [jax.ShapeDtypeStruct(TOKEN, F32)],
        input_output_aliases={i: 2 + i for i in range(2 * n)},
        compiler_params=_split_params(),
    )(*_hbm(*srcs), *lands, *after)
    return outs[:-1], outs[-1]


def _ici_wait(kind, handle, after, name):
    n = (len(handle) - 2) // 2

    def body(*refs):
        _, to_wait = _ici_copies(kind, refs[:n], refs[n:2 * n], refs[2 * n], refs[2 * n + 1])
        for cp in to_wait:
            cp.wait_send()
            cp.wait_recv()

    outs = pl.pallas_call(
        body, name=name, in_specs=[HBM] * (2 * n) + [SEM, SEM] + [pl.BlockSpec(memory_space=pl.ANY)] * len(after),
        out_specs=[HBM] * (2 * n), out_shape=[pltpu.HBM(a.shape, a.dtype) for a in handle[2:]],
        input_output_aliases={i: i for i in range(2 * n)},
        compiler_params=_split_params(),
    )(*handle[2:], handle[0], handle[1], *after)
    return list(outs[n:])


def _sibling_forward(lands, name):
    n = len(lands)

    def body(*refs):
        outs = refs[n:2 * n]
        send_sems, recv_sems = refs[2 * n:]
        x, y, c = _place()

        def copy(a, j, half):
            px, py = _other_chips(x, y)[j]
            blk = outs[a].at[2 * px + py, half]
            return pltpu.make_async_remote_copy(blk, blk, send_sems.at[3 * a + j], recv_sems.at[3 * a + j],
                                                device_id=(x, y, 1 - c), device_id_type=MESH)

        pairs = [(a, j) for a in range(n) for j in range(3)]
        for a, j in pairs:
            copy(a, j, c).start()
        for a, j in pairs:
            copy(a, j, 1 - c).wait_recv()
        for a, j in pairs:
            copy(a, j, c).wait_send()

    return pl.pallas_call(
        body, name=name, in_specs=[HBM] * n, out_specs=[HBM] * n,
        out_shape=[pltpu.HBM(a.shape, a.dtype) for a in lands],
        input_output_aliases={a: a for a in range(n)},
        scratch_shapes=[_dma_sems(3 * n)] * 2,
        compiler_params=_comm_params(),
    )(*_hbm(*lands))


def _sibling_exchange(grads, name):
    n = len(grads)

    def body(*refs):
        ins, outs = refs[:n], refs[n:2 * n]
        send_sems, recv_sems = refs[2 * n:]
        x, y, c = _place()
        copies = []
        for a in range(n):
            for s in range(N_CHIPS):
                cp = pltpu.make_async_remote_copy(ins[a].at[s, 1 - c], outs[a].at[s], send_sems.at[N_CHIPS * a + s],
                                                  recv_sems.at[N_CHIPS * a + s], device_id=(x, y, 1 - c),
                                                  device_id_type=MESH)
                cp.start()
                copies.append(cp)
        for cp in copies:
            cp.wait_recv()
        for cp in copies:
            cp.wait_send()

    return pl.pallas_call(
        body, name=name, in_specs=[HBM] * n, out_specs=[HBM] * n,
        out_shape=[pltpu.HBM((N_CHIPS, g.shape[2], D), F32) for g in grads],
        scratch_shapes=[_dma_sems(N_CHIPS * n)] * 2,
        compiler_params=_comm_params(),
    )(*_hbm(*grads))


def _add_halves(core, grad, recv, dtype, name):
    h = grad.shape[2]

    def body(c_ref, a_ref, b_ref, o_ref):
        o_ref[...] = (a_ref[0] + b_ref[...]).astype(dtype)

    return pl.pallas_call(
        body, name=name,
        grid_spec=pltpu.PrefetchScalarGridSpec(
            num_scalar_prefetch=1, grid=(N_CHIPS,),
            in_specs=[pl.BlockSpec((1, 1, h, D), lambda s, c: (s, c[0], 0, 0)),
                      pl.BlockSpec((1, h, D), lambda s, c: (s, 0, 0))],
            out_specs=pl.BlockSpec((1, h, D), lambda s, c: (s, 0, 0))),
        out_shape=pltpu.HBM((N_CHIPS, h, D), dtype),
        compiler_params=_params(16, dimension_semantics=_seq()),
    )(core, *_hbm(grad, recv))


def _chip_scatter(parts, with_own):
    n = len(parts)

    def body(*refs):
        ins, outs = refs[:n], refs[n:2 * n]
        send_sems, recv_sems, local_sems = refs[2 * n:]
        x, y, c = _place()
        mine = 2 * x + y
        chips = _other_chips(x, y)
        local = [pltpu.make_async_copy(ins[a].at[mine], outs[a].at[mine], local_sems.at[a]) for a in range(n)
                 if with_own[a]]
        for cp in local:
            cp.start()
        sends = []
        for a in range(n):
            for j, (px, py) in enumerate(chips):
                cp = pltpu.make_async_remote_copy(ins[a].at[2 * px + py], outs[a].at[mine], send_sems.at[3 * a + j],
                                                  recv_sems.at[3 * a + j], device_id=(px, py, c), device_id_type=MESH)
                cp.start()
                sends.append(cp)
        for a in range(n):
            for j, (px, py) in enumerate(chips):
                pltpu.make_async_remote_copy(ins[a].at[mine], outs[a].at[2 * px + py], send_sems.at[3 * a + j],
                                             recv_sems.at[3 * a + j], device_id=(px, py, c),
                                             device_id_type=MESH).wait_recv()
        for cp in sends:
            cp.wait_send()
        for cp in local:
            cp.wait()

    return pl.pallas_call(
        body, name="chip_scatter", in_specs=[HBM] * n, out_specs=[HBM] * n,
        out_shape=[pltpu.HBM(p.shape, p.dtype) for p in parts],
        scratch_shapes=[_dma_sems(3 * n)] * 2 + [_dma_sems(n)],
        compiler_params=_comm_params(),
    )(*_hbm(*parts))


def _sum_chips(slots, first, rest, name):
    h = first.shape[1]

    def body(i_ref, a_ref, b_ref, c_ref, d_ref, o_ref):
        o_ref[...] = ((a_ref[...].astype(F32) + b_ref[...].astype(F32)) + c_ref[...].astype(F32)) + d_ref[...].astype(F32)

    slab = lambda k: pl.BlockSpec((1, h, D), lambda i, ix: (ix[k], 0, 0))
    return pl.pallas_call(
        body, name=name,
        grid_spec=pltpu.PrefetchScalarGridSpec(num_scalar_prefetch=1, grid=(1,),
                                               in_specs=[slab(0), slab(1), slab(2), slab(3)], out_specs=slab(4)),
        out_shape=pltpu.HBM((2, h, D), F32),
        compiler_params=_params(16, dimension_semantics=_seq()),
    )(slots, *_hbm(first, rest, rest, rest))


def _join_halves(halves):
    n = len(halves)

    def body(*refs):
        outs = refs[n:2 * n]
        send_sems, recv_sems = refs[2 * n:]
        x, y, c = _place()

        def copy(a, slab):
            return pltpu.make_async_remote_copy(outs[a].at[slab], outs[a].at[slab], send_sems.at[a], recv_sems.at[a],
                                                device_id=(x, y, 1 - c), device_id_type=MESH)

        for a in range(n):
            copy(a, c).start()
        for a in range(n):
            copy(a, 1 - c).wait_recv()
        for a in range(n):
            copy(a, c).wait_send()

    return pl.pallas_call(
        body, name="join_halves", in_specs=[HBM] * n, out_specs=[HBM] * n,
        out_shape=[pltpu.HBM(h.shape, F32) for h in halves],
        input_output_aliases={a: a for a in range(n)},
        scratch_shapes=[_dma_sems(n)] * 2,
        compiler_params=_comm_params(),
    )(*_hbm(*halves))


def _chip_partials(grads, wire_dtypes, names):
    core = lax.axis_index("c").astype(jnp.int32).reshape(1)
    recv = _sibling_exchange(grads, "sibling_exchange_" + names[0])
    return [_add_halves(core, g, r, dt, "add_halves_" + nm) for g, r, dt, nm in zip(grads, recv, wire_dtypes, names)]


def _finish_reduce(parts, got, same_order, names):
    x, y, c = _place()
    others = [2 * px + py for px, py in _other_chips(x, y)]
    own_first = jnp.stack([2 * x + y] + others + [c]).astype(jnp.int32)
    chip_order = jnp.stack([0 * c, 0 * c + 1, 0 * c + 2, 0 * c + 3, c]).astype(jnp.int32)
    halves = [_sum_chips(chip_order, q, q, "sum_chips_" + nm) if fixed else _sum_chips(own_first, p, q, "sum_chips_" + nm)
              for p, q, fixed, nm in zip(parts, got, same_order, names)]
    return [f.reshape(2 * f.shape[1], D) for f in _join_halves(halves)]


def _adamw(w, g, m, v, name):
    rows, cols = w.shape
    if rows % 8 == 0:
        tr = max(t for t in range(8, 257, 8) if rows % t == 0)
        grid, blk = (rows // tr,), pl.BlockSpec((tr, cols), lambda i: (i, 0))
    else:
        grid, blk = (cols // 256,), pl.BlockSpec((rows, 256), lambda i: (0, i))

    def body(w_ref, g_ref, m_ref, v_ref, d_ref, nm_ref, nv_ref):
        gg = g_ref[...]
        nm = ADAM_B1 * m_ref[...] + (1.0 - ADAM_B1) * gg
        nv = ADAM_B2 * v_ref[...] + (1.0 - ADAM_B2) * (gg * gg)
        m_hat = nm / (1.0 - ADAM_B1 ** ADAM_STEP)
        v_hat = nv / (1.0 - ADAM_B2 ** ADAM_STEP)
        d_ref[...] = -ADAM_LR * (m_hat / (jnp.sqrt(v_hat) + ADAM_EPS) + ADAM_WD * w_ref[...])
        nm_ref[...] = nm
        nv_ref[...] = nv

    return pl.pallas_call(
        body, name=name, grid=grid,
        in_specs=[blk] * 4, out_specs=[blk] * 3,
        out_shape=[pltpu.HBM(w.shape, F32)] * 3,
        compiler_params=_params(32, dimension_semantics=_seq()),
    )(*_hbm(w, g, m, v))


def _flat_rows(v, rows):
    flat = v.reshape(-1).astype(F32)
    return jnp.pad(flat, (0, rows * D - flat.shape[0])).reshape(rows, D)


def _small_pack(gr):
    tail = jnp.concatenate([gr["bg2"].reshape(-1), gr["sinks"].reshape(-1), gr["gn"].reshape(-1), gr["loss"].reshape(-1)])
    rows = [gr["meta"].reshape(N_META, D)] + [gr[k].reshape(1, D) for k in
                                              ("ln_in_g", "ln_in_b", "ln1_g", "ln1_b", "ln2_g", "ln2_b")]
    rows += [_flat_rows(gr["b_in"], 3), _flat_rows(gr["wg2"], 4), _flat_rows(tail, 1)]
    packed = jnp.concatenate(rows, axis=0)
    return jnp.pad(packed, ((0, SMALL_ROWS - packed.shape[0]), (0, 0)))


def _small_unpack(p):
    flat = lambda r0, n, size: p[r0:r0 + n].reshape(-1)[:size]
    tail = p[29]
    return dict(meta=p[0:N_META], ln_in_g=p[16], ln_in_b=p[17], ln1_g=p[18], ln1_b=p[19], ln2_g=p[20], ln2_b=p[21],
                b_in=flat(22, 3, D_IN), wg2=flat(25, 4, 16 * 256).reshape(16, 256), bg2=tail[0:256],
                sinks=tail[256:256 + SWA_HEADS], gn=tail[256 + SWA_HEADS:256 + SWA_HEADS + DV],
                loss=tail[256 + SWA_HEADS + DV])


BIG = ("w_in", "w_out", "w_g", "w_u", "w_d")


def kernel(x, meta_tokens, ln_in_g, ln_in_b, w_in, b_in, w_gate_lr2, b_gate_lr2, attn_sinks, gla_norm_g, w_out, ln1_g, ln1_b, w_ffn_gate, w_ffn_up, w_ffn_down, ln2_g, ln2_b, loss_target, m_meta_tokens, m_ln_in_g, m_ln_in_b, m_w_in, m_b_in, m_w_gate_lr2, m_b_gate_lr2, m_attn_sinks, m_gla_norm_g, m_w_out, m_ln1_g, m_ln1_b, m_w_ffn_gate, m_w_ffn_up, m_w_ffn_down, m_ln2_g, m_ln2_b, v_meta_tokens, v_ln_in_g, v_ln_in_b, v_w_in, v_b_in, v_w_gate_lr2, v_b_gate_lr2, v_attn_sinks, v_gla_norm_g, v_w_out, v_ln1_g, v_ln1_b, v_w_ffn_gate, v_w_ffn_up, v_w_ffn_down, v_ln2_g, v_ln2_b):
    chip = 2 * lax.axis_index("x") + lax.axis_index("y")

    halves = lambda a: a.reshape(2, a.shape[0] // 2, a.shape[1])
    r_in = SHARD_ROWS["w_in"]
    first = [halves(jnp.pad(w_in[0].T.astype(BF16), ((0, W_IN_WIN - r_in), (0, 0))))]
    rest = [halves(a) for a in (w_out[0].astype(BF16), w_ffn_gate[0].T.astype(BF16), w_ffn_up[0].T.astype(BF16),
                                w_ffn_down[0].astype(BF16))]
    lands = lambda arrs: [(N_CHIPS,) + a.shape for a in arrs]
    g_meta, g_wg2 = _gather_halves([halves(meta_tokens), halves(w_gate_lr2[0])])
    first_handle, first_token = _ici_start("gather", first, lands(first), [g_meta], "gather_w_in_start")
    rest_handle, token = _ici_start("gather", rest, lands(rest), [first_token], "gather_rest_start")
    meta_full = jnp.concatenate([g_meta[s].reshape(N_META, -1) for s in range(N_CHIPS)], axis=1)
    wg2_full = jnp.concatenate([g_wg2[s].reshape(w_gate_lr2.shape[1], -1) for s in range(N_CHIPS)], axis=1)

    def fetch(handle, shards, after, name):
        got = _sibling_forward(_ici_wait("gather", handle, after, name + "_wait"), name + "_forward")
        return [lax.dynamic_update_index_in_dim(g, s, chip, axis=0) for g, s in zip(got, shards)]

    def fetch_w_in(after):
        g_in, = fetch(first_handle, first, after, "gather_w_in")
        return jnp.pad(g_in.reshape(N_CHIPS, W_IN_WIN, D)[:, :r_in].reshape(D_IN, D), ((0, D_IN_P - D_IN), (0, 0)))

    def fetch_rest(after):
        return [g.reshape(-1, D) for g in fetch(rest_handle, rest, after, "gather_rest")]

    sent = {}

    def ship(key, grads, names):
        parts = _chip_partials([g.reshape(N_CHIPS, 2, -1, D) for g in grads], [BF16] * len(grads), names)
        handle, ship_token = _ici_start("scatter", parts, [p.shape for p in parts], [], "scatter_" + key + "_start")
        sent[key] = (parts, handle)
        return ship_token

    def ship_ffn(g):
        return ship("ffn", [g[k] for k in BIG[1:]], list(BIG[1:]))

    def ship_w_in(dw_in_t):
        win_start = [s * r_in // BF16_ROWS * BF16_ROWS for s in range(N_CHIPS)]
        return ship("w_in", [jnp.stack([dw_in_t[st:st + W_IN_WIN] for st in win_start])], ["w_in"])

    loss_part, dx, gr = _local_step(
        x[0], loss_target[0], meta_full, ln_in_g, ln_in_b, b_in[0], wg2_full, b_gate_lr2[0], attn_sinks[0],
        gla_norm_g[0], ln1_g[0], ln1_b[0], ln2_g[0], ln2_b[0], token, fetch_w_in, fetch_rest, ship_ffn, ship_w_in)
    ffn_got = _ici_wait("scatter", sent["ffn"][1], [dx], "scatter_ffn_wait")
    w_in_got = _ici_wait("scatter", sent["w_in"][1], [dx], "scatter_w_in_wait")

    gr["loss"] = loss_part
    small = jnp.broadcast_to(_small_pack(gr), (N_CHIPS, SMALL_ROWS, D)).reshape(N_CHIPS, 2, -1, D)
    small_parts = _chip_partials([small], [F32], ["small"])
    small_got = list(_chip_scatter(small_parts, [True]))
    red = _finish_reduce(sent["w_in"][0] + sent["ffn"][0] + small_parts, w_in_got + ffn_got + small_got,
                         [False] * len(BIG) + [True], list(BIG) + ["small"])

    big_g = dict(zip(BIG, red))
    big_g["w_in"] = lax.dynamic_slice_in_dim(red[0], chip * (r_in % BF16_ROWS), r_in, axis=0)
    sg = _small_unpack(red[-1])
    col = lambda a, width: lax.dynamic_slice_in_dim(a, chip * width, width, axis=1)
    grads = dict(
        meta_tokens=col(sg["meta"], D // N_CHIPS), ln_in_g=sg["ln_in_g"], ln_in_b=sg["ln_in_b"],
        w_in=big_g["w_in"].T[None], b_in=sg["b_in"][None], w_gate_lr2=col(sg["wg2"], 256 // N_CHIPS)[None],
        b_gate_lr2=sg["bg2"][None], attn_sinks=sg["sinks"][None], gla_norm_g=sg["gn"][None],
        w_out=big_g["w_out"][None], ln1_g=sg["ln1_g"][None], ln1_b=sg["ln1_b"][None],
        w_ffn_gate=big_g["w_g"].T[None], w_ffn_up=big_g["w_u"].T[None], w_ffn_down=big_g["w_d"][None],
        ln2_g=sg["ln2_g"][None], ln2_b=sg["ln2_b"][None])
    weights = dict(meta_tokens=meta_tokens, ln_in_g=ln_in_g, ln_in_b=ln_in_b, w_in=w_in, b_in=b_in,
                   w_gate_lr2=w_gate_lr2, b_gate_lr2=b_gate_lr2, attn_sinks=attn_sinks, gla_norm_g=gla_norm_g,
                   w_out=w_out, ln1_g=ln1_g, ln1_b=ln1_b, w_ffn_gate=w_ffn_gate, w_ffn_up=w_ffn_up,
                   w_ffn_down=w_ffn_down, ln2_g=ln2_g, ln2_b=ln2_b)
    m_in = dict(meta_tokens=m_meta_tokens, ln_in_g=m_ln_in_g, ln_in_b=m_ln_in_b, w_in=m_w_in, b_in=m_b_in,
                w_gate_lr2=m_w_gate_lr2, b_gate_lr2=m_b_gate_lr2, attn_sinks=m_attn_sinks, gla_norm_g=m_gla_norm_g,
                w_out=m_w_out, ln1_g=m_ln1_g, ln1_b=m_ln1_b, w_ffn_gate=m_w_ffn_gate, w_ffn_up=m_w_ffn_up,
                w_ffn_down=m_w_ffn_down, ln2_g=m_ln2_g, ln2_b=m_ln2_b)
    v_in = dict(meta_tokens=v_meta_tokens, ln_in_g=v_ln_in_g, ln_in_b=v_ln_in_b, w_in=v_w_in, b_in=v_b_in,
                w_gate_lr2=v_w_gate_lr2, b_gate_lr2=v_b_gate_lr2, attn_sinks=v_attn_sinks, gla_norm_g=v_gla_norm_g,
                w_out=v_w_out, ln1_g=v_ln1_g, ln1_b=v_ln1_b, w_ffn_gate=v_w_ffn_gate, w_ffn_up=v_w_ffn_up,
                w_ffn_down=v_w_ffn_down, ln2_g=v_ln2_g, ln2_b=v_ln2_b)
    names = list(weights)
    big_names = ("w_in", "w_out", "w_ffn_gate", "w_ffn_up", "w_ffn_down")

    delta, new_m, new_v = {}, {}, {}
    for k, kk in zip(big_names, BIG):
        flip = (lambda a: a.T) if kk in ("w_in", "w_g", "w_u") else (lambda a: a)
        d_, m_, v_ = _adamw(flip(weights[k][0]), big_g[kk], flip(m_in[k][0]), flip(v_in[k][0]), "adamw_" + k)
        delta[k], new_m[k], new_v[k] = (flip(t)[None] for t in (d_, m_, v_))
    small_names = [k for k in names if k not in big_names]
    sizes = [weights[k].size for k in small_names]
    rows_small = -(-sum(sizes) // D)
    rows_small += -rows_small % 8
    cat = lambda src: _flat_rows(jnp.concatenate([src[k].reshape(-1) for k in small_names]), rows_small)
    d_, m_, v_ = _adamw(cat(weights), cat(grads), cat(m_in), cat(v_in), "adamw_small")
    off = 0
    for k, n in zip(small_names, sizes):
        for dst, src in ((delta, d_), (new_m, m_), (new_v, v_)):
            dst[k] = src.reshape(-1)[off:off + n].reshape(weights[k].shape)
        off += n
    grads = {k: grads[k].reshape(weights[k].shape) for k in names}

    return (sg["loss"], dx[None], *[grads[k] for k in names], *[delta[k] for k in names], *[new_m[k] for k in names],
            *[new_v[k] for k in names])
```

```python
import functools

import jax
import jax.numpy as jnp
from jax import lax
from jax.experimental import pallas as pl
from jax.experimental.pallas import tpu as pltpu

F32 = jnp.float32
BF16 = jnp.bfloat16
MESH = pl.DeviceIdType.MESH

D = 1024
SEQ = 4096
N_META = 16
SWA_HEADS, SWA_KV_HEADS, DH = 8, 2, 64
WINDOW = 128
GLA_HEADS, DK, DV = 4, 64, 128
GLA_TAU = 16.0
CH = 64
D_FF = 2816
D_IN = 2320
LN_EPS = 1e-5
RMS_EPS = 1e-6
ALPHA = 2.0 ** 0.25
NEG = -1e30
ADAM_LR, ADAM_B1, ADAM_B2, ADAM_EPS, ADAM_WD, ADAM_STEP = 0.001, 0.9, 0.999, 1e-8, 0.01, 10
O_QS, O_KS, O_VS, O_QG, O_KG, O_VG, O_RG, O_LR = 0, 512, 640, 768, 1024, 1280, 1792, 2304

LANE = 128
BLK = WINDOW
D_IN_P = D_IN + LANE - 16
META_OFF = CH - N_META
HEAD_POS = (0, 4, 1, 5, 2, 6, 3, 7)
LN_ROWS = 512
TOKEN = (8, LANE)
N_CHIPS = 4
SHARD_ROWS = dict(w_in=D_IN // N_CHIPS, w_out=D // N_CHIPS, w_g=D_FF // N_CHIPS, w_u=D_FF // N_CHIPS,
                  w_d=D_FF // N_CHIPS)
SMALL_ROWS = 48
BF16_ROWS = 16
W_IN_WIN = -(-SHARD_ROWS["w_in"] // (2 * BF16_ROWS)) * 2 * BF16_ROWS
VMEM_CAP_MB = 64


def _lp():
    return SEQ + BLK


def _row_tile(cap):
    lp = _lp()
    return max(t for t in range(16, cap + 1, 16) if lp % t == 0)


def _params(vmem_mb, **kw):
    assert vmem_mb <= VMEM_CAP_MB - 6
    return pltpu.CompilerParams(vmem_limit_bytes=vmem_mb << 20, **kw)


def _seq(n=1):
    return ("arbitrary",) * n


def _const(shape):
    return pl.BlockSpec(shape, lambda *_: (0,) * len(shape), pipeline_mode=pl.Buffered(1))


def _acc(shape):
    return pl.BlockSpec(shape, lambda *_: (0,) * len(shape))


def _rows(tm, width):
    return pl.BlockSpec((tm, width), lambda i: (i, 0))


def _dot(a, b):
    return jnp.dot(a.astype(BF16), b.astype(BF16), preferred_element_type=F32)


def _dot_nt(a, b):
    return lax.dot_general(a.astype(BF16), b.astype(BF16), (((1,), (1,)), ((), ())), preferred_element_type=F32)


def _dot_tn(a, b):
    return lax.dot_general(a.astype(BF16), b.astype(BF16), (((0,), (0,)), ((), ())), preferred_element_type=F32)


def _dot_exact(a, b):
    return jnp.dot(a, b, precision=lax.Precision.HIGHEST, preferred_element_type=F32)


def _ln_stats(x):
    mu = jnp.mean(x, axis=-1, keepdims=True)
    xc = x - mu
    rstd = lax.rsqrt(jnp.mean(xc * xc, axis=-1, keepdims=True) + LN_EPS)
    return xc * rstd, rstd


def _ln_bwd(dy, xhat, rstd, g):
    dxh = dy * g
    return rstd * (dxh - jnp.mean(dxh, axis=-1, keepdims=True) - xhat * jnp.mean(dxh * xhat, axis=-1, keepdims=True))


def _sigmoid(x):
    return 1.0 / (1.0 + jnp.exp(-x))


def _iota(shape, dim):
    return lax.broadcasted_iota(jnp.int32, shape, dim)


def _hbm(*arrays):
    return tuple(pltpu.with_memory_space_constraint(a, pltpu.HBM) for a in arrays)


def _ln_in_fwd(x, meta_ext, g, b, token):
    tr = min(LN_ROWS, SEQ)

    def ln(x_ref, g_ref, b_ref, h_ref):
        xhat, _ = _ln_stats(x_ref[...])
        h_ref[...] = xhat * g_ref[...] + b_ref[...]

    def body(x_ref, g_ref, b_ref, token_ref, h_ref):
        ln(x_ref, g_ref, b_ref, h_ref)

    h_real = pl.pallas_call(
        body, name="ln_in_fwd", grid=(SEQ // tr,),
        in_specs=[_rows(tr, D), _const((1, D)), _const((1, D)), _const(TOKEN)],
        out_specs=_rows(tr, D),
        out_shape=pltpu.HBM((_lp(), D), F32),
        compiler_params=_params(32, dimension_semantics=_seq()),
    )(*_hbm(x, g, b), token)

    def meta_body(m_ref, g_ref, b_ref, real_ref, h_ref):
        ln(m_ref, g_ref, b_ref, h_ref)

    return pl.pallas_call(
        meta_body, name="ln_in_fwd_meta", grid=(1,),
        in_specs=[_const((BLK, D)), _const((1, D)), _const((1, D)), pl.BlockSpec(memory_space=pl.ANY)],
        out_specs=pl.BlockSpec((BLK, D), lambda i: (SEQ // BLK, 0)),
        out_shape=pltpu.HBM((_lp(), D), F32),
        input_output_aliases={3: 0},
        compiler_params=_params(16, dimension_semantics=_seq()),
    )(*_hbm(meta_ext, g, b, h_real))


def _in_proj(h0, w_in_t, b_in_p, wg2_p, bg2):
    tm = _row_tile(384)
    lp = _lp()
    widths = (512, 128, 128, 256, 256, 512, 512, 128)
    offs = (O_QS, O_KS, O_VS, O_QG, O_KG, O_VG, O_RG, O_LR)

    def body(h_ref, w_ref, b_ref, wg2_ref, bg2_ref, *outs):
        proj = _dot_nt(h_ref[...], w_ref[...]) + b_ref[...]
        for pos, h in enumerate(HEAD_POS):
            outs[0][:, pos * DH:(pos + 1) * DH] = proj[:, O_QS + h * DH:O_QS + (h + 1) * DH]
        for o_ref, off, wd in zip(outs[1:8], offs[1:], widths[1:]):
            o_ref[...] = proj[:, off:off + wd]
        outs[8][...] = _dot(proj[:, O_LR:O_LR + LANE], wg2_ref[...]) + bg2_ref[...]

    return pl.pallas_call(
        body, name="in_proj", grid=(lp // tm,),
        in_specs=[_rows(tm, D), _const((D_IN_P, D)), _const((1, D_IN_P)), _const((LANE, 256)), _const((1, 256))],
        out_specs=[_rows(tm, w) for w in widths] + [_rows(tm, 256)],
        out_shape=[pltpu.HBM((lp, w), F32) for w in widths] + [pltpu.HBM((lp, 256), F32)],
        compiler_params=_params(40, dimension_semantics=_seq()),
    )(*_hbm(h0, w_in_t, b_in_p, wg2_p, bg2))


def _swa_masks(n):
    nb = SEQ // BLK
    is_meta = n == nb
    ri = _iota((BLK, BLK), 0)
    cj = _iota((BLK, BLK), 1)
    meta_col = ((cj >= META_OFF) & (cj < CH)).astype(jnp.int32)
    meta_q = meta_col * ((cj <= ri) & (ri < CH)).astype(jnp.int32)
    valid_m = jnp.where(is_meta, meta_q, meta_col) > 0
    dist_m = jnp.where(is_meta, ri - cj, n * BLK + ri + CH - cj).astype(F32)
    valid_p = jnp.where((n >= 1) & (n < nb), (cj > ri).astype(jnp.int32), 0) > 0
    dist_p = (ri + BLK - cj).astype(F32)
    valid_c = jnp.where(n < nb, (cj <= ri).astype(jnp.int32), 0) > 0
    dist_c = (ri - cj).astype(F32)
    return (dist_m, dist_p, dist_c), (valid_m, valid_p, valid_c)


def _swa_bias(n):
    dists, valids = _swa_masks(n)
    return (jnp.concatenate([-d for d in dists], axis=1),
            jnp.concatenate([jnp.where(v, 0.0, NEG) for v in valids], axis=1))


def _swa_half(ref, pos, scale=1.0):
    col = ref[:, (pos // 2) * LANE:(pos // 2 + 1) * LANE]
    lane = _iota((BLK, LANE), 1)
    mine = lane < DH if pos % 2 == 0 else lane >= DH
    return jnp.where(mine, col * scale, 0.0).astype(BF16)


def _swa_merge(even, odd):
    return jnp.where(_iota((BLK, LANE), 1) < DH, even, odd)


def _swa_softmax(t, sink):
    m = jnp.maximum(jnp.max(t, axis=-1, keepdims=True), sink)
    e = jnp.exp(t - m)
    e_sink = jnp.exp(sink - m)
    inv = 1.0 / (jnp.sum(e, axis=-1, keepdims=True) + e_sink)
    return e * inv, e_sink * inv


def _swa_kv_specs(width):
    nb = SEQ // BLK
    return [pl.BlockSpec((BLK, width), lambda n: (nb, 0)),
            pl.BlockSpec((BLK, width), lambda n: (jnp.clip(n - 1, 0, nb - 1), 0)),
            pl.BlockSpec((BLK, width), lambda n: (jnp.minimum(n, nb), 0))]


def _swa_fwd(sinks, qs, ks, vs):
    nb = SEQ // BLK
    heads = range(SWA_HEADS)

    def body(sink_ref, q_ref, km_ref, kp_ref, kc_ref, vm_ref, vp_ref, vc_ref, o_ref):
        negdist, maskbias = _swa_bias(pl.program_id(0))
        k_all = jnp.concatenate([km_ref[...], kp_ref[...], kc_ref[...]], axis=0).astype(BF16)
        v_all = jnp.concatenate([vm_ref[...], vp_ref[...], vc_ref[...]], axis=0).astype(BF16)
        q = [_swa_half(q_ref, pos, DH ** -0.5) for pos in heads]
        t = [_dot_nt(q[pos], k_all) + (2.0 ** -(HEAD_POS[pos] + 1) * negdist + maskbias) for pos in heads]
        p = [_swa_softmax(t[pos], sink_ref[HEAD_POS[pos]])[0].astype(BF16) for pos in heads]
        o = [_dot(p[pos], v_all) for pos in heads]
        for col in range(SWA_HEADS // 2):
            o_ref[:, col * LANE:(col + 1) * LANE] = _swa_merge(o[2 * col], o[2 * col + 1])

    kvw = SWA_KV_HEADS * DH
    return pl.pallas_call(
        body, name="swa_fwd", grid=(nb + 1,),
        in_specs=[pl.BlockSpec(memory_space=pltpu.SMEM), _rows(BLK, SWA_HEADS * DH)] + _swa_kv_specs(kvw) + _swa_kv_specs(kvw),
        out_specs=_rows(BLK, SWA_HEADS * DH),
        out_shape=pltpu.HBM((_lp(), SWA_HEADS * DH), F32),
        compiler_params=_params(16, dimension_semantics=_seq()),
    )(sinks, *_hbm(qs, ks, ks, ks, vs, vs, vs))


GLA_PER_STEP = BLK // CH


def _gla_block(s):
    nb = SEQ // BLK
    return jnp.where(s == 0, nb, s - 1)


def _gla_rowmask(s):
    ri = _iota((BLK, 1), 0)
    m = jnp.where(s == 0, ((ri >= META_OFF) & (ri < CH)).astype(jnp.int32), 1)
    return (m > 0).astype(F32) + jnp.zeros((BLK, 1), F32)


def _gla_chunk_masks():
    r, c = _iota((BLK, BLK), 0), _iota((BLK, BLK), 1)
    same = ((r < CH) & (c < CH)) | ((r >= CH) & (c >= CH))
    return same & (r >= c), same & (r <= c), same


def _gla_decay(z, rmask):
    log_g = (jnp.minimum(z, 0.0) - jnp.log1p(jnp.exp(-jnp.abs(z)))) * (rmask / GLA_TAU)
    lower, _, same = _gla_chunk_masks()
    return _dot_exact(lower.astype(F32), log_g), _dot_exact(same.astype(F32), log_g)


def _gla_slices(c, h):
    return slice(c * CH, (c + 1) * CH), slice(h * DK, (h + 1) * DK), slice(h * DV, (h + 1) * DV)


def _gla_fwd(qg, kg, vg, z):
    steps = SEQ // BLK + 1
    kw, vw = GLA_HEADS * DK, GLA_HEADS * DV
    pairs = [(c, h) for c in range(GLA_PER_STEP) for h in range(GLA_HEADS)]

    def body(q_ref, k_ref, v_ref, z_ref, o_ref, st_ref, st):
        s = pl.program_id(0)

        @pl.when(s == 0)
        def _():
            st[...] = jnp.zeros_like(st)

        rmask = _gla_rowmask(s)
        b, b_last = _gla_decay(z_ref[...], rmask)
        q = q_ref[...] * (rmask * DK ** -0.5)
        k = k_ref[...] * rmask
        v = v_ref[...] * rmask
        qe = q * jnp.exp(b)
        ke = k * jnp.exp(-b)
        kd = k * jnp.exp(b_last - b)
        e_last = jnp.exp(b_last)
        causal = _iota((CH, CH), 0) >= _iota((CH, CH), 1)
        a, upd, intra = {}, {}, {}
        for c, h in pairs:
            rows, ks, vs_ = _gla_slices(c, h)
            a[c, h] = jnp.where(causal, _dot_nt(qe[rows, ks], ke[rows, ks]), 0.0)
            upd[c, h] = _dot_tn(v[rows, vs_], kd[rows, ks])
        for c, h in pairs:
            rows, ks, vs_ = _gla_slices(c, h)
            intra[c, h] = _dot(a[c, h], v[rows, vs_])
        state = st[...]
        for c in range(GLA_PER_STEP):
            st_ref[0, c] = state
            for h in range(GLA_HEADS):
                rows, ks, vs_ = _gla_slices(c, h)
                o_ref[rows, vs_] = intra[c, h] + _dot_nt(qe[rows, ks], state[:, ks])
            state = state * e_last[c * CH:c * CH + 1] + jnp.concatenate([upd[c, h] for h in range(GLA_HEADS)], axis=1)
        st[...] = state

    blk = lambda w: pl.BlockSpec((BLK, w), lambda s: (_gla_block(s), 0))
    return pl.pallas_call(
        body, name="gla_fwd", grid=(steps,),
        in_specs=[blk(kw), blk(kw), blk(vw), blk(kw)],
        out_specs=[blk(vw), pl.BlockSpec((1, GLA_PER_STEP, DV, kw), lambda s: (s, 0, 0, 0))],
        out_shape=[pltpu.HBM((_lp(), vw), F32), pltpu.HBM((steps, GLA_PER_STEP, DV, kw), F32)],
        scratch_shapes=[pltpu.VMEM((DV, kw), F32)],
        compiler_params=_params(16, dimension_semantics=_seq()),
    )(*_hbm(qg, kg, vg, z))


def _post_mix(o_s, o_gla, r_g, h0, gn4, w_out, g1, b1):
    tm = _row_tile(384)
    lp = _lp()

    def body(os_ref, og_ref, r_ref, h0_ref, gn_ref, w_ref, g_ref, b_ref, o_ref, pre_ref, h1_ref):
        for pos, h in enumerate(HEAD_POS):
            o_ref[:, h * DH:(h + 1) * DH] = os_ref[:, pos * DH:(pos + 1) * DH].astype(BF16)
        for h in range(GLA_HEADS):
            hs = slice(h * DV, (h + 1) * DV)
            xg = og_ref[:, hs]
            n = xg * lax.rsqrt(jnp.mean(xg * xg, axis=-1, keepdims=True) + RMS_EPS) * gn_ref[...]
            r = r_ref[:, hs]
            o_ref[:, 512 + h * DV:512 + (h + 1) * DV] = (n * (r * _sigmoid(r))).astype(BF16)
        pre = ALPHA * h0_ref[...] + _dot(o_ref[...], w_ref[...])
        pre_ref[...] = pre
        xhat, _ = _ln_stats(pre)
        h1_ref[...] = xhat * g_ref[...] + b_ref[...]

    return pl.pallas_call(
        body, name="post_mix", grid=(lp // tm,),
        in_specs=[_rows(tm, 512), _rows(tm, 512), _rows(tm, 512), _rows(tm, D), _const((1, DV)), _const((D, D)),
                  _const((1, D)), _const((1, D))],
        out_specs=[_rows(tm, D), _rows(tm, D), _rows(tm, D)],
        out_shape=[pltpu.HBM((lp, D), BF16), pltpu.HBM((lp, D), F32),
                   pltpu.HBM((lp, D), F32)],
        compiler_params=_params(32, dimension_semantics=_seq()),
    )(*_hbm(o_s, o_gla, r_g, h0, gn4, w_out, g1, b1))


def _ffn_fwd_loss_bwd(h1, wg_t, wu_t, wd, target, g2, b2):
    lp = _lp()
    tm = max(t for t in range(BLK, 384 + 1, BLK) if lp % t == 0)
    steps = lp // tm
    last_blk = SEQ // BLK - 1
    half = D_FF // 2
    n_t = tm // BLK

    def body(*refs):
        h_ref, wg_ref, wu_ref, wd_ref = refs[:4]
        t_refs = refs[4:4 + n_t]
        g2_ref, b2_ref, a_ref, dgate_ref, dup_ref, dp_ref, loss_ref, dg_ref, db_ref, g_s, u_s, acc = refs[4 + n_t:]
        i = pl.program_id(0)

        @pl.when(i == 0)
        def _():
            acc[...] = jnp.zeros_like(acc)
            dg_ref[...] = jnp.zeros_like(dg_ref)
            db_ref[...] = jnp.zeros_like(db_ref)

        h = h_ref[...]
        hb = h.astype(BF16)
        pre = ALPHA * h
        for j in range(2):
            cols = slice(j * half, (j + 1) * half)
            g = _dot_nt(hb, wg_ref[cols, :])
            u = _dot_nt(hb, wu_ref[cols, :])
            g_s[:, cols] = g
            u_s[:, cols] = u
            pre = pre + _dot(g * _sigmoid(g) * u, wd_ref[cols, :])
        xhat, rstd = _ln_stats(pre)
        real = i * tm + _iota((tm, 1), 0) < SEQ
        target_rows = jnp.concatenate([t[...] for t in t_refs], axis=0)
        diff = jnp.where(real, xhat * g2_ref[...] + b2_ref[...] - target_rows, 0.0)
        acc[...] += jnp.sum(diff * diff, axis=0, keepdims=True)
        dy = diff * (1.0 / D)
        dpre = _ln_bwd(dy, xhat, rstd, g2_ref[...])
        dp_ref[...] = dpre
        dg_ref[...] += jnp.sum(dy * xhat, axis=0, keepdims=True)
        db_ref[...] += jnp.sum(dy, axis=0, keepdims=True)
        dpb = dpre.astype(BF16)
        for j in range(2):
            cols = slice(j * half, (j + 1) * half)
            g, u = g_s[:, cols], u_s[:, cols]
            sg = _sigmoid(g)
            silu = g * sg
            da = _dot_nt(dpb, wd_ref[cols, :])
            a_ref[:, cols] = (silu * u).astype(BF16)
            dgate_ref[:, cols] = (da * u * (sg * (1.0 + g * (1.0 - sg)))).astype(BF16)
            dup_ref[:, cols] = (da * silu).astype(BF16)

        @pl.when(i == steps - 1)
        def _():
            loss_ref[...] = jnp.zeros_like(loss_ref) + (0.5 / D) * jnp.sum(acc[...], axis=1, keepdims=True)

    t_spec = lambda k: pl.BlockSpec((BLK, D), lambda i: (jnp.minimum(i * n_t + k, last_blk), 0))
    return pl.pallas_call(
        body, name="ffn_fwd_loss_bwd", grid=(steps,),
        in_specs=[_rows(tm, D), _const((D_FF, D)), _const((D_FF, D)), _const((D_FF, D))]
        + [t_spec(k) for k in range(n_t)] + [_const((1, D)), _const((1, D))],
        out_specs=[_rows(tm, D_FF), _rows(tm, D_FF), _rows(tm, D_FF), _rows(tm, D), _acc((1, LANE)), _acc((1, D)),
                   _acc((1, D))],
        out_shape=[pltpu.HBM((lp, D_FF), BF16)] * 3 + [pltpu.HBM((lp, D), F32), pltpu.HBM((1, LANE), F32),
                                                         pltpu.HBM((1, D), F32), pltpu.HBM((1, D), F32)],
        scratch_shapes=[pltpu.VMEM((tm, D_FF), F32), pltpu.VMEM((tm, D_FF), F32), pltpu.VMEM((1, D), F32)],
        compiler_params=_params(58, dimension_semantics=_seq()),
    )(*_hbm(h1, wg_t, wu_t, wd, *[target] * n_t, g2, b2))


def _ffn_bwd_input(dpre2, dgate, dup, pre1, wg_t, wu_t, g1):
    tm = _row_tile(384)
    lp = _lp()

    def body(dp_ref, dg_ref, du_ref, p1_ref, wg_ref, wu_ref, g1_ref, dp1_ref, dg1_ref, db1_ref):
        @pl.when(pl.program_id(0) == 0)
        def _():
            dg1_ref[...] = jnp.zeros_like(dg1_ref)
            db1_ref[...] = jnp.zeros_like(db1_ref)

        dh1 = ALPHA * dp_ref[...] + _dot(dg_ref[...], wg_ref[...]) + _dot(du_ref[...], wu_ref[...])
        xhat, rstd = _ln_stats(p1_ref[...])
        dp1_ref[...] = _ln_bwd(dh1, xhat, rstd, g1_ref[...])
        dg1_ref[...] += jnp.sum(dh1 * xhat, axis=0, keepdims=True)
        db1_ref[...] += jnp.sum(dh1, axis=0, keepdims=True)

    return pl.pallas_call(
        body, name="ffn_bwd_input", grid=(lp // tm,),
        in_specs=[_rows(tm, D), _rows(tm, D_FF), _rows(tm, D_FF), _rows(tm, D), _const((D_FF, D)), _const((D_FF, D)),
                  _const((1, D))],
        out_specs=[_rows(tm, D), _acc((1, D)), _acc((1, D))],
        out_shape=[pltpu.HBM((lp, D), F32), pltpu.HBM((1, D), F32), pltpu.HBM((1, D), F32)],
        compiler_params=_params(40, dimension_semantics=_seq()),
    )(*_hbm(dpre2, dgate, dup, pre1, wg_t, wu_t, g1))


def _atb(a, b, name):
    lp = _lp()
    tm = _row_tile(1408)
    n, w = a.shape[1], b.shape[1]
    bw = 512 if n * w * 4 > (4 << 20) else w

    def body(a_ref, b_ref, o_ref):
        @pl.when(pl.program_id(1) == 0)
        def _():
            o_ref[...] = jnp.zeros_like(o_ref)

        o_ref[...] += _dot_tn(a_ref[...], b_ref[...])

    return pl.pallas_call(
        body, name=name, grid=(w // bw, lp // tm),
        in_specs=[pl.BlockSpec((tm, n), lambda j, k: (k, 0)), pl.BlockSpec((tm, bw), lambda j, k: (k, j))],
        out_specs=pl.BlockSpec((n, bw), lambda j, k: (0, j)),
        out_shape=pltpu.HBM((n, w), F32),
        compiler_params=_params(48, dimension_semantics=_seq(2)),
    )(*_hbm(a, b))


def _out_bwd(dpre1, w_out, o_gla, r_g, gn4, token):
    tm = _row_tile(384)
    lp = _lp()

    def body(dp_ref, w_ref, og_ref, r_ref, gn_ref, token_ref, dos_ref, dog_ref, dr_ref, dgn_ref):
        @pl.when(pl.program_id(0) == 0)
        def _():
            dgn_ref[...] = jnp.zeros_like(dgn_ref)

        do = _dot_nt(dp_ref[...], w_ref[...])
        for pos, h in enumerate(HEAD_POS):
            dos_ref[:, pos * DH:(pos + 1) * DH] = do[:, h * DH:(h + 1) * DH]
        gn = gn_ref[...]
        for h in range(GLA_HEADS):
            hs = slice(h * DV, (h + 1) * DV)
            xg = og_ref[:, hs]
            rstd = lax.rsqrt(jnp.mean(xg * xg, axis=-1, keepdims=True) + RMS_EPS)
            nx = xg * rstd
            r = r_ref[:, hs]
            sr = _sigmoid(r)
            d_o = do[:, 512 + h * DV:512 + (h + 1) * DV]
            dr_ref[:, hs] = d_o * (nx * gn) * (sr * (1.0 + r * (1.0 - sr)))
            dn = d_o * (r * sr)
            dgn_ref[...] += jnp.sum(dn * nx, axis=0, keepdims=True)
            dnx = dn * gn
            dog_ref[:, hs] = rstd * (dnx - nx * jnp.mean(dnx * nx, axis=-1, keepdims=True))

    return pl.pallas_call(
        body, name="out_bwd", grid=(lp // tm,),
        in_specs=[_rows(tm, D), _const((D, D)), _rows(tm, 512), _rows(tm, 512), _const((1, DV)), _const(TOKEN)],
        out_specs=[_rows(tm, 512), _rows(tm, 512), _rows(tm, 512), _acc((1, DV))],
        out_shape=[pltpu.HBM((lp, 512), F32)] * 3 + [pltpu.HBM((1, DV), F32)],
        compiler_params=_params(32, dimension_semantics=_seq()),
    )(*_hbm(dpre1, w_out, o_gla, r_g, gn4), token)


def _gla_bwd(qg, kg, vg, z, do_gla, st_all):
    steps = SEQ // BLK + 1
    kw, vw = GLA_HEADS * DK, GLA_HEADS * DV
    pairs = [(c, h) for c in range(GLA_PER_STEP) for h in range(GLA_HEADS)]
    heads = range(GLA_HEADS)

    def body(q_ref, k_ref, v_ref, z_ref, do_ref, st_ref, dq_ref, dk_ref, dv_ref, dz_ref, dst):
        @pl.when(pl.program_id(0) == 0)
        def _():
            dst[...] = jnp.zeros_like(dst)

        rmask = _gla_rowmask(steps - 1 - pl.program_id(0))
        zz = z_ref[...]
        b, b_last = _gla_decay(zz, rmask)
        e_b, e_nb, e_kd, e_last = jnp.exp(b), jnp.exp(-b), jnp.exp(b_last - b), jnp.exp(b_last)
        q = q_ref[...] * (rmask * DK ** -0.5)
        k = k_ref[...] * rmask
        v = v_ref[...] * rmask
        qe, ke, kd = q * e_b, k * e_nb, k * e_kd
        d_o = do_ref[...]
        causal = _iota((CH, CH), 0) >= _iota((CH, CH), 1)
        a, da, dqe, dke, dv_intra, carry = {}, {}, {}, {}, {}, {}
        for c, h in pairs:
            rows, ks, vs_ = _gla_slices(c, h)
            a[c, h] = jnp.where(causal, _dot_nt(qe[rows, ks], ke[rows, ks]), 0.0)
            da[c, h] = jnp.where(causal, _dot_nt(d_o[rows, vs_], v[rows, vs_]), 0.0)
            carry[c, h] = _dot_tn(d_o[rows, vs_], qe[rows, ks])
        for c, h in pairs:
            rows, ks, vs_ = _gla_slices(c, h)
            dqe[c, h] = _dot(d_o[rows, vs_], st_ref[0, c][:, ks]) + _dot(da[c, h], ke[rows, ks])
            dke[c, h] = _dot_tn(da[c, h], qe[rows, ks])
            dv_intra[c, h] = _dot_tn(a[c, h], d_o[rows, vs_])
        dstate = dst[...]
        dkd, db_decay = {}, {}
        for c in reversed(range(GLA_PER_STEP)):
            for h in heads:
                rows, ks, vs_ = _gla_slices(c, h)
                dkd[c, h] = _dot(v[rows, vs_], dstate[:, ks])
                dv_ref[rows, vs_] = dv_intra[c, h] + _dot_nt(kd[rows, ks], dstate[:, ks])
            chunk_last = e_last[c * CH:c * CH + 1]
            db_decay[c] = jnp.sum(dstate * st_ref[0, c], axis=0, keepdims=True) * chunk_last
            dstate = dstate * chunk_last + jnp.concatenate([carry[c, h] for h in heads], axis=1)
        dst[...] = dstate
        rows_of = lambda parts: jnp.concatenate(
            [jnp.concatenate([parts[c, h] for h in heads], axis=1) for c in range(GLA_PER_STEP)], axis=0)
        dqe_all, dke_all, dkd_all = rows_of(dqe), rows_of(dke), rows_of(dkd)
        dq_ref[...] = dqe_all * e_b * (rmask * DK ** -0.5)
        dk_ref[...] = (dke_all * e_nb + dkd_all * e_kd) * rmask
        dkd_kd = dkd_all * kd
        db = dqe_all * qe - dke_all * ke - dkd_kd
        _, upper, same = _gla_chunk_masks()
        decay_rows = jnp.concatenate([jnp.broadcast_to(db_decay[c], (CH, kw)) for c in range(GLA_PER_STEP)], axis=0)
        dlog_g = _dot_exact(upper.astype(F32), db) + _dot_exact(same.astype(F32), dkd_kd) + decay_rows
        dz_ref[...] = dlog_g * (rmask / GLA_TAU) * _sigmoid(-zz)

    blk = lambda w: pl.BlockSpec((BLK, w), lambda s: (_gla_block(steps - 1 - s), 0))
    return pl.pallas_call(
        body, name="gla_bwd", grid=(steps,),
        in_specs=[blk(kw), blk(kw), blk(vw), blk(kw), blk(vw),
                  pl.BlockSpec((1, GLA_PER_STEP, DV, kw), lambda s: (steps - 1 - s, 0, 0, 0))],
        out_specs=[blk(kw), blk(kw), blk(vw), blk(kw)],
        out_shape=[pltpu.HBM((_lp(), kw), F32), pltpu.HBM((_lp(), kw), F32),
                   pltpu.HBM((_lp(), vw), F32), pltpu.HBM((_lp(), kw), F32)],
        scratch_shapes=[pltpu.VMEM((DV, kw), F32)],
        compiler_params=_params(16, dimension_semantics=_seq()),
    )(*_hbm(qg, kg, vg, z, do_gla, st_all))


def _swa_bwd(sinks, qs, ks, vs, do_s):
    nb = SEQ // BLK
    kvw = SWA_KV_HEADS * DH
    scale = DH ** -0.5
    heads = range(SWA_HEADS)

    def body(sink_ref, q_ref, km_ref, kp_ref, kc_ref, vm_ref, vp_ref, vc_ref, do_ref,
             dq_ref, dk_ref, dv_ref, dsink_ref, carry_k, carry_v, meta_k, meta_v):
        n = pl.program_id(0)

        @pl.when(n == 0)
        def _():
            for r in (carry_k, carry_v, meta_k, meta_v):
                r[...] = jnp.zeros_like(r)
            dsink_ref[...] = jnp.zeros_like(dsink_ref)

        @pl.when(n <= nb)
        def _():
            negdist, maskbias = _swa_bias(n)
            lane = _iota((1, LANE), 1)
            k_all = jnp.concatenate([km_ref[...], kp_ref[...], kc_ref[...]], axis=0).astype(BF16)
            v_all = jnp.concatenate([vm_ref[...], vp_ref[...], vc_ref[...]], axis=0).astype(BF16)
            q = [_swa_half(q_ref, pos, scale) for pos in heads]
            d_o = [_swa_half(do_ref, pos) for pos in heads]
            t = [_dot_nt(q[pos], k_all) + (2.0 ** -(HEAD_POS[pos] + 1) * negdist + maskbias) for pos in heads]
            dp = [_dot_nt(d_o[pos], v_all) for pos in heads]
            soft = [_swa_softmax(t[pos], sink_ref[HEAD_POS[pos]]) for pos in heads]
            p = [s[0] for s in soft]
            delta = [jnp.sum(p[pos] * dp[pos], axis=-1, keepdims=True) for pos in heads]
            ds = [(p[pos] * (dp[pos] - delta[pos])).astype(BF16) for pos in heads]
            dq = [_dot(ds[pos], k_all) for pos in heads]
            for col in range(SWA_HEADS // 2):
                dq_ref[:, col * LANE:(col + 1) * LANE] = scale * _swa_merge(dq[2 * col], dq[2 * col + 1])
            dsink = jnp.zeros((1, LANE), F32)
            for pos in heads:
                dsink = dsink + jnp.where(lane == HEAD_POS[pos],
                                          -jnp.sum(soft[pos][1] * delta[pos], axis=0, keepdims=True), 0.0)
            dsink_ref[...] += dsink
            dk3 = _dot_tn(jnp.concatenate(q, axis=0), jnp.concatenate(ds, axis=0)).T
            dv3 = _dot_tn(jnp.concatenate(d_o, axis=0), jnp.concatenate([x.astype(BF16) for x in p], axis=0)).T
            meta_k[...] += dk3[0:BLK]
            meta_v[...] += dv3[0:BLK]
            dk_ref[...] = carry_k[...] + dk3[BLK:2 * BLK]
            dv_ref[...] = carry_v[...] + dv3[BLK:2 * BLK]
            carry_k[...] = dk3[2 * BLK:3 * BLK]
            carry_v[...] = dv3[2 * BLK:3 * BLK]

        @pl.when(n == nb + 1)
        def _():
            dk_ref[...] = meta_k[...]
            dv_ref[...] = meta_v[...]

    kv_out = pl.BlockSpec((BLK, kvw), lambda n: (jnp.where(n == nb + 1, nb, jnp.clip(n - 1, 0, nb - 1)), 0))
    qblk = pl.BlockSpec((BLK, SWA_HEADS * DH), lambda n: (jnp.minimum(n, nb), 0))
    return pl.pallas_call(
        body, name="swa_bwd", grid=(nb + 2,),
        in_specs=[pl.BlockSpec(memory_space=pltpu.SMEM), qblk] + _swa_kv_specs(kvw) + _swa_kv_specs(kvw) + [qblk],
        out_specs=[qblk, kv_out, kv_out, _acc((1, LANE))],
        out_shape=[pltpu.HBM((_lp(), SWA_HEADS * DH), F32), pltpu.HBM((_lp(), kvw), F32),
                   pltpu.HBM((_lp(), kvw), F32), pltpu.HBM((1, LANE), F32)],
        scratch_shapes=[pltpu.VMEM((BLK, kvw), F32)] * 4,
        compiler_params=_params(16, dimension_semantics=_seq()),
    )(sinks, *_hbm(qs, ks, ks, ks, vs, vs, vs, do_s))


def _in_bwd(dqs, dks, dvs, dqg, dkg, dvg, drg, dz, dpre1, w_in_t, wg2_p):
    tm = _row_tile(384)
    lp = _lp()
    widths = (512, 128, 128, 256, 256, 512, 512)
    offs = (O_QS, O_KS, O_VS, O_QG, O_KG, O_VG, O_RG)

    def body(*refs):
        parts, (dz_ref, dp1_ref, w_ref, wg2_ref, dproj_ref, dh0_ref, dbin_ref, dbg_ref) = refs[:7], refs[7:]

        @pl.when(pl.program_id(0) == 0)
        def _():
            dbin_ref[...] = jnp.zeros_like(dbin_ref)
            dbg_ref[...] = jnp.zeros_like(dbg_ref)

        for pos, h in enumerate(HEAD_POS):
            val = parts[0][:, pos * DH:(pos + 1) * DH]
            dproj_ref[:, O_QS + h * DH:O_QS + (h + 1) * DH] = val.astype(BF16)
            dbin_ref[:, O_QS + h * DH:O_QS + (h + 1) * DH] += jnp.sum(val, axis=0, keepdims=True)
        for p_ref, off, wd in zip(parts[1:], offs[1:], widths[1:]):
            val = p_ref[...]
            dproj_ref[:, off:off + wd] = val.astype(BF16)
            dbin_ref[:, off:off + wd] += jnp.sum(val, axis=0, keepdims=True)
        dz = dz_ref[...]
        dlr = _dot_nt(dz, wg2_ref[...])
        dproj_ref[:, O_LR:O_LR + LANE] = dlr.astype(BF16)
        dbin_ref[:, O_LR:O_LR + LANE] += jnp.sum(dlr, axis=0, keepdims=True)
        dbg_ref[...] += jnp.sum(dz, axis=0, keepdims=True)
        dh0_ref[...] = ALPHA * dp1_ref[...] + _dot(dproj_ref[...], w_ref[...])

    return pl.pallas_call(
        body, name="in_bwd", grid=(lp // tm,),
        in_specs=[_rows(tm, w) for w in widths] + [_rows(tm, 256), _rows(tm, D), _const((D_IN_P, D)), _const((LANE, 256))],
        out_specs=[_rows(tm, D_IN_P), _rows(tm, D), _acc((1, D_IN_P)), _acc((1, 256))],
        out_shape=[pltpu.HBM((lp, D_IN_P), BF16), pltpu.HBM((lp, D), F32),
                   pltpu.HBM((1, D_IN_P), F32), pltpu.HBM((1, 256), F32)],
        compiler_params=_params(40, dimension_semantics=_seq()),
    )(*_hbm(dqs, dks, dvs, dqg, dkg, dvg, drg, dz, dpre1, w_in_t, wg2_p))


def _ln_in_bwd(x, meta_ext, dh0, g, token):
    tr = min(LN_ROWS, SEQ)

    def ln_bwd(x_ref, dh_ref, g_ref, dx_ref, dg_ref, db_ref):
        @pl.when(pl.program_id(0) == 0)
        def _():
            dg_ref[...] = jnp.zeros_like(dg_ref)
            db_ref[...] = jnp.zeros_like(db_ref)

        xhat, rstd = _ln_stats(x_ref[...])
        dh = dh_ref[...]
        dx_ref[...] = _ln_bwd(dh, xhat, rstd, g_ref[...])
        dg_ref[...] += jnp.sum(dh * xhat, axis=0, keepdims=True)
        db_ref[...] += jnp.sum(dh, axis=0, keepdims=True)

    def body(x_ref, dh_ref, g_ref, token_ref, dx_ref, dg_ref, db_ref):
        ln_bwd(x_ref, dh_ref, g_ref, dx_ref, dg_ref, db_ref)

    def meta_body(m_ref, dh_ref, g_ref, dm_ref, dg_ref, db_ref):
        ln_bwd(m_ref, dh_ref, g_ref, dm_ref, dg_ref, db_ref)

    sums = [pltpu.HBM((1, D), F32), pltpu.HBM((1, D), F32)]
    dx, dg, db = pl.pallas_call(
        body, name="ln_in_bwd", grid=(SEQ // tr,),
        in_specs=[_rows(tr, D), _rows(tr, D), _const((1, D)), _const(TOKEN)],
        out_specs=[_rows(tr, D), _acc((1, D)), _acc((1, D))],
        out_shape=[pltpu.HBM((SEQ, D), F32)] + sums,
        compiler_params=_params(32, dimension_semantics=_seq()),
    )(*_hbm(x, dh0, g), token)
    dm, dg_m, db_m = pl.pallas_call(
        meta_body, name="ln_in_bwd_meta", grid=(1,),
        in_specs=[_const((BLK, D)), pl.BlockSpec((BLK, D), lambda i: (SEQ // BLK, 0)), _const((1, D))],
        out_specs=[_acc((BLK, D)), _acc((1, D)), _acc((1, D))],
        out_shape=[pltpu.HBM((BLK, D), F32)] + sums,
        compiler_params=_params(16, dimension_semantics=_seq()),
    )(*_hbm(meta_ext, dh0, g))
    return dx, dm, dg + dg_m, db + db_m


def _local_step(x, target, meta_full, ln_in_g, ln_in_b, b_in, wg2, bg2, sinks, gn, g1, b1, g2, b2,
                token, fetch_w_in, fetch_rest, ship_ffn, ship_w_in):
    row = lambda v: v.reshape(1, -1).astype(F32)
    meta_ext = jnp.pad(meta_full, ((META_OFF, BLK - CH), (0, 0)))
    b_in_p = jnp.pad(row(b_in), ((0, 0), (0, D_IN_P - D_IN)))
    wg2_p = jnp.pad(wg2, ((0, LANE - wg2.shape[0]), (0, 0))).astype(BF16)
    gn4 = row(gn)
    sinks = sinks.reshape(-1).astype(F32)

    h0 = _ln_in_fwd(x, meta_ext, row(ln_in_g), row(ln_in_b), token)
    w_in_t = fetch_w_in([h0])
    qs, ks, vs, qg, kg, vg, rg, glr, z = _in_proj(h0, w_in_t, b_in_p, wg2_p, row(bg2))
    o_s = _swa_fwd(sinks, qs, ks, vs)
    o_gla, st_all = _gla_fwd(qg, kg, vg, z)
    w_out, wg_t, wu_t, wd = fetch_rest([o_s, o_gla])
    o, pre1, h1 = _post_mix(o_s, o_gla, rg, h0, gn4, w_out, row(g1), row(b1))
    a, dgate, dup, dpre2, loss, dg2, db2 = _ffn_fwd_loss_bwd(h1, wg_t, wu_t, wd, target, row(g2), row(b2))
    dpre1, dg1, db1 = _ffn_bwd_input(dpre2, dgate, dup, pre1, wg_t, wu_t, row(g1))
    dwd = _atb(a, dpre2, "dw_down")
    dwg_t = _atb(dgate, h1, "dw_gate")
    dwu_t = _atb(dup, h1, "dw_up")
    dw_out = _atb(o, dpre1, "dw_out")
    token = ship_ffn(dict(w_out=dw_out, w_g=dwg_t, w_u=dwu_t, w_d=dwd))
    do_s, do_gla, drg, dgn = _out_bwd(dpre1, w_out, o_gla, rg, gn4, token)
    dqg, dkg, dvg, dz = _gla_bwd(qg, kg, vg, z, do_gla, st_all)
    dqs, dks, dvs, dsinks = _swa_bwd(sinks, qs, ks, vs, do_s)
    dproj, dh0, db_in_p, dbg2 = _in_bwd(dqs, dks, dvs, dqg, dkg, dvg, drg, dz, dpre1, w_in_t, wg2_p)
    token = ship_w_in(_atb(dproj, h0, "dw_in"))
    dwg2_p = _atb(glr, dz, "dw_gate_lr2")
    dx, dmeta_blk, dg_in, db_in_ln = _ln_in_bwd(x, meta_ext, dh0, row(ln_in_g), token)

    grads = dict(
        meta=dmeta_blk[META_OFF:CH], ln_in_g=dg_in, ln_in_b=db_in_ln, ln1_g=dg1, ln1_b=db1, ln2_g=dg2, ln2_b=db2,
        b_in=db_in_p[:, :D_IN], wg2=dwg2_p[:wg2.shape[0]], bg2=dbg2, sinks=dsinks[:, :SWA_HEADS], gn=dgn)
    return loss[0, 0], dx, grads


HBM = pl.BlockSpec(memory_space=pltpu.HBM)


def _place():
    return lax.axis_index("x"), lax.axis_index("y"), lax.axis_index("c")


def _other_chips(x, y):
    return [(1 - x, y), (x, 1 - y), (1 - x, 1 - y)]


def _dma_sems(n):
    return pltpu.SemaphoreType.DMA((n,))


def _comm_params():
    return pltpu.CompilerParams(has_side_effects=True)


def _gather_halves(shards):
    n = len(shards)

    def body(*refs):
        ins, outs = refs[:n], refs[n:2 * n]
        ici_send, ici_recv, d2d_send, d2d_recv = refs[2 * n:]
        x, y, c = _place()
        mine = 2 * x + y
        chips = _other_chips(x, y)

        def ici(a, j, src_chip):
            px, py = chips[j]
            return pltpu.make_async_remote_copy(ins[a].at[c], outs[a].at[src_chip, c], ici_send.at[3 * a + j],
                                                ici_recv.at[3 * a + j], device_id=(px, py, c), device_id_type=MESH)

        def d2d(a, j, half):
            px, py = chips[j]
            blk = outs[a].at[2 * px + py, half]
            return pltpu.make_async_remote_copy(blk, blk, d2d_send.at[3 * a + j], d2d_recv.at[3 * a + j],
                                                device_id=(x, y, 1 - c), device_id_type=MESH)

        sends = [ici(a, j, mine) for a in range(n) for j in range(3)]
        for cp in sends:
            cp.start()
        passed = []
        for a in range(n):
            for j, (px, py) in enumerate(chips):
                ici(a, j, 2 * px + py).wait_recv()
                fwd = d2d(a, j, c)
                fwd.start()
                passed.append(fwd)
        for a in range(n):
            for j in range(3):
                d2d(a, j, 1 - c).wait_recv()
        for cp in sends + passed:
            cp.wait_send()

    gathered = pl.pallas_call(
        body, name="gather_halves",
        in_specs=[HBM] * n, out_specs=[HBM] * n,
        out_shape=[pltpu.HBM((N_CHIPS,) + s.shape, s.dtype) for s in shards],
        scratch_shapes=[_dma_sems(3 * n)] * 4,
        compiler_params=_comm_params(),
    )(*_hbm(*shards))
    mine = 2 * lax.axis_index("x") + lax.axis_index("y")
    return [lax.dynamic_update_index_in_dim(g, s, mine, axis=0) for g, s in zip(gathered, shards)]


SEM = pl.BlockSpec(memory_space=pltpu.SEMAPHORE)


def _ici_copies(kind, srcs, lands, send_sems, recv_sems):
    x, y, c = _place()
    mine = 2 * x + y
    to_start, to_wait = [], []
    for a in range(len(srcs)):
        for j, (px, py) in enumerate(_other_chips(x, y)):
            peer = 2 * px + py
            if kind == "gather":
                src, there, here = srcs[a].at[c], lands[a].at[mine, c], lands[a].at[peer, c]
            else:
                src, there, here = srcs[a].at[peer], lands[a].at[mine], lands[a].at[peer]
            for dst, out in ((there, to_start), (here, to_wait)):
                out.append(pltpu.make_async_remote_copy(src, dst, send_sems.at[3 * a + j], recv_sems.at[3 * a + j],
                                                        device_id=(px, py, c), device_id_type=MESH))
    return to_start, to_wait


def _split_params():
    return pltpu.CompilerParams(has_side_effects=pltpu.SideEffectType.DATAFLOW_SIDE_EFFECTING)


def _ici_start(kind, srcs, land_shapes, after, name):
    n = len(srcs)
    lands = [pltpu.with_memory_space_constraint(lax.empty(s, a.dtype), pltpu.HBM) for s, a in zip(land_shapes, srcs)]

    def body(*refs):
        outs = refs[2 * n + len(after):]
        to_start, _ = _ici_copies(kind, refs[:n], refs[n:2 * n], outs[0], outs[1])
        for cp in to_start:
            cp.start()
        outs[-1][...] = jnp.zeros(TOKEN, F32)

    outs = pl.pallas_call(
        body, name=name, in_specs=[HBM] * (2 * n) + [pl.BlockSpec(memory_space=pl.ANY)] * len(after),
        out_specs=[SEM, SEM] + [HBM] * (2 * n) + [pl.BlockSpec(memory_space=pltpu.VMEM)],
        out_shape=[_dma_sems(3 * n)] * 2 + [pltpu.HBM(a.shape, a.dtype) for a in list(srcs) + lands]
        + [jax.ShapeDtypeStruct(TOKEN, F32)],
        input_output_aliases={i: 2 + i for i in range(2 * n)},
        compiler_params=_split_params(),
    )(*_hbm(*srcs), *lands, *after)
    return outs[:-1], outs[-1]


def _ici_wait(kind, handle, after, name):
    n = (len(handle) - 2) // 2

    def body(*refs):
        _, to_wait = _ici_copies(kind, refs[:n], refs[n:2 * n], refs[2 * n], refs[2 * n + 1])
        for cp in to_wait:
            cp.wait_send()
            cp.wait_recv()

    outs = pl.pallas_call(
        body, name=name, in_specs=[HBM] * (2 * n) + [SEM, SEM] + [pl.BlockSpec(memory_space=pl.ANY)] * len(after),
        out_specs=[HBM] * (2 * n), out_shape=[pltpu.HBM(a.shape, a.dtype) for a in handle[2:]],
        input_output_aliases={i: i for i in range(2 * n)},
        compiler_params=_split_params(),
    )(*handle[2:], handle[0], handle[1], *after)
    return list(outs[n:])


def _sibling_forward(lands, name):
    n = len(lands)

    def body(*refs):
        outs = refs[n:2 * n]
        send_sems, recv_sems = refs[2 * n:]
        x, y, c = _place()

        def copy(a, j, half):
            px, py = _other_chips(x, y)[j]
            blk = outs[a].at[2 * px + py, half]
            return pltpu.make_async_remote_copy(blk, blk, send_sems.at[3 * a + j], recv_sems.at[3 * a + j],
                                                device_id=(x, y, 1 - c), device_id_type=MESH)

        pairs = [(a, j) for a in range(n) for j in range(3)]
        for a, j in pairs:
            copy(a, j, c).start()
        for a, j in pairs:
            copy(a, j, 1 - c).wait_recv()
        for a, j in pairs:
            copy(a, j, c).wait_send()

    return pl.pallas_call(
        body, name=name, in_specs=[HBM] * n, out_specs=[HBM] * n,
        out_shape=[pltpu.HBM(a.shape, a.dtype) for a in lands],
        input_output_aliases={a: a for a in range(n)},
        scratch_shapes=[_dma_sems(3 * n)] * 2,
        compiler_params=_comm_params(),
    )(*_hbm(*lands))


def _sibling_exchange(grads, name):
    n = len(grads)

    def body(*refs):
        ins, outs = refs[:n], refs[n:2 * n]
        send_sems, recv_sems = refs[2 * n:]
        x, y, c = _place()
        copies = []
        for a in range(n):
            for s in range(N_CHIPS):
                cp = pltpu.make_async_remote_copy(ins[a].at[s, 1 - c], outs[a].at[s], send_sems.at[N_CHIPS * a + s],
                                                  recv_sems.at[N_CHIPS * a + s], device_id=(x, y, 1 - c),
                                                  device_id_type=MESH)
                cp.start()
                copies.append(cp)
        for cp in copies:
            cp.wait_recv()
        for cp in copies:
            cp.wait_send()

    return pl.pallas_call(
        body, name=name, in_specs=[HBM] * n, out_specs=[HBM] * n,
        out_shape=[pltpu.HBM((N_CHIPS, g.shape[2], D), F32) for g in grads],
        scratch_shapes=[_dma_sems(N_CHIPS * n)] * 2,
        compiler_params=_comm_params(),
    )(*_hbm(*grads))


def _add_halves(core, grad, recv, dtype, name):
    h = grad.shape[2]

    def body(c_ref, a_ref, b_ref, o_ref):
        o_ref[...] = (a_ref[0] + b_ref[...]).astype(dtype)

    return pl.pallas_call(
        body, name=name,
        grid_spec=pltpu.PrefetchScalarGridSpec(
            num_scalar_prefetch=1, grid=(N_CHIPS,),
            in_specs=[pl.BlockSpec((1, 1, h, D), lambda s, c: (s, c[0], 0, 0)),
                      pl.BlockSpec((1, h, D), lambda s, c: (s, 0, 0))],
            out_specs=pl.BlockSpec((1, h, D), lambda s, c: (s, 0, 0))),
        out_shape=pltpu.HBM((N_CHIPS, h, D), dtype),
        compiler_params=_params(16, dimension_semantics=_seq()),
    )(core, *_hbm(grad, recv))


def _chip_scatter(parts, with_own):
    n = len(parts)

    def body(*refs):
        ins, outs = refs[:n], refs[n:2 * n]
        send_sems, recv_sems, local_sems = refs[2 * n:]
        x, y, c = _place()
        mine = 2 * x + y
        chips = _other_chips(x, y)
        local = [pltpu.make_async_copy(ins[a].at[mine], outs[a].at[mine], local_sems.at[a]) for a in range(n)
                 if with_own[a]]
        for cp in local:
            cp.start()
        sends = []
        for a in range(n):
            for j, (px, py) in enumerate(chips):
                cp = pltpu.make_async_remote_copy(ins[a].at[2 * px + py], outs[a].at[mine], send_sems.at[3 * a + j],
                                                  recv_sems.at[3 * a + j], device_id=(px, py, c), device_id_type=MESH)
                cp.start()
                sends.append(cp)
        for a in range(n):
            for j, (px, py) in enumerate(chips):
                pltpu.make_async_remote_copy(ins[a].at[mine], outs[a].at[2 * px + py], send_sems.at[3 * a + j],
                                             recv_sems.at[3 * a + j], device_id=(px, py, c),
                                             device_id_type=MESH).wait_recv()
        for cp in sends:
            cp.wait_send()
        for cp in local:
            cp.wait()

    return pl.pallas_call(
        body, name="chip_scatter", in_specs=[HBM] * n, out_specs=[HBM] * n,
        out_shape=[pltpu.HBM(p.shape, p.dtype) for p in parts],
        scratch_shapes=[_dma_sems(3 * n)] * 2 + [_dma_sems(n)],
        compiler_params=_comm_params(),
    )(*_hbm(*parts))


def _sum_chips(slots, first, rest, name):
    h = first.shape[1]

    def body(i_ref, a_ref, b_ref, c_ref, d_ref, o_ref):
        o_ref[...] = ((a_ref[...].astype(F32) + b_ref[...].astype(F32)) + c_ref[...].astype(F32)) + d_ref[...].astype(F32)

    slab = lambda k: pl.BlockSpec((1, h, D), lambda i, ix: (ix[k], 0, 0))
    return pl.pallas_call(
        body, name=name,
        grid_spec=pltpu.PrefetchScalarGridSpec(num_scalar_prefetch=1, grid=(1,),
                                               in_specs=[slab(0), slab(1), slab(2), slab(3)], out_specs=slab(4)),
        out_shape=pltpu.HBM((2, h, D), F32),
        compiler_params=_params(16, dimension_semantics=_seq()),
    )(slots, *_hbm(first, rest, rest, rest))


def _join_halves(halves):
    n = len(halves)

    def body(*refs):
        outs = refs[n:2 * n]
        send_sems, recv_sems = refs[2 * n:]
        x, y, c = _place()

        def copy(a, slab):
            return pltpu.make_async_remote_copy(outs[a].at[slab], outs[a].at[slab], send_sems.at[a], recv_sems.at[a],
                                                device_id=(x, y, 1 - c), device_id_type=MESH)

        for a in range(n):
            copy(a, c).start()
        for a in range(n):
            copy(a, 1 - c).wait_recv()
        for a in range(n):
            copy(a, c).wait_send()

    return pl.pallas_call(
        body, name="join_halves", in_specs=[HBM] * n, out_specs=[HBM] * n,
        out_shape=[pltpu.HBM(h.shape, F32) for h in halves],
        input_output_aliases={a: a for a in range(n)},
        scratch_shapes=[_dma_sems(n)] * 2,
        compiler_params=_comm_params(),
    )(*_hbm(*halves))


def _chip_partials(grads, wire_dtypes, names):
    core = lax.axis_index("c").astype(jnp.int32).reshape(1)
    recv = _sibling_exchange(grads, "sibling_exchange_" + names[0])
    return [_add_halves(core, g, r, dt, "add_halves_" + nm) for g, r, dt, nm in zip(grads, recv, wire_dtypes, names)]


def _finish_reduce(parts, got, same_order, names):
    x, y, c = _place()
    others = [2 * px + py for px, py in _other_chips(x, y)]
    own_first = jnp.stack([2 * x + y] + others + [c]).astype(jnp.int32)
    chip_order = jnp.stack([0 * c, 0 * c + 1, 0 * c + 2, 0 * c + 3, c]).astype(jnp.int32)
    halves = [_sum_chips(chip_order, q, q, "sum_chips_" + nm) if fixed else _sum_chips(own_first, p, q, "sum_chips_" + nm)
              for p, q, fixed, nm in zip(parts, got, same_order, names)]
    return [f.reshape(2 * f.shape[1], D) for f in _join_halves(halves)]


def _adamw(w, g, m, v, name):
    rows, cols = w.shape
    if rows % 8 == 0:
        tr = max(t for t in range(8, 257, 8) if rows % t == 0)
        grid, blk = (rows // tr,), pl.BlockSpec((tr, cols), lambda i: (i, 0))
    else:
        grid, blk = (cols // 256,), pl.BlockSpec((rows, 256), lambda i: (0, i))

    def body(w_ref, g_ref, m_ref, v_ref, d_ref, nm_ref, nv_ref):
        d_ref[...], nm_ref[...], nv_ref[...] = _adamw_math(w_ref[...], g_ref[...], m_ref[...], v_ref[...])

    return pl.pallas_call(
        body, name=name, grid=grid,
        in_specs=[blk] * 4, out_specs=[blk] * 3,
        out_shape=[pltpu.HBM(w.shape, F32)] * 3,
        compiler_params=_params(32, dimension_semantics=_seq()),
    )(*_hbm(w, g, m, v))


def _adamw_math(w, g, m, v):
    nm = ADAM_B1 * m + (1.0 - ADAM_B1) * g
    nv = ADAM_B2 * v + (1.0 - ADAM_B2) * (g * g)
    m_hat = nm / (1.0 - ADAM_B1 ** ADAM_STEP)
    v_hat = nv / (1.0 - ADAM_B2 ** ADAM_STEP)
    return -ADAM_LR * (m_hat / (jnp.sqrt(v_hat) + ADAM_EPS) + ADAM_WD * w), nm, nv


SMALL = (("meta_tokens", (N_META, D // N_CHIPS)), ("ln_in_g", (1, D)), ("ln_in_b", (1, D)), ("b_in", (1, D_IN)),
         ("w_gate_lr2", (16, 256 // N_CHIPS)), ("b_gate_lr2", (1, 256)), ("attn_sinks", (1, SWA_HEADS)),
         ("gla_norm_g", (1, DV)), ("ln1_g", (1, D)), ("ln1_b", (1, D)), ("ln2_g", (1, D)), ("ln2_b", (1, D)))
ROW_META, ROW_B_IN, ROW_TAIL, ROW_WG2 = 0, 22, 25, 32
ROW_LN = dict(ln_in_g=16, ln_in_b=17, ln1_g=18, ln1_b=19, ln2_g=20, ln2_b=21)
TAIL_BG2, TAIL_SINKS, TAIL_GN, TAIL_LOSS = 0, 256, 256 + SWA_HEADS, 256 + SWA_HEADS + DV


def _adamw_small(chip, pack, params):
    n = len(SMALL)

    def body(chip_ref, p_ref, *refs):
        ins, outs = refs[:3 * n], refs[3 * n:]
        c = chip_ref[0]

        def mine(width, rows):
            part = lambda s: p_ref[rows, s * width:(s + 1) * width]
            return jnp.where(c == 0, part(0), jnp.where(c == 1, part(1), jnp.where(c == 2, part(2), part(3))))

        tail = lambda lo, width: p_ref[ROW_TAIL:ROW_TAIL + 1, lo:lo + width]
        grads = dict(
            meta_tokens=mine(D // N_CHIPS, slice(ROW_META, ROW_META + N_META)),
            b_in=jnp.concatenate([p_ref[ROW_B_IN:ROW_B_IN + 1, :], p_ref[ROW_B_IN + 1:ROW_B_IN + 2, :],
                                  p_ref[ROW_B_IN + 2:ROW_B_IN + 3, 0:D_IN - 2 * D]], axis=1),
            w_gate_lr2=mine(256 // N_CHIPS, slice(ROW_WG2, ROW_WG2 + 16)),
            b_gate_lr2=tail(TAIL_BG2, 256), attn_sinks=tail(TAIL_SINKS, SWA_HEADS), gla_norm_g=tail(TAIL_GN, DV),
            **{k: p_ref[r:r + 1, :] for k, r in ROW_LN.items()})
        for i, (name, _) in enumerate(SMALL):
            g = grads[name]
            outs[4 * i][...] = g
            outs[4 * i + 1][...], outs[4 * i + 2][...], outs[4 * i + 3][...] = _adamw_math(
                ins[3 * i][...], g, ins[3 * i + 1][...], ins[3 * i + 2][...])

    whole = lambda shape: pl.BlockSpec(shape, lambda i, c: (0, 0))
    outs = pl.pallas_call(
        body, name="adamw_small",
        grid_spec=pltpu.PrefetchScalarGridSpec(
            num_scalar_prefetch=1, grid=(1,),
            in_specs=[whole(pack.shape)] + [whole(s) for _, s in SMALL for _ in range(3)],
            out_specs=[whole(s) for _, s in SMALL for _ in range(4)]),
        out_shape=[pltpu.HBM(s, F32) for _, s in SMALL for _ in range(4)],
        compiler_params=_params(16, dimension_semantics=_seq()),
    )(chip, *_hbm(pack, *[a for p in params for a in p]))
    return [outs[4 * i:4 * i + 4] for i in range(n)]


def _flat_rows(v, rows):
    flat = v.reshape(-1).astype(F32)
    return jnp.pad(flat, (0, rows * D - flat.shape[0])).reshape(rows, D)


def _small_pack(gr):
    tail = jnp.concatenate([gr["bg2"].reshape(-1), gr["sinks"].reshape(-1), gr["gn"].reshape(-1), gr["loss"].reshape(-1)])
    rows = [gr["meta"].reshape(N_META, D)] + [gr[k].reshape(1, D) for k in ROW_LN]
    rows += [_flat_rows(gr["b_in"], 3), _flat_rows(tail, 1), jnp.zeros((ROW_WG2 - ROW_TAIL - 1, D), F32),
             jnp.pad(gr["wg2"], ((0, 0), (0, D - gr["wg2"].shape[1])))]
    return jnp.concatenate(rows, axis=0)


BIG = ("w_in", "w_out", "w_g", "w_u", "w_d")


def kernel(x, meta_tokens, ln_in_g, ln_in_b, w_in, b_in, w_gate_lr2, b_gate_lr2, attn_sinks, gla_norm_g, w_out, ln1_g, ln1_b, w_ffn_gate, w_ffn_up, w_ffn_down, ln2_g, ln2_b, loss_target, m_meta_tokens, m_ln_in_g, m_ln_in_b, m_w_in, m_b_in, m_w_gate_lr2, m_b_gate_lr2, m_attn_sinks, m_gla_norm_g, m_w_out, m_ln1_g, m_ln1_b, m_w_ffn_gate, m_w_ffn_up, m_w_ffn_down, m_ln2_g, m_ln2_b, v_meta_tokens, v_ln_in_g, v_ln_in_b, v_w_in, v_b_in, v_w_gate_lr2, v_b_gate_lr2, v_attn_sinks, v_gla_norm_g, v_w_out, v_ln1_g, v_ln1_b, v_w_ffn_gate, v_w_ffn_up, v_w_ffn_down, v_ln2_g, v_ln2_b):
    chip = 2 * lax.axis_index("x") + lax.axis_index("y")

    halves = lambda a: a.reshape(2, a.shape[0] // 2, a.shape[1])
    r_in = SHARD_ROWS["w_in"]
    first = [halves(jnp.pad(w_in[0].T.astype(BF16), ((0, W_IN_WIN - r_in), (0, 0))))]
    rest = [halves(a) for a in (w_out[0].astype(BF16), w_ffn_gate[0].T.astype(BF16), w_ffn_up[0].T.astype(BF16),
                                w_ffn_down[0].astype(BF16))]
    lands = lambda arrs: [(N_CHIPS,) + a.shape for a in arrs]
    g_meta, g_wg2 = _gather_halves([halves(meta_tokens), halves(w_gate_lr2[0])])
    first_handle, first_token = _ici_start("gather", first, lands(first), [g_meta], "gather_w_in_start")
    rest_handle, token = _ici_start("gather", rest, lands(rest), [first_token], "gather_rest_start")
    meta_full = jnp.concatenate([g_meta[s].reshape(N_META, -1) for s in range(N_CHIPS)], axis=1)
    wg2_full = jnp.concatenate([g_wg2[s].reshape(w_gate_lr2.shape[1], -1) for s in range(N_CHIPS)], axis=1)

    def fetch(handle, shards, after, name):
        got = _sibling_forward(_ici_wait("gather", handle, after, name + "_wait"), name + "_forward")
        return [lax.dynamic_update_index_in_dim(g, s, chip, axis=0) for g, s in zip(got, shards)]

    def fetch_w_in(after):
        g_in, = fetch(first_handle, first, after, "gather_w_in")
        return jnp.pad(g_in.reshape(N_CHIPS, W_IN_WIN, D)[:, :r_in].reshape(D_IN, D), ((0, D_IN_P - D_IN), (0, 0)))

    def fetch_rest(after):
        return [g.reshape(-1, D) for g in fetch(rest_handle, rest, after, "gather_rest")]

    sent = {}

    def ship(key, grads, names):
        parts = _chip_partials([g.reshape(N_CHIPS, 2, -1, D) for g in grads], [BF16] * len(grads), names)
        handle, ship_token = _ici_start("scatter", parts, [p.shape for p in parts], [], "scatter_" + key + "_start")
        sent[key] = (parts, handle)
        return ship_token

    def ship_ffn(g):
        return ship("ffn", [g[k] for k in BIG[1:]], list(BIG[1:]))

    def ship_w_in(dw_in_t):
        win_start = [s * r_in // BF16_ROWS * BF16_ROWS for s in range(N_CHIPS)]
        return ship("w_in", [jnp.stack([dw_in_t[st:st + W_IN_WIN] for st in win_start])], ["w_in"])

    loss_part, dx, gr = _local_step(
        x[0], loss_target[0], meta_full, ln_in_g, ln_in_b, b_in[0], wg2_full, b_gate_lr2[0], attn_sinks[0],
        gla_norm_g[0], ln1_g[0], ln1_b[0], ln2_g[0], ln2_b[0], token, fetch_w_in, fetch_rest, ship_ffn, ship_w_in)
    ffn_got = _ici_wait("scatter", sent["ffn"][1], [dx], "scatter_ffn_wait")
    w_in_got = _ici_wait("scatter", sent["w_in"][1], [dx], "scatter_w_in_wait")

    gr["loss"] = loss_part
    small = jnp.broadcast_to(_small_pack(gr), (N_CHIPS, SMALL_ROWS, D)).reshape(N_CHIPS, 2, -1, D)
    small_parts = _chip_partials([small], [F32], ["small"])
    small_got = list(_chip_scatter(small_parts, [True]))
    red = _finish_reduce(sent["w_in"][0] + sent["ffn"][0] + small_parts, w_in_got + ffn_got + small_got,
                         [False] * len(BIG) + [True], list(BIG) + ["small"])

    big_g = dict(zip(BIG, red))
    big_g["w_in"] = lax.dynamic_slice_in_dim(red[0], chip * (r_in % BF16_ROWS), r_in, axis=0)
    loss = red[-1][ROW_TAIL, TAIL_LOSS]
    grads = dict(w_in=big_g["w_in"].T[None], w_out=big_g["w_out"][None], w_ffn_gate=big_g["w_g"].T[None],
                 w_ffn_up=big_g["w_u"].T[None], w_ffn_down=big_g["w_d"][None])
    weights = dict(meta_tokens=meta_tokens, ln_in_g=ln_in_g, ln_in_b=ln_in_b, w_in=w_in, b_in=b_in,
                   w_gate_lr2=w_gate_lr2, b_gate_lr2=b_gate_lr2, attn_sinks=attn_sinks, gla_norm_g=gla_norm_g,
                   w_out=w_out, ln1_g=ln1_g, ln1_b=ln1_b, w_ffn_gate=w_ffn_gate, w_ffn_up=w_ffn_up,
                   w_ffn_down=w_ffn_down, ln2_g=ln2_g, ln2_b=ln2_b)
    m_in = dict(meta_tokens=m_meta_tokens, ln_in_g=m_ln_in_g, ln_in_b=m_ln_in_b, w_in=m_w_in, b_in=m_b_in,
                w_gate_lr2=m_w_gate_lr2, b_gate_lr2=m_b_gate_lr2, attn_sinks=m_attn_sinks, gla_norm_g=m_gla_norm_g,
                w_out=m_w_out, ln1_g=m_ln1_g, ln1_b=m_ln1_b, w_ffn_gate=m_w_ffn_gate, w_ffn_up=m_w_ffn_up,
                w_ffn_down=m_w_ffn_down, ln2_g=m_ln2_g, ln2_b=m_ln2_b)
    v_in = dict(meta_tokens=v_meta_tokens, ln_in_g=v_ln_in_g, ln_in_b=v_ln_in_b, w_in=v_w_in, b_in=v_b_in,
                w_gate_lr2=v_w_gate_lr2, b_gate_lr2=v_b_gate_lr2, attn_sinks=v_attn_sinks, gla_norm_g=v_gla_norm_g,
                w_out=v_w_out, ln1_g=v_ln1_g, ln1_b=v_ln1_b, w_ffn_gate=v_w_ffn_gate, w_ffn_up=v_w_ffn_up,
                w_ffn_down=v_w_ffn_down, ln2_g=v_ln2_g, ln2_b=v_ln2_b)
    names = list(weights)
    big_names = ("w_in", "w_out", "w_ffn_gate", "w_ffn_up", "w_ffn_down")

    delta, new_m, new_v = {}, {}, {}
    for k, kk in zip(big_names, BIG):
        flip = (lambda a: a.T) if kk in ("w_in", "w_g", "w_u") else (lambda a: a)
        d_, m_, v_ = _adamw(flip(weights[k][0]), big_g[kk], flip(m_in[k][0]), flip(v_in[k][0]), "adamw_" + k)
        delta[k], new_m[k], new_v[k] = (flip(t)[None] for t in (d_, m_, v_))
    small_in = [tuple(src[k].reshape(shape) for src in (weights, m_in, v_in)) for k, shape in SMALL]
    small_out = _adamw_small(chip.astype(jnp.int32).reshape(1), red[-1], small_in)
    for (k, _), results in zip(SMALL, small_out):
        grads[k], delta[k], new_m[k], new_v[k] = (r.reshape(weights[k].shape) for r in results)

    return (loss, dx[None], *[grads[k] for k in names], *[delta[k] for k in names], *[new_m[k] for k in names],
            *[new_v[k] for k in names])
```

```python
import functools

import jax
import jax.numpy as jnp
from jax import lax
from jax.experimental import pallas as pl
from jax.experimental.pallas import tpu as pltpu

F32 = jnp.float32
BF16 = jnp.bfloat16
MESH = pl.DeviceIdType.MESH

D = 1024
SEQ = 4096
N_META = 16
SWA_HEADS, SWA_KV_HEADS, DH = 8, 2, 64
WINDOW = 128
GLA_HEADS, DK, DV = 4, 64, 128
GLA_TAU = 16.0
CH = 64
D_FF = 2816
D_IN = 2320
LN_EPS = 1e-5
RMS_EPS = 1e-6
ALPHA = 2.0 ** 0.25
NEG = -1e30
ADAM_LR, ADAM_B1, ADAM_B2, ADAM_EPS, ADAM_WD, ADAM_STEP = 0.001, 0.9, 0.999, 1e-8, 0.01, 10
O_QS, O_KS, O_VS, O_QG, O_KG, O_VG, O_RG, O_LR = 0, 512, 640, 768, 1024, 1280, 1792, 2304

LANE = 128
BLK = WINDOW
D_IN_P = D_IN + LANE - 16
META_OFF = CH - N_META
HEAD_POS = (0, 4, 1, 5, 2, 6, 3, 7)
LN_ROWS = 512
TOKEN = (8, LANE)
N_CHIPS = 4
SHARD_ROWS = dict(w_in=D_IN // N_CHIPS, w_out=D // N_CHIPS, w_g=D_FF // N_CHIPS, w_u=D_FF // N_CHIPS,
                  w_d=D_FF // N_CHIPS)
SMALL_ROWS = 48
BF16_ROWS = 16
W_IN_WIN = -(-SHARD_ROWS["w_in"] // (2 * BF16_ROWS)) * 2 * BF16_ROWS
VMEM_CAP_MB = 64


def _lp():
    return SEQ + BLK


def _row_tile(cap):
    lp = _lp()
    return max(t for t in range(16, cap + 1, 16) if lp % t == 0)


def _params(vmem_mb, **kw):
    assert vmem_mb <= VMEM_CAP_MB - 6
    return pltpu.CompilerParams(vmem_limit_bytes=vmem_mb << 20, **kw)


def _seq(n=1):
    return ("arbitrary",) * n


def _const(shape):
    return pl.BlockSpec(shape, lambda *_: (0,) * len(shape), pipeline_mode=pl.Buffered(1))


def _acc(shape):
    return pl.BlockSpec(shape, lambda *_: (0,) * len(shape))


def _rows(tm, width):
    return pl.BlockSpec((tm, width), lambda i: (i, 0))


def _dot(a, b):
    return jnp.dot(a.astype(BF16), b.astype(BF16), preferred_element_type=F32)


def _dot_nt(a, b):
    return lax.dot_general(a.astype(BF16), b.astype(BF16), (((1,), (1,)), ((), ())), preferred_element_type=F32)


def _dot_tn(a, b):
    return lax.dot_general(a.astype(BF16), b.astype(BF16), (((0,), (0,)), ((), ())), preferred_element_type=F32)


def _dot_exact(a, b):
    return jnp.dot(a, b, precision=lax.Precision.HIGHEST, preferred_element_type=F32)


def _ln_stats(x):
    mu = jnp.mean(x, axis=-1, keepdims=True)
    xc = x - mu
    rstd = lax.rsqrt(jnp.mean(xc * xc, axis=-1, keepdims=True) + LN_EPS)
    return xc * rstd, rstd


def _ln_bwd(dy, xhat, rstd, g):
    dxh = dy * g
    return rstd * (dxh - jnp.mean(dxh, axis=-1, keepdims=True) - xhat * jnp.mean(dxh * xhat, axis=-1, keepdims=True))


def _sigmoid(x):
    return 1.0 / (1.0 + jnp.exp(-x))


def _iota(shape, dim):
    return lax.broadcasted_iota(jnp.int32, shape, dim)


def _hbm(*arrays):
    return tuple(pltpu.with_memory_space_constraint(a, pltpu.HBM) for a in arrays)


def _ln_in_fwd_real(x, g, b, token):
    tr = min(LN_ROWS, SEQ)

    def body(x_ref, g_ref, b_ref, token_ref, h_ref):
        xhat, _ = _ln_stats(x_ref[...])
        h_ref[...] = xhat * g_ref[...] + b_ref[...]

    return pl.pallas_call(
        body, name="ln_in_fwd", grid=(SEQ // tr,),
        in_specs=[_rows(tr, D), _const((1, D)), _const((1, D)), _const(TOKEN)],
        out_specs=_rows(tr, D),
        out_shape=pltpu.HBM((_lp(), D), F32),
        compiler_params=_params(32, dimension_semantics=_seq()),
    )(*_hbm(x, g, b), token)


def _ln_in_fwd_meta(h_real, meta_ext, g, b):
    def meta_body(m_ref, g_ref, b_ref, real_ref, h_ref):
        xhat, _ = _ln_stats(m_ref[...])
        h_ref[...] = xhat * g_ref[...] + b_ref[...]

    return pl.pallas_call(
        meta_body, name="ln_in_fwd_meta", grid=(1,),
        in_specs=[_const((BLK, D)), _const((1, D)), _const((1, D)), pl.BlockSpec(memory_space=pl.ANY)],
        out_specs=pl.BlockSpec((BLK, D), lambda i: (SEQ // BLK, 0)),
        out_shape=pltpu.HBM((_lp(), D), F32),
        input_output_aliases={3: 0},
        compiler_params=_params(16, dimension_semantics=_seq()),
    )(*_hbm(meta_ext, g, b, h_real))


def _in_proj(h0, w_in_t, b_in_p, wg2_p, bg2):
    tm = _row_tile(384)
    lp = _lp()
    widths = (512, 128, 128, 256, 256, 512, 512, 128)
    offs = (O_QS, O_KS, O_VS, O_QG, O_KG, O_VG, O_RG, O_LR)

    def body(h_ref, w_ref, b_ref, wg2_ref, bg2_ref, *outs):
        proj = _dot_nt(h_ref[...], w_ref[...]) + b_ref[...]
        for pos, h in enumerate(HEAD_POS):
            outs[0][:, pos * DH:(pos + 1) * DH] = proj[:, O_QS + h * DH:O_QS + (h + 1) * DH]
        for o_ref, off, wd in zip(outs[1:8], offs[1:], widths[1:]):
            o_ref[...] = proj[:, off:off + wd]
        outs[8][...] = _dot(proj[:, O_LR:O_LR + LANE], wg2_ref[...]) + bg2_ref[...]

    return pl.pallas_call(
        body, name="in_proj", grid=(lp // tm,),
        in_specs=[_rows(tm, D), _const((D_IN_P, D)), _const((1, D_IN_P)), _const((LANE, 256)), _const((1, 256))],
        out_specs=[_rows(tm, w) for w in widths] + [_rows(tm, 256)],
        out_shape=[pltpu.HBM((lp, w), F32) for w in widths] + [pltpu.HBM((lp, 256), F32)],
        compiler_params=_params(40, dimension_semantics=_seq()),
    )(*_hbm(h0, w_in_t, b_in_p, wg2_p, bg2))


def _swa_masks(n):
    nb = SEQ // BLK
    is_meta = n == nb
    ri = _iota((BLK, BLK), 0)
    cj = _iota((BLK, BLK), 1)
    meta_col = ((cj >= META_OFF) & (cj < CH)).astype(jnp.int32)
    meta_q = meta_col * ((cj <= ri) & (ri < CH)).astype(jnp.int32)
    valid_m = jnp.where(is_meta, meta_q, meta_col) > 0
    dist_m = jnp.where(is_meta, ri - cj, n * BLK + ri + CH - cj).astype(F32)
    valid_p = jnp.where((n >= 1) & (n < nb), (cj > ri).astype(jnp.int32), 0) > 0
    dist_p = (ri + BLK - cj).astype(F32)
    valid_c = jnp.where(n < nb, (cj <= ri).astype(jnp.int32), 0) > 0
    dist_c = (ri - cj).astype(F32)
    return (dist_m, dist_p, dist_c), (valid_m, valid_p, valid_c)


def _swa_bias(n):
    dists, valids = _swa_masks(n)
    return (jnp.concatenate([-d for d in dists], axis=1),
            jnp.concatenate([jnp.where(v, 0.0, NEG) for v in valids], axis=1))


def _swa_half(ref, pos, scale=1.0):
    col = ref[:, (pos // 2) * LANE:(pos // 2 + 1) * LANE]
    lane = _iota((BLK, LANE), 1)
    mine = lane < DH if pos % 2 == 0 else lane >= DH
    return jnp.where(mine, col * scale, 0.0).astype(BF16)


def _swa_merge(even, odd):
    return jnp.where(_iota((BLK, LANE), 1) < DH, even, odd)


def _swa_softmax(t, sink):
    m = jnp.maximum(jnp.max(t, axis=-1, keepdims=True), sink)
    e = jnp.exp(t - m)
    e_sink = jnp.exp(sink - m)
    inv = 1.0 / (jnp.sum(e, axis=-1, keepdims=True) + e_sink)
    return e * inv, e_sink * inv


def _swa_kv_specs(width):
    nb = SEQ // BLK
    return [pl.BlockSpec((BLK, width), lambda n: (nb, 0)),
            pl.BlockSpec((BLK, width), lambda n: (jnp.clip(n - 1, 0, nb - 1), 0)),
            pl.BlockSpec((BLK, width), lambda n: (jnp.minimum(n, nb), 0))]


def _swa_fwd(sinks, qs, ks, vs):
    nb = SEQ // BLK
    heads = range(SWA_HEADS)

    def body(sink_ref, q_ref, km_ref, kp_ref, kc_ref, vm_ref, vp_ref, vc_ref, o_ref):
        negdist, maskbias = _swa_bias(pl.program_id(0))
        k_all = jnp.concatenate([km_ref[...], kp_ref[...], kc_ref[...]], axis=0).astype(BF16)
        v_all = jnp.concatenate([vm_ref[...], vp_ref[...], vc_ref[...]], axis=0).astype(BF16)
        q = [_swa_half(q_ref, pos, DH ** -0.5) for pos in heads]
        t = [_dot_nt(q[pos], k_all) + (2.0 ** -(HEAD_POS[pos] + 1) * negdist + maskbias) for pos in heads]
        p = [_swa_softmax(t[pos], sink_ref[HEAD_POS[pos]])[0].astype(BF16) for pos in heads]
        o = [_dot(p[pos], v_all) for pos in heads]
        for col in range(SWA_HEADS // 2):
            o_ref[:, col * LANE:(col + 1) * LANE] = _swa_merge(o[2 * col], o[2 * col + 1])

    kvw = SWA_KV_HEADS * DH
    return pl.pallas_call(
        body, name="swa_fwd", grid=(nb + 1,),
        in_specs=[pl.BlockSpec(memory_space=pltpu.SMEM), _rows(BLK, SWA_HEADS * DH)] + _swa_kv_specs(kvw) + _swa_kv_specs(kvw),
        out_specs=_rows(BLK, SWA_HEADS * DH),
        out_shape=pltpu.HBM((_lp(), SWA_HEADS * DH), F32),
        compiler_params=_params(16, dimension_semantics=_seq()),
    )(sinks, *_hbm(qs, ks, ks, ks, vs, vs, vs))


GLA_PER_STEP = BLK // CH


def _gla_block(s):
    nb = SEQ // BLK
    return jnp.where(s == 0, nb, s - 1)


def _gla_rowmask(s):
    ri = _iota((BLK, 1), 0)
    m = jnp.where(s == 0, ((ri >= META_OFF) & (ri < CH)).astype(jnp.int32), 1)
    return (m > 0).astype(F32) + jnp.zeros((BLK, 1), F32)


def _gla_chunk_masks():
    r, c = _iota((BLK, BLK), 0), _iota((BLK, BLK), 1)
    same = ((r < CH) & (c < CH)) | ((r >= CH) & (c >= CH))
    return same & (r >= c), same & (r <= c), same


def _gla_decay(z, rmask):
    log_g = (jnp.minimum(z, 0.0) - jnp.log1p(jnp.exp(-jnp.abs(z)))) * (rmask / GLA_TAU)
    lower, _, same = _gla_chunk_masks()
    return _dot_exact(lower.astype(F32), log_g), _dot_exact(same.astype(F32), log_g)


def _gla_slices(c, h):
    return slice(c * CH, (c + 1) * CH), slice(h * DK, (h + 1) * DK), slice(h * DV, (h + 1) * DV)


def _gla_fwd(qg, kg, vg, z):
    steps = SEQ // BLK + 1
    kw, vw = GLA_HEADS * DK, GLA_HEADS * DV
    pairs = [(c, h) for c in range(GLA_PER_STEP) for h in range(GLA_HEADS)]

    def body(q_ref, k_ref, v_ref, z_ref, o_ref, st_ref, st):
        s = pl.program_id(0)

        @pl.when(s == 0)
        def _():
            st[...] = jnp.zeros_like(st)

        rmask = _gla_rowmask(s)
        b, b_last = _gla_decay(z_ref[...], rmask)
        q = q_ref[...] * (rmask * DK ** -0.5)
        k = k_ref[...] * rmask
        v = v_ref[...] * rmask
        qe = q * jnp.exp(b)
        ke = k * jnp.exp(-b)
        kd = k * jnp.exp(b_last - b)
        e_last = jnp.exp(b_last)
        causal = _iota((CH, CH), 0) >= _iota((CH, CH), 1)
        a, upd, intra = {}, {}, {}
        for c, h in pairs:
            rows, ks, vs_ = _gla_slices(c, h)
            a[c, h] = jnp.where(causal, _dot_nt(qe[rows, ks], ke[rows, ks]), 0.0)
            upd[c, h] = _dot_tn(v[rows, vs_], kd[rows, ks])
        for c, h in pairs:
            rows, ks, vs_ = _gla_slices(c, h)
            intra[c, h] = _dot(a[c, h], v[rows, vs_])
        state = st[...]
        for c in range(GLA_PER_STEP):
            st_ref[0, c] = state
            for h in range(GLA_HEADS):
                rows, ks, vs_ = _gla_slices(c, h)
                o_ref[rows, vs_] = intra[c, h] + _dot_nt(qe[rows, ks], state[:, ks])
            state = state * e_last[c * CH:c * CH + 1] + jnp.concatenate([upd[c, h] for h in range(GLA_HEADS)], axis=1)
        st[...] = state

    blk = lambda w: pl.BlockSpec((BLK, w), lambda s: (_gla_block(s), 0))
    return pl.pallas_call(
        body, name="gla_fwd", grid=(steps,),
        in_specs=[blk(kw), blk(kw), blk(vw), blk(kw)],
        out_specs=[blk(vw), pl.BlockSpec((1, GLA_PER_STEP, DV, kw), lambda s: (s, 0, 0, 0))],
        out_shape=[pltpu.HBM((_lp(), vw), F32), pltpu.HBM((steps, GLA_PER_STEP, DV, kw), F32)],
        scratch_shapes=[pltpu.VMEM((DV, kw), F32)],
        compiler_params=_params(16, dimension_semantics=_seq()),
    )(*_hbm(qg, kg, vg, z))


def _post_mix(o_s, o_gla, r_g, h0, gn4, w_out, g1, b1):
    tm = _row_tile(384)
    lp = _lp()

    def body(os_ref, og_ref, r_ref, h0_ref, gn_ref, w_ref, g_ref, b_ref, o_ref, pre_ref, h1_ref):
        for pos, h in enumerate(HEAD_POS):
            o_ref[:, h * DH:(h + 1) * DH] = os_ref[:, pos * DH:(pos + 1) * DH].astype(BF16)
        for h in range(GLA_HEADS):
            hs = slice(h * DV, (h + 1) * DV)
            xg = og_ref[:, hs]
            n = xg * lax.rsqrt(jnp.mean(xg * xg, axis=-1, keepdims=True) + RMS_EPS) * gn_ref[...]
            r = r_ref[:, hs]
            o_ref[:, 512 + h * DV:512 + (h + 1) * DV] = (n * (r * _sigmoid(r))).astype(BF16)
        pre = ALPHA * h0_ref[...] + _dot(o_ref[...], w_ref[...])
        pre_ref[...] = pre
        xhat, _ = _ln_stats(pre)
        h1_ref[...] = xhat * g_ref[...] + b_ref[...]

    return pl.pallas_call(
        body, name="post_mix", grid=(lp // tm,),
        in_specs=[_rows(tm, 512), _rows(tm, 512), _rows(tm, 512), _rows(tm, D), _const((1, DV)), _const((D, D)),
                  _const((1, D)), _const((1, D))],
        out_specs=[_rows(tm, D), _rows(tm, D), _rows(tm, D)],
        out_shape=[pltpu.HBM((lp, D), BF16), pltpu.HBM((lp, D), F32),
                   pltpu.HBM((lp, D), F32)],
        compiler_params=_params(32, dimension_semantics=_seq()),
    )(*_hbm(o_s, o_gla, r_g, h0, gn4, w_out, g1, b1))


def _ffn_fwd_loss_bwd(h1, wg_t, wu_t, wd, target, g2, b2):
    lp = _lp()
    tm = max(t for t in range(BLK, 384 + 1, BLK) if lp % t == 0)
    steps = lp // tm
    last_blk = SEQ // BLK - 1
    half = D_FF // 2
    n_t = tm // BLK

    def body(*refs):
        h_ref, wg_ref, wu_ref, wd_ref = refs[:4]
        t_refs = refs[4:4 + n_t]
        g2_ref, b2_ref, a_ref, dgate_ref, dup_ref, dp_ref, loss_ref, dg_ref, db_ref, g_s, u_s, acc = refs[4 + n_t:]
        i = pl.program_id(0)

        @pl.when(i == 0)
        def _():
            acc[...] = jnp.zeros_like(acc)
            dg_ref[...] = jnp.zeros_like(dg_ref)
            db_ref[...] = jnp.zeros_like(db_ref)

        h = h_ref[...]
        hb = h.astype(BF16)
        pre = ALPHA * h
        for j in range(2):
            cols = slice(j * half, (j + 1) * half)
            g = _dot_nt(hb, wg_ref[cols, :])
            u = _dot_nt(hb, wu_ref[cols, :])
            g_s[:, cols] = g
            u_s[:, cols] = u
            pre = pre + _dot(g * _sigmoid(g) * u, wd_ref[cols, :])
        xhat, rstd = _ln_stats(pre)
        real = i * tm + _iota((tm, 1), 0) < SEQ
        target_rows = jnp.concatenate([t[...] for t in t_refs], axis=0)
        diff = jnp.where(real, xhat * g2_ref[...] + b2_ref[...] - target_rows, 0.0)
        acc[...] += jnp.sum(diff * diff, axis=0, keepdims=True)
        dy = diff * (1.0 / D)
        dpre = _ln_bwd(dy, xhat, rstd, g2_ref[...])
        dp_ref[...] = dpre
        dg_ref[...] += jnp.sum(dy * xhat, axis=0, keepdims=True)
        db_ref[...] += jnp.sum(dy, axis=0, keepdims=True)
        dpb = dpre.astype(BF16)
        for j in range(2):
            cols = slice(j * half, (j + 1) * half)
            g, u = g_s[:, cols], u_s[:, cols]
            sg = _sigmoid(g)
            silu = g * sg
            da = _dot_nt(dpb, wd_ref[cols, :])
            a_ref[:, cols] = (silu * u).astype(BF16)
            dgate_ref[:, cols] = (da * u * (sg * (1.0 + g * (1.0 - sg)))).astype(BF16)
            dup_ref[:, cols] = (da * silu).astype(BF16)

        @pl.when(i == steps - 1)
        def _():
            loss_ref[...] = jnp.zeros_like(loss_ref) + (0.5 / D) * jnp.sum(acc[...], axis=1, keepdims=True)

    t_spec = lambda k: pl.BlockSpec((BLK, D), lambda i: (jnp.minimum(i * n_t + k, last_blk), 0))
    return pl.pallas_call(
        body, name="ffn_fwd_loss_bwd", grid=(steps,),
        in_specs=[_rows(tm, D), _const((D_FF, D)), _const((D_FF, D)), _const((D_FF, D))]
        + [t_spec(k) for k in range(n_t)] + [_const((1, D)), _const((1, D))],
        out_specs=[_rows(tm, D_FF), _rows(tm, D_FF), _rows(tm, D_FF), _rows(tm, D), _acc((1, LANE)), _acc((1, D)),
                   _acc((1, D))],
        out_shape=[pltpu.HBM((lp, D_FF), BF16)] * 3 + [pltpu.HBM((lp, D), F32), pltpu.HBM((1, LANE), F32),
                                                         pltpu.HBM((1, D), F32), pltpu.HBM((1, D), F32)],
        scratch_shapes=[pltpu.VMEM((tm, D_FF), F32), pltpu.VMEM((tm, D_FF), F32), pltpu.VMEM((1, D), F32)],
        compiler_params=_params(58, dimension_semantics=_seq()),
    )(*_hbm(h1, wg_t, wu_t, wd, *[target] * n_t, g2, b2))


def _ffn_out_bwd(dpre2, dgate, dup, pre1, wg_t, wu_t, g1, w_out, o_gla, r_g, gn4):
    tm = _row_tile(384)
    lp = _lp()

    def body(dp_ref, dg_ref, du_ref, p1_ref, wg_ref, wu_ref, g1_ref, w_ref, og_ref, r_ref, gn_ref,
             dp1_ref, dg1_ref, db1_ref, dos_ref, dog_ref, dr_ref, dgn_ref):
        @pl.when(pl.program_id(0) == 0)
        def _():
            for acc_ref in (dg1_ref, db1_ref, dgn_ref):
                acc_ref[...] = jnp.zeros_like(acc_ref)

        dh1 = ALPHA * dp_ref[...] + _dot(dg_ref[...], wg_ref[...]) + _dot(du_ref[...], wu_ref[...])
        xhat, rstd1 = _ln_stats(p1_ref[...])
        dpre1 = _ln_bwd(dh1, xhat, rstd1, g1_ref[...])
        dp1_ref[...] = dpre1
        dg1_ref[...] += jnp.sum(dh1 * xhat, axis=0, keepdims=True)
        db1_ref[...] += jnp.sum(dh1, axis=0, keepdims=True)

        do = _dot_nt(dpre1, w_ref[...])
        for pos, h in enumerate(HEAD_POS):
            dos_ref[:, pos * DH:(pos + 1) * DH] = do[:, h * DH:(h + 1) * DH]
        gn = gn_ref[...]
        for h in range(GLA_HEADS):
            hs = slice(h * DV, (h + 1) * DV)
            xg = og_ref[:, hs]
            rstd = lax.rsqrt(jnp.mean(xg * xg, axis=-1, keepdims=True) + RMS_EPS)
            nx = xg * rstd
            r = r_ref[:, hs]
            sr = _sigmoid(r)
            d_o = do[:, 512 + h * DV:512 + (h + 1) * DV]
            dr_ref[:, hs] = d_o * (nx * gn) * (sr * (1.0 + r * (1.0 - sr)))
            dn = d_o * (r * sr)
            dgn_ref[...] += jnp.sum(dn * nx, axis=0, keepdims=True)
            dnx = dn * gn
            dog_ref[:, hs] = rstd * (dnx - nx * jnp.mean(dnx * nx, axis=-1, keepdims=True))

    return pl.pallas_call(
        body, name="ffn_out_bwd", grid=(lp // tm,),
        in_specs=[_rows(tm, D), _rows(tm, D_FF), _rows(tm, D_FF), _rows(tm, D), _const((D_FF, D)), _const((D_FF, D)),
                  _const((1, D)), _const((D, D)), _rows(tm, 512), _rows(tm, 512), _const((1, DV))],
        out_specs=[_rows(tm, D), _acc((1, D)), _acc((1, D)), _rows(tm, 512), _rows(tm, 512), _rows(tm, 512),
                   _acc((1, DV))],
        out_shape=[pltpu.HBM((lp, D), F32), pltpu.HBM((1, D), F32), pltpu.HBM((1, D), F32)]
        + [pltpu.HBM((lp, 512), F32)] * 3 + [pltpu.HBM((1, DV), F32)],
        compiler_params=_params(48, dimension_semantics=_seq()),
    )(*_hbm(dpre2, dgate, dup, pre1, wg_t, wu_t, g1, w_out, o_gla, r_g, gn4))


def _atb(a, b, name):
    lp = _lp()
    tm = _row_tile(1408)
    n, w = a.shape[1], b.shape[1]
    bw = 512 if n * w * 4 > (4 << 20) else w

    def body(a_ref, b_ref, o_ref):
        @pl.when(pl.program_id(1) == 0)
        def _():
            o_ref[...] = jnp.zeros_like(o_ref)

        o_ref[...] += _dot_tn(a_ref[...], b_ref[...])

    return pl.pallas_call(
        body, name=name, grid=(w // bw, lp // tm),
        in_specs=[pl.BlockSpec((tm, n), lambda j, k: (k, 0)), pl.BlockSpec((tm, bw), lambda j, k: (k, j))],
        out_specs=pl.BlockSpec((n, bw), lambda j, k: (0, j)),
        out_shape=pltpu.HBM((n, w), F32),
        compiler_params=_params(48, dimension_semantics=_seq(2)),
    )(*_hbm(a, b))


def _gla_bwd(qg, kg, vg, z, do_gla, st_all, token):
    steps = SEQ // BLK + 1
    kw, vw = GLA_HEADS * DK, GLA_HEADS * DV
    pairs = [(c, h) for c in range(GLA_PER_STEP) for h in range(GLA_HEADS)]
    heads = range(GLA_HEADS)

    def body(q_ref, k_ref, v_ref, z_ref, do_ref, st_ref, token_ref, dq_ref, dk_ref, dv_ref, dz_ref, dst):
        @pl.when(pl.program_id(0) == 0)
        def _():
            dst[...] = jnp.zeros_like(dst)

        rmask = _gla_rowmask(steps - 1 - pl.program_id(0))
        zz = z_ref[...]
        b, b_last = _gla_decay(zz, rmask)
        e_b, e_nb, e_kd, e_last = jnp.exp(b), jnp.exp(-b), jnp.exp(b_last - b), jnp.exp(b_last)
        q = q_ref[...] * (rmask * DK ** -0.5)
        k = k_ref[...] * rmask
        v = v_ref[...] * rmask
        qe, ke, kd = q * e_b, k * e_nb, k * e_kd
        d_o = do_ref[...]
        causal = _iota((CH, CH), 0) >= _iota((CH, CH), 1)
        a, da, dqe, dke, dv_intra, carry = {}, {}, {}, {}, {}, {}
        for c, h in pairs:
            rows, ks, vs_ = _gla_slices(c, h)
            a[c, h] = jnp.where(causal, _dot_nt(qe[rows, ks], ke[rows, ks]), 0.0)
            da[c, h] = jnp.where(causal, _dot_nt(d_o[rows, vs_], v[rows, vs_]), 0.0)
            carry[c, h] = _dot_tn(d_o[rows, vs_], qe[rows, ks])
        for c, h in pairs:
            rows, ks, vs_ = _gla_slices(c, h)
            dqe[c, h] = _dot(d_o[rows, vs_], st_ref[0, c][:, ks]) + _dot(da[c, h], ke[rows, ks])
            dke[c, h] = _dot_tn(da[c, h], qe[rows, ks])
            dv_intra[c, h] = _dot_tn(a[c, h], d_o[rows, vs_])
        dstate = dst[...]
        dkd, db_decay = {}, {}
        for c in reversed(range(GLA_PER_STEP)):
            for h in heads:
                rows, ks, vs_ = _gla_slices(c, h)
                dkd[c, h] = _dot(v[rows, vs_], dstate[:, ks])
                dv_ref[rows, vs_] = dv_intra[c, h] + _dot_nt(kd[rows, ks], dstate[:, ks])
            chunk_last = e_last[c * CH:c * CH + 1]
            db_decay[c] = jnp.sum(dstate * st_ref[0, c], axis=0, keepdims=True) * chunk_last
            dstate = dstate * chunk_last + jnp.concatenate([carry[c, h] for h in heads], axis=1)
        dst[...] = dstate
        rows_of = lambda parts: jnp.concatenate(
            [jnp.concatenate([parts[c, h] for h in heads], axis=1) for c in range(GLA_PER_STEP)], axis=0)
        dqe_all, dke_all, dkd_all = rows_of(dqe), rows_of(dke), rows_of(dkd)
        dq_ref[...] = dqe_all * e_b * (rmask * DK ** -0.5)
        dk_ref[...] = (dke_all * e_nb + dkd_all * e_kd) * rmask
        dkd_kd = dkd_all * kd
        db = dqe_all * qe - dke_all * ke - dkd_kd
        _, upper, same = _gla_chunk_masks()
        decay_rows = jnp.concatenate([jnp.broadcast_to(db_decay[c], (CH, kw)) for c in range(GLA_PER_STEP)], axis=0)
        dlog_g = _dot_exact(upper.astype(F32), db) + _dot_exact(same.astype(F32), dkd_kd) + decay_rows
        dz_ref[...] = dlog_g * (rmask / GLA_TAU) * _sigmoid(-zz)

    blk = lambda w: pl.BlockSpec((BLK, w), lambda s: (_gla_block(steps - 1 - s), 0))
    return pl.pallas_call(
        body, name="gla_bwd", grid=(steps,),
        in_specs=[blk(kw), blk(kw), blk(vw), blk(kw), blk(vw),
                  pl.BlockSpec((1, GLA_PER_STEP, DV, kw), lambda s: (steps - 1 - s, 0, 0, 0)), _const(TOKEN)],
        out_specs=[blk(kw), blk(kw), blk(vw), blk(kw)],
        out_shape=[pltpu.HBM((_lp(), kw), F32), pltpu.HBM((_lp(), kw), F32),
                   pltpu.HBM((_lp(), vw), F32), pltpu.HBM((_lp(), kw), F32)],
        scratch_shapes=[pltpu.VMEM((DV, kw), F32)],
        compiler_params=_params(16, dimension_semantics=_seq()),
    )(*_hbm(qg, kg, vg, z, do_gla, st_all), token)


def _swa_bwd(sinks, qs, ks, vs, do_s, token):
    nb = SEQ // BLK
    kvw = SWA_KV_HEADS * DH
    scale = DH ** -0.5
    heads = range(SWA_HEADS)

    def body(sink_ref, q_ref, km_ref, kp_ref, kc_ref, vm_ref, vp_ref, vc_ref, do_ref, token_ref,
             dq_ref, dk_ref, dv_ref, dsink_ref, carry_k, carry_v, meta_k, meta_v):
        n = pl.program_id(0)

        @pl.when(n == 0)
        def _():
            for r in (carry_k, carry_v, meta_k, meta_v):
                r[...] = jnp.zeros_like(r)
            dsink_ref[...] = jnp.zeros_like(dsink_ref)

        @pl.when(n <= nb)
        def _():
            negdist, maskbias = _swa_bias(n)
            lane = _iota((1, LANE), 1)
            k_all = jnp.concatenate([km_ref[...], kp_ref[...], kc_ref[...]], axis=0).astype(BF16)
            v_all = jnp.concatenate([vm_ref[...], vp_ref[...], vc_ref[...]], axis=0).astype(BF16)
            q = [_swa_half(q_ref, pos, scale) for pos in heads]
            d_o = [_swa_half(do_ref, pos) for pos in heads]
            t = [_dot_nt(q[pos], k_all) + (2.0 ** -(HEAD_POS[pos] + 1) * negdist + maskbias) for pos in heads]
            dp = [_dot_nt(d_o[pos], v_all) for pos in heads]
            soft = [_swa_softmax(t[pos], sink_ref[HEAD_POS[pos]]) for pos in heads]
            p = [s[0] for s in soft]
            delta = [jnp.sum(p[pos] * dp[pos], axis=-1, keepdims=True) for pos in heads]
            ds = [(p[pos] * (dp[pos] - delta[pos])).astype(BF16) for pos in heads]
            dq = [_dot(ds[pos], k_all) for pos in heads]
            for col in range(SWA_HEADS // 2):
                dq_ref[:, col * LANE:(col + 1) * LANE] = scale * _swa_merge(dq[2 * col], dq[2 * col + 1])
            dsink = jnp.zeros((1, LANE), F32)
            for pos in heads:
                dsink = dsink + jnp.where(lane == HEAD_POS[pos],
                                          -jnp.sum(soft[pos][1] * delta[pos], axis=0, keepdims=True), 0.0)
            dsink_ref[...] += dsink
            dk3 = _dot_tn(jnp.concatenate(q, axis=0), jnp.concatenate(ds, axis=0)).T
            dv3 = _dot_tn(jnp.concatenate(d_o, axis=0), jnp.concatenate([x.astype(BF16) for x in p], axis=0)).T
            meta_k[...] += dk3[0:BLK]
            meta_v[...] += dv3[0:BLK]
            dk_ref[...] = carry_k[...] + dk3[BLK:2 * BLK]
            dv_ref[...] = carry_v[...] + dv3[BLK:2 * BLK]
            carry_k[...] = dk3[2 * BLK:3 * BLK]
            carry_v[...] = dv3[2 * BLK:3 * BLK]

        @pl.when(n == nb + 1)
        def _():
            dk_ref[...] = meta_k[...]
            dv_ref[...] = meta_v[...]

    kv_out = pl.BlockSpec((BLK, kvw), lambda n: (jnp.where(n == nb + 1, nb, jnp.clip(n - 1, 0, nb - 1)), 0))
    qblk = pl.BlockSpec((BLK, SWA_HEADS * DH), lambda n: (jnp.minimum(n, nb), 0))
    return pl.pallas_call(
        body, name="swa_bwd", grid=(nb + 2,),
        in_specs=[pl.BlockSpec(memory_space=pltpu.SMEM), qblk] + _swa_kv_specs(kvw) + _swa_kv_specs(kvw)
        + [qblk, _const(TOKEN)],
        out_specs=[qblk, kv_out, kv_out, _acc((1, LANE))],
        out_shape=[pltpu.HBM((_lp(), SWA_HEADS * DH), F32), pltpu.HBM((_lp(), kvw), F32),
                   pltpu.HBM((_lp(), kvw), F32), pltpu.HBM((1, LANE), F32)],
        scratch_shapes=[pltpu.VMEM((BLK, kvw), F32)] * 4,
        compiler_params=_params(16, dimension_semantics=_seq()),
    )(sinks, *_hbm(qs, ks, ks, ks, vs, vs, vs, do_s), token)


def _in_bwd(dqs, dks, dvs, dqg, dkg, dvg, drg, dz, dpre1, w_in_t, wg2_p):
    tm = _row_tile(384)
    lp = _lp()
    widths = (512, 128, 128, 256, 256, 512, 512)
    offs = (O_QS, O_KS, O_VS, O_QG, O_KG, O_VG, O_RG)

    def body(*refs):
        parts, (dz_ref, dp1_ref, w_ref, wg2_ref, dproj_ref, dh0_ref, dbin_ref, dbg_ref) = refs[:7], refs[7:]

        @pl.when(pl.program_id(0) == 0)
        def _():
            dbin_ref[...] = jnp.zeros_like(dbin_ref)
            dbg_ref[...] = jnp.zeros_like(dbg_ref)

        for pos, h in enumerate(HEAD_POS):
            val = parts[0][:, pos * DH:(pos + 1) * DH]
            dproj_ref[:, O_QS + h * DH:O_QS + (h + 1) * DH] = val.astype(BF16)
            dbin_ref[:, O_QS + h * DH:O_QS + (h + 1) * DH] += jnp.sum(val, axis=0, keepdims=True)
        for p_ref, off, wd in zip(parts[1:], offs[1:], widths[1:]):
            val = p_ref[...]
            dproj_ref[:, off:off + wd] = val.astype(BF16)
            dbin_ref[:, off:off + wd] += jnp.sum(val, axis=0, keepdims=True)
        dz = dz_ref[...]
        dlr = _dot_nt(dz, wg2_ref[...])
        dproj_ref[:, O_LR:O_LR + LANE] = dlr.astype(BF16)
        dbin_ref[:, O_LR:O_LR + LANE] += jnp.sum(dlr, axis=0, keepdims=True)
        dbg_ref[...] += jnp.sum(dz, axis=0, keepdims=True)
        dh0_ref[...] = ALPHA * dp1_ref[...] + _dot(dproj_ref[...], w_ref[...])

    return pl.pallas_call(
        body, name="in_bwd", grid=(lp // tm,),
        in_specs=[_rows(tm, w) for w in widths] + [_rows(tm, 256), _rows(tm, D), _const((D_IN_P, D)), _const((LANE, 256))],
        out_specs=[_rows(tm, D_IN_P), _rows(tm, D), _acc((1, D_IN_P)), _acc((1, 256))],
        out_shape=[pltpu.HBM((lp, D_IN_P), BF16), pltpu.HBM((lp, D), F32),
                   pltpu.HBM((1, D_IN_P), F32), pltpu.HBM((1, 256), F32)],
        compiler_params=_params(40, dimension_semantics=_seq()),
    )(*_hbm(dqs, dks, dvs, dqg, dkg, dvg, drg, dz, dpre1, w_in_t, wg2_p))


def _ln_in_bwd(x, meta_ext, dh0, g, token):
    tr = min(LN_ROWS, SEQ)

    def ln_bwd(x_ref, dh_ref, g_ref, dx_ref, dg_ref, db_ref):
        @pl.when(pl.program_id(0) == 0)
        def _():
            dg_ref[...] = jnp.zeros_like(dg_ref)
            db_ref[...] = jnp.zeros_like(db_ref)

        xhat, rstd = _ln_stats(x_ref[...])
        dh = dh_ref[...]
        dx_ref[...] = _ln_bwd(dh, xhat, rstd, g_ref[...])
        dg_ref[...] += jnp.sum(dh * xhat, axis=0, keepdims=True)
        db_ref[...] += jnp.sum(dh, axis=0, keepdims=True)

    def body(x_ref, dh_ref, g_ref, token_ref, dx_ref, dg_ref, db_ref):
        ln_bwd(x_ref, dh_ref, g_ref, dx_ref, dg_ref, db_ref)

    def meta_body(m_ref, dh_ref, g_ref, dm_ref, dg_ref, db_ref):
        ln_bwd(m_ref, dh_ref, g_ref, dm_ref, dg_ref, db_ref)

    sums = [pltpu.HBM((1, D), F32), pltpu.HBM((1, D), F32)]
    dx, dg, db = pl.pallas_call(
        body, name="ln_in_bwd", grid=(SEQ // tr,),
        in_specs=[_rows(tr, D), _rows(tr, D), _const((1, D)), _const(TOKEN)],
        out_specs=[_rows(tr, D), _acc((1, D)), _acc((1, D))],
        out_shape=[pltpu.HBM((SEQ, D), F32)] + sums,
        compiler_params=_params(32, dimension_semantics=_seq()),
    )(*_hbm(x, dh0, g), token)
    dm, dg_m, db_m = pl.pallas_call(
        meta_body, name="ln_in_bwd_meta", grid=(1,),
        in_specs=[_const((BLK, D)), pl.BlockSpec((BLK, D), lambda i: (SEQ // BLK, 0)), _const((1, D))],
        out_specs=[_acc((BLK, D)), _acc((1, D)), _acc((1, D))],
        out_shape=[pltpu.HBM((BLK, D), F32)] + sums,
        compiler_params=_params(16, dimension_semantics=_seq()),
    )(*_hbm(meta_ext, dh0, g))
    return dx, dm, dg + dg_m, db + db_m


def _local_step(x, target, ln_in_g, ln_in_b, b_in, bg2, sinks, gn, g1, b1, g2, b2,
                token, fetch_first, fetch_rest, ship_ffn, ship_w_in):
    row = lambda v: v.reshape(1, -1).astype(F32)
    b_in_p = jnp.pad(row(b_in), ((0, 0), (0, D_IN_P - D_IN)))
    gn4 = row(gn)
    sinks = sinks.reshape(-1).astype(F32)

    h_real = _ln_in_fwd_real(x, row(ln_in_g), row(ln_in_b), token)
    w_in_t, meta_full, wg2 = fetch_first([h_real])
    meta_ext = jnp.pad(meta_full, ((META_OFF, BLK - CH), (0, 0)))
    wg2_p = jnp.pad(wg2, ((0, LANE - wg2.shape[0]), (0, 0))).astype(BF16)
    h0 = _ln_in_fwd_meta(h_real, meta_ext, row(ln_in_g), row(ln_in_b))
    qs, ks, vs, qg, kg, vg, rg, glr, z = _in_proj(h0, w_in_t, b_in_p, wg2_p, row(bg2))
    o_s = _swa_fwd(sinks, qs, ks, vs)
    o_gla, st_all = _gla_fwd(qg, kg, vg, z)
    w_out, wg_t, wu_t, wd = fetch_rest([o_s, o_gla])
    o, pre1, h1 = _post_mix(o_s, o_gla, rg, h0, gn4, w_out, row(g1), row(b1))
    a, dgate, dup, dpre2, loss, dg2, db2 = _ffn_fwd_loss_bwd(h1, wg_t, wu_t, wd, target, row(g2), row(b2))
    dpre1, dg1, db1, do_s, do_gla, drg, dgn = _ffn_out_bwd(dpre2, dgate, dup, pre1, wg_t, wu_t, row(g1), w_out, o_gla,
                                                           rg, gn4)
    dwd = _atb(a, dpre2, "dw_down")
    dwg_t = _atb(dgate, h1, "dw_gate")
    dwu_t = _atb(dup, h1, "dw_up")
    dw_out = _atb(o, dpre1, "dw_out")
    token = ship_ffn(dict(w_out=dw_out, w_g=dwg_t, w_u=dwu_t, w_d=dwd))
    dqg, dkg, dvg, dz = _gla_bwd(qg, kg, vg, z, do_gla, st_all, token)
    dqs, dks, dvs, dsinks = _swa_bwd(sinks, qs, ks, vs, do_s, token)
    dproj, dh0, db_in_p, dbg2 = _in_bwd(dqs, dks, dvs, dqg, dkg, dvg, drg, dz, dpre1, w_in_t, wg2_p)
    token = ship_w_in(_atb(dproj, h0, "dw_in"))
    dwg2_p = _atb(glr, dz, "dw_gate_lr2")
    dx, dmeta_blk, dg_in, db_in_ln = _ln_in_bwd(x, meta_ext, dh0, row(ln_in_g), token)

    grads = dict(
        meta=dmeta_blk[META_OFF:CH], ln_in_g=dg_in, ln_in_b=db_in_ln, ln1_g=dg1, ln1_b=db1, ln2_g=dg2, ln2_b=db2,
        b_in=db_in_p[:, :D_IN], wg2=dwg2_p[:wg2.shape[0]], bg2=dbg2, sinks=dsinks[:, :SWA_HEADS], gn=dgn)
    return loss[0, 0], dx, grads


HBM = pl.BlockSpec(memory_space=pltpu.HBM)


def _place():
    return lax.axis_index("x"), lax.axis_index("y"), lax.axis_index("c")


def _other_chips(x, y):
    return [(1 - x, y), (x, 1 - y), (1 - x, 1 - y)]


def _dma_sems(n):
    return pltpu.SemaphoreType.DMA((n,))


def _comm_params():
    return pltpu.CompilerParams(has_side_effects=True)


SEM = pl.BlockSpec(memory_space=pltpu.SEMAPHORE)


def _ici_copies(kind, landing, srcs, lands, send_sems, recv_sems):
    x, y, c = _place()
    mine = 2 * x + y
    copies = []
    for a in range(len(srcs)):
        for j, (px, py) in enumerate(_other_chips(x, y)):
            slab = 2 * px + py if landing else mine
            if kind == "gather":
                src, dst = srcs[a].at[c], lands[a].at[slab, c]
            else:
                src, dst = srcs[a].at[2 * px + py], lands[a].at[slab]
            copies.append(pltpu.make_async_remote_copy(src, dst, send_sems.at[3 * a + j], recv_sems.at[3 * a + j],
                                                       device_id=(px, py, c), device_id_type=MESH))
    return copies


def _split_params():
    return pltpu.CompilerParams(has_side_effects=pltpu.SideEffectType.DATAFLOW_SIDE_EFFECTING)


def _ici_start(kind, srcs, land_shapes, after, name):
    n = len(srcs)
    lands = [pltpu.with_memory_space_constraint(lax.empty(s, a.dtype), pltpu.HBM) for s, a in zip(land_shapes, srcs)]

    def body(*refs):
        outs = refs[2 * n + len(after):]
        for cp in _ici_copies(kind, False, refs[:n], refs[n:2 * n], outs[0], outs[1]):
            cp.start()
        outs[-1][...] = jnp.zeros(TOKEN, F32)

    outs = pl.pallas_call(
        body, name=name, in_specs=[HBM] * (2 * n) + [pl.BlockSpec(memory_space=pl.ANY)] * len(after),
        out_specs=[SEM, SEM] + [HBM] * (2 * n) + [pl.BlockSpec(memory_space=pltpu.VMEM)],
        out_shape=[_dma_sems(3 * n)] * 2 + [pltpu.HBM(a.shape, a.dtype) for a in list(srcs) + lands]
        + [jax.ShapeDtypeStruct(TOKEN, F32)],
        input_output_aliases={i: 2 + i for i in range(2 * n)},
        compiler_params=_split_params(),
    )(*_hbm(*srcs), *lands, *after)
    return outs[:-1], outs[-1]


def _ici_wait(kind, handle, after, name):
    n = (len(handle) - 2) // 2

    def body(*refs):
        for cp in _ici_copies(kind, True, refs[:n], refs[n:2 * n], refs[2 * n], refs[2 * n + 1]):
            cp.wait_send()
            cp.wait_recv()

    outs = pl.pallas_call(
        body, name=name, in_specs=[HBM] * (2 * n) + [SEM, SEM] + [pl.BlockSpec(memory_space=pl.ANY)] * len(after),
        out_specs=[HBM] * (2 * n), out_shape=[pltpu.HBM(a.shape, a.dtype) for a in handle[2:]],
        input_output_aliases={i: i for i in range(2 * n)},
        compiler_params=_split_params(),
    )(*handle[2:], handle[0], handle[1], *after)
    return list(outs[n:])


def _sibling_forward(lands, name):
    n = len(lands)

    def body(*refs):
        outs = refs[n:2 * n]
        send_sems, recv_sems = refs[2 * n:]
        x, y, c = _place()

        def copy(a, j, half):
            px, py = _other_chips(x, y)[j]
            blk = outs[a].at[2 * px + py, half]
            return pltpu.make_async_remote_copy(blk, blk, send_sems.at[3 * a + j], recv_sems.at[3 * a + j],
                                                device_id=(x, y, 1 - c), device_id_type=MESH)

        pairs = [(a, j) for a in range(n) for j in range(3)]
        for a, j in pairs:
            copy(a, j, c).start()
        for a, j in pairs:
            copy(a, j, 1 - c).wait_recv()
        for a, j in pairs:
            copy(a, j, c).wait_send()

    return pl.pallas_call(
        body, name=name, in_specs=[HBM] * n, out_specs=[HBM] * n,
        out_shape=[pltpu.HBM(a.shape, a.dtype) for a in lands],
        input_output_aliases={a: a for a in range(n)},
        scratch_shapes=[_dma_sems(3 * n)] * 2,
        compiler_params=_comm_params(),
    )(*_hbm(*lands))


def _sibling_exchange(grads, name):
    n = len(grads)

    def body(*refs):
        ins, outs = refs[:n], refs[n:2 * n]
        send_sems, recv_sems = refs[2 * n:]
        x, y, c = _place()
        copies = []
        for a in range(n):
            for s in range(N_CHIPS):
                cp = pltpu.make_async_remote_copy(ins[a].at[s, 1 - c], outs[a].at[s], send_sems.at[N_CHIPS * a + s],
                                                  recv_sems.at[N_CHIPS * a + s], device_id=(x, y, 1 - c),
                                                  device_id_type=MESH)
                cp.start()
                copies.append(cp)
        for cp in copies:
            cp.wait_recv()
        for cp in copies:
            cp.wait_send()

    return pl.pallas_call(
        body, name=name, in_specs=[HBM] * n, out_specs=[HBM] * n,
        out_shape=[pltpu.HBM((N_CHIPS, g.shape[2], D), F32) for g in grads],
        scratch_shapes=[_dma_sems(N_CHIPS * n)] * 2,
        compiler_params=_comm_params(),
    )(*_hbm(*grads))


def _add_halves(core, grad, recv, dtype, name):
    h = grad.shape[2]

    def body(c_ref, a_ref, b_ref, o_ref):
        o_ref[...] = (a_ref[0] + b_ref[...]).astype(dtype)

    return pl.pallas_call(
        body, name=name,
        grid_spec=pltpu.PrefetchScalarGridSpec(
            num_scalar_prefetch=1, grid=(N_CHIPS,),
            in_specs=[pl.BlockSpec((1, 1, h, D), lambda s, c: (s, c[0], 0, 0)),
                      pl.BlockSpec((1, h, D), lambda s, c: (s, 0, 0))],
            out_specs=pl.BlockSpec((1, h, D), lambda s, c: (s, 0, 0))),
        out_shape=pltpu.HBM((N_CHIPS, h, D), dtype),
        compiler_params=_params(16, dimension_semantics=_seq()),
    )(core, *_hbm(grad, recv))


def _chip_scatter(parts, with_own):
    n = len(parts)

    def body(*refs):
        ins, outs = refs[:n], refs[n:2 * n]
        send_sems, recv_sems, local_sems = refs[2 * n:]
        x, y, c = _place()
        mine = 2 * x + y
        chips = _other_chips(x, y)
        local = [pltpu.make_async_copy(ins[a].at[mine], outs[a].at[mine], local_sems.at[a]) for a in range(n)
                 if with_own[a]]
        for cp in local:
            cp.start()
        sends = []
        for a in range(n):
            for j, (px, py) in enumerate(chips):
                cp = pltpu.make_async_remote_copy(ins[a].at[2 * px + py], outs[a].at[mine], send_sems.at[3 * a + j],
                                                  recv_sems.at[3 * a + j], device_id=(px, py, c), device_id_type=MESH)
                cp.start()
                sends.append(cp)
        for a in range(n):
            for j, (px, py) in enumerate(chips):
                pltpu.make_async_remote_copy(ins[a].at[mine], outs[a].at[2 * px + py], send_sems.at[3 * a + j],
                                             recv_sems.at[3 * a + j], device_id=(px, py, c),
                                             device_id_type=MESH).wait_recv()
        for cp in sends:
            cp.wait_send()
        for cp in local:
            cp.wait()

    return pl.pallas_call(
        body, name="chip_scatter", in_specs=[HBM] * n, out_specs=[HBM] * n,
        out_shape=[pltpu.HBM(p.shape, p.dtype) for p in parts],
        scratch_shapes=[_dma_sems(3 * n)] * 2 + [_dma_sems(n)],
        compiler_params=_comm_params(),
    )(*_hbm(*parts))


def _sum_chips(slots, first, rest, name):
    h = first.shape[1]

    def body(i_ref, a_ref, b_ref, c_ref, d_ref, o_ref):
        o_ref[...] = ((a_ref[...].astype(F32) + b_ref[...].astype(F32)) + c_ref[...].astype(F32)) + d_ref[...].astype(F32)

    slab = lambda k: pl.BlockSpec((1, h, D), lambda i, ix: (ix[k], 0, 0))
    return pl.pallas_call(
        body, name=name,
        grid_spec=pltpu.PrefetchScalarGridSpec(num_scalar_prefetch=1, grid=(1,),
                                               in_specs=[slab(0), slab(1), slab(2), slab(3)], out_specs=slab(4)),
        out_shape=pltpu.HBM((2, h, D), F32),
        compiler_params=_params(16, dimension_semantics=_seq()),
    )(slots, *_hbm(first, rest, rest, rest))


def _join_halves(halves):
    n = len(halves)

    def body(*refs):
        outs = refs[n:2 * n]
        send_sems, recv_sems = refs[2 * n:]
        x, y, c = _place()

        def copy(a, slab):
            return pltpu.make_async_remote_copy(outs[a].at[slab], outs[a].at[slab], send_sems.at[a], recv_sems.at[a],
                                                device_id=(x, y, 1 - c), device_id_type=MESH)

        for a in range(n):
            copy(a, c).start()
        for a in range(n):
            copy(a, 1 - c).wait_recv()
        for a in range(n):
            copy(a, c).wait_send()

    return pl.pallas_call(
        body, name="join_halves", in_specs=[HBM] * n, out_specs=[HBM] * n,
        out_shape=[pltpu.HBM(h.shape, F32) for h in halves],
        input_output_aliases={a: a for a in range(n)},
        scratch_shapes=[_dma_sems(n)] * 2,
        compiler_params=_comm_params(),
    )(*_hbm(*halves))


def _chip_partials(grads, wire_dtypes, names):
    core = lax.axis_index("c").astype(jnp.int32).reshape(1)
    recv = _sibling_exchange(grads, "sibling_exchange_" + names[0])
    return [_add_halves(core, g, r, dt, "add_halves_" + nm) for g, r, dt, nm in zip(grads, recv, wire_dtypes, names)]


def _finish_reduce(parts, got, same_order, names):
    x, y, c = _place()
    others = [2 * px + py for px, py in _other_chips(x, y)]
    own_first = jnp.stack([2 * x + y] + others + [c]).astype(jnp.int32)
    chip_order = jnp.stack([0 * c, 0 * c + 1, 0 * c + 2, 0 * c + 3, c]).astype(jnp.int32)
    halves = [_sum_chips(chip_order, q, q, "sum_chips_" + nm) if fixed else _sum_chips(own_first, p, q, "sum_chips_" + nm)
              for p, q, fixed, nm in zip(parts, got, same_order, names)]
    return [f.reshape(2 * f.shape[1], D) for f in _join_halves(halves)]


def _adamw(w, g, m, v, name):
    rows, cols = w.shape
    if rows % 8 == 0:
        tr = max(t for t in range(8, 257, 8) if rows % t == 0)
        grid, blk = (rows // tr,), pl.BlockSpec((tr, cols), lambda i: (i, 0))
    else:
        grid, blk = (cols // 256,), pl.BlockSpec((rows, 256), lambda i: (0, i))

    def body(w_ref, g_ref, m_ref, v_ref, d_ref, nm_ref, nv_ref):
        d_ref[...], nm_ref[...], nv_ref[...] = _adamw_math(w_ref[...], g_ref[...], m_ref[...], v_ref[...])

    return pl.pallas_call(
        body, name=name, grid=grid,
        in_specs=[blk] * 4, out_specs=[blk] * 3,
        out_shape=[pltpu.HBM(w.shape, F32)] * 3,
        compiler_params=_params(32, dimension_semantics=_seq()),
    )(*_hbm(w, g, m, v))


def _adamw_math(w, g, m, v):
    nm = ADAM_B1 * m + (1.0 - ADAM_B1) * g
    nv = ADAM_B2 * v + (1.0 - ADAM_B2) * (g * g)
    m_hat = nm / (1.0 - ADAM_B1 ** ADAM_STEP)
    v_hat = nv / (1.0 - ADAM_B2 ** ADAM_STEP)
    return -ADAM_LR * (m_hat / (jnp.sqrt(v_hat) + ADAM_EPS) + ADAM_WD * w), nm, nv


SMALL = (("meta_tokens", (N_META, D // N_CHIPS)), ("ln_in_g", (1, D)), ("ln_in_b", (1, D)), ("b_in", (1, D_IN)),
         ("w_gate_lr2", (16, 256 // N_CHIPS)), ("b_gate_lr2", (1, 256)), ("attn_sinks", (1, SWA_HEADS)),
         ("gla_norm_g", (1, DV)), ("ln1_g", (1, D)), ("ln1_b", (1, D)), ("ln2_g", (1, D)), ("ln2_b", (1, D)))
ROW_META, ROW_B_IN, ROW_TAIL, ROW_WG2 = 0, 22, 25, 32
ROW_LN = dict(ln_in_g=16, ln_in_b=17, ln1_g=18, ln1_b=19, ln2_g=20, ln2_b=21)
TAIL_BG2, TAIL_SINKS, TAIL_GN, TAIL_LOSS = 0, 256, 256 + SWA_HEADS, 256 + SWA_HEADS + DV


def _adamw_small(chip, pack, params):
    n = len(SMALL)

    def body(chip_ref, p_ref, *refs):
        ins, outs = refs[:3 * n], refs[3 * n:]
        c = chip_ref[0]

        def mine(width, rows):
            part = lambda s: p_ref[rows, s * width:(s + 1) * width]
            return jnp.where(c == 0, part(0), jnp.where(c == 1, part(1), jnp.where(c == 2, part(2), part(3))))

        tail = lambda lo, width: p_ref[ROW_TAIL:ROW_TAIL + 1, lo:lo + width]
        grads = dict(
            meta_tokens=mine(D // N_CHIPS, slice(ROW_META, ROW_META + N_META)),
            b_in=jnp.concatenate([p_ref[ROW_B_IN:ROW_B_IN + 1, :], p_ref[ROW_B_IN + 1:ROW_B_IN + 2, :],
                                  p_ref[ROW_B_IN + 2:ROW_B_IN + 3, 0:D_IN - 2 * D]], axis=1),
            w_gate_lr2=mine(256 // N_CHIPS, slice(ROW_WG2, ROW_WG2 + 16)),
            b_gate_lr2=tail(TAIL_BG2, 256), attn_sinks=tail(TAIL_SINKS, SWA_HEADS), gla_norm_g=tail(TAIL_GN, DV),
            **{k: p_ref[r:r + 1, :] for k, r in ROW_LN.items()})
        for i, (name, _) in enumerate(SMALL):
            g = grads[name]
            outs[4 * i][...] = g
            outs[4 * i + 1][...], outs[4 * i + 2][...], outs[4 * i + 3][...] = _adamw_math(
                ins[3 * i][...], g, ins[3 * i + 1][...], ins[3 * i + 2][...])

    whole = lambda shape: pl.BlockSpec(shape, lambda i, c: (0, 0))
    outs = pl.pallas_call(
        body, name="adamw_small",
        grid_spec=pltpu.PrefetchScalarGridSpec(
            num_scalar_prefetch=1, grid=(1,),
            in_specs=[whole(pack.shape)] + [whole(s) for _, s in SMALL for _ in range(3)],
            out_specs=[whole(s) for _, s in SMALL for _ in range(4)]),
        out_shape=[pltpu.HBM(s, F32) for _, s in SMALL for _ in range(4)],
        compiler_params=_params(16, dimension_semantics=_seq()),
    )(chip, *_hbm(pack, *[a for p in params for a in p]))
    return [outs[4 * i:4 * i + 4] for i in range(n)]


def _flat_rows(v, rows):
    flat = v.reshape(-1).astype(F32)
    return jnp.pad(flat, (0, rows * D - flat.shape[0])).reshape(rows, D)


def _small_pack(gr):
    tail = jnp.concatenate([gr["bg2"].reshape(-1), gr["sinks"].reshape(-1), gr["gn"].reshape(-1), gr["loss"].reshape(-1)])
    rows = [gr["meta"].reshape(N_META, D)] + [gr[k].reshape(1, D) for k in ROW_LN]
    rows += [_flat_rows(gr["b_in"], 3), _flat_rows(tail, 1), jnp.zeros((ROW_WG2 - ROW_TAIL - 1, D), F32),
             jnp.pad(gr["wg2"], ((0, 0), (0, D - gr["wg2"].shape[1])))]
    return jnp.concatenate(rows, axis=0)


BIG = ("w_in", "w_out", "w_g", "w_u", "w_d")


def kernel(x, meta_tokens, ln_in_g, ln_in_b, w_in, b_in, w_gate_lr2, b_gate_lr2, attn_sinks, gla_norm_g, w_out, ln1_g, ln1_b, w_ffn_gate, w_ffn_up, w_ffn_down, ln2_g, ln2_b, loss_target, m_meta_tokens, m_ln_in_g, m_ln_in_b, m_w_in, m_b_in, m_w_gate_lr2, m_b_gate_lr2, m_attn_sinks, m_gla_norm_g, m_w_out, m_ln1_g, m_ln1_b, m_w_ffn_gate, m_w_ffn_up, m_w_ffn_down, m_ln2_g, m_ln2_b, v_meta_tokens, v_ln_in_g, v_ln_in_b, v_w_in, v_b_in, v_w_gate_lr2, v_b_gate_lr2, v_attn_sinks, v_gla_norm_g, v_w_out, v_ln1_g, v_ln1_b, v_w_ffn_gate, v_w_ffn_up, v_w_ffn_down, v_ln2_g, v_ln2_b):
    chip = 2 * lax.axis_index("x") + lax.axis_index("y")

    halves = lambda a: a.reshape(2, a.shape[0] // 2, a.shape[1])
    r_in = SHARD_ROWS["w_in"]
    first = [halves(a) for a in (jnp.pad(w_in[0].T.astype(BF16), ((0, W_IN_WIN - r_in), (0, 0))), meta_tokens,
                                 w_gate_lr2[0])]
    rest = [halves(a) for a in (w_out[0].astype(BF16), w_ffn_gate[0].T.astype(BF16), w_ffn_up[0].T.astype(BF16),
                                w_ffn_down[0].astype(BF16))]
    lands = lambda arrs: [(N_CHIPS,) + a.shape for a in arrs]
    first_handle, first_token = _ici_start("gather", first, lands(first), [], "gather_first_start")
    rest_handle, token = _ici_start("gather", rest, lands(rest), [first_token], "gather_rest_start")

    def fetch(handle, shards, after, name):
        got = _sibling_forward(_ici_wait("gather", handle, after, name + "_wait"), name + "_forward")
        return [lax.dynamic_update_index_in_dim(g, s, chip, axis=0) for g, s in zip(got, shards)]

    def fetch_first(after):
        g_in, g_meta, g_wg2 = fetch(first_handle, first, after, "gather_first")
        w_in_t = jnp.pad(g_in.reshape(N_CHIPS, W_IN_WIN, D)[:, :r_in].reshape(D_IN, D), ((0, D_IN_P - D_IN), (0, 0)))
        meta_full = jnp.concatenate([g_meta[s].reshape(N_META, -1) for s in range(N_CHIPS)], axis=1)
        wg2_full = jnp.concatenate([g_wg2[s].reshape(w_gate_lr2.shape[1], -1) for s in range(N_CHIPS)], axis=1)
        return w_in_t, meta_full, wg2_full

    def fetch_rest(after):
        return [g.reshape(-1, D) for g in fetch(rest_handle, rest, after, "gather_rest")]

    sent = {}

    def ship(key, grads, names):
        parts = _chip_partials([g.reshape(N_CHIPS, 2, -1, D) for g in grads], [BF16] * len(grads), names)
        handle, ship_token = _ici_start("scatter", parts, [p.shape for p in parts], [], "scatter_" + key + "_start")
        sent[key] = (parts, handle)
        return ship_token

    def ship_ffn(g):
        return ship("ffn", [g[k] for k in BIG[1:]], list(BIG[1:]))

    def ship_w_in(dw_in_t):
        win_start = [s * r_in // BF16_ROWS * BF16_ROWS for s in range(N_CHIPS)]
        return ship("w_in", [jnp.stack([dw_in_t[st:st + W_IN_WIN] for st in win_start])], ["w_in"])

    loss_part, dx, gr = _local_step(
        x[0], loss_target[0], ln_in_g, ln_in_b, b_in[0], b_gate_lr2[0], attn_sinks[0], gla_norm_g[0], ln1_g[0],
        ln1_b[0], ln2_g[0], ln2_b[0], token, fetch_first, fetch_rest, ship_ffn, ship_w_in)
    ffn_got = _ici_wait("scatter", sent["ffn"][1], [dx], "scatter_ffn_wait")
    w_in_got = _ici_wait("scatter", sent["w_in"][1], [dx], "scatter_w_in_wait")

    gr["loss"] = loss_part
    small = jnp.broadcast_to(_small_pack(gr), (N_CHIPS, SMALL_ROWS, D)).reshape(N_CHIPS, 2, -1, D)
    small_parts = _chip_partials([small], [F32], ["small"])
    small_got = list(_chip_scatter(small_parts, [True]))
    red = _finish_reduce(sent["w_in"][0] + sent["ffn"][0] + small_parts, w_in_got + ffn_got + small_got,
                         [False] * len(BIG) + [True], list(BIG) + ["small"])

    big_g = dict(zip(BIG, red))
    big_g["w_in"] = lax.dynamic_slice_in_dim(red[0], chip * (r_in % BF16_ROWS), r_in, axis=0)
    loss = red[-1][ROW_TAIL, TAIL_LOSS]
    grads = dict(w_in=big_g["w_in"].T[None], w_out=big_g["w_out"][None], w_ffn_gate=big_g["w_g"].T[None],
                 w_ffn_up=big_g["w_u"].T[None], w_ffn_down=big_g["w_d"][None])
    weights = dict(meta_tokens=meta_tokens, ln_in_g=ln_in_g, ln_in_b=ln_in_b, w_in=w_in, b_in=b_in,
                   w_gate_lr2=w_gate_lr2, b_gate_lr2=b_gate_lr2, attn_sinks=attn_sinks, gla_norm_g=gla_norm_g,
                   w_out=w_out, ln1_g=ln1_g, ln1_b=ln1_b, w_ffn_gate=w_ffn_gate, w_ffn_up=w_ffn_up,
                   w_ffn_down=w_ffn_down, ln2_g=ln2_g, ln2_b=ln2_b)
    m_in = dict(meta_tokens=m_meta_tokens, ln_in_g=m_ln_in_g, ln_in_b=m_ln_in_b, w_in=m_w_in, b_in=m_b_in,
                w_gate_lr2=m_w_gate_lr2, b_gate_lr2=m_b_gate_lr2, attn_sinks=m_attn_sinks, gla_norm_g=m_gla_norm_g,
                w_out=m_w_out, ln1_g=m_ln1_g, ln1_b=m_ln1_b, w_ffn_gate=m_w_ffn_gate, w_ffn_up=m_w_ffn_up,
                w_ffn_down=m_w_ffn_down, ln2_g=m_ln2_g, ln2_b=m_ln2_b)
    v_in = dict(meta_tokens=v_meta_tokens, ln_in_g=v_ln_in_g, ln_in_b=v_ln_in_b, w_in=v_w_in, b_in=v_b_in,
                w_gate_lr2=v_w_gate_lr2, b_gate_lr2=v_b_gate_lr2, attn_sinks=v_attn_sinks, gla_norm_g=v_gla_norm_g,
                w_out=v_w_out, ln1_g=v_ln1_g, ln1_b=v_ln1_b, w_ffn_gate=v_w_ffn_gate, w_ffn_up=v_w_ffn_up,
                w_ffn_down=v_w_ffn_down, ln2_g=v_ln2_g, ln2_b=v_ln2_b)
    names = list(weights)
    big_names = ("w_in", "w_out", "w_ffn_gate", "w_ffn_up", "w_ffn_down")

    delta, new_m, new_v = {}, {}, {}
    for k, kk in zip(big_names, BIG):
        flip = (lambda a: a.T) if kk in ("w_in", "w_g", "w_u") else (lambda a: a)
        d_, m_, v_ = _adamw(flip(weights[k][0]), big_g[kk], flip(m_in[k][0]), flip(v_in[k][0]), "adamw_" + k)
        delta[k], new_m[k], new_v[k] = (flip(t)[None] for t in (d_, m_, v_))
    small_in = [tuple(src[k].reshape(shape) for src in (weights, m_in, v_in)) for k, shape in SMALL]
    small_out = _adamw_small(chip.astype(jnp.int32).reshape(1), red[-1], small_in)
    for (k, _), results in zip(SMALL, small_out):
        grads[k], delta[k], new_m[k], new_v[k] = (r.reshape(weights[k].shape) for r in results)

    return (loss, dx[None], *[grads[k] for k in names], *[delta[k] for k in names], *[new_m[k] for k in names],
            *[new_v[k] for k in names])
```

```python
import functools

import jax
import jax.numpy as jnp
from jax import lax
from jax.experimental import pallas as pl
from jax.experimental.pallas import tpu as pltpu

F32 = jnp.float32
BF16 = jnp.bfloat16
MESH = pl.DeviceIdType.MESH

D = 1024
SEQ = 4096
N_META = 16
SWA_HEADS, SWA_KV_HEADS, DH = 8, 2, 64
WINDOW = 128
GLA_HEADS, DK, DV = 4, 64, 128
GLA_TAU = 16.0
CH = 64
D_FF = 2816
D_IN = 2320
LN_EPS = 1e-5
RMS_EPS = 1e-6
ALPHA = 2.0 ** 0.25
NEG = -1e30
ADAM_LR, ADAM_B1, ADAM_B2, ADAM_EPS, ADAM_WD, ADAM_STEP = 0.001, 0.9, 0.999, 1e-8, 0.01, 10
O_QS, O_KS, O_VS, O_QG, O_KG, O_VG, O_RG, O_LR = 0, 512, 640, 768, 1024, 1280, 1792, 2304

LANE = 128
BLK = WINDOW
D_IN_P = D_IN + LANE - 16
META_OFF = CH - N_META
HEAD_POS = (0, 4, 1, 5, 2, 6, 3, 7)
LN_ROWS = 512
TOKEN = (8, LANE)
N_CHIPS = 4
SHARD_ROWS = dict(w_in=D_IN // N_CHIPS, w_out=D // N_CHIPS, w_g=D_FF // N_CHIPS, w_u=D_FF // N_CHIPS,
                  w_d=D_FF // N_CHIPS)
SMALL_ROWS = 48
BF16_ROWS = 16
W_IN_WIN = -(-SHARD_ROWS["w_in"] // (2 * BF16_ROWS)) * 2 * BF16_ROWS
VMEM_CAP_MB = 64


def _lp():
    return SEQ + BLK


def _row_tile(cap):
    lp = _lp()
    return max(t for t in range(16, cap + 1, 16) if lp % t == 0)


def _params(vmem_mb, **kw):
    assert vmem_mb <= VMEM_CAP_MB - 6
    return pltpu.CompilerParams(vmem_limit_bytes=vmem_mb << 20, **kw)


def _seq(n=1):
    return ("arbitrary",) * n


def _const(shape):
    return pl.BlockSpec(shape, lambda *_: (0,) * len(shape), pipeline_mode=pl.Buffered(1))


def _acc(shape):
    return pl.BlockSpec(shape, lambda *_: (0,) * len(shape))


def _rows(tm, width):
    return pl.BlockSpec((tm, width), lambda i: (i, 0))


def _dot(a, b):
    return jnp.dot(a.astype(BF16), b.astype(BF16), preferred_element_type=F32)


def _dot_nt(a, b):
    return lax.dot_general(a.astype(BF16), b.astype(BF16), (((1,), (1,)), ((), ())), preferred_element_type=F32)


def _dot_tn(a, b):
    return lax.dot_general(a.astype(BF16), b.astype(BF16), (((0,), (0,)), ((), ())), preferred_element_type=F32)


def _dot_exact(a, b):
    return jnp.dot(a, b, precision=lax.Precision.HIGHEST, preferred_element_type=F32)


def _ln_stats(x):
    mu = jnp.mean(x, axis=-1, keepdims=True)
    xc = x - mu
    rstd = lax.rsqrt(jnp.mean(xc * xc, axis=-1, keepdims=True) + LN_EPS)
    return xc * rstd, rstd


def _ln_bwd(dy, xhat, rstd, g):
    dxh = dy * g
    return rstd * (dxh - jnp.mean(dxh, axis=-1, keepdims=True) - xhat * jnp.mean(dxh * xhat, axis=-1, keepdims=True))


def _sigmoid(x):
    return 1.0 / (1.0 + jnp.exp(-x))


def _iota(shape, dim):
    return lax.broadcasted_iota(jnp.int32, shape, dim)


def _hbm(*arrays):
    return tuple(pltpu.with_memory_space_constraint(a, pltpu.HBM) for a in arrays)


def _ln_in_fwd_real(x, g, b, token):
    tr = min(LN_ROWS, SEQ)

    def body(x_ref, g_ref, b_ref, token_ref, h_ref):
        xhat, _ = _ln_stats(x_ref[...])
        h_ref[...] = xhat * g_ref[...] + b_ref[...]

    return pl.pallas_call(
        body, name="ln_in_fwd", grid=(SEQ // tr,),
        in_specs=[_rows(tr, D), _const((1, D)), _const((1, D)), _const(TOKEN)],
        out_specs=_rows(tr, D),
        out_shape=pltpu.HBM((_lp(), D), F32),
        compiler_params=_params(32, dimension_semantics=_seq()),
    )(*_hbm(x, g, b), token)


def _ln_in_fwd_meta(h_real, meta_ext, g, b):
    def meta_body(m_ref, g_ref, b_ref, real_ref, h_ref):
        xhat, _ = _ln_stats(m_ref[...])
        h_ref[...] = xhat * g_ref[...] + b_ref[...]

    return pl.pallas_call(
        meta_body, name="ln_in_fwd_meta", grid=(1,),
        in_specs=[_const((BLK, D)), _const((1, D)), _const((1, D)), pl.BlockSpec(memory_space=pl.ANY)],
        out_specs=pl.BlockSpec((BLK, D), lambda i: (SEQ // BLK, 0)),
        out_shape=pltpu.HBM((_lp(), D), F32),
        input_output_aliases={3: 0},
        compiler_params=_params(16, dimension_semantics=_seq()),
    )(*_hbm(meta_ext, g, b, h_real))


def _in_proj(h0, w_in_t, b_in_p, wg2_p, bg2):
    tm = _row_tile(384)
    lp = _lp()
    widths = (512, 128, 128, 256, 256, 512, 512, 128)
    offs = (O_QS, O_KS, O_VS, O_QG, O_KG, O_VG, O_RG, O_LR)

    def body(h_ref, w_ref, b_ref, wg2_ref, bg2_ref, *outs):
        proj = _dot_nt(h_ref[...], w_ref[...]) + b_ref[...]
        for pos, h in enumerate(HEAD_POS):
            outs[0][:, pos * DH:(pos + 1) * DH] = proj[:, O_QS + h * DH:O_QS + (h + 1) * DH]
        for o_ref, off, wd in zip(outs[1:8], offs[1:], widths[1:]):
            o_ref[...] = proj[:, off:off + wd]
        outs[8][...] = _dot(proj[:, O_LR:O_LR + LANE], wg2_ref[...]) + bg2_ref[...]

    return pl.pallas_call(
        body, name="in_proj", grid=(lp // tm,),
        in_specs=[_rows(tm, D), _const((D_IN_P, D)), _const((1, D_IN_P)), _const((LANE, 256)), _const((1, 256))],
        out_specs=[_rows(tm, w) for w in widths] + [_rows(tm, 256)],
        out_shape=[pltpu.HBM((lp, w), F32) for w in widths] + [pltpu.HBM((lp, 256), F32)],
        compiler_params=_params(40, dimension_semantics=_seq()),
    )(*_hbm(h0, w_in_t, b_in_p, wg2_p, bg2))


def _swa_masks(n):
    nb = SEQ // BLK
    is_meta = n == nb
    ri = _iota((BLK, BLK), 0)
    cj = _iota((BLK, BLK), 1)
    meta_col = ((cj >= META_OFF) & (cj < CH)).astype(jnp.int32)
    meta_q = meta_col * ((cj <= ri) & (ri < CH)).astype(jnp.int32)
    valid_m = jnp.where(is_meta, meta_q, meta_col) > 0
    dist_m = jnp.where(is_meta, ri - cj, n * BLK + ri + CH - cj).astype(F32)
    valid_p = jnp.where((n >= 1) & (n < nb), (cj > ri).astype(jnp.int32), 0) > 0
    dist_p = (ri + BLK - cj).astype(F32)
    valid_c = jnp.where(n < nb, (cj <= ri).astype(jnp.int32), 0) > 0
    dist_c = (ri - cj).astype(F32)
    return (dist_m, dist_p, dist_c), (valid_m, valid_p, valid_c)


def _swa_bias(n):
    dists, valids = _swa_masks(n)
    return (jnp.concatenate([-d for d in dists], axis=1),
            jnp.concatenate([jnp.where(v, 0.0, NEG) for v in valids], axis=1))


def _swa_half(ref, pos, scale=1.0):
    col = ref[:, (pos // 2) * LANE:(pos // 2 + 1) * LANE]
    lane = _iota((BLK, LANE), 1)
    mine = lane < DH if pos % 2 == 0 else lane >= DH
    return jnp.where(mine, col * scale, 0.0).astype(BF16)


def _swa_merge(even, odd):
    return jnp.where(_iota((BLK, LANE), 1) < DH, even, odd)


def _swa_softmax(t, sink):
    m = jnp.maximum(jnp.max(t, axis=-1, keepdims=True), sink)
    e = jnp.exp(t - m)
    e_sink = jnp.exp(sink - m)
    inv = 1.0 / (jnp.sum(e, axis=-1, keepdims=True) + e_sink)
    return e * inv, e_sink * inv


def _swa_kv_specs(width):
    nb = SEQ // BLK
    return [pl.BlockSpec((BLK, width), lambda n: (nb, 0)),
            pl.BlockSpec((BLK, width), lambda n: (jnp.clip(n - 1, 0, nb - 1), 0)),
            pl.BlockSpec((BLK, width), lambda n: (jnp.minimum(n, nb), 0))]


def _swa_fwd(sinks, qs, ks, vs):
    nb = SEQ // BLK
    heads = range(SWA_HEADS)

    def body(sink_ref, q_ref, km_ref, kp_ref, kc_ref, vm_ref, vp_ref, vc_ref, o_ref):
        negdist, maskbias = _swa_bias(pl.program_id(0))
        k_all = jnp.concatenate([km_ref[...], kp_ref[...], kc_ref[...]], axis=0).astype(BF16)
        v_all = jnp.concatenate([vm_ref[...], vp_ref[...], vc_ref[...]], axis=0).astype(BF16)
        q = [_swa_half(q_ref, pos, DH ** -0.5) for pos in heads]
        t = [_dot_nt(q[pos], k_all) + (2.0 ** -(HEAD_POS[pos] + 1) * negdist + maskbias) for pos in heads]
        p = [_swa_softmax(t[pos], sink_ref[HEAD_POS[pos]])[0].astype(BF16) for pos in heads]
        o = [_dot(p[pos], v_all) for pos in heads]
        for col in range(SWA_HEADS // 2):
            o_ref[:, col * LANE:(col + 1) * LANE] = _swa_merge(o[2 * col], o[2 * col + 1])

    kvw = SWA_KV_HEADS * DH
    return pl.pallas_call(
        body, name="swa_fwd", grid=(nb + 1,),
        in_specs=[pl.BlockSpec(memory_space=pltpu.SMEM), _rows(BLK, SWA_HEADS * DH)] + _swa_kv_specs(kvw) + _swa_kv_specs(kvw),
        out_specs=_rows(BLK, SWA_HEADS * DH),
        out_shape=pltpu.HBM((_lp(), SWA_HEADS * DH), F32),
        compiler_params=_params(16, dimension_semantics=_seq()),
    )(sinks, *_hbm(qs, ks, ks, ks, vs, vs, vs))


GLA_PER_STEP = BLK // CH


def _gla_block(s):
    nb = SEQ // BLK
    return jnp.where(s == 0, nb, s - 1)


def _gla_rowmask(s):
    ri = _iota((BLK, 1), 0)
    m = jnp.where(s == 0, ((ri >= META_OFF) & (ri < CH)).astype(jnp.int32), 1)
    return (m > 0).astype(F32) + jnp.zeros((BLK, 1), F32)


def _gla_chunk_masks():
    r, c = _iota((BLK, BLK), 0), _iota((BLK, BLK), 1)
    same = ((r < CH) & (c < CH)) | ((r >= CH) & (c >= CH))
    return same & (r >= c), same & (r <= c), same


def _gla_decay(z, rmask):
    log_g = (jnp.minimum(z, 0.0) - jnp.log1p(jnp.exp(-jnp.abs(z)))) * (rmask / GLA_TAU)
    lower, _, same = _gla_chunk_masks()
    return _dot_exact(lower.astype(F32), log_g), _dot_exact(same.astype(F32), log_g)


def _gla_slices(c, h):
    return slice(c * CH, (c + 1) * CH), slice(h * DK, (h + 1) * DK), slice(h * DV, (h + 1) * DV)


def _gla_fwd(qg, kg, vg, z):
    steps = SEQ // BLK + 1
    kw, vw = GLA_HEADS * DK, GLA_HEADS * DV
    pairs = [(c, h) for c in range(GLA_PER_STEP) for h in range(GLA_HEADS)]

    def body(q_ref, k_ref, v_ref, z_ref, o_ref, st_ref, st):
        s = pl.program_id(0)

        @pl.when(s == 0)
        def _():
            st[...] = jnp.zeros_like(st)

        rmask = _gla_rowmask(s)
        b, b_last = _gla_decay(z_ref[...], rmask)
        q = q_ref[...] * (rmask * DK ** -0.5)
        k = k_ref[...] * rmask
        v = v_ref[...] * rmask
        qe = q * jnp.exp(b)
        ke = k * jnp.exp(-b)
        kd = k * jnp.exp(b_last - b)
        e_last = jnp.exp(b_last)
        causal = _iota((CH, CH), 0) >= _iota((CH, CH), 1)
        a, upd, intra = {}, {}, {}
        for c, h in pairs:
            rows, ks, vs_ = _gla_slices(c, h)
            a[c, h] = jnp.where(causal, _dot_nt(qe[rows, ks], ke[rows, ks]), 0.0)
            upd[c, h] = _dot_tn(v[rows, vs_], kd[rows, ks])
        for c, h in pairs:
            rows, ks, vs_ = _gla_slices(c, h)
            intra[c, h] = _dot(a[c, h], v[rows, vs_])
        state = st[...]
        for c in range(GLA_PER_STEP):
            st_ref[0, c] = state
            for h in range(GLA_HEADS):
                rows, ks, vs_ = _gla_slices(c, h)
                o_ref[rows, vs_] = intra[c, h] + _dot_nt(qe[rows, ks], state[:, ks])
            state = state * e_last[c * CH:c * CH + 1] + jnp.concatenate([upd[c, h] for h in range(GLA_HEADS)], axis=1)
        st[...] = state

    blk = lambda w: pl.BlockSpec((BLK, w), lambda s: (_gla_block(s), 0))
    return pl.pallas_call(
        body, name="gla_fwd", grid=(steps,),
        in_specs=[blk(kw), blk(kw), blk(vw), blk(kw)],
        out_specs=[blk(vw), pl.BlockSpec((1, GLA_PER_STEP, DV, kw), lambda s: (s, 0, 0, 0))],
        out_shape=[pltpu.HBM((_lp(), vw), F32), pltpu.HBM((steps, GLA_PER_STEP, DV, kw), F32)],
        scratch_shapes=[pltpu.VMEM((DV, kw), F32)],
        compiler_params=_params(16, dimension_semantics=_seq()),
    )(*_hbm(qg, kg, vg, z))


def _post_mix(o_s, o_gla, r_g, h0, gn4, w_out, g1, b1):
    tm = _row_tile(384)
    lp = _lp()

    def body(os_ref, og_ref, r_ref, h0_ref, gn_ref, w_ref, g_ref, b_ref, o_ref, pre_ref, h1_ref):
        for pos, h in enumerate(HEAD_POS):
            o_ref[:, h * DH:(h + 1) * DH] = os_ref[:, pos * DH:(pos + 1) * DH].astype(BF16)
        for h in range(GLA_HEADS):
            hs = slice(h * DV, (h + 1) * DV)
            xg = og_ref[:, hs]
            n = xg * lax.rsqrt(jnp.mean(xg * xg, axis=-1, keepdims=True) + RMS_EPS) * gn_ref[...]
            r = r_ref[:, hs]
            o_ref[:, 512 + h * DV:512 + (h + 1) * DV] = (n * (r * _sigmoid(r))).astype(BF16)
        pre = ALPHA * h0_ref[...] + _dot(o_ref[...], w_ref[...])
        pre_ref[...] = pre
        xhat, _ = _ln_stats(pre)
        h1_ref[...] = xhat * g_ref[...] + b_ref[...]

    return pl.pallas_call(
        body, name="post_mix", grid=(lp // tm,),
        in_specs=[_rows(tm, 512), _rows(tm, 512), _rows(tm, 512), _rows(tm, D), _const((1, DV)), _const((D, D)),
                  _const((1, D)), _const((1, D))],
        out_specs=[_rows(tm, D), _rows(tm, D), _rows(tm, D)],
        out_shape=[pltpu.HBM((lp, D), BF16), pltpu.HBM((lp, D), F32),
                   pltpu.HBM((lp, D), F32)],
        compiler_params=_params(32, dimension_semantics=_seq()),
    )(*_hbm(o_s, o_gla, r_g, h0, gn4, w_out, g1, b1))


def _ffn_fwd_loss_bwd(h1, wg_t, wu_t, wd, target, g2, b2):
    lp = _lp()
    tm = max(t for t in range(BLK, 384 + 1, BLK) if lp % t == 0)
    steps = lp // tm
    last_blk = SEQ // BLK - 1
    half = D_FF // 2
    n_t = tm // BLK

    def body(*refs):
        h_ref, wg_ref, wu_ref, wd_ref = refs[:4]
        t_refs = refs[4:4 + n_t]
        g2_ref, b2_ref, a_ref, dgate_ref, dup_ref, dp_ref, loss_ref, dg_ref, db_ref, g_s, u_s, acc = refs[4 + n_t:]
        i = pl.program_id(0)

        @pl.when(i == 0)
        def _():
            acc[...] = jnp.zeros_like(acc)
            dg_ref[...] = jnp.zeros_like(dg_ref)
            db_ref[...] = jnp.zeros_like(db_ref)

        h = h_ref[...]
        hb = h.astype(BF16)
        pre = ALPHA * h
        for j in range(2):
            cols = slice(j * half, (j + 1) * half)
            g = _dot_nt(hb, wg_ref[cols, :])
            u = _dot_nt(hb, wu_ref[cols, :])
            g_s[:, cols] = g
            u_s[:, cols] = u
            pre = pre + _dot(g * _sigmoid(g) * u, wd_ref[cols, :])
        xhat, rstd = _ln_stats(pre)
        real = i * tm + _iota((tm, 1), 0) < SEQ
        target_rows = jnp.concatenate([t[...] for t in t_refs], axis=0)
        diff = jnp.where(real, xhat * g2_ref[...] + b2_ref[...] - target_rows, 0.0)
        acc[...] += jnp.sum(diff * diff, axis=0, keepdims=True)
        dy = diff * (1.0 / D)
        dpre = _ln_bwd(dy, xhat, rstd, g2_ref[...])
        dp_ref[...] = dpre
        dg_ref[...] += jnp.sum(dy * xhat, axis=0, keepdims=True)
        db_ref[...] += jnp.sum(dy, axis=0, keepdims=True)
        dpb = dpre.astype(BF16)
        for j in range(2):
            cols = slice(j * half, (j + 1) * half)
            g, u = g_s[:, cols], u_s[:, cols]
            sg = _sigmoid(g)
            silu = g * sg
            da = _dot_nt(dpb, wd_ref[cols, :])
            a_ref[:, cols] = (silu * u).astype(BF16)
            dgate_ref[:, cols] = (da * u * (sg * (1.0 + g * (1.0 - sg)))).astype(BF16)
            dup_ref[:, cols] = (da * silu).astype(BF16)

        @pl.when(i == steps - 1)
        def _():
            loss_ref[...] = jnp.zeros_like(loss_ref) + (0.5 / D) * jnp.sum(acc[...], axis=1, keepdims=True)

    t_spec = lambda k: pl.BlockSpec((BLK, D), lambda i: (jnp.minimum(i * n_t + k, last_blk), 0))
    return pl.pallas_call(
        body, name="ffn_fwd_loss_bwd", grid=(steps,),
        in_specs=[_rows(tm, D), _const((D_FF, D)), _const((D_FF, D)), _const((D_FF, D))]
        + [t_spec(k) for k in range(n_t)] + [_const((1, D)), _const((1, D))],
        out_specs=[_rows(tm, D_FF), _rows(tm, D_FF), _rows(tm, D_FF), _rows(tm, D), _acc((1, LANE)), _acc((1, D)),
                   _acc((1, D))],
        out_shape=[pltpu.HBM((lp, D_FF), BF16)] * 3 + [pltpu.HBM((lp, D), F32), pltpu.HBM((1, LANE), F32),
                                                         pltpu.HBM((1, D), F32), pltpu.HBM((1, D), F32)],
        scratch_shapes=[pltpu.VMEM((tm, D_FF), F32), pltpu.VMEM((tm, D_FF), F32), pltpu.VMEM((1, D), F32)],
        compiler_params=_params(58, dimension_semantics=_seq()),
    )(*_hbm(h1, wg_t, wu_t, wd, *[target] * n_t, g2, b2))


def _ffn_out_bwd(dpre2, dgate, dup, pre1, wg_t, wu_t, g1, w_out, o_gla, r_g, gn4):
    tm = _row_tile(384)
    lp = _lp()

    def body(dp_ref, dg_ref, du_ref, p1_ref, wg_ref, wu_ref, g1_ref, w_ref, og_ref, r_ref, gn_ref,
             dp1_ref, dg1_ref, db1_ref, dos_ref, dog_ref, dr_ref, dgn_ref):
        @pl.when(pl.program_id(0) == 0)
        def _():
            for acc_ref in (dg1_ref, db1_ref, dgn_ref):
                acc_ref[...] = jnp.zeros_like(acc_ref)

        dh1 = ALPHA * dp_ref[...] + _dot(dg_ref[...], wg_ref[...]) + _dot(du_ref[...], wu_ref[...])
        xhat, rstd1 = _ln_stats(p1_ref[...])
        dpre1 = _ln_bwd(dh1, xhat, rstd1, g1_ref[...])
        dp1_ref[...] = dpre1
        dg1_ref[...] += jnp.sum(dh1 * xhat, axis=0, keepdims=True)
        db1_ref[...] += jnp.sum(dh1, axis=0, keepdims=True)

        do = _dot_nt(dpre1, w_ref[...])
        for pos, h in enumerate(HEAD_POS):
            dos_ref[:, pos * DH:(pos + 1) * DH] = do[:, h * DH:(h + 1) * DH]
        gn = gn_ref[...]
        for h in range(GLA_HEADS):
            hs = slice(h * DV, (h + 1) * DV)
            xg = og_ref[:, hs]
            rstd = lax.rsqrt(jnp.mean(xg * xg, axis=-1, keepdims=True) + RMS_EPS)
            nx = xg * rstd
            r = r_ref[:, hs]
            sr = _sigmoid(r)
            d_o = do[:, 512 + h * DV:512 + (h + 1) * DV]
            dr_ref[:, hs] = d_o * (nx * gn) * (sr * (1.0 + r * (1.0 - sr)))
            dn = d_o * (r * sr)
            dgn_ref[...] += jnp.sum(dn * nx, axis=0, keepdims=True)
            dnx = dn * gn
            dog_ref[:, hs] = rstd * (dnx - nx * jnp.mean(dnx * nx, axis=-1, keepdims=True))

    return pl.pallas_call(
        body, name="ffn_out_bwd", grid=(lp // tm,),
        in_specs=[_rows(tm, D), _rows(tm, D_FF), _rows(tm, D_FF), _rows(tm, D), _const((D_FF, D)), _const((D_FF, D)),
                  _const((1, D)), _const((D, D)), _rows(tm, 512), _rows(tm, 512), _const((1, DV))],
        out_specs=[_rows(tm, D), _acc((1, D)), _acc((1, D)), _rows(tm, 512), _rows(tm, 512), _rows(tm, 512),
                   _acc((1, DV))],
        out_shape=[pltpu.HBM((lp, D), F32), pltpu.HBM((1, D), F32), pltpu.HBM((1, D), F32)]
        + [pltpu.HBM((lp, 512), F32)] * 3 + [pltpu.HBM((1, DV), F32)],
        compiler_params=_params(48, dimension_semantics=_seq()),
    )(*_hbm(dpre2, dgate, dup, pre1, wg_t, wu_t, g1, w_out, o_gla, r_g, gn4))


def _atb(a, b, name, token=None):
    lp = _lp()
    tm = _row_tile(1408)
    n, w = a.shape[1], b.shape[1]
    bw = 512 if n * w * 4 > (4 << 20) else w
    tokens = [] if token is None else [token]

    def body(a_ref, b_ref, *rest):
        o_ref = rest[-1]

        @pl.when(pl.program_id(1) == 0)
        def _():
            o_ref[...] = jnp.zeros_like(o_ref)

        o_ref[...] += _dot_tn(a_ref[...], b_ref[...])

    return pl.pallas_call(
        body, name=name, grid=(w // bw, lp // tm),
        in_specs=[pl.BlockSpec((tm, n), lambda j, k: (k, 0)), pl.BlockSpec((tm, bw), lambda j, k: (k, j))]
        + [_const(TOKEN)] * len(tokens),
        out_specs=pl.BlockSpec((n, bw), lambda j, k: (0, j)),
        out_shape=pltpu.HBM((n, w), F32),
        compiler_params=_params(48, dimension_semantics=_seq(2)),
    )(*_hbm(a, b), *tokens)


def _gla_bwd(qg, kg, vg, z, do_gla, st_all, token):
    steps = SEQ // BLK + 1
    kw, vw = GLA_HEADS * DK, GLA_HEADS * DV
    pairs = [(c, h) for c in range(GLA_PER_STEP) for h in range(GLA_HEADS)]
    heads = range(GLA_HEADS)

    def body(q_ref, k_ref, v_ref, z_ref, do_ref, st_ref, token_ref, dq_ref, dk_ref, dv_ref, dz_ref, dst):
        @pl.when(pl.program_id(0) == 0)
        def _():
            dst[...] = jnp.zeros_like(dst)

        rmask = _gla_rowmask(steps - 1 - pl.program_id(0))
        zz = z_ref[...]
        b, b_last = _gla_decay(zz, rmask)
        e_b, e_nb, e_kd, e_last = jnp.exp(b), jnp.exp(-b), jnp.exp(b_last - b), jnp.exp(b_last)
        q = q_ref[...] * (rmask * DK ** -0.5)
        k = k_ref[...] * rmask
        v = v_ref[...] * rmask
        qe, ke, kd = q * e_b, k * e_nb, k * e_kd
        d_o = do_ref[...]
        causal = _iota((CH, CH), 0) >= _iota((CH, CH), 1)
        a, da, dqe, dke, dv_intra, carry = {}, {}, {}, {}, {}, {}
        for c, h in pairs:
            rows, ks, vs_ = _gla_slices(c, h)
            a[c, h] = jnp.where(causal, _dot_nt(qe[rows, ks], ke[rows, ks]), 0.0)
            da[c, h] = jnp.where(causal, _dot_nt(d_o[rows, vs_], v[rows, vs_]), 0.0)
            carry[c, h] = _dot_tn(d_o[rows, vs_], qe[rows, ks])
        for c, h in pairs:
            rows, ks, vs_ = _gla_slices(c, h)
            dqe[c, h] = _dot(d_o[rows, vs_], st_ref[0, c][:, ks]) + _dot(da[c, h], ke[rows, ks])
            dke[c, h] = _dot_tn(da[c, h], qe[rows, ks])
            dv_intra[c, h] = _dot_tn(a[c, h], d_o[rows, vs_])
        dstate = dst[...]
        dkd, db_decay = {}, {}
        for c in reversed(range(GLA_PER_STEP)):
            for h in heads:
                rows, ks, vs_ = _gla_slices(c, h)
                dkd[c, h] = _dot(v[rows, vs_], dstate[:, ks])
                dv_ref[rows, vs_] = dv_intra[c, h] + _dot_nt(kd[rows, ks], dstate[:, ks])
            chunk_last = e_last[c * CH:c * CH + 1]
            db_decay[c] = jnp.sum(dstate * st_ref[0, c], axis=0, keepdims=True) * chunk_last
            dstate = dstate * chunk_last + jnp.concatenate([carry[c, h] for h in heads], axis=1)
        dst[...] = dstate
        rows_of = lambda parts: jnp.concatenate(
            [jnp.concatenate([parts[c, h] for h in heads], axis=1) for c in range(GLA_PER_STEP)], axis=0)
        dqe_all, dke_all, dkd_all = rows_of(dqe), rows_of(dke), rows_of(dkd)
        dq_ref[...] = dqe_all * e_b * (rmask * DK ** -0.5)
        dk_ref[...] = (dke_all * e_nb + dkd_all * e_kd) * rmask
        dkd_kd = dkd_all * kd
        db = dqe_all * qe - dke_all * ke - dkd_kd
        _, upper, same = _gla_chunk_masks()
        decay_rows = jnp.concatenate([jnp.broadcast_to(db_decay[c], (CH, kw)) for c in range(GLA_PER_STEP)], axis=0)
        dlog_g = _dot_exact(upper.astype(F32), db) + _dot_exact(same.astype(F32), dkd_kd) + decay_rows
        dz_ref[...] = dlog_g * (rmask / GLA_TAU) * _sigmoid(-zz)

    blk = lambda w: pl.BlockSpec((BLK, w), lambda s: (_gla_block(steps - 1 - s), 0))
    return pl.pallas_call(
        body, name="gla_bwd", grid=(steps,),
        in_specs=[blk(kw), blk(kw), blk(vw), blk(kw), blk(vw),
                  pl.BlockSpec((1, GLA_PER_STEP, DV, kw), lambda s: (steps - 1 - s, 0, 0, 0)), _const(TOKEN)],
        out_specs=[blk(kw), blk(kw), blk(vw), blk(kw)],
        out_shape=[pltpu.HBM((_lp(), kw), F32), pltpu.HBM((_lp(), kw), F32),
                   pltpu.HBM((_lp(), vw), F32), pltpu.HBM((_lp(), kw), F32)],
        scratch_shapes=[pltpu.VMEM((DV, kw), F32)],
        compiler_params=_params(16, dimension_semantics=_seq()),
    )(*_hbm(qg, kg, vg, z, do_gla, st_all), token)


def _swa_bwd(sinks, qs, ks, vs, do_s, token):
    nb = SEQ // BLK
    kvw = SWA_KV_HEADS * DH
    scale = DH ** -0.5
    heads = range(SWA_HEADS)

    def body(sink_ref, q_ref, km_ref, kp_ref, kc_ref, vm_ref, vp_ref, vc_ref, do_ref, token_ref,
             dq_ref, dk_ref, dv_ref, dsink_ref, carry_k, carry_v, meta_k, meta_v):
        n = pl.program_id(0)

        @pl.when(n == 0)
        def _():
            for r in (carry_k, carry_v, meta_k, meta_v):
                r[...] = jnp.zeros_like(r)
            dsink_ref[...] = jnp.zeros_like(dsink_ref)

        @pl.when(n <= nb)
        def _():
            negdist, maskbias = _swa_bias(n)
            lane = _iota((1, LANE), 1)
            k_all = jnp.concatenate([km_ref[...], kp_ref[...], kc_ref[...]], axis=0).astype(BF16)
            v_all = jnp.concatenate([vm_ref[...], vp_ref[...], vc_ref[...]], axis=0).astype(BF16)
            q = [_swa_half(q_ref, pos, scale) for pos in heads]
            d_o = [_swa_half(do_ref, pos) for pos in heads]
            t = [_dot_nt(q[pos], k_all) + (2.0 ** -(HEAD_POS[pos] + 1) * negdist + maskbias) for pos in heads]
            dp = [_dot_nt(d_o[pos], v_all) for pos in heads]
            soft = [_swa_softmax(t[pos], sink_ref[HEAD_POS[pos]]) for pos in heads]
            p = [s[0] for s in soft]
            delta = [jnp.sum(p[pos] * dp[pos], axis=-1, keepdims=True) for pos in heads]
            ds = [(p[pos] * (dp[pos] - delta[pos])).astype(BF16) for pos in heads]
            dq = [_dot(ds[pos], k_all) for pos in heads]
            for col in range(SWA_HEADS // 2):
                dq_ref[:, col * LANE:(col + 1) * LANE] = scale * _swa_merge(dq[2 * col], dq[2 * col + 1])
            dsink = jnp.zeros((1, LANE), F32)
            for pos in heads:
                dsink = dsink + jnp.where(lane == HEAD_POS[pos],
                                          -jnp.sum(soft[pos][1] * delta[pos], axis=0, keepdims=True), 0.0)
            dsink_ref[...] += dsink
            dk3 = _dot_tn(jnp.concatenate(q, axis=0), jnp.concatenate(ds, axis=0)).T
            dv3 = _dot_tn(jnp.concatenate(d_o, axis=0), jnp.concatenate([x.astype(BF16) for x in p], axis=0)).T
            meta_k[...] += dk3[0:BLK]
            meta_v[...] += dv3[0:BLK]
            dk_ref[...] = carry_k[...] + dk3[BLK:2 * BLK]
            dv_ref[...] = carry_v[...] + dv3[BLK:2 * BLK]
            carry_k[...] = dk3[2 * BLK:3 * BLK]
            carry_v[...] = dv3[2 * BLK:3 * BLK]

        @pl.when(n == nb + 1)
        def _():
            dk_ref[...] = meta_k[...]
            dv_ref[...] = meta_v[...]

    kv_out = pl.BlockSpec((BLK, kvw), lambda n: (jnp.where(n == nb + 1, nb, jnp.clip(n - 1, 0, nb - 1)), 0))
    qblk = pl.BlockSpec((BLK, SWA_HEADS * DH), lambda n: (jnp.minimum(n, nb), 0))
    return pl.pallas_call(
        body, name="swa_bwd", grid=(nb + 2,),
        in_specs=[pl.BlockSpec(memory_space=pltpu.SMEM), qblk] + _swa_kv_specs(kvw) + _swa_kv_specs(kvw)
        + [qblk, _const(TOKEN)],
        out_specs=[qblk, kv_out, kv_out, _acc((1, LANE))],
        out_shape=[pltpu.HBM((_lp(), SWA_HEADS * DH), F32), pltpu.HBM((_lp(), kvw), F32),
                   pltpu.HBM((_lp(), kvw), F32), pltpu.HBM((1, LANE), F32)],
        scratch_shapes=[pltpu.VMEM((BLK, kvw), F32)] * 4,
        compiler_params=_params(16, dimension_semantics=_seq()),
    )(sinks, *_hbm(qs, ks, ks, ks, vs, vs, vs, do_s), token)


def _in_bwd(dqs, dks, dvs, dqg, dkg, dvg, drg, dz, dpre1, w_in_t, wg2_p):
    tm = _row_tile(384)
    lp = _lp()
    widths = (512, 128, 128, 256, 256, 512, 512)
    offs = (O_QS, O_KS, O_VS, O_QG, O_KG, O_VG, O_RG)

    def body(*refs):
        parts, (dz_ref, dp1_ref, w_ref, wg2_ref, dproj_ref, dh0_ref, dbin_ref, dbg_ref) = refs[:7], refs[7:]

        @pl.when(pl.program_id(0) == 0)
        def _():
            dbin_ref[...] = jnp.zeros_like(dbin_ref)
            dbg_ref[...] = jnp.zeros_like(dbg_ref)

        for pos, h in enumerate(HEAD_POS):
            val = parts[0][:, pos * DH:(pos + 1) * DH]
            dproj_ref[:, O_QS + h * DH:O_QS + (h + 1) * DH] = val.astype(BF16)
            dbin_ref[:, O_QS + h * DH:O_QS + (h + 1) * DH] += jnp.sum(val, axis=0, keepdims=True)
        for p_ref, off, wd in zip(parts[1:], offs[1:], widths[1:]):
            val = p_ref[...]
            dproj_ref[:, off:off + wd] = val.astype(BF16)
            dbin_ref[:, off:off + wd] += jnp.sum(val, axis=0, keepdims=True)
        dz = dz_ref[...]
        dlr = _dot_nt(dz, wg2_ref[...])
        dproj_ref[:, O_LR:O_LR + LANE] = dlr.astype(BF16)
        dbin_ref[:, O_LR:O_LR + LANE] += jnp.sum(dlr, axis=0, keepdims=True)
        dbg_ref[...] += jnp.sum(dz, axis=0, keepdims=True)
        dh0_ref[...] = ALPHA * dp1_ref[...] + _dot(dproj_ref[...], w_ref[...])

    return pl.pallas_call(
        body, name="in_bwd", grid=(lp // tm,),
        in_specs=[_rows(tm, w) for w in widths] + [_rows(tm, 256), _rows(tm, D), _const((D_IN_P, D)), _const((LANE, 256))],
        out_specs=[_rows(tm, D_IN_P), _rows(tm, D), _acc((1, D_IN_P)), _acc((1, 256))],
        out_shape=[pltpu.HBM((lp, D_IN_P), BF16), pltpu.HBM((lp, D), F32),
                   pltpu.HBM((1, D_IN_P), F32), pltpu.HBM((1, 256), F32)],
        compiler_params=_params(40, dimension_semantics=_seq()),
    )(*_hbm(dqs, dks, dvs, dqg, dkg, dvg, drg, dz, dpre1, w_in_t, wg2_p))


def _ln_in_bwd(x, meta_ext, dh0, g, token):
    tr = min(LN_ROWS, SEQ)

    def ln_bwd(x_ref, dh_ref, g_ref, dx_ref, dg_ref, db_ref):
        @pl.when(pl.program_id(0) == 0)
        def _():
            dg_ref[...] = jnp.zeros_like(dg_ref)
            db_ref[...] = jnp.zeros_like(db_ref)

        xhat, rstd = _ln_stats(x_ref[...])
        dh = dh_ref[...]
        dx_ref[...] = _ln_bwd(dh, xhat, rstd, g_ref[...])
        dg_ref[...] += jnp.sum(dh * xhat, axis=0, keepdims=True)
        db_ref[...] += jnp.sum(dh, axis=0, keepdims=True)

    def body(x_ref, dh_ref, g_ref, token_ref, dx_ref, dg_ref, db_ref):
        ln_bwd(x_ref, dh_ref, g_ref, dx_ref, dg_ref, db_ref)

    def meta_body(m_ref, dh_ref, g_ref, dm_ref, dg_ref, db_ref):
        ln_bwd(m_ref, dh_ref, g_ref, dm_ref, dg_ref, db_ref)

    sums = [pltpu.HBM((1, D), F32), pltpu.HBM((1, D), F32)]
    dx, dg, db = pl.pallas_call(
        body, name="ln_in_bwd", grid=(SEQ // tr,),
        in_specs=[_rows(tr, D), _rows(tr, D), _const((1, D)), _const(TOKEN)],
        out_specs=[_rows(tr, D), _acc((1, D)), _acc((1, D))],
        out_shape=[pltpu.HBM((SEQ, D), F32)] + sums,
        compiler_params=_params(32, dimension_semantics=_seq()),
    )(*_hbm(x, dh0, g), token)
    dm, dg_m, db_m = pl.pallas_call(
        meta_body, name="ln_in_bwd_meta", grid=(1,),
        in_specs=[_const((BLK, D)), pl.BlockSpec((BLK, D), lambda i: (SEQ // BLK, 0)), _const((1, D))],
        out_specs=[_acc((BLK, D)), _acc((1, D)), _acc((1, D))],
        out_shape=[pltpu.HBM((BLK, D), F32)] + sums,
        compiler_params=_params(16, dimension_semantics=_seq()),
    )(*_hbm(meta_ext, dh0, g))
    return dx, dm, dg + dg_m, db + db_m


def _local_step(x, target, ln_in_g, ln_in_b, b_in, bg2, sinks, gn, g1, b1, g2, b2,
                token, fetch_first, fetch_rest, exchange_ffn, ship_ffn, ship_last):
    row = lambda v: v.reshape(1, -1).astype(F32)
    b_in_p = jnp.pad(row(b_in), ((0, 0), (0, D_IN_P - D_IN)))
    gn4 = row(gn)
    sinks = sinks.reshape(-1).astype(F32)

    h_real = _ln_in_fwd_real(x, row(ln_in_g), row(ln_in_b), token)
    w_in_t, meta_full, wg2 = fetch_first([h_real])
    meta_ext = jnp.pad(meta_full, ((META_OFF, BLK - CH), (0, 0)))
    wg2_p = jnp.pad(wg2, ((0, LANE - wg2.shape[0]), (0, 0))).astype(BF16)
    h0 = _ln_in_fwd_meta(h_real, meta_ext, row(ln_in_g), row(ln_in_b))
    qs, ks, vs, qg, kg, vg, rg, glr, z = _in_proj(h0, w_in_t, b_in_p, wg2_p, row(bg2))
    o_s = _swa_fwd(sinks, qs, ks, vs)
    o_gla, st_all = _gla_fwd(qg, kg, vg, z)
    w_out, wg_t, wu_t, wd = fetch_rest([o_s, o_gla])
    o, pre1, h1 = _post_mix(o_s, o_gla, rg, h0, gn4, w_out, row(g1), row(b1))
    a, dgate, dup, dpre2, loss, dg2, db2 = _ffn_fwd_loss_bwd(h1, wg_t, wu_t, wd, target, row(g2), row(b2))
    dpre1, dg1, db1, do_s, do_gla, drg, dgn = _ffn_out_bwd(dpre2, dgate, dup, pre1, wg_t, wu_t, row(g1), w_out, o_gla,
                                                           rg, gn4)
    dwd = _atb(a, dpre2, "dw_down")
    dwg_t = _atb(dgate, h1, "dw_gate")
    dwu_t = _atb(dup, h1, "dw_up")
    token = exchange_ffn(dict(w_g=dwg_t, w_u=dwu_t, w_d=dwd))
    dw_out = _atb(o, dpre1, "dw_out", token)
    token = ship_ffn([dw_out])
    dqg, dkg, dvg, dz = _gla_bwd(qg, kg, vg, z, do_gla, st_all, token)
    dqs, dks, dvs, dsinks = _swa_bwd(sinks, qs, ks, vs, do_s, token)
    dproj, dh0, db_in_p, dbg2 = _in_bwd(dqs, dks, dvs, dqg, dkg, dvg, drg, dz, dpre1, w_in_t, wg2_p)
    token = ship_last(_atb(dproj, h0, "dw_in"), dw_out)
    dwg2_p = _atb(glr, dz, "dw_gate_lr2")
    dx, dmeta_blk, dg_in, db_in_ln = _ln_in_bwd(x, meta_ext, dh0, row(ln_in_g), token)

    grads = dict(
        meta=dmeta_blk[META_OFF:CH], ln_in_g=dg_in, ln_in_b=db_in_ln, ln1_g=dg1, ln1_b=db1, ln2_g=dg2, ln2_b=db2,
        b_in=db_in_p[:, :D_IN], wg2=dwg2_p[:wg2.shape[0]], bg2=dbg2, sinks=dsinks[:, :SWA_HEADS], gn=dgn)
    return loss[0, 0], dx, grads


HBM = pl.BlockSpec(memory_space=pltpu.HBM)


def _place():
    return lax.axis_index("x"), lax.axis_index("y"), lax.axis_index("c")


def _other_chips(x, y):
    return [(1 - x, y), (x, 1 - y), (1 - x, 1 - y)]


def _dma_sems(n):
    return pltpu.SemaphoreType.DMA((n,))


def _comm_params():
    return pltpu.CompilerParams(has_side_effects=True)


SEM = pl.BlockSpec(memory_space=pltpu.SEMAPHORE)


PER_ARRAY = dict(gather=3, scatter=3, sibling=N_CHIPS)


def _ici_copies(kind, landing, srcs, lands, send_sems, recv_sems):
    x, y, c = _place()
    mine = 2 * x + y
    copies = []
    for a in range(len(srcs)):
        if kind == "sibling":
            for s in range(N_CHIPS):
                copies.append(pltpu.make_async_remote_copy(
                    srcs[a].at[s, 1 - c], lands[a].at[s], send_sems.at[N_CHIPS * a + s], recv_sems.at[N_CHIPS * a + s],
                    device_id=(x, y, 1 - c), device_id_type=MESH))
            continue
        for j, (px, py) in enumerate(_other_chips(x, y)):
            slab = 2 * px + py if landing else mine
            if kind == "gather":
                src, dst = srcs[a].at[c], lands[a].at[slab, c]
            else:
                src, dst = srcs[a].at[2 * px + py], lands[a].at[slab]
            copies.append(pltpu.make_async_remote_copy(src, dst, send_sems.at[3 * a + j], recv_sems.at[3 * a + j],
                                                       device_id=(px, py, c), device_id_type=MESH))
    return copies


def _split_params():
    return pltpu.CompilerParams(has_side_effects=pltpu.SideEffectType.DATAFLOW_SIDE_EFFECTING)


def _ici_start(kind, srcs, land_shapes, after, name):
    n = len(srcs)
    lands = [pltpu.with_memory_space_constraint(lax.empty(s, a.dtype), pltpu.HBM) for s, a in zip(land_shapes, srcs)]

    def body(*refs):
        outs = refs[2 * n + len(after):]
        for cp in _ici_copies(kind, False, refs[:n], refs[n:2 * n], outs[0], outs[1]):
            cp.start()
        outs[-1][...] = jnp.zeros(TOKEN, F32)

    outs = pl.pallas_call(
        body, name=name, in_specs=[HBM] * (2 * n) + [pl.BlockSpec(memory_space=pl.ANY)] * len(after),
        out_specs=[SEM, SEM] + [HBM] * (2 * n) + [pl.BlockSpec(memory_space=pltpu.VMEM)],
        out_shape=[_dma_sems(PER_ARRAY[kind] * n)] * 2 + [pltpu.HBM(a.shape, a.dtype) for a in list(srcs) + lands]
        + [jax.ShapeDtypeStruct(TOKEN, F32)],
        input_output_aliases={i: 2 + i for i in range(2 * n)},
        compiler_params=_split_params(),
    )(*_hbm(*srcs), *lands, *after)
    return outs[:-1], outs[-1]


def _ici_wait(kind, handle, after, name):
    n = (len(handle) - 2) // 2

    def body(*refs):
        for cp in _ici_copies(kind, True, refs[:n], refs[n:2 * n], refs[2 * n], refs[2 * n + 1]):
            cp.wait_send()
            cp.wait_recv()

    outs = pl.pallas_call(
        body, name=name, in_specs=[HBM] * (2 * n) + [SEM, SEM] + [pl.BlockSpec(memory_space=pl.ANY)] * len(after),
        out_specs=[HBM] * (2 * n), out_shape=[pltpu.HBM(a.shape, a.dtype) for a in handle[2:]],
        input_output_aliases={i: i for i in range(2 * n)},
        compiler_params=_split_params(),
    )(*handle[2:], handle[0], handle[1], *after)
    return list(outs[n:])


def _sibling_forward(lands, name):
    n = len(lands)

    def body(*refs):
        outs = refs[n:2 * n]
        send_sems, recv_sems = refs[2 * n:]
        x, y, c = _place()

        def copy(a, j, half):
            px, py = _other_chips(x, y)[j]
            blk = outs[a].at[2 * px + py, half]
            return pltpu.make_async_remote_copy(blk, blk, send_sems.at[3 * a + j], recv_sems.at[3 * a + j],
                                                device_id=(x, y, 1 - c), device_id_type=MESH)

        pairs = [(a, j) for a in range(n) for j in range(3)]
        for a, j in pairs:
            copy(a, j, c).start()
        for a, j in pairs:
            copy(a, j, 1 - c).wait_recv()
        for a, j in pairs:
            copy(a, j, c).wait_send()

    return pl.pallas_call(
        body, name=name, in_specs=[HBM] * n, out_specs=[HBM] * n,
        out_shape=[pltpu.HBM(a.shape, a.dtype) for a in lands],
        input_output_aliases={a: a for a in range(n)},
        scratch_shapes=[_dma_sems(3 * n)] * 2,
        compiler_params=_comm_params(),
    )(*_hbm(*lands))


def _sibling_exchange(grads, name):
    n = len(grads)

    def body(*refs):
        ins, outs = refs[:n], refs[n:2 * n]
        send_sems, recv_sems = refs[2 * n:]
        x, y, c = _place()
        copies = []
        for a in range(n):
            for s in range(N_CHIPS):
                cp = pltpu.make_async_remote_copy(ins[a].at[s, 1 - c], outs[a].at[s], send_sems.at[N_CHIPS * a + s],
                                                  recv_sems.at[N_CHIPS * a + s], device_id=(x, y, 1 - c),
                                                  device_id_type=MESH)
                cp.start()
                copies.append(cp)
        for cp in copies:
            cp.wait_recv()
        for cp in copies:
            cp.wait_send()

    return pl.pallas_call(
        body, name=name, in_specs=[HBM] * n, out_specs=[HBM] * n,
        out_shape=[pltpu.HBM((N_CHIPS, g.shape[2], D), F32) for g in grads],
        scratch_shapes=[_dma_sems(N_CHIPS * n)] * 2,
        compiler_params=_comm_params(),
    )(*_hbm(*grads))


def _add_halves(core, grad, recv, dtype, name):
    h = grad.shape[2]

    def body(c_ref, a_ref, b_ref, o_ref):
        o_ref[...] = (a_ref[0] + b_ref[...]).astype(dtype)

    return pl.pallas_call(
        body, name=name,
        grid_spec=pltpu.PrefetchScalarGridSpec(
            num_scalar_prefetch=1, grid=(N_CHIPS,),
            in_specs=[pl.BlockSpec((1, 1, h, D), lambda s, c: (s, c[0], 0, 0)),
                      pl.BlockSpec((1, h, D), lambda s, c: (s, 0, 0))],
            out_specs=pl.BlockSpec((1, h, D), lambda s, c: (s, 0, 0))),
        out_shape=pltpu.HBM((N_CHIPS, h, D), dtype),
        compiler_params=_params(16, dimension_semantics=_seq()),
    )(core, *_hbm(grad, recv))


def _chip_scatter(parts, with_own):
    n = len(parts)

    def body(*refs):
        ins, outs = refs[:n], refs[n:2 * n]
        send_sems, recv_sems, local_sems = refs[2 * n:]
        x, y, c = _place()
        mine = 2 * x + y
        chips = _other_chips(x, y)
        local = [pltpu.make_async_copy(ins[a].at[mine], outs[a].at[mine], local_sems.at[a]) for a in range(n)
                 if with_own[a]]
        for cp in local:
            cp.start()
        sends = []
        for a in range(n):
            for j, (px, py) in enumerate(chips):
                cp = pltpu.make_async_remote_copy(ins[a].at[2 * px + py], outs[a].at[mine], send_sems.at[3 * a + j],
                                                  recv_sems.at[3 * a + j], device_id=(px, py, c), device_id_type=MESH)
                cp.start()
                sends.append(cp)
        for a in range(n):
            for j, (px, py) in enumerate(chips):
                pltpu.make_async_remote_copy(ins[a].at[mine], outs[a].at[2 * px + py], send_sems.at[3 * a + j],
                                             recv_sems.at[3 * a + j], device_id=(px, py, c),
                                             device_id_type=MESH).wait_recv()
        for cp in sends:
            cp.wait_send()
        for cp in local:
            cp.wait()

    return pl.pallas_call(
        body, name="chip_scatter", in_specs=[HBM] * n, out_specs=[HBM] * n,
        out_shape=[pltpu.HBM(p.shape, p.dtype) for p in parts],
        scratch_shapes=[_dma_sems(3 * n)] * 2 + [_dma_sems(n)],
        compiler_params=_comm_params(),
    )(*_hbm(*parts))


def _sum_chips(slots, first, rest, name):
    h = first.shape[1]

    def body(i_ref, a_ref, b_ref, c_ref, d_ref, o_ref):
        o_ref[...] = ((a_ref[...].astype(F32) + b_ref[...].astype(F32)) + c_ref[...].astype(F32)) + d_ref[...].astype(F32)

    slab = lambda k: pl.BlockSpec((1, h, D), lambda i, ix: (ix[k], 0, 0))
    return pl.pallas_call(
        body, name=name,
        grid_spec=pltpu.PrefetchScalarGridSpec(num_scalar_prefetch=1, grid=(1,),
                                               in_specs=[slab(0), slab(1), slab(2), slab(3)], out_specs=slab(4)),
        out_shape=pltpu.HBM((2, h, D), F32),
        compiler_params=_params(16, dimension_semantics=_seq()),
    )(slots, *_hbm(first, rest, rest, rest))


def _join_halves(halves):
    n = len(halves)

    def body(*refs):
        outs = refs[n:2 * n]
        send_sems, recv_sems = refs[2 * n:]
        x, y, c = _place()

        def copy(a, slab):
            return pltpu.make_async_remote_copy(outs[a].at[slab], outs[a].at[slab], send_sems.at[a], recv_sems.at[a],
                                                device_id=(x, y, 1 - c), device_id_type=MESH)

        for a in range(n):
            copy(a, c).start()
        for a in range(n):
            copy(a, 1 - c).wait_recv()
        for a in range(n):
            copy(a, c).wait_send()

    return pl.pallas_call(
        body, name="join_halves", in_specs=[HBM] * n, out_specs=[HBM] * n,
        out_shape=[pltpu.HBM(h.shape, F32) for h in halves],
        input_output_aliases={a: a for a in range(n)},
        scratch_shapes=[_dma_sems(n)] * 2,
        compiler_params=_comm_params(),
    )(*_hbm(*halves))


def _chip_partials(grads, wire_dtypes, names, recv=None):
    core = lax.axis_index("c").astype(jnp.int32).reshape(1)
    if recv is None:
        recv = _sibling_exchange(grads, "sibling_exchange_" + names[0])
    return [_add_halves(core, g, r, dt, "add_halves_" + nm) for g, r, dt, nm in zip(grads, recv, wire_dtypes, names)]


def _finish_reduce(parts, got, same_order, names):
    x, y, c = _place()
    others = [2 * px + py for px, py in _other_chips(x, y)]
    own_first = jnp.stack([2 * x + y] + others + [c]).astype(jnp.int32)
    chip_order = jnp.stack([0 * c, 0 * c + 1, 0 * c + 2, 0 * c + 3, c]).astype(jnp.int32)
    halves = [_sum_chips(chip_order, q, q, "sum_chips_" + nm) if fixed else _sum_chips(own_first, p, q, "sum_chips_" + nm)
              for p, q, fixed, nm in zip(parts, got, same_order, names)]
    return [f.reshape(2 * f.shape[1], D) for f in _join_halves(halves)]


def _adamw(w, g, m, v, name):
    rows, cols = w.shape
    if rows % 8 == 0:
        tr = max(t for t in range(8, 257, 8) if rows % t == 0)
        grid, blk = (rows // tr,), pl.BlockSpec((tr, cols), lambda i: (i, 0))
    else:
        grid, blk = (cols // 256,), pl.BlockSpec((rows, 256), lambda i: (0, i))

    def body(w_ref, g_ref, m_ref, v_ref, d_ref, nm_ref, nv_ref):
        d_ref[...], nm_ref[...], nv_ref[...] = _adamw_math(w_ref[...], g_ref[...], m_ref[...], v_ref[...])

    return pl.pallas_call(
        body, name=name, grid=grid,
        in_specs=[blk] * 4, out_specs=[blk] * 3,
        out_shape=[pltpu.HBM(w.shape, F32)] * 3,
        compiler_params=_params(32, dimension_semantics=_seq()),
    )(*_hbm(w, g, m, v))


def _adamw_math(w, g, m, v):
    nm = ADAM_B1 * m + (1.0 - ADAM_B1) * g
    nv = ADAM_B2 * v + (1.0 - ADAM_B2) * (g * g)
    m_hat = nm / (1.0 - ADAM_B1 ** ADAM_STEP)
    v_hat = nv / (1.0 - ADAM_B2 ** ADAM_STEP)
    return -ADAM_LR * (m_hat / (jnp.sqrt(v_hat) + ADAM_EPS) + ADAM_WD * w), nm, nv


SMALL = (("meta_tokens", (N_META, D // N_CHIPS)), ("ln_in_g", (1, D)), ("ln_in_b", (1, D)), ("b_in", (1, D_IN)),
         ("w_gate_lr2", (16, 256 // N_CHIPS)), ("b_gate_lr2", (1, 256)), ("attn_sinks", (1, SWA_HEADS)),
         ("gla_norm_g", (1, DV)), ("ln1_g", (1, D)), ("ln1_b", (1, D)), ("ln2_g", (1, D)), ("ln2_b", (1, D)))
ROW_META, ROW_B_IN, ROW_TAIL, ROW_WG2 = 0, 22, 25, 32
ROW_LN = dict(ln_in_g=16, ln_in_b=17, ln1_g=18, ln1_b=19, ln2_g=20, ln2_b=21)
TAIL_BG2, TAIL_SINKS, TAIL_GN, TAIL_LOSS = 0, 256, 256 + SWA_HEADS, 256 + SWA_HEADS + DV


def _adamw_small(chip, pack, params):
    n = len(SMALL)

    def body(chip_ref, p_ref, *refs):
        ins, outs = refs[:3 * n], refs[3 * n:]
        c = chip_ref[0]

        def mine(width, rows):
            part = lambda s: p_ref[rows, s * width:(s + 1) * width]
            return jnp.where(c == 0, part(0), jnp.where(c == 1, part(1), jnp.where(c == 2, part(2), part(3))))

        tail = lambda lo, width: p_ref[ROW_TAIL:ROW_TAIL + 1, lo:lo + width]
        grads = dict(
            meta_tokens=mine(D // N_CHIPS, slice(ROW_META, ROW_META + N_META)),
            b_in=jnp.concatenate([p_ref[ROW_B_IN:ROW_B_IN + 1, :], p_ref[ROW_B_IN + 1:ROW_B_IN + 2, :],
                                  p_ref[ROW_B_IN + 2:ROW_B_IN + 3, 0:D_IN - 2 * D]], axis=1),
            w_gate_lr2=mine(256 // N_CHIPS, slice(ROW_WG2, ROW_WG2 + 16)),
            b_gate_lr2=tail(TAIL_BG2, 256), attn_sinks=tail(TAIL_SINKS, SWA_HEADS), gla_norm_g=tail(TAIL_GN, DV),
            **{k: p_ref[r:r + 1, :] for k, r in ROW_LN.items()})
        for i, (name, _) in enumerate(SMALL):
            g = grads[name]
            outs[4 * i][...] = g
            outs[4 * i + 1][...], outs[4 * i + 2][...], outs[4 * i + 3][...] = _adamw_math(
                ins[3 * i][...], g, ins[3 * i + 1][...], ins[3 * i + 2][...])

    whole = lambda shape: pl.BlockSpec(shape, lambda i, c: (0, 0))
    outs = pl.pallas_call(
        body, name="adamw_small",
        grid_spec=pltpu.PrefetchScalarGridSpec(
            num_scalar_prefetch=1, grid=(1,),
            in_specs=[whole(pack.shape)] + [whole(s) for _, s in SMALL for _ in range(3)],
            out_specs=[whole(s) for _, s in SMALL for _ in range(4)]),
        out_shape=[pltpu.HBM(s, F32) for _, s in SMALL for _ in range(4)],
        compiler_params=_params(16, dimension_semantics=_seq()),
    )(chip, *_hbm(pack, *[a for p in params for a in p]))
    return [outs[4 * i:4 * i + 4] for i in range(n)]


def _flat_rows(v, rows):
    flat = v.reshape(-1).astype(F32)
    return jnp.pad(flat, (0, rows * D - flat.shape[0])).reshape(rows, D)


def _small_pack(gr):
    tail = jnp.concatenate([gr["bg2"].reshape(-1), gr["sinks"].reshape(-1), gr["gn"].reshape(-1), gr["loss"].reshape(-1)])
    rows = [gr["meta"].reshape(N_META, D)] + [gr[k].reshape(1, D) for k in ROW_LN]
    rows += [_flat_rows(gr["b_in"], 3), _flat_rows(tail, 1), jnp.zeros((ROW_WG2 - ROW_TAIL - 1, D), F32),
             jnp.pad(gr["wg2"], ((0, 0), (0, D - gr["wg2"].shape[1])))]
    return jnp.concatenate(rows, axis=0)


BIG = ("w_in", "w_out", "w_g", "w_u", "w_d")


def kernel(x, meta_tokens, ln_in_g, ln_in_b, w_in, b_in, w_gate_lr2, b_gate_lr2, attn_sinks, gla_norm_g, w_out, ln1_g, ln1_b, w_ffn_gate, w_ffn_up, w_ffn_down, ln2_g, ln2_b, loss_target, m_meta_tokens, m_ln_in_g, m_ln_in_b, m_w_in, m_b_in, m_w_gate_lr2, m_b_gate_lr2, m_attn_sinks, m_gla_norm_g, m_w_out, m_ln1_g, m_ln1_b, m_w_ffn_gate, m_w_ffn_up, m_w_ffn_down, m_ln2_g, m_ln2_b, v_meta_tokens, v_ln_in_g, v_ln_in_b, v_w_in, v_b_in, v_w_gate_lr2, v_b_gate_lr2, v_attn_sinks, v_gla_norm_g, v_w_out, v_ln1_g, v_ln1_b, v_w_ffn_gate, v_w_ffn_up, v_w_ffn_down, v_ln2_g, v_ln2_b):
    chip = 2 * lax.axis_index("x") + lax.axis_index("y")

    halves = lambda a: a.reshape(2, a.shape[0] // 2, a.shape[1])
    r_in = SHARD_ROWS["w_in"]
    first = [halves(a) for a in (jnp.pad(w_in[0].T.astype(BF16), ((0, W_IN_WIN - r_in), (0, 0))), meta_tokens,
                                 w_gate_lr2[0])]
    rest = [halves(a) for a in (w_out[0].astype(BF16), w_ffn_gate[0].T.astype(BF16), w_ffn_up[0].T.astype(BF16),
                                w_ffn_down[0].astype(BF16))]
    lands = lambda arrs: [(N_CHIPS,) + a.shape for a in arrs]
    first_handle, first_token = _ici_start("gather", first, lands(first), [], "gather_first_start")
    rest_handle, token = _ici_start("gather", rest, lands(rest), [first_token], "gather_rest_start")

    def fetch(handle, shards, after, name):
        got = _sibling_forward(_ici_wait("gather", handle, after, name + "_wait"), name + "_forward")
        return [lax.dynamic_update_index_in_dim(g, s, chip, axis=0) for g, s in zip(got, shards)]

    def fetch_first(after):
        g_in, g_meta, g_wg2 = fetch(first_handle, first, after, "gather_first")
        w_in_t = jnp.pad(g_in.reshape(N_CHIPS, W_IN_WIN, D)[:, :r_in].reshape(D_IN, D), ((0, D_IN_P - D_IN), (0, 0)))
        meta_full = jnp.concatenate([g_meta[s].reshape(N_META, -1) for s in range(N_CHIPS)], axis=1)
        wg2_full = jnp.concatenate([g_wg2[s].reshape(w_gate_lr2.shape[1], -1) for s in range(N_CHIPS)], axis=1)
        return w_in_t, meta_full, wg2_full

    def fetch_rest(after):
        return [g.reshape(-1, D) for g in fetch(rest_handle, rest, after, "gather_rest")]

    sent = {}
    split = lambda grads: [g.reshape(N_CHIPS, 2, -1, D) for g in grads]

    def ship(key, grads, names, recv):
        parts = _chip_partials(grads, [BF16] * len(grads), names, recv)
        handle, ship_token = _ici_start("scatter", parts, [p.shape for p in parts], [], "scatter_" + key + "_start")
        sent[key] = (parts, handle)
        return ship_token

    def exchange_ffn(g):
        grads = split([g[k] for k in BIG[2:]])
        sent["ffn_halves"] = (grads,) + _ici_start("sibling", grads, [(N_CHIPS,) + a.shape[2:] for a in grads], [],
                                                   "sibling_ffn_start")
        return sent["ffn_halves"][2]

    def ship_ffn(after):
        grads, handle, _ = sent["ffn_halves"]
        return ship("ffn", grads, list(BIG[2:]), _ici_wait("sibling", handle, after, "sibling_ffn_wait"))

    def ship_last(dw_in_t, dw_out):
        win_start = [s * r_in // BF16_ROWS * BF16_ROWS for s in range(N_CHIPS)]
        windows = jnp.stack([dw_in_t[st:st + W_IN_WIN] for st in win_start])
        return ship("last", split([windows, dw_out]), list(BIG[:2]), None)

    loss_part, dx, gr = _local_step(
        x[0], loss_target[0], ln_in_g, ln_in_b, b_in[0], b_gate_lr2[0], attn_sinks[0], gla_norm_g[0], ln1_g[0],
        ln1_b[0], ln2_g[0], ln2_b[0], token, fetch_first, fetch_rest, exchange_ffn, ship_ffn, ship_last)
    ffn_got = _ici_wait("scatter", sent["ffn"][1], [dx], "scatter_ffn_wait")
    last_got = _ici_wait("scatter", sent["last"][1], [dx], "scatter_last_wait")

    gr["loss"] = loss_part
    small = jnp.broadcast_to(_small_pack(gr), (N_CHIPS, SMALL_ROWS, D)).reshape(N_CHIPS, 2, -1, D)
    small_parts = _chip_partials([small], [F32], ["small"])
    small_got = list(_chip_scatter(small_parts, [True]))
    red = _finish_reduce(sent["last"][0] + sent["ffn"][0] + small_parts, last_got + ffn_got + small_got,
                         [False] * len(BIG) + [True], list(BIG) + ["small"])

    big_g = dict(zip(BIG, red))
    big_g["w_in"] = lax.dynamic_slice_in_dim(red[0], chip * (r_in % BF16_ROWS), r_in, axis=0)
    loss = red[-1][ROW_TAIL, TAIL_LOSS]
    grads = dict(w_in=big_g["w_in"].T[None], w_out=big_g["w_out"][None], w_ffn_gate=big_g["w_g"].T[None],
                 w_ffn_up=big_g["w_u"].T[None], w_ffn_down=big_g["w_d"][None])
    weights = dict(meta_tokens=meta_tokens, ln_in_g=ln_in_g, ln_in_b=ln_in_b, w_in=w_in, b_in=b_in,
                   w_gate_lr2=w_gate_lr2, b_gate_lr2=b_gate_lr2, attn_sinks=attn_sinks, gla_norm_g=gla_norm_g,
                   w_out=w_out, ln1_g=ln1_g, ln1_b=ln1_b, w_ffn_gate=w_ffn_gate, w_ffn_up=w_ffn_up,
                   w_ffn_down=w_ffn_down, ln2_g=ln2_g, ln2_b=ln2_b)
    m_in = dict(meta_tokens=m_meta_tokens, ln_in_g=m_ln_in_g, ln_in_b=m_ln_in_b, w_in=m_w_in, b_in=m_b_in,
                w_gate_lr2=m_w_gate_lr2, b_gate_lr2=m_b_gate_lr2, attn_sinks=m_attn_sinks, gla_norm_g=m_gla_norm_g,
                w_out=m_w_out, ln1_g=m_ln1_g, ln1_b=m_ln1_b, w_ffn_gate=m_w_ffn_gate, w_ffn_up=m_w_ffn_up,
                w_ffn_down=m_w_ffn_down, ln2_g=m_ln2_g, ln2_b=m_ln2_b)
    v_in = dict(meta_tokens=v_meta_tokens, ln_in_g=v_ln_in_g, ln_in_b=v_ln_in_b, w_in=v_w_in, b_in=v_b_in,
                w_gate_lr2=v_w_gate_lr2, b_gate_lr2=v_b_gate_lr2, attn_sinks=v_attn_sinks, gla_norm_g=v_gla_norm_g,
                w_out=v_w_out, ln1_g=v_ln1_g, ln1_b=v_ln1_b, w_ffn_gate=v_w_ffn_gate, w_ffn_up=v_w_ffn_up,
                w_ffn_down=v_w_ffn_down, ln2_g=v_ln2_g, ln2_b=v_ln2_b)
    names = list(weights)
    big_names = ("w_in", "w_out", "w_ffn_gate", "w_ffn_up", "w_ffn_down")

    delta, new_m, new_v = {}, {}, {}
    for k, kk in zip(big_names, BIG):
        flip = (lambda a: a.T) if kk in ("w_in", "w_g", "w_u") else (lambda a: a)
        d_, m_, v_ = _adamw(flip(weights[k][0]), big_g[kk], flip(m_in[k][0]), flip(v_in[k][0]), "adamw_" + k)
        delta[k], new_m[k], new_v[k] = (flip(t)[None] for t in (d_, m_, v_))
    small_in = [tuple(src[k].reshape(shape) for src in (weights, m_in, v_in)) for k, shape in SMALL]
    small_out = _adamw_small(chip.astype(jnp.int32).reshape(1), red[-1], small_in)
    for (k, _), results in zip(SMALL, small_out):
        grads[k], delta[k], new_m[k], new_v[k] = (r.reshape(weights[k].shape) for r in results)

    return (loss, dx[None], *[grads[k] for k in names], *[delta[k] for k in names], *[new_m[k] for k in names],
            *[new_v[k] for k in names])
```

```python
import jax
import jax.numpy as jnp
from jax import lax
from jax.experimental import pallas as pl
from jax.experimental.pallas import tpu as pltpu

F32 = jnp.float32
BF16 = jnp.bfloat16
MESH = pl.DeviceIdType.MESH

D = 1024
SEQ = 4096
N_META = 16
SWA_HEADS, SWA_KV_HEADS, DH = 8, 2, 64
WINDOW = 128
GLA_HEADS, DK, DV = 4, 64, 128
GLA_TAU = 16.0
CH = 64
D_FF = 2816
D_IN = 2320
LN_EPS = 1e-5
RMS_EPS = 1e-6
ALPHA = 2.0 ** 0.25
NEG = -1e30
ADAM_LR, ADAM_B1, ADAM_B2, ADAM_EPS, ADAM_WD, ADAM_STEP = 0.001, 0.9, 0.999, 1e-8, 0.01, 10
O_QS, O_KS, O_VS, O_QG, O_KG, O_VG, O_RG, O_LR = 0, 512, 640, 768, 1024, 1280, 1792, 2304

LANE = 128
BLK = WINDOW
GATE_RANK = 16
D_IN_P = D_IN + LANE - GATE_RANK
META_OFF = CH - N_META
HEAD_POS = (0, 4, 1, 5, 2, 6, 3, 7)
LN_ROWS = 512
TOKEN = (8, LANE)
N_CHIPS = 4
SHARD_ROWS = dict(w_in=D_IN // N_CHIPS, w_out=D // N_CHIPS, w_g=D_FF // N_CHIPS, w_u=D_FF // N_CHIPS,
                  w_d=D_FF // N_CHIPS)
SMALL_ROWS = 48
BF16_ROWS = 16
W_IN_WIN = -(-SHARD_ROWS["w_in"] // (2 * BF16_ROWS)) * 2 * BF16_ROWS
VMEM_CAP_MB = 64
VMEM_SPARE_MB = 6


def _lp():
    return SEQ + BLK


def _row_tile(cap):
    lp = _lp()
    return max(t for t in range(16, cap + 1, 16) if lp % t == 0)


def _params(vmem_mb, **kw):
    assert vmem_mb <= VMEM_CAP_MB - VMEM_SPARE_MB
    return pltpu.CompilerParams(vmem_limit_bytes=vmem_mb << 20, **kw)


def _seq(n=1):
    return ("arbitrary",) * n


def _const(shape):
    return pl.BlockSpec(shape, lambda *_: (0,) * len(shape), pipeline_mode=pl.Buffered(1))


def _acc(shape):
    return pl.BlockSpec(shape, lambda *_: (0,) * len(shape))


def _rows(tm, width):
    return pl.BlockSpec((tm, width), lambda i: (i, 0))


def _dot(a, b):
    return jnp.dot(a.astype(BF16), b.astype(BF16), preferred_element_type=F32)


def _dot_nt(a, b):
    return lax.dot_general(a.astype(BF16), b.astype(BF16), (((1,), (1,)), ((), ())), preferred_element_type=F32)


def _dot_tn(a, b):
    return lax.dot_general(a.astype(BF16), b.astype(BF16), (((0,), (0,)), ((), ())), preferred_element_type=F32)


def _dot_exact(a, b):
    return jnp.dot(a, b, precision=lax.Precision.HIGHEST, preferred_element_type=F32)


def _ln_stats(x):
    mu = jnp.mean(x, axis=-1, keepdims=True)
    xc = x - mu
    rstd = lax.rsqrt(jnp.mean(xc * xc, axis=-1, keepdims=True) + LN_EPS)
    return xc * rstd, rstd


def _ln_bwd(dy, xhat, rstd, g):
    dxh = dy * g
    return rstd * (dxh - jnp.mean(dxh, axis=-1, keepdims=True) - xhat * jnp.mean(dxh * xhat, axis=-1, keepdims=True))


def _sigmoid(x):
    return 1.0 / (1.0 + jnp.exp(-x))


def _iota(shape, dim):
    return lax.broadcasted_iota(jnp.int32, shape, dim)


def _hbm(*arrays):
    return tuple(pltpu.with_memory_space_constraint(a, pltpu.HBM) for a in arrays)


def _ln_in_fwd_real(x, g, b, token):
    tr = min(LN_ROWS, SEQ)

    def body(x_ref, g_ref, b_ref, token_ref, h_ref):
        xhat, _ = _ln_stats(x_ref[...])
        h_ref[...] = xhat * g_ref[...] + b_ref[...]

    return pl.pallas_call(
        body, name="ln_in_fwd", grid=(SEQ // tr,),
        in_specs=[_rows(tr, D), _const((1, D)), _const((1, D)), _const(TOKEN)],
        out_specs=_rows(tr, D),
        out_shape=pltpu.HBM((_lp(), D), F32),
        compiler_params=_params(32, dimension_semantics=_seq()),
    )(*_hbm(x, g, b), token)


def _ln_in_fwd_meta(h_real, meta_ext, g, b):
    def meta_body(m_ref, g_ref, b_ref, real_ref, h_ref):
        xhat, _ = _ln_stats(m_ref[...])
        h_ref[...] = xhat * g_ref[...] + b_ref[...]

    return pl.pallas_call(
        meta_body, name="ln_in_fwd_meta", grid=(1,),
        in_specs=[_const((BLK, D)), _const((1, D)), _const((1, D)), pl.BlockSpec(memory_space=pl.ANY)],
        out_specs=pl.BlockSpec((BLK, D), lambda i: (SEQ // BLK, 0)),
        out_shape=pltpu.HBM((_lp(), D), F32),
        input_output_aliases={3: 0},
        compiler_params=_params(16, dimension_semantics=_seq()),
    )(*_hbm(meta_ext, g, b, h_real))


def _in_proj(h0, w_in_t, b_in_p, wg2_p, bg2):
    tm = _row_tile(384)
    lp = _lp()
    widths = (512, 128, 128, 256, 256, 512, 512, 128)
    offs = (O_QS, O_KS, O_VS, O_QG, O_KG, O_VG, O_RG, O_LR)

    def body(h_ref, w_ref, b_ref, wg2_ref, bg2_ref, *outs):
        proj = _dot_nt(h_ref[...], w_ref[...]) + b_ref[...]
        for pos, h in enumerate(HEAD_POS):
            outs[0][:, pos * DH:(pos + 1) * DH] = proj[:, O_QS + h * DH:O_QS + (h + 1) * DH]
        for o_ref, off, wd in zip(outs[1:8], offs[1:], widths[1:]):
            o_ref[...] = proj[:, off:off + wd]
        outs[8][...] = _dot(proj[:, O_LR:O_LR + LANE], wg2_ref[...]) + bg2_ref[...]

    return pl.pallas_call(
        body, name="in_proj", grid=(lp // tm,),
        in_specs=[_rows(tm, D), _const((D_IN_P, D)), _const((1, D_IN_P)), _const((LANE, 256)), _const((1, 256))],
        out_specs=[_rows(tm, w) for w in widths] + [_rows(tm, 256)],
        out_shape=[pltpu.HBM((lp, w), F32) for w in widths] + [pltpu.HBM((lp, 256), F32)],
        compiler_params=_params(40, dimension_semantics=_seq()),
    )(*_hbm(h0, w_in_t, b_in_p, wg2_p, bg2))


def _swa_masks(n):
    nb = SEQ // BLK
    is_meta = n == nb
    ri = _iota((BLK, BLK), 0)
    cj = _iota((BLK, BLK), 1)
    meta_col = ((cj >= META_OFF) & (cj < CH)).astype(jnp.int32)
    meta_q = meta_col * ((cj <= ri) & (ri < CH)).astype(jnp.int32)
    valid_m = jnp.where(is_meta, meta_q, meta_col) > 0
    dist_m = jnp.where(is_meta, ri - cj, n * BLK + ri + CH - cj).astype(F32)
    valid_p = jnp.where((n >= 1) & (n < nb), (cj > ri).astype(jnp.int32), 0) > 0
    dist_p = (ri + BLK - cj).astype(F32)
    valid_c = jnp.where(n < nb, (cj <= ri).astype(jnp.int32), 0) > 0
    dist_c = (ri - cj).astype(F32)
    return (dist_m, dist_p, dist_c), (valid_m, valid_p, valid_c)


def _swa_bias(n):
    dists, valids = _swa_masks(n)
    return (jnp.concatenate([-d for d in dists], axis=1),
            jnp.concatenate([jnp.where(v, 0.0, NEG) for v in valids], axis=1))


def _swa_half(ref, pos, scale=1.0):
    col = ref[:, (pos // 2) * LANE:(pos // 2 + 1) * LANE]
    lane = _iota((BLK, LANE), 1)
    mine = lane < DH if pos % 2 == 0 else lane >= DH
    return jnp.where(mine, col * scale, 0.0).astype(BF16)


def _swa_merge(even, odd):
    return jnp.where(_iota((BLK, LANE), 1) < DH, even, odd)


def _swa_softmax(t, sink):
    m = jnp.maximum(jnp.max(t, axis=-1, keepdims=True), sink)
    e = jnp.exp(t - m)
    e_sink = jnp.exp(sink - m)
    inv = 1.0 / (jnp.sum(e, axis=-1, keepdims=True) + e_sink)
    return e * inv, e_sink * inv


def _swa_kv_specs(width):
    nb = SEQ // BLK
    return [pl.BlockSpec((BLK, width), lambda n: (nb, 0)),
            pl.BlockSpec((BLK, width), lambda n: (jnp.clip(n - 1, 0, nb - 1), 0)),
            pl.BlockSpec((BLK, width), lambda n: (jnp.minimum(n, nb), 0))]


def _swa_fwd(sinks, qs, ks, vs):
    nb = SEQ // BLK
    heads = range(SWA_HEADS)

    def body(sink_ref, q_ref, km_ref, kp_ref, kc_ref, vm_ref, vp_ref, vc_ref, o_ref):
        negdist, maskbias = _swa_bias(pl.program_id(0))
        k_all = jnp.concatenate([km_ref[...], kp_ref[...], kc_ref[...]], axis=0).astype(BF16)
        v_all = jnp.concatenate([vm_ref[...], vp_ref[...], vc_ref[...]], axis=0).astype(BF16)
        q = [_swa_half(q_ref, pos, DH ** -0.5) for pos in heads]
        t = [_dot_nt(q[pos], k_all) + (2.0 ** -(HEAD_POS[pos] + 1) * negdist + maskbias) for pos in heads]
        p = [_swa_softmax(t[pos], sink_ref[HEAD_POS[pos]])[0].astype(BF16) for pos in heads]
        o = [_dot(p[pos], v_all) for pos in heads]
        for col in range(SWA_HEADS // 2):
            o_ref[:, col * LANE:(col + 1) * LANE] = _swa_merge(o[2 * col], o[2 * col + 1])

    kvw = SWA_KV_HEADS * DH
    return pl.pallas_call(
        body, name="swa_fwd", grid=(nb + 1,),
        in_specs=[pl.BlockSpec(memory_space=pltpu.SMEM), _rows(BLK, SWA_HEADS * DH)] + _swa_kv_specs(kvw) + _swa_kv_specs(kvw),
        out_specs=_rows(BLK, SWA_HEADS * DH),
        out_shape=pltpu.HBM((_lp(), SWA_HEADS * DH), F32),
        compiler_params=_params(16, dimension_semantics=_seq()),
    )(sinks, *_hbm(qs, ks, ks, ks, vs, vs, vs))


GLA_PER_STEP = BLK // CH


def _gla_block(s):
    nb = SEQ // BLK
    return jnp.where(s == 0, nb, s - 1)


def _gla_rowmask(s):
    ri = _iota((BLK, 1), 0)
    m = jnp.where(s == 0, ((ri >= META_OFF) & (ri < CH)).astype(jnp.int32), 1)
    return (m > 0).astype(F32) + jnp.zeros((BLK, 1), F32)


def _gla_chunk_masks():
    r, c = _iota((BLK, BLK), 0), _iota((BLK, BLK), 1)
    same = ((r < CH) & (c < CH)) | ((r >= CH) & (c >= CH))
    return same & (r >= c), same & (r <= c), same


def _gla_decay(z, rmask):
    log_g = (jnp.minimum(z, 0.0) - jnp.log1p(jnp.exp(-jnp.abs(z)))) * (rmask / GLA_TAU)
    lower, _, same = _gla_chunk_masks()
    return _dot_exact(lower.astype(F32), log_g), _dot_exact(same.astype(F32), log_g)


def _gla_slices(c, h):
    return slice(c * CH, (c + 1) * CH), slice(h * DK, (h + 1) * DK), slice(h * DV, (h + 1) * DV)


def _gla_fwd(qg, kg, vg, z):
    steps = SEQ // BLK + 1
    kw, vw = GLA_HEADS * DK, GLA_HEADS * DV
    pairs = [(c, h) for c in range(GLA_PER_STEP) for h in range(GLA_HEADS)]

    def body(q_ref, k_ref, v_ref, z_ref, o_ref, st_ref, st):
        s = pl.program_id(0)

        @pl.when(s == 0)
        def _():
            st[...] = jnp.zeros_like(st)

        rmask = _gla_rowmask(s)
        b, b_last = _gla_decay(z_ref[...], rmask)
        q = q_ref[...] * (rmask * DK ** -0.5)
        k = k_ref[...] * rmask
        v = v_ref[...] * rmask
        qe = q * jnp.exp(b)
        ke = k * jnp.exp(-b)
        kd = k * jnp.exp(b_last - b)
        e_last = jnp.exp(b_last)
        causal = _iota((CH, CH), 0) >= _iota((CH, CH), 1)
        a, upd, intra = {}, {}, {}
        for c, h in pairs:
            rows, ks, vs_ = _gla_slices(c, h)
            a[c, h] = jnp.where(causal, _dot_nt(qe[rows, ks], ke[rows, ks]), 0.0)
            upd[c, h] = _dot_tn(v[rows, vs_], kd[rows, ks])
        for c, h in pairs:
            rows, ks, vs_ = _gla_slices(c, h)
            intra[c, h] = _dot(a[c, h], v[rows, vs_])
        state = st[...]
        for c in range(GLA_PER_STEP):
            st_ref[0, c] = state
            for h in range(GLA_HEADS):
                rows, ks, vs_ = _gla_slices(c, h)
                o_ref[rows, vs_] = intra[c, h] + _dot_nt(qe[rows, ks], state[:, ks])
            state = state * e_last[c * CH:c * CH + 1] + jnp.concatenate([upd[c, h] for h in range(GLA_HEADS)], axis=1)
        st[...] = state

    blk = lambda w: pl.BlockSpec((BLK, w), lambda s: (_gla_block(s), 0))
    return pl.pallas_call(
        body, name="gla_fwd", grid=(steps,),
        in_specs=[blk(kw), blk(kw), blk(vw), blk(kw)],
        out_specs=[blk(vw), pl.BlockSpec((1, GLA_PER_STEP, DV, kw), lambda s: (s, 0, 0, 0))],
        out_shape=[pltpu.HBM((_lp(), vw), F32), pltpu.HBM((steps, GLA_PER_STEP, DV, kw), F32)],
        scratch_shapes=[pltpu.VMEM((DV, kw), F32)],
        compiler_params=_params(16, dimension_semantics=_seq()),
    )(*_hbm(qg, kg, vg, z))


def _post_mix(o_s, o_gla, r_g, h0, gn4, w_out, g1, b1):
    tm = _row_tile(384)
    lp = _lp()

    def body(os_ref, og_ref, r_ref, h0_ref, gn_ref, w_ref, g_ref, b_ref, o_ref, pre_ref, h1_ref):
        for pos, h in enumerate(HEAD_POS):
            o_ref[:, h * DH:(h + 1) * DH] = os_ref[:, pos * DH:(pos + 1) * DH].astype(BF16)
        for h in range(GLA_HEADS):
            hs = slice(h * DV, (h + 1) * DV)
            xg = og_ref[:, hs]
            n = xg * lax.rsqrt(jnp.mean(xg * xg, axis=-1, keepdims=True) + RMS_EPS) * gn_ref[...]
            r = r_ref[:, hs]
            o_ref[:, 512 + h * DV:512 + (h + 1) * DV] = (n * (r * _sigmoid(r))).astype(BF16)
        pre = ALPHA * h0_ref[...] + _dot(o_ref[...], w_ref[...])
        pre_ref[...] = pre
        xhat, _ = _ln_stats(pre)
        h1_ref[...] = xhat * g_ref[...] + b_ref[...]

    return pl.pallas_call(
        body, name="post_mix", grid=(lp // tm,),
        in_specs=[_rows(tm, 512), _rows(tm, 512), _rows(tm, 512), _rows(tm, D), _const((1, DV)), _const((D, D)),
                  _const((1, D)), _const((1, D))],
        out_specs=[_rows(tm, D), _rows(tm, D), _rows(tm, D)],
        out_shape=[pltpu.HBM((lp, D), BF16), pltpu.HBM((lp, D), F32),
                   pltpu.HBM((lp, D), F32)],
        compiler_params=_params(32, dimension_semantics=_seq()),
    )(*_hbm(o_s, o_gla, r_g, h0, gn4, w_out, g1, b1))


def _ffn_fwd_loss_bwd(h1, wg_t, wu_t, wd, target, g2, b2):
    lp = _lp()
    tm = max(t for t in range(BLK, 384 + 1, BLK) if lp % t == 0)
    steps = lp // tm
    last_blk = SEQ // BLK - 1
    half = D_FF // 2
    n_t = tm // BLK

    def body(*refs):
        h_ref, wg_ref, wu_ref, wd_ref = refs[:4]
        t_refs = refs[4:4 + n_t]
        g2_ref, b2_ref, a_ref, dgate_ref, dup_ref, dp_ref, loss_ref, dg_ref, db_ref, g_s, u_s, acc = refs[4 + n_t:]
        i = pl.program_id(0)

        @pl.when(i == 0)
        def _():
            acc[...] = jnp.zeros_like(acc)
            dg_ref[...] = jnp.zeros_like(dg_ref)
            db_ref[...] = jnp.zeros_like(db_ref)

        h = h_ref[...]
        hb = h.astype(BF16)
        pre = ALPHA * h
        for j in range(2):
            cols = slice(j * half, (j + 1) * half)
            g = _dot_nt(hb, wg_ref[cols, :])
            u = _dot_nt(hb, wu_ref[cols, :])
            g_s[:, cols] = g
            u_s[:, cols] = u
            pre = pre + _dot(g * _sigmoid(g) * u, wd_ref[cols, :])
        xhat, rstd = _ln_stats(pre)
        real = i * tm + _iota((tm, 1), 0) < SEQ
        target_rows = jnp.concatenate([t[...] for t in t_refs], axis=0)
        diff = jnp.where(real, xhat * g2_ref[...] + b2_ref[...] - target_rows, 0.0)
        acc[...] += jnp.sum(diff * diff, axis=0, keepdims=True)
        dy = diff * (1.0 / D)
        dpre = _ln_bwd(dy, xhat, rstd, g2_ref[...])
        dp_ref[...] = dpre
        dg_ref[...] += jnp.sum(dy * xhat, axis=0, keepdims=True)
        db_ref[...] += jnp.sum(dy, axis=0, keepdims=True)
        dpb = dpre.astype(BF16)
        for j in range(2):
            cols = slice(j * half, (j + 1) * half)
            g, u = g_s[:, cols], u_s[:, cols]
            sg = _sigmoid(g)
            silu = g * sg
            da = _dot_nt(dpb, wd_ref[cols, :])
            a_ref[:, cols] = (silu * u).astype(BF16)
            dgate_ref[:, cols] = (da * u * (sg * (1.0 + g * (1.0 - sg)))).astype(BF16)
            dup_ref[:, cols] = (da * silu).astype(BF16)

        @pl.when(i == steps - 1)
        def _():
            loss_ref[...] = jnp.zeros_like(loss_ref) + (0.5 / D) * jnp.sum(acc[...], axis=1, keepdims=True)

    t_spec = lambda k: pl.BlockSpec((BLK, D), lambda i: (jnp.minimum(i * n_t + k, last_blk), 0))
    return pl.pallas_call(
        body, name="ffn_fwd_loss_bwd", grid=(steps,),
        in_specs=[_rows(tm, D), _const((D_FF, D)), _const((D_FF, D)), _const((D_FF, D))]
        + [t_spec(k) for k in range(n_t)] + [_const((1, D)), _const((1, D))],
        out_specs=[_rows(tm, D_FF), _rows(tm, D_FF), _rows(tm, D_FF), _rows(tm, D), _acc((1, LANE)), _acc((1, D)),
                   _acc((1, D))],
        out_shape=[pltpu.HBM((lp, D_FF), BF16)] * 3 + [pltpu.HBM((lp, D), F32), pltpu.HBM((1, LANE), F32),
                                                         pltpu.HBM((1, D), F32), pltpu.HBM((1, D), F32)],
        scratch_shapes=[pltpu.VMEM((tm, D_FF), F32), pltpu.VMEM((tm, D_FF), F32), pltpu.VMEM((1, D), F32)],
        compiler_params=_params(58, dimension_semantics=_seq()),
    )(*_hbm(h1, wg_t, wu_t, wd, *[target] * n_t, g2, b2))


def _ffn_out_bwd(dpre2, dgate, dup, pre1, wg_t, wu_t, g1, w_out, o_gla, r_g, gn4):
    tm = _row_tile(384)
    lp = _lp()

    def body(dp_ref, dg_ref, du_ref, p1_ref, wg_ref, wu_ref, g1_ref, w_ref, og_ref, r_ref, gn_ref,
             dp1_ref, dg1_ref, db1_ref, dos_ref, dog_ref, dr_ref, dgn_ref):
        @pl.when(pl.program_id(0) == 0)
        def _():
            for acc_ref in (dg1_ref, db1_ref, dgn_ref):
                acc_ref[...] = jnp.zeros_like(acc_ref)

        dh1 = ALPHA * dp_ref[...] + _dot(dg_ref[...], wg_ref[...]) + _dot(du_ref[...], wu_ref[...])
        xhat, rstd1 = _ln_stats(p1_ref[...])
        dpre1 = _ln_bwd(dh1, xhat, rstd1, g1_ref[...])
        dp1_ref[...] = dpre1
        dg1_ref[...] += jnp.sum(dh1 * xhat, axis=0, keepdims=True)
        db1_ref[...] += jnp.sum(dh1, axis=0, keepdims=True)

        do = _dot_nt(dpre1, w_ref[...])
        for pos, h in enumerate(HEAD_POS):
            dos_ref[:, pos * DH:(pos + 1) * DH] = do[:, h * DH:(h + 1) * DH]
        gn = gn_ref[...]
        for h in range(GLA_HEADS):
            hs = slice(h * DV, (h + 1) * DV)
            xg = og_ref[:, hs]
            rstd = lax.rsqrt(jnp.mean(xg * xg, axis=-1, keepdims=True) + RMS_EPS)
            nx = xg * rstd
            r = r_ref[:, hs]
            sr = _sigmoid(r)
            d_o = do[:, 512 + h * DV:512 + (h + 1) * DV]
            dr_ref[:, hs] = d_o * (nx * gn) * (sr * (1.0 + r * (1.0 - sr)))
            dn = d_o * (r * sr)
            dgn_ref[...] += jnp.sum(dn * nx, axis=0, keepdims=True)
            dnx = dn * gn
            dog_ref[:, hs] = rstd * (dnx - nx * jnp.mean(dnx * nx, axis=-1, keepdims=True))

    return pl.pallas_call(
        body, name="ffn_out_bwd", grid=(lp // tm,),
        in_specs=[_rows(tm, D), _rows(tm, D_FF), _rows(tm, D_FF), _rows(tm, D), _const((D_FF, D)), _const((D_FF, D)),
                  _const((1, D)), _const((D, D)), _rows(tm, 512), _rows(tm, 512), _const((1, DV))],
        out_specs=[_rows(tm, D), _acc((1, D)), _acc((1, D)), _rows(tm, 512), _rows(tm, 512), _rows(tm, 512),
                   _acc((1, DV))],
        out_shape=[pltpu.HBM((lp, D), F32), pltpu.HBM((1, D), F32), pltpu.HBM((1, D), F32)]
        + [pltpu.HBM((lp, 512), F32)] * 3 + [pltpu.HBM((1, DV), F32)],
        compiler_params=_params(48, dimension_semantics=_seq()),
    )(*_hbm(dpre2, dgate, dup, pre1, wg_t, wu_t, g1, w_out, o_gla, r_g, gn4))


def _atb(a, b, name):
    lp = _lp()
    tm = _row_tile(1408)
    n, w = a.shape[1], b.shape[1]
    bw = 512 if n * w * 4 > (4 << 20) else w

    def body(a_ref, b_ref, o_ref):
        @pl.when(pl.program_id(1) == 0)
        def _():
            o_ref[...] = jnp.zeros_like(o_ref)

        o_ref[...] += _dot_tn(a_ref[...], b_ref[...])

    return pl.pallas_call(
        body, name=name, grid=(w // bw, lp // tm),
        in_specs=[pl.BlockSpec((tm, n), lambda j, k: (k, 0)), pl.BlockSpec((tm, bw), lambda j, k: (k, j))],
        out_specs=pl.BlockSpec((n, bw), lambda j, k: (0, j)),
        out_shape=pltpu.HBM((n, w), F32),
        compiler_params=_params(48, dimension_semantics=_seq(2)),
    )(*_hbm(a, b))


def _gla_bwd(qg, kg, vg, z, do_gla, st_all, token):
    steps = SEQ // BLK + 1
    kw, vw = GLA_HEADS * DK, GLA_HEADS * DV
    pairs = [(c, h) for c in range(GLA_PER_STEP) for h in range(GLA_HEADS)]
    heads = range(GLA_HEADS)

    def body(q_ref, k_ref, v_ref, z_ref, do_ref, st_ref, token_ref, dq_ref, dk_ref, dv_ref, dz_ref, dst):
        @pl.when(pl.program_id(0) == 0)
        def _():
            dst[...] = jnp.zeros_like(dst)

        rmask = _gla_rowmask(steps - 1 - pl.program_id(0))
        zz = z_ref[...]
        b, b_last = _gla_decay(zz, rmask)
        e_b, e_nb, e_kd, e_last = jnp.exp(b), jnp.exp(-b), jnp.exp(b_last - b), jnp.exp(b_last)
        q = q_ref[...] * (rmask * DK ** -0.5)
        k = k_ref[...] * rmask
        v = v_ref[...] * rmask
        qe, ke, kd = q * e_b, k * e_nb, k * e_kd
        d_o = do_ref[...]
        causal = _iota((CH, CH), 0) >= _iota((CH, CH), 1)
        a, da, dqe, dke, dv_intra, carry = {}, {}, {}, {}, {}, {}
        for c, h in pairs:
            rows, ks, vs_ = _gla_slices(c, h)
            a[c, h] = jnp.where(causal, _dot_nt(qe[rows, ks], ke[rows, ks]), 0.0)
            da[c, h] = jnp.where(causal, _dot_nt(d_o[rows, vs_], v[rows, vs_]), 0.0)
            carry[c, h] = _dot_tn(d_o[rows, vs_], qe[rows, ks])
        for c, h in pairs:
            rows, ks, vs_ = _gla_slices(c, h)
            dqe[c, h] = _dot(d_o[rows, vs_], st_ref[0, c][:, ks]) + _dot(da[c, h], ke[rows, ks])
            dke[c, h] = _dot_tn(da[c, h], qe[rows, ks])
            dv_intra[c, h] = _dot_tn(a[c, h], d_o[rows, vs_])
        dstate = dst[...]
        dkd, db_decay = {}, {}
        for c in reversed(range(GLA_PER_STEP)):
            for h in heads:
                rows, ks, vs_ = _gla_slices(c, h)
                dkd[c, h] = _dot(v[rows, vs_], dstate[:, ks])
                dv_ref[rows, vs_] = dv_intra[c, h] + _dot_nt(kd[rows, ks], dstate[:, ks])
            chunk_last = e_last[c * CH:c * CH + 1]
            db_decay[c] = jnp.sum(dstate * st_ref[0, c], axis=0, keepdims=True) * chunk_last
            dstate = dstate * chunk_last + jnp.concatenate([carry[c, h] for h in heads], axis=1)
        dst[...] = dstate
        rows_of = lambda parts: jnp.concatenate(
            [jnp.concatenate([parts[c, h] for h in heads], axis=1) for c in range(GLA_PER_STEP)], axis=0)
        dqe_all, dke_all, dkd_all = rows_of(dqe), rows_of(dke), rows_of(dkd)
        dq_ref[...] = dqe_all * e_b * (rmask * DK ** -0.5)
        dk_ref[...] = (dke_all * e_nb + dkd_all * e_kd) * rmask
        dkd_kd = dkd_all * kd
        db = dqe_all * qe - dke_all * ke - dkd_kd
        _, upper, same = _gla_chunk_masks()
        decay_rows = jnp.concatenate([jnp.broadcast_to(db_decay[c], (CH, kw)) for c in range(GLA_PER_STEP)], axis=0)
        dlog_g = _dot_exact(upper.astype(F32), db) + _dot_exact(same.astype(F32), dkd_kd) + decay_rows
        dz_ref[...] = dlog_g * (rmask / GLA_TAU) * _sigmoid(-zz)

    blk = lambda w: pl.BlockSpec((BLK, w), lambda s: (_gla_block(steps - 1 - s), 0))
    return pl.pallas_call(
        body, name="gla_bwd", grid=(steps,),
        in_specs=[blk(kw), blk(kw), blk(vw), blk(kw), blk(vw),
                  pl.BlockSpec((1, GLA_PER_STEP, DV, kw), lambda s: (steps - 1 - s, 0, 0, 0)), _const(TOKEN)],
        out_specs=[blk(kw), blk(kw), blk(vw), blk(kw)],
        out_shape=[pltpu.HBM((_lp(), kw), F32), pltpu.HBM((_lp(), kw), F32),
                   pltpu.HBM((_lp(), vw), F32), pltpu.HBM((_lp(), kw), F32)],
        scratch_shapes=[pltpu.VMEM((DV, kw), F32)],
        compiler_params=_params(16, dimension_semantics=_seq()),
    )(*_hbm(qg, kg, vg, z, do_gla, st_all), token)


def _swa_bwd(sinks, qs, ks, vs, do_s, token):
    nb = SEQ // BLK
    kvw = SWA_KV_HEADS * DH
    scale = DH ** -0.5
    heads = range(SWA_HEADS)

    def body(sink_ref, q_ref, km_ref, kp_ref, kc_ref, vm_ref, vp_ref, vc_ref, do_ref, token_ref,
             dq_ref, dk_ref, dv_ref, dsink_ref, carry_k, carry_v, meta_k, meta_v):
        n = pl.program_id(0)

        @pl.when(n == 0)
        def _():
            for r in (carry_k, carry_v, meta_k, meta_v):
                r[...] = jnp.zeros_like(r)
            dsink_ref[...] = jnp.zeros_like(dsink_ref)

        @pl.when(n <= nb)
        def _():
            negdist, maskbias = _swa_bias(n)
            lane = _iota((1, LANE), 1)
            k_all = jnp.concatenate([km_ref[...], kp_ref[...], kc_ref[...]], axis=0).astype(BF16)
            v_all = jnp.concatenate([vm_ref[...], vp_ref[...], vc_ref[...]], axis=0).astype(BF16)
            q = [_swa_half(q_ref, pos, scale) for pos in heads]
            d_o = [_swa_half(do_ref, pos) for pos in heads]
            t = [_dot_nt(q[pos], k_all) + (2.0 ** -(HEAD_POS[pos] + 1) * negdist + maskbias) for pos in heads]
            dp = [_dot_nt(d_o[pos], v_all) for pos in heads]
            soft = [_swa_softmax(t[pos], sink_ref[HEAD_POS[pos]]) for pos in heads]
            p = [s[0] for s in soft]
            delta = [jnp.sum(p[pos] * dp[pos], axis=-1, keepdims=True) for pos in heads]
            ds = [(p[pos] * (dp[pos] - delta[pos])).astype(BF16) for pos in heads]
            dq = [_dot(ds[pos], k_all) for pos in heads]
            for col in range(SWA_HEADS // 2):
                dq_ref[:, col * LANE:(col + 1) * LANE] = scale * _swa_merge(dq[2 * col], dq[2 * col + 1])
            dsink = jnp.zeros((1, LANE), F32)
            for pos in heads:
                dsink = dsink + jnp.where(lane == HEAD_POS[pos],
                                          -jnp.sum(soft[pos][1] * delta[pos], axis=0, keepdims=True), 0.0)
            dsink_ref[...] += dsink
            dk3 = _dot_tn(jnp.concatenate(q, axis=0), jnp.concatenate(ds, axis=0)).T
            dv3 = _dot_tn(jnp.concatenate(d_o, axis=0), jnp.concatenate([x.astype(BF16) for x in p], axis=0)).T
            meta_k[...] += dk3[0:BLK]
            meta_v[...] += dv3[0:BLK]
            dk_ref[...] = carry_k[...] + dk3[BLK:2 * BLK]
            dv_ref[...] = carry_v[...] + dv3[BLK:2 * BLK]
            carry_k[...] = dk3[2 * BLK:3 * BLK]
            carry_v[...] = dv3[2 * BLK:3 * BLK]

        @pl.when(n == nb + 1)
        def _():
            dk_ref[...] = meta_k[...]
            dv_ref[...] = meta_v[...]

    kv_out = pl.BlockSpec((BLK, kvw), lambda n: (jnp.where(n == nb + 1, nb, jnp.clip(n - 1, 0, nb - 1)), 0))
    qblk = pl.BlockSpec((BLK, SWA_HEADS * DH), lambda n: (jnp.minimum(n, nb), 0))
    return pl.pallas_call(
        body, name="swa_bwd", grid=(nb + 2,),
        in_specs=[pl.BlockSpec(memory_space=pltpu.SMEM), qblk] + _swa_kv_specs(kvw) + _swa_kv_specs(kvw)
        + [qblk, _const(TOKEN)],
        out_specs=[qblk, kv_out, kv_out, _acc((1, LANE))],
        out_shape=[pltpu.HBM((_lp(), SWA_HEADS * DH), F32), pltpu.HBM((_lp(), kvw), F32),
                   pltpu.HBM((_lp(), kvw), F32), pltpu.HBM((1, LANE), F32)],
        scratch_shapes=[pltpu.VMEM((BLK, kvw), F32)] * 4,
        compiler_params=_params(16, dimension_semantics=_seq()),
    )(sinks, *_hbm(qs, ks, ks, ks, vs, vs, vs, do_s), token)


def _in_bwd(dqs, dks, dvs, dqg, dkg, dvg, drg, dz, dpre1, w_in_t, wg2_p):
    tm = _row_tile(384)
    lp = _lp()
    widths = (512, 128, 128, 256, 256, 512, 512)
    offs = (O_QS, O_KS, O_VS, O_QG, O_KG, O_VG, O_RG)

    def body(*refs):
        parts, (dz_ref, dp1_ref, w_ref, wg2_ref, dproj_ref, dh0_ref, dbin_ref, dbg_ref) = refs[:7], refs[7:]

        @pl.when(pl.program_id(0) == 0)
        def _():
            dbin_ref[...] = jnp.zeros_like(dbin_ref)
            dbg_ref[...] = jnp.zeros_like(dbg_ref)

        for pos, h in enumerate(HEAD_POS):
            val = parts[0][:, pos * DH:(pos + 1) * DH]
            dproj_ref[:, O_QS + h * DH:O_QS + (h + 1) * DH] = val.astype(BF16)
            dbin_ref[:, O_QS + h * DH:O_QS + (h + 1) * DH] += jnp.sum(val, axis=0, keepdims=True)
        for p_ref, off, wd in zip(parts[1:], offs[1:], widths[1:]):
            val = p_ref[...]
            dproj_ref[:, off:off + wd] = val.astype(BF16)
            dbin_ref[:, off:off + wd] += jnp.sum(val, axis=0, keepdims=True)
        dz = dz_ref[...]
        dlr = _dot_nt(dz, wg2_ref[...])
        dproj_ref[:, O_LR:O_LR + LANE] = dlr.astype(BF16)
        dbin_ref[:, O_LR:O_LR + LANE] += jnp.sum(dlr, axis=0, keepdims=True)
        dbg_ref[...] += jnp.sum(dz, axis=0, keepdims=True)
        dh0_ref[...] = ALPHA * dp1_ref[...] + _dot(dproj_ref[...], w_ref[...])

    return pl.pallas_call(
        body, name="in_bwd", grid=(lp // tm,),
        in_specs=[_rows(tm, w) for w in widths] + [_rows(tm, 256), _rows(tm, D), _const((D_IN_P, D)), _const((LANE, 256))],
        out_specs=[_rows(tm, D_IN_P), _rows(tm, D), _acc((1, D_IN_P)), _acc((1, 256))],
        out_shape=[pltpu.HBM((lp, D_IN_P), BF16), pltpu.HBM((lp, D), F32),
                   pltpu.HBM((1, D_IN_P), F32), pltpu.HBM((1, 256), F32)],
        compiler_params=_params(40, dimension_semantics=_seq()),
    )(*_hbm(dqs, dks, dvs, dqg, dkg, dvg, drg, dz, dpre1, w_in_t, wg2_p))


def _ln_in_bwd(x, meta_ext, dh0, g, token):
    tr = min(LN_ROWS, SEQ)

    def ln_bwd(x_ref, dh_ref, g_ref, dx_ref, dg_ref, db_ref):
        @pl.when(pl.program_id(0) == 0)
        def _():
            dg_ref[...] = jnp.zeros_like(dg_ref)
            db_ref[...] = jnp.zeros_like(db_ref)

        xhat, rstd = _ln_stats(x_ref[...])
        dh = dh_ref[...]
        dx_ref[...] = _ln_bwd(dh, xhat, rstd, g_ref[...])
        dg_ref[...] += jnp.sum(dh * xhat, axis=0, keepdims=True)
        db_ref[...] += jnp.sum(dh, axis=0, keepdims=True)

    def body(x_ref, dh_ref, g_ref, token_ref, dx_ref, dg_ref, db_ref):
        ln_bwd(x_ref, dh_ref, g_ref, dx_ref, dg_ref, db_ref)

    def meta_body(m_ref, dh_ref, g_ref, dm_ref, dg_ref, db_ref):
        ln_bwd(m_ref, dh_ref, g_ref, dm_ref, dg_ref, db_ref)

    sums = [pltpu.HBM((1, D), F32), pltpu.HBM((1, D), F32)]
    dx, dg, db = pl.pallas_call(
        body, name="ln_in_bwd", grid=(SEQ // tr,),
        in_specs=[_rows(tr, D), _rows(tr, D), _const((1, D)), _const(TOKEN)],
        out_specs=[_rows(tr, D), _acc((1, D)), _acc((1, D))],
        out_shape=[pltpu.HBM((SEQ, D), F32)] + sums,
        compiler_params=_params(32, dimension_semantics=_seq()),
    )(*_hbm(x, dh0, g), token)
    dm, dg_m, db_m = pl.pallas_call(
        meta_body, name="ln_in_bwd_meta", grid=(1,),
        in_specs=[_const((BLK, D)), pl.BlockSpec((BLK, D), lambda i: (SEQ // BLK, 0)), _const((1, D))],
        out_specs=[_acc((BLK, D)), _acc((1, D)), _acc((1, D))],
        out_shape=[pltpu.HBM((BLK, D), F32)] + sums,
        compiler_params=_params(16, dimension_semantics=_seq()),
    )(*_hbm(meta_ext, dh0, g))
    return dx, dm, dg + dg_m, db + db_m


def _local_step(x, target, ln_in_g, ln_in_b, b_in, bg2, sinks, gn, g1, b1, g2, b2,
                token, fetch_first, fetch_rest, ship_ffn, ship_w_in):
    row = lambda v: v.reshape(1, -1).astype(F32)
    b_in_p = jnp.pad(row(b_in), ((0, 0), (0, D_IN_P - D_IN)))
    gn4 = row(gn)
    sinks = sinks.reshape(-1).astype(F32)

    h_real = _ln_in_fwd_real(x, row(ln_in_g), row(ln_in_b), token)
    w_in_t, meta_full, wg2 = fetch_first([h_real])
    meta_ext = jnp.pad(meta_full, ((META_OFF, BLK - CH), (0, 0)))
    wg2_p = jnp.pad(wg2, ((0, LANE - wg2.shape[0]), (0, 0))).astype(BF16)
    h0 = _ln_in_fwd_meta(h_real, meta_ext, row(ln_in_g), row(ln_in_b))
    qs, ks, vs, qg, kg, vg, rg, glr, z = _in_proj(h0, w_in_t, b_in_p, wg2_p, row(bg2))
    o_s = _swa_fwd(sinks, qs, ks, vs)
    o_gla, st_all = _gla_fwd(qg, kg, vg, z)
    w_out, wg_t, wu_t, wd = fetch_rest([o_s, o_gla])
    o, pre1, h1 = _post_mix(o_s, o_gla, rg, h0, gn4, w_out, row(g1), row(b1))
    a, dgate, dup, dpre2, loss, dg2, db2 = _ffn_fwd_loss_bwd(h1, wg_t, wu_t, wd, target, row(g2), row(b2))
    dpre1, dg1, db1, do_s, do_gla, drg, dgn = _ffn_out_bwd(dpre2, dgate, dup, pre1, wg_t, wu_t, row(g1), w_out, o_gla,
                                                           rg, gn4)
    dwd = _atb(a, dpre2, "dw_down")
    dwg_t = _atb(dgate, h1, "dw_gate")
    dwu_t = _atb(dup, h1, "dw_up")
    dw_out = _atb(o, dpre1, "dw_out")
    token = ship_ffn(dict(w_out=dw_out, w_g=dwg_t, w_u=dwu_t, w_d=dwd))
    dqg, dkg, dvg, dz = _gla_bwd(qg, kg, vg, z, do_gla, st_all, token)
    dqs, dks, dvs, dsinks = _swa_bwd(sinks, qs, ks, vs, do_s, token)
    dproj, dh0, db_in_p, dbg2 = _in_bwd(dqs, dks, dvs, dqg, dkg, dvg, drg, dz, dpre1, w_in_t, wg2_p)
    token = ship_w_in(_atb(dproj, h0, "dw_in"))
    dwg2_p = _atb(glr, dz, "dw_gate_lr2")
    dx, dmeta_blk, dg_in, db_in_ln = _ln_in_bwd(x, meta_ext, dh0, row(ln_in_g), token)

    small = dict(meta_blk=dmeta_blk, ln_in_g=dg_in, ln_in_b=db_in_ln, ln1_g=dg1, ln1_b=db1, ln2_g=dg2, ln2_b=db2,
                 b_in_p=db_in_p, wg2_p=dwg2_p, bg2=dbg2, sinks=dsinks, gn=dgn, loss=loss)
    return dx, small


HBM = pl.BlockSpec(memory_space=pltpu.HBM)


def _place():
    return lax.axis_index("x"), lax.axis_index("y"), lax.axis_index("c")


def _other_chips(x, y):
    return [(1 - x, y), (x, 1 - y), (1 - x, 1 - y)]


def _dma_sems(n):
    return pltpu.SemaphoreType.DMA((n,))


def _comm_params():
    return pltpu.CompilerParams(has_side_effects=True)


SEM = pl.BlockSpec(memory_space=pltpu.SEMAPHORE)


def _ici_copies(kind, landing, srcs, lands, send_sems, recv_sems):
    x, y, c = _place()
    mine = 2 * x + y
    copies = []
    for a in range(len(srcs)):
        for j, (px, py) in enumerate(_other_chips(x, y)):
            slab = 2 * px + py if landing else mine
            if kind == "gather":
                src, dst = srcs[a].at[c], lands[a].at[slab, c]
            else:
                src, dst = srcs[a].at[2 * px + py], lands[a].at[slab]
            copies.append(pltpu.make_async_remote_copy(src, dst, send_sems.at[3 * a + j], recv_sems.at[3 * a + j],
                                                       device_id=(px, py, c), device_id_type=MESH))
    return copies


def _split_params():
    return pltpu.CompilerParams(has_side_effects=pltpu.SideEffectType.DATAFLOW_SIDE_EFFECTING)


def _ici_start(kind, srcs, land_shapes, after, name):
    n = len(srcs)
    lands = [pltpu.with_memory_space_constraint(lax.empty(s, a.dtype), pltpu.HBM) for s, a in zip(land_shapes, srcs)]

    def body(*refs):
        outs = refs[2 * n + len(after):]
        for cp in _ici_copies(kind, False, refs[:n], refs[n:2 * n], outs[0], outs[1]):
            cp.start()
        outs[-1][...] = jnp.zeros(TOKEN, F32)

    outs = pl.pallas_call(
        body, name=name, in_specs=[HBM] * (2 * n) + [pl.BlockSpec(memory_space=pl.ANY)] * len(after),
        out_specs=[SEM, SEM] + [HBM] * (2 * n) + [pl.BlockSpec(memory_space=pltpu.VMEM)],
        out_shape=[_dma_sems(3 * n)] * 2 + [pltpu.HBM(a.shape, a.dtype) for a in list(srcs) + lands]
        + [jax.ShapeDtypeStruct(TOKEN, F32)],
        input_output_aliases={i: 2 + i for i in range(2 * n)},
        compiler_params=_split_params(),
    )(*_hbm(*srcs), *lands, *after)
    return outs[:-1], outs[-1]


def _ici_wait(kind, handle, after, name):
    n = (len(handle) - 2) // 2

    def body(*refs):
        for cp in _ici_copies(kind, True, refs[:n], refs[n:2 * n], refs[2 * n], refs[2 * n + 1]):
            cp.wait_send()
            cp.wait_recv()

    outs = pl.pallas_call(
        body, name=name, in_specs=[HBM] * (2 * n) + [SEM, SEM] + [pl.BlockSpec(memory_space=pl.ANY)] * len(after),
        out_specs=[HBM] * (2 * n), out_shape=[pltpu.HBM(a.shape, a.dtype) for a in handle[2:]],
        input_output_aliases={i: i for i in range(2 * n)},
        compiler_params=_split_params(),
    )(*handle[2:], handle[0], handle[1], *after)
    return list(outs[n:])


def _sibling_forward(lands, name):
    n = len(lands)

    def body(*refs):
        outs = refs[n:2 * n]
        send_sems, recv_sems = refs[2 * n:]
        x, y, c = _place()

        def copy(a, j, half):
            px, py = _other_chips(x, y)[j]
            blk = outs[a].at[2 * px + py, half]
            return pltpu.make_async_remote_copy(blk, blk, send_sems.at[3 * a + j], recv_sems.at[3 * a + j],
                                                device_id=(x, y, 1 - c), device_id_type=MESH)

        pairs = [(a, j) for a in range(n) for j in range(3)]
        for a, j in pairs:
            copy(a, j, c).start()
        for a, j in pairs:
            copy(a, j, 1 - c).wait_recv()
        for a, j in pairs:
            copy(a, j, c).wait_send()

    return pl.pallas_call(
        body, name=name, in_specs=[HBM] * n, out_specs=[HBM] * n,
        out_shape=[pltpu.HBM(a.shape, a.dtype) for a in lands],
        input_output_aliases={a: a for a in range(n)},
        scratch_shapes=[_dma_sems(3 * n)] * 2,
        compiler_params=_comm_params(),
    )(*_hbm(*lands))


def _sibling_exchange(grads, name):
    n = len(grads)

    def body(*refs):
        ins, outs = refs[:n], refs[n:2 * n]
        send_sems, recv_sems = refs[2 * n:]
        x, y, c = _place()
        copies = []
        for a in range(n):
            for s in range(N_CHIPS):
                cp = pltpu.make_async_remote_copy(ins[a].at[s, 1 - c], outs[a].at[s], send_sems.at[N_CHIPS * a + s],
                                                  recv_sems.at[N_CHIPS * a + s], device_id=(x, y, 1 - c),
                                                  device_id_type=MESH)
                cp.start()
                copies.append(cp)
        for cp in copies:
            cp.wait_recv()
        for cp in copies:
            cp.wait_send()

    return pl.pallas_call(
        body, name=name, in_specs=[HBM] * n, out_specs=[HBM] * n,
        out_shape=[pltpu.HBM((N_CHIPS, g.shape[2], D), F32) for g in grads],
        scratch_shapes=[_dma_sems(N_CHIPS * n)] * 2,
        compiler_params=_comm_params(),
    )(*_hbm(*grads))


def _add_halves(core, grad, recv, dtype, name):
    h = grad.shape[2]

    def body(c_ref, a_ref, b_ref, o_ref):
        o_ref[...] = (a_ref[0] + b_ref[...]).astype(dtype)

    return pl.pallas_call(
        body, name=name,
        grid_spec=pltpu.PrefetchScalarGridSpec(
            num_scalar_prefetch=1, grid=(N_CHIPS,),
            in_specs=[pl.BlockSpec((1, 1, h, D), lambda s, c: (s, c[0], 0, 0)),
                      pl.BlockSpec((1, h, D), lambda s, c: (s, 0, 0))],
            out_specs=pl.BlockSpec((1, h, D), lambda s, c: (s, 0, 0))),
        out_shape=pltpu.HBM((N_CHIPS, h, D), dtype),
        compiler_params=_params(16, dimension_semantics=_seq()),
    )(core, *_hbm(grad, recv))


N_DEVICES = 2 * N_CHIPS
PEER_FLIPS = [(dx, dy, dc) for dx in (0, 1) for dy in (0, 1) for dc in (0, 1)][1:]


def _small_exchange(pack):
    n = len(PEER_FLIPS)

    def body(p_ref, out_ref, send_sems, recv_sems):
        x, y, c = _place()
        flip = lambda v, d: 1 - v if d else v

        def copy(k, landing):
            px, py, pc = (flip(v, d) for v, d in zip((x, y, c), PEER_FLIPS[k]))
            slab = 4 * px + 2 * py + pc if landing else 4 * x + 2 * y + c
            return pltpu.make_async_remote_copy(p_ref, out_ref.at[slab], send_sems.at[k], recv_sems.at[k],
                                                device_id=(px, py, pc), device_id_type=MESH)

        for k in range(n):
            copy(k, False).start()
        for k in range(n):
            copy(k, True).wait_recv()
        for k in range(n):
            copy(k, False).wait_send()

    return pl.pallas_call(
        body, name="small_exchange", in_specs=[HBM], out_specs=HBM,
        out_shape=pltpu.HBM((N_DEVICES,) + pack.shape, F32),
        scratch_shapes=[_dma_sems(n)] * 2,
        compiler_params=_comm_params(),
    )(*_hbm(pack))


def _sum_chips(slots, first, rest, name):
    h = first.shape[1]

    def body(i_ref, a_ref, b_ref, c_ref, d_ref, o_ref):
        o_ref[...] = ((a_ref[...].astype(F32) + b_ref[...].astype(F32)) + c_ref[...].astype(F32)) + d_ref[...].astype(F32)

    slab = lambda k: pl.BlockSpec((1, h, D), lambda i, ix: (ix[k], 0, 0))
    return pl.pallas_call(
        body, name=name,
        grid_spec=pltpu.PrefetchScalarGridSpec(num_scalar_prefetch=1, grid=(1,),
                                               in_specs=[slab(0), slab(1), slab(2), slab(3)], out_specs=slab(4)),
        out_shape=pltpu.HBM((2, h, D), F32),
        compiler_params=_params(16, dimension_semantics=_seq()),
    )(slots, *_hbm(first, rest, rest, rest))


def _join_halves(halves):
    n = len(halves)

    def body(*refs):
        outs = refs[n:2 * n]
        send_sems, recv_sems = refs[2 * n:]
        x, y, c = _place()

        def copy(a, slab):
            return pltpu.make_async_remote_copy(outs[a].at[slab], outs[a].at[slab], send_sems.at[a], recv_sems.at[a],
                                                device_id=(x, y, 1 - c), device_id_type=MESH)

        for a in range(n):
            copy(a, c).start()
        for a in range(n):
            copy(a, 1 - c).wait_recv()
        for a in range(n):
            copy(a, c).wait_send()

    return pl.pallas_call(
        body, name="join_halves", in_specs=[HBM] * n, out_specs=[HBM] * n,
        out_shape=[pltpu.HBM(h.shape, F32) for h in halves],
        input_output_aliases={a: a for a in range(n)},
        scratch_shapes=[_dma_sems(n)] * 2,
        compiler_params=_comm_params(),
    )(*_hbm(*halves))


def _chip_partials(grads, wire_dtypes, names):
    core = lax.axis_index("c").astype(jnp.int32).reshape(1)
    recv = _sibling_exchange(grads, "sibling_exchange_" + names[0])
    return [_add_halves(core, g, r, dt, "add_halves_" + nm) for g, r, dt, nm in zip(grads, recv, wire_dtypes, names)]


def _finish_reduce(parts, got, names):
    x, y, c = _place()
    others = [2 * px + py for px, py in _other_chips(x, y)]
    own_first = jnp.stack([2 * x + y] + others + [c]).astype(jnp.int32)
    halves = [_sum_chips(own_first, p, q, "sum_chips_" + nm) for p, q, nm in zip(parts, got, names)]
    return [f.reshape(2 * f.shape[1], D) for f in _join_halves(halves)]


def _adamw(w, g, m, v, name):
    rows, cols = w.shape
    if rows % 8 == 0:
        tr = max(t for t in range(8, 257, 8) if rows % t == 0)
        grid, blk = (rows // tr,), pl.BlockSpec((tr, cols), lambda i: (i, 0))
    else:
        grid, blk = (cols // 256,), pl.BlockSpec((rows, 256), lambda i: (0, i))

    def body(w_ref, g_ref, m_ref, v_ref, d_ref, nm_ref, nv_ref):
        d_ref[...], nm_ref[...], nv_ref[...] = _adamw_math(w_ref[...], g_ref[...], m_ref[...], v_ref[...])

    return pl.pallas_call(
        body, name=name, grid=grid,
        in_specs=[blk] * 4, out_specs=[blk] * 3,
        out_shape=[pltpu.HBM(w.shape, F32)] * 3,
        compiler_params=_params(32, dimension_semantics=_seq()),
    )(*_hbm(w, g, m, v))


def _adamw_math(w, g, m, v):
    nm = ADAM_B1 * m + (1.0 - ADAM_B1) * g
    nv = ADAM_B2 * v + (1.0 - ADAM_B2) * (g * g)
    m_hat = nm / (1.0 - ADAM_B1 ** ADAM_STEP)
    v_hat = nv / (1.0 - ADAM_B2 ** ADAM_STEP)
    return -ADAM_LR * (m_hat / (jnp.sqrt(v_hat) + ADAM_EPS) + ADAM_WD * w), nm, nv


SMALL = (("meta_tokens", (N_META, D // N_CHIPS)), ("ln_in_g", (1, D)), ("ln_in_b", (1, D)), ("b_in", (1, D_IN)),
         ("w_gate_lr2", (GATE_RANK, GLA_HEADS * DK // N_CHIPS)), ("b_gate_lr2", (1, GLA_HEADS * DK)),
         ("attn_sinks", (1, SWA_HEADS)),
         ("gla_norm_g", (1, DV)), ("ln1_g", (1, D)), ("ln1_b", (1, D)), ("ln2_g", (1, D)), ("ln2_b", (1, D)))
ROW_META, ROW_B_IN, ROW_TAIL, ROW_WG2 = 0, 22, 25, 32
ROW_LN = dict(ln_in_g=16, ln_in_b=17, ln1_g=18, ln1_b=19, ln2_g=20, ln2_b=21)
TAIL_BG2, TAIL_SINKS, TAIL_GN, TAIL_LOSS = 0, 256, 256 + SWA_HEADS, 256 + SWA_HEADS + DV


def _adamw_small(place, packs, own, params):
    n = len(SMALL)

    def body(place_ref, packs_ref, own_ref, *refs):
        ins, outs, p_ref = refs[:3 * n], refs[3 * n:-1], refs[-1]
        me, c = place_ref[0], place_ref[1]
        total = jnp.where(me == 0, own_ref[...], packs_ref[0])
        for i in range(1, N_DEVICES):
            total = total + jnp.where(me == i, own_ref[...], packs_ref[i])
        p_ref[...] = total
        outs[4 * n][...] = total[ROW_TAIL:ROW_TAIL + 1, :]

        def mine(width, rows):
            part = lambda s: p_ref[rows, s * width:(s + 1) * width]
            return jnp.where(c == 0, part(0), jnp.where(c == 1, part(1), jnp.where(c == 2, part(2), part(3))))

        tail = lambda lo, width: p_ref[ROW_TAIL:ROW_TAIL + 1, lo:lo + width]
        grads = dict(
            meta_tokens=mine(D // N_CHIPS, slice(ROW_META, ROW_META + N_META)),
            b_in=jnp.concatenate([p_ref[ROW_B_IN:ROW_B_IN + 1, :], p_ref[ROW_B_IN + 1:ROW_B_IN + 2, :],
                                  p_ref[ROW_B_IN + 2:ROW_B_IN + 3, 0:D_IN - 2 * D]], axis=1),
            w_gate_lr2=mine(256 // N_CHIPS, slice(ROW_WG2, ROW_WG2 + 16)),
            b_gate_lr2=tail(TAIL_BG2, 256), attn_sinks=tail(TAIL_SINKS, SWA_HEADS), gla_norm_g=tail(TAIL_GN, DV),
            **{k: p_ref[r:r + 1, :] for k, r in ROW_LN.items()})
        for i, (name, _) in enumerate(SMALL):
            g = grads[name]
            outs[4 * i][...] = g
            outs[4 * i + 1][...], outs[4 * i + 2][...], outs[4 * i + 3][...] = _adamw_math(
                ins[3 * i][...], g, ins[3 * i + 1][...], ins[3 * i + 2][...])

    whole = lambda shape: pl.BlockSpec(shape, lambda i, c: (0,) * len(shape))
    outs = pl.pallas_call(
        body, name="adamw_small",
        grid_spec=pltpu.PrefetchScalarGridSpec(
            num_scalar_prefetch=1, grid=(1,),
            in_specs=[whole(packs.shape), whole(own.shape)] + [whole(s) for _, s in SMALL for _ in range(3)],
            out_specs=[whole(s) for _, s in SMALL for _ in range(4)] + [whole((1, D))],
            scratch_shapes=[pltpu.VMEM(own.shape, F32)]),
        out_shape=[pltpu.HBM(s, F32) for _, s in SMALL for _ in range(4)] + [pltpu.HBM((1, D), F32)],
        compiler_params=_params(16, dimension_semantics=_seq()),
    )(place, *_hbm(packs, own, *[a for p in params for a in p]))
    return [outs[4 * i:4 * i + 4] for i in range(n)], outs[4 * n]


def _small_pack(gr):
    names = ["meta_blk"] + list(ROW_LN) + ["b_in_p", "wg2_p", "bg2", "sinks", "gn", "loss"]
    gate_w = GLA_HEADS * DK

    def body(*refs):
        src, out = dict(zip(names, refs)), refs[-1]
        out[...] = jnp.zeros_like(out)
        out[ROW_META:ROW_META + N_META, :] = src["meta_blk"][META_OFF:CH, :]
        for k, r in ROW_LN.items():
            out[r:r + 1, :] = src[k][...]
        for j in range(-(-D_IN // D)):
            width = min(D, D_IN - j * D)
            out[ROW_B_IN + j:ROW_B_IN + j + 1, 0:width] = src["b_in_p"][:, j * D:j * D + width]
        tail = slice(ROW_TAIL, ROW_TAIL + 1)
        out[tail, TAIL_BG2:TAIL_BG2 + gate_w] = src["bg2"][...]
        out[tail, TAIL_SINKS:TAIL_SINKS + SWA_HEADS] = src["sinks"][:, 0:SWA_HEADS]
        out[tail, TAIL_GN:TAIL_GN + DV] = src["gn"][...]
        out[tail, TAIL_LOSS:TAIL_LOSS + 1] = src["loss"][:, 0:1]
        out[ROW_WG2:ROW_WG2 + GATE_RANK, 0:gate_w] = src["wg2_p"][0:GATE_RANK, :]

    arrays = [gr[k] for k in names]
    return pl.pallas_call(
        body, name="small_pack", grid=(1,),
        in_specs=[_acc(a.shape) for a in arrays], out_specs=_acc((SMALL_ROWS, D)),
        out_shape=pltpu.HBM((SMALL_ROWS, D), F32),
        compiler_params=_params(16, dimension_semantics=_seq()),
    )(*_hbm(*arrays))


BIG = ("w_in", "w_out", "w_g", "w_u", "w_d")


def kernel(x, meta_tokens, ln_in_g, ln_in_b, w_in, b_in, w_gate_lr2, b_gate_lr2, attn_sinks, gla_norm_g, w_out, ln1_g, ln1_b, w_ffn_gate, w_ffn_up, w_ffn_down, ln2_g, ln2_b, loss_target, m_meta_tokens, m_ln_in_g, m_ln_in_b, m_w_in, m_b_in, m_w_gate_lr2, m_b_gate_lr2, m_attn_sinks, m_gla_norm_g, m_w_out, m_ln1_g, m_ln1_b, m_w_ffn_gate, m_w_ffn_up, m_w_ffn_down, m_ln2_g, m_ln2_b, v_meta_tokens, v_ln_in_g, v_ln_in_b, v_w_in, v_b_in, v_w_gate_lr2, v_b_gate_lr2, v_attn_sinks, v_gla_norm_g, v_w_out, v_ln1_g, v_ln1_b, v_w_ffn_gate, v_w_ffn_up, v_w_ffn_down, v_ln2_g, v_ln2_b):
    chip = 2 * lax.axis_index("x") + lax.axis_index("y")

    halves = lambda a: a.reshape(2, a.shape[0] // 2, a.shape[1])
    r_in = SHARD_ROWS["w_in"]
    first = [halves(a) for a in (jnp.pad(w_in[0].T.astype(BF16), ((0, W_IN_WIN - r_in), (0, 0))), meta_tokens,
                                 w_gate_lr2[0])]
    rest = [halves(a) for a in (w_out[0].astype(BF16), w_ffn_gate[0].T.astype(BF16), w_ffn_up[0].T.astype(BF16),
                                w_ffn_down[0].astype(BF16))]
    lands = lambda arrs: [(N_CHIPS,) + a.shape for a in arrs]
    first_handle, first_token = _ici_start("gather", first, lands(first), [], "gather_first_start")
    rest_handle, token = _ici_start("gather", rest, lands(rest), [first_token], "gather_rest_start")

    def fetch(handle, shards, after, name):
        got = _sibling_forward(_ici_wait("gather", handle, after, name + "_wait"), name + "_forward")
        return [lax.dynamic_update_index_in_dim(g, s, chip, axis=0) for g, s in zip(got, shards)]

    def fetch_first(after):
        g_in, g_meta, g_wg2 = fetch(first_handle, first, after, "gather_first")
        w_in_t = jnp.pad(g_in.reshape(N_CHIPS, W_IN_WIN, D)[:, :r_in].reshape(D_IN, D), ((0, D_IN_P - D_IN), (0, 0)))
        meta_full = jnp.concatenate([g_meta[s].reshape(N_META, -1) for s in range(N_CHIPS)], axis=1)
        wg2_full = jnp.concatenate([g_wg2[s].reshape(w_gate_lr2.shape[1], -1) for s in range(N_CHIPS)], axis=1)
        return w_in_t, meta_full, wg2_full

    def fetch_rest(after):
        return [g.reshape(-1, D) for g in fetch(rest_handle, rest, after, "gather_rest")]

    sent = {}

    def ship(key, grads, names):
        parts = _chip_partials([g.reshape(N_CHIPS, 2, -1, D) for g in grads], [BF16] * len(grads), names)
        handle, ship_token = _ici_start("scatter", parts, [p.shape for p in parts], [], "scatter_" + key + "_start")
        sent[key] = (parts, handle)
        return ship_token

    def ship_ffn(g):
        return ship("ffn", [g[k] for k in BIG[1:]], list(BIG[1:]))

    def ship_w_in(dw_in_t):
        win_start = [s * r_in // BF16_ROWS * BF16_ROWS for s in range(N_CHIPS)]
        return ship("w_in", [jnp.stack([dw_in_t[st:st + W_IN_WIN] for st in win_start])], ["w_in"])

    dx, gr = _local_step(
        x[0], loss_target[0], ln_in_g, ln_in_b, b_in[0], b_gate_lr2[0], attn_sinks[0], gla_norm_g[0], ln1_g[0],
        ln1_b[0], ln2_g[0], ln2_b[0], token, fetch_first, fetch_rest, ship_ffn, ship_w_in)
    ffn_got = _ici_wait("scatter", sent["ffn"][1], [dx], "scatter_ffn_wait")
    w_in_got = _ici_wait("scatter", sent["w_in"][1], [dx], "scatter_w_in_wait")

    small_own = _small_pack(gr)
    small_all = _small_exchange(small_own)
    red = _finish_reduce(sent["w_in"][0] + sent["ffn"][0], w_in_got + ffn_got, list(BIG))

    big_g = dict(zip(BIG, red))
    big_g["w_in"] = lax.dynamic_slice_in_dim(red[0], chip * (r_in % BF16_ROWS), r_in, axis=0)
    grads = dict(w_in=big_g["w_in"].T[None], w_out=big_g["w_out"][None], w_ffn_gate=big_g["w_g"].T[None],
                 w_ffn_up=big_g["w_u"].T[None], w_ffn_down=big_g["w_d"][None])
    weights = dict(meta_tokens=meta_tokens, ln_in_g=ln_in_g, ln_in_b=ln_in_b, w_in=w_in, b_in=b_in,
                   w_gate_lr2=w_gate_lr2, b_gate_lr2=b_gate_lr2, attn_sinks=attn_sinks, gla_norm_g=gla_norm_g,
                   w_out=w_out, ln1_g=ln1_g, ln1_b=ln1_b, w_ffn_gate=w_ffn_gate, w_ffn_up=w_ffn_up,
                   w_ffn_down=w_ffn_down, ln2_g=ln2_g, ln2_b=ln2_b)
    m_in = dict(meta_tokens=m_meta_tokens, ln_in_g=m_ln_in_g, ln_in_b=m_ln_in_b, w_in=m_w_in, b_in=m_b_in,
                w_gate_lr2=m_w_gate_lr2, b_gate_lr2=m_b_gate_lr2, attn_sinks=m_attn_sinks, gla_norm_g=m_gla_norm_g,
                w_out=m_w_out, ln1_g=m_ln1_g, ln1_b=m_ln1_b, w_ffn_gate=m_w_ffn_gate, w_ffn_up=m_w_ffn_up,
                w_ffn_down=m_w_ffn_down, ln2_g=m_ln2_g, ln2_b=m_ln2_b)
    v_in = dict(meta_tokens=v_meta_tokens, ln_in_g=v_ln_in_g, ln_in_b=v_ln_in_b, w_in=v_w_in, b_in=v_b_in,
                w_gate_lr2=v_w_gate_lr2, b_gate_lr2=v_b_gate_lr2, attn_sinks=v_attn_sinks, gla_norm_g=v_gla_norm_g,
                w_out=v_w_out, ln1_g=v_ln1_g, ln1_b=v_ln1_b, w_ffn_gate=v_w_ffn_gate, w_ffn_up=v_w_ffn_up,
                w_ffn_down=v_w_ffn_down, ln2_g=v_ln2_g, ln2_b=v_ln2_b)
    names = list(weights)
    big_names = ("w_in", "w_out", "w_ffn_gate", "w_ffn_up", "w_ffn_down")

    delta, new_m, new_v = {}, {}, {}
    for k, kk in zip(big_names, BIG):
        flip = (lambda a: a.T) if kk in ("w_in", "w_g", "w_u") else (lambda a: a)
        d_, m_, v_ = _adamw(flip(weights[k][0]), big_g[kk], flip(m_in[k][0]), flip(v_in[k][0]), "adamw_" + k)
        delta[k], new_m[k], new_v[k] = (flip(t)[None] for t in (d_, m_, v_))
    small_in = [tuple(src[k].reshape(shape) for src in (weights, m_in, v_in)) for k, shape in SMALL]
    place = jnp.stack([2 * chip + lax.axis_index("c"), chip]).astype(jnp.int32)
    small_out, tail_row = _adamw_small(place, small_all, small_own, small_in)
    for (k, _), results in zip(SMALL, small_out):
        grads[k], delta[k], new_m[k], new_v[k] = (r.reshape(weights[k].shape) for r in results)

    return (tail_row[0, TAIL_LOSS], dx[None], *[grads[k] for k in names], *[delta[k] for k in names], *[new_m[k] for k in names],
            *[new_v[k] for k in names])
```

```python
import jax
import jax.numpy as jnp
from jax import lax
from jax.experimental import pallas as pl
from jax.experimental.pallas import tpu as pltpu

F32 = jnp.float32
BF16 = jnp.bfloat16
MESH = pl.DeviceIdType.MESH

D = 1024
SEQ = 4096
N_META = 16
SWA_HEADS, SWA_KV_HEADS, DH = 8, 2, 64
WINDOW = 128
GLA_HEADS, DK, DV = 4, 64, 128
GLA_TAU = 16.0
CH = 64
D_FF = 2816
D_IN = 2320
LN_EPS = 1e-5
RMS_EPS = 1e-6
ALPHA = 2.0 ** 0.25
NEG = -1e30
ADAM_LR, ADAM_B1, ADAM_B2, ADAM_EPS, ADAM_WD, ADAM_STEP = 0.001, 0.9, 0.999, 1e-8, 0.01, 10
O_QS, O_KS, O_VS, O_QG, O_KG, O_VG, O_RG, O_LR = 0, 512, 640, 768, 1024, 1280, 1792, 2304

LANE = 128
BLK = WINDOW
GATE_RANK = 16
D_IN_P = D_IN + LANE - GATE_RANK
META_OFF = CH - N_META
HEAD_POS = (0, 4, 1, 5, 2, 6, 3, 7)
LN_ROWS = 512
TOKEN = (8, LANE)
N_CHIPS = 4
SHARD_ROWS = dict(w_in=D_IN // N_CHIPS, w_out=D // N_CHIPS, w_g=D_FF // N_CHIPS, w_u=D_FF // N_CHIPS,
                  w_d=D_FF // N_CHIPS)
SMALL_ROWS = 48
BF16_ROWS = 16
W_IN_WIN = -(-SHARD_ROWS["w_in"] // (2 * BF16_ROWS)) * 2 * BF16_ROWS
VMEM_CAP_MB = 64
VMEM_SPARE_MB = 6


def _lp():
    return SEQ + BLK


def _row_tile(cap):
    lp = _lp()
    return max(t for t in range(16, cap + 1, 16) if lp % t == 0)


def _params(vmem_mb, **kw):
    assert vmem_mb <= VMEM_CAP_MB - VMEM_SPARE_MB
    return pltpu.CompilerParams(vmem_limit_bytes=vmem_mb << 20, **kw)


def _seq(n=1):
    return ("arbitrary",) * n


def _const(shape):
    return pl.BlockSpec(shape, lambda *_: (0,) * len(shape), pipeline_mode=pl.Buffered(1))


def _acc(shape):
    return pl.BlockSpec(shape, lambda *_: (0,) * len(shape))


def _rows(tm, width):
    return pl.BlockSpec((tm, width), lambda i: (i, 0))


def _dot(a, b):
    return jnp.dot(a.astype(BF16), b.astype(BF16), preferred_element_type=F32)


def _dot_nt(a, b):
    return lax.dot_general(a.astype(BF16), b.astype(BF16), (((1,), (1,)), ((), ())), preferred_element_type=F32)


def _dot_tn(a, b):
    return lax.dot_general(a.astype(BF16), b.astype(BF16), (((0,), (0,)), ((), ())), preferred_element_type=F32)


def _dot_exact(a, b):
    return jnp.dot(a, b, precision=lax.Precision.HIGHEST, preferred_element_type=F32)


def _ln_stats(x):
    mu = jnp.mean(x, axis=-1, keepdims=True)
    xc = x - mu
    rstd = lax.rsqrt(jnp.mean(xc * xc, axis=-1, keepdims=True) + LN_EPS)
    return xc * rstd, rstd


def _ln_bwd(dy, xhat, rstd, g):
    dxh = dy * g
    return rstd * (dxh - jnp.mean(dxh, axis=-1, keepdims=True) - xhat * jnp.mean(dxh * xhat, axis=-1, keepdims=True))


def _sigmoid(x):
    return 1.0 / (1.0 + jnp.exp(-x))


def _iota(shape, dim):
    return lax.broadcasted_iota(jnp.int32, shape, dim)


def _hbm(*arrays):
    return tuple(pltpu.with_memory_space_constraint(a, pltpu.HBM) for a in arrays)


def _ln_in_fwd_real(x, g, b, token):
    tr = min(LN_ROWS, SEQ)

    def body(x_ref, g_ref, b_ref, token_ref, h_ref):
        xhat, _ = _ln_stats(x_ref[...])
        h_ref[...] = xhat * g_ref[...] + b_ref[...]

    return pl.pallas_call(
        body, name="ln_in_fwd", grid=(SEQ // tr,),
        in_specs=[_rows(tr, D), _const((1, D)), _const((1, D)), _const(TOKEN)],
        out_specs=_rows(tr, D),
        out_shape=pltpu.HBM((_lp(), D), F32),
        compiler_params=_params(32, dimension_semantics=_seq()),
    )(*_hbm(x, g, b), token)


def _ln_in_fwd_meta(h_real, meta_ext, g, b):
    def meta_body(m_ref, g_ref, b_ref, real_ref, h_ref):
        xhat, _ = _ln_stats(m_ref[...])
        h_ref[...] = xhat * g_ref[...] + b_ref[...]

    return pl.pallas_call(
        meta_body, name="ln_in_fwd_meta", grid=(1,),
        in_specs=[_const((BLK, D)), _const((1, D)), _const((1, D)), pl.BlockSpec(memory_space=pl.ANY)],
        out_specs=pl.BlockSpec((BLK, D), lambda i: (SEQ // BLK, 0)),
        out_shape=pltpu.HBM((_lp(), D), F32),
        input_output_aliases={3: 0},
        compiler_params=_params(16, dimension_semantics=_seq()),
    )(*_hbm(meta_ext, g, b, h_real))


def _in_proj(h0, w_in_t, b_in_p, wg2_p, bg2):
    tm = _row_tile(384)
    lp = _lp()
    widths = (512, 128, 128, 256, 256, 512, 512, 128)
    offs = (O_QS, O_KS, O_VS, O_QG, O_KG, O_VG, O_RG, O_LR)

    def body(h_ref, w_ref, b_ref, wg2_ref, bg2_ref, *outs):
        proj = _dot_nt(h_ref[...], w_ref[...]) + b_ref[...]
        for pos, h in enumerate(HEAD_POS):
            outs[0][:, pos * DH:(pos + 1) * DH] = proj[:, O_QS + h * DH:O_QS + (h + 1) * DH]
        for o_ref, off, wd in zip(outs[1:8], offs[1:], widths[1:]):
            o_ref[...] = proj[:, off:off + wd]
        outs[8][...] = _dot(proj[:, O_LR:O_LR + LANE], wg2_ref[...]) + bg2_ref[...]

    return pl.pallas_call(
        body, name="in_proj", grid=(lp // tm,),
        in_specs=[_rows(tm, D), _const((D_IN_P, D)), _const((1, D_IN_P)), _const((LANE, 256)), _const((1, 256))],
        out_specs=[_rows(tm, w) for w in widths] + [_rows(tm, 256)],
        out_shape=[pltpu.HBM((lp, w), F32) for w in widths] + [pltpu.HBM((lp, 256), F32)],
        compiler_params=_params(40, dimension_semantics=_seq()),
    )(*_hbm(h0, w_in_t, b_in_p, wg2_p, bg2))


def _swa_masks(n):
    nb = SEQ // BLK
    is_meta = n == nb
    ri = _iota((BLK, BLK), 0)
    cj = _iota((BLK, BLK), 1)
    meta_col = ((cj >= META_OFF) & (cj < CH)).astype(jnp.int32)
    meta_q = meta_col * ((cj <= ri) & (ri < CH)).astype(jnp.int32)
    valid_m = jnp.where(is_meta, meta_q, meta_col) > 0
    dist_m = jnp.where(is_meta, ri - cj, n * BLK + ri + CH - cj).astype(F32)
    valid_p = jnp.where((n >= 1) & (n < nb), (cj > ri).astype(jnp.int32), 0) > 0
    dist_p = (ri + BLK - cj).astype(F32)
    valid_c = jnp.where(n < nb, (cj <= ri).astype(jnp.int32), 0) > 0
    dist_c = (ri - cj).astype(F32)
    return (dist_m, dist_p, dist_c), (valid_m, valid_p, valid_c)


def _swa_bias(n):
    dists, valids = _swa_masks(n)
    return (jnp.concatenate([-d for d in dists], axis=1),
            jnp.concatenate([jnp.where(v, 0.0, NEG) for v in valids], axis=1))


def _swa_half(ref, pos, scale=1.0):
    col = ref[:, (pos // 2) * LANE:(pos // 2 + 1) * LANE]
    lane = _iota((BLK, LANE), 1)
    mine = lane < DH if pos % 2 == 0 else lane >= DH
    return jnp.where(mine, col * scale, 0.0).astype(BF16)


def _swa_merge(even, odd):
    return jnp.where(_iota((BLK, LANE), 1) < DH, even, odd)


def _swa_softmax(t, sink):
    m = jnp.maximum(jnp.max(t, axis=-1, keepdims=True), sink)
    e = jnp.exp(t - m)
    e_sink = jnp.exp(sink - m)
    inv = 1.0 / (jnp.sum(e, axis=-1, keepdims=True) + e_sink)
    return e * inv, e_sink * inv


def _swa_kv_specs(width):
    nb = SEQ // BLK
    return [pl.BlockSpec((BLK, width), lambda n: (nb, 0)),
            pl.BlockSpec((BLK, width), lambda n: (jnp.clip(n - 1, 0, nb - 1), 0)),
            pl.BlockSpec((BLK, width), lambda n: (jnp.minimum(n, nb), 0))]


def _swa_fwd(sinks, qs, ks, vs):
    nb = SEQ // BLK
    heads = range(SWA_HEADS)

    def body(sink_ref, q_ref, km_ref, kp_ref, kc_ref, vm_ref, vp_ref, vc_ref, o_ref):
        negdist, maskbias = _swa_bias(pl.program_id(0))
        k_all = jnp.concatenate([km_ref[...], kp_ref[...], kc_ref[...]], axis=0).astype(BF16)
        v_all = jnp.concatenate([vm_ref[...], vp_ref[...], vc_ref[...]], axis=0).astype(BF16)
        q = [_swa_half(q_ref, pos, DH ** -0.5) for pos in heads]
        t = [_dot_nt(q[pos], k_all) + (2.0 ** -(HEAD_POS[pos] + 1) * negdist + maskbias) for pos in heads]
        p = [_swa_softmax(t[pos], sink_ref[HEAD_POS[pos]])[0].astype(BF16) for pos in heads]
        o = [_dot(p[pos], v_all) for pos in heads]
        for col in range(SWA_HEADS // 2):
            o_ref[:, col * LANE:(col + 1) * LANE] = _swa_merge(o[2 * col], o[2 * col + 1])

    kvw = SWA_KV_HEADS * DH
    return pl.pallas_call(
        body, name="swa_fwd", grid=(nb + 1,),
        in_specs=[pl.BlockSpec(memory_space=pltpu.SMEM), _rows(BLK, SWA_HEADS * DH)] + _swa_kv_specs(kvw) + _swa_kv_specs(kvw),
        out_specs=_rows(BLK, SWA_HEADS * DH),
        out_shape=pltpu.HBM((_lp(), SWA_HEADS * DH), F32),
        compiler_params=_params(16, dimension_semantics=_seq()),
    )(sinks, *_hbm(qs, ks, ks, ks, vs, vs, vs))


GLA_PER_STEP = BLK // CH


def _gla_block(s):
    nb = SEQ // BLK
    return jnp.where(s == 0, nb, s - 1)


def _gla_rowmask(s):
    ri = _iota((BLK, 1), 0)
    m = jnp.where(s == 0, ((ri >= META_OFF) & (ri < CH)).astype(jnp.int32), 1)
    return (m > 0).astype(F32) + jnp.zeros((BLK, 1), F32)


def _gla_chunk_masks():
    r, c = _iota((BLK, BLK), 0), _iota((BLK, BLK), 1)
    same = ((r < CH) & (c < CH)) | ((r >= CH) & (c >= CH))
    return same & (r >= c), same & (r <= c), same


def _gla_decay(z, rmask):
    log_g = (jnp.minimum(z, 0.0) - jnp.log1p(jnp.exp(-jnp.abs(z)))) * (rmask / GLA_TAU)
    lower, _, same = _gla_chunk_masks()
    return _dot_exact(lower.astype(F32), log_g), _dot_exact(same.astype(F32), log_g)


def _gla_slices(c, h):
    return slice(c * CH, (c + 1) * CH), slice(h * DK, (h + 1) * DK), slice(h * DV, (h + 1) * DV)


def _gla_fwd(qg, kg, vg, z):
    steps = SEQ // BLK + 1
    kw, vw = GLA_HEADS * DK, GLA_HEADS * DV
    pairs = [(c, h) for c in range(GLA_PER_STEP) for h in range(GLA_HEADS)]

    def body(q_ref, k_ref, v_ref, z_ref, o_ref, st_ref, st):
        s = pl.program_id(0)

        @pl.when(s == 0)
        def _():
            st[...] = jnp.zeros_like(st)

        rmask = _gla_rowmask(s)
        b, b_last = _gla_decay(z_ref[...], rmask)
        q = q_ref[...] * (rmask * DK ** -0.5)
        k = k_ref[...] * rmask
        v = v_ref[...] * rmask
        qe = q * jnp.exp(b)
        ke = k * jnp.exp(-b)
        kd = k * jnp.exp(b_last - b)
        e_last = jnp.exp(b_last)
        causal = _iota((CH, CH), 0) >= _iota((CH, CH), 1)
        a, upd, intra = {}, {}, {}
        for c, h in pairs:
            rows, ks, vs_ = _gla_slices(c, h)
            a[c, h] = jnp.where(causal, _dot_nt(qe[rows, ks], ke[rows, ks]), 0.0)
            upd[c, h] = _dot_tn(v[rows, vs_], kd[rows, ks])
        for c, h in pairs:
            rows, ks, vs_ = _gla_slices(c, h)
            intra[c, h] = _dot(a[c, h], v[rows, vs_])
        state = st[...]
        for c in range(GLA_PER_STEP):
            st_ref[0, c] = state
            for h in range(GLA_HEADS):
                rows, ks, vs_ = _gla_slices(c, h)
                o_ref[rows, vs_] = intra[c, h] + _dot_nt(qe[rows, ks], state[:, ks])
            state = state * e_last[c * CH:c * CH + 1] + jnp.concatenate([upd[c, h] for h in range(GLA_HEADS)], axis=1)
        st[...] = state

    blk = lambda w: pl.BlockSpec((BLK, w), lambda s: (_gla_block(s), 0))
    return pl.pallas_call(
        body, name="gla_fwd", grid=(steps,),
        in_specs=[blk(kw), blk(kw), blk(vw), blk(kw)],
        out_specs=[blk(vw), pl.BlockSpec((1, GLA_PER_STEP, DV, kw), lambda s: (s, 0, 0, 0))],
        out_shape=[pltpu.HBM((_lp(), vw), F32), pltpu.HBM((steps, GLA_PER_STEP, DV, kw), F32)],
        scratch_shapes=[pltpu.VMEM((DV, kw), F32)],
        compiler_params=_params(16, dimension_semantics=_seq()),
    )(*_hbm(qg, kg, vg, z))


def _post_mix(o_s, o_gla, r_g, h0, gn4, w_out, g1, b1):
    tm = _row_tile(384)
    lp = _lp()

    def body(os_ref, og_ref, r_ref, h0_ref, gn_ref, w_ref, g_ref, b_ref, o_ref, pre_ref, h1_ref):
        for pos, h in enumerate(HEAD_POS):
            o_ref[:, h * DH:(h + 1) * DH] = os_ref[:, pos * DH:(pos + 1) * DH].astype(BF16)
        for h in range(GLA_HEADS):
            hs = slice(h * DV, (h + 1) * DV)
            xg = og_ref[:, hs]
            n = xg * lax.rsqrt(jnp.mean(xg * xg, axis=-1, keepdims=True) + RMS_EPS) * gn_ref[...]
            r = r_ref[:, hs]
            o_ref[:, 512 + h * DV:512 + (h + 1) * DV] = (n * (r * _sigmoid(r))).astype(BF16)
        pre = ALPHA * h0_ref[...] + _dot(o_ref[...], w_ref[...])
        pre_ref[...] = pre
        xhat, _ = _ln_stats(pre)
        h1_ref[...] = xhat * g_ref[...] + b_ref[...]

    return pl.pallas_call(
        body, name="post_mix", grid=(lp // tm,),
        in_specs=[_rows(tm, 512), _rows(tm, 512), _rows(tm, 512), _rows(tm, D), _const((1, DV)), _const((D, D)),
                  _const((1, D)), _const((1, D))],
        out_specs=[_rows(tm, D), _rows(tm, D), _rows(tm, D)],
        out_shape=[pltpu.HBM((lp, D), BF16), pltpu.HBM((lp, D), F32),
                   pltpu.HBM((lp, D), F32)],
        compiler_params=_params(32, dimension_semantics=_seq()),
    )(*_hbm(o_s, o_gla, r_g, h0, gn4, w_out, g1, b1))


def _ffn_fwd_loss_bwd(h1, wg_t, wu_t, wd, target, g2, b2):
    lp = _lp()
    tm = max(t for t in range(BLK, 384 + 1, BLK) if lp % t == 0)
    steps = lp // tm
    last_blk = SEQ // BLK - 1
    half = D_FF // 2
    n_t = tm // BLK

    def body(*refs):
        h_ref, wg_ref, wu_ref, wd_ref = refs[:4]
        t_refs = refs[4:4 + n_t]
        g2_ref, b2_ref, a_ref, dgate_ref, dup_ref, dp_ref, loss_ref, dg_ref, db_ref, g_s, u_s, acc = refs[4 + n_t:]
        i = pl.program_id(0)

        @pl.when(i == 0)
        def _():
            acc[...] = jnp.zeros_like(acc)
            dg_ref[...] = jnp.zeros_like(dg_ref)
            db_ref[...] = jnp.zeros_like(db_ref)

        h = h_ref[...]
        hb = h.astype(BF16)
        pre = ALPHA * h
        for j in range(2):
            cols = slice(j * half, (j + 1) * half)
            g = _dot_nt(hb, wg_ref[cols, :])
            u = _dot_nt(hb, wu_ref[cols, :])
            g_s[:, cols] = g
            u_s[:, cols] = u
            pre = pre + _dot(g * _sigmoid(g) * u, wd_ref[cols, :])
        xhat, rstd = _ln_stats(pre)
        real = i * tm + _iota((tm, 1), 0) < SEQ
        target_rows = jnp.concatenate([t[...] for t in t_refs], axis=0)
        diff = jnp.where(real, xhat * g2_ref[...] + b2_ref[...] - target_rows, 0.0)
        acc[...] += jnp.sum(diff * diff, axis=0, keepdims=True)
        dy = diff * (1.0 / D)
        dpre = _ln_bwd(dy, xhat, rstd, g2_ref[...])
        dp_ref[...] = dpre
        dg_ref[...] += jnp.sum(dy * xhat, axis=0, keepdims=True)
        db_ref[...] += jnp.sum(dy, axis=0, keepdims=True)
        dpb = dpre.astype(BF16)
        for j in range(2):
            cols = slice(j * half, (j + 1) * half)
            g, u = g_s[:, cols], u_s[:, cols]
            sg = _sigmoid(g)
            silu = g * sg
            da = _dot_nt(dpb, wd_ref[cols, :])
            a_ref[:, cols] = (silu * u).astype(BF16)
            dgate_ref[:, cols] = (da * u * (sg * (1.0 + g * (1.0 - sg)))).astype(BF16)
            dup_ref[:, cols] = (da * silu).astype(BF16)

        @pl.when(i == steps - 1)
        def _():
            loss_ref[...] = jnp.zeros_like(loss_ref) + (0.5 / D) * jnp.sum(acc[...], axis=1, keepdims=True)

    t_spec = lambda k: pl.BlockSpec((BLK, D), lambda i: (jnp.minimum(i * n_t + k, last_blk), 0))
    return pl.pallas_call(
        body, name="ffn_fwd_loss_bwd", grid=(steps,),
        in_specs=[_rows(tm, D), _const((D_FF, D)), _const((D_FF, D)), _const((D_FF, D))]
        + [t_spec(k) for k in range(n_t)] + [_const((1, D)), _const((1, D))],
        out_specs=[_rows(tm, D_FF), _rows(tm, D_FF), _rows(tm, D_FF), _rows(tm, D), _acc((1, LANE)), _acc((1, D)),
                   _acc((1, D))],
        out_shape=[pltpu.HBM((lp, D_FF), BF16)] * 3 + [pltpu.HBM((lp, D), F32), pltpu.HBM((1, LANE), F32),
                                                         pltpu.HBM((1, D), F32), pltpu.HBM((1, D), F32)],
        scratch_shapes=[pltpu.VMEM((tm, D_FF), F32), pltpu.VMEM((tm, D_FF), F32), pltpu.VMEM((1, D), F32)],
        compiler_params=_params(58, dimension_semantics=_seq()),
    )(*_hbm(h1, wg_t, wu_t, wd, *[target] * n_t, g2, b2))


def _ffn_out_bwd(dpre2, dgate, dup, pre1, wg_t, wu_t, g1, w_out, o_gla, r_g, gn4):
    tm = _row_tile(384)
    lp = _lp()

    def body(dp_ref, dg_ref, du_ref, p1_ref, wg_ref, wu_ref, g1_ref, w_ref, og_ref, r_ref, gn_ref,
             dp1_ref, dg1_ref, db1_ref, dos_ref, dog_ref, dr_ref, dgn_ref):
        @pl.when(pl.program_id(0) == 0)
        def _():
            for acc_ref in (dg1_ref, db1_ref, dgn_ref):
                acc_ref[...] = jnp.zeros_like(acc_ref)

        dh1 = ALPHA * dp_ref[...] + _dot(dg_ref[...], wg_ref[...]) + _dot(du_ref[...], wu_ref[...])
        xhat, rstd1 = _ln_stats(p1_ref[...])
        dpre1 = _ln_bwd(dh1, xhat, rstd1, g1_ref[...])
        dp1_ref[...] = dpre1
        dg1_ref[...] += jnp.sum(dh1 * xhat, axis=0, keepdims=True)
        db1_ref[...] += jnp.sum(dh1, axis=0, keepdims=True)

        do = _dot_nt(dpre1, w_ref[...])
        for pos, h in enumerate(HEAD_POS):
            dos_ref[:, pos * DH:(pos + 1) * DH] = do[:, h * DH:(h + 1) * DH]
        gn = gn_ref[...]
        for h in range(GLA_HEADS):
            hs = slice(h * DV, (h + 1) * DV)
            xg = og_ref[:, hs]
            rstd = lax.rsqrt(jnp.mean(xg * xg, axis=-1, keepdims=True) + RMS_EPS)
            nx = xg * rstd
            r = r_ref[:, hs]
            sr = _sigmoid(r)
            d_o = do[:, 512 + h * DV:512 + (h + 1) * DV]
            dr_ref[:, hs] = d_o * (nx * gn) * (sr * (1.0 + r * (1.0 - sr)))
            dn = d_o * (r * sr)
            dgn_ref[...] += jnp.sum(dn * nx, axis=0, keepdims=True)
            dnx = dn * gn
            dog_ref[:, hs] = rstd * (dnx - nx * jnp.mean(dnx * nx, axis=-1, keepdims=True))

    return pl.pallas_call(
        body, name="ffn_out_bwd", grid=(lp // tm,),
        in_specs=[_rows(tm, D), _rows(tm, D_FF), _rows(tm, D_FF), _rows(tm, D), _const((D_FF, D)), _const((D_FF, D)),
                  _const((1, D)), _const((D, D)), _rows(tm, 512), _rows(tm, 512), _const((1, DV))],
        out_specs=[_rows(tm, D), _acc((1, D)), _acc((1, D)), _rows(tm, 512), _rows(tm, 512), _rows(tm, 512),
                   _acc((1, DV))],
        out_shape=[pltpu.HBM((lp, D), F32), pltpu.HBM((1, D), F32), pltpu.HBM((1, D), F32)]
        + [pltpu.HBM((lp, 512), F32)] * 3 + [pltpu.HBM((1, DV), F32)],
        compiler_params=_params(48, dimension_semantics=_seq()),
    )(*_hbm(dpre2, dgate, dup, pre1, wg_t, wu_t, g1, w_out, o_gla, r_g, gn4))


def _atb(a, b, name):
    lp = _lp()
    tm = _row_tile(1408)
    n, w = a.shape[1], b.shape[1]
    bw = 512 if n * w * 4 > (4 << 20) else w

    def body(a_ref, b_ref, o_ref):
        @pl.when(pl.program_id(1) == 0)
        def _():
            o_ref[...] = jnp.zeros_like(o_ref)

        o_ref[...] += _dot_tn(a_ref[...], b_ref[...])

    return pl.pallas_call(
        body, name=name, grid=(w // bw, lp // tm),
        in_specs=[pl.BlockSpec((tm, n), lambda j, k: (k, 0)), pl.BlockSpec((tm, bw), lambda j, k: (k, j))],
        out_specs=pl.BlockSpec((n, bw), lambda j, k: (0, j)),
        out_shape=pltpu.HBM((n, w), F32),
        compiler_params=_params(48, dimension_semantics=_seq(2)),
    )(*_hbm(a, b))


def _gla_bwd(qg, kg, vg, z, do_gla, st_all, token):
    steps = SEQ // BLK + 1
    kw, vw = GLA_HEADS * DK, GLA_HEADS * DV
    pairs = [(c, h) for c in range(GLA_PER_STEP) for h in range(GLA_HEADS)]
    heads = range(GLA_HEADS)

    def body(q_ref, k_ref, v_ref, z_ref, do_ref, st_ref, token_ref, dq_ref, dk_ref, dv_ref, dz_ref, dst):
        @pl.when(pl.program_id(0) == 0)
        def _():
            dst[...] = jnp.zeros_like(dst)

        rmask = _gla_rowmask(steps - 1 - pl.program_id(0))
        zz = z_ref[...]
        b, b_last = _gla_decay(zz, rmask)
        e_b, e_nb, e_kd, e_last = jnp.exp(b), jnp.exp(-b), jnp.exp(b_last - b), jnp.exp(b_last)
        q = q_ref[...] * (rmask * DK ** -0.5)
        k = k_ref[...] * rmask
        v = v_ref[...] * rmask
        qe, ke, kd = q * e_b, k * e_nb, k * e_kd
        d_o = do_ref[...]
        causal = _iota((CH, CH), 0) >= _iota((CH, CH), 1)
        a, da, dqe, dke, dv_intra, carry = {}, {}, {}, {}, {}, {}
        for c, h in pairs:
            rows, ks, vs_ = _gla_slices(c, h)
            a[c, h] = jnp.where(causal, _dot_nt(qe[rows, ks], ke[rows, ks]), 0.0)
            da[c, h] = jnp.where(causal, _dot_nt(d_o[rows, vs_], v[rows, vs_]), 0.0)
            carry[c, h] = _dot_tn(d_o[rows, vs_], qe[rows, ks])
        for c, h in pairs:
            rows, ks, vs_ = _gla_slices(c, h)
            dqe[c, h] = _dot(d_o[rows, vs_], st_ref[0, c][:, ks]) + _dot(da[c, h], ke[rows, ks])
            dke[c, h] = _dot_tn(da[c, h], qe[rows, ks])
            dv_intra[c, h] = _dot_tn(a[c, h], d_o[rows, vs_])
        dstate = dst[...]
        dkd, db_decay = {}, {}
        for c in reversed(range(GLA_PER_STEP)):
            for h in heads:
                rows, ks, vs_ = _gla_slices(c, h)
                dkd[c, h] = _dot(v[rows, vs_], dstate[:, ks])
                dv_ref[rows, vs_] = dv_intra[c, h] + _dot_nt(kd[rows, ks], dstate[:, ks])
            chunk_last = e_last[c * CH:c * CH + 1]
            db_decay[c] = jnp.sum(dstate * st_ref[0, c], axis=0, keepdims=True) * chunk_last
            dstate = dstate * chunk_last + jnp.concatenate([carry[c, h] for h in heads], axis=1)
        dst[...] = dstate
        rows_of = lambda parts: jnp.concatenate(
            [jnp.concatenate([parts[c, h] for h in heads], axis=1) for c in range(GLA_PER_STEP)], axis=0)
        dqe_all, dke_all, dkd_all = rows_of(dqe), rows_of(dke), rows_of(dkd)
        dq_ref[...] = dqe_all * e_b * (rmask * DK ** -0.5)
        dk_ref[...] = (dke_all * e_nb + dkd_all * e_kd) * rmask
        dkd_kd = dkd_all * kd
        db = dqe_all * qe - dke_all * ke - dkd_kd
        _, upper, same = _gla_chunk_masks()
        decay_rows = jnp.concatenate([jnp.broadcast_to(db_decay[c], (CH, kw)) for c in range(GLA_PER_STEP)], axis=0)
        dlog_g = _dot_exact(upper.astype(F32), db) + _dot_exact(same.astype(F32), dkd_kd) + decay_rows
        dz_ref[...] = dlog_g * (rmask / GLA_TAU) * _sigmoid(-zz)

    blk = lambda w: pl.BlockSpec((BLK, w), lambda s: (_gla_block(steps - 1 - s), 0))
    return pl.pallas_call(
        body, name="gla_bwd", grid=(steps,),
        in_specs=[blk(kw), blk(kw), blk(vw), blk(kw), blk(vw),
                  pl.BlockSpec((1, GLA_PER_STEP, DV, kw), lambda s: (steps - 1 - s, 0, 0, 0)), _const(TOKEN)],
        out_specs=[blk(kw), blk(kw), blk(vw), blk(kw)],
        out_shape=[pltpu.HBM((_lp(), kw), F32), pltpu.HBM((_lp(), kw), F32),
                   pltpu.HBM((_lp(), vw), F32), pltpu.HBM((_lp(), kw), F32)],
        scratch_shapes=[pltpu.VMEM((DV, kw), F32)],
        compiler_params=_params(16, dimension_semantics=_seq()),
    )(*_hbm(qg, kg, vg, z, do_gla, st_all), token)


def _swa_bwd(sinks, qs, ks, vs, do_s, token):
    nb = SEQ // BLK
    kvw = SWA_KV_HEADS * DH
    scale = DH ** -0.5
    heads = range(SWA_HEADS)

    def body(sink_ref, q_ref, km_ref, kp_ref, kc_ref, vm_ref, vp_ref, vc_ref, do_ref, token_ref,
             dq_ref, dk_ref, dv_ref, dsink_ref, carry_k, carry_v, meta_k, meta_v):
        n = pl.program_id(0)

        @pl.when(n == 0)
        def _():
            for r in (carry_k, carry_v, meta_k, meta_v):
                r[...] = jnp.zeros_like(r)
            dsink_ref[...] = jnp.zeros_like(dsink_ref)

        @pl.when(n <= nb)
        def _():
            negdist, maskbias = _swa_bias(n)
            lane = _iota((1, LANE), 1)
            k_all = jnp.concatenate([km_ref[...], kp_ref[...], kc_ref[...]], axis=0).astype(BF16)
            v_all = jnp.concatenate([vm_ref[...], vp_ref[...], vc_ref[...]], axis=0).astype(BF16)
            q = [_swa_half(q_ref, pos, scale) for pos in heads]
            d_o = [_swa_half(do_ref, pos) for pos in heads]
            t = [_dot_nt(q[pos], k_all) + (2.0 ** -(HEAD_POS[pos] + 1) * negdist + maskbias) for pos in heads]
            dp = [_dot_nt(d_o[pos], v_all) for pos in heads]
            soft = [_swa_softmax(t[pos], sink_ref[HEAD_POS[pos]]) for pos in heads]
            p = [s[0] for s in soft]
            delta = [jnp.sum(p[pos] * dp[pos], axis=-1, keepdims=True) for pos in heads]
            ds = [(p[pos] * (dp[pos] - delta[pos])).astype(BF16) for pos in heads]
            dq = [_dot(ds[pos], k_all) for pos in heads]
            for col in range(SWA_HEADS // 2):
                dq_ref[:, col * LANE:(col + 1) * LANE] = scale * _swa_merge(dq[2 * col], dq[2 * col + 1])
            dsink = jnp.zeros((1, LANE), F32)
            for pos in heads:
                dsink = dsink + jnp.where(lane == HEAD_POS[pos],
                                          -jnp.sum(soft[pos][1] * delta[pos], axis=0, keepdims=True), 0.0)
            dsink_ref[...] += dsink
            dk3 = _dot_tn(jnp.concatenate(q, axis=0), jnp.concatenate(ds, axis=0)).T
            dv3 = _dot_tn(jnp.concatenate(d_o, axis=0), jnp.concatenate([x.astype(BF16) for x in p], axis=0)).T
            meta_k[...] += dk3[0:BLK]
            meta_v[...] += dv3[0:BLK]
            dk_ref[...] = carry_k[...] + dk3[BLK:2 * BLK]
            dv_ref[...] = carry_v[...] + dv3[BLK:2 * BLK]
            carry_k[...] = dk3[2 * BLK:3 * BLK]
            carry_v[...] = dv3[2 * BLK:3 * BLK]

        @pl.when(n == nb + 1)
        def _():
            dk_ref[...] = meta_k[...]
            dv_ref[...] = meta_v[...]

    kv_out = pl.BlockSpec((BLK, kvw), lambda n: (jnp.where(n == nb + 1, nb, jnp.clip(n - 1, 0, nb - 1)), 0))
    qblk = pl.BlockSpec((BLK, SWA_HEADS * DH), lambda n: (jnp.minimum(n, nb), 0))
    return pl.pallas_call(
        body, name="swa_bwd", grid=(nb + 2,),
        in_specs=[pl.BlockSpec(memory_space=pltpu.SMEM), qblk] + _swa_kv_specs(kvw) + _swa_kv_specs(kvw)
        + [qblk, _const(TOKEN)],
        out_specs=[qblk, kv_out, kv_out, _acc((1, LANE))],
        out_shape=[pltpu.HBM((_lp(), SWA_HEADS * DH), F32), pltpu.HBM((_lp(), kvw), F32),
                   pltpu.HBM((_lp(), kvw), F32), pltpu.HBM((1, LANE), F32)],
        scratch_shapes=[pltpu.VMEM((BLK, kvw), F32)] * 4,
        compiler_params=_params(16, dimension_semantics=_seq()),
    )(sinks, *_hbm(qs, ks, ks, ks, vs, vs, vs, do_s), token)


def _in_bwd(dqs, dks, dvs, dqg, dkg, dvg, drg, dz, dpre1, w_in_t, wg2_p):
    tm = _row_tile(384)
    lp = _lp()
    widths = (512, 128, 128, 256, 256, 512, 512)
    offs = (O_QS, O_KS, O_VS, O_QG, O_KG, O_VG, O_RG)

    def body(*refs):
        parts, (dz_ref, dp1_ref, w_ref, wg2_ref, dproj_ref, dh0_ref, dbin_ref, dbg_ref) = refs[:7], refs[7:]

        @pl.when(pl.program_id(0) == 0)
        def _():
            dbin_ref[...] = jnp.zeros_like(dbin_ref)
            dbg_ref[...] = jnp.zeros_like(dbg_ref)

        for pos, h in enumerate(HEAD_POS):
            val = parts[0][:, pos * DH:(pos + 1) * DH]
            dproj_ref[:, O_QS + h * DH:O_QS + (h + 1) * DH] = val.astype(BF16)
            dbin_ref[:, O_QS + h * DH:O_QS + (h + 1) * DH] += jnp.sum(val, axis=0, keepdims=True)
        for p_ref, off, wd in zip(parts[1:], offs[1:], widths[1:]):
            val = p_ref[...]
            dproj_ref[:, off:off + wd] = val.astype(BF16)
            dbin_ref[:, off:off + wd] += jnp.sum(val, axis=0, keepdims=True)
        dz = dz_ref[...]
        dlr = _dot_nt(dz, wg2_ref[...])
        dproj_ref[:, O_LR:O_LR + LANE] = dlr.astype(BF16)
        dbin_ref[:, O_LR:O_LR + LANE] += jnp.sum(dlr, axis=0, keepdims=True)
        dbg_ref[...] += jnp.sum(dz, axis=0, keepdims=True)
        dh0_ref[...] = ALPHA * dp1_ref[...] + _dot(dproj_ref[...], w_ref[...])

    return pl.pallas_call(
        body, name="in_bwd", grid=(lp // tm,),
        in_specs=[_rows(tm, w) for w in widths] + [_rows(tm, 256), _rows(tm, D), _const((D_IN_P, D)), _const((LANE, 256))],
        out_specs=[_rows(tm, D_IN_P), _rows(tm, D), _acc((1, D_IN_P)), _acc((1, 256))],
        out_shape=[pltpu.HBM((lp, D_IN_P), BF16), pltpu.HBM((lp, D), F32),
                   pltpu.HBM((1, D_IN_P), F32), pltpu.HBM((1, 256), F32)],
        compiler_params=_params(40, dimension_semantics=_seq()),
    )(*_hbm(dqs, dks, dvs, dqg, dkg, dvg, drg, dz, dpre1, w_in_t, wg2_p))


def _ln_in_bwd(x, meta_ext, dh0, g, token):
    tr = min(LN_ROWS, SEQ)

    def ln_bwd(x_ref, dh_ref, g_ref, dx_ref, dg_ref, db_ref):
        @pl.when(pl.program_id(0) == 0)
        def _():
            dg_ref[...] = jnp.zeros_like(dg_ref)
            db_ref[...] = jnp.zeros_like(db_ref)

        xhat, rstd = _ln_stats(x_ref[...])
        dh = dh_ref[...]
        dx_ref[...] = _ln_bwd(dh, xhat, rstd, g_ref[...])
        dg_ref[...] += jnp.sum(dh * xhat, axis=0, keepdims=True)
        db_ref[...] += jnp.sum(dh, axis=0, keepdims=True)

    def body(x_ref, dh_ref, g_ref, token_ref, dx_ref, dg_ref, db_ref):
        ln_bwd(x_ref, dh_ref, g_ref, dx_ref, dg_ref, db_ref)

    def meta_body(m_ref, dh_ref, g_ref, dm_ref, dg_ref, db_ref):
        ln_bwd(m_ref, dh_ref, g_ref, dm_ref, dg_ref, db_ref)

    sums = [pltpu.HBM((1, D), F32), pltpu.HBM((1, D), F32)]
    dx, dg, db = pl.pallas_call(
        body, name="ln_in_bwd", grid=(SEQ // tr,),
        in_specs=[_rows(tr, D), _rows(tr, D), _const((1, D)), _const(TOKEN)],
        out_specs=[_rows(tr, D), _acc((1, D)), _acc((1, D))],
        out_shape=[pltpu.HBM((SEQ, D), F32)] + sums,
        compiler_params=_params(32, dimension_semantics=_seq()),
    )(*_hbm(x, dh0, g), token)
    dm, dg_m, db_m = pl.pallas_call(
        meta_body, name="ln_in_bwd_meta", grid=(1,),
        in_specs=[_const((BLK, D)), pl.BlockSpec((BLK, D), lambda i: (SEQ // BLK, 0)), _const((1, D))],
        out_specs=[_acc((BLK, D)), _acc((1, D)), _acc((1, D))],
        out_shape=[pltpu.HBM((BLK, D), F32)] + sums,
        compiler_params=_params(16, dimension_semantics=_seq()),
    )(*_hbm(meta_ext, dh0, g))
    return dx, dm, dg + dg_m, db + db_m


def _local_step(x, target, ln_in_g, ln_in_b, b_in, bg2, sinks, gn, g1, b1, g2, b2,
                token, fetch_first, fetch_rest, ship_ffn, ship_w_in):
    row = lambda v: v.reshape(1, -1).astype(F32)
    b_in_p = jnp.pad(row(b_in), ((0, 0), (0, D_IN_P - D_IN)))
    gn4 = row(gn)
    sinks = sinks.reshape(-1).astype(F32)

    h_real = _ln_in_fwd_real(x, row(ln_in_g), row(ln_in_b), token)
    w_in_t, meta_full, wg2 = fetch_first([h_real])
    meta_ext = jnp.pad(meta_full, ((META_OFF, BLK - CH), (0, 0)))
    wg2_p = jnp.pad(wg2, ((0, LANE - wg2.shape[0]), (0, 0))).astype(BF16)
    h0 = _ln_in_fwd_meta(h_real, meta_ext, row(ln_in_g), row(ln_in_b))
    qs, ks, vs, qg, kg, vg, rg, glr, z = _in_proj(h0, w_in_t, b_in_p, wg2_p, row(bg2))
    o_s = _swa_fwd(sinks, qs, ks, vs)
    o_gla, st_all = _gla_fwd(qg, kg, vg, z)
    w_out, wg_t, wu_t, wd = fetch_rest([o_s, o_gla])
    o, pre1, h1 = _post_mix(o_s, o_gla, rg, h0, gn4, w_out, row(g1), row(b1))
    a, dgate, dup, dpre2, loss, dg2, db2 = _ffn_fwd_loss_bwd(h1, wg_t, wu_t, wd, target, row(g2), row(b2))
    dpre1, dg1, db1, do_s, do_gla, drg, dgn = _ffn_out_bwd(dpre2, dgate, dup, pre1, wg_t, wu_t, row(g1), w_out, o_gla,
                                                           rg, gn4)
    dwd = _atb(a, dpre2, "dw_down")
    dwg_t = _atb(dgate, h1, "dw_gate")
    dwu_t = _atb(dup, h1, "dw_up")
    dw_out = _atb(o, dpre1, "dw_out")
    token = ship_ffn(dict(w_out=dw_out, w_g=dwg_t, w_u=dwu_t, w_d=dwd))
    dqg, dkg, dvg, dz = _gla_bwd(qg, kg, vg, z, do_gla, st_all, token)
    dqs, dks, dvs, dsinks = _swa_bwd(sinks, qs, ks, vs, do_s, token)
    dproj, dh0, db_in_p, dbg2 = _in_bwd(dqs, dks, dvs, dqg, dkg, dvg, drg, dz, dpre1, w_in_t, wg2_p)
    token = ship_w_in(_atb(dproj, h0, "dw_in"))
    dwg2_p = _atb(glr, dz, "dw_gate_lr2")
    dx, dmeta_blk, dg_in, db_in_ln = _ln_in_bwd(x, meta_ext, dh0, row(ln_in_g), token)

    small = dict(meta_blk=dmeta_blk, ln_in_g=dg_in, ln_in_b=db_in_ln, ln1_g=dg1, ln1_b=db1, ln2_g=dg2, ln2_b=db2,
                 b_in_p=db_in_p, wg2_p=dwg2_p, bg2=dbg2, sinks=dsinks, gn=dgn, loss=loss)
    return dx, small


HBM = pl.BlockSpec(memory_space=pltpu.HBM)


def _place():
    return lax.axis_index("x"), lax.axis_index("y"), lax.axis_index("c")


def _other_chips(x, y):
    return [(1 - x, y), (x, 1 - y), (1 - x, 1 - y)]


def _dma_sems(n):
    return pltpu.SemaphoreType.DMA((n,))


def _comm_params():
    return pltpu.CompilerParams(has_side_effects=True)


SEM = pl.BlockSpec(memory_space=pltpu.SEMAPHORE)


def _ici_copies(kind, landing, srcs, lands, send_sems, recv_sems):
    x, y, c = _place()
    mine = 2 * x + y
    copies = []
    for a in range(len(srcs)):
        for j, (px, py) in enumerate(_other_chips(x, y)):
            slab = 2 * px + py if landing else mine
            if kind == "gather":
                src, dst = srcs[a].at[c], lands[a].at[slab, c]
            else:
                src, dst = srcs[a].at[2 * px + py], lands[a].at[slab]
            copies.append(pltpu.make_async_remote_copy(src, dst, send_sems.at[3 * a + j], recv_sems.at[3 * a + j],
                                                       device_id=(px, py, c), device_id_type=MESH))
    return copies


def _split_params():
    return pltpu.CompilerParams(has_side_effects=pltpu.SideEffectType.DATAFLOW_SIDE_EFFECTING)


def _ici_start(kind, srcs, land_shapes, after, name):
    n = len(srcs)
    lands = [pltpu.with_memory_space_constraint(lax.empty(s, a.dtype), pltpu.HBM) for s, a in zip(land_shapes, srcs)]

    def body(*refs):
        outs = refs[2 * n + len(after):]
        for cp in _ici_copies(kind, False, refs[:n], refs[n:2 * n], outs[0], outs[1]):
            cp.start()
        outs[-1][...] = jnp.zeros(TOKEN, F32)

    outs = pl.pallas_call(
        body, name=name, in_specs=[HBM] * (2 * n) + [pl.BlockSpec(memory_space=pl.ANY)] * len(after),
        out_specs=[SEM, SEM] + [HBM] * (2 * n) + [pl.BlockSpec(memory_space=pltpu.VMEM)],
        out_shape=[_dma_sems(3 * n)] * 2 + [pltpu.HBM(a.shape, a.dtype) for a in list(srcs) + lands]
        + [jax.ShapeDtypeStruct(TOKEN, F32)],
        input_output_aliases={i: 2 + i for i in range(2 * n)},
        compiler_params=_split_params(),
    )(*_hbm(*srcs), *lands, *after)
    return outs[:-1], outs[-1]


def _ici_wait(kind, handle, after, name):
    n = (len(handle) - 2) // 2

    def body(*refs):
        for cp in _ici_copies(kind, True, refs[:n], refs[n:2 * n], refs[2 * n], refs[2 * n + 1]):
            cp.wait_send()
            cp.wait_recv()

    outs = pl.pallas_call(
        body, name=name, in_specs=[HBM] * (2 * n) + [SEM, SEM] + [pl.BlockSpec(memory_space=pl.ANY)] * len(after),
        out_specs=[HBM] * (2 * n), out_shape=[pltpu.HBM(a.shape, a.dtype) for a in handle[2:]],
        input_output_aliases={i: i for i in range(2 * n)},
        compiler_params=_split_params(),
    )(*handle[2:], handle[0], handle[1], *after)
    return list(outs[n:])


def _sibling_forward(lands, name):
    n = len(lands)

    def body(*refs):
        outs = refs[n:2 * n]
        send_sems, recv_sems = refs[2 * n:]
        x, y, c = _place()

        def copy(a, j, half):
            px, py = _other_chips(x, y)[j]
            blk = outs[a].at[2 * px + py, half]
            return pltpu.make_async_remote_copy(blk, blk, send_sems.at[3 * a + j], recv_sems.at[3 * a + j],
                                                device_id=(x, y, 1 - c), device_id_type=MESH)

        pairs = [(a, j) for a in range(n) for j in range(3)]
        for a, j in pairs:
            copy(a, j, c).start()
        for a, j in pairs:
            copy(a, j, 1 - c).wait_recv()
        for a, j in pairs:
            copy(a, j, c).wait_send()

    return pl.pallas_call(
        body, name=name, in_specs=[HBM] * n, out_specs=[HBM] * n,
        out_shape=[pltpu.HBM(a.shape, a.dtype) for a in lands],
        input_output_aliases={a: a for a in range(n)},
        scratch_shapes=[_dma_sems(3 * n)] * 2,
        compiler_params=_comm_params(),
    )(*_hbm(*lands))


def _sibling_exchange(grads, name):
    n = len(grads)

    def body(*refs):
        ins, outs = refs[:n], refs[n:2 * n]
        send_sems, recv_sems = refs[2 * n:]
        x, y, c = _place()
        copies = []
        for a in range(n):
            for s in range(N_CHIPS):
                cp = pltpu.make_async_remote_copy(ins[a].at[s, 1 - c], outs[a].at[s], send_sems.at[N_CHIPS * a + s],
                                                  recv_sems.at[N_CHIPS * a + s], device_id=(x, y, 1 - c),
                                                  device_id_type=MESH)
                cp.start()
                copies.append(cp)
        for cp in copies:
            cp.wait_recv()
        for cp in copies:
            cp.wait_send()

    return pl.pallas_call(
        body, name=name, in_specs=[HBM] * n, out_specs=[HBM] * n,
        out_shape=[pltpu.HBM((N_CHIPS, g.shape[2], D), F32) for g in grads],
        scratch_shapes=[_dma_sems(N_CHIPS * n)] * 2,
        compiler_params=_comm_params(),
    )(*_hbm(*grads))


def _add_halves(core, grad, recv, dtype, name):
    h = grad.shape[2]

    def body(c_ref, a_ref, b_ref, o_ref):
        o_ref[...] = (a_ref[0] + b_ref[...]).astype(dtype)

    return pl.pallas_call(
        body, name=name,
        grid_spec=pltpu.PrefetchScalarGridSpec(
            num_scalar_prefetch=1, grid=(N_CHIPS,),
            in_specs=[pl.BlockSpec((1, 1, h, D), lambda s, c: (s, c[0], 0, 0)),
                      pl.BlockSpec((1, h, D), lambda s, c: (s, 0, 0))],
            out_specs=pl.BlockSpec((1, h, D), lambda s, c: (s, 0, 0))),
        out_shape=pltpu.HBM((N_CHIPS, h, D), dtype),
        compiler_params=_params(16, dimension_semantics=_seq()),
    )(core, *_hbm(grad, recv))


N_DEVICES = 2 * N_CHIPS
PEER_FLIPS = [(dx, dy, dc) for dx in (0, 1) for dy in (0, 1) for dc in (0, 1)][1:]


def _small_exchange(pack, after):
    n = len(PEER_FLIPS)

    def body(p_ref, *refs):
        out_ref, send_sems, recv_sems = refs[len(after):]
        x, y, c = _place()
        flip = lambda v, d: 1 - v if d else v

        def copy(k, landing):
            px, py, pc = (flip(v, d) for v, d in zip((x, y, c), PEER_FLIPS[k]))
            slab = 4 * px + 2 * py + pc if landing else 4 * x + 2 * y + c
            return pltpu.make_async_remote_copy(p_ref, out_ref.at[slab], send_sems.at[k], recv_sems.at[k],
                                                device_id=(px, py, pc), device_id_type=MESH)

        for k in range(n):
            copy(k, False).start()
        for k in range(n):
            copy(k, True).wait_recv()
        for k in range(n):
            copy(k, False).wait_send()

    return pl.pallas_call(
        body, name="small_exchange", in_specs=[HBM] + [pl.BlockSpec(memory_space=pl.ANY)] * len(after), out_specs=HBM,
        out_shape=pltpu.HBM((N_DEVICES,) + pack.shape, F32),
        scratch_shapes=[_dma_sems(n)] * 2,
        compiler_params=_comm_params(),
    )(*_hbm(pack), *after)


def _sum_chips(slots, first, rest, name):
    h = first.shape[1]

    def body(i_ref, a_ref, b_ref, c_ref, d_ref, o_ref):
        o_ref[...] = ((a_ref[...].astype(F32) + b_ref[...].astype(F32)) + c_ref[...].astype(F32)) + d_ref[...].astype(F32)

    slab = lambda k: pl.BlockSpec((1, h, D), lambda i, ix: (ix[k], 0, 0))
    return pl.pallas_call(
        body, name=name,
        grid_spec=pltpu.PrefetchScalarGridSpec(num_scalar_prefetch=1, grid=(1,),
                                               in_specs=[slab(0), slab(1), slab(2), slab(3)], out_specs=slab(4)),
        out_shape=pltpu.HBM((2, h, D), F32),
        compiler_params=_params(16, dimension_semantics=_seq()),
    )(slots, *_hbm(first, rest, rest, rest))


def _join_halves(halves):
    n = len(halves)

    def body(*refs):
        outs = refs[n:2 * n]
        send_sems, recv_sems = refs[2 * n:]
        x, y, c = _place()

        def copy(a, slab):
            return pltpu.make_async_remote_copy(outs[a].at[slab], outs[a].at[slab], send_sems.at[a], recv_sems.at[a],
                                                device_id=(x, y, 1 - c), device_id_type=MESH)

        for a in range(n):
            copy(a, c).start()
        for a in range(n):
            copy(a, 1 - c).wait_recv()
        for a in range(n):
            copy(a, c).wait_send()

    return pl.pallas_call(
        body, name="join_halves", in_specs=[HBM] * n, out_specs=[HBM] * n,
        out_shape=[pltpu.HBM(h.shape, F32) for h in halves],
        input_output_aliases={a: a for a in range(n)},
        scratch_shapes=[_dma_sems(n)] * 2,
        compiler_params=_comm_params(),
    )(*_hbm(*halves))


def _chip_partials(grads, wire_dtypes, names):
    core = lax.axis_index("c").astype(jnp.int32).reshape(1)
    recv = _sibling_exchange(grads, "sibling_exchange_" + names[0])
    return [_add_halves(core, g, r, dt, "add_halves_" + nm) for g, r, dt, nm in zip(grads, recv, wire_dtypes, names)]


def _finish_reduce(parts, got, names):
    x, y, c = _place()
    others = [2 * px + py for px, py in _other_chips(x, y)]
    own_first = jnp.stack([2 * x + y] + others + [c]).astype(jnp.int32)
    halves = [_sum_chips(own_first, p, q, "sum_chips_" + nm) for p, q, nm in zip(parts, got, names)]
    return [f.reshape(2 * f.shape[1], D) for f in _join_halves(halves)]


def _adamw(w, g, m, v, name):
    rows, cols = w.shape
    if rows % 8 == 0:
        tr = max(t for t in range(8, 257, 8) if rows % t == 0)
        grid, blk = (rows // tr,), pl.BlockSpec((tr, cols), lambda i: (i, 0))
    else:
        grid, blk = (cols // 256,), pl.BlockSpec((rows, 256), lambda i: (0, i))

    def body(w_ref, g_ref, m_ref, v_ref, d_ref, nm_ref, nv_ref):
        d_ref[...], nm_ref[...], nv_ref[...] = _adamw_math(w_ref[...], g_ref[...], m_ref[...], v_ref[...])

    return pl.pallas_call(
        body, name=name, grid=grid,
        in_specs=[blk] * 4, out_specs=[blk] * 3,
        out_shape=[pltpu.HBM(w.shape, F32)] * 3,
        compiler_params=_params(32, dimension_semantics=_seq()),
    )(*_hbm(w, g, m, v))


def _adamw_math(w, g, m, v):
    nm = ADAM_B1 * m + (1.0 - ADAM_B1) * g
    nv = ADAM_B2 * v + (1.0 - ADAM_B2) * (g * g)
    m_hat = nm / (1.0 - ADAM_B1 ** ADAM_STEP)
    v_hat = nv / (1.0 - ADAM_B2 ** ADAM_STEP)
    return -ADAM_LR * (m_hat / (jnp.sqrt(v_hat) + ADAM_EPS) + ADAM_WD * w), nm, nv


SMALL = (("meta_tokens", (N_META, D // N_CHIPS)), ("ln_in_g", (1, D)), ("ln_in_b", (1, D)), ("b_in", (1, D_IN)),
         ("w_gate_lr2", (GATE_RANK, GLA_HEADS * DK // N_CHIPS)), ("b_gate_lr2", (1, GLA_HEADS * DK)),
         ("attn_sinks", (1, SWA_HEADS)),
         ("gla_norm_g", (1, DV)), ("ln1_g", (1, D)), ("ln1_b", (1, D)), ("ln2_g", (1, D)), ("ln2_b", (1, D)))
ROW_META, ROW_B_IN, ROW_TAIL, ROW_WG2 = 0, 22, 25, 32
ROW_LN = dict(ln_in_g=16, ln_in_b=17, ln1_g=18, ln1_b=19, ln2_g=20, ln2_b=21)
TAIL_BG2, TAIL_SINKS, TAIL_GN, TAIL_LOSS = 0, 256, 256 + SWA_HEADS, 256 + SWA_HEADS + DV


def _adamw_small(place, packs, own, params):
    n = len(SMALL)

    def body(place_ref, packs_ref, own_ref, *refs):
        ins, outs, p_ref = refs[:3 * n], refs[3 * n:-1], refs[-1]
        me, c = place_ref[0], place_ref[1]
        total = jnp.where(me == 0, own_ref[...], packs_ref[0])
        for i in range(1, N_DEVICES):
            total = total + jnp.where(me == i, own_ref[...], packs_ref[i])
        p_ref[...] = total
        outs[4 * n][...] = total[ROW_TAIL:ROW_TAIL + 1, :]

        def mine(width, rows):
            part = lambda s: p_ref[rows, s * width:(s + 1) * width]
            return jnp.where(c == 0, part(0), jnp.where(c == 1, part(1), jnp.where(c == 2, part(2), part(3))))

        tail = lambda lo, width: p_ref[ROW_TAIL:ROW_TAIL + 1, lo:lo + width]
        grads = dict(
            meta_tokens=mine(D // N_CHIPS, slice(ROW_META, ROW_META + N_META)),
            b_in=jnp.concatenate([p_ref[ROW_B_IN:ROW_B_IN + 1, :], p_ref[ROW_B_IN + 1:ROW_B_IN + 2, :],
                                  p_ref[ROW_B_IN + 2:ROW_B_IN + 3, 0:D_IN - 2 * D]], axis=1),
            w_gate_lr2=mine(256 // N_CHIPS, slice(ROW_WG2, ROW_WG2 + 16)),
            b_gate_lr2=tail(TAIL_BG2, 256), attn_sinks=tail(TAIL_SINKS, SWA_HEADS), gla_norm_g=tail(TAIL_GN, DV),
            **{k: p_ref[r:r + 1, :] for k, r in ROW_LN.items()})
        for i, (name, _) in enumerate(SMALL):
            g = grads[name]
            outs[4 * i][...] = g
            outs[4 * i + 1][...], outs[4 * i + 2][...], outs[4 * i + 3][...] = _adamw_math(
                ins[3 * i][...], g, ins[3 * i + 1][...], ins[3 * i + 2][...])

    whole = lambda shape: pl.BlockSpec(shape, lambda i, c: (0,) * len(shape))
    outs = pl.pallas_call(
        body, name="adamw_small",
        grid_spec=pltpu.PrefetchScalarGridSpec(
            num_scalar_prefetch=1, grid=(1,),
            in_specs=[whole(packs.shape), whole(own.shape)] + [whole(s) for _, s in SMALL for _ in range(3)],
            out_specs=[whole(s) for _, s in SMALL for _ in range(4)] + [whole((1, D))],
            scratch_shapes=[pltpu.VMEM(own.shape, F32)]),
        out_shape=[pltpu.HBM(s, F32) for _, s in SMALL for _ in range(4)] + [pltpu.HBM((1, D), F32)],
        compiler_params=_params(16, dimension_semantics=_seq()),
    )(place, *_hbm(packs, own, *[a for p in params for a in p]))
    return [outs[4 * i:4 * i + 4] for i in range(n)], outs[4 * n]


def _small_pack(gr):
    names = ["meta_blk"] + list(ROW_LN) + ["b_in_p", "wg2_p", "bg2", "sinks", "gn", "loss"]
    gate_w = GLA_HEADS * DK

    def body(*refs):
        src, out = dict(zip(names, refs)), refs[-1]
        out[...] = jnp.zeros_like(out)
        out[ROW_META:ROW_META + N_META, :] = src["meta_blk"][META_OFF:CH, :]
        for k, r in ROW_LN.items():
            out[r:r + 1, :] = src[k][...]
        for j in range(-(-D_IN // D)):
            width = min(D, D_IN - j * D)
            out[ROW_B_IN + j:ROW_B_IN + j + 1, 0:width] = src["b_in_p"][:, j * D:j * D + width]
        tail = slice(ROW_TAIL, ROW_TAIL + 1)
        out[tail, TAIL_BG2:TAIL_BG2 + gate_w] = src["bg2"][...]
        out[tail, TAIL_SINKS:TAIL_SINKS + SWA_HEADS] = src["sinks"][:, 0:SWA_HEADS]
        out[tail, TAIL_GN:TAIL_GN + DV] = src["gn"][...]
        out[tail, TAIL_LOSS:TAIL_LOSS + 1] = src["loss"][:, 0:1]
        out[ROW_WG2:ROW_WG2 + GATE_RANK, 0:gate_w] = src["wg2_p"][0:GATE_RANK, :]

    arrays = [gr[k] for k in names]
    return pl.pallas_call(
        body, name="small_pack", grid=(1,),
        in_specs=[_acc(a.shape) for a in arrays], out_specs=_acc((SMALL_ROWS, D)),
        out_shape=pltpu.HBM((SMALL_ROWS, D), F32),
        compiler_params=_params(16, dimension_semantics=_seq()),
    )(*_hbm(*arrays))


BIG = ("w_in", "w_out", "w_g", "w_u", "w_d")


def kernel(x, meta_tokens, ln_in_g, ln_in_b, w_in, b_in, w_gate_lr2, b_gate_lr2, attn_sinks, gla_norm_g, w_out, ln1_g, ln1_b, w_ffn_gate, w_ffn_up, w_ffn_down, ln2_g, ln2_b, loss_target, m_meta_tokens, m_ln_in_g, m_ln_in_b, m_w_in, m_b_in, m_w_gate_lr2, m_b_gate_lr2, m_attn_sinks, m_gla_norm_g, m_w_out, m_ln1_g, m_ln1_b, m_w_ffn_gate, m_w_ffn_up, m_w_ffn_down, m_ln2_g, m_ln2_b, v_meta_tokens, v_ln_in_g, v_ln_in_b, v_w_in, v_b_in, v_w_gate_lr2, v_b_gate_lr2, v_attn_sinks, v_gla_norm_g, v_w_out, v_ln1_g, v_ln1_b, v_w_ffn_gate, v_w_ffn_up, v_w_ffn_down, v_ln2_g, v_ln2_b):
    chip = 2 * lax.axis_index("x") + lax.axis_index("y")

    halves = lambda a: a.reshape(2, a.shape[0] // 2, a.shape[1])
    r_in = SHARD_ROWS["w_in"]
    first = [halves(a) for a in (jnp.pad(w_in[0].T.astype(BF16), ((0, W_IN_WIN - r_in), (0, 0))), meta_tokens,
                                 w_gate_lr2[0])]
    rest = [halves(a) for a in (w_out[0].astype(BF16), w_ffn_gate[0].T.astype(BF16), w_ffn_up[0].T.astype(BF16),
                                w_ffn_down[0].astype(BF16))]
    lands = lambda arrs: [(N_CHIPS,) + a.shape for a in arrs]
    first_handle, first_token = _ici_start("gather", first, lands(first), [], "gather_first_start")
    rest_handle, token = _ici_start("gather", rest, lands(rest), [first_token], "gather_rest_start")

    def fetch(handle, shards, after, name):
        got = _sibling_forward(_ici_wait("gather", handle, after, name + "_wait"), name + "_forward")
        return [lax.dynamic_update_index_in_dim(g, s, chip, axis=0) for g, s in zip(got, shards)]

    def fetch_first(after):
        g_in, g_meta, g_wg2 = fetch(first_handle, first, after, "gather_first")
        w_in_t = jnp.pad(g_in.reshape(N_CHIPS, W_IN_WIN, D)[:, :r_in].reshape(D_IN, D), ((0, D_IN_P - D_IN), (0, 0)))
        meta_full = jnp.concatenate([g_meta[s].reshape(N_META, -1) for s in range(N_CHIPS)], axis=1)
        wg2_full = jnp.concatenate([g_wg2[s].reshape(w_gate_lr2.shape[1], -1) for s in range(N_CHIPS)], axis=1)
        return w_in_t, meta_full, wg2_full

    def fetch_rest(after):
        return [g.reshape(-1, D) for g in fetch(rest_handle, rest, after, "gather_rest")]

    sent = {}

    def ship(key, grads, names):
        parts = _chip_partials([g.reshape(N_CHIPS, 2, -1, D) for g in grads], [BF16] * len(grads), names)
        handle, ship_token = _ici_start("scatter", parts, [p.shape for p in parts], [], "scatter_" + key + "_start")
        sent[key] = (parts, handle)
        return ship_token

    def ship_ffn(g):
        return ship("ffn", [g[k] for k in BIG[1:]], list(BIG[1:]))

    def ship_w_in(dw_in_t):
        win_start = [s * r_in // BF16_ROWS * BF16_ROWS for s in range(N_CHIPS)]
        return ship("w_in", [jnp.stack([dw_in_t[st:st + W_IN_WIN] for st in win_start])], ["w_in"])

    dx, gr = _local_step(
        x[0], loss_target[0], ln_in_g, ln_in_b, b_in[0], b_gate_lr2[0], attn_sinks[0], gla_norm_g[0], ln1_g[0],
        ln1_b[0], ln2_g[0], ln2_b[0], token, fetch_first, fetch_rest, ship_ffn, ship_w_in)
    ffn_got = _ici_wait("scatter", sent["ffn"][1], [dx], "scatter_ffn_wait")
    w_in_got = _ici_wait("scatter", sent["w_in"][1], [dx], "scatter_w_in_wait")

    small_own = _small_pack(gr)
    small_all = _small_exchange(small_own, [w_in_got[0]])
    red = _finish_reduce(sent["w_in"][0] + sent["ffn"][0], w_in_got + ffn_got, list(BIG))

    big_g = dict(zip(BIG, red))
    big_g["w_in"] = lax.dynamic_slice_in_dim(red[0], chip * (r_in % BF16_ROWS), r_in, axis=0)
    grads = dict(w_in=big_g["w_in"].T[None], w_out=big_g["w_out"][None], w_ffn_gate=big_g["w_g"].T[None],
                 w_ffn_up=big_g["w_u"].T[None], w_ffn_down=big_g["w_d"][None])
    weights = dict(meta_tokens=meta_tokens, ln_in_g=ln_in_g, ln_in_b=ln_in_b, w_in=w_in, b_in=b_in,
                   w_gate_lr2=w_gate_lr2, b_gate_lr2=b_gate_lr2, attn_sinks=attn_sinks, gla_norm_g=gla_norm_g,
                   w_out=w_out, ln1_g=ln1_g, ln1_b=ln1_b, w_ffn_gate=w_ffn_gate, w_ffn_up=w_ffn_up,
                   w_ffn_down=w_ffn_down, ln2_g=ln2_g, ln2_b=ln2_b)
    m_in = dict(meta_tokens=m_meta_tokens, ln_in_g=m_ln_in_g, ln_in_b=m_ln_in_b, w_in=m_w_in, b_in=m_b_in,
                w_gate_lr2=m_w_gate_lr2, b_gate_lr2=m_b_gate_lr2, attn_sinks=m_attn_sinks, gla_norm_g=m_gla_norm_g,
                w_out=m_w_out, ln1_g=m_ln1_g, ln1_b=m_ln1_b, w_ffn_gate=m_w_ffn_gate, w_ffn_up=m_w_ffn_up,
                w_ffn_down=m_w_ffn_down, ln2_g=m_ln2_g, ln2_b=m_ln2_b)
    v_in = dict(meta_tokens=v_meta_tokens, ln_in_g=v_ln_in_g, ln_in_b=v_ln_in_b, w_in=v_w_in, b_in=v_b_in,
                w_gate_lr2=v_w_gate_lr2, b_gate_lr2=v_b_gate_lr2, attn_sinks=v_attn_sinks, gla_norm_g=v_gla_norm_g,
                w_out=v_w_out, ln1_g=v_ln1_g, ln1_b=v_ln1_b, w_ffn_gate=v_w_ffn_gate, w_ffn_up=v_w_ffn_up,
                w_ffn_down=v_w_ffn_down, ln2_g=v_ln2_g, ln2_b=v_ln2_b)
    names = list(weights)
    big_names = ("w_in", "w_out", "w_ffn_gate", "w_ffn_up", "w_ffn_down")

    delta, new_m, new_v = {}, {}, {}
    for k, kk in zip(big_names, BIG):
        flip = (lambda a: a.T) if kk in ("w_in", "w_g", "w_u") else (lambda a: a)
        d_, m_, v_ = _adamw(flip(weights[k][0]), big_g[kk], flip(m_in[k][0]), flip(v_in[k][0]), "adamw_" + k)
        delta[k], new_m[k], new_v[k] = (flip(t)[None] for t in (d_, m_, v_))
    small_in = [tuple(src[k].reshape(shape) for src in (weights, m_in, v_in)) for k, shape in SMALL]
    place = jnp.stack([2 * chip + lax.axis_index("c"), chip]).astype(jnp.int32)
    small_out, tail_row = _adamw_small(place, small_all, small_own, small_in)
    for (k, _), results in zip(SMALL, small_out):
        grads[k], delta[k], new_m[k], new_v[k] = (r.reshape(weights[k].shape) for r in results)

    return (tail_row[0, TAIL_LOSS], dx[None], *[grads[k] for k in names], *[delta[k] for k in names], *[new_m[k] for k in names],
            *[new_v[k] for k in names])
```

```python
import jax
import jax.numpy as jnp
from jax import lax
from jax.experimental import pallas as pl
from jax.experimental.pallas import tpu as pltpu

F32 = jnp.float32
BF16 = jnp.bfloat16
MESH = pl.DeviceIdType.MESH

D = 1024
SEQ = 4096
N_META = 16
SWA_HEADS, SWA_KV_HEADS, DH = 8, 2, 64
WINDOW = 128
GLA_HEADS, DK, DV = 4, 64, 128
GLA_TAU = 16.0
CH = 64
D_FF = 2816
D_IN = 2320
LN_EPS = 1e-5
RMS_EPS = 1e-6
ALPHA = 2.0 ** 0.25
NEG = -1e30
ADAM_LR, ADAM_B1, ADAM_B2, ADAM_EPS, ADAM_WD, ADAM_STEP = 0.001, 0.9, 0.999, 1e-8, 0.01, 10
O_QS, O_KS, O_VS, O_QG, O_KG, O_VG, O_RG, O_LR = 0, 512, 640, 768, 1024, 1280, 1792, 2304

LANE = 128
BLK = WINDOW
GATE_RANK = 16
D_IN_P = D_IN + LANE - GATE_RANK
META_OFF = CH - N_META
HEAD_POS = (0, 4, 1, 5, 2, 6, 3, 7)
LN_ROWS = 512
TOKEN = (8, LANE)
N_CHIPS = 4
SHARD_ROWS = dict(w_in=D_IN // N_CHIPS, w_out=D // N_CHIPS, w_g=D_FF // N_CHIPS, w_u=D_FF // N_CHIPS,
                  w_d=D_FF // N_CHIPS)
SMALL_ROWS = 48
BF16_ROWS = 16
W_IN_WIN = -(-SHARD_ROWS["w_in"] // (2 * BF16_ROWS)) * 2 * BF16_ROWS
VMEM_CAP_MB = 64
VMEM_SPARE_MB = 6


def _lp():
    return SEQ + BLK


def _row_tile(cap):
    lp = _lp()
    return max(t for t in range(16, cap + 1, 16) if lp % t == 0)


def _params(vmem_mb, **kw):
    assert vmem_mb <= VMEM_CAP_MB - VMEM_SPARE_MB
    return pltpu.CompilerParams(vmem_limit_bytes=vmem_mb << 20, **kw)


def _seq(n=1):
    return ("arbitrary",) * n


def _const(shape):
    return pl.BlockSpec(shape, lambda *_: (0,) * len(shape), pipeline_mode=pl.Buffered(1))


def _acc(shape):
    return pl.BlockSpec(shape, lambda *_: (0,) * len(shape))


def _rows(tm, width):
    return pl.BlockSpec((tm, width), lambda i: (i, 0))


def _dot(a, b):
    return jnp.dot(a.astype(BF16), b.astype(BF16), preferred_element_type=F32)


def _dot_nt(a, b):
    return lax.dot_general(a.astype(BF16), b.astype(BF16), (((1,), (1,)), ((), ())), preferred_element_type=F32)


def _dot_tn(a, b):
    return lax.dot_general(a.astype(BF16), b.astype(BF16), (((0,), (0,)), ((), ())), preferred_element_type=F32)


def _dot_exact(a, b):
    return jnp.dot(a, b, precision=lax.Precision.HIGHEST, preferred_element_type=F32)


def _ln_stats(x):
    mu = jnp.mean(x, axis=-1, keepdims=True)
    xc = x - mu
    rstd = lax.rsqrt(jnp.mean(xc * xc, axis=-1, keepdims=True) + LN_EPS)
    return xc * rstd, rstd


def _ln_bwd(dy, xhat, rstd, g):
    dxh = dy * g
    return rstd * (dxh - jnp.mean(dxh, axis=-1, keepdims=True) - xhat * jnp.mean(dxh * xhat, axis=-1, keepdims=True))


def _sigmoid(x):
    return 1.0 / (1.0 + jnp.exp(-x))


def _iota(shape, dim):
    return lax.broadcasted_iota(jnp.int32, shape, dim)


def _hbm(*arrays):
    return tuple(pltpu.with_memory_space_constraint(a, pltpu.HBM) for a in arrays)


def _ln_in_fwd_real(x, g, b, token):
    tr = min(LN_ROWS, SEQ)

    def body(x_ref, g_ref, b_ref, token_ref, h_ref):
        xhat, _ = _ln_stats(x_ref[...])
        h_ref[...] = xhat * g_ref[...] + b_ref[...]

    return pl.pallas_call(
        body, name="ln_in_fwd", grid=(SEQ // tr,),
        in_specs=[_rows(tr, D), _const((1, D)), _const((1, D)), _const(TOKEN)],
        out_specs=_rows(tr, D),
        out_shape=pltpu.HBM((_lp(), D), F32),
        compiler_params=_params(32, dimension_semantics=_seq()),
    )(*_hbm(x, g, b), token)


def _ln_in_fwd_meta(h_real, meta_ext, g, b):
    def meta_body(m_ref, g_ref, b_ref, real_ref, h_ref):
        xhat, _ = _ln_stats(m_ref[...])
        h_ref[...] = xhat * g_ref[...] + b_ref[...]

    return pl.pallas_call(
        meta_body, name="ln_in_fwd_meta", grid=(1,),
        in_specs=[_const((BLK, D)), _const((1, D)), _const((1, D)), pl.BlockSpec(memory_space=pl.ANY)],
        out_specs=pl.BlockSpec((BLK, D), lambda i: (SEQ // BLK, 0)),
        out_shape=pltpu.HBM((_lp(), D), F32),
        input_output_aliases={3: 0},
        compiler_params=_params(16, dimension_semantics=_seq()),
    )(*_hbm(meta_ext, g, b, h_real))


def _in_proj(h0, w_in_t, b_in_p, wg2_p, bg2):
    tm = _row_tile(384)
    lp = _lp()
    widths = (512, 128, 128, 256, 256, 512, 512, 128)
    offs = (O_QS, O_KS, O_VS, O_QG, O_KG, O_VG, O_RG, O_LR)

    def body(h_ref, w_ref, b_ref, wg2_ref, bg2_ref, *outs):
        proj = _dot_nt(h_ref[...], w_ref[...]) + b_ref[...]
        for pos, h in enumerate(HEAD_POS):
            outs[0][:, pos * DH:(pos + 1) * DH] = proj[:, O_QS + h * DH:O_QS + (h + 1) * DH]
        for o_ref, off, wd in zip(outs[1:8], offs[1:], widths[1:]):
            o_ref[...] = proj[:, off:off + wd]
        outs[8][...] = _dot(proj[:, O_LR:O_LR + LANE], wg2_ref[...]) + bg2_ref[...]

    return pl.pallas_call(
        body, name="in_proj", grid=(lp // tm,),
        in_specs=[_rows(tm, D), _const((D_IN_P, D)), _const((1, D_IN_P)), _const((LANE, 256)), _const((1, 256))],
        out_specs=[_rows(tm, w) for w in widths] + [_rows(tm, 256)],
        out_shape=[pltpu.HBM((lp, w), F32) for w in widths] + [pltpu.HBM((lp, 256), F32)],
        compiler_params=_params(40, dimension_semantics=_seq()),
    )(*_hbm(h0, w_in_t, b_in_p, wg2_p, bg2))


def _swa_masks(n):
    nb = SEQ // BLK
    is_meta = n == nb
    ri = _iota((BLK, BLK), 0)
    cj = _iota((BLK, BLK), 1)
    meta_col = ((cj >= META_OFF) & (cj < CH)).astype(jnp.int32)
    meta_q = meta_col * ((cj <= ri) & (ri < CH)).astype(jnp.int32)
    valid_m = jnp.where(is_meta, meta_q, meta_col) > 0
    dist_m = jnp.where(is_meta, ri - cj, n * BLK + ri + CH - cj).astype(F32)
    valid_p = jnp.where((n >= 1) & (n < nb), (cj > ri).astype(jnp.int32), 0) > 0
    dist_p = (ri + BLK - cj).astype(F32)
    valid_c = jnp.where(n < nb, (cj <= ri).astype(jnp.int32), 0) > 0
    dist_c = (ri - cj).astype(F32)
    return (dist_m, dist_p, dist_c), (valid_m, valid_p, valid_c)


def _swa_bias(n):
    dists, valids = _swa_masks(n)
    return (jnp.concatenate([-d for d in dists], axis=1),
            jnp.concatenate([jnp.where(v, 0.0, NEG) for v in valids], axis=1))


def _swa_half(ref, pos, scale=1.0):
    col = ref[:, (pos // 2) * LANE:(pos // 2 + 1) * LANE]
    lane = _iota((BLK, LANE), 1)
    mine = lane < DH if pos % 2 == 0 else lane >= DH
    return jnp.where(mine, col * scale, 0.0).astype(BF16)


def _swa_merge(even, odd):
    return jnp.where(_iota((BLK, LANE), 1) < DH, even, odd)


def _swa_softmax(t, sink):
    m = jnp.maximum(jnp.max(t, axis=-1, keepdims=True), sink)
    e = jnp.exp(t - m)
    e_sink = jnp.exp(sink - m)
    inv = 1.0 / (jnp.sum(e, axis=-1, keepdims=True) + e_sink)
    return e * inv, e_sink * inv


def _swa_kv_specs(width):
    nb = SEQ // BLK
    return [pl.BlockSpec((BLK, width), lambda n: (nb, 0)),
            pl.BlockSpec((BLK, width), lambda n: (jnp.clip(n - 1, 0, nb - 1), 0)),
            pl.BlockSpec((BLK, width), lambda n: (jnp.minimum(n, nb), 0))]


def _swa_fwd(sinks, qs, ks, vs):
    nb = SEQ // BLK
    heads = range(SWA_HEADS)

    def body(sink_ref, q_ref, km_ref, kp_ref, kc_ref, vm_ref, vp_ref, vc_ref, o_ref):
        negdist, maskbias = _swa_bias(pl.program_id(0))
        k_all = jnp.concatenate([km_ref[...], kp_ref[...], kc_ref[...]], axis=0).astype(BF16)
        v_all = jnp.concatenate([vm_ref[...], vp_ref[...], vc_ref[...]], axis=0).astype(BF16)
        q = [_swa_half(q_ref, pos, DH ** -0.5) for pos in heads]
        t = [_dot_nt(q[pos], k_all) + (2.0 ** -(HEAD_POS[pos] + 1) * negdist + maskbias) for pos in heads]
        p = [_swa_softmax(t[pos], sink_ref[HEAD_POS[pos]])[0].astype(BF16) for pos in heads]
        o = [_dot(p[pos], v_all) for pos in heads]
        for col in range(SWA_HEADS // 2):
            o_ref[:, col * LANE:(col + 1) * LANE] = _swa_merge(o[2 * col], o[2 * col + 1])

    kvw = SWA_KV_HEADS * DH
    return pl.pallas_call(
        body, name="swa_fwd", grid=(nb + 1,),
        in_specs=[pl.BlockSpec(memory_space=pltpu.SMEM), _rows(BLK, SWA_HEADS * DH)] + _swa_kv_specs(kvw) + _swa_kv_specs(kvw),
        out_specs=_rows(BLK, SWA_HEADS * DH),
        out_shape=pltpu.HBM((_lp(), SWA_HEADS * DH), F32),
        compiler_params=_params(16, dimension_semantics=_seq()),
    )(sinks, *_hbm(qs, ks, ks, ks, vs, vs, vs))


GLA_PER_STEP = BLK // CH


def _gla_block(s):
    nb = SEQ // BLK
    return jnp.where(s == 0, nb, s - 1)


def _gla_rowmask(s):
    ri = _iota((BLK, 1), 0)
    m = jnp.where(s == 0, ((ri >= META_OFF) & (ri < CH)).astype(jnp.int32), 1)
    return (m > 0).astype(F32) + jnp.zeros((BLK, 1), F32)


def _gla_chunk_masks():
    r, c = _iota((BLK, BLK), 0), _iota((BLK, BLK), 1)
    same = ((r < CH) & (c < CH)) | ((r >= CH) & (c >= CH))
    return same & (r >= c), same & (r <= c), same


def _gla_decay(z, rmask):
    log_g = (jnp.minimum(z, 0.0) - jnp.log1p(jnp.exp(-jnp.abs(z)))) * (rmask / GLA_TAU)
    lower, _, same = _gla_chunk_masks()
    return _dot_exact(lower.astype(F32), log_g), _dot_exact(same.astype(F32), log_g)


def _gla_slices(c, h):
    return slice(c * CH, (c + 1) * CH), slice(h * DK, (h + 1) * DK), slice(h * DV, (h + 1) * DV)


def _gla_fwd(qg, kg, vg, z):
    steps = SEQ // BLK + 1
    kw, vw = GLA_HEADS * DK, GLA_HEADS * DV
    pairs = [(c, h) for c in range(GLA_PER_STEP) for h in range(GLA_HEADS)]

    def body(q_ref, k_ref, v_ref, z_ref, o_ref, st_ref, st):
        s = pl.program_id(0)

        @pl.when(s == 0)
        def _():
            st[...] = jnp.zeros_like(st)

        rmask = _gla_rowmask(s)
        b, b_last = _gla_decay(z_ref[...], rmask)
        q = q_ref[...] * (rmask * DK ** -0.5)
        k = k_ref[...] * rmask
        v = v_ref[...] * rmask
        qe = q * jnp.exp(b)
        ke = k * jnp.exp(-b)
        kd = k * jnp.exp(b_last - b)
        e_last = jnp.exp(b_last)
        causal = _iota((CH, CH), 0) >= _iota((CH, CH), 1)
        a, upd, intra = {}, {}, {}
        for c, h in pairs:
            rows, ks, vs_ = _gla_slices(c, h)
            a[c, h] = jnp.where(causal, _dot_nt(qe[rows, ks], ke[rows, ks]), 0.0)
            upd[c, h] = _dot_tn(v[rows, vs_], kd[rows, ks])
        for c, h in pairs:
            rows, ks, vs_ = _gla_slices(c, h)
            intra[c, h] = _dot(a[c, h], v[rows, vs_])
        state = st[...]
        for c in range(GLA_PER_STEP):
            st_ref[0, c] = state
            for h in range(GLA_HEADS):
                rows, ks, vs_ = _gla_slices(c, h)
                o_ref[rows, vs_] = intra[c, h] + _dot_nt(qe[rows, ks], state[:, ks])
            state = state * e_last[c * CH:c * CH + 1] + jnp.concatenate([upd[c, h] for h in range(GLA_HEADS)], axis=1)
        st[...] = state

    blk = lambda w: pl.BlockSpec((BLK, w), lambda s: (_gla_block(s), 0))
    return pl.pallas_call(
        body, name="gla_fwd", grid=(steps,),
        in_specs=[blk(kw), blk(kw), blk(vw), blk(kw)],
        out_specs=[blk(vw), pl.BlockSpec((1, GLA_PER_STEP, DV, kw), lambda s: (s, 0, 0, 0))],
        out_shape=[pltpu.HBM((_lp(), vw), F32), pltpu.HBM((steps, GLA_PER_STEP, DV, kw), F32)],
        scratch_shapes=[pltpu.VMEM((DV, kw), F32)],
        compiler_params=_params(16, dimension_semantics=_seq()),
    )(*_hbm(qg, kg, vg, z))


def _post_mix(o_s, o_gla, r_g, h0, gn4, w_out, g1, b1):
    tm = _row_tile(384)
    lp = _lp()

    def body(os_ref, og_ref, r_ref, h0_ref, gn_ref, w_ref, g_ref, b_ref, o_ref, pre_ref, h1_ref):
        for pos, h in enumerate(HEAD_POS):
            o_ref[:, h * DH:(h + 1) * DH] = os_ref[:, pos * DH:(pos + 1) * DH].astype(BF16)
        for h in range(GLA_HEADS):
            hs = slice(h * DV, (h + 1) * DV)
            xg = og_ref[:, hs]
            n = xg * lax.rsqrt(jnp.mean(xg * xg, axis=-1, keepdims=True) + RMS_EPS) * gn_ref[...]
            r = r_ref[:, hs]
            o_ref[:, 512 + h * DV:512 + (h + 1) * DV] = (n * (r * _sigmoid(r))).astype(BF16)
        pre = ALPHA * h0_ref[...] + _dot(o_ref[...], w_ref[...])
        pre_ref[...] = pre
        xhat, _ = _ln_stats(pre)
        h1_ref[...] = xhat * g_ref[...] + b_ref[...]

    return pl.pallas_call(
        body, name="post_mix", grid=(lp // tm,),
        in_specs=[_rows(tm, 512), _rows(tm, 512), _rows(tm, 512), _rows(tm, D), _const((1, DV)), _const((D, D)),
                  _const((1, D)), _const((1, D))],
        out_specs=[_rows(tm, D), _rows(tm, D), _rows(tm, D)],
        out_shape=[pltpu.HBM((lp, D), BF16), pltpu.HBM((lp, D), F32),
                   pltpu.HBM((lp, D), F32)],
        compiler_params=_params(32, dimension_semantics=_seq()),
    )(*_hbm(o_s, o_gla, r_g, h0, gn4, w_out, g1, b1))


def _ffn_fwd_loss_bwd(h1, wg_t, wu_t, wd, target, g2, b2):
    lp = _lp()
    tm = max(t for t in range(BLK, 384 + 1, BLK) if lp % t == 0)
    steps = lp // tm
    last_blk = SEQ // BLK - 1
    half = D_FF // 2
    n_t = tm // BLK

    def body(*refs):
        h_ref, wg_ref, wu_ref, wd_ref = refs[:4]
        t_refs = refs[4:4 + n_t]
        g2_ref, b2_ref, a_ref, dgate_ref, dup_ref, dp_ref, loss_ref, dg_ref, db_ref, g_s, u_s, acc = refs[4 + n_t:]
        i = pl.program_id(0)

        @pl.when(i == 0)
        def _():
            acc[...] = jnp.zeros_like(acc)
            dg_ref[...] = jnp.zeros_like(dg_ref)
            db_ref[...] = jnp.zeros_like(db_ref)

        h = h_ref[...]
        hb = h.astype(BF16)
        pre = ALPHA * h
        for j in range(2):
            cols = slice(j * half, (j + 1) * half)
            g = _dot_nt(hb, wg_ref[cols, :])
            u = _dot_nt(hb, wu_ref[cols, :])
            g_s[:, cols] = g
            u_s[:, cols] = u
            pre = pre + _dot(g * _sigmoid(g) * u, wd_ref[cols, :])
        xhat, rstd = _ln_stats(pre)
        real = i * tm + _iota((tm, 1), 0) < SEQ
        target_rows = jnp.concatenate([t[...] for t in t_refs], axis=0)
        diff = jnp.where(real, xhat * g2_ref[...] + b2_ref[...] - target_rows, 0.0)
        acc[...] += jnp.sum(diff * diff, axis=0, keepdims=True)
        dy = diff * (1.0 / D)
        dpre = _ln_bwd(dy, xhat, rstd, g2_ref[...])
        dp_ref[...] = dpre
        dg_ref[...] += jnp.sum(dy * xhat, axis=0, keepdims=True)
        db_ref[...] += jnp.sum(dy, axis=0, keepdims=True)
        dpb = dpre.astype(BF16)
        for j in range(2):
            cols = slice(j * half, (j + 1) * half)
            g, u = g_s[:, cols], u_s[:, cols]
            sg = _sigmoid(g)
            silu = g * sg
            da = _dot_nt(dpb, wd_ref[cols, :])
            a_ref[:, cols] = (silu * u).astype(BF16)
            dgate_ref[:, cols] = (da * u * (sg * (1.0 + g * (1.0 - sg)))).astype(BF16)
            dup_ref[:, cols] = (da * silu).astype(BF16)

        @pl.when(i == steps - 1)
        def _():
            loss_ref[...] = jnp.zeros_like(loss_ref) + (0.5 / D) * jnp.sum(acc[...], axis=1, keepdims=True)

    t_spec = lambda k: pl.BlockSpec((BLK, D), lambda i: (jnp.minimum(i * n_t + k, last_blk), 0))
    return pl.pallas_call(
        body, name="ffn_fwd_loss_bwd", grid=(steps,),
        in_specs=[_rows(tm, D), _const((D_FF, D)), _const((D_FF, D)), _const((D_FF, D))]
        + [t_spec(k) for k in range(n_t)] + [_const((1, D)), _const((1, D))],
        out_specs=[_rows(tm, D_FF), _rows(tm, D_FF), _rows(tm, D_FF), _rows(tm, D), _acc((1, LANE)), _acc((1, D)),
                   _acc((1, D))],
        out_shape=[pltpu.HBM((lp, D_FF), BF16)] * 3 + [pltpu.HBM((lp, D), F32), pltpu.HBM((1, LANE), F32),
                                                         pltpu.HBM((1, D), F32), pltpu.HBM((1, D), F32)],
        scratch_shapes=[pltpu.VMEM((tm, D_FF), F32), pltpu.VMEM((tm, D_FF), F32), pltpu.VMEM((1, D), F32)],
        compiler_params=_params(58, dimension_semantics=_seq()),
    )(*_hbm(h1, wg_t, wu_t, wd, *[target] * n_t, g2, b2))


def _ffn_out_bwd(dpre2, dgate, dup, pre1, wg_t, wu_t, g1, w_out, o_gla, r_g, gn4):
    tm = _row_tile(384)
    lp = _lp()

    def body(dp_ref, dg_ref, du_ref, p1_ref, wg_ref, wu_ref, g1_ref, w_ref, og_ref, r_ref, gn_ref,
             dp1_ref, dg1_ref, db1_ref, dos_ref, dog_ref, dr_ref, dgn_ref):
        @pl.when(pl.program_id(0) == 0)
        def _():
            for acc_ref in (dg1_ref, db1_ref, dgn_ref):
                acc_ref[...] = jnp.zeros_like(acc_ref)

        dh1 = ALPHA * dp_ref[...] + _dot(dg_ref[...], wg_ref[...]) + _dot(du_ref[...], wu_ref[...])
        xhat, rstd1 = _ln_stats(p1_ref[...])
        dpre1 = _ln_bwd(dh1, xhat, rstd1, g1_ref[...])
        dp1_ref[...] = dpre1
        dg1_ref[...] += jnp.sum(dh1 * xhat, axis=0, keepdims=True)
        db1_ref[...] += jnp.sum(dh1, axis=0, keepdims=True)

        do = _dot_nt(dpre1, w_ref[...])
        for pos, h in enumerate(HEAD_POS):
            dos_ref[:, pos * DH:(pos + 1) * DH] = do[:, h * DH:(h + 1) * DH]
        gn = gn_ref[...]
        for h in range(GLA_HEADS):
            hs = slice(h * DV, (h + 1) * DV)
            xg = og_ref[:, hs]
            rstd = lax.rsqrt(jnp.mean(xg * xg, axis=-1, keepdims=True) + RMS_EPS)
            nx = xg * rstd
            r = r_ref[:, hs]
            sr = _sigmoid(r)
            d_o = do[:, 512 + h * DV:512 + (h + 1) * DV]
            dr_ref[:, hs] = d_o * (nx * gn) * (sr * (1.0 + r * (1.0 - sr)))
            dn = d_o * (r * sr)
            dgn_ref[...] += jnp.sum(dn * nx, axis=0, keepdims=True)
            dnx = dn * gn
            dog_ref[:, hs] = rstd * (dnx - nx * jnp.mean(dnx * nx, axis=-1, keepdims=True))

    return pl.pallas_call(
        body, name="ffn_out_bwd", grid=(lp // tm,),
        in_specs=[_rows(tm, D), _rows(tm, D_FF), _rows(tm, D_FF), _rows(tm, D), _const((D_FF, D)), _const((D_FF, D)),
                  _const((1, D)), _const((D, D)), _rows(tm, 512), _rows(tm, 512), _const((1, DV))],
        out_specs=[_rows(tm, D), _acc((1, D)), _acc((1, D)), _rows(tm, 512), _rows(tm, 512), _rows(tm, 512),
                   _acc((1, DV))],
        out_shape=[pltpu.HBM((lp, D), F32), pltpu.HBM((1, D), F32), pltpu.HBM((1, D), F32)]
        + [pltpu.HBM((lp, 512), F32)] * 3 + [pltpu.HBM((1, DV), F32)],
        compiler_params=_params(48, dimension_semantics=_seq()),
    )(*_hbm(dpre2, dgate, dup, pre1, wg_t, wu_t, g1, w_out, o_gla, r_g, gn4))


def _atb(a, b, name):
    lp = _lp()
    tm = _row_tile(1408)
    n, w = a.shape[1], b.shape[1]
    bw = 512 if n * w * 4 > (4 << 20) else w

    def body(a_ref, b_ref, o_ref):
        @pl.when(pl.program_id(1) == 0)
        def _():
            o_ref[...] = jnp.zeros_like(o_ref)

        o_ref[...] += _dot_tn(a_ref[...], b_ref[...])

    return pl.pallas_call(
        body, name=name, grid=(w // bw, lp // tm),
        in_specs=[pl.BlockSpec((tm, n), lambda j, k: (k, 0)), pl.BlockSpec((tm, bw), lambda j, k: (k, j))],
        out_specs=pl.BlockSpec((n, bw), lambda j, k: (0, j)),
        out_shape=pltpu.HBM((n, w), F32),
        compiler_params=_params(48, dimension_semantics=_seq(2)),
    )(*_hbm(a, b))


def _gla_bwd(qg, kg, vg, z, do_gla, st_all, token):
    steps = SEQ // BLK + 1
    kw, vw = GLA_HEADS * DK, GLA_HEADS * DV
    pairs = [(c, h) for c in range(GLA_PER_STEP) for h in range(GLA_HEADS)]
    heads = range(GLA_HEADS)

    def body(q_ref, k_ref, v_ref, z_ref, do_ref, st_ref, token_ref, dq_ref, dk_ref, dv_ref, dz_ref, dst):
        @pl.when(pl.program_id(0) == 0)
        def _():
            dst[...] = jnp.zeros_like(dst)

        rmask = _gla_rowmask(steps - 1 - pl.program_id(0))
        zz = z_ref[...]
        b, b_last = _gla_decay(zz, rmask)
        e_b, e_nb, e_kd, e_last = jnp.exp(b), jnp.exp(-b), jnp.exp(b_last - b), jnp.exp(b_last)
        q = q_ref[...] * (rmask * DK ** -0.5)
        k = k_ref[...] * rmask
        v = v_ref[...] * rmask
        qe, ke, kd = q * e_b, k * e_nb, k * e_kd
        d_o = do_ref[...]
        causal = _iota((CH, CH), 0) >= _iota((CH, CH), 1)
        a, da, dqe, dke, dv_intra, carry = {}, {}, {}, {}, {}, {}
        for c, h in pairs:
            rows, ks, vs_ = _gla_slices(c, h)
            a[c, h] = jnp.where(causal, _dot_nt(qe[rows, ks], ke[rows, ks]), 0.0)
            da[c, h] = jnp.where(causal, _dot_nt(d_o[rows, vs_], v[rows, vs_]), 0.0)
            carry[c, h] = _dot_tn(d_o[rows, vs_], qe[rows, ks])
        for c, h in pairs:
            rows, ks, vs_ = _gla_slices(c, h)
            dqe[c, h] = _dot(d_o[rows, vs_], st_ref[0, c][:, ks]) + _dot(da[c, h], ke[rows, ks])
            dke[c, h] = _dot_tn(da[c, h], qe[rows, ks])
            dv_intra[c, h] = _dot_tn(a[c, h], d_o[rows, vs_])
        dstate = dst[...]
        dkd, db_decay = {}, {}
        for c in reversed(range(GLA_PER_STEP)):
            for h in heads:
                rows, ks, vs_ = _gla_slices(c, h)
                dkd[c, h] = _dot(v[rows, vs_], dstate[:, ks])
                dv_ref[rows, vs_] = dv_intra[c, h] + _dot_nt(kd[rows, ks], dstate[:, ks])
            chunk_last = e_last[c * CH:c * CH + 1]
            db_decay[c] = jnp.sum(dstate * st_ref[0, c], axis=0, keepdims=True) * chunk_last
            dstate = dstate * chunk_last + jnp.concatenate([carry[c, h] for h in heads], axis=1)
        dst[...] = dstate
        rows_of = lambda parts: jnp.concatenate(
            [jnp.concatenate([parts[c, h] for h in heads], axis=1) for c in range(GLA_PER_STEP)], axis=0)
        dqe_all, dke_all, dkd_all = rows_of(dqe), rows_of(dke), rows_of(dkd)
        dq_ref[...] = dqe_all * e_b * (rmask * DK ** -0.5)
        dk_ref[...] = (dke_all * e_nb + dkd_all * e_kd) * rmask
        dkd_kd = dkd_all * kd
        db = dqe_all * qe - dke_all * ke - dkd_kd
        _, upper, same = _gla_chunk_masks()
        decay_rows = jnp.concatenate([jnp.broadcast_to(db_decay[c], (CH, kw)) for c in range(GLA_PER_STEP)], axis=0)
        dlog_g = _dot_exact(upper.astype(F32), db) + _dot_exact(same.astype(F32), dkd_kd) + decay_rows
        dz_ref[...] = dlog_g * (rmask / GLA_TAU) * _sigmoid(-zz)

    blk = lambda w: pl.BlockSpec((BLK, w), lambda s: (_gla_block(steps - 1 - s), 0))
    return pl.pallas_call(
        body, name="gla_bwd", grid=(steps,),
        in_specs=[blk(kw), blk(kw), blk(vw), blk(kw), blk(vw),
                  pl.BlockSpec((1, GLA_PER_STEP, DV, kw), lambda s: (steps - 1 - s, 0, 0, 0)), _const(TOKEN)],
        out_specs=[blk(kw), blk(kw), blk(vw), blk(kw)],
        out_shape=[pltpu.HBM((_lp(), kw), F32), pltpu.HBM((_lp(), kw), F32),
                   pltpu.HBM((_lp(), vw), F32), pltpu.HBM((_lp(), kw), F32)],
        scratch_shapes=[pltpu.VMEM((DV, kw), F32)],
        compiler_params=_params(16, dimension_semantics=_seq()),
    )(*_hbm(qg, kg, vg, z, do_gla, st_all), token)


def _swa_bwd(sinks, qs, ks, vs, do_s, token):
    nb = SEQ // BLK
    kvw = SWA_KV_HEADS * DH
    scale = DH ** -0.5
    heads = range(SWA_HEADS)

    def body(sink_ref, q_ref, km_ref, kp_ref, kc_ref, vm_ref, vp_ref, vc_ref, do_ref, token_ref,
             dq_ref, dk_ref, dv_ref, dsink_ref, carry_k, carry_v, meta_k, meta_v):
        n = pl.program_id(0)

        @pl.when(n == 0)
        def _():
            for r in (carry_k, carry_v, meta_k, meta_v):
                r[...] = jnp.zeros_like(r)
            dsink_ref[...] = jnp.zeros_like(dsink_ref)

        @pl.when(n <= nb)
        def _():
            negdist, maskbias = _swa_bias(n)
            lane = _iota((1, LANE), 1)
            k_all = jnp.concatenate([km_ref[...], kp_ref[...], kc_ref[...]], axis=0).astype(BF16)
            v_all = jnp.concatenate([vm_ref[...], vp_ref[...], vc_ref[...]], axis=0).astype(BF16)
            q = [_swa_half(q_ref, pos, scale) for pos in heads]
            d_o = [_swa_half(do_ref, pos) for pos in heads]
            t = [_dot_nt(q[pos], k_all) + (2.0 ** -(HEAD_POS[pos] + 1) * negdist + maskbias) for pos in heads]
            dp = [_dot_nt(d_o[pos], v_all) for pos in heads]
            soft = [_swa_softmax(t[pos], sink_ref[HEAD_POS[pos]]) for pos in heads]
            p = [s[0] for s in soft]
            delta = [jnp.sum(p[pos] * dp[pos], axis=-1, keepdims=True) for pos in heads]
            ds = [(p[pos] * (dp[pos] - delta[pos])).astype(BF16) for pos in heads]
            dq = [_dot(ds[pos], k_all) for pos in heads]
            for col in range(SWA_HEADS // 2):
                dq_ref[:, col * LANE:(col + 1) * LANE] = scale * _swa_merge(dq[2 * col], dq[2 * col + 1])
            dsink = jnp.zeros((1, LANE), F32)
            for pos in heads:
                dsink = dsink + jnp.where(lane == HEAD_POS[pos],
                                          -jnp.sum(soft[pos][1] * delta[pos], axis=0, keepdims=True), 0.0)
            dsink_ref[...] += dsink
            dk3 = _dot_tn(jnp.concatenate(q, axis=0), jnp.concatenate(ds, axis=0)).T
            dv3 = _dot_tn(jnp.concatenate(d_o, axis=0), jnp.concatenate([x.astype(BF16) for x in p], axis=0)).T
            meta_k[...] += dk3[0:BLK]
            meta_v[...] += dv3[0:BLK]
            dk_ref[...] = carry_k[...] + dk3[BLK:2 * BLK]
            dv_ref[...] = carry_v[...] + dv3[BLK:2 * BLK]
            carry_k[...] = dk3[2 * BLK:3 * BLK]
            carry_v[...] = dv3[2 * BLK:3 * BLK]

        @pl.when(n == nb + 1)
        def _():
            dk_ref[...] = meta_k[...]
            dv_ref[...] = meta_v[...]

    kv_out = pl.BlockSpec((BLK, kvw), lambda n: (jnp.where(n == nb + 1, nb, jnp.clip(n - 1, 0, nb - 1)), 0))
    qblk = pl.BlockSpec((BLK, SWA_HEADS * DH), lambda n: (jnp.minimum(n, nb), 0))
    return pl.pallas_call(
        body, name="swa_bwd", grid=(nb + 2,),
        in_specs=[pl.BlockSpec(memory_space=pltpu.SMEM), qblk] + _swa_kv_specs(kvw) + _swa_kv_specs(kvw)
        + [qblk, _const(TOKEN)],
        out_specs=[qblk, kv_out, kv_out, _acc((1, LANE))],
        out_shape=[pltpu.HBM((_lp(), SWA_HEADS * DH), F32), pltpu.HBM((_lp(), kvw), F32),
                   pltpu.HBM((_lp(), kvw), F32), pltpu.HBM((1, LANE), F32)],
        scratch_shapes=[pltpu.VMEM((BLK, kvw), F32)] * 4,
        compiler_params=_params(16, dimension_semantics=_seq()),
    )(sinks, *_hbm(qs, ks, ks, ks, vs, vs, vs, do_s), token)


def _in_bwd(dqs, dks, dvs, dqg, dkg, dvg, drg, dz, dpre1, w_in_t, wg2_p):
    tm = _row_tile(384)
    lp = _lp()
    widths = (512, 128, 128, 256, 256, 512, 512)
    offs = (O_QS, O_KS, O_VS, O_QG, O_KG, O_VG, O_RG)

    def body(*refs):
        parts, (dz_ref, dp1_ref, w_ref, wg2_ref, dproj_ref, dh0_ref, dbin_ref, dbg_ref) = refs[:7], refs[7:]

        @pl.when(pl.program_id(0) == 0)
        def _():
            dbin_ref[...] = jnp.zeros_like(dbin_ref)
            dbg_ref[...] = jnp.zeros_like(dbg_ref)

        for pos, h in enumerate(HEAD_POS):
            val = parts[0][:, pos * DH:(pos + 1) * DH]
            dproj_ref[:, O_QS + h * DH:O_QS + (h + 1) * DH] = val.astype(BF16)
            dbin_ref[:, O_QS + h * DH:O_QS + (h + 1) * DH] += jnp.sum(val, axis=0, keepdims=True)
        for p_ref, off, wd in zip(parts[1:], offs[1:], widths[1:]):
            val = p_ref[...]
            dproj_ref[:, off:off + wd] = val.astype(BF16)
            dbin_ref[:, off:off + wd] += jnp.sum(val, axis=0, keepdims=True)
        dz = dz_ref[...]
        dlr = _dot_nt(dz, wg2_ref[...])
        dproj_ref[:, O_LR:O_LR + LANE] = dlr.astype(BF16)
        dbin_ref[:, O_LR:O_LR + LANE] += jnp.sum(dlr, axis=0, keepdims=True)
        dbg_ref[...] += jnp.sum(dz, axis=0, keepdims=True)
        dh0_ref[...] = ALPHA * dp1_ref[...] + _dot(dproj_ref[...], w_ref[...])

    return pl.pallas_call(
        body, name="in_bwd", grid=(lp // tm,),
        in_specs=[_rows(tm, w) for w in widths] + [_rows(tm, 256), _rows(tm, D), _const((D_IN_P, D)), _const((LANE, 256))],
        out_specs=[_rows(tm, D_IN_P), _rows(tm, D), _acc((1, D_IN_P)), _acc((1, 256))],
        out_shape=[pltpu.HBM((lp, D_IN_P), BF16), pltpu.HBM((lp, D), F32),
                   pltpu.HBM((1, D_IN_P), F32), pltpu.HBM((1, 256), F32)],
        compiler_params=_params(40, dimension_semantics=_seq()),
    )(*_hbm(dqs, dks, dvs, dqg, dkg, dvg, drg, dz, dpre1, w_in_t, wg2_p))


def _ln_in_bwd(x, meta_ext, dh0, g, token):
    tr = min(LN_ROWS, SEQ)

    def ln_bwd(x_ref, dh_ref, g_ref, dx_ref, dg_ref, db_ref):
        @pl.when(pl.program_id(0) == 0)
        def _():
            dg_ref[...] = jnp.zeros_like(dg_ref)
            db_ref[...] = jnp.zeros_like(db_ref)

        xhat, rstd = _ln_stats(x_ref[...])
        dh = dh_ref[...]
        dx_ref[...] = _ln_bwd(dh, xhat, rstd, g_ref[...])
        dg_ref[...] += jnp.sum(dh * xhat, axis=0, keepdims=True)
        db_ref[...] += jnp.sum(dh, axis=0, keepdims=True)

    def body(x_ref, dh_ref, g_ref, token_ref, dx_ref, dg_ref, db_ref):
        ln_bwd(x_ref, dh_ref, g_ref, dx_ref, dg_ref, db_ref)

    def meta_body(m_ref, dh_ref, g_ref, dm_ref, dg_ref, db_ref):
        ln_bwd(m_ref, dh_ref, g_ref, dm_ref, dg_ref, db_ref)

    sums = [pltpu.HBM((1, D), F32), pltpu.HBM((1, D), F32)]
    dx, dg, db = pl.pallas_call(
        body, name="ln_in_bwd", grid=(SEQ // tr,),
        in_specs=[_rows(tr, D), _rows(tr, D), _const((1, D)), _const(TOKEN)],
        out_specs=[_rows(tr, D), _acc((1, D)), _acc((1, D))],
        out_shape=[pltpu.HBM((SEQ, D), F32)] + sums,
        compiler_params=_params(32, dimension_semantics=_seq()),
    )(*_hbm(x, dh0, g), token)
    dm, dg_m, db_m = pl.pallas_call(
        meta_body, name="ln_in_bwd_meta", grid=(1,),
        in_specs=[_const((BLK, D)), pl.BlockSpec((BLK, D), lambda i: (SEQ // BLK, 0)), _const((1, D))],
        out_specs=[_acc((BLK, D)), _acc((1, D)), _acc((1, D))],
        out_shape=[pltpu.HBM((BLK, D), F32)] + sums,
        compiler_params=_params(16, dimension_semantics=_seq()),
    )(*_hbm(meta_ext, dh0, g))
    return dx, dm, dg + dg_m, db + db_m


def _local_step(x, target, ln_in_g, ln_in_b, b_in, bg2, sinks, gn, g1, b1, g2, b2,
                token, fetch_first, fetch_rest, ship_ffn, ship_w_in):
    row = lambda v: v.reshape(1, -1).astype(F32)
    b_in_p = jnp.pad(row(b_in), ((0, 0), (0, D_IN_P - D_IN)))
    gn4 = row(gn)
    sinks = sinks.reshape(-1).astype(F32)

    h_real = _ln_in_fwd_real(x, row(ln_in_g), row(ln_in_b), token)
    w_in_t, meta_full, wg2 = fetch_first([h_real])
    meta_ext = jnp.pad(meta_full, ((META_OFF, BLK - CH), (0, 0)))
    wg2_p = jnp.pad(wg2, ((0, LANE - wg2.shape[0]), (0, 0))).astype(BF16)
    h0 = _ln_in_fwd_meta(h_real, meta_ext, row(ln_in_g), row(ln_in_b))
    qs, ks, vs, qg, kg, vg, rg, glr, z = _in_proj(h0, w_in_t, b_in_p, wg2_p, row(bg2))
    o_s = _swa_fwd(sinks, qs, ks, vs)
    o_gla, st_all = _gla_fwd(qg, kg, vg, z)
    w_out, wg_t, wu_t, wd = fetch_rest([o_s, o_gla])
    o, pre1, h1 = _post_mix(o_s, o_gla, rg, h0, gn4, w_out, row(g1), row(b1))
    a, dgate, dup, dpre2, loss, dg2, db2 = _ffn_fwd_loss_bwd(h1, wg_t, wu_t, wd, target, row(g2), row(b2))
    dpre1, dg1, db1, do_s, do_gla, drg, dgn = _ffn_out_bwd(dpre2, dgate, dup, pre1, wg_t, wu_t, row(g1), w_out, o_gla,
                                                           rg, gn4)
    dwd = _atb(a, dpre2, "dw_down")
    dwg_t = _atb(dgate, h1, "dw_gate")
    dwu_t = _atb(dup, h1, "dw_up")
    dw_out = _atb(o, dpre1, "dw_out")
    token = ship_ffn(dict(w_out=dw_out, w_g=dwg_t, w_u=dwu_t, w_d=dwd))
    dqg, dkg, dvg, dz = _gla_bwd(qg, kg, vg, z, do_gla, st_all, token)
    dqs, dks, dvs, dsinks = _swa_bwd(sinks, qs, ks, vs, do_s, token)
    dproj, dh0, db_in_p, dbg2 = _in_bwd(dqs, dks, dvs, dqg, dkg, dvg, drg, dz, dpre1, w_in_t, wg2_p)
    token = ship_w_in(_atb(dproj, h0, "dw_in"))
    dwg2_p = _atb(glr, dz, "dw_gate_lr2")
    dx, dmeta_blk, dg_in, db_in_ln = _ln_in_bwd(x, meta_ext, dh0, row(ln_in_g), token)

    small = dict(meta_blk=dmeta_blk, ln_in_g=dg_in, ln_in_b=db_in_ln, ln1_g=dg1, ln1_b=db1, ln2_g=dg2, ln2_b=db2,
                 b_in_p=db_in_p, wg2_p=dwg2_p, bg2=dbg2, sinks=dsinks, gn=dgn, loss=loss)
    return dx, small


HBM = pl.BlockSpec(memory_space=pltpu.HBM)


def _place():
    return lax.axis_index("x"), lax.axis_index("y"), lax.axis_index("c")


def _other_chips(x, y):
    return [(1 - x, y), (x, 1 - y), (1 - x, 1 - y)]


def _dma_sems(n):
    return pltpu.SemaphoreType.DMA((n,))


def _comm_params():
    return pltpu.CompilerParams(has_side_effects=True)


SEM = pl.BlockSpec(memory_space=pltpu.SEMAPHORE)


def _ici_copies(kind, landing, srcs, lands, send_sems, recv_sems):
    x, y, c = _place()
    mine = 2 * x + y
    copies = []
    for a in range(len(srcs)):
        for j, (px, py) in enumerate(_other_chips(x, y)):
            slab = 2 * px + py if landing else mine
            if kind == "gather":
                src, dst = srcs[a].at[c], lands[a].at[slab, c]
            else:
                src, dst = srcs[a].at[2 * px + py], lands[a].at[slab]
            copies.append(pltpu.make_async_remote_copy(src, dst, send_sems.at[3 * a + j], recv_sems.at[3 * a + j],
                                                       device_id=(px, py, c), device_id_type=MESH))
    return copies


def _split_params():
    return pltpu.CompilerParams(has_side_effects=pltpu.SideEffectType.DATAFLOW_SIDE_EFFECTING)


def _ici_start(kind, srcs, land_shapes, after, name):
    n = len(srcs)
    lands = [pltpu.with_memory_space_constraint(lax.empty(s, a.dtype), pltpu.HBM) for s, a in zip(land_shapes, srcs)]

    def body(*refs):
        outs = refs[2 * n + len(after):]
        for cp in _ici_copies(kind, False, refs[:n], refs[n:2 * n], outs[0], outs[1]):
            cp.start()
        outs[-1][...] = jnp.zeros(TOKEN, F32)

    outs = pl.pallas_call(
        body, name=name, in_specs=[HBM] * (2 * n) + [pl.BlockSpec(memory_space=pl.ANY)] * len(after),
        out_specs=[SEM, SEM] + [HBM] * (2 * n) + [pl.BlockSpec(memory_space=pltpu.VMEM)],
        out_shape=[_dma_sems(3 * n)] * 2 + [pltpu.HBM(a.shape, a.dtype) for a in list(srcs) + lands]
        + [jax.ShapeDtypeStruct(TOKEN, F32)],
        input_output_aliases={i: 2 + i for i in range(2 * n)},
        compiler_params=_split_params(),
    )(*_hbm(*srcs), *lands, *after)
    return outs[:-1], outs[-1]


def _ici_wait(kind, handle, after, name):
    n = (len(handle) - 2) // 2

    def body(*refs):
        for cp in _ici_copies(kind, True, refs[:n], refs[n:2 * n], refs[2 * n], refs[2 * n + 1]):
            cp.wait_send()
            cp.wait_recv()

    outs = pl.pallas_call(
        body, name=name, in_specs=[HBM] * (2 * n) + [SEM, SEM] + [pl.BlockSpec(memory_space=pl.ANY)] * len(after),
        out_specs=[HBM] * (2 * n), out_shape=[pltpu.HBM(a.shape, a.dtype) for a in handle[2:]],
        input_output_aliases={i: i for i in range(2 * n)},
        compiler_params=_split_params(),
    )(*handle[2:], handle[0], handle[1], *after)
    return list(outs[n:])


def _sibling_forward(lands, name):
    n = len(lands)

    def body(*refs):
        outs = refs[n:2 * n]
        send_sems, recv_sems = refs[2 * n:]
        x, y, c = _place()

        def copy(a, j, half):
            px, py = _other_chips(x, y)[j]
            blk = outs[a].at[2 * px + py, half]
            return pltpu.make_async_remote_copy(blk, blk, send_sems.at[3 * a + j], recv_sems.at[3 * a + j],
                                                device_id=(x, y, 1 - c), device_id_type=MESH)

        pairs = [(a, j) for a in range(n) for j in range(3)]
        for a, j in pairs:
            copy(a, j, c).start()
        for a, j in pairs:
            copy(a, j, 1 - c).wait_recv()
        for a, j in pairs:
            copy(a, j, c).wait_send()

    return pl.pallas_call(
        body, name=name, in_specs=[HBM] * n, out_specs=[HBM] * n,
        out_shape=[pltpu.HBM(a.shape, a.dtype) for a in lands],
        input_output_aliases={a: a for a in range(n)},
        scratch_shapes=[_dma_sems(3 * n)] * 2,
        compiler_params=_comm_params(),
    )(*_hbm(*lands))


def _sibling_sum(core, grad, dtype, name):
    h = grad.shape[2]

    def body(c_ref, a_ref, g_ref, o_ref, land_ref, buf, send_sems, recv_sems, local_sem):
        s = pl.program_id(0)
        x, y, c = _place()

        def copy(k):
            return pltpu.make_async_remote_copy(g_ref.at[k, 1 - c], land_ref.at[k], send_sems.at[k], recv_sems.at[k],
                                                device_id=(x, y, 1 - c), device_id_type=MESH)

        @pl.when(s == 0)
        def _():
            for k in range(N_CHIPS):
                copy(k).start()

        copy(s).wait_recv()
        landed = pltpu.make_async_copy(land_ref.at[s], buf, local_sem)
        landed.start()
        landed.wait()
        o_ref[...] = (a_ref[0] + buf[...][None]).astype(dtype)

        @pl.when(s == N_CHIPS - 1)
        def _():
            for k in range(N_CHIPS):
                copy(k).wait_send()

    out, _ = pl.pallas_call(
        body, name=name,
        grid_spec=pltpu.PrefetchScalarGridSpec(
            num_scalar_prefetch=1, grid=(N_CHIPS,),
            in_specs=[pl.BlockSpec((1, 1, h, D), lambda s, c: (s, c[0], 0, 0)), pl.BlockSpec(memory_space=pl.ANY)],
            out_specs=[pl.BlockSpec((1, h, D), lambda s, c: (s, 0, 0)), pl.BlockSpec(memory_space=pl.ANY)],
            scratch_shapes=[pltpu.VMEM((h, D), F32), _dma_sems(N_CHIPS), _dma_sems(N_CHIPS), pltpu.SemaphoreType.DMA]),
        out_shape=[pltpu.HBM((N_CHIPS, h, D), dtype), pltpu.HBM((N_CHIPS, h, D), F32)],
        compiler_params=_params(16, dimension_semantics=_seq(), has_side_effects=True),
    )(core, *_hbm(grad, grad))
    return out


N_DEVICES = 2 * N_CHIPS
PEER_FLIPS = [(dx, dy, dc) for dx in (0, 1) for dy in (0, 1) for dc in (0, 1)][1:]


def _small_exchange(pack, after):
    n = len(PEER_FLIPS)

    def body(p_ref, *refs):
        out_ref, send_sems, recv_sems = refs[len(after):]
        x, y, c = _place()
        flip = lambda v, d: 1 - v if d else v

        def copy(k, landing):
            px, py, pc = (flip(v, d) for v, d in zip((x, y, c), PEER_FLIPS[k]))
            slab = 4 * px + 2 * py + pc if landing else 4 * x + 2 * y + c
            return pltpu.make_async_remote_copy(p_ref, out_ref.at[slab], send_sems.at[k], recv_sems.at[k],
                                                device_id=(px, py, pc), device_id_type=MESH)

        for k in range(n):
            copy(k, False).start()
        for k in range(n):
            copy(k, True).wait_recv()
        for k in range(n):
            copy(k, False).wait_send()

    return pl.pallas_call(
        body, name="small_exchange", in_specs=[HBM] + [pl.BlockSpec(memory_space=pl.ANY)] * len(after), out_specs=HBM,
        out_shape=pltpu.HBM((N_DEVICES,) + pack.shape, F32),
        scratch_shapes=[_dma_sems(n)] * 2,
        compiler_params=_comm_params(),
    )(*_hbm(pack), *after)


def _sum_chips(slots, first, rest, name):
    h = first.shape[1]

    def body(i_ref, a_ref, b_ref, c_ref, d_ref, o_ref):
        o_ref[...] = ((a_ref[...].astype(F32) + b_ref[...].astype(F32)) + c_ref[...].astype(F32)) + d_ref[...].astype(F32)

    slab = lambda k: pl.BlockSpec((1, h, D), lambda i, ix: (ix[k], 0, 0))
    return pl.pallas_call(
        body, name=name,
        grid_spec=pltpu.PrefetchScalarGridSpec(num_scalar_prefetch=1, grid=(1,),
                                               in_specs=[slab(0), slab(1), slab(2), slab(3)], out_specs=slab(4)),
        out_shape=pltpu.HBM((2, h, D), F32),
        compiler_params=_params(16, dimension_semantics=_seq()),
    )(slots, *_hbm(first, rest, rest, rest))


def _join_halves(halves):
    n = len(halves)

    def body(*refs):
        outs = refs[n:2 * n]
        send_sems, recv_sems = refs[2 * n:]
        x, y, c = _place()

        def copy(a, slab):
            return pltpu.make_async_remote_copy(outs[a].at[slab], outs[a].at[slab], send_sems.at[a], recv_sems.at[a],
                                                device_id=(x, y, 1 - c), device_id_type=MESH)

        for a in range(n):
            copy(a, c).start()
        for a in range(n):
            copy(a, 1 - c).wait_recv()
        for a in range(n):
            copy(a, c).wait_send()

    return pl.pallas_call(
        body, name="join_halves", in_specs=[HBM] * n, out_specs=[HBM] * n,
        out_shape=[pltpu.HBM(h.shape, F32) for h in halves],
        input_output_aliases={a: a for a in range(n)},
        scratch_shapes=[_dma_sems(n)] * 2,
        compiler_params=_comm_params(),
    )(*_hbm(*halves))


def _chip_partials(grads, wire_dtypes, names):
    core = lax.axis_index("c").astype(jnp.int32).reshape(1)
    return [_sibling_sum(core, g, dt, "sibling_sum_" + nm) for g, dt, nm in zip(grads, wire_dtypes, names)]


def _finish_reduce(parts, got, names):
    x, y, c = _place()
    others = [2 * px + py for px, py in _other_chips(x, y)]
    own_first = jnp.stack([2 * x + y] + others + [c]).astype(jnp.int32)
    halves = [_sum_chips(own_first, p, q, "sum_chips_" + nm) for p, q, nm in zip(parts, got, names)]
    return [f.reshape(2 * f.shape[1], D) for f in _join_halves(halves)]


def _adamw(w, g, m, v, name):
    rows, cols = w.shape
    if rows % 8 == 0:
        tr = max(t for t in range(8, 257, 8) if rows % t == 0)
        grid, blk = (rows // tr,), pl.BlockSpec((tr, cols), lambda i: (i, 0))
    else:
        grid, blk = (cols // 256,), pl.BlockSpec((rows, 256), lambda i: (0, i))

    def body(w_ref, g_ref, m_ref, v_ref, d_ref, nm_ref, nv_ref):
        d_ref[...], nm_ref[...], nv_ref[...] = _adamw_math(w_ref[...], g_ref[...], m_ref[...], v_ref[...])

    return pl.pallas_call(
        body, name=name, grid=grid,
        in_specs=[blk] * 4, out_specs=[blk] * 3,
        out_shape=[pltpu.HBM(w.shape, F32)] * 3,
        compiler_params=_params(32, dimension_semantics=_seq()),
    )(*_hbm(w, g, m, v))


def _adamw_math(w, g, m, v):
    nm = ADAM_B1 * m + (1.0 - ADAM_B1) * g
    nv = ADAM_B2 * v + (1.0 - ADAM_B2) * (g * g)
    m_hat = nm / (1.0 - ADAM_B1 ** ADAM_STEP)
    v_hat = nv / (1.0 - ADAM_B2 ** ADAM_STEP)
    return -ADAM_LR * (m_hat / (jnp.sqrt(v_hat) + ADAM_EPS) + ADAM_WD * w), nm, nv


SMALL = (("meta_tokens", (N_META, D // N_CHIPS)), ("ln_in_g", (1, D)), ("ln_in_b", (1, D)), ("b_in", (1, D_IN)),
         ("w_gate_lr2", (GATE_RANK, GLA_HEADS * DK // N_CHIPS)), ("b_gate_lr2", (1, GLA_HEADS * DK)),
         ("attn_sinks", (1, SWA_HEADS)),
         ("gla_norm_g", (1, DV)), ("ln1_g", (1, D)), ("ln1_b", (1, D)), ("ln2_g", (1, D)), ("ln2_b", (1, D)))
ROW_META, ROW_B_IN, ROW_TAIL, ROW_WG2 = 0, 22, 25, 32
ROW_LN = dict(ln_in_g=16, ln_in_b=17, ln1_g=18, ln1_b=19, ln2_g=20, ln2_b=21)
TAIL_BG2, TAIL_SINKS, TAIL_GN, TAIL_LOSS = 0, 256, 256 + SWA_HEADS, 256 + SWA_HEADS + DV


def _adamw_small(place, packs, own, params):
    n = len(SMALL)

    def body(place_ref, packs_ref, own_ref, *refs):
        ins, outs, p_ref = refs[:3 * n], refs[3 * n:-1], refs[-1]
        me, c = place_ref[0], place_ref[1]
        total = jnp.where(me == 0, own_ref[...], packs_ref[0])
        for i in range(1, N_DEVICES):
            total = total + jnp.where(me == i, own_ref[...], packs_ref[i])
        p_ref[...] = total
        outs[4 * n][...] = total[ROW_TAIL:ROW_TAIL + 1, :]

        def mine(width, rows):
            part = lambda s: p_ref[rows, s * width:(s + 1) * width]
            return jnp.where(c == 0, part(0), jnp.where(c == 1, part(1), jnp.where(c == 2, part(2), part(3))))

        tail = lambda lo, width: p_ref[ROW_TAIL:ROW_TAIL + 1, lo:lo + width]
        grads = dict(
            meta_tokens=mine(D // N_CHIPS, slice(ROW_META, ROW_META + N_META)),
            b_in=jnp.concatenate([p_ref[ROW_B_IN:ROW_B_IN + 1, :], p_ref[ROW_B_IN + 1:ROW_B_IN + 2, :],
                                  p_ref[ROW_B_IN + 2:ROW_B_IN + 3, 0:D_IN - 2 * D]], axis=1),
            w_gate_lr2=mine(256 // N_CHIPS, slice(ROW_WG2, ROW_WG2 + 16)),
            b_gate_lr2=tail(TAIL_BG2, 256), attn_sinks=tail(TAIL_SINKS, SWA_HEADS), gla_norm_g=tail(TAIL_GN, DV),
            **{k: p_ref[r:r + 1, :] for k, r in ROW_LN.items()})
        for i, (name, _) in enumerate(SMALL):
            g = grads[name]
            outs[4 * i][...] = g
            outs[4 * i + 1][...], outs[4 * i + 2][...], outs[4 * i + 3][...] = _adamw_math(
                ins[3 * i][...], g, ins[3 * i + 1][...], ins[3 * i + 2][...])

    whole = lambda shape: pl.BlockSpec(shape, lambda i, c: (0,) * len(shape))
    outs = pl.pallas_call(
        body, name="adamw_small",
        grid_spec=pltpu.PrefetchScalarGridSpec(
            num_scalar_prefetch=1, grid=(1,),
            in_specs=[whole(packs.shape), whole(own.shape)] + [whole(s) for _, s in SMALL for _ in range(3)],
            out_specs=[whole(s) for _, s in SMALL for _ in range(4)] + [whole((1, D))],
            scratch_shapes=[pltpu.VMEM(own.shape, F32)]),
        out_shape=[pltpu.HBM(s, F32) for _, s in SMALL for _ in range(4)] + [pltpu.HBM((1, D), F32)],
        compiler_params=_params(16, dimension_semantics=_seq()),
    )(place, *_hbm(packs, own, *[a for p in params for a in p]))
    return [outs[4 * i:4 * i + 4] for i in range(n)], outs[4 * n]


def _small_pack(gr):
    names = ["meta_blk"] + list(ROW_LN) + ["b_in_p", "wg2_p", "bg2", "sinks", "gn", "loss"]
    gate_w = GLA_HEADS * DK

    def body(*refs):
        src, out = dict(zip(names, refs)), refs[-1]
        out[...] = jnp.zeros_like(out)
        out[ROW_META:ROW_META + N_META, :] = src["meta_blk"][META_OFF:CH, :]
        for k, r in ROW_LN.items():
            out[r:r + 1, :] = src[k][...]
        for j in range(-(-D_IN // D)):
            width = min(D, D_IN - j * D)
            out[ROW_B_IN + j:ROW_B_IN + j + 1, 0:width] = src["b_in_p"][:, j * D:j * D + width]
        tail = slice(ROW_TAIL, ROW_TAIL + 1)
        out[tail, TAIL_BG2:TAIL_BG2 + gate_w] = src["bg2"][...]
        out[tail, TAIL_SINKS:TAIL_SINKS + SWA_HEADS] = src["sinks"][:, 0:SWA_HEADS]
        out[tail, TAIL_GN:TAIL_GN + DV] = src["gn"][...]
        out[tail, TAIL_LOSS:TAIL_LOSS + 1] = src["loss"][:, 0:1]
        out[ROW_WG2:ROW_WG2 + GATE_RANK, 0:gate_w] = src["wg2_p"][0:GATE_RANK, :]

    arrays = [gr[k] for k in names]
    return pl.pallas_call(
        body, name="small_pack", grid=(1,),
        in_specs=[_acc(a.shape) for a in arrays], out_specs=_acc((SMALL_ROWS, D)),
        out_shape=pltpu.HBM((SMALL_ROWS, D), F32),
        compiler_params=_params(16, dimension_semantics=_seq()),
    )(*_hbm(*arrays))


BIG = ("w_in", "w_out", "w_g", "w_u", "w_d")


def kernel(x, meta_tokens, ln_in_g, ln_in_b, w_in, b_in, w_gate_lr2, b_gate_lr2, attn_sinks, gla_norm_g, w_out, ln1_g, ln1_b, w_ffn_gate, w_ffn_up, w_ffn_down, ln2_g, ln2_b, loss_target, m_meta_tokens, m_ln_in_g, m_ln_in_b, m_w_in, m_b_in, m_w_gate_lr2, m_b_gate_lr2, m_attn_sinks, m_gla_norm_g, m_w_out, m_ln1_g, m_ln1_b, m_w_ffn_gate, m_w_ffn_up, m_w_ffn_down, m_ln2_g, m_ln2_b, v_meta_tokens, v_ln_in_g, v_ln_in_b, v_w_in, v_b_in, v_w_gate_lr2, v_b_gate_lr2, v_attn_sinks, v_gla_norm_g, v_w_out, v_ln1_g, v_ln1_b, v_w_ffn_gate, v_w_ffn_up, v_w_ffn_down, v_ln2_g, v_ln2_b):
    chip = 2 * lax.axis_index("x") + lax.axis_index("y")

    halves = lambda a: a.reshape(2, a.shape[0] // 2, a.shape[1])
    r_in = SHARD_ROWS["w_in"]
    first = [halves(a) for a in (jnp.pad(w_in[0].T.astype(BF16), ((0, W_IN_WIN - r_in), (0, 0))), meta_tokens,
                                 w_gate_lr2[0])]
    rest = [halves(a) for a in (w_out[0].astype(BF16), w_ffn_gate[0].T.astype(BF16), w_ffn_up[0].T.astype(BF16),
                                w_ffn_down[0].astype(BF16))]
    lands = lambda arrs: [(N_CHIPS,) + a.shape for a in arrs]
    first_handle, first_token = _ici_start("gather", first, lands(first), [], "gather_first_start")
    rest_handle, token = _ici_start("gather", rest, lands(rest), [first_token], "gather_rest_start")

    def fetch(handle, shards, after, name):
        got = _sibling_forward(_ici_wait("gather", handle, after, name + "_wait"), name + "_forward")
        return [lax.dynamic_update_index_in_dim(g, s, chip, axis=0) for g, s in zip(got, shards)]

    def fetch_first(after):
        g_in, g_meta, g_wg2 = fetch(first_handle, first, after, "gather_first")
        w_in_t = jnp.pad(g_in.reshape(N_CHIPS, W_IN_WIN, D)[:, :r_in].reshape(D_IN, D), ((0, D_IN_P - D_IN), (0, 0)))
        meta_full = jnp.concatenate([g_meta[s].reshape(N_META, -1) for s in range(N_CHIPS)], axis=1)
        wg2_full = jnp.concatenate([g_wg2[s].reshape(w_gate_lr2.shape[1], -1) for s in range(N_CHIPS)], axis=1)
        return w_in_t, meta_full, wg2_full

    def fetch_rest(after):
        return [g.reshape(-1, D) for g in fetch(rest_handle, rest, after, "gather_rest")]

    sent = {}

    def ship(key, grads, names):
        parts = _chip_partials([g.reshape(N_CHIPS, 2, -1, D) for g in grads], [BF16] * len(grads), names)
        handle, ship_token = _ici_start("scatter", parts, [p.shape for p in parts], [], "scatter_" + key + "_start")
        sent[key] = (parts, handle)
        return ship_token

    def ship_ffn(g):
        return ship("ffn", [g[k] for k in BIG[1:]], list(BIG[1:]))

    def ship_w_in(dw_in_t):
        win_start = [s * r_in // BF16_ROWS * BF16_ROWS for s in range(N_CHIPS)]
        return ship("w_in", [jnp.stack([dw_in_t[st:st + W_IN_WIN] for st in win_start])], ["w_in"])

    dx, gr = _local_step(
        x[0], loss_target[0], ln_in_g, ln_in_b, b_in[0], b_gate_lr2[0], attn_sinks[0], gla_norm_g[0], ln1_g[0],
        ln1_b[0], ln2_g[0], ln2_b[0], token, fetch_first, fetch_rest, ship_ffn, ship_w_in)
    ffn_got = _ici_wait("scatter", sent["ffn"][1], [dx], "scatter_ffn_wait")
    w_in_got = _ici_wait("scatter", sent["w_in"][1], [dx], "scatter_w_in_wait")

    small_own = _small_pack(gr)
    small_all = _small_exchange(small_own, [w_in_got[0]])
    red = _finish_reduce(sent["w_in"][0] + sent["ffn"][0], w_in_got + ffn_got, list(BIG))

    big_g = dict(zip(BIG, red))
    big_g["w_in"] = lax.dynamic_slice_in_dim(red[0], chip * (r_in % BF16_ROWS), r_in, axis=0)
    grads = dict(w_in=big_g["w_in"].T[None], w_out=big_g["w_out"][None], w_ffn_gate=big_g["w_g"].T[None],
                 w_ffn_up=big_g["w_u"].T[None], w_ffn_down=big_g["w_d"][None])
    weights = dict(meta_tokens=meta_tokens, ln_in_g=ln_in_g, ln_in_b=ln_in_b, w_in=w_in, b_in=b_in,
                   w_gate_lr2=w_gate_lr2, b_gate_lr2=b_gate_lr2, attn_sinks=attn_sinks, gla_norm_g=gla_norm_g,
                   w_out=w_out, ln1_g=ln1_g, ln1_b=ln1_b, w_ffn_gate=w_ffn_gate, w_ffn_up=w_ffn_up,
                   w_ffn_down=w_ffn_down, ln2_g=ln2_g, ln2_b=ln2_b)
    m_in = dict(meta_tokens=m_meta_tokens, ln_in_g=m_ln_in_g, ln_in_b=m_ln_in_b, w_in=m_w_in, b_in=m_b_in,
                w_gate_lr2=m_w_gate_lr2, b_gate_lr2=m_b_gate_lr2, attn_sinks=m_attn_sinks, gla_norm_g=m_gla_norm_g,
                w_out=m_w_out, ln1_g=m_ln1_g, ln1_b=m_ln1_b, w_ffn_gate=m_w_ffn_gate, w_ffn_up=m_w_ffn_up,
                w_ffn_down=m_w_ffn_down, ln2_g=m_ln2_g, ln2_b=m_ln2_b)
    v_in = dict(meta_tokens=v_meta_tokens, ln_in_g=v_ln_in_g, ln_in_b=v_ln_in_b, w_in=v_w_in, b_in=v_b_in,
                w_gate_lr2=v_w_gate_lr2, b_gate_lr2=v_b_gate_lr2, attn_sinks=v_attn_sinks, gla_norm_g=v_gla_norm_g,
                w_out=v_w_out, ln1_g=v_ln1_g, ln1_b=v_ln1_b, w_ffn_gate=v_w_ffn_gate, w_ffn_up=v_w_ffn_up,
                w_ffn_down=v_w_ffn_down, ln2_g=v_ln2_g, ln2_b=v_ln2_b)
    names = list(weights)
    big_names = ("w_in", "w_out", "w_ffn_gate", "w_ffn_up", "w_ffn_down")

    delta, new_m, new_v = {}, {}, {}
    for k, kk in zip(big_names, BIG):
        flip = (lambda a: a.T) if kk in ("w_in", "w_g", "w_u") else (lambda a: a)
        d_, m_, v_ = _adamw(flip(weights[k][0]), big_g[kk], flip(m_in[k][0]), flip(v_in[k][0]), "adamw_" + k)
        delta[k], new_m[k], new_v[k] = (flip(t)[None] for t in (d_, m_, v_))
    small_in = [tuple(src[k].reshape(shape) for src in (weights, m_in, v_in)) for k, shape in SMALL]
    place = jnp.stack([2 * chip + lax.axis_index("c"), chip]).astype(jnp.int32)
    small_out, tail_row = _adamw_small(place, small_all, small_own, small_in)
    for (k, _), results in zip(SMALL, small_out):
        grads[k], delta[k], new_m[k], new_v[k] = (r.reshape(weights[k].shape) for r in results)

    return (tail_row[0, TAIL_LOSS], dx[None], *[grads[k] for k in names], *[delta[k] for k in names], *[new_m[k] for k in names],
            *[new_v[k] for k in names])
```

```python
import jax
import jax.numpy as jnp
from jax import lax
from jax.experimental import pallas as pl
from jax.experimental.pallas import tpu as pltpu

F32 = jnp.float32
BF16 = jnp.bfloat16
MESH = pl.DeviceIdType.MESH

D = 1024
SEQ = 4096
N_META = 16
SWA_HEADS, SWA_KV_HEADS, DH = 8, 2, 64
WINDOW = 128
GLA_HEADS, DK, DV = 4, 64, 128
GLA_TAU = 16.0
CH = 64
D_FF = 2816
D_IN = 2320
LN_EPS = 1e-5
RMS_EPS = 1e-6
ALPHA = 2.0 ** 0.25
NEG = -1e30
ADAM_LR, ADAM_B1, ADAM_B2, ADAM_EPS, ADAM_WD, ADAM_STEP = 0.001, 0.9, 0.999, 1e-8, 0.01, 10
O_QS, O_KS, O_VS, O_QG, O_KG, O_VG, O_RG, O_LR = 0, 512, 640, 768, 1024, 1280, 1792, 2304

LANE = 128
BLK = WINDOW
GATE_RANK = 16
D_IN_P = D_IN + LANE - GATE_RANK
META_OFF = CH - N_META
HEAD_POS = (0, 4, 1, 5, 2, 6, 3, 7)
LN_ROWS = 512
TOKEN = (8, LANE)
N_CHIPS = 4
SHARD_ROWS = dict(w_in=D_IN // N_CHIPS, w_out=D // N_CHIPS, w_g=D_FF // N_CHIPS, w_u=D_FF // N_CHIPS,
                  w_d=D_FF // N_CHIPS)
SMALL_ROWS = 48
BF16_ROWS = 16
W_IN_WIN = -(-SHARD_ROWS["w_in"] // (2 * BF16_ROWS)) * 2 * BF16_ROWS
VMEM_CAP_MB = 64
VMEM_SPARE_MB = 6


def _lp():
    return SEQ + BLK


def _row_tile(cap):
    lp = _lp()
    return max(t for t in range(16, cap + 1, 16) if lp % t == 0)


def _params(vmem_mb, **kw):
    assert vmem_mb <= VMEM_CAP_MB - VMEM_SPARE_MB
    return pltpu.CompilerParams(vmem_limit_bytes=vmem_mb << 20, **kw)


def _seq(n=1):
    return ("arbitrary",) * n


def _const(shape):
    return pl.BlockSpec(shape, lambda *_: (0,) * len(shape), pipeline_mode=pl.Buffered(1))


def _acc(shape):
    return pl.BlockSpec(shape, lambda *_: (0,) * len(shape))


def _rows(tm, width):
    return pl.BlockSpec((tm, width), lambda i: (i, 0))


def _dot(a, b):
    return jnp.dot(a.astype(BF16), b.astype(BF16), preferred_element_type=F32)


def _dot_nt(a, b):
    return lax.dot_general(a.astype(BF16), b.astype(BF16), (((1,), (1,)), ((), ())), preferred_element_type=F32)


def _dot_tn(a, b):
    return lax.dot_general(a.astype(BF16), b.astype(BF16), (((0,), (0,)), ((), ())), preferred_element_type=F32)


def _dot_exact(a, b):
    return jnp.dot(a, b, precision=lax.Precision.HIGHEST, preferred_element_type=F32)


def _ln_stats(x):
    mu = jnp.mean(x, axis=-1, keepdims=True)
    xc = x - mu
    rstd = lax.rsqrt(jnp.mean(xc * xc, axis=-1, keepdims=True) + LN_EPS)
    return xc * rstd, rstd


def _ln_bwd(dy, xhat, rstd, g):
    dxh = dy * g
    return rstd * (dxh - jnp.mean(dxh, axis=-1, keepdims=True) - xhat * jnp.mean(dxh * xhat, axis=-1, keepdims=True))


def _sigmoid(x):
    return 1.0 / (1.0 + jnp.exp(-x))


def _iota(shape, dim):
    return lax.broadcasted_iota(jnp.int32, shape, dim)


def _hbm(*arrays):
    return tuple(pltpu.with_memory_space_constraint(a, pltpu.HBM) for a in arrays)


def _ln_in_fwd_real(x, g, b, token):
    tr = min(LN_ROWS, SEQ)

    def body(x_ref, g_ref, b_ref, token_ref, h_ref):
        xhat, _ = _ln_stats(x_ref[...])
        h_ref[...] = xhat * g_ref[...] + b_ref[...]

    return pl.pallas_call(
        body, name="ln_in_fwd", grid=(SEQ // tr,),
        in_specs=[_rows(tr, D), _const((1, D)), _const((1, D)), _const(TOKEN)],
        out_specs=_rows(tr, D),
        out_shape=pltpu.HBM((_lp(), D), F32),
        compiler_params=_params(32, dimension_semantics=_seq()),
    )(*_hbm(x, g, b), token)


def _ln_in_fwd_meta(h_real, meta_ext, g, b):
    def meta_body(m_ref, g_ref, b_ref, real_ref, h_ref):
        xhat, _ = _ln_stats(m_ref[...])
        h_ref[...] = xhat * g_ref[...] + b_ref[...]

    return pl.pallas_call(
        meta_body, name="ln_in_fwd_meta", grid=(1,),
        in_specs=[_const((BLK, D)), _const((1, D)), _const((1, D)), pl.BlockSpec(memory_space=pl.ANY)],
        out_specs=pl.BlockSpec((BLK, D), lambda i: (SEQ // BLK, 0)),
        out_shape=pltpu.HBM((_lp(), D), F32),
        input_output_aliases={3: 0},
        compiler_params=_params(16, dimension_semantics=_seq()),
    )(*_hbm(meta_ext, g, b, h_real))


def _in_proj(h0, w_in_t, b_in_p, wg2_p, bg2):
    tm = _row_tile(384)
    lp = _lp()
    widths = (512, 128, 128, 256, 256, 512, 512, 128)
    offs = (O_QS, O_KS, O_VS, O_QG, O_KG, O_VG, O_RG, O_LR)

    def body(h_ref, w_ref, b_ref, wg2_ref, bg2_ref, *outs):
        proj = _dot_nt(h_ref[...], w_ref[...]) + b_ref[...]
        for pos, h in enumerate(HEAD_POS):
            outs[0][:, pos * DH:(pos + 1) * DH] = proj[:, O_QS + h * DH:O_QS + (h + 1) * DH]
        for o_ref, off, wd in zip(outs[1:8], offs[1:], widths[1:]):
            o_ref[...] = proj[:, off:off + wd]
        outs[8][...] = _dot(proj[:, O_LR:O_LR + LANE], wg2_ref[...]) + bg2_ref[...]

    return pl.pallas_call(
        body, name="in_proj", grid=(lp // tm,),
        in_specs=[_rows(tm, D), _const((D_IN_P, D)), _const((1, D_IN_P)), _const((LANE, 256)), _const((1, 256))],
        out_specs=[_rows(tm, w) for w in widths] + [_rows(tm, 256)],
        out_shape=[pltpu.HBM((lp, w), F32) for w in widths] + [pltpu.HBM((lp, 256), F32)],
        compiler_params=_params(40, dimension_semantics=_seq()),
    )(*_hbm(h0, w_in_t, b_in_p, wg2_p, bg2))


def _swa_masks(n):
    nb = SEQ // BLK
    is_meta = n == nb
    ri = _iota((BLK, BLK), 0)
    cj = _iota((BLK, BLK), 1)
    meta_col = ((cj >= META_OFF) & (cj < CH)).astype(jnp.int32)
    meta_q = meta_col * ((cj <= ri) & (ri < CH)).astype(jnp.int32)
    valid_m = jnp.where(is_meta, meta_q, meta_col) > 0
    dist_m = jnp.where(is_meta, ri - cj, n * BLK + ri + CH - cj).astype(F32)
    valid_p = jnp.where((n >= 1) & (n < nb), (cj > ri).astype(jnp.int32), 0) > 0
    dist_p = (ri + BLK - cj).astype(F32)
    valid_c = jnp.where(n < nb, (cj <= ri).astype(jnp.int32), 0) > 0
    dist_c = (ri - cj).astype(F32)
    return (dist_m, dist_p, dist_c), (valid_m, valid_p, valid_c)


def _swa_bias(n):
    dists, valids = _swa_masks(n)
    return (jnp.concatenate([-d for d in dists], axis=1),
            jnp.concatenate([jnp.where(v, 0.0, NEG) for v in valids], axis=1))


def _swa_half(ref, pos, scale=1.0):
    col = ref[:, (pos // 2) * LANE:(pos // 2 + 1) * LANE]
    lane = _iota((BLK, LANE), 1)
    mine = lane < DH if pos % 2 == 0 else lane >= DH
    return jnp.where(mine, col * scale, 0.0).astype(BF16)


def _swa_merge(even, odd):
    return jnp.where(_iota((BLK, LANE), 1) < DH, even, odd)


def _swa_softmax(t, sink):
    m = jnp.maximum(jnp.max(t, axis=-1, keepdims=True), sink)
    e = jnp.exp(t - m)
    e_sink = jnp.exp(sink - m)
    inv = 1.0 / (jnp.sum(e, axis=-1, keepdims=True) + e_sink)
    return e * inv, e_sink * inv


def _swa_kv_specs(width):
    nb = SEQ // BLK
    return [pl.BlockSpec((BLK, width), lambda n: (nb, 0)),
            pl.BlockSpec((BLK, width), lambda n: (jnp.clip(n - 1, 0, nb - 1), 0)),
            pl.BlockSpec((BLK, width), lambda n: (jnp.minimum(n, nb), 0))]


def _swa_fwd(sinks, qs, ks, vs):
    nb = SEQ // BLK
    heads = range(SWA_HEADS)

    def body(sink_ref, q_ref, km_ref, kp_ref, kc_ref, vm_ref, vp_ref, vc_ref, o_ref):
        negdist, maskbias = _swa_bias(pl.program_id(0))
        k_all = jnp.concatenate([km_ref[...], kp_ref[...], kc_ref[...]], axis=0).astype(BF16)
        v_all = jnp.concatenate([vm_ref[...], vp_ref[...], vc_ref[...]], axis=0).astype(BF16)
        q = [_swa_half(q_ref, pos, DH ** -0.5) for pos in heads]
        t = [_dot_nt(q[pos], k_all) + (2.0 ** -(HEAD_POS[pos] + 1) * negdist + maskbias) for pos in heads]
        p = [_swa_softmax(t[pos], sink_ref[HEAD_POS[pos]])[0].astype(BF16) for pos in heads]
        o = [_dot(p[pos], v_all) for pos in heads]
        for col in range(SWA_HEADS // 2):
            o_ref[:, col * LANE:(col + 1) * LANE] = _swa_merge(o[2 * col], o[2 * col + 1])

    kvw = SWA_KV_HEADS * DH
    return pl.pallas_call(
        body, name="swa_fwd", grid=(nb + 1,),
        in_specs=[pl.BlockSpec(memory_space=pltpu.SMEM), _rows(BLK, SWA_HEADS * DH)] + _swa_kv_specs(kvw) + _swa_kv_specs(kvw),
        out_specs=_rows(BLK, SWA_HEADS * DH),
        out_shape=pltpu.HBM((_lp(), SWA_HEADS * DH), F32),
        compiler_params=_params(16, dimension_semantics=_seq()),
    )(sinks, *_hbm(qs, ks, ks, ks, vs, vs, vs))


GLA_PER_STEP = BLK // CH


def _gla_block(s):
    nb = SEQ // BLK
    return jnp.where(s == 0, nb, s - 1)


def _gla_rowmask(s):
    ri = _iota((BLK, 1), 0)
    m = jnp.where(s == 0, ((ri >= META_OFF) & (ri < CH)).astype(jnp.int32), 1)
    return (m > 0).astype(F32) + jnp.zeros((BLK, 1), F32)


def _gla_chunk_masks():
    r, c = _iota((BLK, BLK), 0), _iota((BLK, BLK), 1)
    same = ((r < CH) & (c < CH)) | ((r >= CH) & (c >= CH))
    return same & (r >= c), same & (r <= c), same


def _gla_decay(z, rmask):
    log_g = (jnp.minimum(z, 0.0) - jnp.log1p(jnp.exp(-jnp.abs(z)))) * (rmask / GLA_TAU)
    lower, _, same = _gla_chunk_masks()
    return _dot_exact(lower.astype(F32), log_g), _dot_exact(same.astype(F32), log_g)


def _gla_slices(c, h):
    return slice(c * CH, (c + 1) * CH), slice(h * DK, (h + 1) * DK), slice(h * DV, (h + 1) * DV)


def _gla_fwd(qg, kg, vg, z):
    steps = SEQ // BLK + 1
    kw, vw = GLA_HEADS * DK, GLA_HEADS * DV
    pairs = [(c, h) for c in range(GLA_PER_STEP) for h in range(GLA_HEADS)]

    def body(q_ref, k_ref, v_ref, z_ref, o_ref, st_ref, st):
        s = pl.program_id(0)

        @pl.when(s == 0)
        def _():
            st[...] = jnp.zeros_like(st)

        rmask = _gla_rowmask(s)
        b, b_last = _gla_decay(z_ref[...], rmask)
        q = q_ref[...] * (rmask * DK ** -0.5)
        k = k_ref[...] * rmask
        v = v_ref[...] * rmask
        qe = q * jnp.exp(b)
        ke = k * jnp.exp(-b)
        kd = k * jnp.exp(b_last - b)
        e_last = jnp.exp(b_last)
        causal = _iota((CH, CH), 0) >= _iota((CH, CH), 1)
        a, upd, intra = {}, {}, {}
        for c, h in pairs:
            rows, ks, vs_ = _gla_slices(c, h)
            a[c, h] = jnp.where(causal, _dot_nt(qe[rows, ks], ke[rows, ks]), 0.0)
            upd[c, h] = _dot_tn(v[rows, vs_], kd[rows, ks])
        for c, h in pairs:
            rows, ks, vs_ = _gla_slices(c, h)
            intra[c, h] = _dot(a[c, h], v[rows, vs_])
        state = st[...]
        for c in range(GLA_PER_STEP):
            st_ref[0, c] = state
            for h in range(GLA_HEADS):
                rows, ks, vs_ = _gla_slices(c, h)
                o_ref[rows, vs_] = intra[c, h] + _dot_nt(qe[rows, ks], state[:, ks])
            state = state * e_last[c * CH:c * CH + 1] + jnp.concatenate([upd[c, h] for h in range(GLA_HEADS)], axis=1)
        st[...] = state

    blk = lambda w: pl.BlockSpec((BLK, w), lambda s: (_gla_block(s), 0))
    return pl.pallas_call(
        body, name="gla_fwd", grid=(steps,),
        in_specs=[blk(kw), blk(kw), blk(vw), blk(kw)],
        out_specs=[blk(vw), pl.BlockSpec((1, GLA_PER_STEP, DV, kw), lambda s: (s, 0, 0, 0))],
        out_shape=[pltpu.HBM((_lp(), vw), F32), pltpu.HBM((steps, GLA_PER_STEP, DV, kw), F32)],
        scratch_shapes=[pltpu.VMEM((DV, kw), F32)],
        compiler_params=_params(16, dimension_semantics=_seq()),
    )(*_hbm(qg, kg, vg, z))


def _post_mix(o_s, o_gla, r_g, h0, gn4, w_out, g1, b1):
    tm = _row_tile(384)
    lp = _lp()

    def body(os_ref, og_ref, r_ref, h0_ref, gn_ref, w_ref, g_ref, b_ref, o_ref, pre_ref, h1_ref):
        for pos, h in enumerate(HEAD_POS):
            o_ref[:, h * DH:(h + 1) * DH] = os_ref[:, pos * DH:(pos + 1) * DH].astype(BF16)
        for h in range(GLA_HEADS):
            hs = slice(h * DV, (h + 1) * DV)
            xg = og_ref[:, hs]
            n = xg * lax.rsqrt(jnp.mean(xg * xg, axis=-1, keepdims=True) + RMS_EPS) * gn_ref[...]
            r = r_ref[:, hs]
            o_ref[:, 512 + h * DV:512 + (h + 1) * DV] = (n * (r * _sigmoid(r))).astype(BF16)
        pre = ALPHA * h0_ref[...] + _dot(o_ref[...], w_ref[...])
        pre_ref[...] = pre
        xhat, _ = _ln_stats(pre)
        h1_ref[...] = xhat * g_ref[...] + b_ref[...]

    return pl.pallas_call(
        body, name="post_mix", grid=(lp // tm,),
        in_specs=[_rows(tm, 512), _rows(tm, 512), _rows(tm, 512), _rows(tm, D), _const((1, DV)), _const((D, D)),
                  _const((1, D)), _const((1, D))],
        out_specs=[_rows(tm, D), _rows(tm, D), _rows(tm, D)],
        out_shape=[pltpu.HBM((lp, D), BF16), pltpu.HBM((lp, D), F32),
                   pltpu.HBM((lp, D), F32)],
        compiler_params=_params(32, dimension_semantics=_seq()),
    )(*_hbm(o_s, o_gla, r_g, h0, gn4, w_out, g1, b1))


def _ffn_fwd_loss_bwd(h1, wg_t, wu_t, wd, target, g2, b2):
    lp = _lp()
    tm = max(t for t in range(BLK, 384 + 1, BLK) if lp % t == 0)
    steps = lp // tm
    last_blk = SEQ // BLK - 1
    half = D_FF // 2
    n_t = tm // BLK

    def body(*refs):
        h_ref, wg_ref, wu_ref, wd_ref = refs[:4]
        t_refs = refs[4:4 + n_t]
        g2_ref, b2_ref, a_ref, dgate_ref, dup_ref, dp_ref, loss_ref, dg_ref, db_ref, g_s, u_s, acc = refs[4 + n_t:]
        i = pl.program_id(0)

        @pl.when(i == 0)
        def _():
            acc[...] = jnp.zeros_like(acc)
            dg_ref[...] = jnp.zeros_like(dg_ref)
            db_ref[...] = jnp.zeros_like(db_ref)

        h = h_ref[...]
        hb = h.astype(BF16)
        pre = ALPHA * h
        for j in range(2):
            cols = slice(j * half, (j + 1) * half)
            g = _dot_nt(hb, wg_ref[cols, :])
            u = _dot_nt(hb, wu_ref[cols, :])
            g_s[:, cols] = g
            u_s[:, cols] = u
            pre = pre + _dot(g * _sigmoid(g) * u, wd_ref[cols, :])
        xhat, rstd = _ln_stats(pre)
        real = i * tm + _iota((tm, 1), 0) < SEQ
        target_rows = jnp.concatenate([t[...] for t in t_refs], axis=0)
        diff = jnp.where(real, xhat * g2_ref[...] + b2_ref[...] - target_rows, 0.0)
        acc[...] += jnp.sum(diff * diff, axis=0, keepdims=True)
        dy = diff * (1.0 / D)
        dpre = _ln_bwd(dy, xhat, rstd, g2_ref[...])
        dp_ref[...] = dpre
        dg_ref[...] += jnp.sum(dy * xhat, axis=0, keepdims=True)
        db_ref[...] += jnp.sum(dy, axis=0, keepdims=True)
        dpb = dpre.astype(BF16)
        for j in range(2):
            cols = slice(j * half, (j + 1) * half)
            g, u = g_s[:, cols], u_s[:, cols]
            sg = _sigmoid(g)
            silu = g * sg
            da = _dot_nt(dpb, wd_ref[cols, :])
            a_ref[:, cols] = (silu * u).astype(BF16)
            dgate_ref[:, cols] = (da * u * (sg * (1.0 + g * (1.0 - sg)))).astype(BF16)
            dup_ref[:, cols] = (da * silu).astype(BF16)

        @pl.when(i == steps - 1)
        def _():
            loss_ref[...] = jnp.zeros_like(loss_ref) + (0.5 / D) * jnp.sum(acc[...], axis=1, keepdims=True)

    t_spec = lambda k: pl.BlockSpec((BLK, D), lambda i: (jnp.minimum(i * n_t + k, last_blk), 0))
    return pl.pallas_call(
        body, name="ffn_fwd_loss_bwd", grid=(steps,),
        in_specs=[_rows(tm, D), _const((D_FF, D)), _const((D_FF, D)), _const((D_FF, D))]
        + [t_spec(k) for k in range(n_t)] + [_const((1, D)), _const((1, D))],
        out_specs=[_rows(tm, D_FF), _rows(tm, D_FF), _rows(tm, D_FF), _rows(tm, D), _acc((1, LANE)), _acc((1, D)),
                   _acc((1, D))],
        out_shape=[pltpu.HBM((lp, D_FF), BF16)] * 3 + [pltpu.HBM((lp, D), F32), pltpu.HBM((1, LANE), F32),
                                                         pltpu.HBM((1, D), F32), pltpu.HBM((1, D), F32)],
        scratch_shapes=[pltpu.VMEM((tm, D_FF), F32), pltpu.VMEM((tm, D_FF), F32), pltpu.VMEM((1, D), F32)],
        compiler_params=_params(58, dimension_semantics=_seq()),
    )(*_hbm(h1, wg_t, wu_t, wd, *[target] * n_t, g2, b2))


def _ffn_out_bwd(dpre2, dgate, dup, pre1, wg_t, wu_t, g1, w_out, o_gla, r_g, gn4):
    tm = _row_tile(384)
    lp = _lp()

    def body(dp_ref, dg_ref, du_ref, p1_ref, wg_ref, wu_ref, g1_ref, w_ref, og_ref, r_ref, gn_ref,
             dp1_ref, dg1_ref, db1_ref, dos_ref, dog_ref, dr_ref, dgn_ref):
        @pl.when(pl.program_id(0) == 0)
        def _():
            for acc_ref in (dg1_ref, db1_ref, dgn_ref):
                acc_ref[...] = jnp.zeros_like(acc_ref)

        dh1 = ALPHA * dp_ref[...] + _dot(dg_ref[...], wg_ref[...]) + _dot(du_ref[...], wu_ref[...])
        xhat, rstd1 = _ln_stats(p1_ref[...])
        dpre1 = _ln_bwd(dh1, xhat, rstd1, g1_ref[...])
        dp1_ref[...] = dpre1
        dg1_ref[...] += jnp.sum(dh1 * xhat, axis=0, keepdims=True)
        db1_ref[...] += jnp.sum(dh1, axis=0, keepdims=True)

        do = _dot_nt(dpre1, w_ref[...])
        for pos, h in enumerate(HEAD_POS):
            dos_ref[:, pos * DH:(pos + 1) * DH] = do[:, h * DH:(h + 1) * DH]
        gn = gn_ref[...]
        for h in range(GLA_HEADS):
            hs = slice(h * DV, (h + 1) * DV)
            xg = og_ref[:, hs]
            rstd = lax.rsqrt(jnp.mean(xg * xg, axis=-1, keepdims=True) + RMS_EPS)
            nx = xg * rstd
            r = r_ref[:, hs]
            sr = _sigmoid(r)
            d_o = do[:, 512 + h * DV:512 + (h + 1) * DV]
            dr_ref[:, hs] = d_o * (nx * gn) * (sr * (1.0 + r * (1.0 - sr)))
            dn = d_o * (r * sr)
            dgn_ref[...] += jnp.sum(dn * nx, axis=0, keepdims=True)
            dnx = dn * gn
            dog_ref[:, hs] = rstd * (dnx - nx * jnp.mean(dnx * nx, axis=-1, keepdims=True))

    return pl.pallas_call(
        body, name="ffn_out_bwd", grid=(lp // tm,),
        in_specs=[_rows(tm, D), _rows(tm, D_FF), _rows(tm, D_FF), _rows(tm, D), _const((D_FF, D)), _const((D_FF, D)),
                  _const((1, D)), _const((D, D)), _rows(tm, 512), _rows(tm, 512), _const((1, DV))],
        out_specs=[_rows(tm, D), _acc((1, D)), _acc((1, D)), _rows(tm, 512), _rows(tm, 512), _rows(tm, 512),
                   _acc((1, DV))],
        out_shape=[pltpu.HBM((lp, D), F32), pltpu.HBM((1, D), F32), pltpu.HBM((1, D), F32)]
        + [pltpu.HBM((lp, 512), F32)] * 3 + [pltpu.HBM((1, DV), F32)],
        compiler_params=_params(48, dimension_semantics=_seq()),
    )(*_hbm(dpre2, dgate, dup, pre1, wg_t, wu_t, g1, w_out, o_gla, r_g, gn4))


def _atb(a, b, name):
    lp = _lp()
    tm = _row_tile(1408)
    n, w = a.shape[1], b.shape[1]
    bw = 512 if n * w * 4 > (4 << 20) else w

    def body(a_ref, b_ref, o_ref):
        @pl.when(pl.program_id(1) == 0)
        def _():
            o_ref[...] = jnp.zeros_like(o_ref)

        o_ref[...] += _dot_tn(a_ref[...], b_ref[...])

    return pl.pallas_call(
        body, name=name, grid=(w // bw, lp // tm),
        in_specs=[pl.BlockSpec((tm, n), lambda j, k: (k, 0)), pl.BlockSpec((tm, bw), lambda j, k: (k, j))],
        out_specs=pl.BlockSpec((n, bw), lambda j, k: (0, j)),
        out_shape=pltpu.HBM((n, w), F32),
        compiler_params=_params(48, dimension_semantics=_seq(2)),
    )(*_hbm(a, b))


def _gla_bwd(qg, kg, vg, z, do_gla, st_all, token):
    steps = SEQ // BLK + 1
    kw, vw = GLA_HEADS * DK, GLA_HEADS * DV
    pairs = [(c, h) for c in range(GLA_PER_STEP) for h in range(GLA_HEADS)]
    heads = range(GLA_HEADS)

    def body(q_ref, k_ref, v_ref, z_ref, do_ref, st_ref, token_ref, dq_ref, dk_ref, dv_ref, dz_ref, dst):
        @pl.when(pl.program_id(0) == 0)
        def _():
            dst[...] = jnp.zeros_like(dst)

        rmask = _gla_rowmask(steps - 1 - pl.program_id(0))
        zz = z_ref[...]
        b, b_last = _gla_decay(zz, rmask)
        e_b, e_nb, e_kd, e_last = jnp.exp(b), jnp.exp(-b), jnp.exp(b_last - b), jnp.exp(b_last)
        q = q_ref[...] * (rmask * DK ** -0.5)
        k = k_ref[...] * rmask
        v = v_ref[...] * rmask
        qe, ke, kd = q * e_b, k * e_nb, k * e_kd
        d_o = do_ref[...]
        causal = _iota((CH, CH), 0) >= _iota((CH, CH), 1)
        a, da, dqe, dke, dv_intra, carry = {}, {}, {}, {}, {}, {}
        for c, h in pairs:
            rows, ks, vs_ = _gla_slices(c, h)
            a[c, h] = jnp.where(causal, _dot_nt(qe[rows, ks], ke[rows, ks]), 0.0)
            da[c, h] = jnp.where(causal, _dot_nt(d_o[rows, vs_], v[rows, vs_]), 0.0)
            carry[c, h] = _dot_tn(d_o[rows, vs_], qe[rows, ks])
        for c, h in pairs:
            rows, ks, vs_ = _gla_slices(c, h)
            dqe[c, h] = _dot(d_o[rows, vs_], st_ref[0, c][:, ks]) + _dot(da[c, h], ke[rows, ks])
            dke[c, h] = _dot_tn(da[c, h], qe[rows, ks])
            dv_intra[c, h] = _dot_tn(a[c, h], d_o[rows, vs_])
        dstate = dst[...]
        dkd, db_decay = {}, {}
        for c in reversed(range(GLA_PER_STEP)):
            for h in heads:
                rows, ks, vs_ = _gla_slices(c, h)
                dkd[c, h] = _dot(v[rows, vs_], dstate[:, ks])
                dv_ref[rows, vs_] = dv_intra[c, h] + _dot_nt(kd[rows, ks], dstate[:, ks])
            chunk_last = e_last[c * CH:c * CH + 1]
            db_decay[c] = jnp.sum(dstate * st_ref[0, c], axis=0, keepdims=True) * chunk_last
            dstate = dstate * chunk_last + jnp.concatenate([carry[c, h] for h in heads], axis=1)
        dst[...] = dstate
        rows_of = lambda parts: jnp.concatenate(
            [jnp.concatenate([parts[c, h] for h in heads], axis=1) for c in range(GLA_PER_STEP)], axis=0)
        dqe_all, dke_all, dkd_all = rows_of(dqe), rows_of(dke), rows_of(dkd)
        dq_ref[...] = dqe_all * e_b * (rmask * DK ** -0.5)
        dk_ref[...] = (dke_all * e_nb + dkd_all * e_kd) * rmask
        dkd_kd = dkd_all * kd
        db = dqe_all * qe - dke_all * ke - dkd_kd
        _, upper, same = _gla_chunk_masks()
        decay_rows = jnp.concatenate([jnp.broadcast_to(db_decay[c], (CH, kw)) for c in range(GLA_PER_STEP)], axis=0)
        dlog_g = _dot_exact(upper.astype(F32), db) + _dot_exact(same.astype(F32), dkd_kd) + decay_rows
        dz_ref[...] = dlog_g * (rmask / GLA_TAU) * _sigmoid(-zz)

    blk = lambda w: pl.BlockSpec((BLK, w), lambda s: (_gla_block(steps - 1 - s), 0))
    return pl.pallas_call(
        body, name="gla_bwd", grid=(steps,),
        in_specs=[blk(kw), blk(kw), blk(vw), blk(kw), blk(vw),
                  pl.BlockSpec((1, GLA_PER_STEP, DV, kw), lambda s: (steps - 1 - s, 0, 0, 0)), _const(TOKEN)],
        out_specs=[blk(kw), blk(kw), blk(vw), blk(kw)],
        out_shape=[pltpu.HBM((_lp(), kw), F32), pltpu.HBM((_lp(), kw), F32),
                   pltpu.HBM((_lp(), vw), F32), pltpu.HBM((_lp(), kw), F32)],
        scratch_shapes=[pltpu.VMEM((DV, kw), F32)],
        compiler_params=_params(16, dimension_semantics=_seq()),
    )(*_hbm(qg, kg, vg, z, do_gla, st_all), token)


def _swa_bwd(sinks, qs, ks, vs, do_s, token):
    nb = SEQ // BLK
    kvw = SWA_KV_HEADS * DH
    scale = DH ** -0.5
    heads = range(SWA_HEADS)

    def body(sink_ref, q_ref, km_ref, kp_ref, kc_ref, vm_ref, vp_ref, vc_ref, do_ref, token_ref,
             dq_ref, dk_ref, dv_ref, dsink_ref, carry_k, carry_v, meta_k, meta_v):
        n = pl.program_id(0)

        @pl.when(n == 0)
        def _():
            for r in (carry_k, carry_v, meta_k, meta_v):
                r[...] = jnp.zeros_like(r)
            dsink_ref[...] = jnp.zeros_like(dsink_ref)

        @pl.when(n <= nb)
        def _():
            negdist, maskbias = _swa_bias(n)
            lane = _iota((1, LANE), 1)
            k_all = jnp.concatenate([km_ref[...], kp_ref[...], kc_ref[...]], axis=0).astype(BF16)
            v_all = jnp.concatenate([vm_ref[...], vp_ref[...], vc_ref[...]], axis=0).astype(BF16)
            q = [_swa_half(q_ref, pos, scale) for pos in heads]
            d_o = [_swa_half(do_ref, pos) for pos in heads]
            t = [_dot_nt(q[pos], k_all) + (2.0 ** -(HEAD_POS[pos] + 1) * negdist + maskbias) for pos in heads]
            dp = [_dot_nt(d_o[pos], v_all) for pos in heads]
            soft = [_swa_softmax(t[pos], sink_ref[HEAD_POS[pos]]) for pos in heads]
            p = [s[0] for s in soft]
            delta = [jnp.sum(p[pos] * dp[pos], axis=-1, keepdims=True) for pos in heads]
            ds = [(p[pos] * (dp[pos] - delta[pos])).astype(BF16) for pos in heads]
            dq = [_dot(ds[pos], k_all) for pos in heads]
            for col in range(SWA_HEADS // 2):
                dq_ref[:, col * LANE:(col + 1) * LANE] = scale * _swa_merge(dq[2 * col], dq[2 * col + 1])
            dsink = jnp.zeros((1, LANE), F32)
            for pos in heads:
                dsink = dsink + jnp.where(lane == HEAD_POS[pos],
                                          -jnp.sum(soft[pos][1] * delta[pos], axis=0, keepdims=True), 0.0)
            dsink_ref[...] += dsink
            dk3 = _dot_tn(jnp.concatenate(q, axis=0), jnp.concatenate(ds, axis=0)).T
            dv3 = _dot_tn(jnp.concatenate(d_o, axis=0), jnp.concatenate([x.astype(BF16) for x in p], axis=0)).T
            meta_k[...] += dk3[0:BLK]
            meta_v[...] += dv3[0:BLK]
            dk_ref[...] = carry_k[...] + dk3[BLK:2 * BLK]
            dv_ref[...] = carry_v[...] + dv3[BLK:2 * BLK]
            carry_k[...] = dk3[2 * BLK:3 * BLK]
            carry_v[...] = dv3[2 * BLK:3 * BLK]

        @pl.when(n == nb + 1)
        def _():
            dk_ref[...] = meta_k[...]
            dv_ref[...] = meta_v[...]

    kv_out = pl.BlockSpec((BLK, kvw), lambda n: (jnp.where(n == nb + 1, nb, jnp.clip(n - 1, 0, nb - 1)), 0))
    qblk = pl.BlockSpec((BLK, SWA_HEADS * DH), lambda n: (jnp.minimum(n, nb), 0))
    return pl.pallas_call(
        body, name="swa_bwd", grid=(nb + 2,),
        in_specs=[pl.BlockSpec(memory_space=pltpu.SMEM), qblk] + _swa_kv_specs(kvw) + _swa_kv_specs(kvw)
        + [qblk, _const(TOKEN)],
        out_specs=[qblk, kv_out, kv_out, _acc((1, LANE))],
        out_shape=[pltpu.HBM((_lp(), SWA_HEADS * DH), F32), pltpu.HBM((_lp(), kvw), F32),
                   pltpu.HBM((_lp(), kvw), F32), pltpu.HBM((1, LANE), F32)],
        scratch_shapes=[pltpu.VMEM((BLK, kvw), F32)] * 4,
        compiler_params=_params(16, dimension_semantics=_seq()),
    )(sinks, *_hbm(qs, ks, ks, ks, vs, vs, vs, do_s), token)


def _in_bwd(dqs, dks, dvs, dqg, dkg, dvg, drg, dz, dpre1, w_in_t, wg2_p):
    tm = _row_tile(384)
    lp = _lp()
    widths = (512, 128, 128, 256, 256, 512, 512)
    offs = (O_QS, O_KS, O_VS, O_QG, O_KG, O_VG, O_RG)

    def body(*refs):
        parts, (dz_ref, dp1_ref, w_ref, wg2_ref, dproj_ref, dh0_ref, dbin_ref, dbg_ref) = refs[:7], refs[7:]

        @pl.when(pl.program_id(0) == 0)
        def _():
            dbin_ref[...] = jnp.zeros_like(dbin_ref)
            dbg_ref[...] = jnp.zeros_like(dbg_ref)

        for pos, h in enumerate(HEAD_POS):
            val = parts[0][:, pos * DH:(pos + 1) * DH]
            dproj_ref[:, O_QS + h * DH:O_QS + (h + 1) * DH] = val.astype(BF16)
            dbin_ref[:, O_QS + h * DH:O_QS + (h + 1) * DH] += jnp.sum(val, axis=0, keepdims=True)
        for p_ref, off, wd in zip(parts[1:], offs[1:], widths[1:]):
            val = p_ref[...]
            dproj_ref[:, off:off + wd] = val.astype(BF16)
            dbin_ref[:, off:off + wd] += jnp.sum(val, axis=0, keepdims=True)
        dz = dz_ref[...]
        dlr = _dot_nt(dz, wg2_ref[...])
        dproj_ref[:, O_LR:O_LR + LANE] = dlr.astype(BF16)
        dbin_ref[:, O_LR:O_LR + LANE] += jnp.sum(dlr, axis=0, keepdims=True)
        dbg_ref[...] += jnp.sum(dz, axis=0, keepdims=True)
        dh0_ref[...] = ALPHA * dp1_ref[...] + _dot(dproj_ref[...], w_ref[...])

    return pl.pallas_call(
        body, name="in_bwd", grid=(lp // tm,),
        in_specs=[_rows(tm, w) for w in widths] + [_rows(tm, 256), _rows(tm, D), _const((D_IN_P, D)), _const((LANE, 256))],
        out_specs=[_rows(tm, D_IN_P), _rows(tm, D), _acc((1, D_IN_P)), _acc((1, 256))],
        out_shape=[pltpu.HBM((lp, D_IN_P), BF16), pltpu.HBM((lp, D), F32),
                   pltpu.HBM((1, D_IN_P), F32), pltpu.HBM((1, 256), F32)],
        compiler_params=_params(40, dimension_semantics=_seq()),
    )(*_hbm(dqs, dks, dvs, dqg, dkg, dvg, drg, dz, dpre1, w_in_t, wg2_p))


def _ln_in_bwd(x, meta_ext, dh0, g, token):
    tr = min(LN_ROWS, SEQ)

    def ln_bwd(x_ref, dh_ref, g_ref, dx_ref, dg_ref, db_ref):
        @pl.when(pl.program_id(0) == 0)
        def _():
            dg_ref[...] = jnp.zeros_like(dg_ref)
            db_ref[...] = jnp.zeros_like(db_ref)

        xhat, rstd = _ln_stats(x_ref[...])
        dh = dh_ref[...]
        dx_ref[...] = _ln_bwd(dh, xhat, rstd, g_ref[...])
        dg_ref[...] += jnp.sum(dh * xhat, axis=0, keepdims=True)
        db_ref[...] += jnp.sum(dh, axis=0, keepdims=True)

    def body(x_ref, dh_ref, g_ref, token_ref, dx_ref, dg_ref, db_ref):
        ln_bwd(x_ref, dh_ref, g_ref, dx_ref, dg_ref, db_ref)

    def meta_body(m_ref, dh_ref, g_ref, dm_ref, dg_ref, db_ref):
        ln_bwd(m_ref, dh_ref, g_ref, dm_ref, dg_ref, db_ref)

    sums = [pltpu.HBM((1, D), F32), pltpu.HBM((1, D), F32)]
    dx, dg, db = pl.pallas_call(
        body, name="ln_in_bwd", grid=(SEQ // tr,),
        in_specs=[_rows(tr, D), _rows(tr, D), _const((1, D)), _const(TOKEN)],
        out_specs=[_rows(tr, D), _acc((1, D)), _acc((1, D))],
        out_shape=[pltpu.HBM((SEQ, D), F32)] + sums,
        compiler_params=_params(32, dimension_semantics=_seq()),
    )(*_hbm(x, dh0, g), token)
    dm, dg_m, db_m = pl.pallas_call(
        meta_body, name="ln_in_bwd_meta", grid=(1,),
        in_specs=[_const((BLK, D)), pl.BlockSpec((BLK, D), lambda i: (SEQ // BLK, 0)), _const((1, D))],
        out_specs=[_acc((BLK, D)), _acc((1, D)), _acc((1, D))],
        out_shape=[pltpu.HBM((BLK, D), F32)] + sums,
        compiler_params=_params(16, dimension_semantics=_seq()),
    )(*_hbm(meta_ext, dh0, g))
    return dx, dm, dg + dg_m, db + db_m


def _local_step(x, target, ln_in_g, ln_in_b, b_in, bg2, sinks, gn, g1, b1, g2, b2,
                token, fetch_first, fetch_rest, ship_ffn, ship_w_in):
    row = lambda v: v.reshape(1, -1).astype(F32)
    b_in_p = jnp.pad(row(b_in), ((0, 0), (0, D_IN_P - D_IN)))
    gn4 = row(gn)
    sinks = sinks.reshape(-1).astype(F32)

    h_real = _ln_in_fwd_real(x, row(ln_in_g), row(ln_in_b), token)
    w_in_t, meta_full, wg2 = fetch_first([h_real])
    meta_ext = jnp.pad(meta_full, ((META_OFF, BLK - CH), (0, 0)))
    wg2_p = jnp.pad(wg2, ((0, LANE - wg2.shape[0]), (0, 0))).astype(BF16)
    h0 = _ln_in_fwd_meta(h_real, meta_ext, row(ln_in_g), row(ln_in_b))
    qs, ks, vs, qg, kg, vg, rg, glr, z = _in_proj(h0, w_in_t, b_in_p, wg2_p, row(bg2))
    o_s = _swa_fwd(sinks, qs, ks, vs)
    o_gla, st_all = _gla_fwd(qg, kg, vg, z)
    w_out, wg_t, wu_t, wd = fetch_rest([o_s, o_gla])
    o, pre1, h1 = _post_mix(o_s, o_gla, rg, h0, gn4, w_out, row(g1), row(b1))
    a, dgate, dup, dpre2, loss, dg2, db2 = _ffn_fwd_loss_bwd(h1, wg_t, wu_t, wd, target, row(g2), row(b2))
    dpre1, dg1, db1, do_s, do_gla, drg, dgn = _ffn_out_bwd(dpre2, dgate, dup, pre1, wg_t, wu_t, row(g1), w_out, o_gla,
                                                           rg, gn4)
    dwd = _atb(a, dpre2, "dw_down")
    dwg_t = _atb(dgate, h1, "dw_gate")
    dwu_t = _atb(dup, h1, "dw_up")
    dw_out = _atb(o, dpre1, "dw_out")
    token = ship_ffn(dict(w_out=dw_out, w_g=dwg_t, w_u=dwu_t, w_d=dwd))
    dqg, dkg, dvg, dz = _gla_bwd(qg, kg, vg, z, do_gla, st_all, token)
    dqs, dks, dvs, dsinks = _swa_bwd(sinks, qs, ks, vs, do_s, token)
    dproj, dh0, db_in_p, dbg2 = _in_bwd(dqs, dks, dvs, dqg, dkg, dvg, drg, dz, dpre1, w_in_t, wg2_p)
    token = ship_w_in(_atb(dproj, h0, "dw_in"))
    dwg2_p = _atb(glr, dz, "dw_gate_lr2")
    dx, dmeta_blk, dg_in, db_in_ln = _ln_in_bwd(x, meta_ext, dh0, row(ln_in_g), token)

    small = dict(meta_blk=dmeta_blk, ln_in_g=dg_in, ln_in_b=db_in_ln, ln1_g=dg1, ln1_b=db1, ln2_g=dg2, ln2_b=db2,
                 b_in_p=db_in_p, wg2_p=dwg2_p, bg2=dbg2, sinks=dsinks, gn=dgn, loss=loss)
    return dx, small


HBM = pl.BlockSpec(memory_space=pltpu.HBM)


def _place():
    return lax.axis_index("x"), lax.axis_index("y"), lax.axis_index("c")


def _other_chips(x, y):
    return [(1 - x, y), (x, 1 - y), (1 - x, 1 - y)]


def _dma_sems(n):
    return pltpu.SemaphoreType.DMA((n,))


def _comm_params():
    return pltpu.CompilerParams(has_side_effects=True)


SEM = pl.BlockSpec(memory_space=pltpu.SEMAPHORE)


def _ici_copies(kind, landing, srcs, lands, send_sems, recv_sems):
    x, y, c = _place()
    mine = 2 * x + y
    copies = []
    for a in range(len(srcs)):
        for j, (px, py) in enumerate(_other_chips(x, y)):
            slab = 2 * px + py if landing else mine
            if kind == "gather":
                src, dst = srcs[a].at[c], lands[a].at[slab, c]
            else:
                src, dst = srcs[a].at[2 * px + py], lands[a].at[slab]
            copies.append(pltpu.make_async_remote_copy(src, dst, send_sems.at[3 * a + j], recv_sems.at[3 * a + j],
                                                       device_id=(px, py, c), device_id_type=MESH))
    return copies


def _split_params():
    return pltpu.CompilerParams(has_side_effects=pltpu.SideEffectType.DATAFLOW_SIDE_EFFECTING)


def _ici_start(kind, srcs, land_shapes, after, name):
    n = len(srcs)
    lands = [pltpu.with_memory_space_constraint(lax.empty(s, a.dtype), pltpu.HBM) for s, a in zip(land_shapes, srcs)]

    def body(*refs):
        outs = refs[2 * n + len(after):]
        for cp in _ici_copies(kind, False, refs[:n], refs[n:2 * n], outs[0], outs[1]):
            cp.start()
        outs[-1][...] = jnp.zeros(TOKEN, F32)

    outs = pl.pallas_call(
        body, name=name, in_specs=[HBM] * (2 * n) + [pl.BlockSpec(memory_space=pl.ANY)] * len(after),
        out_specs=[SEM, SEM] + [HBM] * (2 * n) + [pl.BlockSpec(memory_space=pltpu.VMEM)],
        out_shape=[_dma_sems(3 * n)] * 2 + [pltpu.HBM(a.shape, a.dtype) for a in list(srcs) + lands]
        + [jax.ShapeDtypeStruct(TOKEN, F32)],
        input_output_aliases={i: 2 + i for i in range(2 * n)},
        compiler_params=_split_params(),
    )(*_hbm(*srcs), *lands, *after)
    return outs[:-1], outs[-1]


def _ici_wait(kind, handle, after, name):
    n = (len(handle) - 2) // 2

    def body(*refs):
        for cp in _ici_copies(kind, True, refs[:n], refs[n:2 * n], refs[2 * n], refs[2 * n + 1]):
            cp.wait_send()
            cp.wait_recv()

    outs = pl.pallas_call(
        body, name=name, in_specs=[HBM] * (2 * n) + [SEM, SEM] + [pl.BlockSpec(memory_space=pl.ANY)] * len(after),
        out_specs=[HBM] * (2 * n), out_shape=[pltpu.HBM(a.shape, a.dtype) for a in handle[2:]],
        input_output_aliases={i: i for i in range(2 * n)},
        compiler_params=_split_params(),
    )(*handle[2:], handle[0], handle[1], *after)
    return list(outs[n:])


def _sibling_forward(lands, name):
    n = len(lands)

    def body(*refs):
        outs = refs[n:2 * n]
        send_sems, recv_sems = refs[2 * n:]
        x, y, c = _place()

        def copy(a, j, half):
            px, py = _other_chips(x, y)[j]
            blk = outs[a].at[2 * px + py, half]
            return pltpu.make_async_remote_copy(blk, blk, send_sems.at[3 * a + j], recv_sems.at[3 * a + j],
                                                device_id=(x, y, 1 - c), device_id_type=MESH)

        pairs = [(a, j) for a in range(n) for j in range(3)]
        for a, j in pairs:
            copy(a, j, c).start()
        for a, j in pairs:
            copy(a, j, 1 - c).wait_recv()
        for a, j in pairs:
            copy(a, j, c).wait_send()

    return pl.pallas_call(
        body, name=name, in_specs=[HBM] * n, out_specs=[HBM] * n,
        out_shape=[pltpu.HBM(a.shape, a.dtype) for a in lands],
        input_output_aliases={a: a for a in range(n)},
        scratch_shapes=[_dma_sems(3 * n)] * 2,
        compiler_params=_comm_params(),
    )(*_hbm(*lands))


def _sibling_exchange(grads, name):
    n = len(grads)

    def body(*refs):
        ins, outs = refs[:n], refs[n:2 * n]
        send_sems, recv_sems = refs[2 * n:]
        x, y, c = _place()
        copies = []
        for a in range(n):
            for s in range(N_CHIPS):
                cp = pltpu.make_async_remote_copy(ins[a].at[s, 1 - c], outs[a].at[s], send_sems.at[N_CHIPS * a + s],
                                                  recv_sems.at[N_CHIPS * a + s], device_id=(x, y, 1 - c),
                                                  device_id_type=MESH)
                cp.start()
                copies.append(cp)
        for cp in copies:
            cp.wait_recv()
        for cp in copies:
            cp.wait_send()

    return pl.pallas_call(
        body, name=name, in_specs=[HBM] * n, out_specs=[HBM] * n,
        out_shape=[pltpu.HBM((N_CHIPS, g.shape[2], D), F32) for g in grads],
        scratch_shapes=[_dma_sems(N_CHIPS * n)] * 2,
        compiler_params=_comm_params(),
    )(*_hbm(*grads))


def _add_halves(core, grads, recvs, dtypes, name):
    n = len(grads)
    heights = [g.shape[2] for g in grads]

    def body(c_ref, *refs):
        for a in range(n):
            refs[2 * n + a][...] = (refs[2 * a][0] + refs[2 * a + 1][...]).astype(dtypes[a])

    slab = lambda h: pl.BlockSpec((1, h, D), lambda s, c: (s, 0, 0))
    mine = lambda h: pl.BlockSpec((1, 1, h, D), lambda s, c: (s, c[0], 0, 0))
    return pl.pallas_call(
        body, name=name,
        grid_spec=pltpu.PrefetchScalarGridSpec(
            num_scalar_prefetch=1, grid=(N_CHIPS,),
            in_specs=[spec(h) for h in heights for spec in (mine, slab)], out_specs=[slab(h) for h in heights]),
        out_shape=[pltpu.HBM((N_CHIPS, h, D), dt) for h, dt in zip(heights, dtypes)],
        compiler_params=_params(32, dimension_semantics=_seq()),
    )(core, *_hbm(*[a for pair in zip(grads, recvs) for a in pair]))


N_DEVICES = 2 * N_CHIPS
PEER_FLIPS = [(dx, dy, dc) for dx in (0, 1) for dy in (0, 1) for dc in (0, 1)][1:]


def _small_exchange(pack, after):
    n = len(PEER_FLIPS)

    def body(p_ref, *refs):
        out_ref, send_sems, recv_sems = refs[len(after):]
        x, y, c = _place()
        flip = lambda v, d: 1 - v if d else v

        def copy(k, landing):
            px, py, pc = (flip(v, d) for v, d in zip((x, y, c), PEER_FLIPS[k]))
            slab = 4 * px + 2 * py + pc if landing else 4 * x + 2 * y + c
            return pltpu.make_async_remote_copy(p_ref, out_ref.at[slab], send_sems.at[k], recv_sems.at[k],
                                                device_id=(px, py, pc), device_id_type=MESH)

        for k in range(n):
            copy(k, False).start()
        for k in range(n):
            copy(k, True).wait_recv()
        for k in range(n):
            copy(k, False).wait_send()

    return pl.pallas_call(
        body, name="small_exchange", in_specs=[HBM] + [pl.BlockSpec(memory_space=pl.ANY)] * len(after), out_specs=HBM,
        out_shape=pltpu.HBM((N_DEVICES,) + pack.shape, F32),
        scratch_shapes=[_dma_sems(n)] * 2,
        compiler_params=_comm_params(),
    )(*_hbm(pack), *after)


def _sum_chips(slots, firsts, rests):
    n = len(firsts)

    def body(i_ref, *refs):
        for a in range(n):
            first, r1, r2, r3 = refs[4 * a:4 * a + 4]
            refs[4 * n + a][...] = ((first[...].astype(F32) + r1[...].astype(F32)) + r2[...].astype(F32)) + r3[...].astype(F32)

    slab = lambda h, k: pl.BlockSpec((1, h, D), lambda i, ix: (ix[k], 0, 0))
    heights = [f.shape[1] for f in firsts]
    return pl.pallas_call(
        body, name="sum_chips",
        grid_spec=pltpu.PrefetchScalarGridSpec(
            num_scalar_prefetch=1, grid=(1,),
            in_specs=[slab(h, k) for h in heights for k in range(4)], out_specs=[slab(h, 4) for h in heights]),
        out_shape=[pltpu.HBM((2, h, D), F32) for h in heights],
        compiler_params=_params(48, dimension_semantics=_seq()),
    )(slots, *_hbm(*[a for f, r in zip(firsts, rests) for a in (f, r, r, r)]))


def _join_halves(halves):
    n = len(halves)

    def body(*refs):
        outs = refs[n:2 * n]
        send_sems, recv_sems = refs[2 * n:]
        x, y, c = _place()

        def copy(a, slab):
            return pltpu.make_async_remote_copy(outs[a].at[slab], outs[a].at[slab], send_sems.at[a], recv_sems.at[a],
                                                device_id=(x, y, 1 - c), device_id_type=MESH)

        for a in range(n):
            copy(a, c).start()
        for a in range(n):
            copy(a, 1 - c).wait_recv()
        for a in range(n):
            copy(a, c).wait_send()

    return pl.pallas_call(
        body, name="join_halves", in_specs=[HBM] * n, out_specs=[HBM] * n,
        out_shape=[pltpu.HBM(h.shape, F32) for h in halves],
        input_output_aliases={a: a for a in range(n)},
        scratch_shapes=[_dma_sems(n)] * 2,
        compiler_params=_comm_params(),
    )(*_hbm(*halves))


def _chip_partials(grads, wire_dtypes, names):
    core = lax.axis_index("c").astype(jnp.int32).reshape(1)
    recv = _sibling_exchange(grads, "sibling_exchange_" + names[0])
    return list(_add_halves(core, grads, recv, wire_dtypes, "add_halves_" + names[0]))


def _finish_reduce(parts, got):
    x, y, c = _place()
    others = [2 * px + py for px, py in _other_chips(x, y)]
    own_first = jnp.stack([2 * x + y] + others + [c]).astype(jnp.int32)
    return [f.reshape(2 * f.shape[1], D) for f in _join_halves(_sum_chips(own_first, parts, got))]


def _adamw(w, g, m, v, name):
    rows, cols = w.shape
    if rows % 8 == 0:
        tr = max(t for t in range(8, 257, 8) if rows % t == 0)
        grid, blk = (rows // tr,), pl.BlockSpec((tr, cols), lambda i: (i, 0))
    else:
        grid, blk = (cols // 256,), pl.BlockSpec((rows, 256), lambda i: (0, i))

    def body(w_ref, g_ref, m_ref, v_ref, d_ref, nm_ref, nv_ref):
        d_ref[...], nm_ref[...], nv_ref[...] = _adamw_math(w_ref[...], g_ref[...], m_ref[...], v_ref[...])

    return pl.pallas_call(
        body, name=name, grid=grid,
        in_specs=[blk] * 4, out_specs=[blk] * 3,
        out_shape=[pltpu.HBM(w.shape, F32)] * 3,
        compiler_params=_params(32, dimension_semantics=_seq()),
    )(*_hbm(w, g, m, v))


def _adamw_math(w, g, m, v):
    nm = ADAM_B1 * m + (1.0 - ADAM_B1) * g
    nv = ADAM_B2 * v + (1.0 - ADAM_B2) * (g * g)
    m_hat = nm / (1.0 - ADAM_B1 ** ADAM_STEP)
    v_hat = nv / (1.0 - ADAM_B2 ** ADAM_STEP)
    return -ADAM_LR * (m_hat / (jnp.sqrt(v_hat) + ADAM_EPS) + ADAM_WD * w), nm, nv


SMALL = (("meta_tokens", (N_META, D // N_CHIPS)), ("ln_in_g", (1, D)), ("ln_in_b", (1, D)), ("b_in", (1, D_IN)),
         ("w_gate_lr2", (GATE_RANK, GLA_HEADS * DK // N_CHIPS)), ("b_gate_lr2", (1, GLA_HEADS * DK)),
         ("attn_sinks", (1, SWA_HEADS)),
         ("gla_norm_g", (1, DV)), ("ln1_g", (1, D)), ("ln1_b", (1, D)), ("ln2_g", (1, D)), ("ln2_b", (1, D)))
ROW_META, ROW_B_IN, ROW_TAIL, ROW_WG2 = 0, 22, 25, 32
ROW_LN = dict(ln_in_g=16, ln_in_b=17, ln1_g=18, ln1_b=19, ln2_g=20, ln2_b=21)
TAIL_BG2, TAIL_SINKS, TAIL_GN, TAIL_LOSS = 0, 256, 256 + SWA_HEADS, 256 + SWA_HEADS + DV


def _adamw_small(place, packs, own, params):
    n = len(SMALL)

    def body(place_ref, packs_ref, own_ref, *refs):
        ins, outs, p_ref = refs[:3 * n], refs[3 * n:-1], refs[-1]
        me, c = place_ref[0], place_ref[1]
        total = jnp.where(me == 0, own_ref[...], packs_ref[0])
        for i in range(1, N_DEVICES):
            total = total + jnp.where(me == i, own_ref[...], packs_ref[i])
        p_ref[...] = total
        outs[4 * n][...] = total[ROW_TAIL:ROW_TAIL + 1, :]

        def mine(width, rows):
            part = lambda s: p_ref[rows, s * width:(s + 1) * width]
            return jnp.where(c == 0, part(0), jnp.where(c == 1, part(1), jnp.where(c == 2, part(2), part(3))))

        tail = lambda lo, width: p_ref[ROW_TAIL:ROW_TAIL + 1, lo:lo + width]
        grads = dict(
            meta_tokens=mine(D // N_CHIPS, slice(ROW_META, ROW_META + N_META)),
            b_in=jnp.concatenate([p_ref[ROW_B_IN:ROW_B_IN + 1, :], p_ref[ROW_B_IN + 1:ROW_B_IN + 2, :],
                                  p_ref[ROW_B_IN + 2:ROW_B_IN + 3, 0:D_IN - 2 * D]], axis=1),
            w_gate_lr2=mine(256 // N_CHIPS, slice(ROW_WG2, ROW_WG2 + 16)),
            b_gate_lr2=tail(TAIL_BG2, 256), attn_sinks=tail(TAIL_SINKS, SWA_HEADS), gla_norm_g=tail(TAIL_GN, DV),
            **{k: p_ref[r:r + 1, :] for k, r in ROW_LN.items()})
        for i, (name, _) in enumerate(SMALL):
            g = grads[name]
            outs[4 * i][...] = g
            outs[4 * i + 1][...], outs[4 * i + 2][...], outs[4 * i + 3][...] = _adamw_math(
                ins[3 * i][...], g, ins[3 * i + 1][...], ins[3 * i + 2][...])

    whole = lambda shape: pl.BlockSpec(shape, lambda i, c: (0,) * len(shape))
    outs = pl.pallas_call(
        body, name="adamw_small",
        grid_spec=pltpu.PrefetchScalarGridSpec(
            num_scalar_prefetch=1, grid=(1,),
            in_specs=[whole(packs.shape), whole(own.shape)] + [whole(s) for _, s in SMALL for _ in range(3)],
            out_specs=[whole(s) for _, s in SMALL for _ in range(4)] + [whole((1, D))],
            scratch_shapes=[pltpu.VMEM(own.shape, F32)]),
        out_shape=[pltpu.HBM(s, F32) for _, s in SMALL for _ in range(4)] + [pltpu.HBM((1, D), F32)],
        compiler_params=_params(16, dimension_semantics=_seq()),
    )(place, *_hbm(packs, own, *[a for p in params for a in p]))
    return [outs[4 * i:4 * i + 4] for i in range(n)], outs[4 * n]


def _small_pack(gr):
    names = ["meta_blk"] + list(ROW_LN) + ["b_in_p", "wg2_p", "bg2", "sinks", "gn", "loss"]
    gate_w = GLA_HEADS * DK

    def body(*refs):
        src, out = dict(zip(names, refs)), refs[-1]
        out[...] = jnp.zeros_like(out)
        out[ROW_META:ROW_META + N_META, :] = src["meta_blk"][META_OFF:CH, :]
        for k, r in ROW_LN.items():
            out[r:r + 1, :] = src[k][...]
        for j in range(-(-D_IN // D)):
            width = min(D, D_IN - j * D)
            out[ROW_B_IN + j:ROW_B_IN + j + 1, 0:width] = src["b_in_p"][:, j * D:j * D + width]
        tail = slice(ROW_TAIL, ROW_TAIL + 1)
        out[tail, TAIL_BG2:TAIL_BG2 + gate_w] = src["bg2"][...]
        out[tail, TAIL_SINKS:TAIL_SINKS + SWA_HEADS] = src["sinks"][:, 0:SWA_HEADS]
        out[tail, TAIL_GN:TAIL_GN + DV] = src["gn"][...]
        out[tail, TAIL_LOSS:TAIL_LOSS + 1] = src["loss"][:, 0:1]
        out[ROW_WG2:ROW_WG2 + GATE_RANK, 0:gate_w] = src["wg2_p"][0:GATE_RANK, :]

    arrays = [gr[k] for k in names]
    return pl.pallas_call(
        body, name="small_pack", grid=(1,),
        in_specs=[_acc(a.shape) for a in arrays], out_specs=_acc((SMALL_ROWS, D)),
        out_shape=pltpu.HBM((SMALL_ROWS, D), F32),
        compiler_params=_params(16, dimension_semantics=_seq()),
    )(*_hbm(*arrays))


BIG = ("w_in", "w_out", "w_g", "w_u", "w_d")


def kernel(x, meta_tokens, ln_in_g, ln_in_b, w_in, b_in, w_gate_lr2, b_gate_lr2, attn_sinks, gla_norm_g, w_out, ln1_g, ln1_b, w_ffn_gate, w_ffn_up, w_ffn_down, ln2_g, ln2_b, loss_target, m_meta_tokens, m_ln_in_g, m_ln_in_b, m_w_in, m_b_in, m_w_gate_lr2, m_b_gate_lr2, m_attn_sinks, m_gla_norm_g, m_w_out, m_ln1_g, m_ln1_b, m_w_ffn_gate, m_w_ffn_up, m_w_ffn_down, m_ln2_g, m_ln2_b, v_meta_tokens, v_ln_in_g, v_ln_in_b, v_w_in, v_b_in, v_w_gate_lr2, v_b_gate_lr2, v_attn_sinks, v_gla_norm_g, v_w_out, v_ln1_g, v_ln1_b, v_w_ffn_gate, v_w_ffn_up, v_w_ffn_down, v_ln2_g, v_ln2_b):
    chip = 2 * lax.axis_index("x") + lax.axis_index("y")

    halves = lambda a: a.reshape(2, a.shape[0] // 2, a.shape[1])
    r_in = SHARD_ROWS["w_in"]
    first = [halves(a) for a in (jnp.pad(w_in[0].T.astype(BF16), ((0, W_IN_WIN - r_in), (0, 0))), meta_tokens,
                                 w_gate_lr2[0])]
    rest = [halves(a) for a in (w_out[0].astype(BF16), w_ffn_gate[0].T.astype(BF16), w_ffn_up[0].T.astype(BF16),
                                w_ffn_down[0].astype(BF16))]
    lands = lambda arrs: [(N_CHIPS,) + a.shape for a in arrs]
    first_handle, first_token = _ici_start("gather", first, lands(first), [], "gather_first_start")
    rest_handle, token = _ici_start("gather", rest, lands(rest), [first_token], "gather_rest_start")

    def fetch(handle, shards, after, name):
        got = _sibling_forward(_ici_wait("gather", handle, after, name + "_wait"), name + "_forward")
        return [lax.dynamic_update_index_in_dim(g, s, chip, axis=0) for g, s in zip(got, shards)]

    def fetch_first(after):
        g_in, g_meta, g_wg2 = fetch(first_handle, first, after, "gather_first")
        w_in_t = jnp.pad(g_in.reshape(N_CHIPS, W_IN_WIN, D)[:, :r_in].reshape(D_IN, D), ((0, D_IN_P - D_IN), (0, 0)))
        meta_full = jnp.concatenate([g_meta[s].reshape(N_META, -1) for s in range(N_CHIPS)], axis=1)
        wg2_full = jnp.concatenate([g_wg2[s].reshape(w_gate_lr2.shape[1], -1) for s in range(N_CHIPS)], axis=1)
        return w_in_t, meta_full, wg2_full

    def fetch_rest(after):
        return [g.reshape(-1, D) for g in fetch(rest_handle, rest, after, "gather_rest")]

    sent = {}

    def ship(key, grads, names):
        parts = _chip_partials([g.reshape(N_CHIPS, 2, -1, D) for g in grads], [BF16] * len(grads), names)
        handle, ship_token = _ici_start("scatter", parts, [p.shape for p in parts], [], "scatter_" + key + "_start")
        sent[key] = (parts, handle)
        return ship_token

    def ship_ffn(g):
        return ship("ffn", [g[k] for k in BIG[1:]], list(BIG[1:]))

    def ship_w_in(dw_in_t):
        win_start = [s * r_in // BF16_ROWS * BF16_ROWS for s in range(N_CHIPS)]
        return ship("w_in", [jnp.stack([dw_in_t[st:st + W_IN_WIN] for st in win_start])], ["w_in"])

    dx, gr = _local_step(
        x[0], loss_target[0], ln_in_g, ln_in_b, b_in[0], b_gate_lr2[0], attn_sinks[0], gla_norm_g[0], ln1_g[0],
        ln1_b[0], ln2_g[0], ln2_b[0], token, fetch_first, fetch_rest, ship_ffn, ship_w_in)
    ffn_got = _ici_wait("scatter", sent["ffn"][1], [dx], "scatter_ffn_wait")
    w_in_got = _ici_wait("scatter", sent["w_in"][1], [dx], "scatter_w_in_wait")

    small_own = _small_pack(gr)
    small_all = _small_exchange(small_own, [w_in_got[0]])
    red = _finish_reduce(sent["w_in"][0] + sent["ffn"][0], w_in_got + ffn_got)

    big_g = dict(zip(BIG, red))
    big_g["w_in"] = lax.dynamic_slice_in_dim(red[0], chip * (r_in % BF16_ROWS), r_in, axis=0)
    grads = dict(w_in=big_g["w_in"].T[None], w_out=big_g["w_out"][None], w_ffn_gate=big_g["w_g"].T[None],
                 w_ffn_up=big_g["w_u"].T[None], w_ffn_down=big_g["w_d"][None])
    weights = dict(meta_tokens=meta_tokens, ln_in_g=ln_in_g, ln_in_b=ln_in_b, w_in=w_in, b_in=b_in,
                   w_gate_lr2=w_gate_lr2, b_gate_lr2=b_gate_lr2, attn_sinks=attn_sinks, gla_norm_g=gla_norm_g,
                   w_out=w_out, ln1_g=ln1_g, ln1_b=ln1_b, w_ffn_gate=w_ffn_gate, w_ffn_up=w_ffn_up,
                   w_ffn_down=w_ffn_down, ln2_g=ln2_g, ln2_b=ln2_b)
    m_in = dict(meta_tokens=m_meta_tokens, ln_in_g=m_ln_in_g, ln_in_b=m_ln_in_b, w_in=m_w_in, b_in=m_b_in,
                w_gate_lr2=m_w_gate_lr2, b_gate_lr2=m_b_gate_lr2, attn_sinks=m_attn_sinks, gla_norm_g=m_gla_norm_g,
                w_out=m_w_out, ln1_g=m_ln1_g, ln1_b=m_ln1_b, w_ffn_gate=m_w_ffn_gate, w_ffn_up=m_w_ffn_up,
                w_ffn_down=m_w_ffn_down, ln2_g=m_ln2_g, ln2_b=m_ln2_b)
    v_in = dict(meta_tokens=v_meta_tokens, ln_in_g=v_ln_in_g, ln_in_b=v_ln_in_b, w_in=v_w_in, b_in=v_b_in,
                w_gate_lr2=v_w_gate_lr2, b_gate_lr2=v_b_gate_lr2, attn_sinks=v_attn_sinks, gla_norm_g=v_gla_norm_g,
                w_out=v_w_out, ln1_g=v_ln1_g, ln1_b=v_ln1_b, w_ffn_gate=v_w_ffn_gate, w_ffn_up=v_w_ffn_up,
                w_ffn_down=v_w_ffn_down, ln2_g=v_ln2_g, ln2_b=v_ln2_b)
    names = list(weights)
    big_names = ("w_in", "w_out", "w_ffn_gate", "w_ffn_up", "w_ffn_down")

    delta, new_m, new_v = {}, {}, {}
    for k, kk in zip(big_names, BIG):
        flip = (lambda a: a.T) if kk in ("w_in", "w_g", "w_u") else (lambda a: a)
        d_, m_, v_ = _adamw(flip(weights[k][0]), big_g[kk], flip(m_in[k][0]), flip(v_in[k][0]), "adamw_" + k)
        delta[k], new_m[k], new_v[k] = (flip(t)[None] for t in (d_, m_, v_))
    small_in = [tuple(src[k].reshape(shape) for src in (weights, m_in, v_in)) for k, shape in SMALL]
    place = jnp.stack([2 * chip + lax.axis_index("c"), chip]).astype(jnp.int32)
    small_out, tail_row = _adamw_small(place, small_all, small_own, small_in)
    for (k, _), results in zip(SMALL, small_out):
        grads[k], delta[k], new_m[k], new_v[k] = (r.reshape(weights[k].shape) for r in results)

    return (tail_row[0, TAIL_LOSS], dx[None], *[grads[k] for k in names], *[delta[k] for k in names], *[new_m[k] for k in names],
            *[new_v[k] for k in names])
```

```python
import jax
import jax.numpy as jnp
from jax import lax
from jax.experimental import pallas as pl
from jax.experimental.pallas import tpu as pltpu

F32 = jnp.float32
BF16 = jnp.bfloat16
MESH = pl.DeviceIdType.MESH

D = 1024
SEQ = 4096
N_META = 16
SWA_HEADS, SWA_KV_HEADS, DH = 8, 2, 64
WINDOW = 128
GLA_HEADS, DK, DV = 4, 64, 128
GLA_TAU = 16.0
CH = 64
D_FF = 2816
D_IN = 2320
LN_EPS = 1e-5
RMS_EPS = 1e-6
ALPHA = 2.0 ** 0.25
NEG = -1e30
ADAM_LR, ADAM_B1, ADAM_B2, ADAM_EPS, ADAM_WD, ADAM_STEP = 0.001, 0.9, 0.999, 1e-8, 0.01, 10
O_QS, O_KS, O_VS, O_QG, O_KG, O_VG, O_RG, O_LR = 0, 512, 640, 768, 1024, 1280, 1792, 2304

LANE = 128
BLK = WINDOW
GATE_RANK = 16
D_IN_P = D_IN + LANE - GATE_RANK
META_OFF = CH - N_META
HEAD_POS = (0, 4, 1, 5, 2, 6, 3, 7)
LN_ROWS = 512
TOKEN = (8, LANE)
N_CHIPS = 4
SHARD_ROWS = dict(w_in=D_IN // N_CHIPS, w_out=D // N_CHIPS, w_g=D_FF // N_CHIPS, w_u=D_FF // N_CHIPS,
                  w_d=D_FF // N_CHIPS)
SMALL_ROWS = 48
BF16_ROWS = 16
W_IN_WIN = -(-SHARD_ROWS["w_in"] // (2 * BF16_ROWS)) * 2 * BF16_ROWS
VMEM_CAP_MB = 64
VMEM_SPARE_MB = 6


def _lp():
    return SEQ + BLK


def _row_tile(cap):
    lp = _lp()
    return max(t for t in range(16, cap + 1, 16) if lp % t == 0)


def _params(vmem_mb, **kw):
    assert vmem_mb <= VMEM_CAP_MB - VMEM_SPARE_MB
    return pltpu.CompilerParams(vmem_limit_bytes=vmem_mb << 20, **kw)


def _seq(n=1):
    return ("arbitrary",) * n


def _const(shape):
    return pl.BlockSpec(shape, lambda *_: (0,) * len(shape), pipeline_mode=pl.Buffered(1))


def _acc(shape):
    return pl.BlockSpec(shape, lambda *_: (0,) * len(shape))


def _rows(tm, width):
    return pl.BlockSpec((tm, width), lambda i: (i, 0))


def _dot(a, b):
    return jnp.dot(a.astype(BF16), b.astype(BF16), preferred_element_type=F32)


def _dot_nt(a, b):
    return lax.dot_general(a.astype(BF16), b.astype(BF16), (((1,), (1,)), ((), ())), preferred_element_type=F32)


def _dot_tn(a, b):
    return lax.dot_general(a.astype(BF16), b.astype(BF16), (((0,), (0,)), ((), ())), preferred_element_type=F32)


def _dot_exact(a, b):
    return jnp.dot(a, b, precision=lax.Precision.HIGHEST, preferred_element_type=F32)


def _ln_stats(x):
    mu = jnp.mean(x, axis=-1, keepdims=True)
    xc = x - mu
    rstd = lax.rsqrt(jnp.mean(xc * xc, axis=-1, keepdims=True) + LN_EPS)
    return xc * rstd, rstd


def _ln_bwd(dy, xhat, rstd, g):
    dxh = dy * g
    return rstd * (dxh - jnp.mean(dxh, axis=-1, keepdims=True) - xhat * jnp.mean(dxh * xhat, axis=-1, keepdims=True))


def _sigmoid(x):
    return 1.0 / (1.0 + jnp.exp(-x))


def _iota(shape, dim):
    return lax.broadcasted_iota(jnp.int32, shape, dim)


def _hbm(*arrays):
    return tuple(pltpu.with_memory_space_constraint(a, pltpu.HBM) for a in arrays)


def _ln_in_fwd_real(x, g, b, token):
    tr = min(LN_ROWS, SEQ)

    def body(x_ref, g_ref, b_ref, token_ref, h_ref):
        xhat, _ = _ln_stats(x_ref[...])
        h_ref[...] = xhat * g_ref[...] + b_ref[...]

    return pl.pallas_call(
        body, name="ln_in_fwd", grid=(SEQ // tr,),
        in_specs=[_rows(tr, D), _const((1, D)), _const((1, D)), _const(TOKEN)],
        out_specs=_rows(tr, D),
        out_shape=pltpu.HBM((_lp(), D), F32),
        compiler_params=_params(32, dimension_semantics=_seq()),
    )(*_hbm(x, g, b), token)


def _ln_in_fwd_meta(h_real, meta_ext, g, b):
    def meta_body(m_ref, g_ref, b_ref, real_ref, h_ref):
        xhat, _ = _ln_stats(m_ref[...])
        h_ref[...] = xhat * g_ref[...] + b_ref[...]

    return pl.pallas_call(
        meta_body, name="ln_in_fwd_meta", grid=(1,),
        in_specs=[_const((BLK, D)), _const((1, D)), _const((1, D)), pl.BlockSpec(memory_space=pl.ANY)],
        out_specs=pl.BlockSpec((BLK, D), lambda i: (SEQ // BLK, 0)),
        out_shape=pltpu.HBM((_lp(), D), F32),
        input_output_aliases={3: 0},
        compiler_params=_params(16, dimension_semantics=_seq()),
    )(*_hbm(meta_ext, g, b, h_real))


def _in_proj(h0, w_in_t, b_in_p, wg2_p, bg2):
    tm = _row_tile(384)
    lp = _lp()
    widths = (512, 128, 128, 256, 256, 512, 512, 128)
    offs = (O_QS, O_KS, O_VS, O_QG, O_KG, O_VG, O_RG, O_LR)

    def body(h_ref, w_ref, b_ref, wg2_ref, bg2_ref, *outs):
        proj = _dot_nt(h_ref[...], w_ref[...]) + b_ref[...]
        for pos, h in enumerate(HEAD_POS):
            outs[0][:, pos * DH:(pos + 1) * DH] = proj[:, O_QS + h * DH:O_QS + (h + 1) * DH]
        for o_ref, off, wd in zip(outs[1:8], offs[1:], widths[1:]):
            o_ref[...] = proj[:, off:off + wd]
        outs[8][...] = _dot(proj[:, O_LR:O_LR + LANE], wg2_ref[...]) + bg2_ref[...]

    return pl.pallas_call(
        body, name="in_proj", grid=(lp // tm,),
        in_specs=[_rows(tm, D), _const((D_IN_P, D)), _const((1, D_IN_P)), _const((LANE, 256)), _const((1, 256))],
        out_specs=[_rows(tm, w) for w in widths] + [_rows(tm, 256)],
        out_shape=[pltpu.HBM((lp, w), F32) for w in widths] + [pltpu.HBM((lp, 256), F32)],
        compiler_params=_params(40, dimension_semantics=_seq()),
    )(*_hbm(h0, w_in_t, b_in_p, wg2_p, bg2))


def _swa_masks(n):
    nb = SEQ // BLK
    is_meta = n == nb
    ri = _iota((BLK, BLK), 0)
    cj = _iota((BLK, BLK), 1)
    meta_col = ((cj >= META_OFF) & (cj < CH)).astype(jnp.int32)
    meta_q = meta_col * ((cj <= ri) & (ri < CH)).astype(jnp.int32)
    valid_m = jnp.where(is_meta, meta_q, meta_col) > 0
    dist_m = jnp.where(is_meta, ri - cj, n * BLK + ri + CH - cj).astype(F32)
    valid_p = jnp.where((n >= 1) & (n < nb), (cj > ri).astype(jnp.int32), 0) > 0
    dist_p = (ri + BLK - cj).astype(F32)
    valid_c = jnp.where(n < nb, (cj <= ri).astype(jnp.int32), 0) > 0
    dist_c = (ri - cj).astype(F32)
    return (dist_m, dist_p, dist_c), (valid_m, valid_p, valid_c)


def _swa_bias(n):
    dists, valids = _swa_masks(n)
    return (jnp.concatenate([-d for d in dists], axis=1),
            jnp.concatenate([jnp.where(v, 0.0, NEG) for v in valids], axis=1))


def _swa_half(ref, pos, scale=1.0):
    col = ref[:, (pos // 2) * LANE:(pos // 2 + 1) * LANE]
    lane = _iota((BLK, LANE), 1)
    mine = lane < DH if pos % 2 == 0 else lane >= DH
    return jnp.where(mine, col * scale, 0.0).astype(BF16)


def _swa_merge(even, odd):
    return jnp.where(_iota((BLK, LANE), 1) < DH, even, odd)


def _swa_softmax(t, sink):
    m = jnp.maximum(jnp.max(t, axis=-1, keepdims=True), sink)
    e = jnp.exp(t - m)
    e_sink = jnp.exp(sink - m)
    inv = 1.0 / (jnp.sum(e, axis=-1, keepdims=True) + e_sink)
    return e * inv, e_sink * inv


def _swa_kv_specs(width):
    nb = SEQ // BLK
    return [pl.BlockSpec((BLK, width), lambda n: (nb, 0)),
            pl.BlockSpec((BLK, width), lambda n: (jnp.clip(n - 1, 0, nb - 1), 0)),
            pl.BlockSpec((BLK, width), lambda n: (jnp.minimum(n, nb), 0))]


def _swa_fwd(sinks, qs, ks, vs):
    nb = SEQ // BLK
    heads = range(SWA_HEADS)

    def body(sink_ref, q_ref, km_ref, kp_ref, kc_ref, vm_ref, vp_ref, vc_ref, o_ref):
        negdist, maskbias = _swa_bias(pl.program_id(0))
        k_all = jnp.concatenate([km_ref[...], kp_ref[...], kc_ref[...]], axis=0).astype(BF16)
        v_all = jnp.concatenate([vm_ref[...], vp_ref[...], vc_ref[...]], axis=0).astype(BF16)
        q = [_swa_half(q_ref, pos, DH ** -0.5) for pos in heads]
        t = [_dot_nt(q[pos], k_all) + (2.0 ** -(HEAD_POS[pos] + 1) * negdist + maskbias) for pos in heads]
        p = [_swa_softmax(t[pos], sink_ref[HEAD_POS[pos]])[0].astype(BF16) for pos in heads]
        o = [_dot(p[pos], v_all) for pos in heads]
        for col in range(SWA_HEADS // 2):
            o_ref[:, col * LANE:(col + 1) * LANE] = _swa_merge(o[2 * col], o[2 * col + 1])

    kvw = SWA_KV_HEADS * DH
    return pl.pallas_call(
        body, name="swa_fwd", grid=(nb + 1,),
        in_specs=[pl.BlockSpec(memory_space=pltpu.SMEM), _rows(BLK, SWA_HEADS * DH)] + _swa_kv_specs(kvw) + _swa_kv_specs(kvw),
        out_specs=_rows(BLK, SWA_HEADS * DH),
        out_shape=pltpu.HBM((_lp(), SWA_HEADS * DH), F32),
        compiler_params=_params(16, dimension_semantics=_seq()),
    )(sinks, *_hbm(qs, ks, ks, ks, vs, vs, vs))


GLA_PER_STEP = BLK // CH


def _gla_block(s):
    nb = SEQ // BLK
    return jnp.where(s == 0, nb, s - 1)


def _gla_rowmask(s):
    ri = _iota((BLK, 1), 0)
    m = jnp.where(s == 0, ((ri >= META_OFF) & (ri < CH)).astype(jnp.int32), 1)
    return (m > 0).astype(F32) + jnp.zeros((BLK, 1), F32)


def _gla_chunk_masks():
    r, c = _iota((BLK, BLK), 0), _iota((BLK, BLK), 1)
    same = ((r < CH) & (c < CH)) | ((r >= CH) & (c >= CH))
    return same & (r >= c), same & (r <= c), same


def _gla_decay(z, rmask):
    log_g = (jnp.minimum(z, 0.0) - jnp.log1p(jnp.exp(-jnp.abs(z)))) * (rmask / GLA_TAU)
    lower, _, same = _gla_chunk_masks()
    return _dot_exact(lower.astype(F32), log_g), _dot_exact(same.astype(F32), log_g)


def _gla_slices(c, h):
    return slice(c * CH, (c + 1) * CH), slice(h * DK, (h + 1) * DK), slice(h * DV, (h + 1) * DV)


def _gla_fwd(qg, kg, vg, z):
    steps = SEQ // BLK + 1
    kw, vw = GLA_HEADS * DK, GLA_HEADS * DV
    pairs = [(c, h) for c in range(GLA_PER_STEP) for h in range(GLA_HEADS)]

    def body(q_ref, k_ref, v_ref, z_ref, o_ref, st_ref, st):
        s = pl.program_id(0)

        @pl.when(s == 0)
        def _():
            st[...] = jnp.zeros_like(st)

        rmask = _gla_rowmask(s)
        b, b_last = _gla_decay(z_ref[...], rmask)
        q = q_ref[...] * (rmask * DK ** -0.5)
        k = k_ref[...] * rmask
        v = v_ref[...] * rmask
        qe = q * jnp.exp(b)
        ke = k * jnp.exp(-b)
        kd = k * jnp.exp(b_last - b)
        e_last = jnp.exp(b_last)
        causal = _iota((CH, CH), 0) >= _iota((CH, CH), 1)
        a, upd, intra = {}, {}, {}
        for c, h in pairs:
            rows, ks, vs_ = _gla_slices(c, h)
            a[c, h] = jnp.where(causal, _dot_nt(qe[rows, ks], ke[rows, ks]), 0.0)
            upd[c, h] = _dot_tn(v[rows, vs_], kd[rows, ks])
        for c, h in pairs:
            rows, ks, vs_ = _gla_slices(c, h)
            intra[c, h] = _dot(a[c, h], v[rows, vs_])
        state = st[...]
        for c in range(GLA_PER_STEP):
            st_ref[0, c] = state
            for h in range(GLA_HEADS):
                rows, ks, vs_ = _gla_slices(c, h)
                o_ref[rows, vs_] = intra[c, h] + _dot_nt(qe[rows, ks], state[:, ks])
            state = state * e_last[c * CH:c * CH + 1] + jnp.concatenate([upd[c, h] for h in range(GLA_HEADS)], axis=1)
        st[...] = state

    blk = lambda w: pl.BlockSpec((BLK, w), lambda s: (_gla_block(s), 0))
    return pl.pallas_call(
        body, name="gla_fwd", grid=(steps,),
        in_specs=[blk(kw), blk(kw), blk(vw), blk(kw)],
        out_specs=[blk(vw), pl.BlockSpec((1, GLA_PER_STEP, DV, kw), lambda s: (s, 0, 0, 0))],
        out_shape=[pltpu.HBM((_lp(), vw), F32), pltpu.HBM((steps, GLA_PER_STEP, DV, kw), F32)],
        scratch_shapes=[pltpu.VMEM((DV, kw), F32)],
        compiler_params=_params(16, dimension_semantics=_seq()),
    )(*_hbm(qg, kg, vg, z))


def _post_mix(o_s, o_gla, r_g, h0, gn4, w_out, g1, b1):
    tm = _row_tile(384)
    lp = _lp()

    def body(os_ref, og_ref, r_ref, h0_ref, gn_ref, w_ref, g_ref, b_ref, o_ref, pre_ref, h1_ref):
        for pos, h in enumerate(HEAD_POS):
            o_ref[:, h * DH:(h + 1) * DH] = os_ref[:, pos * DH:(pos + 1) * DH].astype(BF16)
        for h in range(GLA_HEADS):
            hs = slice(h * DV, (h + 1) * DV)
            xg = og_ref[:, hs]
            n = xg * lax.rsqrt(jnp.mean(xg * xg, axis=-1, keepdims=True) + RMS_EPS) * gn_ref[...]
            r = r_ref[:, hs]
            o_ref[:, 512 + h * DV:512 + (h + 1) * DV] = (n * (r * _sigmoid(r))).astype(BF16)
        pre = ALPHA * h0_ref[...] + _dot(o_ref[...], w_ref[...])
        pre_ref[...] = pre
        xhat, _ = _ln_stats(pre)
        h1_ref[...] = xhat * g_ref[...] + b_ref[...]

    return pl.pallas_call(
        body, name="post_mix", grid=(lp // tm,),
        in_specs=[_rows(tm, 512), _rows(tm, 512), _rows(tm, 512), _rows(tm, D), _const((1, DV)), _const((D, D)),
                  _const((1, D)), _const((1, D))],
        out_specs=[_rows(tm, D), _rows(tm, D), _rows(tm, D)],
        out_shape=[pltpu.HBM((lp, D), BF16), pltpu.HBM((lp, D), F32),
                   pltpu.HBM((lp, D), F32)],
        compiler_params=_params(32, dimension_semantics=_seq()),
    )(*_hbm(o_s, o_gla, r_g, h0, gn4, w_out, g1, b1))


def _ffn_fwd_loss_bwd(h1, wg_t, wu_t, wd, target, g2, b2):
    lp = _lp()
    tm = max(t for t in range(BLK, 384 + 1, BLK) if lp % t == 0)
    steps = lp // tm
    last_blk = SEQ // BLK - 1
    half = D_FF // 2
    n_t = tm // BLK

    def body(*refs):
        h_ref, wg_ref, wu_ref, wd_ref = refs[:4]
        t_refs = refs[4:4 + n_t]
        g2_ref, b2_ref, a_ref, dgate_ref, dup_ref, dp_ref, loss_ref, dg_ref, db_ref, g_s, u_s, acc = refs[4 + n_t:]
        i = pl.program_id(0)

        @pl.when(i == 0)
        def _():
            acc[...] = jnp.zeros_like(acc)
            dg_ref[...] = jnp.zeros_like(dg_ref)
            db_ref[...] = jnp.zeros_like(db_ref)

        h = h_ref[...]
        hb = h.astype(BF16)
        pre = ALPHA * h
        for j in range(2):
            cols = slice(j * half, (j + 1) * half)
            g = _dot_nt(hb, wg_ref[cols, :])
            u = _dot_nt(hb, wu_ref[cols, :])
            g_s[:, cols] = g
            u_s[:, cols] = u
            pre = pre + _dot(g * _sigmoid(g) * u, wd_ref[cols, :])
        xhat, rstd = _ln_stats(pre)
        real = i * tm + _iota((tm, 1), 0) < SEQ
        target_rows = jnp.concatenate([t[...] for t in t_refs], axis=0)
        diff = jnp.where(real, xhat * g2_ref[...] + b2_ref[...] - target_rows, 0.0)
        acc[...] += jnp.sum(diff * diff, axis=0, keepdims=True)
        dy = diff * (1.0 / D)
        dpre = _ln_bwd(dy, xhat, rstd, g2_ref[...])
        dp_ref[...] = dpre
        dg_ref[...] += jnp.sum(dy * xhat, axis=0, keepdims=True)
        db_ref[...] += jnp.sum(dy, axis=0, keepdims=True)
        dpb = dpre.astype(BF16)
        for j in range(2):
            cols = slice(j * half, (j + 1) * half)
            g, u = g_s[:, cols], u_s[:, cols]
            sg = _sigmoid(g)
            silu = g * sg
            da = _dot_nt(dpb, wd_ref[cols, :])
            a_ref[:, cols] = (silu * u).astype(BF16)
            dgate_ref[:, cols] = (da * u * (sg * (1.0 + g * (1.0 - sg)))).astype(BF16)
            dup_ref[:, cols] = (da * silu).astype(BF16)

        @pl.when(i == steps - 1)
        def _():
            loss_ref[...] = jnp.zeros_like(loss_ref) + (0.5 / D) * jnp.sum(acc[...], axis=1, keepdims=True)

    t_spec = lambda k: pl.BlockSpec((BLK, D), lambda i: (jnp.minimum(i * n_t + k, last_blk), 0))
    return pl.pallas_call(
        body, name="ffn_fwd_loss_bwd", grid=(steps,),
        in_specs=[_rows(tm, D), _const((D_FF, D)), _const((D_FF, D)), _const((D_FF, D))]
        + [t_spec(k) for k in range(n_t)] + [_const((1, D)), _const((1, D))],
        out_specs=[_rows(tm, D_FF), _rows(tm, D_FF), _rows(tm, D_FF), _rows(tm, D), _acc((1, LANE)), _acc((1, D)),
                   _acc((1, D))],
        out_shape=[pltpu.HBM((lp, D_FF), BF16)] * 3 + [pltpu.HBM((lp, D), F32), pltpu.HBM((1, LANE), F32),
                                                         pltpu.HBM((1, D), F32), pltpu.HBM((1, D), F32)],
        scratch_shapes=[pltpu.VMEM((tm, D_FF), F32), pltpu.VMEM((tm, D_FF), F32), pltpu.VMEM((1, D), F32)],
        compiler_params=_params(58, dimension_semantics=_seq()),
    )(*_hbm(h1, wg_t, wu_t, wd, *[target] * n_t, g2, b2))


def _ffn_out_bwd(dpre2, dgate, dup, pre1, wg_t, wu_t, g1, w_out, o_gla, r_g, gn4):
    tm = _row_tile(384)
    lp = _lp()

    def body(dp_ref, dg_ref, du_ref, p1_ref, wg_ref, wu_ref, g1_ref, w_ref, og_ref, r_ref, gn_ref,
             dp1_ref, dg1_ref, db1_ref, dos_ref, dog_ref, dr_ref, dgn_ref):
        @pl.when(pl.program_id(0) == 0)
        def _():
            for acc_ref in (dg1_ref, db1_ref, dgn_ref):
                acc_ref[...] = jnp.zeros_like(acc_ref)

        dh1 = ALPHA * dp_ref[...] + _dot(dg_ref[...], wg_ref[...]) + _dot(du_ref[...], wu_ref[...])
        xhat, rstd1 = _ln_stats(p1_ref[...])
        dpre1 = _ln_bwd(dh1, xhat, rstd1, g1_ref[...])
        dp1_ref[...] = dpre1
        dg1_ref[...] += jnp.sum(dh1 * xhat, axis=0, keepdims=True)
        db1_ref[...] += jnp.sum(dh1, axis=0, keepdims=True)

        do = _dot_nt(dpre1, w_ref[...])
        for pos, h in enumerate(HEAD_POS):
            dos_ref[:, pos * DH:(pos + 1) * DH] = do[:, h * DH:(h + 1) * DH]
        gn = gn_ref[...]
        for h in range(GLA_HEADS):
            hs = slice(h * DV, (h + 1) * DV)
            xg = og_ref[:, hs]
            rstd = lax.rsqrt(jnp.mean(xg * xg, axis=-1, keepdims=True) + RMS_EPS)
            nx = xg * rstd
            r = r_ref[:, hs]
            sr = _sigmoid(r)
            d_o = do[:, 512 + h * DV:512 + (h + 1) * DV]
            dr_ref[:, hs] = d_o * (nx * gn) * (sr * (1.0 + r * (1.0 - sr)))
            dn = d_o * (r * sr)
            dgn_ref[...] += jnp.sum(dn * nx, axis=0, keepdims=True)
            dnx = dn * gn
            dog_ref[:, hs] = rstd * (dnx - nx * jnp.mean(dnx * nx, axis=-1, keepdims=True))

    return pl.pallas_call(
        body, name="ffn_out_bwd", grid=(lp // tm,),
        in_specs=[_rows(tm, D), _rows(tm, D_FF), _rows(tm, D_FF), _rows(tm, D), _const((D_FF, D)), _const((D_FF, D)),
                  _const((1, D)), _const((D, D)), _rows(tm, 512), _rows(tm, 512), _const((1, DV))],
        out_specs=[_rows(tm, D), _acc((1, D)), _acc((1, D)), _rows(tm, 512), _rows(tm, 512), _rows(tm, 512),
                   _acc((1, DV))],
        out_shape=[pltpu.HBM((lp, D), F32), pltpu.HBM((1, D), F32), pltpu.HBM((1, D), F32)]
        + [pltpu.HBM((lp, 512), F32)] * 3 + [pltpu.HBM((1, DV), F32)],
        compiler_params=_params(48, dimension_semantics=_seq()),
    )(*_hbm(dpre2, dgate, dup, pre1, wg_t, wu_t, g1, w_out, o_gla, r_g, gn4))


def _atb(a, b, name):
    lp = _lp()
    tm = _row_tile(1408)
    n, w = a.shape[1], b.shape[1]
    bw = 512 if n * w * 4 > (4 << 20) else w

    def body(a_ref, b_ref, o_ref):
        @pl.when(pl.program_id(1) == 0)
        def _():
            o_ref[...] = jnp.zeros_like(o_ref)

        o_ref[...] += _dot_tn(a_ref[...], b_ref[...])

    return pl.pallas_call(
        body, name=name, grid=(w // bw, lp // tm),
        in_specs=[pl.BlockSpec((tm, n), lambda j, k: (k, 0)), pl.BlockSpec((tm, bw), lambda j, k: (k, j))],
        out_specs=pl.BlockSpec((n, bw), lambda j, k: (0, j)),
        out_shape=pltpu.HBM((n, w), F32),
        compiler_params=_params(48, dimension_semantics=_seq(2)),
    )(*_hbm(a, b))


def _gla_bwd(qg, kg, vg, z, do_gla, st_all, token):
    steps = SEQ // BLK + 1
    kw, vw = GLA_HEADS * DK, GLA_HEADS * DV
    pairs = [(c, h) for c in range(GLA_PER_STEP) for h in range(GLA_HEADS)]
    heads = range(GLA_HEADS)

    def body(q_ref, k_ref, v_ref, z_ref, do_ref, st_ref, token_ref, dq_ref, dk_ref, dv_ref, dz_ref, dst):
        @pl.when(pl.program_id(0) == 0)
        def _():
            dst[...] = jnp.zeros_like(dst)

        rmask = _gla_rowmask(steps - 1 - pl.program_id(0))
        zz = z_ref[...]
        b, b_last = _gla_decay(zz, rmask)
        e_b, e_nb, e_kd, e_last = jnp.exp(b), jnp.exp(-b), jnp.exp(b_last - b), jnp.exp(b_last)
        q = q_ref[...] * (rmask * DK ** -0.5)
        k = k_ref[...] * rmask
        v = v_ref[...] * rmask
        qe, ke, kd = q * e_b, k * e_nb, k * e_kd
        d_o = do_ref[...]
        causal = _iota((CH, CH), 0) >= _iota((CH, CH), 1)
        a, da, dqe, dke, dv_intra, carry = {}, {}, {}, {}, {}, {}
        for c, h in pairs:
            rows, ks, vs_ = _gla_slices(c, h)
            a[c, h] = jnp.where(causal, _dot_nt(qe[rows, ks], ke[rows, ks]), 0.0)
            da[c, h] = jnp.where(causal, _dot_nt(d_o[rows, vs_], v[rows, vs_]), 0.0)
            carry[c, h] = _dot_tn(d_o[rows, vs_], qe[rows, ks])
        for c, h in pairs:
            rows, ks, vs_ = _gla_slices(c, h)
            dqe[c, h] = _dot(d_o[rows, vs_], st_ref[0, c][:, ks]) + _dot(da[c, h], ke[rows, ks])
            dke[c, h] = _dot_tn(da[c, h], qe[rows, ks])
            dv_intra[c, h] = _dot_tn(a[c, h], d_o[rows, vs_])
        dstate = dst[...]
        dkd, db_decay = {}, {}
        for c in reversed(range(GLA_PER_STEP)):
            for h in heads:
                rows, ks, vs_ = _gla_slices(c, h)
                dkd[c, h] = _dot(v[rows, vs_], dstate[:, ks])
                dv_ref[rows, vs_] = dv_intra[c, h] + _dot_nt(kd[rows, ks], dstate[:, ks])
            chunk_last = e_last[c * CH:c * CH + 1]
            db_decay[c] = jnp.sum(dstate * st_ref[0, c], axis=0, keepdims=True) * chunk_last
            dstate = dstate * chunk_last + jnp.concatenate([carry[c, h] for h in heads], axis=1)
        dst[...] = dstate
        rows_of = lambda parts: jnp.concatenate(
            [jnp.concatenate([parts[c, h] for h in heads], axis=1) for c in range(GLA_PER_STEP)], axis=0)
        dqe_all, dke_all, dkd_all = rows_of(dqe), rows_of(dke), rows_of(dkd)
        dq_ref[...] = dqe_all * e_b * (rmask * DK ** -0.5)
        dk_ref[...] = (dke_all * e_nb + dkd_all * e_kd) * rmask
        dkd_kd = dkd_all * kd
        db = dqe_all * qe - dke_all * ke - dkd_kd
        _, upper, same = _gla_chunk_masks()
        decay_rows = jnp.concatenate([jnp.broadcast_to(db_decay[c], (CH, kw)) for c in range(GLA_PER_STEP)], axis=0)
        dlog_g = _dot_exact(upper.astype(F32), db) + _dot_exact(same.astype(F32), dkd_kd) + decay_rows
        dz_ref[...] = dlog_g * (rmask / GLA_TAU) * _sigmoid(-zz)

    blk = lambda w: pl.BlockSpec((BLK, w), lambda s: (_gla_block(steps - 1 - s), 0))
    return pl.pallas_call(
        body, name="gla_bwd", grid=(steps,),
        in_specs=[blk(kw), blk(kw), blk(vw), blk(kw), blk(vw),
                  pl.BlockSpec((1, GLA_PER_STEP, DV, kw), lambda s: (steps - 1 - s, 0, 0, 0)), _const(TOKEN)],
        out_specs=[blk(kw), blk(kw), blk(vw), blk(kw)],
        out_shape=[pltpu.HBM((_lp(), kw), F32), pltpu.HBM((_lp(), kw), F32),
                   pltpu.HBM((_lp(), vw), F32), pltpu.HBM((_lp(), kw), F32)],
        scratch_shapes=[pltpu.VMEM((DV, kw), F32)],
        compiler_params=_params(16, dimension_semantics=_seq()),
    )(*_hbm(qg, kg, vg, z, do_gla, st_all), token)


def _swa_bwd(sinks, qs, ks, vs, do_s, token):
    nb = SEQ // BLK
    kvw = SWA_KV_HEADS * DH
    scale = DH ** -0.5
    heads = range(SWA_HEADS)

    def body(sink_ref, q_ref, km_ref, kp_ref, kc_ref, vm_ref, vp_ref, vc_ref, do_ref, token_ref,
             dq_ref, dk_ref, dv_ref, dsink_ref, carry_k, carry_v, meta_k, meta_v):
        n = pl.program_id(0)

        @pl.when(n == 0)
        def _():
            for r in (carry_k, carry_v, meta_k, meta_v):
                r[...] = jnp.zeros_like(r)
            dsink_ref[...] = jnp.zeros_like(dsink_ref)

        @pl.when(n <= nb)
        def _():
            negdist, maskbias = _swa_bias(n)
            lane = _iota((1, LANE), 1)
            k_all = jnp.concatenate([km_ref[...], kp_ref[...], kc_ref[...]], axis=0).astype(BF16)
            v_all = jnp.concatenate([vm_ref[...], vp_ref[...], vc_ref[...]], axis=0).astype(BF16)
            q = [_swa_half(q_ref, pos, scale) for pos in heads]
            d_o = [_swa_half(do_ref, pos) for pos in heads]
            t = [_dot_nt(q[pos], k_all) + (2.0 ** -(HEAD_POS[pos] + 1) * negdist + maskbias) for pos in heads]
            dp = [_dot_nt(d_o[pos], v_all) for pos in heads]
            soft = [_swa_softmax(t[pos], sink_ref[HEAD_POS[pos]]) for pos in heads]
            p = [s[0] for s in soft]
            delta = [jnp.sum(p[pos] * dp[pos], axis=-1, keepdims=True) for pos in heads]
            ds = [(p[pos] * (dp[pos] - delta[pos])).astype(BF16) for pos in heads]
            dq = [_dot(ds[pos], k_all) for pos in heads]
            for col in range(SWA_HEADS // 2):
                dq_ref[:, col * LANE:(col + 1) * LANE] = scale * _swa_merge(dq[2 * col], dq[2 * col + 1])
            dsink = jnp.zeros((1, LANE), F32)
            for pos in heads:
                dsink = dsink + jnp.where(lane == HEAD_POS[pos],
                                          -jnp.sum(soft[pos][1] * delta[pos], axis=0, keepdims=True), 0.0)
            dsink_ref[...] += dsink
            dk3 = _dot_tn(jnp.concatenate(q, axis=0), jnp.concatenate(ds, axis=0)).T
            dv3 = _dot_tn(jnp.concatenate(d_o, axis=0), jnp.concatenate([x.astype(BF16) for x in p], axis=0)).T
            meta_k[...] += dk3[0:BLK]
            meta_v[...] += dv3[0:BLK]
            dk_ref[...] = carry_k[...] + dk3[BLK:2 * BLK]
            dv_ref[...] = carry_v[...] + dv3[BLK:2 * BLK]
            carry_k[...] = dk3[2 * BLK:3 * BLK]
            carry_v[...] = dv3[2 * BLK:3 * BLK]

        @pl.when(n == nb + 1)
        def _():
            dk_ref[...] = meta_k[...]
            dv_ref[...] = meta_v[...]

    kv_out = pl.BlockSpec((BLK, kvw), lambda n: (jnp.where(n == nb + 1, nb, jnp.clip(n - 1, 0, nb - 1)), 0))
    qblk = pl.BlockSpec((BLK, SWA_HEADS * DH), lambda n: (jnp.minimum(n, nb), 0))
    return pl.pallas_call(
        body, name="swa_bwd", grid=(nb + 2,),
        in_specs=[pl.BlockSpec(memory_space=pltpu.SMEM), qblk] + _swa_kv_specs(kvw) + _swa_kv_specs(kvw)
        + [qblk, _const(TOKEN)],
        out_specs=[qblk, kv_out, kv_out, _acc((1, LANE))],
        out_shape=[pltpu.HBM((_lp(), SWA_HEADS * DH), F32), pltpu.HBM((_lp(), kvw), F32),
                   pltpu.HBM((_lp(), kvw), F32), pltpu.HBM((1, LANE), F32)],
        scratch_shapes=[pltpu.VMEM((BLK, kvw), F32)] * 4,
        compiler_params=_params(16, dimension_semantics=_seq()),
    )(sinks, *_hbm(qs, ks, ks, ks, vs, vs, vs, do_s), token)


def _in_bwd(dqs, dks, dvs, dqg, dkg, dvg, drg, dz, dpre1, w_in_t, wg2_p):
    tm = _row_tile(384)
    lp = _lp()
    widths = (512, 128, 128, 256, 256, 512, 512)
    offs = (O_QS, O_KS, O_VS, O_QG, O_KG, O_VG, O_RG)

    def body(*refs):
        parts, (dz_ref, dp1_ref, w_ref, wg2_ref, dproj_ref, dh0_ref, dbin_ref, dbg_ref) = refs[:7], refs[7:]

        @pl.when(pl.program_id(0) == 0)
        def _():
            dbin_ref[...] = jnp.zeros_like(dbin_ref)
            dbg_ref[...] = jnp.zeros_like(dbg_ref)

        for pos, h in enumerate(HEAD_POS):
            val = parts[0][:, pos * DH:(pos + 1) * DH]
            dproj_ref[:, O_QS + h * DH:O_QS + (h + 1) * DH] = val.astype(BF16)
            dbin_ref[:, O_QS + h * DH:O_QS + (h + 1) * DH] += jnp.sum(val, axis=0, keepdims=True)
        for p_ref, off, wd in zip(parts[1:], offs[1:], widths[1:]):
            val = p_ref[...]
            dproj_ref[:, off:off + wd] = val.astype(BF16)
            dbin_ref[:, off:off + wd] += jnp.sum(val, axis=0, keepdims=True)
        dz = dz_ref[...]
        dlr = _dot_nt(dz, wg2_ref[...])
        dproj_ref[:, O_LR:O_LR + LANE] = dlr.astype(BF16)
        dbin_ref[:, O_LR:O_LR + LANE] += jnp.sum(dlr, axis=0, keepdims=True)
        dbg_ref[...] += jnp.sum(dz, axis=0, keepdims=True)
        dh0_ref[...] = ALPHA * dp1_ref[...] + _dot(dproj_ref[...], w_ref[...])

    return pl.pallas_call(
        body, name="in_bwd", grid=(lp // tm,),
        in_specs=[_rows(tm, w) for w in widths] + [_rows(tm, 256), _rows(tm, D), _const((D_IN_P, D)), _const((LANE, 256))],
        out_specs=[_rows(tm, D_IN_P), _rows(tm, D), _acc((1, D_IN_P)), _acc((1, 256))],
        out_shape=[pltpu.HBM((lp, D_IN_P), BF16), pltpu.HBM((lp, D), F32),
                   pltpu.HBM((1, D_IN_P), F32), pltpu.HBM((1, 256), F32)],
        compiler_params=_params(40, dimension_semantics=_seq()),
    )(*_hbm(dqs, dks, dvs, dqg, dkg, dvg, drg, dz, dpre1, w_in_t, wg2_p))


def _ln_in_bwd(x, meta_ext, dh0, g, token):
    tr = min(LN_ROWS, SEQ)

    def ln_bwd(x_ref, dh_ref, g_ref, dx_ref, dg_ref, db_ref):
        @pl.when(pl.program_id(0) == 0)
        def _():
            dg_ref[...] = jnp.zeros_like(dg_ref)
            db_ref[...] = jnp.zeros_like(db_ref)

        xhat, rstd = _ln_stats(x_ref[...])
        dh = dh_ref[...]
        dx_ref[...] = _ln_bwd(dh, xhat, rstd, g_ref[...])
        dg_ref[...] += jnp.sum(dh * xhat, axis=0, keepdims=True)
        db_ref[...] += jnp.sum(dh, axis=0, keepdims=True)

    def body(x_ref, dh_ref, g_ref, token_ref, dx_ref, dg_ref, db_ref):
        ln_bwd(x_ref, dh_ref, g_ref, dx_ref, dg_ref, db_ref)

    def meta_body(m_ref, dh_ref, g_ref, dm_ref, dg_ref, db_ref):
        ln_bwd(m_ref, dh_ref, g_ref, dm_ref, dg_ref, db_ref)

    sums = [pltpu.HBM((1, D), F32), pltpu.HBM((1, D), F32)]
    dx, dg, db = pl.pallas_call(
        body, name="ln_in_bwd", grid=(SEQ // tr,),
        in_specs=[_rows(tr, D), _rows(tr, D), _const((1, D)), _const(TOKEN)],
        out_specs=[_rows(tr, D), _acc((1, D)), _acc((1, D))],
        out_shape=[pltpu.HBM((SEQ, D), F32)] + sums,
        compiler_params=_params(32, dimension_semantics=_seq()),
    )(*_hbm(x, dh0, g), token)
    dm, dg_m, db_m = pl.pallas_call(
        meta_body, name="ln_in_bwd_meta", grid=(1,),
        in_specs=[_const((BLK, D)), pl.BlockSpec((BLK, D), lambda i: (SEQ // BLK, 0)), _const((1, D))],
        out_specs=[_acc((BLK, D)), _acc((1, D)), _acc((1, D))],
        out_shape=[pltpu.HBM((BLK, D), F32)] + sums,
        compiler_params=_params(16, dimension_semantics=_seq()),
    )(*_hbm(meta_ext, dh0, g))
    return dx, dm, dg + dg_m, db + db_m


def _local_step(x, target, ln_in_g, ln_in_b, b_in, bg2, sinks, gn, g1, b1, g2, b2,
                token, fetch_first, fetch_rest, ship_ffn, ship_w_in):
    row = lambda v: v.reshape(1, -1).astype(F32)
    b_in_p = jnp.pad(row(b_in), ((0, 0), (0, D_IN_P - D_IN)))
    gn4 = row(gn)
    sinks = sinks.reshape(-1).astype(F32)

    h_real = _ln_in_fwd_real(x, row(ln_in_g), row(ln_in_b), token)
    w_in_t, meta_full, wg2 = fetch_first([h_real])
    meta_ext = jnp.pad(meta_full, ((META_OFF, BLK - CH), (0, 0)))
    wg2_p = jnp.pad(wg2, ((0, LANE - wg2.shape[0]), (0, 0))).astype(BF16)
    h0 = _ln_in_fwd_meta(h_real, meta_ext, row(ln_in_g), row(ln_in_b))
    qs, ks, vs, qg, kg, vg, rg, glr, z = _in_proj(h0, w_in_t, b_in_p, wg2_p, row(bg2))
    o_s = _swa_fwd(sinks, qs, ks, vs)
    o_gla, st_all = _gla_fwd(qg, kg, vg, z)
    w_out, wg_t, wu_t, wd = fetch_rest([o_s, o_gla])
    o, pre1, h1 = _post_mix(o_s, o_gla, rg, h0, gn4, w_out, row(g1), row(b1))
    a, dgate, dup, dpre2, loss, dg2, db2 = _ffn_fwd_loss_bwd(h1, wg_t, wu_t, wd, target, row(g2), row(b2))
    dpre1, dg1, db1, do_s, do_gla, drg, dgn = _ffn_out_bwd(dpre2, dgate, dup, pre1, wg_t, wu_t, row(g1), w_out, o_gla,
                                                           rg, gn4)
    dwd = _atb(a, dpre2, "dw_down")
    dwg_t = _atb(dgate, h1, "dw_gate")
    dwu_t = _atb(dup, h1, "dw_up")
    dw_out = _atb(o, dpre1, "dw_out")
    token = ship_ffn(dict(w_out=dw_out, w_g=dwg_t, w_u=dwu_t, w_d=dwd))
    dqg, dkg, dvg, dz = _gla_bwd(qg, kg, vg, z, do_gla, st_all, token)
    dqs, dks, dvs, dsinks = _swa_bwd(sinks, qs, ks, vs, do_s, token)
    dproj, dh0, db_in_p, dbg2 = _in_bwd(dqs, dks, dvs, dqg, dkg, dvg, drg, dz, dpre1, w_in_t, wg2_p)
    token = ship_w_in(_atb(dproj, h0, "dw_in"))
    dwg2_p = _atb(glr, dz, "dw_gate_lr2")
    dx, dmeta_blk, dg_in, db_in_ln = _ln_in_bwd(x, meta_ext, dh0, row(ln_in_g), token)

    small = dict(meta_blk=dmeta_blk, ln_in_g=dg_in, ln_in_b=db_in_ln, ln1_g=dg1, ln1_b=db1, ln2_g=dg2, ln2_b=db2,
                 b_in_p=db_in_p, wg2_p=dwg2_p, bg2=dbg2, sinks=dsinks, gn=dgn, loss=loss)
    return dx, small


HBM = pl.BlockSpec(memory_space=pltpu.HBM)


def _place():
    return lax.axis_index("x"), lax.axis_index("y"), lax.axis_index("c")


def _other_chips(x, y):
    return [(1 - x, y), (x, 1 - y), (1 - x, 1 - y)]


def _dma_sems(n):
    return pltpu.SemaphoreType.DMA((n,))


def _comm_params():
    return pltpu.CompilerParams(has_side_effects=True)


SEM = pl.BlockSpec(memory_space=pltpu.SEMAPHORE)


def _ici_copies(kind, landing, srcs, lands, send_sems, recv_sems):
    x, y, c = _place()
    mine = 2 * x + y
    copies = []
    for a in range(len(srcs)):
        for j, (px, py) in enumerate(_other_chips(x, y)):
            slab = 2 * px + py if landing else mine
            if kind == "gather":
                src, dst = srcs[a].at[c], lands[a].at[slab, c]
            else:
                src, dst = srcs[a].at[2 * px + py], lands[a].at[slab]
            copies.append(pltpu.make_async_remote_copy(src, dst, send_sems.at[3 * a + j], recv_sems.at[3 * a + j],
                                                       device_id=(px, py, c), device_id_type=MESH))
    return copies


def _split_params():
    return pltpu.CompilerParams(has_side_effects=pltpu.SideEffectType.DATAFLOW_SIDE_EFFECTING)


def _ici_start(kind, srcs, land_shapes, after, name):
    n = len(srcs)
    lands = [pltpu.with_memory_space_constraint(lax.empty(s, a.dtype), pltpu.HBM) for s, a in zip(land_shapes, srcs)]

    def body(*refs):
        outs = refs[2 * n + len(after):]
        for cp in _ici_copies(kind, False, refs[:n], refs[n:2 * n], outs[0], outs[1]):
            cp.start()
        outs[-1][...] = jnp.zeros(TOKEN, F32)

    outs = pl.pallas_call(
        body, name=name, in_specs=[HBM] * (2 * n) + [pl.BlockSpec(memory_space=pl.ANY)] * len(after),
        out_specs=[SEM, SEM] + [HBM] * (2 * n) + [pl.BlockSpec(memory_space=pltpu.VMEM)],
        out_shape=[_dma_sems(3 * n)] * 2 + [pltpu.HBM(a.shape, a.dtype) for a in list(srcs) + lands]
        + [jax.ShapeDtypeStruct(TOKEN, F32)],
        input_output_aliases={i: 2 + i for i in range(2 * n)},
        compiler_params=_split_params(),
    )(*_hbm(*srcs), *lands, *after)
    return outs[:-1], outs[-1]


def _ici_wait(kind, handle, after, name):
    n = (len(handle) - 2) // 2

    def body(*refs):
        for cp in _ici_copies(kind, True, refs[:n], refs[n:2 * n], refs[2 * n], refs[2 * n + 1]):
            cp.wait_send()
            cp.wait_recv()

    outs = pl.pallas_call(
        body, name=name, in_specs=[HBM] * (2 * n) + [SEM, SEM] + [pl.BlockSpec(memory_space=pl.ANY)] * len(after),
        out_specs=[HBM] * (2 * n), out_shape=[pltpu.HBM(a.shape, a.dtype) for a in handle[2:]],
        input_output_aliases={i: i for i in range(2 * n)},
        compiler_params=_split_params(),
    )(*handle[2:], handle[0], handle[1], *after)
    return list(outs[n:])


def _sibling_forward(lands, name):
    n = len(lands)

    def body(*refs):
        outs = refs[n:2 * n]
        send_sems, recv_sems = refs[2 * n:]
        x, y, c = _place()

        def copy(a, j, half):
            px, py = _other_chips(x, y)[j]
            blk = outs[a].at[2 * px + py, half]
            return pltpu.make_async_remote_copy(blk, blk, send_sems.at[3 * a + j], recv_sems.at[3 * a + j],
                                                device_id=(x, y, 1 - c), device_id_type=MESH)

        pairs = [(a, j) for a in range(n) for j in range(3)]
        for a, j in pairs:
            copy(a, j, c).start()
        for a, j in pairs:
            copy(a, j, 1 - c).wait_recv()
        for a, j in pairs:
            copy(a, j, c).wait_send()

    return pl.pallas_call(
        body, name=name, in_specs=[HBM] * n, out_specs=[HBM] * n,
        out_shape=[pltpu.HBM(a.shape, a.dtype) for a in lands],
        input_output_aliases={a: a for a in range(n)},
        scratch_shapes=[_dma_sems(3 * n)] * 2,
        compiler_params=_comm_params(),
    )(*_hbm(*lands))


def _sibling_exchange(grads, name):
    n = len(grads)

    def body(*refs):
        ins, outs = refs[:n], refs[n:2 * n]
        send_sems, recv_sems = refs[2 * n:]
        x, y, c = _place()
        copies = []
        for a in range(n):
            for s in range(N_CHIPS):
                cp = pltpu.make_async_remote_copy(ins[a].at[s, 1 - c], outs[a].at[s], send_sems.at[N_CHIPS * a + s],
                                                  recv_sems.at[N_CHIPS * a + s], device_id=(x, y, 1 - c),
                                                  device_id_type=MESH)
                cp.start()
                copies.append(cp)
        for cp in copies:
            cp.wait_recv()
        for cp in copies:
            cp.wait_send()

    return pl.pallas_call(
        body, name=name, in_specs=[HBM] * n, out_specs=[HBM] * n,
        out_shape=[pltpu.HBM((N_CHIPS, g.shape[2], D), F32) for g in grads],
        scratch_shapes=[_dma_sems(N_CHIPS * n)] * 2,
        compiler_params=_comm_params(),
    )(*_hbm(*grads))


def _add_halves(core, grads, recvs, dtypes, name):
    n = len(grads)
    heights = [g.shape[2] for g in grads]

    def body(c_ref, *refs):
        for a in range(n):
            refs[2 * n + a][...] = (refs[2 * a][0] + refs[2 * a + 1][...]).astype(dtypes[a])

    slab = lambda h: pl.BlockSpec((1, h, D), lambda s, c: (s, 0, 0))
    mine = lambda h: pl.BlockSpec((1, 1, h, D), lambda s, c: (s, c[0], 0, 0))
    return pl.pallas_call(
        body, name=name,
        grid_spec=pltpu.PrefetchScalarGridSpec(
            num_scalar_prefetch=1, grid=(N_CHIPS,),
            in_specs=[spec(h) for h in heights for spec in (mine, slab)], out_specs=[slab(h) for h in heights]),
        out_shape=[pltpu.HBM((N_CHIPS, h, D), dt) for h, dt in zip(heights, dtypes)],
        compiler_params=_params(32, dimension_semantics=_seq()),
    )(core, *_hbm(*[a for pair in zip(grads, recvs) for a in pair]))


N_DEVICES = 2 * N_CHIPS
PEER_FLIPS = [(dx, dy, dc) for dx in (0, 1) for dy in (0, 1) for dc in (0, 1)][1:]


def _small_exchange(pack, after):
    n = len(PEER_FLIPS)

    def body(p_ref, *refs):
        out_ref, send_sems, recv_sems = refs[len(after):]
        x, y, c = _place()
        flip = lambda v, d: 1 - v if d else v

        def copy(k, landing):
            px, py, pc = (flip(v, d) for v, d in zip((x, y, c), PEER_FLIPS[k]))
            slab = 4 * px + 2 * py + pc if landing else 4 * x + 2 * y + c
            return pltpu.make_async_remote_copy(p_ref, out_ref.at[slab], send_sems.at[k], recv_sems.at[k],
                                                device_id=(px, py, pc), device_id_type=MESH)

        for k in range(n):
            copy(k, False).start()
        for k in range(n):
            copy(k, True).wait_recv()
        for k in range(n):
            copy(k, False).wait_send()

    return pl.pallas_call(
        body, name="small_exchange", in_specs=[HBM] + [pl.BlockSpec(memory_space=pl.ANY)] * len(after), out_specs=HBM,
        out_shape=pltpu.HBM((N_DEVICES,) + pack.shape, F32),
        scratch_shapes=[_dma_sems(n)] * 2,
        compiler_params=_comm_params(),
    )(*_hbm(pack), *after)


def _sum_chips(slots, firsts, rests):
    n = len(firsts)

    def body(i_ref, *refs):
        for a in range(n):
            first, r1, r2, r3 = refs[4 * a:4 * a + 4]
            refs[4 * n + a][...] = ((first[...].astype(F32) + r1[...].astype(F32)) + r2[...].astype(F32)) + r3[...].astype(F32)

    slab = lambda h, k: pl.BlockSpec((1, h, D), lambda i, ix: (ix[k], 0, 0))
    heights = [f.shape[1] for f in firsts]
    return pl.pallas_call(
        body, name="sum_chips",
        grid_spec=pltpu.PrefetchScalarGridSpec(
            num_scalar_prefetch=1, grid=(1,),
            in_specs=[slab(h, k) for h in heights for k in range(4)], out_specs=[slab(h, 4) for h in heights]),
        out_shape=[pltpu.HBM((2, h, D), F32) for h in heights],
        compiler_params=_params(48, dimension_semantics=_seq()),
    )(slots, *_hbm(*[a for f, r in zip(firsts, rests) for a in (f, r, r, r)]))


def _join_halves(halves):
    n = len(halves)

    def body(*refs):
        outs = refs[n:2 * n]
        send_sems, recv_sems = refs[2 * n:]
        x, y, c = _place()

        def copy(a, slab):
            return pltpu.make_async_remote_copy(outs[a].at[slab], outs[a].at[slab], send_sems.at[a], recv_sems.at[a],
                                                device_id=(x, y, 1 - c), device_id_type=MESH)

        for a in range(n):
            copy(a, c).start()
        for a in range(n):
            copy(a, 1 - c).wait_recv()
        for a in range(n):
            copy(a, c).wait_send()

    return pl.pallas_call(
        body, name="join_halves", in_specs=[HBM] * n, out_specs=[HBM] * n,
        out_shape=[pltpu.HBM(h.shape, F32) for h in halves],
        input_output_aliases={a: a for a in range(n)},
        scratch_shapes=[_dma_sems(n)] * 2,
        compiler_params=_comm_params(),
    )(*_hbm(*halves))


def _chip_partials(grads, wire_dtypes, names):
    core = lax.axis_index("c").astype(jnp.int32).reshape(1)
    recv = _sibling_exchange(grads, "sibling_exchange_" + names[0])
    return list(_add_halves(core, grads, recv, wire_dtypes, "add_halves_" + names[0]))


def _finish_reduce(parts, got):
    x, y, c = _place()
    others = [2 * px + py for px, py in _other_chips(x, y)]
    own_first = jnp.stack([2 * x + y] + others + [c]).astype(jnp.int32)
    return [f.reshape(2 * f.shape[1], D) for f in _join_halves(_sum_chips(own_first, parts, got))]


ADAMW_STEPS = 4


def _adamw(params):
    n = len(params)

    def block(shape):
        rows, cols = shape
        if rows % (8 * ADAMW_STEPS) == 0:
            return pl.BlockSpec((rows // ADAMW_STEPS, cols), lambda i: (i, 0))
        assert cols % (LANE * ADAMW_STEPS) == 0
        return pl.BlockSpec((rows, cols // ADAMW_STEPS), lambda i: (0, i))

    def body(*refs):
        for a in range(n):
            w_ref, g_ref, m_ref, v_ref = refs[4 * a:4 * a + 4]
            outs = refs[4 * n + 3 * a:4 * n + 3 * a + 3]
            outs[0][...], outs[1][...], outs[2][...] = _adamw_math(w_ref[...], g_ref[...], m_ref[...], v_ref[...])

    outs = pl.pallas_call(
        body, name="adamw_matrices", grid=(ADAMW_STEPS,),
        in_specs=[block(p[0].shape) for p in params for _ in range(4)],
        out_specs=[block(p[0].shape) for p in params for _ in range(3)],
        out_shape=[pltpu.HBM(p[0].shape, F32) for p in params for _ in range(3)],
        compiler_params=_params(48, dimension_semantics=_seq()),
    )(*_hbm(*[a for p in params for a in p]))
    return [outs[3 * a:3 * a + 3] for a in range(n)]


def _adamw_math(w, g, m, v):
    nm = ADAM_B1 * m + (1.0 - ADAM_B1) * g
    nv = ADAM_B2 * v + (1.0 - ADAM_B2) * (g * g)
    m_hat = nm / (1.0 - ADAM_B1 ** ADAM_STEP)
    v_hat = nv / (1.0 - ADAM_B2 ** ADAM_STEP)
    return -ADAM_LR * (m_hat / (jnp.sqrt(v_hat) + ADAM_EPS) + ADAM_WD * w), nm, nv


SMALL = (("meta_tokens", (N_META, D // N_CHIPS)), ("ln_in_g", (1, D)), ("ln_in_b", (1, D)), ("b_in", (1, D_IN)),
         ("w_gate_lr2", (GATE_RANK, GLA_HEADS * DK // N_CHIPS)), ("b_gate_lr2", (1, GLA_HEADS * DK)),
         ("attn_sinks", (1, SWA_HEADS)),
         ("gla_norm_g", (1, DV)), ("ln1_g", (1, D)), ("ln1_b", (1, D)), ("ln2_g", (1, D)), ("ln2_b", (1, D)))
ROW_META, ROW_B_IN, ROW_TAIL, ROW_WG2 = 0, 22, 25, 32
ROW_LN = dict(ln_in_g=16, ln_in_b=17, ln1_g=18, ln1_b=19, ln2_g=20, ln2_b=21)
TAIL_BG2, TAIL_SINKS, TAIL_GN, TAIL_LOSS = 0, 256, 256 + SWA_HEADS, 256 + SWA_HEADS + DV


def _adamw_small(place, packs, own, params):
    n = len(SMALL)

    def body(place_ref, packs_ref, own_ref, *refs):
        ins, outs, p_ref = refs[:3 * n], refs[3 * n:-1], refs[-1]
        me, c = place_ref[0], place_ref[1]
        total = jnp.where(me == 0, own_ref[...], packs_ref[0])
        for i in range(1, N_DEVICES):
            total = total + jnp.where(me == i, own_ref[...], packs_ref[i])
        p_ref[...] = total
        outs[4 * n][...] = total[ROW_TAIL:ROW_TAIL + 1, :]

        def mine(width, rows):
            part = lambda s: p_ref[rows, s * width:(s + 1) * width]
            return jnp.where(c == 0, part(0), jnp.where(c == 1, part(1), jnp.where(c == 2, part(2), part(3))))

        tail = lambda lo, width: p_ref[ROW_TAIL:ROW_TAIL + 1, lo:lo + width]
        grads = dict(
            meta_tokens=mine(D // N_CHIPS, slice(ROW_META, ROW_META + N_META)),
            b_in=jnp.concatenate([p_ref[ROW_B_IN:ROW_B_IN + 1, :], p_ref[ROW_B_IN + 1:ROW_B_IN + 2, :],
                                  p_ref[ROW_B_IN + 2:ROW_B_IN + 3, 0:D_IN - 2 * D]], axis=1),
            w_gate_lr2=mine(256 // N_CHIPS, slice(ROW_WG2, ROW_WG2 + 16)),
            b_gate_lr2=tail(TAIL_BG2, 256), attn_sinks=tail(TAIL_SINKS, SWA_HEADS), gla_norm_g=tail(TAIL_GN, DV),
            **{k: p_ref[r:r + 1, :] for k, r in ROW_LN.items()})
        for i, (name, _) in enumerate(SMALL):
            g = grads[name]
            outs[4 * i][...] = g
            outs[4 * i + 1][...], outs[4 * i + 2][...], outs[4 * i + 3][...] = _adamw_math(
                ins[3 * i][...], g, ins[3 * i + 1][...], ins[3 * i + 2][...])

    whole = lambda shape: pl.BlockSpec(shape, lambda i, c: (0,) * len(shape))
    outs = pl.pallas_call(
        body, name="adamw_small",
        grid_spec=pltpu.PrefetchScalarGridSpec(
            num_scalar_prefetch=1, grid=(1,),
            in_specs=[whole(packs.shape), whole(own.shape)] + [whole(s) for _, s in SMALL for _ in range(3)],
            out_specs=[whole(s) for _, s in SMALL for _ in range(4)] + [whole((1, D))],
            scratch_shapes=[pltpu.VMEM(own.shape, F32)]),
        out_shape=[pltpu.HBM(s, F32) for _, s in SMALL for _ in range(4)] + [pltpu.HBM((1, D), F32)],
        compiler_params=_params(16, dimension_semantics=_seq()),
    )(place, *_hbm(packs, own, *[a for p in params for a in p]))
    return [outs[4 * i:4 * i + 4] for i in range(n)], outs[4 * n]


def _small_pack(gr):
    names = ["meta_blk"] + list(ROW_LN) + ["b_in_p", "wg2_p", "bg2", "sinks", "gn", "loss"]
    gate_w = GLA_HEADS * DK

    def body(*refs):
        src, out = dict(zip(names, refs)), refs[-1]
        out[...] = jnp.zeros_like(out)
        out[ROW_META:ROW_META + N_META, :] = src["meta_blk"][META_OFF:CH, :]
        for k, r in ROW_LN.items():
            out[r:r + 1, :] = src[k][...]
        for j in range(-(-D_IN // D)):
            width = min(D, D_IN - j * D)
            out[ROW_B_IN + j:ROW_B_IN + j + 1, 0:width] = src["b_in_p"][:, j * D:j * D + width]
        tail = slice(ROW_TAIL, ROW_TAIL + 1)
        out[tail, TAIL_BG2:TAIL_BG2 + gate_w] = src["bg2"][...]
        out[tail, TAIL_SINKS:TAIL_SINKS + SWA_HEADS] = src["sinks"][:, 0:SWA_HEADS]
        out[tail, TAIL_GN:TAIL_GN + DV] = src["gn"][...]
        out[tail, TAIL_LOSS:TAIL_LOSS + 1] = src["loss"][:, 0:1]
        out[ROW_WG2:ROW_WG2 + GATE_RANK, 0:gate_w] = src["wg2_p"][0:GATE_RANK, :]

    arrays = [gr[k] for k in names]
    return pl.pallas_call(
        body, name="small_pack", grid=(1,),
        in_specs=[_acc(a.shape) for a in arrays], out_specs=_acc((SMALL_ROWS, D)),
        out_shape=pltpu.HBM((SMALL_ROWS, D), F32),
        compiler_params=_params(16, dimension_semantics=_seq()),
    )(*_hbm(*arrays))


BIG = ("w_in", "w_out", "w_g", "w_u", "w_d")


def kernel(x, meta_tokens, ln_in_g, ln_in_b, w_in, b_in, w_gate_lr2, b_gate_lr2, attn_sinks, gla_norm_g, w_out, ln1_g, ln1_b, w_ffn_gate, w_ffn_up, w_ffn_down, ln2_g, ln2_b, loss_target, m_meta_tokens, m_ln_in_g, m_ln_in_b, m_w_in, m_b_in, m_w_gate_lr2, m_b_gate_lr2, m_attn_sinks, m_gla_norm_g, m_w_out, m_ln1_g, m_ln1_b, m_w_ffn_gate, m_w_ffn_up, m_w_ffn_down, m_ln2_g, m_ln2_b, v_meta_tokens, v_ln_in_g, v_ln_in_b, v_w_in, v_b_in, v_w_gate_lr2, v_b_gate_lr2, v_attn_sinks, v_gla_norm_g, v_w_out, v_ln1_g, v_ln1_b, v_w_ffn_gate, v_w_ffn_up, v_w_ffn_down, v_ln2_g, v_ln2_b):
    chip = 2 * lax.axis_index("x") + lax.axis_index("y")

    halves = lambda a: a.reshape(2, a.shape[0] // 2, a.shape[1])
    r_in = SHARD_ROWS["w_in"]
    first = [halves(a) for a in (jnp.pad(w_in[0].T.astype(BF16), ((0, W_IN_WIN - r_in), (0, 0))), meta_tokens,
                                 w_gate_lr2[0])]
    rest = [halves(a) for a in (w_out[0].astype(BF16), w_ffn_gate[0].T.astype(BF16), w_ffn_up[0].T.astype(BF16),
                                w_ffn_down[0].astype(BF16))]
    lands = lambda arrs: [(N_CHIPS,) + a.shape for a in arrs]
    first_handle, first_token = _ici_start("gather", first, lands(first), [], "gather_first_start")
    rest_handle, token = _ici_start("gather", rest, lands(rest), [first_token], "gather_rest_start")

    def fetch(handle, shards, after, name):
        got = _sibling_forward(_ici_wait("gather", handle, after, name + "_wait"), name + "_forward")
        return [lax.dynamic_update_index_in_dim(g, s, chip, axis=0) for g, s in zip(got, shards)]

    def fetch_first(after):
        g_in, g_meta, g_wg2 = fetch(first_handle, first, after, "gather_first")
        w_in_t = jnp.pad(g_in.reshape(N_CHIPS, W_IN_WIN, D)[:, :r_in].reshape(D_IN, D), ((0, D_IN_P - D_IN), (0, 0)))
        meta_full = jnp.concatenate([g_meta[s].reshape(N_META, -1) for s in range(N_CHIPS)], axis=1)
        wg2_full = jnp.concatenate([g_wg2[s].reshape(w_gate_lr2.shape[1], -1) for s in range(N_CHIPS)], axis=1)
        return w_in_t, meta_full, wg2_full

    def fetch_rest(after):
        return [g.reshape(-1, D) for g in fetch(rest_handle, rest, after, "gather_rest")]

    sent = {}

    def ship(key, grads, names):
        parts = _chip_partials([g.reshape(N_CHIPS, 2, -1, D) for g in grads], [BF16] * len(grads), names)
        handle, ship_token = _ici_start("scatter", parts, [p.shape for p in parts], [], "scatter_" + key + "_start")
        sent[key] = (parts, handle)
        return ship_token

    def ship_ffn(g):
        return ship("ffn", [g[k] for k in BIG[1:]], list(BIG[1:]))

    def ship_w_in(dw_in_t):
        win_start = [s * r_in // BF16_ROWS * BF16_ROWS for s in range(N_CHIPS)]
        return ship("w_in", [jnp.stack([dw_in_t[st:st + W_IN_WIN] for st in win_start])], ["w_in"])

    dx, gr = _local_step(
        x[0], loss_target[0], ln_in_g, ln_in_b, b_in[0], b_gate_lr2[0], attn_sinks[0], gla_norm_g[0], ln1_g[0],
        ln1_b[0], ln2_g[0], ln2_b[0], token, fetch_first, fetch_rest, ship_ffn, ship_w_in)
    ffn_got = _ici_wait("scatter", sent["ffn"][1], [dx], "scatter_ffn_wait")
    w_in_got = _ici_wait("scatter", sent["w_in"][1], [dx], "scatter_w_in_wait")

    small_own = _small_pack(gr)
    small_all = _small_exchange(small_own, [w_in_got[0]])
    red = _finish_reduce(sent["w_in"][0] + sent["ffn"][0], w_in_got + ffn_got)

    big_g = dict(zip(BIG, red))
    big_g["w_in"] = lax.dynamic_slice_in_dim(red[0], chip * (r_in % BF16_ROWS), r_in, axis=0)
    grads = dict(w_in=big_g["w_in"].T[None], w_out=big_g["w_out"][None], w_ffn_gate=big_g["w_g"].T[None],
                 w_ffn_up=big_g["w_u"].T[None], w_ffn_down=big_g["w_d"][None])
    weights = dict(meta_tokens=meta_tokens, ln_in_g=ln_in_g, ln_in_b=ln_in_b, w_in=w_in, b_in=b_in,
                   w_gate_lr2=w_gate_lr2, b_gate_lr2=b_gate_lr2, attn_sinks=attn_sinks, gla_norm_g=gla_norm_g,
                   w_out=w_out, ln1_g=ln1_g, ln1_b=ln1_b, w_ffn_gate=w_ffn_gate, w_ffn_up=w_ffn_up,
                   w_ffn_down=w_ffn_down, ln2_g=ln2_g, ln2_b=ln2_b)
    m_in = dict(meta_tokens=m_meta_tokens, ln_in_g=m_ln_in_g, ln_in_b=m_ln_in_b, w_in=m_w_in, b_in=m_b_in,
                w_gate_lr2=m_w_gate_lr2, b_gate_lr2=m_b_gate_lr2, attn_sinks=m_attn_sinks, gla_norm_g=m_gla_norm_g,
                w_out=m_w_out, ln1_g=m_ln1_g, ln1_b=m_ln1_b, w_ffn_gate=m_w_ffn_gate, w_ffn_up=m_w_ffn_up,
                w_ffn_down=m_w_ffn_down, ln2_g=m_ln2_g, ln2_b=m_ln2_b)
    v_in = dict(meta_tokens=v_meta_tokens, ln_in_g=v_ln_in_g, ln_in_b=v_ln_in_b, w_in=v_w_in, b_in=v_b_in,
                w_gate_lr2=v_w_gate_lr2, b_gate_lr2=v_b_gate_lr2, attn_sinks=v_attn_sinks, gla_norm_g=v_gla_norm_g,
                w_out=v_w_out, ln1_g=v_ln1_g, ln1_b=v_ln1_b, w_ffn_gate=v_w_ffn_gate, w_ffn_up=v_w_ffn_up,
                w_ffn_down=v_w_ffn_down, ln2_g=v_ln2_g, ln2_b=v_ln2_b)
    names = list(weights)
    big_names = ("w_in", "w_out", "w_ffn_gate", "w_ffn_up", "w_ffn_down")

    delta, new_m, new_v = {}, {}, {}
    flips = [(lambda a: a.T) if kk in ("w_in", "w_g", "w_u") else (lambda a: a) for kk in BIG]
    updated = _adamw([(flip(weights[k][0]), big_g[kk], flip(m_in[k][0]), flip(v_in[k][0]))
                      for k, kk, flip in zip(big_names, BIG, flips)])
    for k, flip, results in zip(big_names, flips, updated):
        delta[k], new_m[k], new_v[k] = (flip(t)[None] for t in results)
    small_in = [tuple(src[k].reshape(shape) for src in (weights, m_in, v_in)) for k, shape in SMALL]
    place = jnp.stack([2 * chip + lax.axis_index("c"), chip]).astype(jnp.int32)
    small_out, tail_row = _adamw_small(place, small_all, small_own, small_in)
    for (k, _), results in zip(SMALL, small_out):
        grads[k], delta[k], new_m[k], new_v[k] = (r.reshape(weights[k].shape) for r in results)

    return (tail_row[0, TAIL_LOSS], dx[None], *[grads[k] for k in names], *[delta[k] for k in names], *[new_m[k] for k in names],
            *[new_v[k] for k in names])
```

```python
import jax
import jax.numpy as jnp
from jax import lax
from jax.experimental import pallas as pl
from jax.experimental.pallas import tpu as pltpu

F32 = jnp.float32
BF16 = jnp.bfloat16
MESH = pl.DeviceIdType.MESH

D = 1024
SEQ = 4096
N_META = 16
SWA_HEADS, SWA_KV_HEADS, DH = 8, 2, 64
WINDOW = 128
GLA_HEADS, DK, DV = 4, 64, 128
GLA_TAU = 16.0
CH = 64
D_FF = 2816
D_IN = 2320
LN_EPS = 1e-5
RMS_EPS = 1e-6
ALPHA = 2.0 ** 0.25
NEG = -1e30
ADAM_LR, ADAM_B1, ADAM_B2, ADAM_EPS, ADAM_WD, ADAM_STEP = 0.001, 0.9, 0.999, 1e-8, 0.01, 10
O_QS, O_KS, O_VS, O_QG, O_KG, O_VG, O_RG, O_LR = 0, 512, 640, 768, 1024, 1280, 1792, 2304

LANE = 128
BLK = WINDOW
GATE_RANK = 16
D_IN_P = D_IN + LANE - GATE_RANK
META_OFF = CH - N_META
HEAD_POS = (0, 4, 1, 5, 2, 6, 3, 7)
LN_ROWS = 512
TOKEN = (8, LANE)
N_CHIPS = 4
SHARD_ROWS = dict(w_in=D_IN // N_CHIPS, w_out=D // N_CHIPS, w_g=D_FF // N_CHIPS, w_u=D_FF // N_CHIPS,
                  w_d=D_FF // N_CHIPS)
SMALL_ROWS = 48
BF16_ROWS = 16
W_IN_WIN = -(-SHARD_ROWS["w_in"] // (2 * BF16_ROWS)) * 2 * BF16_ROWS
VMEM_CAP_MB = 64
VMEM_SPARE_MB = 6


def _lp():
    return SEQ + BLK


def _row_tile(cap):
    lp = _lp()
    return max(t for t in range(16, cap + 1, 16) if lp % t == 0)


def _params(vmem_mb, **kw):
    assert vmem_mb <= VMEM_CAP_MB - VMEM_SPARE_MB
    return pltpu.CompilerParams(vmem_limit_bytes=vmem_mb << 20, **kw)


def _seq(n=1):
    return ("arbitrary",) * n


def _const(shape):
    return pl.BlockSpec(shape, lambda *_: (0,) * len(shape), pipeline_mode=pl.Buffered(1))


def _acc(shape):
    return pl.BlockSpec(shape, lambda *_: (0,) * len(shape))


def _rows(tm, width):
    return pl.BlockSpec((tm, width), lambda i: (i, 0))


def _dot(a, b):
    return jnp.dot(a.astype(BF16), b.astype(BF16), preferred_element_type=F32)


def _dot_nt(a, b):
    return lax.dot_general(a.astype(BF16), b.astype(BF16), (((1,), (1,)), ((), ())), preferred_element_type=F32)


def _dot_tn(a, b):
    return lax.dot_general(a.astype(BF16), b.astype(BF16), (((0,), (0,)), ((), ())), preferred_element_type=F32)


def _dot_exact(a, b):
    return jnp.dot(a, b, precision=lax.Precision.HIGHEST, preferred_element_type=F32)


def _ln_stats(x):
    mu = jnp.mean(x, axis=-1, keepdims=True)
    xc = x - mu
    rstd = lax.rsqrt(jnp.mean(xc * xc, axis=-1, keepdims=True) + LN_EPS)
    return xc * rstd, rstd


def _ln_bwd(dy, xhat, rstd, g):
    dxh = dy * g
    return rstd * (dxh - jnp.mean(dxh, axis=-1, keepdims=True) - xhat * jnp.mean(dxh * xhat, axis=-1, keepdims=True))


def _sigmoid(x):
    return 1.0 / (1.0 + jnp.exp(-x))


def _iota(shape, dim):
    return lax.broadcasted_iota(jnp.int32, shape, dim)


def _hbm(*arrays):
    return tuple(pltpu.with_memory_space_constraint(a, pltpu.HBM) for a in arrays)


def _ln_in_fwd_real(x, g, b, token):
    tr = min(LN_ROWS, SEQ)

    def body(x_ref, g_ref, b_ref, token_ref, h_ref):
        xhat, _ = _ln_stats(x_ref[...])
        h_ref[...] = xhat * g_ref[...] + b_ref[...]

    return pl.pallas_call(
        body, name="ln_in_fwd", grid=(SEQ // tr,),
        in_specs=[_rows(tr, D), _const((1, D)), _const((1, D)), _const(TOKEN)],
        out_specs=_rows(tr, D),
        out_shape=pltpu.HBM((_lp(), D), F32),
        compiler_params=_params(32, dimension_semantics=_seq()),
    )(*_hbm(x, g, b), token)


def _ln_in_fwd_meta(h_real, meta_ext, g, b):
    def meta_body(m_ref, g_ref, b_ref, real_ref, h_ref):
        xhat, _ = _ln_stats(m_ref[...])
        h_ref[...] = xhat * g_ref[...] + b_ref[...]

    return pl.pallas_call(
        meta_body, name="ln_in_fwd_meta", grid=(1,),
        in_specs=[_const((BLK, D)), _const((1, D)), _const((1, D)), pl.BlockSpec(memory_space=pl.ANY)],
        out_specs=pl.BlockSpec((BLK, D), lambda i: (SEQ // BLK, 0)),
        out_shape=pltpu.HBM((_lp(), D), F32),
        input_output_aliases={3: 0},
        compiler_params=_params(16, dimension_semantics=_seq()),
    )(*_hbm(meta_ext, g, b, h_real))


def _in_proj(h0, w_in_t, b_in_p, wg2_p, bg2):
    tm = _row_tile(384)
    lp = _lp()
    widths = (512, 128, 128, 256, 256, 512, 512, 128)
    offs = (O_QS, O_KS, O_VS, O_QG, O_KG, O_VG, O_RG, O_LR)

    def body(h_ref, w_ref, b_ref, wg2_ref, bg2_ref, *outs):
        proj = _dot_nt(h_ref[...], w_ref[...]) + b_ref[...]
        for pos, h in enumerate(HEAD_POS):
            outs[0][:, pos * DH:(pos + 1) * DH] = proj[:, O_QS + h * DH:O_QS + (h + 1) * DH]
        for o_ref, off, wd in zip(outs[1:8], offs[1:], widths[1:]):
            o_ref[...] = proj[:, off:off + wd]
        outs[8][...] = _dot(proj[:, O_LR:O_LR + LANE], wg2_ref[...]) + bg2_ref[...]

    return pl.pallas_call(
        body, name="in_proj", grid=(lp // tm,),
        in_specs=[_rows(tm, D), _const((D_IN_P, D)), _const((1, D_IN_P)), _const((LANE, 256)), _const((1, 256))],
        out_specs=[_rows(tm, w) for w in widths] + [_rows(tm, 256)],
        out_shape=[pltpu.HBM((lp, w), F32) for w in widths] + [pltpu.HBM((lp, 256), F32)],
        compiler_params=_params(40, dimension_semantics=_seq()),
    )(*_hbm(h0, w_in_t, b_in_p, wg2_p, bg2))


def _swa_masks(n):
    nb = SEQ // BLK
    is_meta = n == nb
    ri = _iota((BLK, BLK), 0)
    cj = _iota((BLK, BLK), 1)
    meta_col = ((cj >= META_OFF) & (cj < CH)).astype(jnp.int32)
    meta_q = meta_col * ((cj <= ri) & (ri < CH)).astype(jnp.int32)
    valid_m = jnp.where(is_meta, meta_q, meta_col) > 0
    dist_m = jnp.where(is_meta, ri - cj, n * BLK + ri + CH - cj).astype(F32)
    valid_p = jnp.where((n >= 1) & (n < nb), (cj > ri).astype(jnp.int32), 0) > 0
    dist_p = (ri + BLK - cj).astype(F32)
    valid_c = jnp.where(n < nb, (cj <= ri).astype(jnp.int32), 0) > 0
    dist_c = (ri - cj).astype(F32)
    return (dist_m, dist_p, dist_c), (valid_m, valid_p, valid_c)


def _swa_bias(n):
    dists, valids = _swa_masks(n)
    return (jnp.concatenate([-d for d in dists], axis=1),
            jnp.concatenate([jnp.where(v, 0.0, NEG) for v in valids], axis=1))


def _swa_half(ref, pos, scale=1.0):
    col = ref[:, (pos // 2) * LANE:(pos // 2 + 1) * LANE]
    lane = _iota((BLK, LANE), 1)
    mine = lane < DH if pos % 2 == 0 else lane >= DH
    return jnp.where(mine, col * scale, 0.0).astype(BF16)


def _swa_merge(even, odd):
    return jnp.where(_iota((BLK, LANE), 1) < DH, even, odd)


def _swa_softmax(t, sink):
    m = jnp.maximum(jnp.max(t, axis=-1, keepdims=True), sink)
    e = jnp.exp(t - m)
    e_sink = jnp.exp(sink - m)
    inv = 1.0 / (jnp.sum(e, axis=-1, keepdims=True) + e_sink)
    return e * inv, e_sink * inv


def _swa_kv_specs(width):
    nb = SEQ // BLK
    return [pl.BlockSpec((BLK, width), lambda n: (nb, 0)),
            pl.BlockSpec((BLK, width), lambda n: (jnp.clip(n - 1, 0, nb - 1), 0)),
            pl.BlockSpec((BLK, width), lambda n: (jnp.minimum(n, nb), 0))]


def _swa_fwd(sinks, qs, ks, vs):
    nb = SEQ // BLK
    heads = range(SWA_HEADS)

    def body(sink_ref, q_ref, km_ref, kp_ref, kc_ref, vm_ref, vp_ref, vc_ref, o_ref):
        negdist, maskbias = _swa_bias(pl.program_id(0))
        k_all = jnp.concatenate([km_ref[...], kp_ref[...], kc_ref[...]], axis=0).astype(BF16)
        v_all = jnp.concatenate([vm_ref[...], vp_ref[...], vc_ref[...]], axis=0).astype(BF16)
        q = [_swa_half(q_ref, pos, DH ** -0.5) for pos in heads]
        t = [_dot_nt(q[pos], k_all) + (2.0 ** -(HEAD_POS[pos] + 1) * negdist + maskbias) for pos in heads]
        p = [_swa_softmax(t[pos], sink_ref[HEAD_POS[pos]])[0].astype(BF16) for pos in heads]
        o = [_dot(p[pos], v_all) for pos in heads]
        for col in range(SWA_HEADS // 2):
            o_ref[:, col * LANE:(col + 1) * LANE] = _swa_merge(o[2 * col], o[2 * col + 1])

    kvw = SWA_KV_HEADS * DH
    return pl.pallas_call(
        body, name="swa_fwd", grid=(nb + 1,),
        in_specs=[pl.BlockSpec(memory_space=pltpu.SMEM), _rows(BLK, SWA_HEADS * DH)] + _swa_kv_specs(kvw) + _swa_kv_specs(kvw),
        out_specs=_rows(BLK, SWA_HEADS * DH),
        out_shape=pltpu.HBM((_lp(), SWA_HEADS * DH), F32),
        compiler_params=_params(16, dimension_semantics=_seq()),
    )(sinks, *_hbm(qs, ks, ks, ks, vs, vs, vs))


GLA_PER_STEP = BLK // CH


def _gla_block(s):
    nb = SEQ // BLK
    return jnp.where(s == 0, nb, s - 1)


def _gla_rowmask(s):
    ri = _iota((BLK, 1), 0)
    m = jnp.where(s == 0, ((ri >= META_OFF) & (ri < CH)).astype(jnp.int32), 1)
    return (m > 0).astype(F32) + jnp.zeros((BLK, 1), F32)


def _gla_chunk_masks():
    r, c = _iota((BLK, BLK), 0), _iota((BLK, BLK), 1)
    same = ((r < CH) & (c < CH)) | ((r >= CH) & (c >= CH))
    return same & (r >= c), same & (r <= c), same


def _gla_decay(z, rmask):
    log_g = (jnp.minimum(z, 0.0) - jnp.log1p(jnp.exp(-jnp.abs(z)))) * (rmask / GLA_TAU)
    lower, _, same = _gla_chunk_masks()
    return _dot_exact(lower.astype(F32), log_g), _dot_exact(same.astype(F32), log_g)


def _gla_slices(c, h):
    return slice(c * CH, (c + 1) * CH), slice(h * DK, (h + 1) * DK), slice(h * DV, (h + 1) * DV)


def _gla_fwd(qg, kg, vg, z):
    steps = SEQ // BLK + 1
    kw, vw = GLA_HEADS * DK, GLA_HEADS * DV
    pairs = [(c, h) for c in range(GLA_PER_STEP) for h in range(GLA_HEADS)]

    def body(q_ref, k_ref, v_ref, z_ref, o_ref, st_ref, st):
        s = pl.program_id(0)

        @pl.when(s == 0)
        def _():
            st[...] = jnp.zeros_like(st)

        rmask = _gla_rowmask(s)
        b, b_last = _gla_decay(z_ref[...], rmask)
        q = q_ref[...] * (rmask * DK ** -0.5)
        k = k_ref[...] * rmask
        v = v_ref[...] * rmask
        qe = q * jnp.exp(b)
        ke = k * jnp.exp(-b)
        kd = k * jnp.exp(b_last - b)
        e_last = jnp.exp(b_last)
        causal = _iota((CH, CH), 0) >= _iota((CH, CH), 1)
        a, upd, intra = {}, {}, {}
        for c, h in pairs:
            rows, ks, vs_ = _gla_slices(c, h)
            a[c, h] = jnp.where(causal, _dot_nt(qe[rows, ks], ke[rows, ks]), 0.0)
            upd[c, h] = _dot_tn(v[rows, vs_], kd[rows, ks])
        for c, h in pairs:
            rows, ks, vs_ = _gla_slices(c, h)
            intra[c, h] = _dot(a[c, h], v[rows, vs_])
        state = st[...]
        for c in range(GLA_PER_STEP):
            st_ref[0, c] = state
            for h in range(GLA_HEADS):
                rows, ks, vs_ = _gla_slices(c, h)
                o_ref[rows, vs_] = intra[c, h] + _dot_nt(qe[rows, ks], state[:, ks])
            state = state * e_last[c * CH:c * CH + 1] + jnp.concatenate([upd[c, h] for h in range(GLA_HEADS)], axis=1)
        st[...] = state

    blk = lambda w: pl.BlockSpec((BLK, w), lambda s: (_gla_block(s), 0))
    return pl.pallas_call(
        body, name="gla_fwd", grid=(steps,),
        in_specs=[blk(kw), blk(kw), blk(vw), blk(kw)],
        out_specs=[blk(vw), pl.BlockSpec((1, GLA_PER_STEP, DV, kw), lambda s: (s, 0, 0, 0))],
        out_shape=[pltpu.HBM((_lp(), vw), F32), pltpu.HBM((steps, GLA_PER_STEP, DV, kw), F32)],
        scratch_shapes=[pltpu.VMEM((DV, kw), F32)],
        compiler_params=_params(16, dimension_semantics=_seq()),
    )(*_hbm(qg, kg, vg, z))


def _post_mix(o_s, o_gla, r_g, h0, gn4, w_out, g1, b1):
    tm = _row_tile(384)
    lp = _lp()

    def body(os_ref, og_ref, r_ref, h0_ref, gn_ref, w_ref, g_ref, b_ref, o_ref, pre_ref, h1_ref):
        for pos, h in enumerate(HEAD_POS):
            o_ref[:, h * DH:(h + 1) * DH] = os_ref[:, pos * DH:(pos + 1) * DH].astype(BF16)
        for h in range(GLA_HEADS):
            hs = slice(h * DV, (h + 1) * DV)
            xg = og_ref[:, hs]
            n = xg * lax.rsqrt(jnp.mean(xg * xg, axis=-1, keepdims=True) + RMS_EPS) * gn_ref[...]
            r = r_ref[:, hs]
            o_ref[:, 512 + h * DV:512 + (h + 1) * DV] = (n * (r * _sigmoid(r))).astype(BF16)
        pre = ALPHA * h0_ref[...] + _dot(o_ref[...], w_ref[...])
        pre_ref[...] = pre
        xhat, _ = _ln_stats(pre)
        h1_ref[...] = xhat * g_ref[...] + b_ref[...]

    return pl.pallas_call(
        body, name="post_mix", grid=(lp // tm,),
        in_specs=[_rows(tm, 512), _rows(tm, 512), _rows(tm, 512), _rows(tm, D), _const((1, DV)), _const((D, D)),
                  _const((1, D)), _const((1, D))],
        out_specs=[_rows(tm, D), _rows(tm, D), _rows(tm, D)],
        out_shape=[pltpu.HBM((lp, D), BF16), pltpu.HBM((lp, D), F32),
                   pltpu.HBM((lp, D), F32)],
        compiler_params=_params(32, dimension_semantics=_seq()),
    )(*_hbm(o_s, o_gla, r_g, h0, gn4, w_out, g1, b1))


def _ffn_fwd_loss_bwd(h1, wg_t, wu_t, wd, target, g2, b2):
    lp = _lp()
    tm = max(t for t in range(BLK, 384 + 1, BLK) if lp % t == 0)
    steps = lp // tm
    last_blk = SEQ // BLK - 1
    half = D_FF // 2
    n_t = tm // BLK

    def body(*refs):
        h_ref, wg_ref, wu_ref, wd_ref = refs[:4]
        t_refs = refs[4:4 + n_t]
        g2_ref, b2_ref, a_ref, dgate_ref, dup_ref, dp_ref, loss_ref, dg_ref, db_ref, g_s, u_s, acc = refs[4 + n_t:]
        i = pl.program_id(0)

        @pl.when(i == 0)
        def _():
            acc[...] = jnp.zeros_like(acc)
            dg_ref[...] = jnp.zeros_like(dg_ref)
            db_ref[...] = jnp.zeros_like(db_ref)

        h = h_ref[...]
        hb = h.astype(BF16)
        pre = ALPHA * h
        for j in range(2):
            cols = slice(j * half, (j + 1) * half)
            g = _dot_nt(hb, wg_ref[cols, :])
            u = _dot_nt(hb, wu_ref[cols, :])
            g_s[:, cols] = g
            u_s[:, cols] = u
            pre = pre + _dot(g * _sigmoid(g) * u, wd_ref[cols, :])
        xhat, rstd = _ln_stats(pre)
        real = i * tm + _iota((tm, 1), 0) < SEQ
        target_rows = jnp.concatenate([t[...] for t in t_refs], axis=0)
        diff = jnp.where(real, xhat * g2_ref[...] + b2_ref[...] - target_rows, 0.0)
        acc[...] += jnp.sum(diff * diff, axis=0, keepdims=True)
        dy = diff * (1.0 / D)
        dpre = _ln_bwd(dy, xhat, rstd, g2_ref[...])
        dp_ref[...] = dpre
        dg_ref[...] += jnp.sum(dy * xhat, axis=0, keepdims=True)
        db_ref[...] += jnp.sum(dy, axis=0, keepdims=True)
        dpb = dpre.astype(BF16)
        for j in range(2):
            cols = slice(j * half, (j + 1) * half)
            g, u = g_s[:, cols], u_s[:, cols]
            sg = _sigmoid(g)
            silu = g * sg
            da = _dot_nt(dpb, wd_ref[cols, :])
            a_ref[:, cols] = (silu * u).astype(BF16)
            dgate_ref[:, cols] = (da * u * (sg * (1.0 + g * (1.0 - sg)))).astype(BF16)
            dup_ref[:, cols] = (da * silu).astype(BF16)

        @pl.when(i == steps - 1)
        def _():
            loss_ref[...] = jnp.zeros_like(loss_ref) + (0.5 / D) * jnp.sum(acc[...], axis=1, keepdims=True)

    t_spec = lambda k: pl.BlockSpec((BLK, D), lambda i: (jnp.minimum(i * n_t + k, last_blk), 0))
    return pl.pallas_call(
        body, name="ffn_fwd_loss_bwd", grid=(steps,),
        in_specs=[_rows(tm, D), _const((D_FF, D)), _const((D_FF, D)), _const((D_FF, D))]
        + [t_spec(k) for k in range(n_t)] + [_const((1, D)), _const((1, D))],
        out_specs=[_rows(tm, D_FF), _rows(tm, D_FF), _rows(tm, D_FF), _rows(tm, D), _acc((1, LANE)), _acc((1, D)),
                   _acc((1, D))],
        out_shape=[pltpu.HBM((lp, D_FF), BF16)] * 3 + [pltpu.HBM((lp, D), F32), pltpu.HBM((1, LANE), F32),
                                                         pltpu.HBM((1, D), F32), pltpu.HBM((1, D), F32)],
        scratch_shapes=[pltpu.VMEM((tm, D_FF), F32), pltpu.VMEM((tm, D_FF), F32), pltpu.VMEM((1, D), F32)],
        compiler_params=_params(58, dimension_semantics=_seq()),
    )(*_hbm(h1, wg_t, wu_t, wd, *[target] * n_t, g2, b2))


def _ffn_out_bwd(dpre2, dgate, dup, pre1, wg_t, wu_t, g1, w_out, o_gla, r_g, gn4):
    tm = _row_tile(384)
    lp = _lp()

    def body(dp_ref, dg_ref, du_ref, p1_ref, wg_ref, wu_ref, g1_ref, w_ref, og_ref, r_ref, gn_ref,
             dp1_ref, dg1_ref, db1_ref, dos_ref, dog_ref, dr_ref, dgn_ref):
        @pl.when(pl.program_id(0) == 0)
        def _():
            for acc_ref in (dg1_ref, db1_ref, dgn_ref):
                acc_ref[...] = jnp.zeros_like(acc_ref)

        dh1 = ALPHA * dp_ref[...] + _dot(dg_ref[...], wg_ref[...]) + _dot(du_ref[...], wu_ref[...])
        xhat, rstd1 = _ln_stats(p1_ref[...])
        dpre1 = _ln_bwd(dh1, xhat, rstd1, g1_ref[...])
        dp1_ref[...] = dpre1
        dg1_ref[...] += jnp.sum(dh1 * xhat, axis=0, keepdims=True)
        db1_ref[...] += jnp.sum(dh1, axis=0, keepdims=True)

        do = _dot_nt(dpre1, w_ref[...])
        for pos, h in enumerate(HEAD_POS):
            dos_ref[:, pos * DH:(pos + 1) * DH] = do[:, h * DH:(h + 1) * DH]
        gn = gn_ref[...]
        for h in range(GLA_HEADS):
            hs = slice(h * DV, (h + 1) * DV)
            xg = og_ref[:, hs]
            rstd = lax.rsqrt(jnp.mean(xg * xg, axis=-1, keepdims=True) + RMS_EPS)
            nx = xg * rstd
            r = r_ref[:, hs]
            sr = _sigmoid(r)
            d_o = do[:, 512 + h * DV:512 + (h + 1) * DV]
            dr_ref[:, hs] = d_o * (nx * gn) * (sr * (1.0 + r * (1.0 - sr)))
            dn = d_o * (r * sr)
            dgn_ref[...] += jnp.sum(dn * nx, axis=0, keepdims=True)
            dnx = dn * gn
            dog_ref[:, hs] = rstd * (dnx - nx * jnp.mean(dnx * nx, axis=-1, keepdims=True))

    return pl.pallas_call(
        body, name="ffn_out_bwd", grid=(lp // tm,),
        in_specs=[_rows(tm, D), _rows(tm, D_FF), _rows(tm, D_FF), _rows(tm, D), _const((D_FF, D)), _const((D_FF, D)),
                  _const((1, D)), _const((D, D)), _rows(tm, 512), _rows(tm, 512), _const((1, DV))],
        out_specs=[_rows(tm, D), _acc((1, D)), _acc((1, D)), _rows(tm, 512), _rows(tm, 512), _rows(tm, 512),
                   _acc((1, DV))],
        out_shape=[pltpu.HBM((lp, D), F32), pltpu.HBM((1, D), F32), pltpu.HBM((1, D), F32)]
        + [pltpu.HBM((lp, 512), F32)] * 3 + [pltpu.HBM((1, DV), F32)],
        compiler_params=_params(48, dimension_semantics=_seq()),
    )(*_hbm(dpre2, dgate, dup, pre1, wg_t, wu_t, g1, w_out, o_gla, r_g, gn4))


def _atb(a, b, name, token=None):
    lp = _lp()
    tm = _row_tile(1408)
    n, w = a.shape[1], b.shape[1]
    bw = 512 if n * w * 4 > (4 << 20) else w
    tokens = [] if token is None else [token]

    def body(a_ref, b_ref, *rest):
        o_ref = rest[-1]

        @pl.when(pl.program_id(1) == 0)
        def _():
            o_ref[...] = jnp.zeros_like(o_ref)

        o_ref[...] += _dot_tn(a_ref[...], b_ref[...])

    return pl.pallas_call(
        body, name=name, grid=(w // bw, lp // tm),
        in_specs=[pl.BlockSpec((tm, n), lambda j, k: (k, 0)), pl.BlockSpec((tm, bw), lambda j, k: (k, j))]
        + [_const(TOKEN)] * len(tokens),
        out_specs=pl.BlockSpec((n, bw), lambda j, k: (0, j)),
        out_shape=pltpu.HBM((n, w), F32),
        compiler_params=_params(48, dimension_semantics=_seq(2)),
    )(*_hbm(a, b), *tokens)


def _gla_bwd(qg, kg, vg, z, do_gla, st_all, token):
    steps = SEQ // BLK + 1
    kw, vw = GLA_HEADS * DK, GLA_HEADS * DV
    pairs = [(c, h) for c in range(GLA_PER_STEP) for h in range(GLA_HEADS)]
    heads = range(GLA_HEADS)

    def body(q_ref, k_ref, v_ref, z_ref, do_ref, st_ref, token_ref, dq_ref, dk_ref, dv_ref, dz_ref, dst):
        @pl.when(pl.program_id(0) == 0)
        def _():
            dst[...] = jnp.zeros_like(dst)

        rmask = _gla_rowmask(steps - 1 - pl.program_id(0))
        zz = z_ref[...]
        b, b_last = _gla_decay(zz, rmask)
        e_b, e_nb, e_kd, e_last = jnp.exp(b), jnp.exp(-b), jnp.exp(b_last - b), jnp.exp(b_last)
        q = q_ref[...] * (rmask * DK ** -0.5)
        k = k_ref[...] * rmask
        v = v_ref[...] * rmask
        qe, ke, kd = q * e_b, k * e_nb, k * e_kd
        d_o = do_ref[...]
        causal = _iota((CH, CH), 0) >= _iota((CH, CH), 1)
        a, da, dqe, dke, dv_intra, carry = {}, {}, {}, {}, {}, {}
        for c, h in pairs:
            rows, ks, vs_ = _gla_slices(c, h)
            a[c, h] = jnp.where(causal, _dot_nt(qe[rows, ks], ke[rows, ks]), 0.0)
            da[c, h] = jnp.where(causal, _dot_nt(d_o[rows, vs_], v[rows, vs_]), 0.0)
            carry[c, h] = _dot_tn(d_o[rows, vs_], qe[rows, ks])
        for c, h in pairs:
            rows, ks, vs_ = _gla_slices(c, h)
            dqe[c, h] = _dot(d_o[rows, vs_], st_ref[0, c][:, ks]) + _dot(da[c, h], ke[rows, ks])
            dke[c, h] = _dot_tn(da[c, h], qe[rows, ks])
            dv_intra[c, h] = _dot_tn(a[c, h], d_o[rows, vs_])
        dstate = dst[...]
        dkd, db_decay = {}, {}
        for c in reversed(range(GLA_PER_STEP)):
            for h in heads:
                rows, ks, vs_ = _gla_slices(c, h)
                dkd[c, h] = _dot(v[rows, vs_], dstate[:, ks])
                dv_ref[rows, vs_] = dv_intra[c, h] + _dot_nt(kd[rows, ks], dstate[:, ks])
            chunk_last = e_last[c * CH:c * CH + 1]
            db_decay[c] = jnp.sum(dstate * st_ref[0, c], axis=0, keepdims=True) * chunk_last
            dstate = dstate * chunk_last + jnp.concatenate([carry[c, h] for h in heads], axis=1)
        dst[...] = dstate
        rows_of = lambda parts: jnp.concatenate(
            [jnp.concatenate([parts[c, h] for h in heads], axis=1) for c in range(GLA_PER_STEP)], axis=0)
        dqe_all, dke_all, dkd_all = rows_of(dqe), rows_of(dke), rows_of(dkd)
        dq_ref[...] = dqe_all * e_b * (rmask * DK ** -0.5)
        dk_ref[...] = (dke_all * e_nb + dkd_all * e_kd) * rmask
        dkd_kd = dkd_all * kd
        db = dqe_all * qe - dke_all * ke - dkd_kd
        _, upper, same = _gla_chunk_masks()
        decay_rows = jnp.concatenate([jnp.broadcast_to(db_decay[c], (CH, kw)) for c in range(GLA_PER_STEP)], axis=0)
        dlog_g = _dot_exact(upper.astype(F32), db) + _dot_exact(same.astype(F32), dkd_kd) + decay_rows
        dz_ref[...] = dlog_g * (rmask / GLA_TAU) * _sigmoid(-zz)

    blk = lambda w: pl.BlockSpec((BLK, w), lambda s: (_gla_block(steps - 1 - s), 0))
    return pl.pallas_call(
        body, name="gla_bwd", grid=(steps,),
        in_specs=[blk(kw), blk(kw), blk(vw), blk(kw), blk(vw),
                  pl.BlockSpec((1, GLA_PER_STEP, DV, kw), lambda s: (steps - 1 - s, 0, 0, 0)), _const(TOKEN)],
        out_specs=[blk(kw), blk(kw), blk(vw), blk(kw)],
        out_shape=[pltpu.HBM((_lp(), kw), F32), pltpu.HBM((_lp(), kw), F32),
                   pltpu.HBM((_lp(), vw), F32), pltpu.HBM((_lp(), kw), F32)],
        scratch_shapes=[pltpu.VMEM((DV, kw), F32)],
        compiler_params=_params(16, dimension_semantics=_seq()),
    )(*_hbm(qg, kg, vg, z, do_gla, st_all), token)


def _swa_bwd(sinks, qs, ks, vs, do_s, token):
    nb = SEQ // BLK
    kvw = SWA_KV_HEADS * DH
    scale = DH ** -0.5
    heads = range(SWA_HEADS)

    def body(sink_ref, q_ref, km_ref, kp_ref, kc_ref, vm_ref, vp_ref, vc_ref, do_ref, token_ref,
             dq_ref, dk_ref, dv_ref, dsink_ref, carry_k, carry_v, meta_k, meta_v):
        n = pl.program_id(0)

        @pl.when(n == 0)
        def _():
            for r in (carry_k, carry_v, meta_k, meta_v):
                r[...] = jnp.zeros_like(r)
            dsink_ref[...] = jnp.zeros_like(dsink_ref)

        @pl.when(n <= nb)
        def _():
            negdist, maskbias = _swa_bias(n)
            lane = _iota((1, LANE), 1)
            k_all = jnp.concatenate([km_ref[...], kp_ref[...], kc_ref[...]], axis=0).astype(BF16)
            v_all = jnp.concatenate([vm_ref[...], vp_ref[...], vc_ref[...]], axis=0).astype(BF16)
            q = [_swa_half(q_ref, pos, scale) for pos in heads]
            d_o = [_swa_half(do_ref, pos) for pos in heads]
            t = [_dot_nt(q[pos], k_all) + (2.0 ** -(HEAD_POS[pos] + 1) * negdist + maskbias) for pos in heads]
            dp = [_dot_nt(d_o[pos], v_all) for pos in heads]
            soft = [_swa_softmax(t[pos], sink_ref[HEAD_POS[pos]]) for pos in heads]
            p = [s[0] for s in soft]
            delta = [jnp.sum(p[pos] * dp[pos], axis=-1, keepdims=True) for pos in heads]
            ds = [(p[pos] * (dp[pos] - delta[pos])).astype(BF16) for pos in heads]
            dq = [_dot(ds[pos], k_all) for pos in heads]
            for col in range(SWA_HEADS // 2):
                dq_ref[:, col * LANE:(col + 1) * LANE] = scale * _swa_merge(dq[2 * col], dq[2 * col + 1])
            dsink = jnp.zeros((1, LANE), F32)
            for pos in heads:
                dsink = dsink + jnp.where(lane == HEAD_POS[pos],
                                          -jnp.sum(soft[pos][1] * delta[pos], axis=0, keepdims=True), 0.0)
            dsink_ref[...] += dsink
            dk3 = _dot_tn(jnp.concatenate(q, axis=0), jnp.concatenate(ds, axis=0)).T
            dv3 = _dot_tn(jnp.concatenate(d_o, axis=0), jnp.concatenate([x.astype(BF16) for x in p], axis=0)).T
            meta_k[...] += dk3[0:BLK]
            meta_v[...] += dv3[0:BLK]
            dk_ref[...] = carry_k[...] + dk3[BLK:2 * BLK]
            dv_ref[...] = carry_v[...] + dv3[BLK:2 * BLK]
            carry_k[...] = dk3[2 * BLK:3 * BLK]
            carry_v[...] = dv3[2 * BLK:3 * BLK]

        @pl.when(n == nb + 1)
        def _():
            dk_ref[...] = meta_k[...]
            dv_ref[...] = meta_v[...]

    kv_out = pl.BlockSpec((BLK, kvw), lambda n: (jnp.where(n == nb + 1, nb, jnp.clip(n - 1, 0, nb - 1)), 0))
    qblk = pl.BlockSpec((BLK, SWA_HEADS * DH), lambda n: (jnp.minimum(n, nb), 0))
    return pl.pallas_call(
        body, name="swa_bwd", grid=(nb + 2,),
        in_specs=[pl.BlockSpec(memory_space=pltpu.SMEM), qblk] + _swa_kv_specs(kvw) + _swa_kv_specs(kvw)
        + [qblk, _const(TOKEN)],
        out_specs=[qblk, kv_out, kv_out, _acc((1, LANE))],
        out_shape=[pltpu.HBM((_lp(), SWA_HEADS * DH), F32), pltpu.HBM((_lp(), kvw), F32),
                   pltpu.HBM((_lp(), kvw), F32), pltpu.HBM((1, LANE), F32)],
        scratch_shapes=[pltpu.VMEM((BLK, kvw), F32)] * 4,
        compiler_params=_params(16, dimension_semantics=_seq()),
    )(sinks, *_hbm(qs, ks, ks, ks, vs, vs, vs, do_s), token)


def _in_bwd(dqs, dks, dvs, dqg, dkg, dvg, drg, dz, dpre1, w_in_t, wg2_p):
    tm = _row_tile(384)
    lp = _lp()
    widths = (512, 128, 128, 256, 256, 512, 512)
    offs = (O_QS, O_KS, O_VS, O_QG, O_KG, O_VG, O_RG)

    def body(*refs):
        parts, (dz_ref, dp1_ref, w_ref, wg2_ref, dproj_ref, dh0_ref, dbin_ref, dbg_ref) = refs[:7], refs[7:]

        @pl.when(pl.program_id(0) == 0)
        def _():
            dbin_ref[...] = jnp.zeros_like(dbin_ref)
            dbg_ref[...] = jnp.zeros_like(dbg_ref)

        for pos, h in enumerate(HEAD_POS):
            val = parts[0][:, pos * DH:(pos + 1) * DH]
            dproj_ref[:, O_QS + h * DH:O_QS + (h + 1) * DH] = val.astype(BF16)
            dbin_ref[:, O_QS + h * DH:O_QS + (h + 1) * DH] += jnp.sum(val, axis=0, keepdims=True)
        for p_ref, off, wd in zip(parts[1:], offs[1:], widths[1:]):
            val = p_ref[...]
            dproj_ref[:, off:off + wd] = val.astype(BF16)
            dbin_ref[:, off:off + wd] += jnp.sum(val, axis=0, keepdims=True)
        dz = dz_ref[...]
        dlr = _dot_nt(dz, wg2_ref[...])
        dproj_ref[:, O_LR:O_LR + LANE] = dlr.astype(BF16)
        dbin_ref[:, O_LR:O_LR + LANE] += jnp.sum(dlr, axis=0, keepdims=True)
        dbg_ref[...] += jnp.sum(dz, axis=0, keepdims=True)
        dh0_ref[...] = ALPHA * dp1_ref[...] + _dot(dproj_ref[...], w_ref[...])

    return pl.pallas_call(
        body, name="in_bwd", grid=(lp // tm,),
        in_specs=[_rows(tm, w) for w in widths] + [_rows(tm, 256), _rows(tm, D), _const((D_IN_P, D)), _const((LANE, 256))],
        out_specs=[_rows(tm, D_IN_P), _rows(tm, D), _acc((1, D_IN_P)), _acc((1, 256))],
        out_shape=[pltpu.HBM((lp, D_IN_P), BF16), pltpu.HBM((lp, D), F32),
                   pltpu.HBM((1, D_IN_P), F32), pltpu.HBM((1, 256), F32)],
        compiler_params=_params(40, dimension_semantics=_seq()),
    )(*_hbm(dqs, dks, dvs, dqg, dkg, dvg, drg, dz, dpre1, w_in_t, wg2_p))


def _ln_in_bwd(x, meta_ext, dh0, g, token):
    tr = min(LN_ROWS, SEQ)

    def ln_bwd(x_ref, dh_ref, g_ref, dx_ref, dg_ref, db_ref):
        @pl.when(pl.program_id(0) == 0)
        def _():
            dg_ref[...] = jnp.zeros_like(dg_ref)
            db_ref[...] = jnp.zeros_like(db_ref)

        xhat, rstd = _ln_stats(x_ref[...])
        dh = dh_ref[...]
        dx_ref[...] = _ln_bwd(dh, xhat, rstd, g_ref[...])
        dg_ref[...] += jnp.sum(dh * xhat, axis=0, keepdims=True)
        db_ref[...] += jnp.sum(dh, axis=0, keepdims=True)

    def body(x_ref, dh_ref, g_ref, token_ref, dx_ref, dg_ref, db_ref):
        ln_bwd(x_ref, dh_ref, g_ref, dx_ref, dg_ref, db_ref)

    def meta_body(m_ref, dh_ref, g_ref, dm_ref, dg_ref, db_ref):
        ln_bwd(m_ref, dh_ref, g_ref, dm_ref, dg_ref, db_ref)

    sums = [pltpu.HBM((1, D), F32), pltpu.HBM((1, D), F32)]
    dx, dg, db = pl.pallas_call(
        body, name="ln_in_bwd", grid=(SEQ // tr,),
        in_specs=[_rows(tr, D), _rows(tr, D), _const((1, D)), _const(TOKEN)],
        out_specs=[_rows(tr, D), _acc((1, D)), _acc((1, D))],
        out_shape=[pltpu.HBM((SEQ, D), F32)] + sums,
        compiler_params=_params(32, dimension_semantics=_seq()),
    )(*_hbm(x, dh0, g), token)
    dm, dg_m, db_m = pl.pallas_call(
        meta_body, name="ln_in_bwd_meta", grid=(1,),
        in_specs=[_const((BLK, D)), pl.BlockSpec((BLK, D), lambda i: (SEQ // BLK, 0)), _const((1, D))],
        out_specs=[_acc((BLK, D)), _acc((1, D)), _acc((1, D))],
        out_shape=[pltpu.HBM((BLK, D), F32)] + sums,
        compiler_params=_params(16, dimension_semantics=_seq()),
    )(*_hbm(meta_ext, dh0, g))
    return dx, dm, dg + dg_m, db + db_m


def _local_step(x, target, ln_in_g, ln_in_b, b_in, bg2, sinks, gn, g1, b1, g2, b2,
                token, fetch_first, fetch_rest, exchange_ffn, ship_ffn, ship_w_in):
    row = lambda v: v.reshape(1, -1).astype(F32)
    b_in_p = jnp.pad(row(b_in), ((0, 0), (0, D_IN_P - D_IN)))
    gn4 = row(gn)
    sinks = sinks.reshape(-1).astype(F32)

    h_real = _ln_in_fwd_real(x, row(ln_in_g), row(ln_in_b), token)
    w_in_t, meta_full, wg2 = fetch_first([h_real])
    meta_ext = jnp.pad(meta_full, ((META_OFF, BLK - CH), (0, 0)))
    wg2_p = jnp.pad(wg2, ((0, LANE - wg2.shape[0]), (0, 0))).astype(BF16)
    h0 = _ln_in_fwd_meta(h_real, meta_ext, row(ln_in_g), row(ln_in_b))
    qs, ks, vs, qg, kg, vg, rg, glr, z = _in_proj(h0, w_in_t, b_in_p, wg2_p, row(bg2))
    o_s = _swa_fwd(sinks, qs, ks, vs)
    o_gla, st_all = _gla_fwd(qg, kg, vg, z)
    w_out, wg_t, wu_t, wd = fetch_rest([o_s, o_gla])
    o, pre1, h1 = _post_mix(o_s, o_gla, rg, h0, gn4, w_out, row(g1), row(b1))
    a, dgate, dup, dpre2, loss, dg2, db2 = _ffn_fwd_loss_bwd(h1, wg_t, wu_t, wd, target, row(g2), row(b2))
    dpre1, dg1, db1, do_s, do_gla, drg, dgn = _ffn_out_bwd(dpre2, dgate, dup, pre1, wg_t, wu_t, row(g1), w_out, o_gla,
                                                           rg, gn4)
    dwd = _atb(a, dpre2, "dw_down")
    dwg_t = _atb(dgate, h1, "dw_gate")
    dwu_t = _atb(dup, h1, "dw_up")
    token = exchange_ffn(dict(w_g=dwg_t, w_u=dwu_t, w_d=dwd))
    token = ship_ffn(_atb(o, dpre1, "dw_out", token))
    dqg, dkg, dvg, dz = _gla_bwd(qg, kg, vg, z, do_gla, st_all, token)
    dqs, dks, dvs, dsinks = _swa_bwd(sinks, qs, ks, vs, do_s, token)
    dproj, dh0, db_in_p, dbg2 = _in_bwd(dqs, dks, dvs, dqg, dkg, dvg, drg, dz, dpre1, w_in_t, wg2_p)
    token = ship_w_in(_atb(dproj, h0, "dw_in"))
    dwg2_p = _atb(glr, dz, "dw_gate_lr2")
    dx, dmeta_blk, dg_in, db_in_ln = _ln_in_bwd(x, meta_ext, dh0, row(ln_in_g), token)

    small = dict(meta_blk=dmeta_blk, ln_in_g=dg_in, ln_in_b=db_in_ln, ln1_g=dg1, ln1_b=db1, ln2_g=dg2, ln2_b=db2,
                 b_in_p=db_in_p, wg2_p=dwg2_p, bg2=dbg2, sinks=dsinks, gn=dgn, loss=loss)
    return dx, small


HBM = pl.BlockSpec(memory_space=pltpu.HBM)


def _place():
    return lax.axis_index("x"), lax.axis_index("y"), lax.axis_index("c")


def _other_chips(x, y):
    return [(1 - x, y), (x, 1 - y), (1 - x, 1 - y)]


def _dma_sems(n):
    return pltpu.SemaphoreType.DMA((n,))


def _comm_params():
    return pltpu.CompilerParams(has_side_effects=True)


SEM = pl.BlockSpec(memory_space=pltpu.SEMAPHORE)


PER_ARRAY = dict(gather=3, scatter=3, sibling=N_CHIPS)


def _ici_copies(kind, landing, srcs, lands, send_sems, recv_sems):
    x, y, c = _place()
    mine = 2 * x + y
    copies = []
    for a in range(len(srcs)):
        if kind == "sibling":
            for s in range(N_CHIPS):
                copies.append(pltpu.make_async_remote_copy(
                    srcs[a].at[s, 1 - c], lands[a].at[s], send_sems.at[N_CHIPS * a + s], recv_sems.at[N_CHIPS * a + s],
                    device_id=(x, y, 1 - c), device_id_type=MESH))
            continue
        for j, (px, py) in enumerate(_other_chips(x, y)):
            slab = 2 * px + py if landing else mine
            if kind == "gather":
                src, dst = srcs[a].at[c], lands[a].at[slab, c]
            else:
                src, dst = srcs[a].at[2 * px + py], lands[a].at[slab]
            copies.append(pltpu.make_async_remote_copy(src, dst, send_sems.at[3 * a + j], recv_sems.at[3 * a + j],
                                                       device_id=(px, py, c), device_id_type=MESH))
    return copies


def _split_params():
    return pltpu.CompilerParams(has_side_effects=pltpu.SideEffectType.DATAFLOW_SIDE_EFFECTING)


def _ici_start(kind, srcs, land_shapes, after, name):
    n = len(srcs)
    lands = [pltpu.with_memory_space_constraint(lax.empty(s, a.dtype), pltpu.HBM) for s, a in zip(land_shapes, srcs)]

    def body(*refs):
        outs = refs[2 * n + len(after):]
        for cp in _ici_copies(kind, False, refs[:n], refs[n:2 * n], outs[0], outs[1]):
            cp.start()
        outs[-1][...] = jnp.zeros(TOKEN, F32)

    outs = pl.pallas_call(
        body, name=name, in_specs=[HBM] * (2 * n) + [pl.BlockSpec(memory_space=pl.ANY)] * len(after),
        out_specs=[SEM, SEM] + [HBM] * (2 * n) + [pl.BlockSpec(memory_space=pltpu.VMEM)],
        out_shape=[_dma_sems(PER_ARRAY[kind] * n)] * 2 + [pltpu.HBM(a.shape, a.dtype) for a in list(srcs) + lands]
        + [jax.ShapeDtypeStruct(TOKEN, F32)],
        input_output_aliases={i: 2 + i for i in range(2 * n)},
        compiler_params=_split_params(),
    )(*_hbm(*srcs), *lands, *after)
    return outs[:-1], outs[-1]


def _ici_wait(kind, handle, after, name):
    n = (len(handle) - 2) // 2

    def body(*refs):
        for cp in _ici_copies(kind, True, refs[:n], refs[n:2 * n], refs[2 * n], refs[2 * n + 1]):
            cp.wait_send()
            cp.wait_recv()

    outs = pl.pallas_call(
        body, name=name, in_specs=[HBM] * (2 * n) + [SEM, SEM] + [pl.BlockSpec(memory_space=pl.ANY)] * len(after),
        out_specs=[HBM] * (2 * n), out_shape=[pltpu.HBM(a.shape, a.dtype) for a in handle[2:]],
        input_output_aliases={i: i for i in range(2 * n)},
        compiler_params=_split_params(),
    )(*handle[2:], handle[0], handle[1], *after)
    return list(outs[n:])


def _sibling_forward(lands, name):
    n = len(lands)

    def body(*refs):
        outs = refs[n:2 * n]
        send_sems, recv_sems = refs[2 * n:]
        x, y, c = _place()

        def copy(a, j, half):
            px, py = _other_chips(x, y)[j]
            blk = outs[a].at[2 * px + py, half]
            return pltpu.make_async_remote_copy(blk, blk, send_sems.at[3 * a + j], recv_sems.at[3 * a + j],
                                                device_id=(x, y, 1 - c), device_id_type=MESH)

        pairs = [(a, j) for a in range(n) for j in range(3)]
        for a, j in pairs:
            copy(a, j, c).start()
        for a, j in pairs:
            copy(a, j, 1 - c).wait_recv()
        for a, j in pairs:
            copy(a, j, c).wait_send()

    return pl.pallas_call(
        body, name=name, in_specs=[HBM] * n, out_specs=[HBM] * n,
        out_shape=[pltpu.HBM(a.shape, a.dtype) for a in lands],
        input_output_aliases={a: a for a in range(n)},
        scratch_shapes=[_dma_sems(3 * n)] * 2,
        compiler_params=_comm_params(),
    )(*_hbm(*lands))


def _sibling_exchange(grads, name):
    n = len(grads)

    def body(*refs):
        ins, outs = refs[:n], refs[n:2 * n]
        send_sems, recv_sems = refs[2 * n:]
        x, y, c = _place()
        copies = []
        for a in range(n):
            for s in range(N_CHIPS):
                cp = pltpu.make_async_remote_copy(ins[a].at[s, 1 - c], outs[a].at[s], send_sems.at[N_CHIPS * a + s],
                                                  recv_sems.at[N_CHIPS * a + s], device_id=(x, y, 1 - c),
                                                  device_id_type=MESH)
                cp.start()
                copies.append(cp)
        for cp in copies:
            cp.wait_recv()
        for cp in copies:
            cp.wait_send()

    return pl.pallas_call(
        body, name=name, in_specs=[HBM] * n, out_specs=[HBM] * n,
        out_shape=[pltpu.HBM((N_CHIPS, g.shape[2], D), F32) for g in grads],
        scratch_shapes=[_dma_sems(N_CHIPS * n)] * 2,
        compiler_params=_comm_params(),
    )(*_hbm(*grads))


def _add_halves(core, grads, recvs, dtypes, name):
    n = len(grads)
    heights = [g.shape[2] for g in grads]

    def body(c_ref, *refs):
        for a in range(n):
            refs[2 * n + a][...] = (refs[2 * a][0] + refs[2 * a + 1][...]).astype(dtypes[a])

    slab = lambda h: pl.BlockSpec((1, h, D), lambda s, c: (s, 0, 0))
    mine = lambda h: pl.BlockSpec((1, 1, h, D), lambda s, c: (s, c[0], 0, 0))
    return pl.pallas_call(
        body, name=name,
        grid_spec=pltpu.PrefetchScalarGridSpec(
            num_scalar_prefetch=1, grid=(N_CHIPS,),
            in_specs=[spec(h) for h in heights for spec in (mine, slab)], out_specs=[slab(h) for h in heights]),
        out_shape=[pltpu.HBM((N_CHIPS, h, D), dt) for h, dt in zip(heights, dtypes)],
        compiler_params=_params(32, dimension_semantics=_seq()),
    )(core, *_hbm(*[a for pair in zip(grads, recvs) for a in pair]))


N_DEVICES = 2 * N_CHIPS
PEER_FLIPS = [(dx, dy, dc) for dx in (0, 1) for dy in (0, 1) for dc in (0, 1)][1:]


def _small_exchange(pack, after):
    n = len(PEER_FLIPS)

    def body(p_ref, *refs):
        out_ref, send_sems, recv_sems = refs[len(after):]
        x, y, c = _place()
        flip = lambda v, d: 1 - v if d else v

        def copy(k, landing):
            px, py, pc = (flip(v, d) for v, d in zip((x, y, c), PEER_FLIPS[k]))
            slab = 4 * px + 2 * py + pc if landing else 4 * x + 2 * y + c
            return pltpu.make_async_remote_copy(p_ref, out_ref.at[slab], send_sems.at[k], recv_sems.at[k],
                                                device_id=(px, py, pc), device_id_type=MESH)

        for k in range(n):
            copy(k, False).start()
        for k in range(n):
            copy(k, True).wait_recv()
        for k in range(n):
            copy(k, False).wait_send()

    return pl.pallas_call(
        body, name="small_exchange", in_specs=[HBM] + [pl.BlockSpec(memory_space=pl.ANY)] * len(after), out_specs=HBM,
        out_shape=pltpu.HBM((N_DEVICES,) + pack.shape, F32),
        scratch_shapes=[_dma_sems(n)] * 2,
        compiler_params=_comm_params(),
    )(*_hbm(pack), *after)


def _sum_chips(slots, firsts, rests):
    n = len(firsts)

    def body(i_ref, *refs):
        for a in range(n):
            first, r1, r2, r3 = refs[4 * a:4 * a + 4]
            refs[4 * n + a][...] = ((first[...].astype(F32) + r1[...].astype(F32)) + r2[...].astype(F32)) + r3[...].astype(F32)

    slab = lambda h, k: pl.BlockSpec((1, h, D), lambda i, ix: (ix[k], 0, 0))
    heights = [f.shape[1] for f in firsts]
    return pl.pallas_call(
        body, name="sum_chips",
        grid_spec=pltpu.PrefetchScalarGridSpec(
            num_scalar_prefetch=1, grid=(1,),
            in_specs=[slab(h, k) for h in heights for k in range(4)], out_specs=[slab(h, 4) for h in heights]),
        out_shape=[pltpu.HBM((2, h, D), F32) for h in heights],
        compiler_params=_params(48, dimension_semantics=_seq()),
    )(slots, *_hbm(*[a for f, r in zip(firsts, rests) for a in (f, r, r, r)]))


def _join_halves(halves):
    n = len(halves)

    def body(*refs):
        outs = refs[n:2 * n]
        send_sems, recv_sems = refs[2 * n:]
        x, y, c = _place()

        def copy(a, slab):
            return pltpu.make_async_remote_copy(outs[a].at[slab], outs[a].at[slab], send_sems.at[a], recv_sems.at[a],
                                                device_id=(x, y, 1 - c), device_id_type=MESH)

        for a in range(n):
            copy(a, c).start()
        for a in range(n):
            copy(a, 1 - c).wait_recv()
        for a in range(n):
            copy(a, c).wait_send()

    return pl.pallas_call(
        body, name="join_halves", in_specs=[HBM] * n, out_specs=[HBM] * n,
        out_shape=[pltpu.HBM(h.shape, F32) for h in halves],
        input_output_aliases={a: a for a in range(n)},
        scratch_shapes=[_dma_sems(n)] * 2,
        compiler_params=_comm_params(),
    )(*_hbm(*halves))


def _chip_partials(grads, wire_dtypes, names, fetched=()):
    core = lax.axis_index("c").astype(jnp.int32).reshape(1)
    todo = len(grads) - len(fetched)
    recv = list(_sibling_exchange(grads[:todo], "sibling_exchange_" + names[0])) + list(fetched)
    return list(_add_halves(core, grads, recv, wire_dtypes, "add_halves_" + names[0]))


def _finish_reduce(parts, got):
    x, y, c = _place()
    others = [2 * px + py for px, py in _other_chips(x, y)]
    own_first = jnp.stack([2 * x + y] + others + [c]).astype(jnp.int32)
    return [f.reshape(2 * f.shape[1], D) for f in _join_halves(_sum_chips(own_first, parts, got))]


ADAMW_STEPS = 4


def _adamw(params):
    n = len(params)

    def block(shape):
        rows, cols = shape
        if rows % (8 * ADAMW_STEPS) == 0:
            return pl.BlockSpec((rows // ADAMW_STEPS, cols), lambda i: (i, 0))
        assert cols % (LANE * ADAMW_STEPS) == 0
        return pl.BlockSpec((rows, cols // ADAMW_STEPS), lambda i: (0, i))

    def body(*refs):
        for a in range(n):
            w_ref, g_ref, m_ref, v_ref = refs[4 * a:4 * a + 4]
            outs = refs[4 * n + 3 * a:4 * n + 3 * a + 3]
            outs[0][...], outs[1][...], outs[2][...] = _adamw_math(w_ref[...], g_ref[...], m_ref[...], v_ref[...])

    outs = pl.pallas_call(
        body, name="adamw_matrices", grid=(ADAMW_STEPS,),
        in_specs=[block(p[0].shape) for p in params for _ in range(4)],
        out_specs=[block(p[0].shape) for p in params for _ in range(3)],
        out_shape=[pltpu.HBM(p[0].shape, F32) for p in params for _ in range(3)],
        compiler_params=_params(48, dimension_semantics=_seq()),
    )(*_hbm(*[a for p in params for a in p]))
    return [outs[3 * a:3 * a + 3] for a in range(n)]


def _adamw_math(w, g, m, v):
    nm = ADAM_B1 * m + (1.0 - ADAM_B1) * g
    nv = ADAM_B2 * v + (1.0 - ADAM_B2) * (g * g)
    m_hat = nm / (1.0 - ADAM_B1 ** ADAM_STEP)
    v_hat = nv / (1.0 - ADAM_B2 ** ADAM_STEP)
    return -ADAM_LR * (m_hat / (jnp.sqrt(v_hat) + ADAM_EPS) + ADAM_WD * w), nm, nv


SMALL = (("meta_tokens", (N_META, D // N_CHIPS)), ("ln_in_g", (1, D)), ("ln_in_b", (1, D)), ("b_in", (1, D_IN)),
         ("w_gate_lr2", (GATE_RANK, GLA_HEADS * DK // N_CHIPS)), ("b_gate_lr2", (1, GLA_HEADS * DK)),
         ("attn_sinks", (1, SWA_HEADS)),
         ("gla_norm_g", (1, DV)), ("ln1_g", (1, D)), ("ln1_b", (1, D)), ("ln2_g", (1, D)), ("ln2_b", (1, D)))
ROW_META, ROW_B_IN, ROW_TAIL, ROW_WG2 = 0, 22, 25, 32
ROW_LN = dict(ln_in_g=16, ln_in_b=17, ln1_g=18, ln1_b=19, ln2_g=20, ln2_b=21)
TAIL_BG2, TAIL_SINKS, TAIL_GN, TAIL_LOSS = 0, 256, 256 + SWA_HEADS, 256 + SWA_HEADS + DV


def _adamw_small(place, packs, own, params):
    n = len(SMALL)

    def body(place_ref, packs_ref, own_ref, *refs):
        ins, outs, p_ref = refs[:3 * n], refs[3 * n:-1], refs[-1]
        me, c = place_ref[0], place_ref[1]
        total = jnp.where(me == 0, own_ref[...], packs_ref[0])
        for i in range(1, N_DEVICES):
            total = total + jnp.where(me == i, own_ref[...], packs_ref[i])
        p_ref[...] = total
        outs[4 * n][...] = total[ROW_TAIL:ROW_TAIL + 1, :]

        def mine(width, rows):
            part = lambda s: p_ref[rows, s * width:(s + 1) * width]
            return jnp.where(c == 0, part(0), jnp.where(c == 1, part(1), jnp.where(c == 2, part(2), part(3))))

        tail = lambda lo, width: p_ref[ROW_TAIL:ROW_TAIL + 1, lo:lo + width]
        grads = dict(
            meta_tokens=mine(D // N_CHIPS, slice(ROW_META, ROW_META + N_META)),
            b_in=jnp.concatenate([p_ref[ROW_B_IN:ROW_B_IN + 1, :], p_ref[ROW_B_IN + 1:ROW_B_IN + 2, :],
                                  p_ref[ROW_B_IN + 2:ROW_B_IN + 3, 0:D_IN - 2 * D]], axis=1),
            w_gate_lr2=mine(256 // N_CHIPS, slice(ROW_WG2, ROW_WG2 + 16)),
            b_gate_lr2=tail(TAIL_BG2, 256), attn_sinks=tail(TAIL_SINKS, SWA_HEADS), gla_norm_g=tail(TAIL_GN, DV),
            **{k: p_ref[r:r + 1, :] for k, r in ROW_LN.items()})
        for i, (name, _) in enumerate(SMALL):
            g = grads[name]
            outs[4 * i][...] = g
            outs[4 * i + 1][...], outs[4 * i + 2][...], outs[4 * i + 3][...] = _adamw_math(
                ins[3 * i][...], g, ins[3 * i + 1][...], ins[3 * i + 2][...])

    whole = lambda shape: pl.BlockSpec(shape, lambda i, c: (0,) * len(shape))
    outs = pl.pallas_call(
        body, name="adamw_small",
        grid_spec=pltpu.PrefetchScalarGridSpec(
            num_scalar_prefetch=1, grid=(1,),
            in_specs=[whole(packs.shape), whole(own.shape)] + [whole(s) for _, s in SMALL for _ in range(3)],
            out_specs=[whole(s) for _, s in SMALL for _ in range(4)] + [whole((1, D))],
            scratch_shapes=[pltpu.VMEM(own.shape, F32)]),
        out_shape=[pltpu.HBM(s, F32) for _, s in SMALL for _ in range(4)] + [pltpu.HBM((1, D), F32)],
        compiler_params=_params(16, dimension_semantics=_seq()),
    )(place, *_hbm(packs, own, *[a for p in params for a in p]))
    return [outs[4 * i:4 * i + 4] for i in range(n)], outs[4 * n]


def _small_pack(gr):
    names = ["meta_blk"] + list(ROW_LN) + ["b_in_p", "wg2_p", "bg2", "sinks", "gn", "loss"]
    gate_w = GLA_HEADS * DK

    def body(*refs):
        src, out = dict(zip(names, refs)), refs[-1]
        out[...] = jnp.zeros_like(out)
        out[ROW_META:ROW_META + N_META, :] = src["meta_blk"][META_OFF:CH, :]
        for k, r in ROW_LN.items():
            out[r:r + 1, :] = src[k][...]
        for j in range(-(-D_IN // D)):
            width = min(D, D_IN - j * D)
            out[ROW_B_IN + j:ROW_B_IN + j + 1, 0:width] = src["b_in_p"][:, j * D:j * D + width]
        tail = slice(ROW_TAIL, ROW_TAIL + 1)
        out[tail, TAIL_BG2:TAIL_BG2 + gate_w] = src["bg2"][...]
        out[tail, TAIL_SINKS:TAIL_SINKS + SWA_HEADS] = src["sinks"][:, 0:SWA_HEADS]
        out[tail, TAIL_GN:TAIL_GN + DV] = src["gn"][...]
        out[tail, TAIL_LOSS:TAIL_LOSS + 1] = src["loss"][:, 0:1]
        out[ROW_WG2:ROW_WG2 + GATE_RANK, 0:gate_w] = src["wg2_p"][0:GATE_RANK, :]

    arrays = [gr[k] for k in names]
    return pl.pallas_call(
        body, name="small_pack", grid=(1,),
        in_specs=[_acc(a.shape) for a in arrays], out_specs=_acc((SMALL_ROWS, D)),
        out_shape=pltpu.HBM((SMALL_ROWS, D), F32),
        compiler_params=_params(16, dimension_semantics=_seq()),
    )(*_hbm(*arrays))


BIG = ("w_in", "w_out", "w_g", "w_u", "w_d")


def kernel(x, meta_tokens, ln_in_g, ln_in_b, w_in, b_in, w_gate_lr2, b_gate_lr2, attn_sinks, gla_norm_g, w_out, ln1_g, ln1_b, w_ffn_gate, w_ffn_up, w_ffn_down, ln2_g, ln2_b, loss_target, m_meta_tokens, m_ln_in_g, m_ln_in_b, m_w_in, m_b_in, m_w_gate_lr2, m_b_gate_lr2, m_attn_sinks, m_gla_norm_g, m_w_out, m_ln1_g, m_ln1_b, m_w_ffn_gate, m_w_ffn_up, m_w_ffn_down, m_ln2_g, m_ln2_b, v_meta_tokens, v_ln_in_g, v_ln_in_b, v_w_in, v_b_in, v_w_gate_lr2, v_b_gate_lr2, v_attn_sinks, v_gla_norm_g, v_w_out, v_ln1_g, v_ln1_b, v_w_ffn_gate, v_w_ffn_up, v_w_ffn_down, v_ln2_g, v_ln2_b):
    chip = 2 * lax.axis_index("x") + lax.axis_index("y")

    halves = lambda a: a.reshape(2, a.shape[0] // 2, a.shape[1])
    r_in = SHARD_ROWS["w_in"]
    first = [halves(a) for a in (jnp.pad(w_in[0].T.astype(BF16), ((0, W_IN_WIN - r_in), (0, 0))), meta_tokens,
                                 w_gate_lr2[0])]
    rest = [halves(a) for a in (w_out[0].astype(BF16), w_ffn_gate[0].T.astype(BF16), w_ffn_up[0].T.astype(BF16),
                                w_ffn_down[0].astype(BF16))]
    lands = lambda arrs: [(N_CHIPS,) + a.shape for a in arrs]
    first_handle, first_token = _ici_start("gather", first, lands(first), [], "gather_first_start")
    rest_handle, token = _ici_start("gather", rest, lands(rest), [first_token], "gather_rest_start")

    def fetch(handle, shards, after, name):
        got = _sibling_forward(_ici_wait("gather", handle, after, name + "_wait"), name + "_forward")
        return [lax.dynamic_update_index_in_dim(g, s, chip, axis=0) for g, s in zip(got, shards)]

    def fetch_first(after):
        g_in, g_meta, g_wg2 = fetch(first_handle, first, after, "gather_first")
        w_in_t = jnp.pad(g_in.reshape(N_CHIPS, W_IN_WIN, D)[:, :r_in].reshape(D_IN, D), ((0, D_IN_P - D_IN), (0, 0)))
        meta_full = jnp.concatenate([g_meta[s].reshape(N_META, -1) for s in range(N_CHIPS)], axis=1)
        wg2_full = jnp.concatenate([g_wg2[s].reshape(w_gate_lr2.shape[1], -1) for s in range(N_CHIPS)], axis=1)
        return w_in_t, meta_full, wg2_full

    def fetch_rest(after):
        return [g.reshape(-1, D) for g in fetch(rest_handle, rest, after, "gather_rest")]

    sent = {}
    split = lambda grads: [g.reshape(N_CHIPS, 2, -1, D) for g in grads]

    def ship(key, grads, names, fetched=()):
        parts = _chip_partials(grads, [BF16] * len(grads), names, fetched)
        handle, ship_token = _ici_start("scatter", parts, [p.shape for p in parts], [], "scatter_" + key + "_start")
        sent[key] = (parts, handle)
        return ship_token

    def exchange_ffn(g):
        grads = split([g[k] for k in BIG[2:]])
        handle, exchange_token = _ici_start("sibling", grads, [(N_CHIPS,) + a.shape[2:] for a in grads], [],
                                            "sibling_ffn_start")
        sent["ffn_halves"] = (grads, handle)
        return exchange_token

    def ship_ffn(dw_out):
        grads, handle = sent["ffn_halves"]
        fetched = _ici_wait("sibling", handle, [dw_out], "sibling_ffn_wait")
        return ship("ffn", split([dw_out]) + grads, list(BIG[1:]), fetched)

    def ship_w_in(dw_in_t):
        win_start = [s * r_in // BF16_ROWS * BF16_ROWS for s in range(N_CHIPS)]
        return ship("w_in", split([jnp.stack([dw_in_t[st:st + W_IN_WIN] for st in win_start])]), ["w_in"])

    dx, gr = _local_step(
        x[0], loss_target[0], ln_in_g, ln_in_b, b_in[0], b_gate_lr2[0], attn_sinks[0], gla_norm_g[0], ln1_g[0],
        ln1_b[0], ln2_g[0], ln2_b[0], token, fetch_first, fetch_rest, exchange_ffn, ship_ffn, ship_w_in)
    ffn_got = _ici_wait("scatter", sent["ffn"][1], [dx], "scatter_ffn_wait")
    w_in_got = _ici_wait("scatter", sent["w_in"][1], [dx], "scatter_w_in_wait")

    small_own = _small_pack(gr)
    small_all = _small_exchange(small_own, [w_in_got[0]])
    red = _finish_reduce(sent["w_in"][0] + sent["ffn"][0], w_in_got + ffn_got)

    big_g = dict(zip(BIG, red))
    big_g["w_in"] = lax.dynamic_slice_in_dim(red[0], chip * (r_in % BF16_ROWS), r_in, axis=0)
    grads = dict(w_in=big_g["w_in"].T[None], w_out=big_g["w_out"][None], w_ffn_gate=big_g["w_g"].T[None],
                 w_ffn_up=big_g["w_u"].T[None], w_ffn_down=big_g["w_d"][None])
    weights = dict(meta_tokens=meta_tokens, ln_in_g=ln_in_g, ln_in_b=ln_in_b, w_in=w_in, b_in=b_in,
                   w_gate_lr2=w_gate_lr2, b_gate_lr2=b_gate_lr2, attn_sinks=attn_sinks, gla_norm_g=gla_norm_g,
                   w_out=w_out, ln1_g=ln1_g, ln1_b=ln1_b, w_ffn_gate=w_ffn_gate, w_ffn_up=w_ffn_up,
                   w_ffn_down=w_ffn_down, ln2_g=ln2_g, ln2_b=ln2_b)
    m_in = dict(meta_tokens=m_meta_tokens, ln_in_g=m_ln_in_g, ln_in_b=m_ln_in_b, w_in=m_w_in, b_in=m_b_in,
                w_gate_lr2=m_w_gate_lr2, b_gate_lr2=m_b_gate_lr2, attn_sinks=m_attn_sinks, gla_norm_g=m_gla_norm_g,
                w_out=m_w_out, ln1_g=m_ln1_g, ln1_b=m_ln1_b, w_ffn_gate=m_w_ffn_gate, w_ffn_up=m_w_ffn_up,
                w_ffn_down=m_w_ffn_down, ln2_g=m_ln2_g, ln2_b=m_ln2_b)
    v_in = dict(meta_tokens=v_meta_tokens, ln_in_g=v_ln_in_g, ln_in_b=v_ln_in_b, w_in=v_w_in, b_in=v_b_in,
                w_gate_lr2=v_w_gate_lr2, b_gate_lr2=v_b_gate_lr2, attn_sinks=v_attn_sinks, gla_norm_g=v_gla_norm_g,
                w_out=v_w_out, ln1_g=v_ln1_g, ln1_b=v_ln1_b, w_ffn_gate=v_w_ffn_gate, w_ffn_up=v_w_ffn_up,
                w_ffn_down=v_w_ffn_down, ln2_g=v_ln2_g, ln2_b=v_ln2_b)
    names = list(weights)
    big_names = ("w_in", "w_out", "w_ffn_gate", "w_ffn_up", "w_ffn_down")

    delta, new_m, new_v = {}, {}, {}
    flips = [(lambda a: a.T) if kk in ("w_in", "w_g", "w_u") else (lambda a: a) for kk in BIG]
    updated = _adamw([(flip(weights[k][0]), big_g[kk], flip(m_in[k][0]), flip(v_in[k][0]))
                      for k, kk, flip in zip(big_names, BIG, flips)])
    for k, flip, results in zip(big_names, flips, updated):
        delta[k], new_m[k], new_v[k] = (flip(t)[None] for t in results)
    small_in = [tuple(src[k].reshape(shape) for src in (weights, m_in, v_in)) for k, shape in SMALL]
    place = jnp.stack([2 * chip + lax.axis_index("c"), chip]).astype(jnp.int32)
    small_out, tail_row = _adamw_small(place, small_all, small_own, small_in)
    for (k, _), results in zip(SMALL, small_out):
        grads[k], delta[k], new_m[k], new_v[k] = (r.reshape(weights[k].shape) for r in results)

    return (tail_row[0, TAIL_LOSS], dx[None], *[grads[k] for k in names], *[delta[k] for k in names], *[new_m[k] for k in names],
            *[new_v[k] for k in names])
```

```python
import jax
import jax.numpy as jnp
from jax import lax
from jax.experimental import pallas as pl
from jax.experimental.pallas import tpu as pltpu

F32 = jnp.float32
BF16 = jnp.bfloat16
MESH = pl.DeviceIdType.MESH

D = 1024
SEQ = 4096
N_META = 16
SWA_HEADS, SWA_KV_HEADS, DH = 8, 2, 64
WINDOW = 128
GLA_HEADS, DK, DV = 4, 64, 128
GLA_TAU = 16.0
CH = 64
D_FF = 2816
D_IN = 2320
LN_EPS = 1e-5
RMS_EPS = 1e-6
ALPHA = 2.0 ** 0.25
NEG = -1e30
ADAM_LR, ADAM_B1, ADAM_B2, ADAM_EPS, ADAM_WD, ADAM_STEP = 0.001, 0.9, 0.999, 1e-8, 0.01, 10
O_QS, O_KS, O_VS, O_QG, O_KG, O_VG, O_RG, O_LR = 0, 512, 640, 768, 1024, 1280, 1792, 2304

LANE = 128
BLK = WINDOW
GATE_RANK = 16
D_IN_P = D_IN + LANE - GATE_RANK
META_OFF = CH - N_META
HEAD_POS = (0, 4, 1, 5, 2, 6, 3, 7)
LN_ROWS = 512
TOKEN = (8, LANE)
N_CHIPS = 4
SHARD_ROWS = dict(w_in=D_IN // N_CHIPS, w_out=D // N_CHIPS, w_g=D_FF // N_CHIPS, w_u=D_FF // N_CHIPS,
                  w_d=D_FF // N_CHIPS)
SMALL_ROWS = 48
BF16_ROWS = 16
W_IN_WIN = -(-SHARD_ROWS["w_in"] // (2 * BF16_ROWS)) * 2 * BF16_ROWS
VMEM_CAP_MB = 64
VMEM_SPARE_MB = 6


def _lp():
    return SEQ + BLK


def _row_tile(cap):
    lp = _lp()
    return max(t for t in range(16, cap + 1, 16) if lp % t == 0)


def _params(vmem_mb, **kw):
    assert vmem_mb <= VMEM_CAP_MB - VMEM_SPARE_MB
    return pltpu.CompilerParams(vmem_limit_bytes=vmem_mb << 20, **kw)


def _seq(n=1):
    return ("arbitrary",) * n


def _const(shape):
    return pl.BlockSpec(shape, lambda *_: (0,) * len(shape), pipeline_mode=pl.Buffered(1))


def _acc(shape):
    return pl.BlockSpec(shape, lambda *_: (0,) * len(shape))


def _rows(tm, width):
    return pl.BlockSpec((tm, width), lambda i: (i, 0))


def _dot(a, b):
    return jnp.dot(a.astype(BF16), b.astype(BF16), preferred_element_type=F32)


def _dot_nt(a, b):
    return lax.dot_general(a.astype(BF16), b.astype(BF16), (((1,), (1,)), ((), ())), preferred_element_type=F32)


def _dot_tn(a, b):
    return lax.dot_general(a.astype(BF16), b.astype(BF16), (((0,), (0,)), ((), ())), preferred_element_type=F32)


def _dot_exact(a, b):
    return jnp.dot(a, b, precision=lax.Precision.HIGHEST, preferred_element_type=F32)


def _ln_stats(x):
    mu = jnp.mean(x, axis=-1, keepdims=True)
    xc = x - mu
    rstd = lax.rsqrt(jnp.mean(xc * xc, axis=-1, keepdims=True) + LN_EPS)
    return xc * rstd, rstd


def _ln_bwd(dy, xhat, rstd, g):
    dxh = dy * g
    return rstd * (dxh - jnp.mean(dxh, axis=-1, keepdims=True) - xhat * jnp.mean(dxh * xhat, axis=-1, keepdims=True))


def _sigmoid(x):
    return 1.0 / (1.0 + jnp.exp(-x))


def _iota(shape, dim):
    return lax.broadcasted_iota(jnp.int32, shape, dim)


def _hbm(*arrays):
    return tuple(pltpu.with_memory_space_constraint(a, pltpu.HBM) for a in arrays)


def _ln_in_fwd_real(x, g, b, token):
    tr = min(LN_ROWS, SEQ)

    def body(x_ref, g_ref, b_ref, token_ref, h_ref):
        xhat, _ = _ln_stats(x_ref[...])
        h_ref[...] = xhat * g_ref[...] + b_ref[...]

    return pl.pallas_call(
        body, name="ln_in_fwd", grid=(SEQ // tr,),
        in_specs=[_rows(tr, D), _const((1, D)), _const((1, D)), _const(TOKEN)],
        out_specs=_rows(tr, D),
        out_shape=pltpu.HBM((_lp(), D), F32),
        compiler_params=_params(32, dimension_semantics=_seq()),
    )(*_hbm(x, g, b), token)


def _ln_in_fwd_meta(h_real, meta_ext, g, b):
    def meta_body(m_ref, g_ref, b_ref, real_ref, h_ref):
        xhat, _ = _ln_stats(m_ref[...])
        h_ref[...] = xhat * g_ref[...] + b_ref[...]

    return pl.pallas_call(
        meta_body, name="ln_in_fwd_meta", grid=(1,),
        in_specs=[_const((BLK, D)), _const((1, D)), _const((1, D)), pl.BlockSpec(memory_space=pl.ANY)],
        out_specs=pl.BlockSpec((BLK, D), lambda i: (SEQ // BLK, 0)),
        out_shape=pltpu.HBM((_lp(), D), F32),
        input_output_aliases={3: 0},
        compiler_params=_params(16, dimension_semantics=_seq()),
    )(*_hbm(meta_ext, g, b, h_real))


def _in_proj(h0, w_in_t, b_in_p, wg2_p, bg2):
    tm = _row_tile(384)
    lp = _lp()
    widths = (512, 128, 128, 256, 256, 512, 512, 128)
    offs = (O_QS, O_KS, O_VS, O_QG, O_KG, O_VG, O_RG, O_LR)

    def body(h_ref, w_ref, b_ref, wg2_ref, bg2_ref, *outs):
        proj = _dot_nt(h_ref[...], w_ref[...]) + b_ref[...]
        for pos, h in enumerate(HEAD_POS):
            outs[0][:, pos * DH:(pos + 1) * DH] = proj[:, O_QS + h * DH:O_QS + (h + 1) * DH]
        for o_ref, off, wd in zip(outs[1:8], offs[1:], widths[1:]):
            o_ref[...] = proj[:, off:off + wd]
        outs[8][...] = _dot(proj[:, O_LR:O_LR + LANE], wg2_ref[...]) + bg2_ref[...]

    return pl.pallas_call(
        body, name="in_proj", grid=(lp // tm,),
        in_specs=[_rows(tm, D), _const((D_IN_P, D)), _const((1, D_IN_P)), _const((LANE, 256)), _const((1, 256))],
        out_specs=[_rows(tm, w) for w in widths] + [_rows(tm, 256)],
        out_shape=[pltpu.HBM((lp, w), F32) for w in widths] + [pltpu.HBM((lp, 256), F32)],
        compiler_params=_params(40, dimension_semantics=_seq()),
    )(*_hbm(h0, w_in_t, b_in_p, wg2_p, bg2))


def _swa_masks(n):
    nb = SEQ // BLK
    is_meta = n == nb
    ri = _iota((BLK, BLK), 0)
    cj = _iota((BLK, BLK), 1)
    meta_col = ((cj >= META_OFF) & (cj < CH)).astype(jnp.int32)
    meta_q = meta_col * ((cj <= ri) & (ri < CH)).astype(jnp.int32)
    valid_m = jnp.where(is_meta, meta_q, meta_col) > 0
    dist_m = jnp.where(is_meta, ri - cj, n * BLK + ri + CH - cj).astype(F32)
    valid_p = jnp.where((n >= 1) & (n < nb), (cj > ri).astype(jnp.int32), 0) > 0
    dist_p = (ri + BLK - cj).astype(F32)
    valid_c = jnp.where(n < nb, (cj <= ri).astype(jnp.int32), 0) > 0
    dist_c = (ri - cj).astype(F32)
    return (dist_m, dist_p, dist_c), (valid_m, valid_p, valid_c)


def _swa_bias(n):
    dists, valids = _swa_masks(n)
    return (jnp.concatenate([-d for d in dists], axis=1),
            jnp.concatenate([jnp.where(v, 0.0, NEG) for v in valids], axis=1))


def _swa_half(ref, pos, scale=1.0):
    col = ref[:, (pos // 2) * LANE:(pos // 2 + 1) * LANE]
    lane = _iota((BLK, LANE), 1)
    mine = lane < DH if pos % 2 == 0 else lane >= DH
    return jnp.where(mine, col * scale, 0.0).astype(BF16)


def _swa_merge(even, odd):
    return jnp.where(_iota((BLK, LANE), 1) < DH, even, odd)


def _swa_softmax(t, sink):
    m = jnp.maximum(jnp.max(t, axis=-1, keepdims=True), sink)
    e = jnp.exp(t - m)
    e_sink = jnp.exp(sink - m)
    inv = 1.0 / (jnp.sum(e, axis=-1, keepdims=True) + e_sink)
    return e * inv, e_sink * inv


def _swa_kv_specs(width):
    nb = SEQ // BLK
    return [pl.BlockSpec((BLK, width), lambda n: (nb, 0)),
            pl.BlockSpec((BLK, width), lambda n: (jnp.clip(n - 1, 0, nb - 1), 0)),
            pl.BlockSpec((BLK, width), lambda n: (jnp.minimum(n, nb), 0))]


def _swa_fwd(sinks, qs, ks, vs):
    nb = SEQ // BLK
    heads = range(SWA_HEADS)

    def body(sink_ref, q_ref, km_ref, kp_ref, kc_ref, vm_ref, vp_ref, vc_ref, o_ref):
        negdist, maskbias = _swa_bias(pl.program_id(0))
        k_all = jnp.concatenate([km_ref[...], kp_ref[...], kc_ref[...]], axis=0).astype(BF16)
        v_all = jnp.concatenate([vm_ref[...], vp_ref[...], vc_ref[...]], axis=0).astype(BF16)
        q = [_swa_half(q_ref, pos, DH ** -0.5) for pos in heads]
        t = [_dot_nt(q[pos], k_all) + (2.0 ** -(HEAD_POS[pos] + 1) * negdist + maskbias) for pos in heads]
        p = [_swa_softmax(t[pos], sink_ref[HEAD_POS[pos]])[0].astype(BF16) for pos in heads]
        o = [_dot(p[pos], v_all) for pos in heads]
        for col in range(SWA_HEADS // 2):
            o_ref[:, col * LANE:(col + 1) * LANE] = _swa_merge(o[2 * col], o[2 * col + 1])

    kvw = SWA_KV_HEADS * DH
    return pl.pallas_call(
        body, name="swa_fwd", grid=(nb + 1,),
        in_specs=[pl.BlockSpec(memory_space=pltpu.SMEM), _rows(BLK, SWA_HEADS * DH)] + _swa_kv_specs(kvw) + _swa_kv_specs(kvw),
        out_specs=_rows(BLK, SWA_HEADS * DH),
        out_shape=pltpu.HBM((_lp(), SWA_HEADS * DH), F32),
        compiler_params=_params(16, dimension_semantics=_seq()),
    )(sinks, *_hbm(qs, ks, ks, ks, vs, vs, vs))


GLA_PER_STEP = BLK // CH


def _gla_block(s):
    nb = SEQ // BLK
    return jnp.where(s == 0, nb, s - 1)


def _gla_rowmask(s):
    ri = _iota((BLK, 1), 0)
    m = jnp.where(s == 0, ((ri >= META_OFF) & (ri < CH)).astype(jnp.int32), 1)
    return (m > 0).astype(F32) + jnp.zeros((BLK, 1), F32)


def _gla_chunk_masks():
    r, c = _iota((BLK, BLK), 0), _iota((BLK, BLK), 1)
    same = ((r < CH) & (c < CH)) | ((r >= CH) & (c >= CH))
    return same & (r >= c), same & (r <= c), same


def _gla_decay(z, rmask):
    log_g = (jnp.minimum(z, 0.0) - jnp.log1p(jnp.exp(-jnp.abs(z)))) * (rmask / GLA_TAU)
    lower, _, same = _gla_chunk_masks()
    return _dot_exact(lower.astype(F32), log_g), _dot_exact(same.astype(F32), log_g)


def _gla_slices(c, h):
    return slice(c * CH, (c + 1) * CH), slice(h * DK, (h + 1) * DK), slice(h * DV, (h + 1) * DV)


def _gla_fwd(qg, kg, vg, z):
    steps = SEQ // BLK + 1
    kw, vw = GLA_HEADS * DK, GLA_HEADS * DV
    pairs = [(c, h) for c in range(GLA_PER_STEP) for h in range(GLA_HEADS)]

    def body(q_ref, k_ref, v_ref, z_ref, o_ref, st_ref, st):
        s = pl.program_id(0)

        @pl.when(s == 0)
        def _():
            st[...] = jnp.zeros_like(st)

        rmask = _gla_rowmask(s)
        b, b_last = _gla_decay(z_ref[...], rmask)
        q = q_ref[...] * (rmask * DK ** -0.5)
        k = k_ref[...] * rmask
        v = v_ref[...] * rmask
        qe = q * jnp.exp(b)
        ke = k * jnp.exp(-b)
        kd = k * jnp.exp(b_last - b)
        e_last = jnp.exp(b_last)
        causal = _iota((CH, CH), 0) >= _iota((CH, CH), 1)
        a, upd, intra = {}, {}, {}
        for c, h in pairs:
            rows, ks, vs_ = _gla_slices(c, h)
            a[c, h] = jnp.where(causal, _dot_nt(qe[rows, ks], ke[rows, ks]), 0.0)
            upd[c, h] = _dot_tn(v[rows, vs_], kd[rows, ks])
        for c, h in pairs:
            rows, ks, vs_ = _gla_slices(c, h)
            intra[c, h] = _dot(a[c, h], v[rows, vs_])
        state = st[...]
        for c in range(GLA_PER_STEP):
            st_ref[0, c] = state
            for h in range(GLA_HEADS):
                rows, ks, vs_ = _gla_slices(c, h)
                o_ref[rows, vs_] = intra[c, h] + _dot_nt(qe[rows, ks], state[:, ks])
            state = state * e_last[c * CH:c * CH + 1] + jnp.concatenate([upd[c, h] for h in range(GLA_HEADS)], axis=1)
        st[...] = state

    blk = lambda w: pl.BlockSpec((BLK, w), lambda s: (_gla_block(s), 0))
    return pl.pallas_call(
        body, name="gla_fwd", grid=(steps,),
        in_specs=[blk(kw), blk(kw), blk(vw), blk(kw)],
        out_specs=[blk(vw), pl.BlockSpec((1, GLA_PER_STEP, DV, kw), lambda s: (s, 0, 0, 0))],
        out_shape=[pltpu.HBM((_lp(), vw), F32), pltpu.HBM((steps, GLA_PER_STEP, DV, kw), F32)],
        scratch_shapes=[pltpu.VMEM((DV, kw), F32)],
        compiler_params=_params(16, dimension_semantics=_seq()),
    )(*_hbm(qg, kg, vg, z))


def _post_mix(o_s, o_gla, r_g, h0, gn4, w_out, g1, b1):
    tm = _row_tile(384)
    lp = _lp()

    def body(os_ref, og_ref, r_ref, h0_ref, gn_ref, w_ref, g_ref, b_ref, o_ref, pre_ref, h1_ref):
        for pos, h in enumerate(HEAD_POS):
            o_ref[:, h * DH:(h + 1) * DH] = os_ref[:, pos * DH:(pos + 1) * DH].astype(BF16)
        for h in range(GLA_HEADS):
            hs = slice(h * DV, (h + 1) * DV)
            xg = og_ref[:, hs]
            n = xg * lax.rsqrt(jnp.mean(xg * xg, axis=-1, keepdims=True) + RMS_EPS) * gn_ref[...]
            r = r_ref[:, hs]
            o_ref[:, 512 + h * DV:512 + (h + 1) * DV] = (n * (r * _sigmoid(r))).astype(BF16)
        pre = ALPHA * h0_ref[...] + _dot(o_ref[...], w_ref[...])
        pre_ref[...] = pre
        xhat, _ = _ln_stats(pre)
        h1_ref[...] = xhat * g_ref[...] + b_ref[...]

    return pl.pallas_call(
        body, name="post_mix", grid=(lp // tm,),
        in_specs=[_rows(tm, 512), _rows(tm, 512), _rows(tm, 512), _rows(tm, D), _const((1, DV)), _const((D, D)),
                  _const((1, D)), _const((1, D))],
        out_specs=[_rows(tm, D), _rows(tm, D), _rows(tm, D)],
        out_shape=[pltpu.HBM((lp, D), BF16), pltpu.HBM((lp, D), F32),
                   pltpu.HBM((lp, D), F32)],
        compiler_params=_params(32, dimension_semantics=_seq()),
    )(*_hbm(o_s, o_gla, r_g, h0, gn4, w_out, g1, b1))


def _ffn_fwd_loss_bwd(h1, wg_t, wu_t, wd, target, g2, b2):
    lp = _lp()
    tm = max(t for t in range(BLK, 384 + 1, BLK) if lp % t == 0)
    steps = lp // tm
    last_blk = SEQ // BLK - 1
    half = D_FF // 2
    n_t = tm // BLK

    def body(*refs):
        h_ref, wg_ref, wu_ref, wd_ref = refs[:4]
        t_refs = refs[4:4 + n_t]
        g2_ref, b2_ref, a_ref, dgate_ref, dup_ref, dp_ref, loss_ref, dg_ref, db_ref, g_s, u_s, acc = refs[4 + n_t:]
        i = pl.program_id(0)

        @pl.when(i == 0)
        def _():
            acc[...] = jnp.zeros_like(acc)
            dg_ref[...] = jnp.zeros_like(dg_ref)
            db_ref[...] = jnp.zeros_like(db_ref)

        h = h_ref[...]
        hb = h.astype(BF16)
        pre = ALPHA * h
        for j in range(2):
            cols = slice(j * half, (j + 1) * half)
            g = _dot_nt(hb, wg_ref[cols, :])
            u = _dot_nt(hb, wu_ref[cols, :])
            g_s[:, cols] = g
            u_s[:, cols] = u
            pre = pre + _dot(g * _sigmoid(g) * u, wd_ref[cols, :])
        xhat, rstd = _ln_stats(pre)
        real = i * tm + _iota((tm, 1), 0) < SEQ
        target_rows = jnp.concatenate([t[...] for t in t_refs], axis=0)
        diff = jnp.where(real, xhat * g2_ref[...] + b2_ref[...] - target_rows, 0.0)
        acc[...] += jnp.sum(diff * diff, axis=0, keepdims=True)
        dy = diff * (1.0 / D)
        dpre = _ln_bwd(dy, xhat, rstd, g2_ref[...])
        dp_ref[...] = dpre
        dg_ref[...] += jnp.sum(dy * xhat, axis=0, keepdims=True)
        db_ref[...] += jnp.sum(dy, axis=0, keepdims=True)
        dpb = dpre.astype(BF16)
        for j in range(2):
            cols = slice(j * half, (j + 1) * half)
            g, u = g_s[:, cols], u_s[:, cols]
            sg = _sigmoid(g)
            silu = g * sg
            da = _dot_nt(dpb, wd_ref[cols, :])
            a_ref[:, cols] = (silu * u).astype(BF16)
            dgate_ref[:, cols] = (da * u * (sg * (1.0 + g * (1.0 - sg)))).astype(BF16)
            dup_ref[:, cols] = (da * silu).astype(BF16)

        @pl.when(i == steps - 1)
        def _():
            loss_ref[...] = jnp.zeros_like(loss_ref) + (0.5 / D) * jnp.sum(acc[...], axis=1, keepdims=True)

    t_spec = lambda k: pl.BlockSpec((BLK, D), lambda i: (jnp.minimum(i * n_t + k, last_blk), 0))
    return pl.pallas_call(
        body, name="ffn_fwd_loss_bwd", grid=(steps,),
        in_specs=[_rows(tm, D), _const((D_FF, D)), _const((D_FF, D)), _const((D_FF, D))]
        + [t_spec(k) for k in range(n_t)] + [_const((1, D)), _const((1, D))],
        out_specs=[_rows(tm, D_FF), _rows(tm, D_FF), _rows(tm, D_FF), _rows(tm, D), _acc((1, LANE)), _acc((1, D)),
                   _acc((1, D))],
        out_shape=[pltpu.HBM((lp, D_FF), BF16)] * 3 + [pltpu.HBM((lp, D), F32), pltpu.HBM((1, LANE), F32),
                                                         pltpu.HBM((1, D), F32), pltpu.HBM((1, D), F32)],
        scratch_shapes=[pltpu.VMEM((tm, D_FF), F32), pltpu.VMEM((tm, D_FF), F32), pltpu.VMEM((1, D), F32)],
        compiler_params=_params(58, dimension_semantics=_seq()),
    )(*_hbm(h1, wg_t, wu_t, wd, *[target] * n_t, g2, b2))


def _ffn_out_bwd(dpre2, dgate, dup, pre1, wg_t, wu_t, g1, w_out, o_gla, r_g, gn4):
    tm = _row_tile(384)
    lp = _lp()

    def body(dp_ref, dg_ref, du_ref, p1_ref, wg_ref, wu_ref, g1_ref, w_ref, og_ref, r_ref, gn_ref,
             dp1_ref, dg1_ref, db1_ref, dos_ref, dog_ref, dr_ref, dgn_ref):
        @pl.when(pl.program_id(0) == 0)
        def _():
            for acc_ref in (dg1_ref, db1_ref, dgn_ref):
                acc_ref[...] = jnp.zeros_like(acc_ref)

        dh1 = ALPHA * dp_ref[...] + _dot(dg_ref[...], wg_ref[...]) + _dot(du_ref[...], wu_ref[...])
        xhat, rstd1 = _ln_stats(p1_ref[...])
        dpre1 = _ln_bwd(dh1, xhat, rstd1, g1_ref[...])
        dp1_ref[...] = dpre1
        dg1_ref[...] += jnp.sum(dh1 * xhat, axis=0, keepdims=True)
        db1_ref[...] += jnp.sum(dh1, axis=0, keepdims=True)

        do = _dot_nt(dpre1, w_ref[...])
        for pos, h in enumerate(HEAD_POS):
            dos_ref[:, pos * DH:(pos + 1) * DH] = do[:, h * DH:(h + 1) * DH]
        gn = gn_ref[...]
        for h in range(GLA_HEADS):
            hs = slice(h * DV, (h + 1) * DV)
            xg = og_ref[:, hs]
            rstd = lax.rsqrt(jnp.mean(xg * xg, axis=-1, keepdims=True) + RMS_EPS)
            nx = xg * rstd
            r = r_ref[:, hs]
            sr = _sigmoid(r)
            d_o = do[:, 512 + h * DV:512 + (h + 1) * DV]
            dr_ref[:, hs] = d_o * (nx * gn) * (sr * (1.0 + r * (1.0 - sr)))
            dn = d_o * (r * sr)
            dgn_ref[...] += jnp.sum(dn * nx, axis=0, keepdims=True)
            dnx = dn * gn
            dog_ref[:, hs] = rstd * (dnx - nx * jnp.mean(dnx * nx, axis=-1, keepdims=True))

    return pl.pallas_call(
        body, name="ffn_out_bwd", grid=(lp // tm,),
        in_specs=[_rows(tm, D), _rows(tm, D_FF), _rows(tm, D_FF), _rows(tm, D), _const((D_FF, D)), _const((D_FF, D)),
                  _const((1, D)), _const((D, D)), _rows(tm, 512), _rows(tm, 512), _const((1, DV))],
        out_specs=[_rows(tm, D), _acc((1, D)), _acc((1, D)), _rows(tm, 512), _rows(tm, 512), _rows(tm, 512),
                   _acc((1, DV))],
        out_shape=[pltpu.HBM((lp, D), F32), pltpu.HBM((1, D), F32), pltpu.HBM((1, D), F32)]
        + [pltpu.HBM((lp, 512), F32)] * 3 + [pltpu.HBM((1, DV), F32)],
        compiler_params=_params(48, dimension_semantics=_seq()),
    )(*_hbm(dpre2, dgate, dup, pre1, wg_t, wu_t, g1, w_out, o_gla, r_g, gn4))


def _atb(a, b, name, token=None):
    lp = _lp()
    tm = _row_tile(1408)
    n, w = a.shape[1], b.shape[1]
    bw = 512 if n * w * 4 > (4 << 20) else w
    tokens = [] if token is None else [token]

    def body(a_ref, b_ref, *rest):
        o_ref = rest[-1]

        @pl.when(pl.program_id(1) == 0)
        def _():
            o_ref[...] = jnp.zeros_like(o_ref)

        o_ref[...] += _dot_tn(a_ref[...], b_ref[...])

    return pl.pallas_call(
        body, name=name, grid=(w // bw, lp // tm),
        in_specs=[pl.BlockSpec((tm, n), lambda j, k: (k, 0)), pl.BlockSpec((tm, bw), lambda j, k: (k, j))]
        + [_const(TOKEN)] * len(tokens),
        out_specs=pl.BlockSpec((n, bw), lambda j, k: (0, j)),
        out_shape=pltpu.HBM((n, w), F32),
        compiler_params=_params(48, dimension_semantics=_seq(2)),
    )(*_hbm(a, b), *tokens)


def _gla_bwd(qg, kg, vg, z, do_gla, st_all, token):
    steps = SEQ // BLK + 1
    kw, vw = GLA_HEADS * DK, GLA_HEADS * DV
    pairs = [(c, h) for c in range(GLA_PER_STEP) for h in range(GLA_HEADS)]
    heads = range(GLA_HEADS)

    def body(q_ref, k_ref, v_ref, z_ref, do_ref, st_ref, token_ref, dq_ref, dk_ref, dv_ref, dz_ref, dst):
        @pl.when(pl.program_id(0) == 0)
        def _():
            dst[...] = jnp.zeros_like(dst)

        rmask = _gla_rowmask(steps - 1 - pl.program_id(0))
        zz = z_ref[...]
        b, b_last = _gla_decay(zz, rmask)
        e_b, e_nb, e_kd, e_last = jnp.exp(b), jnp.exp(-b), jnp.exp(b_last - b), jnp.exp(b_last)
        q = q_ref[...] * (rmask * DK ** -0.5)
        k = k_ref[...] * rmask
        v = v_ref[...] * rmask
        qe, ke, kd = q * e_b, k * e_nb, k * e_kd
        d_o = do_ref[...]
        causal = _iota((CH, CH), 0) >= _iota((CH, CH), 1)
        a, da, dqe, dke, dv_intra, carry = {}, {}, {}, {}, {}, {}
        for c, h in pairs:
            rows, ks, vs_ = _gla_slices(c, h)
            a[c, h] = jnp.where(causal, _dot_nt(qe[rows, ks], ke[rows, ks]), 0.0)
            da[c, h] = jnp.where(causal, _dot_nt(d_o[rows, vs_], v[rows, vs_]), 0.0)
            carry[c, h] = _dot_tn(d_o[rows, vs_], qe[rows, ks])
        for c, h in pairs:
            rows, ks, vs_ = _gla_slices(c, h)
            dqe[c, h] = _dot(d_o[rows, vs_], st_ref[0, c][:, ks]) + _dot(da[c, h], ke[rows, ks])
            dke[c, h] = _dot_tn(da[c, h], qe[rows, ks])
            dv_intra[c, h] = _dot_tn(a[c, h], d_o[rows, vs_])
        dstate = dst[...]
        dkd, db_decay = {}, {}
        for c in reversed(range(GLA_PER_STEP)):
            for h in heads:
                rows, ks, vs_ = _gla_slices(c, h)
                dkd[c, h] = _dot(v[rows, vs_], dstate[:, ks])
                dv_ref[rows, vs_] = dv_intra[c, h] + _dot_nt(kd[rows, ks], dstate[:, ks])
            chunk_last = e_last[c * CH:c * CH + 1]
            db_decay[c] = jnp.sum(dstate * st_ref[0, c], axis=0, keepdims=True) * chunk_last
            dstate = dstate * chunk_last + jnp.concatenate([carry[c, h] for h in heads], axis=1)
        dst[...] = dstate
        rows_of = lambda parts: jnp.concatenate(
            [jnp.concatenate([parts[c, h] for h in heads], axis=1) for c in range(GLA_PER_STEP)], axis=0)
        dqe_all, dke_all, dkd_all = rows_of(dqe), rows_of(dke), rows_of(dkd)
        dq_ref[...] = dqe_all * e_b * (rmask * DK ** -0.5)
        dk_ref[...] = (dke_all * e_nb + dkd_all * e_kd) * rmask
        dkd_kd = dkd_all * kd
        db = dqe_all * qe - dke_all * ke - dkd_kd
        _, upper, same = _gla_chunk_masks()
        decay_rows = jnp.concatenate([jnp.broadcast_to(db_decay[c], (CH, kw)) for c in range(GLA_PER_STEP)], axis=0)
        dlog_g = _dot_exact(upper.astype(F32), db) + _dot_exact(same.astype(F32), dkd_kd) + decay_rows
        dz_ref[...] = dlog_g * (rmask / GLA_TAU) * _sigmoid(-zz)

    blk = lambda w: pl.BlockSpec((BLK, w), lambda s: (_gla_block(steps - 1 - s), 0))
    return pl.pallas_call(
        body, name="gla_bwd", grid=(steps,),
        in_specs=[blk(kw), blk(kw), blk(vw), blk(kw), blk(vw),
                  pl.BlockSpec((1, GLA_PER_STEP, DV, kw), lambda s: (steps - 1 - s, 0, 0, 0)), _const(TOKEN)],
        out_specs=[blk(kw), blk(kw), blk(vw), blk(kw)],
        out_shape=[pltpu.HBM((_lp(), kw), F32), pltpu.HBM((_lp(), kw), F32),
                   pltpu.HBM((_lp(), vw), F32), pltpu.HBM((_lp(), kw), F32)],
        scratch_shapes=[pltpu.VMEM((DV, kw), F32)],
        compiler_params=_params(16, dimension_semantics=_seq()),
    )(*_hbm(qg, kg, vg, z, do_gla, st_all), token)


def _swa_bwd(sinks, qs, ks, vs, do_s, token):
    nb = SEQ // BLK
    kvw = SWA_KV_HEADS * DH
    scale = DH ** -0.5
    heads = range(SWA_HEADS)

    def body(sink_ref, q_ref, km_ref, kp_ref, kc_ref, vm_ref, vp_ref, vc_ref, do_ref, token_ref,
             dq_ref, dk_ref, dv_ref, dsink_ref, carry_k, carry_v, meta_k, meta_v):
        n = pl.program_id(0)

        @pl.when(n == 0)
        def _():
            for r in (carry_k, carry_v, meta_k, meta_v):
                r[...] = jnp.zeros_like(r)
            dsink_ref[...] = jnp.zeros_like(dsink_ref)

        @pl.when(n <= nb)
        def _():
            negdist, maskbias = _swa_bias(n)
            lane = _iota((1, LANE), 1)
            k_all = jnp.concatenate([km_ref[...], kp_ref[...], kc_ref[...]], axis=0).astype(BF16)
            v_all = jnp.concatenate([vm_ref[...], vp_ref[...], vc_ref[...]], axis=0).astype(BF16)
            q = [_swa_half(q_ref, pos, scale) for pos in heads]
            d_o = [_swa_half(do_ref, pos) for pos in heads]
            t = [_dot_nt(q[pos], k_all) + (2.0 ** -(HEAD_POS[pos] + 1) * negdist + maskbias) for pos in heads]
            dp = [_dot_nt(d_o[pos], v_all) for pos in heads]
            soft = [_swa_softmax(t[pos], sink_ref[HEAD_POS[pos]]) for pos in heads]
            p = [s[0] for s in soft]
            delta = [jnp.sum(p[pos] * dp[pos], axis=-1, keepdims=True) for pos in heads]
            ds = [(p[pos] * (dp[pos] - delta[pos])).astype(BF16) for pos in heads]
            dq = [_dot(ds[pos], k_all) for pos in heads]
            for col in range(SWA_HEADS // 2):
                dq_ref[:, col * LANE:(col + 1) * LANE] = scale * _swa_merge(dq[2 * col], dq[2 * col + 1])
            dsink = jnp.zeros((1, LANE), F32)
            for pos in heads:
                dsink = dsink + jnp.where(lane == HEAD_POS[pos],
                                          -jnp.sum(soft[pos][1] * delta[pos], axis=0, keepdims=True), 0.0)
            dsink_ref[...] += dsink
            dk3 = _dot_tn(jnp.concatenate(q, axis=0), jnp.concatenate(ds, axis=0)).T
            dv3 = _dot_tn(jnp.concatenate(d_o, axis=0), jnp.concatenate([x.astype(BF16) for x in p], axis=0)).T
            meta_k[...] += dk3[0:BLK]
            meta_v[...] += dv3[0:BLK]
            dk_ref[...] = carry_k[...] + dk3[BLK:2 * BLK]
            dv_ref[...] = carry_v[...] + dv3[BLK:2 * BLK]
            carry_k[...] = dk3[2 * BLK:3 * BLK]
            carry_v[...] = dv3[2 * BLK:3 * BLK]

        @pl.when(n == nb + 1)
        def _():
            dk_ref[...] = meta_k[...]
            dv_ref[...] = meta_v[...]

    kv_out = pl.BlockSpec((BLK, kvw), lambda n: (jnp.where(n == nb + 1, nb, jnp.clip(n - 1, 0, nb - 1)), 0))
    qblk = pl.BlockSpec((BLK, SWA_HEADS * DH), lambda n: (jnp.minimum(n, nb), 0))
    return pl.pallas_call(
        body, name="swa_bwd", grid=(nb + 2,),
        in_specs=[pl.BlockSpec(memory_space=pltpu.SMEM), qblk] + _swa_kv_specs(kvw) + _swa_kv_specs(kvw)
        + [qblk, _const(TOKEN)],
        out_specs=[qblk, kv_out, kv_out, _acc((1, LANE))],
        out_shape=[pltpu.HBM((_lp(), SWA_HEADS * DH), F32), pltpu.HBM((_lp(), kvw), F32),
                   pltpu.HBM((_lp(), kvw), F32), pltpu.HBM((1, LANE), F32)],
        scratch_shapes=[pltpu.VMEM((BLK, kvw), F32)] * 4,
        compiler_params=_params(16, dimension_semantics=_seq()),
    )(sinks, *_hbm(qs, ks, ks, ks, vs, vs, vs, do_s), token)


def _in_bwd(dqs, dks, dvs, dqg, dkg, dvg, drg, dz, dpre1, w_in_t, wg2_p):
    tm = _row_tile(384)
    lp = _lp()
    widths = (512, 128, 128, 256, 256, 512, 512)
    offs = (O_QS, O_KS, O_VS, O_QG, O_KG, O_VG, O_RG)

    def body(*refs):
        parts, (dz_ref, dp1_ref, w_ref, wg2_ref, dproj_ref, dh0_ref, dbin_ref, dbg_ref) = refs[:7], refs[7:]

        @pl.when(pl.program_id(0) == 0)
        def _():
            dbin_ref[...] = jnp.zeros_like(dbin_ref)
            dbg_ref[...] = jnp.zeros_like(dbg_ref)

        for pos, h in enumerate(HEAD_POS):
            val = parts[0][:, pos * DH:(pos + 1) * DH]
            dproj_ref[:, O_QS + h * DH:O_QS + (h + 1) * DH] = val.astype(BF16)
            dbin_ref[:, O_QS + h * DH:O_QS + (h + 1) * DH] += jnp.sum(val, axis=0, keepdims=True)
        for p_ref, off, wd in zip(parts[1:], offs[1:], widths[1:]):
            val = p_ref[...]
            dproj_ref[:, off:off + wd] = val.astype(BF16)
            dbin_ref[:, off:off + wd] += jnp.sum(val, axis=0, keepdims=True)
        dz = dz_ref[...]
        dlr = _dot_nt(dz, wg2_ref[...])
        dproj_ref[:, O_LR:O_LR + LANE] = dlr.astype(BF16)
        dbin_ref[:, O_LR:O_LR + LANE] += jnp.sum(dlr, axis=0, keepdims=True)
        dbg_ref[...] += jnp.sum(dz, axis=0, keepdims=True)
        dh0_ref[...] = ALPHA * dp1_ref[...] + _dot(dproj_ref[...], w_ref[...])

    return pl.pallas_call(
        body, name="in_bwd", grid=(lp // tm,),
        in_specs=[_rows(tm, w) for w in widths] + [_rows(tm, 256), _rows(tm, D), _const((D_IN_P, D)), _const((LANE, 256))],
        out_specs=[_rows(tm, D_IN_P), _rows(tm, D), _acc((1, D_IN_P)), _acc((1, 256))],
        out_shape=[pltpu.HBM((lp, D_IN_P), BF16), pltpu.HBM((lp, D), F32),
                   pltpu.HBM((1, D_IN_P), F32), pltpu.HBM((1, 256), F32)],
        compiler_params=_params(40, dimension_semantics=_seq()),
    )(*_hbm(dqs, dks, dvs, dqg, dkg, dvg, drg, dz, dpre1, w_in_t, wg2_p))


def _ln_in_bwd(x, meta_ext, dh0, g, token):
    tr = min(LN_ROWS, SEQ)

    def ln_bwd(x_ref, dh_ref, g_ref, dx_ref, dg_ref, db_ref):
        @pl.when(pl.program_id(0) == 0)
        def _():
            dg_ref[...] = jnp.zeros_like(dg_ref)
            db_ref[...] = jnp.zeros_like(db_ref)

        xhat, rstd = _ln_stats(x_ref[...])
        dh = dh_ref[...]
        dx_ref[...] = _ln_bwd(dh, xhat, rstd, g_ref[...])
        dg_ref[...] += jnp.sum(dh * xhat, axis=0, keepdims=True)
        db_ref[...] += jnp.sum(dh, axis=0, keepdims=True)

    def body(x_ref, dh_ref, g_ref, token_ref, dx_ref, dg_ref, db_ref):
        ln_bwd(x_ref, dh_ref, g_ref, dx_ref, dg_ref, db_ref)

    def meta_body(m_ref, dh_ref, g_ref, dm_ref, dg_ref, db_ref):
        ln_bwd(m_ref, dh_ref, g_ref, dm_ref, dg_ref, db_ref)

    sums = [pltpu.HBM((1, D), F32), pltpu.HBM((1, D), F32)]
    dx, dg, db = pl.pallas_call(
        body, name="ln_in_bwd", grid=(SEQ // tr,),
        in_specs=[_rows(tr, D), _rows(tr, D), _const((1, D)), _const(TOKEN)],
        out_specs=[_rows(tr, D), _acc((1, D)), _acc((1, D))],
        out_shape=[pltpu.HBM((SEQ, D), F32)] + sums,
        compiler_params=_params(32, dimension_semantics=_seq()),
    )(*_hbm(x, dh0, g), token)
    dm, dg_m, db_m = pl.pallas_call(
        meta_body, name="ln_in_bwd_meta", grid=(1,),
        in_specs=[_const((BLK, D)), pl.BlockSpec((BLK, D), lambda i: (SEQ // BLK, 0)), _const((1, D))],
        out_specs=[_acc((BLK, D)), _acc((1, D)), _acc((1, D))],
        out_shape=[pltpu.HBM((BLK, D), F32)] + sums,
        compiler_params=_params(16, dimension_semantics=_seq()),
    )(*_hbm(meta_ext, dh0, g))
    return dx, dm, dg + dg_m, db + db_m


def _local_step(x, target, ln_in_g, ln_in_b, b_in, bg2, sinks, gn, g1, b1, g2, b2,
                token, fetch_first, fetch_rest, exchange_ffn, ship_ffn, ship_w_in):
    row = lambda v: v.reshape(1, -1).astype(F32)
    b_in_p = jnp.pad(row(b_in), ((0, 0), (0, D_IN_P - D_IN)))
    gn4 = row(gn)
    sinks = sinks.reshape(-1).astype(F32)

    h_real = _ln_in_fwd_real(x, row(ln_in_g), row(ln_in_b), token)
    w_in_t, meta_full, wg2 = fetch_first([h_real])
    meta_ext = jnp.pad(meta_full, ((META_OFF, BLK - CH), (0, 0)))
    wg2_p = jnp.pad(wg2, ((0, LANE - wg2.shape[0]), (0, 0))).astype(BF16)
    h0 = _ln_in_fwd_meta(h_real, meta_ext, row(ln_in_g), row(ln_in_b))
    qs, ks, vs, qg, kg, vg, rg, glr, z = _in_proj(h0, w_in_t, b_in_p, wg2_p, row(bg2))
    o_s = _swa_fwd(sinks, qs, ks, vs)
    o_gla, st_all = _gla_fwd(qg, kg, vg, z)
    w_out, wg_t, wu_t, wd = fetch_rest([o_s, o_gla])
    o, pre1, h1 = _post_mix(o_s, o_gla, rg, h0, gn4, w_out, row(g1), row(b1))
    a, dgate, dup, dpre2, loss, dg2, db2 = _ffn_fwd_loss_bwd(h1, wg_t, wu_t, wd, target, row(g2), row(b2))
    dpre1, dg1, db1, do_s, do_gla, drg, dgn = _ffn_out_bwd(dpre2, dgate, dup, pre1, wg_t, wu_t, row(g1), w_out, o_gla,
                                                           rg, gn4)
    dwd = _atb(a, dpre2, "dw_down")
    dwg_t = _atb(dgate, h1, "dw_gate")
    dwu_t = _atb(dup, h1, "dw_up")
    token = exchange_ffn(dict(w_g=dwg_t, w_u=dwu_t, w_d=dwd))
    token = ship_ffn(_atb(o, dpre1, "dw_out", token))
    dqg, dkg, dvg, dz = _gla_bwd(qg, kg, vg, z, do_gla, st_all, token)
    dqs, dks, dvs, dsinks = _swa_bwd(sinks, qs, ks, vs, do_s, token)
    dproj, dh0, db_in_p, dbg2 = _in_bwd(dqs, dks, dvs, dqg, dkg, dvg, drg, dz, dpre1, w_in_t, wg2_p)
    token = ship_w_in(_atb(dproj, h0, "dw_in"))
    dwg2_p = _atb(glr, dz, "dw_gate_lr2")
    dx, dmeta_blk, dg_in, db_in_ln = _ln_in_bwd(x, meta_ext, dh0, row(ln_in_g), token)

    small = dict(meta_blk=dmeta_blk, ln_in_g=dg_in, ln_in_b=db_in_ln, ln1_g=dg1, ln1_b=db1, ln2_g=dg2, ln2_b=db2,
                 b_in_p=db_in_p, wg2_p=dwg2_p, bg2=dbg2, sinks=dsinks, gn=dgn, loss=loss)
    return dx, small


HBM = pl.BlockSpec(memory_space=pltpu.HBM)


def _place():
    return lax.axis_index("x"), lax.axis_index("y"), lax.axis_index("c")


def _other_chips(x, y):
    return [(1 - x, y), (x, 1 - y), (1 - x, 1 - y)]


def _dma_sems(n):
    return pltpu.SemaphoreType.DMA((n,))


def _comm_params():
    return pltpu.CompilerParams(has_side_effects=True)


SEM = pl.BlockSpec(memory_space=pltpu.SEMAPHORE)


PER_ARRAY = dict(gather=3, scatter=3, sibling=N_CHIPS)


def _ici_copies(kind, landing, srcs, lands, send_sems, recv_sems):
    x, y, c = _place()
    mine = 2 * x + y
    copies = []
    for a in range(len(srcs)):
        if kind == "sibling":
            for s in range(N_CHIPS):
                copies.append(pltpu.make_async_remote_copy(
                    srcs[a].at[s, 1 - c], lands[a].at[s], send_sems.at[N_CHIPS * a + s], recv_sems.at[N_CHIPS * a + s],
                    device_id=(x, y, 1 - c), device_id_type=MESH))
            continue
        for j, (px, py) in enumerate(_other_chips(x, y)):
            slab = 2 * px + py if landing else mine
            if kind == "gather":
                src, dst = srcs[a].at[c], lands[a].at[slab, c]
            else:
                src, dst = srcs[a].at[2 * px + py], lands[a].at[slab]
            copies.append(pltpu.make_async_remote_copy(src, dst, send_sems.at[3 * a + j], recv_sems.at[3 * a + j],
                                                       device_id=(px, py, c), device_id_type=MESH))
    return copies


def _split_params():
    return pltpu.CompilerParams(has_side_effects=pltpu.SideEffectType.DATAFLOW_SIDE_EFFECTING)


def _ici_start(kind, srcs, land_shapes, after, name):
    n = len(srcs)
    lands = [pltpu.with_memory_space_constraint(lax.empty(s, a.dtype), pltpu.HBM) for s, a in zip(land_shapes, srcs)]

    def body(*refs):
        outs = refs[2 * n + len(after):]
        for cp in _ici_copies(kind, False, refs[:n], refs[n:2 * n], outs[0], outs[1]):
            cp.start()
        outs[-1][...] = jnp.zeros(TOKEN, F32)

    outs = pl.pallas_call(
        body, name=name, in_specs=[HBM] * (2 * n) + [pl.BlockSpec(memory_space=pl.ANY)] * len(after),
        out_specs=[SEM, SEM] + [HBM] * (2 * n) + [pl.BlockSpec(memory_space=pltpu.VMEM)],
        out_shape=[_dma_sems(PER_ARRAY[kind] * n)] * 2 + [pltpu.HBM(a.shape, a.dtype) for a in list(srcs) + lands]
        + [jax.ShapeDtypeStruct(TOKEN, F32)],
        input_output_aliases={i: 2 + i for i in range(2 * n)},
        compiler_params=_split_params(),
    )(*_hbm(*srcs), *lands, *after)
    return outs[:-1], outs[-1]


def _ici_wait(kind, handle, after, name):
    n = (len(handle) - 2) // 2

    def body(*refs):
        for cp in _ici_copies(kind, True, refs[:n], refs[n:2 * n], refs[2 * n], refs[2 * n + 1]):
            cp.wait_send()
            cp.wait_recv()

    outs = pl.pallas_call(
        body, name=name, in_specs=[HBM] * (2 * n) + [SEM, SEM] + [pl.BlockSpec(memory_space=pl.ANY)] * len(after),
        out_specs=[HBM] * (2 * n), out_shape=[pltpu.HBM(a.shape, a.dtype) for a in handle[2:]],
        input_output_aliases={i: i for i in range(2 * n)},
        compiler_params=_split_params(),
    )(*handle[2:], handle[0], handle[1], *after)
    return list(outs[:n]), list(outs[n:])


def _sibling_forward(lands, name):
    n = len(lands)

    def body(*refs):
        outs = refs[n:2 * n]
        send_sems, recv_sems = refs[2 * n:]
        x, y, c = _place()

        def copy(a, j, half):
            px, py = _other_chips(x, y)[j]
            blk = outs[a].at[2 * px + py, half]
            return pltpu.make_async_remote_copy(blk, blk, send_sems.at[3 * a + j], recv_sems.at[3 * a + j],
                                                device_id=(x, y, 1 - c), device_id_type=MESH)

        pairs = [(a, j) for a in range(n) for j in range(3)]
        for a, j in pairs:
            copy(a, j, c).start()
        for a, j in pairs:
            copy(a, j, 1 - c).wait_recv()
        for a, j in pairs:
            copy(a, j, c).wait_send()

    return pl.pallas_call(
        body, name=name, in_specs=[HBM] * n, out_specs=[HBM] * n,
        out_shape=[pltpu.HBM(a.shape, a.dtype) for a in lands],
        input_output_aliases={a: a for a in range(n)},
        scratch_shapes=[_dma_sems(3 * n)] * 2,
        compiler_params=_comm_params(),
    )(*_hbm(*lands))


def _sibling_exchange(grads, name):
    n = len(grads)

    def body(*refs):
        ins, outs = refs[:n], refs[n:2 * n]
        send_sems, recv_sems = refs[2 * n:]
        x, y, c = _place()
        copies = []
        for a in range(n):
            for s in range(N_CHIPS):
                cp = pltpu.make_async_remote_copy(ins[a].at[s, 1 - c], outs[a].at[s], send_sems.at[N_CHIPS * a + s],
                                                  recv_sems.at[N_CHIPS * a + s], device_id=(x, y, 1 - c),
                                                  device_id_type=MESH)
                cp.start()
                copies.append(cp)
        for cp in copies:
            cp.wait_recv()
        for cp in copies:
            cp.wait_send()

    return pl.pallas_call(
        body, name=name, in_specs=[HBM] * n, out_specs=[HBM] * n,
        out_shape=[pltpu.HBM((N_CHIPS, g.shape[2], D), F32) for g in grads],
        scratch_shapes=[_dma_sems(N_CHIPS * n)] * 2,
        compiler_params=_comm_params(),
    )(*_hbm(*grads))


def _add_halves(core, grads, recvs, dtypes, name):
    n = len(grads)
    heights = [g.shape[2] for g in grads]

    def body(c_ref, *refs):
        for a in range(n):
            refs[2 * n + a][...] = (refs[2 * a][0] + refs[2 * a + 1][...]).astype(dtypes[a])

    slab = lambda h: pl.BlockSpec((1, h, D), lambda s, c: (s, 0, 0))
    mine = lambda h: pl.BlockSpec((1, 1, h, D), lambda s, c: (s, c[0], 0, 0))
    return pl.pallas_call(
        body, name=name,
        grid_spec=pltpu.PrefetchScalarGridSpec(
            num_scalar_prefetch=1, grid=(N_CHIPS,),
            in_specs=[spec(h) for h in heights for spec in (mine, slab)], out_specs=[slab(h) for h in heights]),
        out_shape=[pltpu.HBM((N_CHIPS, h, D), dt) for h, dt in zip(heights, dtypes)],
        compiler_params=_params(32, dimension_semantics=_seq()),
    )(core, *_hbm(*[a for pair in zip(grads, recvs) for a in pair]))


N_DEVICES = 2 * N_CHIPS
PEER_FLIPS = [(dx, dy, dc) for dx in (0, 1) for dy in (0, 1) for dc in (0, 1)][1:]


def _small_exchange(pack, after):
    n = len(PEER_FLIPS)

    def body(p_ref, *refs):
        out_ref, send_sems, recv_sems = refs[len(after):]
        x, y, c = _place()
        flip = lambda v, d: 1 - v if d else v

        def copy(k, landing):
            px, py, pc = (flip(v, d) for v, d in zip((x, y, c), PEER_FLIPS[k]))
            slab = 4 * px + 2 * py + pc if landing else 4 * x + 2 * y + c
            return pltpu.make_async_remote_copy(p_ref, out_ref.at[slab], send_sems.at[k], recv_sems.at[k],
                                                device_id=(px, py, pc), device_id_type=MESH)

        for k in range(n):
            copy(k, False).start()
        for k in range(n):
            copy(k, True).wait_recv()
        for k in range(n):
            copy(k, False).wait_send()

    return pl.pallas_call(
        body, name="small_exchange", in_specs=[HBM] + [pl.BlockSpec(memory_space=pl.ANY)] * len(after), out_specs=HBM,
        out_shape=pltpu.HBM((N_DEVICES,) + pack.shape, F32),
        scratch_shapes=[_dma_sems(n)] * 2,
        compiler_params=_comm_params(),
    )(*_hbm(pack), *after)


def _sum_chips(slots, firsts, rests):
    n = len(firsts)

    def body(i_ref, *refs):
        for a in range(n):
            first, r1, r2, r3 = refs[4 * a:4 * a + 4]
            refs[4 * n + a][...] = ((first[...].astype(F32) + r1[...].astype(F32)) + r2[...].astype(F32)) + r3[...].astype(F32)

    slab = lambda h, k: pl.BlockSpec((1, h, D), lambda i, ix: (ix[k], 0, 0))
    heights = [f.shape[1] for f in firsts]
    return pl.pallas_call(
        body, name="sum_chips",
        grid_spec=pltpu.PrefetchScalarGridSpec(
            num_scalar_prefetch=1, grid=(1,),
            in_specs=[slab(h, k) for h in heights for k in range(4)], out_specs=[slab(h, 4) for h in heights]),
        out_shape=[pltpu.HBM((2, h, D), F32) for h in heights],
        compiler_params=_params(48, dimension_semantics=_seq()),
    )(slots, *_hbm(*[a for f, r in zip(firsts, rests) for a in (f, r, r, r)]))


def _join_halves(halves):
    n = len(halves)

    def body(*refs):
        outs = refs[n:2 * n]
        send_sems, recv_sems = refs[2 * n:]
        x, y, c = _place()

        def copy(a, slab):
            return pltpu.make_async_remote_copy(outs[a].at[slab], outs[a].at[slab], send_sems.at[a], recv_sems.at[a],
                                                device_id=(x, y, 1 - c), device_id_type=MESH)

        for a in range(n):
            copy(a, c).start()
        for a in range(n):
            copy(a, 1 - c).wait_recv()
        for a in range(n):
            copy(a, c).wait_send()

    return pl.pallas_call(
        body, name="join_halves", in_specs=[HBM] * n, out_specs=[HBM] * n,
        out_shape=[pltpu.HBM(h.shape, F32) for h in halves],
        input_output_aliases={a: a for a in range(n)},
        scratch_shapes=[_dma_sems(n)] * 2,
        compiler_params=_comm_params(),
    )(*_hbm(*halves))


def _chip_partials(grads, wire_dtypes, names, fetched=()):
    core = lax.axis_index("c").astype(jnp.int32).reshape(1)
    todo = len(grads) - len(fetched)
    recv = list(_sibling_exchange(grads[:todo], "sibling_exchange_" + names[0])) + list(fetched)
    return list(_add_halves(core, grads, recv, wire_dtypes, "add_halves_" + names[0]))


def _finish_reduce(parts, got):
    x, y, c = _place()
    others = [2 * px + py for px, py in _other_chips(x, y)]
    own_first = jnp.stack([2 * x + y] + others + [c]).astype(jnp.int32)
    return [f.reshape(2 * f.shape[1], D) for f in _join_halves(_sum_chips(own_first, parts, got))]


ADAMW_STEPS = 4


def _adamw(params):
    n = len(params)

    def block(shape):
        rows, cols = shape
        if rows % (8 * ADAMW_STEPS) == 0:
            return pl.BlockSpec((rows // ADAMW_STEPS, cols), lambda i: (i, 0))
        assert cols % (LANE * ADAMW_STEPS) == 0
        return pl.BlockSpec((rows, cols // ADAMW_STEPS), lambda i: (0, i))

    def body(*refs):
        for a in range(n):
            w_ref, g_ref, m_ref, v_ref = refs[4 * a:4 * a + 4]
            outs = refs[4 * n + 3 * a:4 * n + 3 * a + 3]
            outs[0][...], outs[1][...], outs[2][...] = _adamw_math(w_ref[...], g_ref[...], m_ref[...], v_ref[...])

    outs = pl.pallas_call(
        body, name="adamw_matrices", grid=(ADAMW_STEPS,),
        in_specs=[block(p[0].shape) for p in params for _ in range(4)],
        out_specs=[block(p[0].shape) for p in params for _ in range(3)],
        out_shape=[pltpu.HBM(p[0].shape, F32) for p in params for _ in range(3)],
        compiler_params=_params(48, dimension_semantics=_seq()),
    )(*_hbm(*[a for p in params for a in p]))
    return [outs[3 * a:3 * a + 3] for a in range(n)]


def _adamw_math(w, g, m, v):
    nm = ADAM_B1 * m + (1.0 - ADAM_B1) * g
    nv = ADAM_B2 * v + (1.0 - ADAM_B2) * (g * g)
    m_hat = nm / (1.0 - ADAM_B1 ** ADAM_STEP)
    v_hat = nv / (1.0 - ADAM_B2 ** ADAM_STEP)
    return -ADAM_LR * (m_hat / (jnp.sqrt(v_hat) + ADAM_EPS) + ADAM_WD * w), nm, nv


SMALL = (("meta_tokens", (N_META, D // N_CHIPS)), ("ln_in_g", (1, D)), ("ln_in_b", (1, D)), ("b_in", (1, D_IN)),
         ("w_gate_lr2", (GATE_RANK, GLA_HEADS * DK // N_CHIPS)), ("b_gate_lr2", (1, GLA_HEADS * DK)),
         ("attn_sinks", (1, SWA_HEADS)),
         ("gla_norm_g", (1, DV)), ("ln1_g", (1, D)), ("ln1_b", (1, D)), ("ln2_g", (1, D)), ("ln2_b", (1, D)))
ROW_META, ROW_B_IN, ROW_TAIL, ROW_WG2 = 0, 22, 25, 32
ROW_LN = dict(ln_in_g=16, ln_in_b=17, ln1_g=18, ln1_b=19, ln2_g=20, ln2_b=21)
TAIL_BG2, TAIL_SINKS, TAIL_GN, TAIL_LOSS = 0, 256, 256 + SWA_HEADS, 256 + SWA_HEADS + DV


def _adamw_small(place, packs, own, params):
    n = len(SMALL)

    def body(place_ref, packs_ref, own_ref, *refs):
        ins, outs, p_ref = refs[:3 * n], refs[3 * n:-1], refs[-1]
        me, c = place_ref[0], place_ref[1]
        total = jnp.where(me == 0, own_ref[...], packs_ref[0])
        for i in range(1, N_DEVICES):
            total = total + jnp.where(me == i, own_ref[...], packs_ref[i])
        p_ref[...] = total
        outs[4 * n][...] = total[ROW_TAIL:ROW_TAIL + 1, :]

        def mine(width, rows):
            part = lambda s: p_ref[rows, s * width:(s + 1) * width]
            return jnp.where(c == 0, part(0), jnp.where(c == 1, part(1), jnp.where(c == 2, part(2), part(3))))

        tail = lambda lo, width: p_ref[ROW_TAIL:ROW_TAIL + 1, lo:lo + width]
        grads = dict(
            meta_tokens=mine(D // N_CHIPS, slice(ROW_META, ROW_META + N_META)),
            b_in=jnp.concatenate([p_ref[ROW_B_IN:ROW_B_IN + 1, :], p_ref[ROW_B_IN + 1:ROW_B_IN + 2, :],
                                  p_ref[ROW_B_IN + 2:ROW_B_IN + 3, 0:D_IN - 2 * D]], axis=1),
            w_gate_lr2=mine(256 // N_CHIPS, slice(ROW_WG2, ROW_WG2 + 16)),
            b_gate_lr2=tail(TAIL_BG2, 256), attn_sinks=tail(TAIL_SINKS, SWA_HEADS), gla_norm_g=tail(TAIL_GN, DV),
            **{k: p_ref[r:r + 1, :] for k, r in ROW_LN.items()})
        for i, (name, _) in enumerate(SMALL):
            g = grads[name]
            outs[4 * i][...] = g
            outs[4 * i + 1][...], outs[4 * i + 2][...], outs[4 * i + 3][...] = _adamw_math(
                ins[3 * i][...], g, ins[3 * i + 1][...], ins[3 * i + 2][...])

    whole = lambda shape: pl.BlockSpec(shape, lambda i, c: (0,) * len(shape))
    outs = pl.pallas_call(
        body, name="adamw_small",
        grid_spec=pltpu.PrefetchScalarGridSpec(
            num_scalar_prefetch=1, grid=(1,),
            in_specs=[whole(packs.shape), whole(own.shape)] + [whole(s) for _, s in SMALL for _ in range(3)],
            out_specs=[whole(s) for _, s in SMALL for _ in range(4)] + [whole((1, D))],
            scratch_shapes=[pltpu.VMEM(own.shape, F32)]),
        out_shape=[pltpu.HBM(s, F32) for _, s in SMALL for _ in range(4)] + [pltpu.HBM((1, D), F32)],
        compiler_params=_params(16, dimension_semantics=_seq()),
    )(place, *_hbm(packs, own, *[a for p in params for a in p]))
    return [outs[4 * i:4 * i + 4] for i in range(n)], outs[4 * n]


def _small_pack(gr):
    names = ["meta_blk"] + list(ROW_LN) + ["b_in_p", "wg2_p", "bg2", "sinks", "gn", "loss"]
    gate_w = GLA_HEADS * DK

    def body(*refs):
        src, out = dict(zip(names, refs)), refs[-1]
        out[...] = jnp.zeros_like(out)
        out[ROW_META:ROW_META + N_META, :] = src["meta_blk"][META_OFF:CH, :]
        for k, r in ROW_LN.items():
            out[r:r + 1, :] = src[k][...]
        for j in range(-(-D_IN // D)):
            width = min(D, D_IN - j * D)
            out[ROW_B_IN + j:ROW_B_IN + j + 1, 0:width] = src["b_in_p"][:, j * D:j * D + width]
        tail = slice(ROW_TAIL, ROW_TAIL + 1)
        out[tail, TAIL_BG2:TAIL_BG2 + gate_w] = src["bg2"][...]
        out[tail, TAIL_SINKS:TAIL_SINKS + SWA_HEADS] = src["sinks"][:, 0:SWA_HEADS]
        out[tail, TAIL_GN:TAIL_GN + DV] = src["gn"][...]
        out[tail, TAIL_LOSS:TAIL_LOSS + 1] = src["loss"][:, 0:1]
        out[ROW_WG2:ROW_WG2 + GATE_RANK, 0:gate_w] = src["wg2_p"][0:GATE_RANK, :]

    arrays = [gr[k] for k in names]
    return pl.pallas_call(
        body, name="small_pack", grid=(1,),
        in_specs=[_acc(a.shape) for a in arrays], out_specs=_acc((SMALL_ROWS, D)),
        out_shape=pltpu.HBM((SMALL_ROWS, D), F32),
        compiler_params=_params(16, dimension_semantics=_seq()),
    )(*_hbm(*arrays))


BIG = ("w_in", "w_out", "w_g", "w_u", "w_d")


def kernel(x, meta_tokens, ln_in_g, ln_in_b, w_in, b_in, w_gate_lr2, b_gate_lr2, attn_sinks, gla_norm_g, w_out, ln1_g, ln1_b, w_ffn_gate, w_ffn_up, w_ffn_down, ln2_g, ln2_b, loss_target, m_meta_tokens, m_ln_in_g, m_ln_in_b, m_w_in, m_b_in, m_w_gate_lr2, m_b_gate_lr2, m_attn_sinks, m_gla_norm_g, m_w_out, m_ln1_g, m_ln1_b, m_w_ffn_gate, m_w_ffn_up, m_w_ffn_down, m_ln2_g, m_ln2_b, v_meta_tokens, v_ln_in_g, v_ln_in_b, v_w_in, v_b_in, v_w_gate_lr2, v_b_gate_lr2, v_attn_sinks, v_gla_norm_g, v_w_out, v_ln1_g, v_ln1_b, v_w_ffn_gate, v_w_ffn_up, v_w_ffn_down, v_ln2_g, v_ln2_b):
    chip = 2 * lax.axis_index("x") + lax.axis_index("y")

    halves = lambda a: a.reshape(2, a.shape[0] // 2, a.shape[1])
    r_in = SHARD_ROWS["w_in"]
    first = [halves(a) for a in (jnp.pad(w_in[0].T.astype(BF16), ((0, W_IN_WIN - r_in), (0, 0))), meta_tokens,
                                 w_gate_lr2[0])]
    rest = [halves(a) for a in (w_out[0].astype(BF16), w_ffn_gate[0].T.astype(BF16), w_ffn_up[0].T.astype(BF16),
                                w_ffn_down[0].astype(BF16))]
    lands = lambda arrs: [(N_CHIPS,) + a.shape for a in arrs]
    first_handle, first_token = _ici_start("gather", first, lands(first), [], "gather_first_start")
    rest_handle, token = _ici_start("gather", rest, lands(rest), [first_token], "gather_rest_start")

    def fetch(handle, after, name):
        shards, landed = _ici_wait("gather", handle, after, name + "_wait")
        got = _sibling_forward(landed, name + "_forward")
        return [lax.dynamic_update_index_in_dim(g, s, chip, axis=0) for g, s in zip(got, shards)]

    def fetch_first(after):
        g_in, g_meta, g_wg2 = fetch(first_handle, after, "gather_first")
        w_in_t = jnp.pad(g_in.reshape(N_CHIPS, W_IN_WIN, D)[:, :r_in].reshape(D_IN, D), ((0, D_IN_P - D_IN), (0, 0)))
        meta_full = jnp.concatenate([g_meta[s].reshape(N_META, -1) for s in range(N_CHIPS)], axis=1)
        wg2_full = jnp.concatenate([g_wg2[s].reshape(w_gate_lr2.shape[1], -1) for s in range(N_CHIPS)], axis=1)
        return w_in_t, meta_full, wg2_full

    def fetch_rest(after):
        return [g.reshape(-1, D) for g in fetch(rest_handle, after, "gather_rest")]

    sent = {}
    split = lambda grads: [g.reshape(N_CHIPS, 2, -1, D) for g in grads]

    def ship(key, grads, names, fetched=()):
        parts = _chip_partials(grads, [BF16] * len(grads), names, fetched)
        sent[key], ship_token = _ici_start("scatter", parts, [p.shape for p in parts], [], "scatter_" + key + "_start")
        return ship_token

    def exchange_ffn(g):
        grads = split([g[k] for k in BIG[2:]])
        sent["ffn_halves"], exchange_token = _ici_start("sibling", grads, [(N_CHIPS,) + a.shape[2:] for a in grads], [],
                                                        "sibling_ffn_start")
        return exchange_token

    def ship_ffn(dw_out):
        grads, fetched = _ici_wait("sibling", sent["ffn_halves"], [dw_out], "sibling_ffn_wait")
        return ship("ffn", split([dw_out]) + grads, list(BIG[1:]), fetched)

    def ship_w_in(dw_in_t):
        win_start = [s * r_in // BF16_ROWS * BF16_ROWS for s in range(N_CHIPS)]
        return ship("w_in", split([jnp.stack([dw_in_t[st:st + W_IN_WIN] for st in win_start])]), ["w_in"])

    dx, gr = _local_step(
        x[0], loss_target[0], ln_in_g, ln_in_b, b_in[0], b_gate_lr2[0], attn_sinks[0], gla_norm_g[0], ln1_g[0],
        ln1_b[0], ln2_g[0], ln2_b[0], token, fetch_first, fetch_rest, exchange_ffn, ship_ffn, ship_w_in)
    ffn_parts, ffn_got = _ici_wait("scatter", sent["ffn"], [dx], "scatter_ffn_wait")
    w_in_parts, w_in_got = _ici_wait("scatter", sent["w_in"], [dx], "scatter_w_in_wait")

    small_own = _small_pack(gr)
    small_all = _small_exchange(small_own, [w_in_got[0]])
    red = _finish_reduce(w_in_parts + ffn_parts, w_in_got + ffn_got)

    big_g = dict(zip(BIG, red))
    big_g["w_in"] = lax.dynamic_slice_in_dim(red[0], chip * (r_in % BF16_ROWS), r_in, axis=0)
    grads = dict(w_in=big_g["w_in"].T[None], w_out=big_g["w_out"][None], w_ffn_gate=big_g["w_g"].T[None],
                 w_ffn_up=big_g["w_u"].T[None], w_ffn_down=big_g["w_d"][None])
    weights = dict(meta_tokens=meta_tokens, ln_in_g=ln_in_g, ln_in_b=ln_in_b, w_in=w_in, b_in=b_in,
                   w_gate_lr2=w_gate_lr2, b_gate_lr2=b_gate_lr2, attn_sinks=attn_sinks, gla_norm_g=gla_norm_g,
                   w_out=w_out, ln1_g=ln1_g, ln1_b=ln1_b, w_ffn_gate=w_ffn_gate, w_ffn_up=w_ffn_up,
                   w_ffn_down=w_ffn_down, ln2_g=ln2_g, ln2_b=ln2_b)
    m_in = dict(meta_tokens=m_meta_tokens, ln_in_g=m_ln_in_g, ln_in_b=m_ln_in_b, w_in=m_w_in, b_in=m_b_in,
                w_gate_lr2=m_w_gate_lr2, b_gate_lr2=m_b_gate_lr2, attn_sinks=m_attn_sinks, gla_norm_g=m_gla_norm_g,
                w_out=m_w_out, ln1_g=m_ln1_g, ln1_b=m_ln1_b, w_ffn_gate=m_w_ffn_gate, w_ffn_up=m_w_ffn_up,
                w_ffn_down=m_w_ffn_down, ln2_g=m_ln2_g, ln2_b=m_ln2_b)
    v_in = dict(meta_tokens=v_meta_tokens, ln_in_g=v_ln_in_g, ln_in_b=v_ln_in_b, w_in=v_w_in, b_in=v_b_in,
                w_gate_lr2=v_w_gate_lr2, b_gate_lr2=v_b_gate_lr2, attn_sinks=v_attn_sinks, gla_norm_g=v_gla_norm_g,
                w_out=v_w_out, ln1_g=v_ln1_g, ln1_b=v_ln1_b, w_ffn_gate=v_w_ffn_gate, w_ffn_up=v_w_ffn_up,
                w_ffn_down=v_w_ffn_down, ln2_g=v_ln2_g, ln2_b=v_ln2_b)
    names = list(weights)
    big_names = ("w_in", "w_out", "w_ffn_gate", "w_ffn_up", "w_ffn_down")

    delta, new_m, new_v = {}, {}, {}
    flips = [(lambda a: a.T) if kk in ("w_in", "w_g", "w_u") else (lambda a: a) for kk in BIG]
    updated = _adamw([(flip(weights[k][0]), big_g[kk], flip(m_in[k][0]), flip(v_in[k][0]))
                      for k, kk, flip in zip(big_names, BIG, flips)])
    for k, flip, results in zip(big_names, flips, updated):
        delta[k], new_m[k], new_v[k] = (flip(t)[None] for t in results)
    small_in = [tuple(src[k].reshape(shape) for src in (weights, m_in, v_in)) for k, shape in SMALL]
    place = jnp.stack([2 * chip + lax.axis_index("c"), chip]).astype(jnp.int32)
    small_out, tail_row = _adamw_small(place, small_all, small_own, small_in)
    for (k, _), results in zip(SMALL, small_out):
        grads[k], delta[k], new_m[k], new_v[k] = (r.reshape(weights[k].shape) for r in results)

    return (tail_row[0, TAIL_LOSS], dx[None], *[grads[k] for k in names], *[delta[k] for k in names], *[new_m[k] for k in names],
            *[new_v[k] for k in names])
```

```python
import jax
import jax.numpy as jnp
from jax import lax
from jax.experimental import pallas as pl
from jax.experimental.pallas import tpu as pltpu

F32 = jnp.float32
BF16 = jnp.bfloat16
MESH = pl.DeviceIdType.MESH

D = 1024
SEQ = 4096
N_META = 16
SWA_HEADS, SWA_KV_HEADS, DH = 8, 2, 64
WINDOW = 128
GLA_HEADS, DK, DV = 4, 64, 128
GLA_TAU = 16.0
CH = 64
D_FF = 2816
D_IN = 2320
LN_EPS = 1e-5
RMS_EPS = 1e-6
ALPHA = 2.0 ** 0.25
NEG = -1e30
ADAM_LR, ADAM_B1, ADAM_B2, ADAM_EPS, ADAM_WD, ADAM_STEP = 0.001, 0.9, 0.999, 1e-8, 0.01, 10
O_QS, O_KS, O_VS, O_QG, O_KG, O_VG, O_RG, O_LR = 0, 512, 640, 768, 1024, 1280, 1792, 2304

LANE = 128
BLK = WINDOW
GATE_RANK = 16
D_IN_P = D_IN + LANE - GATE_RANK
META_OFF = CH - N_META
HEAD_POS = (0, 4, 1, 5, 2, 6, 3, 7)
LN_ROWS = 512
TOKEN = (8, LANE)
N_CHIPS = 4
SHARD_ROWS = dict(w_in=D_IN // N_CHIPS, w_out=D // N_CHIPS, w_g=D_FF // N_CHIPS, w_u=D_FF // N_CHIPS,
                  w_d=D_FF // N_CHIPS)
SMALL_ROWS = 48
BF16_ROWS = 16
W_IN_WIN = -(-SHARD_ROWS["w_in"] // (2 * BF16_ROWS)) * 2 * BF16_ROWS
VMEM_CAP_MB = 64
VMEM_SPARE_MB = 6


def _lp():
    return SEQ + BLK


def _row_tile(cap):
    lp = _lp()
    return max(t for t in range(16, cap + 1, 16) if lp % t == 0)


def _params(vmem_mb, **kw):
    assert vmem_mb <= VMEM_CAP_MB - VMEM_SPARE_MB
    return pltpu.CompilerParams(vmem_limit_bytes=vmem_mb << 20, **kw)


def _seq(n=1):
    return ("arbitrary",) * n


def _const(shape):
    return pl.BlockSpec(shape, lambda *_: (0,) * len(shape), pipeline_mode=pl.Buffered(1))


def _acc(shape):
    return pl.BlockSpec(shape, lambda *_: (0,) * len(shape))


def _rows(tm, width):
    return pl.BlockSpec((tm, width), lambda i: (i, 0))


def _dot(a, b):
    return jnp.dot(a.astype(BF16), b.astype(BF16), preferred_element_type=F32)


def _dot_nt(a, b):
    return lax.dot_general(a.astype(BF16), b.astype(BF16), (((1,), (1,)), ((), ())), preferred_element_type=F32)


def _dot_tn(a, b):
    return lax.dot_general(a.astype(BF16), b.astype(BF16), (((0,), (0,)), ((), ())), preferred_element_type=F32)


def _dot_exact(a, b):
    return jnp.dot(a, b, precision=lax.Precision.HIGHEST, preferred_element_type=F32)


def _ln_stats(x):
    mu = jnp.mean(x, axis=-1, keepdims=True)
    xc = x - mu
    rstd = lax.rsqrt(jnp.mean(xc * xc, axis=-1, keepdims=True) + LN_EPS)
    return xc * rstd, rstd


def _ln_bwd(dy, xhat, rstd, g):
    dxh = dy * g
    return rstd * (dxh - jnp.mean(dxh, axis=-1, keepdims=True) - xhat * jnp.mean(dxh * xhat, axis=-1, keepdims=True))


def _sigmoid(x):
    return 1.0 / (1.0 + jnp.exp(-x))


def _iota(shape, dim):
    return lax.broadcasted_iota(jnp.int32, shape, dim)


def _hbm(*arrays):
    return tuple(pltpu.with_memory_space_constraint(a, pltpu.HBM) for a in arrays)


def _ln_in_fwd_real(x, g, b, token):
    tr = min(LN_ROWS, SEQ)

    def body(x_ref, g_ref, b_ref, token_ref, h_ref):
        xhat, _ = _ln_stats(x_ref[...])
        h_ref[...] = xhat * g_ref[...] + b_ref[...]

    return pl.pallas_call(
        body, name="ln_in_fwd", grid=(SEQ // tr,),
        in_specs=[_rows(tr, D), _const((1, D)), _const((1, D)), _const(TOKEN)],
        out_specs=_rows(tr, D),
        out_shape=pltpu.HBM((_lp(), D), F32),
        compiler_params=_params(32, dimension_semantics=_seq()),
    )(*_hbm(x, g, b), token)


def _ln_in_fwd_meta(h_real, meta_ext, g, b):
    def meta_body(m_ref, g_ref, b_ref, real_ref, h_ref):
        xhat, _ = _ln_stats(m_ref[...])
        h_ref[...] = xhat * g_ref[...] + b_ref[...]

    return pl.pallas_call(
        meta_body, name="ln_in_fwd_meta", grid=(1,),
        in_specs=[_const((BLK, D)), _const((1, D)), _const((1, D)), pl.BlockSpec(memory_space=pl.ANY)],
        out_specs=pl.BlockSpec((BLK, D), lambda i: (SEQ // BLK, 0)),
        out_shape=pltpu.HBM((_lp(), D), F32),
        input_output_aliases={3: 0},
        compiler_params=_params(16, dimension_semantics=_seq()),
    )(*_hbm(meta_ext, g, b, h_real))


def _in_proj(h0, w_in_t, b_in_p, wg2_p, bg2):
    tm = _row_tile(384)
    lp = _lp()
    widths = (512, 128, 128, 256, 256, 512, 512, 128)
    offs = (O_QS, O_KS, O_VS, O_QG, O_KG, O_VG, O_RG, O_LR)

    def body(h_ref, w_ref, b_ref, wg2_ref, bg2_ref, *outs):
        proj = _dot_nt(h_ref[...], w_ref[...]) + b_ref[...]
        for pos, h in enumerate(HEAD_POS):
            outs[0][:, pos * DH:(pos + 1) * DH] = proj[:, O_QS + h * DH:O_QS + (h + 1) * DH]
        for o_ref, off, wd in zip(outs[1:8], offs[1:], widths[1:]):
            o_ref[...] = proj[:, off:off + wd]
        outs[8][...] = _dot(proj[:, O_LR:O_LR + LANE], wg2_ref[...]) + bg2_ref[...]

    return pl.pallas_call(
        body, name="in_proj", grid=(lp // tm,),
        in_specs=[_rows(tm, D), _const((D_IN_P, D)), _const((1, D_IN_P)), _const((LANE, 256)), _const((1, 256))],
        out_specs=[_rows(tm, w) for w in widths] + [_rows(tm, 256)],
        out_shape=[pltpu.HBM((lp, w), F32) for w in widths] + [pltpu.HBM((lp, 256), F32)],
        compiler_params=_params(40, dimension_semantics=_seq()),
    )(*_hbm(h0, w_in_t, b_in_p, wg2_p, bg2))


def _swa_masks(n):
    nb = SEQ // BLK
    is_meta = n == nb
    ri = _iota((BLK, BLK), 0)
    cj = _iota((BLK, BLK), 1)
    meta_col = ((cj >= META_OFF) & (cj < CH)).astype(jnp.int32)
    meta_q = meta_col * ((cj <= ri) & (ri < CH)).astype(jnp.int32)
    valid_m = jnp.where(is_meta, meta_q, meta_col) > 0
    dist_m = jnp.where(is_meta, ri - cj, n * BLK + ri + CH - cj).astype(F32)
    valid_p = jnp.where((n >= 1) & (n < nb), (cj > ri).astype(jnp.int32), 0) > 0
    dist_p = (ri + BLK - cj).astype(F32)
    valid_c = jnp.where(n < nb, (cj <= ri).astype(jnp.int32), 0) > 0
    dist_c = (ri - cj).astype(F32)
    return (dist_m, dist_p, dist_c), (valid_m, valid_p, valid_c)


def _swa_bias(n):
    dists, valids = _swa_masks(n)
    return (jnp.concatenate([-d for d in dists], axis=1),
            jnp.concatenate([jnp.where(v, 0.0, NEG) for v in valids], axis=1))


def _swa_half(ref, pos, scale=1.0):
    col = ref[:, (pos // 2) * LANE:(pos // 2 + 1) * LANE]
    lane = _iota((BLK, LANE), 1)
    mine = lane < DH if pos % 2 == 0 else lane >= DH
    return jnp.where(mine, col * scale, 0.0).astype(BF16)


def _swa_merge(even, odd):
    return jnp.where(_iota((BLK, LANE), 1) < DH, even, odd)


def _swa_softmax(t, sink):
    m = jnp.maximum(jnp.max(t, axis=-1, keepdims=True), sink)
    e = jnp.exp(t - m)
    e_sink = jnp.exp(sink - m)
    inv = 1.0 / (jnp.sum(e, axis=-1, keepdims=True) + e_sink)
    return e * inv, e_sink * inv


def _swa_kv_specs(width):
    nb = SEQ // BLK
    return [pl.BlockSpec((BLK, width), lambda n: (nb, 0)),
            pl.BlockSpec((BLK, width), lambda n: (jnp.clip(n - 1, 0, nb - 1), 0)),
            pl.BlockSpec((BLK, width), lambda n: (jnp.minimum(n, nb), 0))]


def _swa_fwd(sinks, qs, ks, vs):
    nb = SEQ // BLK
    heads = range(SWA_HEADS)

    def body(sink_ref, q_ref, km_ref, kp_ref, kc_ref, vm_ref, vp_ref, vc_ref, o_ref):
        negdist, maskbias = _swa_bias(pl.program_id(0))
        k_all = jnp.concatenate([km_ref[...], kp_ref[...], kc_ref[...]], axis=0).astype(BF16)
        v_all = jnp.concatenate([vm_ref[...], vp_ref[...], vc_ref[...]], axis=0).astype(BF16)
        q = [_swa_half(q_ref, pos, DH ** -0.5) for pos in heads]
        t = [_dot_nt(q[pos], k_all) + (2.0 ** -(HEAD_POS[pos] + 1) * negdist + maskbias) for pos in heads]
        p = [_swa_softmax(t[pos], sink_ref[HEAD_POS[pos]])[0].astype(BF16) for pos in heads]
        o = [_dot(p[pos], v_all) for pos in heads]
        for col in range(SWA_HEADS // 2):
            o_ref[:, col * LANE:(col + 1) * LANE] = _swa_merge(o[2 * col], o[2 * col + 1])

    kvw = SWA_KV_HEADS * DH
    return pl.pallas_call(
        body, name="swa_fwd", grid=(nb + 1,),
        in_specs=[pl.BlockSpec(memory_space=pltpu.SMEM), _rows(BLK, SWA_HEADS * DH)] + _swa_kv_specs(kvw) + _swa_kv_specs(kvw),
        out_specs=_rows(BLK, SWA_HEADS * DH),
        out_shape=pltpu.HBM((_lp(), SWA_HEADS * DH), F32),
        compiler_params=_params(16, dimension_semantics=_seq()),
    )(sinks, *_hbm(qs, ks, ks, ks, vs, vs, vs))


GLA_PER_STEP = BLK // CH


def _gla_block(s):
    nb = SEQ // BLK
    return jnp.where(s == 0, nb, s - 1)


def _gla_rowmask(s):
    ri = _iota((BLK, 1), 0)
    m = jnp.where(s == 0, ((ri >= META_OFF) & (ri < CH)).astype(jnp.int32), 1)
    return (m > 0).astype(F32) + jnp.zeros((BLK, 1), F32)


def _gla_chunk_masks():
    r, c = _iota((BLK, BLK), 0), _iota((BLK, BLK), 1)
    same = ((r < CH) & (c < CH)) | ((r >= CH) & (c >= CH))
    return same & (r >= c), same & (r <= c), same


def _gla_decay(z, rmask):
    log_g = (jnp.minimum(z, 0.0) - jnp.log1p(jnp.exp(-jnp.abs(z)))) * (rmask / GLA_TAU)
    lower, _, same = _gla_chunk_masks()
    return _dot_exact(lower.astype(F32), log_g), _dot_exact(same.astype(F32), log_g)


def _gla_slices(c, h):
    return slice(c * CH, (c + 1) * CH), slice(h * DK, (h + 1) * DK), slice(h * DV, (h + 1) * DV)


def _gla_fwd(qg, kg, vg, z):
    steps = SEQ // BLK + 1
    kw, vw = GLA_HEADS * DK, GLA_HEADS * DV
    pairs = [(c, h) for c in range(GLA_PER_STEP) for h in range(GLA_HEADS)]

    def body(q_ref, k_ref, v_ref, z_ref, o_ref, st_ref, st):
        s = pl.program_id(0)

        @pl.when(s == 0)
        def _():
            st[...] = jnp.zeros_like(st)

        rmask = _gla_rowmask(s)
        b, b_last = _gla_decay(z_ref[...], rmask)
        q = q_ref[...] * (rmask * DK ** -0.5)
        k = k_ref[...] * rmask
        v = v_ref[...] * rmask
        qe = q * jnp.exp(b)
        ke = k * jnp.exp(-b)
        kd = k * jnp.exp(b_last - b)
        e_last = jnp.exp(b_last)
        causal = _iota((CH, CH), 0) >= _iota((CH, CH), 1)
        a, upd, intra = {}, {}, {}
        for c, h in pairs:
            rows, ks, vs_ = _gla_slices(c, h)
            a[c, h] = jnp.where(causal, _dot_nt(qe[rows, ks], ke[rows, ks]), 0.0)
            upd[c, h] = _dot_tn(v[rows, vs_], kd[rows, ks])
        for c, h in pairs:
            rows, ks, vs_ = _gla_slices(c, h)
            intra[c, h] = _dot(a[c, h], v[rows, vs_])
        state = st[...]
        for c in range(GLA_PER_STEP):
            st_ref[0, c] = state
            for h in range(GLA_HEADS):
                rows, ks, vs_ = _gla_slices(c, h)
                o_ref[rows, vs_] = intra[c, h] + _dot_nt(qe[rows, ks], state[:, ks])
            state = state * e_last[c * CH:c * CH + 1] + jnp.concatenate([upd[c, h] for h in range(GLA_HEADS)], axis=1)
        st[...] = state

    blk = lambda w: pl.BlockSpec((BLK, w), lambda s: (_gla_block(s), 0))
    return pl.pallas_call(
        body, name="gla_fwd", grid=(steps,),
        in_specs=[blk(kw), blk(kw), blk(vw), blk(kw)],
        out_specs=[blk(vw), pl.BlockSpec((1, GLA_PER_STEP, DV, kw), lambda s: (s, 0, 0, 0))],
        out_shape=[pltpu.HBM((_lp(), vw), F32), pltpu.HBM((steps, GLA_PER_STEP, DV, kw), F32)],
        scratch_shapes=[pltpu.VMEM((DV, kw), F32)],
        compiler_params=_params(16, dimension_semantics=_seq()),
    )(*_hbm(qg, kg, vg, z))


def _post_mix(o_s, o_gla, r_g, h0, gn4, w_out, g1, b1, token):
    tm = _row_tile(384)
    lp = _lp()

    def body(os_ref, og_ref, r_ref, h0_ref, gn_ref, w_ref, g_ref, b_ref, token_ref, o_ref, pre_ref, h1_ref):
        for pos, h in enumerate(HEAD_POS):
            o_ref[:, h * DH:(h + 1) * DH] = os_ref[:, pos * DH:(pos + 1) * DH].astype(BF16)
        for h in range(GLA_HEADS):
            hs = slice(h * DV, (h + 1) * DV)
            xg = og_ref[:, hs]
            n = xg * lax.rsqrt(jnp.mean(xg * xg, axis=-1, keepdims=True) + RMS_EPS) * gn_ref[...]
            r = r_ref[:, hs]
            o_ref[:, 512 + h * DV:512 + (h + 1) * DV] = (n * (r * _sigmoid(r))).astype(BF16)
        pre = ALPHA * h0_ref[...] + _dot(o_ref[...], w_ref[...])
        pre_ref[...] = pre
        xhat, _ = _ln_stats(pre)
        h1_ref[...] = xhat * g_ref[...] + b_ref[...]

    return pl.pallas_call(
        body, name="post_mix", grid=(lp // tm,),
        in_specs=[_rows(tm, 512), _rows(tm, 512), _rows(tm, 512), _rows(tm, D), _const((1, DV)), _const((D, D)),
                  _const((1, D)), _const((1, D)), _const(TOKEN)],
        out_specs=[_rows(tm, D), _rows(tm, D), _rows(tm, D)],
        out_shape=[pltpu.HBM((lp, D), BF16), pltpu.HBM((lp, D), F32),
                   pltpu.HBM((lp, D), F32)],
        compiler_params=_params(32, dimension_semantics=_seq()),
    )(*_hbm(o_s, o_gla, r_g, h0, gn4, w_out, g1, b1), token)


def _ffn_fwd_loss_bwd(h1, wg_t, wu_t, wd, target, g2, b2):
    lp = _lp()
    tm = max(t for t in range(BLK, 384 + 1, BLK) if lp % t == 0)
    steps = lp // tm
    last_blk = SEQ // BLK - 1
    half = D_FF // 2
    n_t = tm // BLK

    def body(*refs):
        h_ref, wg_ref, wu_ref, wd_ref = refs[:4]
        t_refs = refs[4:4 + n_t]
        g2_ref, b2_ref, a_ref, dgate_ref, dup_ref, dp_ref, loss_ref, dg_ref, db_ref, g_s, u_s, acc = refs[4 + n_t:]
        i = pl.program_id(0)

        @pl.when(i == 0)
        def _():
            acc[...] = jnp.zeros_like(acc)
            dg_ref[...] = jnp.zeros_like(dg_ref)
            db_ref[...] = jnp.zeros_like(db_ref)

        h = h_ref[...]
        hb = h.astype(BF16)
        pre = ALPHA * h
        for j in range(2):
            cols = slice(j * half, (j + 1) * half)
            g = _dot_nt(hb, wg_ref[cols, :])
            u = _dot_nt(hb, wu_ref[cols, :])
            g_s[:, cols] = g
            u_s[:, cols] = u
            pre = pre + _dot(g * _sigmoid(g) * u, wd_ref[cols, :])
        xhat, rstd = _ln_stats(pre)
        real = i * tm + _iota((tm, 1), 0) < SEQ
        target_rows = jnp.concatenate([t[...] for t in t_refs], axis=0)
        diff = jnp.where(real, xhat * g2_ref[...] + b2_ref[...] - target_rows, 0.0)
        acc[...] += jnp.sum(diff * diff, axis=0, keepdims=True)
        dy = diff * (1.0 / D)
        dpre = _ln_bwd(dy, xhat, rstd, g2_ref[...])
        dp_ref[...] = dpre
        dg_ref[...] += jnp.sum(dy * xhat, axis=0, keepdims=True)
        db_ref[...] += jnp.sum(dy, axis=0, keepdims=True)
        dpb = dpre.astype(BF16)
        for j in range(2):
            cols = slice(j * half, (j + 1) * half)
            g, u = g_s[:, cols], u_s[:, cols]
            sg = _sigmoid(g)
            silu = g * sg
            da = _dot_nt(dpb, wd_ref[cols, :])
            a_ref[:, cols] = (silu * u).astype(BF16)
            dgate_ref[:, cols] = (da * u * (sg * (1.0 + g * (1.0 - sg)))).astype(BF16)
            dup_ref[:, cols] = (da * silu).astype(BF16)

        @pl.when(i == steps - 1)
        def _():
            loss_ref[...] = jnp.zeros_like(loss_ref) + (0.5 / D) * jnp.sum(acc[...], axis=1, keepdims=True)

    t_spec = lambda k: pl.BlockSpec((BLK, D), lambda i: (jnp.minimum(i * n_t + k, last_blk), 0))
    return pl.pallas_call(
        body, name="ffn_fwd_loss_bwd", grid=(steps,),
        in_specs=[_rows(tm, D), _const((D_FF, D)), _const((D_FF, D)), _const((D_FF, D))]
        + [t_spec(k) for k in range(n_t)] + [_const((1, D)), _const((1, D))],
        out_specs=[_rows(tm, D_FF), _rows(tm, D_FF), _rows(tm, D_FF), _rows(tm, D), _acc((1, LANE)), _acc((1, D)),
                   _acc((1, D))],
        out_shape=[pltpu.HBM((lp, D_FF), BF16)] * 3 + [pltpu.HBM((lp, D), F32), pltpu.HBM((1, LANE), F32),
                                                         pltpu.HBM((1, D), F32), pltpu.HBM((1, D), F32)],
        scratch_shapes=[pltpu.VMEM((tm, D_FF), F32), pltpu.VMEM((tm, D_FF), F32), pltpu.VMEM((1, D), F32)],
        compiler_params=_params(58, dimension_semantics=_seq()),
    )(*_hbm(h1, wg_t, wu_t, wd, *[target] * n_t, g2, b2))


def _ffn_out_bwd(dpre2, dgate, dup, pre1, wg_t, wu_t, g1, w_out, o_gla, r_g, gn4):
    tm = _row_tile(384)
    lp = _lp()

    def body(dp_ref, dg_ref, du_ref, p1_ref, wg_ref, wu_ref, g1_ref, w_ref, og_ref, r_ref, gn_ref,
             dp1_ref, dg1_ref, db1_ref, dos_ref, dog_ref, dr_ref, dgn_ref):
        @pl.when(pl.program_id(0) == 0)
        def _():
            for acc_ref in (dg1_ref, db1_ref, dgn_ref):
                acc_ref[...] = jnp.zeros_like(acc_ref)

        dh1 = ALPHA * dp_ref[...] + _dot(dg_ref[...], wg_ref[...]) + _dot(du_ref[...], wu_ref[...])
        xhat, rstd1 = _ln_stats(p1_ref[...])
        dpre1 = _ln_bwd(dh1, xhat, rstd1, g1_ref[...])
        dp1_ref[...] = dpre1
        dg1_ref[...] += jnp.sum(dh1 * xhat, axis=0, keepdims=True)
        db1_ref[...] += jnp.sum(dh1, axis=0, keepdims=True)

        do = _dot_nt(dpre1, w_ref[...])
        for pos, h in enumerate(HEAD_POS):
            dos_ref[:, pos * DH:(pos + 1) * DH] = do[:, h * DH:(h + 1) * DH]
        gn = gn_ref[...]
        for h in range(GLA_HEADS):
            hs = slice(h * DV, (h + 1) * DV)
            xg = og_ref[:, hs]
            rstd = lax.rsqrt(jnp.mean(xg * xg, axis=-1, keepdims=True) + RMS_EPS)
            nx = xg * rstd
            r = r_ref[:, hs]
            sr = _sigmoid(r)
            d_o = do[:, 512 + h * DV:512 + (h + 1) * DV]
            dr_ref[:, hs] = d_o * (nx * gn) * (sr * (1.0 + r * (1.0 - sr)))
            dn = d_o * (r * sr)
            dgn_ref[...] += jnp.sum(dn * nx, axis=0, keepdims=True)
            dnx = dn * gn
            dog_ref[:, hs] = rstd * (dnx - nx * jnp.mean(dnx * nx, axis=-1, keepdims=True))

    return pl.pallas_call(
        body, name="ffn_out_bwd", grid=(lp // tm,),
        in_specs=[_rows(tm, D), _rows(tm, D_FF), _rows(tm, D_FF), _rows(tm, D), _const((D_FF, D)), _const((D_FF, D)),
                  _const((1, D)), _const((D, D)), _rows(tm, 512), _rows(tm, 512), _const((1, DV))],
        out_specs=[_rows(tm, D), _acc((1, D)), _acc((1, D)), _rows(tm, 512), _rows(tm, 512), _rows(tm, 512),
                   _acc((1, DV))],
        out_shape=[pltpu.HBM((lp, D), F32), pltpu.HBM((1, D), F32), pltpu.HBM((1, D), F32)]
        + [pltpu.HBM((lp, 512), F32)] * 3 + [pltpu.HBM((1, DV), F32)],
        compiler_params=_params(48, dimension_semantics=_seq()),
    )(*_hbm(dpre2, dgate, dup, pre1, wg_t, wu_t, g1, w_out, o_gla, r_g, gn4))


def _atb(a, b, name, token=None):
    lp = _lp()
    tm = _row_tile(1408)
    n, w = a.shape[1], b.shape[1]
    bw = 512 if n * w * 4 > (4 << 20) else w
    tokens = [] if token is None else [token]

    def body(a_ref, b_ref, *rest):
        o_ref = rest[-1]

        @pl.when(pl.program_id(1) == 0)
        def _():
            o_ref[...] = jnp.zeros_like(o_ref)

        o_ref[...] += _dot_tn(a_ref[...], b_ref[...])

    return pl.pallas_call(
        body, name=name, grid=(w // bw, lp // tm),
        in_specs=[pl.BlockSpec((tm, n), lambda j, k: (k, 0)), pl.BlockSpec((tm, bw), lambda j, k: (k, j))]
        + [_const(TOKEN)] * len(tokens),
        out_specs=pl.BlockSpec((n, bw), lambda j, k: (0, j)),
        out_shape=pltpu.HBM((n, w), F32),
        compiler_params=_params(48, dimension_semantics=_seq(2)),
    )(*_hbm(a, b), *tokens)


def _gla_bwd(qg, kg, vg, z, do_gla, st_all, token):
    steps = SEQ // BLK + 1
    kw, vw = GLA_HEADS * DK, GLA_HEADS * DV
    pairs = [(c, h) for c in range(GLA_PER_STEP) for h in range(GLA_HEADS)]
    heads = range(GLA_HEADS)

    def body(q_ref, k_ref, v_ref, z_ref, do_ref, st_ref, token_ref, dq_ref, dk_ref, dv_ref, dz_ref, dst):
        @pl.when(pl.program_id(0) == 0)
        def _():
            dst[...] = jnp.zeros_like(dst)

        rmask = _gla_rowmask(steps - 1 - pl.program_id(0))
        zz = z_ref[...]
        b, b_last = _gla_decay(zz, rmask)
        e_b, e_nb, e_kd, e_last = jnp.exp(b), jnp.exp(-b), jnp.exp(b_last - b), jnp.exp(b_last)
        q = q_ref[...] * (rmask * DK ** -0.5)
        k = k_ref[...] * rmask
        v = v_ref[...] * rmask
        qe, ke, kd = q * e_b, k * e_nb, k * e_kd
        d_o = do_ref[...]
        causal = _iota((CH, CH), 0) >= _iota((CH, CH), 1)
        a, da, dqe, dke, dv_intra, carry = {}, {}, {}, {}, {}, {}
        for c, h in pairs:
            rows, ks, vs_ = _gla_slices(c, h)
            a[c, h] = jnp.where(causal, _dot_nt(qe[rows, ks], ke[rows, ks]), 0.0)
            da[c, h] = jnp.where(causal, _dot_nt(d_o[rows, vs_], v[rows, vs_]), 0.0)
            carry[c, h] = _dot_tn(d_o[rows, vs_], qe[rows, ks])
        for c, h in pairs:
            rows, ks, vs_ = _gla_slices(c, h)
            dqe[c, h] = _dot(d_o[rows, vs_], st_ref[0, c][:, ks]) + _dot(da[c, h], ke[rows, ks])
            dke[c, h] = _dot_tn(da[c, h], qe[rows, ks])
            dv_intra[c, h] = _dot_tn(a[c, h], d_o[rows, vs_])
        dstate = dst[...]
        dkd, db_decay = {}, {}
        for c in reversed(range(GLA_PER_STEP)):
            for h in heads:
                rows, ks, vs_ = _gla_slices(c, h)
                dkd[c, h] = _dot(v[rows, vs_], dstate[:, ks])
                dv_ref[rows, vs_] = dv_intra[c, h] + _dot_nt(kd[rows, ks], dstate[:, ks])
            chunk_last = e_last[c * CH:c * CH + 1]
            db_decay[c] = jnp.sum(dstate * st_ref[0, c], axis=0, keepdims=True) * chunk_last
            dstate = dstate * chunk_last + jnp.concatenate([carry[c, h] for h in heads], axis=1)
        dst[...] = dstate
        rows_of = lambda parts: jnp.concatenate(
            [jnp.concatenate([parts[c, h] for h in heads], axis=1) for c in range(GLA_PER_STEP)], axis=0)
        dqe_all, dke_all, dkd_all = rows_of(dqe), rows_of(dke), rows_of(dkd)
        dq_ref[...] = dqe_all * e_b * (rmask * DK ** -0.5)
        dk_ref[...] = (dke_all * e_nb + dkd_all * e_kd) * rmask
        dkd_kd = dkd_all * kd
        db = dqe_all * qe - dke_all * ke - dkd_kd
        _, upper, same = _gla_chunk_masks()
        decay_rows = jnp.concatenate([jnp.broadcast_to(db_decay[c], (CH, kw)) for c in range(GLA_PER_STEP)], axis=0)
        dlog_g = _dot_exact(upper.astype(F32), db) + _dot_exact(same.astype(F32), dkd_kd) + decay_rows
        dz_ref[...] = dlog_g * (rmask / GLA_TAU) * _sigmoid(-zz)

    blk = lambda w: pl.BlockSpec((BLK, w), lambda s: (_gla_block(steps - 1 - s), 0))
    return pl.pallas_call(
        body, name="gla_bwd", grid=(steps,),
        in_specs=[blk(kw), blk(kw), blk(vw), blk(kw), blk(vw),
                  pl.BlockSpec((1, GLA_PER_STEP, DV, kw), lambda s: (steps - 1 - s, 0, 0, 0)), _const(TOKEN)],
        out_specs=[blk(kw), blk(kw), blk(vw), blk(kw)],
        out_shape=[pltpu.HBM((_lp(), kw), F32), pltpu.HBM((_lp(), kw), F32),
                   pltpu.HBM((_lp(), vw), F32), pltpu.HBM((_lp(), kw), F32)],
        scratch_shapes=[pltpu.VMEM((DV, kw), F32)],
        compiler_params=_params(16, dimension_semantics=_seq()),
    )(*_hbm(qg, kg, vg, z, do_gla, st_all), token)


def _swa_bwd(sinks, qs, ks, vs, do_s, token):
    nb = SEQ // BLK
    kvw = SWA_KV_HEADS * DH
    scale = DH ** -0.5
    heads = range(SWA_HEADS)

    def body(sink_ref, q_ref, km_ref, kp_ref, kc_ref, vm_ref, vp_ref, vc_ref, do_ref, token_ref,
             dq_ref, dk_ref, dv_ref, dsink_ref, carry_k, carry_v, meta_k, meta_v):
        n = pl.program_id(0)

        @pl.when(n == 0)
        def _():
            for r in (carry_k, carry_v, meta_k, meta_v):
                r[...] = jnp.zeros_like(r)
            dsink_ref[...] = jnp.zeros_like(dsink_ref)

        @pl.when(n <= nb)
        def _():
            negdist, maskbias = _swa_bias(n)
            lane = _iota((1, LANE), 1)
            k_all = jnp.concatenate([km_ref[...], kp_ref[...], kc_ref[...]], axis=0).astype(BF16)
            v_all = jnp.concatenate([vm_ref[...], vp_ref[...], vc_ref[...]], axis=0).astype(BF16)
            q = [_swa_half(q_ref, pos, scale) for pos in heads]
            d_o = [_swa_half(do_ref, pos) for pos in heads]
            t = [_dot_nt(q[pos], k_all) + (2.0 ** -(HEAD_POS[pos] + 1) * negdist + maskbias) for pos in heads]
            dp = [_dot_nt(d_o[pos], v_all) for pos in heads]
            soft = [_swa_softmax(t[pos], sink_ref[HEAD_POS[pos]]) for pos in heads]
            p = [s[0] for s in soft]
            delta = [jnp.sum(p[pos] * dp[pos], axis=-1, keepdims=True) for pos in heads]
            ds = [(p[pos] * (dp[pos] - delta[pos])).astype(BF16) for pos in heads]
            dq = [_dot(ds[pos], k_all) for pos in heads]
            for col in range(SWA_HEADS // 2):
                dq_ref[:, col * LANE:(col + 1) * LANE] = scale * _swa_merge(dq[2 * col], dq[2 * col + 1])
            dsink = jnp.zeros((1, LANE), F32)
            for pos in heads:
                dsink = dsink + jnp.where(lane == HEAD_POS[pos],
                                          -jnp.sum(soft[pos][1] * delta[pos], axis=0, keepdims=True), 0.0)
            dsink_ref[...] += dsink
            dk3 = _dot_tn(jnp.concatenate(q, axis=0), jnp.concatenate(ds, axis=0)).T
            dv3 = _dot_tn(jnp.concatenate(d_o, axis=0), jnp.concatenate([x.astype(BF16) for x in p], axis=0)).T
            meta_k[...] += dk3[0:BLK]
            meta_v[...] += dv3[0:BLK]
            dk_ref[...] = carry_k[...] + dk3[BLK:2 * BLK]
            dv_ref[...] = carry_v[...] + dv3[BLK:2 * BLK]
            carry_k[...] = dk3[2 * BLK:3 * BLK]
            carry_v[...] = dv3[2 * BLK:3 * BLK]

        @pl.when(n == nb + 1)
        def _():
            dk_ref[...] = meta_k[...]
            dv_ref[...] = meta_v[...]

    kv_out = pl.BlockSpec((BLK, kvw), lambda n: (jnp.where(n == nb + 1, nb, jnp.clip(n - 1, 0, nb - 1)), 0))
    qblk = pl.BlockSpec((BLK, SWA_HEADS * DH), lambda n: (jnp.minimum(n, nb), 0))
    return pl.pallas_call(
        body, name="swa_bwd", grid=(nb + 2,),
        in_specs=[pl.BlockSpec(memory_space=pltpu.SMEM), qblk] + _swa_kv_specs(kvw) + _swa_kv_specs(kvw)
        + [qblk, _const(TOKEN)],
        out_specs=[qblk, kv_out, kv_out, _acc((1, LANE))],
        out_shape=[pltpu.HBM((_lp(), SWA_HEADS * DH), F32), pltpu.HBM((_lp(), kvw), F32),
                   pltpu.HBM((_lp(), kvw), F32), pltpu.HBM((1, LANE), F32)],
        scratch_shapes=[pltpu.VMEM((BLK, kvw), F32)] * 4,
        compiler_params=_params(16, dimension_semantics=_seq()),
    )(sinks, *_hbm(qs, ks, ks, ks, vs, vs, vs, do_s), token)


def _in_bwd(dqs, dks, dvs, dqg, dkg, dvg, drg, dz, dpre1, w_in_t, wg2_p):
    tm = _row_tile(384)
    lp = _lp()
    widths = (512, 128, 128, 256, 256, 512, 512)
    offs = (O_QS, O_KS, O_VS, O_QG, O_KG, O_VG, O_RG)

    def body(*refs):
        parts, (dz_ref, dp1_ref, w_ref, wg2_ref, dproj_ref, dh0_ref, dbin_ref, dbg_ref) = refs[:7], refs[7:]

        @pl.when(pl.program_id(0) == 0)
        def _():
            dbin_ref[...] = jnp.zeros_like(dbin_ref)
            dbg_ref[...] = jnp.zeros_like(dbg_ref)

        for pos, h in enumerate(HEAD_POS):
            val = parts[0][:, pos * DH:(pos + 1) * DH]
            dproj_ref[:, O_QS + h * DH:O_QS + (h + 1) * DH] = val.astype(BF16)
            dbin_ref[:, O_QS + h * DH:O_QS + (h + 1) * DH] += jnp.sum(val, axis=0, keepdims=True)
        for p_ref, off, wd in zip(parts[1:], offs[1:], widths[1:]):
            val = p_ref[...]
            dproj_ref[:, off:off + wd] = val.astype(BF16)
            dbin_ref[:, off:off + wd] += jnp.sum(val, axis=0, keepdims=True)
        dz = dz_ref[...]
        dlr = _dot_nt(dz, wg2_ref[...])
        dproj_ref[:, O_LR:O_LR + LANE] = dlr.astype(BF16)
        dbin_ref[:, O_LR:O_LR + LANE] += jnp.sum(dlr, axis=0, keepdims=True)
        dbg_ref[...] += jnp.sum(dz, axis=0, keepdims=True)
        dh0_ref[...] = ALPHA * dp1_ref[...] + _dot(dproj_ref[...], w_ref[...])

    return pl.pallas_call(
        body, name="in_bwd", grid=(lp // tm,),
        in_specs=[_rows(tm, w) for w in widths] + [_rows(tm, 256), _rows(tm, D), _const((D_IN_P, D)), _const((LANE, 256))],
        out_specs=[_rows(tm, D_IN_P), _rows(tm, D), _acc((1, D_IN_P)), _acc((1, 256))],
        out_shape=[pltpu.HBM((lp, D_IN_P), BF16), pltpu.HBM((lp, D), F32),
                   pltpu.HBM((1, D_IN_P), F32), pltpu.HBM((1, 256), F32)],
        compiler_params=_params(40, dimension_semantics=_seq()),
    )(*_hbm(dqs, dks, dvs, dqg, dkg, dvg, drg, dz, dpre1, w_in_t, wg2_p))


def _ln_in_bwd(x, meta_ext, dh0, g, token):
    tr = min(LN_ROWS, SEQ)

    def ln_bwd(x_ref, dh_ref, g_ref, dx_ref, dg_ref, db_ref):
        @pl.when(pl.program_id(0) == 0)
        def _():
            dg_ref[...] = jnp.zeros_like(dg_ref)
            db_ref[...] = jnp.zeros_like(db_ref)

        xhat, rstd = _ln_stats(x_ref[...])
        dh = dh_ref[...]
        dx_ref[...] = _ln_bwd(dh, xhat, rstd, g_ref[...])
        dg_ref[...] += jnp.sum(dh * xhat, axis=0, keepdims=True)
        db_ref[...] += jnp.sum(dh, axis=0, keepdims=True)

    def body(x_ref, dh_ref, g_ref, token_ref, dx_ref, dg_ref, db_ref):
        ln_bwd(x_ref, dh_ref, g_ref, dx_ref, dg_ref, db_ref)

    def meta_body(m_ref, dh_ref, g_ref, dm_ref, dg_ref, db_ref):
        ln_bwd(m_ref, dh_ref, g_ref, dm_ref, dg_ref, db_ref)

    sums = [pltpu.HBM((1, D), F32), pltpu.HBM((1, D), F32)]
    dx, dg, db = pl.pallas_call(
        body, name="ln_in_bwd", grid=(SEQ // tr,),
        in_specs=[_rows(tr, D), _rows(tr, D), _const((1, D)), _const(TOKEN)],
        out_specs=[_rows(tr, D), _acc((1, D)), _acc((1, D))],
        out_shape=[pltpu.HBM((SEQ, D), F32)] + sums,
        compiler_params=_params(32, dimension_semantics=_seq()),
    )(*_hbm(x, dh0, g), token)
    dm, dg_m, db_m = pl.pallas_call(
        meta_body, name="ln_in_bwd_meta", grid=(1,),
        in_specs=[_const((BLK, D)), pl.BlockSpec((BLK, D), lambda i: (SEQ // BLK, 0)), _const((1, D))],
        out_specs=[_acc((BLK, D)), _acc((1, D)), _acc((1, D))],
        out_shape=[pltpu.HBM((BLK, D), F32)] + sums,
        compiler_params=_params(16, dimension_semantics=_seq()),
    )(*_hbm(meta_ext, dh0, g))
    return dx, dm, dg + dg_m, db + db_m


def _local_step(x, target, ln_in_g, ln_in_b, b_in, bg2, sinks, gn, g1, b1, g2, b2,
                token, fetch_first, fetch_rest, fetch_ffn, exchange_ffn, ship_ffn, ship_w_in):
    row = lambda v: v.reshape(1, -1).astype(F32)
    b_in_p = jnp.pad(row(b_in), ((0, 0), (0, D_IN_P - D_IN)))
    gn4 = row(gn)
    sinks = sinks.reshape(-1).astype(F32)

    h_real = _ln_in_fwd_real(x, row(ln_in_g), row(ln_in_b), token)
    w_in_t, w_out, meta_full, wg2 = fetch_first([h_real])
    meta_ext = jnp.pad(meta_full, ((META_OFF, BLK - CH), (0, 0)))
    wg2_p = jnp.pad(wg2, ((0, LANE - wg2.shape[0]), (0, 0))).astype(BF16)
    h0 = _ln_in_fwd_meta(h_real, meta_ext, row(ln_in_g), row(ln_in_b))
    qs, ks, vs, qg, kg, vg, rg, glr, z = _in_proj(h0, w_in_t, b_in_p, wg2_p, row(bg2))
    o_s = _swa_fwd(sinks, qs, ks, vs)
    o_gla, st_all = _gla_fwd(qg, kg, vg, z)
    token = fetch_rest([o_s, o_gla])
    o, pre1, h1 = _post_mix(o_s, o_gla, rg, h0, gn4, w_out, row(g1), row(b1), token)
    wg_t, wu_t, wd = fetch_ffn([pre1])
    a, dgate, dup, dpre2, loss, dg2, db2 = _ffn_fwd_loss_bwd(h1, wg_t, wu_t, wd, target, row(g2), row(b2))
    dpre1, dg1, db1, do_s, do_gla, drg, dgn = _ffn_out_bwd(dpre2, dgate, dup, pre1, wg_t, wu_t, row(g1), w_out, o_gla,
                                                           rg, gn4)
    dwd = _atb(a, dpre2, "dw_down")
    dwg_t = _atb(dgate, h1, "dw_gate")
    dwu_t = _atb(dup, h1, "dw_up")
    token = exchange_ffn(dict(w_g=dwg_t, w_u=dwu_t, w_d=dwd))
    token = ship_ffn(_atb(o, dpre1, "dw_out", token))
    dqg, dkg, dvg, dz = _gla_bwd(qg, kg, vg, z, do_gla, st_all, token)
    dqs, dks, dvs, dsinks = _swa_bwd(sinks, qs, ks, vs, do_s, token)
    dproj, dh0, db_in_p, dbg2 = _in_bwd(dqs, dks, dvs, dqg, dkg, dvg, drg, dz, dpre1, w_in_t, wg2_p)
    token = ship_w_in(_atb(dproj, h0, "dw_in"))
    dwg2_p = _atb(glr, dz, "dw_gate_lr2")
    dx, dmeta_blk, dg_in, db_in_ln = _ln_in_bwd(x, meta_ext, dh0, row(ln_in_g), token)

    small = dict(meta_blk=dmeta_blk, ln_in_g=dg_in, ln_in_b=db_in_ln, ln1_g=dg1, ln1_b=db1, ln2_g=dg2, ln2_b=db2,
                 b_in_p=db_in_p, wg2_p=dwg2_p, bg2=dbg2, sinks=dsinks, gn=dgn, loss=loss)
    return dx, small


HBM = pl.BlockSpec(memory_space=pltpu.HBM)


def _place():
    return lax.axis_index("x"), lax.axis_index("y"), lax.axis_index("c")


def _other_chips(x, y):
    return [(1 - x, y), (x, 1 - y), (1 - x, 1 - y)]


def _dma_sems(n):
    return pltpu.SemaphoreType.DMA((n,))


def _comm_params():
    return pltpu.CompilerParams(has_side_effects=True)


SEM = pl.BlockSpec(memory_space=pltpu.SEMAPHORE)


PER_ARRAY = dict(gather=3, scatter=3, sibling=N_CHIPS)


def _ici_copies(kind, landing, srcs, lands, send_sems, recv_sems):
    x, y, c = _place()
    mine = 2 * x + y
    copies = []
    for a in range(len(srcs)):
        if kind == "sibling":
            for s in range(N_CHIPS):
                copies.append(pltpu.make_async_remote_copy(
                    srcs[a].at[s, 1 - c], lands[a].at[s], send_sems.at[N_CHIPS * a + s], recv_sems.at[N_CHIPS * a + s],
                    device_id=(x, y, 1 - c), device_id_type=MESH))
            continue
        for j, (px, py) in enumerate(_other_chips(x, y)):
            slab = 2 * px + py if landing else mine
            if kind == "gather":
                src, dst = srcs[a].at[c], lands[a].at[slab, c]
            else:
                src, dst = srcs[a].at[2 * px + py], lands[a].at[slab]
            copies.append(pltpu.make_async_remote_copy(src, dst, send_sems.at[3 * a + j], recv_sems.at[3 * a + j],
                                                       device_id=(px, py, c), device_id_type=MESH))
    return copies


def _split_params():
    return pltpu.CompilerParams(has_side_effects=pltpu.SideEffectType.DATAFLOW_SIDE_EFFECTING)


def _ici_start(kind, srcs, land_shapes, after, name):
    n = len(srcs)
    lands = [pltpu.with_memory_space_constraint(lax.empty(s, a.dtype), pltpu.HBM) for s, a in zip(land_shapes, srcs)]

    def body(*refs):
        outs = refs[2 * n + len(after):]
        for cp in _ici_copies(kind, False, refs[:n], refs[n:2 * n], outs[0], outs[1]):
            cp.start()
        outs[-1][...] = jnp.zeros(TOKEN, F32)

    outs = pl.pallas_call(
        body, name=name, in_specs=[HBM] * (2 * n) + [pl.BlockSpec(memory_space=pl.ANY)] * len(after),
        out_specs=[SEM, SEM] + [HBM] * (2 * n) + [pl.BlockSpec(memory_space=pltpu.VMEM)],
        out_shape=[_dma_sems(PER_ARRAY[kind] * n)] * 2 + [pltpu.HBM(a.shape, a.dtype) for a in list(srcs) + lands]
        + [jax.ShapeDtypeStruct(TOKEN, F32)],
        input_output_aliases={i: 2 + i for i in range(2 * n)},
        compiler_params=_split_params(),
    )(*_hbm(*srcs), *lands, *after)
    return outs[:-1], outs[-1]


def _ici_wait(kind, handle, after, name):
    n = (len(handle) - 2) // 2

    def body(*refs):
        for cp in _ici_copies(kind, True, refs[:n], refs[n:2 * n], refs[2 * n], refs[2 * n + 1]):
            cp.wait_send()
            cp.wait_recv()

    outs = pl.pallas_call(
        body, name=name, in_specs=[HBM] * (2 * n) + [SEM, SEM] + [pl.BlockSpec(memory_space=pl.ANY)] * len(after),
        out_specs=[HBM] * (2 * n), out_shape=[pltpu.HBM(a.shape, a.dtype) for a in handle[2:]],
        input_output_aliases={i: i for i in range(2 * n)},
        compiler_params=_split_params(),
    )(*handle[2:], handle[0], handle[1], *after)
    return list(outs[:n]), list(outs[n:])


def _forward_copies(landing, arrs, send_sems, recv_sems):
    x, y, c = _place()
    copies = []
    for a in range(len(arrs)):
        for j, (px, py) in enumerate(_other_chips(x, y)):
            half = 1 - c if landing else c
            copies.append(pltpu.make_async_remote_copy(
                arrs[a].at[2 * px + py, c], arrs[a].at[2 * px + py, half], send_sems.at[3 * a + j],
                recv_sems.at[3 * a + j], device_id=(x, y, 1 - c), device_id_type=MESH))
    return copies


def _sibling_forward(lands, name):
    n = len(lands)

    def body(*refs):
        outs = refs[n:2 * n]
        send_sems, recv_sems = refs[2 * n:]
        sends = _forward_copies(False, outs, send_sems, recv_sems)
        for cp in sends:
            cp.start()
        for cp in _forward_copies(True, outs, send_sems, recv_sems):
            cp.wait_recv()
        for cp in sends:
            cp.wait_send()

    return pl.pallas_call(
        body, name=name, in_specs=[HBM] * n, out_specs=[HBM] * n,
        out_shape=[pltpu.HBM(a.shape, a.dtype) for a in lands],
        input_output_aliases={a: a for a in range(n)},
        scratch_shapes=[_dma_sems(3 * n)] * 2,
        compiler_params=_comm_params(),
    )(*_hbm(*lands))


def _forward_start(lands, name):
    n = len(lands)

    def body(*refs):
        outs = refs[n:]
        for cp in _forward_copies(False, refs[:n], outs[0], outs[1]):
            cp.start()
        outs[-1][...] = jnp.zeros(TOKEN, F32)

    outs = pl.pallas_call(
        body, name=name, in_specs=[HBM] * n,
        out_specs=[SEM, SEM] + [HBM] * n + [pl.BlockSpec(memory_space=pltpu.VMEM)],
        out_shape=[_dma_sems(3 * n)] * 2 + [pltpu.HBM(a.shape, a.dtype) for a in lands]
        + [jax.ShapeDtypeStruct(TOKEN, F32)],
        input_output_aliases={i: 2 + i for i in range(n)},
        compiler_params=_split_params(),
    )(*_hbm(*lands))
    return outs[:-1], outs[-1]


def _forward_wait(handle, after, name):
    n = len(handle) - 2

    def body(*refs):
        for cp in _forward_copies(True, refs[:n], refs[n], refs[n + 1]):
            cp.wait_send()
            cp.wait_recv()

    return list(pl.pallas_call(
        body, name=name, in_specs=[HBM] * n + [SEM, SEM] + [pl.BlockSpec(memory_space=pl.ANY)] * len(after),
        out_specs=[HBM] * n, out_shape=[pltpu.HBM(a.shape, a.dtype) for a in handle[2:]],
        input_output_aliases={i: i for i in range(n)},
        compiler_params=_split_params(),
    )(*handle[2:], handle[0], handle[1], *after))


def _sibling_exchange(grads, name):
    n = len(grads)

    def body(*refs):
        ins, outs = refs[:n], refs[n:2 * n]
        send_sems, recv_sems = refs[2 * n:]
        x, y, c = _place()
        copies = []
        for a in range(n):
            for s in range(N_CHIPS):
                cp = pltpu.make_async_remote_copy(ins[a].at[s, 1 - c], outs[a].at[s], send_sems.at[N_CHIPS * a + s],
                                                  recv_sems.at[N_CHIPS * a + s], device_id=(x, y, 1 - c),
                                                  device_id_type=MESH)
                cp.start()
                copies.append(cp)
        for cp in copies:
            cp.wait_recv()
        for cp in copies:
            cp.wait_send()

    return pl.pallas_call(
        body, name=name, in_specs=[HBM] * n, out_specs=[HBM] * n,
        out_shape=[pltpu.HBM((N_CHIPS, g.shape[2], D), F32) for g in grads],
        scratch_shapes=[_dma_sems(N_CHIPS * n)] * 2,
        compiler_params=_comm_params(),
    )(*_hbm(*grads))


def _add_halves(core, grads, recvs, dtypes, name):
    n = len(grads)
    heights = [g.shape[2] for g in grads]

    def body(c_ref, *refs):
        for a in range(n):
            refs[2 * n + a][...] = (refs[2 * a][0] + refs[2 * a + 1][...]).astype(dtypes[a])

    slab = lambda h: pl.BlockSpec((1, h, D), lambda s, c: (s, 0, 0))
    mine = lambda h: pl.BlockSpec((1, 1, h, D), lambda s, c: (s, c[0], 0, 0))
    return pl.pallas_call(
        body, name=name,
        grid_spec=pltpu.PrefetchScalarGridSpec(
            num_scalar_prefetch=1, grid=(N_CHIPS,),
            in_specs=[spec(h) for h in heights for spec in (mine, slab)], out_specs=[slab(h) for h in heights]),
        out_shape=[pltpu.HBM((N_CHIPS, h, D), dt) for h, dt in zip(heights, dtypes)],
        compiler_params=_params(32, dimension_semantics=_seq()),
    )(core, *_hbm(*[a for pair in zip(grads, recvs) for a in pair]))


N_DEVICES = 2 * N_CHIPS
PEER_FLIPS = [(dx, dy, dc) for dx in (0, 1) for dy in (0, 1) for dc in (0, 1)][1:]


def _small_exchange(pack, after):
    n = len(PEER_FLIPS)

    def body(p_ref, *refs):
        out_ref, send_sems, recv_sems = refs[len(after):]
        x, y, c = _place()
        flip = lambda v, d: 1 - v if d else v

        def copy(k, landing):
            px, py, pc = (flip(v, d) for v, d in zip((x, y, c), PEER_FLIPS[k]))
            slab = 4 * px + 2 * py + pc if landing else 4 * x + 2 * y + c
            return pltpu.make_async_remote_copy(p_ref, out_ref.at[slab], send_sems.at[k], recv_sems.at[k],
                                                device_id=(px, py, pc), device_id_type=MESH)

        for k in range(n):
            copy(k, False).start()
        for k in range(n):
            copy(k, True).wait_recv()
        for k in range(n):
            copy(k, False).wait_send()

    return pl.pallas_call(
        body, name="small_exchange", in_specs=[HBM] + [pl.BlockSpec(memory_space=pl.ANY)] * len(after), out_specs=HBM,
        out_shape=pltpu.HBM((N_DEVICES,) + pack.shape, F32),
        scratch_shapes=[_dma_sems(n)] * 2,
        compiler_params=_comm_params(),
    )(*_hbm(pack), *after)


def _sum_chips(slots, firsts, rests):
    n = len(firsts)

    def body(i_ref, *refs):
        for a in range(n):
            first, r1, r2, r3 = refs[4 * a:4 * a + 4]
            refs[4 * n + a][...] = ((first[...].astype(F32) + r1[...].astype(F32)) + r2[...].astype(F32)) + r3[...].astype(F32)

    slab = lambda h, k: pl.BlockSpec((1, h, D), lambda i, ix: (ix[k], 0, 0))
    heights = [f.shape[1] for f in firsts]
    return pl.pallas_call(
        body, name="sum_chips",
        grid_spec=pltpu.PrefetchScalarGridSpec(
            num_scalar_prefetch=1, grid=(1,),
            in_specs=[slab(h, k) for h in heights for k in range(4)], out_specs=[slab(h, 4) for h in heights]),
        out_shape=[pltpu.HBM((2, h, D), F32) for h in heights],
        compiler_params=_params(48, dimension_semantics=_seq()),
    )(slots, *_hbm(*[a for f, r in zip(firsts, rests) for a in (f, r, r, r)]))


def _join_halves(halves):
    n = len(halves)

    def body(*refs):
        outs = refs[n:2 * n]
        send_sems, recv_sems = refs[2 * n:]
        x, y, c = _place()

        def copy(a, slab):
            return pltpu.make_async_remote_copy(outs[a].at[slab], outs[a].at[slab], send_sems.at[a], recv_sems.at[a],
                                                device_id=(x, y, 1 - c), device_id_type=MESH)

        for a in range(n):
            copy(a, c).start()
        for a in range(n):
            copy(a, 1 - c).wait_recv()
        for a in range(n):
            copy(a, c).wait_send()

    return pl.pallas_call(
        body, name="join_halves", in_specs=[HBM] * n, out_specs=[HBM] * n,
        out_shape=[pltpu.HBM(h.shape, F32) for h in halves],
        input_output_aliases={a: a for a in range(n)},
        scratch_shapes=[_dma_sems(n)] * 2,
        compiler_params=_comm_params(),
    )(*_hbm(*halves))


def _chip_partials(grads, wire_dtypes, names, fetched=()):
    core = lax.axis_index("c").astype(jnp.int32).reshape(1)
    todo = len(grads) - len(fetched)
    recv = list(_sibling_exchange(grads[:todo], "sibling_exchange_" + names[0])) + list(fetched)
    return list(_add_halves(core, grads, recv, wire_dtypes, "add_halves_" + names[0]))


def _finish_reduce(parts, got):
    x, y, c = _place()
    others = [2 * px + py for px, py in _other_chips(x, y)]
    own_first = jnp.stack([2 * x + y] + others + [c]).astype(jnp.int32)
    return [f.reshape(2 * f.shape[1], D) for f in _join_halves(_sum_chips(own_first, parts, got))]


ADAMW_STEPS = 4


def _adamw(params):
    n = len(params)

    def block(shape):
        rows, cols = shape
        if rows % (8 * ADAMW_STEPS) == 0:
            return pl.BlockSpec((rows // ADAMW_STEPS, cols), lambda i: (i, 0))
        assert cols % (LANE * ADAMW_STEPS) == 0
        return pl.BlockSpec((rows, cols // ADAMW_STEPS), lambda i: (0, i))

    def body(*refs):
        for a in range(n):
            w_ref, g_ref, m_ref, v_ref = refs[4 * a:4 * a + 4]
            outs = refs[4 * n + 3 * a:4 * n + 3 * a + 3]
            outs[0][...], outs[1][...], outs[2][...] = _adamw_math(w_ref[...], g_ref[...], m_ref[...], v_ref[...])

    outs = pl.pallas_call(
        body, name="adamw_matrices", grid=(ADAMW_STEPS,),
        in_specs=[block(p[0].shape) for p in params for _ in range(4)],
        out_specs=[block(p[0].shape) for p in params for _ in range(3)],
        out_shape=[pltpu.HBM(p[0].shape, F32) for p in params for _ in range(3)],
        compiler_params=_params(48, dimension_semantics=_seq()),
    )(*_hbm(*[a for p in params for a in p]))
    return [outs[3 * a:3 * a + 3] for a in range(n)]


def _adamw_math(w, g, m, v):
    nm = ADAM_B1 * m + (1.0 - ADAM_B1) * g
    nv = ADAM_B2 * v + (1.0 - ADAM_B2) * (g * g)
    m_hat = nm / (1.0 - ADAM_B1 ** ADAM_STEP)
    v_hat = nv / (1.0 - ADAM_B2 ** ADAM_STEP)
    return -ADAM_LR * (m_hat / (jnp.sqrt(v_hat) + ADAM_EPS) + ADAM_WD * w), nm, nv


SMALL = (("meta_tokens", (N_META, D // N_CHIPS)), ("ln_in_g", (1, D)), ("ln_in_b", (1, D)), ("b_in", (1, D_IN)),
         ("w_gate_lr2", (GATE_RANK, GLA_HEADS * DK // N_CHIPS)), ("b_gate_lr2", (1, GLA_HEADS * DK)),
         ("attn_sinks", (1, SWA_HEADS)),
         ("gla_norm_g", (1, DV)), ("ln1_g", (1, D)), ("ln1_b", (1, D)), ("ln2_g", (1, D)), ("ln2_b", (1, D)))
ROW_META, ROW_B_IN, ROW_TAIL, ROW_WG2 = 0, 22, 25, 32
ROW_LN = dict(ln_in_g=16, ln_in_b=17, ln1_g=18, ln1_b=19, ln2_g=20, ln2_b=21)
TAIL_BG2, TAIL_SINKS, TAIL_GN, TAIL_LOSS = 0, 256, 256 + SWA_HEADS, 256 + SWA_HEADS + DV


def _adamw_small(place, packs, own, params):
    n = len(SMALL)

    def body(place_ref, packs_ref, own_ref, *refs):
        ins, outs, p_ref = refs[:3 * n], refs[3 * n:-1], refs[-1]
        me, c = place_ref[0], place_ref[1]
        total = jnp.where(me == 0, own_ref[...], packs_ref[0])
        for i in range(1, N_DEVICES):
            total = total + jnp.where(me == i, own_ref[...], packs_ref[i])
        p_ref[...] = total
        outs[4 * n][...] = total[ROW_TAIL:ROW_TAIL + 1, :]

        def mine(width, rows):
            part = lambda s: p_ref[rows, s * width:(s + 1) * width]
            return jnp.where(c == 0, part(0), jnp.where(c == 1, part(1), jnp.where(c == 2, part(2), part(3))))

        tail = lambda lo, width: p_ref[ROW_TAIL:ROW_TAIL + 1, lo:lo + width]
        grads = dict(
            meta_tokens=mine(D // N_CHIPS, slice(ROW_META, ROW_META + N_META)),
            b_in=jnp.concatenate([p_ref[ROW_B_IN:ROW_B_IN + 1, :], p_ref[ROW_B_IN + 1:ROW_B_IN + 2, :],
                                  p_ref[ROW_B_IN + 2:ROW_B_IN + 3, 0:D_IN - 2 * D]], axis=1),
            w_gate_lr2=mine(256 // N_CHIPS, slice(ROW_WG2, ROW_WG2 + 16)),
            b_gate_lr2=tail(TAIL_BG2, 256), attn_sinks=tail(TAIL_SINKS, SWA_HEADS), gla_norm_g=tail(TAIL_GN, DV),
            **{k: p_ref[r:r + 1, :] for k, r in ROW_LN.items()})
        for i, (name, _) in enumerate(SMALL):
            g = grads[name]
            outs[4 * i][...] = g
            outs[4 * i + 1][...], outs[4 * i + 2][...], outs[4 * i + 3][...] = _adamw_math(
                ins[3 * i][...], g, ins[3 * i + 1][...], ins[3 * i + 2][...])

    whole = lambda shape: pl.BlockSpec(shape, lambda i, c: (0,) * len(shape))
    outs = pl.pallas_call(
        body, name="adamw_small",
        grid_spec=pltpu.PrefetchScalarGridSpec(
            num_scalar_prefetch=1, grid=(1,),
            in_specs=[whole(packs.shape), whole(own.shape)] + [whole(s) for _, s in SMALL for _ in range(3)],
            out_specs=[whole(s) for _, s in SMALL for _ in range(4)] + [whole((1, D))],
            scratch_shapes=[pltpu.VMEM(own.shape, F32)]),
        out_shape=[pltpu.HBM(s, F32) for _, s in SMALL for _ in range(4)] + [pltpu.HBM((1, D), F32)],
        compiler_params=_params(16, dimension_semantics=_seq()),
    )(place, *_hbm(packs, own, *[a for p in params for a in p]))
    return [outs[4 * i:4 * i + 4] for i in range(n)], outs[4 * n]


def _small_pack(gr):
    names = ["meta_blk"] + list(ROW_LN) + ["b_in_p", "wg2_p", "bg2", "sinks", "gn", "loss"]
    gate_w = GLA_HEADS * DK

    def body(*refs):
        src, out = dict(zip(names, refs)), refs[-1]
        out[...] = jnp.zeros_like(out)
        out[ROW_META:ROW_META + N_META, :] = src["meta_blk"][META_OFF:CH, :]
        for k, r in ROW_LN.items():
            out[r:r + 1, :] = src[k][...]
        for j in range(-(-D_IN // D)):
            width = min(D, D_IN - j * D)
            out[ROW_B_IN + j:ROW_B_IN + j + 1, 0:width] = src["b_in_p"][:, j * D:j * D + width]
        tail = slice(ROW_TAIL, ROW_TAIL + 1)
        out[tail, TAIL_BG2:TAIL_BG2 + gate_w] = src["bg2"][...]
        out[tail, TAIL_SINKS:TAIL_SINKS + SWA_HEADS] = src["sinks"][:, 0:SWA_HEADS]
        out[tail, TAIL_GN:TAIL_GN + DV] = src["gn"][...]
        out[tail, TAIL_LOSS:TAIL_LOSS + 1] = src["loss"][:, 0:1]
        out[ROW_WG2:ROW_WG2 + GATE_RANK, 0:gate_w] = src["wg2_p"][0:GATE_RANK, :]

    arrays = [gr[k] for k in names]
    return pl.pallas_call(
        body, name="small_pack", grid=(1,),
        in_specs=[_acc(a.shape) for a in arrays], out_specs=_acc((SMALL_ROWS, D)),
        out_shape=pltpu.HBM((SMALL_ROWS, D), F32),
        compiler_params=_params(16, dimension_semantics=_seq()),
    )(*_hbm(*arrays))


BIG = ("w_in", "w_out", "w_g", "w_u", "w_d")


def kernel(x, meta_tokens, ln_in_g, ln_in_b, w_in, b_in, w_gate_lr2, b_gate_lr2, attn_sinks, gla_norm_g, w_out, ln1_g, ln1_b, w_ffn_gate, w_ffn_up, w_ffn_down, ln2_g, ln2_b, loss_target, m_meta_tokens, m_ln_in_g, m_ln_in_b, m_w_in, m_b_in, m_w_gate_lr2, m_b_gate_lr2, m_attn_sinks, m_gla_norm_g, m_w_out, m_ln1_g, m_ln1_b, m_w_ffn_gate, m_w_ffn_up, m_w_ffn_down, m_ln2_g, m_ln2_b, v_meta_tokens, v_ln_in_g, v_ln_in_b, v_w_in, v_b_in, v_w_gate_lr2, v_b_gate_lr2, v_attn_sinks, v_gla_norm_g, v_w_out, v_ln1_g, v_ln1_b, v_w_ffn_gate, v_w_ffn_up, v_w_ffn_down, v_ln2_g, v_ln2_b):
    chip = 2 * lax.axis_index("x") + lax.axis_index("y")

    halves = lambda a: a.reshape(2, a.shape[0] // 2, a.shape[1])
    r_in = SHARD_ROWS["w_in"]
    first = [halves(a) for a in (jnp.pad(w_in[0].T.astype(BF16), ((0, W_IN_WIN - r_in), (0, 0))), w_out[0].astype(BF16),
                                 meta_tokens, w_gate_lr2[0])]
    rest = [halves(a) for a in (w_ffn_gate[0].T.astype(BF16), w_ffn_up[0].T.astype(BF16), w_ffn_down[0].astype(BF16))]
    lands = lambda arrs: [(N_CHIPS,) + a.shape for a in arrs]
    first_handle, first_token = _ici_start("gather", first, lands(first), [], "gather_first_start")
    rest_handle, token = _ici_start("gather", rest, lands(rest), [first_token], "gather_rest_start")
    own_slab = lambda got, shards: [lax.dynamic_update_index_in_dim(g, s, chip, axis=0) for g, s in zip(got, shards)]
    fetching = {}

    def fetch_first(after):
        shards, landed = _ici_wait("gather", first_handle, after, "gather_first_wait")
        g_in, g_out, g_meta, g_wg2 = own_slab(_sibling_forward(landed, "gather_first_forward"), shards)
        w_in_t = jnp.pad(g_in.reshape(N_CHIPS, W_IN_WIN, D)[:, :r_in].reshape(D_IN, D), ((0, D_IN_P - D_IN), (0, 0)))
        meta_full = jnp.concatenate([g_meta[s].reshape(N_META, -1) for s in range(N_CHIPS)], axis=1)
        wg2_full = jnp.concatenate([g_wg2[s].reshape(w_gate_lr2.shape[1], -1) for s in range(N_CHIPS)], axis=1)
        return w_in_t, g_out.reshape(-1, D), meta_full, wg2_full

    def fetch_rest(after):
        fetching["shards"], landed = _ici_wait("gather", rest_handle, after, "gather_rest_wait")
        fetching["handle"], forward_token = _forward_start(landed, "gather_rest_forward_start")
        return forward_token

    def fetch_ffn(after):
        got = _forward_wait(fetching["handle"], after, "gather_rest_forward_wait")
        return [g.reshape(-1, D) for g in own_slab(got, fetching["shards"])]

    sent = {}
    split = lambda grads: [g.reshape(N_CHIPS, 2, -1, D) for g in grads]

    def ship(key, grads, names, fetched=()):
        parts = _chip_partials(grads, [BF16] * len(grads), names, fetched)
        sent[key], ship_token = _ici_start("scatter", parts, [p.shape for p in parts], [], "scatter_" + key + "_start")
        return ship_token

    def exchange_ffn(g):
        grads = split([g[k] for k in BIG[2:]])
        sent["ffn_halves"], exchange_token = _ici_start("sibling", grads, [(N_CHIPS,) + a.shape[2:] for a in grads], [],
                                                        "sibling_ffn_start")
        return exchange_token

    def ship_ffn(dw_out):
        grads, fetched = _ici_wait("sibling", sent["ffn_halves"], [dw_out], "sibling_ffn_wait")
        return ship("ffn", split([dw_out]) + grads, list(BIG[1:]), fetched)

    def ship_w_in(dw_in_t):
        win_start = [s * r_in // BF16_ROWS * BF16_ROWS for s in range(N_CHIPS)]
        return ship("w_in", split([jnp.stack([dw_in_t[st:st + W_IN_WIN] for st in win_start])]), ["w_in"])

    dx, gr = _local_step(
        x[0], loss_target[0], ln_in_g, ln_in_b, b_in[0], b_gate_lr2[0], attn_sinks[0], gla_norm_g[0], ln1_g[0],
        ln1_b[0], ln2_g[0], ln2_b[0], token, fetch_first, fetch_rest, fetch_ffn, exchange_ffn, ship_ffn, ship_w_in)
    ffn_parts, ffn_got = _ici_wait("scatter", sent["ffn"], [dx], "scatter_ffn_wait")
    w_in_parts, w_in_got = _ici_wait("scatter", sent["w_in"], [dx], "scatter_w_in_wait")

    small_own = _small_pack(gr)
    small_all = _small_exchange(small_own, [w_in_got[0]])
    red = _finish_reduce(w_in_parts + ffn_parts, w_in_got + ffn_got)

    big_g = dict(zip(BIG, red))
    big_g["w_in"] = lax.dynamic_slice_in_dim(red[0], chip * (r_in % BF16_ROWS), r_in, axis=0)
    grads = dict(w_in=big_g["w_in"].T[None], w_out=big_g["w_out"][None], w_ffn_gate=big_g["w_g"].T[None],
                 w_ffn_up=big_g["w_u"].T[None], w_ffn_down=big_g["w_d"][None])
    weights = dict(meta_tokens=meta_tokens, ln_in_g=ln_in_g, ln_in_b=ln_in_b, w_in=w_in, b_in=b_in,
                   w_gate_lr2=w_gate_lr2, b_gate_lr2=b_gate_lr2, attn_sinks=attn_sinks, gla_norm_g=gla_norm_g,
                   w_out=w_out, ln1_g=ln1_g, ln1_b=ln1_b, w_ffn_gate=w_ffn_gate, w_ffn_up=w_ffn_up,
                   w_ffn_down=w_ffn_down, ln2_g=ln2_g, ln2_b=ln2_b)
    m_in = dict(meta_tokens=m_meta_tokens, ln_in_g=m_ln_in_g, ln_in_b=m_ln_in_b, w_in=m_w_in, b_in=m_b_in,
                w_gate_lr2=m_w_gate_lr2, b_gate_lr2=m_b_gate_lr2, attn_sinks=m_attn_sinks, gla_norm_g=m_gla_norm_g,
                w_out=m_w_out, ln1_g=m_ln1_g, ln1_b=m_ln1_b, w_ffn_gate=m_w_ffn_gate, w_ffn_up=m_w_ffn_up,
                w_ffn_down=m_w_ffn_down, ln2_g=m_ln2_g, ln2_b=m_ln2_b)
    v_in = dict(meta_tokens=v_meta_tokens, ln_in_g=v_ln_in_g, ln_in_b=v_ln_in_b, w_in=v_w_in, b_in=v_b_in,
                w_gate_lr2=v_w_gate_lr2, b_gate_lr2=v_b_gate_lr2, attn_sinks=v_attn_sinks, gla_norm_g=v_gla_norm_g,
                w_out=v_w_out, ln1_g=v_ln1_g, ln1_b=v_ln1_b, w_ffn_gate=v_w_ffn_gate, w_ffn_up=v_w_ffn_up,
                w_ffn_down=v_w_ffn_down, ln2_g=v_ln2_g, ln2_b=v_ln2_b)
    names = list(weights)
    big_names = ("w_in", "w_out", "w_ffn_gate", "w_ffn_up", "w_ffn_down")

    delta, new_m, new_v = {}, {}, {}
    flips = [(lambda a: a.T) if kk in ("w_in", "w_g", "w_u") else (lambda a: a) for kk in BIG]
    updated = _adamw([(flip(weights[k][0]), big_g[kk], flip(m_in[k][0]), flip(v_in[k][0]))
                      for k, kk, flip in zip(big_names, BIG, flips)])
    for k, flip, results in zip(big_names, flips, updated):
        delta[k], new_m[k], new_v[k] = (flip(t)[None] for t in results)
    small_in = [tuple(src[k].reshape(shape) for src in (weights, m_in, v_in)) for k, shape in SMALL]
    place = jnp.stack([2 * chip + lax.axis_index("c"), chip]).astype(jnp.int32)
    small_out, tail_row = _adamw_small(place, small_all, small_own, small_in)
    for (k, _), results in zip(SMALL, small_out):
        grads[k], delta[k], new_m[k], new_v[k] = (r.reshape(weights[k].shape) for r in results)

    return (tail_row[0, TAIL_LOSS], dx[None], *[grads[k] for k in names], *[delta[k] for k in names], *[new_m[k] for k in names],
            *[new_v[k] for k in names])
```

```python
import jax
import jax.numpy as jnp
from jax import lax
from jax.experimental import pallas as pl
from jax.experimental.pallas import tpu as pltpu

F32 = jnp.float32
BF16 = jnp.bfloat16
MESH = pl.DeviceIdType.MESH

D = 1024
SEQ = 4096
N_META = 16
SWA_HEADS, SWA_KV_HEADS, DH = 8, 2, 64
WINDOW = 128
GLA_HEADS, DK, DV = 4, 64, 128
GLA_TAU = 16.0
CH = 64
D_FF = 2816
D_IN = 2320
LN_EPS = 1e-5
RMS_EPS = 1e-6
ALPHA = 2.0 ** 0.25
NEG = -1e30
ADAM_LR, ADAM_B1, ADAM_B2, ADAM_EPS, ADAM_WD, ADAM_STEP = 0.001, 0.9, 0.999, 1e-8, 0.01, 10
O_QS, O_KS, O_VS, O_QG, O_KG, O_VG, O_RG, O_LR = 0, 512, 640, 768, 1024, 1280, 1792, 2304

LANE = 128
BLK = WINDOW
GATE_RANK = 16
D_IN_P = D_IN + LANE - GATE_RANK
META_OFF = CH - N_META
HEAD_POS = (0, 4, 1, 5, 2, 6, 3, 7)
LN_ROWS = 512
TOKEN = (8, LANE)
N_CHIPS = 4
SHARD_ROWS = dict(w_in=D_IN // N_CHIPS, w_out=D // N_CHIPS, w_g=D_FF // N_CHIPS, w_u=D_FF // N_CHIPS,
                  w_d=D_FF // N_CHIPS)
SMALL_ROWS = 48
BF16_ROWS = 16
W_IN_WIN = -(-SHARD_ROWS["w_in"] // (2 * BF16_ROWS)) * 2 * BF16_ROWS
VMEM_CAP_MB = 64
VMEM_SPARE_MB = 6


def _lp():
    return SEQ + BLK


def _row_tile(cap):
    lp = _lp()
    return max(t for t in range(16, cap + 1, 16) if lp % t == 0)


def _params(vmem_mb, **kw):
    assert vmem_mb <= VMEM_CAP_MB - VMEM_SPARE_MB
    return pltpu.CompilerParams(vmem_limit_bytes=vmem_mb << 20, **kw)


def _seq(n=1):
    return ("arbitrary",) * n


def _const(shape):
    return pl.BlockSpec(shape, lambda *_: (0,) * len(shape), pipeline_mode=pl.Buffered(1))


def _acc(shape):
    return pl.BlockSpec(shape, lambda *_: (0,) * len(shape))


def _rows(tm, width):
    return pl.BlockSpec((tm, width), lambda i: (i, 0))


def _dot(a, b):
    return jnp.dot(a.astype(BF16), b.astype(BF16), preferred_element_type=F32)


def _dot_nt(a, b):
    return lax.dot_general(a.astype(BF16), b.astype(BF16), (((1,), (1,)), ((), ())), preferred_element_type=F32)


def _dot_tn(a, b):
    return lax.dot_general(a.astype(BF16), b.astype(BF16), (((0,), (0,)), ((), ())), preferred_element_type=F32)


def _dot_exact(a, b):
    return jnp.dot(a, b, precision=lax.Precision.HIGHEST, preferred_element_type=F32)


def _ln_stats(x):
    mu = jnp.mean(x, axis=-1, keepdims=True)
    xc = x - mu
    rstd = lax.rsqrt(jnp.mean(xc * xc, axis=-1, keepdims=True) + LN_EPS)
    return xc * rstd, rstd


def _ln_bwd(dy, xhat, rstd, g):
    dxh = dy * g
    return rstd * (dxh - jnp.mean(dxh, axis=-1, keepdims=True) - xhat * jnp.mean(dxh * xhat, axis=-1, keepdims=True))


def _sigmoid(x):
    return 1.0 / (1.0 + jnp.exp(-x))


def _iota(shape, dim):
    return lax.broadcasted_iota(jnp.int32, shape, dim)


def _hbm(*arrays):
    return tuple(pltpu.with_memory_space_constraint(a, pltpu.HBM) for a in arrays)


def _ln_in_fwd_real(x, g, b, token):
    tr = min(LN_ROWS, SEQ)

    def body(x_ref, g_ref, b_ref, token_ref, h_ref):
        xhat, _ = _ln_stats(x_ref[...])
        h_ref[...] = xhat * g_ref[...] + b_ref[...]

    return pl.pallas_call(
        body, name="ln_in_fwd", grid=(SEQ // tr,),
        in_specs=[_rows(tr, D), _const((1, D)), _const((1, D)), _const(TOKEN)],
        out_specs=_rows(tr, D),
        out_shape=pltpu.HBM((_lp(), D), F32),
        compiler_params=_params(32, dimension_semantics=_seq()),
    )(*_hbm(x, g, b), token)


def _ln_in_fwd_meta(h_real, meta_ext, g, b):
    def meta_body(m_ref, g_ref, b_ref, real_ref, h_ref):
        xhat, _ = _ln_stats(m_ref[...])
        h_ref[...] = xhat * g_ref[...] + b_ref[...]

    return pl.pallas_call(
        meta_body, name="ln_in_fwd_meta", grid=(1,),
        in_specs=[_const((BLK, D)), _const((1, D)), _const((1, D)), pl.BlockSpec(memory_space=pl.ANY)],
        out_specs=pl.BlockSpec((BLK, D), lambda i: (SEQ // BLK, 0)),
        out_shape=pltpu.HBM((_lp(), D), F32),
        input_output_aliases={3: 0},
        compiler_params=_params(16, dimension_semantics=_seq()),
    )(*_hbm(meta_ext, g, b, h_real))


def _in_proj(h0, w_in_t, b_in_p, wg2_p, bg2):
    tm = _row_tile(384)
    lp = _lp()
    widths = (512, 128, 128, 256, 256, 512, 512, 128)
    offs = (O_QS, O_KS, O_VS, O_QG, O_KG, O_VG, O_RG, O_LR)

    def body(h_ref, w_ref, b_ref, wg2_ref, bg2_ref, *outs):
        proj = _dot_nt(h_ref[...], w_ref[...]) + b_ref[...]
        for pos, h in enumerate(HEAD_POS):
            outs[0][:, pos * DH:(pos + 1) * DH] = proj[:, O_QS + h * DH:O_QS + (h + 1) * DH]
        for o_ref, off, wd in zip(outs[1:8], offs[1:], widths[1:]):
            o_ref[...] = proj[:, off:off + wd]
        outs[8][...] = _dot(proj[:, O_LR:O_LR + LANE], wg2_ref[...]) + bg2_ref[...]

    return pl.pallas_call(
        body, name="in_proj", grid=(lp // tm,),
        in_specs=[_rows(tm, D), _const((D_IN_P, D)), _const((1, D_IN_P)), _const((LANE, 256)), _const((1, 256))],
        out_specs=[_rows(tm, w) for w in widths] + [_rows(tm, 256)],
        out_shape=[pltpu.HBM((lp, w), F32) for w in widths] + [pltpu.HBM((lp, 256), F32)],
        compiler_params=_params(40, dimension_semantics=_seq()),
    )(*_hbm(h0, w_in_t, b_in_p, wg2_p, bg2))


def _swa_masks(n):
    nb = SEQ // BLK
    is_meta = n == nb
    ri = _iota((BLK, BLK), 0)
    cj = _iota((BLK, BLK), 1)
    meta_col = ((cj >= META_OFF) & (cj < CH)).astype(jnp.int32)
    meta_q = meta_col * ((cj <= ri) & (ri < CH)).astype(jnp.int32)
    valid_m = jnp.where(is_meta, meta_q, meta_col) > 0
    dist_m = jnp.where(is_meta, ri - cj, n * BLK + ri + CH - cj).astype(F32)
    valid_p = jnp.where((n >= 1) & (n < nb), (cj > ri).astype(jnp.int32), 0) > 0
    dist_p = (ri + BLK - cj).astype(F32)
    valid_c = jnp.where(n < nb, (cj <= ri).astype(jnp.int32), 0) > 0
    dist_c = (ri - cj).astype(F32)
    return (dist_m, dist_p, dist_c), (valid_m, valid_p, valid_c)


def _swa_bias(n):
    dists, valids = _swa_masks(n)
    return (jnp.concatenate([-d for d in dists], axis=1),
            jnp.concatenate([jnp.where(v, 0.0, NEG) for v in valids], axis=1))


def _swa_half(ref, pos, scale=1.0):
    col = ref[:, (pos // 2) * LANE:(pos // 2 + 1) * LANE]
    lane = _iota((BLK, LANE), 1)
    mine = lane < DH if pos % 2 == 0 else lane >= DH
    return jnp.where(mine, col * scale, 0.0).astype(BF16)


def _swa_merge(even, odd):
    return jnp.where(_iota((BLK, LANE), 1) < DH, even, odd)


def _swa_softmax(t, sink):
    m = jnp.maximum(jnp.max(t, axis=-1, keepdims=True), sink)
    e = jnp.exp(t - m)
    e_sink = jnp.exp(sink - m)
    inv = 1.0 / (jnp.sum(e, axis=-1, keepdims=True) + e_sink)
    return e * inv, e_sink * inv


def _swa_kv_specs(width):
    nb = SEQ // BLK
    return [pl.BlockSpec((BLK, width), lambda n: (nb, 0)),
            pl.BlockSpec((BLK, width), lambda n: (jnp.clip(n - 1, 0, nb - 1), 0)),
            pl.BlockSpec((BLK, width), lambda n: (jnp.minimum(n, nb), 0))]


def _swa_fwd(sinks, qs, ks, vs):
    nb = SEQ // BLK
    heads = range(SWA_HEADS)

    def body(sink_ref, q_ref, km_ref, kp_ref, kc_ref, vm_ref, vp_ref, vc_ref, o_ref):
        negdist, maskbias = _swa_bias(pl.program_id(0))
        k_all = jnp.concatenate([km_ref[...], kp_ref[...], kc_ref[...]], axis=0).astype(BF16)
        v_all = jnp.concatenate([vm_ref[...], vp_ref[...], vc_ref[...]], axis=0).astype(BF16)
        q = [_swa_half(q_ref, pos, DH ** -0.5) for pos in heads]
        t = [_dot_nt(q[pos], k_all) + (2.0 ** -(HEAD_POS[pos] + 1) * negdist + maskbias) for pos in heads]
        p = [_swa_softmax(t[pos], sink_ref[HEAD_POS[pos]])[0].astype(BF16) for pos in heads]
        o = [_dot(p[pos], v_all) for pos in heads]
        for col in range(SWA_HEADS // 2):
            o_ref[:, col * LANE:(col + 1) * LANE] = _swa_merge(o[2 * col], o[2 * col + 1])

    kvw = SWA_KV_HEADS * DH
    return pl.pallas_call(
        body, name="swa_fwd", grid=(nb + 1,),
        in_specs=[pl.BlockSpec(memory_space=pltpu.SMEM), _rows(BLK, SWA_HEADS * DH)] + _swa_kv_specs(kvw) + _swa_kv_specs(kvw),
        out_specs=_rows(BLK, SWA_HEADS * DH),
        out_shape=pltpu.HBM((_lp(), SWA_HEADS * DH), F32),
        compiler_params=_params(16, dimension_semantics=_seq()),
    )(sinks, *_hbm(qs, ks, ks, ks, vs, vs, vs))


GLA_PER_STEP = BLK // CH


def _gla_block(s):
    nb = SEQ // BLK
    return jnp.where(s == 0, nb, s - 1)


def _gla_rowmask(s):
    ri = _iota((BLK, 1), 0)
    m = jnp.where(s == 0, ((ri >= META_OFF) & (ri < CH)).astype(jnp.int32), 1)
    return (m > 0).astype(F32) + jnp.zeros((BLK, 1), F32)


def _gla_chunk_masks():
    r, c = _iota((BLK, BLK), 0), _iota((BLK, BLK), 1)
    same = ((r < CH) & (c < CH)) | ((r >= CH) & (c >= CH))
    return same & (r >= c), same & (r <= c), same


def _gla_decay(z, rmask):
    log_g = (jnp.minimum(z, 0.0) - jnp.log1p(jnp.exp(-jnp.abs(z)))) * (rmask / GLA_TAU)
    lower, _, same = _gla_chunk_masks()
    return _dot_exact(lower.astype(F32), log_g), _dot_exact(same.astype(F32), log_g)


def _gla_slices(c, h):
    return slice(c * CH, (c + 1) * CH), slice(h * DK, (h + 1) * DK), slice(h * DV, (h + 1) * DV)


def _gla_fwd(qg, kg, vg, z):
    steps = SEQ // BLK + 1
    kw, vw = GLA_HEADS * DK, GLA_HEADS * DV
    pairs = [(c, h) for c in range(GLA_PER_STEP) for h in range(GLA_HEADS)]

    def body(q_ref, k_ref, v_ref, z_ref, o_ref, st_ref, st):
        s = pl.program_id(0)

        @pl.when(s == 0)
        def _():
            st[...] = jnp.zeros_like(st)

        rmask = _gla_rowmask(s)
        b, b_last = _gla_decay(z_ref[...], rmask)
        q = q_ref[...] * (rmask * DK ** -0.5)
        k = k_ref[...] * rmask
        v = v_ref[...] * rmask
        qe = q * jnp.exp(b)
        ke = k * jnp.exp(-b)
        kd = k * jnp.exp(b_last - b)
        e_last = jnp.exp(b_last)
        causal = _iota((CH, CH), 0) >= _iota((CH, CH), 1)
        a, upd, intra = {}, {}, {}
        for c, h in pairs:
            rows, ks, vs_ = _gla_slices(c, h)
            a[c, h] = jnp.where(causal, _dot_nt(qe[rows, ks], ke[rows, ks]), 0.0)
            upd[c, h] = _dot_tn(v[rows, vs_], kd[rows, ks])
        for c, h in pairs:
            rows, ks, vs_ = _gla_slices(c, h)
            intra[c, h] = _dot(a[c, h], v[rows, vs_])
        state = st[...]
        for c in range(GLA_PER_STEP):
            st_ref[0, c] = state
            for h in range(GLA_HEADS):
                rows, ks, vs_ = _gla_slices(c, h)
                o_ref[rows, vs_] = intra[c, h] + _dot_nt(qe[rows, ks], state[:, ks])
            state = state * e_last[c * CH:c * CH + 1] + jnp.concatenate([upd[c, h] for h in range(GLA_HEADS)], axis=1)
        st[...] = state

    blk = lambda w: pl.BlockSpec((BLK, w), lambda s: (_gla_block(s), 0))
    return pl.pallas_call(
        body, name="gla_fwd", grid=(steps,),
        in_specs=[blk(kw), blk(kw), blk(vw), blk(kw)],
        out_specs=[blk(vw), pl.BlockSpec((1, GLA_PER_STEP, DV, kw), lambda s: (s, 0, 0, 0))],
        out_shape=[pltpu.HBM((_lp(), vw), F32), pltpu.HBM((steps, GLA_PER_STEP, DV, kw), F32)],
        scratch_shapes=[pltpu.VMEM((DV, kw), F32)],
        compiler_params=_params(16, dimension_semantics=_seq()),
    )(*_hbm(qg, kg, vg, z))


def _post_mix(o_s, o_gla, r_g, h0, gn4, w_out, g1, b1, token):
    tm = _row_tile(384)
    lp = _lp()

    def body(os_ref, og_ref, r_ref, h0_ref, gn_ref, w_ref, g_ref, b_ref, token_ref, o_ref, pre_ref, h1_ref):
        for pos, h in enumerate(HEAD_POS):
            o_ref[:, h * DH:(h + 1) * DH] = os_ref[:, pos * DH:(pos + 1) * DH].astype(BF16)
        for h in range(GLA_HEADS):
            hs = slice(h * DV, (h + 1) * DV)
            xg = og_ref[:, hs]
            n = xg * lax.rsqrt(jnp.mean(xg * xg, axis=-1, keepdims=True) + RMS_EPS) * gn_ref[...]
            r = r_ref[:, hs]
            o_ref[:, 512 + h * DV:512 + (h + 1) * DV] = (n * (r * _sigmoid(r))).astype(BF16)
        pre = ALPHA * h0_ref[...] + _dot(o_ref[...], w_ref[...])
        pre_ref[...] = pre
        xhat, _ = _ln_stats(pre)
        h1_ref[...] = xhat * g_ref[...] + b_ref[...]

    return pl.pallas_call(
        body, name="post_mix", grid=(lp // tm,),
        in_specs=[_rows(tm, 512), _rows(tm, 512), _rows(tm, 512), _rows(tm, D), _const((1, DV)), _const((D, D)),
                  _const((1, D)), _const((1, D)), _const(TOKEN)],
        out_specs=[_rows(tm, D), _rows(tm, D), _rows(tm, D)],
        out_shape=[pltpu.HBM((lp, D), BF16), pltpu.HBM((lp, D), F32),
                   pltpu.HBM((lp, D), F32)],
        compiler_params=_params(32, dimension_semantics=_seq()),
    )(*_hbm(o_s, o_gla, r_g, h0, gn4, w_out, g1, b1), token)


def _ffn_fwd_loss_bwd(h1, wg_t, wu_t, wd, target, g2, b2):
    lp = _lp()
    tm = max(t for t in range(BLK, 384 + 1, BLK) if lp % t == 0)
    steps = lp // tm
    last_blk = SEQ // BLK - 1
    half = D_FF // 2
    n_t = tm // BLK

    def body(*refs):
        h_ref, wg_ref, wu_ref, wd_ref = refs[:4]
        t_refs = refs[4:4 + n_t]
        g2_ref, b2_ref, a_ref, dgate_ref, dup_ref, dp_ref, loss_ref, dg_ref, db_ref, g_s, u_s, acc = refs[4 + n_t:]
        i = pl.program_id(0)

        @pl.when(i == 0)
        def _():
            acc[...] = jnp.zeros_like(acc)
            dg_ref[...] = jnp.zeros_like(dg_ref)
            db_ref[...] = jnp.zeros_like(db_ref)

        h = h_ref[...]
        hb = h.astype(BF16)
        pre = ALPHA * h
        for j in range(2):
            cols = slice(j * half, (j + 1) * half)
            g = _dot_nt(hb, wg_ref[cols, :])
            u = _dot_nt(hb, wu_ref[cols, :])
            g_s[:, cols] = g
            u_s[:, cols] = u
            pre = pre + _dot(g * _sigmoid(g) * u, wd_ref[cols, :])
        xhat, rstd = _ln_stats(pre)
        real = i * tm + _iota((tm, 1), 0) < SEQ
        target_rows = jnp.concatenate([t[...] for t in t_refs], axis=0)
        diff = jnp.where(real, xhat * g2_ref[...] + b2_ref[...] - target_rows, 0.0)
        acc[...] += jnp.sum(diff * diff, axis=0, keepdims=True)
        dy = diff * (1.0 / D)
        dpre = _ln_bwd(dy, xhat, rstd, g2_ref[...])
        dp_ref[...] = dpre
        dg_ref[...] += jnp.sum(dy * xhat, axis=0, keepdims=True)
        db_ref[...] += jnp.sum(dy, axis=0, keepdims=True)
        dpb = dpre.astype(BF16)
        for j in range(2):
            cols = slice(j * half, (j + 1) * half)
            g, u = g_s[:, cols], u_s[:, cols]
            sg = _sigmoid(g)
            silu = g * sg
            da = _dot_nt(dpb, wd_ref[cols, :])
            a_ref[:, cols] = (silu * u).astype(BF16)
            dgate_ref[:, cols] = (da * u * (sg * (1.0 + g * (1.0 - sg)))).astype(BF16)
            dup_ref[:, cols] = (da * silu).astype(BF16)

        @pl.when(i == steps - 1)
        def _():
            loss_ref[...] = jnp.zeros_like(loss_ref) + (0.5 / D) * jnp.sum(acc[...], axis=1, keepdims=True)

    t_spec = lambda k: pl.BlockSpec((BLK, D), lambda i: (jnp.minimum(i * n_t + k, last_blk), 0))
    return pl.pallas_call(
        body, name="ffn_fwd_loss_bwd", grid=(steps,),
        in_specs=[_rows(tm, D), _const((D_FF, D)), _const((D_FF, D)), _const((D_FF, D))]
        + [t_spec(k) for k in range(n_t)] + [_const((1, D)), _const((1, D))],
        out_specs=[_rows(tm, D_FF), _rows(tm, D_FF), _rows(tm, D_FF), _rows(tm, D), _acc((1, LANE)), _acc((1, D)),
                   _acc((1, D))],
        out_shape=[pltpu.HBM((lp, D_FF), BF16)] * 3 + [pltpu.HBM((lp, D), F32), pltpu.HBM((1, LANE), F32),
                                                         pltpu.HBM((1, D), F32), pltpu.HBM((1, D), F32)],
        scratch_shapes=[pltpu.VMEM((tm, D_FF), F32), pltpu.VMEM((tm, D_FF), F32), pltpu.VMEM((1, D), F32)],
        compiler_params=_params(58, dimension_semantics=_seq()),
    )(*_hbm(h1, wg_t, wu_t, wd, *[target] * n_t, g2, b2))


def _ffn_out_bwd(dpre2, dgate, dup, pre1, wg_t, wu_t, g1, w_out, o_gla, r_g, gn4):
    tm = _row_tile(384)
    lp = _lp()

    def body(dp_ref, dg_ref, du_ref, p1_ref, wg_ref, wu_ref, g1_ref, w_ref, og_ref, r_ref, gn_ref,
             dp1_ref, dg1_ref, db1_ref, dos_ref, dog_ref, dr_ref, dgn_ref):
        @pl.when(pl.program_id(0) == 0)
        def _():
            for acc_ref in (dg1_ref, db1_ref, dgn_ref):
                acc_ref[...] = jnp.zeros_like(acc_ref)

        dh1 = ALPHA * dp_ref[...] + _dot(dg_ref[...], wg_ref[...]) + _dot(du_ref[...], wu_ref[...])
        xhat, rstd1 = _ln_stats(p1_ref[...])
        dpre1 = _ln_bwd(dh1, xhat, rstd1, g1_ref[...])
        dp1_ref[...] = dpre1
        dg1_ref[...] += jnp.sum(dh1 * xhat, axis=0, keepdims=True)
        db1_ref[...] += jnp.sum(dh1, axis=0, keepdims=True)

        do = _dot_nt(dpre1, w_ref[...])
        for pos, h in enumerate(HEAD_POS):
            dos_ref[:, pos * DH:(pos + 1) * DH] = do[:, h * DH:(h + 1) * DH]
        gn = gn_ref[...]
        for h in range(GLA_HEADS):
            hs = slice(h * DV, (h + 1) * DV)
            xg = og_ref[:, hs]
            rstd = lax.rsqrt(jnp.mean(xg * xg, axis=-1, keepdims=True) + RMS_EPS)
            nx = xg * rstd
            r = r_ref[:, hs]
            sr = _sigmoid(r)
            d_o = do[:, 512 + h * DV:512 + (h + 1) * DV]
            dr_ref[:, hs] = d_o * (nx * gn) * (sr * (1.0 + r * (1.0 - sr)))
            dn = d_o * (r * sr)
            dgn_ref[...] += jnp.sum(dn * nx, axis=0, keepdims=True)
            dnx = dn * gn
            dog_ref[:, hs] = rstd * (dnx - nx * jnp.mean(dnx * nx, axis=-1, keepdims=True))

    return pl.pallas_call(
        body, name="ffn_out_bwd", grid=(lp // tm,),
        in_specs=[_rows(tm, D), _rows(tm, D_FF), _rows(tm, D_FF), _rows(tm, D), _const((D_FF, D)), _const((D_FF, D)),
                  _const((1, D)), _const((D, D)), _rows(tm, 512), _rows(tm, 512), _const((1, DV))],
        out_specs=[_rows(tm, D), _acc((1, D)), _acc((1, D)), _rows(tm, 512), _rows(tm, 512), _rows(tm, 512),
                   _acc((1, DV))],
        out_shape=[pltpu.HBM((lp, D), F32), pltpu.HBM((1, D), F32), pltpu.HBM((1, D), F32)]
        + [pltpu.HBM((lp, 512), F32)] * 3 + [pltpu.HBM((1, DV), F32)],
        compiler_params=_params(48, dimension_semantics=_seq()),
    )(*_hbm(dpre2, dgate, dup, pre1, wg_t, wu_t, g1, w_out, o_gla, r_g, gn4))


def _atb(a, b, name, token=None):
    lp = _lp()
    tm = _row_tile(1408)
    n, w = a.shape[1], b.shape[1]
    bw = 512 if n * w * 4 > (4 << 20) else w
    tokens = [] if token is None else [token]

    def body(a_ref, b_ref, *rest):
        o_ref = rest[-1]

        @pl.when(pl.program_id(1) == 0)
        def _():
            o_ref[...] = jnp.zeros_like(o_ref)

        o_ref[...] += _dot_tn(a_ref[...], b_ref[...])

    return pl.pallas_call(
        body, name=name, grid=(w // bw, lp // tm),
        in_specs=[pl.BlockSpec((tm, n), lambda j, k: (k, 0)), pl.BlockSpec((tm, bw), lambda j, k: (k, j))]
        + [_const(TOKEN)] * len(tokens),
        out_specs=pl.BlockSpec((n, bw), lambda j, k: (0, j)),
        out_shape=pltpu.HBM((n, w), F32),
        compiler_params=_params(48, dimension_semantics=_seq(2)),
    )(*_hbm(a, b), *tokens)


def _gla_bwd(qg, kg, vg, z, do_gla, st_all, token):
    steps = SEQ // BLK + 1
    kw, vw = GLA_HEADS * DK, GLA_HEADS * DV
    pairs = [(c, h) for c in range(GLA_PER_STEP) for h in range(GLA_HEADS)]
    heads = range(GLA_HEADS)

    def body(q_ref, k_ref, v_ref, z_ref, do_ref, st_ref, token_ref, dq_ref, dk_ref, dv_ref, dz_ref, dst):
        @pl.when(pl.program_id(0) == 0)
        def _():
            dst[...] = jnp.zeros_like(dst)

        rmask = _gla_rowmask(steps - 1 - pl.program_id(0))
        zz = z_ref[...]
        b, b_last = _gla_decay(zz, rmask)
        e_b, e_nb, e_kd, e_last = jnp.exp(b), jnp.exp(-b), jnp.exp(b_last - b), jnp.exp(b_last)
        q = q_ref[...] * (rmask * DK ** -0.5)
        k = k_ref[...] * rmask
        v = v_ref[...] * rmask
        qe, ke, kd = q * e_b, k * e_nb, k * e_kd
        d_o = do_ref[...]
        causal = _iota((CH, CH), 0) >= _iota((CH, CH), 1)
        a, da, dqe, dke, dv_intra, carry = {}, {}, {}, {}, {}, {}
        for c, h in pairs:
            rows, ks, vs_ = _gla_slices(c, h)
            a[c, h] = jnp.where(causal, _dot_nt(qe[rows, ks], ke[rows, ks]), 0.0)
            da[c, h] = jnp.where(causal, _dot_nt(d_o[rows, vs_], v[rows, vs_]), 0.0)
            carry[c, h] = _dot_tn(d_o[rows, vs_], qe[rows, ks])
        for c, h in pairs:
            rows, ks, vs_ = _gla_slices(c, h)
            dqe[c, h] = _dot(d_o[rows, vs_], st_ref[0, c][:, ks]) + _dot(da[c, h], ke[rows, ks])
            dke[c, h] = _dot_tn(da[c, h], qe[rows, ks])
            dv_intra[c, h] = _dot_tn(a[c, h], d_o[rows, vs_])
        dstate = dst[...]
        dkd, db_decay = {}, {}
        for c in reversed(range(GLA_PER_STEP)):
            for h in heads:
                rows, ks, vs_ = _gla_slices(c, h)
                dkd[c, h] = _dot(v[rows, vs_], dstate[:, ks])
                dv_ref[rows, vs_] = dv_intra[c, h] + _dot_nt(kd[rows, ks], dstate[:, ks])
            chunk_last = e_last[c * CH:c * CH + 1]
            db_decay[c] = jnp.sum(dstate * st_ref[0, c], axis=0, keepdims=True) * chunk_last
            dstate = dstate * chunk_last + jnp.concatenate([carry[c, h] for h in heads], axis=1)
        dst[...] = dstate
        rows_of = lambda parts: jnp.concatenate(
            [jnp.concatenate([parts[c, h] for h in heads], axis=1) for c in range(GLA_PER_STEP)], axis=0)
        dqe_all, dke_all, dkd_all = rows_of(dqe), rows_of(dke), rows_of(dkd)
        dq_ref[...] = dqe_all * e_b * (rmask * DK ** -0.5)
        dk_ref[...] = (dke_all * e_nb + dkd_all * e_kd) * rmask
        dkd_kd = dkd_all * kd
        db = dqe_all * qe - dke_all * ke - dkd_kd
        _, upper, same = _gla_chunk_masks()
        decay_rows = jnp.concatenate([jnp.broadcast_to(db_decay[c], (CH, kw)) for c in range(GLA_PER_STEP)], axis=0)
        dlog_g = _dot_exact(upper.astype(F32), db) + _dot_exact(same.astype(F32), dkd_kd) + decay_rows
        dz_ref[...] = dlog_g * (rmask / GLA_TAU) * _sigmoid(-zz)

    blk = lambda w: pl.BlockSpec((BLK, w), lambda s: (_gla_block(steps - 1 - s), 0))
    return pl.pallas_call(
        body, name="gla_bwd", grid=(steps,),
        in_specs=[blk(kw), blk(kw), blk(vw), blk(kw), blk(vw),
                  pl.BlockSpec((1, GLA_PER_STEP, DV, kw), lambda s: (steps - 1 - s, 0, 0, 0)), _const(TOKEN)],
        out_specs=[blk(kw), blk(kw), blk(vw), blk(kw)],
        out_shape=[pltpu.HBM((_lp(), kw), F32), pltpu.HBM((_lp(), kw), F32),
                   pltpu.HBM((_lp(), vw), F32), pltpu.HBM((_lp(), kw), F32)],
        scratch_shapes=[pltpu.VMEM((DV, kw), F32)],
        compiler_params=_params(16, dimension_semantics=_seq()),
    )(*_hbm(qg, kg, vg, z, do_gla, st_all), token)


def _swa_bwd(sinks, qs, ks, vs, do_s, token):
    nb = SEQ // BLK
    kvw = SWA_KV_HEADS * DH
    scale = DH ** -0.5
    heads = range(SWA_HEADS)

    def body(sink_ref, q_ref, km_ref, kp_ref, kc_ref, vm_ref, vp_ref, vc_ref, do_ref, token_ref,
             dq_ref, dk_ref, dv_ref, dsink_ref, carry_k, carry_v, meta_k, meta_v):
        n = pl.program_id(0)

        @pl.when(n == 0)
        def _():
            for r in (carry_k, carry_v, meta_k, meta_v):
                r[...] = jnp.zeros_like(r)
            dsink_ref[...] = jnp.zeros_like(dsink_ref)

        @pl.when(n <= nb)
        def _():
            negdist, maskbias = _swa_bias(n)
            lane = _iota((1, LANE), 1)
            k_all = jnp.concatenate([km_ref[...], kp_ref[...], kc_ref[...]], axis=0).astype(BF16)
            v_all = jnp.concatenate([vm_ref[...], vp_ref[...], vc_ref[...]], axis=0).astype(BF16)
            q = [_swa_half(q_ref, pos, scale) for pos in heads]
            d_o = [_swa_half(do_ref, pos) for pos in heads]
            t = [_dot_nt(q[pos], k_all) + (2.0 ** -(HEAD_POS[pos] + 1) * negdist + maskbias) for pos in heads]
            dp = [_dot_nt(d_o[pos], v_all) for pos in heads]
            soft = [_swa_softmax(t[pos], sink_ref[HEAD_POS[pos]]) for pos in heads]
            p = [s[0] for s in soft]
            delta = [jnp.sum(p[pos] * dp[pos], axis=-1, keepdims=True) for pos in heads]
            ds = [(p[pos] * (dp[pos] - delta[pos])).astype(BF16) for pos in heads]
            dq = [_dot(ds[pos], k_all) for pos in heads]
            for col in range(SWA_HEADS // 2):
                dq_ref[:, col * LANE:(col + 1) * LANE] = scale * _swa_merge(dq[2 * col], dq[2 * col + 1])
            dsink = jnp.zeros((1, LANE), F32)
            for pos in heads:
                dsink = dsink + jnp.where(lane == HEAD_POS[pos],
                                          -jnp.sum(soft[pos][1] * delta[pos], axis=0, keepdims=True), 0.0)
            dsink_ref[...] += dsink
            dk3 = _dot_tn(jnp.concatenate(q, axis=0), jnp.concatenate(ds, axis=0)).T
            dv3 = _dot_tn(jnp.concatenate(d_o, axis=0), jnp.concatenate([x.astype(BF16) for x in p], axis=0)).T
            meta_k[...] += dk3[0:BLK]
            meta_v[...] += dv3[0:BLK]
            dk_ref[...] = carry_k[...] + dk3[BLK:2 * BLK]
            dv_ref[...] = carry_v[...] + dv3[BLK:2 * BLK]
            carry_k[...] = dk3[2 * BLK:3 * BLK]
            carry_v[...] = dv3[2 * BLK:3 * BLK]

        @pl.when(n == nb + 1)
        def _():
            dk_ref[...] = meta_k[...]
            dv_ref[...] = meta_v[...]

    kv_out = pl.BlockSpec((BLK, kvw), lambda n: (jnp.where(n == nb + 1, nb, jnp.clip(n - 1, 0, nb - 1)), 0))
    qblk = pl.BlockSpec((BLK, SWA_HEADS * DH), lambda n: (jnp.minimum(n, nb), 0))
    return pl.pallas_call(
        body, name="swa_bwd", grid=(nb + 2,),
        in_specs=[pl.BlockSpec(memory_space=pltpu.SMEM), qblk] + _swa_kv_specs(kvw) + _swa_kv_specs(kvw)
        + [qblk, _const(TOKEN)],
        out_specs=[qblk, kv_out, kv_out, _acc((1, LANE))],
        out_shape=[pltpu.HBM((_lp(), SWA_HEADS * DH), F32), pltpu.HBM((_lp(), kvw), F32),
                   pltpu.HBM((_lp(), kvw), F32), pltpu.HBM((1, LANE), F32)],
        scratch_shapes=[pltpu.VMEM((BLK, kvw), F32)] * 4,
        compiler_params=_params(16, dimension_semantics=_seq()),
    )(sinks, *_hbm(qs, ks, ks, ks, vs, vs, vs, do_s), token)


def _in_bwd(dqs, dks, dvs, dqg, dkg, dvg, drg, dz, dpre1, w_in_t, wg2_p):
    tm = _row_tile(384)
    lp = _lp()
    widths = (512, 128, 128, 256, 256, 512, 512)
    offs = (O_QS, O_KS, O_VS, O_QG, O_KG, O_VG, O_RG)

    def body(*refs):
        parts, (dz_ref, dp1_ref, w_ref, wg2_ref, dproj_ref, dh0_ref, dbin_ref, dbg_ref) = refs[:7], refs[7:]

        @pl.when(pl.program_id(0) == 0)
        def _():
            dbin_ref[...] = jnp.zeros_like(dbin_ref)
            dbg_ref[...] = jnp.zeros_like(dbg_ref)

        for pos, h in enumerate(HEAD_POS):
            val = parts[0][:, pos * DH:(pos + 1) * DH]
            dproj_ref[:, O_QS + h * DH:O_QS + (h + 1) * DH] = val.astype(BF16)
            dbin_ref[:, O_QS + h * DH:O_QS + (h + 1) * DH] += jnp.sum(val, axis=0, keepdims=True)
        for p_ref, off, wd in zip(parts[1:], offs[1:], widths[1:]):
            val = p_ref[...]
            dproj_ref[:, off:off + wd] = val.astype(BF16)
            dbin_ref[:, off:off + wd] += jnp.sum(val, axis=0, keepdims=True)
        dz = dz_ref[...]
        dlr = _dot_nt(dz, wg2_ref[...])
        dproj_ref[:, O_LR:O_LR + LANE] = dlr.astype(BF16)
        dbin_ref[:, O_LR:O_LR + LANE] += jnp.sum(dlr, axis=0, keepdims=True)
        dbg_ref[...] += jnp.sum(dz, axis=0, keepdims=True)
        dh0_ref[...] = ALPHA * dp1_ref[...] + _dot(dproj_ref[...], w_ref[...])

    return pl.pallas_call(
        body, name="in_bwd", grid=(lp // tm,),
        in_specs=[_rows(tm, w) for w in widths] + [_rows(tm, 256), _rows(tm, D), _const((D_IN_P, D)), _const((LANE, 256))],
        out_specs=[_rows(tm, D_IN_P), _rows(tm, D), _acc((1, D_IN_P)), _acc((1, 256))],
        out_shape=[pltpu.HBM((lp, D_IN_P), BF16), pltpu.HBM((lp, D), F32),
                   pltpu.HBM((1, D_IN_P), F32), pltpu.HBM((1, 256), F32)],
        compiler_params=_params(40, dimension_semantics=_seq()),
    )(*_hbm(dqs, dks, dvs, dqg, dkg, dvg, drg, dz, dpre1, w_in_t, wg2_p))


def _ln_in_bwd(x, meta_ext, dh0, g, token):
    tr = min(LN_ROWS, SEQ)

    def ln_bwd(x_ref, dh_ref, g_ref, dx_ref, dg_ref, db_ref):
        @pl.when(pl.program_id(0) == 0)
        def _():
            dg_ref[...] = jnp.zeros_like(dg_ref)
            db_ref[...] = jnp.zeros_like(db_ref)

        xhat, rstd = _ln_stats(x_ref[...])
        dh = dh_ref[...]
        dx_ref[...] = _ln_bwd(dh, xhat, rstd, g_ref[...])
        dg_ref[...] += jnp.sum(dh * xhat, axis=0, keepdims=True)
        db_ref[...] += jnp.sum(dh, axis=0, keepdims=True)

    def body(x_ref, dh_ref, g_ref, token_ref, dx_ref, dg_ref, db_ref):
        ln_bwd(x_ref, dh_ref, g_ref, dx_ref, dg_ref, db_ref)

    def meta_body(m_ref, dh_ref, g_ref, dm_ref, dg_ref, db_ref):
        ln_bwd(m_ref, dh_ref, g_ref, dm_ref, dg_ref, db_ref)

    sums = [pltpu.HBM((1, D), F32), pltpu.HBM((1, D), F32)]
    dx, dg, db = pl.pallas_call(
        body, name="ln_in_bwd", grid=(SEQ // tr,),
        in_specs=[_rows(tr, D), _rows(tr, D), _const((1, D)), _const(TOKEN)],
        out_specs=[_rows(tr, D), _acc((1, D)), _acc((1, D))],
        out_shape=[pltpu.HBM((SEQ, D), F32)] + sums,
        compiler_params=_params(32, dimension_semantics=_seq()),
    )(*_hbm(x, dh0, g), token)
    dm, dg_m, db_m = pl.pallas_call(
        meta_body, name="ln_in_bwd_meta", grid=(1,),
        in_specs=[_const((BLK, D)), pl.BlockSpec((BLK, D), lambda i: (SEQ // BLK, 0)), _const((1, D))],
        out_specs=[_acc((BLK, D)), _acc((1, D)), _acc((1, D))],
        out_shape=[pltpu.HBM((BLK, D), F32)] + sums,
        compiler_params=_params(16, dimension_semantics=_seq()),
    )(*_hbm(meta_ext, dh0, g))
    return dx, dm, dg + dg_m, db + db_m


def _local_step(x, target, ln_in_g, ln_in_b, b_in, bg2, sinks, gn, g1, b1, g2, b2,
                token, fetch_first, fetch_rest, fetch_ffn, exchange_ffn, ship_ffn, ship_w_in):
    row = lambda v: v.reshape(1, -1).astype(F32)
    b_in_p = jnp.pad(row(b_in), ((0, 0), (0, D_IN_P - D_IN)))
    gn4 = row(gn)
    sinks = sinks.reshape(-1).astype(F32)

    h_real = _ln_in_fwd_real(x, row(ln_in_g), row(ln_in_b), token)
    w_in_t, meta_full, wg2 = fetch_first([h_real])
    meta_ext = jnp.pad(meta_full, ((META_OFF, BLK - CH), (0, 0)))
    wg2_p = jnp.pad(wg2, ((0, LANE - wg2.shape[0]), (0, 0))).astype(BF16)
    h0 = _ln_in_fwd_meta(h_real, meta_ext, row(ln_in_g), row(ln_in_b))
    qs, ks, vs, qg, kg, vg, rg, glr, z = _in_proj(h0, w_in_t, b_in_p, wg2_p, row(bg2))
    o_s = _swa_fwd(sinks, qs, ks, vs)
    o_gla, st_all = _gla_fwd(qg, kg, vg, z)
    w_out, token = fetch_rest([o_s, o_gla])
    o, pre1, h1 = _post_mix(o_s, o_gla, rg, h0, gn4, w_out, row(g1), row(b1), token)
    wg_t, wu_t, wd = fetch_ffn([pre1])
    a, dgate, dup, dpre2, loss, dg2, db2 = _ffn_fwd_loss_bwd(h1, wg_t, wu_t, wd, target, row(g2), row(b2))
    dpre1, dg1, db1, do_s, do_gla, drg, dgn = _ffn_out_bwd(dpre2, dgate, dup, pre1, wg_t, wu_t, row(g1), w_out, o_gla,
                                                           rg, gn4)
    dwd = _atb(a, dpre2, "dw_down")
    dwg_t = _atb(dgate, h1, "dw_gate")
    dwu_t = _atb(dup, h1, "dw_up")
    token = exchange_ffn(dict(w_g=dwg_t, w_u=dwu_t, w_d=dwd))
    token = ship_ffn(_atb(o, dpre1, "dw_out", token))
    dqg, dkg, dvg, dz = _gla_bwd(qg, kg, vg, z, do_gla, st_all, token)
    dqs, dks, dvs, dsinks = _swa_bwd(sinks, qs, ks, vs, do_s, token)
    dproj, dh0, db_in_p, dbg2 = _in_bwd(dqs, dks, dvs, dqg, dkg, dvg, drg, dz, dpre1, w_in_t, wg2_p)
    token = ship_w_in(_atb(dproj, h0, "dw_in"))
    dwg2_p = _atb(glr, dz, "dw_gate_lr2")
    dx, dmeta_blk, dg_in, db_in_ln = _ln_in_bwd(x, meta_ext, dh0, row(ln_in_g), token)

    small = dict(meta_blk=dmeta_blk, ln_in_g=dg_in, ln_in_b=db_in_ln, ln1_g=dg1, ln1_b=db1, ln2_g=dg2, ln2_b=db2,
                 b_in_p=db_in_p, wg2_p=dwg2_p, bg2=dbg2, sinks=dsinks, gn=dgn, loss=loss)
    return dx, small


HBM = pl.BlockSpec(memory_space=pltpu.HBM)


def _place():
    return lax.axis_index("x"), lax.axis_index("y"), lax.axis_index("c")


def _other_chips(x, y):
    return [(1 - x, y), (x, 1 - y), (1 - x, 1 - y)]


def _dma_sems(n):
    return pltpu.SemaphoreType.DMA((n,))


def _comm_params():
    return pltpu.CompilerParams(has_side_effects=True)


SEM = pl.BlockSpec(memory_space=pltpu.SEMAPHORE)


PER_ARRAY = dict(gather=3, scatter=3, sibling=N_CHIPS)


def _ici_copies(kind, landing, srcs, lands, send_sems, recv_sems):
    x, y, c = _place()
    mine = 2 * x + y
    copies = []
    for a in range(len(srcs)):
        if kind == "sibling":
            for s in range(N_CHIPS):
                copies.append(pltpu.make_async_remote_copy(
                    srcs[a].at[s, 1 - c], lands[a].at[s], send_sems.at[N_CHIPS * a + s], recv_sems.at[N_CHIPS * a + s],
                    device_id=(x, y, 1 - c), device_id_type=MESH))
            continue
        for j, (px, py) in enumerate(_other_chips(x, y)):
            slab = 2 * px + py if landing else mine
            if kind == "gather":
                src, dst = srcs[a].at[c], lands[a].at[slab, c]
            else:
                src, dst = srcs[a].at[2 * px + py], lands[a].at[slab]
            copies.append(pltpu.make_async_remote_copy(src, dst, send_sems.at[3 * a + j], recv_sems.at[3 * a + j],
                                                       device_id=(px, py, c), device_id_type=MESH))
    return copies


def _split_params():
    return pltpu.CompilerParams(has_side_effects=pltpu.SideEffectType.DATAFLOW_SIDE_EFFECTING)


def _ici_start(kind, srcs, land_shapes, after, name):
    n = len(srcs)
    lands = [pltpu.with_memory_space_constraint(lax.empty(s, a.dtype), pltpu.HBM) for s, a in zip(land_shapes, srcs)]

    def body(*refs):
        outs = refs[2 * n + len(after):]
        for cp in _ici_copies(kind, False, refs[:n], refs[n:2 * n], outs[0], outs[1]):
            cp.start()
        outs[-1][...] = jnp.zeros(TOKEN, F32)

    outs = pl.pallas_call(
        body, name=name, in_specs=[HBM] * (2 * n) + [pl.BlockSpec(memory_space=pl.ANY)] * len(after),
        out_specs=[SEM, SEM] + [HBM] * (2 * n) + [pl.BlockSpec(memory_space=pltpu.VMEM)],
        out_shape=[_dma_sems(PER_ARRAY[kind] * n)] * 2 + [pltpu.HBM(a.shape, a.dtype) for a in list(srcs) + lands]
        + [jax.ShapeDtypeStruct(TOKEN, F32)],
        input_output_aliases={i: 2 + i for i in range(2 * n)},
        compiler_params=_split_params(),
    )(*_hbm(*srcs), *lands, *after)
    return outs[:-1], outs[-1]


def _ici_wait(kind, handle, after, name):
    n = (len(handle) - 2) // 2

    def body(*refs):
        for cp in _ici_copies(kind, True, refs[:n], refs[n:2 * n], refs[2 * n], refs[2 * n + 1]):
            cp.wait_send()
            cp.wait_recv()

    outs = pl.pallas_call(
        body, name=name, in_specs=[HBM] * (2 * n) + [SEM, SEM] + [pl.BlockSpec(memory_space=pl.ANY)] * len(after),
        out_specs=[HBM] * (2 * n), out_shape=[pltpu.HBM(a.shape, a.dtype) for a in handle[2:]],
        input_output_aliases={i: i for i in range(2 * n)},
        compiler_params=_split_params(),
    )(*handle[2:], handle[0], handle[1], *after)
    return list(outs[:n]), list(outs[n:])


def _forward_copies(landing, arrs, send_sems, recv_sems):
    x, y, c = _place()
    copies = []
    for a in range(len(arrs)):
        for j, (px, py) in enumerate(_other_chips(x, y)):
            half = 1 - c if landing else c
            copies.append(pltpu.make_async_remote_copy(
                arrs[a].at[2 * px + py, c], arrs[a].at[2 * px + py, half], send_sems.at[3 * a + j],
                recv_sems.at[3 * a + j], device_id=(x, y, 1 - c), device_id_type=MESH))
    return copies


def _sibling_forward(lands, name):
    n = len(lands)

    def body(*refs):
        outs = refs[n:2 * n]
        send_sems, recv_sems = refs[2 * n:]
        sends = _forward_copies(False, outs, send_sems, recv_sems)
        for cp in sends:
            cp.start()
        for cp in _forward_copies(True, outs, send_sems, recv_sems):
            cp.wait_recv()
        for cp in sends:
            cp.wait_send()

    return pl.pallas_call(
        body, name=name, in_specs=[HBM] * n, out_specs=[HBM] * n,
        out_shape=[pltpu.HBM(a.shape, a.dtype) for a in lands],
        input_output_aliases={a: a for a in range(n)},
        scratch_shapes=[_dma_sems(3 * n)] * 2,
        compiler_params=_comm_params(),
    )(*_hbm(*lands))


def _forward_start(lands, name):
    n = len(lands)

    def body(*refs):
        outs = refs[n:]
        for cp in _forward_copies(False, refs[:n], outs[0], outs[1]):
            cp.start()
        outs[-1][...] = jnp.zeros(TOKEN, F32)

    outs = pl.pallas_call(
        body, name=name, in_specs=[HBM] * n,
        out_specs=[SEM, SEM] + [HBM] * n + [pl.BlockSpec(memory_space=pltpu.VMEM)],
        out_shape=[_dma_sems(3 * n)] * 2 + [pltpu.HBM(a.shape, a.dtype) for a in lands]
        + [jax.ShapeDtypeStruct(TOKEN, F32)],
        input_output_aliases={i: 2 + i for i in range(n)},
        compiler_params=_split_params(),
    )(*_hbm(*lands))
    return outs[:-1], outs[-1]


def _forward_wait(handle, after, name):
    n = len(handle) - 2

    def body(*refs):
        for cp in _forward_copies(True, refs[:n], refs[n], refs[n + 1]):
            cp.wait_send()
            cp.wait_recv()

    return list(pl.pallas_call(
        body, name=name, in_specs=[HBM] * n + [SEM, SEM] + [pl.BlockSpec(memory_space=pl.ANY)] * len(after),
        out_specs=[HBM] * n, out_shape=[pltpu.HBM(a.shape, a.dtype) for a in handle[2:]],
        input_output_aliases={i: i for i in range(n)},
        compiler_params=_split_params(),
    )(*handle[2:], handle[0], handle[1], *after))


def _sibling_exchange(grads, name):
    n = len(grads)

    def body(*refs):
        ins, outs = refs[:n], refs[n:2 * n]
        send_sems, recv_sems = refs[2 * n:]
        x, y, c = _place()
        copies = []
        for a in range(n):
            for s in range(N_CHIPS):
                cp = pltpu.make_async_remote_copy(ins[a].at[s, 1 - c], outs[a].at[s], send_sems.at[N_CHIPS * a + s],
                                                  recv_sems.at[N_CHIPS * a + s], device_id=(x, y, 1 - c),
                                                  device_id_type=MESH)
                cp.start()
                copies.append(cp)
        for cp in copies:
            cp.wait_recv()
        for cp in copies:
            cp.wait_send()

    return pl.pallas_call(
        body, name=name, in_specs=[HBM] * n, out_specs=[HBM] * n,
        out_shape=[pltpu.HBM((N_CHIPS, g.shape[2], D), F32) for g in grads],
        scratch_shapes=[_dma_sems(N_CHIPS * n)] * 2,
        compiler_params=_comm_params(),
    )(*_hbm(*grads))


def _add_halves(core, grads, recvs, dtypes, name):
    n = len(grads)
    heights = [g.shape[2] for g in grads]

    def body(c_ref, *refs):
        for a in range(n):
            refs[2 * n + a][...] = (refs[2 * a][0] + refs[2 * a + 1][...]).astype(dtypes[a])

    slab = lambda h: pl.BlockSpec((1, h, D), lambda s, c: (s, 0, 0))
    mine = lambda h: pl.BlockSpec((1, 1, h, D), lambda s, c: (s, c[0], 0, 0))
    return pl.pallas_call(
        body, name=name,
        grid_spec=pltpu.PrefetchScalarGridSpec(
            num_scalar_prefetch=1, grid=(N_CHIPS,),
            in_specs=[spec(h) for h in heights for spec in (mine, slab)], out_specs=[slab(h) for h in heights]),
        out_shape=[pltpu.HBM((N_CHIPS, h, D), dt) for h, dt in zip(heights, dtypes)],
        compiler_params=_params(32, dimension_semantics=_seq()),
    )(core, *_hbm(*[a for pair in zip(grads, recvs) for a in pair]))


N_DEVICES = 2 * N_CHIPS
PEER_FLIPS = [(dx, dy, dc) for dx in (0, 1) for dy in (0, 1) for dc in (0, 1)][1:]


def _small_exchange(pack, after):
    n = len(PEER_FLIPS)

    def body(p_ref, *refs):
        out_ref, send_sems, recv_sems = refs[len(after):]
        x, y, c = _place()
        flip = lambda v, d: 1 - v if d else v

        def copy(k, landing):
            px, py, pc = (flip(v, d) for v, d in zip((x, y, c), PEER_FLIPS[k]))
            slab = 4 * px + 2 * py + pc if landing else 4 * x + 2 * y + c
            return pltpu.make_async_remote_copy(p_ref, out_ref.at[slab], send_sems.at[k], recv_sems.at[k],
                                                device_id=(px, py, pc), device_id_type=MESH)

        for k in range(n):
            copy(k, False).start()
        for k in range(n):
            copy(k, True).wait_recv()
        for k in range(n):
            copy(k, False).wait_send()

    return pl.pallas_call(
        body, name="small_exchange", in_specs=[HBM] + [pl.BlockSpec(memory_space=pl.ANY)] * len(after), out_specs=HBM,
        out_shape=pltpu.HBM((N_DEVICES,) + pack.shape, F32),
        scratch_shapes=[_dma_sems(n)] * 2,
        compiler_params=_comm_params(),
    )(*_hbm(pack), *after)


def _sum_chips(slots, firsts, rests):
    n = len(firsts)

    def body(i_ref, *refs):
        for a in range(n):
            first, r1, r2, r3 = refs[4 * a:4 * a + 4]
            refs[4 * n + a][...] = ((first[...].astype(F32) + r1[...].astype(F32)) + r2[...].astype(F32)) + r3[...].astype(F32)

    slab = lambda h, k: pl.BlockSpec((1, h, D), lambda i, ix: (ix[k], 0, 0))
    heights = [f.shape[1] for f in firsts]
    return pl.pallas_call(
        body, name="sum_chips",
        grid_spec=pltpu.PrefetchScalarGridSpec(
            num_scalar_prefetch=1, grid=(1,),
            in_specs=[slab(h, k) for h in heights for k in range(4)], out_specs=[slab(h, 4) for h in heights]),
        out_shape=[pltpu.HBM((2, h, D), F32) for h in heights],
        compiler_params=_params(48, dimension_semantics=_seq()),
    )(slots, *_hbm(*[a for f, r in zip(firsts, rests) for a in (f, r, r, r)]))


def _join_halves(halves):
    n = len(halves)

    def body(*refs):
        outs = refs[n:2 * n]
        send_sems, recv_sems = refs[2 * n:]
        x, y, c = _place()

        def copy(a, slab):
            return pltpu.make_async_remote_copy(outs[a].at[slab], outs[a].at[slab], send_sems.at[a], recv_sems.at[a],
                                                device_id=(x, y, 1 - c), device_id_type=MESH)

        for a in range(n):
            copy(a, c).start()
        for a in range(n):
            copy(a, 1 - c).wait_recv()
        for a in range(n):
            copy(a, c).wait_send()

    return pl.pallas_call(
        body, name="join_halves", in_specs=[HBM] * n, out_specs=[HBM] * n,
        out_shape=[pltpu.HBM(h.shape, F32) for h in halves],
        input_output_aliases={a: a for a in range(n)},
        scratch_shapes=[_dma_sems(n)] * 2,
        compiler_params=_comm_params(),
    )(*_hbm(*halves))


def _chip_partials(grads, wire_dtypes, names, fetched=()):
    core = lax.axis_index("c").astype(jnp.int32).reshape(1)
    todo = len(grads) - len(fetched)
    recv = list(_sibling_exchange(grads[:todo], "sibling_exchange_" + names[0])) + list(fetched)
    return list(_add_halves(core, grads, recv, wire_dtypes, "add_halves_" + names[0]))


def _finish_reduce(parts, got):
    x, y, c = _place()
    others = [2 * px + py for px, py in _other_chips(x, y)]
    own_first = jnp.stack([2 * x + y] + others + [c]).astype(jnp.int32)
    return [f.reshape(2 * f.shape[1], D) for f in _join_halves(_sum_chips(own_first, parts, got))]


ADAMW_STEPS = 4


def _adamw(params):
    n = len(params)

    def block(shape):
        rows, cols = shape
        if rows % (8 * ADAMW_STEPS) == 0:
            return pl.BlockSpec((rows // ADAMW_STEPS, cols), lambda i: (i, 0))
        assert cols % (LANE * ADAMW_STEPS) == 0
        return pl.BlockSpec((rows, cols // ADAMW_STEPS), lambda i: (0, i))

    def body(*refs):
        for a in range(n):
            w_ref, g_ref, m_ref, v_ref = refs[4 * a:4 * a + 4]
            outs = refs[4 * n + 3 * a:4 * n + 3 * a + 3]
            outs[0][...], outs[1][...], outs[2][...] = _adamw_math(w_ref[...], g_ref[...], m_ref[...], v_ref[...])

    outs = pl.pallas_call(
        body, name="adamw_matrices", grid=(ADAMW_STEPS,),
        in_specs=[block(p[0].shape) for p in params for _ in range(4)],
        out_specs=[block(p[0].shape) for p in params for _ in range(3)],
        out_shape=[pltpu.HBM(p[0].shape, F32) for p in params for _ in range(3)],
        compiler_params=_params(48, dimension_semantics=_seq()),
    )(*_hbm(*[a for p in params for a in p]))
    return [outs[3 * a:3 * a + 3] for a in range(n)]


def _adamw_math(w, g, m, v):
    nm = ADAM_B1 * m + (1.0 - ADAM_B1) * g
    nv = ADAM_B2 * v + (1.0 - ADAM_B2) * (g * g)
    m_hat = nm / (1.0 - ADAM_B1 ** ADAM_STEP)
    v_hat = nv / (1.0 - ADAM_B2 ** ADAM_STEP)
    return -ADAM_LR * (m_hat / (jnp.sqrt(v_hat) + ADAM_EPS) + ADAM_WD * w), nm, nv


SMALL = (("meta_tokens", (N_META, D // N_CHIPS)), ("ln_in_g", (1, D)), ("ln_in_b", (1, D)), ("b_in", (1, D_IN)),
         ("w_gate_lr2", (GATE_RANK, GLA_HEADS * DK // N_CHIPS)), ("b_gate_lr2", (1, GLA_HEADS * DK)),
         ("attn_sinks", (1, SWA_HEADS)),
         ("gla_norm_g", (1, DV)), ("ln1_g", (1, D)), ("ln1_b", (1, D)), ("ln2_g", (1, D)), ("ln2_b", (1, D)))
ROW_META, ROW_B_IN, ROW_TAIL, ROW_WG2 = 0, 22, 25, 32
ROW_LN = dict(ln_in_g=16, ln_in_b=17, ln1_g=18, ln1_b=19, ln2_g=20, ln2_b=21)
TAIL_BG2, TAIL_SINKS, TAIL_GN, TAIL_LOSS = 0, 256, 256 + SWA_HEADS, 256 + SWA_HEADS + DV


def _adamw_small(place, packs, own, params):
    n = len(SMALL)

    def body(place_ref, packs_ref, own_ref, *refs):
        ins, outs, p_ref = refs[:3 * n], refs[3 * n:-1], refs[-1]
        me, c = place_ref[0], place_ref[1]
        total = jnp.where(me == 0, own_ref[...], packs_ref[0])
        for i in range(1, N_DEVICES):
            total = total + jnp.where(me == i, own_ref[...], packs_ref[i])
        p_ref[...] = total
        outs[4 * n][...] = total[ROW_TAIL:ROW_TAIL + 1, :]

        def mine(width, rows):
            part = lambda s: p_ref[rows, s * width:(s + 1) * width]
            return jnp.where(c == 0, part(0), jnp.where(c == 1, part(1), jnp.where(c == 2, part(2), part(3))))

        tail = lambda lo, width: p_ref[ROW_TAIL:ROW_TAIL + 1, lo:lo + width]
        grads = dict(
            meta_tokens=mine(D // N_CHIPS, slice(ROW_META, ROW_META + N_META)),
            b_in=jnp.concatenate([p_ref[ROW_B_IN:ROW_B_IN + 1, :], p_ref[ROW_B_IN + 1:ROW_B_IN + 2, :],
                                  p_ref[ROW_B_IN + 2:ROW_B_IN + 3, 0:D_IN - 2 * D]], axis=1),
            w_gate_lr2=mine(256 // N_CHIPS, slice(ROW_WG2, ROW_WG2 + 16)),
            b_gate_lr2=tail(TAIL_BG2, 256), attn_sinks=tail(TAIL_SINKS, SWA_HEADS), gla_norm_g=tail(TAIL_GN, DV),
            **{k: p_ref[r:r + 1, :] for k, r in ROW_LN.items()})
        for i, (name, _) in enumerate(SMALL):
            g = grads[name]
            outs[4 * i][...] = g
            outs[4 * i + 1][...], outs[4 * i + 2][...], outs[4 * i + 3][...] = _adamw_math(
                ins[3 * i][...], g, ins[3 * i + 1][...], ins[3 * i + 2][...])

    whole = lambda shape: pl.BlockSpec(shape, lambda i, c: (0,) * len(shape))
    outs = pl.pallas_call(
        body, name="adamw_small",
        grid_spec=pltpu.PrefetchScalarGridSpec(
            num_scalar_prefetch=1, grid=(1,),
            in_specs=[whole(packs.shape), whole(own.shape)] + [whole(s) for _, s in SMALL for _ in range(3)],
            out_specs=[whole(s) for _, s in SMALL for _ in range(4)] + [whole((1, D))],
            scratch_shapes=[pltpu.VMEM(own.shape, F32)]),
        out_shape=[pltpu.HBM(s, F32) for _, s in SMALL for _ in range(4)] + [pltpu.HBM((1, D), F32)],
        compiler_params=_params(16, dimension_semantics=_seq()),
    )(place, *_hbm(packs, own, *[a for p in params for a in p]))
    return [outs[4 * i:4 * i + 4] for i in range(n)], outs[4 * n]


def _small_pack(gr):
    names = ["meta_blk"] + list(ROW_LN) + ["b_in_p", "wg2_p", "bg2", "sinks", "gn", "loss"]
    gate_w = GLA_HEADS * DK

    def body(*refs):
        src, out = dict(zip(names, refs)), refs[-1]
        out[...] = jnp.zeros_like(out)
        out[ROW_META:ROW_META + N_META, :] = src["meta_blk"][META_OFF:CH, :]
        for k, r in ROW_LN.items():
            out[r:r + 1, :] = src[k][...]
        for j in range(-(-D_IN // D)):
            width = min(D, D_IN - j * D)
            out[ROW_B_IN + j:ROW_B_IN + j + 1, 0:width] = src["b_in_p"][:, j * D:j * D + width]
        tail = slice(ROW_TAIL, ROW_TAIL + 1)
        out[tail, TAIL_BG2:TAIL_BG2 + gate_w] = src["bg2"][...]
        out[tail, TAIL_SINKS:TAIL_SINKS + SWA_HEADS] = src["sinks"][:, 0:SWA_HEADS]
        out[tail, TAIL_GN:TAIL_GN + DV] = src["gn"][...]
        out[tail, TAIL_LOSS:TAIL_LOSS + 1] = src["loss"][:, 0:1]
        out[ROW_WG2:ROW_WG2 + GATE_RANK, 0:gate_w] = src["wg2_p"][0:GATE_RANK, :]

    arrays = [gr[k] for k in names]
    return pl.pallas_call(
        body, name="small_pack", grid=(1,),
        in_specs=[_acc(a.shape) for a in arrays], out_specs=_acc((SMALL_ROWS, D)),
        out_shape=pltpu.HBM((SMALL_ROWS, D), F32),
        compiler_params=_params(16, dimension_semantics=_seq()),
    )(*_hbm(*arrays))


BIG = ("w_in", "w_out", "w_g", "w_u", "w_d")


def kernel(x, meta_tokens, ln_in_g, ln_in_b, w_in, b_in, w_gate_lr2, b_gate_lr2, attn_sinks, gla_norm_g, w_out, ln1_g, ln1_b, w_ffn_gate, w_ffn_up, w_ffn_down, ln2_g, ln2_b, loss_target, m_meta_tokens, m_ln_in_g, m_ln_in_b, m_w_in, m_b_in, m_w_gate_lr2, m_b_gate_lr2, m_attn_sinks, m_gla_norm_g, m_w_out, m_ln1_g, m_ln1_b, m_w_ffn_gate, m_w_ffn_up, m_w_ffn_down, m_ln2_g, m_ln2_b, v_meta_tokens, v_ln_in_g, v_ln_in_b, v_w_in, v_b_in, v_w_gate_lr2, v_b_gate_lr2, v_attn_sinks, v_gla_norm_g, v_w_out, v_ln1_g, v_ln1_b, v_w_ffn_gate, v_w_ffn_up, v_w_ffn_down, v_ln2_g, v_ln2_b):
    chip = 2 * lax.axis_index("x") + lax.axis_index("y")

    halves = lambda a: a.reshape(2, a.shape[0] // 2, a.shape[1])
    r_in = SHARD_ROWS["w_in"]
    first = [halves(a) for a in (jnp.pad(w_in[0].T.astype(BF16), ((0, W_IN_WIN - r_in), (0, 0))), meta_tokens,
                                 w_gate_lr2[0])]
    rest = [halves(a) for a in (w_out[0].astype(BF16), w_ffn_gate[0].T.astype(BF16), w_ffn_up[0].T.astype(BF16),
                                w_ffn_down[0].astype(BF16))]
    lands = lambda arrs: [(N_CHIPS,) + a.shape for a in arrs]
    first_handle, first_token = _ici_start("gather", first, lands(first), [], "gather_first_start")
    rest_handle, token = _ici_start("gather", rest, lands(rest), [first_token], "gather_rest_start")
    own_slab = lambda got, shards: [lax.dynamic_update_index_in_dim(g, s, chip, axis=0) for g, s in zip(got, shards)]
    fetching = {}

    def fetch_first(after):
        shards, landed = _ici_wait("gather", first_handle, after, "gather_first_wait")
        g_in, g_meta, g_wg2 = own_slab(_sibling_forward(landed, "gather_first_forward"), shards)
        w_in_t = jnp.pad(g_in.reshape(N_CHIPS, W_IN_WIN, D)[:, :r_in].reshape(D_IN, D), ((0, D_IN_P - D_IN), (0, 0)))
        meta_full = jnp.concatenate([g_meta[s].reshape(N_META, -1) for s in range(N_CHIPS)], axis=1)
        wg2_full = jnp.concatenate([g_wg2[s].reshape(w_gate_lr2.shape[1], -1) for s in range(N_CHIPS)], axis=1)
        return w_in_t, meta_full, wg2_full

    def fetch_rest(after):
        shards, landed = _ici_wait("gather", rest_handle, after, "gather_rest_wait")
        g_out, = own_slab(_sibling_forward(landed[:1], "gather_w_out_forward"), shards[:1])
        fetching["shards"] = shards[1:]
        fetching["handle"], forward_token = _forward_start(landed[1:], "gather_ffn_forward_start")
        return g_out.reshape(-1, D), forward_token

    def fetch_ffn(after):
        got = _forward_wait(fetching["handle"], after, "gather_ffn_forward_wait")
        return [g.reshape(-1, D) for g in own_slab(got, fetching["shards"])]

    sent = {}
    split = lambda grads: [g.reshape(N_CHIPS, 2, -1, D) for g in grads]

    def ship(key, grads, names, fetched=()):
        parts = _chip_partials(grads, [BF16] * len(grads), names, fetched)
        sent[key], ship_token = _ici_start("scatter", parts, [p.shape for p in parts], [], "scatter_" + key + "_start")
        return ship_token

    def exchange_ffn(g):
        grads = split([g[k] for k in BIG[2:]])
        sent["ffn_halves"], exchange_token = _ici_start("sibling", grads, [(N_CHIPS,) + a.shape[2:] for a in grads], [],
                                                        "sibling_ffn_start")
        return exchange_token

    def ship_ffn(dw_out):
        grads, fetched = _ici_wait("sibling", sent["ffn_halves"], [dw_out], "sibling_ffn_wait")
        return ship("ffn", split([dw_out]) + grads, list(BIG[1:]), fetched)

    def ship_w_in(dw_in_t):
        win_start = [s * r_in // BF16_ROWS * BF16_ROWS for s in range(N_CHIPS)]
        return ship("w_in", split([jnp.stack([dw_in_t[st:st + W_IN_WIN] for st in win_start])]), ["w_in"])

    dx, gr = _local_step(
        x[0], loss_target[0], ln_in_g, ln_in_b, b_in[0], b_gate_lr2[0], attn_sinks[0], gla_norm_g[0], ln1_g[0],
        ln1_b[0], ln2_g[0], ln2_b[0], token, fetch_first, fetch_rest, fetch_ffn, exchange_ffn, ship_ffn, ship_w_in)
    ffn_parts, ffn_got = _ici_wait("scatter", sent["ffn"], [dx], "scatter_ffn_wait")
    w_in_parts, w_in_got = _ici_wait("scatter", sent["w_in"], [dx], "scatter_w_in_wait")

    small_own = _small_pack(gr)
    small_all = _small_exchange(small_own, [w_in_got[0]])
    red = _finish_reduce(w_in_parts + ffn_parts, w_in_got + ffn_got)

    big_g = dict(zip(BIG, red))
    big_g["w_in"] = lax.dynamic_slice_in_dim(red[0], chip * (r_in % BF16_ROWS), r_in, axis=0)
    grads = dict(w_in=big_g["w_in"].T[None], w_out=big_g["w_out"][None], w_ffn_gate=big_g["w_g"].T[None],
                 w_ffn_up=big_g["w_u"].T[None], w_ffn_down=big_g["w_d"][None])
    weights = dict(meta_tokens=meta_tokens, ln_in_g=ln_in_g, ln_in_b=ln_in_b, w_in=w_in, b_in=b_in,
                   w_gate_lr2=w_gate_lr2, b_gate_lr2=b_gate_lr2, attn_sinks=attn_sinks, gla_norm_g=gla_norm_g,
                   w_out=w_out, ln1_g=ln1_g, ln1_b=ln1_b, w_ffn_gate=w_ffn_gate, w_ffn_up=w_ffn_up,
                   w_ffn_down=w_ffn_down, ln2_g=ln2_g, ln2_b=ln2_b)
    m_in = dict(meta_tokens=m_meta_tokens, ln_in_g=m_ln_in_g, ln_in_b=m_ln_in_b, w_in=m_w_in, b_in=m_b_in,
                w_gate_lr2=m_w_gate_lr2, b_gate_lr2=m_b_gate_lr2, attn_sinks=m_attn_sinks, gla_norm_g=m_gla_norm_g,
                w_out=m_w_out, ln1_g=m_ln1_g, ln1_b=m_ln1_b, w_ffn_gate=m_w_ffn_gate, w_ffn_up=m_w_ffn_up,
                w_ffn_down=m_w_ffn_down, ln2_g=m_ln2_g, ln2_b=m_ln2_b)
    v_in = dict(meta_tokens=v_meta_tokens, ln_in_g=v_ln_in_g, ln_in_b=v_ln_in_b, w_in=v_w_in, b_in=v_b_in,
                w_gate_lr2=v_w_gate_lr2, b_gate_lr2=v_b_gate_lr2, attn_sinks=v_attn_sinks, gla_norm_g=v_gla_norm_g,
                w_out=v_w_out, ln1_g=v_ln1_g, ln1_b=v_ln1_b, w_ffn_gate=v_w_ffn_gate, w_ffn_up=v_w_ffn_up,
                w_ffn_down=v_w_ffn_down, ln2_g=v_ln2_g, ln2_b=v_ln2_b)
    names = list(weights)
    big_names = ("w_in", "w_out", "w_ffn_gate", "w_ffn_up", "w_ffn_down")

    delta, new_m, new_v = {}, {}, {}
    flips = [(lambda a: a.T) if kk in ("w_in", "w_g", "w_u") else (lambda a: a) for kk in BIG]
    updated = _adamw([(flip(weights[k][0]), big_g[kk], flip(m_in[k][0]), flip(v_in[k][0]))
                      for k, kk, flip in zip(big_names, BIG, flips)])
    for k, flip, results in zip(big_names, flips, updated):
        delta[k], new_m[k], new_v[k] = (flip(t)[None] for t in results)
    small_in = [tuple(src[k].reshape(shape) for src in (weights, m_in, v_in)) for k, shape in SMALL]
    place = jnp.stack([2 * chip + lax.axis_index("c"), chip]).astype(jnp.int32)
    small_out, tail_row = _adamw_small(place, small_all, small_own, small_in)
    for (k, _), results in zip(SMALL, small_out):
        grads[k], delta[k], new_m[k], new_v[k] = (r.reshape(weights[k].shape) for r in results)

    return (tail_row[0, TAIL_LOSS], dx[None], *[grads[k] for k in names], *[delta[k] for k in names], *[new_m[k] for k in names],
            *[new_v[k] for k in names])
```

```python
import jax
import jax.numpy as jnp
from jax import lax
from jax.experimental import pallas as pl
from jax.experimental.pallas import tpu as pltpu

F32 = jnp.float32
BF16 = jnp.bfloat16
MESH = pl.DeviceIdType.MESH

D = 1024
SEQ = 4096
N_META = 16
SWA_HEADS, SWA_KV_HEADS, DH = 8, 2, 64
WINDOW = 128
GLA_HEADS, DK, DV = 4, 64, 128
GLA_TAU = 16.0
CH = 64
D_FF = 2816
D_IN = 2320
LN_EPS = 1e-5
RMS_EPS = 1e-6
ALPHA = 2.0 ** 0.25
NEG = -1e30
ADAM_LR, ADAM_B1, ADAM_B2, ADAM_EPS, ADAM_WD, ADAM_STEP = 0.001, 0.9, 0.999, 1e-8, 0.01, 10
O_QS, O_KS, O_VS, O_QG, O_KG, O_VG, O_RG, O_LR = 0, 512, 640, 768, 1024, 1280, 1792, 2304

LANE = 128
BLK = WINDOW
GATE_RANK = 16
D_IN_P = D_IN + LANE - GATE_RANK
META_OFF = CH - N_META
HEAD_POS = (0, 4, 1, 5, 2, 6, 3, 7)
LN_ROWS = 512
TOKEN = (8, LANE)
N_CHIPS = 4
SHARD_ROWS = dict(w_in=D_IN // N_CHIPS, w_out=D // N_CHIPS, w_g=D_FF // N_CHIPS, w_u=D_FF // N_CHIPS,
                  w_d=D_FF // N_CHIPS)
SMALL_ROWS = 48
BF16_ROWS = 16
W_IN_WIN = -(-SHARD_ROWS["w_in"] // (2 * BF16_ROWS)) * 2 * BF16_ROWS
VMEM_CAP_MB = 64
VMEM_SPARE_MB = 6


def _lp():
    return SEQ + BLK


def _row_tile(cap):
    lp = _lp()
    return max(t for t in range(16, cap + 1, 16) if lp % t == 0)


def _params(vmem_mb, **kw):
    assert vmem_mb <= VMEM_CAP_MB - VMEM_SPARE_MB
    return pltpu.CompilerParams(vmem_limit_bytes=vmem_mb << 20, **kw)


def _seq(n=1):
    return ("arbitrary",) * n


def _const(shape):
    return pl.BlockSpec(shape, lambda *_: (0,) * len(shape), pipeline_mode=pl.Buffered(1))


def _acc(shape):
    return pl.BlockSpec(shape, lambda *_: (0,) * len(shape))


def _rows(tm, width):
    return pl.BlockSpec((tm, width), lambda i: (i, 0))


def _dot(a, b):
    return jnp.dot(a.astype(BF16), b.astype(BF16), preferred_element_type=F32)


def _dot_nt(a, b):
    return lax.dot_general(a.astype(BF16), b.astype(BF16), (((1,), (1,)), ((), ())), preferred_element_type=F32)


def _dot_tn(a, b):
    return lax.dot_general(a.astype(BF16), b.astype(BF16), (((0,), (0,)), ((), ())), preferred_element_type=F32)


def _dot_exact(a, b):
    return jnp.dot(a, b, precision=lax.Precision.HIGHEST, preferred_element_type=F32)


def _ln_stats(x):
    mu = jnp.mean(x, axis=-1, keepdims=True)
    xc = x - mu
    rstd = lax.rsqrt(jnp.mean(xc * xc, axis=-1, keepdims=True) + LN_EPS)
    return xc * rstd, rstd


def _ln_bwd(dy, xhat, rstd, g):
    dxh = dy * g
    return rstd * (dxh - jnp.mean(dxh, axis=-1, keepdims=True) - xhat * jnp.mean(dxh * xhat, axis=-1, keepdims=True))


def _sigmoid(x):
    return 1.0 / (1.0 + jnp.exp(-x))


def _iota(shape, dim):
    return lax.broadcasted_iota(jnp.int32, shape, dim)


def _hbm(*arrays):
    return tuple(pltpu.with_memory_space_constraint(a, pltpu.HBM) for a in arrays)


def _ln_in_fwd_real(x, g, b, token):
    tr = min(LN_ROWS, SEQ)

    def body(x_ref, g_ref, b_ref, token_ref, h_ref):
        xhat, _ = _ln_stats(x_ref[...])
        h_ref[...] = xhat * g_ref[...] + b_ref[...]

    return pl.pallas_call(
        body, name="ln_in_fwd", grid=(SEQ // tr,),
        in_specs=[_rows(tr, D), _const((1, D)), _const((1, D)), _const(TOKEN)],
        out_specs=_rows(tr, D),
        out_shape=pltpu.HBM((_lp(), D), F32),
        compiler_params=_params(32, dimension_semantics=_seq()),
    )(*_hbm(x, g, b), token)


def _ln_in_fwd_meta(h_real, meta_ext, g, b):
    def meta_body(m_ref, g_ref, b_ref, real_ref, h_ref):
        xhat, _ = _ln_stats(m_ref[...])
        h_ref[...] = xhat * g_ref[...] + b_ref[...]

    return pl.pallas_call(
        meta_body, name="ln_in_fwd_meta", grid=(1,),
        in_specs=[_const((BLK, D)), _const((1, D)), _const((1, D)), pl.BlockSpec(memory_space=pl.ANY)],
        out_specs=pl.BlockSpec((BLK, D), lambda i: (SEQ // BLK, 0)),
        out_shape=pltpu.HBM((_lp(), D), F32),
        input_output_aliases={3: 0},
        compiler_params=_params(16, dimension_semantics=_seq()),
    )(*_hbm(meta_ext, g, b, h_real))


def _in_proj(h0, w_in_t, b_in_p, wg2_p, bg2):
    tm = _row_tile(384)
    lp = _lp()
    widths = (512, 128, 128, 256, 256, 512, 512, 128)
    offs = (O_QS, O_KS, O_VS, O_QG, O_KG, O_VG, O_RG, O_LR)

    def body(h_ref, w_ref, b_ref, wg2_ref, bg2_ref, *outs):
        proj = _dot_nt(h_ref[...], w_ref[...]) + b_ref[...]
        for pos, h in enumerate(HEAD_POS):
            outs[0][:, pos * DH:(pos + 1) * DH] = proj[:, O_QS + h * DH:O_QS + (h + 1) * DH]
        for o_ref, off, wd in zip(outs[1:8], offs[1:], widths[1:]):
            o_ref[...] = proj[:, off:off + wd]
        outs[8][...] = _dot(proj[:, O_LR:O_LR + LANE], wg2_ref[...]) + bg2_ref[...]

    return pl.pallas_call(
        body, name="in_proj", grid=(lp // tm,),
        in_specs=[_rows(tm, D), _const((D_IN_P, D)), _const((1, D_IN_P)), _const((LANE, 256)), _const((1, 256))],
        out_specs=[_rows(tm, w) for w in widths] + [_rows(tm, 256)],
        out_shape=[pltpu.HBM((lp, w), F32) for w in widths] + [pltpu.HBM((lp, 256), F32)],
        compiler_params=_params(40, dimension_semantics=_seq()),
    )(*_hbm(h0, w_in_t, b_in_p, wg2_p, bg2))


def _swa_masks(n):
    nb = SEQ // BLK
    is_meta = n == nb
    ri = _iota((BLK, BLK), 0)
    cj = _iota((BLK, BLK), 1)
    meta_col = ((cj >= META_OFF) & (cj < CH)).astype(jnp.int32)
    meta_q = meta_col * ((cj <= ri) & (ri < CH)).astype(jnp.int32)
    valid_m = jnp.where(is_meta, meta_q, meta_col) > 0
    dist_m = jnp.where(is_meta, ri - cj, n * BLK + ri + CH - cj).astype(F32)
    valid_p = jnp.where((n >= 1) & (n < nb), (cj > ri).astype(jnp.int32), 0) > 0
    dist_p = (ri + BLK - cj).astype(F32)
    valid_c = jnp.where(n < nb, (cj <= ri).astype(jnp.int32), 0) > 0
    dist_c = (ri - cj).astype(F32)
    return (dist_m, dist_p, dist_c), (valid_m, valid_p, valid_c)


def _swa_bias(n):
    dists, valids = _swa_masks(n)
    return (jnp.concatenate([-d for d in dists], axis=1),
            jnp.concatenate([jnp.where(v, 0.0, NEG) for v in valids], axis=1))


def _swa_half(ref, pos, scale=1.0):
    col = ref[:, (pos // 2) * LANE:(pos // 2 + 1) * LANE]
    lane = _iota((BLK, LANE), 1)
    mine = lane < DH if pos % 2 == 0 else lane >= DH
    return jnp.where(mine, col * scale, 0.0).astype(BF16)


def _swa_merge(even, odd):
    return jnp.where(_iota((BLK, LANE), 1) < DH, even, odd)


def _swa_softmax(t, sink):
    m = jnp.maximum(jnp.max(t, axis=-1, keepdims=True), sink)
    e = jnp.exp(t - m)
    e_sink = jnp.exp(sink - m)
    inv = 1.0 / (jnp.sum(e, axis=-1, keepdims=True) + e_sink)
    return e * inv, e_sink * inv


def _swa_kv_specs(width):
    nb = SEQ // BLK
    return [pl.BlockSpec((BLK, width), lambda n: (nb, 0)),
            pl.BlockSpec((BLK, width), lambda n: (jnp.clip(n - 1, 0, nb - 1), 0)),
            pl.BlockSpec((BLK, width), lambda n: (jnp.minimum(n, nb), 0))]


def _swa_fwd(sinks, qs, ks, vs):
    nb = SEQ // BLK
    heads = range(SWA_HEADS)

    def body(sink_ref, q_ref, km_ref, kp_ref, kc_ref, vm_ref, vp_ref, vc_ref, o_ref):
        negdist, maskbias = _swa_bias(pl.program_id(0))
        k_all = jnp.concatenate([km_ref[...], kp_ref[...], kc_ref[...]], axis=0).astype(BF16)
        v_all = jnp.concatenate([vm_ref[...], vp_ref[...], vc_ref[...]], axis=0).astype(BF16)
        q = [_swa_half(q_ref, pos, DH ** -0.5) for pos in heads]
        t = [_dot_nt(q[pos], k_all) + (2.0 ** -(HEAD_POS[pos] + 1) * negdist + maskbias) for pos in heads]
        p = [_swa_softmax(t[pos], sink_ref[HEAD_POS[pos]])[0].astype(BF16) for pos in heads]
        o = [_dot(p[pos], v_all) for pos in heads]
        for col in range(SWA_HEADS // 2):
            o_ref[:, col * LANE:(col + 1) * LANE] = _swa_merge(o[2 * col], o[2 * col + 1])

    kvw = SWA_KV_HEADS * DH
    return pl.pallas_call(
        body, name="swa_fwd", grid=(nb + 1,),
        in_specs=[pl.BlockSpec(memory_space=pltpu.SMEM), _rows(BLK, SWA_HEADS * DH)] + _swa_kv_specs(kvw) + _swa_kv_specs(kvw),
        out_specs=_rows(BLK, SWA_HEADS * DH),
        out_shape=pltpu.HBM((_lp(), SWA_HEADS * DH), F32),
        compiler_params=_params(16, dimension_semantics=_seq()),
    )(sinks, *_hbm(qs, ks, ks, ks, vs, vs, vs))


GLA_PER_STEP = BLK // CH


def _gla_block(s):
    nb = SEQ // BLK
    return jnp.where(s == 0, nb, s - 1)


def _gla_rowmask(s):
    ri = _iota((BLK, 1), 0)
    m = jnp.where(s == 0, ((ri >= META_OFF) & (ri < CH)).astype(jnp.int32), 1)
    return (m > 0).astype(F32) + jnp.zeros((BLK, 1), F32)


def _gla_chunk_masks():
    r, c = _iota((BLK, BLK), 0), _iota((BLK, BLK), 1)
    same = ((r < CH) & (c < CH)) | ((r >= CH) & (c >= CH))
    return same & (r >= c), same & (r <= c), same


def _gla_decay(z, rmask):
    log_g = (jnp.minimum(z, 0.0) - jnp.log1p(jnp.exp(-jnp.abs(z)))) * (rmask / GLA_TAU)
    lower, _, same = _gla_chunk_masks()
    return _dot_exact(lower.astype(F32), log_g), _dot_exact(same.astype(F32), log_g)


def _gla_slices(c, h):
    return slice(c * CH, (c + 1) * CH), slice(h * DK, (h + 1) * DK), slice(h * DV, (h + 1) * DV)


def _gla_fwd(qg, kg, vg, z):
    steps = SEQ // BLK + 1
    kw, vw = GLA_HEADS * DK, GLA_HEADS * DV
    pairs = [(c, h) for c in range(GLA_PER_STEP) for h in range(GLA_HEADS)]

    def body(q_ref, k_ref, v_ref, z_ref, o_ref, st_ref, st):
        s = pl.program_id(0)

        @pl.when(s == 0)
        def _():
            st[...] = jnp.zeros_like(st)

        rmask = _gla_rowmask(s)
        b, b_last = _gla_decay(z_ref[...], rmask)
        q = q_ref[...] * (rmask * DK ** -0.5)
        k = k_ref[...] * rmask
        v = v_ref[...] * rmask
        qe = q * jnp.exp(b)
        ke = k * jnp.exp(-b)
        kd = k * jnp.exp(b_last - b)
        e_last = jnp.exp(b_last)
        causal = _iota((CH, CH), 0) >= _iota((CH, CH), 1)
        a, upd, intra = {}, {}, {}
        for c, h in pairs:
            rows, ks, vs_ = _gla_slices(c, h)
            a[c, h] = jnp.where(causal, _dot_nt(qe[rows, ks], ke[rows, ks]), 0.0)
            upd[c, h] = _dot_tn(v[rows, vs_], kd[rows, ks])
        for c, h in pairs:
            rows, ks, vs_ = _gla_slices(c, h)
            intra[c, h] = _dot(a[c, h], v[rows, vs_])
        state = st[...]
        for c in range(GLA_PER_STEP):
            st_ref[0, c] = state
            for h in range(GLA_HEADS):
                rows, ks, vs_ = _gla_slices(c, h)
                o_ref[rows, vs_] = intra[c, h] + _dot_nt(qe[rows, ks], state[:, ks])
            state = state * e_last[c * CH:c * CH + 1] + jnp.concatenate([upd[c, h] for h in range(GLA_HEADS)], axis=1)
        st[...] = state

    blk = lambda w: pl.BlockSpec((BLK, w), lambda s: (_gla_block(s), 0))
    return pl.pallas_call(
        body, name="gla_fwd", grid=(steps,),
        in_specs=[blk(kw), blk(kw), blk(vw), blk(kw)],
        out_specs=[blk(vw), pl.BlockSpec((1, GLA_PER_STEP, DV, kw), lambda s: (s, 0, 0, 0))],
        out_shape=[pltpu.HBM((_lp(), vw), F32), pltpu.HBM((steps, GLA_PER_STEP, DV, kw), F32)],
        scratch_shapes=[pltpu.VMEM((DV, kw), F32)],
        compiler_params=_params(16, dimension_semantics=_seq()),
    )(*_hbm(qg, kg, vg, z))


def _post_mix(o_s, o_gla, r_g, h0, gn4, w_out, g1, b1, token):
    tm = _row_tile(384)
    lp = _lp()

    def body(os_ref, og_ref, r_ref, h0_ref, gn_ref, w_ref, g_ref, b_ref, token_ref, o_ref, pre_ref, h1_ref):
        for pos, h in enumerate(HEAD_POS):
            o_ref[:, h * DH:(h + 1) * DH] = os_ref[:, pos * DH:(pos + 1) * DH].astype(BF16)
        for h in range(GLA_HEADS):
            hs = slice(h * DV, (h + 1) * DV)
            xg = og_ref[:, hs]
            n = xg * lax.rsqrt(jnp.mean(xg * xg, axis=-1, keepdims=True) + RMS_EPS) * gn_ref[...]
            r = r_ref[:, hs]
            o_ref[:, 512 + h * DV:512 + (h + 1) * DV] = (n * (r * _sigmoid(r))).astype(BF16)
        pre = ALPHA * h0_ref[...] + _dot(o_ref[...], w_ref[...])
        pre_ref[...] = pre
        xhat, _ = _ln_stats(pre)
        h1_ref[...] = xhat * g_ref[...] + b_ref[...]

    return pl.pallas_call(
        body, name="post_mix", grid=(lp // tm,),
        in_specs=[_rows(tm, 512), _rows(tm, 512), _rows(tm, 512), _rows(tm, D), _const((1, DV)), _const((D, D)),
                  _const((1, D)), _const((1, D)), _const(TOKEN)],
        out_specs=[_rows(tm, D), _rows(tm, D), _rows(tm, D)],
        out_shape=[pltpu.HBM((lp, D), BF16), pltpu.HBM((lp, D), F32),
                   pltpu.HBM((lp, D), F32)],
        compiler_params=_params(32, dimension_semantics=_seq()),
    )(*_hbm(o_s, o_gla, r_g, h0, gn4, w_out, g1, b1), token)


def _ffn_fwd_loss_bwd(h1, wg_t, wu_t, wd, target, g2, b2):
    lp = _lp()
    tm = max(t for t in range(BLK, 384 + 1, BLK) if lp % t == 0)
    steps = lp // tm
    last_blk = SEQ // BLK - 1
    half = D_FF // 2
    n_t = tm // BLK

    def body(*refs):
        h_ref, wg_ref, wu_ref, wd_ref = refs[:4]
        t_refs = refs[4:4 + n_t]
        g2_ref, b2_ref, a_ref, dgate_ref, dup_ref, dp_ref, loss_ref, dg_ref, db_ref, g_s, u_s, acc = refs[4 + n_t:]
        i = pl.program_id(0)

        @pl.when(i == 0)
        def _():
            acc[...] = jnp.zeros_like(acc)
            dg_ref[...] = jnp.zeros_like(dg_ref)
            db_ref[...] = jnp.zeros_like(db_ref)

        h = h_ref[...]
        hb = h.astype(BF16)
        pre = ALPHA * h
        for j in range(2):
            cols = slice(j * half, (j + 1) * half)
            g = _dot_nt(hb, wg_ref[cols, :])
            u = _dot_nt(hb, wu_ref[cols, :])
            g_s[:, cols] = g
            u_s[:, cols] = u
            pre = pre + _dot(g * _sigmoid(g) * u, wd_ref[cols, :])
        xhat, rstd = _ln_stats(pre)
        real = i * tm + _iota((tm, 1), 0) < SEQ
        target_rows = jnp.concatenate([t[...] for t in t_refs], axis=0)
        diff = jnp.where(real, xhat * g2_ref[...] + b2_ref[...] - target_rows, 0.0)
        acc[...] += jnp.sum(diff * diff, axis=0, keepdims=True)
        dy = diff * (1.0 / D)
        dpre = _ln_bwd(dy, xhat, rstd, g2_ref[...])
        dp_ref[...] = dpre
        dg_ref[...] += jnp.sum(dy * xhat, axis=0, keepdims=True)
        db_ref[...] += jnp.sum(dy, axis=0, keepdims=True)
        dpb = dpre.astype(BF16)
        for j in range(2):
            cols = slice(j * half, (j + 1) * half)
            g, u = g_s[:, cols], u_s[:, cols]
            sg = _sigmoid(g)
            silu = g * sg
            da = _dot_nt(dpb, wd_ref[cols, :])
            a_ref[:, cols] = (silu * u).astype(BF16)
            dgate_ref[:, cols] = (da * u * (sg * (1.0 + g * (1.0 - sg)))).astype(BF16)
            dup_ref[:, cols] = (da * silu).astype(BF16)

        @pl.when(i == steps - 1)
        def _():
            loss_ref[...] = jnp.zeros_like(loss_ref) + (0.5 / D) * jnp.sum(acc[...], axis=1, keepdims=True)

    t_spec = lambda k: pl.BlockSpec((BLK, D), lambda i: (jnp.minimum(i * n_t + k, last_blk), 0))
    return pl.pallas_call(
        body, name="ffn_fwd_loss_bwd", grid=(steps,),
        in_specs=[_rows(tm, D), _const((D_FF, D)), _const((D_FF, D)), _const((D_FF, D))]
        + [t_spec(k) for k in range(n_t)] + [_const((1, D)), _const((1, D))],
        out_specs=[_rows(tm, D_FF), _rows(tm, D_FF), _rows(tm, D_FF), _rows(tm, D), _acc((1, LANE)), _acc((1, D)),
                   _acc((1, D))],
        out_shape=[pltpu.HBM((lp, D_FF), BF16)] * 3 + [pltpu.HBM((lp, D), F32), pltpu.HBM((1, LANE), F32),
                                                         pltpu.HBM((1, D), F32), pltpu.HBM((1, D), F32)],
        scratch_shapes=[pltpu.VMEM((tm, D_FF), F32), pltpu.VMEM((tm, D_FF), F32), pltpu.VMEM((1, D), F32)],
        compiler_params=_params(58, dimension_semantics=_seq()),
    )(*_hbm(h1, wg_t, wu_t, wd, *[target] * n_t, g2, b2))


def _ffn_out_bwd(dpre2, dgate, dup, pre1, wg_t, wu_t, g1, w_out, o_gla, r_g, gn4):
    tm = _row_tile(384)
    lp = _lp()

    def body(dp_ref, dg_ref, du_ref, p1_ref, wg_ref, wu_ref, g1_ref, w_ref, og_ref, r_ref, gn_ref,
             dp1_ref, dg1_ref, db1_ref, dos_ref, dog_ref, dr_ref, dgn_ref):
        @pl.when(pl.program_id(0) == 0)
        def _():
            for acc_ref in (dg1_ref, db1_ref, dgn_ref):
                acc_ref[...] = jnp.zeros_like(acc_ref)

        dh1 = ALPHA * dp_ref[...] + _dot(dg_ref[...], wg_ref[...]) + _dot(du_ref[...], wu_ref[...])
        xhat, rstd1 = _ln_stats(p1_ref[...])
        dpre1 = _ln_bwd(dh1, xhat, rstd1, g1_ref[...])
        dp1_ref[...] = dpre1
        dg1_ref[...] += jnp.sum(dh1 * xhat, axis=0, keepdims=True)
        db1_ref[...] += jnp.sum(dh1, axis=0, keepdims=True)

        do = _dot_nt(dpre1, w_ref[...])
        for pos, h in enumerate(HEAD_POS):
            dos_ref[:, pos * DH:(pos + 1) * DH] = do[:, h * DH:(h + 1) * DH]
        gn = gn_ref[...]
        for h in range(GLA_HEADS):
            hs = slice(h * DV, (h + 1) * DV)
            xg = og_ref[:, hs]
            rstd = lax.rsqrt(jnp.mean(xg * xg, axis=-1, keepdims=True) + RMS_EPS)
            nx = xg * rstd
            r = r_ref[:, hs]
            sr = _sigmoid(r)
            d_o = do[:, 512 + h * DV:512 + (h + 1) * DV]
            dr_ref[:, hs] = d_o * (nx * gn) * (sr * (1.0 + r * (1.0 - sr)))
            dn = d_o * (r * sr)
            dgn_ref[...] += jnp.sum(dn * nx, axis=0, keepdims=True)
            dnx = dn * gn
            dog_ref[:, hs] = rstd * (dnx - nx * jnp.mean(dnx * nx, axis=-1, keepdims=True))

    return pl.pallas_call(
        body, name="ffn_out_bwd", grid=(lp // tm,),
        in_specs=[_rows(tm, D), _rows(tm, D_FF), _rows(tm, D_FF), _rows(tm, D), _const((D_FF, D)), _const((D_FF, D)),
                  _const((1, D)), _const((D, D)), _rows(tm, 512), _rows(tm, 512), _const((1, DV))],
        out_specs=[_rows(tm, D), _acc((1, D)), _acc((1, D)), _rows(tm, 512), _rows(tm, 512), _rows(tm, 512),
                   _acc((1, DV))],
        out_shape=[pltpu.HBM((lp, D), F32), pltpu.HBM((1, D), F32), pltpu.HBM((1, D), F32)]
        + [pltpu.HBM((lp, 512), F32)] * 3 + [pltpu.HBM((1, DV), F32)],
        compiler_params=_params(48, dimension_semantics=_seq()),
    )(*_hbm(dpre2, dgate, dup, pre1, wg_t, wu_t, g1, w_out, o_gla, r_g, gn4))


def _atb(a, b, name, token=None):
    lp = _lp()
    tm = _row_tile(1408)
    n, w = a.shape[1], b.shape[1]
    bw = 512 if n * w * 4 > (4 << 20) else w
    tokens = [] if token is None else [token]

    def body(a_ref, b_ref, *rest):
        o_ref = rest[-1]

        @pl.when(pl.program_id(1) == 0)
        def _():
            o_ref[...] = jnp.zeros_like(o_ref)

        o_ref[...] += _dot_tn(a_ref[...], b_ref[...])

    return pl.pallas_call(
        body, name=name, grid=(w // bw, lp // tm),
        in_specs=[pl.BlockSpec((tm, n), lambda j, k: (k, 0)), pl.BlockSpec((tm, bw), lambda j, k: (k, j))]
        + [_const(TOKEN)] * len(tokens),
        out_specs=pl.BlockSpec((n, bw), lambda j, k: (0, j)),
        out_shape=pltpu.HBM((n, w), F32),
        compiler_params=_params(48, dimension_semantics=_seq(2)),
    )(*_hbm(a, b), *tokens)


def _gla_bwd(qg, kg, vg, z, do_gla, st_all, token):
    steps = SEQ // BLK + 1
    kw, vw = GLA_HEADS * DK, GLA_HEADS * DV
    pairs = [(c, h) for c in range(GLA_PER_STEP) for h in range(GLA_HEADS)]
    heads = range(GLA_HEADS)

    def body(q_ref, k_ref, v_ref, z_ref, do_ref, st_ref, token_ref, dq_ref, dk_ref, dv_ref, dz_ref, dst):
        @pl.when(pl.program_id(0) == 0)
        def _():
            dst[...] = jnp.zeros_like(dst)

        rmask = _gla_rowmask(steps - 1 - pl.program_id(0))
        zz = z_ref[...]
        b, b_last = _gla_decay(zz, rmask)
        e_b, e_nb, e_kd, e_last = jnp.exp(b), jnp.exp(-b), jnp.exp(b_last - b), jnp.exp(b_last)
        q = q_ref[...] * (rmask * DK ** -0.5)
        k = k_ref[...] * rmask
        v = v_ref[...] * rmask
        qe, ke, kd = q * e_b, k * e_nb, k * e_kd
        d_o = do_ref[...]
        causal = _iota((CH, CH), 0) >= _iota((CH, CH), 1)
        a, da, dqe, dke, dv_intra, carry = {}, {}, {}, {}, {}, {}
        for c, h in pairs:
            rows, ks, vs_ = _gla_slices(c, h)
            a[c, h] = jnp.where(causal, _dot_nt(qe[rows, ks], ke[rows, ks]), 0.0)
            da[c, h] = jnp.where(causal, _dot_nt(d_o[rows, vs_], v[rows, vs_]), 0.0)
            carry[c, h] = _dot_tn(d_o[rows, vs_], qe[rows, ks])
        for c, h in pairs:
            rows, ks, vs_ = _gla_slices(c, h)
            dqe[c, h] = _dot(d_o[rows, vs_], st_ref[0, c][:, ks]) + _dot(da[c, h], ke[rows, ks])
            dke[c, h] = _dot_tn(da[c, h], qe[rows, ks])
            dv_intra[c, h] = _dot_tn(a[c, h], d_o[rows, vs_])
        dstate = dst[...]
        dkd, db_decay = {}, {}
        for c in reversed(range(GLA_PER_STEP)):
            for h in heads:
                rows, ks, vs_ = _gla_slices(c, h)
                dkd[c, h] = _dot(v[rows, vs_], dstate[:, ks])
                dv_ref[rows, vs_] = dv_intra[c, h] + _dot_nt(kd[rows, ks], dstate[:, ks])
            chunk_last = e_last[c * CH:c * CH + 1]
            db_decay[c] = jnp.sum(dstate * st_ref[0, c], axis=0, keepdims=True) * chunk_last
            dstate = dstate * chunk_last + jnp.concatenate([carry[c, h] for h in heads], axis=1)
        dst[...] = dstate
        rows_of = lambda parts: jnp.concatenate(
            [jnp.concatenate([parts[c, h] for h in heads], axis=1) for c in range(GLA_PER_STEP)], axis=0)
        dqe_all, dke_all, dkd_all = rows_of(dqe), rows_of(dke), rows_of(dkd)
        dq_ref[...] = dqe_all * e_b * (rmask * DK ** -0.5)
        dk_ref[...] = (dke_all * e_nb + dkd_all * e_kd) * rmask
        dkd_kd = dkd_all * kd
        db = dqe_all * qe - dke_all * ke - dkd_kd
        _, upper, same = _gla_chunk_masks()
        decay_rows = jnp.concatenate([jnp.broadcast_to(db_decay[c], (CH, kw)) for c in range(GLA_PER_STEP)], axis=0)
        dlog_g = _dot_exact(upper.astype(F32), db) + _dot_exact(same.astype(F32), dkd_kd) + decay_rows
        dz_ref[...] = dlog_g * (rmask / GLA_TAU) * _sigmoid(-zz)

    blk = lambda w: pl.BlockSpec((BLK, w), lambda s: (_gla_block(steps - 1 - s), 0))
    return pl.pallas_call(
        body, name="gla_bwd", grid=(steps,),
        in_specs=[blk(kw), blk(kw), blk(vw), blk(kw), blk(vw),
                  pl.BlockSpec((1, GLA_PER_STEP, DV, kw), lambda s: (steps - 1 - s, 0, 0, 0)), _const(TOKEN)],
        out_specs=[blk(kw), blk(kw), blk(vw), blk(kw)],
        out_shape=[pltpu.HBM((_lp(), kw), F32), pltpu.HBM((_lp(), kw), F32),
                   pltpu.HBM((_lp(), vw), F32), pltpu.HBM((_lp(), kw), F32)],
        scratch_shapes=[pltpu.VMEM((DV, kw), F32)],
        compiler_params=_params(16, dimension_semantics=_seq()),
    )(*_hbm(qg, kg, vg, z, do_gla, st_all), token)


def _swa_bwd(sinks, qs, ks, vs, do_s, token):
    nb = SEQ // BLK
    kvw = SWA_KV_HEADS * DH
    scale = DH ** -0.5
    heads = range(SWA_HEADS)

    def body(sink_ref, q_ref, km_ref, kp_ref, kc_ref, vm_ref, vp_ref, vc_ref, do_ref, token_ref,
             dq_ref, dk_ref, dv_ref, dsink_ref, carry_k, carry_v, meta_k, meta_v):
        n = pl.program_id(0)

        @pl.when(n == 0)
        def _():
            for r in (carry_k, carry_v, meta_k, meta_v):
                r[...] = jnp.zeros_like(r)
            dsink_ref[...] = jnp.zeros_like(dsink_ref)

        @pl.when(n <= nb)
        def _():
            negdist, maskbias = _swa_bias(n)
            lane = _iota((1, LANE), 1)
            k_all = jnp.concatenate([km_ref[...], kp_ref[...], kc_ref[...]], axis=0).astype(BF16)
            v_all = jnp.concatenate([vm_ref[...], vp_ref[...], vc_ref[...]], axis=0).astype(BF16)
            q = [_swa_half(q_ref, pos, scale) for pos in heads]
            d_o = [_swa_half(do_ref, pos) for pos in heads]
            t = [_dot_nt(q[pos], k_all) + (2.0 ** -(HEAD_POS[pos] + 1) * negdist + maskbias) for pos in heads]
            dp = [_dot_nt(d_o[pos], v_all) for pos in heads]
            soft = [_swa_softmax(t[pos], sink_ref[HEAD_POS[pos]]) for pos in heads]
            p = [s[0] for s in soft]
            delta = [jnp.sum(p[pos] * dp[pos], axis=-1, keepdims=True) for pos in heads]
            ds = [(p[pos] * (dp[pos] - delta[pos])).astype(BF16) for pos in heads]
            dq = [_dot(ds[pos], k_all) for pos in heads]
            for col in range(SWA_HEADS // 2):
                dq_ref[:, col * LANE:(col + 1) * LANE] = scale * _swa_merge(dq[2 * col], dq[2 * col + 1])
            dsink = jnp.zeros((1, LANE), F32)
            for pos in heads:
                dsink = dsink + jnp.where(lane == HEAD_POS[pos],
                                          -jnp.sum(soft[pos][1] * delta[pos], axis=0, keepdims=True), 0.0)
            dsink_ref[...] += dsink
            dk3 = _dot_tn(jnp.concatenate(q, axis=0), jnp.concatenate(ds, axis=0)).T
            dv3 = _dot_tn(jnp.concatenate(d_o, axis=0), jnp.concatenate([x.astype(BF16) for x in p], axis=0)).T
            meta_k[...] += dk3[0:BLK]
            meta_v[...] += dv3[0:BLK]
            dk_ref[...] = carry_k[...] + dk3[BLK:2 * BLK]
            dv_ref[...] = carry_v[...] + dv3[BLK:2 * BLK]
            carry_k[...] = dk3[2 * BLK:3 * BLK]
            carry_v[...] = dv3[2 * BLK:3 * BLK]

        @pl.when(n == nb + 1)
        def _():
            dk_ref[...] = meta_k[...]
            dv_ref[...] = meta_v[...]

    kv_out = pl.BlockSpec((BLK, kvw), lambda n: (jnp.where(n == nb + 1, nb, jnp.clip(n - 1, 0, nb - 1)), 0))
    qblk = pl.BlockSpec((BLK, SWA_HEADS * DH), lambda n: (jnp.minimum(n, nb), 0))
    return pl.pallas_call(
        body, name="swa_bwd", grid=(nb + 2,),
        in_specs=[pl.BlockSpec(memory_space=pltpu.SMEM), qblk] + _swa_kv_specs(kvw) + _swa_kv_specs(kvw)
        + [qblk, _const(TOKEN)],
        out_specs=[qblk, kv_out, kv_out, _acc((1, LANE))],
        out_shape=[pltpu.HBM((_lp(), SWA_HEADS * DH), F32), pltpu.HBM((_lp(), kvw), F32),
                   pltpu.HBM((_lp(), kvw), F32), pltpu.HBM((1, LANE), F32)],
        scratch_shapes=[pltpu.VMEM((BLK, kvw), F32)] * 4,
        compiler_params=_params(16, dimension_semantics=_seq()),
    )(sinks, *_hbm(qs, ks, ks, ks, vs, vs, vs, do_s), token)


def _in_bwd(dqs, dks, dvs, dqg, dkg, dvg, drg, dz, dpre1, w_in_t, wg2_p):
    tm = _row_tile(384)
    lp = _lp()
    widths = (512, 128, 128, 256, 256, 512, 512)
    offs = (O_QS, O_KS, O_VS, O_QG, O_KG, O_VG, O_RG)

    def body(*refs):
        parts, (dz_ref, dp1_ref, w_ref, wg2_ref, dproj_ref, dh0_ref, dbin_ref, dbg_ref) = refs[:7], refs[7:]

        @pl.when(pl.program_id(0) == 0)
        def _():
            dbin_ref[...] = jnp.zeros_like(dbin_ref)
            dbg_ref[...] = jnp.zeros_like(dbg_ref)

        for pos, h in enumerate(HEAD_POS):
            val = parts[0][:, pos * DH:(pos + 1) * DH]
            dproj_ref[:, O_QS + h * DH:O_QS + (h + 1) * DH] = val.astype(BF16)
            dbin_ref[:, O_QS + h * DH:O_QS + (h + 1) * DH] += jnp.sum(val, axis=0, keepdims=True)
        for p_ref, off, wd in zip(parts[1:], offs[1:], widths[1:]):
            val = p_ref[...]
            dproj_ref[:, off:off + wd] = val.astype(BF16)
            dbin_ref[:, off:off + wd] += jnp.sum(val, axis=0, keepdims=True)
        dz = dz_ref[...]
        dlr = _dot_nt(dz, wg2_ref[...])
        dproj_ref[:, O_LR:O_LR + LANE] = dlr.astype(BF16)
        dbin_ref[:, O_LR:O_LR + LANE] += jnp.sum(dlr, axis=0, keepdims=True)
        dbg_ref[...] += jnp.sum(dz, axis=0, keepdims=True)
        dh0_ref[...] = ALPHA * dp1_ref[...] + _dot(dproj_ref[...], w_ref[...])

    return pl.pallas_call(
        body, name="in_bwd", grid=(lp // tm,),
        in_specs=[_rows(tm, w) for w in widths] + [_rows(tm, 256), _rows(tm, D), _const((D_IN_P, D)), _const((LANE, 256))],
        out_specs=[_rows(tm, D_IN_P), _rows(tm, D), _acc((1, D_IN_P)), _acc((1, 256))],
        out_shape=[pltpu.HBM((lp, D_IN_P), BF16), pltpu.HBM((lp, D), F32),
                   pltpu.HBM((1, D_IN_P), F32), pltpu.HBM((1, 256), F32)],
        compiler_params=_params(40, dimension_semantics=_seq()),
    )(*_hbm(dqs, dks, dvs, dqg, dkg, dvg, drg, dz, dpre1, w_in_t, wg2_p))


def _ln_in_bwd(x, meta_ext, dh0, g, token):
    tr = min(LN_ROWS, SEQ)

    def ln_bwd(x_ref, dh_ref, g_ref, dx_ref, dg_ref, db_ref):
        @pl.when(pl.program_id(0) == 0)
        def _():
            dg_ref[...] = jnp.zeros_like(dg_ref)
            db_ref[...] = jnp.zeros_like(db_ref)

        xhat, rstd = _ln_stats(x_ref[...])
        dh = dh_ref[...]
        dx_ref[...] = _ln_bwd(dh, xhat, rstd, g_ref[...])
        dg_ref[...] += jnp.sum(dh * xhat, axis=0, keepdims=True)
        db_ref[...] += jnp.sum(dh, axis=0, keepdims=True)

    def body(x_ref, dh_ref, g_ref, token_ref, dx_ref, dg_ref, db_ref):
        ln_bwd(x_ref, dh_ref, g_ref, dx_ref, dg_ref, db_ref)

    def meta_body(m_ref, dh_ref, g_ref, dm_ref, dg_ref, db_ref):
        ln_bwd(m_ref, dh_ref, g_ref, dm_ref, dg_ref, db_ref)

    sums = [pltpu.HBM((1, D), F32), pltpu.HBM((1, D), F32)]
    dx, dg, db = pl.pallas_call(
        body, name="ln_in_bwd", grid=(SEQ // tr,),
        in_specs=[_rows(tr, D), _rows(tr, D), _const((1, D)), _const(TOKEN)],
        out_specs=[_rows(tr, D), _acc((1, D)), _acc((1, D))],
        out_shape=[pltpu.HBM((SEQ, D), F32)] + sums,
        compiler_params=_params(32, dimension_semantics=_seq()),
    )(*_hbm(x, dh0, g), token)
    dm, dg_m, db_m = pl.pallas_call(
        meta_body, name="ln_in_bwd_meta", grid=(1,),
        in_specs=[_const((BLK, D)), pl.BlockSpec((BLK, D), lambda i: (SEQ // BLK, 0)), _const((1, D))],
        out_specs=[_acc((BLK, D)), _acc((1, D)), _acc((1, D))],
        out_shape=[pltpu.HBM((BLK, D), F32)] + sums,
        compiler_params=_params(16, dimension_semantics=_seq()),
    )(*_hbm(meta_ext, dh0, g))
    return dx, dm, dg + dg_m, db + db_m


def _local_step(x, target, ln_in_g, ln_in_b, b_in, bg2, sinks, gn, g1, b1, g2, b2,
                token, fetch_first, fetch_rest, fetch_ffn, exchange_ffn, ship_ffn, ship_w_in):
    row = lambda v: v.reshape(1, -1).astype(F32)
    b_in_p = jnp.pad(row(b_in), ((0, 0), (0, D_IN_P - D_IN)))
    gn4 = row(gn)
    sinks = sinks.reshape(-1).astype(F32)

    h_real = _ln_in_fwd_real(x, row(ln_in_g), row(ln_in_b), token)
    w_in_t, meta_full, wg2 = fetch_first([h_real])
    meta_ext = jnp.pad(meta_full, ((META_OFF, BLK - CH), (0, 0)))
    wg2_p = jnp.pad(wg2, ((0, LANE - wg2.shape[0]), (0, 0))).astype(BF16)
    h0 = _ln_in_fwd_meta(h_real, meta_ext, row(ln_in_g), row(ln_in_b))
    qs, ks, vs, qg, kg, vg, rg, glr, z = _in_proj(h0, w_in_t, b_in_p, wg2_p, row(bg2))
    o_s = _swa_fwd(sinks, qs, ks, vs)
    o_gla, st_all = _gla_fwd(qg, kg, vg, z)
    w_out, token = fetch_rest([o_s, o_gla])
    o, pre1, h1 = _post_mix(o_s, o_gla, rg, h0, gn4, w_out, row(g1), row(b1), token)
    wg_t, wu_t, wd = fetch_ffn([pre1])
    a, dgate, dup, dpre2, loss, dg2, db2 = _ffn_fwd_loss_bwd(h1, wg_t, wu_t, wd, target, row(g2), row(b2))
    dpre1, dg1, db1, do_s, do_gla, drg, dgn = _ffn_out_bwd(dpre2, dgate, dup, pre1, wg_t, wu_t, row(g1), w_out, o_gla,
                                                           rg, gn4)
    dwd = _atb(a, dpre2, "dw_down")
    dwg_t = _atb(dgate, h1, "dw_gate")
    dwu_t = _atb(dup, h1, "dw_up")
    token = exchange_ffn(dict(w_g=dwg_t, w_u=dwu_t, w_d=dwd))
    token = ship_ffn(_atb(o, dpre1, "dw_out", token))
    dqg, dkg, dvg, dz = _gla_bwd(qg, kg, vg, z, do_gla, st_all, token)
    dqs, dks, dvs, dsinks = _swa_bwd(sinks, qs, ks, vs, do_s, token)
    dproj, dh0, db_in_p, dbg2 = _in_bwd(dqs, dks, dvs, dqg, dkg, dvg, drg, dz, dpre1, w_in_t, wg2_p)
    token = ship_w_in(_atb(dproj, h0, "dw_in"))
    dwg2_p = _atb(glr, dz, "dw_gate_lr2")
    dx, dmeta_blk, dg_in, db_in_ln = _ln_in_bwd(x, meta_ext, dh0, row(ln_in_g), token)

    small = dict(meta_blk=dmeta_blk, ln_in_g=dg_in, ln_in_b=db_in_ln, ln1_g=dg1, ln1_b=db1, ln2_g=dg2, ln2_b=db2,
                 b_in_p=db_in_p, wg2_p=dwg2_p, bg2=dbg2, sinks=dsinks, gn=dgn, loss=loss)
    return dx, small


HBM = pl.BlockSpec(memory_space=pltpu.HBM)


def _place():
    return lax.axis_index("x"), lax.axis_index("y"), lax.axis_index("c")


def _other_chips(x, y):
    return [(1 - x, y), (x, 1 - y), (1 - x, 1 - y)]


def _dma_sems(n):
    return pltpu.SemaphoreType.DMA((n,))


def _comm_params():
    return pltpu.CompilerParams(has_side_effects=True)


SEM = pl.BlockSpec(memory_space=pltpu.SEMAPHORE)


PER_ARRAY = dict(gather=3, scatter=3, sibling=N_CHIPS)


def _ici_copies(kind, landing, srcs, lands, send_sems, recv_sems):
    x, y, c = _place()
    mine = 2 * x + y
    copies = []
    for a in range(len(srcs)):
        if kind == "sibling":
            for s in range(N_CHIPS):
                copies.append(pltpu.make_async_remote_copy(
                    srcs[a].at[s, 1 - c], lands[a].at[s], send_sems.at[N_CHIPS * a + s], recv_sems.at[N_CHIPS * a + s],
                    device_id=(x, y, 1 - c), device_id_type=MESH))
            continue
        for j, (px, py) in enumerate(_other_chips(x, y)):
            slab = 2 * px + py if landing else mine
            if kind == "gather":
                src, dst = srcs[a].at[c], lands[a].at[slab, c]
            else:
                src, dst = srcs[a].at[2 * px + py], lands[a].at[slab]
            copies.append(pltpu.make_async_remote_copy(src, dst, send_sems.at[3 * a + j], recv_sems.at[3 * a + j],
                                                       device_id=(px, py, c), device_id_type=MESH))
    return copies


def _split_params():
    return pltpu.CompilerParams(has_side_effects=pltpu.SideEffectType.DATAFLOW_SIDE_EFFECTING)


def _ici_start(kind, srcs, land_shapes, after, name):
    n = len(srcs)
    lands = [pltpu.with_memory_space_constraint(lax.empty(s, a.dtype), pltpu.HBM) for s, a in zip(land_shapes, srcs)]

    def body(*refs):
        outs = refs[2 * n + len(after):]
        for cp in _ici_copies(kind, False, refs[:n], refs[n:2 * n], outs[0], outs[1]):
            cp.start()
        outs[-1][...] = jnp.zeros(TOKEN, F32)

    outs = pl.pallas_call(
        body, name=name, in_specs=[HBM] * (2 * n) + [pl.BlockSpec(memory_space=pl.ANY)] * len(after),
        out_specs=[SEM, SEM] + [HBM] * (2 * n) + [pl.BlockSpec(memory_space=pltpu.VMEM)],
        out_shape=[_dma_sems(PER_ARRAY[kind] * n)] * 2 + [pltpu.HBM(a.shape, a.dtype) for a in list(srcs) + lands]
        + [jax.ShapeDtypeStruct(TOKEN, F32)],
        input_output_aliases={i: 2 + i for i in range(2 * n)},
        compiler_params=_split_params(),
    )(*_hbm(*srcs), *lands, *after)
    return outs[:-1], outs[-1]


def _ici_wait(kind, handle, after, name):
    n = (len(handle) - 2) // 2

    def body(*refs):
        for cp in _ici_copies(kind, True, refs[:n], refs[n:2 * n], refs[2 * n], refs[2 * n + 1]):
            cp.wait_send()
            cp.wait_recv()

    outs = pl.pallas_call(
        body, name=name, in_specs=[HBM] * (2 * n) + [SEM, SEM] + [pl.BlockSpec(memory_space=pl.ANY)] * len(after),
        out_specs=[HBM] * (2 * n), out_shape=[pltpu.HBM(a.shape, a.dtype) for a in handle[2:]],
        input_output_aliases={i: i for i in range(2 * n)},
        compiler_params=_split_params(),
    )(*handle[2:], handle[0], handle[1], *after)
    return list(outs[:n]), list(outs[n:])


def _forward_copies(landing, arrs, send_sems, recv_sems):
    x, y, c = _place()
    copies = []
    for a in range(len(arrs)):
        for j, (px, py) in enumerate(_other_chips(x, y)):
            half = 1 - c if landing else c
            copies.append(pltpu.make_async_remote_copy(
                arrs[a].at[2 * px + py, c], arrs[a].at[2 * px + py, half], send_sems.at[3 * a + j],
                recv_sems.at[3 * a + j], device_id=(x, y, 1 - c), device_id_type=MESH))
    return copies


def _sibling_forward(lands, name):
    n = len(lands)

    def body(*refs):
        outs = refs[n:2 * n]
        send_sems, recv_sems = refs[2 * n:]
        sends = _forward_copies(False, outs, send_sems, recv_sems)
        for cp in sends:
            cp.start()
        for cp in _forward_copies(True, outs, send_sems, recv_sems):
            cp.wait_recv()
        for cp in sends:
            cp.wait_send()

    return pl.pallas_call(
        body, name=name, in_specs=[HBM] * n, out_specs=[HBM] * n,
        out_shape=[pltpu.HBM(a.shape, a.dtype) for a in lands],
        input_output_aliases={a: a for a in range(n)},
        scratch_shapes=[_dma_sems(3 * n)] * 2,
        compiler_params=_comm_params(),
    )(*_hbm(*lands))


def _forward_start(lands, name):
    n = len(lands)

    def body(*refs):
        outs = refs[n:]
        for cp in _forward_copies(False, refs[:n], outs[0], outs[1]):
            cp.start()
        outs[-1][...] = jnp.zeros(TOKEN, F32)

    outs = pl.pallas_call(
        body, name=name, in_specs=[HBM] * n,
        out_specs=[SEM, SEM] + [HBM] * n + [pl.BlockSpec(memory_space=pltpu.VMEM)],
        out_shape=[_dma_sems(3 * n)] * 2 + [pltpu.HBM(a.shape, a.dtype) for a in lands]
        + [jax.ShapeDtypeStruct(TOKEN, F32)],
        input_output_aliases={i: 2 + i for i in range(n)},
        compiler_params=_split_params(),
    )(*_hbm(*lands))
    return outs[:-1], outs[-1]


def _forward_wait(handle, after, name):
    n = len(handle) - 2

    def body(*refs):
        for cp in _forward_copies(True, refs[:n], refs[n], refs[n + 1]):
            cp.wait_send()
            cp.wait_recv()

    return list(pl.pallas_call(
        body, name=name, in_specs=[HBM] * n + [SEM, SEM] + [pl.BlockSpec(memory_space=pl.ANY)] * len(after),
        out_specs=[HBM] * n, out_shape=[pltpu.HBM(a.shape, a.dtype) for a in handle[2:]],
        input_output_aliases={i: i for i in range(n)},
        compiler_params=_split_params(),
    )(*handle[2:], handle[0], handle[1], *after))


def _sibling_exchange(grads, name):
    n = len(grads)

    def body(*refs):
        ins, outs = refs[:n], refs[n:2 * n]
        send_sems, recv_sems = refs[2 * n:]
        x, y, c = _place()
        copies = []
        for a in range(n):
            for s in range(N_CHIPS):
                cp = pltpu.make_async_remote_copy(ins[a].at[s, 1 - c], outs[a].at[s], send_sems.at[N_CHIPS * a + s],
                                                  recv_sems.at[N_CHIPS * a + s], device_id=(x, y, 1 - c),
                                                  device_id_type=MESH)
                cp.start()
                copies.append(cp)
        for cp in copies:
            cp.wait_recv()
        for cp in copies:
            cp.wait_send()

    return pl.pallas_call(
        body, name=name, in_specs=[HBM] * n, out_specs=[HBM] * n,
        out_shape=[pltpu.HBM((N_CHIPS, g.shape[2], D), F32) for g in grads],
        scratch_shapes=[_dma_sems(N_CHIPS * n)] * 2,
        compiler_params=_comm_params(),
    )(*_hbm(*grads))


def _add_halves(core, grads, recvs, dtypes, name):
    n = len(grads)
    heights = [g.shape[2] for g in grads]

    def body(c_ref, *refs):
        for a in range(n):
            refs[2 * n + a][...] = (refs[2 * a][0] + refs[2 * a + 1][...]).astype(dtypes[a])

    slab = lambda h: pl.BlockSpec((1, h, D), lambda s, c: (s, 0, 0))
    mine = lambda h: pl.BlockSpec((1, 1, h, D), lambda s, c: (s, c[0], 0, 0))
    return pl.pallas_call(
        body, name=name,
        grid_spec=pltpu.PrefetchScalarGridSpec(
            num_scalar_prefetch=1, grid=(N_CHIPS,),
            in_specs=[spec(h) for h in heights for spec in (mine, slab)], out_specs=[slab(h) for h in heights]),
        out_shape=[pltpu.HBM((N_CHIPS, h, D), dt) for h, dt in zip(heights, dtypes)],
        compiler_params=_params(32, dimension_semantics=_seq()),
    )(core, *_hbm(*[a for pair in zip(grads, recvs) for a in pair]))


N_DEVICES = 2 * N_CHIPS
PEER_FLIPS = [(dx, dy, dc) for dx in (0, 1) for dy in (0, 1) for dc in (0, 1)][1:]


def _small_copies(landing, p_ref, out_ref, send_sems, recv_sems):
    x, y, c = _place()
    flip = lambda v, d: 1 - v if d else v
    copies = []
    for k, flips in enumerate(PEER_FLIPS):
        px, py, pc = (flip(v, d) for v, d in zip((x, y, c), flips))
        slab = 4 * px + 2 * py + pc if landing else 4 * x + 2 * y + c
        copies.append(pltpu.make_async_remote_copy(p_ref, out_ref.at[slab], send_sems.at[k], recv_sems.at[k],
                                                   device_id=(px, py, pc), device_id_type=MESH))
    return copies


def _small_start(pack, after):
    n = len(PEER_FLIPS)
    land = pltpu.with_memory_space_constraint(lax.empty((N_DEVICES,) + pack.shape, F32), pltpu.HBM)

    def body(p_ref, land_ref, *refs):
        outs = refs[len(after):]
        for cp in _small_copies(False, p_ref, land_ref, outs[0], outs[1]):
            cp.start()
        outs[-1][...] = jnp.zeros(TOKEN, F32)

    outs = pl.pallas_call(
        body, name="small_exchange_start", in_specs=[HBM, HBM] + [pl.BlockSpec(memory_space=pl.ANY)] * len(after),
        out_specs=[SEM, SEM, HBM, HBM, pl.BlockSpec(memory_space=pltpu.VMEM)],
        out_shape=[_dma_sems(n), _dma_sems(n), pltpu.HBM(pack.shape, F32), pltpu.HBM(land.shape, F32),
                   jax.ShapeDtypeStruct(TOKEN, F32)],
        input_output_aliases={0: 2, 1: 3},
        compiler_params=_split_params(),
    )(*_hbm(pack), land, *after)
    return outs[:-1], outs[-1]


def _small_wait(handle, after):
    def body(p_ref, land_ref, send_sems, recv_sems, *rest):
        for cp in _small_copies(True, p_ref, land_ref, send_sems, recv_sems):
            cp.wait_send()
            cp.wait_recv()

    return pl.pallas_call(
        body, name="small_exchange_wait", in_specs=[HBM, HBM, SEM, SEM] + [pl.BlockSpec(memory_space=pl.ANY)] * len(after),
        out_specs=[HBM, HBM], out_shape=[pltpu.HBM(a.shape, F32) for a in handle[2:]],
        input_output_aliases={0: 0, 1: 1},
        compiler_params=_split_params(),
    )(handle[2], handle[3], handle[0], handle[1], *after)


def _sum_chips(slots, firsts, rests, after):
    n = len(firsts)

    def body(i_ref, *refs):
        outs = refs[4 * n + len(after):]
        for a in range(n):
            first, r1, r2, r3 = refs[4 * a:4 * a + 4]
            outs[a][...] = ((first[...].astype(F32) + r1[...].astype(F32)) + r2[...].astype(F32)) + r3[...].astype(F32)

    slab = lambda h, k: pl.BlockSpec((1, h, D), lambda i, ix: (ix[k], 0, 0))
    heights = [f.shape[1] for f in firsts]
    return pl.pallas_call(
        body, name="sum_chips",
        grid_spec=pltpu.PrefetchScalarGridSpec(
            num_scalar_prefetch=1, grid=(1,),
            in_specs=[slab(h, k) for h in heights for k in range(4)] + [pl.BlockSpec(memory_space=pl.ANY)] * len(after),
            out_specs=[slab(h, 4) for h in heights]),
        out_shape=[pltpu.HBM((2, h, D), F32) for h in heights],
        compiler_params=_params(48, dimension_semantics=_seq()),
    )(slots, *_hbm(*[a for f, r in zip(firsts, rests) for a in (f, r, r, r)]), *after)


def _join_halves(halves):
    n = len(halves)

    def body(*refs):
        outs = refs[n:2 * n]
        send_sems, recv_sems = refs[2 * n:]
        x, y, c = _place()

        def copy(a, slab):
            return pltpu.make_async_remote_copy(outs[a].at[slab], outs[a].at[slab], send_sems.at[a], recv_sems.at[a],
                                                device_id=(x, y, 1 - c), device_id_type=MESH)

        for a in range(n):
            copy(a, c).start()
        for a in range(n):
            copy(a, 1 - c).wait_recv()
        for a in range(n):
            copy(a, c).wait_send()

    return pl.pallas_call(
        body, name="join_halves", in_specs=[HBM] * n, out_specs=[HBM] * n,
        out_shape=[pltpu.HBM(h.shape, F32) for h in halves],
        input_output_aliases={a: a for a in range(n)},
        scratch_shapes=[_dma_sems(n)] * 2,
        compiler_params=_comm_params(),
    )(*_hbm(*halves))


def _chip_partials(grads, wire_dtypes, names, fetched=()):
    core = lax.axis_index("c").astype(jnp.int32).reshape(1)
    todo = len(grads) - len(fetched)
    recv = list(_sibling_exchange(grads[:todo], "sibling_exchange_" + names[0])) + list(fetched)
    return list(_add_halves(core, grads, recv, wire_dtypes, "add_halves_" + names[0]))


def _finish_reduce(parts, got, after):
    x, y, c = _place()
    others = [2 * px + py for px, py in _other_chips(x, y)]
    own_first = jnp.stack([2 * x + y] + others + [c]).astype(jnp.int32)
    return [f.reshape(2 * f.shape[1], D) for f in _join_halves(_sum_chips(own_first, parts, got, after))]


ADAMW_STEPS = 4


def _adamw(params):
    n = len(params)

    def block(shape):
        rows, cols = shape
        if rows % (8 * ADAMW_STEPS) == 0:
            return pl.BlockSpec((rows // ADAMW_STEPS, cols), lambda i: (i, 0))
        assert cols % (LANE * ADAMW_STEPS) == 0
        return pl.BlockSpec((rows, cols // ADAMW_STEPS), lambda i: (0, i))

    def body(*refs):
        for a in range(n):
            w_ref, g_ref, m_ref, v_ref = refs[4 * a:4 * a + 4]
            outs = refs[4 * n + 3 * a:4 * n + 3 * a + 3]
            outs[0][...], outs[1][...], outs[2][...] = _adamw_math(w_ref[...], g_ref[...], m_ref[...], v_ref[...])

    outs = pl.pallas_call(
        body, name="adamw_matrices", grid=(ADAMW_STEPS,),
        in_specs=[block(p[0].shape) for p in params for _ in range(4)],
        out_specs=[block(p[0].shape) for p in params for _ in range(3)],
        out_shape=[pltpu.HBM(p[0].shape, F32) for p in params for _ in range(3)],
        compiler_params=_params(48, dimension_semantics=_seq()),
    )(*_hbm(*[a for p in params for a in p]))
    return [outs[3 * a:3 * a + 3] for a in range(n)]


def _adamw_math(w, g, m, v):
    nm = ADAM_B1 * m + (1.0 - ADAM_B1) * g
    nv = ADAM_B2 * v + (1.0 - ADAM_B2) * (g * g)
    m_hat = nm / (1.0 - ADAM_B1 ** ADAM_STEP)
    v_hat = nv / (1.0 - ADAM_B2 ** ADAM_STEP)
    return -ADAM_LR * (m_hat / (jnp.sqrt(v_hat) + ADAM_EPS) + ADAM_WD * w), nm, nv


SMALL = (("meta_tokens", (N_META, D // N_CHIPS)), ("ln_in_g", (1, D)), ("ln_in_b", (1, D)), ("b_in", (1, D_IN)),
         ("w_gate_lr2", (GATE_RANK, GLA_HEADS * DK // N_CHIPS)), ("b_gate_lr2", (1, GLA_HEADS * DK)),
         ("attn_sinks", (1, SWA_HEADS)),
         ("gla_norm_g", (1, DV)), ("ln1_g", (1, D)), ("ln1_b", (1, D)), ("ln2_g", (1, D)), ("ln2_b", (1, D)))
ROW_META, ROW_B_IN, ROW_TAIL, ROW_WG2 = 0, 22, 25, 32
ROW_LN = dict(ln_in_g=16, ln_in_b=17, ln1_g=18, ln1_b=19, ln2_g=20, ln2_b=21)
TAIL_BG2, TAIL_SINKS, TAIL_GN, TAIL_LOSS = 0, 256, 256 + SWA_HEADS, 256 + SWA_HEADS + DV


def _adamw_small(place, packs, own, params):
    n = len(SMALL)

    def body(place_ref, packs_ref, own_ref, *refs):
        ins, outs, p_ref = refs[:3 * n], refs[3 * n:-1], refs[-1]
        me, c = place_ref[0], place_ref[1]
        total = jnp.where(me == 0, own_ref[...], packs_ref[0])
        for i in range(1, N_DEVICES):
            total = total + jnp.where(me == i, own_ref[...], packs_ref[i])
        p_ref[...] = total
        outs[4 * n][...] = total[ROW_TAIL:ROW_TAIL + 1, :]

        def mine(width, rows):
            part = lambda s: p_ref[rows, s * width:(s + 1) * width]
            return jnp.where(c == 0, part(0), jnp.where(c == 1, part(1), jnp.where(c == 2, part(2), part(3))))

        tail = lambda lo, width: p_ref[ROW_TAIL:ROW_TAIL + 1, lo:lo + width]
        grads = dict(
            meta_tokens=mine(D // N_CHIPS, slice(ROW_META, ROW_META + N_META)),
            b_in=jnp.concatenate([p_ref[ROW_B_IN:ROW_B_IN + 1, :], p_ref[ROW_B_IN + 1:ROW_B_IN + 2, :],
                                  p_ref[ROW_B_IN + 2:ROW_B_IN + 3, 0:D_IN - 2 * D]], axis=1),
            w_gate_lr2=mine(256 // N_CHIPS, slice(ROW_WG2, ROW_WG2 + 16)),
            b_gate_lr2=tail(TAIL_BG2, 256), attn_sinks=tail(TAIL_SINKS, SWA_HEADS), gla_norm_g=tail(TAIL_GN, DV),
            **{k: p_ref[r:r + 1, :] for k, r in ROW_LN.items()})
        for i, (name, _) in enumerate(SMALL):
            g = grads[name]
            outs[4 * i][...] = g
            outs[4 * i + 1][...], outs[4 * i + 2][...], outs[4 * i + 3][...] = _adamw_math(
                ins[3 * i][...], g, ins[3 * i + 1][...], ins[3 * i + 2][...])

    whole = lambda shape: pl.BlockSpec(shape, lambda i, c: (0,) * len(shape))
    outs = pl.pallas_call(
        body, name="adamw_small",
        grid_spec=pltpu.PrefetchScalarGridSpec(
            num_scalar_prefetch=1, grid=(1,),
            in_specs=[whole(packs.shape), whole(own.shape)] + [whole(s) for _, s in SMALL for _ in range(3)],
            out_specs=[whole(s) for _, s in SMALL for _ in range(4)] + [whole((1, D))],
            scratch_shapes=[pltpu.VMEM(own.shape, F32)]),
        out_shape=[pltpu.HBM(s, F32) for _, s in SMALL for _ in range(4)] + [pltpu.HBM((1, D), F32)],
        compiler_params=_params(16, dimension_semantics=_seq()),
    )(place, *_hbm(packs, own, *[a for p in params for a in p]))
    return [outs[4 * i:4 * i + 4] for i in range(n)], outs[4 * n]


def _small_pack(gr):
    names = ["meta_blk"] + list(ROW_LN) + ["b_in_p", "wg2_p", "bg2", "sinks", "gn", "loss"]
    gate_w = GLA_HEADS * DK

    def body(*refs):
        src, out = dict(zip(names, refs)), refs[-1]
        out[...] = jnp.zeros_like(out)
        out[ROW_META:ROW_META + N_META, :] = src["meta_blk"][META_OFF:CH, :]
        for k, r in ROW_LN.items():
            out[r:r + 1, :] = src[k][...]
        for j in range(-(-D_IN // D)):
            width = min(D, D_IN - j * D)
            out[ROW_B_IN + j:ROW_B_IN + j + 1, 0:width] = src["b_in_p"][:, j * D:j * D + width]
        tail = slice(ROW_TAIL, ROW_TAIL + 1)
        out[tail, TAIL_BG2:TAIL_BG2 + gate_w] = src["bg2"][...]
        out[tail, TAIL_SINKS:TAIL_SINKS + SWA_HEADS] = src["sinks"][:, 0:SWA_HEADS]
        out[tail, TAIL_GN:TAIL_GN + DV] = src["gn"][...]
        out[tail, TAIL_LOSS:TAIL_LOSS + 1] = src["loss"][:, 0:1]
        out[ROW_WG2:ROW_WG2 + GATE_RANK, 0:gate_w] = src["wg2_p"][0:GATE_RANK, :]

    arrays = [gr[k] for k in names]
    return pl.pallas_call(
        body, name="small_pack", grid=(1,),
        in_specs=[_acc(a.shape) for a in arrays], out_specs=_acc((SMALL_ROWS, D)),
        out_shape=pltpu.HBM((SMALL_ROWS, D), F32),
        compiler_params=_params(16, dimension_semantics=_seq()),
    )(*_hbm(*arrays))


BIG = ("w_in", "w_out", "w_g", "w_u", "w_d")


def kernel(x, meta_tokens, ln_in_g, ln_in_b, w_in, b_in, w_gate_lr2, b_gate_lr2, attn_sinks, gla_norm_g, w_out, ln1_g, ln1_b, w_ffn_gate, w_ffn_up, w_ffn_down, ln2_g, ln2_b, loss_target, m_meta_tokens, m_ln_in_g, m_ln_in_b, m_w_in, m_b_in, m_w_gate_lr2, m_b_gate_lr2, m_attn_sinks, m_gla_norm_g, m_w_out, m_ln1_g, m_ln1_b, m_w_ffn_gate, m_w_ffn_up, m_w_ffn_down, m_ln2_g, m_ln2_b, v_meta_tokens, v_ln_in_g, v_ln_in_b, v_w_in, v_b_in, v_w_gate_lr2, v_b_gate_lr2, v_attn_sinks, v_gla_norm_g, v_w_out, v_ln1_g, v_ln1_b, v_w_ffn_gate, v_w_ffn_up, v_w_ffn_down, v_ln2_g, v_ln2_b):
    chip = 2 * lax.axis_index("x") + lax.axis_index("y")

    halves = lambda a: a.reshape(2, a.shape[0] // 2, a.shape[1])
    r_in = SHARD_ROWS["w_in"]
    first = [halves(a) for a in (jnp.pad(w_in[0].T.astype(BF16), ((0, W_IN_WIN - r_in), (0, 0))), meta_tokens,
                                 w_gate_lr2[0])]
    rest = [halves(a) for a in (w_out[0].astype(BF16), w_ffn_gate[0].T.astype(BF16), w_ffn_up[0].T.astype(BF16),
                                w_ffn_down[0].astype(BF16))]
    lands = lambda arrs: [(N_CHIPS,) + a.shape for a in arrs]
    first_handle, first_token = _ici_start("gather", first, lands(first), [], "gather_first_start")
    rest_handle, token = _ici_start("gather", rest, lands(rest), [first_token], "gather_rest_start")
    own_slab = lambda got, shards: [lax.dynamic_update_index_in_dim(g, s, chip, axis=0) for g, s in zip(got, shards)]
    fetching = {}

    def fetch_first(after):
        shards, landed = _ici_wait("gather", first_handle, after, "gather_first_wait")
        g_in, g_meta, g_wg2 = own_slab(_sibling_forward(landed, "gather_first_forward"), shards)
        w_in_t = jnp.pad(g_in.reshape(N_CHIPS, W_IN_WIN, D)[:, :r_in].reshape(D_IN, D), ((0, D_IN_P - D_IN), (0, 0)))
        meta_full = jnp.concatenate([g_meta[s].reshape(N_META, -1) for s in range(N_CHIPS)], axis=1)
        wg2_full = jnp.concatenate([g_wg2[s].reshape(w_gate_lr2.shape[1], -1) for s in range(N_CHIPS)], axis=1)
        return w_in_t, meta_full, wg2_full

    def fetch_rest(after):
        shards, landed = _ici_wait("gather", rest_handle, after, "gather_rest_wait")
        g_out, = own_slab(_sibling_forward(landed[:1], "gather_w_out_forward"), shards[:1])
        fetching["shards"] = shards[1:]
        fetching["handle"], forward_token = _forward_start(landed[1:], "gather_ffn_forward_start")
        return g_out.reshape(-1, D), forward_token

    def fetch_ffn(after):
        got = _forward_wait(fetching["handle"], after, "gather_ffn_forward_wait")
        return [g.reshape(-1, D) for g in own_slab(got, fetching["shards"])]

    sent = {}
    split = lambda grads: [g.reshape(N_CHIPS, 2, -1, D) for g in grads]

    def ship(key, grads, names, fetched=()):
        parts = _chip_partials(grads, [BF16] * len(grads), names, fetched)
        sent[key], ship_token = _ici_start("scatter", parts, [p.shape for p in parts], [], "scatter_" + key + "_start")
        return ship_token

    def exchange_ffn(g):
        grads = split([g[k] for k in BIG[2:]])
        sent["ffn_halves"], exchange_token = _ici_start("sibling", grads, [(N_CHIPS,) + a.shape[2:] for a in grads], [],
                                                        "sibling_ffn_start")
        return exchange_token

    def ship_ffn(dw_out):
        grads, fetched = _ici_wait("sibling", sent["ffn_halves"], [dw_out], "sibling_ffn_wait")
        return ship("ffn", split([dw_out]) + grads, list(BIG[1:]), fetched)

    def ship_w_in(dw_in_t):
        win_start = [s * r_in // BF16_ROWS * BF16_ROWS for s in range(N_CHIPS)]
        return ship("w_in", split([jnp.stack([dw_in_t[st:st + W_IN_WIN] for st in win_start])]), ["w_in"])

    dx, gr = _local_step(
        x[0], loss_target[0], ln_in_g, ln_in_b, b_in[0], b_gate_lr2[0], attn_sinks[0], gla_norm_g[0], ln1_g[0],
        ln1_b[0], ln2_g[0], ln2_b[0], token, fetch_first, fetch_rest, fetch_ffn, exchange_ffn, ship_ffn, ship_w_in)
    ffn_parts, ffn_got = _ici_wait("scatter", sent["ffn"], [dx], "scatter_ffn_wait")
    w_in_parts, w_in_got = _ici_wait("scatter", sent["w_in"], [dx], "scatter_w_in_wait")

    small_handle, token = _small_start(_small_pack(gr), [w_in_got[0]])
    red = _finish_reduce(w_in_parts + ffn_parts, w_in_got + ffn_got, [token])

    big_g = dict(zip(BIG, red))
    big_g["w_in"] = lax.dynamic_slice_in_dim(red[0], chip * (r_in % BF16_ROWS), r_in, axis=0)
    grads = dict(w_in=big_g["w_in"].T[None], w_out=big_g["w_out"][None], w_ffn_gate=big_g["w_g"].T[None],
                 w_ffn_up=big_g["w_u"].T[None], w_ffn_down=big_g["w_d"][None])
    weights = dict(meta_tokens=meta_tokens, ln_in_g=ln_in_g, ln_in_b=ln_in_b, w_in=w_in, b_in=b_in,
                   w_gate_lr2=w_gate_lr2, b_gate_lr2=b_gate_lr2, attn_sinks=attn_sinks, gla_norm_g=gla_norm_g,
                   w_out=w_out, ln1_g=ln1_g, ln1_b=ln1_b, w_ffn_gate=w_ffn_gate, w_ffn_up=w_ffn_up,
                   w_ffn_down=w_ffn_down, ln2_g=ln2_g, ln2_b=ln2_b)
    m_in = dict(meta_tokens=m_meta_tokens, ln_in_g=m_ln_in_g, ln_in_b=m_ln_in_b, w_in=m_w_in, b_in=m_b_in,
                w_gate_lr2=m_w_gate_lr2, b_gate_lr2=m_b_gate_lr2, attn_sinks=m_attn_sinks, gla_norm_g=m_gla_norm_g,
                w_out=m_w_out, ln1_g=m_ln1_g, ln1_b=m_ln1_b, w_ffn_gate=m_w_ffn_gate, w_ffn_up=m_w_ffn_up,
                w_ffn_down=m_w_ffn_down, ln2_g=m_ln2_g, ln2_b=m_ln2_b)
    v_in = dict(meta_tokens=v_meta_tokens, ln_in_g=v_ln_in_g, ln_in_b=v_ln_in_b, w_in=v_w_in, b_in=v_b_in,
                w_gate_lr2=v_w_gate_lr2, b_gate_lr2=v_b_gate_lr2, attn_sinks=v_attn_sinks, gla_norm_g=v_gla_norm_g,
                w_out=v_w_out, ln1_g=v_ln1_g, ln1_b=v_ln1_b, w_ffn_gate=v_w_ffn_gate, w_ffn_up=v_w_ffn_up,
                w_ffn_down=v_w_ffn_down, ln2_g=v_ln2_g, ln2_b=v_ln2_b)
    names = list(weights)
    big_names = ("w_in", "w_out", "w_ffn_gate", "w_ffn_up", "w_ffn_down")

    delta, new_m, new_v = {}, {}, {}
    flips = [(lambda a: a.T) if kk in ("w_in", "w_g", "w_u") else (lambda a: a) for kk in BIG]
    updated = _adamw([(flip(weights[k][0]), big_g[kk], flip(m_in[k][0]), flip(v_in[k][0]))
                      for k, kk, flip in zip(big_names, BIG, flips)])
    for k, flip, results in zip(big_names, flips, updated):
        delta[k], new_m[k], new_v[k] = (flip(t)[None] for t in results)
    small_in = [tuple(src[k].reshape(shape) for src in (weights, m_in, v_in)) for k, shape in SMALL]
    place = jnp.stack([2 * chip + lax.axis_index("c"), chip]).astype(jnp.int32)
    small_own, small_all = _small_wait(small_handle, [updated[0][0]])
    small_out, tail_row = _adamw_small(place, small_all, small_own, small_in)
    for (k, _), results in zip(SMALL, small_out):
        grads[k], delta[k], new_m[k], new_v[k] = (r.reshape(weights[k].shape) for r in results)

    return (tail_row[0, TAIL_LOSS], dx[None], *[grads[k] for k in names], *[delta[k] for k in names], *[new_m[k] for k in names],
            *[new_v[k] for k in names])
```

```python
import jax
import jax.numpy as jnp
from jax import lax
from jax.experimental import pallas as pl
from jax.experimental.pallas import tpu as pltpu

F32 = jnp.float32
BF16 = jnp.bfloat16
MESH = pl.DeviceIdType.MESH

D = 1024
SEQ = 4096
N_META = 16
SWA_HEADS, SWA_KV_HEADS, DH = 8, 2, 64
WINDOW = 128
GLA_HEADS, DK, DV = 4, 64, 128
GLA_TAU = 16.0
CH = 64
D_FF = 2816
D_IN = 2320
LN_EPS = 1e-5
RMS_EPS = 1e-6
ALPHA = 2.0 ** 0.25
NEG = -1e30
ADAM_LR, ADAM_B1, ADAM_B2, ADAM_EPS, ADAM_WD, ADAM_STEP = 0.001, 0.9, 0.999, 1e-8, 0.01, 10
O_QS, O_KS, O_VS, O_QG, O_KG, O_VG, O_RG, O_LR = 0, 512, 640, 768, 1024, 1280, 1792, 2304

LANE = 128
BLK = WINDOW
GATE_RANK = 16
D_IN_P = D_IN + LANE - GATE_RANK
META_OFF = CH - N_META
HEAD_POS = (0, 4, 1, 5, 2, 6, 3, 7)
LN_ROWS = 512
TOKEN = (8, LANE)
N_CHIPS = 4
SHARD_ROWS = dict(w_in=D_IN // N_CHIPS, w_out=D // N_CHIPS, w_g=D_FF // N_CHIPS, w_u=D_FF // N_CHIPS,
                  w_d=D_FF // N_CHIPS)
SMALL_ROWS = 48
BF16_ROWS = 16
W_IN_WIN = -(-SHARD_ROWS["w_in"] // (2 * BF16_ROWS)) * 2 * BF16_ROWS
VMEM_CAP_MB = 64
VMEM_SPARE_MB = 6


def _lp():
    return SEQ + BLK


def _row_tile(cap):
    lp = _lp()
    return max(t for t in range(16, cap + 1, 16) if lp % t == 0)


def _params(vmem_mb, **kw):
    assert vmem_mb <= VMEM_CAP_MB - VMEM_SPARE_MB
    return pltpu.CompilerParams(vmem_limit_bytes=vmem_mb << 20, **kw)


def _seq(n=1):
    return ("arbitrary",) * n


def _const(shape):
    return pl.BlockSpec(shape, lambda *_: (0,) * len(shape), pipeline_mode=pl.Buffered(1))


def _acc(shape):
    return pl.BlockSpec(shape, lambda *_: (0,) * len(shape))


def _rows(tm, width):
    return pl.BlockSpec((tm, width), lambda i: (i, 0))


def _dot(a, b):
    return jnp.dot(a.astype(BF16), b.astype(BF16), preferred_element_type=F32)


def _dot_nt(a, b):
    return lax.dot_general(a.astype(BF16), b.astype(BF16), (((1,), (1,)), ((), ())), preferred_element_type=F32)


def _dot_tn(a, b):
    return lax.dot_general(a.astype(BF16), b.astype(BF16), (((0,), (0,)), ((), ())), preferred_element_type=F32)


def _dot_exact(a, b):
    return jnp.dot(a, b, precision=lax.Precision.HIGHEST, preferred_element_type=F32)


def _ln_stats(x):
    mu = jnp.mean(x, axis=-1, keepdims=True)
    xc = x - mu
    rstd = lax.rsqrt(jnp.mean(xc * xc, axis=-1, keepdims=True) + LN_EPS)
    return xc * rstd, rstd


def _ln_bwd(dy, xhat, rstd, g):
    dxh = dy * g
    return rstd * (dxh - jnp.mean(dxh, axis=-1, keepdims=True) - xhat * jnp.mean(dxh * xhat, axis=-1, keepdims=True))


def _sigmoid(x):
    return 1.0 / (1.0 + jnp.exp(-x))


def _iota(shape, dim):
    return lax.broadcasted_iota(jnp.int32, shape, dim)


def _hbm(*arrays):
    return tuple(pltpu.with_memory_space_constraint(a, pltpu.HBM) for a in arrays)


def _ln_in_fwd_real(x, g, b, token):
    tr = min(LN_ROWS, SEQ)

    def body(x_ref, g_ref, b_ref, token_ref, h_ref):
        xhat, _ = _ln_stats(x_ref[...])
        h_ref[...] = xhat * g_ref[...] + b_ref[...]

    return pl.pallas_call(
        body, name="ln_in_fwd", grid=(SEQ // tr,),
        in_specs=[_rows(tr, D), _const((1, D)), _const((1, D)), _const(TOKEN)],
        out_specs=_rows(tr, D),
        out_shape=pltpu.HBM((_lp(), D), F32),
        compiler_params=_params(32, dimension_semantics=_seq()),
    )(*_hbm(x, g, b), token)


def _ln_in_fwd_meta(h_real, meta_ext, g, b):
    def meta_body(m_ref, g_ref, b_ref, real_ref, h_ref):
        xhat, _ = _ln_stats(m_ref[...])
        h_ref[...] = xhat * g_ref[...] + b_ref[...]

    return pl.pallas_call(
        meta_body, name="ln_in_fwd_meta", grid=(1,),
        in_specs=[_const((BLK, D)), _const((1, D)), _const((1, D)), pl.BlockSpec(memory_space=pl.ANY)],
        out_specs=pl.BlockSpec((BLK, D), lambda i: (SEQ // BLK, 0)),
        out_shape=pltpu.HBM((_lp(), D), F32),
        input_output_aliases={3: 0},
        compiler_params=_params(16, dimension_semantics=_seq()),
    )(*_hbm(meta_ext, g, b, h_real))


def _in_proj(h0, w_in_t, b_in_p, wg2_p, bg2):
    tm = _row_tile(384)
    lp = _lp()
    widths = (512, 128, 128, 256, 256, 512, 512, 128)
    offs = (O_QS, O_KS, O_VS, O_QG, O_KG, O_VG, O_RG, O_LR)

    def body(h_ref, w_ref, b_ref, wg2_ref, bg2_ref, *outs):
        proj = _dot_nt(h_ref[...], w_ref[...]) + b_ref[...]
        for pos, h in enumerate(HEAD_POS):
            outs[0][:, pos * DH:(pos + 1) * DH] = proj[:, O_QS + h * DH:O_QS + (h + 1) * DH]
        for o_ref, off, wd in zip(outs[1:8], offs[1:], widths[1:]):
            o_ref[...] = proj[:, off:off + wd]
        outs[8][...] = _dot(proj[:, O_LR:O_LR + LANE], wg2_ref[...]) + bg2_ref[...]

    return pl.pallas_call(
        body, name="in_proj", grid=(lp // tm,),
        in_specs=[_rows(tm, D), _const((D_IN_P, D)), _const((1, D_IN_P)), _const((LANE, 256)), _const((1, 256))],
        out_specs=[_rows(tm, w) for w in widths] + [_rows(tm, 256)],
        out_shape=[pltpu.HBM((lp, w), F32) for w in widths] + [pltpu.HBM((lp, 256), F32)],
        compiler_params=_params(40, dimension_semantics=_seq()),
    )(*_hbm(h0, w_in_t, b_in_p, wg2_p, bg2))


def _swa_masks(n):
    nb = SEQ // BLK
    is_meta = n == nb
    ri = _iota((BLK, BLK), 0)
    cj = _iota((BLK, BLK), 1)
    meta_col = ((cj >= META_OFF) & (cj < CH)).astype(jnp.int32)
    meta_q = meta_col * ((cj <= ri) & (ri < CH)).astype(jnp.int32)
    valid_m = jnp.where(is_meta, meta_q, meta_col) > 0
    dist_m = jnp.where(is_meta, ri - cj, n * BLK + ri + CH - cj).astype(F32)
    valid_p = jnp.where((n >= 1) & (n < nb), (cj > ri).astype(jnp.int32), 0) > 0
    dist_p = (ri + BLK - cj).astype(F32)
    valid_c = jnp.where(n < nb, (cj <= ri).astype(jnp.int32), 0) > 0
    dist_c = (ri - cj).astype(F32)
    return (dist_m, dist_p, dist_c), (valid_m, valid_p, valid_c)


def _swa_bias(n):
    dists, valids = _swa_masks(n)
    return (jnp.concatenate([-d for d in dists], axis=1),
            jnp.concatenate([jnp.where(v, 0.0, NEG) for v in valids], axis=1))


def _swa_half(ref, pos, scale=1.0):
    col = ref[:, (pos // 2) * LANE:(pos // 2 + 1) * LANE]
    lane = _iota((BLK, LANE), 1)
    mine = lane < DH if pos % 2 == 0 else lane >= DH
    return jnp.where(mine, col * scale, 0.0).astype(BF16)


def _swa_merge(even, odd):
    return jnp.where(_iota((BLK, LANE), 1) < DH, even, odd)


def _swa_softmax(t, sink):
    m = jnp.maximum(jnp.max(t, axis=-1, keepdims=True), sink)
    e = jnp.exp(t - m)
    e_sink = jnp.exp(sink - m)
    inv = 1.0 / (jnp.sum(e, axis=-1, keepdims=True) + e_sink)
    return e * inv, e_sink * inv


def _swa_kv_specs(width):
    nb = SEQ // BLK
    return [pl.BlockSpec((BLK, width), lambda n: (nb, 0)),
            pl.BlockSpec((BLK, width), lambda n: (jnp.clip(n - 1, 0, nb - 1), 0)),
            pl.BlockSpec((BLK, width), lambda n: (jnp.minimum(n, nb), 0))]


def _swa_fwd(sinks, qs, ks, vs):
    nb = SEQ // BLK
    heads = range(SWA_HEADS)

    def body(sink_ref, q_ref, km_ref, kp_ref, kc_ref, vm_ref, vp_ref, vc_ref, o_ref):
        negdist, maskbias = _swa_bias(pl.program_id(0))
        k_all = jnp.concatenate([km_ref[...], kp_ref[...], kc_ref[...]], axis=0).astype(BF16)
        v_all = jnp.concatenate([vm_ref[...], vp_ref[...], vc_ref[...]], axis=0).astype(BF16)
        q = [_swa_half(q_ref, pos, DH ** -0.5) for pos in heads]
        t = [_dot_nt(q[pos], k_all) + (2.0 ** -(HEAD_POS[pos] + 1) * negdist + maskbias) for pos in heads]
        p = [_swa_softmax(t[pos], sink_ref[HEAD_POS[pos]])[0].astype(BF16) for pos in heads]
        o = [_dot(p[pos], v_all) for pos in heads]
        for col in range(SWA_HEADS // 2):
            o_ref[:, col * LANE:(col + 1) * LANE] = _swa_merge(o[2 * col], o[2 * col + 1])

    kvw = SWA_KV_HEADS * DH
    return pl.pallas_call(
        body, name="swa_fwd", grid=(nb + 1,),
        in_specs=[pl.BlockSpec(memory_space=pltpu.SMEM), _rows(BLK, SWA_HEADS * DH)] + _swa_kv_specs(kvw) + _swa_kv_specs(kvw),
        out_specs=_rows(BLK, SWA_HEADS * DH),
        out_shape=pltpu.HBM((_lp(), SWA_HEADS * DH), F32),
        compiler_params=_params(16, dimension_semantics=_seq()),
    )(sinks, *_hbm(qs, ks, ks, ks, vs, vs, vs))


GLA_PER_STEP = BLK // CH


def _gla_block(s):
    nb = SEQ // BLK
    return jnp.where(s == 0, nb, s - 1)


def _gla_rowmask(s):
    ri = _iota((BLK, 1), 0)
    m = jnp.where(s == 0, ((ri >= META_OFF) & (ri < CH)).astype(jnp.int32), 1)
    return (m > 0).astype(F32) + jnp.zeros((BLK, 1), F32)


def _gla_chunk_masks():
    r, c = _iota((BLK, BLK), 0), _iota((BLK, BLK), 1)
    same = ((r < CH) & (c < CH)) | ((r >= CH) & (c >= CH))
    return same & (r >= c), same & (r <= c), same


def _gla_decay(z, rmask):
    log_g = (jnp.minimum(z, 0.0) - jnp.log1p(jnp.exp(-jnp.abs(z)))) * (rmask / GLA_TAU)
    lower, _, same = _gla_chunk_masks()
    return _dot_exact(lower.astype(F32), log_g), _dot_exact(same.astype(F32), log_g)


def _gla_slices(c, h):
    return slice(c * CH, (c + 1) * CH), slice(h * DK, (h + 1) * DK), slice(h * DV, (h + 1) * DV)


def _gla_fwd(qg, kg, vg, z):
    steps = SEQ // BLK + 1
    kw, vw = GLA_HEADS * DK, GLA_HEADS * DV
    pairs = [(c, h) for c in range(GLA_PER_STEP) for h in range(GLA_HEADS)]

    def body(q_ref, k_ref, v_ref, z_ref, o_ref, st_ref, st):
        s = pl.program_id(0)

        @pl.when(s == 0)
        def _():
            st[...] = jnp.zeros_like(st)

        rmask = _gla_rowmask(s)
        b, b_last = _gla_decay(z_ref[...], rmask)
        q = q_ref[...] * (rmask * DK ** -0.5)
        k = k_ref[...] * rmask
        v = v_ref[...] * rmask
        qe = q * jnp.exp(b)
        ke = k * jnp.exp(-b)
        kd = k * jnp.exp(b_last - b)
        e_last = jnp.exp(b_last)
        causal = _iota((CH, CH), 0) >= _iota((CH, CH), 1)
        a, upd, intra = {}, {}, {}
        for c, h in pairs:
            rows, ks, vs_ = _gla_slices(c, h)
            a[c, h] = jnp.where(causal, _dot_nt(qe[rows, ks], ke[rows, ks]), 0.0)
            upd[c, h] = _dot_tn(v[rows, vs_], kd[rows, ks])
        for c, h in pairs:
            rows, ks, vs_ = _gla_slices(c, h)
            intra[c, h] = _dot(a[c, h], v[rows, vs_])
        state = st[...]
        for c in range(GLA_PER_STEP):
            st_ref[0, c] = state
            for h in range(GLA_HEADS):
                rows, ks, vs_ = _gla_slices(c, h)
                o_ref[rows, vs_] = intra[c, h] + _dot_nt(qe[rows, ks], state[:, ks])
            state = state * e_last[c * CH:c * CH + 1] + jnp.concatenate([upd[c, h] for h in range(GLA_HEADS)], axis=1)
        st[...] = state

    blk = lambda w: pl.BlockSpec((BLK, w), lambda s: (_gla_block(s), 0))
    return pl.pallas_call(
        body, name="gla_fwd", grid=(steps,),
        in_specs=[blk(kw), blk(kw), blk(vw), blk(kw)],
        out_specs=[blk(vw), pl.BlockSpec((1, GLA_PER_STEP, DV, kw), lambda s: (s, 0, 0, 0))],
        out_shape=[pltpu.HBM((_lp(), vw), F32), pltpu.HBM((steps, GLA_PER_STEP, DV, kw), F32)],
        scratch_shapes=[pltpu.VMEM((DV, kw), F32)],
        compiler_params=_params(16, dimension_semantics=_seq()),
    )(*_hbm(qg, kg, vg, z))


def _post_mix(o_s, o_gla, r_g, h0, gn4, w_out, g1, b1, token):
    tm = _row_tile(384)
    lp = _lp()

    def body(os_ref, og_ref, r_ref, h0_ref, gn_ref, w_ref, g_ref, b_ref, token_ref, o_ref, pre_ref, h1_ref):
        for pos, h in enumerate(HEAD_POS):
            o_ref[:, h * DH:(h + 1) * DH] = os_ref[:, pos * DH:(pos + 1) * DH].astype(BF16)
        for h in range(GLA_HEADS):
            hs = slice(h * DV, (h + 1) * DV)
            xg = og_ref[:, hs]
            n = xg * lax.rsqrt(jnp.mean(xg * xg, axis=-1, keepdims=True) + RMS_EPS) * gn_ref[...]
            r = r_ref[:, hs]
            o_ref[:, 512 + h * DV:512 + (h + 1) * DV] = (n * (r * _sigmoid(r))).astype(BF16)
        pre = ALPHA * h0_ref[...] + _dot(o_ref[...], w_ref[...])
        pre_ref[...] = pre
        xhat, _ = _ln_stats(pre)
        h1_ref[...] = xhat * g_ref[...] + b_ref[...]

    return pl.pallas_call(
        body, name="post_mix", grid=(lp // tm,),
        in_specs=[_rows(tm, 512), _rows(tm, 512), _rows(tm, 512), _rows(tm, D), _const((1, DV)), _const((D, D)),
                  _const((1, D)), _const((1, D)), _const(TOKEN)],
        out_specs=[_rows(tm, D), _rows(tm, D), _rows(tm, D)],
        out_shape=[pltpu.HBM((lp, D), BF16), pltpu.HBM((lp, D), F32),
                   pltpu.HBM((lp, D), F32)],
        compiler_params=_params(32, dimension_semantics=_seq()),
    )(*_hbm(o_s, o_gla, r_g, h0, gn4, w_out, g1, b1), token)


def _ffn_fwd_loss_bwd(h1, wg_t, wu_t, wd, target, g2, b2):
    lp = _lp()
    tm = max(t for t in range(BLK, 384 + 1, BLK) if lp % t == 0)
    steps = lp // tm
    last_blk = SEQ // BLK - 1
    half = D_FF // 2
    n_t = tm // BLK

    def body(*refs):
        h_ref, wg_ref, wu_ref, wd_ref = refs[:4]
        t_refs = refs[4:4 + n_t]
        g2_ref, b2_ref, a_ref, dgate_ref, dup_ref, dp_ref, loss_ref, dg_ref, db_ref, g_s, u_s, acc = refs[4 + n_t:]
        i = pl.program_id(0)

        @pl.when(i == 0)
        def _():
            acc[...] = jnp.zeros_like(acc)
            dg_ref[...] = jnp.zeros_like(dg_ref)
            db_ref[...] = jnp.zeros_like(db_ref)

        h = h_ref[...]
        hb = h.astype(BF16)
        pre = ALPHA * h
        for j in range(2):
            cols = slice(j * half, (j + 1) * half)
            g = _dot_nt(hb, wg_ref[cols, :])
            u = _dot_nt(hb, wu_ref[cols, :])
            g_s[:, cols] = g
            u_s[:, cols] = u
            pre = pre + _dot(g * _sigmoid(g) * u, wd_ref[cols, :])
        xhat, rstd = _ln_stats(pre)
        real = i * tm + _iota((tm, 1), 0) < SEQ
        target_rows = jnp.concatenate([t[...] for t in t_refs], axis=0)
        diff = jnp.where(real, xhat * g2_ref[...] + b2_ref[...] - target_rows, 0.0)
        acc[...] += jnp.sum(diff * diff, axis=0, keepdims=True)
        dy = diff * (1.0 / D)
        dpre = _ln_bwd(dy, xhat, rstd, g2_ref[...])
        dp_ref[...] = dpre
        dg_ref[...] += jnp.sum(dy * xhat, axis=0, keepdims=True)
        db_ref[...] += jnp.sum(dy, axis=0, keepdims=True)
        dpb = dpre.astype(BF16)
        for j in range(2):
            cols = slice(j * half, (j + 1) * half)
            g, u = g_s[:, cols], u_s[:, cols]
            sg = _sigmoid(g)
            silu = g * sg
            da = _dot_nt(dpb, wd_ref[cols, :])
            a_ref[:, cols] = (silu * u).astype(BF16)
            dgate_ref[:, cols] = (da * u * (sg * (1.0 + g * (1.0 - sg)))).astype(BF16)
            dup_ref[:, cols] = (da * silu).astype(BF16)

        @pl.when(i == steps - 1)
        def _():
            loss_ref[...] = jnp.zeros_like(loss_ref) + (0.5 / D) * jnp.sum(acc[...], axis=1, keepdims=True)

    t_spec = lambda k: pl.BlockSpec((BLK, D), lambda i: (jnp.minimum(i * n_t + k, last_blk), 0))
    return pl.pallas_call(
        body, name="ffn_fwd_loss_bwd", grid=(steps,),
        in_specs=[_rows(tm, D), _const((D_FF, D)), _const((D_FF, D)), _const((D_FF, D))]
        + [t_spec(k) for k in range(n_t)] + [_const((1, D)), _const((1, D))],
        out_specs=[_rows(tm, D_FF), _rows(tm, D_FF), _rows(tm, D_FF), _rows(tm, D), _acc((1, LANE)), _acc((1, D)),
                   _acc((1, D))],
        out_shape=[pltpu.HBM((lp, D_FF), BF16)] * 3 + [pltpu.HBM((lp, D), F32), pltpu.HBM((1, LANE), F32),
                                                         pltpu.HBM((1, D), F32), pltpu.HBM((1, D), F32)],
        scratch_shapes=[pltpu.VMEM((tm, D_FF), F32), pltpu.VMEM((tm, D_FF), F32), pltpu.VMEM((1, D), F32)],
        compiler_params=_params(58, dimension_semantics=_seq()),
    )(*_hbm(h1, wg_t, wu_t, wd, *[target] * n_t, g2, b2))


def _ffn_out_bwd(dpre2, dgate, dup, pre1, wg_t, wu_t, g1, w_out, o_gla, r_g, gn4):
    tm = _row_tile(384)
    lp = _lp()

    def body(dp_ref, dg_ref, du_ref, p1_ref, wg_ref, wu_ref, g1_ref, w_ref, og_ref, r_ref, gn_ref,
             dp1_ref, dg1_ref, db1_ref, dos_ref, dog_ref, dr_ref, dgn_ref):
        @pl.when(pl.program_id(0) == 0)
        def _():
            for acc_ref in (dg1_ref, db1_ref, dgn_ref):
                acc_ref[...] = jnp.zeros_like(acc_ref)

        dh1 = ALPHA * dp_ref[...] + _dot(dg_ref[...], wg_ref[...]) + _dot(du_ref[...], wu_ref[...])
        xhat, rstd1 = _ln_stats(p1_ref[...])
        dpre1 = _ln_bwd(dh1, xhat, rstd1, g1_ref[...])
        dp1_ref[...] = dpre1
        dg1_ref[...] += jnp.sum(dh1 * xhat, axis=0, keepdims=True)
        db1_ref[...] += jnp.sum(dh1, axis=0, keepdims=True)

        do = _dot_nt(dpre1, w_ref[...])
        for pos, h in enumerate(HEAD_POS):
            dos_ref[:, pos * DH:(pos + 1) * DH] = do[:, h * DH:(h + 1) * DH]
        gn = gn_ref[...]
        for h in range(GLA_HEADS):
            hs = slice(h * DV, (h + 1) * DV)
            xg = og_ref[:, hs]
            rstd = lax.rsqrt(jnp.mean(xg * xg, axis=-1, keepdims=True) + RMS_EPS)
            nx = xg * rstd
            r = r_ref[:, hs]
            sr = _sigmoid(r)
            d_o = do[:, 512 + h * DV:512 + (h + 1) * DV]
            dr_ref[:, hs] = d_o * (nx * gn) * (sr * (1.0 + r * (1.0 - sr)))
            dn = d_o * (r * sr)
            dgn_ref[...] += jnp.sum(dn * nx, axis=0, keepdims=True)
            dnx = dn * gn
            dog_ref[:, hs] = rstd * (dnx - nx * jnp.mean(dnx * nx, axis=-1, keepdims=True))

    return pl.pallas_call(
        body, name="ffn_out_bwd", grid=(lp // tm,),
        in_specs=[_rows(tm, D), _rows(tm, D_FF), _rows(tm, D_FF), _rows(tm, D), _const((D_FF, D)), _const((D_FF, D)),
                  _const((1, D)), _const((D, D)), _rows(tm, 512), _rows(tm, 512), _const((1, DV))],
        out_specs=[_rows(tm, D), _acc((1, D)), _acc((1, D)), _rows(tm, 512), _rows(tm, 512), _rows(tm, 512),
                   _acc((1, DV))],
        out_shape=[pltpu.HBM((lp, D), F32), pltpu.HBM((1, D), F32), pltpu.HBM((1, D), F32)]
        + [pltpu.HBM((lp, 512), F32)] * 3 + [pltpu.HBM((1, DV), F32)],
        compiler_params=_params(48, dimension_semantics=_seq()),
    )(*_hbm(dpre2, dgate, dup, pre1, wg_t, wu_t, g1, w_out, o_gla, r_g, gn4))


def _atb(a, b, name, token=None):
    lp = _lp()
    tm = _row_tile(1408)
    n, w = a.shape[1], b.shape[1]
    bw = 512 if n * w * 4 > (4 << 20) else w
    tokens = [] if token is None else [token]

    def body(a_ref, b_ref, *rest):
        o_ref = rest[-1]

        @pl.when(pl.program_id(1) == 0)
        def _():
            o_ref[...] = jnp.zeros_like(o_ref)

        o_ref[...] += _dot_tn(a_ref[...], b_ref[...])

    return pl.pallas_call(
        body, name=name, grid=(w // bw, lp // tm),
        in_specs=[pl.BlockSpec((tm, n), lambda j, k: (k, 0)), pl.BlockSpec((tm, bw), lambda j, k: (k, j))]
        + [_const(TOKEN)] * len(tokens),
        out_specs=pl.BlockSpec((n, bw), lambda j, k: (0, j)),
        out_shape=pltpu.HBM((n, w), F32),
        compiler_params=_params(48, dimension_semantics=_seq(2)),
    )(*_hbm(a, b), *tokens)


def _gla_bwd(qg, kg, vg, z, do_gla, st_all, token):
    steps = SEQ // BLK + 1
    kw, vw = GLA_HEADS * DK, GLA_HEADS * DV
    pairs = [(c, h) for c in range(GLA_PER_STEP) for h in range(GLA_HEADS)]
    heads = range(GLA_HEADS)

    def body(q_ref, k_ref, v_ref, z_ref, do_ref, st_ref, token_ref, dq_ref, dk_ref, dv_ref, dz_ref, dst):
        @pl.when(pl.program_id(0) == 0)
        def _():
            dst[...] = jnp.zeros_like(dst)

        rmask = _gla_rowmask(steps - 1 - pl.program_id(0))
        zz = z_ref[...]
        b, b_last = _gla_decay(zz, rmask)
        e_b, e_nb, e_kd, e_last = jnp.exp(b), jnp.exp(-b), jnp.exp(b_last - b), jnp.exp(b_last)
        q = q_ref[...] * (rmask * DK ** -0.5)
        k = k_ref[...] * rmask
        v = v_ref[...] * rmask
        qe, ke, kd = q * e_b, k * e_nb, k * e_kd
        d_o = do_ref[...]
        causal = _iota((CH, CH), 0) >= _iota((CH, CH), 1)
        a, da, dqe, dke, dv_intra, carry = {}, {}, {}, {}, {}, {}
        for c, h in pairs:
            rows, ks, vs_ = _gla_slices(c, h)
            a[c, h] = jnp.where(causal, _dot_nt(qe[rows, ks], ke[rows, ks]), 0.0)
            da[c, h] = jnp.where(causal, _dot_nt(d_o[rows, vs_], v[rows, vs_]), 0.0)
            carry[c, h] = _dot_tn(d_o[rows, vs_], qe[rows, ks])
        for c, h in pairs:
            rows, ks, vs_ = _gla_slices(c, h)
            dqe[c, h] = _dot(d_o[rows, vs_], st_ref[0, c][:, ks]) + _dot(da[c, h], ke[rows, ks])
            dke[c, h] = _dot_tn(da[c, h], qe[rows, ks])
            dv_intra[c, h] = _dot_tn(a[c, h], d_o[rows, vs_])
        dstate = dst[...]
        dkd, db_decay = {}, {}
        for c in reversed(range(GLA_PER_STEP)):
            for h in heads:
                rows, ks, vs_ = _gla_slices(c, h)
                dkd[c, h] = _dot(v[rows, vs_], dstate[:, ks])
                dv_ref[rows, vs_] = dv_intra[c, h] + _dot_nt(kd[rows, ks], dstate[:, ks])
            chunk_last = e_last[c * CH:c * CH + 1]
            db_decay[c] = jnp.sum(dstate * st_ref[0, c], axis=0, keepdims=True) * chunk_last
            dstate = dstate * chunk_last + jnp.concatenate([carry[c, h] for h in heads], axis=1)
        dst[...] = dstate
        rows_of = lambda parts: jnp.concatenate(
            [jnp.concatenate([parts[c, h] for h in heads], axis=1) for c in range(GLA_PER_STEP)], axis=0)
        dqe_all, dke_all, dkd_all = rows_of(dqe), rows_of(dke), rows_of(dkd)
        dq_ref[...] = dqe_all * e_b * (rmask * DK ** -0.5)
        dk_ref[...] = (dke_all * e_nb + dkd_all * e_kd) * rmask
        dkd_kd = dkd_all * kd
        db = dqe_all * qe - dke_all * ke - dkd_kd
        _, upper, same = _gla_chunk_masks()
        decay_rows = jnp.concatenate([jnp.broadcast_to(db_decay[c], (CH, kw)) for c in range(GLA_PER_STEP)], axis=0)
        dlog_g = _dot_exact(upper.astype(F32), db) + _dot_exact(same.astype(F32), dkd_kd) + decay_rows
        dz_ref[...] = dlog_g * (rmask / GLA_TAU) * _sigmoid(-zz)

    blk = lambda w: pl.BlockSpec((BLK, w), lambda s: (_gla_block(steps - 1 - s), 0))
    return pl.pallas_call(
        body, name="gla_bwd", grid=(steps,),
        in_specs=[blk(kw), blk(kw), blk(vw), blk(kw), blk(vw),
                  pl.BlockSpec((1, GLA_PER_STEP, DV, kw), lambda s: (steps - 1 - s, 0, 0, 0)), _const(TOKEN)],
        out_specs=[blk(kw), blk(kw), blk(vw), blk(kw)],
        out_shape=[pltpu.HBM((_lp(), kw), F32), pltpu.HBM((_lp(), kw), F32),
                   pltpu.HBM((_lp(), vw), F32), pltpu.HBM((_lp(), kw), F32)],
        scratch_shapes=[pltpu.VMEM((DV, kw), F32)],
        compiler_params=_params(16, dimension_semantics=_seq()),
    )(*_hbm(qg, kg, vg, z, do_gla, st_all), token)


def _swa_bwd(sinks, qs, ks, vs, do_s, token):
    nb = SEQ // BLK
    kvw = SWA_KV_HEADS * DH
    scale = DH ** -0.5
    heads = range(SWA_HEADS)

    def body(sink_ref, q_ref, km_ref, kp_ref, kc_ref, vm_ref, vp_ref, vc_ref, do_ref, token_ref,
             dq_ref, dk_ref, dv_ref, dsink_ref, carry_k, carry_v, meta_k, meta_v):
        n = pl.program_id(0)

        @pl.when(n == 0)
        def _():
            for r in (carry_k, carry_v, meta_k, meta_v):
                r[...] = jnp.zeros_like(r)
            dsink_ref[...] = jnp.zeros_like(dsink_ref)

        @pl.when(n <= nb)
        def _():
            negdist, maskbias = _swa_bias(n)
            lane = _iota((1, LANE), 1)
            k_all = jnp.concatenate([km_ref[...], kp_ref[...], kc_ref[...]], axis=0).astype(BF16)
            v_all = jnp.concatenate([vm_ref[...], vp_ref[...], vc_ref[...]], axis=0).astype(BF16)
            q = [_swa_half(q_ref, pos, scale) for pos in heads]
            d_o = [_swa_half(do_ref, pos) for pos in heads]
            t = [_dot_nt(q[pos], k_all) + (2.0 ** -(HEAD_POS[pos] + 1) * negdist + maskbias) for pos in heads]
            dp = [_dot_nt(d_o[pos], v_all) for pos in heads]
            soft = [_swa_softmax(t[pos], sink_ref[HEAD_POS[pos]]) for pos in heads]
            p = [s[0] for s in soft]
            delta = [jnp.sum(p[pos] * dp[pos], axis=-1, keepdims=True) for pos in heads]
            ds = [(p[pos] * (dp[pos] - delta[pos])).astype(BF16) for pos in heads]
            dq = [_dot(ds[pos], k_all) for pos in heads]
            for col in range(SWA_HEADS // 2):
                dq_ref[:, col * LANE:(col + 1) * LANE] = scale * _swa_merge(dq[2 * col], dq[2 * col + 1])
            dsink = jnp.zeros((1, LANE), F32)
            for pos in heads:
                dsink = dsink + jnp.where(lane == HEAD_POS[pos],
                                          -jnp.sum(soft[pos][1] * delta[pos], axis=0, keepdims=True), 0.0)
            dsink_ref[...] += dsink
            dk3 = _dot_tn(jnp.concatenate(q, axis=0), jnp.concatenate(ds, axis=0)).T
            dv3 = _dot_tn(jnp.concatenate(d_o, axis=0), jnp.concatenate([x.astype(BF16) for x in p], axis=0)).T
            meta_k[...] += dk3[0:BLK]
            meta_v[...] += dv3[0:BLK]
            dk_ref[...] = carry_k[...] + dk3[BLK:2 * BLK]
            dv_ref[...] = carry_v[...] + dv3[BLK:2 * BLK]
            carry_k[...] = dk3[2 * BLK:3 * BLK]
            carry_v[...] = dv3[2 * BLK:3 * BLK]

        @pl.when(n == nb + 1)
        def _():
            dk_ref[...] = meta_k[...]
            dv_ref[...] = meta_v[...]

    kv_out = pl.BlockSpec((BLK, kvw), lambda n: (jnp.where(n == nb + 1, nb, jnp.clip(n - 1, 0, nb - 1)), 0))
    qblk = pl.BlockSpec((BLK, SWA_HEADS * DH), lambda n: (jnp.minimum(n, nb), 0))
    return pl.pallas_call(
        body, name="swa_bwd", grid=(nb + 2,),
        in_specs=[pl.BlockSpec(memory_space=pltpu.SMEM), qblk] + _swa_kv_specs(kvw) + _swa_kv_specs(kvw)
        + [qblk, _const(TOKEN)],
        out_specs=[qblk, kv_out, kv_out, _acc((1, LANE))],
        out_shape=[pltpu.HBM((_lp(), SWA_HEADS * DH), F32), pltpu.HBM((_lp(), kvw), F32),
                   pltpu.HBM((_lp(), kvw), F32), pltpu.HBM((1, LANE), F32)],
        scratch_shapes=[pltpu.VMEM((BLK, kvw), F32)] * 4,
        compiler_params=_params(16, dimension_semantics=_seq()),
    )(sinks, *_hbm(qs, ks, ks, ks, vs, vs, vs, do_s), token)


def _in_bwd(dqs, dks, dvs, dqg, dkg, dvg, drg, dz, dpre1, w_in_t, wg2_p):
    tm = _row_tile(384)
    lp = _lp()
    widths = (512, 128, 128, 256, 256, 512, 512)
    offs = (O_QS, O_KS, O_VS, O_QG, O_KG, O_VG, O_RG)

    def body(*refs):
        parts, (dz_ref, dp1_ref, w_ref, wg2_ref, dproj_ref, dh0_ref, dbin_ref, dbg_ref) = refs[:7], refs[7:]

        @pl.when(pl.program_id(0) == 0)
        def _():
            dbin_ref[...] = jnp.zeros_like(dbin_ref)
            dbg_ref[...] = jnp.zeros_like(dbg_ref)

        for pos, h in enumerate(HEAD_POS):
            val = parts[0][:, pos * DH:(pos + 1) * DH]
            dproj_ref[:, O_QS + h * DH:O_QS + (h + 1) * DH] = val.astype(BF16)
            dbin_ref[:, O_QS + h * DH:O_QS + (h + 1) * DH] += jnp.sum(val, axis=0, keepdims=True)
        for p_ref, off, wd in zip(parts[1:], offs[1:], widths[1:]):
            val = p_ref[...]
            dproj_ref[:, off:off + wd] = val.astype(BF16)
            dbin_ref[:, off:off + wd] += jnp.sum(val, axis=0, keepdims=True)
        dz = dz_ref[...]
        dlr = _dot_nt(dz, wg2_ref[...])
        dproj_ref[:, O_LR:O_LR + LANE] = dlr.astype(BF16)
        dbin_ref[:, O_LR:O_LR + LANE] += jnp.sum(dlr, axis=0, keepdims=True)
        dbg_ref[...] += jnp.sum(dz, axis=0, keepdims=True)
        dh0_ref[...] = ALPHA * dp1_ref[...] + _dot(dproj_ref[...], w_ref[...])

    return pl.pallas_call(
        body, name="in_bwd", grid=(lp // tm,),
        in_specs=[_rows(tm, w) for w in widths] + [_rows(tm, 256), _rows(tm, D), _const((D_IN_P, D)), _const((LANE, 256))],
        out_specs=[_rows(tm, D_IN_P), _rows(tm, D), _acc((1, D_IN_P)), _acc((1, 256))],
        out_shape=[pltpu.HBM((lp, D_IN_P), BF16), pltpu.HBM((lp, D), F32),
                   pltpu.HBM((1, D_IN_P), F32), pltpu.HBM((1, 256), F32)],
        compiler_params=_params(40, dimension_semantics=_seq()),
    )(*_hbm(dqs, dks, dvs, dqg, dkg, dvg, drg, dz, dpre1, w_in_t, wg2_p))


def _ln_in_bwd(x, meta_ext, dh0, g, token):
    tr = min(LN_ROWS, SEQ)

    def ln_bwd(x_ref, dh_ref, g_ref, dx_ref, dg_ref, db_ref):
        @pl.when(pl.program_id(0) == 0)
        def _():
            dg_ref[...] = jnp.zeros_like(dg_ref)
            db_ref[...] = jnp.zeros_like(db_ref)

        xhat, rstd = _ln_stats(x_ref[...])
        dh = dh_ref[...]
        dx_ref[...] = _ln_bwd(dh, xhat, rstd, g_ref[...])
        dg_ref[...] += jnp.sum(dh * xhat, axis=0, keepdims=True)
        db_ref[...] += jnp.sum(dh, axis=0, keepdims=True)

    def body(x_ref, dh_ref, g_ref, token_ref, dx_ref, dg_ref, db_ref):
        ln_bwd(x_ref, dh_ref, g_ref, dx_ref, dg_ref, db_ref)

    def meta_body(m_ref, dh_ref, g_ref, dm_ref, dg_ref, db_ref):
        ln_bwd(m_ref, dh_ref, g_ref, dm_ref, dg_ref, db_ref)

    sums = [pltpu.HBM((1, D), F32), pltpu.HBM((1, D), F32)]
    dx, dg, db = pl.pallas_call(
        body, name="ln_in_bwd", grid=(SEQ // tr,),
        in_specs=[_rows(tr, D), _rows(tr, D), _const((1, D)), _const(TOKEN)],
        out_specs=[_rows(tr, D), _acc((1, D)), _acc((1, D))],
        out_shape=[pltpu.HBM((SEQ, D), F32)] + sums,
        compiler_params=_params(32, dimension_semantics=_seq()),
    )(*_hbm(x, dh0, g), token)
    dm, dg_m, db_m = pl.pallas_call(
        meta_body, name="ln_in_bwd_meta", grid=(1,),
        in_specs=[_const((BLK, D)), pl.BlockSpec((BLK, D), lambda i: (SEQ // BLK, 0)), _const((1, D))],
        out_specs=[_acc((BLK, D)), _acc((1, D)), _acc((1, D))],
        out_shape=[pltpu.HBM((BLK, D), F32)] + sums,
        compiler_params=_params(16, dimension_semantics=_seq()),
    )(*_hbm(meta_ext, dh0, g))
    return dx, dm, dg + dg_m, db + db_m


def _local_step(x, target, ln_in_g, ln_in_b, b_in, bg2, sinks, gn, g1, b1, g2, b2,
                token, fetch_first, fetch_rest, fetch_ffn, exchange_ffn, ship_ffn, ship_w_in):
    row = lambda v: v.reshape(1, -1).astype(F32)
    b_in_p = jnp.pad(row(b_in), ((0, 0), (0, D_IN_P - D_IN)))
    gn4 = row(gn)
    sinks = sinks.reshape(-1).astype(F32)

    h_real = _ln_in_fwd_real(x, row(ln_in_g), row(ln_in_b), token)
    w_in_t, meta_full, wg2 = fetch_first([h_real])
    meta_ext = jnp.pad(meta_full, ((META_OFF, BLK - CH), (0, 0)))
    wg2_p = jnp.pad(wg2, ((0, LANE - wg2.shape[0]), (0, 0))).astype(BF16)
    h0 = _ln_in_fwd_meta(h_real, meta_ext, row(ln_in_g), row(ln_in_b))
    qs, ks, vs, qg, kg, vg, rg, glr, z = _in_proj(h0, w_in_t, b_in_p, wg2_p, row(bg2))
    o_s = _swa_fwd(sinks, qs, ks, vs)
    o_gla, st_all = _gla_fwd(qg, kg, vg, z)
    w_out, token = fetch_rest([o_s, o_gla])
    o, pre1, h1 = _post_mix(o_s, o_gla, rg, h0, gn4, w_out, row(g1), row(b1), token)
    wg_t, wu_t, wd = fetch_ffn([pre1])
    a, dgate, dup, dpre2, loss, dg2, db2 = _ffn_fwd_loss_bwd(h1, wg_t, wu_t, wd, target, row(g2), row(b2))
    dpre1, dg1, db1, do_s, do_gla, drg, dgn = _ffn_out_bwd(dpre2, dgate, dup, pre1, wg_t, wu_t, row(g1), w_out, o_gla,
                                                           rg, gn4)
    dwd = _atb(a, dpre2, "dw_down")
    dwg_t = _atb(dgate, h1, "dw_gate")
    dwu_t = _atb(dup, h1, "dw_up")
    token = exchange_ffn(dict(w_g=dwg_t, w_u=dwu_t, w_d=dwd))
    token = ship_ffn(_atb(o, dpre1, "dw_out", token))
    dqg, dkg, dvg, dz = _gla_bwd(qg, kg, vg, z, do_gla, st_all, token)
    dqs, dks, dvs, dsinks = _swa_bwd(sinks, qs, ks, vs, do_s, token)
    dproj, dh0, db_in_p, dbg2 = _in_bwd(dqs, dks, dvs, dqg, dkg, dvg, drg, dz, dpre1, w_in_t, wg2_p)
    token = ship_w_in(_atb(dproj, h0, "dw_in"))
    dwg2_p = _atb(glr, dz, "dw_gate_lr2")
    dx, dmeta_blk, dg_in, db_in_ln = _ln_in_bwd(x, meta_ext, dh0, row(ln_in_g), token)

    small = dict(meta_blk=dmeta_blk, ln_in_g=dg_in, ln_in_b=db_in_ln, ln1_g=dg1, ln1_b=db1, ln2_g=dg2, ln2_b=db2,
                 b_in_p=db_in_p, wg2_p=dwg2_p, bg2=dbg2, sinks=dsinks, gn=dgn, loss=loss)
    return dx, small


HBM = pl.BlockSpec(memory_space=pltpu.HBM)


def _place():
    return lax.axis_index("x"), lax.axis_index("y"), lax.axis_index("c")


def _other_chips(x, y):
    return [(1 - x, y), (x, 1 - y), (1 - x, 1 - y)]


def _dma_sems(n):
    return pltpu.SemaphoreType.DMA((n,))


def _comm_params():
    return pltpu.CompilerParams(has_side_effects=True)


SEM = pl.BlockSpec(memory_space=pltpu.SEMAPHORE)


PER_ARRAY = dict(gather=3, scatter=3, sibling=N_CHIPS)


def _ici_copies(kind, landing, srcs, lands, send_sems, recv_sems):
    x, y, c = _place()
    mine = 2 * x + y
    copies = []
    for a in range(len(srcs)):
        if kind == "sibling":
            for s in range(N_CHIPS):
                copies.append(pltpu.make_async_remote_copy(
                    srcs[a].at[s, 1 - c], lands[a].at[s], send_sems.at[N_CHIPS * a + s], recv_sems.at[N_CHIPS * a + s],
                    device_id=(x, y, 1 - c), device_id_type=MESH))
            continue
        for j, (px, py) in enumerate(_other_chips(x, y)):
            slab = 2 * px + py if landing else mine
            if kind == "gather":
                src, dst = srcs[a].at[c], lands[a].at[slab, c]
            else:
                src, dst = srcs[a].at[2 * px + py], lands[a].at[slab]
            copies.append(pltpu.make_async_remote_copy(src, dst, send_sems.at[3 * a + j], recv_sems.at[3 * a + j],
                                                       device_id=(px, py, c), device_id_type=MESH))
    return copies


def _split_params():
    return pltpu.CompilerParams(has_side_effects=pltpu.SideEffectType.DATAFLOW_SIDE_EFFECTING)


def _ici_start(kind, srcs, land_shapes, after, name):
    n = len(srcs)
    lands = [pltpu.with_memory_space_constraint(lax.empty(s, a.dtype), pltpu.HBM) for s, a in zip(land_shapes, srcs)]

    def body(*refs):
        outs = refs[2 * n + len(after):]
        for cp in _ici_copies(kind, False, refs[:n], refs[n:2 * n], outs[0], outs[1]):
            cp.start()
        outs[-1][...] = jnp.zeros(TOKEN, F32)

    outs = pl.pallas_call(
        body, name=name, in_specs=[HBM] * (2 * n) + [pl.BlockSpec(memory_space=pl.ANY)] * len(after),
        out_specs=[SEM, SEM] + [HBM] * (2 * n) + [pl.BlockSpec(memory_space=pltpu.VMEM)],
        out_shape=[_dma_sems(PER_ARRAY[kind] * n)] * 2 + [pltpu.HBM(a.shape, a.dtype) for a in list(srcs) + lands]
        + [jax.ShapeDtypeStruct(TOKEN, F32)],
        input_output_aliases={i: 2 + i for i in range(2 * n)},
        compiler_params=_split_params(),
    )(*_hbm(*srcs), *lands, *after)
    return outs[:-1], outs[-1]


def _ici_wait(kind, handle, after, name):
    n = (len(handle) - 2) // 2

    def body(*refs):
        for cp in _ici_copies(kind, True, refs[:n], refs[n:2 * n], refs[2 * n], refs[2 * n + 1]):
            cp.wait_send()
            cp.wait_recv()

    outs = pl.pallas_call(
        body, name=name, in_specs=[HBM] * (2 * n) + [SEM, SEM] + [pl.BlockSpec(memory_space=pl.ANY)] * len(after),
        out_specs=[HBM] * (2 * n), out_shape=[pltpu.HBM(a.shape, a.dtype) for a in handle[2:]],
        input_output_aliases={i: i for i in range(2 * n)},
        compiler_params=_split_params(),
    )(*handle[2:], handle[0], handle[1], *after)
    return list(outs[:n]), list(outs[n:])


def _forward_copies(landing, arrs, send_sems, recv_sems):
    x, y, c = _place()
    copies = []
    for a in range(len(arrs)):
        for j, (px, py) in enumerate(_other_chips(x, y)):
            half = 1 - c if landing else c
            copies.append(pltpu.make_async_remote_copy(
                arrs[a].at[2 * px + py, c], arrs[a].at[2 * px + py, half], send_sems.at[3 * a + j],
                recv_sems.at[3 * a + j], device_id=(x, y, 1 - c), device_id_type=MESH))
    return copies


def _sibling_forward(lands, name):
    n = len(lands)

    def body(*refs):
        outs = refs[n:2 * n]
        send_sems, recv_sems = refs[2 * n:]
        sends = _forward_copies(False, outs, send_sems, recv_sems)
        for cp in sends:
            cp.start()
        for cp in _forward_copies(True, outs, send_sems, recv_sems):
            cp.wait_recv()
        for cp in sends:
            cp.wait_send()

    return pl.pallas_call(
        body, name=name, in_specs=[HBM] * n, out_specs=[HBM] * n,
        out_shape=[pltpu.HBM(a.shape, a.dtype) for a in lands],
        input_output_aliases={a: a for a in range(n)},
        scratch_shapes=[_dma_sems(3 * n)] * 2,
        compiler_params=_comm_params(),
    )(*_hbm(*lands))


def _forward_start(lands, name):
    n = len(lands)

    def body(*refs):
        outs = refs[n:]
        for cp in _forward_copies(False, refs[:n], outs[0], outs[1]):
            cp.start()
        outs[-1][...] = jnp.zeros(TOKEN, F32)

    outs = pl.pallas_call(
        body, name=name, in_specs=[HBM] * n,
        out_specs=[SEM, SEM] + [HBM] * n + [pl.BlockSpec(memory_space=pltpu.VMEM)],
        out_shape=[_dma_sems(3 * n)] * 2 + [pltpu.HBM(a.shape, a.dtype) for a in lands]
        + [jax.ShapeDtypeStruct(TOKEN, F32)],
        input_output_aliases={i: 2 + i for i in range(n)},
        compiler_params=_split_params(),
    )(*_hbm(*lands))
    return outs[:-1], outs[-1]


def _forward_wait(handle, after, name):
    n = len(handle) - 2

    def body(*refs):
        for cp in _forward_copies(True, refs[:n], refs[n], refs[n + 1]):
            cp.wait_send()
            cp.wait_recv()

    return list(pl.pallas_call(
        body, name=name, in_specs=[HBM] * n + [SEM, SEM] + [pl.BlockSpec(memory_space=pl.ANY)] * len(after),
        out_specs=[HBM] * n, out_shape=[pltpu.HBM(a.shape, a.dtype) for a in handle[2:]],
        input_output_aliases={i: i for i in range(n)},
        compiler_params=_split_params(),
    )(*handle[2:], handle[0], handle[1], *after))


def _sibling_exchange(grads, name):
    n = len(grads)

    def body(*refs):
        ins, outs = refs[:n], refs[n:2 * n]
        send_sems, recv_sems = refs[2 * n:]
        x, y, c = _place()
        copies = []
        for a in range(n):
            for s in range(N_CHIPS):
                cp = pltpu.make_async_remote_copy(ins[a].at[s, 1 - c], outs[a].at[s], send_sems.at[N_CHIPS * a + s],
                                                  recv_sems.at[N_CHIPS * a + s], device_id=(x, y, 1 - c),
                                                  device_id_type=MESH)
                cp.start()
                copies.append(cp)
        for cp in copies:
            cp.wait_recv()
        for cp in copies:
            cp.wait_send()

    return pl.pallas_call(
        body, name=name, in_specs=[HBM] * n, out_specs=[HBM] * n,
        out_shape=[pltpu.HBM((N_CHIPS, g.shape[2], D), F32) for g in grads],
        scratch_shapes=[_dma_sems(N_CHIPS * n)] * 2,
        compiler_params=_comm_params(),
    )(*_hbm(*grads))


def _add_halves(core, grads, recvs, dtypes, name):
    n = len(grads)
    heights = [g.shape[2] for g in grads]

    def body(c_ref, *refs):
        for a in range(n):
            refs[2 * n + a][...] = (refs[2 * a][0] + refs[2 * a + 1][...]).astype(dtypes[a])

    slab = lambda h: pl.BlockSpec((1, h, D), lambda s, c: (s, 0, 0))
    mine = lambda h: pl.BlockSpec((1, 1, h, D), lambda s, c: (s, c[0], 0, 0))
    return pl.pallas_call(
        body, name=name,
        grid_spec=pltpu.PrefetchScalarGridSpec(
            num_scalar_prefetch=1, grid=(N_CHIPS,),
            in_specs=[spec(h) for h in heights for spec in (mine, slab)], out_specs=[slab(h) for h in heights]),
        out_shape=[pltpu.HBM((N_CHIPS, h, D), dt) for h, dt in zip(heights, dtypes)],
        compiler_params=_params(32, dimension_semantics=_seq()),
    )(core, *_hbm(*[a for pair in zip(grads, recvs) for a in pair]))


N_DEVICES = 2 * N_CHIPS
PEER_FLIPS = [(dx, dy, dc) for dx in (0, 1) for dy in (0, 1) for dc in (0, 1)][1:]


def _small_copies(landing, p_ref, out_ref, send_sems, recv_sems):
    x, y, c = _place()
    flip = lambda v, d: 1 - v if d else v
    copies = []
    for k, flips in enumerate(PEER_FLIPS):
        px, py, pc = (flip(v, d) for v, d in zip((x, y, c), flips))
        slab = 4 * px + 2 * py + pc if landing else 4 * x + 2 * y + c
        copies.append(pltpu.make_async_remote_copy(p_ref, out_ref.at[slab], send_sems.at[k], recv_sems.at[k],
                                                   device_id=(px, py, pc), device_id_type=MESH))
    return copies


def _small_start(pack, after):
    n = len(PEER_FLIPS)
    land = pltpu.with_memory_space_constraint(lax.empty((N_DEVICES,) + pack.shape, F32), pltpu.HBM)

    def body(p_ref, land_ref, *refs):
        outs = refs[len(after):]
        for cp in _small_copies(False, p_ref, land_ref, outs[0], outs[1]):
            cp.start()
        outs[-1][...] = jnp.zeros(TOKEN, F32)

    outs = pl.pallas_call(
        body, name="small_exchange_start", in_specs=[HBM, HBM] + [pl.BlockSpec(memory_space=pl.ANY)] * len(after),
        out_specs=[SEM, SEM, HBM, HBM, pl.BlockSpec(memory_space=pltpu.VMEM)],
        out_shape=[_dma_sems(n), _dma_sems(n), pltpu.HBM(pack.shape, F32), pltpu.HBM(land.shape, F32),
                   jax.ShapeDtypeStruct(TOKEN, F32)],
        input_output_aliases={0: 2, 1: 3},
        compiler_params=_split_params(),
    )(*_hbm(pack), land, *after)
    return outs[:-1], outs[-1]


def _small_wait(handle, after):
    def body(p_ref, land_ref, send_sems, recv_sems, *rest):
        for cp in _small_copies(True, p_ref, land_ref, send_sems, recv_sems):
            cp.wait_send()
            cp.wait_recv()

    return pl.pallas_call(
        body, name="small_exchange_wait", in_specs=[HBM, HBM, SEM, SEM] + [pl.BlockSpec(memory_space=pl.ANY)] * len(after),
        out_specs=[HBM, HBM], out_shape=[pltpu.HBM(a.shape, F32) for a in handle[2:]],
        input_output_aliases={0: 0, 1: 1},
        compiler_params=_split_params(),
    )(handle[2], handle[3], handle[0], handle[1], *after)


def _sum_chips(slots, firsts, rests, after):
    n = len(firsts)

    def body(i_ref, *refs):
        outs = refs[4 * n + len(after):]
        for a in range(n):
            first, r1, r2, r3 = refs[4 * a:4 * a + 4]
            outs[a][...] = ((first[...].astype(F32) + r1[...].astype(F32)) + r2[...].astype(F32)) + r3[...].astype(F32)

    slab = lambda h, k: pl.BlockSpec((1, h, D), lambda i, ix: (ix[k], 0, 0))
    heights = [f.shape[1] for f in firsts]
    return pl.pallas_call(
        body, name="sum_chips",
        grid_spec=pltpu.PrefetchScalarGridSpec(
            num_scalar_prefetch=1, grid=(1,),
            in_specs=[slab(h, k) for h in heights for k in range(4)] + [pl.BlockSpec(memory_space=pl.ANY)] * len(after),
            out_specs=[slab(h, 4) for h in heights]),
        out_shape=[pltpu.HBM((2, h, D), F32) for h in heights],
        compiler_params=_params(48, dimension_semantics=_seq()),
    )(slots, *_hbm(*[a for f, r in zip(firsts, rests) for a in (f, r, r, r)]), *after)


def _join_halves(halves):
    n = len(halves)

    def body(*refs):
        outs = refs[n:2 * n]
        send_sems, recv_sems = refs[2 * n:]
        x, y, c = _place()

        def copy(a, slab):
            return pltpu.make_async_remote_copy(outs[a].at[slab], outs[a].at[slab], send_sems.at[a], recv_sems.at[a],
                                                device_id=(x, y, 1 - c), device_id_type=MESH)

        for a in range(n):
            copy(a, c).start()
        for a in range(n):
            copy(a, 1 - c).wait_recv()
        for a in range(n):
            copy(a, c).wait_send()

    return pl.pallas_call(
        body, name="join_halves", in_specs=[HBM] * n, out_specs=[HBM] * n,
        out_shape=[pltpu.HBM(h.shape, F32) for h in halves],
        input_output_aliases={a: a for a in range(n)},
        scratch_shapes=[_dma_sems(n)] * 2,
        compiler_params=_comm_params(),
    )(*_hbm(*halves))


def _chip_partials(grads, wire_dtypes, names, fetched=()):
    core = lax.axis_index("c").astype(jnp.int32).reshape(1)
    todo = len(grads) - len(fetched)
    recv = list(_sibling_exchange(grads[:todo], "sibling_exchange_" + names[0])) + list(fetched)
    return list(_add_halves(core, grads, recv, wire_dtypes, "add_halves_" + names[0]))


def _finish_reduce(parts, got, after):
    x, y, c = _place()
    others = [2 * px + py for px, py in _other_chips(x, y)]
    own_first = jnp.stack([2 * x + y] + others + [c]).astype(jnp.int32)
    return [f.reshape(2 * f.shape[1], D) for f in _join_halves(_sum_chips(own_first, parts, got, after))]


ADAMW_STEPS = 8


def _adamw(params, chip, window_step):
    n = len(params)

    def block(shape, by_rows):
        rows, cols = shape
        if by_rows:
            return pl.BlockSpec((rows // ADAMW_STEPS, cols), lambda i, c: (i, 0))
        assert cols % (LANE * ADAMW_STEPS) == 0
        return pl.BlockSpec((rows, cols // ADAMW_STEPS), lambda i, c: (0, i))

    by_rows = [w.shape[0] % (8 * ADAMW_STEPS) == 0 and g.shape == w.shape for w, g, _, _ in params]

    def body(c_ref, *refs):
        for a in range(n):
            w_ref, g_ref, m_ref, v_ref = refs[4 * a:4 * a + 4]
            outs = refs[4 * n + 4 * a:4 * n + 4 * a + 4]
            rows = w_ref.shape[0]
            g = g_ref[0:rows]
            if g_ref.shape[0] != rows:
                for s in range(1, N_CHIPS):
                    g = jnp.where(c_ref[0] == s, g_ref[s * window_step:s * window_step + rows], g)
            outs[0][...] = g
            outs[1][...], outs[2][...], outs[3][...] = _adamw_math(w_ref[...], g, m_ref[...], v_ref[...])

    outs = pl.pallas_call(
        body, name="adamw_matrices",
        grid_spec=pltpu.PrefetchScalarGridSpec(
            num_scalar_prefetch=1, grid=(ADAMW_STEPS,),
            in_specs=[block(a.shape, r) for p, r in zip(params, by_rows) for a in p],
            out_specs=[block(p[0].shape, r) for p, r in zip(params, by_rows) for _ in range(4)]),
        out_shape=[pltpu.HBM(p[0].shape, F32) for p in params for _ in range(4)],
        compiler_params=_params(48, dimension_semantics=_seq()),
    )(chip, *_hbm(*[a for p in params for a in p]))
    return [outs[4 * a:4 * a + 4] for a in range(n)]


def _adamw_math(w, g, m, v):
    nm = ADAM_B1 * m + (1.0 - ADAM_B1) * g
    nv = ADAM_B2 * v + (1.0 - ADAM_B2) * (g * g)
    m_hat = nm / (1.0 - ADAM_B1 ** ADAM_STEP)
    v_hat = nv / (1.0 - ADAM_B2 ** ADAM_STEP)
    return -ADAM_LR * (m_hat / (jnp.sqrt(v_hat) + ADAM_EPS) + ADAM_WD * w), nm, nv


SMALL = (("meta_tokens", (N_META, D // N_CHIPS)), ("ln_in_g", (1, D)), ("ln_in_b", (1, D)), ("b_in", (1, D_IN)),
         ("w_gate_lr2", (GATE_RANK, GLA_HEADS * DK // N_CHIPS)), ("b_gate_lr2", (1, GLA_HEADS * DK)),
         ("attn_sinks", (1, SWA_HEADS)),
         ("gla_norm_g", (1, DV)), ("ln1_g", (1, D)), ("ln1_b", (1, D)), ("ln2_g", (1, D)), ("ln2_b", (1, D)))
ROW_META, ROW_B_IN, ROW_TAIL, ROW_WG2 = 0, 22, 25, 32
ROW_LN = dict(ln_in_g=16, ln_in_b=17, ln1_g=18, ln1_b=19, ln2_g=20, ln2_b=21)
TAIL_BG2, TAIL_SINKS, TAIL_GN, TAIL_LOSS = 0, 256, 256 + SWA_HEADS, 256 + SWA_HEADS + DV


def _adamw_small(place, packs, own, params):
    n = len(SMALL)

    def body(place_ref, packs_ref, own_ref, *refs):
        ins, outs, p_ref = refs[:3 * n], refs[3 * n:-1], refs[-1]
        me, c = place_ref[0], place_ref[1]
        total = jnp.where(me == 0, own_ref[...], packs_ref[0])
        for i in range(1, N_DEVICES):
            total = total + jnp.where(me == i, own_ref[...], packs_ref[i])
        p_ref[...] = total
        outs[4 * n][...] = total[ROW_TAIL:ROW_TAIL + 1, :]

        def mine(width, rows):
            part = lambda s: p_ref[rows, s * width:(s + 1) * width]
            return jnp.where(c == 0, part(0), jnp.where(c == 1, part(1), jnp.where(c == 2, part(2), part(3))))

        tail = lambda lo, width: p_ref[ROW_TAIL:ROW_TAIL + 1, lo:lo + width]
        grads = dict(
            meta_tokens=mine(D // N_CHIPS, slice(ROW_META, ROW_META + N_META)),
            b_in=jnp.concatenate([p_ref[ROW_B_IN:ROW_B_IN + 1, :], p_ref[ROW_B_IN + 1:ROW_B_IN + 2, :],
                                  p_ref[ROW_B_IN + 2:ROW_B_IN + 3, 0:D_IN - 2 * D]], axis=1),
            w_gate_lr2=mine(256 // N_CHIPS, slice(ROW_WG2, ROW_WG2 + 16)),
            b_gate_lr2=tail(TAIL_BG2, 256), attn_sinks=tail(TAIL_SINKS, SWA_HEADS), gla_norm_g=tail(TAIL_GN, DV),
            **{k: p_ref[r:r + 1, :] for k, r in ROW_LN.items()})
        for i, (name, _) in enumerate(SMALL):
            g = grads[name]
            outs[4 * i][...] = g
            outs[4 * i + 1][...], outs[4 * i + 2][...], outs[4 * i + 3][...] = _adamw_math(
                ins[3 * i][...], g, ins[3 * i + 1][...], ins[3 * i + 2][...])

    whole = lambda shape: pl.BlockSpec(shape, lambda i, c: (0,) * len(shape))
    outs = pl.pallas_call(
        body, name="adamw_small",
        grid_spec=pltpu.PrefetchScalarGridSpec(
            num_scalar_prefetch=1, grid=(1,),
            in_specs=[whole(packs.shape), whole(own.shape)] + [whole(s) for _, s in SMALL for _ in range(3)],
            out_specs=[whole(s) for _, s in SMALL for _ in range(4)] + [whole((1, D))],
            scratch_shapes=[pltpu.VMEM(own.shape, F32)]),
        out_shape=[pltpu.HBM(s, F32) for _, s in SMALL for _ in range(4)] + [pltpu.HBM((1, D), F32)],
        compiler_params=_params(16, dimension_semantics=_seq()),
    )(place, *_hbm(packs, own, *[a for p in params for a in p]))
    return [outs[4 * i:4 * i + 4] for i in range(n)], outs[4 * n]


def _small_pack(gr):
    names = ["meta_blk"] + list(ROW_LN) + ["b_in_p", "wg2_p", "bg2", "sinks", "gn", "loss"]
    gate_w = GLA_HEADS * DK

    def body(*refs):
        src, out = dict(zip(names, refs)), refs[-1]
        out[...] = jnp.zeros_like(out)
        out[ROW_META:ROW_META + N_META, :] = src["meta_blk"][META_OFF:CH, :]
        for k, r in ROW_LN.items():
            out[r:r + 1, :] = src[k][...]
        for j in range(-(-D_IN // D)):
            width = min(D, D_IN - j * D)
            out[ROW_B_IN + j:ROW_B_IN + j + 1, 0:width] = src["b_in_p"][:, j * D:j * D + width]
        tail = slice(ROW_TAIL, ROW_TAIL + 1)
        out[tail, TAIL_BG2:TAIL_BG2 + gate_w] = src["bg2"][...]
        out[tail, TAIL_SINKS:TAIL_SINKS + SWA_HEADS] = src["sinks"][:, 0:SWA_HEADS]
        out[tail, TAIL_GN:TAIL_GN + DV] = src["gn"][...]
        out[tail, TAIL_LOSS:TAIL_LOSS + 1] = src["loss"][:, 0:1]
        out[ROW_WG2:ROW_WG2 + GATE_RANK, 0:gate_w] = src["wg2_p"][0:GATE_RANK, :]

    arrays = [gr[k] for k in names]
    return pl.pallas_call(
        body, name="small_pack", grid=(1,),
        in_specs=[_acc(a.shape) for a in arrays], out_specs=_acc((SMALL_ROWS, D)),
        out_shape=pltpu.HBM((SMALL_ROWS, D), F32),
        compiler_params=_params(16, dimension_semantics=_seq()),
    )(*_hbm(*arrays))


BIG = ("w_in", "w_out", "w_g", "w_u", "w_d")


def kernel(x, meta_tokens, ln_in_g, ln_in_b, w_in, b_in, w_gate_lr2, b_gate_lr2, attn_sinks, gla_norm_g, w_out, ln1_g, ln1_b, w_ffn_gate, w_ffn_up, w_ffn_down, ln2_g, ln2_b, loss_target, m_meta_tokens, m_ln_in_g, m_ln_in_b, m_w_in, m_b_in, m_w_gate_lr2, m_b_gate_lr2, m_attn_sinks, m_gla_norm_g, m_w_out, m_ln1_g, m_ln1_b, m_w_ffn_gate, m_w_ffn_up, m_w_ffn_down, m_ln2_g, m_ln2_b, v_meta_tokens, v_ln_in_g, v_ln_in_b, v_w_in, v_b_in, v_w_gate_lr2, v_b_gate_lr2, v_attn_sinks, v_gla_norm_g, v_w_out, v_ln1_g, v_ln1_b, v_w_ffn_gate, v_w_ffn_up, v_w_ffn_down, v_ln2_g, v_ln2_b):
    chip = 2 * lax.axis_index("x") + lax.axis_index("y")

    halves = lambda a: a.reshape(2, a.shape[0] // 2, a.shape[1])
    r_in = SHARD_ROWS["w_in"]
    first = [halves(a) for a in (jnp.pad(w_in[0].T.astype(BF16), ((0, W_IN_WIN - r_in), (0, 0))), meta_tokens,
                                 w_gate_lr2[0])]
    rest = [halves(a) for a in (w_out[0].astype(BF16), w_ffn_gate[0].T.astype(BF16), w_ffn_up[0].T.astype(BF16),
                                w_ffn_down[0].astype(BF16))]
    lands = lambda arrs: [(N_CHIPS,) + a.shape for a in arrs]
    first_handle, first_token = _ici_start("gather", first, lands(first), [], "gather_first_start")
    rest_handle, token = _ici_start("gather", rest, lands(rest), [first_token], "gather_rest_start")
    own_slab = lambda got, shards: [lax.dynamic_update_index_in_dim(g, s, chip, axis=0) for g, s in zip(got, shards)]
    fetching = {}

    def fetch_first(after):
        shards, landed = _ici_wait("gather", first_handle, after, "gather_first_wait")
        g_in, g_meta, g_wg2 = own_slab(_sibling_forward(landed, "gather_first_forward"), shards)
        w_in_t = jnp.pad(g_in.reshape(N_CHIPS, W_IN_WIN, D)[:, :r_in].reshape(D_IN, D), ((0, D_IN_P - D_IN), (0, 0)))
        meta_full = jnp.concatenate([g_meta[s].reshape(N_META, -1) for s in range(N_CHIPS)], axis=1)
        wg2_full = jnp.concatenate([g_wg2[s].reshape(w_gate_lr2.shape[1], -1) for s in range(N_CHIPS)], axis=1)
        return w_in_t, meta_full, wg2_full

    def fetch_rest(after):
        shards, landed = _ici_wait("gather", rest_handle, after, "gather_rest_wait")
        g_out, = own_slab(_sibling_forward(landed[:1], "gather_w_out_forward"), shards[:1])
        fetching["shards"] = shards[1:]
        fetching["handle"], forward_token = _forward_start(landed[1:], "gather_ffn_forward_start")
        return g_out.reshape(-1, D), forward_token

    def fetch_ffn(after):
        got = _forward_wait(fetching["handle"], after, "gather_ffn_forward_wait")
        return [g.reshape(-1, D) for g in own_slab(got, fetching["shards"])]

    sent = {}
    split = lambda grads: [g.reshape(N_CHIPS, 2, -1, D) for g in grads]

    def ship(key, grads, names, fetched=()):
        parts = _chip_partials(grads, [BF16] * len(grads), names, fetched)
        sent[key], ship_token = _ici_start("scatter", parts, [p.shape for p in parts], [], "scatter_" + key + "_start")
        return ship_token

    def exchange_ffn(g):
        grads = split([g[k] for k in BIG[2:]])
        sent["ffn_halves"], exchange_token = _ici_start("sibling", grads, [(N_CHIPS,) + a.shape[2:] for a in grads], [],
                                                        "sibling_ffn_start")
        return exchange_token

    def ship_ffn(dw_out):
        grads, fetched = _ici_wait("sibling", sent["ffn_halves"], [dw_out], "sibling_ffn_wait")
        return ship("ffn", split([dw_out]) + grads, list(BIG[1:]), fetched)

    def ship_w_in(dw_in_t):
        win_start = [s * r_in // BF16_ROWS * BF16_ROWS for s in range(N_CHIPS)]
        return ship("w_in", split([jnp.stack([dw_in_t[st:st + W_IN_WIN] for st in win_start])]), ["w_in"])

    dx, gr = _local_step(
        x[0], loss_target[0], ln_in_g, ln_in_b, b_in[0], b_gate_lr2[0], attn_sinks[0], gla_norm_g[0], ln1_g[0],
        ln1_b[0], ln2_g[0], ln2_b[0], token, fetch_first, fetch_rest, fetch_ffn, exchange_ffn, ship_ffn, ship_w_in)
    ffn_parts, ffn_got = _ici_wait("scatter", sent["ffn"], [dx], "scatter_ffn_wait")
    w_in_parts, w_in_got = _ici_wait("scatter", sent["w_in"], [dx], "scatter_w_in_wait")

    small_handle, token = _small_start(_small_pack(gr), [w_in_got[0]])
    red = _finish_reduce(w_in_parts + ffn_parts, w_in_got + ffn_got, [token])

    big_g = dict(zip(BIG, red))
    weights = dict(meta_tokens=meta_tokens, ln_in_g=ln_in_g, ln_in_b=ln_in_b, w_in=w_in, b_in=b_in,
                   w_gate_lr2=w_gate_lr2, b_gate_lr2=b_gate_lr2, attn_sinks=attn_sinks, gla_norm_g=gla_norm_g,
                   w_out=w_out, ln1_g=ln1_g, ln1_b=ln1_b, w_ffn_gate=w_ffn_gate, w_ffn_up=w_ffn_up,
                   w_ffn_down=w_ffn_down, ln2_g=ln2_g, ln2_b=ln2_b)
    m_in = dict(meta_tokens=m_meta_tokens, ln_in_g=m_ln_in_g, ln_in_b=m_ln_in_b, w_in=m_w_in, b_in=m_b_in,
                w_gate_lr2=m_w_gate_lr2, b_gate_lr2=m_b_gate_lr2, attn_sinks=m_attn_sinks, gla_norm_g=m_gla_norm_g,
                w_out=m_w_out, ln1_g=m_ln1_g, ln1_b=m_ln1_b, w_ffn_gate=m_w_ffn_gate, w_ffn_up=m_w_ffn_up,
                w_ffn_down=m_w_ffn_down, ln2_g=m_ln2_g, ln2_b=m_ln2_b)
    v_in = dict(meta_tokens=v_meta_tokens, ln_in_g=v_ln_in_g, ln_in_b=v_ln_in_b, w_in=v_w_in, b_in=v_b_in,
                w_gate_lr2=v_w_gate_lr2, b_gate_lr2=v_b_gate_lr2, attn_sinks=v_attn_sinks, gla_norm_g=v_gla_norm_g,
                w_out=v_w_out, ln1_g=v_ln1_g, ln1_b=v_ln1_b, w_ffn_gate=v_w_ffn_gate, w_ffn_up=v_w_ffn_up,
                w_ffn_down=v_w_ffn_down, ln2_g=v_ln2_g, ln2_b=v_ln2_b)
    names = list(weights)
    big_names = ("w_in", "w_out", "w_ffn_gate", "w_ffn_up", "w_ffn_down")

    grads, delta, new_m, new_v = {}, {}, {}, {}
    flips = [(lambda a: a.T) if kk in ("w_in", "w_g", "w_u") else (lambda a: a) for kk in BIG]
    updated = _adamw([(flip(weights[k][0]), big_g[kk], flip(m_in[k][0]), flip(v_in[k][0]))
                      for k, kk, flip in zip(big_names, BIG, flips)],
                     chip.astype(jnp.int32).reshape(1), r_in % BF16_ROWS)
    for k, flip, results in zip(big_names, flips, updated):
        grads[k], delta[k], new_m[k], new_v[k] = (flip(t)[None] for t in results)
    small_in = [tuple(src[k].reshape(shape) for src in (weights, m_in, v_in)) for k, shape in SMALL]
    place = jnp.stack([2 * chip + lax.axis_index("c"), chip]).astype(jnp.int32)
    small_own, small_all = _small_wait(small_handle, [updated[0][0]])
    small_out, tail_row = _adamw_small(place, small_all, small_own, small_in)
    for (k, _), results in zip(SMALL, small_out):
        grads[k], delta[k], new_m[k], new_v[k] = (r.reshape(weights[k].shape) for r in results)

    return (tail_row[0, TAIL_LOSS], dx[None], *[grads[k] for k in names], *[delta[k] for k in names], *[new_m[k] for k in names],
            *[new_v[k] for k in names])
```

```python
import jax
import jax.numpy as jnp
from jax import lax
from jax.experimental import pallas as pl
from jax.experimental.pallas import tpu as pltpu

F32 = jnp.float32
BF16 = jnp.bfloat16
MESH = pl.DeviceIdType.MESH

D = 1024
SEQ = 4096
N_META = 16
SWA_HEADS, SWA_KV_HEADS, DH = 8, 2, 64
WINDOW = 128
GLA_HEADS, DK, DV = 4, 64, 128
GLA_TAU = 16.0
CH = 64
D_FF = 2816
D_IN = 2320
LN_EPS = 1e-5
RMS_EPS = 1e-6
ALPHA = 2.0 ** 0.25
NEG = -1e30
ADAM_LR, ADAM_B1, ADAM_B2, ADAM_EPS, ADAM_WD, ADAM_STEP = 0.001, 0.9, 0.999, 1e-8, 0.01, 10
O_QS, O_KS, O_VS, O_QG, O_KG, O_VG, O_RG, O_LR = 0, 512, 640, 768, 1024, 1280, 1792, 2304

LANE = 128
BLK = WINDOW
GATE_RANK = 16
D_IN_P = D_IN + LANE - GATE_RANK
META_OFF = CH - N_META
HEAD_POS = (0, 4, 1, 5, 2, 6, 3, 7)
LN_ROWS = 512
TOKEN = (8, LANE)
N_CHIPS = 4
SHARD_ROWS = dict(w_in=D_IN // N_CHIPS, w_out=D // N_CHIPS, w_g=D_FF // N_CHIPS, w_u=D_FF // N_CHIPS,
                  w_d=D_FF // N_CHIPS)
SMALL_ROWS = 48
BF16_ROWS = 16
W_IN_WIN = -(-SHARD_ROWS["w_in"] // (2 * BF16_ROWS)) * 2 * BF16_ROWS
W_IN_STARTS = tuple(s * SHARD_ROWS["w_in"] // BF16_ROWS * BF16_ROWS for s in range(N_CHIPS))
VMEM_CAP_MB = 64
VMEM_SPARE_MB = 6


def _lp():
    return SEQ + BLK


def _row_tile(cap):
    lp = _lp()
    return max(t for t in range(16, cap + 1, 16) if lp % t == 0)


def _params(vmem_mb, **kw):
    assert vmem_mb <= VMEM_CAP_MB - VMEM_SPARE_MB
    return pltpu.CompilerParams(vmem_limit_bytes=vmem_mb << 20, **kw)


def _seq(n=1):
    return ("arbitrary",) * n


def _const(shape):
    return pl.BlockSpec(shape, lambda *_: (0,) * len(shape), pipeline_mode=pl.Buffered(1))


def _acc(shape):
    return pl.BlockSpec(shape, lambda *_: (0,) * len(shape))


def _rows(tm, width):
    return pl.BlockSpec((tm, width), lambda i: (i, 0))


def _dot(a, b):
    return jnp.dot(a.astype(BF16), b.astype(BF16), preferred_element_type=F32)


def _dot_nt(a, b):
    return lax.dot_general(a.astype(BF16), b.astype(BF16), (((1,), (1,)), ((), ())), preferred_element_type=F32)


def _dot_tn(a, b):
    return lax.dot_general(a.astype(BF16), b.astype(BF16), (((0,), (0,)), ((), ())), preferred_element_type=F32)


def _dot_exact(a, b):
    return jnp.dot(a, b, precision=lax.Precision.HIGHEST, preferred_element_type=F32)


def _ln_stats(x):
    mu = jnp.mean(x, axis=-1, keepdims=True)
    xc = x - mu
    rstd = lax.rsqrt(jnp.mean(xc * xc, axis=-1, keepdims=True) + LN_EPS)
    return xc * rstd, rstd


def _ln_bwd(dy, xhat, rstd, g):
    dxh = dy * g
    return rstd * (dxh - jnp.mean(dxh, axis=-1, keepdims=True) - xhat * jnp.mean(dxh * xhat, axis=-1, keepdims=True))


def _sigmoid(x):
    return 1.0 / (1.0 + jnp.exp(-x))


def _iota(shape, dim):
    return lax.broadcasted_iota(jnp.int32, shape, dim)


def _hbm(*arrays):
    return tuple(pltpu.with_memory_space_constraint(a, pltpu.HBM) for a in arrays)


def _ln_in_fwd_real(x, g, b, token):
    tr = min(LN_ROWS, SEQ)

    def body(x_ref, g_ref, b_ref, token_ref, h_ref):
        xhat, _ = _ln_stats(x_ref[...])
        h_ref[...] = xhat * g_ref[...] + b_ref[...]

    return pl.pallas_call(
        body, name="ln_in_fwd", grid=(SEQ // tr,),
        in_specs=[_rows(tr, D), _const((1, D)), _const((1, D)), _const(TOKEN)],
        out_specs=_rows(tr, D),
        out_shape=pltpu.HBM((_lp(), D), F32),
        compiler_params=_params(32, dimension_semantics=_seq()),
    )(*_hbm(x, g, b), token)


def _ln_in_fwd_meta(h_real, meta_ext, g, b):
    def meta_body(m_ref, g_ref, b_ref, real_ref, h_ref):
        xhat, _ = _ln_stats(m_ref[...])
        h_ref[...] = xhat * g_ref[...] + b_ref[...]

    return pl.pallas_call(
        meta_body, name="ln_in_fwd_meta", grid=(1,),
        in_specs=[_const((BLK, D)), _const((1, D)), _const((1, D)), pl.BlockSpec(memory_space=pl.ANY)],
        out_specs=pl.BlockSpec((BLK, D), lambda i: (SEQ // BLK, 0)),
        out_shape=pltpu.HBM((_lp(), D), F32),
        input_output_aliases={3: 0},
        compiler_params=_params(16, dimension_semantics=_seq()),
    )(*_hbm(meta_ext, g, b, h_real))


def _in_proj(h0, w_in_windows, b_in_p, wg2_p, bg2):
    tm = _row_tile(384)
    lp = _lp()
    widths = (512, 128, 128, 256, 256, 512, 512, 128)
    offs = (O_QS, O_KS, O_VS, O_QG, O_KG, O_VG, O_RG, O_LR)
    shard = SHARD_ROWS["w_in"]

    def body(h_ref, win_ref, b_ref, wg2_ref, bg2_ref, *outs):
        w_ref = outs[9]

        @pl.when(pl.program_id(0) == 0)
        def _():
            for s in range(N_CHIPS):
                w_ref[shard * s:shard * (s + 1), :] = win_ref[s, 0:shard, :]
            w_ref[D_IN:D_IN_P, :] = jnp.zeros((D_IN_P - D_IN, D), BF16)

        proj = _dot_nt(h_ref[...], w_ref[...]) + b_ref[...]
        for pos, h in enumerate(HEAD_POS):
            outs[0][:, pos * DH:(pos + 1) * DH] = proj[:, O_QS + h * DH:O_QS + (h + 1) * DH]
        for o_ref, off, wd in zip(outs[1:8], offs[1:], widths[1:]):
            o_ref[...] = proj[:, off:off + wd]
        outs[8][...] = _dot(proj[:, O_LR:O_LR + LANE], wg2_ref[...]) + bg2_ref[...]

    return pl.pallas_call(
        body, name="in_proj", grid=(lp // tm,),
        in_specs=[_rows(tm, D), _const(w_in_windows.shape), _const((1, D_IN_P)), _const((LANE, 256)), _const((1, 256))],
        out_specs=[_rows(tm, w) for w in widths] + [_rows(tm, 256), _acc((D_IN_P, D))],
        out_shape=[pltpu.HBM((lp, w), F32) for w in widths] + [pltpu.HBM((lp, 256), F32), pltpu.HBM((D_IN_P, D), BF16)],
        compiler_params=_params(48, dimension_semantics=_seq()),
    )(*_hbm(h0, w_in_windows, b_in_p, wg2_p, bg2))


def _swa_masks(n):
    nb = SEQ // BLK
    is_meta = n == nb
    ri = _iota((BLK, BLK), 0)
    cj = _iota((BLK, BLK), 1)
    meta_col = ((cj >= META_OFF) & (cj < CH)).astype(jnp.int32)
    meta_q = meta_col * ((cj <= ri) & (ri < CH)).astype(jnp.int32)
    valid_m = jnp.where(is_meta, meta_q, meta_col) > 0
    dist_m = jnp.where(is_meta, ri - cj, n * BLK + ri + CH - cj).astype(F32)
    valid_p = jnp.where((n >= 1) & (n < nb), (cj > ri).astype(jnp.int32), 0) > 0
    dist_p = (ri + BLK - cj).astype(F32)
    valid_c = jnp.where(n < nb, (cj <= ri).astype(jnp.int32), 0) > 0
    dist_c = (ri - cj).astype(F32)
    return (dist_m, dist_p, dist_c), (valid_m, valid_p, valid_c)


def _swa_bias(n):
    dists, valids = _swa_masks(n)
    return (jnp.concatenate([-d for d in dists], axis=1),
            jnp.concatenate([jnp.where(v, 0.0, NEG) for v in valids], axis=1))


def _swa_half(ref, pos, scale=1.0):
    col = ref[:, (pos // 2) * LANE:(pos // 2 + 1) * LANE]
    lane = _iota((BLK, LANE), 1)
    mine = lane < DH if pos % 2 == 0 else lane >= DH
    return jnp.where(mine, col * scale, 0.0).astype(BF16)


def _swa_merge(even, odd):
    return jnp.where(_iota((BLK, LANE), 1) < DH, even, odd)


def _swa_softmax(t, sink):
    m = jnp.maximum(jnp.max(t, axis=-1, keepdims=True), sink)
    e = jnp.exp(t - m)
    e_sink = jnp.exp(sink - m)
    inv = 1.0 / (jnp.sum(e, axis=-1, keepdims=True) + e_sink)
    return e * inv, e_sink * inv


def _swa_kv_specs(width):
    nb = SEQ // BLK
    return [pl.BlockSpec((BLK, width), lambda n: (nb, 0)),
            pl.BlockSpec((BLK, width), lambda n: (jnp.clip(n - 1, 0, nb - 1), 0)),
            pl.BlockSpec((BLK, width), lambda n: (jnp.minimum(n, nb), 0))]


def _swa_fwd(sinks, qs, ks, vs):
    nb = SEQ // BLK
    heads = range(SWA_HEADS)

    def body(sink_ref, q_ref, km_ref, kp_ref, kc_ref, vm_ref, vp_ref, vc_ref, o_ref):
        negdist, maskbias = _swa_bias(pl.program_id(0))
        k_all = jnp.concatenate([km_ref[...], kp_ref[...], kc_ref[...]], axis=0).astype(BF16)
        v_all = jnp.concatenate([vm_ref[...], vp_ref[...], vc_ref[...]], axis=0).astype(BF16)
        q = [_swa_half(q_ref, pos, DH ** -0.5) for pos in heads]
        t = [_dot_nt(q[pos], k_all) + (2.0 ** -(HEAD_POS[pos] + 1) * negdist + maskbias) for pos in heads]
        p = [_swa_softmax(t[pos], sink_ref[HEAD_POS[pos]])[0].astype(BF16) for pos in heads]
        o = [_dot(p[pos], v_all) for pos in heads]
        for col in range(SWA_HEADS // 2):
            o_ref[:, col * LANE:(col + 1) * LANE] = _swa_merge(o[2 * col], o[2 * col + 1])

    kvw = SWA_KV_HEADS * DH
    return pl.pallas_call(
        body, name="swa_fwd", grid=(nb + 1,),
        in_specs=[pl.BlockSpec(memory_space=pltpu.SMEM), _rows(BLK, SWA_HEADS * DH)] + _swa_kv_specs(kvw) + _swa_kv_specs(kvw),
        out_specs=_rows(BLK, SWA_HEADS * DH),
        out_shape=pltpu.HBM((_lp(), SWA_HEADS * DH), F32),
        compiler_params=_params(16, dimension_semantics=_seq()),
    )(sinks, *_hbm(qs, ks, ks, ks, vs, vs, vs))


GLA_PER_STEP = BLK // CH


def _gla_block(s):
    nb = SEQ // BLK
    return jnp.where(s == 0, nb, s - 1)


def _gla_rowmask(s):
    ri = _iota((BLK, 1), 0)
    m = jnp.where(s == 0, ((ri >= META_OFF) & (ri < CH)).astype(jnp.int32), 1)
    return (m > 0).astype(F32) + jnp.zeros((BLK, 1), F32)


def _gla_chunk_masks():
    r, c = _iota((BLK, BLK), 0), _iota((BLK, BLK), 1)
    same = ((r < CH) & (c < CH)) | ((r >= CH) & (c >= CH))
    return same & (r >= c), same & (r <= c), same


def _gla_decay(z, rmask):
    log_g = (jnp.minimum(z, 0.0) - jnp.log1p(jnp.exp(-jnp.abs(z)))) * (rmask / GLA_TAU)
    lower, _, same = _gla_chunk_masks()
    return _dot_exact(lower.astype(F32), log_g), _dot_exact(same.astype(F32), log_g)


def _gla_slices(c, h):
    return slice(c * CH, (c + 1) * CH), slice(h * DK, (h + 1) * DK), slice(h * DV, (h + 1) * DV)


def _gla_fwd(qg, kg, vg, z):
    steps = SEQ // BLK + 1
    kw, vw = GLA_HEADS * DK, GLA_HEADS * DV
    pairs = [(c, h) for c in range(GLA_PER_STEP) for h in range(GLA_HEADS)]

    def body(q_ref, k_ref, v_ref, z_ref, o_ref, st_ref, st):
        s = pl.program_id(0)

        @pl.when(s == 0)
        def _():
            st[...] = jnp.zeros_like(st)

        rmask = _gla_rowmask(s)
        b, b_last = _gla_decay(z_ref[...], rmask)
        q = q_ref[...] * (rmask * DK ** -0.5)
        k = k_ref[...] * rmask
        v = v_ref[...] * rmask
        qe = q * jnp.exp(b)
        ke = k * jnp.exp(-b)
        kd = k * jnp.exp(b_last - b)
        e_last = jnp.exp(b_last)
        causal = _iota((CH, CH), 0) >= _iota((CH, CH), 1)
        a, upd, intra = {}, {}, {}
        for c, h in pairs:
            rows, ks, vs_ = _gla_slices(c, h)
            a[c, h] = jnp.where(causal, _dot_nt(qe[rows, ks], ke[rows, ks]), 0.0)
            upd[c, h] = _dot_tn(v[rows, vs_], kd[rows, ks])
        for c, h in pairs:
            rows, ks, vs_ = _gla_slices(c, h)
            intra[c, h] = _dot(a[c, h], v[rows, vs_])
        state = st[...]
        for c in range(GLA_PER_STEP):
            st_ref[0, c] = state
            for h in range(GLA_HEADS):
                rows, ks, vs_ = _gla_slices(c, h)
                o_ref[rows, vs_] = intra[c, h] + _dot_nt(qe[rows, ks], state[:, ks])
            state = state * e_last[c * CH:c * CH + 1] + jnp.concatenate([upd[c, h] for h in range(GLA_HEADS)], axis=1)
        st[...] = state

    blk = lambda w: pl.BlockSpec((BLK, w), lambda s: (_gla_block(s), 0))
    return pl.pallas_call(
        body, name="gla_fwd", grid=(steps,),
        in_specs=[blk(kw), blk(kw), blk(vw), blk(kw)],
        out_specs=[blk(vw), pl.BlockSpec((1, GLA_PER_STEP, DV, kw), lambda s: (s, 0, 0, 0))],
        out_shape=[pltpu.HBM((_lp(), vw), F32), pltpu.HBM((steps, GLA_PER_STEP, DV, kw), F32)],
        scratch_shapes=[pltpu.VMEM((DV, kw), F32)],
        compiler_params=_params(16, dimension_semantics=_seq()),
    )(*_hbm(qg, kg, vg, z))


def _post_mix(o_s, o_gla, r_g, h0, gn4, w_out, g1, b1, token):
    tm = _row_tile(384)
    lp = _lp()

    def body(os_ref, og_ref, r_ref, h0_ref, gn_ref, w_ref, g_ref, b_ref, token_ref, o_ref, pre_ref, h1_ref):
        for pos, h in enumerate(HEAD_POS):
            o_ref[:, h * DH:(h + 1) * DH] = os_ref[:, pos * DH:(pos + 1) * DH].astype(BF16)
        for h in range(GLA_HEADS):
            hs = slice(h * DV, (h + 1) * DV)
            xg = og_ref[:, hs]
            n = xg * lax.rsqrt(jnp.mean(xg * xg, axis=-1, keepdims=True) + RMS_EPS) * gn_ref[...]
            r = r_ref[:, hs]
            o_ref[:, 512 + h * DV:512 + (h + 1) * DV] = (n * (r * _sigmoid(r))).astype(BF16)
        pre = ALPHA * h0_ref[...] + _dot(o_ref[...], w_ref[...])
        pre_ref[...] = pre
        xhat, _ = _ln_stats(pre)
        h1_ref[...] = xhat * g_ref[...] + b_ref[...]

    return pl.pallas_call(
        body, name="post_mix", grid=(lp // tm,),
        in_specs=[_rows(tm, 512), _rows(tm, 512), _rows(tm, 512), _rows(tm, D), _const((1, DV)), _const((D, D)),
                  _const((1, D)), _const((1, D)), _const(TOKEN)],
        out_specs=[_rows(tm, D), _rows(tm, D), _rows(tm, D)],
        out_shape=[pltpu.HBM((lp, D), BF16), pltpu.HBM((lp, D), F32),
                   pltpu.HBM((lp, D), F32)],
        compiler_params=_params(32, dimension_semantics=_seq()),
    )(*_hbm(o_s, o_gla, r_g, h0, gn4, w_out, g1, b1), token)


def _ffn_fwd_loss_bwd(h1, wg_t, wu_t, wd, target, g2, b2):
    lp = _lp()
    tm = max(t for t in range(BLK, 384 + 1, BLK) if lp % t == 0)
    steps = lp // tm
    last_blk = SEQ // BLK - 1
    half = D_FF // 2
    n_t = tm // BLK

    def body(*refs):
        h_ref, wg_ref, wu_ref, wd_ref = refs[:4]
        t_refs = refs[4:4 + n_t]
        g2_ref, b2_ref, a_ref, dgate_ref, dup_ref, dp_ref, loss_ref, dg_ref, db_ref, g_s, u_s, acc = refs[4 + n_t:]
        i = pl.program_id(0)

        @pl.when(i == 0)
        def _():
            acc[...] = jnp.zeros_like(acc)
            dg_ref[...] = jnp.zeros_like(dg_ref)
            db_ref[...] = jnp.zeros_like(db_ref)

        h = h_ref[...]
        hb = h.astype(BF16)
        pre = ALPHA * h
        for j in range(2):
            cols = slice(j * half, (j + 1) * half)
            g = _dot_nt(hb, wg_ref[cols, :])
            u = _dot_nt(hb, wu_ref[cols, :])
            g_s[:, cols] = g
            u_s[:, cols] = u
            pre = pre + _dot(g * _sigmoid(g) * u, wd_ref[cols, :])
        xhat, rstd = _ln_stats(pre)
        real = i * tm + _iota((tm, 1), 0) < SEQ
        target_rows = jnp.concatenate([t[...] for t in t_refs], axis=0)
        diff = jnp.where(real, xhat * g2_ref[...] + b2_ref[...] - target_rows, 0.0)
        acc[...] += jnp.sum(diff * diff, axis=0, keepdims=True)
        dy = diff * (1.0 / D)
        dpre = _ln_bwd(dy, xhat, rstd, g2_ref[...])
        dp_ref[...] = dpre
        dg_ref[...] += jnp.sum(dy * xhat, axis=0, keepdims=True)
        db_ref[...] += jnp.sum(dy, axis=0, keepdims=True)
        dpb = dpre.astype(BF16)
        for j in range(2):
            cols = slice(j * half, (j + 1) * half)
            g, u = g_s[:, cols], u_s[:, cols]
            sg = _sigmoid(g)
            silu = g * sg
            da = _dot_nt(dpb, wd_ref[cols, :])
            a_ref[:, cols] = (silu * u).astype(BF16)
            dgate_ref[:, cols] = (da * u * (sg * (1.0 + g * (1.0 - sg)))).astype(BF16)
            dup_ref[:, cols] = (da * silu).astype(BF16)

        @pl.when(i == steps - 1)
        def _():
            loss_ref[...] = jnp.zeros_like(loss_ref) + (0.5 / D) * jnp.sum(acc[...], axis=1, keepdims=True)

    t_spec = lambda k: pl.BlockSpec((BLK, D), lambda i: (jnp.minimum(i * n_t + k, last_blk), 0))
    return pl.pallas_call(
        body, name="ffn_fwd_loss_bwd", grid=(steps,),
        in_specs=[_rows(tm, D), _const((D_FF, D)), _const((D_FF, D)), _const((D_FF, D))]
        + [t_spec(k) for k in range(n_t)] + [_const((1, D)), _const((1, D))],
        out_specs=[_rows(tm, D_FF), _rows(tm, D_FF), _rows(tm, D_FF), _rows(tm, D), _acc((1, LANE)), _acc((1, D)),
                   _acc((1, D))],
        out_shape=[pltpu.HBM((lp, D_FF), BF16)] * 3 + [pltpu.HBM((lp, D), F32), pltpu.HBM((1, LANE), F32),
                                                         pltpu.HBM((1, D), F32), pltpu.HBM((1, D), F32)],
        scratch_shapes=[pltpu.VMEM((tm, D_FF), F32), pltpu.VMEM((tm, D_FF), F32), pltpu.VMEM((1, D), F32)],
        compiler_params=_params(58, dimension_semantics=_seq()),
    )(*_hbm(h1, wg_t, wu_t, wd, *[target] * n_t, g2, b2))


def _ffn_out_bwd(dpre2, dgate, dup, pre1, wg_t, wu_t, g1, w_out, o_gla, r_g, gn4):
    tm = _row_tile(384)
    lp = _lp()

    def body(dp_ref, dg_ref, du_ref, p1_ref, wg_ref, wu_ref, g1_ref, w_ref, og_ref, r_ref, gn_ref,
             dp1_ref, dg1_ref, db1_ref, dos_ref, dog_ref, dr_ref, dgn_ref):
        @pl.when(pl.program_id(0) == 0)
        def _():
            for acc_ref in (dg1_ref, db1_ref, dgn_ref):
                acc_ref[...] = jnp.zeros_like(acc_ref)

        dh1 = ALPHA * dp_ref[...] + _dot(dg_ref[...], wg_ref[...]) + _dot(du_ref[...], wu_ref[...])
        xhat, rstd1 = _ln_stats(p1_ref[...])
        dpre1 = _ln_bwd(dh1, xhat, rstd1, g1_ref[...])
        dp1_ref[...] = dpre1
        dg1_ref[...] += jnp.sum(dh1 * xhat, axis=0, keepdims=True)
        db1_ref[...] += jnp.sum(dh1, axis=0, keepdims=True)

        do = _dot_nt(dpre1, w_ref[...])
        for pos, h in enumerate(HEAD_POS):
            dos_ref[:, pos * DH:(pos + 1) * DH] = do[:, h * DH:(h + 1) * DH]
        gn = gn_ref[...]
        for h in range(GLA_HEADS):
            hs = slice(h * DV, (h + 1) * DV)
            xg = og_ref[:, hs]
            rstd = lax.rsqrt(jnp.mean(xg * xg, axis=-1, keepdims=True) + RMS_EPS)
            nx = xg * rstd
            r = r_ref[:, hs]
            sr = _sigmoid(r)
            d_o = do[:, 512 + h * DV:512 + (h + 1) * DV]
            dr_ref[:, hs] = d_o * (nx * gn) * (sr * (1.0 + r * (1.0 - sr)))
            dn = d_o * (r * sr)
            dgn_ref[...] += jnp.sum(dn * nx, axis=0, keepdims=True)
            dnx = dn * gn
            dog_ref[:, hs] = rstd * (dnx - nx * jnp.mean(dnx * nx, axis=-1, keepdims=True))

    return pl.pallas_call(
        body, name="ffn_out_bwd", grid=(lp // tm,),
        in_specs=[_rows(tm, D), _rows(tm, D_FF), _rows(tm, D_FF), _rows(tm, D), _const((D_FF, D)), _const((D_FF, D)),
                  _const((1, D)), _const((D, D)), _rows(tm, 512), _rows(tm, 512), _const((1, DV))],
        out_specs=[_rows(tm, D), _acc((1, D)), _acc((1, D)), _rows(tm, 512), _rows(tm, 512), _rows(tm, 512),
                   _acc((1, DV))],
        out_shape=[pltpu.HBM((lp, D), F32), pltpu.HBM((1, D), F32), pltpu.HBM((1, D), F32)]
        + [pltpu.HBM((lp, 512), F32)] * 3 + [pltpu.HBM((1, DV), F32)],
        compiler_params=_params(48, dimension_semantics=_seq()),
    )(*_hbm(dpre2, dgate, dup, pre1, wg_t, wu_t, g1, w_out, o_gla, r_g, gn4))


def _atb(a, b, name, token=None, windows=None):
    lp = _lp()
    tm = _row_tile(1408)
    n, w = a.shape[1], b.shape[1]
    bw = 512 if n * w * 4 > (4 << 20) else w
    tokens = [] if token is None else [token]
    steps = lp // tm

    def body(a_ref, b_ref, *rest):
        o_ref, acc_ref = rest[len(tokens):] if windows else (rest[-1], rest[-1])

        @pl.when(pl.program_id(1) == 0)
        def _():
            acc_ref[...] = jnp.zeros_like(acc_ref)

        acc_ref[...] += _dot_tn(a_ref[...], b_ref[...])

        if windows:
            @pl.when(pl.program_id(1) == steps - 1)
            def _():
                for s, start in enumerate(windows[0]):
                    o_ref[s] = acc_ref[start:start + windows[1], :]

    if windows:
        count, height = len(windows[0]), windows[1]
        out_spec, out_shape = pl.BlockSpec((count, height, bw), lambda j, k: (0, 0, j)), (count, height, w)
    else:
        out_spec, out_shape = pl.BlockSpec((n, bw), lambda j, k: (0, j)), (n, w)
    return pl.pallas_call(
        body, name=name, grid=(w // bw, steps),
        in_specs=[pl.BlockSpec((tm, n), lambda j, k: (k, 0)), pl.BlockSpec((tm, bw), lambda j, k: (k, j))]
        + [_const(TOKEN)] * len(tokens),
        out_specs=out_spec, out_shape=pltpu.HBM(out_shape, F32),
        scratch_shapes=[pltpu.VMEM((n, bw), F32)] if windows else [],
        compiler_params=_params(48, dimension_semantics=_seq(2)),
    )(*_hbm(a, b), *tokens)


def _gla_bwd(qg, kg, vg, z, do_gla, st_all, token):
    steps = SEQ // BLK + 1
    kw, vw = GLA_HEADS * DK, GLA_HEADS * DV
    pairs = [(c, h) for c in range(GLA_PER_STEP) for h in range(GLA_HEADS)]
    heads = range(GLA_HEADS)

    def body(q_ref, k_ref, v_ref, z_ref, do_ref, st_ref, token_ref, dq_ref, dk_ref, dv_ref, dz_ref, dst):
        @pl.when(pl.program_id(0) == 0)
        def _():
            dst[...] = jnp.zeros_like(dst)

        rmask = _gla_rowmask(steps - 1 - pl.program_id(0))
        zz = z_ref[...]
        b, b_last = _gla_decay(zz, rmask)
        e_b, e_nb, e_kd, e_last = jnp.exp(b), jnp.exp(-b), jnp.exp(b_last - b), jnp.exp(b_last)
        q = q_ref[...] * (rmask * DK ** -0.5)
        k = k_ref[...] * rmask
        v = v_ref[...] * rmask
        qe, ke, kd = q * e_b, k * e_nb, k * e_kd
        d_o = do_ref[...]
        causal = _iota((CH, CH), 0) >= _iota((CH, CH), 1)
        a, da, dqe, dke, dv_intra, carry = {}, {}, {}, {}, {}, {}
        for c, h in pairs:
            rows, ks, vs_ = _gla_slices(c, h)
            a[c, h] = jnp.where(causal, _dot_nt(qe[rows, ks], ke[rows, ks]), 0.0)
            da[c, h] = jnp.where(causal, _dot_nt(d_o[rows, vs_], v[rows, vs_]), 0.0)
            carry[c, h] = _dot_tn(d_o[rows, vs_], qe[rows, ks])
        for c, h in pairs:
            rows, ks, vs_ = _gla_slices(c, h)
            dqe[c, h] = _dot(d_o[rows, vs_], st_ref[0, c][:, ks]) + _dot(da[c, h], ke[rows, ks])
            dke[c, h] = _dot_tn(da[c, h], qe[rows, ks])
            dv_intra[c, h] = _dot_tn(a[c, h], d_o[rows, vs_])
        dstate = dst[...]
        dkd, db_decay = {}, {}
        for c in reversed(range(GLA_PER_STEP)):
            for h in heads:
                rows, ks, vs_ = _gla_slices(c, h)
                dkd[c, h] = _dot(v[rows, vs_], dstate[:, ks])
                dv_ref[rows, vs_] = dv_intra[c, h] + _dot_nt(kd[rows, ks], dstate[:, ks])
            chunk_last = e_last[c * CH:c * CH + 1]
            db_decay[c] = jnp.sum(dstate * st_ref[0, c], axis=0, keepdims=True) * chunk_last
            dstate = dstate * chunk_last + jnp.concatenate([carry[c, h] for h in heads], axis=1)
        dst[...] = dstate
        rows_of = lambda parts: jnp.concatenate(
            [jnp.concatenate([parts[c, h] for h in heads], axis=1) for c in range(GLA_PER_STEP)], axis=0)
        dqe_all, dke_all, dkd_all = rows_of(dqe), rows_of(dke), rows_of(dkd)
        dq_ref[...] = dqe_all * e_b * (rmask * DK ** -0.5)
        dk_ref[...] = (dke_all * e_nb + dkd_all * e_kd) * rmask
        dkd_kd = dkd_all * kd
        db = dqe_all * qe - dke_all * ke - dkd_kd
        _, upper, same = _gla_chunk_masks()
        decay_rows = jnp.concatenate([jnp.broadcast_to(db_decay[c], (CH, kw)) for c in range(GLA_PER_STEP)], axis=0)
        dlog_g = _dot_exact(upper.astype(F32), db) + _dot_exact(same.astype(F32), dkd_kd) + decay_rows
        dz_ref[...] = dlog_g * (rmask / GLA_TAU) * _sigmoid(-zz)

    blk = lambda w: pl.BlockSpec((BLK, w), lambda s: (_gla_block(steps - 1 - s), 0))
    return pl.pallas_call(
        body, name="gla_bwd", grid=(steps,),
        in_specs=[blk(kw), blk(kw), blk(vw), blk(kw), blk(vw),
                  pl.BlockSpec((1, GLA_PER_STEP, DV, kw), lambda s: (steps - 1 - s, 0, 0, 0)), _const(TOKEN)],
        out_specs=[blk(kw), blk(kw), blk(vw), blk(kw)],
        out_shape=[pltpu.HBM((_lp(), kw), F32), pltpu.HBM((_lp(), kw), F32),
                   pltpu.HBM((_lp(), vw), F32), pltpu.HBM((_lp(), kw), F32)],
        scratch_shapes=[pltpu.VMEM((DV, kw), F32)],
        compiler_params=_params(16, dimension_semantics=_seq()),
    )(*_hbm(qg, kg, vg, z, do_gla, st_all), token)


def _swa_bwd(sinks, qs, ks, vs, do_s, token):
    nb = SEQ // BLK
    kvw = SWA_KV_HEADS * DH
    scale = DH ** -0.5
    heads = range(SWA_HEADS)

    def body(sink_ref, q_ref, km_ref, kp_ref, kc_ref, vm_ref, vp_ref, vc_ref, do_ref, token_ref,
             dq_ref, dk_ref, dv_ref, dsink_ref, carry_k, carry_v, meta_k, meta_v):
        n = pl.program_id(0)

        @pl.when(n == 0)
        def _():
            for r in (carry_k, carry_v, meta_k, meta_v):
                r[...] = jnp.zeros_like(r)
            dsink_ref[...] = jnp.zeros_like(dsink_ref)

        @pl.when(n <= nb)
        def _():
            negdist, maskbias = _swa_bias(n)
            lane = _iota((1, LANE), 1)
            k_all = jnp.concatenate([km_ref[...], kp_ref[...], kc_ref[...]], axis=0).astype(BF16)
            v_all = jnp.concatenate([vm_ref[...], vp_ref[...], vc_ref[...]], axis=0).astype(BF16)
            q = [_swa_half(q_ref, pos, scale) for pos in heads]
            d_o = [_swa_half(do_ref, pos) for pos in heads]
            t = [_dot_nt(q[pos], k_all) + (2.0 ** -(HEAD_POS[pos] + 1) * negdist + maskbias) for pos in heads]
            dp = [_dot_nt(d_o[pos], v_all) for pos in heads]
            soft = [_swa_softmax(t[pos], sink_ref[HEAD_POS[pos]]) for pos in heads]
            p = [s[0] for s in soft]
            delta = [jnp.sum(p[pos] * dp[pos], axis=-1, keepdims=True) for pos in heads]
            ds = [(p[pos] * (dp[pos] - delta[pos])).astype(BF16) for pos in heads]
            dq = [_dot(ds[pos], k_all) for pos in heads]
            for col in range(SWA_HEADS // 2):
                dq_ref[:, col * LANE:(col + 1) * LANE] = scale * _swa_merge(dq[2 * col], dq[2 * col + 1])
            dsink = jnp.zeros((1, LANE), F32)
            for pos in heads:
                dsink = dsink + jnp.where(lane == HEAD_POS[pos],
                                          -jnp.sum(soft[pos][1] * delta[pos], axis=0, keepdims=True), 0.0)
            dsink_ref[...] += dsink
            dk3 = _dot_tn(jnp.concatenate(q, axis=0), jnp.concatenate(ds, axis=0)).T
            dv3 = _dot_tn(jnp.concatenate(d_o, axis=0), jnp.concatenate([x.astype(BF16) for x in p], axis=0)).T
            meta_k[...] += dk3[0:BLK]
            meta_v[...] += dv3[0:BLK]
            dk_ref[...] = carry_k[...] + dk3[BLK:2 * BLK]
            dv_ref[...] = carry_v[...] + dv3[BLK:2 * BLK]
            carry_k[...] = dk3[2 * BLK:3 * BLK]
            carry_v[...] = dv3[2 * BLK:3 * BLK]

        @pl.when(n == nb + 1)
        def _():
            dk_ref[...] = meta_k[...]
            dv_ref[...] = meta_v[...]

    kv_out = pl.BlockSpec((BLK, kvw), lambda n: (jnp.where(n == nb + 1, nb, jnp.clip(n - 1, 0, nb - 1)), 0))
    qblk = pl.BlockSpec((BLK, SWA_HEADS * DH), lambda n: (jnp.minimum(n, nb), 0))
    return pl.pallas_call(
        body, name="swa_bwd", grid=(nb + 2,),
        in_specs=[pl.BlockSpec(memory_space=pltpu.SMEM), qblk] + _swa_kv_specs(kvw) + _swa_kv_specs(kvw)
        + [qblk, _const(TOKEN)],
        out_specs=[qblk, kv_out, kv_out, _acc((1, LANE))],
        out_shape=[pltpu.HBM((_lp(), SWA_HEADS * DH), F32), pltpu.HBM((_lp(), kvw), F32),
                   pltpu.HBM((_lp(), kvw), F32), pltpu.HBM((1, LANE), F32)],
        scratch_shapes=[pltpu.VMEM((BLK, kvw), F32)] * 4,
        compiler_params=_params(16, dimension_semantics=_seq()),
    )(sinks, *_hbm(qs, ks, ks, ks, vs, vs, vs, do_s), token)


def _in_bwd(dqs, dks, dvs, dqg, dkg, dvg, drg, dz, dpre1, w_in_t, wg2_p):
    tm = _row_tile(384)
    lp = _lp()
    widths = (512, 128, 128, 256, 256, 512, 512)
    offs = (O_QS, O_KS, O_VS, O_QG, O_KG, O_VG, O_RG)

    def body(*refs):
        parts, (dz_ref, dp1_ref, w_ref, wg2_ref, dproj_ref, dh0_ref, dbin_ref, dbg_ref) = refs[:7], refs[7:]

        @pl.when(pl.program_id(0) == 0)
        def _():
            dbin_ref[...] = jnp.zeros_like(dbin_ref)
            dbg_ref[...] = jnp.zeros_like(dbg_ref)

        for pos, h in enumerate(HEAD_POS):
            val = parts[0][:, pos * DH:(pos + 1) * DH]
            dproj_ref[:, O_QS + h * DH:O_QS + (h + 1) * DH] = val.astype(BF16)
            dbin_ref[:, O_QS + h * DH:O_QS + (h + 1) * DH] += jnp.sum(val, axis=0, keepdims=True)
        for p_ref, off, wd in zip(parts[1:], offs[1:], widths[1:]):
            val = p_ref[...]
            dproj_ref[:, off:off + wd] = val.astype(BF16)
            dbin_ref[:, off:off + wd] += jnp.sum(val, axis=0, keepdims=True)
        dz = dz_ref[...]
        dlr = _dot_nt(dz, wg2_ref[...])
        dproj_ref[:, O_LR:O_LR + LANE] = dlr.astype(BF16)
        dbin_ref[:, O_LR:O_LR + LANE] += jnp.sum(dlr, axis=0, keepdims=True)
        dbg_ref[...] += jnp.sum(dz, axis=0, keepdims=True)
        dh0_ref[...] = ALPHA * dp1_ref[...] + _dot(dproj_ref[...], w_ref[...])

    return pl.pallas_call(
        body, name="in_bwd", grid=(lp // tm,),
        in_specs=[_rows(tm, w) for w in widths] + [_rows(tm, 256), _rows(tm, D), _const((D_IN_P, D)), _const((LANE, 256))],
        out_specs=[_rows(tm, D_IN_P), _rows(tm, D), _acc((1, D_IN_P)), _acc((1, 256))],
        out_shape=[pltpu.HBM((lp, D_IN_P), BF16), pltpu.HBM((lp, D), F32),
                   pltpu.HBM((1, D_IN_P), F32), pltpu.HBM((1, 256), F32)],
        compiler_params=_params(40, dimension_semantics=_seq()),
    )(*_hbm(dqs, dks, dvs, dqg, dkg, dvg, drg, dz, dpre1, w_in_t, wg2_p))


def _ln_in_bwd(x, meta_ext, dh0, g, token):
    tr = min(LN_ROWS, SEQ)

    def ln_bwd(x_ref, dh_ref, g_ref, dx_ref, dg_ref, db_ref):
        @pl.when(pl.program_id(0) == 0)
        def _():
            dg_ref[...] = jnp.zeros_like(dg_ref)
            db_ref[...] = jnp.zeros_like(db_ref)

        xhat, rstd = _ln_stats(x_ref[...])
        dh = dh_ref[...]
        dx_ref[...] = _ln_bwd(dh, xhat, rstd, g_ref[...])
        dg_ref[...] += jnp.sum(dh * xhat, axis=0, keepdims=True)
        db_ref[...] += jnp.sum(dh, axis=0, keepdims=True)

    def body(x_ref, dh_ref, g_ref, token_ref, dx_ref, dg_ref, db_ref):
        ln_bwd(x_ref, dh_ref, g_ref, dx_ref, dg_ref, db_ref)

    def meta_body(m_ref, dh_ref, g_ref, dm_ref, dg_ref, db_ref):
        ln_bwd(m_ref, dh_ref, g_ref, dm_ref, dg_ref, db_ref)

    sums = [pltpu.HBM((1, D), F32), pltpu.HBM((1, D), F32)]
    dx, dg, db = pl.pallas_call(
        body, name="ln_in_bwd", grid=(SEQ // tr,),
        in_specs=[_rows(tr, D), _rows(tr, D), _const((1, D)), _const(TOKEN)],
        out_specs=[_rows(tr, D), _acc((1, D)), _acc((1, D))],
        out_shape=[pltpu.HBM((SEQ, D), F32)] + sums,
        compiler_params=_params(32, dimension_semantics=_seq()),
    )(*_hbm(x, dh0, g), token)
    dm, dg_m, db_m = pl.pallas_call(
        meta_body, name="ln_in_bwd_meta", grid=(1,),
        in_specs=[_const((BLK, D)), pl.BlockSpec((BLK, D), lambda i: (SEQ // BLK, 0)), _const((1, D))],
        out_specs=[_acc((BLK, D)), _acc((1, D)), _acc((1, D))],
        out_shape=[pltpu.HBM((BLK, D), F32)] + sums,
        compiler_params=_params(16, dimension_semantics=_seq()),
    )(*_hbm(meta_ext, dh0, g))
    return dx, dm, dg + dg_m, db + db_m


def _local_step(x, target, ln_in_g, ln_in_b, b_in, bg2, sinks, gn, g1, b1, g2, b2,
                token, fetch_first, fetch_rest, fetch_ffn, exchange_ffn, ship_ffn, ship_w_in):
    row = lambda v: v.reshape(1, -1).astype(F32)
    b_in_p = jnp.pad(row(b_in), ((0, 0), (0, D_IN_P - D_IN)))
    gn4 = row(gn)
    sinks = sinks.reshape(-1).astype(F32)

    h_real = _ln_in_fwd_real(x, row(ln_in_g), row(ln_in_b), token)
    w_in_windows, meta_full, wg2 = fetch_first([h_real])
    meta_ext = jnp.pad(meta_full, ((META_OFF, BLK - CH), (0, 0)))
    wg2_p = jnp.pad(wg2, ((0, LANE - wg2.shape[0]), (0, 0))).astype(BF16)
    h0 = _ln_in_fwd_meta(h_real, meta_ext, row(ln_in_g), row(ln_in_b))
    qs, ks, vs, qg, kg, vg, rg, glr, z, w_in_t = _in_proj(h0, w_in_windows, b_in_p, wg2_p, row(bg2))
    o_s = _swa_fwd(sinks, qs, ks, vs)
    o_gla, st_all = _gla_fwd(qg, kg, vg, z)
    w_out, token = fetch_rest([o_s, o_gla])
    o, pre1, h1 = _post_mix(o_s, o_gla, rg, h0, gn4, w_out, row(g1), row(b1), token)
    wg_t, wu_t, wd = fetch_ffn([pre1])
    a, dgate, dup, dpre2, loss, dg2, db2 = _ffn_fwd_loss_bwd(h1, wg_t, wu_t, wd, target, row(g2), row(b2))
    dpre1, dg1, db1, do_s, do_gla, drg, dgn = _ffn_out_bwd(dpre2, dgate, dup, pre1, wg_t, wu_t, row(g1), w_out, o_gla,
                                                           rg, gn4)
    dwd = _atb(a, dpre2, "dw_down")
    dwg_t = _atb(dgate, h1, "dw_gate")
    dwu_t = _atb(dup, h1, "dw_up")
    token = exchange_ffn(dict(w_g=dwg_t, w_u=dwu_t, w_d=dwd))
    token = ship_ffn(_atb(o, dpre1, "dw_out", token))
    dqg, dkg, dvg, dz = _gla_bwd(qg, kg, vg, z, do_gla, st_all, token)
    dqs, dks, dvs, dsinks = _swa_bwd(sinks, qs, ks, vs, do_s, token)
    dproj, dh0, db_in_p, dbg2 = _in_bwd(dqs, dks, dvs, dqg, dkg, dvg, drg, dz, dpre1, w_in_t, wg2_p)
    token = ship_w_in(_atb(dproj, h0, "dw_in", windows=(W_IN_STARTS, W_IN_WIN)))
    dwg2_p = _atb(glr, dz, "dw_gate_lr2")
    dx, dmeta_blk, dg_in, db_in_ln = _ln_in_bwd(x, meta_ext, dh0, row(ln_in_g), token)

    small = dict(meta_blk=dmeta_blk, ln_in_g=dg_in, ln_in_b=db_in_ln, ln1_g=dg1, ln1_b=db1, ln2_g=dg2, ln2_b=db2,
                 b_in_p=db_in_p, wg2_p=dwg2_p, bg2=dbg2, sinks=dsinks, gn=dgn, loss=loss)
    return dx, small


HBM = pl.BlockSpec(memory_space=pltpu.HBM)


def _place():
    return lax.axis_index("x"), lax.axis_index("y"), lax.axis_index("c")


def _other_chips(x, y):
    return [(1 - x, y), (x, 1 - y), (1 - x, 1 - y)]


def _dma_sems(n):
    return pltpu.SemaphoreType.DMA((n,))


def _comm_params():
    return pltpu.CompilerParams(has_side_effects=True)


SEM = pl.BlockSpec(memory_space=pltpu.SEMAPHORE)


PER_ARRAY = dict(gather=3, scatter=3, sibling=N_CHIPS)


def _ici_copies(kind, landing, srcs, lands, send_sems, recv_sems):
    x, y, c = _place()
    mine = 2 * x + y
    copies = []
    for a in range(len(srcs)):
        if kind == "sibling":
            for s in range(N_CHIPS):
                copies.append(pltpu.make_async_remote_copy(
                    srcs[a].at[s, 1 - c], lands[a].at[s], send_sems.at[N_CHIPS * a + s], recv_sems.at[N_CHIPS * a + s],
                    device_id=(x, y, 1 - c), device_id_type=MESH))
            continue
        for j, (px, py) in enumerate(_other_chips(x, y)):
            slab = 2 * px + py if landing else mine
            if kind == "gather":
                src, dst = srcs[a].at[c], lands[a].at[slab, c]
            else:
                src, dst = srcs[a].at[2 * px + py], lands[a].at[slab]
            copies.append(pltpu.make_async_remote_copy(src, dst, send_sems.at[3 * a + j], recv_sems.at[3 * a + j],
                                                       device_id=(px, py, c), device_id_type=MESH))
    return copies


def _split_params():
    return pltpu.CompilerParams(has_side_effects=pltpu.SideEffectType.DATAFLOW_SIDE_EFFECTING)


def _ici_start(kind, srcs, land_shapes, after, name):
    n = len(srcs)
    lands = [pltpu.with_memory_space_constraint(lax.empty(s, a.dtype), pltpu.HBM) for s, a in zip(land_shapes, srcs)]

    def body(*refs):
        outs = refs[2 * n + len(after):]
        for cp in _ici_copies(kind, False, refs[:n], refs[n:2 * n], outs[0], outs[1]):
            cp.start()
        outs[-1][...] = jnp.zeros(TOKEN, F32)

    outs = pl.pallas_call(
        body, name=name, in_specs=[HBM] * (2 * n) + [pl.BlockSpec(memory_space=pl.ANY)] * len(after),
        out_specs=[SEM, SEM] + [HBM] * (2 * n) + [pl.BlockSpec(memory_space=pltpu.VMEM)],
        out_shape=[_dma_sems(PER_ARRAY[kind] * n)] * 2 + [pltpu.HBM(a.shape, a.dtype) for a in list(srcs) + lands]
        + [jax.ShapeDtypeStruct(TOKEN, F32)],
        input_output_aliases={i: 2 + i for i in range(2 * n)},
        compiler_params=_split_params(),
    )(*_hbm(*srcs), *lands, *after)
    return outs[:-1], outs[-1]


def _ici_wait(kind, handle, after, name):
    n = (len(handle) - 2) // 2

    def body(*refs):
        for cp in _ici_copies(kind, True, refs[:n], refs[n:2 * n], refs[2 * n], refs[2 * n + 1]):
            cp.wait_send()
            cp.wait_recv()

    outs = pl.pallas_call(
        body, name=name, in_specs=[HBM] * (2 * n) + [SEM, SEM] + [pl.BlockSpec(memory_space=pl.ANY)] * len(after),
        out_specs=[HBM] * (2 * n), out_shape=[pltpu.HBM(a.shape, a.dtype) for a in handle[2:]],
        input_output_aliases={i: i for i in range(2 * n)},
        compiler_params=_split_params(),
    )(*handle[2:], handle[0], handle[1], *after)
    return list(outs[:n]), list(outs[n:])


def _forward_copies(landing, arrs, send_sems, recv_sems):
    x, y, c = _place()
    copies = []
    for a in range(len(arrs)):
        for j, (px, py) in enumerate(_other_chips(x, y)):
            half = 1 - c if landing else c
            copies.append(pltpu.make_async_remote_copy(
                arrs[a].at[2 * px + py, c], arrs[a].at[2 * px + py, half], send_sems.at[3 * a + j],
                recv_sems.at[3 * a + j], device_id=(x, y, 1 - c), device_id_type=MESH))
    return copies


def _sibling_forward(lands, name):
    n = len(lands)

    def body(*refs):
        outs = refs[n:2 * n]
        send_sems, recv_sems = refs[2 * n:]
        sends = _forward_copies(False, outs, send_sems, recv_sems)
        for cp in sends:
            cp.start()
        for cp in _forward_copies(True, outs, send_sems, recv_sems):
            cp.wait_recv()
        for cp in sends:
            cp.wait_send()

    return pl.pallas_call(
        body, name=name, in_specs=[HBM] * n, out_specs=[HBM] * n,
        out_shape=[pltpu.HBM(a.shape, a.dtype) for a in lands],
        input_output_aliases={a: a for a in range(n)},
        scratch_shapes=[_dma_sems(3 * n)] * 2,
        compiler_params=_comm_params(),
    )(*_hbm(*lands))


def _forward_start(lands, name):
    n = len(lands)

    def body(*refs):
        outs = refs[n:]
        for cp in _forward_copies(False, refs[:n], outs[0], outs[1]):
            cp.start()
        outs[-1][...] = jnp.zeros(TOKEN, F32)

    outs = pl.pallas_call(
        body, name=name, in_specs=[HBM] * n,
        out_specs=[SEM, SEM] + [HBM] * n + [pl.BlockSpec(memory_space=pltpu.VMEM)],
        out_shape=[_dma_sems(3 * n)] * 2 + [pltpu.HBM(a.shape, a.dtype) for a in lands]
        + [jax.ShapeDtypeStruct(TOKEN, F32)],
        input_output_aliases={i: 2 + i for i in range(n)},
        compiler_params=_split_params(),
    )(*_hbm(*lands))
    return outs[:-1], outs[-1]


def _forward_wait(handle, after, name):
    n = len(handle) - 2

    def body(*refs):
        for cp in _forward_copies(True, refs[:n], refs[n], refs[n + 1]):
            cp.wait_send()
            cp.wait_recv()

    return list(pl.pallas_call(
        body, name=name, in_specs=[HBM] * n + [SEM, SEM] + [pl.BlockSpec(memory_space=pl.ANY)] * len(after),
        out_specs=[HBM] * n, out_shape=[pltpu.HBM(a.shape, a.dtype) for a in handle[2:]],
        input_output_aliases={i: i for i in range(n)},
        compiler_params=_split_params(),
    )(*handle[2:], handle[0], handle[1], *after))


def _sibling_exchange(grads, name):
    n = len(grads)

    def body(*refs):
        ins, outs = refs[:n], refs[n:2 * n]
        send_sems, recv_sems = refs[2 * n:]
        x, y, c = _place()
        copies = []
        for a in range(n):
            for s in range(N_CHIPS):
                cp = pltpu.make_async_remote_copy(ins[a].at[s, 1 - c], outs[a].at[s], send_sems.at[N_CHIPS * a + s],
                                                  recv_sems.at[N_CHIPS * a + s], device_id=(x, y, 1 - c),
                                                  device_id_type=MESH)
                cp.start()
                copies.append(cp)
        for cp in copies:
            cp.wait_recv()
        for cp in copies:
            cp.wait_send()

    return pl.pallas_call(
        body, name=name, in_specs=[HBM] * n, out_specs=[HBM] * n,
        out_shape=[pltpu.HBM((N_CHIPS, g.shape[2], D), F32) for g in grads],
        scratch_shapes=[_dma_sems(N_CHIPS * n)] * 2,
        compiler_params=_comm_params(),
    )(*_hbm(*grads))


def _add_halves(core, grads, recvs, dtypes, name):
    n = len(grads)
    heights = [g.shape[2] for g in grads]

    def body(c_ref, *refs):
        for a in range(n):
            refs[2 * n + a][...] = (refs[2 * a][0] + refs[2 * a + 1][...]).astype(dtypes[a])

    slab = lambda h: pl.BlockSpec((1, h, D), lambda s, c: (s, 0, 0))
    mine = lambda h: pl.BlockSpec((1, 1, h, D), lambda s, c: (s, c[0], 0, 0))
    return pl.pallas_call(
        body, name=name,
        grid_spec=pltpu.PrefetchScalarGridSpec(
            num_scalar_prefetch=1, grid=(N_CHIPS,),
            in_specs=[spec(h) for h in heights for spec in (mine, slab)], out_specs=[slab(h) for h in heights]),
        out_shape=[pltpu.HBM((N_CHIPS, h, D), dt) for h, dt in zip(heights, dtypes)],
        compiler_params=_params(32, dimension_semantics=_seq()),
    )(core, *_hbm(*[a for pair in zip(grads, recvs) for a in pair]))


N_DEVICES = 2 * N_CHIPS
PEER_FLIPS = [(dx, dy, dc) for dx in (0, 1) for dy in (0, 1) for dc in (0, 1)][1:]


def _small_copies(landing, p_ref, out_ref, send_sems, recv_sems):
    x, y, c = _place()
    flip = lambda v, d: 1 - v if d else v
    copies = []
    for k, flips in enumerate(PEER_FLIPS):
        px, py, pc = (flip(v, d) for v, d in zip((x, y, c), flips))
        slab = 4 * px + 2 * py + pc if landing else 4 * x + 2 * y + c
        copies.append(pltpu.make_async_remote_copy(p_ref, out_ref.at[slab], send_sems.at[k], recv_sems.at[k],
                                                   device_id=(px, py, pc), device_id_type=MESH))
    return copies


def _small_start(pack, after):
    n = len(PEER_FLIPS)
    land = pltpu.with_memory_space_constraint(lax.empty((N_DEVICES,) + pack.shape, F32), pltpu.HBM)

    def body(p_ref, land_ref, *refs):
        outs = refs[len(after):]
        for cp in _small_copies(False, p_ref, land_ref, outs[0], outs[1]):
            cp.start()
        outs[-1][...] = jnp.zeros(TOKEN, F32)

    outs = pl.pallas_call(
        body, name="small_exchange_start", in_specs=[HBM, HBM] + [pl.BlockSpec(memory_space=pl.ANY)] * len(after),
        out_specs=[SEM, SEM, HBM, HBM, pl.BlockSpec(memory_space=pltpu.VMEM)],
        out_shape=[_dma_sems(n), _dma_sems(n), pltpu.HBM(pack.shape, F32), pltpu.HBM(land.shape, F32),
                   jax.ShapeDtypeStruct(TOKEN, F32)],
        input_output_aliases={0: 2, 1: 3},
        compiler_params=_split_params(),
    )(*_hbm(pack), land, *after)
    return outs[:-1], outs[-1]


def _small_wait(handle, after):
    def body(p_ref, land_ref, send_sems, recv_sems, *rest):
        for cp in _small_copies(True, p_ref, land_ref, send_sems, recv_sems):
            cp.wait_send()
            cp.wait_recv()

    return pl.pallas_call(
        body, name="small_exchange_wait", in_specs=[HBM, HBM, SEM, SEM] + [pl.BlockSpec(memory_space=pl.ANY)] * len(after),
        out_specs=[HBM, HBM], out_shape=[pltpu.HBM(a.shape, F32) for a in handle[2:]],
        input_output_aliases={0: 0, 1: 1},
        compiler_params=_split_params(),
    )(handle[2], handle[3], handle[0], handle[1], *after)


def _sum_chips(slots, firsts, rests, after):
    n = len(firsts)

    def body(i_ref, *refs):
        outs = refs[4 * n + len(after):]
        for a in range(n):
            first, r1, r2, r3 = refs[4 * a:4 * a + 4]
            outs[a][...] = ((first[...].astype(F32) + r1[...].astype(F32)) + r2[...].astype(F32)) + r3[...].astype(F32)

    slab = lambda h, k: pl.BlockSpec((1, h, D), lambda i, ix: (ix[k], 0, 0))
    heights = [f.shape[1] for f in firsts]
    return pl.pallas_call(
        body, name="sum_chips",
        grid_spec=pltpu.PrefetchScalarGridSpec(
            num_scalar_prefetch=1, grid=(1,),
            in_specs=[slab(h, k) for h in heights for k in range(4)] + [pl.BlockSpec(memory_space=pl.ANY)] * len(after),
            out_specs=[slab(h, 4) for h in heights]),
        out_shape=[pltpu.HBM((2, h, D), F32) for h in heights],
        compiler_params=_params(48, dimension_semantics=_seq()),
    )(slots, *_hbm(*[a for f, r in zip(firsts, rests) for a in (f, r, r, r)]), *after)


def _join_halves(halves):
    n = len(halves)

    def body(*refs):
        outs = refs[n:2 * n]
        send_sems, recv_sems = refs[2 * n:]
        x, y, c = _place()

        def copy(a, slab):
            return pltpu.make_async_remote_copy(outs[a].at[slab], outs[a].at[slab], send_sems.at[a], recv_sems.at[a],
                                                device_id=(x, y, 1 - c), device_id_type=MESH)

        for a in range(n):
            copy(a, c).start()
        for a in range(n):
            copy(a, 1 - c).wait_recv()
        for a in range(n):
            copy(a, c).wait_send()

    return pl.pallas_call(
        body, name="join_halves", in_specs=[HBM] * n, out_specs=[HBM] * n,
        out_shape=[pltpu.HBM(h.shape, F32) for h in halves],
        input_output_aliases={a: a for a in range(n)},
        scratch_shapes=[_dma_sems(n)] * 2,
        compiler_params=_comm_params(),
    )(*_hbm(*halves))


def _chip_partials(grads, wire_dtypes, names, fetched=()):
    core = lax.axis_index("c").astype(jnp.int32).reshape(1)
    todo = len(grads) - len(fetched)
    recv = list(_sibling_exchange(grads[:todo], "sibling_exchange_" + names[0])) + list(fetched)
    return list(_add_halves(core, grads, recv, wire_dtypes, "add_halves_" + names[0]))


def _finish_reduce(parts, got, after):
    x, y, c = _place()
    others = [2 * px + py for px, py in _other_chips(x, y)]
    own_first = jnp.stack([2 * x + y] + others + [c]).astype(jnp.int32)
    return [f.reshape(2 * f.shape[1], D) for f in _join_halves(_sum_chips(own_first, parts, got, after))]


ADAMW_STEPS = 8


def _adamw(params, chip, window_step):
    n = len(params)

    def block(shape, by_rows):
        rows, cols = shape
        if by_rows:
            return pl.BlockSpec((rows // ADAMW_STEPS, cols), lambda i, c: (i, 0))
        assert cols % (LANE * ADAMW_STEPS) == 0
        return pl.BlockSpec((rows, cols // ADAMW_STEPS), lambda i, c: (0, i))

    by_rows = [w.shape[0] % (8 * ADAMW_STEPS) == 0 and g.shape == w.shape for w, g, _, _ in params]

    def body(c_ref, *refs):
        for a in range(n):
            w_ref, g_ref, m_ref, v_ref = refs[4 * a:4 * a + 4]
            outs = refs[4 * n + 4 * a:4 * n + 4 * a + 4]
            rows = w_ref.shape[0]
            g = g_ref[0:rows]
            if g_ref.shape[0] != rows:
                for s in range(1, N_CHIPS):
                    g = jnp.where(c_ref[0] == s, g_ref[s * window_step:s * window_step + rows], g)
            outs[0][...] = g
            outs[1][...], outs[2][...], outs[3][...] = _adamw_math(w_ref[...], g, m_ref[...], v_ref[...])

    outs = pl.pallas_call(
        body, name="adamw_matrices",
        grid_spec=pltpu.PrefetchScalarGridSpec(
            num_scalar_prefetch=1, grid=(ADAMW_STEPS,),
            in_specs=[block(a.shape, r) for p, r in zip(params, by_rows) for a in p],
            out_specs=[block(p[0].shape, r) for p, r in zip(params, by_rows) for _ in range(4)]),
        out_shape=[pltpu.HBM(p[0].shape, F32) for p in params for _ in range(4)],
        compiler_params=_params(48, dimension_semantics=_seq()),
    )(chip, *_hbm(*[a for p in params for a in p]))
    return [outs[4 * a:4 * a + 4] for a in range(n)]


def _adamw_math(w, g, m, v):
    nm = ADAM_B1 * m + (1.0 - ADAM_B1) * g
    nv = ADAM_B2 * v + (1.0 - ADAM_B2) * (g * g)
    m_hat = nm / (1.0 - ADAM_B1 ** ADAM_STEP)
    v_hat = nv / (1.0 - ADAM_B2 ** ADAM_STEP)
    return -ADAM_LR * (m_hat / (jnp.sqrt(v_hat) + ADAM_EPS) + ADAM_WD * w), nm, nv


SMALL = (("meta_tokens", (N_META, D // N_CHIPS)), ("ln_in_g", (1, D)), ("ln_in_b", (1, D)), ("b_in", (1, D_IN)),
         ("w_gate_lr2", (GATE_RANK, GLA_HEADS * DK // N_CHIPS)), ("b_gate_lr2", (1, GLA_HEADS * DK)),
         ("attn_sinks", (1, SWA_HEADS)),
         ("gla_norm_g", (1, DV)), ("ln1_g", (1, D)), ("ln1_b", (1, D)), ("ln2_g", (1, D)), ("ln2_b", (1, D)))
ROW_META, ROW_B_IN, ROW_TAIL, ROW_WG2 = 0, 22, 25, 32
ROW_LN = dict(ln_in_g=16, ln_in_b=17, ln1_g=18, ln1_b=19, ln2_g=20, ln2_b=21)
TAIL_BG2, TAIL_SINKS, TAIL_GN, TAIL_LOSS = 0, 256, 256 + SWA_HEADS, 256 + SWA_HEADS + DV


def _adamw_small(place, packs, own, params):
    n = len(SMALL)

    def body(place_ref, packs_ref, own_ref, *refs):
        ins, outs, p_ref = refs[:3 * n], refs[3 * n:-1], refs[-1]
        me, c = place_ref[0], place_ref[1]
        total = jnp.where(me == 0, own_ref[...], packs_ref[0])
        for i in range(1, N_DEVICES):
            total = total + jnp.where(me == i, own_ref[...], packs_ref[i])
        p_ref[...] = total
        outs[4 * n][...] = total[ROW_TAIL:ROW_TAIL + 1, :]

        def mine(width, rows):
            part = lambda s: p_ref[rows, s * width:(s + 1) * width]
            return jnp.where(c == 0, part(0), jnp.where(c == 1, part(1), jnp.where(c == 2, part(2), part(3))))

        tail = lambda lo, width: p_ref[ROW_TAIL:ROW_TAIL + 1, lo:lo + width]
        grads = dict(
            meta_tokens=mine(D // N_CHIPS, slice(ROW_META, ROW_META + N_META)),
            b_in=jnp.concatenate([p_ref[ROW_B_IN:ROW_B_IN + 1, :], p_ref[ROW_B_IN + 1:ROW_B_IN + 2, :],
                                  p_ref[ROW_B_IN + 2:ROW_B_IN + 3, 0:D_IN - 2 * D]], axis=1),
            w_gate_lr2=mine(256 // N_CHIPS, slice(ROW_WG2, ROW_WG2 + 16)),
            b_gate_lr2=tail(TAIL_BG2, 256), attn_sinks=tail(TAIL_SINKS, SWA_HEADS), gla_norm_g=tail(TAIL_GN, DV),
            **{k: p_ref[r:r + 1, :] for k, r in ROW_LN.items()})
        for i, (name, _) in enumerate(SMALL):
            g = grads[name]
            outs[4 * i][...] = g
            outs[4 * i + 1][...], outs[4 * i + 2][...], outs[4 * i + 3][...] = _adamw_math(
                ins[3 * i][...], g, ins[3 * i + 1][...], ins[3 * i + 2][...])

    whole = lambda shape: pl.BlockSpec(shape, lambda i, c: (0,) * len(shape))
    outs = pl.pallas_call(
        body, name="adamw_small",
        grid_spec=pltpu.PrefetchScalarGridSpec(
            num_scalar_prefetch=1, grid=(1,),
            in_specs=[whole(packs.shape), whole(own.shape)] + [whole(s) for _, s in SMALL for _ in range(3)],
            out_specs=[whole(s) for _, s in SMALL for _ in range(4)] + [whole((1, D))],
            scratch_shapes=[pltpu.VMEM(own.shape, F32)]),
        out_shape=[pltpu.HBM(s, F32) for _, s in SMALL for _ in range(4)] + [pltpu.HBM((1, D), F32)],
        compiler_params=_params(16, dimension_semantics=_seq()),
    )(place, *_hbm(packs, own, *[a for p in params for a in p]))
    return [outs[4 * i:4 * i + 4] for i in range(n)], outs[4 * n]


def _small_pack(gr):
    names = ["meta_blk"] + list(ROW_LN) + ["b_in_p", "wg2_p", "bg2", "sinks", "gn", "loss"]
    gate_w = GLA_HEADS * DK

    def body(*refs):
        src, out = dict(zip(names, refs)), refs[-1]
        out[...] = jnp.zeros_like(out)
        out[ROW_META:ROW_META + N_META, :] = src["meta_blk"][META_OFF:CH, :]
        for k, r in ROW_LN.items():
            out[r:r + 1, :] = src[k][...]
        for j in range(-(-D_IN // D)):
            width = min(D, D_IN - j * D)
            out[ROW_B_IN + j:ROW_B_IN + j + 1, 0:width] = src["b_in_p"][:, j * D:j * D + width]
        tail = slice(ROW_TAIL, ROW_TAIL + 1)
        out[tail, TAIL_BG2:TAIL_BG2 + gate_w] = src["bg2"][...]
        out[tail, TAIL_SINKS:TAIL_SINKS + SWA_HEADS] = src["sinks"][:, 0:SWA_HEADS]
        out[tail, TAIL_GN:TAIL_GN + DV] = src["gn"][...]
        out[tail, TAIL_LOSS:TAIL_LOSS + 1] = src["loss"][:, 0:1]
        out[ROW_WG2:ROW_WG2 + GATE_RANK, 0:gate_w] = src["wg2_p"][0:GATE_RANK, :]

    arrays = [gr[k] for k in names]
    return pl.pallas_call(
        body, name="small_pack", grid=(1,),
        in_specs=[_acc(a.shape) for a in arrays], out_specs=_acc((SMALL_ROWS, D)),
        out_shape=pltpu.HBM((SMALL_ROWS, D), F32),
        compiler_params=_params(16, dimension_semantics=_seq()),
    )(*_hbm(*arrays))


BIG = ("w_in", "w_out", "w_g", "w_u", "w_d")


def kernel(x, meta_tokens, ln_in_g, ln_in_b, w_in, b_in, w_gate_lr2, b_gate_lr2, attn_sinks, gla_norm_g, w_out, ln1_g, ln1_b, w_ffn_gate, w_ffn_up, w_ffn_down, ln2_g, ln2_b, loss_target, m_meta_tokens, m_ln_in_g, m_ln_in_b, m_w_in, m_b_in, m_w_gate_lr2, m_b_gate_lr2, m_attn_sinks, m_gla_norm_g, m_w_out, m_ln1_g, m_ln1_b, m_w_ffn_gate, m_w_ffn_up, m_w_ffn_down, m_ln2_g, m_ln2_b, v_meta_tokens, v_ln_in_g, v_ln_in_b, v_w_in, v_b_in, v_w_gate_lr2, v_b_gate_lr2, v_attn_sinks, v_gla_norm_g, v_w_out, v_ln1_g, v_ln1_b, v_w_ffn_gate, v_w_ffn_up, v_w_ffn_down, v_ln2_g, v_ln2_b):
    chip = 2 * lax.axis_index("x") + lax.axis_index("y")

    halves = lambda a: a.reshape(2, a.shape[0] // 2, a.shape[1])
    r_in = SHARD_ROWS["w_in"]
    first = [halves(a) for a in (jnp.pad(w_in[0].T.astype(BF16), ((0, W_IN_WIN - r_in), (0, 0))), meta_tokens,
                                 w_gate_lr2[0])]
    rest = [halves(a) for a in (w_out[0].astype(BF16), w_ffn_gate[0].T.astype(BF16), w_ffn_up[0].T.astype(BF16),
                                w_ffn_down[0].astype(BF16))]
    lands = lambda arrs: [(N_CHIPS,) + a.shape for a in arrs]
    first_handle, first_token = _ici_start("gather", first, lands(first), [], "gather_first_start")
    rest_handle, token = _ici_start("gather", rest, lands(rest), [first_token], "gather_rest_start")
    own_slab = lambda got, shards: [lax.dynamic_update_index_in_dim(g, s, chip, axis=0) for g, s in zip(got, shards)]
    fetching = {}

    def fetch_first(after):
        shards, landed = _ici_wait("gather", first_handle, after, "gather_first_wait")
        g_in, g_meta, g_wg2 = own_slab(_sibling_forward(landed, "gather_first_forward"), shards)
        w_in_windows = g_in.reshape(N_CHIPS, W_IN_WIN, D)
        meta_full = jnp.concatenate([g_meta[s].reshape(N_META, -1) for s in range(N_CHIPS)], axis=1)
        wg2_full = jnp.concatenate([g_wg2[s].reshape(w_gate_lr2.shape[1], -1) for s in range(N_CHIPS)], axis=1)
        return w_in_windows, meta_full, wg2_full

    def fetch_rest(after):
        shards, landed = _ici_wait("gather", rest_handle, after, "gather_rest_wait")
        g_out, = own_slab(_sibling_forward(landed[:1], "gather_w_out_forward"), shards[:1])
        fetching["shards"] = shards[1:]
        fetching["handle"], forward_token = _forward_start(landed[1:], "gather_ffn_forward_start")
        return g_out.reshape(-1, D), forward_token

    def fetch_ffn(after):
        got = _forward_wait(fetching["handle"], after, "gather_ffn_forward_wait")
        return [g.reshape(-1, D) for g in own_slab(got, fetching["shards"])]

    sent = {}
    split = lambda grads: [g.reshape(N_CHIPS, 2, -1, D) for g in grads]

    def ship(key, grads, names, fetched=()):
        parts = _chip_partials(grads, [BF16] * len(grads), names, fetched)
        sent[key], ship_token = _ici_start("scatter", parts, [p.shape for p in parts], [], "scatter_" + key + "_start")
        return ship_token

    def exchange_ffn(g):
        grads = split([g[k] for k in BIG[2:]])
        sent["ffn_halves"], exchange_token = _ici_start("sibling", grads, [(N_CHIPS,) + a.shape[2:] for a in grads], [],
                                                        "sibling_ffn_start")
        return exchange_token

    def ship_ffn(dw_out):
        grads, fetched = _ici_wait("sibling", sent["ffn_halves"], [dw_out], "sibling_ffn_wait")
        return ship("ffn", split([dw_out]) + grads, list(BIG[1:]), fetched)

    def ship_w_in(dw_in_windows):
        return ship("w_in", split([dw_in_windows]), ["w_in"])

    dx, gr = _local_step(
        x[0], loss_target[0], ln_in_g, ln_in_b, b_in[0], b_gate_lr2[0], attn_sinks[0], gla_norm_g[0], ln1_g[0],
        ln1_b[0], ln2_g[0], ln2_b[0], token, fetch_first, fetch_rest, fetch_ffn, exchange_ffn, ship_ffn, ship_w_in)
    ffn_parts, ffn_got = _ici_wait("scatter", sent["ffn"], [dx], "scatter_ffn_wait")
    w_in_parts, w_in_got = _ici_wait("scatter", sent["w_in"], [dx], "scatter_w_in_wait")

    small_handle, token = _small_start(_small_pack(gr), [w_in_got[0]])
    red = _finish_reduce(w_in_parts + ffn_parts, w_in_got + ffn_got, [token])

    big_g = dict(zip(BIG, red))
    weights = dict(meta_tokens=meta_tokens, ln_in_g=ln_in_g, ln_in_b=ln_in_b, w_in=w_in, b_in=b_in,
                   w_gate_lr2=w_gate_lr2, b_gate_lr2=b_gate_lr2, attn_sinks=attn_sinks, gla_norm_g=gla_norm_g,
                   w_out=w_out, ln1_g=ln1_g, ln1_b=ln1_b, w_ffn_gate=w_ffn_gate, w_ffn_up=w_ffn_up,
                   w_ffn_down=w_ffn_down, ln2_g=ln2_g, ln2_b=ln2_b)
    m_in = dict(meta_tokens=m_meta_tokens, ln_in_g=m_ln_in_g, ln_in_b=m_ln_in_b, w_in=m_w_in, b_in=m_b_in,
                w_gate_lr2=m_w_gate_lr2, b_gate_lr2=m_b_gate_lr2, attn_sinks=m_attn_sinks, gla_norm_g=m_gla_norm_g,
                w_out=m_w_out, ln1_g=m_ln1_g, ln1_b=m_ln1_b, w_ffn_gate=m_w_ffn_gate, w_ffn_up=m_w_ffn_up,
                w_ffn_down=m_w_ffn_down, ln2_g=m_ln2_g, ln2_b=m_ln2_b)
    v_in = dict(meta_tokens=v_meta_tokens, ln_in_g=v_ln_in_g, ln_in_b=v_ln_in_b, w_in=v_w_in, b_in=v_b_in,
                w_gate_lr2=v_w_gate_lr2, b_gate_lr2=v_b_gate_lr2, attn_sinks=v_attn_sinks, gla_norm_g=v_gla_norm_g,
                w_out=v_w_out, ln1_g=v_ln1_g, ln1_b=v_ln1_b, w_ffn_gate=v_w_ffn_gate, w_ffn_up=v_w_ffn_up,
                w_ffn_down=v_w_ffn_down, ln2_g=v_ln2_g, ln2_b=v_ln2_b)
    names = list(weights)
    big_names = ("w_in", "w_out", "w_ffn_gate", "w_ffn_up", "w_ffn_down")

    grads, delta, new_m, new_v = {}, {}, {}, {}
    flips = [(lambda a: a.T) if kk in ("w_in", "w_g", "w_u") else (lambda a: a) for kk in BIG]
    updated = _adamw([(flip(weights[k][0]), big_g[kk], flip(m_in[k][0]), flip(v_in[k][0]))
                      for k, kk, flip in zip(big_names, BIG, flips)],
                     chip.astype(jnp.int32).reshape(1), r_in % BF16_ROWS)
    for k, flip, results in zip(big_names, flips, updated):
        grads[k], delta[k], new_m[k], new_v[k] = (flip(t)[None] for t in results)
    small_in = [tuple(src[k].reshape(shape) for src in (weights, m_in, v_in)) for k, shape in SMALL]
    place = jnp.stack([2 * chip + lax.axis_index("c"), chip]).astype(jnp.int32)
    small_own, small_all = _small_wait(small_handle, [updated[0][0]])
    small_out, tail_row = _adamw_small(place, small_all, small_own, small_in)
    for (k, _), results in zip(SMALL, small_out):
        grads[k], delta[k], new_m[k], new_v[k] = (r.reshape(weights[k].shape) for r in results)

    return (tail_row[0, TAIL_LOSS], dx[None], *[grads[k] for k in names], *[delta[k] for k in names], *[new_m[k] for k in names],
            *[new_v[k] for k in names])
```

```python
import jax
import jax.numpy as jnp
from jax import lax
from jax.experimental import pallas as pl
from jax.experimental.pallas import tpu as pltpu

F32 = jnp.float32
BF16 = jnp.bfloat16
MESH = pl.DeviceIdType.MESH

D = 1024
SEQ = 4096
N_META = 16
SWA_HEADS, SWA_KV_HEADS, DH = 8, 2, 64
WINDOW = 128
GLA_HEADS, DK, DV = 4, 64, 128
GLA_TAU = 16.0
CH = 64
D_FF = 2816
D_IN = 2320
LN_EPS = 1e-5
RMS_EPS = 1e-6
ALPHA = 2.0 ** 0.25
NEG = -1e30
ADAM_LR, ADAM_B1, ADAM_B2, ADAM_EPS, ADAM_WD, ADAM_STEP = 0.001, 0.9, 0.999, 1e-8, 0.01, 10
O_QS, O_KS, O_VS, O_QG, O_KG, O_VG, O_RG, O_LR = 0, 512, 640, 768, 1024, 1280, 1792, 2304

LANE = 128
BLK = WINDOW
GATE_RANK = 16
D_IN_P = D_IN + LANE - GATE_RANK
META_OFF = CH - N_META
HEAD_POS = (0, 4, 1, 5, 2, 6, 3, 7)
LN_ROWS = 512
TOKEN = (8, LANE)
N_CHIPS = 4
SHARD_ROWS = dict(w_in=D_IN // N_CHIPS, w_out=D // N_CHIPS, w_g=D_FF // N_CHIPS, w_u=D_FF // N_CHIPS,
                  w_d=D_FF // N_CHIPS)
SMALL_ROWS = 48
BF16_ROWS = 16
W_IN_WIN = -(-SHARD_ROWS["w_in"] // (2 * BF16_ROWS)) * 2 * BF16_ROWS
W_IN_STARTS = tuple(s * SHARD_ROWS["w_in"] // BF16_ROWS * BF16_ROWS for s in range(N_CHIPS))
VMEM_CAP_MB = 64
VMEM_SPARE_MB = 6


def _lp():
    return SEQ + BLK


def _row_tile(cap):
    lp = _lp()
    return max(t for t in range(16, cap + 1, 16) if lp % t == 0)


def _params(vmem_mb, **kw):
    assert vmem_mb <= VMEM_CAP_MB - VMEM_SPARE_MB
    return pltpu.CompilerParams(vmem_limit_bytes=vmem_mb << 20, **kw)


def _seq(n=1):
    return ("arbitrary",) * n


def _const(shape):
    return pl.BlockSpec(shape, lambda *_: (0,) * len(shape), pipeline_mode=pl.Buffered(1))


def _acc(shape):
    return pl.BlockSpec(shape, lambda *_: (0,) * len(shape))


def _rows(tm, width):
    return pl.BlockSpec((tm, width), lambda i: (i, 0))


def _dot(a, b):
    return jnp.dot(a.astype(BF16), b.astype(BF16), preferred_element_type=F32)


def _dot_nt(a, b):
    return lax.dot_general(a.astype(BF16), b.astype(BF16), (((1,), (1,)), ((), ())), preferred_element_type=F32)


def _dot_tn(a, b):
    return lax.dot_general(a.astype(BF16), b.astype(BF16), (((0,), (0,)), ((), ())), preferred_element_type=F32)


def _dot_exact(a, b):
    return jnp.dot(a, b, precision=lax.Precision.HIGHEST, preferred_element_type=F32)


def _ln_stats(x):
    mu = jnp.mean(x, axis=-1, keepdims=True)
    xc = x - mu
    rstd = lax.rsqrt(jnp.mean(xc * xc, axis=-1, keepdims=True) + LN_EPS)
    return xc * rstd, rstd


def _ln_bwd(dy, xhat, rstd, g):
    dxh = dy * g
    return rstd * (dxh - jnp.mean(dxh, axis=-1, keepdims=True) - xhat * jnp.mean(dxh * xhat, axis=-1, keepdims=True))


def _sigmoid(x):
    return 1.0 / (1.0 + jnp.exp(-x))


def _iota(shape, dim):
    return lax.broadcasted_iota(jnp.int32, shape, dim)


def _hbm(*arrays):
    return tuple(pltpu.with_memory_space_constraint(a, pltpu.HBM) for a in arrays)


def _ln_in_fwd_real(x, g, b, token):
    tr = min(LN_ROWS, SEQ)

    def body(x_ref, g_ref, b_ref, token_ref, h_ref):
        xhat, _ = _ln_stats(x_ref[...])
        h_ref[...] = xhat * g_ref[...] + b_ref[...]

    return pl.pallas_call(
        body, name="ln_in_fwd", grid=(SEQ // tr,),
        in_specs=[_rows(tr, D), _const((1, D)), _const((1, D)), _const(TOKEN)],
        out_specs=_rows(tr, D),
        out_shape=pltpu.HBM((_lp(), D), F32),
        compiler_params=_params(32, dimension_semantics=_seq()),
    )(*_hbm(x, g, b), token)


def _ln_in_fwd_meta(h_real, meta_ext, g, b):
    def meta_body(m_ref, g_ref, b_ref, real_ref, h_ref):
        xhat, _ = _ln_stats(m_ref[...])
        h_ref[...] = xhat * g_ref[...] + b_ref[...]

    return pl.pallas_call(
        meta_body, name="ln_in_fwd_meta", grid=(1,),
        in_specs=[_const((BLK, D)), _const((1, D)), _const((1, D)), pl.BlockSpec(memory_space=pl.ANY)],
        out_specs=pl.BlockSpec((BLK, D), lambda i: (SEQ // BLK, 0)),
        out_shape=pltpu.HBM((_lp(), D), F32),
        input_output_aliases={3: 0},
        compiler_params=_params(16, dimension_semantics=_seq()),
    )(*_hbm(meta_ext, g, b, h_real))


def _in_proj(h0, w_in_windows, b_in_p, wg2_p, bg2):
    tm = _row_tile(384)
    lp = _lp()
    widths = (512, 128, 128, 256, 256, 512, 512, 128)
    offs = (O_QS, O_KS, O_VS, O_QG, O_KG, O_VG, O_RG, O_LR)
    shard = SHARD_ROWS["w_in"]

    def body(h_ref, win_ref, b_ref, wg2_ref, bg2_ref, *outs):
        w_ref = outs[9]

        @pl.when(pl.program_id(0) == 0)
        def _():
            for s in range(N_CHIPS):
                w_ref[shard * s:shard * (s + 1), :] = win_ref[s, 0:shard, :]
            w_ref[D_IN:D_IN_P, :] = jnp.zeros((D_IN_P - D_IN, D), BF16)

        proj = _dot_nt(h_ref[...], w_ref[...]) + b_ref[...]
        for pos, h in enumerate(HEAD_POS):
            outs[0][:, pos * DH:(pos + 1) * DH] = proj[:, O_QS + h * DH:O_QS + (h + 1) * DH]
        for o_ref, off, wd in zip(outs[1:8], offs[1:], widths[1:]):
            o_ref[...] = proj[:, off:off + wd]
        outs[8][...] = _dot(proj[:, O_LR:O_LR + LANE], wg2_ref[...]) + bg2_ref[...]

    return pl.pallas_call(
        body, name="in_proj", grid=(lp // tm,),
        in_specs=[_rows(tm, D), _const(w_in_windows.shape), _const((1, D_IN_P)), _const((LANE, 256)), _const((1, 256))],
        out_specs=[_rows(tm, w) for w in widths] + [_rows(tm, 256), _acc((D_IN_P, D))],
        out_shape=[pltpu.HBM((lp, w), F32) for w in widths] + [pltpu.HBM((lp, 256), F32), pltpu.HBM((D_IN_P, D), BF16)],
        compiler_params=_params(48, dimension_semantics=_seq()),
    )(*_hbm(h0, w_in_windows, b_in_p, wg2_p, bg2))


def _swa_masks(n):
    nb = SEQ // BLK
    is_meta = n == nb
    ri = _iota((BLK, BLK), 0)
    cj = _iota((BLK, BLK), 1)
    meta_col = ((cj >= META_OFF) & (cj < CH)).astype(jnp.int32)
    meta_q = meta_col * ((cj <= ri) & (ri < CH)).astype(jnp.int32)
    valid_m = jnp.where(is_meta, meta_q, meta_col) > 0
    dist_m = jnp.where(is_meta, ri - cj, n * BLK + ri + CH - cj).astype(F32)
    valid_p = jnp.where((n >= 1) & (n < nb), (cj > ri).astype(jnp.int32), 0) > 0
    dist_p = (ri + BLK - cj).astype(F32)
    valid_c = jnp.where(n < nb, (cj <= ri).astype(jnp.int32), 0) > 0
    dist_c = (ri - cj).astype(F32)
    return (dist_m, dist_p, dist_c), (valid_m, valid_p, valid_c)


def _swa_bias(n):
    dists, valids = _swa_masks(n)
    return (jnp.concatenate([-d for d in dists], axis=1),
            jnp.concatenate([jnp.where(v, 0.0, NEG) for v in valids], axis=1))


def _swa_half(ref, pos, scale=1.0):
    col = ref[:, (pos // 2) * LANE:(pos // 2 + 1) * LANE]
    lane = _iota((BLK, LANE), 1)
    mine = lane < DH if pos % 2 == 0 else lane >= DH
    return jnp.where(mine, col * scale, 0.0).astype(BF16)


def _swa_merge(even, odd):
    return jnp.where(_iota((BLK, LANE), 1) < DH, even, odd)


def _swa_softmax(t, sink):
    m = jnp.maximum(jnp.max(t, axis=-1, keepdims=True), sink)
    e = jnp.exp(t - m)
    e_sink = jnp.exp(sink - m)
    inv = 1.0 / (jnp.sum(e, axis=-1, keepdims=True) + e_sink)
    return e * inv, e_sink * inv


def _swa_kv_specs(width):
    nb = SEQ // BLK
    return [pl.BlockSpec((BLK, width), lambda n: (nb, 0)),
            pl.BlockSpec((BLK, width), lambda n: (jnp.clip(n - 1, 0, nb - 1), 0)),
            pl.BlockSpec((BLK, width), lambda n: (jnp.minimum(n, nb), 0))]


def _swa_fwd(sinks, qs, ks, vs):
    nb = SEQ // BLK
    heads = range(SWA_HEADS)

    def body(sink_ref, q_ref, km_ref, kp_ref, kc_ref, vm_ref, vp_ref, vc_ref, o_ref):
        negdist, maskbias = _swa_bias(pl.program_id(0))
        k_all = jnp.concatenate([km_ref[...], kp_ref[...], kc_ref[...]], axis=0).astype(BF16)
        v_all = jnp.concatenate([vm_ref[...], vp_ref[...], vc_ref[...]], axis=0).astype(BF16)
        q = [_swa_half(q_ref, pos, DH ** -0.5) for pos in heads]
        t = [_dot_nt(q[pos], k_all) + (2.0 ** -(HEAD_POS[pos] + 1) * negdist + maskbias) for pos in heads]
        p = [_swa_softmax(t[pos], sink_ref[HEAD_POS[pos]])[0].astype(BF16) for pos in heads]
        o = [_dot(p[pos], v_all) for pos in heads]
        for col in range(SWA_HEADS // 2):
            o_ref[:, col * LANE:(col + 1) * LANE] = _swa_merge(o[2 * col], o[2 * col + 1])

    kvw = SWA_KV_HEADS * DH
    return pl.pallas_call(
        body, name="swa_fwd", grid=(nb + 1,),
        in_specs=[pl.BlockSpec(memory_space=pltpu.SMEM), _rows(BLK, SWA_HEADS * DH)] + _swa_kv_specs(kvw) + _swa_kv_specs(kvw),
        out_specs=_rows(BLK, SWA_HEADS * DH),
        out_shape=pltpu.HBM((_lp(), SWA_HEADS * DH), F32),
        compiler_params=_params(16, dimension_semantics=_seq()),
    )(sinks, *_hbm(qs, ks, ks, ks, vs, vs, vs))


GLA_PER_STEP = BLK // CH


def _gla_block(s):
    nb = SEQ // BLK
    return jnp.where(s == 0, nb, s - 1)


def _gla_rowmask(s):
    ri = _iota((BLK, 1), 0)
    m = jnp.where(s == 0, ((ri >= META_OFF) & (ri < CH)).astype(jnp.int32), 1)
    return (m > 0).astype(F32) + jnp.zeros((BLK, 1), F32)


def _gla_chunk_masks():
    r, c = _iota((BLK, BLK), 0), _iota((BLK, BLK), 1)
    same = ((r < CH) & (c < CH)) | ((r >= CH) & (c >= CH))
    return same & (r >= c), same & (r <= c), same


def _gla_decay(z, rmask):
    log_g = (jnp.minimum(z, 0.0) - jnp.log1p(jnp.exp(-jnp.abs(z)))) * (rmask / GLA_TAU)
    lower, _, same = _gla_chunk_masks()
    return _dot_exact(lower.astype(F32), log_g), _dot_exact(same.astype(F32), log_g)


def _gla_slices(c, h):
    return slice(c * CH, (c + 1) * CH), slice(h * DK, (h + 1) * DK), slice(h * DV, (h + 1) * DV)


def _gla_fwd(qg, kg, vg, z):
    steps = SEQ // BLK + 1
    kw, vw = GLA_HEADS * DK, GLA_HEADS * DV
    pairs = [(c, h) for c in range(GLA_PER_STEP) for h in range(GLA_HEADS)]

    def body(q_ref, k_ref, v_ref, z_ref, o_ref, st_ref, st):
        s = pl.program_id(0)

        @pl.when(s == 0)
        def _():
            st[...] = jnp.zeros_like(st)

        rmask = _gla_rowmask(s)
        b, b_last = _gla_decay(z_ref[...], rmask)
        q = q_ref[...] * (rmask * DK ** -0.5)
        k = k_ref[...] * rmask
        v = v_ref[...] * rmask
        qe = q * jnp.exp(b)
        ke = k * jnp.exp(-b)
        kd = k * jnp.exp(b_last - b)
        e_last = jnp.exp(b_last)
        causal = _iota((CH, CH), 0) >= _iota((CH, CH), 1)
        a, upd, intra = {}, {}, {}
        for c, h in pairs:
            rows, ks, vs_ = _gla_slices(c, h)
            a[c, h] = jnp.where(causal, _dot_nt(qe[rows, ks], ke[rows, ks]), 0.0)
            upd[c, h] = _dot_tn(v[rows, vs_], kd[rows, ks])
        for c, h in pairs:
            rows, ks, vs_ = _gla_slices(c, h)
            intra[c, h] = _dot(a[c, h], v[rows, vs_])
        state = st[...]
        for c in range(GLA_PER_STEP):
            st_ref[0, c] = state
            for h in range(GLA_HEADS):
                rows, ks, vs_ = _gla_slices(c, h)
                o_ref[rows, vs_] = intra[c, h] + _dot_nt(qe[rows, ks], state[:, ks])
            state = state * e_last[c * CH:c * CH + 1] + jnp.concatenate([upd[c, h] for h in range(GLA_HEADS)], axis=1)
        st[...] = state

    blk = lambda w: pl.BlockSpec((BLK, w), lambda s: (_gla_block(s), 0))
    return pl.pallas_call(
        body, name="gla_fwd", grid=(steps,),
        in_specs=[blk(kw), blk(kw), blk(vw), blk(kw)],
        out_specs=[blk(vw), pl.BlockSpec((1, GLA_PER_STEP, DV, kw), lambda s: (s, 0, 0, 0))],
        out_shape=[pltpu.HBM((_lp(), vw), F32), pltpu.HBM((steps, GLA_PER_STEP, DV, kw), F32)],
        scratch_shapes=[pltpu.VMEM((DV, kw), F32)],
        compiler_params=_params(16, dimension_semantics=_seq()),
    )(*_hbm(qg, kg, vg, z))


def _post_mix(o_s, o_gla, r_g, h0, gn4, w_out, g1, b1, token):
    tm = _row_tile(384)
    lp = _lp()

    def body(os_ref, og_ref, r_ref, h0_ref, gn_ref, w_ref, g_ref, b_ref, token_ref, o_ref, pre_ref, h1_ref):
        for pos, h in enumerate(HEAD_POS):
            o_ref[:, h * DH:(h + 1) * DH] = os_ref[:, pos * DH:(pos + 1) * DH].astype(BF16)
        for h in range(GLA_HEADS):
            hs = slice(h * DV, (h + 1) * DV)
            xg = og_ref[:, hs]
            n = xg * lax.rsqrt(jnp.mean(xg * xg, axis=-1, keepdims=True) + RMS_EPS) * gn_ref[...]
            r = r_ref[:, hs]
            o_ref[:, 512 + h * DV:512 + (h + 1) * DV] = (n * (r * _sigmoid(r))).astype(BF16)
        pre = ALPHA * h0_ref[...] + _dot(o_ref[...], w_ref[...])
        pre_ref[...] = pre
        xhat, _ = _ln_stats(pre)
        h1_ref[...] = xhat * g_ref[...] + b_ref[...]

    return pl.pallas_call(
        body, name="post_mix", grid=(lp // tm,),
        in_specs=[_rows(tm, 512), _rows(tm, 512), _rows(tm, 512), _rows(tm, D), _const((1, DV)), _const((D, D)),
                  _const((1, D)), _const((1, D)), _const(TOKEN)],
        out_specs=[_rows(tm, D), _rows(tm, D), _rows(tm, D)],
        out_shape=[pltpu.HBM((lp, D), BF16), pltpu.HBM((lp, D), F32),
                   pltpu.HBM((lp, D), F32)],
        compiler_params=_params(32, dimension_semantics=_seq()),
    )(*_hbm(o_s, o_gla, r_g, h0, gn4, w_out, g1, b1), token)


def _ffn_fwd_loss_bwd(h1, wg_t, wu_t, wd, target, g2, b2):
    lp = _lp()
    tm = max(t for t in range(BLK, 384 + 1, BLK) if lp % t == 0)
    steps = lp // tm
    last_blk = SEQ // BLK - 1
    half = D_FF // 2
    n_t = tm // BLK

    def body(*refs):
        h_ref, wg_ref, wu_ref, wd_ref = refs[:4]
        t_refs = refs[4:4 + n_t]
        g2_ref, b2_ref, a_ref, dgate_ref, dup_ref, dp_ref, loss_ref, dg_ref, db_ref, g_s, u_s, acc = refs[4 + n_t:]
        i = pl.program_id(0)

        @pl.when(i == 0)
        def _():
            acc[...] = jnp.zeros_like(acc)
            dg_ref[...] = jnp.zeros_like(dg_ref)
            db_ref[...] = jnp.zeros_like(db_ref)

        h = h_ref[...]
        hb = h.astype(BF16)
        pre = ALPHA * h
        for j in range(2):
            cols = slice(j * half, (j + 1) * half)
            g = _dot_nt(hb, wg_ref[cols, :])
            u = _dot_nt(hb, wu_ref[cols, :])
            g_s[:, cols] = g
            u_s[:, cols] = u
            pre = pre + _dot(g * _sigmoid(g) * u, wd_ref[cols, :])
        xhat, rstd = _ln_stats(pre)
        real = i * tm + _iota((tm, 1), 0) < SEQ
        target_rows = jnp.concatenate([t[...] for t in t_refs], axis=0)
        diff = jnp.where(real, xhat * g2_ref[...] + b2_ref[...] - target_rows, 0.0)
        acc[...] += jnp.sum(diff * diff, axis=0, keepdims=True)
        dy = diff * (1.0 / D)
        dpre = _ln_bwd(dy, xhat, rstd, g2_ref[...])
        dp_ref[...] = dpre
        dg_ref[...] += jnp.sum(dy * xhat, axis=0, keepdims=True)
        db_ref[...] += jnp.sum(dy, axis=0, keepdims=True)
        dpb = dpre.astype(BF16)
        for j in range(2):
            cols = slice(j * half, (j + 1) * half)
            g, u = g_s[:, cols], u_s[:, cols]
            sg = _sigmoid(g)
            silu = g * sg
            da = _dot_nt(dpb, wd_ref[cols, :])
            a_ref[:, cols] = (silu * u).astype(BF16)
            dgate_ref[:, cols] = (da * u * (sg * (1.0 + g * (1.0 - sg)))).astype(BF16)
            dup_ref[:, cols] = (da * silu).astype(BF16)

        @pl.when(i == steps - 1)
        def _():
            loss_ref[...] = jnp.zeros_like(loss_ref) + (0.5 / D) * jnp.sum(acc[...], axis=1, keepdims=True)

    t_spec = lambda k: pl.BlockSpec((BLK, D), lambda i: (jnp.minimum(i * n_t + k, last_blk), 0))
    return pl.pallas_call(
        body, name="ffn_fwd_loss_bwd", grid=(steps,),
        in_specs=[_rows(tm, D), _const((D_FF, D)), _const((D_FF, D)), _const((D_FF, D))]
        + [t_spec(k) for k in range(n_t)] + [_const((1, D)), _const((1, D))],
        out_specs=[_rows(tm, D_FF), _rows(tm, D_FF), _rows(tm, D_FF), _rows(tm, D), _acc((1, LANE)), _acc((1, D)),
                   _acc((1, D))],
        out_shape=[pltpu.HBM((lp, D_FF), BF16)] * 3 + [pltpu.HBM((lp, D), F32), pltpu.HBM((1, LANE), F32),
                                                         pltpu.HBM((1, D), F32), pltpu.HBM((1, D), F32)],
        scratch_shapes=[pltpu.VMEM((tm, D_FF), F32), pltpu.VMEM((tm, D_FF), F32), pltpu.VMEM((1, D), F32)],
        compiler_params=_params(58, dimension_semantics=_seq()),
    )(*_hbm(h1, wg_t, wu_t, wd, *[target] * n_t, g2, b2))


def _ffn_out_bwd(dpre2, dgate, dup, pre1, wg_t, wu_t, g1, w_out, o_gla, r_g, gn4):
    tm = _row_tile(384)
    lp = _lp()

    def body(dp_ref, dg_ref, du_ref, p1_ref, wg_ref, wu_ref, g1_ref, w_ref, og_ref, r_ref, gn_ref,
             dp1_ref, dg1_ref, db1_ref, dos_ref, dog_ref, dr_ref, dgn_ref):
        @pl.when(pl.program_id(0) == 0)
        def _():
            for acc_ref in (dg1_ref, db1_ref, dgn_ref):
                acc_ref[...] = jnp.zeros_like(acc_ref)

        dh1 = ALPHA * dp_ref[...] + _dot(dg_ref[...], wg_ref[...]) + _dot(du_ref[...], wu_ref[...])
        xhat, rstd1 = _ln_stats(p1_ref[...])
        dpre1 = _ln_bwd(dh1, xhat, rstd1, g1_ref[...])
        dp1_ref[...] = dpre1
        dg1_ref[...] += jnp.sum(dh1 * xhat, axis=0, keepdims=True)
        db1_ref[...] += jnp.sum(dh1, axis=0, keepdims=True)

        do = _dot_nt(dpre1, w_ref[...])
        for pos, h in enumerate(HEAD_POS):
            dos_ref[:, pos * DH:(pos + 1) * DH] = do[:, h * DH:(h + 1) * DH]
        gn = gn_ref[...]
        for h in range(GLA_HEADS):
            hs = slice(h * DV, (h + 1) * DV)
            xg = og_ref[:, hs]
            rstd = lax.rsqrt(jnp.mean(xg * xg, axis=-1, keepdims=True) + RMS_EPS)
            nx = xg * rstd
            r = r_ref[:, hs]
            sr = _sigmoid(r)
            d_o = do[:, 512 + h * DV:512 + (h + 1) * DV]
            dr_ref[:, hs] = d_o * (nx * gn) * (sr * (1.0 + r * (1.0 - sr)))
            dn = d_o * (r * sr)
            dgn_ref[...] += jnp.sum(dn * nx, axis=0, keepdims=True)
            dnx = dn * gn
            dog_ref[:, hs] = rstd * (dnx - nx * jnp.mean(dnx * nx, axis=-1, keepdims=True))

    return pl.pallas_call(
        body, name="ffn_out_bwd", grid=(lp // tm,),
        in_specs=[_rows(tm, D), _rows(tm, D_FF), _rows(tm, D_FF), _rows(tm, D), _const((D_FF, D)), _const((D_FF, D)),
                  _const((1, D)), _const((D, D)), _rows(tm, 512), _rows(tm, 512), _const((1, DV))],
        out_specs=[_rows(tm, D), _acc((1, D)), _acc((1, D)), _rows(tm, 512), _rows(tm, 512), _rows(tm, 512),
                   _acc((1, DV))],
        out_shape=[pltpu.HBM((lp, D), F32), pltpu.HBM((1, D), F32), pltpu.HBM((1, D), F32)]
        + [pltpu.HBM((lp, 512), F32)] * 3 + [pltpu.HBM((1, DV), F32)],
        compiler_params=_params(48, dimension_semantics=_seq()),
    )(*_hbm(dpre2, dgate, dup, pre1, wg_t, wu_t, g1, w_out, o_gla, r_g, gn4))


def _atb(a, b, name, token=None, windows=None):
    lp = _lp()
    tm = _row_tile(1408)
    n, w = a.shape[1], b.shape[1]
    bw = 512 if n * w * 4 > (4 << 20) else w
    tokens = [] if token is None else [token]
    steps = lp // tm

    def body(a_ref, b_ref, *rest):
        o_ref, acc_ref = rest[len(tokens):] if windows else (rest[-1], rest[-1])

        @pl.when(pl.program_id(1) == 0)
        def _():
            acc_ref[...] = jnp.zeros_like(acc_ref)

        acc_ref[...] += _dot_tn(a_ref[...], b_ref[...])

        if windows:
            @pl.when(pl.program_id(1) == steps - 1)
            def _():
                for s, start in enumerate(windows[0]):
                    o_ref[s] = acc_ref[start:start + windows[1], :]

    if windows:
        count, height = len(windows[0]), windows[1]
        out_spec, out_shape = pl.BlockSpec((count, height, bw), lambda j, k: (0, 0, j)), (count, height, w)
    else:
        out_spec, out_shape = pl.BlockSpec((n, bw), lambda j, k: (0, j)), (n, w)
    return pl.pallas_call(
        body, name=name, grid=(w // bw, steps),
        in_specs=[pl.BlockSpec((tm, n), lambda j, k: (k, 0)), pl.BlockSpec((tm, bw), lambda j, k: (k, j))]
        + [_const(TOKEN)] * len(tokens),
        out_specs=out_spec, out_shape=pltpu.HBM(out_shape, F32),
        scratch_shapes=[pltpu.VMEM((n, bw), F32)] if windows else [],
        compiler_params=_params(48, dimension_semantics=_seq(2)),
    )(*_hbm(a, b), *tokens)


def _gla_bwd(qg, kg, vg, z, do_gla, st_all, token):
    steps = SEQ // BLK + 1
    kw, vw = GLA_HEADS * DK, GLA_HEADS * DV
    pairs = [(c, h) for c in range(GLA_PER_STEP) for h in range(GLA_HEADS)]
    heads = range(GLA_HEADS)

    def body(q_ref, k_ref, v_ref, z_ref, do_ref, st_ref, token_ref, dq_ref, dk_ref, dv_ref, dz_ref, dst):
        @pl.when(pl.program_id(0) == 0)
        def _():
            dst[...] = jnp.zeros_like(dst)

        rmask = _gla_rowmask(steps - 1 - pl.program_id(0))
        zz = z_ref[...]
        b, b_last = _gla_decay(zz, rmask)
        e_b, e_nb, e_kd, e_last = jnp.exp(b), jnp.exp(-b), jnp.exp(b_last - b), jnp.exp(b_last)
        q = q_ref[...] * (rmask * DK ** -0.5)
        k = k_ref[...] * rmask
        v = v_ref[...] * rmask
        qe, ke, kd = q * e_b, k * e_nb, k * e_kd
        d_o = do_ref[...]
        causal = _iota((CH, CH), 0) >= _iota((CH, CH), 1)
        a, da, dqe, dke, dv_intra, carry = {}, {}, {}, {}, {}, {}
        for c, h in pairs:
            rows, ks, vs_ = _gla_slices(c, h)
            a[c, h] = jnp.where(causal, _dot_nt(qe[rows, ks], ke[rows, ks]), 0.0)
            da[c, h] = jnp.where(causal, _dot_nt(d_o[rows, vs_], v[rows, vs_]), 0.0)
            carry[c, h] = _dot_tn(d_o[rows, vs_], qe[rows, ks])
        for c, h in pairs:
            rows, ks, vs_ = _gla_slices(c, h)
            dqe[c, h] = _dot(d_o[rows, vs_], st_ref[0, c][:, ks]) + _dot(da[c, h], ke[rows, ks])
            dke[c, h] = _dot_tn(da[c, h], qe[rows, ks])
            dv_intra[c, h] = _dot_tn(a[c, h], d_o[rows, vs_])
        dstate = dst[...]
        dkd, db_decay = {}, {}
        for c in reversed(range(GLA_PER_STEP)):
            for h in heads:
                rows, ks, vs_ = _gla_slices(c, h)
                dkd[c, h] = _dot(v[rows, vs_], dstate[:, ks])
                dv_ref[rows, vs_] = dv_intra[c, h] + _dot_nt(kd[rows, ks], dstate[:, ks])
            chunk_last = e_last[c * CH:c * CH + 1]
            db_decay[c] = jnp.sum(dstate * st_ref[0, c], axis=0, keepdims=True) * chunk_last
            dstate = dstate * chunk_last + jnp.concatenate([carry[c, h] for h in heads], axis=1)
        dst[...] = dstate
        rows_of = lambda parts: jnp.concatenate(
            [jnp.concatenate([parts[c, h] for h in heads], axis=1) for c in range(GLA_PER_STEP)], axis=0)
        dqe_all, dke_all, dkd_all = rows_of(dqe), rows_of(dke), rows_of(dkd)
        dq_ref[...] = dqe_all * e_b * (rmask * DK ** -0.5)
        dk_ref[...] = (dke_all * e_nb + dkd_all * e_kd) * rmask
        dkd_kd = dkd_all * kd
        db = dqe_all * qe - dke_all * ke - dkd_kd
        _, upper, same = _gla_chunk_masks()
        decay_rows = jnp.concatenate([jnp.broadcast_to(db_decay[c], (CH, kw)) for c in range(GLA_PER_STEP)], axis=0)
        dlog_g = _dot_exact(upper.astype(F32), db) + _dot_exact(same.astype(F32), dkd_kd) + decay_rows
        dz_ref[...] = dlog_g * (rmask / GLA_TAU) * _sigmoid(-zz)

    blk = lambda w: pl.BlockSpec((BLK, w), lambda s: (_gla_block(steps - 1 - s), 0))
    return pl.pallas_call(
        body, name="gla_bwd", grid=(steps,),
        in_specs=[blk(kw), blk(kw), blk(vw), blk(kw), blk(vw),
                  pl.BlockSpec((1, GLA_PER_STEP, DV, kw), lambda s: (steps - 1 - s, 0, 0, 0)), _const(TOKEN)],
        out_specs=[blk(kw), blk(kw), blk(vw), blk(kw)],
        out_shape=[pltpu.HBM((_lp(), kw), F32), pltpu.HBM((_lp(), kw), F32),
                   pltpu.HBM((_lp(), vw), F32), pltpu.HBM((_lp(), kw), F32)],
        scratch_shapes=[pltpu.VMEM((DV, kw), F32)],
        compiler_params=_params(16, dimension_semantics=_seq()),
    )(*_hbm(qg, kg, vg, z, do_gla, st_all), token)


def _swa_bwd(sinks, qs, ks, vs, do_s, token):
    nb = SEQ // BLK
    kvw = SWA_KV_HEADS * DH
    scale = DH ** -0.5
    heads = range(SWA_HEADS)

    def body(sink_ref, q_ref, km_ref, kp_ref, kc_ref, vm_ref, vp_ref, vc_ref, do_ref, token_ref,
             dq_ref, dk_ref, dv_ref, dsink_ref, carry_k, carry_v, meta_k, meta_v):
        n = pl.program_id(0)

        @pl.when(n == 0)
        def _():
            for r in (carry_k, carry_v, meta_k, meta_v):
                r[...] = jnp.zeros_like(r)
            dsink_ref[...] = jnp.zeros_like(dsink_ref)

        @pl.when(n <= nb)
        def _():
            negdist, maskbias = _swa_bias(n)
            lane = _iota((1, LANE), 1)
            k_all = jnp.concatenate([km_ref[...], kp_ref[...], kc_ref[...]], axis=0).astype(BF16)
            v_all = jnp.concatenate([vm_ref[...], vp_ref[...], vc_ref[...]], axis=0).astype(BF16)
            q = [_swa_half(q_ref, pos, scale) for pos in heads]
            d_o = [_swa_half(do_ref, pos) for pos in heads]
            t = [_dot_nt(q[pos], k_all) + (2.0 ** -(HEAD_POS[pos] + 1) * negdist + maskbias) for pos in heads]
            dp = [_dot_nt(d_o[pos], v_all) for pos in heads]
            soft = [_swa_softmax(t[pos], sink_ref[HEAD_POS[pos]]) for pos in heads]
            p = [s[0] for s in soft]
            delta = [jnp.sum(p[pos] * dp[pos], axis=-1, keepdims=True) for pos in heads]
            ds = [(p[pos] * (dp[pos] - delta[pos])).astype(BF16) for pos in heads]
            dq = [_dot(ds[pos], k_all) for pos in heads]
            for col in range(SWA_HEADS // 2):
                dq_ref[:, col * LANE:(col + 1) * LANE] = scale * _swa_merge(dq[2 * col], dq[2 * col + 1])
            dsink = jnp.zeros((1, LANE), F32)
            for pos in heads:
                dsink = dsink + jnp.where(lane == HEAD_POS[pos],
                                          -jnp.sum(soft[pos][1] * delta[pos], axis=0, keepdims=True), 0.0)
            dsink_ref[...] += dsink
            dk3 = _dot_tn(jnp.concatenate(q, axis=0), jnp.concatenate(ds, axis=0)).T
            dv3 = _dot_tn(jnp.concatenate(d_o, axis=0), jnp.concatenate([x.astype(BF16) for x in p], axis=0)).T
            meta_k[...] += dk3[0:BLK]
            meta_v[...] += dv3[0:BLK]
            dk_ref[...] = carry_k[...] + dk3[BLK:2 * BLK]
            dv_ref[...] = carry_v[...] + dv3[BLK:2 * BLK]
            carry_k[...] = dk3[2 * BLK:3 * BLK]
            carry_v[...] = dv3[2 * BLK:3 * BLK]

        @pl.when(n == nb + 1)
        def _():
            dk_ref[...] = meta_k[...]
            dv_ref[...] = meta_v[...]

    kv_out = pl.BlockSpec((BLK, kvw), lambda n: (jnp.where(n == nb + 1, nb, jnp.clip(n - 1, 0, nb - 1)), 0))
    qblk = pl.BlockSpec((BLK, SWA_HEADS * DH), lambda n: (jnp.minimum(n, nb), 0))
    return pl.pallas_call(
        body, name="swa_bwd", grid=(nb + 2,),
        in_specs=[pl.BlockSpec(memory_space=pltpu.SMEM), qblk] + _swa_kv_specs(kvw) + _swa_kv_specs(kvw)
        + [qblk, _const(TOKEN)],
        out_specs=[qblk, kv_out, kv_out, _acc((1, LANE))],
        out_shape=[pltpu.HBM((_lp(), SWA_HEADS * DH), F32), pltpu.HBM((_lp(), kvw), F32),
                   pltpu.HBM((_lp(), kvw), F32), pltpu.HBM((1, LANE), F32)],
        scratch_shapes=[pltpu.VMEM((BLK, kvw), F32)] * 4,
        compiler_params=_params(16, dimension_semantics=_seq()),
    )(sinks, *_hbm(qs, ks, ks, ks, vs, vs, vs, do_s), token)


def _in_bwd(dqs, dks, dvs, dqg, dkg, dvg, drg, dz, dpre1, w_in_t, wg2_p):
    tm = _row_tile(384)
    lp = _lp()
    widths = (512, 128, 128, 256, 256, 512, 512)
    offs = (O_QS, O_KS, O_VS, O_QG, O_KG, O_VG, O_RG)

    def body(*refs):
        parts, (dz_ref, dp1_ref, w_ref, wg2_ref, dproj_ref, dh0_ref, dbin_ref, dbg_ref) = refs[:7], refs[7:]

        @pl.when(pl.program_id(0) == 0)
        def _():
            dbin_ref[...] = jnp.zeros_like(dbin_ref)
            dbg_ref[...] = jnp.zeros_like(dbg_ref)

        for pos, h in enumerate(HEAD_POS):
            val = parts[0][:, pos * DH:(pos + 1) * DH]
            dproj_ref[:, O_QS + h * DH:O_QS + (h + 1) * DH] = val.astype(BF16)
            dbin_ref[:, O_QS + h * DH:O_QS + (h + 1) * DH] += jnp.sum(val, axis=0, keepdims=True)
        for p_ref, off, wd in zip(parts[1:], offs[1:], widths[1:]):
            val = p_ref[...]
            dproj_ref[:, off:off + wd] = val.astype(BF16)
            dbin_ref[:, off:off + wd] += jnp.sum(val, axis=0, keepdims=True)
        dz = dz_ref[...]
        dlr = _dot_nt(dz, wg2_ref[...])
        dproj_ref[:, O_LR:O_LR + LANE] = dlr.astype(BF16)
        dbin_ref[:, O_LR:O_LR + LANE] += jnp.sum(dlr, axis=0, keepdims=True)
        dbg_ref[...] += jnp.sum(dz, axis=0, keepdims=True)
        dh0_ref[...] = ALPHA * dp1_ref[...] + _dot(dproj_ref[...], w_ref[...])

    return pl.pallas_call(
        body, name="in_bwd", grid=(lp // tm,),
        in_specs=[_rows(tm, w) for w in widths] + [_rows(tm, 256), _rows(tm, D), _const((D_IN_P, D)), _const((LANE, 256))],
        out_specs=[_rows(tm, D_IN_P), _rows(tm, D), _acc((1, D_IN_P)), _acc((1, 256))],
        out_shape=[pltpu.HBM((lp, D_IN_P), BF16), pltpu.HBM((lp, D), F32),
                   pltpu.HBM((1, D_IN_P), F32), pltpu.HBM((1, 256), F32)],
        compiler_params=_params(40, dimension_semantics=_seq()),
    )(*_hbm(dqs, dks, dvs, dqg, dkg, dvg, drg, dz, dpre1, w_in_t, wg2_p))


def _ln_in_bwd(x, meta_ext, dh0, g, token):
    tr = min(LN_ROWS, SEQ)

    def ln_bwd(x_ref, dh_ref, g_ref, dx_ref, dg_ref, db_ref):
        @pl.when(pl.program_id(0) == 0)
        def _():
            dg_ref[...] = jnp.zeros_like(dg_ref)
            db_ref[...] = jnp.zeros_like(db_ref)

        xhat, rstd = _ln_stats(x_ref[...])
        dh = dh_ref[...]
        dx_ref[...] = _ln_bwd(dh, xhat, rstd, g_ref[...])
        dg_ref[...] += jnp.sum(dh * xhat, axis=0, keepdims=True)
        db_ref[...] += jnp.sum(dh, axis=0, keepdims=True)

    def body(x_ref, dh_ref, g_ref, token_ref, dx_ref, dg_ref, db_ref):
        ln_bwd(x_ref, dh_ref, g_ref, dx_ref, dg_ref, db_ref)

    def meta_body(m_ref, dh_ref, g_ref, dm_ref, dg_ref, db_ref):
        ln_bwd(m_ref, dh_ref, g_ref, dm_ref, dg_ref, db_ref)

    sums = [pltpu.HBM((1, D), F32), pltpu.HBM((1, D), F32)]
    dx, dg, db = pl.pallas_call(
        body, name="ln_in_bwd", grid=(SEQ // tr,),
        in_specs=[_rows(tr, D), _rows(tr, D), _const((1, D)), _const(TOKEN)],
        out_specs=[_rows(tr, D), _acc((1, D)), _acc((1, D))],
        out_shape=[pltpu.HBM((SEQ, D), F32)] + sums,
        compiler_params=_params(32, dimension_semantics=_seq()),
    )(*_hbm(x, dh0, g), token)
    dm, dg_m, db_m = pl.pallas_call(
        meta_body, name="ln_in_bwd_meta", grid=(1,),
        in_specs=[_const((BLK, D)), pl.BlockSpec((BLK, D), lambda i: (SEQ // BLK, 0)), _const((1, D))],
        out_specs=[_acc((BLK, D)), _acc((1, D)), _acc((1, D))],
        out_shape=[pltpu.HBM((BLK, D), F32)] + sums,
        compiler_params=_params(16, dimension_semantics=_seq()),
    )(*_hbm(meta_ext, dh0, g))
    return dx, dm, dg + dg_m, db + db_m


def _local_step(x, target, ln_in_g, ln_in_b, b_in, bg2, sinks, gn, g1, b1, g2, b2,
                token, fetch_first, fetch_rest, fetch_ffn, exchange_ffn, ship_ffn, ship_w_in):
    row = lambda v: v.reshape(1, -1).astype(F32)
    b_in_p = jnp.pad(row(b_in), ((0, 0), (0, D_IN_P - D_IN)))
    gn4 = row(gn)
    sinks = sinks.reshape(-1).astype(F32)

    h_real = _ln_in_fwd_real(x, row(ln_in_g), row(ln_in_b), token)
    w_in_windows, meta_full, wg2 = fetch_first([h_real])
    meta_ext = jnp.pad(meta_full, ((META_OFF, BLK - CH), (0, 0)))
    wg2_p = jnp.pad(wg2, ((0, LANE - wg2.shape[0]), (0, 0))).astype(BF16)
    h0 = _ln_in_fwd_meta(h_real, meta_ext, row(ln_in_g), row(ln_in_b))
    qs, ks, vs, qg, kg, vg, rg, glr, z, w_in_t = _in_proj(h0, w_in_windows, b_in_p, wg2_p, row(bg2))
    o_s = _swa_fwd(sinks, qs, ks, vs)
    o_gla, st_all = _gla_fwd(qg, kg, vg, z)
    w_out, token = fetch_rest([o_s, o_gla])
    o, pre1, h1 = _post_mix(o_s, o_gla, rg, h0, gn4, w_out, row(g1), row(b1), token)
    wg_t, wu_t, wd = fetch_ffn([pre1])
    a, dgate, dup, dpre2, loss, dg2, db2 = _ffn_fwd_loss_bwd(h1, wg_t, wu_t, wd, target, row(g2), row(b2))
    dpre1, dg1, db1, do_s, do_gla, drg, dgn = _ffn_out_bwd(dpre2, dgate, dup, pre1, wg_t, wu_t, row(g1), w_out, o_gla,
                                                           rg, gn4)
    dwd = _atb(a, dpre2, "dw_down")
    dwg_t = _atb(dgate, h1, "dw_gate")
    dwu_t = _atb(dup, h1, "dw_up")
    token = exchange_ffn(dict(w_g=dwg_t, w_u=dwu_t, w_d=dwd))
    token = ship_ffn(_atb(o, dpre1, "dw_out", token))
    dqg, dkg, dvg, dz = _gla_bwd(qg, kg, vg, z, do_gla, st_all, token)
    dqs, dks, dvs, dsinks = _swa_bwd(sinks, qs, ks, vs, do_s, token)
    dproj, dh0, db_in_p, dbg2 = _in_bwd(dqs, dks, dvs, dqg, dkg, dvg, drg, dz, dpre1, w_in_t, wg2_p)
    token = ship_w_in(_atb(dproj, h0, "dw_in", windows=(W_IN_STARTS, W_IN_WIN)))
    dwg2_p = _atb(glr, dz, "dw_gate_lr2")
    dx, dmeta_blk, dg_in, db_in_ln = _ln_in_bwd(x, meta_ext, dh0, row(ln_in_g), token)

    small = dict(meta_blk=dmeta_blk, ln_in_g=dg_in, ln_in_b=db_in_ln, ln1_g=dg1, ln1_b=db1, ln2_g=dg2, ln2_b=db2,
                 b_in_p=db_in_p, wg2_p=dwg2_p, bg2=dbg2, sinks=dsinks, gn=dgn, loss=loss)
    return dx, small


HBM = pl.BlockSpec(memory_space=pltpu.HBM)


def _place():
    return lax.axis_index("x"), lax.axis_index("y"), lax.axis_index("c")


def _other_chips(x, y):
    return [(1 - x, y), (x, 1 - y), (1 - x, 1 - y)]


def _dma_sems(n):
    return pltpu.SemaphoreType.DMA((n,))


def _comm_params():
    return pltpu.CompilerParams(has_side_effects=True)


SEM = pl.BlockSpec(memory_space=pltpu.SEMAPHORE)


PER_ARRAY = dict(gather=3, scatter=3, sibling=N_CHIPS)


def _ici_copies(kind, landing, srcs, lands, send_sems, recv_sems):
    x, y, c = _place()
    mine = 2 * x + y
    copies = []
    for a in range(len(srcs)):
        if kind == "sibling":
            for s in range(N_CHIPS):
                copies.append(pltpu.make_async_remote_copy(
                    srcs[a].at[s, 1 - c], lands[a].at[s], send_sems.at[N_CHIPS * a + s], recv_sems.at[N_CHIPS * a + s],
                    device_id=(x, y, 1 - c), device_id_type=MESH))
            continue
        for j, (px, py) in enumerate(_other_chips(x, y)):
            slab = 2 * px + py if landing else mine
            if kind == "gather":
                src, dst = srcs[a].at[c], lands[a].at[slab, c]
            else:
                src, dst = srcs[a].at[2 * px + py], lands[a].at[slab]
            copies.append(pltpu.make_async_remote_copy(src, dst, send_sems.at[3 * a + j], recv_sems.at[3 * a + j],
                                                       device_id=(px, py, c), device_id_type=MESH))
    return copies


def _split_params():
    return pltpu.CompilerParams(has_side_effects=pltpu.SideEffectType.DATAFLOW_SIDE_EFFECTING)


def _ici_start(kind, srcs, land_shapes, after, name):
    n = len(srcs)
    lands = [pltpu.with_memory_space_constraint(lax.empty(s, a.dtype), pltpu.HBM) for s, a in zip(land_shapes, srcs)]

    def body(*refs):
        outs = refs[2 * n + len(after):]
        for cp in _ici_copies(kind, False, refs[:n], refs[n:2 * n], outs[0], outs[1]):
            cp.start()
        outs[-1][...] = jnp.zeros(TOKEN, F32)

    outs = pl.pallas_call(
        body, name=name, in_specs=[HBM] * (2 * n) + [pl.BlockSpec(memory_space=pl.ANY)] * len(after),
        out_specs=[SEM, SEM] + [HBM] * (2 * n) + [pl.BlockSpec(memory_space=pltpu.VMEM)],
        out_shape=[_dma_sems(PER_ARRAY[kind] * n)] * 2 + [pltpu.HBM(a.shape, a.dtype) for a in list(srcs) + lands]
        + [jax.ShapeDtypeStruct(TOKEN, F32)],
        input_output_aliases={i: 2 + i for i in range(2 * n)},
        compiler_params=_split_params(),
    )(*_hbm(*srcs), *lands, *after)
    return outs[:-1], outs[-1]


def _ici_wait(kind, handle, after, name):
    n = (len(handle) - 2) // 2

    def body(*refs):
        for cp in _ici_copies(kind, True, refs[:n], refs[n:2 * n], refs[2 * n], refs[2 * n + 1]):
            cp.wait_send()
            cp.wait_recv()

    outs = pl.pallas_call(
        body, name=name, in_specs=[HBM] * (2 * n) + [SEM, SEM] + [pl.BlockSpec(memory_space=pl.ANY)] * len(after),
        out_specs=[HBM] * (2 * n), out_shape=[pltpu.HBM(a.shape, a.dtype) for a in handle[2:]],
        input_output_aliases={i: i for i in range(2 * n)},
        compiler_params=_split_params(),
    )(*handle[2:], handle[0], handle[1], *after)
    return list(outs[:n]), list(outs[n:])


def _forward_copies(landing, arrs, send_sems, recv_sems):
    x, y, c = _place()
    copies = []
    for a in range(len(arrs)):
        for j, (px, py) in enumerate(_other_chips(x, y)):
            half = 1 - c if landing else c
            copies.append(pltpu.make_async_remote_copy(
                arrs[a].at[2 * px + py, c], arrs[a].at[2 * px + py, half], send_sems.at[3 * a + j],
                recv_sems.at[3 * a + j], device_id=(x, y, 1 - c), device_id_type=MESH))
    return copies


def _sibling_forward(lands, name):
    n = len(lands)

    def body(*refs):
        outs = refs[n:2 * n]
        send_sems, recv_sems = refs[2 * n:]
        sends = _forward_copies(False, outs, send_sems, recv_sems)
        for cp in sends:
            cp.start()
        for cp in _forward_copies(True, outs, send_sems, recv_sems):
            cp.wait_recv()
        for cp in sends:
            cp.wait_send()

    return pl.pallas_call(
        body, name=name, in_specs=[HBM] * n, out_specs=[HBM] * n,
        out_shape=[pltpu.HBM(a.shape, a.dtype) for a in lands],
        input_output_aliases={a: a for a in range(n)},
        scratch_shapes=[_dma_sems(3 * n)] * 2,
        compiler_params=_comm_params(),
    )(*_hbm(*lands))


def _forward_start(lands, name):
    n = len(lands)

    def body(*refs):
        outs = refs[n:]
        for cp in _forward_copies(False, refs[:n], outs[0], outs[1]):
            cp.start()
        outs[-1][...] = jnp.zeros(TOKEN, F32)

    outs = pl.pallas_call(
        body, name=name, in_specs=[HBM] * n,
        out_specs=[SEM, SEM] + [HBM] * n + [pl.BlockSpec(memory_space=pltpu.VMEM)],
        out_shape=[_dma_sems(3 * n)] * 2 + [pltpu.HBM(a.shape, a.dtype) for a in lands]
        + [jax.ShapeDtypeStruct(TOKEN, F32)],
        input_output_aliases={i: 2 + i for i in range(n)},
        compiler_params=_split_params(),
    )(*_hbm(*lands))
    return outs[:-1], outs[-1]


def _forward_wait(handle, after, name):
    n = len(handle) - 2

    def body(*refs):
        for cp in _forward_copies(True, refs[:n], refs[n], refs[n + 1]):
            cp.wait_send()
            cp.wait_recv()

    return list(pl.pallas_call(
        body, name=name, in_specs=[HBM] * n + [SEM, SEM] + [pl.BlockSpec(memory_space=pl.ANY)] * len(after),
        out_specs=[HBM] * n, out_shape=[pltpu.HBM(a.shape, a.dtype) for a in handle[2:]],
        input_output_aliases={i: i for i in range(n)},
        compiler_params=_split_params(),
    )(*handle[2:], handle[0], handle[1], *after))


def _sibling_exchange(grads, name):
    n = len(grads)

    def body(*refs):
        ins, outs = refs[:n], refs[n:2 * n]
        send_sems, recv_sems = refs[2 * n:]
        x, y, c = _place()
        copies = []
        for a in range(n):
            for s in range(N_CHIPS):
                cp = pltpu.make_async_remote_copy(ins[a].at[s, 1 - c], outs[a].at[s], send_sems.at[N_CHIPS * a + s],
                                                  recv_sems.at[N_CHIPS * a + s], device_id=(x, y, 1 - c),
                                                  device_id_type=MESH)
                cp.start()
                copies.append(cp)
        for cp in copies:
            cp.wait_recv()
        for cp in copies:
            cp.wait_send()

    return pl.pallas_call(
        body, name=name, in_specs=[HBM] * n, out_specs=[HBM] * n,
        out_shape=[pltpu.HBM((N_CHIPS, g.shape[2], D), F32) for g in grads],
        scratch_shapes=[_dma_sems(N_CHIPS * n)] * 2,
        compiler_params=_comm_params(),
    )(*_hbm(*grads))


def _add_halves(core, grads, recvs, dtypes, name):
    n = len(grads)
    heights = [g.shape[2] for g in grads]

    def body(c_ref, *refs):
        for a in range(n):
            refs[2 * n + a][...] = (refs[2 * a][0] + refs[2 * a + 1][...]).astype(dtypes[a])

    slab = lambda h: pl.BlockSpec((1, h, D), lambda s, c: (s, 0, 0))
    mine = lambda h: pl.BlockSpec((1, 1, h, D), lambda s, c: (s, c[0], 0, 0))
    return pl.pallas_call(
        body, name=name,
        grid_spec=pltpu.PrefetchScalarGridSpec(
            num_scalar_prefetch=1, grid=(N_CHIPS,),
            in_specs=[spec(h) for h in heights for spec in (mine, slab)], out_specs=[slab(h) for h in heights]),
        out_shape=[pltpu.HBM((N_CHIPS, h, D), dt) for h, dt in zip(heights, dtypes)],
        compiler_params=_params(32, dimension_semantics=_seq()),
    )(core, *_hbm(*[a for pair in zip(grads, recvs) for a in pair]))


N_DEVICES = 2 * N_CHIPS
PEER_FLIPS = [(dx, dy, dc) for dx in (0, 1) for dy in (0, 1) for dc in (0, 1)][1:]


def _small_copies(landing, p_ref, out_ref, send_sems, recv_sems):
    x, y, c = _place()
    flip = lambda v, d: 1 - v if d else v
    copies = []
    for k, flips in enumerate(PEER_FLIPS):
        px, py, pc = (flip(v, d) for v, d in zip((x, y, c), flips))
        slab = 4 * px + 2 * py + pc if landing else 4 * x + 2 * y + c
        copies.append(pltpu.make_async_remote_copy(p_ref, out_ref.at[slab], send_sems.at[k], recv_sems.at[k],
                                                   device_id=(px, py, pc), device_id_type=MESH))
    return copies


def _small_start(pack, after):
    n = len(PEER_FLIPS)
    land = pltpu.with_memory_space_constraint(lax.empty((N_DEVICES,) + pack.shape, F32), pltpu.HBM)

    def body(p_ref, land_ref, *refs):
        outs = refs[len(after):]
        for cp in _small_copies(False, p_ref, land_ref, outs[0], outs[1]):
            cp.start()
        outs[-1][...] = jnp.zeros(TOKEN, F32)

    outs = pl.pallas_call(
        body, name="small_exchange_start", in_specs=[HBM, HBM] + [pl.BlockSpec(memory_space=pl.ANY)] * len(after),
        out_specs=[SEM, SEM, HBM, HBM, pl.BlockSpec(memory_space=pltpu.VMEM)],
        out_shape=[_dma_sems(n), _dma_sems(n), pltpu.HBM(pack.shape, F32), pltpu.HBM(land.shape, F32),
                   jax.ShapeDtypeStruct(TOKEN, F32)],
        input_output_aliases={0: 2, 1: 3},
        compiler_params=_split_params(),
    )(*_hbm(pack), land, *after)
    return outs[:-1], outs[-1]


def _small_wait(handle, after):
    def body(p_ref, land_ref, send_sems, recv_sems, *rest):
        for cp in _small_copies(True, p_ref, land_ref, send_sems, recv_sems):
            cp.wait_send()
            cp.wait_recv()

    return pl.pallas_call(
        body, name="small_exchange_wait", in_specs=[HBM, HBM, SEM, SEM] + [pl.BlockSpec(memory_space=pl.ANY)] * len(after),
        out_specs=[HBM, HBM], out_shape=[pltpu.HBM(a.shape, F32) for a in handle[2:]],
        input_output_aliases={0: 0, 1: 1},
        compiler_params=_split_params(),
    )(handle[2], handle[3], handle[0], handle[1], *after)


def _sum_chips(slots, firsts, rests, after):
    n = len(firsts)

    def body(i_ref, *refs):
        outs = refs[4 * n + len(after):]
        for a in range(n):
            first, r1, r2, r3 = refs[4 * a:4 * a + 4]
            outs[a][...] = ((first[...].astype(F32) + r1[...].astype(F32)) + r2[...].astype(F32)) + r3[...].astype(F32)

    slab = lambda h, k: pl.BlockSpec((1, h, D), lambda i, ix: (ix[k], 0, 0))
    heights = [f.shape[1] for f in firsts]
    return pl.pallas_call(
        body, name="sum_chips",
        grid_spec=pltpu.PrefetchScalarGridSpec(
            num_scalar_prefetch=1, grid=(1,),
            in_specs=[slab(h, k) for h in heights for k in range(4)] + [pl.BlockSpec(memory_space=pl.ANY)] * len(after),
            out_specs=[slab(h, 4) for h in heights]),
        out_shape=[pltpu.HBM((2, h, D), F32) for h in heights],
        compiler_params=_params(48, dimension_semantics=_seq()),
    )(slots, *_hbm(*[a for f, r in zip(firsts, rests) for a in (f, r, r, r)]), *after)


def _join_halves(halves):
    n = len(halves)

    def body(*refs):
        outs = refs[n:2 * n]
        send_sems, recv_sems = refs[2 * n:]
        x, y, c = _place()

        def copy(a, slab):
            return pltpu.make_async_remote_copy(outs[a].at[slab], outs[a].at[slab], send_sems.at[a], recv_sems.at[a],
                                                device_id=(x, y, 1 - c), device_id_type=MESH)

        for a in range(n):
            copy(a, c).start()
        for a in range(n):
            copy(a, 1 - c).wait_recv()
        for a in range(n):
            copy(a, c).wait_send()

    return pl.pallas_call(
        body, name="join_halves", in_specs=[HBM] * n, out_specs=[HBM] * n,
        out_shape=[pltpu.HBM(h.shape, F32) for h in halves],
        input_output_aliases={a: a for a in range(n)},
        scratch_shapes=[_dma_sems(n)] * 2,
        compiler_params=_comm_params(),
    )(*_hbm(*halves))


def _chip_partials(grads, wire_dtypes, names, fetched=()):
    core = lax.axis_index("c").astype(jnp.int32).reshape(1)
    todo = len(grads) - len(fetched)
    recv = list(_sibling_exchange(grads[:todo], "sibling_exchange_" + names[0])) + list(fetched)
    return list(_add_halves(core, grads, recv, wire_dtypes, "add_halves_" + names[0]))


def _finish_reduce(parts, got, after):
    x, y, c = _place()
    others = [2 * px + py for px, py in _other_chips(x, y)]
    own_first = jnp.stack([2 * x + y] + others + [c]).astype(jnp.int32)
    return [f.reshape(2 * f.shape[1], D) for f in _join_halves(_sum_chips(own_first, parts, got, after))]


ADAMW_STEPS = 8


def _adamw(params, by_row, chip, window_step):
    n = len(params)
    rows, _, cols = by_row[0].shape
    block = lambda shape: pl.BlockSpec((shape[0] // ADAMW_STEPS, shape[1]), lambda i, c: (i, 0))
    assert all(a.shape[0] % (8 * ADAMW_STEPS) == 0 for p in params for a in p)

    def body(c_ref, *refs):
        w_hbm, g_ref, m_hbm, v_hbm = refs[4 * n:4 * n + 4]
        results, (ins_ref, outs_ref, sems) = refs[8 * n + 4:8 * n + 8], refs[8 * n + 8:]
        loads = [pltpu.make_async_copy(src.at[:, 0, :], ins_ref.at[k], sems.at[k])
                 for k, src in enumerate((w_hbm, m_hbm, v_hbm))]
        stores = [pltpu.make_async_copy(outs_ref.at[k], dst.at[:, 0, :], sems.at[3 + k]) for k, dst in enumerate(results)]
        first = pl.program_id(0) == 0

        @pl.when(first)
        def _():
            for cp in loads:
                cp.start()

        for a in range(n):
            w_ref, a_g_ref, m_ref, v_ref = refs[4 * a:4 * a + 4]
            outs = refs[4 * n + 4 + 4 * a:4 * n + 8 + 4 * a]
            g = a_g_ref[...]
            outs[0][...] = g
            outs[1][...], outs[2][...], outs[3][...] = _adamw_math(w_ref[...], g, m_ref[...], v_ref[...])

        @pl.when(first)
        def _():
            for cp in loads:
                cp.wait()
            for lo in range(0, cols, LANE):
                lanes = slice(lo, lo + LANE)
                g = g_ref[0:rows, lanes]
                for s in range(1, N_CHIPS):
                    g = jnp.where(c_ref[0] == s, g_ref[s * window_step:s * window_step + rows, lanes], g)
                outs_ref[0, :, lanes] = g
                outs_ref[1, :, lanes], outs_ref[2, :, lanes], outs_ref[3, :, lanes] = _adamw_math(
                    ins_ref[0, :, lanes], g, ins_ref[1, :, lanes], ins_ref[2, :, lanes])
            for cp in stores:
                cp.start()

        @pl.when(pl.program_id(0) == ADAMW_STEPS - 1)
        def _():
            for cp in stores:
                cp.wait()

    outs = pl.pallas_call(
        body, name="adamw_matrices",
        grid_spec=pltpu.PrefetchScalarGridSpec(
            num_scalar_prefetch=1, grid=(ADAMW_STEPS,),
            in_specs=[block(a.shape) for p in params for a in p] + [HBM, _const(by_row[1].shape), HBM, HBM],
            out_specs=[block(p[0].shape) for p in params for _ in range(4)] + [HBM] * 4,
            scratch_shapes=[pltpu.VMEM((3, rows, cols), F32), pltpu.VMEM((4, rows, cols), F32), _dma_sems(7)]),
        out_shape=[pltpu.HBM(p[0].shape, F32) for p in params for _ in range(4)] + [pltpu.HBM((rows, 1, cols), F32)] * 4,
        compiler_params=_params(48, dimension_semantics=_seq()),
    )(chip, *_hbm(*[a for p in params for a in p], *by_row))
    return [outs[4 * a:4 * a + 4] for a in range(n)], outs[4 * n:]


def _adamw_math(w, g, m, v):
    nm = ADAM_B1 * m + (1.0 - ADAM_B1) * g
    nv = ADAM_B2 * v + (1.0 - ADAM_B2) * (g * g)
    m_hat = nm / (1.0 - ADAM_B1 ** ADAM_STEP)
    v_hat = nv / (1.0 - ADAM_B2 ** ADAM_STEP)
    return -ADAM_LR * (m_hat / (jnp.sqrt(v_hat) + ADAM_EPS) + ADAM_WD * w), nm, nv


SMALL = (("meta_tokens", (N_META, D // N_CHIPS)), ("ln_in_g", (1, D)), ("ln_in_b", (1, D)), ("b_in", (1, D_IN)),
         ("w_gate_lr2", (GATE_RANK, GLA_HEADS * DK // N_CHIPS)), ("b_gate_lr2", (1, GLA_HEADS * DK)),
         ("attn_sinks", (1, SWA_HEADS)),
         ("gla_norm_g", (1, DV)), ("ln1_g", (1, D)), ("ln1_b", (1, D)), ("ln2_g", (1, D)), ("ln2_b", (1, D)))
ROW_META, ROW_B_IN, ROW_TAIL, ROW_WG2 = 0, 22, 25, 32
ROW_LN = dict(ln_in_g=16, ln_in_b=17, ln1_g=18, ln1_b=19, ln2_g=20, ln2_b=21)
TAIL_BG2, TAIL_SINKS, TAIL_GN, TAIL_LOSS = 0, 256, 256 + SWA_HEADS, 256 + SWA_HEADS + DV


def _adamw_small(place, packs, own, params):
    n = len(SMALL)

    def body(place_ref, packs_ref, own_ref, *refs):
        ins, outs, p_ref = refs[:3 * n], refs[3 * n:-1], refs[-1]
        me, c = place_ref[0], place_ref[1]
        total = jnp.where(me == 0, own_ref[...], packs_ref[0])
        for i in range(1, N_DEVICES):
            total = total + jnp.where(me == i, own_ref[...], packs_ref[i])
        p_ref[...] = total
        outs[4 * n][...] = total[ROW_TAIL:ROW_TAIL + 1, :]

        def mine(width, rows):
            part = lambda s: p_ref[rows, s * width:(s + 1) * width]
            return jnp.where(c == 0, part(0), jnp.where(c == 1, part(1), jnp.where(c == 2, part(2), part(3))))

        tail = lambda lo, width: p_ref[ROW_TAIL:ROW_TAIL + 1, lo:lo + width]
        grads = dict(
            meta_tokens=mine(D // N_CHIPS, slice(ROW_META, ROW_META + N_META)),
            b_in=jnp.concatenate([p_ref[ROW_B_IN:ROW_B_IN + 1, :], p_ref[ROW_B_IN + 1:ROW_B_IN + 2, :],
                                  p_ref[ROW_B_IN + 2:ROW_B_IN + 3, 0:D_IN - 2 * D]], axis=1),
            w_gate_lr2=mine(256 // N_CHIPS, slice(ROW_WG2, ROW_WG2 + 16)),
            b_gate_lr2=tail(TAIL_BG2, 256), attn_sinks=tail(TAIL_SINKS, SWA_HEADS), gla_norm_g=tail(TAIL_GN, DV),
            **{k: p_ref[r:r + 1, :] for k, r in ROW_LN.items()})
        for i, (name, _) in enumerate(SMALL):
            g = grads[name]
            outs[4 * i][...] = g
            outs[4 * i + 1][...], outs[4 * i + 2][...], outs[4 * i + 3][...] = _adamw_math(
                ins[3 * i][...], g, ins[3 * i + 1][...], ins[3 * i + 2][...])

    whole = lambda shape: pl.BlockSpec(shape, lambda i, c: (0,) * len(shape))
    outs = pl.pallas_call(
        body, name="adamw_small",
        grid_spec=pltpu.PrefetchScalarGridSpec(
            num_scalar_prefetch=1, grid=(1,),
            in_specs=[whole(packs.shape), whole(own.shape)] + [whole(s) for _, s in SMALL for _ in range(3)],
            out_specs=[whole(s) for _, s in SMALL for _ in range(4)] + [whole((1, D))],
            scratch_shapes=[pltpu.VMEM(own.shape, F32)]),
        out_shape=[pltpu.HBM(s, F32) for _, s in SMALL for _ in range(4)] + [pltpu.HBM((1, D), F32)],
        compiler_params=_params(16, dimension_semantics=_seq()),
    )(place, *_hbm(packs, own, *[a for p in params for a in p]))
    return [outs[4 * i:4 * i + 4] for i in range(n)], outs[4 * n]


def _small_pack(gr):
    names = ["meta_blk"] + list(ROW_LN) + ["b_in_p", "wg2_p", "bg2", "sinks", "gn", "loss"]
    gate_w = GLA_HEADS * DK

    def body(*refs):
        src, out = dict(zip(names, refs)), refs[-1]
        out[...] = jnp.zeros_like(out)
        out[ROW_META:ROW_META + N_META, :] = src["meta_blk"][META_OFF:CH, :]
        for k, r in ROW_LN.items():
            out[r:r + 1, :] = src[k][...]
        for j in range(-(-D_IN // D)):
            width = min(D, D_IN - j * D)
            out[ROW_B_IN + j:ROW_B_IN + j + 1, 0:width] = src["b_in_p"][:, j * D:j * D + width]
        tail = slice(ROW_TAIL, ROW_TAIL + 1)
        out[tail, TAIL_BG2:TAIL_BG2 + gate_w] = src["bg2"][...]
        out[tail, TAIL_SINKS:TAIL_SINKS + SWA_HEADS] = src["sinks"][:, 0:SWA_HEADS]
        out[tail, TAIL_GN:TAIL_GN + DV] = src["gn"][...]
        out[tail, TAIL_LOSS:TAIL_LOSS + 1] = src["loss"][:, 0:1]
        out[ROW_WG2:ROW_WG2 + GATE_RANK, 0:gate_w] = src["wg2_p"][0:GATE_RANK, :]

    arrays = [gr[k] for k in names]
    return pl.pallas_call(
        body, name="small_pack", grid=(1,),
        in_specs=[_acc(a.shape) for a in arrays], out_specs=_acc((SMALL_ROWS, D)),
        out_shape=pltpu.HBM((SMALL_ROWS, D), F32),
        compiler_params=_params(16, dimension_semantics=_seq()),
    )(*_hbm(*arrays))


BIG = ("w_in", "w_out", "w_g", "w_u", "w_d")


def kernel(x, meta_tokens, ln_in_g, ln_in_b, w_in, b_in, w_gate_lr2, b_gate_lr2, attn_sinks, gla_norm_g, w_out, ln1_g, ln1_b, w_ffn_gate, w_ffn_up, w_ffn_down, ln2_g, ln2_b, loss_target, m_meta_tokens, m_ln_in_g, m_ln_in_b, m_w_in, m_b_in, m_w_gate_lr2, m_b_gate_lr2, m_attn_sinks, m_gla_norm_g, m_w_out, m_ln1_g, m_ln1_b, m_w_ffn_gate, m_w_ffn_up, m_w_ffn_down, m_ln2_g, m_ln2_b, v_meta_tokens, v_ln_in_g, v_ln_in_b, v_w_in, v_b_in, v_w_gate_lr2, v_b_gate_lr2, v_attn_sinks, v_gla_norm_g, v_w_out, v_ln1_g, v_ln1_b, v_w_ffn_gate, v_w_ffn_up, v_w_ffn_down, v_ln2_g, v_ln2_b):
    chip = 2 * lax.axis_index("x") + lax.axis_index("y")

    halves = lambda a: a.reshape(2, a.shape[0] // 2, a.shape[1])
    r_in = SHARD_ROWS["w_in"]
    first = [halves(a) for a in (jnp.pad(w_in[0].T.astype(BF16), ((0, W_IN_WIN - r_in), (0, 0))), meta_tokens,
                                 w_gate_lr2[0])]
    rest = [halves(a) for a in (w_out[0].astype(BF16), w_ffn_gate[0].T.astype(BF16), w_ffn_up[0].T.astype(BF16),
                                w_ffn_down[0].astype(BF16))]
    lands = lambda arrs: [(N_CHIPS,) + a.shape for a in arrs]
    first_handle, first_token = _ici_start("gather", first, lands(first), [], "gather_first_start")
    rest_handle, token = _ici_start("gather", rest, lands(rest), [first_token], "gather_rest_start")
    own_slab = lambda got, shards: [lax.dynamic_update_index_in_dim(g, s, chip, axis=0) for g, s in zip(got, shards)]
    fetching = {}

    def fetch_first(after):
        shards, landed = _ici_wait("gather", first_handle, after, "gather_first_wait")
        g_in, g_meta, g_wg2 = own_slab(_sibling_forward(landed, "gather_first_forward"), shards)
        w_in_windows = g_in.reshape(N_CHIPS, W_IN_WIN, D)
        meta_full = jnp.concatenate([g_meta[s].reshape(N_META, -1) for s in range(N_CHIPS)], axis=1)
        wg2_full = jnp.concatenate([g_wg2[s].reshape(w_gate_lr2.shape[1], -1) for s in range(N_CHIPS)], axis=1)
        return w_in_windows, meta_full, wg2_full

    def fetch_rest(after):
        shards, landed = _ici_wait("gather", rest_handle, after, "gather_rest_wait")
        g_out, = own_slab(_sibling_forward(landed[:1], "gather_w_out_forward"), shards[:1])
        fetching["shards"] = shards[1:]
        fetching["handle"], forward_token = _forward_start(landed[1:], "gather_ffn_forward_start")
        return g_out.reshape(-1, D), forward_token

    def fetch_ffn(after):
        got = _forward_wait(fetching["handle"], after, "gather_ffn_forward_wait")
        return [g.reshape(-1, D) for g in own_slab(got, fetching["shards"])]

    sent = {}
    split = lambda grads: [g.reshape(N_CHIPS, 2, -1, D) for g in grads]

    def ship(key, grads, names, fetched=()):
        parts = _chip_partials(grads, [BF16] * len(grads), names, fetched)
        sent[key], ship_token = _ici_start("scatter", parts, [p.shape for p in parts], [], "scatter_" + key + "_start")
        return ship_token

    def exchange_ffn(g):
        grads = split([g[k] for k in BIG[2:]])
        sent["ffn_halves"], exchange_token = _ici_start("sibling", grads, [(N_CHIPS,) + a.shape[2:] for a in grads], [],
                                                        "sibling_ffn_start")
        return exchange_token

    def ship_ffn(dw_out):
        grads, fetched = _ici_wait("sibling", sent["ffn_halves"], [dw_out], "sibling_ffn_wait")
        return ship("ffn", split([dw_out]) + grads, list(BIG[1:]), fetched)

    def ship_w_in(dw_in_windows):
        return ship("w_in", split([dw_in_windows]), ["w_in"])

    dx, gr = _local_step(
        x[0], loss_target[0], ln_in_g, ln_in_b, b_in[0], b_gate_lr2[0], attn_sinks[0], gla_norm_g[0], ln1_g[0],
        ln1_b[0], ln2_g[0], ln2_b[0], token, fetch_first, fetch_rest, fetch_ffn, exchange_ffn, ship_ffn, ship_w_in)
    ffn_parts, ffn_got = _ici_wait("scatter", sent["ffn"], [dx], "scatter_ffn_wait")
    w_in_parts, w_in_got = _ici_wait("scatter", sent["w_in"], [dx], "scatter_w_in_wait")

    small_handle, token = _small_start(_small_pack(gr), [w_in_got[0]])
    red = _finish_reduce(w_in_parts + ffn_parts, w_in_got + ffn_got, [token])

    big_g = dict(zip(BIG, red))
    weights = dict(meta_tokens=meta_tokens, ln_in_g=ln_in_g, ln_in_b=ln_in_b, w_in=w_in, b_in=b_in,
                   w_gate_lr2=w_gate_lr2, b_gate_lr2=b_gate_lr2, attn_sinks=attn_sinks, gla_norm_g=gla_norm_g,
                   w_out=w_out, ln1_g=ln1_g, ln1_b=ln1_b, w_ffn_gate=w_ffn_gate, w_ffn_up=w_ffn_up,
                   w_ffn_down=w_ffn_down, ln2_g=ln2_g, ln2_b=ln2_b)
    m_in = dict(meta_tokens=m_meta_tokens, ln_in_g=m_ln_in_g, ln_in_b=m_ln_in_b, w_in=m_w_in, b_in=m_b_in,
                w_gate_lr2=m_w_gate_lr2, b_gate_lr2=m_b_gate_lr2, attn_sinks=m_attn_sinks, gla_norm_g=m_gla_norm_g,
                w_out=m_w_out, ln1_g=m_ln1_g, ln1_b=m_ln1_b, w_ffn_gate=m_w_ffn_gate, w_ffn_up=m_w_ffn_up,
                w_ffn_down=m_w_ffn_down, ln2_g=m_ln2_g, ln2_b=m_ln2_b)
    v_in = dict(meta_tokens=v_meta_tokens, ln_in_g=v_ln_in_g, ln_in_b=v_ln_in_b, w_in=v_w_in, b_in=v_b_in,
                w_gate_lr2=v_w_gate_lr2, b_gate_lr2=v_b_gate_lr2, attn_sinks=v_attn_sinks, gla_norm_g=v_gla_norm_g,
                w_out=v_w_out, ln1_g=v_ln1_g, ln1_b=v_ln1_b, w_ffn_gate=v_w_ffn_gate, w_ffn_up=v_w_ffn_up,
                w_ffn_down=v_w_ffn_down, ln2_g=v_ln2_g, ln2_b=v_ln2_b)
    names = list(weights)
    big_names = ("w_in", "w_out", "w_ffn_gate", "w_ffn_up", "w_ffn_down")

    grads, delta, new_m, new_v = {}, {}, {}, {}
    flips = [(lambda a: a.T) if kk in ("w_g", "w_u") else (lambda a: a) for kk in BIG[1:]]
    by_row = lambda a: jnp.transpose(a, (2, 0, 1))
    updated, updated_w_in = _adamw(
        [(flip(weights[k][0]), big_g[kk], flip(m_in[k][0]), flip(v_in[k][0]))
         for k, kk, flip in zip(big_names[1:], BIG[1:], flips)],
        (by_row(w_in), big_g["w_in"], by_row(m_w_in), by_row(v_w_in)), chip.astype(jnp.int32).reshape(1), r_in % BF16_ROWS)
    for k, flip, results in zip(big_names[1:], flips, updated):
        grads[k], delta[k], new_m[k], new_v[k] = (flip(t)[None] for t in results)
    grads["w_in"], delta["w_in"], new_m["w_in"], new_v["w_in"] = (jnp.transpose(t, (1, 2, 0)) for t in updated_w_in)
    small_in = [tuple(src[k].reshape(shape) for src in (weights, m_in, v_in)) for k, shape in SMALL]
    place = jnp.stack([2 * chip + lax.axis_index("c"), chip]).astype(jnp.int32)
    small_own, small_all = _small_wait(small_handle, [updated[0][0]])
    small_out, tail_row = _adamw_small(place, small_all, small_own, small_in)
    for (k, _), results in zip(SMALL, small_out):
        grads[k], delta[k], new_m[k], new_v[k] = (r.reshape(weights[k].shape) for r in results)

    return (tail_row[0, TAIL_LOSS], dx[None], *[grads[k] for k in names], *[delta[k] for k in names], *[new_m[k] for k in names],
            *[new_v[k] for k in names])
```

```python
import jax
import jax.numpy as jnp
from jax import lax
from jax.experimental import pallas as pl
from jax.experimental.pallas import tpu as pltpu

F32 = jnp.float32
BF16 = jnp.bfloat16
MESH = pl.DeviceIdType.MESH

D = 1024
SEQ = 4096
N_META = 16
SWA_HEADS, SWA_KV_HEADS, DH = 8, 2, 64
WINDOW = 128
GLA_HEADS, DK, DV = 4, 64, 128
GLA_TAU = 16.0
CH = 64
D_FF = 2816
D_IN = 2320
LN_EPS = 1e-5
RMS_EPS = 1e-6
ALPHA = 2.0 ** 0.25
NEG = -1e30
ADAM_LR, ADAM_B1, ADAM_B2, ADAM_EPS, ADAM_WD, ADAM_STEP = 0.001, 0.9, 0.999, 1e-8, 0.01, 10
O_QS, O_KS, O_VS, O_QG, O_KG, O_VG, O_RG, O_LR = 0, 512, 640, 768, 1024, 1280, 1792, 2304

LANE = 128
BLK = WINDOW
GATE_RANK = 16
D_IN_P = D_IN + LANE - GATE_RANK
META_OFF = CH - N_META
HEAD_POS = (0, 4, 1, 5, 2, 6, 3, 7)
LN_ROWS = 512
TOKEN = (8, LANE)
N_CHIPS = 4
SHARD_ROWS = dict(w_in=D_IN // N_CHIPS, w_out=D // N_CHIPS, w_g=D_FF // N_CHIPS, w_u=D_FF // N_CHIPS,
                  w_d=D_FF // N_CHIPS)
SMALL_ROWS = 48
BF16_ROWS = 16
W_IN_WIN = -(-SHARD_ROWS["w_in"] // (2 * BF16_ROWS)) * 2 * BF16_ROWS
W_IN_STARTS = tuple(s * SHARD_ROWS["w_in"] // BF16_ROWS * BF16_ROWS for s in range(N_CHIPS))
VMEM_CAP_MB = 64
VMEM_SPARE_MB = 6


def _lp():
    return SEQ + BLK


def _row_tile(cap):
    lp = _lp()
    return max(t for t in range(16, cap + 1, 16) if lp % t == 0)


def _params(vmem_mb, **kw):
    assert vmem_mb <= VMEM_CAP_MB - VMEM_SPARE_MB
    return pltpu.CompilerParams(vmem_limit_bytes=vmem_mb << 20, **kw)


def _seq(n=1):
    return ("arbitrary",) * n


def _const(shape):
    return pl.BlockSpec(shape, lambda *_: (0,) * len(shape), pipeline_mode=pl.Buffered(1))


def _acc(shape):
    return pl.BlockSpec(shape, lambda *_: (0,) * len(shape))


def _rows(tm, width):
    return pl.BlockSpec((tm, width), lambda i: (i, 0))


def _dot(a, b):
    return jnp.dot(a.astype(BF16), b.astype(BF16), preferred_element_type=F32)


def _dot_nt(a, b):
    return lax.dot_general(a.astype(BF16), b.astype(BF16), (((1,), (1,)), ((), ())), preferred_element_type=F32)


def _dot_tn(a, b):
    return lax.dot_general(a.astype(BF16), b.astype(BF16), (((0,), (0,)), ((), ())), preferred_element_type=F32)


def _dot_exact(a, b):
    return jnp.dot(a, b, precision=lax.Precision.HIGHEST, preferred_element_type=F32)


def _ln_stats(x):
    mu = jnp.mean(x, axis=-1, keepdims=True)
    xc = x - mu
    rstd = lax.rsqrt(jnp.mean(xc * xc, axis=-1, keepdims=True) + LN_EPS)
    return xc * rstd, rstd


def _ln_bwd(dy, xhat, rstd, g):
    dxh = dy * g
    return rstd * (dxh - jnp.mean(dxh, axis=-1, keepdims=True) - xhat * jnp.mean(dxh * xhat, axis=-1, keepdims=True))


def _sigmoid(x):
    return 1.0 / (1.0 + jnp.exp(-x))


def _iota(shape, dim):
    return lax.broadcasted_iota(jnp.int32, shape, dim)


def _hbm(*arrays):
    return tuple(pltpu.with_memory_space_constraint(a, pltpu.HBM) for a in arrays)


def _ln_in_fwd_real(x, g, b, token):
    tr = min(LN_ROWS, SEQ)

    def body(x_ref, g_ref, b_ref, token_ref, h_ref):
        xhat, _ = _ln_stats(x_ref[...])
        h_ref[...] = xhat * g_ref[...] + b_ref[...]

    return pl.pallas_call(
        body, name="ln_in_fwd", grid=(SEQ // tr,),
        in_specs=[_rows(tr, D), _const((1, D)), _const((1, D)), _const(TOKEN)],
        out_specs=_rows(tr, D),
        out_shape=pltpu.HBM((_lp(), D), F32),
        compiler_params=_params(32, dimension_semantics=_seq()),
    )(*_hbm(x, g, b), token)


def _ln_in_fwd_meta(h_real, meta_ext, g, b):
    def meta_body(m_ref, g_ref, b_ref, real_ref, h_ref):
        xhat, _ = _ln_stats(m_ref[...])
        h_ref[...] = xhat * g_ref[...] + b_ref[...]

    return pl.pallas_call(
        meta_body, name="ln_in_fwd_meta", grid=(1,),
        in_specs=[_const((BLK, D)), _const((1, D)), _const((1, D)), pl.BlockSpec(memory_space=pl.ANY)],
        out_specs=pl.BlockSpec((BLK, D), lambda i: (SEQ // BLK, 0)),
        out_shape=pltpu.HBM((_lp(), D), F32),
        input_output_aliases={3: 0},
        compiler_params=_params(16, dimension_semantics=_seq()),
    )(*_hbm(meta_ext, g, b, h_real))


def _in_proj(h0, w_in_windows, b_in_p, wg2_p, bg2):
    tm = _row_tile(384)
    lp = _lp()
    widths = (512, 128, 128, 256, 256, 512, 512, 128)
    offs = (O_QS, O_KS, O_VS, O_QG, O_KG, O_VG, O_RG, O_LR)
    shard = SHARD_ROWS["w_in"]

    def body(h_ref, win_ref, b_ref, wg2_ref, bg2_ref, *outs):
        w_ref = outs[9]

        @pl.when(pl.program_id(0) == 0)
        def _():
            for s in range(N_CHIPS):
                w_ref[shard * s:shard * (s + 1), :] = win_ref[s, 0:shard, :]
            w_ref[D_IN:D_IN_P, :] = jnp.zeros((D_IN_P - D_IN, D), BF16)

        proj = _dot_nt(h_ref[...], w_ref[...]) + b_ref[...]
        for pos, h in enumerate(HEAD_POS):
            outs[0][:, pos * DH:(pos + 1) * DH] = proj[:, O_QS + h * DH:O_QS + (h + 1) * DH]
        for o_ref, off, wd in zip(outs[1:8], offs[1:], widths[1:]):
            o_ref[...] = proj[:, off:off + wd]
        outs[8][...] = _dot(proj[:, O_LR:O_LR + LANE], wg2_ref[...]) + bg2_ref[...]

    return pl.pallas_call(
        body, name="in_proj", grid=(lp // tm,),
        in_specs=[_rows(tm, D), _const(w_in_windows.shape), _const((1, D_IN_P)), _const((LANE, 256)), _const((1, 256))],
        out_specs=[_rows(tm, w) for w in widths] + [_rows(tm, 256), _acc((D_IN_P, D))],
        out_shape=[pltpu.HBM((lp, w), F32) for w in widths] + [pltpu.HBM((lp, 256), F32), pltpu.HBM((D_IN_P, D), BF16)],
        compiler_params=_params(48, dimension_semantics=_seq()),
    )(*_hbm(h0, w_in_windows, b_in_p, wg2_p, bg2))


def _swa_masks(n):
    nb = SEQ // BLK
    is_meta = n == nb
    ri = _iota((BLK, BLK), 0)
    cj = _iota((BLK, BLK), 1)
    meta_col = ((cj >= META_OFF) & (cj < CH)).astype(jnp.int32)
    meta_q = meta_col * ((cj <= ri) & (ri < CH)).astype(jnp.int32)
    valid_m = jnp.where(is_meta, meta_q, meta_col) > 0
    dist_m = jnp.where(is_meta, ri - cj, n * BLK + ri + CH - cj).astype(F32)
    valid_p = jnp.where((n >= 1) & (n < nb), (cj > ri).astype(jnp.int32), 0) > 0
    dist_p = (ri + BLK - cj).astype(F32)
    valid_c = jnp.where(n < nb, (cj <= ri).astype(jnp.int32), 0) > 0
    dist_c = (ri - cj).astype(F32)
    return (dist_m, dist_p, dist_c), (valid_m, valid_p, valid_c)


def _swa_bias(n):
    dists, valids = _swa_masks(n)
    return (jnp.concatenate([-d for d in dists], axis=1),
            jnp.concatenate([jnp.where(v, 0.0, NEG) for v in valids], axis=1))


def _swa_half(ref, pos, scale=1.0):
    col = ref[:, (pos // 2) * LANE:(pos // 2 + 1) * LANE]
    lane = _iota((BLK, LANE), 1)
    mine = lane < DH if pos % 2 == 0 else lane >= DH
    return jnp.where(mine, col * scale, 0.0).astype(BF16)


def _swa_merge(even, odd):
    return jnp.where(_iota((BLK, LANE), 1) < DH, even, odd)


def _swa_softmax(t, sink):
    m = jnp.maximum(jnp.max(t, axis=-1, keepdims=True), sink)
    e = jnp.exp(t - m)
    e_sink = jnp.exp(sink - m)
    inv = 1.0 / (jnp.sum(e, axis=-1, keepdims=True) + e_sink)
    return e * inv, e_sink * inv


def _swa_kv_specs(width):
    nb = SEQ // BLK
    return [pl.BlockSpec((BLK, width), lambda n: (nb, 0)),
            pl.BlockSpec((BLK, width), lambda n: (jnp.clip(n - 1, 0, nb - 1), 0)),
            pl.BlockSpec((BLK, width), lambda n: (jnp.minimum(n, nb), 0))]


def _swa_fwd(sinks, qs, ks, vs):
    nb = SEQ // BLK
    heads = range(SWA_HEADS)

    def body(sink_ref, q_ref, km_ref, kp_ref, kc_ref, vm_ref, vp_ref, vc_ref, o_ref):
        negdist, maskbias = _swa_bias(pl.program_id(0))
        k_all = jnp.concatenate([km_ref[...], kp_ref[...], kc_ref[...]], axis=0).astype(BF16)
        v_all = jnp.concatenate([vm_ref[...], vp_ref[...], vc_ref[...]], axis=0).astype(BF16)
        q = [_swa_half(q_ref, pos, DH ** -0.5) for pos in heads]
        t = [_dot_nt(q[pos], k_all) + (2.0 ** -(HEAD_POS[pos] + 1) * negdist + maskbias) for pos in heads]
        p = [_swa_softmax(t[pos], sink_ref[HEAD_POS[pos]])[0].astype(BF16) for pos in heads]
        o = [_dot(p[pos], v_all) for pos in heads]
        for col in range(SWA_HEADS // 2):
            o_ref[:, col * LANE:(col + 1) * LANE] = _swa_merge(o[2 * col], o[2 * col + 1])

    kvw = SWA_KV_HEADS * DH
    return pl.pallas_call(
        body, name="swa_fwd", grid=(nb + 1,),
        in_specs=[pl.BlockSpec(memory_space=pltpu.SMEM), _rows(BLK, SWA_HEADS * DH)] + _swa_kv_specs(kvw) + _swa_kv_specs(kvw),
        out_specs=_rows(BLK, SWA_HEADS * DH),
        out_shape=pltpu.HBM((_lp(), SWA_HEADS * DH), F32),
        compiler_params=_params(16, dimension_semantics=_seq()),
    )(sinks, *_hbm(qs, ks, ks, ks, vs, vs, vs))


GLA_PER_STEP = BLK // CH


def _gla_block(s):
    nb = SEQ // BLK
    return jnp.where(s == 0, nb, s - 1)


def _gla_rowmask(s):
    ri = _iota((BLK, 1), 0)
    m = jnp.where(s == 0, ((ri >= META_OFF) & (ri < CH)).astype(jnp.int32), 1)
    return (m > 0).astype(F32) + jnp.zeros((BLK, 1), F32)


def _gla_chunk_masks():
    r, c = _iota((BLK, BLK), 0), _iota((BLK, BLK), 1)
    same = ((r < CH) & (c < CH)) | ((r >= CH) & (c >= CH))
    return same & (r >= c), same & (r <= c), same


def _gla_decay(z, rmask):
    log_g = (jnp.minimum(z, 0.0) - jnp.log1p(jnp.exp(-jnp.abs(z)))) * (rmask / GLA_TAU)
    lower, _, same = _gla_chunk_masks()
    return _dot_exact(lower.astype(F32), log_g), _dot_exact(same.astype(F32), log_g)


def _gla_slices(c, h):
    return slice(c * CH, (c + 1) * CH), slice(h * DK, (h + 1) * DK), slice(h * DV, (h + 1) * DV)


def _gla_fwd(qg, kg, vg, z):
    steps = SEQ // BLK + 1
    kw, vw = GLA_HEADS * DK, GLA_HEADS * DV
    pairs = [(c, h) for c in range(GLA_PER_STEP) for h in range(GLA_HEADS)]

    def body(q_ref, k_ref, v_ref, z_ref, o_ref, st_ref, st):
        s = pl.program_id(0)

        @pl.when(s == 0)
        def _():
            st[...] = jnp.zeros_like(st)

        rmask = _gla_rowmask(s)
        b, b_last = _gla_decay(z_ref[...], rmask)
        q = q_ref[...] * (rmask * DK ** -0.5)
        k = k_ref[...] * rmask
        v = v_ref[...] * rmask
        qe = q * jnp.exp(b)
        ke = k * jnp.exp(-b)
        kd = k * jnp.exp(b_last - b)
        e_last = jnp.exp(b_last)
        causal = _iota((CH, CH), 0) >= _iota((CH, CH), 1)
        a, upd, intra = {}, {}, {}
        for c, h in pairs:
            rows, ks, vs_ = _gla_slices(c, h)
            a[c, h] = jnp.where(causal, _dot_nt(qe[rows, ks], ke[rows, ks]), 0.0)
            upd[c, h] = _dot_tn(v[rows, vs_], kd[rows, ks])
        for c, h in pairs:
            rows, ks, vs_ = _gla_slices(c, h)
            intra[c, h] = _dot(a[c, h], v[rows, vs_])
        state = st[...]
        for c in range(GLA_PER_STEP):
            st_ref[0, c] = state
            for h in range(GLA_HEADS):
                rows, ks, vs_ = _gla_slices(c, h)
                o_ref[rows, vs_] = intra[c, h] + _dot_nt(qe[rows, ks], state[:, ks])
            state = state * e_last[c * CH:c * CH + 1] + jnp.concatenate([upd[c, h] for h in range(GLA_HEADS)], axis=1)
        st[...] = state

    blk = lambda w: pl.BlockSpec((BLK, w), lambda s: (_gla_block(s), 0))
    return pl.pallas_call(
        body, name="gla_fwd", grid=(steps,),
        in_specs=[blk(kw), blk(kw), blk(vw), blk(kw)],
        out_specs=[blk(vw), pl.BlockSpec((1, GLA_PER_STEP, DV, kw), lambda s: (s, 0, 0, 0))],
        out_shape=[pltpu.HBM((_lp(), vw), F32), pltpu.HBM((steps, GLA_PER_STEP, DV, kw), F32)],
        scratch_shapes=[pltpu.VMEM((DV, kw), F32)],
        compiler_params=_params(16, dimension_semantics=_seq()),
    )(*_hbm(qg, kg, vg, z))


def _post_mix(o_s, o_gla, r_g, h0, gn4, w_out, g1, b1, token):
    tm = _row_tile(384)
    lp = _lp()

    def body(os_ref, og_ref, r_ref, h0_ref, gn_ref, w_ref, g_ref, b_ref, token_ref, o_ref, pre_ref, h1_ref):
        for pos, h in enumerate(HEAD_POS):
            o_ref[:, h * DH:(h + 1) * DH] = os_ref[:, pos * DH:(pos + 1) * DH].astype(BF16)
        for h in range(GLA_HEADS):
            hs = slice(h * DV, (h + 1) * DV)
            xg = og_ref[:, hs]
            n = xg * lax.rsqrt(jnp.mean(xg * xg, axis=-1, keepdims=True) + RMS_EPS) * gn_ref[...]
            r = r_ref[:, hs]
            o_ref[:, 512 + h * DV:512 + (h + 1) * DV] = (n * (r * _sigmoid(r))).astype(BF16)
        pre = ALPHA * h0_ref[...] + _dot(o_ref[...], w_ref[...])
        pre_ref[...] = pre
        xhat, _ = _ln_stats(pre)
        h1_ref[...] = xhat * g_ref[...] + b_ref[...]

    return pl.pallas_call(
        body, name="post_mix", grid=(lp // tm,),
        in_specs=[_rows(tm, 512), _rows(tm, 512), _rows(tm, 512), _rows(tm, D), _const((1, DV)), _const((D, D)),
                  _const((1, D)), _const((1, D)), _const(TOKEN)],
        out_specs=[_rows(tm, D), _rows(tm, D), _rows(tm, D)],
        out_shape=[pltpu.HBM((lp, D), BF16), pltpu.HBM((lp, D), F32),
                   pltpu.HBM((lp, D), F32)],
        compiler_params=_params(32, dimension_semantics=_seq()),
    )(*_hbm(o_s, o_gla, r_g, h0, gn4, w_out, g1, b1), token)


def _ffn_fwd_loss_bwd(h1, wg_t, wu_t, wd, target, g2, b2):
    lp = _lp()
    tm = max(t for t in range(BLK, 384 + 1, BLK) if lp % t == 0)
    steps = lp // tm
    last_blk = SEQ // BLK - 1
    half = D_FF // 2
    n_t = tm // BLK

    def body(*refs):
        h_ref, wg_ref, wu_ref, wd_ref = refs[:4]
        t_refs = refs[4:4 + n_t]
        g2_ref, b2_ref, a_ref, dgate_ref, dup_ref, dp_ref, loss_ref, dg_ref, db_ref, g_s, u_s, acc = refs[4 + n_t:]
        i = pl.program_id(0)

        @pl.when(i == 0)
        def _():
            acc[...] = jnp.zeros_like(acc)
            dg_ref[...] = jnp.zeros_like(dg_ref)
            db_ref[...] = jnp.zeros_like(db_ref)

        h = h_ref[...]
        hb = h.astype(BF16)
        pre = ALPHA * h
        for j in range(2):
            cols = slice(j * half, (j + 1) * half)
            g = _dot_nt(hb, wg_ref[cols, :])
            u = _dot_nt(hb, wu_ref[cols, :])
            g_s[:, cols] = g
            u_s[:, cols] = u
            pre = pre + _dot(g * _sigmoid(g) * u, wd_ref[cols, :])
        xhat, rstd = _ln_stats(pre)
        real = i * tm + _iota((tm, 1), 0) < SEQ
        target_rows = jnp.concatenate([t[...] for t in t_refs], axis=0)
        diff = jnp.where(real, xhat * g2_ref[...] + b2_ref[...] - target_rows, 0.0)
        acc[...] += jnp.sum(diff * diff, axis=0, keepdims=True)
        dy = diff * (1.0 / D)
        dpre = _ln_bwd(dy, xhat, rstd, g2_ref[...])
        dp_ref[...] = dpre
        dg_ref[...] += jnp.sum(dy * xhat, axis=0, keepdims=True)
        db_ref[...] += jnp.sum(dy, axis=0, keepdims=True)
        dpb = dpre.astype(BF16)
        for j in range(2):
            cols = slice(j * half, (j + 1) * half)
            g, u = g_s[:, cols], u_s[:, cols]
            sg = _sigmoid(g)
            silu = g * sg
            da = _dot_nt(dpb, wd_ref[cols, :])
            a_ref[:, cols] = (silu * u).astype(BF16)
            dgate_ref[:, cols] = (da * u * (sg * (1.0 + g * (1.0 - sg)))).astype(BF16)
            dup_ref[:, cols] = (da * silu).astype(BF16)

        @pl.when(i == steps - 1)
        def _():
            loss_ref[...] = jnp.zeros_like(loss_ref) + (0.5 / D) * jnp.sum(acc[...], axis=1, keepdims=True)

    t_spec = lambda k: pl.BlockSpec((BLK, D), lambda i: (jnp.minimum(i * n_t + k, last_blk), 0))
    return pl.pallas_call(
        body, name="ffn_fwd_loss_bwd", grid=(steps,),
        in_specs=[_rows(tm, D), _const((D_FF, D)), _const((D_FF, D)), _const((D_FF, D))]
        + [t_spec(k) for k in range(n_t)] + [_const((1, D)), _const((1, D))],
        out_specs=[_rows(tm, D_FF), _rows(tm, D_FF), _rows(tm, D_FF), _rows(tm, D), _acc((1, LANE)), _acc((1, D)),
                   _acc((1, D))],
        out_shape=[pltpu.HBM((lp, D_FF), BF16)] * 3 + [pltpu.HBM((lp, D), F32), pltpu.HBM((1, LANE), F32),
                                                         pltpu.HBM((1, D), F32), pltpu.HBM((1, D), F32)],
        scratch_shapes=[pltpu.VMEM((tm, D_FF), F32), pltpu.VMEM((tm, D_FF), F32), pltpu.VMEM((1, D), F32)],
        compiler_params=_params(58, dimension_semantics=_seq()),
    )(*_hbm(h1, wg_t, wu_t, wd, *[target] * n_t, g2, b2))


def _ffn_out_bwd(dpre2, dgate, dup, pre1, wg_t, wu_t, g1, w_out, o_gla, r_g, gn4):
    tm = _row_tile(384)
    lp = _lp()

    def body(dp_ref, dg_ref, du_ref, p1_ref, wg_ref, wu_ref, g1_ref, w_ref, og_ref, r_ref, gn_ref,
             dp1_ref, dg1_ref, db1_ref, dos_ref, dog_ref, dr_ref, dgn_ref):
        @pl.when(pl.program_id(0) == 0)
        def _():
            for acc_ref in (dg1_ref, db1_ref, dgn_ref):
                acc_ref[...] = jnp.zeros_like(acc_ref)

        dh1 = ALPHA * dp_ref[...] + _dot(dg_ref[...], wg_ref[...]) + _dot(du_ref[...], wu_ref[...])
        xhat, rstd1 = _ln_stats(p1_ref[...])
        dpre1 = _ln_bwd(dh1, xhat, rstd1, g1_ref[...])
        dp1_ref[...] = dpre1
        dg1_ref[...] += jnp.sum(dh1 * xhat, axis=0, keepdims=True)
        db1_ref[...] += jnp.sum(dh1, axis=0, keepdims=True)

        do = _dot_nt(dpre1, w_ref[...])
        for pos, h in enumerate(HEAD_POS):
            dos_ref[:, pos * DH:(pos + 1) * DH] = do[:, h * DH:(h + 1) * DH]
        gn = gn_ref[...]
        for h in range(GLA_HEADS):
            hs = slice(h * DV, (h + 1) * DV)
            xg = og_ref[:, hs]
            rstd = lax.rsqrt(jnp.mean(xg * xg, axis=-1, keepdims=True) + RMS_EPS)
            nx = xg * rstd
            r = r_ref[:, hs]
            sr = _sigmoid(r)
            d_o = do[:, 512 + h * DV:512 + (h + 1) * DV]
            dr_ref[:, hs] = d_o * (nx * gn) * (sr * (1.0 + r * (1.0 - sr)))
            dn = d_o * (r * sr)
            dgn_ref[...] += jnp.sum(dn * nx, axis=0, keepdims=True)
            dnx = dn * gn
            dog_ref[:, hs] = rstd * (dnx - nx * jnp.mean(dnx * nx, axis=-1, keepdims=True))

    return pl.pallas_call(
        body, name="ffn_out_bwd", grid=(lp // tm,),
        in_specs=[_rows(tm, D), _rows(tm, D_FF), _rows(tm, D_FF), _rows(tm, D), _const((D_FF, D)), _const((D_FF, D)),
                  _const((1, D)), _const((D, D)), _rows(tm, 512), _rows(tm, 512), _const((1, DV))],
        out_specs=[_rows(tm, D), _acc((1, D)), _acc((1, D)), _rows(tm, 512), _rows(tm, 512), _rows(tm, 512),
                   _acc((1, DV))],
        out_shape=[pltpu.HBM((lp, D), F32), pltpu.HBM((1, D), F32), pltpu.HBM((1, D), F32)]
        + [pltpu.HBM((lp, 512), F32)] * 3 + [pltpu.HBM((1, DV), F32)],
        compiler_params=_params(48, dimension_semantics=_seq()),
    )(*_hbm(dpre2, dgate, dup, pre1, wg_t, wu_t, g1, w_out, o_gla, r_g, gn4))


def _atb(a, b, name, token=None, windows=None):
    lp = _lp()
    tm = _row_tile(1408)
    n, w = a.shape[1], b.shape[1]
    bw = 512 if n * w * 4 > (4 << 20) else w
    tokens = [] if token is None else [token]
    steps = lp // tm

    def body(a_ref, b_ref, *rest):
        o_ref, acc_ref = rest[len(tokens):] if windows else (rest[-1], rest[-1])

        @pl.when(pl.program_id(1) == 0)
        def _():
            acc_ref[...] = jnp.zeros_like(acc_ref)

        acc_ref[...] += _dot_tn(a_ref[...], b_ref[...])

        if windows:
            @pl.when(pl.program_id(1) == steps - 1)
            def _():
                for s, start in enumerate(windows[0]):
                    o_ref[s] = acc_ref[start:start + windows[1], :]

    if windows:
        count, height = len(windows[0]), windows[1]
        out_spec, out_shape = pl.BlockSpec((count, height, bw), lambda j, k: (0, 0, j)), (count, height, w)
    else:
        out_spec, out_shape = pl.BlockSpec((n, bw), lambda j, k: (0, j)), (n, w)
    return pl.pallas_call(
        body, name=name, grid=(w // bw, steps),
        in_specs=[pl.BlockSpec((tm, n), lambda j, k: (k, 0)), pl.BlockSpec((tm, bw), lambda j, k: (k, j))]
        + [_const(TOKEN)] * len(tokens),
        out_specs=out_spec, out_shape=pltpu.HBM(out_shape, F32),
        scratch_shapes=[pltpu.VMEM((n, bw), F32)] if windows else [],
        compiler_params=_params(48, dimension_semantics=_seq(2)),
    )(*_hbm(a, b), *tokens)


def _gla_bwd(qg, kg, vg, z, do_gla, st_all, token):
    steps = SEQ // BLK + 1
    kw, vw = GLA_HEADS * DK, GLA_HEADS * DV
    pairs = [(c, h) for c in range(GLA_PER_STEP) for h in range(GLA_HEADS)]
    heads = range(GLA_HEADS)

    def body(q_ref, k_ref, v_ref, z_ref, do_ref, st_ref, token_ref, dq_ref, dk_ref, dv_ref, dz_ref, dst):
        @pl.when(pl.program_id(0) == 0)
        def _():
            dst[...] = jnp.zeros_like(dst)

        rmask = _gla_rowmask(steps - 1 - pl.program_id(0))
        zz = z_ref[...]
        b, b_last = _gla_decay(zz, rmask)
        e_b, e_nb, e_kd, e_last = jnp.exp(b), jnp.exp(-b), jnp.exp(b_last - b), jnp.exp(b_last)
        q = q_ref[...] * (rmask * DK ** -0.5)
        k = k_ref[...] * rmask
        v = v_ref[...] * rmask
        qe, ke, kd = q * e_b, k * e_nb, k * e_kd
        d_o = do_ref[...]
        causal = _iota((CH, CH), 0) >= _iota((CH, CH), 1)
        a, da, dqe, dke, dv_intra, carry = {}, {}, {}, {}, {}, {}
        for c, h in pairs:
            rows, ks, vs_ = _gla_slices(c, h)
            a[c, h] = jnp.where(causal, _dot_nt(qe[rows, ks], ke[rows, ks]), 0.0)
            da[c, h] = jnp.where(causal, _dot_nt(d_o[rows, vs_], v[rows, vs_]), 0.0)
            carry[c, h] = _dot_tn(d_o[rows, vs_], qe[rows, ks])
        for c, h in pairs:
            rows, ks, vs_ = _gla_slices(c, h)
            dqe[c, h] = _dot(d_o[rows, vs_], st_ref[0, c][:, ks]) + _dot(da[c, h], ke[rows, ks])
            dke[c, h] = _dot_tn(da[c, h], qe[rows, ks])
            dv_intra[c, h] = _dot_tn(a[c, h], d_o[rows, vs_])
        dstate = dst[...]
        dkd, db_decay = {}, {}
        for c in reversed(range(GLA_PER_STEP)):
            for h in heads:
                rows, ks, vs_ = _gla_slices(c, h)
                dkd[c, h] = _dot(v[rows, vs_], dstate[:, ks])
                dv_ref[rows, vs_] = dv_intra[c, h] + _dot_nt(kd[rows, ks], dstate[:, ks])
            chunk_last = e_last[c * CH:c * CH + 1]
            db_decay[c] = jnp.sum(dstate * st_ref[0, c], axis=0, keepdims=True) * chunk_last
            dstate = dstate * chunk_last + jnp.concatenate([carry[c, h] for h in heads], axis=1)
        dst[...] = dstate
        rows_of = lambda parts: jnp.concatenate(
            [jnp.concatenate([parts[c, h] for h in heads], axis=1) for c in range(GLA_PER_STEP)], axis=0)
        dqe_all, dke_all, dkd_all = rows_of(dqe), rows_of(dke), rows_of(dkd)
        dq_ref[...] = dqe_all * e_b * (rmask * DK ** -0.5)
        dk_ref[...] = (dke_all * e_nb + dkd_all * e_kd) * rmask
        dkd_kd = dkd_all * kd
        db = dqe_all * qe - dke_all * ke - dkd_kd
        _, upper, same = _gla_chunk_masks()
        decay_rows = jnp.concatenate([jnp.broadcast_to(db_decay[c], (CH, kw)) for c in range(GLA_PER_STEP)], axis=0)
        dlog_g = _dot_exact(upper.astype(F32), db) + _dot_exact(same.astype(F32), dkd_kd) + decay_rows
        dz_ref[...] = dlog_g * (rmask / GLA_TAU) * _sigmoid(-zz)

    blk = lambda w: pl.BlockSpec((BLK, w), lambda s: (_gla_block(steps - 1 - s), 0))
    return pl.pallas_call(
        body, name="gla_bwd", grid=(steps,),
        in_specs=[blk(kw), blk(kw), blk(vw), blk(kw), blk(vw),
                  pl.BlockSpec((1, GLA_PER_STEP, DV, kw), lambda s: (steps - 1 - s, 0, 0, 0)), _const(TOKEN)],
        out_specs=[blk(kw), blk(kw), blk(vw), blk(kw)],
        out_shape=[pltpu.HBM((_lp(), kw), F32), pltpu.HBM((_lp(), kw), F32),
                   pltpu.HBM((_lp(), vw), F32), pltpu.HBM((_lp(), kw), F32)],
        scratch_shapes=[pltpu.VMEM((DV, kw), F32)],
        compiler_params=_params(16, dimension_semantics=_seq()),
    )(*_hbm(qg, kg, vg, z, do_gla, st_all), token)


def _swa_bwd(sinks, qs, ks, vs, do_s, token):
    nb = SEQ // BLK
    kvw = SWA_KV_HEADS * DH
    scale = DH ** -0.5
    heads = range(SWA_HEADS)

    def body(sink_ref, q_ref, km_ref, kp_ref, kc_ref, vm_ref, vp_ref, vc_ref, do_ref, token_ref,
             dq_ref, dk_ref, dv_ref, dsink_ref, carry_k, carry_v, meta_k, meta_v):
        n = pl.program_id(0)

        @pl.when(n == 0)
        def _():
            for r in (carry_k, carry_v, meta_k, meta_v):
                r[...] = jnp.zeros_like(r)
            dsink_ref[...] = jnp.zeros_like(dsink_ref)

        @pl.when(n <= nb)
        def _():
            negdist, maskbias = _swa_bias(n)
            lane = _iota((1, LANE), 1)
            k_all = jnp.concatenate([km_ref[...], kp_ref[...], kc_ref[...]], axis=0).astype(BF16)
            v_all = jnp.concatenate([vm_ref[...], vp_ref[...], vc_ref[...]], axis=0).astype(BF16)
            q = [_swa_half(q_ref, pos, scale) for pos in heads]
            d_o = [_swa_half(do_ref, pos) for pos in heads]
            t = [_dot_nt(q[pos], k_all) + (2.0 ** -(HEAD_POS[pos] + 1) * negdist + maskbias) for pos in heads]
            dp = [_dot_nt(d_o[pos], v_all) for pos in heads]
            soft = [_swa_softmax(t[pos], sink_ref[HEAD_POS[pos]]) for pos in heads]
            p = [s[0] for s in soft]
            delta = [jnp.sum(p[pos] * dp[pos], axis=-1, keepdims=True) for pos in heads]
            ds = [(p[pos] * (dp[pos] - delta[pos])).astype(BF16) for pos in heads]
            dq = [_dot(ds[pos], k_all) for pos in heads]
            for col in range(SWA_HEADS // 2):
                dq_ref[:, col * LANE:(col + 1) * LANE] = scale * _swa_merge(dq[2 * col], dq[2 * col + 1])
            dsink = jnp.zeros((1, LANE), F32)
            for pos in heads:
                dsink = dsink + jnp.where(lane == HEAD_POS[pos],
                                          -jnp.sum(soft[pos][1] * delta[pos], axis=0, keepdims=True), 0.0)
            dsink_ref[...] += dsink
            dk3 = _dot_tn(jnp.concatenate(q, axis=0), jnp.concatenate(ds, axis=0)).T
            dv3 = _dot_tn(jnp.concatenate(d_o, axis=0), jnp.concatenate([x.astype(BF16) for x in p], axis=0)).T
            meta_k[...] += dk3[0:BLK]
            meta_v[...] += dv3[0:BLK]
            dk_ref[...] = carry_k[...] + dk3[BLK:2 * BLK]
            dv_ref[...] = carry_v[...] + dv3[BLK:2 * BLK]
            carry_k[...] = dk3[2 * BLK:3 * BLK]
            carry_v[...] = dv3[2 * BLK:3 * BLK]

        @pl.when(n == nb + 1)
        def _():
            dk_ref[...] = meta_k[...]
            dv_ref[...] = meta_v[...]

    kv_out = pl.BlockSpec((BLK, kvw), lambda n: (jnp.where(n == nb + 1, nb, jnp.clip(n - 1, 0, nb - 1)), 0))
    qblk = pl.BlockSpec((BLK, SWA_HEADS * DH), lambda n: (jnp.minimum(n, nb), 0))
    return pl.pallas_call(
        body, name="swa_bwd", grid=(nb + 2,),
        in_specs=[pl.BlockSpec(memory_space=pltpu.SMEM), qblk] + _swa_kv_specs(kvw) + _swa_kv_specs(kvw)
        + [qblk, _const(TOKEN)],
        out_specs=[qblk, kv_out, kv_out, _acc((1, LANE))],
        out_shape=[pltpu.HBM((_lp(), SWA_HEADS * DH), F32), pltpu.HBM((_lp(), kvw), F32),
                   pltpu.HBM((_lp(), kvw), F32), pltpu.HBM((1, LANE), F32)],
        scratch_shapes=[pltpu.VMEM((BLK, kvw), F32)] * 4,
        compiler_params=_params(16, dimension_semantics=_seq()),
    )(sinks, *_hbm(qs, ks, ks, ks, vs, vs, vs, do_s), token)


def _in_bwd(dqs, dks, dvs, dqg, dkg, dvg, drg, dz, dpre1, w_in_t, wg2_p):
    tm = _row_tile(384)
    lp = _lp()
    widths = (512, 128, 128, 256, 256, 512, 512)
    offs = (O_QS, O_KS, O_VS, O_QG, O_KG, O_VG, O_RG)

    def body(*refs):
        parts, (dz_ref, dp1_ref, w_ref, wg2_ref, dproj_ref, dh0_ref, dbin_ref, dbg_ref) = refs[:7], refs[7:]

        @pl.when(pl.program_id(0) == 0)
        def _():
            dbin_ref[...] = jnp.zeros_like(dbin_ref)
            dbg_ref[...] = jnp.zeros_like(dbg_ref)

        for pos, h in enumerate(HEAD_POS):
            val = parts[0][:, pos * DH:(pos + 1) * DH]
            dproj_ref[:, O_QS + h * DH:O_QS + (h + 1) * DH] = val.astype(BF16)
            dbin_ref[:, O_QS + h * DH:O_QS + (h + 1) * DH] += jnp.sum(val, axis=0, keepdims=True)
        for p_ref, off, wd in zip(parts[1:], offs[1:], widths[1:]):
            val = p_ref[...]
            dproj_ref[:, off:off + wd] = val.astype(BF16)
            dbin_ref[:, off:off + wd] += jnp.sum(val, axis=0, keepdims=True)
        dz = dz_ref[...]
        dlr = _dot_nt(dz, wg2_ref[...])
        dproj_ref[:, O_LR:O_LR + LANE] = dlr.astype(BF16)
        dbin_ref[:, O_LR:O_LR + LANE] += jnp.sum(dlr, axis=0, keepdims=True)
        dbg_ref[...] += jnp.sum(dz, axis=0, keepdims=True)
        dh0_ref[...] = ALPHA * dp1_ref[...] + _dot(dproj_ref[...], w_ref[...])

    return pl.pallas_call(
        body, name="in_bwd", grid=(lp // tm,),
        in_specs=[_rows(tm, w) for w in widths] + [_rows(tm, 256), _rows(tm, D), _const((D_IN_P, D)), _const((LANE, 256))],
        out_specs=[_rows(tm, D_IN_P), _rows(tm, D), _acc((1, D_IN_P)), _acc((1, 256))],
        out_shape=[pltpu.HBM((lp, D_IN_P), BF16), pltpu.HBM((lp, D), F32),
                   pltpu.HBM((1, D_IN_P), F32), pltpu.HBM((1, 256), F32)],
        compiler_params=_params(40, dimension_semantics=_seq()),
    )(*_hbm(dqs, dks, dvs, dqg, dkg, dvg, drg, dz, dpre1, w_in_t, wg2_p))


def _ln_in_bwd(x, meta_ext, dh0, g, token):
    tr = min(LN_ROWS, SEQ)

    def ln_bwd(x_ref, dh_ref, g_ref, dx_ref, dg_ref, db_ref):
        @pl.when(pl.program_id(0) == 0)
        def _():
            dg_ref[...] = jnp.zeros_like(dg_ref)
            db_ref[...] = jnp.zeros_like(db_ref)

        xhat, rstd = _ln_stats(x_ref[...])
        dh = dh_ref[...]
        dx_ref[...] = _ln_bwd(dh, xhat, rstd, g_ref[...])
        dg_ref[...] += jnp.sum(dh * xhat, axis=0, keepdims=True)
        db_ref[...] += jnp.sum(dh, axis=0, keepdims=True)

    def body(x_ref, dh_ref, g_ref, token_ref, dx_ref, dg_ref, db_ref):
        ln_bwd(x_ref, dh_ref, g_ref, dx_ref, dg_ref, db_ref)

    def meta_body(m_ref, dh_ref, g_ref, dm_ref, dg_ref, db_ref):
        ln_bwd(m_ref, dh_ref, g_ref, dm_ref, dg_ref, db_ref)

    sums = [pltpu.HBM((1, D), F32), pltpu.HBM((1, D), F32)]
    dx, dg, db = pl.pallas_call(
        body, name="ln_in_bwd", grid=(SEQ // tr,),
        in_specs=[_rows(tr, D), _rows(tr, D), _const((1, D)), _const(TOKEN)],
        out_specs=[_rows(tr, D), _acc((1, D)), _acc((1, D))],
        out_shape=[pltpu.HBM((SEQ, D), F32)] + sums,
        compiler_params=_params(32, dimension_semantics=_seq()),
    )(*_hbm(x, dh0, g), token)
    dm, dg_m, db_m = pl.pallas_call(
        meta_body, name="ln_in_bwd_meta", grid=(1,),
        in_specs=[_const((BLK, D)), pl.BlockSpec((BLK, D), lambda i: (SEQ // BLK, 0)), _const((1, D))],
        out_specs=[_acc((BLK, D)), _acc((1, D)), _acc((1, D))],
        out_shape=[pltpu.HBM((BLK, D), F32)] + sums,
        compiler_params=_params(16, dimension_semantics=_seq()),
    )(*_hbm(meta_ext, dh0, g))
    return dx, dm, dg + dg_m, db + db_m


def _local_step(x, target, ln_in_g, ln_in_b, b_in, bg2, sinks, gn, g1, b1, g2, b2,
                token, fetch_first, fetch_rest, fetch_ffn, exchange_ffn, ship_ffn, ship_w_in):
    row = lambda v: v.reshape(1, -1).astype(F32)
    b_in_p = jnp.pad(row(b_in), ((0, 0), (0, D_IN_P - D_IN)))
    gn4 = row(gn)
    sinks = sinks.reshape(-1).astype(F32)

    h_real = _ln_in_fwd_real(x, row(ln_in_g), row(ln_in_b), token)
    w_in_windows, meta_full, wg2 = fetch_first([h_real])
    meta_ext = jnp.pad(meta_full, ((META_OFF, BLK - CH), (0, 0)))
    wg2_p = jnp.pad(wg2, ((0, LANE - wg2.shape[0]), (0, 0))).astype(BF16)
    h0 = _ln_in_fwd_meta(h_real, meta_ext, row(ln_in_g), row(ln_in_b))
    qs, ks, vs, qg, kg, vg, rg, glr, z, w_in_t = _in_proj(h0, w_in_windows, b_in_p, wg2_p, row(bg2))
    o_s = _swa_fwd(sinks, qs, ks, vs)
    o_gla, st_all = _gla_fwd(qg, kg, vg, z)
    w_out, token = fetch_rest([o_s, o_gla])
    o, pre1, h1 = _post_mix(o_s, o_gla, rg, h0, gn4, w_out, row(g1), row(b1), token)
    wg_t, wu_t, wd = fetch_ffn([pre1])
    a, dgate, dup, dpre2, loss, dg2, db2 = _ffn_fwd_loss_bwd(h1, wg_t, wu_t, wd, target, row(g2), row(b2))
    dpre1, dg1, db1, do_s, do_gla, drg, dgn = _ffn_out_bwd(dpre2, dgate, dup, pre1, wg_t, wu_t, row(g1), w_out, o_gla,
                                                           rg, gn4)
    dwd = _atb(a, dpre2, "dw_down")
    dwg_t = _atb(dgate, h1, "dw_gate")
    dwu_t = _atb(dup, h1, "dw_up")
    token = exchange_ffn(dict(w_out=_atb(o, dpre1, "dw_out"), w_g=dwg_t, w_u=dwu_t, w_d=dwd))
    dqg, dkg, dvg, dz = _gla_bwd(qg, kg, vg, z, do_gla, st_all, token)
    token = ship_ffn([dqg])
    dqs, dks, dvs, dsinks = _swa_bwd(sinks, qs, ks, vs, do_s, token)
    dproj, dh0, db_in_p, dbg2 = _in_bwd(dqs, dks, dvs, dqg, dkg, dvg, drg, dz, dpre1, w_in_t, wg2_p)
    token = ship_w_in(_atb(dproj, h0, "dw_in", windows=(W_IN_STARTS, W_IN_WIN)))
    dwg2_p = _atb(glr, dz, "dw_gate_lr2")
    dx, dmeta_blk, dg_in, db_in_ln = _ln_in_bwd(x, meta_ext, dh0, row(ln_in_g), token)

    small = dict(meta_blk=dmeta_blk, ln_in_g=dg_in, ln_in_b=db_in_ln, ln1_g=dg1, ln1_b=db1, ln2_g=dg2, ln2_b=db2,
                 b_in_p=db_in_p, wg2_p=dwg2_p, bg2=dbg2, sinks=dsinks, gn=dgn, loss=loss)
    return dx, small


HBM = pl.BlockSpec(memory_space=pltpu.HBM)


def _place():
    return lax.axis_index("x"), lax.axis_index("y"), lax.axis_index("c")


def _other_chips(x, y):
    return [(1 - x, y), (x, 1 - y), (1 - x, 1 - y)]


def _dma_sems(n):
    return pltpu.SemaphoreType.DMA((n,))


def _comm_params():
    return pltpu.CompilerParams(has_side_effects=True)


SEM = pl.BlockSpec(memory_space=pltpu.SEMAPHORE)


PER_ARRAY = dict(gather=3, scatter=3, sibling=N_CHIPS)


def _ici_copies(kind, landing, srcs, lands, send_sems, recv_sems):
    x, y, c = _place()
    mine = 2 * x + y
    copies = []
    for a in range(len(srcs)):
        if kind == "sibling":
            for s in range(N_CHIPS):
                copies.append(pltpu.make_async_remote_copy(
                    srcs[a].at[s, 1 - c], lands[a].at[s], send_sems.at[N_CHIPS * a + s], recv_sems.at[N_CHIPS * a + s],
                    device_id=(x, y, 1 - c), device_id_type=MESH))
            continue
        for j, (px, py) in enumerate(_other_chips(x, y)):
            slab = 2 * px + py if landing else mine
            if kind == "gather":
                src, dst = srcs[a].at[c], lands[a].at[slab, c]
            else:
                src, dst = srcs[a].at[2 * px + py], lands[a].at[slab]
            copies.append(pltpu.make_async_remote_copy(src, dst, send_sems.at[3 * a + j], recv_sems.at[3 * a + j],
                                                       device_id=(px, py, c), device_id_type=MESH))
    return copies


def _split_params():
    return pltpu.CompilerParams(has_side_effects=pltpu.SideEffectType.DATAFLOW_SIDE_EFFECTING)


def _ici_start(kind, srcs, land_shapes, after, name):
    n = len(srcs)
    lands = [pltpu.with_memory_space_constraint(lax.empty(s, a.dtype), pltpu.HBM) for s, a in zip(land_shapes, srcs)]

    def body(*refs):
        outs = refs[2 * n + len(after):]
        for cp in _ici_copies(kind, False, refs[:n], refs[n:2 * n], outs[0], outs[1]):
            cp.start()
        outs[-1][...] = jnp.zeros(TOKEN, F32)

    outs = pl.pallas_call(
        body, name=name, in_specs=[HBM] * (2 * n) + [pl.BlockSpec(memory_space=pl.ANY)] * len(after),
        out_specs=[SEM, SEM] + [HBM] * (2 * n) + [pl.BlockSpec(memory_space=pltpu.VMEM)],
        out_shape=[_dma_sems(PER_ARRAY[kind] * n)] * 2 + [pltpu.HBM(a.shape, a.dtype) for a in list(srcs) + lands]
        + [jax.ShapeDtypeStruct(TOKEN, F32)],
        input_output_aliases={i: 2 + i for i in range(2 * n)},
        compiler_params=_split_params(),
    )(*_hbm(*srcs), *lands, *after)
    return outs[:-1], outs[-1]


def _ici_wait(kind, handle, after, name):
    n = (len(handle) - 2) // 2

    def body(*refs):
        for cp in _ici_copies(kind, True, refs[:n], refs[n:2 * n], refs[2 * n], refs[2 * n + 1]):
            cp.wait_send()
            cp.wait_recv()

    outs = pl.pallas_call(
        body, name=name, in_specs=[HBM] * (2 * n) + [SEM, SEM] + [pl.BlockSpec(memory_space=pl.ANY)] * len(after),
        out_specs=[HBM] * (2 * n), out_shape=[pltpu.HBM(a.shape, a.dtype) for a in handle[2:]],
        input_output_aliases={i: i for i in range(2 * n)},
        compiler_params=_split_params(),
    )(*handle[2:], handle[0], handle[1], *after)
    return list(outs[:n]), list(outs[n:])


def _forward_copies(landing, arrs, send_sems, recv_sems):
    x, y, c = _place()
    copies = []
    for a in range(len(arrs)):
        for j, (px, py) in enumerate(_other_chips(x, y)):
            half = 1 - c if landing else c
            copies.append(pltpu.make_async_remote_copy(
                arrs[a].at[2 * px + py, c], arrs[a].at[2 * px + py, half], send_sems.at[3 * a + j],
                recv_sems.at[3 * a + j], device_id=(x, y, 1 - c), device_id_type=MESH))
    return copies


def _sibling_forward(lands, name):
    n = len(lands)

    def body(*refs):
        outs = refs[n:2 * n]
        send_sems, recv_sems = refs[2 * n:]
        sends = _forward_copies(False, outs, send_sems, recv_sems)
        for cp in sends:
            cp.start()
        for cp in _forward_copies(True, outs, send_sems, recv_sems):
            cp.wait_recv()
        for cp in sends:
            cp.wait_send()

    return pl.pallas_call(
        body, name=name, in_specs=[HBM] * n, out_specs=[HBM] * n,
        out_shape=[pltpu.HBM(a.shape, a.dtype) for a in lands],
        input_output_aliases={a: a for a in range(n)},
        scratch_shapes=[_dma_sems(3 * n)] * 2,
        compiler_params=_comm_params(),
    )(*_hbm(*lands))


def _forward_start(lands, name):
    n = len(lands)

    def body(*refs):
        outs = refs[n:]
        for cp in _forward_copies(False, refs[:n], outs[0], outs[1]):
            cp.start()
        outs[-1][...] = jnp.zeros(TOKEN, F32)

    outs = pl.pallas_call(
        body, name=name, in_specs=[HBM] * n,
        out_specs=[SEM, SEM] + [HBM] * n + [pl.BlockSpec(memory_space=pltpu.VMEM)],
        out_shape=[_dma_sems(3 * n)] * 2 + [pltpu.HBM(a.shape, a.dtype) for a in lands]
        + [jax.ShapeDtypeStruct(TOKEN, F32)],
        input_output_aliases={i: 2 + i for i in range(n)},
        compiler_params=_split_params(),
    )(*_hbm(*lands))
    return outs[:-1], outs[-1]


def _forward_wait(handle, after, name):
    n = len(handle) - 2

    def body(*refs):
        for cp in _forward_copies(True, refs[:n], refs[n], refs[n + 1]):
            cp.wait_send()
            cp.wait_recv()

    return list(pl.pallas_call(
        body, name=name, in_specs=[HBM] * n + [SEM, SEM] + [pl.BlockSpec(memory_space=pl.ANY)] * len(after),
        out_specs=[HBM] * n, out_shape=[pltpu.HBM(a.shape, a.dtype) for a in handle[2:]],
        input_output_aliases={i: i for i in range(n)},
        compiler_params=_split_params(),
    )(*handle[2:], handle[0], handle[1], *after))


def _sibling_exchange(grads, name):
    n = len(grads)

    def body(*refs):
        ins, outs = refs[:n], refs[n:2 * n]
        send_sems, recv_sems = refs[2 * n:]
        x, y, c = _place()
        copies = []
        for a in range(n):
            for s in range(N_CHIPS):
                cp = pltpu.make_async_remote_copy(ins[a].at[s, 1 - c], outs[a].at[s], send_sems.at[N_CHIPS * a + s],
                                                  recv_sems.at[N_CHIPS * a + s], device_id=(x, y, 1 - c),
                                                  device_id_type=MESH)
                cp.start()
                copies.append(cp)
        for cp in copies:
            cp.wait_recv()
        for cp in copies:
            cp.wait_send()

    return pl.pallas_call(
        body, name=name, in_specs=[HBM] * n, out_specs=[HBM] * n,
        out_shape=[pltpu.HBM((N_CHIPS, g.shape[2], D), F32) for g in grads],
        scratch_shapes=[_dma_sems(N_CHIPS * n)] * 2,
        compiler_params=_comm_params(),
    )(*_hbm(*grads))


def _add_halves(core, grads, recvs, dtypes, name):
    n = len(grads)
    heights = [g.shape[2] for g in grads]

    def body(c_ref, *refs):
        for a in range(n):
            refs[2 * n + a][...] = (refs[2 * a][0] + refs[2 * a + 1][...]).astype(dtypes[a])

    slab = lambda h: pl.BlockSpec((1, h, D), lambda s, c: (s, 0, 0))
    mine = lambda h: pl.BlockSpec((1, 1, h, D), lambda s, c: (s, c[0], 0, 0))
    return pl.pallas_call(
        body, name=name,
        grid_spec=pltpu.PrefetchScalarGridSpec(
            num_scalar_prefetch=1, grid=(N_CHIPS,),
            in_specs=[spec(h) for h in heights for spec in (mine, slab)], out_specs=[slab(h) for h in heights]),
        out_shape=[pltpu.HBM((N_CHIPS, h, D), dt) for h, dt in zip(heights, dtypes)],
        compiler_params=_params(32, dimension_semantics=_seq()),
    )(core, *_hbm(*[a for pair in zip(grads, recvs) for a in pair]))


N_DEVICES = 2 * N_CHIPS
PEER_FLIPS = [(dx, dy, dc) for dx in (0, 1) for dy in (0, 1) for dc in (0, 1)][1:]


def _small_copies(landing, p_ref, out_ref, send_sems, recv_sems):
    x, y, c = _place()
    flip = lambda v, d: 1 - v if d else v
    copies = []
    for k, flips in enumerate(PEER_FLIPS):
        px, py, pc = (flip(v, d) for v, d in zip((x, y, c), flips))
        slab = 4 * px + 2 * py + pc if landing else 4 * x + 2 * y + c
        copies.append(pltpu.make_async_remote_copy(p_ref, out_ref.at[slab], send_sems.at[k], recv_sems.at[k],
                                                   device_id=(px, py, pc), device_id_type=MESH))
    return copies


def _small_start(pack, after):
    n = len(PEER_FLIPS)
    land = pltpu.with_memory_space_constraint(lax.empty((N_DEVICES,) + pack.shape, F32), pltpu.HBM)

    def body(p_ref, land_ref, *refs):
        outs = refs[len(after):]
        for cp in _small_copies(False, p_ref, land_ref, outs[0], outs[1]):
            cp.start()
        outs[-1][...] = jnp.zeros(TOKEN, F32)

    outs = pl.pallas_call(
        body, name="small_exchange_start", in_specs=[HBM, HBM] + [pl.BlockSpec(memory_space=pl.ANY)] * len(after),
        out_specs=[SEM, SEM, HBM, HBM, pl.BlockSpec(memory_space=pltpu.VMEM)],
        out_shape=[_dma_sems(n), _dma_sems(n), pltpu.HBM(pack.shape, F32), pltpu.HBM(land.shape, F32),
                   jax.ShapeDtypeStruct(TOKEN, F32)],
        input_output_aliases={0: 2, 1: 3},
        compiler_params=_split_params(),
    )(*_hbm(pack), land, *after)
    return outs[:-1], outs[-1]


def _small_wait(handle, after):
    def body(p_ref, land_ref, send_sems, recv_sems, *rest):
        for cp in _small_copies(True, p_ref, land_ref, send_sems, recv_sems):
            cp.wait_send()
            cp.wait_recv()

    return pl.pallas_call(
        body, name="small_exchange_wait", in_specs=[HBM, HBM, SEM, SEM] + [pl.BlockSpec(memory_space=pl.ANY)] * len(after),
        out_specs=[HBM, HBM], out_shape=[pltpu.HBM(a.shape, F32) for a in handle[2:]],
        input_output_aliases={0: 0, 1: 1},
        compiler_params=_split_params(),
    )(handle[2], handle[3], handle[0], handle[1], *after)


def _sum_chips(slots, firsts, rests, after):
    n = len(firsts)

    def body(i_ref, *refs):
        outs = refs[4 * n + len(after):]
        for a in range(n):
            first, r1, r2, r3 = refs[4 * a:4 * a + 4]
            outs[a][...] = ((first[...].astype(F32) + r1[...].astype(F32)) + r2[...].astype(F32)) + r3[...].astype(F32)

    slab = lambda h, k: pl.BlockSpec((1, h, D), lambda i, ix: (ix[k], 0, 0))
    heights = [f.shape[1] for f in firsts]
    return pl.pallas_call(
        body, name="sum_chips",
        grid_spec=pltpu.PrefetchScalarGridSpec(
            num_scalar_prefetch=1, grid=(1,),
            in_specs=[slab(h, k) for h in heights for k in range(4)] + [pl.BlockSpec(memory_space=pl.ANY)] * len(after),
            out_specs=[slab(h, 4) for h in heights]),
        out_shape=[pltpu.HBM((2, h, D), F32) for h in heights],
        compiler_params=_params(48, dimension_semantics=_seq()),
    )(slots, *_hbm(*[a for f, r in zip(firsts, rests) for a in (f, r, r, r)]), *after)


def _join_halves(halves):
    n = len(halves)

    def body(*refs):
        outs = refs[n:2 * n]
        send_sems, recv_sems = refs[2 * n:]
        x, y, c = _place()

        def copy(a, slab):
            return pltpu.make_async_remote_copy(outs[a].at[slab], outs[a].at[slab], send_sems.at[a], recv_sems.at[a],
                                                device_id=(x, y, 1 - c), device_id_type=MESH)

        for a in range(n):
            copy(a, c).start()
        for a in range(n):
            copy(a, 1 - c).wait_recv()
        for a in range(n):
            copy(a, c).wait_send()

    return pl.pallas_call(
        body, name="join_halves", in_specs=[HBM] * n, out_specs=[HBM] * n,
        out_shape=[pltpu.HBM(h.shape, F32) for h in halves],
        input_output_aliases={a: a for a in range(n)},
        scratch_shapes=[_dma_sems(n)] * 2,
        compiler_params=_comm_params(),
    )(*_hbm(*halves))


def _chip_partials(grads, wire_dtypes, names, fetched=()):
    core = lax.axis_index("c").astype(jnp.int32).reshape(1)
    todo = len(grads) - len(fetched)
    recv = (list(_sibling_exchange(grads[:todo], "sibling_exchange_" + names[0])) if todo else []) + list(fetched)
    return list(_add_halves(core, grads, recv, wire_dtypes, "add_halves_" + names[0]))


def _finish_reduce(parts, got, after):
    x, y, c = _place()
    others = [2 * px + py for px, py in _other_chips(x, y)]
    own_first = jnp.stack([2 * x + y] + others + [c]).astype(jnp.int32)
    return [f.reshape(2 * f.shape[1], D) for f in _join_halves(_sum_chips(own_first, parts, got, after))]


ADAMW_STEPS = 8


def _adamw(params, by_row, chip, window_step):
    n = len(params)
    rows, _, cols = by_row[0].shape
    block = lambda shape: pl.BlockSpec((shape[0] // ADAMW_STEPS, shape[1]), lambda i, c: (i, 0))
    assert all(a.shape[0] % (8 * ADAMW_STEPS) == 0 for p in params for a in p)

    def body(c_ref, *refs):
        w_hbm, g_ref, m_hbm, v_hbm = refs[4 * n:4 * n + 4]
        results, (ins_ref, outs_ref, sems) = refs[8 * n + 4:8 * n + 8], refs[8 * n + 8:]
        loads = [pltpu.make_async_copy(src.at[:, 0, :], ins_ref.at[k], sems.at[k])
                 for k, src in enumerate((w_hbm, m_hbm, v_hbm))]
        stores = [pltpu.make_async_copy(outs_ref.at[k], dst.at[:, 0, :], sems.at[3 + k]) for k, dst in enumerate(results)]
        first = pl.program_id(0) == 0

        @pl.when(first)
        def _():
            for cp in loads:
                cp.start()

        for a in range(n):
            w_ref, a_g_ref, m_ref, v_ref = refs[4 * a:4 * a + 4]
            outs = refs[4 * n + 4 + 4 * a:4 * n + 8 + 4 * a]
            g = a_g_ref[...]
            outs[0][...] = g
            outs[1][...], outs[2][...], outs[3][...] = _adamw_math(w_ref[...], g, m_ref[...], v_ref[...])

        @pl.when(first)
        def _():
            for cp in loads:
                cp.wait()
            for lo in range(0, cols, LANE):
                lanes = slice(lo, lo + LANE)
                g = g_ref[0:rows, lanes]
                for s in range(1, N_CHIPS):
                    g = jnp.where(c_ref[0] == s, g_ref[s * window_step:s * window_step + rows, lanes], g)
                outs_ref[0, :, lanes] = g
                outs_ref[1, :, lanes], outs_ref[2, :, lanes], outs_ref[3, :, lanes] = _adamw_math(
                    ins_ref[0, :, lanes], g, ins_ref[1, :, lanes], ins_ref[2, :, lanes])
            for cp in stores:
                cp.start()

        @pl.when(pl.program_id(0) == ADAMW_STEPS - 1)
        def _():
            for cp in stores:
                cp.wait()

    outs = pl.pallas_call(
        body, name="adamw_matrices",
        grid_spec=pltpu.PrefetchScalarGridSpec(
            num_scalar_prefetch=1, grid=(ADAMW_STEPS,),
            in_specs=[block(a.shape) for p in params for a in p] + [HBM, _const(by_row[1].shape), HBM, HBM],
            out_specs=[block(p[0].shape) for p in params for _ in range(4)] + [HBM] * 4,
            scratch_shapes=[pltpu.VMEM((3, rows, cols), F32), pltpu.VMEM((4, rows, cols), F32), _dma_sems(7)]),
        out_shape=[pltpu.HBM(p[0].shape, F32) for p in params for _ in range(4)] + [pltpu.HBM((rows, 1, cols), F32)] * 4,
        compiler_params=_params(48, dimension_semantics=_seq()),
    )(chip, *_hbm(*[a for p in params for a in p], *by_row))
    return [outs[4 * a:4 * a + 4] for a in range(n)], outs[4 * n:]


def _adamw_math(w, g, m, v):
    nm = ADAM_B1 * m + (1.0 - ADAM_B1) * g
    nv = ADAM_B2 * v + (1.0 - ADAM_B2) * (g * g)
    m_hat = nm / (1.0 - ADAM_B1 ** ADAM_STEP)
    v_hat = nv / (1.0 - ADAM_B2 ** ADAM_STEP)
    return -ADAM_LR * (m_hat / (jnp.sqrt(v_hat) + ADAM_EPS) + ADAM_WD * w), nm, nv


SMALL = (("meta_tokens", (N_META, D // N_CHIPS)), ("ln_in_g", (1, D)), ("ln_in_b", (1, D)), ("b_in", (1, D_IN)),
         ("w_gate_lr2", (GATE_RANK, GLA_HEADS * DK // N_CHIPS)), ("b_gate_lr2", (1, GLA_HEADS * DK)),
         ("attn_sinks", (1, SWA_HEADS)),
         ("gla_norm_g", (1, DV)), ("ln1_g", (1, D)), ("ln1_b", (1, D)), ("ln2_g", (1, D)), ("ln2_b", (1, D)))
ROW_META, ROW_B_IN, ROW_TAIL, ROW_WG2 = 0, 22, 25, 32
ROW_LN = dict(ln_in_g=16, ln_in_b=17, ln1_g=18, ln1_b=19, ln2_g=20, ln2_b=21)
TAIL_BG2, TAIL_SINKS, TAIL_GN, TAIL_LOSS = 0, 256, 256 + SWA_HEADS, 256 + SWA_HEADS + DV


def _adamw_small(place, packs, own, params):
    n = len(SMALL)

    def body(place_ref, packs_ref, own_ref, *refs):
        ins, outs, p_ref = refs[:3 * n], refs[3 * n:-1], refs[-1]
        me, c = place_ref[0], place_ref[1]
        total = jnp.where(me == 0, own_ref[...], packs_ref[0])
        for i in range(1, N_DEVICES):
            total = total + jnp.where(me == i, own_ref[...], packs_ref[i])
        p_ref[...] = total
        outs[4 * n][...] = total[ROW_TAIL:ROW_TAIL + 1, :]

        def mine(width, rows):
            part = lambda s: p_ref[rows, s * width:(s + 1) * width]
            return jnp.where(c == 0, part(0), jnp.where(c == 1, part(1), jnp.where(c == 2, part(2), part(3))))

        tail = lambda lo, width: p_ref[ROW_TAIL:ROW_TAIL + 1, lo:lo + width]
        grads = dict(
            meta_tokens=mine(D // N_CHIPS, slice(ROW_META, ROW_META + N_META)),
            b_in=jnp.concatenate([p_ref[ROW_B_IN:ROW_B_IN + 1, :], p_ref[ROW_B_IN + 1:ROW_B_IN + 2, :],
                                  p_ref[ROW_B_IN + 2:ROW_B_IN + 3, 0:D_IN - 2 * D]], axis=1),
            w_gate_lr2=mine(256 // N_CHIPS, slice(ROW_WG2, ROW_WG2 + 16)),
            b_gate_lr2=tail(TAIL_BG2, 256), attn_sinks=tail(TAIL_SINKS, SWA_HEADS), gla_norm_g=tail(TAIL_GN, DV),
            **{k: p_ref[r:r + 1, :] for k, r in ROW_LN.items()})
        for i, (name, _) in enumerate(SMALL):
            g = grads[name]
            outs[4 * i][...] = g
            outs[4 * i + 1][...], outs[4 * i + 2][...], outs[4 * i + 3][...] = _adamw_math(
                ins[3 * i][...], g, ins[3 * i + 1][...], ins[3 * i + 2][...])

    whole = lambda shape: pl.BlockSpec(shape, lambda i, c: (0,) * len(shape))
    outs = pl.pallas_call(
        body, name="adamw_small",
        grid_spec=pltpu.PrefetchScalarGridSpec(
            num_scalar_prefetch=1, grid=(1,),
            in_specs=[whole(packs.shape), whole(own.shape)] + [whole(s) for _, s in SMALL for _ in range(3)],
            out_specs=[whole(s) for _, s in SMALL for _ in range(4)] + [whole((1, D))],
            scratch_shapes=[pltpu.VMEM(own.shape, F32)]),
        out_shape=[pltpu.HBM(s, F32) for _, s in SMALL for _ in range(4)] + [pltpu.HBM((1, D), F32)],
        compiler_params=_params(16, dimension_semantics=_seq()),
    )(place, *_hbm(packs, own, *[a for p in params for a in p]))
    return [outs[4 * i:4 * i + 4] for i in range(n)], outs[4 * n]


def _small_pack(gr):
    names = ["meta_blk"] + list(ROW_LN) + ["b_in_p", "wg2_p", "bg2", "sinks", "gn", "loss"]
    gate_w = GLA_HEADS * DK

    def body(*refs):
        src, out = dict(zip(names, refs)), refs[-1]
        out[...] = jnp.zeros_like(out)
        out[ROW_META:ROW_META + N_META, :] = src["meta_blk"][META_OFF:CH, :]
        for k, r in ROW_LN.items():
            out[r:r + 1, :] = src[k][...]
        for j in range(-(-D_IN // D)):
            width = min(D, D_IN - j * D)
            out[ROW_B_IN + j:ROW_B_IN + j + 1, 0:width] = src["b_in_p"][:, j * D:j * D + width]
        tail = slice(ROW_TAIL, ROW_TAIL + 1)
        out[tail, TAIL_BG2:TAIL_BG2 + gate_w] = src["bg2"][...]
        out[tail, TAIL_SINKS:TAIL_SINKS + SWA_HEADS] = src["sinks"][:, 0:SWA_HEADS]
        out[tail, TAIL_GN:TAIL_GN + DV] = src["gn"][...]
        out[tail, TAIL_LOSS:TAIL_LOSS + 1] = src["loss"][:, 0:1]
        out[ROW_WG2:ROW_WG2 + GATE_RANK, 0:gate_w] = src["wg2_p"][0:GATE_RANK, :]

    arrays = [gr[k] for k in names]
    return pl.pallas_call(
        body, name="small_pack", grid=(1,),
        in_specs=[_acc(a.shape) for a in arrays], out_specs=_acc((SMALL_ROWS, D)),
        out_shape=pltpu.HBM((SMALL_ROWS, D), F32),
        compiler_params=_params(16, dimension_semantics=_seq()),
    )(*_hbm(*arrays))


BIG = ("w_in", "w_out", "w_g", "w_u", "w_d")


def kernel(x, meta_tokens, ln_in_g, ln_in_b, w_in, b_in, w_gate_lr2, b_gate_lr2, attn_sinks, gla_norm_g, w_out, ln1_g, ln1_b, w_ffn_gate, w_ffn_up, w_ffn_down, ln2_g, ln2_b, loss_target, m_meta_tokens, m_ln_in_g, m_ln_in_b, m_w_in, m_b_in, m_w_gate_lr2, m_b_gate_lr2, m_attn_sinks, m_gla_norm_g, m_w_out, m_ln1_g, m_ln1_b, m_w_ffn_gate, m_w_ffn_up, m_w_ffn_down, m_ln2_g, m_ln2_b, v_meta_tokens, v_ln_in_g, v_ln_in_b, v_w_in, v_b_in, v_w_gate_lr2, v_b_gate_lr2, v_attn_sinks, v_gla_norm_g, v_w_out, v_ln1_g, v_ln1_b, v_w_ffn_gate, v_w_ffn_up, v_w_ffn_down, v_ln2_g, v_ln2_b):
    chip = 2 * lax.axis_index("x") + lax.axis_index("y")

    halves = lambda a: a.reshape(2, a.shape[0] // 2, a.shape[1])
    r_in = SHARD_ROWS["w_in"]
    first = [halves(a) for a in (jnp.pad(w_in[0].T.astype(BF16), ((0, W_IN_WIN - r_in), (0, 0))), meta_tokens,
                                 w_gate_lr2[0])]
    rest = [halves(a) for a in (w_out[0].astype(BF16), w_ffn_gate[0].T.astype(BF16), w_ffn_up[0].T.astype(BF16),
                                w_ffn_down[0].astype(BF16))]
    lands = lambda arrs: [(N_CHIPS,) + a.shape for a in arrs]
    first_handle, first_token = _ici_start("gather", first, lands(first), [], "gather_first_start")
    rest_handle, token = _ici_start("gather", rest, lands(rest), [first_token], "gather_rest_start")
    own_slab = lambda got, shards: [lax.dynamic_update_index_in_dim(g, s, chip, axis=0) for g, s in zip(got, shards)]
    fetching = {}

    def fetch_first(after):
        shards, landed = _ici_wait("gather", first_handle, after, "gather_first_wait")
        g_in, g_meta, g_wg2 = own_slab(_sibling_forward(landed, "gather_first_forward"), shards)
        w_in_windows = g_in.reshape(N_CHIPS, W_IN_WIN, D)
        meta_full = jnp.concatenate([g_meta[s].reshape(N_META, -1) for s in range(N_CHIPS)], axis=1)
        wg2_full = jnp.concatenate([g_wg2[s].reshape(w_gate_lr2.shape[1], -1) for s in range(N_CHIPS)], axis=1)
        return w_in_windows, meta_full, wg2_full

    def fetch_rest(after):
        shards, landed = _ici_wait("gather", rest_handle, after, "gather_rest_wait")
        g_out, = own_slab(_sibling_forward(landed[:1], "gather_w_out_forward"), shards[:1])
        fetching["shards"] = shards[1:]
        fetching["handle"], forward_token = _forward_start(landed[1:], "gather_ffn_forward_start")
        return g_out.reshape(-1, D), forward_token

    def fetch_ffn(after):
        got = _forward_wait(fetching["handle"], after, "gather_ffn_forward_wait")
        return [g.reshape(-1, D) for g in own_slab(got, fetching["shards"])]

    sent = {}
    split = lambda grads: [g.reshape(N_CHIPS, 2, -1, D) for g in grads]

    def ship(key, grads, names, fetched=()):
        parts = _chip_partials(grads, [BF16] * len(grads), names, fetched)
        sent[key], ship_token = _ici_start("scatter", parts, [p.shape for p in parts], [], "scatter_" + key + "_start")
        return ship_token

    def exchange_ffn(g):
        grads = split([g[k] for k in BIG[1:]])
        sent["ffn_halves"], exchange_token = _ici_start("sibling", grads, [(N_CHIPS,) + a.shape[2:] for a in grads], [],
                                                        "sibling_ffn_start")
        return exchange_token

    def ship_ffn(after):
        grads, fetched = _ici_wait("sibling", sent["ffn_halves"], after, "sibling_ffn_wait")
        return ship("ffn", grads, list(BIG[1:]), fetched)

    def ship_w_in(dw_in_windows):
        return ship("w_in", split([dw_in_windows]), ["w_in"])

    dx, gr = _local_step(
        x[0], loss_target[0], ln_in_g, ln_in_b, b_in[0], b_gate_lr2[0], attn_sinks[0], gla_norm_g[0], ln1_g[0],
        ln1_b[0], ln2_g[0], ln2_b[0], token, fetch_first, fetch_rest, fetch_ffn, exchange_ffn, ship_ffn, ship_w_in)
    ffn_parts, ffn_got = _ici_wait("scatter", sent["ffn"], [dx], "scatter_ffn_wait")
    w_in_parts, w_in_got = _ici_wait("scatter", sent["w_in"], [dx], "scatter_w_in_wait")

    small_handle, token = _small_start(_small_pack(gr), [w_in_got[0]])
    red = _finish_reduce(w_in_parts + ffn_parts, w_in_got + ffn_got, [token])

    big_g = dict(zip(BIG, red))
    weights = dict(meta_tokens=meta_tokens, ln_in_g=ln_in_g, ln_in_b=ln_in_b, w_in=w_in, b_in=b_in,
                   w_gate_lr2=w_gate_lr2, b_gate_lr2=b_gate_lr2, attn_sinks=attn_sinks, gla_norm_g=gla_norm_g,
                   w_out=w_out, ln1_g=ln1_g, ln1_b=ln1_b, w_ffn_gate=w_ffn_gate, w_ffn_up=w_ffn_up,
                   w_ffn_down=w_ffn_down, ln2_g=ln2_g, ln2_b=ln2_b)
    m_in = dict(meta_tokens=m_meta_tokens, ln_in_g=m_ln_in_g, ln_in_b=m_ln_in_b, w_in=m_w_in, b_in=m_b_in,
                w_gate_lr2=m_w_gate_lr2, b_gate_lr2=m_b_gate_lr2, attn_sinks=m_attn_sinks, gla_norm_g=m_gla_norm_g,
                w_out=m_w_out, ln1_g=m_ln1_g, ln1_b=m_ln1_b, w_ffn_gate=m_w_ffn_gate, w_ffn_up=m_w_ffn_up,
                w_ffn_down=m_w_ffn_down, ln2_g=m_ln2_g, ln2_b=m_ln2_b)
    v_in = dict(meta_tokens=v_meta_tokens, ln_in_g=v_ln_in_g, ln_in_b=v_ln_in_b, w_in=v_w_in, b_in=v_b_in,
                w_gate_lr2=v_w_gate_lr2, b_gate_lr2=v_b_gate_lr2, attn_sinks=v_attn_sinks, gla_norm_g=v_gla_norm_g,
                w_out=v_w_out, ln1_g=v_ln1_g, ln1_b=v_ln1_b, w_ffn_gate=v_w_ffn_gate, w_ffn_up=v_w_ffn_up,
                w_ffn_down=v_w_ffn_down, ln2_g=v_ln2_g, ln2_b=v_ln2_b)
    names = list(weights)
    big_names = ("w_in", "w_out", "w_ffn_gate", "w_ffn_up", "w_ffn_down")

    grads, delta, new_m, new_v = {}, {}, {}, {}
    flips = [(lambda a: a.T) if kk in ("w_g", "w_u") else (lambda a: a) for kk in BIG[1:]]
    by_row = lambda a: jnp.transpose(a, (2, 0, 1))
    updated, updated_w_in = _adamw(
        [(flip(weights[k][0]), big_g[kk], flip(m_in[k][0]), flip(v_in[k][0]))
         for k, kk, flip in zip(big_names[1:], BIG[1:], flips)],
        (by_row(w_in), big_g["w_in"], by_row(m_w_in), by_row(v_w_in)), chip.astype(jnp.int32).reshape(1), r_in % BF16_ROWS)
    for k, flip, results in zip(big_names[1:], flips, updated):
        grads[k], delta[k], new_m[k], new_v[k] = (flip(t)[None] for t in results)
    grads["w_in"], delta["w_in"], new_m["w_in"], new_v["w_in"] = (jnp.transpose(t, (1, 2, 0)) for t in updated_w_in)
    small_in = [tuple(src[k].reshape(shape) for src in (weights, m_in, v_in)) for k, shape in SMALL]
    place = jnp.stack([2 * chip + lax.axis_index("c"), chip]).astype(jnp.int32)
    small_own, small_all = _small_wait(small_handle, [updated[0][0]])
    small_out, tail_row = _adamw_small(place, small_all, small_own, small_in)
    for (k, _), results in zip(SMALL, small_out):
        grads[k], delta[k], new_m[k], new_v[k] = (r.reshape(weights[k].shape) for r in results)

    return (tail_row[0, TAIL_LOSS], dx[None], *[grads[k] for k in names], *[delta[k] for k in names], *[new_m[k] for k in names],
            *[new_v[k] for k in names])
```

```python
import jax
import jax.numpy as jnp
from jax import lax
from jax.experimental import pallas as pl
from jax.experimental.pallas import tpu as pltpu

F32 = jnp.float32
BF16 = jnp.bfloat16
MESH = pl.DeviceIdType.MESH

D = 1024
SEQ = 4096
N_META = 16
SWA_HEADS, SWA_KV_HEADS, DH = 8, 2, 64
WINDOW = 128
GLA_HEADS, DK, DV = 4, 64, 128
GLA_TAU = 16.0
CH = 64
D_FF = 2816
D_IN = 2320
LN_EPS = 1e-5
RMS_EPS = 1e-6
ALPHA = 2.0 ** 0.25
NEG = -1e30
ADAM_LR, ADAM_B1, ADAM_B2, ADAM_EPS, ADAM_WD, ADAM_STEP = 0.001, 0.9, 0.999, 1e-8, 0.01, 10
O_QS, O_KS, O_VS, O_QG, O_KG, O_VG, O_RG, O_LR = 0, 512, 640, 768, 1024, 1280, 1792, 2304

LANE = 128
BLK = WINDOW
GATE_RANK = 16
D_IN_P = D_IN + LANE - GATE_RANK
META_OFF = CH - N_META
HEAD_POS = (0, 4, 1, 5, 2, 6, 3, 7)
LN_ROWS = 512
TOKEN = (8, LANE)
N_CHIPS = 4
SHARD_ROWS = dict(w_in=D_IN // N_CHIPS, w_out=D // N_CHIPS, w_g=D_FF // N_CHIPS, w_u=D_FF // N_CHIPS,
                  w_d=D_FF // N_CHIPS)
SMALL_ROWS = 48
BF16_ROWS = 16
W_IN_WIN = -(-SHARD_ROWS["w_in"] // (2 * BF16_ROWS)) * 2 * BF16_ROWS
W_IN_STARTS = tuple(s * SHARD_ROWS["w_in"] // BF16_ROWS * BF16_ROWS for s in range(N_CHIPS))
VMEM_CAP_MB = 64
VMEM_SPARE_MB = 6


def _lp():
    return SEQ + BLK


def _row_tile(cap):
    lp = _lp()
    return max(t for t in range(16, cap + 1, 16) if lp % t == 0)


def _params(vmem_mb, **kw):
    assert vmem_mb <= VMEM_CAP_MB - VMEM_SPARE_MB
    return pltpu.CompilerParams(vmem_limit_bytes=vmem_mb << 20, **kw)


def _seq(n=1):
    return ("arbitrary",) * n


def _const(shape):
    return pl.BlockSpec(shape, lambda *_: (0,) * len(shape), pipeline_mode=pl.Buffered(1))


def _acc(shape):
    return pl.BlockSpec(shape, lambda *_: (0,) * len(shape))


def _rows(tm, width):
    return pl.BlockSpec((tm, width), lambda i: (i, 0))


def _dot(a, b):
    return jnp.dot(a.astype(BF16), b.astype(BF16), preferred_element_type=F32)


def _dot_nt(a, b):
    return lax.dot_general(a.astype(BF16), b.astype(BF16), (((1,), (1,)), ((), ())), preferred_element_type=F32)


def _dot_tn(a, b):
    return lax.dot_general(a.astype(BF16), b.astype(BF16), (((0,), (0,)), ((), ())), preferred_element_type=F32)


def _dot_exact(a, b):
    return jnp.dot(a, b, precision=lax.Precision.HIGHEST, preferred_element_type=F32)


def _ln_stats(x):
    mu = jnp.mean(x, axis=-1, keepdims=True)
    xc = x - mu
    rstd = lax.rsqrt(jnp.mean(xc * xc, axis=-1, keepdims=True) + LN_EPS)
    return xc * rstd, rstd


def _ln_bwd(dy, xhat, rstd, g):
    dxh = dy * g
    return rstd * (dxh - jnp.mean(dxh, axis=-1, keepdims=True) - xhat * jnp.mean(dxh * xhat, axis=-1, keepdims=True))


def _sigmoid(x):
    return 1.0 / (1.0 + jnp.exp(-x))


def _iota(shape, dim):
    return lax.broadcasted_iota(jnp.int32, shape, dim)


def _hbm(*arrays):
    return tuple(pltpu.with_memory_space_constraint(a, pltpu.HBM) for a in arrays)


def _ln_in_fwd_real(x, g, b, token):
    tr = min(LN_ROWS, SEQ)

    def body(x_ref, g_ref, b_ref, token_ref, h_ref):
        xhat, _ = _ln_stats(x_ref[...])
        h_ref[...] = xhat * g_ref[...] + b_ref[...]

    return pl.pallas_call(
        body, name="ln_in_fwd", grid=(SEQ // tr,),
        in_specs=[_rows(tr, D), _const((1, D)), _const((1, D)), _const(TOKEN)],
        out_specs=_rows(tr, D),
        out_shape=pltpu.HBM((_lp(), D), F32),
        compiler_params=_params(32, dimension_semantics=_seq()),
    )(*_hbm(x, g, b), token)


def _ln_in_fwd_meta(h_real, meta_ext, g, b):
    def meta_body(m_ref, g_ref, b_ref, real_ref, h_ref):
        xhat, _ = _ln_stats(m_ref[...])
        h_ref[...] = xhat * g_ref[...] + b_ref[...]

    return pl.pallas_call(
        meta_body, name="ln_in_fwd_meta", grid=(1,),
        in_specs=[_const((BLK, D)), _const((1, D)), _const((1, D)), pl.BlockSpec(memory_space=pl.ANY)],
        out_specs=pl.BlockSpec((BLK, D), lambda i: (SEQ // BLK, 0)),
        out_shape=pltpu.HBM((_lp(), D), F32),
        input_output_aliases={3: 0},
        compiler_params=_params(16, dimension_semantics=_seq()),
    )(*_hbm(meta_ext, g, b, h_real))


def _in_proj(h0, w_in_windows, b_in_p, wg2_p, bg2):
    tm = _row_tile(384)
    lp = _lp()
    widths = (512, 128, 128, 256, 256, 512, 512, 128)
    offs = (O_QS, O_KS, O_VS, O_QG, O_KG, O_VG, O_RG, O_LR)
    shard = SHARD_ROWS["w_in"]

    def body(h_ref, win_ref, b_ref, wg2_ref, bg2_ref, *outs):
        w_ref = outs[9]

        @pl.when(pl.program_id(0) == 0)
        def _():
            for s in range(N_CHIPS):
                w_ref[shard * s:shard * (s + 1), :] = win_ref[s, 0:shard, :]
            w_ref[D_IN:D_IN_P, :] = jnp.zeros((D_IN_P - D_IN, D), BF16)

        proj = _dot_nt(h_ref[...], w_ref[...]) + b_ref[...]
        for pos, h in enumerate(HEAD_POS):
            outs[0][:, pos * DH:(pos + 1) * DH] = proj[:, O_QS + h * DH:O_QS + (h + 1) * DH]
        for o_ref, off, wd in zip(outs[1:8], offs[1:], widths[1:]):
            o_ref[...] = proj[:, off:off + wd]
        outs[8][...] = _dot(proj[:, O_LR:O_LR + LANE], wg2_ref[...]) + bg2_ref[...]

    return pl.pallas_call(
        body, name="in_proj", grid=(lp // tm,),
        in_specs=[_rows(tm, D), _const(w_in_windows.shape), _const((1, D_IN_P)), _const((LANE, 256)), _const((1, 256))],
        out_specs=[_rows(tm, w) for w in widths] + [_rows(tm, 256), _acc((D_IN_P, D))],
        out_shape=[pltpu.HBM((lp, w), F32) for w in widths] + [pltpu.HBM((lp, 256), F32), pltpu.HBM((D_IN_P, D), BF16)],
        compiler_params=_params(48, dimension_semantics=_seq()),
    )(*_hbm(h0, w_in_windows, b_in_p, wg2_p, bg2))


def _swa_masks(n):
    nb = SEQ // BLK
    is_meta = n == nb
    ri = _iota((BLK, BLK), 0)
    cj = _iota((BLK, BLK), 1)
    meta_col = ((cj >= META_OFF) & (cj < CH)).astype(jnp.int32)
    meta_q = meta_col * ((cj <= ri) & (ri < CH)).astype(jnp.int32)
    valid_m = jnp.where(is_meta, meta_q, meta_col) > 0
    dist_m = jnp.where(is_meta, ri - cj, n * BLK + ri + CH - cj).astype(F32)
    valid_p = jnp.where((n >= 1) & (n < nb), (cj > ri).astype(jnp.int32), 0) > 0
    dist_p = (ri + BLK - cj).astype(F32)
    valid_c = jnp.where(n < nb, (cj <= ri).astype(jnp.int32), 0) > 0
    dist_c = (ri - cj).astype(F32)
    return (dist_m, dist_p, dist_c), (valid_m, valid_p, valid_c)


def _swa_bias(n):
    dists, valids = _swa_masks(n)
    return (jnp.concatenate([-d for d in dists], axis=1),
            jnp.concatenate([jnp.where(v, 0.0, NEG) for v in valids], axis=1))


def _swa_half(ref, pos, scale=1.0):
    col = ref[:, (pos // 2) * LANE:(pos // 2 + 1) * LANE]
    lane = _iota((BLK, LANE), 1)
    mine = lane < DH if pos % 2 == 0 else lane >= DH
    return jnp.where(mine, col * scale, 0.0).astype(BF16)


def _swa_merge(even, odd):
    return jnp.where(_iota((BLK, LANE), 1) < DH, even, odd)


def _swa_softmax(t, sink):
    m = jnp.maximum(jnp.max(t, axis=-1, keepdims=True), sink)
    e = jnp.exp(t - m)
    e_sink = jnp.exp(sink - m)
    inv = 1.0 / (jnp.sum(e, axis=-1, keepdims=True) + e_sink)
    return e * inv, e_sink * inv


def _swa_kv_specs(width):
    nb = SEQ // BLK
    return [pl.BlockSpec((BLK, width), lambda n: (nb, 0)),
            pl.BlockSpec((BLK, width), lambda n: (jnp.clip(n - 1, 0, nb - 1), 0)),
            pl.BlockSpec((BLK, width), lambda n: (jnp.minimum(n, nb), 0))]


def _swa_fwd(sinks, qs, ks, vs):
    nb = SEQ // BLK
    heads = range(SWA_HEADS)

    def body(sink_ref, q_ref, km_ref, kp_ref, kc_ref, vm_ref, vp_ref, vc_ref, o_ref):
        negdist, maskbias = _swa_bias(pl.program_id(0))
        k_all = jnp.concatenate([km_ref[...], kp_ref[...], kc_ref[...]], axis=0).astype(BF16)
        v_all = jnp.concatenate([vm_ref[...], vp_ref[...], vc_ref[...]], axis=0).astype(BF16)
        q = [_swa_half(q_ref, pos, DH ** -0.5) for pos in heads]
        t = [_dot_nt(q[pos], k_all) + (2.0 ** -(HEAD_POS[pos] + 1) * negdist + maskbias) for pos in heads]
        p = [_swa_softmax(t[pos], sink_ref[HEAD_POS[pos]])[0].astype(BF16) for pos in heads]
        o = [_dot(p[pos], v_all) for pos in heads]
        for col in range(SWA_HEADS // 2):
            o_ref[:, col * LANE:(col + 1) * LANE] = _swa_merge(o[2 * col], o[2 * col + 1])

    kvw = SWA_KV_HEADS * DH
    return pl.pallas_call(
        body, name="swa_fwd", grid=(nb + 1,),
        in_specs=[pl.BlockSpec(memory_space=pltpu.SMEM), _rows(BLK, SWA_HEADS * DH)] + _swa_kv_specs(kvw) + _swa_kv_specs(kvw),
        out_specs=_rows(BLK, SWA_HEADS * DH),
        out_shape=pltpu.HBM((_lp(), SWA_HEADS * DH), F32),
        compiler_params=_params(16, dimension_semantics=_seq()),
    )(sinks, *_hbm(qs, ks, ks, ks, vs, vs, vs))


GLA_PER_STEP = BLK // CH


def _gla_block(s):
    nb = SEQ // BLK
    return jnp.where(s == 0, nb, s - 1)


def _gla_rowmask(s):
    ri = _iota((BLK, 1), 0)
    m = jnp.where(s == 0, ((ri >= META_OFF) & (ri < CH)).astype(jnp.int32), 1)
    return (m > 0).astype(F32) + jnp.zeros((BLK, 1), F32)


def _gla_chunk_masks():
    r, c = _iota((BLK, BLK), 0), _iota((BLK, BLK), 1)
    same = ((r < CH) & (c < CH)) | ((r >= CH) & (c >= CH))
    return same & (r >= c), same & (r <= c), same


def _gla_decay(z, rmask):
    log_g = (jnp.minimum(z, 0.0) - jnp.log1p(jnp.exp(-jnp.abs(z)))) * (rmask / GLA_TAU)
    lower, _, same = _gla_chunk_masks()
    return _dot_exact(lower.astype(F32), log_g), _dot_exact(same.astype(F32), log_g)


def _gla_slices(c, h):
    return slice(c * CH, (c + 1) * CH), slice(h * DK, (h + 1) * DK), slice(h * DV, (h + 1) * DV)


def _gla_fwd(qg, kg, vg, z):
    steps = SEQ // BLK + 1
    kw, vw = GLA_HEADS * DK, GLA_HEADS * DV
    pairs = [(c, h) for c in range(GLA_PER_STEP) for h in range(GLA_HEADS)]

    def body(q_ref, k_ref, v_ref, z_ref, o_ref, st_ref, st):
        s = pl.program_id(0)

        @pl.when(s == 0)
        def _():
            st[...] = jnp.zeros_like(st)

        rmask = _gla_rowmask(s)
        b, b_last = _gla_decay(z_ref[...], rmask)
        q = q_ref[...] * (rmask * DK ** -0.5)
        k = k_ref[...] * rmask
        v = v_ref[...] * rmask
        qe = q * jnp.exp(b)
        ke = k * jnp.exp(-b)
        kd = k * jnp.exp(b_last - b)
        e_last = jnp.exp(b_last)
        causal = _iota((CH, CH), 0) >= _iota((CH, CH), 1)
        a, upd, intra = {}, {}, {}
        for c, h in pairs:
            rows, ks, vs_ = _gla_slices(c, h)
            a[c, h] = jnp.where(causal, _dot_nt(qe[rows, ks], ke[rows, ks]), 0.0)
            upd[c, h] = _dot_tn(v[rows, vs_], kd[rows, ks])
        for c, h in pairs:
            rows, ks, vs_ = _gla_slices(c, h)
            intra[c, h] = _dot(a[c, h], v[rows, vs_])
        state = st[...]
        for c in range(GLA_PER_STEP):
            st_ref[0, c] = state
            for h in range(GLA_HEADS):
                rows, ks, vs_ = _gla_slices(c, h)
                o_ref[rows, vs_] = intra[c, h] + _dot_nt(qe[rows, ks], state[:, ks])
            state = state * e_last[c * CH:c * CH + 1] + jnp.concatenate([upd[c, h] for h in range(GLA_HEADS)], axis=1)
        st[...] = state

    blk = lambda w: pl.BlockSpec((BLK, w), lambda s: (_gla_block(s), 0))
    return pl.pallas_call(
        body, name="gla_fwd", grid=(steps,),
        in_specs=[blk(kw), blk(kw), blk(vw), blk(kw)],
        out_specs=[blk(vw), pl.BlockSpec((1, GLA_PER_STEP, DV, kw), lambda s: (s, 0, 0, 0))],
        out_shape=[pltpu.HBM((_lp(), vw), F32), pltpu.HBM((steps, GLA_PER_STEP, DV, kw), F32)],
        scratch_shapes=[pltpu.VMEM((DV, kw), F32)],
        compiler_params=_params(16, dimension_semantics=_seq()),
    )(*_hbm(qg, kg, vg, z))


def _post_mix(o_s, o_gla, r_g, h0, gn4, w_out, g1, b1, token):
    tm = _row_tile(384)
    lp = _lp()

    def body(os_ref, og_ref, r_ref, h0_ref, gn_ref, w_ref, g_ref, b_ref, token_ref, o_ref, pre_ref, h1_ref):
        for pos, h in enumerate(HEAD_POS):
            o_ref[:, h * DH:(h + 1) * DH] = os_ref[:, pos * DH:(pos + 1) * DH].astype(BF16)
        for h in range(GLA_HEADS):
            hs = slice(h * DV, (h + 1) * DV)
            xg = og_ref[:, hs]
            n = xg * lax.rsqrt(jnp.mean(xg * xg, axis=-1, keepdims=True) + RMS_EPS) * gn_ref[...]
            r = r_ref[:, hs]
            o_ref[:, 512 + h * DV:512 + (h + 1) * DV] = (n * (r * _sigmoid(r))).astype(BF16)
        pre = ALPHA * h0_ref[...] + _dot(o_ref[...], w_ref[...])
        pre_ref[...] = pre
        xhat, _ = _ln_stats(pre)
        h1_ref[...] = xhat * g_ref[...] + b_ref[...]

    return pl.pallas_call(
        body, name="post_mix", grid=(lp // tm,),
        in_specs=[_rows(tm, 512), _rows(tm, 512), _rows(tm, 512), _rows(tm, D), _const((1, DV)), _const((D, D)),
                  _const((1, D)), _const((1, D)), _const(TOKEN)],
        out_specs=[_rows(tm, D), _rows(tm, D), _rows(tm, D)],
        out_shape=[pltpu.HBM((lp, D), BF16), pltpu.HBM((lp, D), F32),
                   pltpu.HBM((lp, D), F32)],
        compiler_params=_params(32, dimension_semantics=_seq()),
    )(*_hbm(o_s, o_gla, r_g, h0, gn4, w_out, g1, b1), token)


def _ffn_fwd_loss_bwd(h1, wg_t, wu_t, wd, target, g2, b2):
    lp = _lp()
    tm = max(t for t in range(BLK, 384 + 1, BLK) if lp % t == 0)
    steps = lp // tm
    last_blk = SEQ // BLK - 1
    half = D_FF // 2
    n_t = tm // BLK

    def body(*refs):
        h_ref, wg_ref, wu_ref, wd_ref = refs[:4]
        t_refs = refs[4:4 + n_t]
        g2_ref, b2_ref, a_ref, dgate_ref, dup_ref, dp_ref, loss_ref, dg_ref, db_ref, g_s, u_s, acc = refs[4 + n_t:]
        i = pl.program_id(0)

        @pl.when(i == 0)
        def _():
            acc[...] = jnp.zeros_like(acc)
            dg_ref[...] = jnp.zeros_like(dg_ref)
            db_ref[...] = jnp.zeros_like(db_ref)

        h = h_ref[...]
        hb = h.astype(BF16)
        pre = ALPHA * h
        for j in range(2):
            cols = slice(j * half, (j + 1) * half)
            g = _dot_nt(hb, wg_ref[cols, :])
            u = _dot_nt(hb, wu_ref[cols, :])
            g_s[:, cols] = g
            u_s[:, cols] = u
            pre = pre + _dot(g * _sigmoid(g) * u, wd_ref[cols, :])
        xhat, rstd = _ln_stats(pre)
        real = i * tm + _iota((tm, 1), 0) < SEQ
        target_rows = jnp.concatenate([t[...] for t in t_refs], axis=0)
        diff = jnp.where(real, xhat * g2_ref[...] + b2_ref[...] - target_rows, 0.0)
        acc[...] += jnp.sum(diff * diff, axis=0, keepdims=True)
        dy = diff * (1.0 / D)
        dpre = _ln_bwd(dy, xhat, rstd, g2_ref[...])
        dp_ref[...] = dpre
        dg_ref[...] += jnp.sum(dy * xhat, axis=0, keepdims=True)
        db_ref[...] += jnp.sum(dy, axis=0, keepdims=True)
        dpb = dpre.astype(BF16)
        for j in range(2):
            cols = slice(j * half, (j + 1) * half)
            g, u = g_s[:, cols], u_s[:, cols]
            sg = _sigmoid(g)
            silu = g * sg
            da = _dot_nt(dpb, wd_ref[cols, :])
            a_ref[:, cols] = (silu * u).astype(BF16)
            dgate_ref[:, cols] = (da * u * (sg * (1.0 + g * (1.0 - sg)))).astype(BF16)
            dup_ref[:, cols] = (da * silu).astype(BF16)

        @pl.when(i == steps - 1)
        def _():
            loss_ref[...] = jnp.zeros_like(loss_ref) + (0.5 / D) * jnp.sum(acc[...], axis=1, keepdims=True)

    t_spec = lambda k: pl.BlockSpec((BLK, D), lambda i: (jnp.minimum(i * n_t + k, last_blk), 0))
    return pl.pallas_call(
        body, name="ffn_fwd_loss_bwd", grid=(steps,),
        in_specs=[_rows(tm, D), _const((D_FF, D)), _const((D_FF, D)), _const((D_FF, D))]
        + [t_spec(k) for k in range(n_t)] + [_const((1, D)), _const((1, D))],
        out_specs=[_rows(tm, D_FF), _rows(tm, D_FF), _rows(tm, D_FF), _rows(tm, D), _acc((1, LANE)), _acc((1, D)),
                   _acc((1, D))],
        out_shape=[pltpu.HBM((lp, D_FF), BF16)] * 3 + [pltpu.HBM((lp, D), F32), pltpu.HBM((1, LANE), F32),
                                                         pltpu.HBM((1, D), F32), pltpu.HBM((1, D), F32)],
        scratch_shapes=[pltpu.VMEM((tm, D_FF), F32), pltpu.VMEM((tm, D_FF), F32), pltpu.VMEM((1, D), F32)],
        compiler_params=_params(58, dimension_semantics=_seq()),
    )(*_hbm(h1, wg_t, wu_t, wd, *[target] * n_t, g2, b2))


def _ffn_out_bwd(dpre2, dgate, dup, pre1, wg_t, wu_t, g1, w_out, o_gla, r_g, gn4):
    tm = _row_tile(384)
    lp = _lp()

    def body(dp_ref, dg_ref, du_ref, p1_ref, wg_ref, wu_ref, g1_ref, w_ref, og_ref, r_ref, gn_ref,
             dp1_ref, dg1_ref, db1_ref, dos_ref, dog_ref, dr_ref, dgn_ref):
        @pl.when(pl.program_id(0) == 0)
        def _():
            for acc_ref in (dg1_ref, db1_ref, dgn_ref):
                acc_ref[...] = jnp.zeros_like(acc_ref)

        dh1 = ALPHA * dp_ref[...] + _dot(dg_ref[...], wg_ref[...]) + _dot(du_ref[...], wu_ref[...])
        xhat, rstd1 = _ln_stats(p1_ref[...])
        dpre1 = _ln_bwd(dh1, xhat, rstd1, g1_ref[...])
        dp1_ref[...] = dpre1
        dg1_ref[...] += jnp.sum(dh1 * xhat, axis=0, keepdims=True)
        db1_ref[...] += jnp.sum(dh1, axis=0, keepdims=True)

        do = _dot_nt(dpre1, w_ref[...])
        for pos, h in enumerate(HEAD_POS):
            dos_ref[:, pos * DH:(pos + 1) * DH] = do[:, h * DH:(h + 1) * DH]
        gn = gn_ref[...]
        for h in range(GLA_HEADS):
            hs = slice(h * DV, (h + 1) * DV)
            xg = og_ref[:, hs]
            rstd = lax.rsqrt(jnp.mean(xg * xg, axis=-1, keepdims=True) + RMS_EPS)
            nx = xg * rstd
            r = r_ref[:, hs]
            sr = _sigmoid(r)
            d_o = do[:, 512 + h * DV:512 + (h + 1) * DV]
            dr_ref[:, hs] = d_o * (nx * gn) * (sr * (1.0 + r * (1.0 - sr)))
            dn = d_o * (r * sr)
            dgn_ref[...] += jnp.sum(dn * nx, axis=0, keepdims=True)
            dnx = dn * gn
            dog_ref[:, hs] = rstd * (dnx - nx * jnp.mean(dnx * nx, axis=-1, keepdims=True))

    return pl.pallas_call(
        body, name="ffn_out_bwd", grid=(lp // tm,),
        in_specs=[_rows(tm, D), _rows(tm, D_FF), _rows(tm, D_FF), _rows(tm, D), _const((D_FF, D)), _const((D_FF, D)),
                  _const((1, D)), _const((D, D)), _rows(tm, 512), _rows(tm, 512), _const((1, DV))],
        out_specs=[_rows(tm, D), _acc((1, D)), _acc((1, D)), _rows(tm, 512), _rows(tm, 512), _rows(tm, 512),
                   _acc((1, DV))],
        out_shape=[pltpu.HBM((lp, D), F32), pltpu.HBM((1, D), F32), pltpu.HBM((1, D), F32)]
        + [pltpu.HBM((lp, 512), F32)] * 3 + [pltpu.HBM((1, DV), F32)],
        compiler_params=_params(48, dimension_semantics=_seq()),
    )(*_hbm(dpre2, dgate, dup, pre1, wg_t, wu_t, g1, w_out, o_gla, r_g, gn4))


def _atb(a, b, name, token=None, windows=None):
    lp = _lp()
    tm = _row_tile(1408)
    n, w = a.shape[1], b.shape[1]
    bw = 512 if n * w * 4 > (4 << 20) else w
    tokens = [] if token is None else [token]
    steps = lp // tm

    def body(a_ref, b_ref, *rest):
        o_ref, acc_ref = rest[len(tokens):] if windows else (rest[-1], rest[-1])

        @pl.when(pl.program_id(1) == 0)
        def _():
            acc_ref[...] = jnp.zeros_like(acc_ref)

        acc_ref[...] += _dot_tn(a_ref[...], b_ref[...])

        if windows:
            @pl.when(pl.program_id(1) == steps - 1)
            def _():
                for s, start in enumerate(windows[0]):
                    o_ref[s] = acc_ref[start:start + windows[1], :]

    if windows:
        count, height = len(windows[0]), windows[1]
        out_spec, out_shape = pl.BlockSpec((count, height, bw), lambda j, k: (0, 0, j)), (count, height, w)
    else:
        out_spec, out_shape = pl.BlockSpec((n, bw), lambda j, k: (0, j)), (n, w)
    return pl.pallas_call(
        body, name=name, grid=(w // bw, steps),
        in_specs=[pl.BlockSpec((tm, n), lambda j, k: (k, 0)), pl.BlockSpec((tm, bw), lambda j, k: (k, j))]
        + [_const(TOKEN)] * len(tokens),
        out_specs=out_spec, out_shape=pltpu.HBM(out_shape, F32),
        scratch_shapes=[pltpu.VMEM((n, bw), F32)] if windows else [],
        compiler_params=_params(48, dimension_semantics=_seq(2)),
    )(*_hbm(a, b), *tokens)


def _gla_bwd(qg, kg, vg, z, do_gla, st_all, token):
    steps = SEQ // BLK + 1
    kw, vw = GLA_HEADS * DK, GLA_HEADS * DV
    pairs = [(c, h) for c in range(GLA_PER_STEP) for h in range(GLA_HEADS)]
    heads = range(GLA_HEADS)

    def body(q_ref, k_ref, v_ref, z_ref, do_ref, st_ref, token_ref, dq_ref, dk_ref, dv_ref, dz_ref, dst):
        @pl.when(pl.program_id(0) == 0)
        def _():
            dst[...] = jnp.zeros_like(dst)

        rmask = _gla_rowmask(steps - 1 - pl.program_id(0))
        zz = z_ref[...]
        b, b_last = _gla_decay(zz, rmask)
        e_b, e_nb, e_kd, e_last = jnp.exp(b), jnp.exp(-b), jnp.exp(b_last - b), jnp.exp(b_last)
        q = q_ref[...] * (rmask * DK ** -0.5)
        k = k_ref[...] * rmask
        v = v_ref[...] * rmask
        qe, ke, kd = q * e_b, k * e_nb, k * e_kd
        d_o = do_ref[...]
        causal = _iota((CH, CH), 0) >= _iota((CH, CH), 1)
        a, da, dqe, dke, dv_intra, carry = {}, {}, {}, {}, {}, {}
        for c, h in pairs:
            rows, ks, vs_ = _gla_slices(c, h)
            a[c, h] = jnp.where(causal, _dot_nt(qe[rows, ks], ke[rows, ks]), 0.0)
            da[c, h] = jnp.where(causal, _dot_nt(d_o[rows, vs_], v[rows, vs_]), 0.0)
            carry[c, h] = _dot_tn(d_o[rows, vs_], qe[rows, ks])
        for c, h in pairs:
            rows, ks, vs_ = _gla_slices(c, h)
            dqe[c, h] = _dot(d_o[rows, vs_], st_ref[0, c][:, ks]) + _dot(da[c, h], ke[rows, ks])
            dke[c, h] = _dot_tn(da[c, h], qe[rows, ks])
            dv_intra[c, h] = _dot_tn(a[c, h], d_o[rows, vs_])
        dstate = dst[...]
        dkd, db_decay = {}, {}
        for c in reversed(range(GLA_PER_STEP)):
            for h in heads:
                rows, ks, vs_ = _gla_slices(c, h)
                dkd[c, h] = _dot(v[rows, vs_], dstate[:, ks])
                dv_ref[rows, vs_] = dv_intra[c, h] + _dot_nt(kd[rows, ks], dstate[:, ks])
            chunk_last = e_last[c * CH:c * CH + 1]
            db_decay[c] = jnp.sum(dstate * st_ref[0, c], axis=0, keepdims=True) * chunk_last
            dstate = dstate * chunk_last + jnp.concatenate([carry[c, h] for h in heads], axis=1)
        dst[...] = dstate
        rows_of = lambda parts: jnp.concatenate(
            [jnp.concatenate([parts[c, h] for h in heads], axis=1) for c in range(GLA_PER_STEP)], axis=0)
        dqe_all, dke_all, dkd_all = rows_of(dqe), rows_of(dke), rows_of(dkd)
        dq_ref[...] = dqe_all * e_b * (rmask * DK ** -0.5)
        dk_ref[...] = (dke_all * e_nb + dkd_all * e_kd) * rmask
        dkd_kd = dkd_all * kd
        db = dqe_all * qe - dke_all * ke - dkd_kd
        _, upper, same = _gla_chunk_masks()
        decay_rows = jnp.concatenate([jnp.broadcast_to(db_decay[c], (CH, kw)) for c in range(GLA_PER_STEP)], axis=0)
        dlog_g = _dot_exact(upper.astype(F32), db) + _dot_exact(same.astype(F32), dkd_kd) + decay_rows
        dz_ref[...] = dlog_g * (rmask / GLA_TAU) * _sigmoid(-zz)

    blk = lambda w: pl.BlockSpec((BLK, w), lambda s: (_gla_block(steps - 1 - s), 0))
    return pl.pallas_call(
        body, name="gla_bwd", grid=(steps,),
        in_specs=[blk(kw), blk(kw), blk(vw), blk(kw), blk(vw),
                  pl.BlockSpec((1, GLA_PER_STEP, DV, kw), lambda s: (steps - 1 - s, 0, 0, 0)), _const(TOKEN)],
        out_specs=[blk(kw), blk(kw), blk(vw), blk(kw)],
        out_shape=[pltpu.HBM((_lp(), kw), F32), pltpu.HBM((_lp(), kw), F32),
                   pltpu.HBM((_lp(), vw), F32), pltpu.HBM((_lp(), kw), F32)],
        scratch_shapes=[pltpu.VMEM((DV, kw), F32)],
        compiler_params=_params(16, dimension_semantics=_seq()),
    )(*_hbm(qg, kg, vg, z, do_gla, st_all), token)


def _swa_bwd(sinks, qs, ks, vs, do_s, token):
    nb = SEQ // BLK
    kvw = SWA_KV_HEADS * DH
    scale = DH ** -0.5
    heads = range(SWA_HEADS)

    def body(sink_ref, q_ref, km_ref, kp_ref, kc_ref, vm_ref, vp_ref, vc_ref, do_ref, token_ref,
             dq_ref, dk_ref, dv_ref, dsink_ref, carry_k, carry_v, meta_k, meta_v):
        n = pl.program_id(0)

        @pl.when(n == 0)
        def _():
            for r in (carry_k, carry_v, meta_k, meta_v):
                r[...] = jnp.zeros_like(r)
            dsink_ref[...] = jnp.zeros_like(dsink_ref)

        @pl.when(n <= nb)
        def _():
            negdist, maskbias = _swa_bias(n)
            lane = _iota((1, LANE), 1)
            k_all = jnp.concatenate([km_ref[...], kp_ref[...], kc_ref[...]], axis=0).astype(BF16)
            v_all = jnp.concatenate([vm_ref[...], vp_ref[...], vc_ref[...]], axis=0).astype(BF16)
            q = [_swa_half(q_ref, pos, scale) for pos in heads]
            d_o = [_swa_half(do_ref, pos) for pos in heads]
            t = [_dot_nt(q[pos], k_all) + (2.0 ** -(HEAD_POS[pos] + 1) * negdist + maskbias) for pos in heads]
            dp = [_dot_nt(d_o[pos], v_all) for pos in heads]
            soft = [_swa_softmax(t[pos], sink_ref[HEAD_POS[pos]]) for pos in heads]
            p = [s[0] for s in soft]
            delta = [jnp.sum(p[pos] * dp[pos], axis=-1, keepdims=True) for pos in heads]
            ds = [(p[pos] * (dp[pos] - delta[pos])).astype(BF16) for pos in heads]
            dq = [_dot(ds[pos], k_all) for pos in heads]
            for col in range(SWA_HEADS // 2):
                dq_ref[:, col * LANE:(col + 1) * LANE] = scale * _swa_merge(dq[2 * col], dq[2 * col + 1])
            dsink = jnp.zeros((1, LANE), F32)
            for pos in heads:
                dsink = dsink + jnp.where(lane == HEAD_POS[pos],
                                          -jnp.sum(soft[pos][1] * delta[pos], axis=0, keepdims=True), 0.0)
            dsink_ref[...] += dsink
            dk3 = _dot_tn(jnp.concatenate(q, axis=0), jnp.concatenate(ds, axis=0)).T
            dv3 = _dot_tn(jnp.concatenate(d_o, axis=0), jnp.concatenate([x.astype(BF16) for x in p], axis=0)).T
            meta_k[...] += dk3[0:BLK]
            meta_v[...] += dv3[0:BLK]
            dk_ref[...] = carry_k[...] + dk3[BLK:2 * BLK]
            dv_ref[...] = carry_v[...] + dv3[BLK:2 * BLK]
            carry_k[...] = dk3[2 * BLK:3 * BLK]
            carry_v[...] = dv3[2 * BLK:3 * BLK]

        @pl.when(n == nb + 1)
        def _():
            dk_ref[...] = meta_k[...]
            dv_ref[...] = meta_v[...]

    kv_out = pl.BlockSpec((BLK, kvw), lambda n: (jnp.where(n == nb + 1, nb, jnp.clip(n - 1, 0, nb - 1)), 0))
    qblk = pl.BlockSpec((BLK, SWA_HEADS * DH), lambda n: (jnp.minimum(n, nb), 0))
    return pl.pallas_call(
        body, name="swa_bwd", grid=(nb + 2,),
        in_specs=[pl.BlockSpec(memory_space=pltpu.SMEM), qblk] + _swa_kv_specs(kvw) + _swa_kv_specs(kvw)
        + [qblk, _const(TOKEN)],
        out_specs=[qblk, kv_out, kv_out, _acc((1, LANE))],
        out_shape=[pltpu.HBM((_lp(), SWA_HEADS * DH), F32), pltpu.HBM((_lp(), kvw), F32),
                   pltpu.HBM((_lp(), kvw), F32), pltpu.HBM((1, LANE), F32)],
        scratch_shapes=[pltpu.VMEM((BLK, kvw), F32)] * 4,
        compiler_params=_params(16, dimension_semantics=_seq()),
    )(sinks, *_hbm(qs, ks, ks, ks, vs, vs, vs, do_s), token)


def _in_bwd(dqs, dks, dvs, dqg, dkg, dvg, drg, dz, dpre1, w_in_t, wg2_p):
    tm = _row_tile(384)
    lp = _lp()
    widths = (512, 128, 128, 256, 256, 512, 512)
    offs = (O_QS, O_KS, O_VS, O_QG, O_KG, O_VG, O_RG)

    def body(*refs):
        parts, (dz_ref, dp1_ref, w_ref, wg2_ref, dproj_ref, dh0_ref, dbin_ref, dbg_ref) = refs[:7], refs[7:]

        @pl.when(pl.program_id(0) == 0)
        def _():
            dbin_ref[...] = jnp.zeros_like(dbin_ref)
            dbg_ref[...] = jnp.zeros_like(dbg_ref)

        for pos, h in enumerate(HEAD_POS):
            val = parts[0][:, pos * DH:(pos + 1) * DH]
            dproj_ref[:, O_QS + h * DH:O_QS + (h + 1) * DH] = val.astype(BF16)
            dbin_ref[:, O_QS + h * DH:O_QS + (h + 1) * DH] += jnp.sum(val, axis=0, keepdims=True)
        for p_ref, off, wd in zip(parts[1:], offs[1:], widths[1:]):
            val = p_ref[...]
            dproj_ref[:, off:off + wd] = val.astype(BF16)
            dbin_ref[:, off:off + wd] += jnp.sum(val, axis=0, keepdims=True)
        dz = dz_ref[...]
        dlr = _dot_nt(dz, wg2_ref[...])
        dproj_ref[:, O_LR:O_LR + LANE] = dlr.astype(BF16)
        dbin_ref[:, O_LR:O_LR + LANE] += jnp.sum(dlr, axis=0, keepdims=True)
        dbg_ref[...] += jnp.sum(dz, axis=0, keepdims=True)
        dh0_ref[...] = ALPHA * dp1_ref[...] + _dot(dproj_ref[...], w_ref[...])

    return pl.pallas_call(
        body, name="in_bwd", grid=(lp // tm,),
        in_specs=[_rows(tm, w) for w in widths] + [_rows(tm, 256), _rows(tm, D), _const((D_IN_P, D)), _const((LANE, 256))],
        out_specs=[_rows(tm, D_IN_P), _rows(tm, D), _acc((1, D_IN_P)), _acc((1, 256))],
        out_shape=[pltpu.HBM((lp, D_IN_P), BF16), pltpu.HBM((lp, D), F32),
                   pltpu.HBM((1, D_IN_P), F32), pltpu.HBM((1, 256), F32)],
        compiler_params=_params(40, dimension_semantics=_seq()),
    )(*_hbm(dqs, dks, dvs, dqg, dkg, dvg, drg, dz, dpre1, w_in_t, wg2_p))


def _ln_in_bwd(x, meta_ext, dh0, g, token):
    tr = min(LN_ROWS, SEQ)

    def ln_bwd(x_ref, dh_ref, g_ref, dx_ref, dg_ref, db_ref):
        @pl.when(pl.program_id(0) == 0)
        def _():
            dg_ref[...] = jnp.zeros_like(dg_ref)
            db_ref[...] = jnp.zeros_like(db_ref)

        xhat, rstd = _ln_stats(x_ref[...])
        dh = dh_ref[...]
        dx_ref[...] = _ln_bwd(dh, xhat, rstd, g_ref[...])
        dg_ref[...] += jnp.sum(dh * xhat, axis=0, keepdims=True)
        db_ref[...] += jnp.sum(dh, axis=0, keepdims=True)

    def body(x_ref, dh_ref, g_ref, token_ref, dx_ref, dg_ref, db_ref):
        ln_bwd(x_ref, dh_ref, g_ref, dx_ref, dg_ref, db_ref)

    def meta_body(m_ref, dh_ref, g_ref, dm_ref, dg_ref, db_ref):
        ln_bwd(m_ref, dh_ref, g_ref, dm_ref, dg_ref, db_ref)

    sums = [pltpu.HBM((1, D), F32), pltpu.HBM((1, D), F32)]
    dx, dg, db = pl.pallas_call(
        body, name="ln_in_bwd", grid=(SEQ // tr,),
        in_specs=[_rows(tr, D), _rows(tr, D), _const((1, D)), _const(TOKEN)],
        out_specs=[_rows(tr, D), _acc((1, D)), _acc((1, D))],
        out_shape=[pltpu.HBM((SEQ, D), F32)] + sums,
        compiler_params=_params(32, dimension_semantics=_seq()),
    )(*_hbm(x, dh0, g), token)
    dm, dg_m, db_m = pl.pallas_call(
        meta_body, name="ln_in_bwd_meta", grid=(1,),
        in_specs=[_const((BLK, D)), pl.BlockSpec((BLK, D), lambda i: (SEQ // BLK, 0)), _const((1, D))],
        out_specs=[_acc((BLK, D)), _acc((1, D)), _acc((1, D))],
        out_shape=[pltpu.HBM((BLK, D), F32)] + sums,
        compiler_params=_params(16, dimension_semantics=_seq()),
    )(*_hbm(meta_ext, dh0, g))
    return dx, dm, dg + dg_m, db + db_m


def _local_step(x, target, ln_in_g, ln_in_b, b_in, bg2, sinks, gn, g1, b1, g2, b2,
                token, fetch_first, fetch_rest, fetch_ffn, exchange_ffn, ship_ffn, ship_w_in):
    row = lambda v: v.reshape(1, -1).astype(F32)
    b_in_p = jnp.pad(row(b_in), ((0, 0), (0, D_IN_P - D_IN)))
    gn4 = row(gn)
    sinks = sinks.reshape(-1).astype(F32)

    h_real = _ln_in_fwd_real(x, row(ln_in_g), row(ln_in_b), token)
    w_in_windows, meta_full, wg2 = fetch_first([h_real])
    meta_ext = jnp.pad(meta_full, ((META_OFF, BLK - CH), (0, 0)))
    wg2_p = jnp.pad(wg2, ((0, LANE - wg2.shape[0]), (0, 0))).astype(BF16)
    h0 = _ln_in_fwd_meta(h_real, meta_ext, row(ln_in_g), row(ln_in_b))
    qs, ks, vs, qg, kg, vg, rg, glr, z, w_in_t = _in_proj(h0, w_in_windows, b_in_p, wg2_p, row(bg2))
    o_s = _swa_fwd(sinks, qs, ks, vs)
    o_gla, st_all = _gla_fwd(qg, kg, vg, z)
    w_out, token = fetch_rest([o_s, o_gla])
    o, pre1, h1 = _post_mix(o_s, o_gla, rg, h0, gn4, w_out, row(g1), row(b1), token)
    wg_t, wu_t, wd = fetch_ffn([pre1])
    a, dgate, dup, dpre2, loss, dg2, db2 = _ffn_fwd_loss_bwd(h1, wg_t, wu_t, wd, target, row(g2), row(b2))
    dpre1, dg1, db1, do_s, do_gla, drg, dgn = _ffn_out_bwd(dpre2, dgate, dup, pre1, wg_t, wu_t, row(g1), w_out, o_gla,
                                                           rg, gn4)
    dwd = _atb(a, dpre2, "dw_down")
    dwg_t = _atb(dgate, h1, "dw_gate")
    dwu_t = _atb(dup, h1, "dw_up")
    token = exchange_ffn(dict(w_out=_atb(o, dpre1, "dw_out"), w_g=dwg_t, w_u=dwu_t, w_d=dwd))
    dqg, dkg, dvg, dz = _gla_bwd(qg, kg, vg, z, do_gla, st_all, token)
    token = ship_ffn([dqg])
    dqs, dks, dvs, dsinks = _swa_bwd(sinks, qs, ks, vs, do_s, token)
    dproj, dh0, db_in_p, dbg2 = _in_bwd(dqs, dks, dvs, dqg, dkg, dvg, drg, dz, dpre1, w_in_t, wg2_p)
    token = ship_w_in(_atb(dproj, h0, "dw_in", windows=(W_IN_STARTS, W_IN_WIN)))
    dwg2_p = _atb(glr, dz, "dw_gate_lr2")
    dx, dmeta_blk, dg_in, db_in_ln = _ln_in_bwd(x, meta_ext, dh0, row(ln_in_g), token)

    small = dict(meta_blk=dmeta_blk, ln_in_g=dg_in, ln_in_b=db_in_ln, ln1_g=dg1, ln1_b=db1, ln2_g=dg2, ln2_b=db2,
                 b_in_p=db_in_p, wg2_p=dwg2_p, bg2=dbg2, sinks=dsinks, gn=dgn, loss=loss)
    return dx, small


HBM = pl.BlockSpec(memory_space=pltpu.HBM)


def _place():
    return lax.axis_index("x"), lax.axis_index("y"), lax.axis_index("c")


def _other_chips(x, y):
    return [(1 - x, y), (x, 1 - y), (1 - x, 1 - y)]


def _dma_sems(n):
    return pltpu.SemaphoreType.DMA((n,))


def _comm_params():
    return pltpu.CompilerParams(has_side_effects=True)


SEM = pl.BlockSpec(memory_space=pltpu.SEMAPHORE)


PER_ARRAY = dict(gather=3, scatter=3, sibling=N_CHIPS)


def _ici_copies(kind, landing, srcs, lands, send_sems, recv_sems):
    x, y, c = _place()
    mine = 2 * x + y
    copies = []
    for a in range(len(srcs)):
        if kind == "sibling":
            for s in range(N_CHIPS):
                copies.append(pltpu.make_async_remote_copy(
                    srcs[a].at[s, 1 - c], lands[a].at[s], send_sems.at[N_CHIPS * a + s], recv_sems.at[N_CHIPS * a + s],
                    device_id=(x, y, 1 - c), device_id_type=MESH))
            continue
        for j, (px, py) in enumerate(_other_chips(x, y)):
            slab = 2 * px + py if landing else mine
            if kind == "gather":
                src, dst = srcs[a].at[c], lands[a].at[slab, c]
            else:
                src, dst = srcs[a].at[2 * px + py], lands[a].at[slab]
            copies.append(pltpu.make_async_remote_copy(src, dst, send_sems.at[3 * a + j], recv_sems.at[3 * a + j],
                                                       device_id=(px, py, c), device_id_type=MESH))
    return copies


def _split_params():
    return pltpu.CompilerParams(has_side_effects=pltpu.SideEffectType.DATAFLOW_SIDE_EFFECTING)


def _ici_start(kind, srcs, land_shapes, after, name):
    n = len(srcs)
    lands = [pltpu.with_memory_space_constraint(lax.empty(s, a.dtype), pltpu.HBM) for s, a in zip(land_shapes, srcs)]

    def body(*refs):
        outs = refs[2 * n + len(after):]
        for cp in _ici_copies(kind, False, refs[:n], refs[n:2 * n], outs[0], outs[1]):
            cp.start()
        outs[-1][...] = jnp.zeros(TOKEN, F32)

    outs = pl.pallas_call(
        body, name=name, in_specs=[HBM] * (2 * n) + [pl.BlockSpec(memory_space=pl.ANY)] * len(after),
        out_specs=[SEM, SEM] + [HBM] * (2 * n) + [pl.BlockSpec(memory_space=pltpu.VMEM)],
        out_shape=[_dma_sems(PER_ARRAY[kind] * n)] * 2 + [pltpu.HBM(a.shape, a.dtype) for a in list(srcs) + lands]
        + [jax.ShapeDtypeStruct(TOKEN, F32)],
        input_output_aliases={i: 2 + i for i in range(2 * n)},
        compiler_params=_split_params(),
    )(*_hbm(*srcs), *lands, *after)
    return outs[:-1], outs[-1]


def _ici_wait(kind, handle, after, name):
    n = (len(handle) - 2) // 2

    def body(*refs):
        for cp in _ici_copies(kind, True, refs[:n], refs[n:2 * n], refs[2 * n], refs[2 * n + 1]):
            cp.wait_send()
            cp.wait_recv()

    outs = pl.pallas_call(
        body, name=name, in_specs=[HBM] * (2 * n) + [SEM, SEM] + [pl.BlockSpec(memory_space=pl.ANY)] * len(after),
        out_specs=[HBM] * (2 * n), out_shape=[pltpu.HBM(a.shape, a.dtype) for a in handle[2:]],
        input_output_aliases={i: i for i in range(2 * n)},
        compiler_params=_split_params(),
    )(*handle[2:], handle[0], handle[1], *after)
    return list(outs[:n]), list(outs[n:])


def _forward_copies(landing, arrs, send_sems, recv_sems):
    x, y, c = _place()
    copies = []
    for a in range(len(arrs)):
        for j, (px, py) in enumerate(_other_chips(x, y)):
            half = 1 - c if landing else c
            copies.append(pltpu.make_async_remote_copy(
                arrs[a].at[2 * px + py, c], arrs[a].at[2 * px + py, half], send_sems.at[3 * a + j],
                recv_sems.at[3 * a + j], device_id=(x, y, 1 - c), device_id_type=MESH))
    return copies


def _sibling_forward(lands, name):
    n = len(lands)

    def body(*refs):
        outs = refs[n:2 * n]
        send_sems, recv_sems = refs[2 * n:]
        sends = _forward_copies(False, outs, send_sems, recv_sems)
        for cp in sends:
            cp.start()
        for cp in _forward_copies(True, outs, send_sems, recv_sems):
            cp.wait_recv()
        for cp in sends:
            cp.wait_send()

    return pl.pallas_call(
        body, name=name, in_specs=[HBM] * n, out_specs=[HBM] * n,
        out_shape=[pltpu.HBM(a.shape, a.dtype) for a in lands],
        input_output_aliases={a: a for a in range(n)},
        scratch_shapes=[_dma_sems(3 * n)] * 2,
        compiler_params=_comm_params(),
    )(*_hbm(*lands))


def _forward_start(lands, name):
    n = len(lands)

    def body(*refs):
        outs = refs[n:]
        for cp in _forward_copies(False, refs[:n], outs[0], outs[1]):
            cp.start()
        outs[-1][...] = jnp.zeros(TOKEN, F32)

    outs = pl.pallas_call(
        body, name=name, in_specs=[HBM] * n,
        out_specs=[SEM, SEM] + [HBM] * n + [pl.BlockSpec(memory_space=pltpu.VMEM)],
        out_shape=[_dma_sems(3 * n)] * 2 + [pltpu.HBM(a.shape, a.dtype) for a in lands]
        + [jax.ShapeDtypeStruct(TOKEN, F32)],
        input_output_aliases={i: 2 + i for i in range(n)},
        compiler_params=_split_params(),
    )(*_hbm(*lands))
    return outs[:-1], outs[-1]


def _forward_wait(handle, after, name):
    n = len(handle) - 2

    def body(*refs):
        for cp in _forward_copies(True, refs[:n], refs[n], refs[n + 1]):
            cp.wait_send()
            cp.wait_recv()

    return list(pl.pallas_call(
        body, name=name, in_specs=[HBM] * n + [SEM, SEM] + [pl.BlockSpec(memory_space=pl.ANY)] * len(after),
        out_specs=[HBM] * n, out_shape=[pltpu.HBM(a.shape, a.dtype) for a in handle[2:]],
        input_output_aliases={i: i for i in range(n)},
        compiler_params=_split_params(),
    )(*handle[2:], handle[0], handle[1], *after))


def _sibling_exchange(grads, name):
    n = len(grads)

    def body(*refs):
        ins, outs = refs[:n], refs[n:2 * n]
        send_sems, recv_sems = refs[2 * n:]
        x, y, c = _place()
        copies = []
        for a in range(n):
            for s in range(N_CHIPS):
                cp = pltpu.make_async_remote_copy(ins[a].at[s, 1 - c], outs[a].at[s], send_sems.at[N_CHIPS * a + s],
                                                  recv_sems.at[N_CHIPS * a + s], device_id=(x, y, 1 - c),
                                                  device_id_type=MESH)
                cp.start()
                copies.append(cp)
        for cp in copies:
            cp.wait_recv()
        for cp in copies:
            cp.wait_send()

    return pl.pallas_call(
        body, name=name, in_specs=[HBM] * n, out_specs=[HBM] * n,
        out_shape=[pltpu.HBM((N_CHIPS, g.shape[2], D), F32) for g in grads],
        scratch_shapes=[_dma_sems(N_CHIPS * n)] * 2,
        compiler_params=_comm_params(),
    )(*_hbm(*grads))


def _add_halves(core, grads, recvs, dtypes, name):
    n = len(grads)
    heights = [g.shape[2] for g in grads]

    def body(c_ref, *refs):
        for a in range(n):
            refs[2 * n + a][...] = (refs[2 * a][0] + refs[2 * a + 1][...]).astype(dtypes[a])

    slab = lambda h: pl.BlockSpec((1, h, D), lambda s, c: (s, 0, 0))
    mine = lambda h: pl.BlockSpec((1, 1, h, D), lambda s, c: (s, c[0], 0, 0))
    return pl.pallas_call(
        body, name=name,
        grid_spec=pltpu.PrefetchScalarGridSpec(
            num_scalar_prefetch=1, grid=(N_CHIPS,),
            in_specs=[spec(h) for h in heights for spec in (mine, slab)], out_specs=[slab(h) for h in heights]),
        out_shape=[pltpu.HBM((N_CHIPS, h, D), dt) for h, dt in zip(heights, dtypes)],
        compiler_params=_params(32, dimension_semantics=_seq()),
    )(core, *_hbm(*[a for pair in zip(grads, recvs) for a in pair]))


N_DEVICES = 2 * N_CHIPS
PEER_FLIPS = [(dx, dy, dc) for dx in (0, 1) for dy in (0, 1) for dc in (0, 1)][1:]


def _small_copies(landing, p_ref, out_ref, send_sems, recv_sems):
    x, y, c = _place()
    flip = lambda v, d: 1 - v if d else v
    copies = []
    for k, flips in enumerate(PEER_FLIPS):
        px, py, pc = (flip(v, d) for v, d in zip((x, y, c), flips))
        slab = 4 * px + 2 * py + pc if landing else 4 * x + 2 * y + c
        copies.append(pltpu.make_async_remote_copy(p_ref, out_ref.at[slab], send_sems.at[k], recv_sems.at[k],
                                                   device_id=(px, py, pc), device_id_type=MESH))
    return copies


def _small_start(pack, after):
    n = len(PEER_FLIPS)
    land = pltpu.with_memory_space_constraint(lax.empty((N_DEVICES,) + pack.shape, F32), pltpu.HBM)

    def body(p_ref, land_ref, *refs):
        outs = refs[len(after):]
        for cp in _small_copies(False, p_ref, land_ref, outs[0], outs[1]):
            cp.start()
        outs[-1][...] = jnp.zeros(TOKEN, F32)

    outs = pl.pallas_call(
        body, name="small_exchange_start", in_specs=[HBM, HBM] + [pl.BlockSpec(memory_space=pl.ANY)] * len(after),
        out_specs=[SEM, SEM, HBM, HBM, pl.BlockSpec(memory_space=pltpu.VMEM)],
        out_shape=[_dma_sems(n), _dma_sems(n), pltpu.HBM(pack.shape, F32), pltpu.HBM(land.shape, F32),
                   jax.ShapeDtypeStruct(TOKEN, F32)],
        input_output_aliases={0: 2, 1: 3},
        compiler_params=_split_params(),
    )(*_hbm(pack), land, *after)
    return outs[:-1], outs[-1]


def _small_wait(handle, after):
    def body(p_ref, land_ref, send_sems, recv_sems, *rest):
        for cp in _small_copies(True, p_ref, land_ref, send_sems, recv_sems):
            cp.wait_send()
            cp.wait_recv()

    return pl.pallas_call(
        body, name="small_exchange_wait", in_specs=[HBM, HBM, SEM, SEM] + [pl.BlockSpec(memory_space=pl.ANY)] * len(after),
        out_specs=[HBM, HBM], out_shape=[pltpu.HBM(a.shape, F32) for a in handle[2:]],
        input_output_aliases={0: 0, 1: 1},
        compiler_params=_split_params(),
    )(handle[2], handle[3], handle[0], handle[1], *after)


def _sum_chips(slots, firsts, rests, after, name):
    n = len(firsts)

    def body(i_ref, *refs):
        outs = refs[4 * n + len(after):]
        for a in range(n):
            first, r1, r2, r3 = refs[4 * a:4 * a + 4]
            outs[a][...] = ((first[...].astype(F32) + r1[...].astype(F32)) + r2[...].astype(F32)) + r3[...].astype(F32)

    slab = lambda h, k: pl.BlockSpec((1, h, D), lambda i, ix: (ix[k], 0, 0))
    heights = [f.shape[1] for f in firsts]
    return pl.pallas_call(
        body, name=name,
        grid_spec=pltpu.PrefetchScalarGridSpec(
            num_scalar_prefetch=1, grid=(1,),
            in_specs=[slab(h, k) for h in heights for k in range(4)] + [pl.BlockSpec(memory_space=pl.ANY)] * len(after),
            out_specs=[slab(h, 4) for h in heights]),
        out_shape=[pltpu.HBM((2, h, D), F32) for h in heights],
        compiler_params=_params(48, dimension_semantics=_seq()),
    )(slots, *_hbm(*[a for f, r in zip(firsts, rests) for a in (f, r, r, r)]), *after)


def _join_copies(landing, arrs, send_sems, recv_sems):
    x, y, c = _place()
    slab = 1 - c if landing else c
    return [pltpu.make_async_remote_copy(arr.at[slab], arr.at[slab], send_sems.at[a], recv_sems.at[a],
                                         device_id=(x, y, 1 - c), device_id_type=MESH) for a, arr in enumerate(arrs)]


def _join_halves(halves, name):
    n = len(halves)

    def body(*refs):
        outs = refs[n:2 * n]
        send_sems, recv_sems = refs[2 * n:]
        sends = _join_copies(False, outs, send_sems, recv_sems)
        for cp in sends:
            cp.start()
        for cp in _join_copies(True, outs, send_sems, recv_sems):
            cp.wait_recv()
        for cp in sends:
            cp.wait_send()

    return list(pl.pallas_call(
        body, name=name, in_specs=[HBM] * n, out_specs=[HBM] * n,
        out_shape=[pltpu.HBM(h.shape, F32) for h in halves],
        input_output_aliases={a: a for a in range(n)},
        scratch_shapes=[_dma_sems(n)] * 2,
        compiler_params=_comm_params(),
    )(*_hbm(*halves)))


def _join_start(halves, name):
    n = len(halves)

    def body(*refs):
        outs = refs[n:]
        for cp in _join_copies(False, refs[:n], outs[0], outs[1]):
            cp.start()
        outs[-1][...] = jnp.zeros(TOKEN, F32)

    outs = pl.pallas_call(
        body, name=name, in_specs=[HBM] * n,
        out_specs=[SEM, SEM] + [HBM] * n + [pl.BlockSpec(memory_space=pltpu.VMEM)],
        out_shape=[_dma_sems(n)] * 2 + [pltpu.HBM(a.shape, a.dtype) for a in halves] + [jax.ShapeDtypeStruct(TOKEN, F32)],
        input_output_aliases={i: 2 + i for i in range(n)},
        compiler_params=_split_params(),
    )(*_hbm(*halves))
    return outs[:-1], outs[-1]


def _join_wait(handle, after, name):
    n = len(handle) - 2

    def body(*refs):
        for cp in _join_copies(True, refs[:n], refs[n], refs[n + 1]):
            cp.wait_send()
            cp.wait_recv()

    return list(pl.pallas_call(
        body, name=name, in_specs=[HBM] * n + [SEM, SEM] + [pl.BlockSpec(memory_space=pl.ANY)] * len(after),
        out_specs=[HBM] * n, out_shape=[pltpu.HBM(a.shape, a.dtype) for a in handle[2:]],
        input_output_aliases={i: i for i in range(n)},
        compiler_params=_split_params(),
    )(*handle[2:], handle[0], handle[1], *after))


def _chip_partials(grads, wire_dtypes, names, fetched=()):
    core = lax.axis_index("c").astype(jnp.int32).reshape(1)
    todo = len(grads) - len(fetched)
    recv = (list(_sibling_exchange(grads[:todo], "sibling_exchange_" + names[0])) if todo else []) + list(fetched)
    return list(_add_halves(core, grads, recv, wire_dtypes, "add_halves_" + names[0]))


def _chip_sums(parts, got, after, name):
    x, y, c = _place()
    others = [2 * px + py for px, py in _other_chips(x, y)]
    own_first = jnp.stack([2 * x + y] + others + [c]).astype(jnp.int32)
    return list(_sum_chips(own_first, parts, got, after, name))


ADAMW_STEPS = 8


def _adamw(params, by_row, chip, window_step):
    n = len(params)
    rows, _, cols = by_row[0].shape
    block = lambda shape: pl.BlockSpec((shape[0] // ADAMW_STEPS, shape[1]), lambda i, c: (i, 0))
    assert all(a.shape[0] % (8 * ADAMW_STEPS) == 0 for p in params for a in p)

    def body(c_ref, *refs):
        w_hbm, g_ref, m_hbm, v_hbm = refs[4 * n:4 * n + 4]
        results, (ins_ref, outs_ref, sems) = refs[8 * n + 4:8 * n + 8], refs[8 * n + 8:]
        loads = [pltpu.make_async_copy(src.at[:, 0, :], ins_ref.at[k], sems.at[k])
                 for k, src in enumerate((w_hbm, m_hbm, v_hbm))]
        stores = [pltpu.make_async_copy(outs_ref.at[k], dst.at[:, 0, :], sems.at[3 + k]) for k, dst in enumerate(results)]
        first = pl.program_id(0) == 0

        @pl.when(first)
        def _():
            for cp in loads:
                cp.start()

        for a in range(n):
            w_ref, a_g_ref, m_ref, v_ref = refs[4 * a:4 * a + 4]
            outs = refs[4 * n + 4 + 4 * a:4 * n + 8 + 4 * a]
            g = a_g_ref[...]
            outs[0][...] = g
            outs[1][...], outs[2][...], outs[3][...] = _adamw_math(w_ref[...], g, m_ref[...], v_ref[...])

        @pl.when(first)
        def _():
            for cp in loads:
                cp.wait()
            for lo in range(0, cols, LANE):
                lanes = slice(lo, lo + LANE)
                g = g_ref[0:rows, lanes]
                for s in range(1, N_CHIPS):
                    g = jnp.where(c_ref[0] == s, g_ref[s * window_step:s * window_step + rows, lanes], g)
                outs_ref[0, :, lanes] = g
                outs_ref[1, :, lanes], outs_ref[2, :, lanes], outs_ref[3, :, lanes] = _adamw_math(
                    ins_ref[0, :, lanes], g, ins_ref[1, :, lanes], ins_ref[2, :, lanes])
            for cp in stores:
                cp.start()

        @pl.when(pl.program_id(0) == ADAMW_STEPS - 1)
        def _():
            for cp in stores:
                cp.wait()

    outs = pl.pallas_call(
        body, name="adamw_matrices",
        grid_spec=pltpu.PrefetchScalarGridSpec(
            num_scalar_prefetch=1, grid=(ADAMW_STEPS,),
            in_specs=[block(a.shape) for p in params for a in p] + [HBM, _const(by_row[1].shape), HBM, HBM],
            out_specs=[block(p[0].shape) for p in params for _ in range(4)] + [HBM] * 4,
            scratch_shapes=[pltpu.VMEM((3, rows, cols), F32), pltpu.VMEM((4, rows, cols), F32), _dma_sems(7)]),
        out_shape=[pltpu.HBM(p[0].shape, F32) for p in params for _ in range(4)] + [pltpu.HBM((rows, 1, cols), F32)] * 4,
        compiler_params=_params(48, dimension_semantics=_seq()),
    )(chip, *_hbm(*[a for p in params for a in p], *by_row))
    return [outs[4 * a:4 * a + 4] for a in range(n)], outs[4 * n:]


def _adamw_math(w, g, m, v):
    nm = ADAM_B1 * m + (1.0 - ADAM_B1) * g
    nv = ADAM_B2 * v + (1.0 - ADAM_B2) * (g * g)
    m_hat = nm / (1.0 - ADAM_B1 ** ADAM_STEP)
    v_hat = nv / (1.0 - ADAM_B2 ** ADAM_STEP)
    return -ADAM_LR * (m_hat / (jnp.sqrt(v_hat) + ADAM_EPS) + ADAM_WD * w), nm, nv


SMALL = (("meta_tokens", (N_META, D // N_CHIPS)), ("ln_in_g", (1, D)), ("ln_in_b", (1, D)), ("b_in", (1, D_IN)),
         ("w_gate_lr2", (GATE_RANK, GLA_HEADS * DK // N_CHIPS)), ("b_gate_lr2", (1, GLA_HEADS * DK)),
         ("attn_sinks", (1, SWA_HEADS)),
         ("gla_norm_g", (1, DV)), ("ln1_g", (1, D)), ("ln1_b", (1, D)), ("ln2_g", (1, D)), ("ln2_b", (1, D)))
ROW_META, ROW_B_IN, ROW_TAIL, ROW_WG2 = 0, 22, 25, 32
ROW_LN = dict(ln_in_g=16, ln_in_b=17, ln1_g=18, ln1_b=19, ln2_g=20, ln2_b=21)
TAIL_BG2, TAIL_SINKS, TAIL_GN, TAIL_LOSS = 0, 256, 256 + SWA_HEADS, 256 + SWA_HEADS + DV


def _adamw_small(place, packs, own, params):
    n = len(SMALL)

    def body(place_ref, packs_ref, own_ref, *refs):
        ins, outs, p_ref = refs[:3 * n], refs[3 * n:-1], refs[-1]
        me, c = place_ref[0], place_ref[1]
        total = jnp.where(me == 0, own_ref[...], packs_ref[0])
        for i in range(1, N_DEVICES):
            total = total + jnp.where(me == i, own_ref[...], packs_ref[i])
        p_ref[...] = total
        outs[4 * n][...] = total[ROW_TAIL:ROW_TAIL + 1, :]

        def mine(width, rows):
            part = lambda s: p_ref[rows, s * width:(s + 1) * width]
            return jnp.where(c == 0, part(0), jnp.where(c == 1, part(1), jnp.where(c == 2, part(2), part(3))))

        tail = lambda lo, width: p_ref[ROW_TAIL:ROW_TAIL + 1, lo:lo + width]
        grads = dict(
            meta_tokens=mine(D // N_CHIPS, slice(ROW_META, ROW_META + N_META)),
            b_in=jnp.concatenate([p_ref[ROW_B_IN:ROW_B_IN + 1, :], p_ref[ROW_B_IN + 1:ROW_B_IN + 2, :],
                                  p_ref[ROW_B_IN + 2:ROW_B_IN + 3, 0:D_IN - 2 * D]], axis=1),
            w_gate_lr2=mine(256 // N_CHIPS, slice(ROW_WG2, ROW_WG2 + 16)),
            b_gate_lr2=tail(TAIL_BG2, 256), attn_sinks=tail(TAIL_SINKS, SWA_HEADS), gla_norm_g=tail(TAIL_GN, DV),
            **{k: p_ref[r:r + 1, :] for k, r in ROW_LN.items()})
        for i, (name, _) in enumerate(SMALL):
            g = grads[name]
            outs[4 * i][...] = g
            outs[4 * i + 1][...], outs[4 * i + 2][...], outs[4 * i + 3][...] = _adamw_math(
                ins[3 * i][...], g, ins[3 * i + 1][...], ins[3 * i + 2][...])

    whole = lambda shape: pl.BlockSpec(shape, lambda i, c: (0,) * len(shape))
    outs = pl.pallas_call(
        body, name="adamw_small",
        grid_spec=pltpu.PrefetchScalarGridSpec(
            num_scalar_prefetch=1, grid=(1,),
            in_specs=[whole(packs.shape), whole(own.shape)] + [whole(s) for _, s in SMALL for _ in range(3)],
            out_specs=[whole(s) for _, s in SMALL for _ in range(4)] + [whole((1, D))],
            scratch_shapes=[pltpu.VMEM(own.shape, F32)]),
        out_shape=[pltpu.HBM(s, F32) for _, s in SMALL for _ in range(4)] + [pltpu.HBM((1, D), F32)],
        compiler_params=_params(16, dimension_semantics=_seq()),
    )(place, *_hbm(packs, own, *[a for p in params for a in p]))
    return [outs[4 * i:4 * i + 4] for i in range(n)], outs[4 * n]


def _small_pack(gr):
    names = ["meta_blk"] + list(ROW_LN) + ["b_in_p", "wg2_p", "bg2", "sinks", "gn", "loss"]
    gate_w = GLA_HEADS * DK

    def body(*refs):
        src, out = dict(zip(names, refs)), refs[-1]
        out[...] = jnp.zeros_like(out)
        out[ROW_META:ROW_META + N_META, :] = src["meta_blk"][META_OFF:CH, :]
        for k, r in ROW_LN.items():
            out[r:r + 1, :] = src[k][...]
        for j in range(-(-D_IN // D)):
            width = min(D, D_IN - j * D)
            out[ROW_B_IN + j:ROW_B_IN + j + 1, 0:width] = src["b_in_p"][:, j * D:j * D + width]
        tail = slice(ROW_TAIL, ROW_TAIL + 1)
        out[tail, TAIL_BG2:TAIL_BG2 + gate_w] = src["bg2"][...]
        out[tail, TAIL_SINKS:TAIL_SINKS + SWA_HEADS] = src["sinks"][:, 0:SWA_HEADS]
        out[tail, TAIL_GN:TAIL_GN + DV] = src["gn"][...]
        out[tail, TAIL_LOSS:TAIL_LOSS + 1] = src["loss"][:, 0:1]
        out[ROW_WG2:ROW_WG2 + GATE_RANK, 0:gate_w] = src["wg2_p"][0:GATE_RANK, :]

    arrays = [gr[k] for k in names]
    return pl.pallas_call(
        body, name="small_pack", grid=(1,),
        in_specs=[_acc(a.shape) for a in arrays], out_specs=_acc((SMALL_ROWS, D)),
        out_shape=pltpu.HBM((SMALL_ROWS, D), F32),
        compiler_params=_params(16, dimension_semantics=_seq()),
    )(*_hbm(*arrays))


BIG = ("w_in", "w_out", "w_g", "w_u", "w_d")


def kernel(x, meta_tokens, ln_in_g, ln_in_b, w_in, b_in, w_gate_lr2, b_gate_lr2, attn_sinks, gla_norm_g, w_out, ln1_g, ln1_b, w_ffn_gate, w_ffn_up, w_ffn_down, ln2_g, ln2_b, loss_target, m_meta_tokens, m_ln_in_g, m_ln_in_b, m_w_in, m_b_in, m_w_gate_lr2, m_b_gate_lr2, m_attn_sinks, m_gla_norm_g, m_w_out, m_ln1_g, m_ln1_b, m_w_ffn_gate, m_w_ffn_up, m_w_ffn_down, m_ln2_g, m_ln2_b, v_meta_tokens, v_ln_in_g, v_ln_in_b, v_w_in, v_b_in, v_w_gate_lr2, v_b_gate_lr2, v_attn_sinks, v_gla_norm_g, v_w_out, v_ln1_g, v_ln1_b, v_w_ffn_gate, v_w_ffn_up, v_w_ffn_down, v_ln2_g, v_ln2_b):
    chip = 2 * lax.axis_index("x") + lax.axis_index("y")

    halves = lambda a: a.reshape(2, a.shape[0] // 2, a.shape[1])
    r_in = SHARD_ROWS["w_in"]
    first = [halves(a) for a in (jnp.pad(w_in[0].T.astype(BF16), ((0, W_IN_WIN - r_in), (0, 0))), meta_tokens,
                                 w_gate_lr2[0])]
    rest = [halves(a) for a in (w_out[0].astype(BF16), w_ffn_gate[0].T.astype(BF16), w_ffn_up[0].T.astype(BF16),
                                w_ffn_down[0].astype(BF16))]
    lands = lambda arrs: [(N_CHIPS,) + a.shape for a in arrs]
    first_handle, first_token = _ici_start("gather", first, lands(first), [], "gather_first_start")
    rest_handle, token = _ici_start("gather", rest, lands(rest), [first_token], "gather_rest_start")
    own_slab = lambda got, shards: [lax.dynamic_update_index_in_dim(g, s, chip, axis=0) for g, s in zip(got, shards)]
    fetching = {}

    def fetch_first(after):
        shards, landed = _ici_wait("gather", first_handle, after, "gather_first_wait")
        g_in, g_meta, g_wg2 = own_slab(_sibling_forward(landed, "gather_first_forward"), shards)
        w_in_windows = g_in.reshape(N_CHIPS, W_IN_WIN, D)
        meta_full = jnp.concatenate([g_meta[s].reshape(N_META, -1) for s in range(N_CHIPS)], axis=1)
        wg2_full = jnp.concatenate([g_wg2[s].reshape(w_gate_lr2.shape[1], -1) for s in range(N_CHIPS)], axis=1)
        return w_in_windows, meta_full, wg2_full

    def fetch_rest(after):
        shards, landed = _ici_wait("gather", rest_handle, after, "gather_rest_wait")
        g_out, = own_slab(_sibling_forward(landed[:1], "gather_w_out_forward"), shards[:1])
        fetching["shards"] = shards[1:]
        fetching["handle"], forward_token = _forward_start(landed[1:], "gather_ffn_forward_start")
        return g_out.reshape(-1, D), forward_token

    def fetch_ffn(after):
        got = _forward_wait(fetching["handle"], after, "gather_ffn_forward_wait")
        return [g.reshape(-1, D) for g in own_slab(got, fetching["shards"])]

    sent = {}
    split = lambda grads: [g.reshape(N_CHIPS, 2, -1, D) for g in grads]

    def ship(key, grads, names, fetched=()):
        parts = _chip_partials(grads, [BF16] * len(grads), names, fetched)
        sent[key], ship_token = _ici_start("scatter", parts, [p.shape for p in parts], [], "scatter_" + key + "_start")
        return ship_token

    def exchange_ffn(g):
        grads = split([g[k] for k in BIG[1:]])
        sent["ffn_halves"], exchange_token = _ici_start("sibling", grads, [(N_CHIPS,) + a.shape[2:] for a in grads], [],
                                                        "sibling_ffn_start")
        return exchange_token

    def ship_ffn(after):
        grads, fetched = _ici_wait("sibling", sent["ffn_halves"], after, "sibling_ffn_wait")
        return ship("ffn", grads, list(BIG[1:]), fetched)

    def ship_w_in(dw_in_windows):
        return ship("w_in", split([dw_in_windows]), ["w_in"])

    dx, gr = _local_step(
        x[0], loss_target[0], ln_in_g, ln_in_b, b_in[0], b_gate_lr2[0], attn_sinks[0], gla_norm_g[0], ln1_g[0],
        ln1_b[0], ln2_g[0], ln2_b[0], token, fetch_first, fetch_rest, fetch_ffn, exchange_ffn, ship_ffn, ship_w_in)
    ffn_parts, ffn_got = _ici_wait("scatter", sent["ffn"], [dx], "scatter_ffn_wait")
    join_handle, token = _join_start(_chip_sums(ffn_parts, ffn_got, [], "sum_chips_ffn"), "join_ffn_start")
    w_in_parts, w_in_got = _ici_wait("scatter", sent["w_in"], [token], "scatter_w_in_wait")

    small_handle, token = _small_start(_small_pack(gr), [w_in_got[0]])
    w_in_joined = _join_halves(_chip_sums(w_in_parts, w_in_got, [token], "sum_chips_w_in"), "join_w_in")
    red = [f.reshape(2 * f.shape[1], D) for f in w_in_joined + _join_wait(join_handle, w_in_joined, "join_ffn_wait")]

    big_g = dict(zip(BIG, red))
    weights = dict(meta_tokens=meta_tokens, ln_in_g=ln_in_g, ln_in_b=ln_in_b, w_in=w_in, b_in=b_in,
                   w_gate_lr2=w_gate_lr2, b_gate_lr2=b_gate_lr2, attn_sinks=attn_sinks, gla_norm_g=gla_norm_g,
                   w_out=w_out, ln1_g=ln1_g, ln1_b=ln1_b, w_ffn_gate=w_ffn_gate, w_ffn_up=w_ffn_up,
                   w_ffn_down=w_ffn_down, ln2_g=ln2_g, ln2_b=ln2_b)
    m_in = dict(meta_tokens=m_meta_tokens, ln_in_g=m_ln_in_g, ln_in_b=m_ln_in_b, w_in=m_w_in, b_in=m_b_in,
                w_gate_lr2=m_w_gate_lr2, b_gate_lr2=m_b_gate_lr2, attn_sinks=m_attn_sinks, gla_norm_g=m_gla_norm_g,
                w_out=m_w_out, ln1_g=m_ln1_g, ln1_b=m_ln1_b, w_ffn_gate=m_w_ffn_gate, w_ffn_up=m_w_ffn_up,
                w_ffn_down=m_w_ffn_down, ln2_g=m_ln2_g, ln2_b=m_ln2_b)
    v_in = dict(meta_tokens=v_meta_tokens, ln_in_g=v_ln_in_g, ln_in_b=v_ln_in_b, w_in=v_w_in, b_in=v_b_in,
                w_gate_lr2=v_w_gate_lr2, b_gate_lr2=v_b_gate_lr2, attn_sinks=v_attn_sinks, gla_norm_g=v_gla_norm_g,
                w_out=v_w_out, ln1_g=v_ln1_g, ln1_b=v_ln1_b, w_ffn_gate=v_w_ffn_gate, w_ffn_up=v_w_ffn_up,
                w_ffn_down=v_w_ffn_down, ln2_g=v_ln2_g, ln2_b=v_ln2_b)
    names = list(weights)
    big_names = ("w_in", "w_out", "w_ffn_gate", "w_ffn_up", "w_ffn_down")

    grads, delta, new_m, new_v = {}, {}, {}, {}
    flips = [(lambda a: a.T) if kk in ("w_g", "w_u") else (lambda a: a) for kk in BIG[1:]]
    by_row = lambda a: jnp.transpose(a, (2, 0, 1))
    updated, updated_w_in = _adamw(
        [(flip(weights[k][0]), big_g[kk], flip(m_in[k][0]), flip(v_in[k][0]))
         for k, kk, flip in zip(big_names[1:], BIG[1:], flips)],
        (by_row(w_in), big_g["w_in"], by_row(m_w_in), by_row(v_w_in)), chip.astype(jnp.int32).reshape(1), r_in % BF16_ROWS)
    for k, flip, results in zip(big_names[1:], flips, updated):
        grads[k], delta[k], new_m[k], new_v[k] = (flip(t)[None] for t in results)
    grads["w_in"], delta["w_in"], new_m["w_in"], new_v["w_in"] = (jnp.transpose(t, (1, 2, 0)) for t in updated_w_in)
    small_in = [tuple(src[k].reshape(shape) for src in (weights, m_in, v_in)) for k, shape in SMALL]
    place = jnp.stack([2 * chip + lax.axis_index("c"), chip]).astype(jnp.int32)
    small_own, small_all = _small_wait(small_handle, [updated[0][0]])
    small_out, tail_row = _adamw_small(place, small_all, small_own, small_in)
    for (k, _), results in zip(SMALL, small_out):
        grads[k], delta[k], new_m[k], new_v[k] = (r.reshape(weights[k].shape) for r in results)

    return (tail_row[0, TAIL_LOSS], dx[None], *[grads[k] for k in names], *[delta[k] for k in names], *[new_m[k] for k in names],
            *[new_v[k] for k in names])
```

```python
import jax
import jax.numpy as jnp
from jax import lax
from jax.experimental import pallas as pl
from jax.experimental.pallas import tpu as pltpu

F32 = jnp.float32
BF16 = jnp.bfloat16
MESH = pl.DeviceIdType.MESH

D = 1024
SEQ = 4096
N_META = 16
SWA_HEADS, SWA_KV_HEADS, DH = 8, 2, 64
WINDOW = 128
GLA_HEADS, DK, DV = 4, 64, 128
GLA_TAU = 16.0
CH = 64
D_FF = 2816
D_IN = 2320
LN_EPS = 1e-5
RMS_EPS = 1e-6
ALPHA = 2.0 ** 0.25
NEG = -1e30
ADAM_LR, ADAM_B1, ADAM_B2, ADAM_EPS, ADAM_WD, ADAM_STEP = 0.001, 0.9, 0.999, 1e-8, 0.01, 10
O_QS, O_KS, O_VS, O_QG, O_KG, O_VG, O_RG, O_LR = 0, 512, 640, 768, 1024, 1280, 1792, 2304

LANE = 128
BLK = WINDOW
GATE_RANK = 16
D_IN_P = D_IN + LANE - GATE_RANK
META_OFF = CH - N_META
HEAD_POS = (0, 4, 1, 5, 2, 6, 3, 7)
LN_ROWS = 512
TOKEN = (8, LANE)
N_CHIPS = 4
SHARD_ROWS = dict(w_in=D_IN // N_CHIPS, w_out=D // N_CHIPS, w_g=D_FF // N_CHIPS, w_u=D_FF // N_CHIPS,
                  w_d=D_FF // N_CHIPS)
SMALL_ROWS = 48
BF16_ROWS = 16
W_IN_WIN = -(-SHARD_ROWS["w_in"] // (2 * BF16_ROWS)) * 2 * BF16_ROWS
W_IN_STARTS = tuple(s * SHARD_ROWS["w_in"] // BF16_ROWS * BF16_ROWS for s in range(N_CHIPS))
VMEM_CAP_MB = 64
VMEM_SPARE_MB = 6


def _lp():
    return SEQ + BLK


def _row_tile(cap):
    lp = _lp()
    return max(t for t in range(16, cap + 1, 16) if lp % t == 0)


def _params(vmem_mb, **kw):
    assert vmem_mb <= VMEM_CAP_MB - VMEM_SPARE_MB
    return pltpu.CompilerParams(vmem_limit_bytes=vmem_mb << 20, **kw)


def _seq(n=1):
    return ("arbitrary",) * n


def _const(shape):
    return pl.BlockSpec(shape, lambda *_: (0,) * len(shape), pipeline_mode=pl.Buffered(1))


def _acc(shape):
    return pl.BlockSpec(shape, lambda *_: (0,) * len(shape))


def _rows(tm, width):
    return pl.BlockSpec((tm, width), lambda i: (i, 0))


def _dot(a, b):
    return jnp.dot(a.astype(BF16), b.astype(BF16), preferred_element_type=F32)


def _dot_nt(a, b):
    return lax.dot_general(a.astype(BF16), b.astype(BF16), (((1,), (1,)), ((), ())), preferred_element_type=F32)


def _dot_tn(a, b):
    return lax.dot_general(a.astype(BF16), b.astype(BF16), (((0,), (0,)), ((), ())), preferred_element_type=F32)


def _dot_exact(a, b):
    return jnp.dot(a, b, precision=lax.Precision.HIGHEST, preferred_element_type=F32)


def _ln_stats(x):
    mu = jnp.mean(x, axis=-1, keepdims=True)
    xc = x - mu
    rstd = lax.rsqrt(jnp.mean(xc * xc, axis=-1, keepdims=True) + LN_EPS)
    return xc * rstd, rstd


def _ln_bwd(dy, xhat, rstd, g):
    dxh = dy * g
    return rstd * (dxh - jnp.mean(dxh, axis=-1, keepdims=True) - xhat * jnp.mean(dxh * xhat, axis=-1, keepdims=True))


def _sigmoid(x):
    return 1.0 / (1.0 + jnp.exp(-x))


def _iota(shape, dim):
    return lax.broadcasted_iota(jnp.int32, shape, dim)


def _hbm(*arrays):
    return tuple(pltpu.with_memory_space_constraint(a, pltpu.HBM) for a in arrays)


def _ln_in_fwd_real(x, g, b, token):
    tr = min(LN_ROWS, SEQ)

    def body(x_ref, g_ref, b_ref, token_ref, h_ref):
        xhat, _ = _ln_stats(x_ref[...])
        h_ref[...] = xhat * g_ref[...] + b_ref[...]

    return pl.pallas_call(
        body, name="ln_in_fwd", grid=(SEQ // tr,),
        in_specs=[_rows(tr, D), _const((1, D)), _const((1, D)), _const(TOKEN)],
        out_specs=_rows(tr, D),
        out_shape=pltpu.HBM((_lp(), D), F32),
        compiler_params=_params(32, dimension_semantics=_seq()),
    )(*_hbm(x, g, b), token)


def _ln_in_fwd_meta(h_real, meta_ext, g, b):
    def meta_body(m_ref, g_ref, b_ref, real_ref, h_ref):
        xhat, _ = _ln_stats(m_ref[...])
        h_ref[...] = xhat * g_ref[...] + b_ref[...]

    return pl.pallas_call(
        meta_body, name="ln_in_fwd_meta", grid=(1,),
        in_specs=[_const((BLK, D)), _const((1, D)), _const((1, D)), pl.BlockSpec(memory_space=pl.ANY)],
        out_specs=pl.BlockSpec((BLK, D), lambda i: (SEQ // BLK, 0)),
        out_shape=pltpu.HBM((_lp(), D), F32),
        input_output_aliases={3: 0},
        compiler_params=_params(16, dimension_semantics=_seq()),
    )(*_hbm(meta_ext, g, b, h_real))


def _in_proj(h0, w_in_windows, b_in_p, wg2_p, bg2):
    tm = _row_tile(384)
    lp = _lp()
    widths = (512, 128, 128, 256, 256, 512, 512, 128)
    offs = (O_QS, O_KS, O_VS, O_QG, O_KG, O_VG, O_RG, O_LR)
    shard = SHARD_ROWS["w_in"]

    def body(h_ref, win_ref, b_ref, wg2_ref, bg2_ref, *outs):
        w_ref = outs[9]

        @pl.when(pl.program_id(0) == 0)
        def _():
            for s in range(N_CHIPS):
                w_ref[shard * s:shard * (s + 1), :] = win_ref[s, 0:shard, :]
            w_ref[D_IN:D_IN_P, :] = jnp.zeros((D_IN_P - D_IN, D), BF16)

        proj = _dot_nt(h_ref[...], w_ref[...]) + b_ref[...]
        for pos, h in enumerate(HEAD_POS):
            outs[0][:, pos * DH:(pos + 1) * DH] = proj[:, O_QS + h * DH:O_QS + (h + 1) * DH]
        for o_ref, off, wd in zip(outs[1:8], offs[1:], widths[1:]):
            o_ref[...] = proj[:, off:off + wd]
        outs[8][...] = _dot(proj[:, O_LR:O_LR + LANE], wg2_ref[...]) + bg2_ref[...]

    return pl.pallas_call(
        body, name="in_proj", grid=(lp // tm,),
        in_specs=[_rows(tm, D), _const(w_in_windows.shape), _const((1, D_IN_P)), _const((LANE, 256)), _const((1, 256))],
        out_specs=[_rows(tm, w) for w in widths] + [_rows(tm, 256), _acc((D_IN_P, D))],
        out_shape=[pltpu.HBM((lp, w), F32) for w in widths] + [pltpu.HBM((lp, 256), F32), pltpu.HBM((D_IN_P, D), BF16)],
        compiler_params=_params(48, dimension_semantics=_seq()),
    )(*_hbm(h0, w_in_windows, b_in_p, wg2_p, bg2))


def _swa_masks(n):
    nb = SEQ // BLK
    is_meta = n == nb
    ri = _iota((BLK, BLK), 0)
    cj = _iota((BLK, BLK), 1)
    meta_col = ((cj >= META_OFF) & (cj < CH)).astype(jnp.int32)
    meta_q = meta_col * ((cj <= ri) & (ri < CH)).astype(jnp.int32)
    valid_m = jnp.where(is_meta, meta_q, meta_col) > 0
    dist_m = jnp.where(is_meta, ri - cj, n * BLK + ri + CH - cj).astype(F32)
    valid_p = jnp.where((n >= 1) & (n < nb), (cj > ri).astype(jnp.int32), 0) > 0
    dist_p = (ri + BLK - cj).astype(F32)
    valid_c = jnp.where(n < nb, (cj <= ri).astype(jnp.int32), 0) > 0
    dist_c = (ri - cj).astype(F32)
    return (dist_m, dist_p, dist_c), (valid_m, valid_p, valid_c)


def _swa_bias(n):
    dists, valids = _swa_masks(n)
    return (jnp.concatenate([-d for d in dists], axis=1),
            jnp.concatenate([jnp.where(v, 0.0, NEG) for v in valids], axis=1))


def _swa_half(ref, pos, scale=1.0):
    col = ref[:, (pos // 2) * LANE:(pos // 2 + 1) * LANE]
    lane = _iota((BLK, LANE), 1)
    mine = lane < DH if pos % 2 == 0 else lane >= DH
    return jnp.where(mine, col * scale, 0.0).astype(BF16)


def _swa_merge(even, odd):
    return jnp.where(_iota((BLK, LANE), 1) < DH, even, odd)


def _swa_softmax(t, sink):
    m = jnp.maximum(jnp.max(t, axis=-1, keepdims=True), sink)
    e = jnp.exp(t - m)
    e_sink = jnp.exp(sink - m)
    inv = 1.0 / (jnp.sum(e, axis=-1, keepdims=True) + e_sink)
    return e * inv, e_sink * inv


def _swa_kv_specs(width):
    nb = SEQ // BLK
    return [pl.BlockSpec((BLK, width), lambda n: (nb, 0)),
            pl.BlockSpec((BLK, width), lambda n: (jnp.clip(n - 1, 0, nb - 1), 0)),
            pl.BlockSpec((BLK, width), lambda n: (jnp.minimum(n, nb), 0))]


def _swa_fwd(sinks, qs, ks, vs):
    nb = SEQ // BLK
    heads = range(SWA_HEADS)

    def body(sink_ref, q_ref, km_ref, kp_ref, kc_ref, vm_ref, vp_ref, vc_ref, o_ref):
        negdist, maskbias = _swa_bias(pl.program_id(0))
        k_all = jnp.concatenate([km_ref[...], kp_ref[...], kc_ref[...]], axis=0).astype(BF16)
        v_all = jnp.concatenate([vm_ref[...], vp_ref[...], vc_ref[...]], axis=0).astype(BF16)
        q = [_swa_half(q_ref, pos, DH ** -0.5) for pos in heads]
        t = [_dot_nt(q[pos], k_all) + (2.0 ** -(HEAD_POS[pos] + 1) * negdist + maskbias) for pos in heads]
        p = [_swa_softmax(t[pos], sink_ref[HEAD_POS[pos]])[0].astype(BF16) for pos in heads]
        o = [_dot(p[pos], v_all) for pos in heads]
        for col in range(SWA_HEADS // 2):
            o_ref[:, col * LANE:(col + 1) * LANE] = _swa_merge(o[2 * col], o[2 * col + 1])

    kvw = SWA_KV_HEADS * DH
    return pl.pallas_call(
        body, name="swa_fwd", grid=(nb + 1,),
        in_specs=[pl.BlockSpec(memory_space=pltpu.SMEM), _rows(BLK, SWA_HEADS * DH)] + _swa_kv_specs(kvw) + _swa_kv_specs(kvw),
        out_specs=_rows(BLK, SWA_HEADS * DH),
        out_shape=pltpu.HBM((_lp(), SWA_HEADS * DH), F32),
        compiler_params=_params(16, dimension_semantics=_seq()),
    )(sinks, *_hbm(qs, ks, ks, ks, vs, vs, vs))


GLA_PER_STEP = BLK // CH


def _gla_block(s):
    nb = SEQ // BLK
    return jnp.where(s == 0, nb, s - 1)


def _gla_rowmask(s):
    ri = _iota((BLK, 1), 0)
    m = jnp.where(s == 0, ((ri >= META_OFF) & (ri < CH)).astype(jnp.int32), 1)
    return (m > 0).astype(F32) + jnp.zeros((BLK, 1), F32)


def _gla_chunk_masks():
    r, c = _iota((BLK, BLK), 0), _iota((BLK, BLK), 1)
    same = ((r < CH) & (c < CH)) | ((r >= CH) & (c >= CH))
    return same & (r >= c), same & (r <= c), same


def _gla_decay(z, rmask):
    log_g = (jnp.minimum(z, 0.0) - jnp.log1p(jnp.exp(-jnp.abs(z)))) * (rmask / GLA_TAU)
    lower, _, same = _gla_chunk_masks()
    return _dot_exact(lower.astype(F32), log_g), _dot_exact(same.astype(F32), log_g)


def _gla_slices(c, h):
    return slice(c * CH, (c + 1) * CH), slice(h * DK, (h + 1) * DK), slice(h * DV, (h + 1) * DV)


def _gla_fwd(qg, kg, vg, z):
    steps = SEQ // BLK + 1
    kw, vw = GLA_HEADS * DK, GLA_HEADS * DV
    pairs = [(c, h) for c in range(GLA_PER_STEP) for h in range(GLA_HEADS)]

    def body(q_ref, k_ref, v_ref, z_ref, o_ref, st_ref, st):
        s = pl.program_id(0)

        @pl.when(s == 0)
        def _():
            st[...] = jnp.zeros_like(st)

        rmask = _gla_rowmask(s)
        b, b_last = _gla_decay(z_ref[...], rmask)
        q = q_ref[...] * (rmask * DK ** -0.5)
        k = k_ref[...] * rmask
        v = v_ref[...] * rmask
        qe = q * jnp.exp(b)
        ke = k * jnp.exp(-b)
        kd = k * jnp.exp(b_last - b)
        e_last = jnp.exp(b_last)
        causal = _iota((CH, CH), 0) >= _iota((CH, CH), 1)
        a, upd, intra = {}, {}, {}
        for c, h in pairs:
            rows, ks, vs_ = _gla_slices(c, h)
            a[c, h] = jnp.where(causal, _dot_nt(qe[rows, ks], ke[rows, ks]), 0.0)
            upd[c, h] = _dot_tn(v[rows, vs_], kd[rows, ks])
        for c, h in pairs:
            rows, ks, vs_ = _gla_slices(c, h)
            intra[c, h] = _dot(a[c, h], v[rows, vs_])
        state = st[...]
        for c in range(GLA_PER_STEP):
            st_ref[0, c] = state
            for h in range(GLA_HEADS):
                rows, ks, vs_ = _gla_slices(c, h)
                o_ref[rows, vs_] = intra[c, h] + _dot_nt(qe[rows, ks], state[:, ks])
            state = state * e_last[c * CH:c * CH + 1] + jnp.concatenate([upd[c, h] for h in range(GLA_HEADS)], axis=1)
        st[...] = state

    blk = lambda w: pl.BlockSpec((BLK, w), lambda s: (_gla_block(s), 0))
    return pl.pallas_call(
        body, name="gla_fwd", grid=(steps,),
        in_specs=[blk(kw), blk(kw), blk(vw), blk(kw)],
        out_specs=[blk(vw), pl.BlockSpec((1, GLA_PER_STEP, DV, kw), lambda s: (s, 0, 0, 0))],
        out_shape=[pltpu.HBM((_lp(), vw), F32), pltpu.HBM((steps, GLA_PER_STEP, DV, kw), F32)],
        scratch_shapes=[pltpu.VMEM((DV, kw), F32)],
        compiler_params=_params(16, dimension_semantics=_seq()),
    )(*_hbm(qg, kg, vg, z))


def _post_mix(o_s, o_gla, r_g, h0, gn4, w_out, g1, b1, token):
    tm = _row_tile(384)
    lp = _lp()

    def body(os_ref, og_ref, r_ref, h0_ref, gn_ref, w_ref, g_ref, b_ref, token_ref, o_ref, pre_ref, h1_ref):
        for pos, h in enumerate(HEAD_POS):
            o_ref[:, h * DH:(h + 1) * DH] = os_ref[:, pos * DH:(pos + 1) * DH].astype(BF16)
        for h in range(GLA_HEADS):
            hs = slice(h * DV, (h + 1) * DV)
            xg = og_ref[:, hs]
            n = xg * lax.rsqrt(jnp.mean(xg * xg, axis=-1, keepdims=True) + RMS_EPS) * gn_ref[...]
            r = r_ref[:, hs]
            o_ref[:, 512 + h * DV:512 + (h + 1) * DV] = (n * (r * _sigmoid(r))).astype(BF16)
        pre = ALPHA * h0_ref[...] + _dot(o_ref[...], w_ref[...])
        pre_ref[...] = pre
        xhat, _ = _ln_stats(pre)
        h1_ref[...] = xhat * g_ref[...] + b_ref[...]

    return pl.pallas_call(
        body, name="post_mix", grid=(lp // tm,),
        in_specs=[_rows(tm, 512), _rows(tm, 512), _rows(tm, 512), _rows(tm, D), _const((1, DV)), _const((D, D)),
                  _const((1, D)), _const((1, D)), _const(TOKEN)],
        out_specs=[_rows(tm, D), _rows(tm, D), _rows(tm, D)],
        out_shape=[pltpu.HBM((lp, D), BF16), pltpu.HBM((lp, D), F32),
                   pltpu.HBM((lp, D), F32)],
        compiler_params=_params(32, dimension_semantics=_seq()),
    )(*_hbm(o_s, o_gla, r_g, h0, gn4, w_out, g1, b1), token)


def _ffn_fwd_loss_bwd(h1, wg_t, wu_t, wd, target, g2, b2):
    lp = _lp()
    tm = max(t for t in range(BLK, 384 + 1, BLK) if lp % t == 0)
    steps = lp // tm
    last_blk = SEQ // BLK - 1
    half = D_FF // 2
    n_t = tm // BLK

    def body(*refs):
        h_ref, wg_ref, wu_ref, wd_ref = refs[:4]
        t_refs = refs[4:4 + n_t]
        g2_ref, b2_ref, a_ref, dgate_ref, dup_ref, dp_ref, loss_ref, dg_ref, db_ref, g_s, u_s, acc = refs[4 + n_t:]
        i = pl.program_id(0)

        @pl.when(i == 0)
        def _():
            acc[...] = jnp.zeros_like(acc)
            dg_ref[...] = jnp.zeros_like(dg_ref)
            db_ref[...] = jnp.zeros_like(db_ref)

        h = h_ref[...]
        hb = h.astype(BF16)
        pre = ALPHA * h
        for j in range(2):
            cols = slice(j * half, (j + 1) * half)
            g = _dot_nt(hb, wg_ref[cols, :])
            u = _dot_nt(hb, wu_ref[cols, :])
            g_s[:, cols] = g
            u_s[:, cols] = u
            pre = pre + _dot(g * _sigmoid(g) * u, wd_ref[cols, :])
        xhat, rstd = _ln_stats(pre)
        real = i * tm + _iota((tm, 1), 0) < SEQ
        target_rows = jnp.concatenate([t[...] for t in t_refs], axis=0)
        diff = jnp.where(real, xhat * g2_ref[...] + b2_ref[...] - target_rows, 0.0)
        acc[...] += jnp.sum(diff * diff, axis=0, keepdims=True)
        dy = diff * (1.0 / D)
        dpre = _ln_bwd(dy, xhat, rstd, g2_ref[...])
        dp_ref[...] = dpre
        dg_ref[...] += jnp.sum(dy * xhat, axis=0, keepdims=True)
        db_ref[...] += jnp.sum(dy, axis=0, keepdims=True)
        dpb = dpre.astype(BF16)
        for j in range(2):
            cols = slice(j * half, (j + 1) * half)
            g, u = g_s[:, cols], u_s[:, cols]
            sg = _sigmoid(g)
            silu = g * sg
            da = _dot_nt(dpb, wd_ref[cols, :])
            a_ref[:, cols] = (silu * u).astype(BF16)
            dgate_ref[:, cols] = (da * u * (sg * (1.0 + g * (1.0 - sg)))).astype(BF16)
            dup_ref[:, cols] = (da * silu).astype(BF16)

        @pl.when(i == steps - 1)
        def _():
            loss_ref[...] = jnp.zeros_like(loss_ref) + (0.5 / D) * jnp.sum(acc[...], axis=1, keepdims=True)

    t_spec = lambda k: pl.BlockSpec((BLK, D), lambda i: (jnp.minimum(i * n_t + k, last_blk), 0))
    return pl.pallas_call(
        body, name="ffn_fwd_loss_bwd", grid=(steps,),
        in_specs=[_rows(tm, D), _const((D_FF, D)), _const((D_FF, D)), _const((D_FF, D))]
        + [t_spec(k) for k in range(n_t)] + [_const((1, D)), _const((1, D))],
        out_specs=[_rows(tm, D_FF), _rows(tm, D_FF), _rows(tm, D_FF), _rows(tm, D), _acc((1, LANE)), _acc((1, D)),
                   _acc((1, D))],
        out_shape=[pltpu.HBM((lp, D_FF), BF16)] * 3 + [pltpu.HBM((lp, D), F32), pltpu.HBM((1, LANE), F32),
                                                         pltpu.HBM((1, D), F32), pltpu.HBM((1, D), F32)],
        scratch_shapes=[pltpu.VMEM((tm, D_FF), F32), pltpu.VMEM((tm, D_FF), F32), pltpu.VMEM((1, D), F32)],
        compiler_params=_params(58, dimension_semantics=_seq()),
    )(*_hbm(h1, wg_t, wu_t, wd, *[target] * n_t, g2, b2))


def _ffn_out_bwd(dpre2, dgate, dup, pre1, wg_t, wu_t, g1, w_out, o_gla, r_g, gn4):
    tm = _row_tile(384)
    lp = _lp()

    def body(dp_ref, dg_ref, du_ref, p1_ref, wg_ref, wu_ref, g1_ref, w_ref, og_ref, r_ref, gn_ref,
             dp1_ref, dg1_ref, db1_ref, dos_ref, dog_ref, dr_ref, dgn_ref):
        @pl.when(pl.program_id(0) == 0)
        def _():
            for acc_ref in (dg1_ref, db1_ref, dgn_ref):
                acc_ref[...] = jnp.zeros_like(acc_ref)

        dh1 = ALPHA * dp_ref[...] + _dot(dg_ref[...], wg_ref[...]) + _dot(du_ref[...], wu_ref[...])
        xhat, rstd1 = _ln_stats(p1_ref[...])
        dpre1 = _ln_bwd(dh1, xhat, rstd1, g1_ref[...])
        dp1_ref[...] = dpre1
        dg1_ref[...] += jnp.sum(dh1 * xhat, axis=0, keepdims=True)
        db1_ref[...] += jnp.sum(dh1, axis=0, keepdims=True)

        do = _dot_nt(dpre1, w_ref[...])
        for pos, h in enumerate(HEAD_POS):
            dos_ref[:, pos * DH:(pos + 1) * DH] = do[:, h * DH:(h + 1) * DH]
        gn = gn_ref[...]
        for h in range(GLA_HEADS):
            hs = slice(h * DV, (h + 1) * DV)
            xg = og_ref[:, hs]
            rstd = lax.rsqrt(jnp.mean(xg * xg, axis=-1, keepdims=True) + RMS_EPS)
            nx = xg * rstd
            r = r_ref[:, hs]
            sr = _sigmoid(r)
            d_o = do[:, 512 + h * DV:512 + (h + 1) * DV]
            dr_ref[:, hs] = d_o * (nx * gn) * (sr * (1.0 + r * (1.0 - sr)))
            dn = d_o * (r * sr)
            dgn_ref[...] += jnp.sum(dn * nx, axis=0, keepdims=True)
            dnx = dn * gn
            dog_ref[:, hs] = rstd * (dnx - nx * jnp.mean(dnx * nx, axis=-1, keepdims=True))

    return pl.pallas_call(
        body, name="ffn_out_bwd", grid=(lp // tm,),
        in_specs=[_rows(tm, D), _rows(tm, D_FF), _rows(tm, D_FF), _rows(tm, D), _const((D_FF, D)), _const((D_FF, D)),
                  _const((1, D)), _const((D, D)), _rows(tm, 512), _rows(tm, 512), _const((1, DV))],
        out_specs=[_rows(tm, D), _acc((1, D)), _acc((1, D)), _rows(tm, 512), _rows(tm, 512), _rows(tm, 512),
                   _acc((1, DV))],
        out_shape=[pltpu.HBM((lp, D), F32), pltpu.HBM((1, D), F32), pltpu.HBM((1, D), F32)]
        + [pltpu.HBM((lp, 512), F32)] * 3 + [pltpu.HBM((1, DV), F32)],
        compiler_params=_params(48, dimension_semantics=_seq()),
    )(*_hbm(dpre2, dgate, dup, pre1, wg_t, wu_t, g1, w_out, o_gla, r_g, gn4))


def _atb(a, b, name, token=None, windows=None):
    lp = _lp()
    tm = _row_tile(1408)
    n, w = a.shape[1], b.shape[1]
    bw = 512 if n * w * 4 > (4 << 20) else w
    tokens = [] if token is None else [token]
    steps = lp // tm

    def body(a_ref, b_ref, *rest):
        o_ref, acc_ref = rest[len(tokens):] if windows else (rest[-1], rest[-1])

        @pl.when(pl.program_id(1) == 0)
        def _():
            acc_ref[...] = jnp.zeros_like(acc_ref)

        acc_ref[...] += _dot_tn(a_ref[...], b_ref[...])

        if windows:
            @pl.when(pl.program_id(1) == steps - 1)
            def _():
                for s, start in enumerate(windows[0]):
                    o_ref[s] = acc_ref[start:start + windows[1], :]

    if windows:
        count, height = len(windows[0]), windows[1]
        out_spec, out_shape = pl.BlockSpec((count, height, bw), lambda j, k: (0, 0, j)), (count, height, w)
    else:
        out_spec, out_shape = pl.BlockSpec((n, bw), lambda j, k: (0, j)), (n, w)
    return pl.pallas_call(
        body, name=name, grid=(w // bw, steps),
        in_specs=[pl.BlockSpec((tm, n), lambda j, k: (k, 0)), pl.BlockSpec((tm, bw), lambda j, k: (k, j))]
        + [_const(TOKEN)] * len(tokens),
        out_specs=out_spec, out_shape=pltpu.HBM(out_shape, F32),
        scratch_shapes=[pltpu.VMEM((n, bw), F32)] if windows else [],
        compiler_params=_params(48, dimension_semantics=_seq(2)),
    )(*_hbm(a, b), *tokens)


def _gla_bwd(qg, kg, vg, z, do_gla, st_all, token):
    steps = SEQ // BLK + 1
    kw, vw = GLA_HEADS * DK, GLA_HEADS * DV
    pairs = [(c, h) for c in range(GLA_PER_STEP) for h in range(GLA_HEADS)]
    heads = range(GLA_HEADS)

    def body(q_ref, k_ref, v_ref, z_ref, do_ref, st_ref, token_ref, dq_ref, dk_ref, dv_ref, dz_ref, dst):
        @pl.when(pl.program_id(0) == 0)
        def _():
            dst[...] = jnp.zeros_like(dst)

        rmask = _gla_rowmask(steps - 1 - pl.program_id(0))
        zz = z_ref[...]
        b, b_last = _gla_decay(zz, rmask)
        e_b, e_nb, e_kd, e_last = jnp.exp(b), jnp.exp(-b), jnp.exp(b_last - b), jnp.exp(b_last)
        q = q_ref[...] * (rmask * DK ** -0.5)
        k = k_ref[...] * rmask
        v = v_ref[...] * rmask
        qe, ke, kd = q * e_b, k * e_nb, k * e_kd
        d_o = do_ref[...]
        causal = _iota((CH, CH), 0) >= _iota((CH, CH), 1)
        a, da, dqe, dke, dv_intra, carry = {}, {}, {}, {}, {}, {}
        for c, h in pairs:
            rows, ks, vs_ = _gla_slices(c, h)
            a[c, h] = jnp.where(causal, _dot_nt(qe[rows, ks], ke[rows, ks]), 0.0)
            da[c, h] = jnp.where(causal, _dot_nt(d_o[rows, vs_], v[rows, vs_]), 0.0)
            carry[c, h] = _dot_tn(d_o[rows, vs_], qe[rows, ks])
        for c, h in pairs:
            rows, ks, vs_ = _gla_slices(c, h)
            dqe[c, h] = _dot(d_o[rows, vs_], st_ref[0, c][:, ks]) + _dot(da[c, h], ke[rows, ks])
            dke[c, h] = _dot_tn(da[c, h], qe[rows, ks])
            dv_intra[c, h] = _dot_tn(a[c, h], d_o[rows, vs_])
        dstate = dst[...]
        dkd, db_decay = {}, {}
        for c in reversed(range(GLA_PER_STEP)):
            for h in heads:
                rows, ks, vs_ = _gla_slices(c, h)
                dkd[c, h] = _dot(v[rows, vs_], dstate[:, ks])
                dv_ref[rows, vs_] = dv_intra[c, h] + _dot_nt(kd[rows, ks], dstate[:, ks])
            chunk_last = e_last[c * CH:c * CH + 1]
            db_decay[c] = jnp.sum(dstate * st_ref[0, c], axis=0, keepdims=True) * chunk_last
            dstate = dstate * chunk_last + jnp.concatenate([carry[c, h] for h in heads], axis=1)
        dst[...] = dstate
        rows_of = lambda parts: jnp.concatenate(
            [jnp.concatenate([parts[c, h] for h in heads], axis=1) for c in range(GLA_PER_STEP)], axis=0)
        dqe_all, dke_all, dkd_all = rows_of(dqe), rows_of(dke), rows_of(dkd)
        dq_ref[...] = dqe_all * e_b * (rmask * DK ** -0.5)
        dk_ref[...] = (dke_all * e_nb + dkd_all * e_kd) * rmask
        dkd_kd = dkd_all * kd
        db = dqe_all * qe - dke_all * ke - dkd_kd
        _, upper, same = _gla_chunk_masks()
        decay_rows = jnp.concatenate([jnp.broadcast_to(db_decay[c], (CH, kw)) for c in range(GLA_PER_STEP)], axis=0)
        dlog_g = _dot_exact(upper.astype(F32), db) + _dot_exact(same.astype(F32), dkd_kd) + decay_rows
        dz_ref[...] = dlog_g * (rmask / GLA_TAU) * _sigmoid(-zz)

    blk = lambda w: pl.BlockSpec((BLK, w), lambda s: (_gla_block(steps - 1 - s), 0))
    return pl.pallas_call(
        body, name="gla_bwd", grid=(steps,),
        in_specs=[blk(kw), blk(kw), blk(vw), blk(kw), blk(vw),
                  pl.BlockSpec((1, GLA_PER_STEP, DV, kw), lambda s: (steps - 1 - s, 0, 0, 0)), _const(TOKEN)],
        out_specs=[blk(kw), blk(kw), blk(vw), blk(kw)],
        out_shape=[pltpu.HBM((_lp(), kw), F32), pltpu.HBM((_lp(), kw), F32),
                   pltpu.HBM((_lp(), vw), F32), pltpu.HBM((_lp(), kw), F32)],
        scratch_shapes=[pltpu.VMEM((DV, kw), F32)],
        compiler_params=_params(16, dimension_semantics=_seq()),
    )(*_hbm(qg, kg, vg, z, do_gla, st_all), token)


def _swa_bwd(sinks, qs, ks, vs, do_s, token):
    nb = SEQ // BLK
    kvw = SWA_KV_HEADS * DH
    scale = DH ** -0.5
    heads = range(SWA_HEADS)

    def body(sink_ref, q_ref, km_ref, kp_ref, kc_ref, vm_ref, vp_ref, vc_ref, do_ref, token_ref,
             dq_ref, dk_ref, dv_ref, dsink_ref, carry_k, carry_v, meta_k, meta_v):
        n = pl.program_id(0)

        @pl.when(n == 0)
        def _():
            for r in (carry_k, carry_v, meta_k, meta_v):
                r[...] = jnp.zeros_like(r)
            dsink_ref[...] = jnp.zeros_like(dsink_ref)

        @pl.when(n <= nb)
        def _():
            negdist, maskbias = _swa_bias(n)
            lane = _iota((1, LANE), 1)
            k_all = jnp.concatenate([km_ref[...], kp_ref[...], kc_ref[...]], axis=0).astype(BF16)
            v_all = jnp.concatenate([vm_ref[...], vp_ref[...], vc_ref[...]], axis=0).astype(BF16)
            q = [_swa_half(q_ref, pos, scale) for pos in heads]
            d_o = [_swa_half(do_ref, pos) for pos in heads]
            t = [_dot_nt(q[pos], k_all) + (2.0 ** -(HEAD_POS[pos] + 1) * negdist + maskbias) for pos in heads]
            dp = [_dot_nt(d_o[pos], v_all) for pos in heads]
            soft = [_swa_softmax(t[pos], sink_ref[HEAD_POS[pos]]) for pos in heads]
            p = [s[0] for s in soft]
            delta = [jnp.sum(p[pos] * dp[pos], axis=-1, keepdims=True) for pos in heads]
            ds = [(p[pos] * (dp[pos] - delta[pos])).astype(BF16) for pos in heads]
            dq = [_dot(ds[pos], k_all) for pos in heads]
            for col in range(SWA_HEADS // 2):
                dq_ref[:, col * LANE:(col + 1) * LANE] = scale * _swa_merge(dq[2 * col], dq[2 * col + 1])
            dsink = jnp.zeros((1, LANE), F32)
            for pos in heads:
                dsink = dsink + jnp.where(lane == HEAD_POS[pos],
                                          -jnp.sum(soft[pos][1] * delta[pos], axis=0, keepdims=True), 0.0)
            dsink_ref[...] += dsink
            dk3 = _dot_tn(jnp.concatenate(q, axis=0), jnp.concatenate(ds, axis=0)).T
            dv3 = _dot_tn(jnp.concatenate(d_o, axis=0), jnp.concatenate([x.astype(BF16) for x in p], axis=0)).T
            meta_k[...] += dk3[0:BLK]
            meta_v[...] += dv3[0:BLK]
            dk_ref[...] = carry_k[...] + dk3[BLK:2 * BLK]
            dv_ref[...] = carry_v[...] + dv3[BLK:2 * BLK]
            carry_k[...] = dk3[2 * BLK:3 * BLK]
            carry_v[...] = dv3[2 * BLK:3 * BLK]

        @pl.when(n == nb + 1)
        def _():
            dk_ref[...] = meta_k[...]
            dv_ref[...] = meta_v[...]

    kv_out = pl.BlockSpec((BLK, kvw), lambda n: (jnp.where(n == nb + 1, nb, jnp.clip(n - 1, 0, nb - 1)), 0))
    qblk = pl.BlockSpec((BLK, SWA_HEADS * DH), lambda n: (jnp.minimum(n, nb), 0))
    return pl.pallas_call(
        body, name="swa_bwd", grid=(nb + 2,),
        in_specs=[pl.BlockSpec(memory_space=pltpu.SMEM), qblk] + _swa_kv_specs(kvw) + _swa_kv_specs(kvw)
        + [qblk, _const(TOKEN)],
        out_specs=[qblk, kv_out, kv_out, _acc((1, LANE))],
        out_shape=[pltpu.HBM((_lp(), SWA_HEADS * DH), F32), pltpu.HBM((_lp(), kvw), F32),
                   pltpu.HBM((_lp(), kvw), F32), pltpu.HBM((1, LANE), F32)],
        scratch_shapes=[pltpu.VMEM((BLK, kvw), F32)] * 4,
        compiler_params=_params(16, dimension_semantics=_seq()),
    )(sinks, *_hbm(qs, ks, ks, ks, vs, vs, vs, do_s), token)


def _in_bwd(dqs, dks, dvs, dqg, dkg, dvg, drg, dz, dpre1, w_in_t, wg2_p):
    tm = _row_tile(384)
    lp = _lp()
    widths = (512, 128, 128, 256, 256, 512, 512)
    offs = (O_QS, O_KS, O_VS, O_QG, O_KG, O_VG, O_RG)

    def body(*refs):
        parts, (dz_ref, dp1_ref, w_ref, wg2_ref, dproj_ref, dh0_ref, dbin_ref, dbg_ref) = refs[:7], refs[7:]

        @pl.when(pl.program_id(0) == 0)
        def _():
            dbin_ref[...] = jnp.zeros_like(dbin_ref)
            dbg_ref[...] = jnp.zeros_like(dbg_ref)

        for pos, h in enumerate(HEAD_POS):
            val = parts[0][:, pos * DH:(pos + 1) * DH]
            dproj_ref[:, O_QS + h * DH:O_QS + (h + 1) * DH] = val.astype(BF16)
            dbin_ref[:, O_QS + h * DH:O_QS + (h + 1) * DH] += jnp.sum(val, axis=0, keepdims=True)
        for p_ref, off, wd in zip(parts[1:], offs[1:], widths[1:]):
            val = p_ref[...]
            dproj_ref[:, off:off + wd] = val.astype(BF16)
            dbin_ref[:, off:off + wd] += jnp.sum(val, axis=0, keepdims=True)
        dz = dz_ref[...]
        dlr = _dot_nt(dz, wg2_ref[...])
        dproj_ref[:, O_LR:O_LR + LANE] = dlr.astype(BF16)
        dbin_ref[:, O_LR:O_LR + LANE] += jnp.sum(dlr, axis=0, keepdims=True)
        dbg_ref[...] += jnp.sum(dz, axis=0, keepdims=True)
        dh0_ref[...] = ALPHA * dp1_ref[...] + _dot(dproj_ref[...], w_ref[...])

    return pl.pallas_call(
        body, name="in_bwd", grid=(lp // tm,),
        in_specs=[_rows(tm, w) for w in widths] + [_rows(tm, 256), _rows(tm, D), _const((D_IN_P, D)), _const((LANE, 256))],
        out_specs=[_rows(tm, D_IN_P), _rows(tm, D), _acc((1, D_IN_P)), _acc((1, 256))],
        out_shape=[pltpu.HBM((lp, D_IN_P), BF16), pltpu.HBM((lp, D), F32),
                   pltpu.HBM((1, D_IN_P), F32), pltpu.HBM((1, 256), F32)],
        compiler_params=_params(40, dimension_semantics=_seq()),
    )(*_hbm(dqs, dks, dvs, dqg, dkg, dvg, drg, dz, dpre1, w_in_t, wg2_p))


def _ln_in_bwd(x, meta_ext, dh0, g, token):
    tr = min(LN_ROWS, SEQ)

    def ln_bwd(x_ref, dh_ref, g_ref, dx_ref, dg_ref, db_ref):
        @pl.when(pl.program_id(0) == 0)
        def _():
            dg_ref[...] = jnp.zeros_like(dg_ref)
            db_ref[...] = jnp.zeros_like(db_ref)

        xhat, rstd = _ln_stats(x_ref[...])
        dh = dh_ref[...]
        dx_ref[...] = _ln_bwd(dh, xhat, rstd, g_ref[...])
        dg_ref[...] += jnp.sum(dh * xhat, axis=0, keepdims=True)
        db_ref[...] += jnp.sum(dh, axis=0, keepdims=True)

    def body(x_ref, dh_ref, g_ref, token_ref, dx_ref, dg_ref, db_ref):
        ln_bwd(x_ref, dh_ref, g_ref, dx_ref, dg_ref, db_ref)

    def meta_body(m_ref, dh_ref, g_ref, dm_ref, dg_ref, db_ref):
        ln_bwd(m_ref, dh_ref, g_ref, dm_ref, dg_ref, db_ref)

    sums = [pltpu.HBM((1, D), F32), pltpu.HBM((1, D), F32)]
    dx, dg, db = pl.pallas_call(
        body, name="ln_in_bwd", grid=(SEQ // tr,),
        in_specs=[_rows(tr, D), _rows(tr, D), _const((1, D)), _const(TOKEN)],
        out_specs=[_rows(tr, D), _acc((1, D)), _acc((1, D))],
        out_shape=[pltpu.HBM((SEQ, D), F32)] + sums,
        compiler_params=_params(32, dimension_semantics=_seq()),
    )(*_hbm(x, dh0, g), token)
    dm, dg_m, db_m = pl.pallas_call(
        meta_body, name="ln_in_bwd_meta", grid=(1,),
        in_specs=[_const((BLK, D)), pl.BlockSpec((BLK, D), lambda i: (SEQ // BLK, 0)), _const((1, D))],
        out_specs=[_acc((BLK, D)), _acc((1, D)), _acc((1, D))],
        out_shape=[pltpu.HBM((BLK, D), F32)] + sums,
        compiler_params=_params(16, dimension_semantics=_seq()),
    )(*_hbm(meta_ext, dh0, g))
    return dx, dm, dg + dg_m, db + db_m


def _local_step(x, target, ln_in_g, ln_in_b, b_in, bg2, sinks, gn, g1, b1, g2, b2,
                token, fetch_first, fetch_rest, fetch_ffn, exchange_ffn, ship_ffn, ship_w_in):
    row = lambda v: v.reshape(1, -1).astype(F32)
    b_in_p = jnp.pad(row(b_in), ((0, 0), (0, D_IN_P - D_IN)))
    gn4 = row(gn)
    sinks = sinks.reshape(-1).astype(F32)

    h_real = _ln_in_fwd_real(x, row(ln_in_g), row(ln_in_b), token)
    w_in_windows, meta_full, wg2 = fetch_first([h_real])
    meta_ext = jnp.pad(meta_full, ((META_OFF, BLK - CH), (0, 0)))
    wg2_p = jnp.pad(wg2, ((0, LANE - wg2.shape[0]), (0, 0))).astype(BF16)
    h0 = _ln_in_fwd_meta(h_real, meta_ext, row(ln_in_g), row(ln_in_b))
    qs, ks, vs, qg, kg, vg, rg, glr, z, w_in_t = _in_proj(h0, w_in_windows, b_in_p, wg2_p, row(bg2))
    o_s = _swa_fwd(sinks, qs, ks, vs)
    o_gla, st_all = _gla_fwd(qg, kg, vg, z)
    w_out, token = fetch_rest([o_s, o_gla])
    o, pre1, h1 = _post_mix(o_s, o_gla, rg, h0, gn4, w_out, row(g1), row(b1), token)
    wg_t, wu_t, wd = fetch_ffn([pre1])
    a, dgate, dup, dpre2, loss, dg2, db2 = _ffn_fwd_loss_bwd(h1, wg_t, wu_t, wd, target, row(g2), row(b2))
    dpre1, dg1, db1, do_s, do_gla, drg, dgn = _ffn_out_bwd(dpre2, dgate, dup, pre1, wg_t, wu_t, row(g1), w_out, o_gla,
                                                           rg, gn4)
    dwd = _atb(a, dpre2, "dw_down")
    dwg_t = _atb(dgate, h1, "dw_gate")
    dwu_t = _atb(dup, h1, "dw_up")
    token = exchange_ffn(dict(w_out=_atb(o, dpre1, "dw_out"), w_g=dwg_t, w_u=dwu_t, w_d=dwd))
    dqg, dkg, dvg, dz = _gla_bwd(qg, kg, vg, z, do_gla, st_all, token)
    token = ship_ffn([dqg])
    dqs, dks, dvs, dsinks = _swa_bwd(sinks, qs, ks, vs, do_s, token)
    dproj, dh0, db_in_p, dbg2 = _in_bwd(dqs, dks, dvs, dqg, dkg, dvg, drg, dz, dpre1, w_in_t, wg2_p)
    token = ship_w_in(_atb(dproj, h0, "dw_in", windows=(W_IN_STARTS, W_IN_WIN)))
    dwg2_p = _atb(glr, dz, "dw_gate_lr2")
    dx, dmeta_blk, dg_in, db_in_ln = _ln_in_bwd(x, meta_ext, dh0, row(ln_in_g), token)

    small = dict(meta_blk=dmeta_blk, ln_in_g=dg_in, ln_in_b=db_in_ln, ln1_g=dg1, ln1_b=db1, ln2_g=dg2, ln2_b=db2,
                 b_in_p=db_in_p, wg2_p=dwg2_p, bg2=dbg2, sinks=dsinks, gn=dgn, loss=loss)
    return dx, small


HBM = pl.BlockSpec(memory_space=pltpu.HBM)


def _place():
    return lax.axis_index("x"), lax.axis_index("y"), lax.axis_index("c")


def _other_chips(x, y):
    return [(1 - x, y), (x, 1 - y), (1 - x, 1 - y)]


def _dma_sems(n):
    return pltpu.SemaphoreType.DMA((n,))


def _comm_params():
    return pltpu.CompilerParams(has_side_effects=True)


SEM = pl.BlockSpec(memory_space=pltpu.SEMAPHORE)


PER_ARRAY = dict(gather=3, scatter=3, sibling=N_CHIPS)


def _ici_copies(kind, landing, srcs, lands, send_sems, recv_sems):
    x, y, c = _place()
    mine = 2 * x + y
    copies = []
    for a in range(len(srcs)):
        if kind == "sibling":
            for s in range(N_CHIPS):
                copies.append(pltpu.make_async_remote_copy(
                    srcs[a].at[s, 1 - c], lands[a].at[s], send_sems.at[N_CHIPS * a + s], recv_sems.at[N_CHIPS * a + s],
                    device_id=(x, y, 1 - c), device_id_type=MESH))
            continue
        for j, (px, py) in enumerate(_other_chips(x, y)):
            slab = 2 * px + py if landing else mine
            if kind == "gather":
                src, dst = srcs[a].at[c], lands[a].at[slab, c]
            else:
                src, dst = srcs[a].at[2 * px + py], lands[a].at[slab]
            copies.append(pltpu.make_async_remote_copy(src, dst, send_sems.at[3 * a + j], recv_sems.at[3 * a + j],
                                                       device_id=(px, py, c), device_id_type=MESH))
    return copies


def _split_params():
    return pltpu.CompilerParams(has_side_effects=pltpu.SideEffectType.DATAFLOW_SIDE_EFFECTING)


def _ici_start(kind, srcs, land_shapes, after, name):
    n = len(srcs)
    lands = [pltpu.with_memory_space_constraint(lax.empty(s, a.dtype), pltpu.HBM) for s, a in zip(land_shapes, srcs)]

    def body(*refs):
        outs = refs[2 * n + len(after):]
        for cp in _ici_copies(kind, False, refs[:n], refs[n:2 * n], outs[0], outs[1]):
            cp.start()
        outs[-1][...] = jnp.zeros(TOKEN, F32)

    outs = pl.pallas_call(
        body, name=name, in_specs=[HBM] * (2 * n) + [pl.BlockSpec(memory_space=pl.ANY)] * len(after),
        out_specs=[SEM, SEM] + [HBM] * (2 * n) + [pl.BlockSpec(memory_space=pltpu.VMEM)],
        out_shape=[_dma_sems(PER_ARRAY[kind] * n)] * 2 + [pltpu.HBM(a.shape, a.dtype) for a in list(srcs) + lands]
        + [jax.ShapeDtypeStruct(TOKEN, F32)],
        input_output_aliases={i: 2 + i for i in range(2 * n)},
        compiler_params=_split_params(),
    )(*_hbm(*srcs), *lands, *after)
    return outs[:-1], outs[-1]


def _ici_wait(kind, handle, after, name):
    n = (len(handle) - 2) // 2

    def body(*refs):
        for cp in _ici_copies(kind, True, refs[:n], refs[n:2 * n], refs[2 * n], refs[2 * n + 1]):
            cp.wait_send()
            cp.wait_recv()

    outs = pl.pallas_call(
        body, name=name, in_specs=[HBM] * (2 * n) + [SEM, SEM] + [pl.BlockSpec(memory_space=pl.ANY)] * len(after),
        out_specs=[HBM] * (2 * n), out_shape=[pltpu.HBM(a.shape, a.dtype) for a in handle[2:]],
        input_output_aliases={i: i for i in range(2 * n)},
        compiler_params=_split_params(),
    )(*handle[2:], handle[0], handle[1], *after)
    return list(outs[:n]), list(outs[n:])


def _forward_copies(landing, arrs, send_sems, recv_sems):
    x, y, c = _place()
    copies = []
    for a in range(len(arrs)):
        for j, (px, py) in enumerate(_other_chips(x, y)):
            half = 1 - c if landing else c
            copies.append(pltpu.make_async_remote_copy(
                arrs[a].at[2 * px + py, c], arrs[a].at[2 * px + py, half], send_sems.at[3 * a + j],
                recv_sems.at[3 * a + j], device_id=(x, y, 1 - c), device_id_type=MESH))
    return copies


def _gather_wait_forward(handle, groups, after, name):
    n = (len(handle) - 2) // 2
    assert sum(groups) == n

    def body(*refs):
        outs = refs[2 * n + 2 + len(after):]
        lands, sems = outs[n:2 * n], outs[2 * n:-1]
        arrivals = _ici_copies("gather", True, refs[:n], refs[n:2 * n], refs[2 * n], refs[2 * n + 1])
        sends, first = [], 0
        for g, count in enumerate(groups):
            sends += _forward_copies(False, lands[first:first + count], sems[2 * g], sems[2 * g + 1])
            first += count
        for cp, send in zip(arrivals, sends):
            cp.wait_recv()
            send.start()
        for cp in arrivals:
            cp.wait_send()
        outs[-1][...] = jnp.zeros(TOKEN, F32)

    outs = pl.pallas_call(
        body, name=name, in_specs=[HBM] * (2 * n) + [SEM, SEM] + [pl.BlockSpec(memory_space=pl.ANY)] * len(after),
        out_specs=[HBM] * (2 * n) + [SEM] * (2 * len(groups)) + [pl.BlockSpec(memory_space=pltpu.VMEM)],
        out_shape=[pltpu.HBM(a.shape, a.dtype) for a in handle[2:]]
        + [_dma_sems(3 * count) for count in groups for _ in range(2)] + [jax.ShapeDtypeStruct(TOKEN, F32)],
        input_output_aliases={i: i for i in range(2 * n)},
        compiler_params=_split_params(),
    )(*handle[2:], handle[0], handle[1], *after)
    lands, sems, handles, first = outs[n:2 * n], outs[2 * n:-1], [], 0
    for g, count in enumerate(groups):
        handles.append([sems[2 * g], sems[2 * g + 1], *lands[first:first + count]])
        first += count
    return list(outs[:n]), handles, outs[-1]


def _forward_wait(handle, after, name):
    n = len(handle) - 2

    def body(*refs):
        for cp in _forward_copies(True, refs[:n], refs[n], refs[n + 1]):
            cp.wait_send()
            cp.wait_recv()

    return list(pl.pallas_call(
        body, name=name, in_specs=[HBM] * n + [SEM, SEM] + [pl.BlockSpec(memory_space=pl.ANY)] * len(after),
        out_specs=[HBM] * n, out_shape=[pltpu.HBM(a.shape, a.dtype) for a in handle[2:]],
        input_output_aliases={i: i for i in range(n)},
        compiler_params=_split_params(),
    )(*handle[2:], handle[0], handle[1], *after))


def _sibling_exchange(grads, name):
    n = len(grads)

    def body(*refs):
        ins, outs = refs[:n], refs[n:2 * n]
        send_sems, recv_sems = refs[2 * n:]
        x, y, c = _place()
        copies = []
        for a in range(n):
            for s in range(N_CHIPS):
                cp = pltpu.make_async_remote_copy(ins[a].at[s, 1 - c], outs[a].at[s], send_sems.at[N_CHIPS * a + s],
                                                  recv_sems.at[N_CHIPS * a + s], device_id=(x, y, 1 - c),
                                                  device_id_type=MESH)
                cp.start()
                copies.append(cp)
        for cp in copies:
            cp.wait_recv()
        for cp in copies:
            cp.wait_send()

    return pl.pallas_call(
        body, name=name, in_specs=[HBM] * n, out_specs=[HBM] * n,
        out_shape=[pltpu.HBM((N_CHIPS, g.shape[2], D), F32) for g in grads],
        scratch_shapes=[_dma_sems(N_CHIPS * n)] * 2,
        compiler_params=_comm_params(),
    )(*_hbm(*grads))


def _add_halves(core, grads, recvs, dtypes, name):
    n = len(grads)
    heights = [g.shape[2] for g in grads]

    def body(c_ref, *refs):
        for a in range(n):
            refs[2 * n + a][...] = (refs[2 * a][0] + refs[2 * a + 1][...]).astype(dtypes[a])

    slab = lambda h: pl.BlockSpec((1, h, D), lambda s, c: (s, 0, 0))
    mine = lambda h: pl.BlockSpec((1, 1, h, D), lambda s, c: (s, c[0], 0, 0))
    return pl.pallas_call(
        body, name=name,
        grid_spec=pltpu.PrefetchScalarGridSpec(
            num_scalar_prefetch=1, grid=(N_CHIPS,),
            in_specs=[spec(h) for h in heights for spec in (mine, slab)], out_specs=[slab(h) for h in heights]),
        out_shape=[pltpu.HBM((N_CHIPS, h, D), dt) for h, dt in zip(heights, dtypes)],
        compiler_params=_params(32, dimension_semantics=_seq()),
    )(core, *_hbm(*[a for pair in zip(grads, recvs) for a in pair]))


N_DEVICES = 2 * N_CHIPS
PEER_FLIPS = [(dx, dy, dc) for dx in (0, 1) for dy in (0, 1) for dc in (0, 1)][1:]


def _small_copies(landing, p_ref, out_ref, send_sems, recv_sems):
    x, y, c = _place()
    flip = lambda v, d: 1 - v if d else v
    copies = []
    for k, flips in enumerate(PEER_FLIPS):
        px, py, pc = (flip(v, d) for v, d in zip((x, y, c), flips))
        slab = 4 * px + 2 * py + pc if landing else 4 * x + 2 * y + c
        copies.append(pltpu.make_async_remote_copy(p_ref, out_ref.at[slab], send_sems.at[k], recv_sems.at[k],
                                                   device_id=(px, py, pc), device_id_type=MESH))
    return copies


def _small_start(pack, after):
    n = len(PEER_FLIPS)
    land = pltpu.with_memory_space_constraint(lax.empty((N_DEVICES,) + pack.shape, F32), pltpu.HBM)

    def body(p_ref, land_ref, *refs):
        outs = refs[len(after):]
        for cp in _small_copies(False, p_ref, land_ref, outs[0], outs[1]):
            cp.start()
        outs[-1][...] = jnp.zeros(TOKEN, F32)

    outs = pl.pallas_call(
        body, name="small_exchange_start", in_specs=[HBM, HBM] + [pl.BlockSpec(memory_space=pl.ANY)] * len(after),
        out_specs=[SEM, SEM, HBM, HBM, pl.BlockSpec(memory_space=pltpu.VMEM)],
        out_shape=[_dma_sems(n), _dma_sems(n), pltpu.HBM(pack.shape, F32), pltpu.HBM(land.shape, F32),
                   jax.ShapeDtypeStruct(TOKEN, F32)],
        input_output_aliases={0: 2, 1: 3},
        compiler_params=_split_params(),
    )(*_hbm(pack), land, *after)
    return outs[:-1], outs[-1]


def _small_wait(handle, after):
    def body(p_ref, land_ref, send_sems, recv_sems, *rest):
        for cp in _small_copies(True, p_ref, land_ref, send_sems, recv_sems):
            cp.wait_send()
            cp.wait_recv()

    return pl.pallas_call(
        body, name="small_exchange_wait", in_specs=[HBM, HBM, SEM, SEM] + [pl.BlockSpec(memory_space=pl.ANY)] * len(after),
        out_specs=[HBM, HBM], out_shape=[pltpu.HBM(a.shape, F32) for a in handle[2:]],
        input_output_aliases={0: 0, 1: 1},
        compiler_params=_split_params(),
    )(handle[2], handle[3], handle[0], handle[1], *after)


def _sum_chips(slots, firsts, rests, after, name):
    n = len(firsts)

    def body(i_ref, *refs):
        outs = refs[4 * n + len(after):]
        for a in range(n):
            first, r1, r2, r3 = refs[4 * a:4 * a + 4]
            outs[a][...] = ((first[...].astype(F32) + r1[...].astype(F32)) + r2[...].astype(F32)) + r3[...].astype(F32)

    slab = lambda h, k: pl.BlockSpec((1, h, D), lambda i, ix: (ix[k], 0, 0))
    heights = [f.shape[1] for f in firsts]
    return pl.pallas_call(
        body, name=name,
        grid_spec=pltpu.PrefetchScalarGridSpec(
            num_scalar_prefetch=1, grid=(1,),
            in_specs=[slab(h, k) for h in heights for k in range(4)] + [pl.BlockSpec(memory_space=pl.ANY)] * len(after),
            out_specs=[slab(h, 4) for h in heights]),
        out_shape=[pltpu.HBM((2, h, D), F32) for h in heights],
        compiler_params=_params(48, dimension_semantics=_seq()),
    )(slots, *_hbm(*[a for f, r in zip(firsts, rests) for a in (f, r, r, r)]), *after)


def _join_copies(landing, arrs, send_sems, recv_sems):
    x, y, c = _place()
    slab = 1 - c if landing else c
    return [pltpu.make_async_remote_copy(arr.at[slab], arr.at[slab], send_sems.at[a], recv_sems.at[a],
                                         device_id=(x, y, 1 - c), device_id_type=MESH) for a, arr in enumerate(arrs)]


def _join_halves(halves, name):
    n = len(halves)

    def body(*refs):
        outs = refs[n:2 * n]
        send_sems, recv_sems = refs[2 * n:]
        sends = _join_copies(False, outs, send_sems, recv_sems)
        for cp in sends:
            cp.start()
        for cp in _join_copies(True, outs, send_sems, recv_sems):
            cp.wait_recv()
        for cp in sends:
            cp.wait_send()

    return list(pl.pallas_call(
        body, name=name, in_specs=[HBM] * n, out_specs=[HBM] * n,
        out_shape=[pltpu.HBM(h.shape, F32) for h in halves],
        input_output_aliases={a: a for a in range(n)},
        scratch_shapes=[_dma_sems(n)] * 2,
        compiler_params=_comm_params(),
    )(*_hbm(*halves)))


def _join_start(halves, name):
    n = len(halves)

    def body(*refs):
        outs = refs[n:]
        for cp in _join_copies(False, refs[:n], outs[0], outs[1]):
            cp.start()
        outs[-1][...] = jnp.zeros(TOKEN, F32)

    outs = pl.pallas_call(
        body, name=name, in_specs=[HBM] * n,
        out_specs=[SEM, SEM] + [HBM] * n + [pl.BlockSpec(memory_space=pltpu.VMEM)],
        out_shape=[_dma_sems(n)] * 2 + [pltpu.HBM(a.shape, a.dtype) for a in halves] + [jax.ShapeDtypeStruct(TOKEN, F32)],
        input_output_aliases={i: 2 + i for i in range(n)},
        compiler_params=_split_params(),
    )(*_hbm(*halves))
    return outs[:-1], outs[-1]


def _join_wait(handle, after, name):
    n = len(handle) - 2

    def body(*refs):
        for cp in _join_copies(True, refs[:n], refs[n], refs[n + 1]):
            cp.wait_send()
            cp.wait_recv()

    return list(pl.pallas_call(
        body, name=name, in_specs=[HBM] * n + [SEM, SEM] + [pl.BlockSpec(memory_space=pl.ANY)] * len(after),
        out_specs=[HBM] * n, out_shape=[pltpu.HBM(a.shape, a.dtype) for a in handle[2:]],
        input_output_aliases={i: i for i in range(n)},
        compiler_params=_split_params(),
    )(*handle[2:], handle[0], handle[1], *after))


def _chip_partials(grads, wire_dtypes, names, fetched=()):
    core = lax.axis_index("c").astype(jnp.int32).reshape(1)
    todo = len(grads) - len(fetched)
    recv = (list(_sibling_exchange(grads[:todo], "sibling_exchange_" + names[0])) if todo else []) + list(fetched)
    return list(_add_halves(core, grads, recv, wire_dtypes, "add_halves_" + names[0]))


def _chip_sums(parts, got, after, name):
    x, y, c = _place()
    others = [2 * px + py for px, py in _other_chips(x, y)]
    own_first = jnp.stack([2 * x + y] + others + [c]).astype(jnp.int32)
    return list(_sum_chips(own_first, parts, got, after, name))


ADAMW_STEPS = 8


def _adamw(params, by_row, chip, window_step):
    n = len(params)
    rows, _, cols = by_row[0].shape
    block = lambda shape: pl.BlockSpec((shape[0] // ADAMW_STEPS, shape[1]), lambda i, c: (i, 0))
    assert all(a.shape[0] % (8 * ADAMW_STEPS) == 0 for p in params for a in p)

    def body(c_ref, *refs):
        w_hbm, g_ref, m_hbm, v_hbm = refs[4 * n:4 * n + 4]
        results, (ins_ref, outs_ref, sems) = refs[8 * n + 4:8 * n + 8], refs[8 * n + 8:]
        loads = [pltpu.make_async_copy(src.at[:, 0, :], ins_ref.at[k], sems.at[k])
                 for k, src in enumerate((w_hbm, m_hbm, v_hbm))]
        stores = [pltpu.make_async_copy(outs_ref.at[k], dst.at[:, 0, :], sems.at[3 + k]) for k, dst in enumerate(results)]
        first = pl.program_id(0) == 0

        @pl.when(first)
        def _():
            for cp in loads:
                cp.start()

        for a in range(n):
            w_ref, a_g_ref, m_ref, v_ref = refs[4 * a:4 * a + 4]
            outs = refs[4 * n + 4 + 4 * a:4 * n + 8 + 4 * a]
            g = a_g_ref[...]
            outs[0][...] = g
            outs[1][...], outs[2][...], outs[3][...] = _adamw_math(w_ref[...], g, m_ref[...], v_ref[...])

        @pl.when(first)
        def _():
            for cp in loads:
                cp.wait()
            for lo in range(0, cols, LANE):
                lanes = slice(lo, lo + LANE)
                g = g_ref[0:rows, lanes]
                for s in range(1, N_CHIPS):
                    g = jnp.where(c_ref[0] == s, g_ref[s * window_step:s * window_step + rows, lanes], g)
                outs_ref[0, :, lanes] = g
                outs_ref[1, :, lanes], outs_ref[2, :, lanes], outs_ref[3, :, lanes] = _adamw_math(
                    ins_ref[0, :, lanes], g, ins_ref[1, :, lanes], ins_ref[2, :, lanes])
            for cp in stores:
                cp.start()

        @pl.when(pl.program_id(0) == ADAMW_STEPS - 1)
        def _():
            for cp in stores:
                cp.wait()

    outs = pl.pallas_call(
        body, name="adamw_matrices",
        grid_spec=pltpu.PrefetchScalarGridSpec(
            num_scalar_prefetch=1, grid=(ADAMW_STEPS,),
            in_specs=[block(a.shape) for p in params for a in p] + [HBM, _const(by_row[1].shape), HBM, HBM],
            out_specs=[block(p[0].shape) for p in params for _ in range(4)] + [HBM] * 4,
            scratch_shapes=[pltpu.VMEM((3, rows, cols), F32), pltpu.VMEM((4, rows, cols), F32), _dma_sems(7)]),
        out_shape=[pltpu.HBM(p[0].shape, F32) for p in params for _ in range(4)] + [pltpu.HBM((rows, 1, cols), F32)] * 4,
        compiler_params=_params(48, dimension_semantics=_seq()),
    )(chip, *_hbm(*[a for p in params for a in p], *by_row))
    return [outs[4 * a:4 * a + 4] for a in range(n)], outs[4 * n:]


def _adamw_math(w, g, m, v):
    nm = ADAM_B1 * m + (1.0 - ADAM_B1) * g
    nv = ADAM_B2 * v + (1.0 - ADAM_B2) * (g * g)
    m_hat = nm / (1.0 - ADAM_B1 ** ADAM_STEP)
    v_hat = nv / (1.0 - ADAM_B2 ** ADAM_STEP)
    return -ADAM_LR * (m_hat / (jnp.sqrt(v_hat) + ADAM_EPS) + ADAM_WD * w), nm, nv


SMALL = (("meta_tokens", (N_META, D // N_CHIPS)), ("ln_in_g", (1, D)), ("ln_in_b", (1, D)), ("b_in", (1, D_IN)),
         ("w_gate_lr2", (GATE_RANK, GLA_HEADS * DK // N_CHIPS)), ("b_gate_lr2", (1, GLA_HEADS * DK)),
         ("attn_sinks", (1, SWA_HEADS)),
         ("gla_norm_g", (1, DV)), ("ln1_g", (1, D)), ("ln1_b", (1, D)), ("ln2_g", (1, D)), ("ln2_b", (1, D)))
ROW_META, ROW_B_IN, ROW_TAIL, ROW_WG2 = 0, 22, 25, 32
ROW_LN = dict(ln_in_g=16, ln_in_b=17, ln1_g=18, ln1_b=19, ln2_g=20, ln2_b=21)
TAIL_BG2, TAIL_SINKS, TAIL_GN, TAIL_LOSS = 0, 256, 256 + SWA_HEADS, 256 + SWA_HEADS + DV


def _adamw_small(place, packs, own, params):
    n = len(SMALL)

    def body(place_ref, packs_ref, own_ref, *refs):
        ins, outs, p_ref = refs[:3 * n], refs[3 * n:-1], refs[-1]
        me, c = place_ref[0], place_ref[1]
        total = jnp.where(me == 0, own_ref[...], packs_ref[0])
        for i in range(1, N_DEVICES):
            total = total + jnp.where(me == i, own_ref[...], packs_ref[i])
        p_ref[...] = total
        outs[4 * n][...] = total[ROW_TAIL:ROW_TAIL + 1, :]

        def mine(width, rows):
            part = lambda s: p_ref[rows, s * width:(s + 1) * width]
            return jnp.where(c == 0, part(0), jnp.where(c == 1, part(1), jnp.where(c == 2, part(2), part(3))))

        tail = lambda lo, width: p_ref[ROW_TAIL:ROW_TAIL + 1, lo:lo + width]
        grads = dict(
            meta_tokens=mine(D // N_CHIPS, slice(ROW_META, ROW_META + N_META)),
            b_in=jnp.concatenate([p_ref[ROW_B_IN:ROW_B_IN + 1, :], p_ref[ROW_B_IN + 1:ROW_B_IN + 2, :],
                                  p_ref[ROW_B_IN + 2:ROW_B_IN + 3, 0:D_IN - 2 * D]], axis=1),
            w_gate_lr2=mine(256 // N_CHIPS, slice(ROW_WG2, ROW_WG2 + 16)),
            b_gate_lr2=tail(TAIL_BG2, 256), attn_sinks=tail(TAIL_SINKS, SWA_HEADS), gla_norm_g=tail(TAIL_GN, DV),
            **{k: p_ref[r:r + 1, :] for k, r in ROW_LN.items()})
        for i, (name, _) in enumerate(SMALL):
            g = grads[name]
            outs[4 * i][...] = g
            outs[4 * i + 1][...], outs[4 * i + 2][...], outs[4 * i + 3][...] = _adamw_math(
                ins[3 * i][...], g, ins[3 * i + 1][...], ins[3 * i + 2][...])

    whole = lambda shape: pl.BlockSpec(shape, lambda i, c: (0,) * len(shape))
    outs = pl.pallas_call(
        body, name="adamw_small",
        grid_spec=pltpu.PrefetchScalarGridSpec(
            num_scalar_prefetch=1, grid=(1,),
            in_specs=[whole(packs.shape), whole(own.shape)] + [whole(s) for _, s in SMALL for _ in range(3)],
            out_specs=[whole(s) for _, s in SMALL for _ in range(4)] + [whole((1, D))],
            scratch_shapes=[pltpu.VMEM(own.shape, F32)]),
        out_shape=[pltpu.HBM(s, F32) for _, s in SMALL for _ in range(4)] + [pltpu.HBM((1, D), F32)],
        compiler_params=_params(16, dimension_semantics=_seq()),
    )(place, *_hbm(packs, own, *[a for p in params for a in p]))
    return [outs[4 * i:4 * i + 4] for i in range(n)], outs[4 * n]


def _small_pack(gr):
    names = ["meta_blk"] + list(ROW_LN) + ["b_in_p", "wg2_p", "bg2", "sinks", "gn", "loss"]
    gate_w = GLA_HEADS * DK

    def body(*refs):
        src, out = dict(zip(names, refs)), refs[-1]
        out[...] = jnp.zeros_like(out)
        out[ROW_META:ROW_META + N_META, :] = src["meta_blk"][META_OFF:CH, :]
        for k, r in ROW_LN.items():
            out[r:r + 1, :] = src[k][...]
        for j in range(-(-D_IN // D)):
            width = min(D, D_IN - j * D)
            out[ROW_B_IN + j:ROW_B_IN + j + 1, 0:width] = src["b_in_p"][:, j * D:j * D + width]
        tail = slice(ROW_TAIL, ROW_TAIL + 1)
        out[tail, TAIL_BG2:TAIL_BG2 + gate_w] = src["bg2"][...]
        out[tail, TAIL_SINKS:TAIL_SINKS + SWA_HEADS] = src["sinks"][:, 0:SWA_HEADS]
        out[tail, TAIL_GN:TAIL_GN + DV] = src["gn"][...]
        out[tail, TAIL_LOSS:TAIL_LOSS + 1] = src["loss"][:, 0:1]
        out[ROW_WG2:ROW_WG2 + GATE_RANK, 0:gate_w] = src["wg2_p"][0:GATE_RANK, :]

    arrays = [gr[k] for k in names]
    return pl.pallas_call(
        body, name="small_pack", grid=(1,),
        in_specs=[_acc(a.shape) for a in arrays], out_specs=_acc((SMALL_ROWS, D)),
        out_shape=pltpu.HBM((SMALL_ROWS, D), F32),
        compiler_params=_params(16, dimension_semantics=_seq()),
    )(*_hbm(*arrays))


BIG = ("w_in", "w_out", "w_g", "w_u", "w_d")


def kernel(x, meta_tokens, ln_in_g, ln_in_b, w_in, b_in, w_gate_lr2, b_gate_lr2, attn_sinks, gla_norm_g, w_out, ln1_g, ln1_b, w_ffn_gate, w_ffn_up, w_ffn_down, ln2_g, ln2_b, loss_target, m_meta_tokens, m_ln_in_g, m_ln_in_b, m_w_in, m_b_in, m_w_gate_lr2, m_b_gate_lr2, m_attn_sinks, m_gla_norm_g, m_w_out, m_ln1_g, m_ln1_b, m_w_ffn_gate, m_w_ffn_up, m_w_ffn_down, m_ln2_g, m_ln2_b, v_meta_tokens, v_ln_in_g, v_ln_in_b, v_w_in, v_b_in, v_w_gate_lr2, v_b_gate_lr2, v_attn_sinks, v_gla_norm_g, v_w_out, v_ln1_g, v_ln1_b, v_w_ffn_gate, v_w_ffn_up, v_w_ffn_down, v_ln2_g, v_ln2_b):
    chip = 2 * lax.axis_index("x") + lax.axis_index("y")

    halves = lambda a: a.reshape(2, a.shape[0] // 2, a.shape[1])
    r_in = SHARD_ROWS["w_in"]
    first = [halves(a) for a in (jnp.pad(w_in[0].T.astype(BF16), ((0, W_IN_WIN - r_in), (0, 0))), meta_tokens,
                                 w_gate_lr2[0])]
    rest = [halves(a) for a in (w_out[0].astype(BF16), w_ffn_gate[0].T.astype(BF16), w_ffn_up[0].T.astype(BF16),
                                w_ffn_down[0].astype(BF16))]
    lands = lambda arrs: [(N_CHIPS,) + a.shape for a in arrs]
    first_handle, first_token = _ici_start("gather", first, lands(first), [], "gather_first_start")
    rest_handle, token = _ici_start("gather", rest, lands(rest), [first_token], "gather_rest_start")
    own_slab = lambda got, shards: [lax.dynamic_update_index_in_dim(g, s, chip, axis=0) for g, s in zip(got, shards)]
    fetching = {}

    def fetch_first(after):
        shards, (forwarding,), _ = _gather_wait_forward(first_handle, [len(first)], after, "gather_first_wait")
        g_in, g_meta, g_wg2 = own_slab(_forward_wait(forwarding, [], "gather_first_forward_wait"), shards)
        w_in_windows = g_in.reshape(N_CHIPS, W_IN_WIN, D)
        meta_full = jnp.concatenate([g_meta[s].reshape(N_META, -1) for s in range(N_CHIPS)], axis=1)
        wg2_full = jnp.concatenate([g_wg2[s].reshape(w_gate_lr2.shape[1], -1) for s in range(N_CHIPS)], axis=1)
        return w_in_windows, meta_full, wg2_full

    def fetch_rest(after):
        shards, (w_out_forwarding, fetching["handle"]), forward_token = _gather_wait_forward(
            rest_handle, [1, len(rest) - 1], after, "gather_rest_wait")
        g_out, = own_slab(_forward_wait(w_out_forwarding, [], "gather_w_out_forward_wait"), shards[:1])
        fetching["shards"] = shards[1:]
        return g_out.reshape(-1, D), forward_token

    def fetch_ffn(after):
        got = _forward_wait(fetching["handle"], after, "gather_ffn_forward_wait")
        return [g.reshape(-1, D) for g in own_slab(got, fetching["shards"])]

    sent = {}
    split = lambda grads: [g.reshape(N_CHIPS, 2, -1, D) for g in grads]

    def ship(key, grads, names, fetched=()):
        parts = _chip_partials(grads, [BF16] * len(grads), names, fetched)
        sent[key], ship_token = _ici_start("scatter", parts, [p.shape for p in parts], [], "scatter_" + key + "_start")
        return ship_token

    def exchange_ffn(g):
        grads = split([g[k] for k in BIG[1:]])
        sent["ffn_halves"], exchange_token = _ici_start("sibling", grads, [(N_CHIPS,) + a.shape[2:] for a in grads], [],
                                                        "sibling_ffn_start")
        return exchange_token

    def ship_ffn(after):
        grads, fetched = _ici_wait("sibling", sent["ffn_halves"], after, "sibling_ffn_wait")
        return ship("ffn", grads, list(BIG[1:]), fetched)

    def ship_w_in(dw_in_windows):
        return ship("w_in", split([dw_in_windows]), ["w_in"])

    dx, gr = _local_step(
        x[0], loss_target[0], ln_in_g, ln_in_b, b_in[0], b_gate_lr2[0], attn_sinks[0], gla_norm_g[0], ln1_g[0],
        ln1_b[0], ln2_g[0], ln2_b[0], token, fetch_first, fetch_rest, fetch_ffn, exchange_ffn, ship_ffn, ship_w_in)
    ffn_parts, ffn_got = _ici_wait("scatter", sent["ffn"], [dx], "scatter_ffn_wait")
    join_handle, token = _join_start(_chip_sums(ffn_parts, ffn_got, [], "sum_chips_ffn"), "join_ffn_start")
    w_in_parts, w_in_got = _ici_wait("scatter", sent["w_in"], [token], "scatter_w_in_wait")

    small_handle, token = _small_start(_small_pack(gr), [w_in_got[0]])
    w_in_joined = _join_halves(_chip_sums(w_in_parts, w_in_got, [token], "sum_chips_w_in"), "join_w_in")
    red = [f.reshape(2 * f.shape[1], D) for f in w_in_joined + _join_wait(join_handle, w_in_joined, "join_ffn_wait")]

    big_g = dict(zip(BIG, red))
    weights = dict(meta_tokens=meta_tokens, ln_in_g=ln_in_g, ln_in_b=ln_in_b, w_in=w_in, b_in=b_in,
                   w_gate_lr2=w_gate_lr2, b_gate_lr2=b_gate_lr2, attn_sinks=attn_sinks, gla_norm_g=gla_norm_g,
                   w_out=w_out, ln1_g=ln1_g, ln1_b=ln1_b, w_ffn_gate=w_ffn_gate, w_ffn_up=w_ffn_up,
                   w_ffn_down=w_ffn_down, ln2_g=ln2_g, ln2_b=ln2_b)
    m_in = dict(meta_tokens=m_meta_tokens, ln_in_g=m_ln_in_g, ln_in_b=m_ln_in_b, w_in=m_w_in, b_in=m_b_in,
                w_gate_lr2=m_w_gate_lr2, b_gate_lr2=m_b_gate_lr2, attn_sinks=m_attn_sinks, gla_norm_g=m_gla_norm_g,
                w_out=m_w_out, ln1_g=m_ln1_g, ln1_b=m_ln1_b, w_ffn_gate=m_w_ffn_gate, w_ffn_up=m_w_ffn_up,
                w_ffn_down=m_w_ffn_down, ln2_g=m_ln2_g, ln2_b=m_ln2_b)
    v_in = dict(meta_tokens=v_meta_tokens, ln_in_g=v_ln_in_g, ln_in_b=v_ln_in_b, w_in=v_w_in, b_in=v_b_in,
                w_gate_lr2=v_w_gate_lr2, b_gate_lr2=v_b_gate_lr2, attn_sinks=v_attn_sinks, gla_norm_g=v_gla_norm_g,
                w_out=v_w_out, ln1_g=v_ln1_g, ln1_b=v_ln1_b, w_ffn_gate=v_w_ffn_gate, w_ffn_up=v_w_ffn_up,
                w_ffn_down=v_w_ffn_down, ln2_g=v_ln2_g, ln2_b=v_ln2_b)
    names = list(weights)
    big_names = ("w_in", "w_out", "w_ffn_gate", "w_ffn_up", "w_ffn_down")

    grads, delta, new_m, new_v = {}, {}, {}, {}
    flips = [(lambda a: a.T) if kk in ("w_g", "w_u") else (lambda a: a) for kk in BIG[1:]]
    by_row = lambda a: jnp.transpose(a, (2, 0, 1))
    updated, updated_w_in = _adamw(
        [(flip(weights[k][0]), big_g[kk], flip(m_in[k][0]), flip(v_in[k][0]))
         for k, kk, flip in zip(big_names[1:], BIG[1:], flips)],
        (by_row(w_in), big_g["w_in"], by_row(m_w_in), by_row(v_w_in)), chip.astype(jnp.int32).reshape(1), r_in % BF16_ROWS)
    for k, flip, results in zip(big_names[1:], flips, updated):
        grads[k], delta[k], new_m[k], new_v[k] = (flip(t)[None] for t in results)
    grads["w_in"], delta["w_in"], new_m["w_in"], new_v["w_in"] = (jnp.transpose(t, (1, 2, 0)) for t in updated_w_in)
    small_in = [tuple(src[k].reshape(shape) for src in (weights, m_in, v_in)) for k, shape in SMALL]
    place = jnp.stack([2 * chip + lax.axis_index("c"), chip]).astype(jnp.int32)
    small_own, small_all = _small_wait(small_handle, [updated[0][0]])
    small_out, tail_row = _adamw_small(place, small_all, small_own, small_in)
    for (k, _), results in zip(SMALL, small_out):
        grads[k], delta[k], new_m[k], new_v[k] = (r.reshape(weights[k].shape) for r in results)

    return (tail_row[0, TAIL_LOSS], dx[None], *[grads[k] for k in names], *[delta[k] for k in names], *[new_m[k] for k in names],
            *[new_v[k] for k in names])
```

```python
import jax
import jax.numpy as jnp
from jax import lax
from jax.experimental import pallas as pl
from jax.experimental.pallas import tpu as pltpu

F32 = jnp.float32
BF16 = jnp.bfloat16
MESH = pl.DeviceIdType.MESH

D = 1024
SEQ = 4096
N_META = 16
SWA_HEADS, SWA_KV_HEADS, DH = 8, 2, 64
WINDOW = 128
GLA_HEADS, DK, DV = 4, 64, 128
GLA_TAU = 16.0
CH = 64
D_FF = 2816
D_IN = 2320
LN_EPS = 1e-5
RMS_EPS = 1e-6
ALPHA = 2.0 ** 0.25
NEG = -1e30
ADAM_LR, ADAM_B1, ADAM_B2, ADAM_EPS, ADAM_WD, ADAM_STEP = 0.001, 0.9, 0.999, 1e-8, 0.01, 10
O_QS, O_KS, O_VS, O_QG, O_KG, O_VG, O_RG, O_LR = 0, 512, 640, 768, 1024, 1280, 1792, 2304

LANE = 128
BLK = WINDOW
GATE_RANK = 16
D_IN_P = D_IN + LANE - GATE_RANK
META_OFF = CH - N_META
HEAD_POS = (0, 4, 1, 5, 2, 6, 3, 7)
LN_ROWS = 512
TOKEN = (8, LANE)
N_CHIPS = 4
SHARD_ROWS = dict(w_in=D_IN // N_CHIPS, w_out=D // N_CHIPS, w_g=D_FF // N_CHIPS, w_u=D_FF // N_CHIPS,
                  w_d=D_FF // N_CHIPS)
SMALL_ROWS = 48
BF16_ROWS = 16
W_IN_WIN = -(-SHARD_ROWS["w_in"] // (2 * BF16_ROWS)) * 2 * BF16_ROWS
W_IN_STARTS = tuple(s * SHARD_ROWS["w_in"] // BF16_ROWS * BF16_ROWS for s in range(N_CHIPS))
VMEM_CAP_MB = 64
VMEM_SPARE_MB = 6


def _lp():
    return SEQ + BLK


def _row_tile(cap):
    lp = _lp()
    return max(t for t in range(16, cap + 1, 16) if lp % t == 0)


def _params(vmem_mb, **kw):
    assert vmem_mb <= VMEM_CAP_MB - VMEM_SPARE_MB
    return pltpu.CompilerParams(vmem_limit_bytes=vmem_mb << 20, **kw)


def _seq(n=1):
    return ("arbitrary",) * n


def _const(shape):
    return pl.BlockSpec(shape, lambda *_: (0,) * len(shape), pipeline_mode=pl.Buffered(1))


def _acc(shape):
    return pl.BlockSpec(shape, lambda *_: (0,) * len(shape))


def _rows(tm, width):
    return pl.BlockSpec((tm, width), lambda i: (i, 0))


def _dot(a, b):
    return jnp.dot(a.astype(BF16), b.astype(BF16), preferred_element_type=F32)


def _dot_nt(a, b):
    return lax.dot_general(a.astype(BF16), b.astype(BF16), (((1,), (1,)), ((), ())), preferred_element_type=F32)


def _dot_tn(a, b):
    return lax.dot_general(a.astype(BF16), b.astype(BF16), (((0,), (0,)), ((), ())), preferred_element_type=F32)


def _dot_exact(a, b):
    return jnp.dot(a, b, precision=lax.Precision.HIGHEST, preferred_element_type=F32)


def _ln_stats(x):
    mu = jnp.mean(x, axis=-1, keepdims=True)
    xc = x - mu
    rstd = lax.rsqrt(jnp.mean(xc * xc, axis=-1, keepdims=True) + LN_EPS)
    return xc * rstd, rstd


def _ln_bwd(dy, xhat, rstd, g):
    dxh = dy * g
    return rstd * (dxh - jnp.mean(dxh, axis=-1, keepdims=True) - xhat * jnp.mean(dxh * xhat, axis=-1, keepdims=True))


def _sigmoid(x):
    return 1.0 / (1.0 + jnp.exp(-x))


def _iota(shape, dim):
    return lax.broadcasted_iota(jnp.int32, shape, dim)


def _hbm(*arrays):
    return tuple(pltpu.with_memory_space_constraint(a, pltpu.HBM) for a in arrays)


def _ln_in_fwd_real(x, g, b, token):
    tr = min(LN_ROWS, SEQ)

    def body(x_ref, g_ref, b_ref, token_ref, h_ref):
        xhat, _ = _ln_stats(x_ref[...])
        h_ref[...] = xhat * g_ref[...] + b_ref[...]

    return pl.pallas_call(
        body, name="ln_in_fwd", grid=(SEQ // tr,),
        in_specs=[_rows(tr, D), _const((1, D)), _const((1, D)), _const(TOKEN)],
        out_specs=_rows(tr, D),
        out_shape=pltpu.HBM((_lp(), D), F32),
        compiler_params=_params(32, dimension_semantics=_seq()),
    )(*_hbm(x, g, b), token)


def _ln_in_fwd_meta(h_real, meta_ext, g, b):
    def meta_body(m_ref, g_ref, b_ref, real_ref, h_ref):
        xhat, _ = _ln_stats(m_ref[...])
        h_ref[...] = xhat * g_ref[...] + b_ref[...]

    return pl.pallas_call(
        meta_body, name="ln_in_fwd_meta", grid=(1,),
        in_specs=[_const((BLK, D)), _const((1, D)), _const((1, D)), pl.BlockSpec(memory_space=pl.ANY)],
        out_specs=pl.BlockSpec((BLK, D), lambda i: (SEQ // BLK, 0)),
        out_shape=pltpu.HBM((_lp(), D), F32),
        input_output_aliases={3: 0},
        compiler_params=_params(16, dimension_semantics=_seq()),
    )(*_hbm(meta_ext, g, b, h_real))


def _in_proj(h0, w_in_windows, b_in_p, wg2_p, bg2):
    tm = _row_tile(384)
    lp = _lp()
    widths = (512, 128, 128, 256, 256, 512, 512, 128)
    offs = (O_QS, O_KS, O_VS, O_QG, O_KG, O_VG, O_RG, O_LR)
    shard = SHARD_ROWS["w_in"]

    def body(h_ref, win_ref, b_ref, wg2_ref, bg2_ref, *outs):
        w_ref = outs[9]

        @pl.when(pl.program_id(0) == 0)
        def _():
            for s in range(N_CHIPS):
                w_ref[shard * s:shard * (s + 1), :] = win_ref[s, 0:shard, :]
            w_ref[D_IN:D_IN_P, :] = jnp.zeros((D_IN_P - D_IN, D), BF16)

        proj = _dot_nt(h_ref[...], w_ref[...]) + b_ref[...]
        for pos, h in enumerate(HEAD_POS):
            outs[0][:, pos * DH:(pos + 1) * DH] = proj[:, O_QS + h * DH:O_QS + (h + 1) * DH]
        for o_ref, off, wd in zip(outs[1:8], offs[1:], widths[1:]):
            o_ref[...] = proj[:, off:off + wd]
        outs[8][...] = _dot(proj[:, O_LR:O_LR + LANE], wg2_ref[...]) + bg2_ref[...]

    return pl.pallas_call(
        body, name="in_proj", grid=(lp // tm,),
        in_specs=[_rows(tm, D), _const(w_in_windows.shape), _const((1, D_IN_P)), _const((LANE, 256)), _const((1, 256))],
        out_specs=[_rows(tm, w) for w in widths] + [_rows(tm, 256), _acc((D_IN_P, D))],
        out_shape=[pltpu.HBM((lp, w), F32) for w in widths] + [pltpu.HBM((lp, 256), F32), pltpu.HBM((D_IN_P, D), BF16)],
        compiler_params=_params(48, dimension_semantics=_seq()),
    )(*_hbm(h0, w_in_windows, b_in_p, wg2_p, bg2))


def _swa_masks(n):
    nb = SEQ // BLK
    is_meta = n == nb
    ri = _iota((BLK, BLK), 0)
    cj = _iota((BLK, BLK), 1)
    meta_col = ((cj >= META_OFF) & (cj < CH)).astype(jnp.int32)
    meta_q = meta_col * ((cj <= ri) & (ri < CH)).astype(jnp.int32)
    valid_m = jnp.where(is_meta, meta_q, meta_col) > 0
    dist_m = jnp.where(is_meta, ri - cj, n * BLK + ri + CH - cj).astype(F32)
    valid_p = jnp.where((n >= 1) & (n < nb), (cj > ri).astype(jnp.int32), 0) > 0
    dist_p = (ri + BLK - cj).astype(F32)
    valid_c = jnp.where(n < nb, (cj <= ri).astype(jnp.int32), 0) > 0
    dist_c = (ri - cj).astype(F32)
    return (dist_m, dist_p, dist_c), (valid_m, valid_p, valid_c)


def _swa_bias(n):
    dists, valids = _swa_masks(n)
    return (jnp.concatenate([-d for d in dists], axis=1),
            jnp.concatenate([jnp.where(v, 0.0, NEG) for v in valids], axis=1))


def _swa_half(ref, pos, scale=1.0):
    col = ref[:, (pos // 2) * LANE:(pos // 2 + 1) * LANE]
    lane = _iota((BLK, LANE), 1)
    mine = lane < DH if pos % 2 == 0 else lane >= DH
    return jnp.where(mine, col * scale, 0.0).astype(BF16)


def _swa_merge(even, odd):
    return jnp.where(_iota((BLK, LANE), 1) < DH, even, odd)


def _swa_softmax(t, sink):
    m = jnp.maximum(jnp.max(t, axis=-1, keepdims=True), sink)
    e = jnp.exp(t - m)
    e_sink = jnp.exp(sink - m)
    inv = 1.0 / (jnp.sum(e, axis=-1, keepdims=True) + e_sink)
    return e * inv, e_sink * inv


def _swa_kv_specs(width):
    nb = SEQ // BLK
    return [pl.BlockSpec((BLK, width), lambda n: (nb, 0)),
            pl.BlockSpec((BLK, width), lambda n: (jnp.clip(n - 1, 0, nb - 1), 0)),
            pl.BlockSpec((BLK, width), lambda n: (jnp.minimum(n, nb), 0))]


def _swa_fwd(sinks, qs, ks, vs):
    nb = SEQ // BLK
    heads = range(SWA_HEADS)

    def body(sink_ref, q_ref, km_ref, kp_ref, kc_ref, vm_ref, vp_ref, vc_ref, o_ref):
        negdist, maskbias = _swa_bias(pl.program_id(0))
        k_all = jnp.concatenate([km_ref[...], kp_ref[...], kc_ref[...]], axis=0).astype(BF16)
        v_all = jnp.concatenate([vm_ref[...], vp_ref[...], vc_ref[...]], axis=0).astype(BF16)
        q = [_swa_half(q_ref, pos, DH ** -0.5) for pos in heads]
        t = [_dot_nt(q[pos], k_all) + (2.0 ** -(HEAD_POS[pos] + 1) * negdist + maskbias) for pos in heads]
        p = [_swa_softmax(t[pos], sink_ref[HEAD_POS[pos]])[0].astype(BF16) for pos in heads]
        o = [_dot(p[pos], v_all) for pos in heads]
        for col in range(SWA_HEADS // 2):
            o_ref[:, col * LANE:(col + 1) * LANE] = _swa_merge(o[2 * col], o[2 * col + 1])

    kvw = SWA_KV_HEADS * DH
    return pl.pallas_call(
        body, name="swa_fwd", grid=(nb + 1,),
        in_specs=[pl.BlockSpec(memory_space=pltpu.SMEM), _rows(BLK, SWA_HEADS * DH)] + _swa_kv_specs(kvw) + _swa_kv_specs(kvw),
        out_specs=_rows(BLK, SWA_HEADS * DH),
        out_shape=pltpu.HBM((_lp(), SWA_HEADS * DH), F32),
        compiler_params=_params(16, dimension_semantics=_seq()),
    )(sinks, *_hbm(qs, ks, ks, ks, vs, vs, vs))


GLA_PER_STEP = BLK // CH


def _gla_block(s):
    nb = SEQ // BLK
    return jnp.where(s == 0, nb, s - 1)


def _gla_rowmask(s):
    ri = _iota((BLK, 1), 0)
    m = jnp.where(s == 0, ((ri >= META_OFF) & (ri < CH)).astype(jnp.int32), 1)
    return (m > 0).astype(F32) + jnp.zeros((BLK, 1), F32)


def _gla_chunk_masks():
    r, c = _iota((BLK, BLK), 0), _iota((BLK, BLK), 1)
    same = ((r < CH) & (c < CH)) | ((r >= CH) & (c >= CH))
    return same & (r >= c), same & (r <= c), same


def _gla_decay(z, rmask):
    log_g = (jnp.minimum(z, 0.0) - jnp.log1p(jnp.exp(-jnp.abs(z)))) * (rmask / GLA_TAU)
    lower, _, same = _gla_chunk_masks()
    return _dot_exact(lower.astype(F32), log_g), _dot_exact(same.astype(F32), log_g)


def _gla_slices(c, h):
    return slice(c * CH, (c + 1) * CH), slice(h * DK, (h + 1) * DK), slice(h * DV, (h + 1) * DV)


def _gla_fwd(qg, kg, vg, z):
    steps = SEQ // BLK + 1
    kw, vw = GLA_HEADS * DK, GLA_HEADS * DV
    pairs = [(c, h) for c in range(GLA_PER_STEP) for h in range(GLA_HEADS)]

    def body(q_ref, k_ref, v_ref, z_ref, o_ref, st_ref, st):
        s = pl.program_id(0)

        @pl.when(s == 0)
        def _():
            st[...] = jnp.zeros_like(st)

        rmask = _gla_rowmask(s)
        b, b_last = _gla_decay(z_ref[...], rmask)
        q = q_ref[...] * (rmask * DK ** -0.5)
        k = k_ref[...] * rmask
        v = v_ref[...] * rmask
        qe = q * jnp.exp(b)
        ke = k * jnp.exp(-b)
        kd = k * jnp.exp(b_last - b)
        e_last = jnp.exp(b_last)
        causal = _iota((CH, CH), 0) >= _iota((CH, CH), 1)
        a, upd, intra = {}, {}, {}
        for c, h in pairs:
            rows, ks, vs_ = _gla_slices(c, h)
            a[c, h] = jnp.where(causal, _dot_nt(qe[rows, ks], ke[rows, ks]), 0.0)
            upd[c, h] = _dot_tn(v[rows, vs_], kd[rows, ks])
        for c, h in pairs:
            rows, ks, vs_ = _gla_slices(c, h)
            intra[c, h] = _dot(a[c, h], v[rows, vs_])
        state = st[...]
        for c in range(GLA_PER_STEP):
            st_ref[0, c] = state
            for h in range(GLA_HEADS):
                rows, ks, vs_ = _gla_slices(c, h)
                o_ref[rows, vs_] = intra[c, h] + _dot_nt(qe[rows, ks], state[:, ks])
            state = state * e_last[c * CH:c * CH + 1] + jnp.concatenate([upd[c, h] for h in range(GLA_HEADS)], axis=1)
        st[...] = state

    blk = lambda w: pl.BlockSpec((BLK, w), lambda s: (_gla_block(s), 0))
    return pl.pallas_call(
        body, name="gla_fwd", grid=(steps,),
        in_specs=[blk(kw), blk(kw), blk(vw), blk(kw)],
        out_specs=[blk(vw), pl.BlockSpec((1, GLA_PER_STEP, DV, kw), lambda s: (s, 0, 0, 0))],
        out_shape=[pltpu.HBM((_lp(), vw), F32), pltpu.HBM((steps, GLA_PER_STEP, DV, kw), F32)],
        scratch_shapes=[pltpu.VMEM((DV, kw), F32)],
        compiler_params=_params(16, dimension_semantics=_seq()),
    )(*_hbm(qg, kg, vg, z))


def _post_mix(o_s, o_gla, r_g, h0, gn4, w_out, g1, b1, token):
    tm = _row_tile(384)
    lp = _lp()

    def body(os_ref, og_ref, r_ref, h0_ref, gn_ref, w_ref, g_ref, b_ref, token_ref, o_ref, pre_ref, h1_ref):
        for pos, h in enumerate(HEAD_POS):
            o_ref[:, h * DH:(h + 1) * DH] = os_ref[:, pos * DH:(pos + 1) * DH].astype(BF16)
        for h in range(GLA_HEADS):
            hs = slice(h * DV, (h + 1) * DV)
            xg = og_ref[:, hs]
            n = xg * lax.rsqrt(jnp.mean(xg * xg, axis=-1, keepdims=True) + RMS_EPS) * gn_ref[...]
            r = r_ref[:, hs]
            o_ref[:, 512 + h * DV:512 + (h + 1) * DV] = (n * (r * _sigmoid(r))).astype(BF16)
        pre = ALPHA * h0_ref[...] + _dot(o_ref[...], w_ref[...])
        pre_ref[...] = pre
        xhat, _ = _ln_stats(pre)
        h1_ref[...] = xhat * g_ref[...] + b_ref[...]

    return pl.pallas_call(
        body, name="post_mix", grid=(lp // tm,),
        in_specs=[_rows(tm, 512), _rows(tm, 512), _rows(tm, 512), _rows(tm, D), _const((1, DV)), _const((D, D)),
                  _const((1, D)), _const((1, D)), _const(TOKEN)],
        out_specs=[_rows(tm, D), _rows(tm, D), _rows(tm, D)],
        out_shape=[pltpu.HBM((lp, D), BF16), pltpu.HBM((lp, D), F32),
                   pltpu.HBM((lp, D), F32)],
        compiler_params=_params(32, dimension_semantics=_seq()),
    )(*_hbm(o_s, o_gla, r_g, h0, gn4, w_out, g1, b1), token)


def _ffn_fwd_loss_bwd(h1, wg_t, wu_t, wd, target, g2, b2):
    lp = _lp()
    tm = max(t for t in range(BLK, 384 + 1, BLK) if lp % t == 0)
    steps = lp // tm
    last_blk = SEQ // BLK - 1
    half = D_FF // 2
    n_t = tm // BLK

    def body(*refs):
        h_ref, wg_ref, wu_ref, wd_ref = refs[:4]
        t_refs = refs[4:4 + n_t]
        g2_ref, b2_ref, a_ref, dgate_ref, dup_ref, dp_ref, loss_ref, dg_ref, db_ref, g_s, u_s, acc = refs[4 + n_t:]
        i = pl.program_id(0)

        @pl.when(i == 0)
        def _():
            acc[...] = jnp.zeros_like(acc)
            dg_ref[...] = jnp.zeros_like(dg_ref)
            db_ref[...] = jnp.zeros_like(db_ref)

        h = h_ref[...]
        hb = h.astype(BF16)
        pre = ALPHA * h
        for j in range(2):
            cols = slice(j * half, (j + 1) * half)
            g = _dot_nt(hb, wg_ref[cols, :])
            u = _dot_nt(hb, wu_ref[cols, :])
            g_s[:, cols] = g
            u_s[:, cols] = u
            pre = pre + _dot(g * _sigmoid(g) * u, wd_ref[cols, :])
        xhat, rstd = _ln_stats(pre)
        real = i * tm + _iota((tm, 1), 0) < SEQ
        target_rows = jnp.concatenate([t[...] for t in t_refs], axis=0)
        diff = jnp.where(real, xhat * g2_ref[...] + b2_ref[...] - target_rows, 0.0)
        acc[...] += jnp.sum(diff * diff, axis=0, keepdims=True)
        dy = diff * (1.0 / D)
        dpre = _ln_bwd(dy, xhat, rstd, g2_ref[...])
        dp_ref[...] = dpre
        dg_ref[...] += jnp.sum(dy * xhat, axis=0, keepdims=True)
        db_ref[...] += jnp.sum(dy, axis=0, keepdims=True)
        dpb = dpre.astype(BF16)
        for j in range(2):
            cols = slice(j * half, (j + 1) * half)
            g, u = g_s[:, cols], u_s[:, cols]
            sg = _sigmoid(g)
            silu = g * sg
            da = _dot_nt(dpb, wd_ref[cols, :])
            a_ref[:, cols] = (silu * u).astype(BF16)
            dgate_ref[:, cols] = (da * u * (sg * (1.0 + g * (1.0 - sg)))).astype(BF16)
            dup_ref[:, cols] = (da * silu).astype(BF16)

        @pl.when(i == steps - 1)
        def _():
            loss_ref[...] = jnp.zeros_like(loss_ref) + (0.5 / D) * jnp.sum(acc[...], axis=1, keepdims=True)

    t_spec = lambda k: pl.BlockSpec((BLK, D), lambda i: (jnp.minimum(i * n_t + k, last_blk), 0))
    return pl.pallas_call(
        body, name="ffn_fwd_loss_bwd", grid=(steps,),
        in_specs=[_rows(tm, D), _const((D_FF, D)), _const((D_FF, D)), _const((D_FF, D))]
        + [t_spec(k) for k in range(n_t)] + [_const((1, D)), _const((1, D))],
        out_specs=[_rows(tm, D_FF), _rows(tm, D_FF), _rows(tm, D_FF), _rows(tm, D), _acc((1, LANE)), _acc((1, D)),
                   _acc((1, D))],
        out_shape=[pltpu.HBM((lp, D_FF), BF16)] * 3 + [pltpu.HBM((lp, D), F32), pltpu.HBM((1, LANE), F32),
                                                         pltpu.HBM((1, D), F32), pltpu.HBM((1, D), F32)],
        scratch_shapes=[pltpu.VMEM((tm, D_FF), F32), pltpu.VMEM((tm, D_FF), F32), pltpu.VMEM((1, D), F32)],
        compiler_params=_params(58, dimension_semantics=_seq()),
    )(*_hbm(h1, wg_t, wu_t, wd, *[target] * n_t, g2, b2))


def _ffn_out_bwd(dpre2, dgate, dup, pre1, wg_t, wu_t, g1, w_out, o_gla, r_g, gn4):
    tm = _row_tile(384)
    lp = _lp()

    def body(dp_ref, dg_ref, du_ref, p1_ref, wg_ref, wu_ref, g1_ref, w_ref, og_ref, r_ref, gn_ref,
             dp1_ref, dg1_ref, db1_ref, dos_ref, dog_ref, dr_ref, dgn_ref):
        @pl.when(pl.program_id(0) == 0)
        def _():
            for acc_ref in (dg1_ref, db1_ref, dgn_ref):
                acc_ref[...] = jnp.zeros_like(acc_ref)

        dh1 = ALPHA * dp_ref[...] + _dot(dg_ref[...], wg_ref[...]) + _dot(du_ref[...], wu_ref[...])
        xhat, rstd1 = _ln_stats(p1_ref[...])
        dpre1 = _ln_bwd(dh1, xhat, rstd1, g1_ref[...])
        dp1_ref[...] = dpre1
        dg1_ref[...] += jnp.sum(dh1 * xhat, axis=0, keepdims=True)
        db1_ref[...] += jnp.sum(dh1, axis=0, keepdims=True)

        do = _dot_nt(dpre1, w_ref[...])
        for pos, h in enumerate(HEAD_POS):
            dos_ref[:, pos * DH:(pos + 1) * DH] = do[:, h * DH:(h + 1) * DH]
        gn = gn_ref[...]
        for h in range(GLA_HEADS):
            hs = slice(h * DV, (h + 1) * DV)
            xg = og_ref[:, hs]
            rstd = lax.rsqrt(jnp.mean(xg * xg, axis=-1, keepdims=True) + RMS_EPS)
            nx = xg * rstd
            r = r_ref[:, hs]
            sr = _sigmoid(r)
            d_o = do[:, 512 + h * DV:512 + (h + 1) * DV]
            dr_ref[:, hs] = d_o * (nx * gn) * (sr * (1.0 + r * (1.0 - sr)))
            dn = d_o * (r * sr)
            dgn_ref[...] += jnp.sum(dn * nx, axis=0, keepdims=True)
            dnx = dn * gn
            dog_ref[:, hs] = rstd * (dnx - nx * jnp.mean(dnx * nx, axis=-1, keepdims=True))

    return pl.pallas_call(
        body, name="ffn_out_bwd", grid=(lp // tm,),
        in_specs=[_rows(tm, D), _rows(tm, D_FF), _rows(tm, D_FF), _rows(tm, D), _const((D_FF, D)), _const((D_FF, D)),
                  _const((1, D)), _const((D, D)), _rows(tm, 512), _rows(tm, 512), _const((1, DV))],
        out_specs=[_rows(tm, D), _acc((1, D)), _acc((1, D)), _rows(tm, 512), _rows(tm, 512), _rows(tm, 512),
                   _acc((1, DV))],
        out_shape=[pltpu.HBM((lp, D), F32), pltpu.HBM((1, D), F32), pltpu.HBM((1, D), F32)]
        + [pltpu.HBM((lp, 512), F32)] * 3 + [pltpu.HBM((1, DV), F32)],
        compiler_params=_params(48, dimension_semantics=_seq()),
    )(*_hbm(dpre2, dgate, dup, pre1, wg_t, wu_t, g1, w_out, o_gla, r_g, gn4))


def _atb(a, b, name, token=None, windows=None):
    lp = _lp()
    tm = _row_tile(1408)
    n, w = a.shape[1], b.shape[1]
    bw = 512 if n * w * 4 > (4 << 20) else w
    tokens = [] if token is None else [token]
    steps = lp // tm

    def body(a_ref, b_ref, *rest):
        o_ref, acc_ref = rest[len(tokens):] if windows else (rest[-1], rest[-1])

        @pl.when(pl.program_id(1) == 0)
        def _():
            acc_ref[...] = jnp.zeros_like(acc_ref)

        acc_ref[...] += _dot_tn(a_ref[...], b_ref[...])

        if windows:
            @pl.when(pl.program_id(1) == steps - 1)
            def _():
                for s, start in enumerate(windows[0]):
                    o_ref[s] = acc_ref[start:start + windows[1], :]

    if windows:
        count, height = len(windows[0]), windows[1]
        out_spec, out_shape = pl.BlockSpec((count, height, bw), lambda j, k: (0, 0, j)), (count, height, w)
    else:
        out_spec, out_shape = pl.BlockSpec((n, bw), lambda j, k: (0, j)), (n, w)
    return pl.pallas_call(
        body, name=name, grid=(w // bw, steps),
        in_specs=[pl.BlockSpec((tm, n), lambda j, k: (k, 0)), pl.BlockSpec((tm, bw), lambda j, k: (k, j))]
        + [_const(TOKEN)] * len(tokens),
        out_specs=out_spec, out_shape=pltpu.HBM(out_shape, F32),
        scratch_shapes=[pltpu.VMEM((n, bw), F32)] if windows else [],
        compiler_params=_params(48, dimension_semantics=_seq(2)),
    )(*_hbm(a, b), *tokens)


def _gla_bwd(qg, kg, vg, z, do_gla, st_all, token):
    steps = SEQ // BLK + 1
    kw, vw = GLA_HEADS * DK, GLA_HEADS * DV
    pairs = [(c, h) for c in range(GLA_PER_STEP) for h in range(GLA_HEADS)]
    heads = range(GLA_HEADS)

    def body(q_ref, k_ref, v_ref, z_ref, do_ref, st_ref, token_ref, dq_ref, dk_ref, dv_ref, dz_ref, dst):
        @pl.when(pl.program_id(0) == 0)
        def _():
            dst[...] = jnp.zeros_like(dst)

        rmask = _gla_rowmask(steps - 1 - pl.program_id(0))
        zz = z_ref[...]
        b, b_last = _gla_decay(zz, rmask)
        e_b, e_nb, e_kd, e_last = jnp.exp(b), jnp.exp(-b), jnp.exp(b_last - b), jnp.exp(b_last)
        q = q_ref[...] * (rmask * DK ** -0.5)
        k = k_ref[...] * rmask
        v = v_ref[...] * rmask
        qe, ke, kd = q * e_b, k * e_nb, k * e_kd
        d_o = do_ref[...]
        causal = _iota((CH, CH), 0) >= _iota((CH, CH), 1)
        a, da, dqe, dke, dv_intra, carry = {}, {}, {}, {}, {}, {}
        for c, h in pairs:
            rows, ks, vs_ = _gla_slices(c, h)
            a[c, h] = jnp.where(causal, _dot_nt(qe[rows, ks], ke[rows, ks]), 0.0)
            da[c, h] = jnp.where(causal, _dot_nt(d_o[rows, vs_], v[rows, vs_]), 0.0)
            carry[c, h] = _dot_tn(d_o[rows, vs_], qe[rows, ks])
        for c, h in pairs:
            rows, ks, vs_ = _gla_slices(c, h)
            dqe[c, h] = _dot(d_o[rows, vs_], st_ref[0, c][:, ks]) + _dot(da[c, h], ke[rows, ks])
            dke[c, h] = _dot_tn(da[c, h], qe[rows, ks])
            dv_intra[c, h] = _dot_tn(a[c, h], d_o[rows, vs_])
        dstate = dst[...]
        dkd, db_decay = {}, {}
        for c in reversed(range(GLA_PER_STEP)):
            for h in heads:
                rows, ks, vs_ = _gla_slices(c, h)
                dkd[c, h] = _dot(v[rows, vs_], dstate[:, ks])
                dv_ref[rows, vs_] = dv_intra[c, h] + _dot_nt(kd[rows, ks], dstate[:, ks])
            chunk_last = e_last[c * CH:c * CH + 1]
            db_decay[c] = jnp.sum(dstate * st_ref[0, c], axis=0, keepdims=True) * chunk_last
            dstate = dstate * chunk_last + jnp.concatenate([carry[c, h] for h in heads], axis=1)
        dst[...] = dstate
        rows_of = lambda parts: jnp.concatenate(
            [jnp.concatenate([parts[c, h] for h in heads], axis=1) for c in range(GLA_PER_STEP)], axis=0)
        dqe_all, dke_all, dkd_all = rows_of(dqe), rows_of(dke), rows_of(dkd)
        dq_ref[...] = dqe_all * e_b * (rmask * DK ** -0.5)
        dk_ref[...] = (dke_all * e_nb + dkd_all * e_kd) * rmask
        dkd_kd = dkd_all * kd
        db = dqe_all * qe - dke_all * ke - dkd_kd
        _, upper, same = _gla_chunk_masks()
        decay_rows = jnp.concatenate([jnp.broadcast_to(db_decay[c], (CH, kw)) for c in range(GLA_PER_STEP)], axis=0)
        dlog_g = _dot_exact(upper.astype(F32), db) + _dot_exact(same.astype(F32), dkd_kd) + decay_rows
        dz_ref[...] = dlog_g * (rmask / GLA_TAU) * _sigmoid(-zz)

    blk = lambda w: pl.BlockSpec((BLK, w), lambda s: (_gla_block(steps - 1 - s), 0))
    return pl.pallas_call(
        body, name="gla_bwd", grid=(steps,),
        in_specs=[blk(kw), blk(kw), blk(vw), blk(kw), blk(vw),
                  pl.BlockSpec((1, GLA_PER_STEP, DV, kw), lambda s: (steps - 1 - s, 0, 0, 0)), _const(TOKEN)],
        out_specs=[blk(kw), blk(kw), blk(vw), blk(kw)],
        out_shape=[pltpu.HBM((_lp(), kw), F32), pltpu.HBM((_lp(), kw), F32),
                   pltpu.HBM((_lp(), vw), F32), pltpu.HBM((_lp(), kw), F32)],
        scratch_shapes=[pltpu.VMEM((DV, kw), F32)],
        compiler_params=_params(16, dimension_semantics=_seq()),
    )(*_hbm(qg, kg, vg, z, do_gla, st_all), token)


def _swa_bwd(sinks, qs, ks, vs, do_s, token):
    nb = SEQ // BLK
    kvw = SWA_KV_HEADS * DH
    scale = DH ** -0.5
    heads = range(SWA_HEADS)

    def body(sink_ref, q_ref, km_ref, kp_ref, kc_ref, vm_ref, vp_ref, vc_ref, do_ref, token_ref,
             dq_ref, dk_ref, dv_ref, dsink_ref, carry_k, carry_v, meta_k, meta_v):
        n = pl.program_id(0)

        @pl.when(n == 0)
        def _():
            for r in (carry_k, carry_v, meta_k, meta_v):
                r[...] = jnp.zeros_like(r)
            dsink_ref[...] = jnp.zeros_like(dsink_ref)

        @pl.when(n <= nb)
        def _():
            negdist, maskbias = _swa_bias(n)
            lane = _iota((1, LANE), 1)
            k_all = jnp.concatenate([km_ref[...], kp_ref[...], kc_ref[...]], axis=0).astype(BF16)
            v_all = jnp.concatenate([vm_ref[...], vp_ref[...], vc_ref[...]], axis=0).astype(BF16)
            q = [_swa_half(q_ref, pos, scale) for pos in heads]
            d_o = [_swa_half(do_ref, pos) for pos in heads]
            t = [_dot_nt(q[pos], k_all) + (2.0 ** -(HEAD_POS[pos] + 1) * negdist + maskbias) for pos in heads]
            dp = [_dot_nt(d_o[pos], v_all) for pos in heads]
            soft = [_swa_softmax(t[pos], sink_ref[HEAD_POS[pos]]) for pos in heads]
            p = [s[0] for s in soft]
            delta = [jnp.sum(p[pos] * dp[pos], axis=-1, keepdims=True) for pos in heads]
            ds = [(p[pos] * (dp[pos] - delta[pos])).astype(BF16) for pos in heads]
            dq = [_dot(ds[pos], k_all) for pos in heads]
            for col in range(SWA_HEADS // 2):
                dq_ref[:, col * LANE:(col + 1) * LANE] = scale * _swa_merge(dq[2 * col], dq[2 * col + 1])
            dsink = jnp.zeros((1, LANE), F32)
            for pos in heads:
                dsink = dsink + jnp.where(lane == HEAD_POS[pos],
                                          -jnp.sum(soft[pos][1] * delta[pos], axis=0, keepdims=True), 0.0)
            dsink_ref[...] += dsink
            dk3 = _dot_tn(jnp.concatenate(q, axis=0), jnp.concatenate(ds, axis=0)).T
            dv3 = _dot_tn(jnp.concatenate(d_o, axis=0), jnp.concatenate([x.astype(BF16) for x in p], axis=0)).T
            meta_k[...] += dk3[0:BLK]
            meta_v[...] += dv3[0:BLK]
            dk_ref[...] = carry_k[...] + dk3[BLK:2 * BLK]
            dv_ref[...] = carry_v[...] + dv3[BLK:2 * BLK]
            carry_k[...] = dk3[2 * BLK:3 * BLK]
            carry_v[...] = dv3[2 * BLK:3 * BLK]

        @pl.when(n == nb + 1)
        def _():
            dk_ref[...] = meta_k[...]
            dv_ref[...] = meta_v[...]

    kv_out = pl.BlockSpec((BLK, kvw), lambda n: (jnp.where(n == nb + 1, nb, jnp.clip(n - 1, 0, nb - 1)), 0))
    qblk = pl.BlockSpec((BLK, SWA_HEADS * DH), lambda n: (jnp.minimum(n, nb), 0))
    return pl.pallas_call(
        body, name="swa_bwd", grid=(nb + 2,),
        in_specs=[pl.BlockSpec(memory_space=pltpu.SMEM), qblk] + _swa_kv_specs(kvw) + _swa_kv_specs(kvw)
        + [qblk, _const(TOKEN)],
        out_specs=[qblk, kv_out, kv_out, _acc((1, LANE))],
        out_shape=[pltpu.HBM((_lp(), SWA_HEADS * DH), F32), pltpu.HBM((_lp(), kvw), F32),
                   pltpu.HBM((_lp(), kvw), F32), pltpu.HBM((1, LANE), F32)],
        scratch_shapes=[pltpu.VMEM((BLK, kvw), F32)] * 4,
        compiler_params=_params(16, dimension_semantics=_seq()),
    )(sinks, *_hbm(qs, ks, ks, ks, vs, vs, vs, do_s), token)


def _in_bwd(dqs, dks, dvs, dqg, dkg, dvg, drg, dz, dpre1, w_in_t, wg2_p):
    tm = _row_tile(384)
    lp = _lp()
    widths = (512, 128, 128, 256, 256, 512, 512)
    offs = (O_QS, O_KS, O_VS, O_QG, O_KG, O_VG, O_RG)

    def body(*refs):
        parts, (dz_ref, dp1_ref, w_ref, wg2_ref, dproj_ref, dh0_ref, dbin_ref, dbg_ref) = refs[:7], refs[7:]

        @pl.when(pl.program_id(0) == 0)
        def _():
            dbin_ref[...] = jnp.zeros_like(dbin_ref)
            dbg_ref[...] = jnp.zeros_like(dbg_ref)

        for pos, h in enumerate(HEAD_POS):
            val = parts[0][:, pos * DH:(pos + 1) * DH]
            dproj_ref[:, O_QS + h * DH:O_QS + (h + 1) * DH] = val.astype(BF16)
            dbin_ref[:, O_QS + h * DH:O_QS + (h + 1) * DH] += jnp.sum(val, axis=0, keepdims=True)
        for p_ref, off, wd in zip(parts[1:], offs[1:], widths[1:]):
            val = p_ref[...]
            dproj_ref[:, off:off + wd] = val.astype(BF16)
            dbin_ref[:, off:off + wd] += jnp.sum(val, axis=0, keepdims=True)
        dz = dz_ref[...]
        dlr = _dot_nt(dz, wg2_ref[...])
        dproj_ref[:, O_LR:O_LR + LANE] = dlr.astype(BF16)
        dbin_ref[:, O_LR:O_LR + LANE] += jnp.sum(dlr, axis=0, keepdims=True)
        dbg_ref[...] += jnp.sum(dz, axis=0, keepdims=True)
        dh0_ref[...] = ALPHA * dp1_ref[...] + _dot(dproj_ref[...], w_ref[...])

    return pl.pallas_call(
        body, name="in_bwd", grid=(lp // tm,),
        in_specs=[_rows(tm, w) for w in widths] + [_rows(tm, 256), _rows(tm, D), _const((D_IN_P, D)), _const((LANE, 256))],
        out_specs=[_rows(tm, D_IN_P), _rows(tm, D), _acc((1, D_IN_P)), _acc((1, 256))],
        out_shape=[pltpu.HBM((lp, D_IN_P), BF16), pltpu.HBM((lp, D), F32),
                   pltpu.HBM((1, D_IN_P), F32), pltpu.HBM((1, 256), F32)],
        compiler_params=_params(40, dimension_semantics=_seq()),
    )(*_hbm(dqs, dks, dvs, dqg, dkg, dvg, drg, dz, dpre1, w_in_t, wg2_p))


def _ln_in_bwd(x, meta_ext, dh0, g, token):
    tr = min(LN_ROWS, SEQ)

    def ln_bwd(x_ref, dh_ref, g_ref, dx_ref, dg_ref, db_ref):
        @pl.when(pl.program_id(0) == 0)
        def _():
            dg_ref[...] = jnp.zeros_like(dg_ref)
            db_ref[...] = jnp.zeros_like(db_ref)

        xhat, rstd = _ln_stats(x_ref[...])
        dh = dh_ref[...]
        dx_ref[...] = _ln_bwd(dh, xhat, rstd, g_ref[...])
        dg_ref[...] += jnp.sum(dh * xhat, axis=0, keepdims=True)
        db_ref[...] += jnp.sum(dh, axis=0, keepdims=True)

    def body(x_ref, dh_ref, g_ref, token_ref, dx_ref, dg_ref, db_ref):
        ln_bwd(x_ref, dh_ref, g_ref, dx_ref, dg_ref, db_ref)

    def meta_body(m_ref, dh_ref, g_ref, dm_ref, dg_ref, db_ref):
        ln_bwd(m_ref, dh_ref, g_ref, dm_ref, dg_ref, db_ref)

    sums = [pltpu.HBM((1, D), F32), pltpu.HBM((1, D), F32)]
    dx, dg, db = pl.pallas_call(
        body, name="ln_in_bwd", grid=(SEQ // tr,),
        in_specs=[_rows(tr, D), _rows(tr, D), _const((1, D)), _const(TOKEN)],
        out_specs=[_rows(tr, D), _acc((1, D)), _acc((1, D))],
        out_shape=[pltpu.HBM((SEQ, D), F32)] + sums,
        compiler_params=_params(32, dimension_semantics=_seq()),
    )(*_hbm(x, dh0, g), token)
    dm, dg_m, db_m = pl.pallas_call(
        meta_body, name="ln_in_bwd_meta", grid=(1,),
        in_specs=[_const((BLK, D)), pl.BlockSpec((BLK, D), lambda i: (SEQ // BLK, 0)), _const((1, D))],
        out_specs=[_acc((BLK, D)), _acc((1, D)), _acc((1, D))],
        out_shape=[pltpu.HBM((BLK, D), F32)] + sums,
        compiler_params=_params(16, dimension_semantics=_seq()),
    )(*_hbm(meta_ext, dh0, g))
    return dx, dm, dg + dg_m, db + db_m


def _local_step(x, target, ln_in_g, ln_in_b, b_in, bg2, sinks, gn, g1, b1, g2, b2,
                token, fetch_first, fetch_rest, fetch_ffn, exchange_ffn, ship_ffn, exchange_w_in, ship_w_in):
    row = lambda v: v.reshape(1, -1).astype(F32)
    b_in_p = jnp.pad(row(b_in), ((0, 0), (0, D_IN_P - D_IN)))
    gn4 = row(gn)
    sinks = sinks.reshape(-1).astype(F32)

    h_real = _ln_in_fwd_real(x, row(ln_in_g), row(ln_in_b), token)
    w_in_windows, meta_full, wg2 = fetch_first([h_real])
    meta_ext = jnp.pad(meta_full, ((META_OFF, BLK - CH), (0, 0)))
    wg2_p = jnp.pad(wg2, ((0, LANE - wg2.shape[0]), (0, 0))).astype(BF16)
    h0 = _ln_in_fwd_meta(h_real, meta_ext, row(ln_in_g), row(ln_in_b))
    qs, ks, vs, qg, kg, vg, rg, glr, z, w_in_t = _in_proj(h0, w_in_windows, b_in_p, wg2_p, row(bg2))
    o_s = _swa_fwd(sinks, qs, ks, vs)
    o_gla, st_all = _gla_fwd(qg, kg, vg, z)
    w_out, token = fetch_rest([o_s, o_gla])
    o, pre1, h1 = _post_mix(o_s, o_gla, rg, h0, gn4, w_out, row(g1), row(b1), token)
    wg_t, wu_t, wd = fetch_ffn([pre1])
    a, dgate, dup, dpre2, loss, dg2, db2 = _ffn_fwd_loss_bwd(h1, wg_t, wu_t, wd, target, row(g2), row(b2))
    dpre1, dg1, db1, do_s, do_gla, drg, dgn = _ffn_out_bwd(dpre2, dgate, dup, pre1, wg_t, wu_t, row(g1), w_out, o_gla,
                                                           rg, gn4)
    dwd = _atb(a, dpre2, "dw_down")
    dwg_t = _atb(dgate, h1, "dw_gate")
    dwu_t = _atb(dup, h1, "dw_up")
    token = exchange_ffn(dict(w_out=_atb(o, dpre1, "dw_out"), w_g=dwg_t, w_u=dwu_t, w_d=dwd))
    dqg, dkg, dvg, dz = _gla_bwd(qg, kg, vg, z, do_gla, st_all, token)
    token = ship_ffn([dqg])
    dqs, dks, dvs, dsinks = _swa_bwd(sinks, qs, ks, vs, do_s, token)
    dproj, dh0, db_in_p, dbg2 = _in_bwd(dqs, dks, dvs, dqg, dkg, dvg, drg, dz, dpre1, w_in_t, wg2_p)
    token = exchange_w_in(_atb(dproj, h0, "dw_in", windows=(W_IN_STARTS, W_IN_WIN)))
    dwg2_p = _atb(glr, dz, "dw_gate_lr2", token)
    token = ship_w_in([dwg2_p])
    dx, dmeta_blk, dg_in, db_in_ln = _ln_in_bwd(x, meta_ext, dh0, row(ln_in_g), token)

    small = dict(meta_blk=dmeta_blk, ln_in_g=dg_in, ln_in_b=db_in_ln, ln1_g=dg1, ln1_b=db1, ln2_g=dg2, ln2_b=db2,
                 b_in_p=db_in_p, wg2_p=dwg2_p, bg2=dbg2, sinks=dsinks, gn=dgn, loss=loss)
    return dx, small


HBM = pl.BlockSpec(memory_space=pltpu.HBM)


def _place():
    return lax.axis_index("x"), lax.axis_index("y"), lax.axis_index("c")


def _other_chips(x, y):
    return [(1 - x, y), (x, 1 - y), (1 - x, 1 - y)]


def _dma_sems(n):
    return pltpu.SemaphoreType.DMA((n,))


def _comm_params():
    return pltpu.CompilerParams(has_side_effects=True)


SEM = pl.BlockSpec(memory_space=pltpu.SEMAPHORE)


PER_ARRAY = dict(gather=3, scatter=3, sibling=N_CHIPS)


def _ici_copies(kind, landing, srcs, lands, send_sems, recv_sems):
    x, y, c = _place()
    mine = 2 * x + y
    copies = []
    for a in range(len(srcs)):
        if kind == "sibling":
            for s in range(N_CHIPS):
                copies.append(pltpu.make_async_remote_copy(
                    srcs[a].at[s, 1 - c], lands[a].at[s], send_sems.at[N_CHIPS * a + s], recv_sems.at[N_CHIPS * a + s],
                    device_id=(x, y, 1 - c), device_id_type=MESH))
            continue
        for j, (px, py) in enumerate(_other_chips(x, y)):
            slab = 2 * px + py if landing else mine
            if kind == "gather":
                src, dst = srcs[a].at[c], lands[a].at[slab, c]
            else:
                src, dst = srcs[a].at[2 * px + py], lands[a].at[slab]
            copies.append(pltpu.make_async_remote_copy(src, dst, send_sems.at[3 * a + j], recv_sems.at[3 * a + j],
                                                       device_id=(px, py, c), device_id_type=MESH))
    return copies


def _split_params():
    return pltpu.CompilerParams(has_side_effects=pltpu.SideEffectType.DATAFLOW_SIDE_EFFECTING)


def _ici_start(kind, srcs, land_shapes, after, name):
    n = len(srcs)
    lands = [pltpu.with_memory_space_constraint(lax.empty(s, a.dtype), pltpu.HBM) for s, a in zip(land_shapes, srcs)]

    def body(*refs):
        outs = refs[2 * n + len(after):]
        for cp in _ici_copies(kind, False, refs[:n], refs[n:2 * n], outs[0], outs[1]):
            cp.start()
        outs[-1][...] = jnp.zeros(TOKEN, F32)

    outs = pl.pallas_call(
        body, name=name, in_specs=[HBM] * (2 * n) + [pl.BlockSpec(memory_space=pl.ANY)] * len(after),
        out_specs=[SEM, SEM] + [HBM] * (2 * n) + [pl.BlockSpec(memory_space=pltpu.VMEM)],
        out_shape=[_dma_sems(PER_ARRAY[kind] * n)] * 2 + [pltpu.HBM(a.shape, a.dtype) for a in list(srcs) + lands]
        + [jax.ShapeDtypeStruct(TOKEN, F32)],
        input_output_aliases={i: 2 + i for i in range(2 * n)},
        compiler_params=_split_params(),
    )(*_hbm(*srcs), *lands, *after)
    return outs[:-1], outs[-1]


def _ici_wait(kind, handle, after, name):
    n = (len(handle) - 2) // 2

    def body(*refs):
        for cp in _ici_copies(kind, True, refs[:n], refs[n:2 * n], refs[2 * n], refs[2 * n + 1]):
            cp.wait_send()
            cp.wait_recv()

    outs = pl.pallas_call(
        body, name=name, in_specs=[HBM] * (2 * n) + [SEM, SEM] + [pl.BlockSpec(memory_space=pl.ANY)] * len(after),
        out_specs=[HBM] * (2 * n), out_shape=[pltpu.HBM(a.shape, a.dtype) for a in handle[2:]],
        input_output_aliases={i: i for i in range(2 * n)},
        compiler_params=_split_params(),
    )(*handle[2:], handle[0], handle[1], *after)
    return list(outs[:n]), list(outs[n:])


def _forward_copies(landing, arrs, send_sems, recv_sems):
    x, y, c = _place()
    copies = []
    for a in range(len(arrs)):
        for j, (px, py) in enumerate(_other_chips(x, y)):
            half = 1 - c if landing else c
            copies.append(pltpu.make_async_remote_copy(
                arrs[a].at[2 * px + py, c], arrs[a].at[2 * px + py, half], send_sems.at[3 * a + j],
                recv_sems.at[3 * a + j], device_id=(x, y, 1 - c), device_id_type=MESH))
    return copies


def _gather_wait_forward(handle, groups, after, name):
    n = (len(handle) - 2) // 2
    assert sum(groups) == n

    def body(*refs):
        outs = refs[2 * n + 2 + len(after):]
        lands, sems = outs[n:2 * n], outs[2 * n:-1]
        arrivals = _ici_copies("gather", True, refs[:n], refs[n:2 * n], refs[2 * n], refs[2 * n + 1])
        sends, first = [], 0
        for g, count in enumerate(groups):
            sends += _forward_copies(False, lands[first:first + count], sems[2 * g], sems[2 * g + 1])
            first += count
        for cp, send in zip(arrivals, sends):
            cp.wait_recv()
            send.start()
        for cp in arrivals:
            cp.wait_send()
        outs[-1][...] = jnp.zeros(TOKEN, F32)

    outs = pl.pallas_call(
        body, name=name, in_specs=[HBM] * (2 * n) + [SEM, SEM] + [pl.BlockSpec(memory_space=pl.ANY)] * len(after),
        out_specs=[HBM] * (2 * n) + [SEM] * (2 * len(groups)) + [pl.BlockSpec(memory_space=pltpu.VMEM)],
        out_shape=[pltpu.HBM(a.shape, a.dtype) for a in handle[2:]]
        + [_dma_sems(3 * count) for count in groups for _ in range(2)] + [jax.ShapeDtypeStruct(TOKEN, F32)],
        input_output_aliases={i: i for i in range(2 * n)},
        compiler_params=_split_params(),
    )(*handle[2:], handle[0], handle[1], *after)
    lands, sems, handles, first = outs[n:2 * n], outs[2 * n:-1], [], 0
    for g, count in enumerate(groups):
        handles.append([sems[2 * g], sems[2 * g + 1], *lands[first:first + count]])
        first += count
    return list(outs[:n]), handles, outs[-1]


def _forward_wait(handle, after, name):
    n = len(handle) - 2

    def body(*refs):
        for cp in _forward_copies(True, refs[:n], refs[n], refs[n + 1]):
            cp.wait_send()
            cp.wait_recv()

    return list(pl.pallas_call(
        body, name=name, in_specs=[HBM] * n + [SEM, SEM] + [pl.BlockSpec(memory_space=pl.ANY)] * len(after),
        out_specs=[HBM] * n, out_shape=[pltpu.HBM(a.shape, a.dtype) for a in handle[2:]],
        input_output_aliases={i: i for i in range(n)},
        compiler_params=_split_params(),
    )(*handle[2:], handle[0], handle[1], *after))


def _add_halves(core, grads, recvs, dtypes, name):
    n = len(grads)
    heights = [g.shape[2] for g in grads]

    def body(c_ref, *refs):
        for a in range(n):
            refs[2 * n + a][...] = (refs[2 * a][0] + refs[2 * a + 1][...]).astype(dtypes[a])

    slab = lambda h: pl.BlockSpec((1, h, D), lambda s, c: (s, 0, 0))
    mine = lambda h: pl.BlockSpec((1, 1, h, D), lambda s, c: (s, c[0], 0, 0))
    return pl.pallas_call(
        body, name=name,
        grid_spec=pltpu.PrefetchScalarGridSpec(
            num_scalar_prefetch=1, grid=(N_CHIPS,),
            in_specs=[spec(h) for h in heights for spec in (mine, slab)], out_specs=[slab(h) for h in heights]),
        out_shape=[pltpu.HBM((N_CHIPS, h, D), dt) for h, dt in zip(heights, dtypes)],
        compiler_params=_params(32, dimension_semantics=_seq()),
    )(core, *_hbm(*[a for pair in zip(grads, recvs) for a in pair]))


N_DEVICES = 2 * N_CHIPS
PEER_FLIPS = [(dx, dy, dc) for dx in (0, 1) for dy in (0, 1) for dc in (0, 1)][1:]


def _small_copies(landing, p_ref, out_ref, send_sems, recv_sems):
    x, y, c = _place()
    flip = lambda v, d: 1 - v if d else v
    copies = []
    for k, flips in enumerate(PEER_FLIPS):
        px, py, pc = (flip(v, d) for v, d in zip((x, y, c), flips))
        slab = 4 * px + 2 * py + pc if landing else 4 * x + 2 * y + c
        copies.append(pltpu.make_async_remote_copy(p_ref, out_ref.at[slab], send_sems.at[k], recv_sems.at[k],
                                                   device_id=(px, py, pc), device_id_type=MESH))
    return copies


def _small_start(pack, after):
    n = len(PEER_FLIPS)
    land = pltpu.with_memory_space_constraint(lax.empty((N_DEVICES,) + pack.shape, F32), pltpu.HBM)

    def body(p_ref, land_ref, *refs):
        outs = refs[len(after):]
        for cp in _small_copies(False, p_ref, land_ref, outs[0], outs[1]):
            cp.start()
        outs[-1][...] = jnp.zeros(TOKEN, F32)

    outs = pl.pallas_call(
        body, name="small_exchange_start", in_specs=[HBM, HBM] + [pl.BlockSpec(memory_space=pl.ANY)] * len(after),
        out_specs=[SEM, SEM, HBM, HBM, pl.BlockSpec(memory_space=pltpu.VMEM)],
        out_shape=[_dma_sems(n), _dma_sems(n), pltpu.HBM(pack.shape, F32), pltpu.HBM(land.shape, F32),
                   jax.ShapeDtypeStruct(TOKEN, F32)],
        input_output_aliases={0: 2, 1: 3},
        compiler_params=_split_params(),
    )(*_hbm(pack), land, *after)
    return outs[:-1], outs[-1]


def _small_wait(handle, after):
    def body(p_ref, land_ref, send_sems, recv_sems, *rest):
        for cp in _small_copies(True, p_ref, land_ref, send_sems, recv_sems):
            cp.wait_send()
            cp.wait_recv()

    return pl.pallas_call(
        body, name="small_exchange_wait", in_specs=[HBM, HBM, SEM, SEM] + [pl.BlockSpec(memory_space=pl.ANY)] * len(after),
        out_specs=[HBM, HBM], out_shape=[pltpu.HBM(a.shape, F32) for a in handle[2:]],
        input_output_aliases={0: 0, 1: 1},
        compiler_params=_split_params(),
    )(handle[2], handle[3], handle[0], handle[1], *after)


def _sum_chips(slots, firsts, rests, after, name):
    n = len(firsts)

    def body(i_ref, *refs):
        outs = refs[4 * n + len(after):]
        for a in range(n):
            first, r1, r2, r3 = refs[4 * a:4 * a + 4]
            outs[a][...] = ((first[...].astype(F32) + r1[...].astype(F32)) + r2[...].astype(F32)) + r3[...].astype(F32)

    slab = lambda h, k: pl.BlockSpec((1, h, D), lambda i, ix: (ix[k], 0, 0))
    heights = [f.shape[1] for f in firsts]
    return pl.pallas_call(
        body, name=name,
        grid_spec=pltpu.PrefetchScalarGridSpec(
            num_scalar_prefetch=1, grid=(1,),
            in_specs=[slab(h, k) for h in heights for k in range(4)] + [pl.BlockSpec(memory_space=pl.ANY)] * len(after),
            out_specs=[slab(h, 4) for h in heights]),
        out_shape=[pltpu.HBM((2, h, D), F32) for h in heights],
        compiler_params=_params(48, dimension_semantics=_seq()),
    )(slots, *_hbm(*[a for f, r in zip(firsts, rests) for a in (f, r, r, r)]), *after)


def _join_copies(landing, arrs, send_sems, recv_sems):
    x, y, c = _place()
    slab = 1 - c if landing else c
    return [pltpu.make_async_remote_copy(arr.at[slab], arr.at[slab], send_sems.at[a], recv_sems.at[a],
                                         device_id=(x, y, 1 - c), device_id_type=MESH) for a, arr in enumerate(arrs)]


def _join_halves(halves, name):
    n = len(halves)

    def body(*refs):
        outs = refs[n:2 * n]
        send_sems, recv_sems = refs[2 * n:]
        sends = _join_copies(False, outs, send_sems, recv_sems)
        for cp in sends:
            cp.start()
        for cp in _join_copies(True, outs, send_sems, recv_sems):
            cp.wait_recv()
        for cp in sends:
            cp.wait_send()

    return list(pl.pallas_call(
        body, name=name, in_specs=[HBM] * n, out_specs=[HBM] * n,
        out_shape=[pltpu.HBM(h.shape, F32) for h in halves],
        input_output_aliases={a: a for a in range(n)},
        scratch_shapes=[_dma_sems(n)] * 2,
        compiler_params=_comm_params(),
    )(*_hbm(*halves)))


def _join_start(halves, name):
    n = len(halves)

    def body(*refs):
        outs = refs[n:]
        for cp in _join_copies(False, refs[:n], outs[0], outs[1]):
            cp.start()
        outs[-1][...] = jnp.zeros(TOKEN, F32)

    outs = pl.pallas_call(
        body, name=name, in_specs=[HBM] * n,
        out_specs=[SEM, SEM] + [HBM] * n + [pl.BlockSpec(memory_space=pltpu.VMEM)],
        out_shape=[_dma_sems(n)] * 2 + [pltpu.HBM(a.shape, a.dtype) for a in halves] + [jax.ShapeDtypeStruct(TOKEN, F32)],
        input_output_aliases={i: 2 + i for i in range(n)},
        compiler_params=_split_params(),
    )(*_hbm(*halves))
    return outs[:-1], outs[-1]


def _join_wait(handle, after, name):
    n = len(handle) - 2

    def body(*refs):
        for cp in _join_copies(True, refs[:n], refs[n], refs[n + 1]):
            cp.wait_send()
            cp.wait_recv()

    return list(pl.pallas_call(
        body, name=name, in_specs=[HBM] * n + [SEM, SEM] + [pl.BlockSpec(memory_space=pl.ANY)] * len(after),
        out_specs=[HBM] * n, out_shape=[pltpu.HBM(a.shape, a.dtype) for a in handle[2:]],
        input_output_aliases={i: i for i in range(n)},
        compiler_params=_split_params(),
    )(*handle[2:], handle[0], handle[1], *after))


def _chip_partials(grads, fetched, wire_dtypes, name):
    core = lax.axis_index("c").astype(jnp.int32).reshape(1)
    return list(_add_halves(core, grads, fetched, wire_dtypes, name))


def _chip_sums(parts, got, after, name):
    x, y, c = _place()
    others = [2 * px + py for px, py in _other_chips(x, y)]
    own_first = jnp.stack([2 * x + y] + others + [c]).astype(jnp.int32)
    return list(_sum_chips(own_first, parts, got, after, name))


ADAMW_STEPS = 8


def _adamw(params, by_row, chip, window_step):
    n = len(params)
    rows, _, cols = by_row[0].shape
    block = lambda shape: pl.BlockSpec((shape[0] // ADAMW_STEPS, shape[1]), lambda i, c: (i, 0))
    assert all(a.shape[0] % (8 * ADAMW_STEPS) == 0 for p in params for a in p)

    def body(c_ref, *refs):
        w_hbm, g_ref, m_hbm, v_hbm = refs[4 * n:4 * n + 4]
        results, (ins_ref, outs_ref, sems) = refs[8 * n + 4:8 * n + 8], refs[8 * n + 8:]
        loads = [pltpu.make_async_copy(src.at[:, 0, :], ins_ref.at[k], sems.at[k])
                 for k, src in enumerate((w_hbm, m_hbm, v_hbm))]
        stores = [pltpu.make_async_copy(outs_ref.at[k], dst.at[:, 0, :], sems.at[3 + k]) for k, dst in enumerate(results)]
        first = pl.program_id(0) == 0

        @pl.when(first)
        def _():
            for cp in loads:
                cp.start()

        for a in range(n):
            w_ref, a_g_ref, m_ref, v_ref = refs[4 * a:4 * a + 4]
            outs = refs[4 * n + 4 + 4 * a:4 * n + 8 + 4 * a]
            g = a_g_ref[...]
            outs[0][...] = g
            outs[1][...], outs[2][...], outs[3][...] = _adamw_math(w_ref[...], g, m_ref[...], v_ref[...])

        @pl.when(first)
        def _():
            for cp in loads:
                cp.wait()
            for lo in range(0, cols, LANE):
                lanes = slice(lo, lo + LANE)
                g = g_ref[0:rows, lanes]
                for s in range(1, N_CHIPS):
                    g = jnp.where(c_ref[0] == s, g_ref[s * window_step:s * window_step + rows, lanes], g)
                outs_ref[0, :, lanes] = g
                outs_ref[1, :, lanes], outs_ref[2, :, lanes], outs_ref[3, :, lanes] = _adamw_math(
                    ins_ref[0, :, lanes], g, ins_ref[1, :, lanes], ins_ref[2, :, lanes])
            for cp in stores:
                cp.start()

        @pl.when(pl.program_id(0) == ADAMW_STEPS - 1)
        def _():
            for cp in stores:
                cp.wait()

    outs = pl.pallas_call(
        body, name="adamw_matrices",
        grid_spec=pltpu.PrefetchScalarGridSpec(
            num_scalar_prefetch=1, grid=(ADAMW_STEPS,),
            in_specs=[block(a.shape) for p in params for a in p] + [HBM, _const(by_row[1].shape), HBM, HBM],
            out_specs=[block(p[0].shape) for p in params for _ in range(4)] + [HBM] * 4,
            scratch_shapes=[pltpu.VMEM((3, rows, cols), F32), pltpu.VMEM((4, rows, cols), F32), _dma_sems(7)]),
        out_shape=[pltpu.HBM(p[0].shape, F32) for p in params for _ in range(4)] + [pltpu.HBM((rows, 1, cols), F32)] * 4,
        compiler_params=_params(48, dimension_semantics=_seq()),
    )(chip, *_hbm(*[a for p in params for a in p], *by_row))
    return [outs[4 * a:4 * a + 4] for a in range(n)], outs[4 * n:]


def _adamw_math(w, g, m, v):
    nm = ADAM_B1 * m + (1.0 - ADAM_B1) * g
    nv = ADAM_B2 * v + (1.0 - ADAM_B2) * (g * g)
    m_hat = nm / (1.0 - ADAM_B1 ** ADAM_STEP)
    v_hat = nv / (1.0 - ADAM_B2 ** ADAM_STEP)
    return -ADAM_LR * (m_hat / (jnp.sqrt(v_hat) + ADAM_EPS) + ADAM_WD * w), nm, nv


SMALL = (("meta_tokens", (N_META, D // N_CHIPS)), ("ln_in_g", (1, D)), ("ln_in_b", (1, D)), ("b_in", (1, D_IN)),
         ("w_gate_lr2", (GATE_RANK, GLA_HEADS * DK // N_CHIPS)), ("b_gate_lr2", (1, GLA_HEADS * DK)),
         ("attn_sinks", (1, SWA_HEADS)),
         ("gla_norm_g", (1, DV)), ("ln1_g", (1, D)), ("ln1_b", (1, D)), ("ln2_g", (1, D)), ("ln2_b", (1, D)))
ROW_META, ROW_B_IN, ROW_TAIL, ROW_WG2 = 0, 22, 25, 32
ROW_LN = dict(ln_in_g=16, ln_in_b=17, ln1_g=18, ln1_b=19, ln2_g=20, ln2_b=21)
TAIL_BG2, TAIL_SINKS, TAIL_GN, TAIL_LOSS = 0, 256, 256 + SWA_HEADS, 256 + SWA_HEADS + DV


def _adamw_small(place, packs, own, params):
    n = len(SMALL)

    def body(place_ref, packs_ref, own_ref, *refs):
        ins, outs, p_ref = refs[:3 * n], refs[3 * n:-1], refs[-1]
        me, c = place_ref[0], place_ref[1]
        total = jnp.where(me == 0, own_ref[...], packs_ref[0])
        for i in range(1, N_DEVICES):
            total = total + jnp.where(me == i, own_ref[...], packs_ref[i])
        p_ref[...] = total
        outs[4 * n][...] = total[ROW_TAIL:ROW_TAIL + 1, :]

        def mine(width, rows):
            part = lambda s: p_ref[rows, s * width:(s + 1) * width]
            return jnp.where(c == 0, part(0), jnp.where(c == 1, part(1), jnp.where(c == 2, part(2), part(3))))

        tail = lambda lo, width: p_ref[ROW_TAIL:ROW_TAIL + 1, lo:lo + width]
        grads = dict(
            meta_tokens=mine(D // N_CHIPS, slice(ROW_META, ROW_META + N_META)),
            b_in=jnp.concatenate([p_ref[ROW_B_IN:ROW_B_IN + 1, :], p_ref[ROW_B_IN + 1:ROW_B_IN + 2, :],
                                  p_ref[ROW_B_IN + 2:ROW_B_IN + 3, 0:D_IN - 2 * D]], axis=1),
            w_gate_lr2=mine(256 // N_CHIPS, slice(ROW_WG2, ROW_WG2 + 16)),
            b_gate_lr2=tail(TAIL_BG2, 256), attn_sinks=tail(TAIL_SINKS, SWA_HEADS), gla_norm_g=tail(TAIL_GN, DV),
            **{k: p_ref[r:r + 1, :] for k, r in ROW_LN.items()})
        for i, (name, _) in enumerate(SMALL):
            g = grads[name]
            outs[4 * i][...] = g
            outs[4 * i + 1][...], outs[4 * i + 2][...], outs[4 * i + 3][...] = _adamw_math(
                ins[3 * i][...], g, ins[3 * i + 1][...], ins[3 * i + 2][...])

    whole = lambda shape: pl.BlockSpec(shape, lambda i, c: (0,) * len(shape))
    outs = pl.pallas_call(
        body, name="adamw_small",
        grid_spec=pltpu.PrefetchScalarGridSpec(
            num_scalar_prefetch=1, grid=(1,),
            in_specs=[whole(packs.shape), whole(own.shape)] + [whole(s) for _, s in SMALL for _ in range(3)],
            out_specs=[whole(s) for _, s in SMALL for _ in range(4)] + [whole((1, D))],
            scratch_shapes=[pltpu.VMEM(own.shape, F32)]),
        out_shape=[pltpu.HBM(s, F32) for _, s in SMALL for _ in range(4)] + [pltpu.HBM((1, D), F32)],
        compiler_params=_params(16, dimension_semantics=_seq()),
    )(place, *_hbm(packs, own, *[a for p in params for a in p]))
    return [outs[4 * i:4 * i + 4] for i in range(n)], outs[4 * n]


def _small_pack(gr):
    names = ["meta_blk"] + list(ROW_LN) + ["b_in_p", "wg2_p", "bg2", "sinks", "gn", "loss"]
    gate_w = GLA_HEADS * DK

    def body(*refs):
        src, out = dict(zip(names, refs)), refs[-1]
        out[...] = jnp.zeros_like(out)
        out[ROW_META:ROW_META + N_META, :] = src["meta_blk"][META_OFF:CH, :]
        for k, r in ROW_LN.items():
            out[r:r + 1, :] = src[k][...]
        for j in range(-(-D_IN // D)):
            width = min(D, D_IN - j * D)
            out[ROW_B_IN + j:ROW_B_IN + j + 1, 0:width] = src["b_in_p"][:, j * D:j * D + width]
        tail = slice(ROW_TAIL, ROW_TAIL + 1)
        out[tail, TAIL_BG2:TAIL_BG2 + gate_w] = src["bg2"][...]
        out[tail, TAIL_SINKS:TAIL_SINKS + SWA_HEADS] = src["sinks"][:, 0:SWA_HEADS]
        out[tail, TAIL_GN:TAIL_GN + DV] = src["gn"][...]
        out[tail, TAIL_LOSS:TAIL_LOSS + 1] = src["loss"][:, 0:1]
        out[ROW_WG2:ROW_WG2 + GATE_RANK, 0:gate_w] = src["wg2_p"][0:GATE_RANK, :]

    arrays = [gr[k] for k in names]
    return pl.pallas_call(
        body, name="small_pack", grid=(1,),
        in_specs=[_acc(a.shape) for a in arrays], out_specs=_acc((SMALL_ROWS, D)),
        out_shape=pltpu.HBM((SMALL_ROWS, D), F32),
        compiler_params=_params(16, dimension_semantics=_seq()),
    )(*_hbm(*arrays))


BIG = ("w_in", "w_out", "w_g", "w_u", "w_d")


def kernel(x, meta_tokens, ln_in_g, ln_in_b, w_in, b_in, w_gate_lr2, b_gate_lr2, attn_sinks, gla_norm_g, w_out, ln1_g, ln1_b, w_ffn_gate, w_ffn_up, w_ffn_down, ln2_g, ln2_b, loss_target, m_meta_tokens, m_ln_in_g, m_ln_in_b, m_w_in, m_b_in, m_w_gate_lr2, m_b_gate_lr2, m_attn_sinks, m_gla_norm_g, m_w_out, m_ln1_g, m_ln1_b, m_w_ffn_gate, m_w_ffn_up, m_w_ffn_down, m_ln2_g, m_ln2_b, v_meta_tokens, v_ln_in_g, v_ln_in_b, v_w_in, v_b_in, v_w_gate_lr2, v_b_gate_lr2, v_attn_sinks, v_gla_norm_g, v_w_out, v_ln1_g, v_ln1_b, v_w_ffn_gate, v_w_ffn_up, v_w_ffn_down, v_ln2_g, v_ln2_b):
    chip = 2 * lax.axis_index("x") + lax.axis_index("y")

    halves = lambda a: a.reshape(2, a.shape[0] // 2, a.shape[1])
    r_in = SHARD_ROWS["w_in"]
    first = [halves(a) for a in (jnp.pad(w_in[0].T.astype(BF16), ((0, W_IN_WIN - r_in), (0, 0))), meta_tokens,
                                 w_gate_lr2[0])]
    rest = [halves(a) for a in (w_out[0].astype(BF16), w_ffn_gate[0].T.astype(BF16), w_ffn_up[0].T.astype(BF16),
                                w_ffn_down[0].astype(BF16))]
    lands = lambda arrs: [(N_CHIPS,) + a.shape for a in arrs]
    first_handle, first_token = _ici_start("gather", first, lands(first), [], "gather_first_start")
    rest_handle, token = _ici_start("gather", rest, lands(rest), [first_token], "gather_rest_start")
    own_slab = lambda got, shards: [lax.dynamic_update_index_in_dim(g, s, chip, axis=0) for g, s in zip(got, shards)]
    fetching = {}

    def fetch_first(after):
        shards, (forwarding,), _ = _gather_wait_forward(first_handle, [len(first)], after, "gather_first_wait")
        g_in, g_meta, g_wg2 = own_slab(_forward_wait(forwarding, [], "gather_first_forward_wait"), shards)
        w_in_windows = g_in.reshape(N_CHIPS, W_IN_WIN, D)
        meta_full = jnp.concatenate([g_meta[s].reshape(N_META, -1) for s in range(N_CHIPS)], axis=1)
        wg2_full = jnp.concatenate([g_wg2[s].reshape(w_gate_lr2.shape[1], -1) for s in range(N_CHIPS)], axis=1)
        return w_in_windows, meta_full, wg2_full

    def fetch_rest(after):
        shards, (w_out_forwarding, fetching["handle"]), forward_token = _gather_wait_forward(
            rest_handle, [1, len(rest) - 1], after, "gather_rest_wait")
        g_out, = own_slab(_forward_wait(w_out_forwarding, [], "gather_w_out_forward_wait"), shards[:1])
        fetching["shards"] = shards[1:]
        return g_out.reshape(-1, D), forward_token

    def fetch_ffn(after):
        got = _forward_wait(fetching["handle"], after, "gather_ffn_forward_wait")
        return [g.reshape(-1, D) for g in own_slab(got, fetching["shards"])]

    sent = {}
    split = lambda grads: [g.reshape(N_CHIPS, 2, -1, D) for g in grads]

    def exchange(key, grads):
        grads = split(grads)
        sent[key + "_halves"], exchange_token = _ici_start(
            "sibling", grads, [(N_CHIPS,) + a.shape[2:] for a in grads], [], "sibling_" + key + "_start")
        return exchange_token

    def ship(key, after):
        grads, fetched = _ici_wait("sibling", sent[key + "_halves"], after, "sibling_" + key + "_wait")
        parts = _chip_partials(grads, fetched, [BF16] * len(grads), "add_halves_" + key)
        sent[key], ship_token = _ici_start("scatter", parts, [p.shape for p in parts], [], "scatter_" + key + "_start")
        return ship_token

    dx, gr = _local_step(
        x[0], loss_target[0], ln_in_g, ln_in_b, b_in[0], b_gate_lr2[0], attn_sinks[0], gla_norm_g[0], ln1_g[0],
        ln1_b[0], ln2_g[0], ln2_b[0], token, fetch_first, fetch_rest, fetch_ffn,
        lambda g: exchange("ffn", [g[k] for k in BIG[1:]]), lambda after: ship("ffn", after),
        lambda g: exchange("w_in", [g]), lambda after: ship("w_in", after))
    ffn_parts, ffn_got = _ici_wait("scatter", sent["ffn"], [dx], "scatter_ffn_wait")
    join_handle, token = _join_start(_chip_sums(ffn_parts, ffn_got, [], "sum_chips_ffn"), "join_ffn_start")
    w_in_parts, w_in_got = _ici_wait("scatter", sent["w_in"], [token], "scatter_w_in_wait")

    small_handle, token = _small_start(_small_pack(gr), [w_in_got[0]])
    w_in_joined = _join_halves(_chip_sums(w_in_parts, w_in_got, [token], "sum_chips_w_in"), "join_w_in")
    red = [f.reshape(2 * f.shape[1], D) for f in w_in_joined + _join_wait(join_handle, w_in_joined, "join_ffn_wait")]

    big_g = dict(zip(BIG, red))
    weights = dict(meta_tokens=meta_tokens, ln_in_g=ln_in_g, ln_in_b=ln_in_b, w_in=w_in, b_in=b_in,
                   w_gate_lr2=w_gate_lr2, b_gate_lr2=b_gate_lr2, attn_sinks=attn_sinks, gla_norm_g=gla_norm_g,
                   w_out=w_out, ln1_g=ln1_g, ln1_b=ln1_b, w_ffn_gate=w_ffn_gate, w_ffn_up=w_ffn_up,
                   w_ffn_down=w_ffn_down, ln2_g=ln2_g, ln2_b=ln2_b)
    m_in = dict(meta_tokens=m_meta_tokens, ln_in_g=m_ln_in_g, ln_in_b=m_ln_in_b, w_in=m_w_in, b_in=m_b_in,
                w_gate_lr2=m_w_gate_lr2, b_gate_lr2=m_b_gate_lr2, attn_sinks=m_attn_sinks, gla_norm_g=m_gla_norm_g,
                w_out=m_w_out, ln1_g=m_ln1_g, ln1_b=m_ln1_b, w_ffn_gate=m_w_ffn_gate, w_ffn_up=m_w_ffn_up,
                w_ffn_down=m_w_ffn_down, ln2_g=m_ln2_g, ln2_b=m_ln2_b)
    v_in = dict(meta_tokens=v_meta_tokens, ln_in_g=v_ln_in_g, ln_in_b=v_ln_in_b, w_in=v_w_in, b_in=v_b_in,
                w_gate_lr2=v_w_gate_lr2, b_gate_lr2=v_b_gate_lr2, attn_sinks=v_attn_sinks, gla_norm_g=v_gla_norm_g,
                w_out=v_w_out, ln1_g=v_ln1_g, ln1_b=v_ln1_b, w_ffn_gate=v_w_ffn_gate, w_ffn_up=v_w_ffn_up,
                w_ffn_down=v_w_ffn_down, ln2_g=v_ln2_g, ln2_b=v_ln2_b)
    names = list(weights)
    big_names = ("w_in", "w_out", "w_ffn_gate", "w_ffn_up", "w_ffn_down")

    grads, delta, new_m, new_v = {}, {}, {}, {}
    flips = [(lambda a: a.T) if kk in ("w_g", "w_u") else (lambda a: a) for kk in BIG[1:]]
    by_row = lambda a: jnp.transpose(a, (2, 0, 1))
    updated, updated_w_in = _adamw(
        [(flip(weights[k][0]), big_g[kk], flip(m_in[k][0]), flip(v_in[k][0]))
         for k, kk, flip in zip(big_names[1:], BIG[1:], flips)],
        (by_row(w_in), big_g["w_in"], by_row(m_w_in), by_row(v_w_in)), chip.astype(jnp.int32).reshape(1), r_in % BF16_ROWS)
    for k, flip, results in zip(big_names[1:], flips, updated):
        grads[k], delta[k], new_m[k], new_v[k] = (flip(t)[None] for t in results)
    grads["w_in"], delta["w_in"], new_m["w_in"], new_v["w_in"] = (jnp.transpose(t, (1, 2, 0)) for t in updated_w_in)
    small_in = [tuple(src[k].reshape(shape) for src in (weights, m_in, v_in)) for k, shape in SMALL]
    place = jnp.stack([2 * chip + lax.axis_index("c"), chip]).astype(jnp.int32)
    small_own, small_all = _small_wait(small_handle, [updated[0][0]])
    small_out, tail_row = _adamw_small(place, small_all, small_own, small_in)
    for (k, _), results in zip(SMALL, small_out):
        grads[k], delta[k], new_m[k], new_v[k] = (r.reshape(weights[k].shape) for r in results)

    return (tail_row[0, TAIL_LOSS], dx[None], *[grads[k] for k in names], *[delta[k] for k in names], *[new_m[k] for k in names],
            *[new_v[k] for k in names])
```

```python
import jax
import jax.numpy as jnp
from jax import lax
from jax.experimental import pallas as pl
from jax.experimental.pallas import tpu as pltpu

F32 = jnp.float32
BF16 = jnp.bfloat16
MESH = pl.DeviceIdType.MESH

D = 1024
SEQ = 4096
N_META = 16
SWA_HEADS, SWA_KV_HEADS, DH = 8, 2, 64
WINDOW = 128
GLA_HEADS, DK, DV = 4, 64, 128
GLA_TAU = 16.0
CH = 64
D_FF = 2816
D_IN = 2320
LN_EPS = 1e-5
RMS_EPS = 1e-6
ALPHA = 2.0 ** 0.25
NEG = -1e30
ADAM_LR, ADAM_B1, ADAM_B2, ADAM_EPS, ADAM_WD, ADAM_STEP = 0.001, 0.9, 0.999, 1e-8, 0.01, 10
O_QS, O_KS, O_VS, O_QG, O_KG, O_VG, O_RG, O_LR = 0, 512, 640, 768, 1024, 1280, 1792, 2304

LANE = 128
BLK = WINDOW
GATE_RANK = 16
D_IN_P = D_IN + LANE - GATE_RANK
META_OFF = CH - N_META
HEAD_POS = (0, 4, 1, 5, 2, 6, 3, 7)
LN_ROWS = 512
TOKEN = (8, LANE)
N_CHIPS = 4
SHARD_ROWS = dict(w_in=D_IN // N_CHIPS, w_out=D // N_CHIPS, w_g=D_FF // N_CHIPS, w_u=D_FF // N_CHIPS,
                  w_d=D_FF // N_CHIPS)
SMALL_ROWS = 48
BF16_ROWS = 16
W_IN_WIN = -(-SHARD_ROWS["w_in"] // (2 * BF16_ROWS)) * 2 * BF16_ROWS
W_IN_STARTS = tuple(s * SHARD_ROWS["w_in"] // BF16_ROWS * BF16_ROWS for s in range(N_CHIPS))
VMEM_CAP_MB = 64
VMEM_SPARE_MB = 6


def _lp():
    return SEQ + BLK


def _row_tile(cap):
    lp = _lp()
    return max(t for t in range(16, cap + 1, 16) if lp % t == 0)


def _params(vmem_mb, **kw):
    assert vmem_mb <= VMEM_CAP_MB - VMEM_SPARE_MB
    return pltpu.CompilerParams(vmem_limit_bytes=vmem_mb << 20, **kw)


def _seq(n=1):
    return ("arbitrary",) * n


def _const(shape):
    return pl.BlockSpec(shape, lambda *_: (0,) * len(shape), pipeline_mode=pl.Buffered(1))


def _acc(shape):
    return pl.BlockSpec(shape, lambda *_: (0,) * len(shape))


def _rows(tm, width):
    return pl.BlockSpec((tm, width), lambda i: (i, 0))


def _dot(a, b):
    return jnp.dot(a.astype(BF16), b.astype(BF16), preferred_element_type=F32)


def _dot_nt(a, b):
    return lax.dot_general(a.astype(BF16), b.astype(BF16), (((1,), (1,)), ((), ())), preferred_element_type=F32)


def _dot_tn(a, b):
    return lax.dot_general(a.astype(BF16), b.astype(BF16), (((0,), (0,)), ((), ())), preferred_element_type=F32)


def _dot_exact(a, b):
    return jnp.dot(a, b, precision=lax.Precision.HIGHEST, preferred_element_type=F32)


def _ln_stats(x):
    mu = jnp.mean(x, axis=-1, keepdims=True)
    xc = x - mu
    rstd = lax.rsqrt(jnp.mean(xc * xc, axis=-1, keepdims=True) + LN_EPS)
    return xc * rstd, rstd


def _ln_bwd(dy, xhat, rstd, g):
    dxh = dy * g
    return rstd * (dxh - jnp.mean(dxh, axis=-1, keepdims=True) - xhat * jnp.mean(dxh * xhat, axis=-1, keepdims=True))


def _sigmoid(x):
    return 1.0 / (1.0 + jnp.exp(-x))


def _iota(shape, dim):
    return lax.broadcasted_iota(jnp.int32, shape, dim)


def _hbm(*arrays):
    return tuple(pltpu.with_memory_space_constraint(a, pltpu.HBM) for a in arrays)


def _ln_in_fwd_real(x, g, b, token):
    tr = min(LN_ROWS, SEQ)

    def body(x_ref, g_ref, b_ref, token_ref, h_ref):
        xhat, _ = _ln_stats(x_ref[...])
        h_ref[...] = xhat * g_ref[...] + b_ref[...]

    return pl.pallas_call(
        body, name="ln_in_fwd", grid=(SEQ // tr,),
        in_specs=[_rows(tr, D), _const((1, D)), _const((1, D)), _const(TOKEN)],
        out_specs=_rows(tr, D),
        out_shape=pltpu.HBM((_lp(), D), F32),
        compiler_params=_params(32, dimension_semantics=_seq()),
    )(*_hbm(x, g, b), token)


def _ln_in_fwd_meta(h_real, meta_ext, g, b):
    def meta_body(m_ref, g_ref, b_ref, real_ref, h_ref):
        xhat, _ = _ln_stats(m_ref[...])
        h_ref[...] = xhat * g_ref[...] + b_ref[...]

    return pl.pallas_call(
        meta_body, name="ln_in_fwd_meta", grid=(1,),
        in_specs=[_const((BLK, D)), _const((1, D)), _const((1, D)), pl.BlockSpec(memory_space=pl.ANY)],
        out_specs=pl.BlockSpec((BLK, D), lambda i: (SEQ // BLK, 0)),
        out_shape=pltpu.HBM((_lp(), D), F32),
        input_output_aliases={3: 0},
        compiler_params=_params(16, dimension_semantics=_seq()),
    )(*_hbm(meta_ext, g, b, h_real))


def _in_proj(h0, w_in_windows, b_in_p, wg2_p, bg2):
    tm = _row_tile(384)
    lp = _lp()
    widths = (512, 128, 128, 256, 256, 512, 512, 128)
    offs = (O_QS, O_KS, O_VS, O_QG, O_KG, O_VG, O_RG, O_LR)
    shard = SHARD_ROWS["w_in"]

    def body(h_ref, win_ref, b_ref, wg2_ref, bg2_ref, *outs):
        w_ref = outs[9]

        @pl.when(pl.program_id(0) == 0)
        def _():
            for s in range(N_CHIPS):
                w_ref[shard * s:shard * (s + 1), :] = win_ref[s, 0:shard, :]
            w_ref[D_IN:D_IN_P, :] = jnp.zeros((D_IN_P - D_IN, D), BF16)

        proj = _dot_nt(h_ref[...], w_ref[...]) + b_ref[...]
        for pos, h in enumerate(HEAD_POS):
            outs[0][:, pos * DH:(pos + 1) * DH] = proj[:, O_QS + h * DH:O_QS + (h + 1) * DH]
        for o_ref, off, wd in zip(outs[1:8], offs[1:], widths[1:]):
            o_ref[...] = proj[:, off:off + wd]
        outs[8][...] = _dot(proj[:, O_LR:O_LR + LANE], wg2_ref[...]) + bg2_ref[...]

    return pl.pallas_call(
        body, name="in_proj", grid=(lp // tm,),
        in_specs=[_rows(tm, D), _const(w_in_windows.shape), _const((1, D_IN_P)), _const((LANE, 256)), _const((1, 256))],
        out_specs=[_rows(tm, w) for w in widths] + [_rows(tm, 256), _acc((D_IN_P, D))],
        out_shape=[pltpu.HBM((lp, w), F32) for w in widths] + [pltpu.HBM((lp, 256), F32), pltpu.HBM((D_IN_P, D), BF16)],
        compiler_params=_params(48, dimension_semantics=_seq()),
    )(*_hbm(h0, w_in_windows, b_in_p, wg2_p, bg2))


def _swa_masks(n):
    nb = SEQ // BLK
    is_meta = n == nb
    ri = _iota((BLK, BLK), 0)
    cj = _iota((BLK, BLK), 1)
    meta_col = ((cj >= META_OFF) & (cj < CH)).astype(jnp.int32)
    meta_q = meta_col * ((cj <= ri) & (ri < CH)).astype(jnp.int32)
    valid_m = jnp.where(is_meta, meta_q, meta_col) > 0
    dist_m = jnp.where(is_meta, ri - cj, n * BLK + ri + CH - cj).astype(F32)
    valid_p = jnp.where((n >= 1) & (n < nb), (cj > ri).astype(jnp.int32), 0) > 0
    dist_p = (ri + BLK - cj).astype(F32)
    valid_c = jnp.where(n < nb, (cj <= ri).astype(jnp.int32), 0) > 0
    dist_c = (ri - cj).astype(F32)
    return (dist_m, dist_p, dist_c), (valid_m, valid_p, valid_c)


def _swa_bias(n):
    dists, valids = _swa_masks(n)
    return (jnp.concatenate([-d for d in dists], axis=1),
            jnp.concatenate([jnp.where(v, 0.0, NEG) for v in valids], axis=1))


def _swa_half(ref, pos, scale=1.0):
    col = ref[:, (pos // 2) * LANE:(pos // 2 + 1) * LANE]
    lane = _iota((BLK, LANE), 1)
    mine = lane < DH if pos % 2 == 0 else lane >= DH
    return jnp.where(mine, col * scale, 0.0).astype(BF16)


def _swa_merge(even, odd):
    return jnp.where(_iota((BLK, LANE), 1) < DH, even, odd)


def _swa_softmax(t, sink):
    m = jnp.maximum(jnp.max(t, axis=-1, keepdims=True), sink)
    e = jnp.exp(t - m)
    e_sink = jnp.exp(sink - m)
    inv = 1.0 / (jnp.sum(e, axis=-1, keepdims=True) + e_sink)
    return e * inv, e_sink * inv


def _swa_kv_specs(width):
    nb = SEQ // BLK
    return [pl.BlockSpec((BLK, width), lambda n: (nb, 0)),
            pl.BlockSpec((BLK, width), lambda n: (jnp.clip(n - 1, 0, nb - 1), 0)),
            pl.BlockSpec((BLK, width), lambda n: (jnp.minimum(n, nb), 0))]


def _swa_fwd(sinks, qs, ks, vs):
    nb = SEQ // BLK
    heads = range(SWA_HEADS)

    def body(sink_ref, q_ref, km_ref, kp_ref, kc_ref, vm_ref, vp_ref, vc_ref, o_ref):
        negdist, maskbias = _swa_bias(pl.program_id(0))
        k_all = jnp.concatenate([km_ref[...], kp_ref[...], kc_ref[...]], axis=0).astype(BF16)
        v_all = jnp.concatenate([vm_ref[...], vp_ref[...], vc_ref[...]], axis=0).astype(BF16)
        q = [_swa_half(q_ref, pos, DH ** -0.5) for pos in heads]
        t = [_dot_nt(q[pos], k_all) + (2.0 ** -(HEAD_POS[pos] + 1) * negdist + maskbias) for pos in heads]
        p = [_swa_softmax(t[pos], sink_ref[HEAD_POS[pos]])[0].astype(BF16) for pos in heads]
        o = [_dot(p[pos], v_all) for pos in heads]
        for col in range(SWA_HEADS // 2):
            o_ref[:, col * LANE:(col + 1) * LANE] = _swa_merge(o[2 * col], o[2 * col + 1])

    kvw = SWA_KV_HEADS * DH
    return pl.pallas_call(
        body, name="swa_fwd", grid=(nb + 1,),
        in_specs=[pl.BlockSpec(memory_space=pltpu.SMEM), _rows(BLK, SWA_HEADS * DH)] + _swa_kv_specs(kvw) + _swa_kv_specs(kvw),
        out_specs=_rows(BLK, SWA_HEADS * DH),
        out_shape=pltpu.HBM((_lp(), SWA_HEADS * DH), F32),
        compiler_params=_params(16, dimension_semantics=_seq()),
    )(sinks, *_hbm(qs, ks, ks, ks, vs, vs, vs))


GLA_PER_STEP = BLK // CH


def _gla_block(s):
    nb = SEQ // BLK
    return jnp.where(s == 0, nb, s - 1)


def _gla_rowmask(s):
    ri = _iota((BLK, 1), 0)
    m = jnp.where(s == 0, ((ri >= META_OFF) & (ri < CH)).astype(jnp.int32), 1)
    return (m > 0).astype(F32) + jnp.zeros((BLK, 1), F32)


def _gla_chunk_masks():
    r, c = _iota((BLK, BLK), 0), _iota((BLK, BLK), 1)
    same = ((r < CH) & (c < CH)) | ((r >= CH) & (c >= CH))
    return same & (r >= c), same & (r <= c), same


def _gla_decay(z, rmask):
    log_g = (jnp.minimum(z, 0.0) - jnp.log1p(jnp.exp(-jnp.abs(z)))) * (rmask / GLA_TAU)
    lower, _, same = _gla_chunk_masks()
    return _dot_exact(lower.astype(F32), log_g), _dot_exact(same.astype(F32), log_g)


def _gla_slices(c, h):
    return slice(c * CH, (c + 1) * CH), slice(h * DK, (h + 1) * DK), slice(h * DV, (h + 1) * DV)


def _gla_fwd(qg, kg, vg, z):
    steps = SEQ // BLK + 1
    kw, vw = GLA_HEADS * DK, GLA_HEADS * DV
    pairs = [(c, h) for c in range(GLA_PER_STEP) for h in range(GLA_HEADS)]

    def body(q_ref, k_ref, v_ref, z_ref, o_ref, st_ref, st):
        s = pl.program_id(0)

        @pl.when(s == 0)
        def _():
            st[...] = jnp.zeros_like(st)

        rmask = _gla_rowmask(s)
        b, b_last = _gla_decay(z_ref[...], rmask)
        q = q_ref[...] * (rmask * DK ** -0.5)
        k = k_ref[...] * rmask
        v = v_ref[...] * rmask
        qe = q * jnp.exp(b)
        ke = k * jnp.exp(-b)
        kd = k * jnp.exp(b_last - b)
        e_last = jnp.exp(b_last)
        causal = _iota((CH, CH), 0) >= _iota((CH, CH), 1)
        a, upd, intra = {}, {}, {}
        for c, h in pairs:
            rows, ks, vs_ = _gla_slices(c, h)
            a[c, h] = jnp.where(causal, _dot_nt(qe[rows, ks], ke[rows, ks]), 0.0)
            upd[c, h] = _dot_tn(v[rows, vs_], kd[rows, ks])
        for c, h in pairs:
            rows, ks, vs_ = _gla_slices(c, h)
            intra[c, h] = _dot(a[c, h], v[rows, vs_])
        state = st[...]
        for c in range(GLA_PER_STEP):
            st_ref[0, c] = state
            for h in range(GLA_HEADS):
                rows, ks, vs_ = _gla_slices(c, h)
                o_ref[rows, vs_] = intra[c, h] + _dot_nt(qe[rows, ks], state[:, ks])
            state = state * e_last[c * CH:c * CH + 1] + jnp.concatenate([upd[c, h] for h in range(GLA_HEADS)], axis=1)
        st[...] = state

    blk = lambda w: pl.BlockSpec((BLK, w), lambda s: (_gla_block(s), 0))
    return pl.pallas_call(
        body, name="gla_fwd", grid=(steps,),
        in_specs=[blk(kw), blk(kw), blk(vw), blk(kw)],
        out_specs=[blk(vw), pl.BlockSpec((1, GLA_PER_STEP, DV, kw), lambda s: (s, 0, 0, 0))],
        out_shape=[pltpu.HBM((_lp(), vw), F32), pltpu.HBM((steps, GLA_PER_STEP, DV, kw), F32)],
        scratch_shapes=[pltpu.VMEM((DV, kw), F32)],
        compiler_params=_params(16, dimension_semantics=_seq()),
    )(*_hbm(qg, kg, vg, z))


def _post_mix(o_s, o_gla, r_g, h0, gn4, w_out, g1, b1, token):
    tm = _row_tile(384)
    lp = _lp()

    def body(os_ref, og_ref, r_ref, h0_ref, gn_ref, w_ref, g_ref, b_ref, token_ref, o_ref, pre_ref, h1_ref):
        for pos, h in enumerate(HEAD_POS):
            o_ref[:, h * DH:(h + 1) * DH] = os_ref[:, pos * DH:(pos + 1) * DH].astype(BF16)
        for h in range(GLA_HEADS):
            hs = slice(h * DV, (h + 1) * DV)
            xg = og_ref[:, hs]
            n = xg * lax.rsqrt(jnp.mean(xg * xg, axis=-1, keepdims=True) + RMS_EPS) * gn_ref[...]
            r = r_ref[:, hs]
            o_ref[:, 512 + h * DV:512 + (h + 1) * DV] = (n * (r * _sigmoid(r))).astype(BF16)
        pre = ALPHA * h0_ref[...] + _dot(o_ref[...], w_ref[...])
        pre_ref[...] = pre
        xhat, _ = _ln_stats(pre)
        h1_ref[...] = xhat * g_ref[...] + b_ref[...]

    return pl.pallas_call(
        body, name="post_mix", grid=(lp // tm,),
        in_specs=[_rows(tm, 512), _rows(tm, 512), _rows(tm, 512), _rows(tm, D), _const((1, DV)), _const((D, D)),
                  _const((1, D)), _const((1, D)), _const(TOKEN)],
        out_specs=[_rows(tm, D), _rows(tm, D), _rows(tm, D)],
        out_shape=[pltpu.HBM((lp, D), BF16), pltpu.HBM((lp, D), F32),
                   pltpu.HBM((lp, D), F32)],
        compiler_params=_params(32, dimension_semantics=_seq()),
    )(*_hbm(o_s, o_gla, r_g, h0, gn4, w_out, g1, b1), token)


def _ffn_fwd_loss_bwd(h1, wg_t, wu_t, wd, target, g2, b2):
    lp = _lp()
    tm = max(t for t in range(BLK, 384 + 1, BLK) if lp % t == 0)
    steps = lp // tm
    last_blk = SEQ // BLK - 1
    half = D_FF // 2
    n_t = tm // BLK

    def body(*refs):
        h_ref = refs[0]
        t_refs = refs[4:4 + n_t]
        g2_ref, b2_ref, a_ref, dgate_ref, dup_ref, dp_ref, loss_ref, dg_ref, db_ref, g_s, u_s, acc = refs[4 + n_t:-4]
        wg_ref, wu_ref, wd_ref, sems = refs[-4:]
        i = pl.program_id(0)
        fetch = lambda k, j: pltpu.make_async_copy(refs[1 + k].at[pl.ds(j * half, half), :],
                                                   refs[-4 + k].at[pl.ds(j * half, half), :], sems.at[3 * j + k])

        def ready(k, j):
            @pl.when(i == 0)
            def _():
                fetch(k, j).wait()

        @pl.when(i == 0)
        def _():
            for j in range(2):
                for k in range(3):
                    fetch(k, j).start()
            acc[...] = jnp.zeros_like(acc)
            dg_ref[...] = jnp.zeros_like(dg_ref)
            db_ref[...] = jnp.zeros_like(db_ref)

        h = h_ref[...]
        hb = h.astype(BF16)
        pre = ALPHA * h
        for j in range(2):
            cols = slice(j * half, (j + 1) * half)
            ready(0, j)
            g = _dot_nt(hb, wg_ref[cols, :])
            ready(1, j)
            u = _dot_nt(hb, wu_ref[cols, :])
            g_s[:, cols] = g
            u_s[:, cols] = u
            ready(2, j)
            pre = pre + _dot(g * _sigmoid(g) * u, wd_ref[cols, :])
        xhat, rstd = _ln_stats(pre)
        real = i * tm + _iota((tm, 1), 0) < SEQ
        target_rows = jnp.concatenate([t[...] for t in t_refs], axis=0)
        diff = jnp.where(real, xhat * g2_ref[...] + b2_ref[...] - target_rows, 0.0)
        acc[...] += jnp.sum(diff * diff, axis=0, keepdims=True)
        dy = diff * (1.0 / D)
        dpre = _ln_bwd(dy, xhat, rstd, g2_ref[...])
        dp_ref[...] = dpre
        dg_ref[...] += jnp.sum(dy * xhat, axis=0, keepdims=True)
        db_ref[...] += jnp.sum(dy, axis=0, keepdims=True)
        dpb = dpre.astype(BF16)
        for j in range(2):
            cols = slice(j * half, (j + 1) * half)
            g, u = g_s[:, cols], u_s[:, cols]
            sg = _sigmoid(g)
            silu = g * sg
            da = _dot_nt(dpb, wd_ref[cols, :])
            a_ref[:, cols] = (silu * u).astype(BF16)
            dgate_ref[:, cols] = (da * u * (sg * (1.0 + g * (1.0 - sg)))).astype(BF16)
            dup_ref[:, cols] = (da * silu).astype(BF16)

        @pl.when(i == steps - 1)
        def _():
            loss_ref[...] = jnp.zeros_like(loss_ref) + (0.5 / D) * jnp.sum(acc[...], axis=1, keepdims=True)

    t_spec = lambda k: pl.BlockSpec((BLK, D), lambda i: (jnp.minimum(i * n_t + k, last_blk), 0))
    return pl.pallas_call(
        body, name="ffn_fwd_loss_bwd", grid=(steps,),
        in_specs=[_rows(tm, D)] + [pl.BlockSpec(memory_space=pl.ANY)] * 3
        + [t_spec(k) for k in range(n_t)] + [_const((1, D)), _const((1, D))],
        out_specs=[_rows(tm, D_FF), _rows(tm, D_FF), _rows(tm, D_FF), _rows(tm, D), _acc((1, LANE)), _acc((1, D)),
                   _acc((1, D))],
        out_shape=[pltpu.HBM((lp, D_FF), BF16)] * 3 + [pltpu.HBM((lp, D), F32), pltpu.HBM((1, LANE), F32),
                                                         pltpu.HBM((1, D), F32), pltpu.HBM((1, D), F32)],
        scratch_shapes=[pltpu.VMEM((tm, D_FF), F32), pltpu.VMEM((tm, D_FF), F32), pltpu.VMEM((1, D), F32)]
        + [pltpu.VMEM((D_FF, D), BF16)] * 3 + [pltpu.SemaphoreType.DMA((6,))],
        compiler_params=_params(58, dimension_semantics=_seq()),
    )(*_hbm(h1, wg_t, wu_t, wd, *[target] * n_t, g2, b2))


def _ffn_out_bwd(dpre2, dgate, dup, pre1, wg_t, wu_t, g1, w_out, o_gla, r_g, gn4):
    tm = _row_tile(384)
    lp = _lp()

    def body(dp_ref, dg_ref, du_ref, p1_ref, wg_hbm, wu_hbm, g1_ref, w_hbm, og_ref, r_ref, gn_ref,
             dp1_ref, dg1_ref, db1_ref, dos_ref, dog_ref, dr_ref, dgn_ref, wg_ref, wu_ref, w_ref, sems):
        first = pl.program_id(0) == 0
        fetches = [pltpu.make_async_copy(src, dst, sems.at[k])
                   for k, (src, dst) in enumerate(((wg_hbm, wg_ref), (wu_hbm, wu_ref), (w_hbm, w_ref)))]

        def ready(k):
            @pl.when(first)
            def _():
                fetches[k].wait()

        @pl.when(first)
        def _():
            for cp in fetches:
                cp.start()
            for acc_ref in (dg1_ref, db1_ref, dgn_ref):
                acc_ref[...] = jnp.zeros_like(acc_ref)

        ready(0)
        dh1 = ALPHA * dp_ref[...] + _dot(dg_ref[...], wg_ref[...])
        ready(1)
        dh1 = dh1 + _dot(du_ref[...], wu_ref[...])
        xhat, rstd1 = _ln_stats(p1_ref[...])
        dpre1 = _ln_bwd(dh1, xhat, rstd1, g1_ref[...])
        dp1_ref[...] = dpre1
        dg1_ref[...] += jnp.sum(dh1 * xhat, axis=0, keepdims=True)
        db1_ref[...] += jnp.sum(dh1, axis=0, keepdims=True)

        ready(2)
        do = _dot_nt(dpre1, w_ref[...])
        for pos, h in enumerate(HEAD_POS):
            dos_ref[:, pos * DH:(pos + 1) * DH] = do[:, h * DH:(h + 1) * DH]
        gn = gn_ref[...]
        for h in range(GLA_HEADS):
            hs = slice(h * DV, (h + 1) * DV)
            xg = og_ref[:, hs]
            rstd = lax.rsqrt(jnp.mean(xg * xg, axis=-1, keepdims=True) + RMS_EPS)
            nx = xg * rstd
            r = r_ref[:, hs]
            sr = _sigmoid(r)
            d_o = do[:, 512 + h * DV:512 + (h + 1) * DV]
            dr_ref[:, hs] = d_o * (nx * gn) * (sr * (1.0 + r * (1.0 - sr)))
            dn = d_o * (r * sr)
            dgn_ref[...] += jnp.sum(dn * nx, axis=0, keepdims=True)
            dnx = dn * gn
            dog_ref[:, hs] = rstd * (dnx - nx * jnp.mean(dnx * nx, axis=-1, keepdims=True))

    return pl.pallas_call(
        body, name="ffn_out_bwd", grid=(lp // tm,),
        in_specs=[_rows(tm, D), _rows(tm, D_FF), _rows(tm, D_FF), _rows(tm, D)] + [pl.BlockSpec(memory_space=pl.ANY)] * 2
        + [_const((1, D)), pl.BlockSpec(memory_space=pl.ANY), _rows(tm, 512), _rows(tm, 512), _const((1, DV))],
        out_specs=[_rows(tm, D), _acc((1, D)), _acc((1, D)), _rows(tm, 512), _rows(tm, 512), _rows(tm, 512),
                   _acc((1, DV))],
        out_shape=[pltpu.HBM((lp, D), F32), pltpu.HBM((1, D), F32), pltpu.HBM((1, D), F32)]
        + [pltpu.HBM((lp, 512), F32)] * 3 + [pltpu.HBM((1, DV), F32)],
        scratch_shapes=[pltpu.VMEM((D_FF, D), BF16), pltpu.VMEM((D_FF, D), BF16), pltpu.VMEM((D, D), BF16),
                        pltpu.SemaphoreType.DMA((3,))],
        compiler_params=_params(48, dimension_semantics=_seq()),
    )(*_hbm(dpre2, dgate, dup, pre1, wg_t, wu_t, g1, w_out, o_gla, r_g, gn4))


def _atb(a, b, name, token=None, windows=None):
    lp = _lp()
    tm = _row_tile(1408)
    n, w = a.shape[1], b.shape[1]
    bw = 512 if n * w * 4 > (4 << 20) else w
    tokens = [] if token is None else [token]
    steps = lp // tm

    def body(a_ref, b_ref, *rest):
        o_ref, acc_ref = rest[len(tokens):] if windows else (rest[-1], rest[-1])

        @pl.when(pl.program_id(1) == 0)
        def _():
            acc_ref[...] = jnp.zeros_like(acc_ref)

        acc_ref[...] += _dot_tn(a_ref[...], b_ref[...])

        if windows:
            @pl.when(pl.program_id(1) == steps - 1)
            def _():
                for s, start in enumerate(windows[0]):
                    o_ref[s] = acc_ref[start:start + windows[1], :]

    if windows:
        count, height = len(windows[0]), windows[1]
        out_spec, out_shape = pl.BlockSpec((count, height, bw), lambda j, k: (0, 0, j)), (count, height, w)
    else:
        out_spec, out_shape = pl.BlockSpec((n, bw), lambda j, k: (0, j)), (n, w)
    return pl.pallas_call(
        body, name=name, grid=(w // bw, steps),
        in_specs=[pl.BlockSpec((tm, n), lambda j, k: (k, 0)), pl.BlockSpec((tm, bw), lambda j, k: (k, j))]
        + [_const(TOKEN)] * len(tokens),
        out_specs=out_spec, out_shape=pltpu.HBM(out_shape, F32),
        scratch_shapes=[pltpu.VMEM((n, bw), F32)] if windows else [],
        compiler_params=_params(48, dimension_semantics=_seq(2)),
    )(*_hbm(a, b), *tokens)


def _gla_bwd(qg, kg, vg, z, do_gla, st_all, token):
    steps = SEQ // BLK + 1
    kw, vw = GLA_HEADS * DK, GLA_HEADS * DV
    pairs = [(c, h) for c in range(GLA_PER_STEP) for h in range(GLA_HEADS)]
    heads = range(GLA_HEADS)

    def body(q_ref, k_ref, v_ref, z_ref, do_ref, st_ref, token_ref, dq_ref, dk_ref, dv_ref, dz_ref, dst):
        @pl.when(pl.program_id(0) == 0)
        def _():
            dst[...] = jnp.zeros_like(dst)

        rmask = _gla_rowmask(steps - 1 - pl.program_id(0))
        zz = z_ref[...]
        b, b_last = _gla_decay(zz, rmask)
        e_b, e_nb, e_kd, e_last = jnp.exp(b), jnp.exp(-b), jnp.exp(b_last - b), jnp.exp(b_last)
        q = q_ref[...] * (rmask * DK ** -0.5)
        k = k_ref[...] * rmask
        v = v_ref[...] * rmask
        qe, ke, kd = q * e_b, k * e_nb, k * e_kd
        d_o = do_ref[...]
        causal = _iota((CH, CH), 0) >= _iota((CH, CH), 1)
        a, da, dqe, dke, dv_intra, carry = {}, {}, {}, {}, {}, {}
        for c, h in pairs:
            rows, ks, vs_ = _gla_slices(c, h)
            a[c, h] = jnp.where(causal, _dot_nt(qe[rows, ks], ke[rows, ks]), 0.0)
            da[c, h] = jnp.where(causal, _dot_nt(d_o[rows, vs_], v[rows, vs_]), 0.0)
            carry[c, h] = _dot_tn(d_o[rows, vs_], qe[rows, ks])
        for c, h in pairs:
            rows, ks, vs_ = _gla_slices(c, h)
            dqe[c, h] = _dot(d_o[rows, vs_], st_ref[0, c][:, ks]) + _dot(da[c, h], ke[rows, ks])
            dke[c, h] = _dot_tn(da[c, h], qe[rows, ks])
            dv_intra[c, h] = _dot_tn(a[c, h], d_o[rows, vs_])
        dstate = dst[...]
        dkd, db_decay = {}, {}
        for c in reversed(range(GLA_PER_STEP)):
            for h in heads:
                rows, ks, vs_ = _gla_slices(c, h)
                dkd[c, h] = _dot(v[rows, vs_], dstate[:, ks])
                dv_ref[rows, vs_] = dv_intra[c, h] + _dot_nt(kd[rows, ks], dstate[:, ks])
            chunk_last = e_last[c * CH:c * CH + 1]
            db_decay[c] = jnp.sum(dstate * st_ref[0, c], axis=0, keepdims=True) * chunk_last
            dstate = dstate * chunk_last + jnp.concatenate([carry[c, h] for h in heads], axis=1)
        dst[...] = dstate
        rows_of = lambda parts: jnp.concatenate(
            [jnp.concatenate([parts[c, h] for h in heads], axis=1) for c in range(GLA_PER_STEP)], axis=0)
        dqe_all, dke_all, dkd_all = rows_of(dqe), rows_of(dke), rows_of(dkd)
        dq_ref[...] = dqe_all * e_b * (rmask * DK ** -0.5)
        dk_ref[...] = (dke_all * e_nb + dkd_all * e_kd) * rmask
        dkd_kd = dkd_all * kd
        db = dqe_all * qe - dke_all * ke - dkd_kd
        _, upper, same = _gla_chunk_masks()
        decay_rows = jnp.concatenate([jnp.broadcast_to(db_decay[c], (CH, kw)) for c in range(GLA_PER_STEP)], axis=0)
        dlog_g = _dot_exact(upper.astype(F32), db) + _dot_exact(same.astype(F32), dkd_kd) + decay_rows
        dz_ref[...] = dlog_g * (rmask / GLA_TAU) * _sigmoid(-zz)

    blk = lambda w: pl.BlockSpec((BLK, w), lambda s: (_gla_block(steps - 1 - s), 0))
    return pl.pallas_call(
        body, name="gla_bwd", grid=(steps,),
        in_specs=[blk(kw), blk(kw), blk(vw), blk(kw), blk(vw),
                  pl.BlockSpec((1, GLA_PER_STEP, DV, kw), lambda s: (steps - 1 - s, 0, 0, 0)), _const(TOKEN)],
        out_specs=[blk(kw), blk(kw), blk(vw), blk(kw)],
        out_shape=[pltpu.HBM((_lp(), kw), F32), pltpu.HBM((_lp(), kw), F32),
                   pltpu.HBM((_lp(), vw), F32), pltpu.HBM((_lp(), kw), F32)],
        scratch_shapes=[pltpu.VMEM((DV, kw), F32)],
        compiler_params=_params(16, dimension_semantics=_seq()),
    )(*_hbm(qg, kg, vg, z, do_gla, st_all), token)


def _swa_bwd(sinks, qs, ks, vs, do_s, token):
    nb = SEQ // BLK
    kvw = SWA_KV_HEADS * DH
    scale = DH ** -0.5
    heads = range(SWA_HEADS)

    def body(sink_ref, q_ref, km_ref, kp_ref, kc_ref, vm_ref, vp_ref, vc_ref, do_ref, token_ref,
             dq_ref, dk_ref, dv_ref, dsink_ref, carry_k, carry_v, meta_k, meta_v):
        n = pl.program_id(0)

        @pl.when(n == 0)
        def _():
            for r in (carry_k, carry_v, meta_k, meta_v):
                r[...] = jnp.zeros_like(r)
            dsink_ref[...] = jnp.zeros_like(dsink_ref)

        @pl.when(n <= nb)
        def _():
            negdist, maskbias = _swa_bias(n)
            lane = _iota((1, LANE), 1)
            k_all = jnp.concatenate([km_ref[...], kp_ref[...], kc_ref[...]], axis=0).astype(BF16)
            v_all = jnp.concatenate([vm_ref[...], vp_ref[...], vc_ref[...]], axis=0).astype(BF16)
            q = [_swa_half(q_ref, pos, scale) for pos in heads]
            d_o = [_swa_half(do_ref, pos) for pos in heads]
            t = [_dot_nt(q[pos], k_all) + (2.0 ** -(HEAD_POS[pos] + 1) * negdist + maskbias) for pos in heads]
            dp = [_dot_nt(d_o[pos], v_all) for pos in heads]
            soft = [_swa_softmax(t[pos], sink_ref[HEAD_POS[pos]]) for pos in heads]
            p = [s[0] for s in soft]
            delta = [jnp.sum(p[pos] * dp[pos], axis=-1, keepdims=True) for pos in heads]
            ds = [(p[pos] * (dp[pos] - delta[pos])).astype(BF16) for pos in heads]
            dq = [_dot(ds[pos], k_all) for pos in heads]
            for col in range(SWA_HEADS // 2):
                dq_ref[:, col * LANE:(col + 1) * LANE] = scale * _swa_merge(dq[2 * col], dq[2 * col + 1])
            dsink = jnp.zeros((1, LANE), F32)
            for pos in heads:
                dsink = dsink + jnp.where(lane == HEAD_POS[pos],
                                          -jnp.sum(soft[pos][1] * delta[pos], axis=0, keepdims=True), 0.0)
            dsink_ref[...] += dsink
            dk3 = _dot_tn(jnp.concatenate(q, axis=0), jnp.concatenate(ds, axis=0)).T
            dv3 = _dot_tn(jnp.concatenate(d_o, axis=0), jnp.concatenate([x.astype(BF16) for x in p], axis=0)).T
            meta_k[...] += dk3[0:BLK]
            meta_v[...] += dv3[0:BLK]
            dk_ref[...] = carry_k[...] + dk3[BLK:2 * BLK]
            dv_ref[...] = carry_v[...] + dv3[BLK:2 * BLK]
            carry_k[...] = dk3[2 * BLK:3 * BLK]
            carry_v[...] = dv3[2 * BLK:3 * BLK]

        @pl.when(n == nb + 1)
        def _():
            dk_ref[...] = meta_k[...]
            dv_ref[...] = meta_v[...]

    kv_out = pl.BlockSpec((BLK, kvw), lambda n: (jnp.where(n == nb + 1, nb, jnp.clip(n - 1, 0, nb - 1)), 0))
    qblk = pl.BlockSpec((BLK, SWA_HEADS * DH), lambda n: (jnp.minimum(n, nb), 0))
    return pl.pallas_call(
        body, name="swa_bwd", grid=(nb + 2,),
        in_specs=[pl.BlockSpec(memory_space=pltpu.SMEM), qblk] + _swa_kv_specs(kvw) + _swa_kv_specs(kvw)
        + [qblk, _const(TOKEN)],
        out_specs=[qblk, kv_out, kv_out, _acc((1, LANE))],
        out_shape=[pltpu.HBM((_lp(), SWA_HEADS * DH), F32), pltpu.HBM((_lp(), kvw), F32),
                   pltpu.HBM((_lp(), kvw), F32), pltpu.HBM((1, LANE), F32)],
        scratch_shapes=[pltpu.VMEM((BLK, kvw), F32)] * 4,
        compiler_params=_params(16, dimension_semantics=_seq()),
    )(sinks, *_hbm(qs, ks, ks, ks, vs, vs, vs, do_s), token)


def _in_bwd(dqs, dks, dvs, dqg, dkg, dvg, drg, dz, dpre1, w_in_t, wg2_p):
    tm = _row_tile(384)
    lp = _lp()
    widths = (512, 128, 128, 256, 256, 512, 512)
    offs = (O_QS, O_KS, O_VS, O_QG, O_KG, O_VG, O_RG)

    def body(*refs):
        parts, (dz_ref, dp1_ref, w_ref, wg2_ref, dproj_ref, dh0_ref, dbin_ref, dbg_ref) = refs[:7], refs[7:]

        @pl.when(pl.program_id(0) == 0)
        def _():
            dbin_ref[...] = jnp.zeros_like(dbin_ref)
            dbg_ref[...] = jnp.zeros_like(dbg_ref)

        for pos, h in enumerate(HEAD_POS):
            val = parts[0][:, pos * DH:(pos + 1) * DH]
            dproj_ref[:, O_QS + h * DH:O_QS + (h + 1) * DH] = val.astype(BF16)
            dbin_ref[:, O_QS + h * DH:O_QS + (h + 1) * DH] += jnp.sum(val, axis=0, keepdims=True)
        for p_ref, off, wd in zip(parts[1:], offs[1:], widths[1:]):
            val = p_ref[...]
            dproj_ref[:, off:off + wd] = val.astype(BF16)
            dbin_ref[:, off:off + wd] += jnp.sum(val, axis=0, keepdims=True)
        dz = dz_ref[...]
        dlr = _dot_nt(dz, wg2_ref[...])
        dproj_ref[:, O_LR:O_LR + LANE] = dlr.astype(BF16)
        dbin_ref[:, O_LR:O_LR + LANE] += jnp.sum(dlr, axis=0, keepdims=True)
        dbg_ref[...] += jnp.sum(dz, axis=0, keepdims=True)
        dh0_ref[...] = ALPHA * dp1_ref[...] + _dot(dproj_ref[...], w_ref[...])

    return pl.pallas_call(
        body, name="in_bwd", grid=(lp // tm,),
        in_specs=[_rows(tm, w) for w in widths] + [_rows(tm, 256), _rows(tm, D), _const((D_IN_P, D)), _const((LANE, 256))],
        out_specs=[_rows(tm, D_IN_P), _rows(tm, D), _acc((1, D_IN_P)), _acc((1, 256))],
        out_shape=[pltpu.HBM((lp, D_IN_P), BF16), pltpu.HBM((lp, D), F32),
                   pltpu.HBM((1, D_IN_P), F32), pltpu.HBM((1, 256), F32)],
        compiler_params=_params(40, dimension_semantics=_seq()),
    )(*_hbm(dqs, dks, dvs, dqg, dkg, dvg, drg, dz, dpre1, w_in_t, wg2_p))


def _ln_in_bwd(x, meta_ext, dh0, g, token):
    tr = min(LN_ROWS, SEQ)

    def ln_bwd(x_ref, dh_ref, g_ref, dx_ref, dg_ref, db_ref):
        @pl.when(pl.program_id(0) == 0)
        def _():
            dg_ref[...] = jnp.zeros_like(dg_ref)
            db_ref[...] = jnp.zeros_like(db_ref)

        xhat, rstd = _ln_stats(x_ref[...])
        dh = dh_ref[...]
        dx_ref[...] = _ln_bwd(dh, xhat, rstd, g_ref[...])
        dg_ref[...] += jnp.sum(dh * xhat, axis=0, keepdims=True)
        db_ref[...] += jnp.sum(dh, axis=0, keepdims=True)

    def body(x_ref, dh_ref, g_ref, token_ref, dx_ref, dg_ref, db_ref):
        ln_bwd(x_ref, dh_ref, g_ref, dx_ref, dg_ref, db_ref)

    def meta_body(m_ref, dh_ref, g_ref, dm_ref, dg_ref, db_ref):
        ln_bwd(m_ref, dh_ref, g_ref, dm_ref, dg_ref, db_ref)

    sums = [pltpu.HBM((1, D), F32), pltpu.HBM((1, D), F32)]
    dx, dg, db = pl.pallas_call(
        body, name="ln_in_bwd", grid=(SEQ // tr,),
        in_specs=[_rows(tr, D), _rows(tr, D), _const((1, D)), _const(TOKEN)],
        out_specs=[_rows(tr, D), _acc((1, D)), _acc((1, D))],
        out_shape=[pltpu.HBM((SEQ, D), F32)] + sums,
        compiler_params=_params(32, dimension_semantics=_seq()),
    )(*_hbm(x, dh0, g), token)
    dm, dg_m, db_m = pl.pallas_call(
        meta_body, name="ln_in_bwd_meta", grid=(1,),
        in_specs=[_const((BLK, D)), pl.BlockSpec((BLK, D), lambda i: (SEQ // BLK, 0)), _const((1, D))],
        out_specs=[_acc((BLK, D)), _acc((1, D)), _acc((1, D))],
        out_shape=[pltpu.HBM((BLK, D), F32)] + sums,
        compiler_params=_params(16, dimension_semantics=_seq()),
    )(*_hbm(meta_ext, dh0, g))
    return dx, dm, dg + dg_m, db + db_m


def _local_step(x, target, ln_in_g, ln_in_b, b_in, bg2, sinks, gn, g1, b1, g2, b2,
                token, fetch_first, fetch_rest, fetch_ffn, exchange_ffn, ship_ffn, exchange_w_in, ship_w_in):
    row = lambda v: v.reshape(1, -1).astype(F32)
    b_in_p = jnp.pad(row(b_in), ((0, 0), (0, D_IN_P - D_IN)))
    gn4 = row(gn)
    sinks = sinks.reshape(-1).astype(F32)

    h_real = _ln_in_fwd_real(x, row(ln_in_g), row(ln_in_b), token)
    w_in_windows, meta_full, wg2 = fetch_first([h_real])
    meta_ext = jnp.pad(meta_full, ((META_OFF, BLK - CH), (0, 0)))
    wg2_p = jnp.pad(wg2, ((0, LANE - wg2.shape[0]), (0, 0))).astype(BF16)
    h0 = _ln_in_fwd_meta(h_real, meta_ext, row(ln_in_g), row(ln_in_b))
    qs, ks, vs, qg, kg, vg, rg, glr, z, w_in_t = _in_proj(h0, w_in_windows, b_in_p, wg2_p, row(bg2))
    o_s = _swa_fwd(sinks, qs, ks, vs)
    o_gla, st_all = _gla_fwd(qg, kg, vg, z)
    w_out, token = fetch_rest([o_s, o_gla])
    o, pre1, h1 = _post_mix(o_s, o_gla, rg, h0, gn4, w_out, row(g1), row(b1), token)
    wg_t, wu_t, wd = fetch_ffn([pre1])
    a, dgate, dup, dpre2, loss, dg2, db2 = _ffn_fwd_loss_bwd(h1, wg_t, wu_t, wd, target, row(g2), row(b2))
    dpre1, dg1, db1, do_s, do_gla, drg, dgn = _ffn_out_bwd(dpre2, dgate, dup, pre1, wg_t, wu_t, row(g1), w_out, o_gla,
                                                           rg, gn4)
    dwd = _atb(a, dpre2, "dw_down")
    dwg_t = _atb(dgate, h1, "dw_gate")
    dwu_t = _atb(dup, h1, "dw_up")
    token = exchange_ffn(dict(w_out=_atb(o, dpre1, "dw_out"), w_g=dwg_t, w_u=dwu_t, w_d=dwd))
    dqg, dkg, dvg, dz = _gla_bwd(qg, kg, vg, z, do_gla, st_all, token)
    token = ship_ffn([dqg])
    dqs, dks, dvs, dsinks = _swa_bwd(sinks, qs, ks, vs, do_s, token)
    dproj, dh0, db_in_p, dbg2 = _in_bwd(dqs, dks, dvs, dqg, dkg, dvg, drg, dz, dpre1, w_in_t, wg2_p)
    token = exchange_w_in(_atb(dproj, h0, "dw_in", windows=(W_IN_STARTS, W_IN_WIN)))
    dwg2_p = _atb(glr, dz, "dw_gate_lr2", token)
    token = ship_w_in([dwg2_p])
    dx, dmeta_blk, dg_in, db_in_ln = _ln_in_bwd(x, meta_ext, dh0, row(ln_in_g), token)

    small = dict(meta_blk=dmeta_blk, ln_in_g=dg_in, ln_in_b=db_in_ln, ln1_g=dg1, ln1_b=db1, ln2_g=dg2, ln2_b=db2,
                 b_in_p=db_in_p, wg2_p=dwg2_p, bg2=dbg2, sinks=dsinks, gn=dgn, loss=loss)
    return dx, small


HBM = pl.BlockSpec(memory_space=pltpu.HBM)


def _place():
    return lax.axis_index("x"), lax.axis_index("y"), lax.axis_index("c")


def _other_chips(x, y):
    return [(1 - x, y), (x, 1 - y), (1 - x, 1 - y)]


def _dma_sems(n):
    return pltpu.SemaphoreType.DMA((n,))


def _comm_params():
    return pltpu.CompilerParams(has_side_effects=True)


SEM = pl.BlockSpec(memory_space=pltpu.SEMAPHORE)


PER_ARRAY = dict(gather=3, scatter=3, sibling=N_CHIPS)


def _ici_copies(kind, landing, srcs, lands, send_sems, recv_sems):
    x, y, c = _place()
    mine = 2 * x + y
    copies = []
    for a in range(len(srcs)):
        if kind == "sibling":
            for s in range(N_CHIPS):
                copies.append(pltpu.make_async_remote_copy(
                    srcs[a].at[s, 1 - c], lands[a].at[s], send_sems.at[N_CHIPS * a + s], recv_sems.at[N_CHIPS * a + s],
                    device_id=(x, y, 1 - c), device_id_type=MESH))
            continue
        for j, (px, py) in enumerate(_other_chips(x, y)):
            slab = 2 * px + py if landing else mine
            if kind == "gather":
                src, dst = srcs[a].at[c], lands[a].at[slab, c]
            else:
                src, dst = srcs[a].at[2 * px + py], lands[a].at[slab]
            copies.append(pltpu.make_async_remote_copy(src, dst, send_sems.at[3 * a + j], recv_sems.at[3 * a + j],
                                                       device_id=(px, py, c), device_id_type=MESH))
    return copies


def _split_params():
    return pltpu.CompilerParams(has_side_effects=pltpu.SideEffectType.DATAFLOW_SIDE_EFFECTING)


def _ici_start(kind, srcs, land_shapes, after, name):
    n = len(srcs)
    lands = [pltpu.with_memory_space_constraint(lax.empty(s, a.dtype), pltpu.HBM) for s, a in zip(land_shapes, srcs)]

    def body(*refs):
        outs = refs[2 * n + len(after):]
        for cp in _ici_copies(kind, False, refs[:n], refs[n:2 * n], outs[0], outs[1]):
            cp.start()
        outs[-1][...] = jnp.zeros(TOKEN, F32)

    outs = pl.pallas_call(
        body, name=name, in_specs=[HBM] * (2 * n) + [pl.BlockSpec(memory_space=pl.ANY)] * len(after),
        out_specs=[SEM, SEM] + [HBM] * (2 * n) + [pl.BlockSpec(memory_space=pltpu.VMEM)],
        out_shape=[_dma_sems(PER_ARRAY[kind] * n)] * 2 + [pltpu.HBM(a.shape, a.dtype) for a in list(srcs) + lands]
        + [jax.ShapeDtypeStruct(TOKEN, F32)],
        input_output_aliases={i: 2 + i for i in range(2 * n)},
        compiler_params=_split_params(),
    )(*_hbm(*srcs), *lands, *after)
    return outs[:-1], outs[-1]


def _ici_wait(kind, handle, after, name):
    n = (len(handle) - 2) // 2

    def body(*refs):
        for cp in _ici_copies(kind, True, refs[:n], refs[n:2 * n], refs[2 * n], refs[2 * n + 1]):
            cp.wait_send()
            cp.wait_recv()

    outs = pl.pallas_call(
        body, name=name, in_specs=[HBM] * (2 * n) + [SEM, SEM] + [pl.BlockSpec(memory_space=pl.ANY)] * len(after),
        out_specs=[HBM] * (2 * n), out_shape=[pltpu.HBM(a.shape, a.dtype) for a in handle[2:]],
        input_output_aliases={i: i for i in range(2 * n)},
        compiler_params=_split_params(),
    )(*handle[2:], handle[0], handle[1], *after)
    return list(outs[:n]), list(outs[n:])


def _forward_copies(landing, arrs, send_sems, recv_sems):
    x, y, c = _place()
    copies = []
    for a in range(len(arrs)):
        for j, (px, py) in enumerate(_other_chips(x, y)):
            half = 1 - c if landing else c
            copies.append(pltpu.make_async_remote_copy(
                arrs[a].at[2 * px + py, c], arrs[a].at[2 * px + py, half], send_sems.at[3 * a + j],
                recv_sems.at[3 * a + j], device_id=(x, y, 1 - c), device_id_type=MESH))
    return copies


def _gather_wait_forward(handle, groups, after, name):
    n = (len(handle) - 2) // 2
    assert sum(groups) == n

    def body(*refs):
        outs = refs[2 * n + 2 + len(after):]
        lands, sems = outs[n:2 * n], outs[2 * n:-1]
        arrivals = _ici_copies("gather", True, refs[:n], refs[n:2 * n], refs[2 * n], refs[2 * n + 1])
        sends, first = [], 0
        for g, count in enumerate(groups):
            sends += _forward_copies(False, lands[first:first + count], sems[2 * g], sems[2 * g + 1])
            first += count
        for cp, send in zip(arrivals, sends):
            cp.wait_recv()
            send.start()
        for cp in arrivals:
            cp.wait_send()
        outs[-1][...] = jnp.zeros(TOKEN, F32)

    outs = pl.pallas_call(
        body, name=name, in_specs=[HBM] * (2 * n) + [SEM, SEM] + [pl.BlockSpec(memory_space=pl.ANY)] * len(after),
        out_specs=[HBM] * (2 * n) + [SEM] * (2 * len(groups)) + [pl.BlockSpec(memory_space=pltpu.VMEM)],
        out_shape=[pltpu.HBM(a.shape, a.dtype) for a in handle[2:]]
        + [_dma_sems(3 * count) for count in groups for _ in range(2)] + [jax.ShapeDtypeStruct(TOKEN, F32)],
        input_output_aliases={i: i for i in range(2 * n)},
        compiler_params=_split_params(),
    )(*handle[2:], handle[0], handle[1], *after)
    lands, sems, handles, first = outs[n:2 * n], outs[2 * n:-1], [], 0
    for g, count in enumerate(groups):
        handles.append([sems[2 * g], sems[2 * g + 1], *lands[first:first + count]])
        first += count
    return list(outs[:n]), handles, outs[-1]


def _forward_wait(handle, after, name):
    n = len(handle) - 2

    def body(*refs):
        for cp in _forward_copies(True, refs[:n], refs[n], refs[n + 1]):
            cp.wait_send()
            cp.wait_recv()

    return list(pl.pallas_call(
        body, name=name, in_specs=[HBM] * n + [SEM, SEM] + [pl.BlockSpec(memory_space=pl.ANY)] * len(after),
        out_specs=[HBM] * n, out_shape=[pltpu.HBM(a.shape, a.dtype) for a in handle[2:]],
        input_output_aliases={i: i for i in range(n)},
        compiler_params=_split_params(),
    )(*handle[2:], handle[0], handle[1], *after))


def _add_halves(core, grads, recvs, dtypes, name):
    n = len(grads)
    heights = [g.shape[2] for g in grads]

    def body(c_ref, *refs):
        for a in range(n):
            refs[2 * n + a][...] = (refs[2 * a][0] + refs[2 * a + 1][...]).astype(dtypes[a])

    slab = lambda h: pl.BlockSpec((1, h, D), lambda s, c: (s, 0, 0))
    mine = lambda h: pl.BlockSpec((1, 1, h, D), lambda s, c: (s, c[0], 0, 0))
    return pl.pallas_call(
        body, name=name,
        grid_spec=pltpu.PrefetchScalarGridSpec(
            num_scalar_prefetch=1, grid=(N_CHIPS,),
            in_specs=[spec(h) for h in heights for spec in (mine, slab)], out_specs=[slab(h) for h in heights]),
        out_shape=[pltpu.HBM((N_CHIPS, h, D), dt) for h, dt in zip(heights, dtypes)],
        compiler_params=_params(32, dimension_semantics=_seq()),
    )(core, *_hbm(*[a for pair in zip(grads, recvs) for a in pair]))


N_DEVICES = 2 * N_CHIPS
PEER_FLIPS = [(dx, dy, dc) for dx in (0, 1) for dy in (0, 1) for dc in (0, 1)][1:]


def _small_copies(landing, p_ref, out_ref, send_sems, recv_sems):
    x, y, c = _place()
    flip = lambda v, d: 1 - v if d else v
    copies = []
    for k, flips in enumerate(PEER_FLIPS):
        px, py, pc = (flip(v, d) for v, d in zip((x, y, c), flips))
        slab = 4 * px + 2 * py + pc if landing else 4 * x + 2 * y + c
        copies.append(pltpu.make_async_remote_copy(p_ref, out_ref.at[slab], send_sems.at[k], recv_sems.at[k],
                                                   device_id=(px, py, pc), device_id_type=MESH))
    return copies


def _small_start(pack, after):
    n = len(PEER_FLIPS)
    land = pltpu.with_memory_space_constraint(lax.empty((N_DEVICES,) + pack.shape, F32), pltpu.HBM)

    def body(p_ref, land_ref, *refs):
        outs = refs[len(after):]
        for cp in _small_copies(False, p_ref, land_ref, outs[0], outs[1]):
            cp.start()
        outs[-1][...] = jnp.zeros(TOKEN, F32)

    outs = pl.pallas_call(
        body, name="small_exchange_start", in_specs=[HBM, HBM] + [pl.BlockSpec(memory_space=pl.ANY)] * len(after),
        out_specs=[SEM, SEM, HBM, HBM, pl.BlockSpec(memory_space=pltpu.VMEM)],
        out_shape=[_dma_sems(n), _dma_sems(n), pltpu.HBM(pack.shape, F32), pltpu.HBM(land.shape, F32),
                   jax.ShapeDtypeStruct(TOKEN, F32)],
        input_output_aliases={0: 2, 1: 3},
        compiler_params=_split_params(),
    )(*_hbm(pack), land, *after)
    return outs[:-1], outs[-1]


def _small_wait(handle, after):
    def body(p_ref, land_ref, send_sems, recv_sems, *rest):
        for cp in _small_copies(True, p_ref, land_ref, send_sems, recv_sems):
            cp.wait_send()
            cp.wait_recv()

    return pl.pallas_call(
        body, name="small_exchange_wait", in_specs=[HBM, HBM, SEM, SEM] + [pl.BlockSpec(memory_space=pl.ANY)] * len(after),
        out_specs=[HBM, HBM], out_shape=[pltpu.HBM(a.shape, F32) for a in handle[2:]],
        input_output_aliases={0: 0, 1: 1},
        compiler_params=_split_params(),
    )(handle[2], handle[3], handle[0], handle[1], *after)


def _sum_chips(slots, firsts, rests, after, name):
    n = len(firsts)

    def body(i_ref, *refs):
        outs = refs[4 * n + len(after):]
        for a in range(n):
            first, r1, r2, r3 = refs[4 * a:4 * a + 4]
            outs[a][...] = ((first[...].astype(F32) + r1[...].astype(F32)) + r2[...].astype(F32)) + r3[...].astype(F32)

    slab = lambda h, k: pl.BlockSpec((1, h, D), lambda i, ix: (ix[k], 0, 0))
    heights = [f.shape[1] for f in firsts]
    return pl.pallas_call(
        body, name=name,
        grid_spec=pltpu.PrefetchScalarGridSpec(
            num_scalar_prefetch=1, grid=(1,),
            in_specs=[slab(h, k) for h in heights for k in range(4)] + [pl.BlockSpec(memory_space=pl.ANY)] * len(after),
            out_specs=[slab(h, 4) for h in heights]),
        out_shape=[pltpu.HBM((2, h, D), F32) for h in heights],
        compiler_params=_params(48, dimension_semantics=_seq()),
    )(slots, *_hbm(*[a for f, r in zip(firsts, rests) for a in (f, r, r, r)]), *after)


def _join_copies(landing, arrs, send_sems, recv_sems):
    x, y, c = _place()
    slab = 1 - c if landing else c
    return [pltpu.make_async_remote_copy(arr.at[slab], arr.at[slab], send_sems.at[a], recv_sems.at[a],
                                         device_id=(x, y, 1 - c), device_id_type=MESH) for a, arr in enumerate(arrs)]


def _join_halves(halves, name):
    n = len(halves)

    def body(*refs):
        outs = refs[n:2 * n]
        send_sems, recv_sems = refs[2 * n:]
        sends = _join_copies(False, outs, send_sems, recv_sems)
        for cp in sends:
            cp.start()
        for cp in _join_copies(True, outs, send_sems, recv_sems):
            cp.wait_recv()
        for cp in sends:
            cp.wait_send()

    return list(pl.pallas_call(
        body, name=name, in_specs=[HBM] * n, out_specs=[HBM] * n,
        out_shape=[pltpu.HBM(h.shape, F32) for h in halves],
        input_output_aliases={a: a for a in range(n)},
        scratch_shapes=[_dma_sems(n)] * 2,
        compiler_params=_comm_params(),
    )(*_hbm(*halves)))


def _join_start(halves, name):
    n = len(halves)

    def body(*refs):
        outs = refs[n:]
        for cp in _join_copies(False, refs[:n], outs[0], outs[1]):
            cp.start()
        outs[-1][...] = jnp.zeros(TOKEN, F32)

    outs = pl.pallas_call(
        body, name=name, in_specs=[HBM] * n,
        out_specs=[SEM, SEM] + [HBM] * n + [pl.BlockSpec(memory_space=pltpu.VMEM)],
        out_shape=[_dma_sems(n)] * 2 + [pltpu.HBM(a.shape, a.dtype) for a in halves] + [jax.ShapeDtypeStruct(TOKEN, F32)],
        input_output_aliases={i: 2 + i for i in range(n)},
        compiler_params=_split_params(),
    )(*_hbm(*halves))
    return outs[:-1], outs[-1]


def _join_wait(handle, after, name):
    n = len(handle) - 2

    def body(*refs):
        for cp in _join_copies(True, refs[:n], refs[n], refs[n + 1]):
            cp.wait_send()
            cp.wait_recv()

    return list(pl.pallas_call(
        body, name=name, in_specs=[HBM] * n + [SEM, SEM] + [pl.BlockSpec(memory_space=pl.ANY)] * len(after),
        out_specs=[HBM] * n, out_shape=[pltpu.HBM(a.shape, a.dtype) for a in handle[2:]],
        input_output_aliases={i: i for i in range(n)},
        compiler_params=_split_params(),
    )(*handle[2:], handle[0], handle[1], *after))


def _chip_partials(grads, fetched, wire_dtypes, name):
    core = lax.axis_index("c").astype(jnp.int32).reshape(1)
    return list(_add_halves(core, grads, fetched, wire_dtypes, name))


def _chip_sums(parts, got, after, name):
    x, y, c = _place()
    others = [2 * px + py for px, py in _other_chips(x, y)]
    own_first = jnp.stack([2 * x + y] + others + [c]).astype(jnp.int32)
    return list(_sum_chips(own_first, parts, got, after, name))


ADAMW_STEPS = 8


def _adamw(params, by_row, chip, window_step):
    n = len(params)
    rows, _, cols = by_row[0].shape
    block = lambda shape: pl.BlockSpec((shape[0] // ADAMW_STEPS, shape[1]), lambda i, c: (i, 0))
    assert all(a.shape[0] % (8 * ADAMW_STEPS) == 0 for p in params for a in p)

    def body(c_ref, *refs):
        w_hbm, g_ref, m_hbm, v_hbm = refs[4 * n:4 * n + 4]
        results, (ins_ref, outs_ref, sems) = refs[8 * n + 4:8 * n + 8], refs[8 * n + 8:]
        loads = [pltpu.make_async_copy(src.at[:, 0, :], ins_ref.at[k], sems.at[k])
                 for k, src in enumerate((w_hbm, m_hbm, v_hbm))]
        stores = [pltpu.make_async_copy(outs_ref.at[k], dst.at[:, 0, :], sems.at[3 + k]) for k, dst in enumerate(results)]
        first = pl.program_id(0) == 0

        @pl.when(first)
        def _():
            for cp in loads:
                cp.start()

        for a in range(n):
            w_ref, a_g_ref, m_ref, v_ref = refs[4 * a:4 * a + 4]
            outs = refs[4 * n + 4 + 4 * a:4 * n + 8 + 4 * a]
            g = a_g_ref[...]
            outs[0][...] = g
            outs[1][...], outs[2][...], outs[3][...] = _adamw_math(w_ref[...], g, m_ref[...], v_ref[...])

        @pl.when(first)
        def _():
            for cp in loads:
                cp.wait()
            for lo in range(0, cols, LANE):
                lanes = slice(lo, lo + LANE)
                g = g_ref[0:rows, lanes]
                for s in range(1, N_CHIPS):
                    g = jnp.where(c_ref[0] == s, g_ref[s * window_step:s * window_step + rows, lanes], g)
                outs_ref[0, :, lanes] = g
                outs_ref[1, :, lanes], outs_ref[2, :, lanes], outs_ref[3, :, lanes] = _adamw_math(
                    ins_ref[0, :, lanes], g, ins_ref[1, :, lanes], ins_ref[2, :, lanes])
            for cp in stores:
                cp.start()

        @pl.when(pl.program_id(0) == ADAMW_STEPS - 1)
        def _():
            for cp in stores:
                cp.wait()

    outs = pl.pallas_call(
        body, name="adamw_matrices",
        grid_spec=pltpu.PrefetchScalarGridSpec(
            num_scalar_prefetch=1, grid=(ADAMW_STEPS,),
            in_specs=[block(a.shape) for p in params for a in p] + [HBM, _const(by_row[1].shape), HBM, HBM],
            out_specs=[block(p[0].shape) for p in params for _ in range(4)] + [HBM] * 4,
            scratch_shapes=[pltpu.VMEM((3, rows, cols), F32), pltpu.VMEM((4, rows, cols), F32), _dma_sems(7)]),
        out_shape=[pltpu.HBM(p[0].shape, F32) for p in params for _ in range(4)] + [pltpu.HBM((rows, 1, cols), F32)] * 4,
        compiler_params=_params(48, dimension_semantics=_seq()),
    )(chip, *_hbm(*[a for p in params for a in p], *by_row))
    return [outs[4 * a:4 * a + 4] for a in range(n)], outs[4 * n:]


def _adamw_math(w, g, m, v):
    nm = ADAM_B1 * m + (1.0 - ADAM_B1) * g
    nv = ADAM_B2 * v + (1.0 - ADAM_B2) * (g * g)
    m_hat = nm / (1.0 - ADAM_B1 ** ADAM_STEP)
    v_hat = nv / (1.0 - ADAM_B2 ** ADAM_STEP)
    return -ADAM_LR * (m_hat / (jnp.sqrt(v_hat) + ADAM_EPS) + ADAM_WD * w), nm, nv


SMALL = (("meta_tokens", (N_META, D // N_CHIPS)), ("ln_in_g", (1, D)), ("ln_in_b", (1, D)), ("b_in", (1, D_IN)),
         ("w_gate_lr2", (GATE_RANK, GLA_HEADS * DK // N_CHIPS)), ("b_gate_lr2", (1, GLA_HEADS * DK)),
         ("attn_sinks", (1, SWA_HEADS)),
         ("gla_norm_g", (1, DV)), ("ln1_g", (1, D)), ("ln1_b", (1, D)), ("ln2_g", (1, D)), ("ln2_b", (1, D)))
ROW_META, ROW_B_IN, ROW_TAIL, ROW_WG2 = 0, 22, 25, 32
ROW_LN = dict(ln_in_g=16, ln_in_b=17, ln1_g=18, ln1_b=19, ln2_g=20, ln2_b=21)
TAIL_BG2, TAIL_SINKS, TAIL_GN, TAIL_LOSS = 0, 256, 256 + SWA_HEADS, 256 + SWA_HEADS + DV


def _adamw_small(place, packs, own, params):
    n = len(SMALL)

    def body(place_ref, packs_ref, own_ref, *refs):
        ins, outs, p_ref = refs[:3 * n], refs[3 * n:-1], refs[-1]
        me, c = place_ref[0], place_ref[1]
        total = jnp.where(me == 0, own_ref[...], packs_ref[0])
        for i in range(1, N_DEVICES):
            total = total + jnp.where(me == i, own_ref[...], packs_ref[i])
        p_ref[...] = total
        outs[4 * n][...] = total[ROW_TAIL:ROW_TAIL + 1, :]

        def mine(width, rows):
            part = lambda s: p_ref[rows, s * width:(s + 1) * width]
            return jnp.where(c == 0, part(0), jnp.where(c == 1, part(1), jnp.where(c == 2, part(2), part(3))))

        tail = lambda lo, width: p_ref[ROW_TAIL:ROW_TAIL + 1, lo:lo + width]
        grads = dict(
            meta_tokens=mine(D // N_CHIPS, slice(ROW_META, ROW_META + N_META)),
            b_in=jnp.concatenate([p_ref[ROW_B_IN:ROW_B_IN + 1, :], p_ref[ROW_B_IN + 1:ROW_B_IN + 2, :],
                                  p_ref[ROW_B_IN + 2:ROW_B_IN + 3, 0:D_IN - 2 * D]], axis=1),
            w_gate_lr2=mine(256 // N_CHIPS, slice(ROW_WG2, ROW_WG2 + 16)),
            b_gate_lr2=tail(TAIL_BG2, 256), attn_sinks=tail(TAIL_SINKS, SWA_HEADS), gla_norm_g=tail(TAIL_GN, DV),
            **{k: p_ref[r:r + 1, :] for k, r in ROW_LN.items()})
        for i, (name, _) in enumerate(SMALL):
            g = grads[name]
            outs[4 * i][...] = g
            outs[4 * i + 1][...], outs[4 * i + 2][...], outs[4 * i + 3][...] = _adamw_math(
                ins[3 * i][...], g, ins[3 * i + 1][...], ins[3 * i + 2][...])

    whole = lambda shape: pl.BlockSpec(shape, lambda i, c: (0,) * len(shape))
    outs = pl.pallas_call(
        body, name="adamw_small",
        grid_spec=pltpu.PrefetchScalarGridSpec(
            num_scalar_prefetch=1, grid=(1,),
            in_specs=[whole(packs.shape), whole(own.shape)] + [whole(s) for _, s in SMALL for _ in range(3)],
            out_specs=[whole(s) for _, s in SMALL for _ in range(4)] + [whole((1, D))],
            scratch_shapes=[pltpu.VMEM(own.shape, F32)]),
        out_shape=[pltpu.HBM(s, F32) for _, s in SMALL for _ in range(4)] + [pltpu.HBM((1, D), F32)],
        compiler_params=_params(16, dimension_semantics=_seq()),
    )(place, *_hbm(packs, own, *[a for p in params for a in p]))
    return [outs[4 * i:4 * i + 4] for i in range(n)], outs[4 * n]


def _small_pack(gr):
    names = ["meta_blk"] + list(ROW_LN) + ["b_in_p", "wg2_p", "bg2", "sinks", "gn", "loss"]
    gate_w = GLA_HEADS * DK

    def body(*refs):
        src, out = dict(zip(names, refs)), refs[-1]
        out[...] = jnp.zeros_like(out)
        out[ROW_META:ROW_META + N_META, :] = src["meta_blk"][META_OFF:CH, :]
        for k, r in ROW_LN.items():
            out[r:r + 1, :] = src[k][...]
        for j in range(-(-D_IN // D)):
            width = min(D, D_IN - j * D)
            out[ROW_B_IN + j:ROW_B_IN + j + 1, 0:width] = src["b_in_p"][:, j * D:j * D + width]
        tail = slice(ROW_TAIL, ROW_TAIL + 1)
        out[tail, TAIL_BG2:TAIL_BG2 + gate_w] = src["bg2"][...]
        out[tail, TAIL_SINKS:TAIL_SINKS + SWA_HEADS] = src["sinks"][:, 0:SWA_HEADS]
        out[tail, TAIL_GN:TAIL_GN + DV] = src["gn"][...]
        out[tail, TAIL_LOSS:TAIL_LOSS + 1] = src["loss"][:, 0:1]
        out[ROW_WG2:ROW_WG2 + GATE_RANK, 0:gate_w] = src["wg2_p"][0:GATE_RANK, :]

    arrays = [gr[k] for k in names]
    return pl.pallas_call(
        body, name="small_pack", grid=(1,),
        in_specs=[_acc(a.shape) for a in arrays], out_specs=_acc((SMALL_ROWS, D)),
        out_shape=pltpu.HBM((SMALL_ROWS, D), F32),
        compiler_params=_params(16, dimension_semantics=_seq()),
    )(*_hbm(*arrays))


BIG = ("w_in", "w_out", "w_g", "w_u", "w_d")


def kernel(x, meta_tokens, ln_in_g, ln_in_b, w_in, b_in, w_gate_lr2, b_gate_lr2, attn_sinks, gla_norm_g, w_out, ln1_g, ln1_b, w_ffn_gate, w_ffn_up, w_ffn_down, ln2_g, ln2_b, loss_target, m_meta_tokens, m_ln_in_g, m_ln_in_b, m_w_in, m_b_in, m_w_gate_lr2, m_b_gate_lr2, m_attn_sinks, m_gla_norm_g, m_w_out, m_ln1_g, m_ln1_b, m_w_ffn_gate, m_w_ffn_up, m_w_ffn_down, m_ln2_g, m_ln2_b, v_meta_tokens, v_ln_in_g, v_ln_in_b, v_w_in, v_b_in, v_w_gate_lr2, v_b_gate_lr2, v_attn_sinks, v_gla_norm_g, v_w_out, v_ln1_g, v_ln1_b, v_w_ffn_gate, v_w_ffn_up, v_w_ffn_down, v_ln2_g, v_ln2_b):
    chip = 2 * lax.axis_index("x") + lax.axis_index("y")

    halves = lambda a: a.reshape(2, a.shape[0] // 2, a.shape[1])
    r_in = SHARD_ROWS["w_in"]
    first = [halves(a) for a in (jnp.pad(w_in[0].T.astype(BF16), ((0, W_IN_WIN - r_in), (0, 0))), meta_tokens,
                                 w_gate_lr2[0])]
    rest = [halves(a) for a in (w_out[0].astype(BF16), w_ffn_gate[0].T.astype(BF16), w_ffn_up[0].T.astype(BF16),
                                w_ffn_down[0].astype(BF16))]
    lands = lambda arrs: [(N_CHIPS,) + a.shape for a in arrs]
    first_handle, first_token = _ici_start("gather", first, lands(first), [], "gather_first_start")
    rest_handle, token = _ici_start("gather", rest, lands(rest), [first_token], "gather_rest_start")
    own_slab = lambda got, shards: [lax.dynamic_update_index_in_dim(g, s, chip, axis=0) for g, s in zip(got, shards)]
    fetching = {}

    def fetch_first(after):
        shards, (forwarding,), _ = _gather_wait_forward(first_handle, [len(first)], after, "gather_first_wait")
        g_in, g_meta, g_wg2 = own_slab(_forward_wait(forwarding, [], "gather_first_forward_wait"), shards)
        w_in_windows = g_in.reshape(N_CHIPS, W_IN_WIN, D)
        meta_full = jnp.concatenate([g_meta[s].reshape(N_META, -1) for s in range(N_CHIPS)], axis=1)
        wg2_full = jnp.concatenate([g_wg2[s].reshape(w_gate_lr2.shape[1], -1) for s in range(N_CHIPS)], axis=1)
        return w_in_windows, meta_full, wg2_full

    def fetch_rest(after):
        shards, (w_out_forwarding, fetching["handle"]), forward_token = _gather_wait_forward(
            rest_handle, [1, len(rest) - 1], after, "gather_rest_wait")
        g_out, = own_slab(_forward_wait(w_out_forwarding, [], "gather_w_out_forward_wait"), shards[:1])
        fetching["shards"] = shards[1:]
        return g_out.reshape(-1, D), forward_token

    def fetch_ffn(after):
        got = _forward_wait(fetching["handle"], after, "gather_ffn_forward_wait")
        return [g.reshape(-1, D) for g in own_slab(got, fetching["shards"])]

    sent = {}
    split = lambda grads: [g.reshape(N_CHIPS, 2, -1, D) for g in grads]

    def exchange(key, grads):
        grads = split(grads)
        sent[key + "_halves"], exchange_token = _ici_start(
            "sibling", grads, [(N_CHIPS,) + a.shape[2:] for a in grads], [], "sibling_" + key + "_start")
        return exchange_token

    def ship(key, after):
        grads, fetched = _ici_wait("sibling", sent[key + "_halves"], after, "sibling_" + key + "_wait")
        parts = _chip_partials(grads, fetched, [BF16] * len(grads), "add_halves_" + key)
        sent[key], ship_token = _ici_start("scatter", parts, [p.shape for p in parts], [], "scatter_" + key + "_start")
        return ship_token

    dx, gr = _local_step(
        x[0], loss_target[0], ln_in_g, ln_in_b, b_in[0], b_gate_lr2[0], attn_sinks[0], gla_norm_g[0], ln1_g[0],
        ln1_b[0], ln2_g[0], ln2_b[0], token, fetch_first, fetch_rest, fetch_ffn,
        lambda g: exchange("ffn", [g[k] for k in BIG[1:]]), lambda after: ship("ffn", after),
        lambda g: exchange("w_in", [g]), lambda after: ship("w_in", after))
    ffn_parts, ffn_got = _ici_wait("scatter", sent["ffn"], [dx], "scatter_ffn_wait")
    join_handle, token = _join_start(_chip_sums(ffn_parts, ffn_got, [], "sum_chips_ffn"), "join_ffn_start")
    w_in_parts, w_in_got = _ici_wait("scatter", sent["w_in"], [token], "scatter_w_in_wait")

    small_handle, token = _small_start(_small_pack(gr), [w_in_got[0]])
    w_in_joined = _join_halves(_chip_sums(w_in_parts, w_in_got, [token], "sum_chips_w_in"), "join_w_in")
    red = [f.reshape(2 * f.shape[1], D) for f in w_in_joined + _join_wait(join_handle, w_in_joined, "join_ffn_wait")]

    big_g = dict(zip(BIG, red))
    weights = dict(meta_tokens=meta_tokens, ln_in_g=ln_in_g, ln_in_b=ln_in_b, w_in=w_in, b_in=b_in,
                   w_gate_lr2=w_gate_lr2, b_gate_lr2=b_gate_lr2, attn_sinks=attn_sinks, gla_norm_g=gla_norm_g,
                   w_out=w_out, ln1_g=ln1_g, ln1_b=ln1_b, w_ffn_gate=w_ffn_gate, w_ffn_up=w_ffn_up,
                   w_ffn_down=w_ffn_down, ln2_g=ln2_g, ln2_b=ln2_b)
    m_in = dict(meta_tokens=m_meta_tokens, ln_in_g=m_ln_in_g, ln_in_b=m_ln_in_b, w_in=m_w_in, b_in=m_b_in,
                w_gate_lr2=m_w_gate_lr2, b_gate_lr2=m_b_gate_lr2, attn_sinks=m_attn_sinks, gla_norm_g=m_gla_norm_g,
                w_out=m_w_out, ln1_g=m_ln1_g, ln1_b=m_ln1_b, w_ffn_gate=m_w_ffn_gate, w_ffn_up=m_w_ffn_up,
                w_ffn_down=m_w_ffn_down, ln2_g=m_ln2_g, ln2_b=m_ln2_b)
    v_in = dict(meta_tokens=v_meta_tokens, ln_in_g=v_ln_in_g, ln_in_b=v_ln_in_b, w_in=v_w_in, b_in=v_b_in,
                w_gate_lr2=v_w_gate_lr2, b_gate_lr2=v_b_gate_lr2, attn_sinks=v_attn_sinks, gla_norm_g=v_gla_norm_g,
                w_out=v_w_out, ln1_g=v_ln1_g, ln1_b=v_ln1_b, w_ffn_gate=v_w_ffn_gate, w_ffn_up=v_w_ffn_up,
                w_ffn_down=v_w_ffn_down, ln2_g=v_ln2_g, ln2_b=v_ln2_b)
    names = list(weights)
    big_names = ("w_in", "w_out", "w_ffn_gate", "w_ffn_up", "w_ffn_down")

    grads, delta, new_m, new_v = {}, {}, {}, {}
    flips = [(lambda a: a.T) if kk in ("w_g", "w_u") else (lambda a: a) for kk in BIG[1:]]
    by_row = lambda a: jnp.transpose(a, (2, 0, 1))
    updated, updated_w_in = _adamw(
        [(flip(weights[k][0]), big_g[kk], flip(m_in[k][0]), flip(v_in[k][0]))
         for k, kk, flip in zip(big_names[1:], BIG[1:], flips)],
        (by_row(w_in), big_g["w_in"], by_row(m_w_in), by_row(v_w_in)), chip.astype(jnp.int32).reshape(1), r_in % BF16_ROWS)
    for k, flip, results in zip(big_names[1:], flips, updated):
        grads[k], delta[k], new_m[k], new_v[k] = (flip(t)[None] for t in results)
    grads["w_in"], delta["w_in"], new_m["w_in"], new_v["w_in"] = (jnp.transpose(t, (1, 2, 0)) for t in updated_w_in)
    small_in = [tuple(src[k].reshape(shape) for src in (weights, m_in, v_in)) for k, shape in SMALL]
    place = jnp.stack([2 * chip + lax.axis_index("c"), chip]).astype(jnp.int32)
    small_own, small_all = _small_wait(small_handle, [updated[0][0]])
    small_out, tail_row = _adamw_small(place, small_all, small_own, small_in)
    for (k, _), results in zip(SMALL, small_out):
        grads[k], delta[k], new_m[k], new_v[k] = (r.reshape(weights[k].shape) for r in results)

    return (tail_row[0, TAIL_LOSS], dx[None], *[grads[k] for k in names], *[delta[k] for k in names], *[new_m[k] for k in names],
            *[new_v[k] for k in names])
```

```python
import jax
import jax.numpy as jnp
from jax import lax
from jax.experimental import pallas as pl
from jax.experimental.pallas import tpu as pltpu

F32 = jnp.float32
BF16 = jnp.bfloat16
MESH = pl.DeviceIdType.MESH

D = 1024
SEQ = 4096
N_META = 16
SWA_HEADS, SWA_KV_HEADS, DH = 8, 2, 64
WINDOW = 128
GLA_HEADS, DK, DV = 4, 64, 128
GLA_TAU = 16.0
CH = 64
D_FF = 2816
D_IN = 2320
LN_EPS = 1e-5
RMS_EPS = 1e-6
ALPHA = 2.0 ** 0.25
NEG = -1e30
ADAM_LR, ADAM_B1, ADAM_B2, ADAM_EPS, ADAM_WD, ADAM_STEP = 0.001, 0.9, 0.999, 1e-8, 0.01, 10
O_QS, O_KS, O_VS, O_QG, O_KG, O_VG, O_RG, O_LR = 0, 512, 640, 768, 1024, 1280, 1792, 2304

LANE = 128
BLK = WINDOW
GATE_RANK = 16
D_IN_P = D_IN + LANE - GATE_RANK
META_OFF = CH - N_META
HEAD_POS = (0, 4, 1, 5, 2, 6, 3, 7)
LN_ROWS = 512
TOKEN = (8, LANE)
N_CHIPS = 4
SHARD_ROWS = dict(w_in=D_IN // N_CHIPS, w_out=D // N_CHIPS, w_g=D_FF // N_CHIPS, w_u=D_FF // N_CHIPS,
                  w_d=D_FF // N_CHIPS)
SMALL_ROWS = 48
BF16_ROWS = 16
W_IN_WIN = -(-SHARD_ROWS["w_in"] // (2 * BF16_ROWS)) * 2 * BF16_ROWS
W_IN_STARTS = tuple(s * SHARD_ROWS["w_in"] // BF16_ROWS * BF16_ROWS for s in range(N_CHIPS))
VMEM_CAP_MB = 64
VMEM_SPARE_MB = 6


def _lp():
    return SEQ + BLK


def _row_tile(cap):
    lp = _lp()
    return max(t for t in range(16, cap + 1, 16) if lp % t == 0)


def _params(vmem_mb, **kw):
    assert vmem_mb <= VMEM_CAP_MB - VMEM_SPARE_MB
    return pltpu.CompilerParams(vmem_limit_bytes=vmem_mb << 20, **kw)


def _seq(n=1):
    return ("arbitrary",) * n


def _const(shape):
    return pl.BlockSpec(shape, lambda *_: (0,) * len(shape), pipeline_mode=pl.Buffered(1))


def _acc(shape):
    return pl.BlockSpec(shape, lambda *_: (0,) * len(shape))


def _rows(tm, width):
    return pl.BlockSpec((tm, width), lambda i: (i, 0))


def _dot(a, b):
    return jnp.dot(a.astype(BF16), b.astype(BF16), preferred_element_type=F32)


def _dot_nt(a, b):
    return lax.dot_general(a.astype(BF16), b.astype(BF16), (((1,), (1,)), ((), ())), preferred_element_type=F32)


def _dot_tn(a, b):
    return lax.dot_general(a.astype(BF16), b.astype(BF16), (((0,), (0,)), ((), ())), preferred_element_type=F32)


def _dot_exact(a, b):
    return jnp.dot(a, b, precision=lax.Precision.HIGHEST, preferred_element_type=F32)


def _ln_stats(x):
    mu = jnp.mean(x, axis=-1, keepdims=True)
    xc = x - mu
    rstd = lax.rsqrt(jnp.mean(xc * xc, axis=-1, keepdims=True) + LN_EPS)
    return xc * rstd, rstd


def _ln_bwd(dy, xhat, rstd, g):
    dxh = dy * g
    return rstd * (dxh - jnp.mean(dxh, axis=-1, keepdims=True) - xhat * jnp.mean(dxh * xhat, axis=-1, keepdims=True))


def _sigmoid(x):
    return 1.0 / (1.0 + jnp.exp(-x))


def _iota(shape, dim):
    return lax.broadcasted_iota(jnp.int32, shape, dim)


def _hbm(*arrays):
    return tuple(pltpu.with_memory_space_constraint(a, pltpu.HBM) for a in arrays)


def _ln_in_fwd_real(x, g, b, token):
    tr = min(LN_ROWS, SEQ)

    def body(x_ref, g_ref, b_ref, token_ref, h_ref):
        xhat, _ = _ln_stats(x_ref[...])
        h_ref[...] = xhat * g_ref[...] + b_ref[...]

    return pl.pallas_call(
        body, name="ln_in_fwd", grid=(SEQ // tr,),
        in_specs=[_rows(tr, D), _const((1, D)), _const((1, D)), _const(TOKEN)],
        out_specs=_rows(tr, D),
        out_shape=pltpu.HBM((_lp(), D), F32),
        compiler_params=_params(32, dimension_semantics=_seq()),
    )(*_hbm(x, g, b), token)


def _ln_in_fwd_meta(h_real, meta_ext, g, b):
    def meta_body(m_ref, g_ref, b_ref, real_ref, h_ref):
        xhat, _ = _ln_stats(m_ref[...])
        h_ref[...] = xhat * g_ref[...] + b_ref[...]

    return pl.pallas_call(
        meta_body, name="ln_in_fwd_meta", grid=(1,),
        in_specs=[_const((BLK, D)), _const((1, D)), _const((1, D)), pl.BlockSpec(memory_space=pl.ANY)],
        out_specs=pl.BlockSpec((BLK, D), lambda i: (SEQ // BLK, 0)),
        out_shape=pltpu.HBM((_lp(), D), F32),
        input_output_aliases={3: 0},
        compiler_params=_params(16, dimension_semantics=_seq()),
    )(*_hbm(meta_ext, g, b, h_real))


def _in_proj(h0, w_in_windows, b_in_p, wg2_p, bg2):
    tm = _row_tile(384)
    lp = _lp()
    widths = (512, 128, 128, 256, 256, 512, 512, 128)
    offs = (O_QS, O_KS, O_VS, O_QG, O_KG, O_VG, O_RG, O_LR)
    shard = SHARD_ROWS["w_in"]

    def body(h_ref, win_ref, b_ref, wg2_ref, bg2_ref, *outs):
        w_ref = outs[9]

        @pl.when(pl.program_id(0) == 0)
        def _():
            for s in range(N_CHIPS):
                w_ref[shard * s:shard * (s + 1), :] = win_ref[s, 0:shard, :]
            w_ref[D_IN:D_IN_P, :] = jnp.zeros((D_IN_P - D_IN, D), BF16)

        proj = _dot_nt(h_ref[...], w_ref[...]) + b_ref[...]
        for pos, h in enumerate(HEAD_POS):
            outs[0][:, pos * DH:(pos + 1) * DH] = proj[:, O_QS + h * DH:O_QS + (h + 1) * DH]
        for o_ref, off, wd in zip(outs[1:8], offs[1:], widths[1:]):
            o_ref[...] = proj[:, off:off + wd]
        outs[8][...] = _dot(proj[:, O_LR:O_LR + LANE], wg2_ref[...]) + bg2_ref[...]

    return pl.pallas_call(
        body, name="in_proj", grid=(lp // tm,),
        in_specs=[_rows(tm, D), _const(w_in_windows.shape), _const((1, D_IN_P)), _const((LANE, 256)), _const((1, 256))],
        out_specs=[_rows(tm, w) for w in widths] + [_rows(tm, 256), _acc((D_IN_P, D))],
        out_shape=[pltpu.HBM((lp, w), F32) for w in widths] + [pltpu.HBM((lp, 256), F32), pltpu.HBM((D_IN_P, D), BF16)],
        compiler_params=_params(48, dimension_semantics=_seq()),
    )(*_hbm(h0, w_in_windows, b_in_p, wg2_p, bg2))


def _swa_masks(n):
    nb = SEQ // BLK
    is_meta = n == nb
    ri = _iota((BLK, BLK), 0)
    cj = _iota((BLK, BLK), 1)
    meta_col = ((cj >= META_OFF) & (cj < CH)).astype(jnp.int32)
    meta_q = meta_col * ((cj <= ri) & (ri < CH)).astype(jnp.int32)
    valid_m = jnp.where(is_meta, meta_q, meta_col) > 0
    dist_m = jnp.where(is_meta, ri - cj, n * BLK + ri + CH - cj).astype(F32)
    valid_p = jnp.where((n >= 1) & (n < nb), (cj > ri).astype(jnp.int32), 0) > 0
    dist_p = (ri + BLK - cj).astype(F32)
    valid_c = jnp.where(n < nb, (cj <= ri).astype(jnp.int32), 0) > 0
    dist_c = (ri - cj).astype(F32)
    return (dist_m, dist_p, dist_c), (valid_m, valid_p, valid_c)


def _swa_bias(n):
    dists, valids = _swa_masks(n)
    return (jnp.concatenate([-d for d in dists], axis=1),
            jnp.concatenate([jnp.where(v, 0.0, NEG) for v in valids], axis=1))


def _swa_half(ref, pos, scale=1.0):
    col = ref[:, (pos // 2) * LANE:(pos // 2 + 1) * LANE]
    lane = _iota((BLK, LANE), 1)
    mine = lane < DH if pos % 2 == 0 else lane >= DH
    return jnp.where(mine, col * scale, 0.0).astype(BF16)


def _swa_merge(even, odd):
    return jnp.where(_iota((BLK, LANE), 1) < DH, even, odd)


def _swa_softmax(t, sink):
    m = jnp.maximum(jnp.max(t, axis=-1, keepdims=True), sink)
    e = jnp.exp(t - m)
    e_sink = jnp.exp(sink - m)
    inv = 1.0 / (jnp.sum(e, axis=-1, keepdims=True) + e_sink)
    return e * inv, e_sink * inv


def _swa_kv_specs(width):
    nb = SEQ // BLK
    return [pl.BlockSpec((BLK, width), lambda n: (nb, 0)),
            pl.BlockSpec((BLK, width), lambda n: (jnp.clip(n - 1, 0, nb - 1), 0)),
            pl.BlockSpec((BLK, width), lambda n: (jnp.minimum(n, nb), 0))]


def _swa_fwd(sinks, qs, ks, vs):
    nb = SEQ // BLK
    heads = range(SWA_HEADS)

    def body(sink_ref, q_ref, km_ref, kp_ref, kc_ref, vm_ref, vp_ref, vc_ref, o_ref):
        negdist, maskbias = _swa_bias(pl.program_id(0))
        k_all = jnp.concatenate([km_ref[...], kp_ref[...], kc_ref[...]], axis=0).astype(BF16)
        v_all = jnp.concatenate([vm_ref[...], vp_ref[...], vc_ref[...]], axis=0).astype(BF16)
        q = [_swa_half(q_ref, pos, DH ** -0.5) for pos in heads]
        t = [_dot_nt(q[pos], k_all) + (2.0 ** -(HEAD_POS[pos] + 1) * negdist + maskbias) for pos in heads]
        p = [_swa_softmax(t[pos], sink_ref[HEAD_POS[pos]])[0].astype(BF16) for pos in heads]
        o = [_dot(p[pos], v_all) for pos in heads]
        for col in range(SWA_HEADS // 2):
            o_ref[:, col * LANE:(col + 1) * LANE] = _swa_merge(o[2 * col], o[2 * col + 1])

    kvw = SWA_KV_HEADS * DH
    return pl.pallas_call(
        body, name="swa_fwd", grid=(nb + 1,),
        in_specs=[pl.BlockSpec(memory_space=pltpu.SMEM), _rows(BLK, SWA_HEADS * DH)] + _swa_kv_specs(kvw) + _swa_kv_specs(kvw),
        out_specs=_rows(BLK, SWA_HEADS * DH),
        out_shape=pltpu.HBM((_lp(), SWA_HEADS * DH), F32),
        compiler_params=_params(16, dimension_semantics=_seq()),
    )(sinks, *_hbm(qs, ks, ks, ks, vs, vs, vs))


GLA_PER_STEP = BLK // CH


def _gla_block(s):
    nb = SEQ // BLK
    return jnp.where(s == 0, nb, s - 1)


def _gla_rowmask(s):
    ri = _iota((BLK, 1), 0)
    m = jnp.where(s == 0, ((ri >= META_OFF) & (ri < CH)).astype(jnp.int32), 1)
    return (m > 0).astype(F32) + jnp.zeros((BLK, 1), F32)


def _gla_chunk_masks():
    r, c = _iota((BLK, BLK), 0), _iota((BLK, BLK), 1)
    same = ((r < CH) & (c < CH)) | ((r >= CH) & (c >= CH))
    return same & (r >= c), same & (r <= c), same


def _gla_decay(z, rmask):
    log_g = (jnp.minimum(z, 0.0) - jnp.log1p(jnp.exp(-jnp.abs(z)))) * (rmask / GLA_TAU)
    lower, _, same = _gla_chunk_masks()
    return _dot_exact(lower.astype(F32), log_g), _dot_exact(same.astype(F32), log_g)


def _gla_slices(c, h):
    return slice(c * CH, (c + 1) * CH), slice(h * DK, (h + 1) * DK), slice(h * DV, (h + 1) * DV)


def _gla_fwd(qg, kg, vg, z):
    steps = SEQ // BLK + 1
    kw, vw = GLA_HEADS * DK, GLA_HEADS * DV
    pairs = [(c, h) for c in range(GLA_PER_STEP) for h in range(GLA_HEADS)]

    def body(q_ref, k_ref, v_ref, z_ref, o_ref, st_ref, st):
        s = pl.program_id(0)

        @pl.when(s == 0)
        def _():
            st[...] = jnp.zeros_like(st)

        rmask = _gla_rowmask(s)
        b, b_last = _gla_decay(z_ref[...], rmask)
        q = q_ref[...] * (rmask * DK ** -0.5)
        k = k_ref[...] * rmask
        v = v_ref[...] * rmask
        qe = q * jnp.exp(b)
        ke = k * jnp.exp(-b)
        kd = k * jnp.exp(b_last - b)
        e_last = jnp.exp(b_last)
        causal = _iota((CH, CH), 0) >= _iota((CH, CH), 1)
        a, upd, intra = {}, {}, {}
        for c, h in pairs:
            rows, ks, vs_ = _gla_slices(c, h)
            a[c, h] = jnp.where(causal, _dot_nt(qe[rows, ks], ke[rows, ks]), 0.0)
            upd[c, h] = _dot_tn(v[rows, vs_], kd[rows, ks])
        for c, h in pairs:
            rows, ks, vs_ = _gla_slices(c, h)
            intra[c, h] = _dot(a[c, h], v[rows, vs_])
        state = st[...]
        for c in range(GLA_PER_STEP):
            st_ref[0, c] = state
            for h in range(GLA_HEADS):
                rows, ks, vs_ = _gla_slices(c, h)
                o_ref[rows, vs_] = intra[c, h] + _dot_nt(qe[rows, ks], state[:, ks])
            state = state * e_last[c * CH:c * CH + 1] + jnp.concatenate([upd[c, h] for h in range(GLA_HEADS)], axis=1)
        st[...] = state

    blk = lambda w: pl.BlockSpec((BLK, w), lambda s: (_gla_block(s), 0))
    return pl.pallas_call(
        body, name="gla_fwd", grid=(steps,),
        in_specs=[blk(kw), blk(kw), blk(vw), blk(kw)],
        out_specs=[blk(vw), pl.BlockSpec((1, GLA_PER_STEP, DV, kw), lambda s: (s, 0, 0, 0))],
        out_shape=[pltpu.HBM((_lp(), vw), F32), pltpu.HBM((steps, GLA_PER_STEP, DV, kw), F32)],
        scratch_shapes=[pltpu.VMEM((DV, kw), F32)],
        compiler_params=_params(16, dimension_semantics=_seq()),
    )(*_hbm(qg, kg, vg, z))


def _post_mix(o_s, o_gla, r_g, h0, gn4, w_out, g1, b1, token):
    tm = _row_tile(384)
    lp = _lp()

    def body(os_ref, og_ref, r_ref, h0_ref, gn_ref, w_ref, g_ref, b_ref, token_ref, o_ref, pre_ref, h1_ref):
        for pos, h in enumerate(HEAD_POS):
            o_ref[:, h * DH:(h + 1) * DH] = os_ref[:, pos * DH:(pos + 1) * DH].astype(BF16)
        for h in range(GLA_HEADS):
            hs = slice(h * DV, (h + 1) * DV)
            xg = og_ref[:, hs]
            n = xg * lax.rsqrt(jnp.mean(xg * xg, axis=-1, keepdims=True) + RMS_EPS) * gn_ref[...]
            r = r_ref[:, hs]
            o_ref[:, 512 + h * DV:512 + (h + 1) * DV] = (n * (r * _sigmoid(r))).astype(BF16)
        pre = ALPHA * h0_ref[...] + _dot(o_ref[...], w_ref[...])
        pre_ref[...] = pre
        xhat, _ = _ln_stats(pre)
        h1_ref[...] = xhat * g_ref[...] + b_ref[...]

    return pl.pallas_call(
        body, name="post_mix", grid=(lp // tm,),
        in_specs=[_rows(tm, 512), _rows(tm, 512), _rows(tm, 512), _rows(tm, D), _const((1, DV)), _const((D, D)),
                  _const((1, D)), _const((1, D)), _const(TOKEN)],
        out_specs=[_rows(tm, D), _rows(tm, D), _rows(tm, D)],
        out_shape=[pltpu.HBM((lp, D), BF16), pltpu.HBM((lp, D), F32),
                   pltpu.HBM((lp, D), F32)],
        compiler_params=_params(32, dimension_semantics=_seq()),
    )(*_hbm(o_s, o_gla, r_g, h0, gn4, w_out, g1, b1), token)


def _ffn_fwd_loss_bwd(h1, wg_t, wu_t, wd, target, g2, b2):
    lp = _lp()
    tm = max(t for t in range(BLK, 384 + 1, BLK) if lp % t == 0)
    steps = lp // tm
    last_blk = SEQ // BLK - 1
    half = D_FF // 2
    n_t = tm // BLK

    def body(*refs):
        h_ref, wg_ref, wu_ref, wd_ref = refs[:4]
        t_refs = refs[4:4 + n_t]
        g2_ref, b2_ref, a_ref, dgate_ref, dup_ref, dp_ref, loss_ref, dg_ref, db_ref, g_s, u_s, acc = refs[4 + n_t:]
        i = pl.program_id(0)

        @pl.when(i == 0)
        def _():
            acc[...] = jnp.zeros_like(acc)
            dg_ref[...] = jnp.zeros_like(dg_ref)
            db_ref[...] = jnp.zeros_like(db_ref)

        h = h_ref[...]
        hb = h.astype(BF16)
        pre = ALPHA * h
        for j in range(2):
            cols = slice(j * half, (j + 1) * half)
            g = _dot_nt(hb, wg_ref[cols, :])
            u = _dot_nt(hb, wu_ref[cols, :])
            g_s[:, cols] = g
            u_s[:, cols] = u
            pre = pre + _dot(g * _sigmoid(g) * u, wd_ref[cols, :])
        xhat, rstd = _ln_stats(pre)
        real = i * tm + _iota((tm, 1), 0) < SEQ
        target_rows = jnp.concatenate([t[...] for t in t_refs], axis=0)
        diff = jnp.where(real, xhat * g2_ref[...] + b2_ref[...] - target_rows, 0.0)
        acc[...] += jnp.sum(diff * diff, axis=0, keepdims=True)
        dy = diff * (1.0 / D)
        dpre = _ln_bwd(dy, xhat, rstd, g2_ref[...])
        dp_ref[...] = dpre
        dg_ref[...] += jnp.sum(dy * xhat, axis=0, keepdims=True)
        db_ref[...] += jnp.sum(dy, axis=0, keepdims=True)
        dpb = dpre.astype(BF16)
        for j in range(2):
            cols = slice(j * half, (j + 1) * half)
            g, u = g_s[:, cols], u_s[:, cols]
            sg = _sigmoid(g)
            silu = g * sg
            da = _dot_nt(dpb, wd_ref[cols, :])
            a_ref[:, cols] = (silu * u).astype(BF16)
            dgate_ref[:, cols] = (da * u * (sg * (1.0 + g * (1.0 - sg)))).astype(BF16)
            dup_ref[:, cols] = (da * silu).astype(BF16)

        @pl.when(i == steps - 1)
        def _():
            loss_ref[...] = jnp.zeros_like(loss_ref) + (0.5 / D) * jnp.sum(acc[...], axis=1, keepdims=True)

    t_spec = lambda k: pl.BlockSpec((BLK, D), lambda i: (jnp.minimum(i * n_t + k, last_blk), 0))
    return pl.pallas_call(
        body, name="ffn_fwd_loss_bwd", grid=(steps,),
        in_specs=[_rows(tm, D), _const((D_FF, D)), _const((D_FF, D)), _const((D_FF, D))]
        + [t_spec(k) for k in range(n_t)] + [_const((1, D)), _const((1, D))],
        out_specs=[_rows(tm, D_FF), _rows(tm, D_FF), _rows(tm, D_FF), _rows(tm, D), _acc((1, LANE)), _acc((1, D)),
                   _acc((1, D))],
        out_shape=[pltpu.HBM((lp, D_FF), BF16)] * 3 + [pltpu.HBM((lp, D), F32), pltpu.HBM((1, LANE), F32),
                                                         pltpu.HBM((1, D), F32), pltpu.HBM((1, D), F32)],
        scratch_shapes=[pltpu.VMEM((tm, D_FF), F32), pltpu.VMEM((tm, D_FF), F32), pltpu.VMEM((1, D), F32)],
        compiler_params=_params(58, dimension_semantics=_seq()),
    )(*_hbm(h1, wg_t, wu_t, wd, *[target] * n_t, g2, b2))


def _ffn_out_bwd(dpre2, dgate, dup, pre1, wg_t, wu_t, g1, w_out, o_gla, r_g, gn4):
    tm = _row_tile(384)
    lp = _lp()

    def body(dp_ref, dg_ref, du_ref, p1_ref, wg_ref, wu_ref, g1_ref, w_ref, og_ref, r_ref, gn_ref,
             dp1_ref, dg1_ref, db1_ref, dos_ref, dog_ref, dr_ref, dgn_ref):
        @pl.when(pl.program_id(0) == 0)
        def _():
            for acc_ref in (dg1_ref, db1_ref, dgn_ref):
                acc_ref[...] = jnp.zeros_like(acc_ref)

        dh1 = ALPHA * dp_ref[...] + _dot(dg_ref[...], wg_ref[...]) + _dot(du_ref[...], wu_ref[...])
        xhat, rstd1 = _ln_stats(p1_ref[...])
        dpre1 = _ln_bwd(dh1, xhat, rstd1, g1_ref[...])
        dp1_ref[...] = dpre1
        dg1_ref[...] += jnp.sum(dh1 * xhat, axis=0, keepdims=True)
        db1_ref[...] += jnp.sum(dh1, axis=0, keepdims=True)

        do = _dot_nt(dpre1, w_ref[...])
        for pos, h in enumerate(HEAD_POS):
            dos_ref[:, pos * DH:(pos + 1) * DH] = do[:, h * DH:(h + 1) * DH]
        gn = gn_ref[...]
        for h in range(GLA_HEADS):
            hs = slice(h * DV, (h + 1) * DV)
            xg = og_ref[:, hs]
            rstd = lax.rsqrt(jnp.mean(xg * xg, axis=-1, keepdims=True) + RMS_EPS)
            nx = xg * rstd
            r = r_ref[:, hs]
            sr = _sigmoid(r)
            d_o = do[:, 512 + h * DV:512 + (h + 1) * DV]
            dr_ref[:, hs] = d_o * (nx * gn) * (sr * (1.0 + r * (1.0 - sr)))
            dn = d_o * (r * sr)
            dgn_ref[...] += jnp.sum(dn * nx, axis=0, keepdims=True)
            dnx = dn * gn
            dog_ref[:, hs] = rstd * (dnx - nx * jnp.mean(dnx * nx, axis=-1, keepdims=True))

    return pl.pallas_call(
        body, name="ffn_out_bwd", grid=(lp // tm,),
        in_specs=[_rows(tm, D), _rows(tm, D_FF), _rows(tm, D_FF), _rows(tm, D), _const((D_FF, D)), _const((D_FF, D)),
                  _const((1, D)), _const((D, D)), _rows(tm, 512), _rows(tm, 512), _const((1, DV))],
        out_specs=[_rows(tm, D), _acc((1, D)), _acc((1, D)), _rows(tm, 512), _rows(tm, 512), _rows(tm, 512),
                   _acc((1, DV))],
        out_shape=[pltpu.HBM((lp, D), F32), pltpu.HBM((1, D), F32), pltpu.HBM((1, D), F32)]
        + [pltpu.HBM((lp, 512), F32)] * 3 + [pltpu.HBM((1, DV), F32)],
        compiler_params=_params(48, dimension_semantics=_seq()),
    )(*_hbm(dpre2, dgate, dup, pre1, wg_t, wu_t, g1, w_out, o_gla, r_g, gn4))


def _atb(a, b, name, token=None, windows=None):
    lp = _lp()
    tm = _row_tile(1408)
    n, w = a.shape[1], b.shape[1]
    bw = 512 if n * w * 4 > (4 << 20) else w
    tokens = [] if token is None else [token]
    steps = lp // tm

    def body(a_ref, b_ref, *rest):
        o_ref, acc_ref = rest[len(tokens):] if windows else (rest[-1], rest[-1])

        @pl.when(pl.program_id(1) == 0)
        def _():
            acc_ref[...] = jnp.zeros_like(acc_ref)

        acc_ref[...] += _dot_tn(a_ref[...], b_ref[...])

        if windows:
            @pl.when(pl.program_id(1) == steps - 1)
            def _():
                for s, start in enumerate(windows[0]):
                    o_ref[s] = acc_ref[start:start + windows[1], :]

    if windows:
        count, height = len(windows[0]), windows[1]
        out_spec, out_shape = pl.BlockSpec((count, height, bw), lambda j, k: (0, 0, j)), (count, height, w)
    else:
        out_spec, out_shape = pl.BlockSpec((n, bw), lambda j, k: (0, j)), (n, w)
    return pl.pallas_call(
        body, name=name, grid=(w // bw, steps),
        in_specs=[pl.BlockSpec((tm, n), lambda j, k: (k, 0)), pl.BlockSpec((tm, bw), lambda j, k: (k, j))]
        + [_const(TOKEN)] * len(tokens),
        out_specs=out_spec, out_shape=pltpu.HBM(out_shape, F32),
        scratch_shapes=[pltpu.VMEM((n, bw), F32)] if windows else [],
        compiler_params=_params(48, dimension_semantics=_seq(2)),
    )(*_hbm(a, b), *tokens)


def _gla_bwd(qg, kg, vg, z, do_gla, st_all, token):
    steps = SEQ // BLK + 1
    kw, vw = GLA_HEADS * DK, GLA_HEADS * DV
    pairs = [(c, h) for c in range(GLA_PER_STEP) for h in range(GLA_HEADS)]
    heads = range(GLA_HEADS)

    def body(q_ref, k_ref, v_ref, z_ref, do_ref, st_ref, token_ref, dq_ref, dk_ref, dv_ref, dz_ref, dst):
        @pl.when(pl.program_id(0) == 0)
        def _():
            dst[...] = jnp.zeros_like(dst)

        rmask = _gla_rowmask(steps - 1 - pl.program_id(0))
        zz = z_ref[...]
        b, b_last = _gla_decay(zz, rmask)
        e_b, e_nb, e_kd, e_last = jnp.exp(b), jnp.exp(-b), jnp.exp(b_last - b), jnp.exp(b_last)
        q = q_ref[...] * (rmask * DK ** -0.5)
        k = k_ref[...] * rmask
        v = v_ref[...] * rmask
        qe, ke, kd = q * e_b, k * e_nb, k * e_kd
        d_o = do_ref[...]
        causal = _iota((CH, CH), 0) >= _iota((CH, CH), 1)
        a, da, dqe, dke, dv_intra, carry = {}, {}, {}, {}, {}, {}
        for c, h in pairs:
            rows, ks, vs_ = _gla_slices(c, h)
            a[c, h] = jnp.where(causal, _dot_nt(qe[rows, ks], ke[rows, ks]), 0.0)
            da[c, h] = jnp.where(causal, _dot_nt(d_o[rows, vs_], v[rows, vs_]), 0.0)
            carry[c, h] = _dot_tn(d_o[rows, vs_], qe[rows, ks])
        for c, h in pairs:
            rows, ks, vs_ = _gla_slices(c, h)
            dqe[c, h] = _dot(d_o[rows, vs_], st_ref[0, c][:, ks]) + _dot(da[c, h], ke[rows, ks])
            dke[c, h] = _dot_tn(da[c, h], qe[rows, ks])
            dv_intra[c, h] = _dot_tn(a[c, h], d_o[rows, vs_])
        dstate = dst[...]
        dkd, db_decay = {}, {}
        for c in reversed(range(GLA_PER_STEP)):
            for h in heads:
                rows, ks, vs_ = _gla_slices(c, h)
                dkd[c, h] = _dot(v[rows, vs_], dstate[:, ks])
                dv_ref[rows, vs_] = dv_intra[c, h] + _dot_nt(kd[rows, ks], dstate[:, ks])
            chunk_last = e_last[c * CH:c * CH + 1]
            db_decay[c] = jnp.sum(dstate * st_ref[0, c], axis=0, keepdims=True) * chunk_last
            dstate = dstate * chunk_last + jnp.concatenate([carry[c, h] for h in heads], axis=1)
        dst[...] = dstate
        rows_of = lambda parts: jnp.concatenate(
            [jnp.concatenate([parts[c, h] for h in heads], axis=1) for c in range(GLA_PER_STEP)], axis=0)
        dqe_all, dke_all, dkd_all = rows_of(dqe), rows_of(dke), rows_of(dkd)
        dq_ref[...] = dqe_all * e_b * (rmask * DK ** -0.5)
        dk_ref[...] = (dke_all * e_nb + dkd_all * e_kd) * rmask
        dkd_kd = dkd_all * kd
        db = dqe_all * qe - dke_all * ke - dkd_kd
        _, upper, same = _gla_chunk_masks()
        decay_rows = jnp.concatenate([jnp.broadcast_to(db_decay[c], (CH, kw)) for c in range(GLA_PER_STEP)], axis=0)
        dlog_g = _dot_exact(upper.astype(F32), db) + _dot_exact(same.astype(F32), dkd_kd) + decay_rows
        dz_ref[...] = dlog_g * (rmask / GLA_TAU) * _sigmoid(-zz)

    blk = lambda w: pl.BlockSpec((BLK, w), lambda s: (_gla_block(steps - 1 - s), 0))
    return pl.pallas_call(
        body, name="gla_bwd", grid=(steps,),
        in_specs=[blk(kw), blk(kw), blk(vw), blk(kw), blk(vw),
                  pl.BlockSpec((1, GLA_PER_STEP, DV, kw), lambda s: (steps - 1 - s, 0, 0, 0)), _const(TOKEN)],
        out_specs=[blk(kw), blk(kw), blk(vw), blk(kw)],
        out_shape=[pltpu.HBM((_lp(), kw), F32), pltpu.HBM((_lp(), kw), F32),
                   pltpu.HBM((_lp(), vw), F32), pltpu.HBM((_lp(), kw), F32)],
        scratch_shapes=[pltpu.VMEM((DV, kw), F32)],
        compiler_params=_params(16, dimension_semantics=_seq()),
    )(*_hbm(qg, kg, vg, z, do_gla, st_all), token)


def _swa_bwd(sinks, qs, ks, vs, do_s, token):
    nb = SEQ // BLK
    kvw = SWA_KV_HEADS * DH
    scale = DH ** -0.5
    heads = range(SWA_HEADS)

    def body(sink_ref, q_ref, km_ref, kp_ref, kc_ref, vm_ref, vp_ref, vc_ref, do_ref, token_ref,
             dq_ref, dk_ref, dv_ref, dsink_ref, carry_k, carry_v, meta_k, meta_v):
        n = pl.program_id(0)

        @pl.when(n == 0)
        def _():
            for r in (carry_k, carry_v, meta_k, meta_v):
                r[...] = jnp.zeros_like(r)
            dsink_ref[...] = jnp.zeros_like(dsink_ref)

        @pl.when(n <= nb)
        def _():
            negdist, maskbias = _swa_bias(n)
            lane = _iota((1, LANE), 1)
            k_all = jnp.concatenate([km_ref[...], kp_ref[...], kc_ref[...]], axis=0).astype(BF16)
            v_all = jnp.concatenate([vm_ref[...], vp_ref[...], vc_ref[...]], axis=0).astype(BF16)
            q = [_swa_half(q_ref, pos, scale) for pos in heads]
            d_o = [_swa_half(do_ref, pos) for pos in heads]
            t = [_dot_nt(q[pos], k_all) + (2.0 ** -(HEAD_POS[pos] + 1) * negdist + maskbias) for pos in heads]
            dp = [_dot_nt(d_o[pos], v_all) for pos in heads]
            soft = [_swa_softmax(t[pos], sink_ref[HEAD_POS[pos]]) for pos in heads]
            p = [s[0] for s in soft]
            delta = [jnp.sum(p[pos] * dp[pos], axis=-1, keepdims=True) for pos in heads]
            ds = [(p[pos] * (dp[pos] - delta[pos])).astype(BF16) for pos in heads]
            dq = [_dot(ds[pos], k_all) for pos in heads]
            for col in range(SWA_HEADS // 2):
                dq_ref[:, col * LANE:(col + 1) * LANE] = scale * _swa_merge(dq[2 * col], dq[2 * col + 1])
            dsink = jnp.zeros((1, LANE), F32)
            for pos in heads:
                dsink = dsink + jnp.where(lane == HEAD_POS[pos],
                                          -jnp.sum(soft[pos][1] * delta[pos], axis=0, keepdims=True), 0.0)
            dsink_ref[...] += dsink
            dk3 = _dot_tn(jnp.concatenate(q, axis=0), jnp.concatenate(ds, axis=0)).T
            dv3 = _dot_tn(jnp.concatenate(d_o, axis=0), jnp.concatenate([x.astype(BF16) for x in p], axis=0)).T
            meta_k[...] += dk3[0:BLK]
            meta_v[...] += dv3[0:BLK]
            dk_ref[...] = carry_k[...] + dk3[BLK:2 * BLK]
            dv_ref[...] = carry_v[...] + dv3[BLK:2 * BLK]
            carry_k[...] = dk3[2 * BLK:3 * BLK]
            carry_v[...] = dv3[2 * BLK:3 * BLK]

        @pl.when(n == nb + 1)
        def _():
            dk_ref[...] = meta_k[...]
            dv_ref[...] = meta_v[...]

    kv_out = pl.BlockSpec((BLK, kvw), lambda n: (jnp.where(n == nb + 1, nb, jnp.clip(n - 1, 0, nb - 1)), 0))
    qblk = pl.BlockSpec((BLK, SWA_HEADS * DH), lambda n: (jnp.minimum(n, nb), 0))
    return pl.pallas_call(
        body, name="swa_bwd", grid=(nb + 2,),
        in_specs=[pl.BlockSpec(memory_space=pltpu.SMEM), qblk] + _swa_kv_specs(kvw) + _swa_kv_specs(kvw)
        + [qblk, _const(TOKEN)],
        out_specs=[qblk, kv_out, kv_out, _acc((1, LANE))],
        out_shape=[pltpu.HBM((_lp(), SWA_HEADS * DH), F32), pltpu.HBM((_lp(), kvw), F32),
                   pltpu.HBM((_lp(), kvw), F32), pltpu.HBM((1, LANE), F32)],
        scratch_shapes=[pltpu.VMEM((BLK, kvw), F32)] * 4,
        compiler_params=_params(16, dimension_semantics=_seq()),
    )(sinks, *_hbm(qs, ks, ks, ks, vs, vs, vs, do_s), token)


def _in_bwd(dqs, dks, dvs, dqg, dkg, dvg, drg, dz, dpre1, w_in_t, wg2_p):
    tm = _row_tile(384)
    lp = _lp()
    widths = (512, 128, 128, 256, 256, 512, 512)
    offs = (O_QS, O_KS, O_VS, O_QG, O_KG, O_VG, O_RG)

    def body(*refs):
        parts, (dz_ref, dp1_ref, w_ref, wg2_ref, dproj_ref, dh0_ref, dbin_ref, dbg_ref) = refs[:7], refs[7:]

        @pl.when(pl.program_id(0) == 0)
        def _():
            dbin_ref[...] = jnp.zeros_like(dbin_ref)
            dbg_ref[...] = jnp.zeros_like(dbg_ref)

        for pos, h in enumerate(HEAD_POS):
            val = parts[0][:, pos * DH:(pos + 1) * DH]
            dproj_ref[:, O_QS + h * DH:O_QS + (h + 1) * DH] = val.astype(BF16)
            dbin_ref[:, O_QS + h * DH:O_QS + (h + 1) * DH] += jnp.sum(val, axis=0, keepdims=True)
        for p_ref, off, wd in zip(parts[1:], offs[1:], widths[1:]):
            val = p_ref[...]
            dproj_ref[:, off:off + wd] = val.astype(BF16)
            dbin_ref[:, off:off + wd] += jnp.sum(val, axis=0, keepdims=True)
        dz = dz_ref[...]
        dlr = _dot_nt(dz, wg2_ref[...])
        dproj_ref[:, O_LR:O_LR + LANE] = dlr.astype(BF16)
        dbin_ref[:, O_LR:O_LR + LANE] += jnp.sum(dlr, axis=0, keepdims=True)
        dbg_ref[...] += jnp.sum(dz, axis=0, keepdims=True)
        dh0_ref[...] = ALPHA * dp1_ref[...] + _dot(dproj_ref[...], w_ref[...])

    return pl.pallas_call(
        body, name="in_bwd", grid=(lp // tm,),
        in_specs=[_rows(tm, w) for w in widths] + [_rows(tm, 256), _rows(tm, D), _const((D_IN_P, D)), _const((LANE, 256))],
        out_specs=[_rows(tm, D_IN_P), _rows(tm, D), _acc((1, D_IN_P)), _acc((1, 256))],
        out_shape=[pltpu.HBM((lp, D_IN_P), BF16), pltpu.HBM((lp, D), F32),
                   pltpu.HBM((1, D_IN_P), F32), pltpu.HBM((1, 256), F32)],
        compiler_params=_params(40, dimension_semantics=_seq()),
    )(*_hbm(dqs, dks, dvs, dqg, dkg, dvg, drg, dz, dpre1, w_in_t, wg2_p))


def _ln_in_bwd(x, meta_ext, dh0, g, token):
    tr = min(LN_ROWS, SEQ)

    def ln_bwd(x_ref, dh_ref, g_ref, dx_ref, dg_ref, db_ref, so_far=None):
        @pl.when(pl.program_id(0) == 0)
        def _():
            dg_ref[...] = jnp.zeros_like(dg_ref) if so_far is None else so_far[0][...]
            db_ref[...] = jnp.zeros_like(db_ref) if so_far is None else so_far[1][...]

        xhat, rstd = _ln_stats(x_ref[...])
        dh = dh_ref[...]
        dx_ref[...] = _ln_bwd(dh, xhat, rstd, g_ref[...])
        dg_ref[...] += jnp.sum(dh * xhat, axis=0, keepdims=True)
        db_ref[...] += jnp.sum(dh, axis=0, keepdims=True)

    def body(x_ref, dh_ref, g_ref, token_ref, dx_ref, dg_ref, db_ref):
        ln_bwd(x_ref, dh_ref, g_ref, dx_ref, dg_ref, db_ref)

    def meta_body(m_ref, dh_ref, g_ref, dg_real_ref, db_real_ref, dm_ref, dg_ref, db_ref):
        ln_bwd(m_ref, dh_ref, g_ref, dm_ref, dg_ref, db_ref, (dg_real_ref, db_real_ref))

    sums = [pltpu.HBM((1, D), F32), pltpu.HBM((1, D), F32)]
    dx, dg, db = pl.pallas_call(
        body, name="ln_in_bwd", grid=(SEQ // tr,),
        in_specs=[_rows(tr, D), _rows(tr, D), _const((1, D)), _const(TOKEN)],
        out_specs=[_rows(tr, D), _acc((1, D)), _acc((1, D))],
        out_shape=[pltpu.HBM((SEQ, D), F32)] + sums,
        compiler_params=_params(32, dimension_semantics=_seq()),
    )(*_hbm(x, dh0, g), token)
    dm, dg, db = pl.pallas_call(
        meta_body, name="ln_in_bwd_meta", grid=(1,),
        in_specs=[_const((BLK, D)), pl.BlockSpec((BLK, D), lambda i: (SEQ // BLK, 0))] + [_const((1, D))] * 3,
        out_specs=[_acc((BLK, D)), _acc((1, D)), _acc((1, D))],
        out_shape=[pltpu.HBM((BLK, D), F32)] + sums,
        compiler_params=_params(16, dimension_semantics=_seq()),
    )(*_hbm(meta_ext, dh0, g, dg, db))
    return dx, dm, dg, db


def _local_step(x, target, ln_in_g, ln_in_b, b_in, bg2, sinks, gn, g1, b1, g2, b2,
                token, fetch_first, fetch_rest, fetch_ffn, exchange_ffn, ship_ffn, exchange_w_in, ship_w_in):
    row = lambda v: v.reshape(1, -1).astype(F32)
    b_in_p = jnp.pad(row(b_in), ((0, 0), (0, D_IN_P - D_IN)))
    gn4 = row(gn)
    sinks = sinks.reshape(-1).astype(F32)

    h_real = _ln_in_fwd_real(x, row(ln_in_g), row(ln_in_b), token)
    w_in_windows, meta_full, wg2 = fetch_first([h_real])
    meta_ext = jnp.pad(meta_full, ((META_OFF, BLK - CH), (0, 0)))
    wg2_p = jnp.pad(wg2, ((0, LANE - wg2.shape[0]), (0, 0))).astype(BF16)
    h0 = _ln_in_fwd_meta(h_real, meta_ext, row(ln_in_g), row(ln_in_b))
    qs, ks, vs, qg, kg, vg, rg, glr, z, w_in_t = _in_proj(h0, w_in_windows, b_in_p, wg2_p, row(bg2))
    o_s = _swa_fwd(sinks, qs, ks, vs)
    o_gla, st_all = _gla_fwd(qg, kg, vg, z)
    w_out, token = fetch_rest([o_s, o_gla])
    o, pre1, h1 = _post_mix(o_s, o_gla, rg, h0, gn4, w_out, row(g1), row(b1), token)
    wg_t, wu_t, wd = fetch_ffn([pre1])
    a, dgate, dup, dpre2, loss, dg2, db2 = _ffn_fwd_loss_bwd(h1, wg_t, wu_t, wd, target, row(g2), row(b2))
    dpre1, dg1, db1, do_s, do_gla, drg, dgn = _ffn_out_bwd(dpre2, dgate, dup, pre1, wg_t, wu_t, row(g1), w_out, o_gla,
                                                           rg, gn4)
    dwd = _atb(a, dpre2, "dw_down")
    dwg_t = _atb(dgate, h1, "dw_gate")
    dwu_t = _atb(dup, h1, "dw_up")
    token = exchange_ffn(dict(w_out=_atb(o, dpre1, "dw_out"), w_g=dwg_t, w_u=dwu_t, w_d=dwd))
    dqg, dkg, dvg, dz = _gla_bwd(qg, kg, vg, z, do_gla, st_all, token)
    token = ship_ffn([dqg])
    dqs, dks, dvs, dsinks = _swa_bwd(sinks, qs, ks, vs, do_s, token)
    dproj, dh0, db_in_p, dbg2 = _in_bwd(dqs, dks, dvs, dqg, dkg, dvg, drg, dz, dpre1, w_in_t, wg2_p)
    token = exchange_w_in(_atb(dproj, h0, "dw_in", windows=(W_IN_STARTS, W_IN_WIN)))
    dwg2_p = _atb(glr, dz, "dw_gate_lr2", token)
    token = ship_w_in([dwg2_p])
    dx, dmeta_blk, dg_in, db_in_ln = _ln_in_bwd(x, meta_ext, dh0, row(ln_in_g), token)

    small = dict(meta_blk=dmeta_blk, ln_in_g=dg_in, ln_in_b=db_in_ln, ln1_g=dg1, ln1_b=db1, ln2_g=dg2, ln2_b=db2,
                 b_in_p=db_in_p, wg2_p=dwg2_p, bg2=dbg2, sinks=dsinks, gn=dgn, loss=loss)
    return dx, small


HBM = pl.BlockSpec(memory_space=pltpu.HBM)


def _place():
    return lax.axis_index("x"), lax.axis_index("y"), lax.axis_index("c")


def _other_chips(x, y):
    return [(1 - x, y), (x, 1 - y), (1 - x, 1 - y)]


def _dma_sems(n):
    return pltpu.SemaphoreType.DMA((n,))


def _comm_params():
    return pltpu.CompilerParams(has_side_effects=True)


SEM = pl.BlockSpec(memory_space=pltpu.SEMAPHORE)


PER_ARRAY = dict(gather=3, scatter=3, sibling=N_CHIPS)


def _ici_copies(kind, landing, srcs, lands, send_sems, recv_sems):
    x, y, c = _place()
    mine = 2 * x + y
    copies = []
    for a in range(len(srcs)):
        if kind == "sibling":
            for s in range(N_CHIPS):
                copies.append(pltpu.make_async_remote_copy(
                    srcs[a].at[s, 1 - c], lands[a].at[s], send_sems.at[N_CHIPS * a + s], recv_sems.at[N_CHIPS * a + s],
                    device_id=(x, y, 1 - c), device_id_type=MESH))
            continue
        for j, (px, py) in enumerate(_other_chips(x, y)):
            slab = 2 * px + py if landing else mine
            if kind == "gather":
                src, dst = srcs[a].at[c], lands[a].at[slab, c]
            else:
                src, dst = srcs[a].at[2 * px + py], lands[a].at[slab]
            copies.append(pltpu.make_async_remote_copy(src, dst, send_sems.at[3 * a + j], recv_sems.at[3 * a + j],
                                                       device_id=(px, py, c), device_id_type=MESH))
    return copies


def _split_params():
    return pltpu.CompilerParams(has_side_effects=pltpu.SideEffectType.DATAFLOW_SIDE_EFFECTING)


def _ici_start(kind, srcs, land_shapes, after, name):
    n = len(srcs)
    lands = [pltpu.with_memory_space_constraint(lax.empty(s, a.dtype), pltpu.HBM) for s, a in zip(land_shapes, srcs)]

    def body(*refs):
        outs = refs[2 * n + len(after):]
        for cp in _ici_copies(kind, False, refs[:n], refs[n:2 * n], outs[0], outs[1]):
            cp.start()
        outs[-1][...] = jnp.zeros(TOKEN, F32)

    outs = pl.pallas_call(
        body, name=name, in_specs=[HBM] * (2 * n) + [pl.BlockSpec(memory_space=pl.ANY)] * len(after),
        out_specs=[SEM, SEM] + [HBM] * (2 * n) + [pl.BlockSpec(memory_space=pltpu.VMEM)],
        out_shape=[_dma_sems(PER_ARRAY[kind] * n)] * 2 + [pltpu.HBM(a.shape, a.dtype) for a in list(srcs) + lands]
        + [jax.ShapeDtypeStruct(TOKEN, F32)],
        input_output_aliases={i: 2 + i for i in range(2 * n)},
        compiler_params=_split_params(),
    )(*_hbm(*srcs), *lands, *after)
    return outs[:-1], outs[-1]


def _ici_wait(kind, handle, after, name):
    n = (len(handle) - 2) // 2

    def body(*refs):
        for cp in _ici_copies(kind, True, refs[:n], refs[n:2 * n], refs[2 * n], refs[2 * n + 1]):
            cp.wait_send()
            cp.wait_recv()

    outs = pl.pallas_call(
        body, name=name, in_specs=[HBM] * (2 * n) + [SEM, SEM] + [pl.BlockSpec(memory_space=pl.ANY)] * len(after),
        out_specs=[HBM] * (2 * n), out_shape=[pltpu.HBM(a.shape, a.dtype) for a in handle[2:]],
        input_output_aliases={i: i for i in range(2 * n)},
        compiler_params=_split_params(),
    )(*handle[2:], handle[0], handle[1], *after)
    return list(outs[:n]), list(outs[n:])


def _forward_copies(landing, arrs, send_sems, recv_sems):
    x, y, c = _place()
    copies = []
    for a in range(len(arrs)):
        for j, (px, py) in enumerate(_other_chips(x, y)):
            half = 1 - c if landing else c
            copies.append(pltpu.make_async_remote_copy(
                arrs[a].at[2 * px + py, c], arrs[a].at[2 * px + py, half], send_sems.at[3 * a + j],
                recv_sems.at[3 * a + j], device_id=(x, y, 1 - c), device_id_type=MESH))
    return copies


def _gather_wait_forward(handle, groups, after, name):
    n = (len(handle) - 2) // 2
    assert sum(groups) == n

    def body(*refs):
        outs = refs[2 * n + 2 + len(after):]
        lands, sems = outs[n:2 * n], outs[2 * n:-1]
        arrivals = _ici_copies("gather", True, refs[:n], refs[n:2 * n], refs[2 * n], refs[2 * n + 1])
        sends, first = [], 0
        for g, count in enumerate(groups):
            sends += _forward_copies(False, lands[first:first + count], sems[2 * g], sems[2 * g + 1])
            first += count
        for cp, send in zip(arrivals, sends):
            cp.wait_recv()
            send.start()
        for cp in arrivals:
            cp.wait_send()
        outs[-1][...] = jnp.zeros(TOKEN, F32)

    outs = pl.pallas_call(
        body, name=name, in_specs=[HBM] * (2 * n) + [SEM, SEM] + [pl.BlockSpec(memory_space=pl.ANY)] * len(after),
        out_specs=[HBM] * (2 * n) + [SEM] * (2 * len(groups)) + [pl.BlockSpec(memory_space=pltpu.VMEM)],
        out_shape=[pltpu.HBM(a.shape, a.dtype) for a in handle[2:]]
        + [_dma_sems(3 * count) for count in groups for _ in range(2)] + [jax.ShapeDtypeStruct(TOKEN, F32)],
        input_output_aliases={i: i for i in range(2 * n)},
        compiler_params=_split_params(),
    )(*handle[2:], handle[0], handle[1], *after)
    lands, sems, handles, first = outs[n:2 * n], outs[2 * n:-1], [], 0
    for g, count in enumerate(groups):
        handles.append([sems[2 * g], sems[2 * g + 1], *lands[first:first + count]])
        first += count
    return list(outs[:n]), handles, outs[-1]


def _forward_wait(handle, after, name):
    n = len(handle) - 2

    def body(*refs):
        for cp in _forward_copies(True, refs[:n], refs[n], refs[n + 1]):
            cp.wait_send()
            cp.wait_recv()

    return list(pl.pallas_call(
        body, name=name, in_specs=[HBM] * n + [SEM, SEM] + [pl.BlockSpec(memory_space=pl.ANY)] * len(after),
        out_specs=[HBM] * n, out_shape=[pltpu.HBM(a.shape, a.dtype) for a in handle[2:]],
        input_output_aliases={i: i for i in range(n)},
        compiler_params=_split_params(),
    )(*handle[2:], handle[0], handle[1], *after))


def _add_halves(core, grads, recvs, dtypes, name):
    n = len(grads)
    heights = [g.shape[2] for g in grads]

    def body(c_ref, *refs):
        for a in range(n):
            refs[2 * n + a][...] = (refs[2 * a][0] + refs[2 * a + 1][...]).astype(dtypes[a])

    slab = lambda h: pl.BlockSpec((1, h, D), lambda s, c: (s, 0, 0))
    mine = lambda h: pl.BlockSpec((1, 1, h, D), lambda s, c: (s, c[0], 0, 0))
    return pl.pallas_call(
        body, name=name,
        grid_spec=pltpu.PrefetchScalarGridSpec(
            num_scalar_prefetch=1, grid=(N_CHIPS,),
            in_specs=[spec(h) for h in heights for spec in (mine, slab)], out_specs=[slab(h) for h in heights]),
        out_shape=[pltpu.HBM((N_CHIPS, h, D), dt) for h, dt in zip(heights, dtypes)],
        compiler_params=_params(32, dimension_semantics=_seq()),
    )(core, *_hbm(*[a for pair in zip(grads, recvs) for a in pair]))


N_DEVICES = 2 * N_CHIPS
PEER_FLIPS = [(dx, dy, dc) for dx in (0, 1) for dy in (0, 1) for dc in (0, 1)][1:]


def _small_copies(landing, p_ref, out_ref, send_sems, recv_sems):
    x, y, c = _place()
    flip = lambda v, d: 1 - v if d else v
    copies = []
    for k, flips in enumerate(PEER_FLIPS):
        px, py, pc = (flip(v, d) for v, d in zip((x, y, c), flips))
        slab = 4 * px + 2 * py + pc if landing else 4 * x + 2 * y + c
        copies.append(pltpu.make_async_remote_copy(p_ref, out_ref.at[slab], send_sems.at[k], recv_sems.at[k],
                                                   device_id=(px, py, pc), device_id_type=MESH))
    return copies


def _small_start(pack, after):
    n = len(PEER_FLIPS)
    land = pltpu.with_memory_space_constraint(lax.empty((N_DEVICES,) + pack.shape, F32), pltpu.HBM)

    def body(p_ref, land_ref, *refs):
        outs = refs[len(after):]
        for cp in _small_copies(False, p_ref, land_ref, outs[0], outs[1]):
            cp.start()
        outs[-1][...] = jnp.zeros(TOKEN, F32)

    outs = pl.pallas_call(
        body, name="small_exchange_start", in_specs=[HBM, HBM] + [pl.BlockSpec(memory_space=pl.ANY)] * len(after),
        out_specs=[SEM, SEM, HBM, HBM, pl.BlockSpec(memory_space=pltpu.VMEM)],
        out_shape=[_dma_sems(n), _dma_sems(n), pltpu.HBM(pack.shape, F32), pltpu.HBM(land.shape, F32),
                   jax.ShapeDtypeStruct(TOKEN, F32)],
        input_output_aliases={0: 2, 1: 3},
        compiler_params=_split_params(),
    )(*_hbm(pack), land, *after)
    return outs[:-1], outs[-1]


def _small_wait(handle, after):
    def body(p_ref, land_ref, send_sems, recv_sems, *rest):
        for cp in _small_copies(True, p_ref, land_ref, send_sems, recv_sems):
            cp.wait_send()
            cp.wait_recv()

    return pl.pallas_call(
        body, name="small_exchange_wait", in_specs=[HBM, HBM, SEM, SEM] + [pl.BlockSpec(memory_space=pl.ANY)] * len(after),
        out_specs=[HBM, HBM], out_shape=[pltpu.HBM(a.shape, F32) for a in handle[2:]],
        input_output_aliases={0: 0, 1: 1},
        compiler_params=_split_params(),
    )(handle[2], handle[3], handle[0], handle[1], *after)


def _sum_chips(slots, firsts, rests, after, name):
    n = len(firsts)

    def body(i_ref, *refs):
        outs = refs[4 * n + len(after):]
        for a in range(n):
            first, r1, r2, r3 = refs[4 * a:4 * a + 4]
            outs[a][...] = ((first[...].astype(F32) + r1[...].astype(F32)) + r2[...].astype(F32)) + r3[...].astype(F32)

    slab = lambda h, k: pl.BlockSpec((1, h, D), lambda i, ix: (ix[k], 0, 0))
    heights = [f.shape[1] for f in firsts]
    return pl.pallas_call(
        body, name=name,
        grid_spec=pltpu.PrefetchScalarGridSpec(
            num_scalar_prefetch=1, grid=(1,),
            in_specs=[slab(h, k) for h in heights for k in range(4)] + [pl.BlockSpec(memory_space=pl.ANY)] * len(after),
            out_specs=[slab(h, 4) for h in heights]),
        out_shape=[pltpu.HBM((2, h, D), F32) for h in heights],
        compiler_params=_params(48, dimension_semantics=_seq()),
    )(slots, *_hbm(*[a for f, r in zip(firsts, rests) for a in (f, r, r, r)]), *after)


def _join_copies(landing, arrs, send_sems, recv_sems):
    x, y, c = _place()
    slab = 1 - c if landing else c
    return [pltpu.make_async_remote_copy(arr.at[slab], arr.at[slab], send_sems.at[a], recv_sems.at[a],
                                         device_id=(x, y, 1 - c), device_id_type=MESH) for a, arr in enumerate(arrs)]


def _join_halves(halves, name):
    n = len(halves)

    def body(*refs):
        outs = refs[n:2 * n]
        send_sems, recv_sems = refs[2 * n:]
        sends = _join_copies(False, outs, send_sems, recv_sems)
        for cp in sends:
            cp.start()
        for cp in _join_copies(True, outs, send_sems, recv_sems):
            cp.wait_recv()
        for cp in sends:
            cp.wait_send()

    return list(pl.pallas_call(
        body, name=name, in_specs=[HBM] * n, out_specs=[HBM] * n,
        out_shape=[pltpu.HBM(h.shape, F32) for h in halves],
        input_output_aliases={a: a for a in range(n)},
        scratch_shapes=[_dma_sems(n)] * 2,
        compiler_params=_comm_params(),
    )(*_hbm(*halves)))


def _join_start(halves, name):
    n = len(halves)

    def body(*refs):
        outs = refs[n:]
        for cp in _join_copies(False, refs[:n], outs[0], outs[1]):
            cp.start()
        outs[-1][...] = jnp.zeros(TOKEN, F32)

    outs = pl.pallas_call(
        body, name=name, in_specs=[HBM] * n,
        out_specs=[SEM, SEM] + [HBM] * n + [pl.BlockSpec(memory_space=pltpu.VMEM)],
        out_shape=[_dma_sems(n)] * 2 + [pltpu.HBM(a.shape, a.dtype) for a in halves] + [jax.ShapeDtypeStruct(TOKEN, F32)],
        input_output_aliases={i: 2 + i for i in range(n)},
        compiler_params=_split_params(),
    )(*_hbm(*halves))
    return outs[:-1], outs[-1]


def _join_wait(handle, after, name):
    n = len(handle) - 2

    def body(*refs):
        for cp in _join_copies(True, refs[:n], refs[n], refs[n + 1]):
            cp.wait_send()
            cp.wait_recv()

    return list(pl.pallas_call(
        body, name=name, in_specs=[HBM] * n + [SEM, SEM] + [pl.BlockSpec(memory_space=pl.ANY)] * len(after),
        out_specs=[HBM] * n, out_shape=[pltpu.HBM(a.shape, a.dtype) for a in handle[2:]],
        input_output_aliases={i: i for i in range(n)},
        compiler_params=_split_params(),
    )(*handle[2:], handle[0], handle[1], *after))


def _chip_partials(grads, fetched, wire_dtypes, name):
    core = lax.axis_index("c").astype(jnp.int32).reshape(1)
    return list(_add_halves(core, grads, fetched, wire_dtypes, name))


def _chip_sums(parts, got, after, name):
    x, y, c = _place()
    others = [2 * px + py for px, py in _other_chips(x, y)]
    own_first = jnp.stack([2 * x + y] + others + [c]).astype(jnp.int32)
    return list(_sum_chips(own_first, parts, got, after, name))


ADAMW_STEPS = 8


def _adamw(params, by_row, chip, window_step):
    n = len(params)
    rows, _, cols = by_row[0].shape
    block = lambda shape: pl.BlockSpec((shape[0] // ADAMW_STEPS, shape[1]), lambda i, c: (i, 0))
    assert all(a.shape[0] % (8 * ADAMW_STEPS) == 0 for p in params for a in p)

    def body(c_ref, *refs):
        w_hbm, g_ref, m_hbm, v_hbm = refs[4 * n:4 * n + 4]
        results, (ins_ref, outs_ref, sems) = refs[8 * n + 4:8 * n + 8], refs[8 * n + 8:]
        loads = [pltpu.make_async_copy(src.at[:, 0, :], ins_ref.at[k], sems.at[k])
                 for k, src in enumerate((w_hbm, m_hbm, v_hbm))]
        stores = [pltpu.make_async_copy(outs_ref.at[k], dst.at[:, 0, :], sems.at[3 + k]) for k, dst in enumerate(results)]
        first = pl.program_id(0) == 0

        @pl.when(first)
        def _():
            for cp in loads:
                cp.start()

        for a in range(n):
            w_ref, a_g_ref, m_ref, v_ref = refs[4 * a:4 * a + 4]
            outs = refs[4 * n + 4 + 4 * a:4 * n + 8 + 4 * a]
            g = a_g_ref[...]
            outs[0][...] = g
            outs[1][...], outs[2][...], outs[3][...] = _adamw_math(w_ref[...], g, m_ref[...], v_ref[...])

        @pl.when(first)
        def _():
            for cp in loads:
                cp.wait()
            for lo in range(0, cols, LANE):
                lanes = slice(lo, lo + LANE)
                g = g_ref[0:rows, lanes]
                for s in range(1, N_CHIPS):
                    g = jnp.where(c_ref[0] == s, g_ref[s * window_step:s * window_step + rows, lanes], g)
                outs_ref[0, :, lanes] = g
                outs_ref[1, :, lanes], outs_ref[2, :, lanes], outs_ref[3, :, lanes] = _adamw_math(
                    ins_ref[0, :, lanes], g, ins_ref[1, :, lanes], ins_ref[2, :, lanes])
            for cp in stores:
                cp.start()

        @pl.when(pl.program_id(0) == ADAMW_STEPS - 1)
        def _():
            for cp in stores:
                cp.wait()

    outs = pl.pallas_call(
        body, name="adamw_matrices",
        grid_spec=pltpu.PrefetchScalarGridSpec(
            num_scalar_prefetch=1, grid=(ADAMW_STEPS,),
            in_specs=[block(a.shape) for p in params for a in p] + [HBM, _const(by_row[1].shape), HBM, HBM],
            out_specs=[block(p[0].shape) for p in params for _ in range(4)] + [HBM] * 4,
            scratch_shapes=[pltpu.VMEM((3, rows, cols), F32), pltpu.VMEM((4, rows, cols), F32), _dma_sems(7)]),
        out_shape=[pltpu.HBM(p[0].shape, F32) for p in params for _ in range(4)] + [pltpu.HBM((rows, 1, cols), F32)] * 4,
        compiler_params=_params(48, dimension_semantics=_seq()),
    )(chip, *_hbm(*[a for p in params for a in p], *by_row))
    return [outs[4 * a:4 * a + 4] for a in range(n)], outs[4 * n:]


def _adamw_math(w, g, m, v):
    nm = ADAM_B1 * m + (1.0 - ADAM_B1) * g
    nv = ADAM_B2 * v + (1.0 - ADAM_B2) * (g * g)
    m_hat = nm / (1.0 - ADAM_B1 ** ADAM_STEP)
    v_hat = nv / (1.0 - ADAM_B2 ** ADAM_STEP)
    return -ADAM_LR * (m_hat / (jnp.sqrt(v_hat) + ADAM_EPS) + ADAM_WD * w), nm, nv


SMALL = (("meta_tokens", (N_META, D // N_CHIPS)), ("ln_in_g", (1, D)), ("ln_in_b", (1, D)), ("b_in", (1, D_IN)),
         ("w_gate_lr2", (GATE_RANK, GLA_HEADS * DK // N_CHIPS)), ("b_gate_lr2", (1, GLA_HEADS * DK)),
         ("attn_sinks", (1, SWA_HEADS)),
         ("gla_norm_g", (1, DV)), ("ln1_g", (1, D)), ("ln1_b", (1, D)), ("ln2_g", (1, D)), ("ln2_b", (1, D)))
ROW_META, ROW_B_IN, ROW_TAIL, ROW_WG2 = 0, 22, 25, 32
ROW_LN = dict(ln_in_g=16, ln_in_b=17, ln1_g=18, ln1_b=19, ln2_g=20, ln2_b=21)
TAIL_BG2, TAIL_SINKS, TAIL_GN, TAIL_LOSS = 0, 256, 256 + SWA_HEADS, 256 + SWA_HEADS + DV


def _adamw_small(place, packs, own, params):
    n = len(SMALL)

    def body(place_ref, packs_ref, own_ref, *refs):
        ins, outs, p_ref = refs[:3 * n], refs[3 * n:-1], refs[-1]
        me, c = place_ref[0], place_ref[1]
        total = jnp.where(me == 0, own_ref[...], packs_ref[0])
        for i in range(1, N_DEVICES):
            total = total + jnp.where(me == i, own_ref[...], packs_ref[i])
        p_ref[...] = total
        outs[4 * n][...] = total[ROW_TAIL:ROW_TAIL + 1, TAIL_LOSS:TAIL_LOSS + 1]

        def mine(width, rows):
            part = lambda s: p_ref[rows, s * width:(s + 1) * width]
            return jnp.where(c == 0, part(0), jnp.where(c == 1, part(1), jnp.where(c == 2, part(2), part(3))))

        tail = lambda lo, width: p_ref[ROW_TAIL:ROW_TAIL + 1, lo:lo + width]
        grads = dict(
            meta_tokens=mine(D // N_CHIPS, slice(ROW_META, ROW_META + N_META)),
            b_in=jnp.concatenate([p_ref[ROW_B_IN:ROW_B_IN + 1, :], p_ref[ROW_B_IN + 1:ROW_B_IN + 2, :],
                                  p_ref[ROW_B_IN + 2:ROW_B_IN + 3, 0:D_IN - 2 * D]], axis=1),
            w_gate_lr2=mine(256 // N_CHIPS, slice(ROW_WG2, ROW_WG2 + 16)),
            b_gate_lr2=tail(TAIL_BG2, 256), attn_sinks=tail(TAIL_SINKS, SWA_HEADS), gla_norm_g=tail(TAIL_GN, DV),
            **{k: p_ref[r:r + 1, :] for k, r in ROW_LN.items()})
        for i, (name, _) in enumerate(SMALL):
            g = grads[name]
            outs[4 * i][...] = g
            outs[4 * i + 1][...], outs[4 * i + 2][...], outs[4 * i + 3][...] = _adamw_math(
                ins[3 * i][...], g, ins[3 * i + 1][...], ins[3 * i + 2][...])

    whole = lambda shape: pl.BlockSpec(shape, lambda i, c: (0,) * len(shape))
    outs = pl.pallas_call(
        body, name="adamw_small",
        grid_spec=pltpu.PrefetchScalarGridSpec(
            num_scalar_prefetch=1, grid=(1,),
            in_specs=[whole(packs.shape), whole(own.shape)] + [whole(s) for _, s in SMALL for _ in range(3)],
            out_specs=[whole(s) for _, s in SMALL for _ in range(4)] + [whole((1, 1))],
            scratch_shapes=[pltpu.VMEM(own.shape, F32)]),
        out_shape=[pltpu.HBM(s, F32) for _, s in SMALL for _ in range(4)] + [pltpu.HBM((1, 1), F32)],
        compiler_params=_params(16, dimension_semantics=_seq()),
    )(place, *_hbm(packs, own, *[a for p in params for a in p]))
    return [outs[4 * i:4 * i + 4] for i in range(n)], outs[4 * n]


def _small_pack(gr):
    names = ["meta_blk"] + list(ROW_LN) + ["b_in_p", "wg2_p", "bg2", "sinks", "gn", "loss"]
    gate_w = GLA_HEADS * DK

    def body(*refs):
        src, out = dict(zip(names, refs)), refs[-1]
        out[...] = jnp.zeros_like(out)
        out[ROW_META:ROW_META + N_META, :] = src["meta_blk"][META_OFF:CH, :]
        for k, r in ROW_LN.items():
            out[r:r + 1, :] = src[k][...]
        for j in range(-(-D_IN // D)):
            width = min(D, D_IN - j * D)
            out[ROW_B_IN + j:ROW_B_IN + j + 1, 0:width] = src["b_in_p"][:, j * D:j * D + width]
        tail = slice(ROW_TAIL, ROW_TAIL + 1)
        out[tail, TAIL_BG2:TAIL_BG2 + gate_w] = src["bg2"][...]
        out[tail, TAIL_SINKS:TAIL_SINKS + SWA_HEADS] = src["sinks"][:, 0:SWA_HEADS]
        out[tail, TAIL_GN:TAIL_GN + DV] = src["gn"][...]
        out[tail, TAIL_LOSS:TAIL_LOSS + 1] = src["loss"][:, 0:1]
        out[ROW_WG2:ROW_WG2 + GATE_RANK, 0:gate_w] = src["wg2_p"][0:GATE_RANK, :]

    arrays = [gr[k] for k in names]
    return pl.pallas_call(
        body, name="small_pack", grid=(1,),
        in_specs=[_acc(a.shape) for a in arrays], out_specs=_acc((SMALL_ROWS, D)),
        out_shape=pltpu.HBM((SMALL_ROWS, D), F32),
        compiler_params=_params(16, dimension_semantics=_seq()),
    )(*_hbm(*arrays))


BIG = ("w_in", "w_out", "w_g", "w_u", "w_d")


def kernel(x, meta_tokens, ln_in_g, ln_in_b, w_in, b_in, w_gate_lr2, b_gate_lr2, attn_sinks, gla_norm_g, w_out, ln1_g, ln1_b, w_ffn_gate, w_ffn_up, w_ffn_down, ln2_g, ln2_b, loss_target, m_meta_tokens, m_ln_in_g, m_ln_in_b, m_w_in, m_b_in, m_w_gate_lr2, m_b_gate_lr2, m_attn_sinks, m_gla_norm_g, m_w_out, m_ln1_g, m_ln1_b, m_w_ffn_gate, m_w_ffn_up, m_w_ffn_down, m_ln2_g, m_ln2_b, v_meta_tokens, v_ln_in_g, v_ln_in_b, v_w_in, v_b_in, v_w_gate_lr2, v_b_gate_lr2, v_attn_sinks, v_gla_norm_g, v_w_out, v_ln1_g, v_ln1_b, v_w_ffn_gate, v_w_ffn_up, v_w_ffn_down, v_ln2_g, v_ln2_b):
    chip = 2 * lax.axis_index("x") + lax.axis_index("y")

    halves = lambda a: a.reshape(2, a.shape[0] // 2, a.shape[1])
    r_in = SHARD_ROWS["w_in"]
    first = [halves(a) for a in (jnp.pad(w_in[0].T.astype(BF16), ((0, W_IN_WIN - r_in), (0, 0))), meta_tokens,
                                 w_gate_lr2[0])]
    rest = [halves(a) for a in (w_out[0].astype(BF16), w_ffn_gate[0].T.astype(BF16), w_ffn_up[0].T.astype(BF16),
                                w_ffn_down[0].astype(BF16))]
    lands = lambda arrs: [(N_CHIPS,) + a.shape for a in arrs]
    first_handle, first_token = _ici_start("gather", first, lands(first), [], "gather_first_start")
    rest_handle, token = _ici_start("gather", rest, lands(rest), [first_token], "gather_rest_start")
    own_slab = lambda got, shards: [lax.dynamic_update_index_in_dim(g, s, chip, axis=0) for g, s in zip(got, shards)]
    fetching = {}

    def fetch_first(after):
        shards, (forwarding,), _ = _gather_wait_forward(first_handle, [len(first)], after, "gather_first_wait")
        g_in, g_meta, g_wg2 = own_slab(_forward_wait(forwarding, [], "gather_first_forward_wait"), shards)
        w_in_windows = g_in.reshape(N_CHIPS, W_IN_WIN, D)
        meta_full = jnp.concatenate([g_meta[s].reshape(N_META, -1) for s in range(N_CHIPS)], axis=1)
        wg2_full = jnp.concatenate([g_wg2[s].reshape(w_gate_lr2.shape[1], -1) for s in range(N_CHIPS)], axis=1)
        return w_in_windows, meta_full, wg2_full

    def fetch_rest(after):
        shards, (w_out_forwarding, fetching["handle"]), forward_token = _gather_wait_forward(
            rest_handle, [1, len(rest) - 1], after, "gather_rest_wait")
        g_out, = own_slab(_forward_wait(w_out_forwarding, [], "gather_w_out_forward_wait"), shards[:1])
        fetching["shards"] = shards[1:]
        return g_out.reshape(-1, D), forward_token

    def fetch_ffn(after):
        got = _forward_wait(fetching["handle"], after, "gather_ffn_forward_wait")
        return [g.reshape(-1, D) for g in own_slab(got, fetching["shards"])]

    sent = {}
    split = lambda grads: [g.reshape(N_CHIPS, 2, -1, D) for g in grads]

    def exchange(key, grads):
        grads = split(grads)
        sent[key + "_halves"], exchange_token = _ici_start(
            "sibling", grads, [(N_CHIPS,) + a.shape[2:] for a in grads], [], "sibling_" + key + "_start")
        return exchange_token

    def ship(key, after):
        grads, fetched = _ici_wait("sibling", sent[key + "_halves"], after, "sibling_" + key + "_wait")
        parts = _chip_partials(grads, fetched, [BF16] * len(grads), "add_halves_" + key)
        sent[key], ship_token = _ici_start("scatter", parts, [p.shape for p in parts], [], "scatter_" + key + "_start")
        return ship_token

    dx, gr = _local_step(
        x[0], loss_target[0], ln_in_g, ln_in_b, b_in[0], b_gate_lr2[0], attn_sinks[0], gla_norm_g[0], ln1_g[0],
        ln1_b[0], ln2_g[0], ln2_b[0], token, fetch_first, fetch_rest, fetch_ffn,
        lambda g: exchange("ffn", [g[k] for k in BIG[1:]]), lambda after: ship("ffn", after),
        lambda g: exchange("w_in", [g]), lambda after: ship("w_in", after))
    ffn_parts, ffn_got = _ici_wait("scatter", sent["ffn"], [dx], "scatter_ffn_wait")
    join_handle, token = _join_start(_chip_sums(ffn_parts, ffn_got, [], "sum_chips_ffn"), "join_ffn_start")
    w_in_parts, w_in_got = _ici_wait("scatter", sent["w_in"], [token], "scatter_w_in_wait")

    small_handle, token = _small_start(_small_pack(gr), [w_in_got[0]])
    w_in_joined = _join_halves(_chip_sums(w_in_parts, w_in_got, [token], "sum_chips_w_in"), "join_w_in")
    red = [f.reshape(2 * f.shape[1], D) for f in w_in_joined + _join_wait(join_handle, w_in_joined, "join_ffn_wait")]

    big_g = dict(zip(BIG, red))
    weights = dict(meta_tokens=meta_tokens, ln_in_g=ln_in_g, ln_in_b=ln_in_b, w_in=w_in, b_in=b_in,
                   w_gate_lr2=w_gate_lr2, b_gate_lr2=b_gate_lr2, attn_sinks=attn_sinks, gla_norm_g=gla_norm_g,
                   w_out=w_out, ln1_g=ln1_g, ln1_b=ln1_b, w_ffn_gate=w_ffn_gate, w_ffn_up=w_ffn_up,
                   w_ffn_down=w_ffn_down, ln2_g=ln2_g, ln2_b=ln2_b)
    m_in = dict(meta_tokens=m_meta_tokens, ln_in_g=m_ln_in_g, ln_in_b=m_ln_in_b, w_in=m_w_in, b_in=m_b_in,
                w_gate_lr2=m_w_gate_lr2, b_gate_lr2=m_b_gate_lr2, attn_sinks=m_attn_sinks, gla_norm_g=m_gla_norm_g,
                w_out=m_w_out, ln1_g=m_ln1_g, ln1_b=m_ln1_b, w_ffn_gate=m_w_ffn_gate, w_ffn_up=m_w_ffn_up,
                w_ffn_down=m_w_ffn_down, ln2_g=m_ln2_g, ln2_b=m_ln2_b)
    v_in = dict(meta_tokens=v_meta_tokens, ln_in_g=v_ln_in_g, ln_in_b=v_ln_in_b, w_in=v_w_in, b_in=v_b_in,
                w_gate_lr2=v_w_gate_lr2, b_gate_lr2=v_b_gate_lr2, attn_sinks=v_attn_sinks, gla_norm_g=v_gla_norm_g,
                w_out=v_w_out, ln1_g=v_ln1_g, ln1_b=v_ln1_b, w_ffn_gate=v_w_ffn_gate, w_ffn_up=v_w_ffn_up,
                w_ffn_down=v_w_ffn_down, ln2_g=v_ln2_g, ln2_b=v_ln2_b)
    names = list(weights)
    big_names = ("w_in", "w_out", "w_ffn_gate", "w_ffn_up", "w_ffn_down")

    grads, delta, new_m, new_v = {}, {}, {}, {}
    flips = [(lambda a: a.T) if kk in ("w_g", "w_u") else (lambda a: a) for kk in BIG[1:]]
    by_row = lambda a: jnp.transpose(a, (2, 0, 1))
    updated, updated_w_in = _adamw(
        [(flip(weights[k][0]), big_g[kk], flip(m_in[k][0]), flip(v_in[k][0]))
         for k, kk, flip in zip(big_names[1:], BIG[1:], flips)],
        (by_row(w_in), big_g["w_in"], by_row(m_w_in), by_row(v_w_in)), chip.astype(jnp.int32).reshape(1), r_in % BF16_ROWS)
    for k, flip, results in zip(big_names[1:], flips, updated):
        grads[k], delta[k], new_m[k], new_v[k] = (flip(t)[None] for t in results)
    grads["w_in"], delta["w_in"], new_m["w_in"], new_v["w_in"] = (jnp.transpose(t, (1, 2, 0)) for t in updated_w_in)
    small_in = [tuple(src[k].reshape(shape) for src in (weights, m_in, v_in)) for k, shape in SMALL]
    place = jnp.stack([2 * chip + lax.axis_index("c"), chip]).astype(jnp.int32)
    small_own, small_all = _small_wait(small_handle, [updated[0][0]])
    small_out, loss = _adamw_small(place, small_all, small_own, small_in)
    for (k, _), results in zip(SMALL, small_out):
        grads[k], delta[k], new_m[k], new_v[k] = (r.reshape(weights[k].shape) for r in results)

    return (loss.reshape(()), dx[None], *[grads[k] for k in names], *[delta[k] for k in names], *[new_m[k] for k in names],
            *[new_v[k] for k in names])
```

```python
import jax
import jax.numpy as jnp
from jax import lax
from jax.experimental import pallas as pl
from jax.experimental.pallas import tpu as pltpu

F32 = jnp.float32
BF16 = jnp.bfloat16
MESH = pl.DeviceIdType.MESH

D = 1024
SEQ = 4096
N_META = 16
SWA_HEADS, SWA_KV_HEADS, DH = 8, 2, 64
WINDOW = 128
GLA_HEADS, DK, DV = 4, 64, 128
GLA_TAU = 16.0
CH = 64
D_FF = 2816
D_IN = 2320
LN_EPS = 1e-5
RMS_EPS = 1e-6
ALPHA = 2.0 ** 0.25
NEG = -1e30
ADAM_LR, ADAM_B1, ADAM_B2, ADAM_EPS, ADAM_WD, ADAM_STEP = 0.001, 0.9, 0.999, 1e-8, 0.01, 10
O_QS, O_KS, O_VS, O_QG, O_KG, O_VG, O_RG, O_LR = 0, 512, 640, 768, 1024, 1280, 1792, 2304

LANE = 128
BLK = WINDOW
GATE_RANK = 16
D_IN_P = D_IN + LANE - GATE_RANK
META_OFF = CH - N_META
HEAD_POS = (0, 4, 1, 5, 2, 6, 3, 7)
LN_ROWS = 512
TOKEN = (8, LANE)
N_CHIPS = 4
SHARD_ROWS = dict(w_in=D_IN // N_CHIPS, w_out=D // N_CHIPS, w_g=D_FF // N_CHIPS, w_u=D_FF // N_CHIPS,
                  w_d=D_FF // N_CHIPS)
SMALL_ROWS = 48
BF16_ROWS = 16
W_IN_WIN = -(-SHARD_ROWS["w_in"] // (2 * BF16_ROWS)) * 2 * BF16_ROWS
W_IN_STARTS = tuple(s * SHARD_ROWS["w_in"] // BF16_ROWS * BF16_ROWS for s in range(N_CHIPS))
VMEM_CAP_MB = 64
VMEM_SPARE_MB = 6


def _lp():
    return SEQ + BLK


def _row_tile(cap):
    lp = _lp()
    return max(t for t in range(16, cap + 1, 16) if lp % t == 0)


def _params(vmem_mb, **kw):
    assert vmem_mb <= VMEM_CAP_MB - VMEM_SPARE_MB
    return pltpu.CompilerParams(vmem_limit_bytes=vmem_mb << 20, **kw)


def _seq(n=1):
    return ("arbitrary",) * n


def _const(shape):
    return pl.BlockSpec(shape, lambda *_: (0,) * len(shape), pipeline_mode=pl.Buffered(1))


def _acc(shape):
    return pl.BlockSpec(shape, lambda *_: (0,) * len(shape))


def _rows(tm, width):
    return pl.BlockSpec((tm, width), lambda i: (i, 0))


def _dot(a, b):
    return jnp.dot(a.astype(BF16), b.astype(BF16), preferred_element_type=F32)


def _dot_nt(a, b):
    return lax.dot_general(a.astype(BF16), b.astype(BF16), (((1,), (1,)), ((), ())), preferred_element_type=F32)


def _dot_tn(a, b):
    return lax.dot_general(a.astype(BF16), b.astype(BF16), (((0,), (0,)), ((), ())), preferred_element_type=F32)


def _dot_exact(a, b):
    return jnp.dot(a, b, precision=lax.Precision.HIGHEST, preferred_element_type=F32)


def _ln_stats(x):
    mu = jnp.mean(x, axis=-1, keepdims=True)
    xc = x - mu
    rstd = lax.rsqrt(jnp.mean(xc * xc, axis=-1, keepdims=True) + LN_EPS)
    return xc * rstd, rstd


def _ln_bwd(dy, xhat, rstd, g):
    dxh = dy * g
    return rstd * (dxh - jnp.mean(dxh, axis=-1, keepdims=True) - xhat * jnp.mean(dxh * xhat, axis=-1, keepdims=True))


def _sigmoid(x):
    return 1.0 / (1.0 + jnp.exp(-x))


def _iota(shape, dim):
    return lax.broadcasted_iota(jnp.int32, shape, dim)


def _hbm(*arrays):
    return tuple(pltpu.with_memory_space_constraint(a, pltpu.HBM) for a in arrays)


def _ln_in_fwd_real(x, g, b, token):
    tr = min(LN_ROWS, SEQ)

    def body(x_ref, g_ref, b_ref, token_ref, h_ref):
        xhat, _ = _ln_stats(x_ref[...])
        h_ref[...] = xhat * g_ref[...] + b_ref[...]

    return pl.pallas_call(
        body, name="ln_in_fwd", grid=(SEQ // tr,),
        in_specs=[_rows(tr, D), _const((1, D)), _const((1, D)), _const(TOKEN)],
        out_specs=_rows(tr, D),
        out_shape=pltpu.HBM((_lp(), D), F32),
        compiler_params=_params(32, dimension_semantics=_seq()),
    )(*_hbm(x, g, b), token)


def _ln_in_fwd_meta(h_real, meta_ext, g, b):
    def meta_body(m_ref, g_ref, b_ref, real_ref, h_ref):
        xhat, _ = _ln_stats(m_ref[...])
        h_ref[...] = xhat * g_ref[...] + b_ref[...]

    return pl.pallas_call(
        meta_body, name="ln_in_fwd_meta", grid=(1,),
        in_specs=[_const((BLK, D)), _const((1, D)), _const((1, D)), pl.BlockSpec(memory_space=pl.ANY)],
        out_specs=pl.BlockSpec((BLK, D), lambda i: (SEQ // BLK, 0)),
        out_shape=pltpu.HBM((_lp(), D), F32),
        input_output_aliases={3: 0},
        compiler_params=_params(16, dimension_semantics=_seq()),
    )(*_hbm(meta_ext, g, b, h_real))


def _in_proj(h0, w_in_windows, b_in_p, wg2_p, bg2):
    tm = _row_tile(384)
    lp = _lp()
    widths = (512, 128, 128, 256, 256, 512, 512, 128)
    offs = (O_QS, O_KS, O_VS, O_QG, O_KG, O_VG, O_RG, O_LR)
    shard = SHARD_ROWS["w_in"]

    def body(h_ref, win_ref, b_ref, wg2_ref, bg2_ref, *outs):
        w_ref = outs[9]

        @pl.when(pl.program_id(0) == 0)
        def _():
            for s in range(N_CHIPS):
                w_ref[shard * s:shard * (s + 1), :] = win_ref[s, 0:shard, :]
            w_ref[D_IN:D_IN_P, :] = jnp.zeros((D_IN_P - D_IN, D), BF16)

        proj = _dot_nt(h_ref[...], w_ref[...]) + b_ref[...]
        for pos, h in enumerate(HEAD_POS):
            outs[0][:, pos * DH:(pos + 1) * DH] = proj[:, O_QS + h * DH:O_QS + (h + 1) * DH]
        for o_ref, off, wd in zip(outs[1:8], offs[1:], widths[1:]):
            o_ref[...] = proj[:, off:off + wd]
        outs[8][...] = _dot(proj[:, O_LR:O_LR + LANE], wg2_ref[...]) + bg2_ref[...]

    return pl.pallas_call(
        body, name="in_proj", grid=(lp // tm,),
        in_specs=[_rows(tm, D), _const(w_in_windows.shape), _const((1, D_IN_P)), _const((LANE, 256)), _const((1, 256))],
        out_specs=[_rows(tm, w) for w in widths] + [_rows(tm, 256), _acc((D_IN_P, D))],
        out_shape=[pltpu.HBM((lp, w), F32) for w in widths] + [pltpu.HBM((lp, 256), F32), pltpu.HBM((D_IN_P, D), BF16)],
        compiler_params=_params(48, dimension_semantics=_seq()),
    )(*_hbm(h0, w_in_windows, b_in_p, wg2_p, bg2))


def _swa_masks(n):
    nb = SEQ // BLK
    is_meta = n == nb
    ri = _iota((BLK, BLK), 0)
    cj = _iota((BLK, BLK), 1)
    meta_col = ((cj >= META_OFF) & (cj < CH)).astype(jnp.int32)
    meta_q = meta_col * ((cj <= ri) & (ri < CH)).astype(jnp.int32)
    valid_m = jnp.where(is_meta, meta_q, meta_col) > 0
    dist_m = jnp.where(is_meta, ri - cj, n * BLK + ri + CH - cj).astype(F32)
    valid_p = jnp.where((n >= 1) & (n < nb), (cj > ri).astype(jnp.int32), 0) > 0
    dist_p = (ri + BLK - cj).astype(F32)
    valid_c = jnp.where(n < nb, (cj <= ri).astype(jnp.int32), 0) > 0
    dist_c = (ri - cj).astype(F32)
    return (dist_m, dist_p, dist_c), (valid_m, valid_p, valid_c)


def _swa_bias(n):
    dists, valids = _swa_masks(n)
    return (jnp.concatenate([-d for d in dists], axis=1),
            jnp.concatenate([jnp.where(v, 0.0, NEG) for v in valids], axis=1))


def _swa_half(ref, pos, scale=1.0):
    col = ref[:, (pos // 2) * LANE:(pos // 2 + 1) * LANE]
    lane = _iota((BLK, LANE), 1)
    mine = lane < DH if pos % 2 == 0 else lane >= DH
    return jnp.where(mine, col * scale, 0.0).astype(BF16)


def _swa_merge(even, odd):
    return jnp.where(_iota((BLK, LANE), 1) < DH, even, odd)


def _swa_softmax(t, sink):
    m = jnp.maximum(jnp.max(t, axis=-1, keepdims=True), sink)
    e = jnp.exp(t - m)
    e_sink = jnp.exp(sink - m)
    inv = 1.0 / (jnp.sum(e, axis=-1, keepdims=True) + e_sink)
    return e * inv, e_sink * inv


def _swa_kv_specs(width):
    nb = SEQ // BLK
    return [pl.BlockSpec((BLK, width), lambda n: (nb, 0)),
            pl.BlockSpec((BLK, width), lambda n: (jnp.clip(n - 1, 0, nb - 1), 0)),
            pl.BlockSpec((BLK, width), lambda n: (jnp.minimum(n, nb), 0))]


def _swa_fwd(sinks, qs, ks, vs):
    nb = SEQ // BLK
    heads = range(SWA_HEADS)

    def body(sink_ref, q_ref, km_ref, kp_ref, kc_ref, vm_ref, vp_ref, vc_ref, o_ref):
        negdist, maskbias = _swa_bias(pl.program_id(0))
        k_all = jnp.concatenate([km_ref[...], kp_ref[...], kc_ref[...]], axis=0).astype(BF16)
        v_all = jnp.concatenate([vm_ref[...], vp_ref[...], vc_ref[...]], axis=0).astype(BF16)
        q = [_swa_half(q_ref, pos, DH ** -0.5) for pos in heads]
        t = [_dot_nt(q[pos], k_all) + (2.0 ** -(HEAD_POS[pos] + 1) * negdist + maskbias) for pos in heads]
        p = [_swa_softmax(t[pos], sink_ref[HEAD_POS[pos]])[0].astype(BF16) for pos in heads]
        o = [_dot(p[pos], v_all) for pos in heads]
        for col in range(SWA_HEADS // 2):
            o_ref[:, col * LANE:(col + 1) * LANE] = _swa_merge(o[2 * col], o[2 * col + 1])

    kvw = SWA_KV_HEADS * DH
    return pl.pallas_call(
        body, name="swa_fwd", grid=(nb + 1,),
        in_specs=[pl.BlockSpec(memory_space=pltpu.SMEM), _rows(BLK, SWA_HEADS * DH)] + _swa_kv_specs(kvw) + _swa_kv_specs(kvw),
        out_specs=_rows(BLK, SWA_HEADS * DH),
        out_shape=pltpu.HBM((_lp(), SWA_HEADS * DH), F32),
        compiler_params=_params(16, dimension_semantics=_seq()),
    )(sinks, *_hbm(qs, ks, ks, ks, vs, vs, vs))


GLA_PER_STEP = BLK // CH


def _gla_block(s):
    nb = SEQ // BLK
    return jnp.where(s == 0, nb, s - 1)


def _gla_rowmask(s):
    ri = _iota((BLK, 1), 0)
    m = jnp.where(s == 0, ((ri >= META_OFF) & (ri < CH)).astype(jnp.int32), 1)
    return (m > 0).astype(F32) + jnp.zeros((BLK, 1), F32)


def _gla_chunk_masks():
    r, c = _iota((BLK, BLK), 0), _iota((BLK, BLK), 1)
    same = ((r < CH) & (c < CH)) | ((r >= CH) & (c >= CH))
    return same & (r >= c), same & (r <= c), same


def _gla_decay(z, rmask):
    log_g = (jnp.minimum(z, 0.0) - jnp.log1p(jnp.exp(-jnp.abs(z)))) * (rmask / GLA_TAU)
    lower, _, same = _gla_chunk_masks()
    return _dot_exact(lower.astype(F32), log_g), _dot_exact(same.astype(F32), log_g)


def _gla_slices(c, h):
    return slice(c * CH, (c + 1) * CH), slice(h * DK, (h + 1) * DK), slice(h * DV, (h + 1) * DV)


def _gla_fwd(qg, kg, vg, z):
    steps = SEQ // BLK + 1
    kw, vw = GLA_HEADS * DK, GLA_HEADS * DV
    pairs = [(c, h) for c in range(GLA_PER_STEP) for h in range(GLA_HEADS)]

    def body(q_ref, k_ref, v_ref, z_ref, o_ref, st_ref, st):
        s = pl.program_id(0)

        @pl.when(s == 0)
        def _():
            st[...] = jnp.zeros_like(st)

        rmask = _gla_rowmask(s)
        b, b_last = _gla_decay(z_ref[...], rmask)
        q = q_ref[...] * (rmask * DK ** -0.5)
        k = k_ref[...] * rmask
        v = v_ref[...] * rmask
        qe = q * jnp.exp(b)
        ke = k * jnp.exp(-b)
        kd = k * jnp.exp(b_last - b)
        e_last = jnp.exp(b_last)
        causal = _iota((CH, CH), 0) >= _iota((CH, CH), 1)
        a, upd, intra = {}, {}, {}
        for c, h in pairs:
            rows, ks, vs_ = _gla_slices(c, h)
            a[c, h] = jnp.where(causal, _dot_nt(qe[rows, ks], ke[rows, ks]), 0.0)
            upd[c, h] = _dot_tn(v[rows, vs_], kd[rows, ks])
        for c, h in pairs:
            rows, ks, vs_ = _gla_slices(c, h)
            intra[c, h] = _dot(a[c, h], v[rows, vs_])
        state = st[...]
        for c in range(GLA_PER_STEP):
            st_ref[0, c] = state
            for h in range(GLA_HEADS):
                rows, ks, vs_ = _gla_slices(c, h)
                o_ref[rows, vs_] = intra[c, h] + _dot_nt(qe[rows, ks], state[:, ks])
            state = state * e_last[c * CH:c * CH + 1] + jnp.concatenate([upd[c, h] for h in range(GLA_HEADS)], axis=1)
        st[...] = state

    blk = lambda w: pl.BlockSpec((BLK, w), lambda s: (_gla_block(s), 0))
    return pl.pallas_call(
        body, name="gla_fwd", grid=(steps,),
        in_specs=[blk(kw), blk(kw), blk(vw), blk(kw)],
        out_specs=[blk(vw), pl.BlockSpec((1, GLA_PER_STEP, DV, kw), lambda s: (s, 0, 0, 0))],
        out_shape=[pltpu.HBM((_lp(), vw), F32), pltpu.HBM((steps, GLA_PER_STEP, DV, kw), F32)],
        scratch_shapes=[pltpu.VMEM((DV, kw), F32)],
        compiler_params=_params(16, dimension_semantics=_seq()),
    )(*_hbm(qg, kg, vg, z))


def _post_mix(o_s, o_gla, r_g, h0, gn4, w_out, g1, b1, token):
    tm = _row_tile(384)
    lp = _lp()

    def body(os_ref, og_ref, r_ref, h0_ref, gn_ref, w_ref, g_ref, b_ref, token_ref, o_ref, pre_ref, h1_ref):
        for pos, h in enumerate(HEAD_POS):
            o_ref[:, h * DH:(h + 1) * DH] = os_ref[:, pos * DH:(pos + 1) * DH].astype(BF16)
        for h in range(GLA_HEADS):
            hs = slice(h * DV, (h + 1) * DV)
            xg = og_ref[:, hs]
            n = xg * lax.rsqrt(jnp.mean(xg * xg, axis=-1, keepdims=True) + RMS_EPS) * gn_ref[...]
            r = r_ref[:, hs]
            o_ref[:, 512 + h * DV:512 + (h + 1) * DV] = (n * (r * _sigmoid(r))).astype(BF16)
        pre = ALPHA * h0_ref[...] + _dot(o_ref[...], w_ref[...])
        pre_ref[...] = pre
        xhat, _ = _ln_stats(pre)
        h1_ref[...] = xhat * g_ref[...] + b_ref[...]

    return pl.pallas_call(
        body, name="post_mix", grid=(lp // tm,),
        in_specs=[_rows(tm, 512), _rows(tm, 512), _rows(tm, 512), _rows(tm, D), _const((1, DV)), _const((D, D)),
                  _const((1, D)), _const((1, D)), _const(TOKEN)],
        out_specs=[_rows(tm, D), _rows(tm, D), _rows(tm, D)],
        out_shape=[pltpu.HBM((lp, D), BF16), pltpu.HBM((lp, D), F32),
                   pltpu.HBM((lp, D), F32)],
        compiler_params=_params(32, dimension_semantics=_seq()),
    )(*_hbm(o_s, o_gla, r_g, h0, gn4, w_out, g1, b1), token)


def _ffn_fwd_loss_bwd(h1, wg_t, wu_t, wd, target, g2, b2):
    lp = _lp()
    tm = max(t for t in range(BLK, 384 + 1, BLK) if lp % t == 0)
    steps = lp // tm
    last_blk = SEQ // BLK - 1
    half = D_FF // 2
    n_t = tm // BLK

    def body(*refs):
        h_ref, wg_ref, wu_ref, wd_ref = refs[:4]
        t_refs = refs[4:4 + n_t]
        g2_ref, b2_ref, a_ref, dgate_ref, dup_ref, dp_ref, loss_ref, dg_ref, db_ref, g_s, u_s, acc = refs[4 + n_t:]
        i = pl.program_id(0)

        @pl.when(i == 0)
        def _():
            acc[...] = jnp.zeros_like(acc)
            dg_ref[...] = jnp.zeros_like(dg_ref)
            db_ref[...] = jnp.zeros_like(db_ref)

        h = h_ref[...]
        hb = h.astype(BF16)
        pre = ALPHA * h
        for j in range(2):
            cols = slice(j * half, (j + 1) * half)
            g = _dot_nt(hb, wg_ref[cols, :])
            u = _dot_nt(hb, wu_ref[cols, :])
            g_s[:, cols] = g
            u_s[:, cols] = u
            pre = pre + _dot(g * _sigmoid(g) * u, wd_ref[cols, :])
        xhat, rstd = _ln_stats(pre)
        real = i * tm + _iota((tm, 1), 0) < SEQ
        target_rows = jnp.concatenate([t[...] for t in t_refs], axis=0)
        diff = jnp.where(real, xhat * g2_ref[...] + b2_ref[...] - target_rows, 0.0)
        acc[...] += jnp.sum(diff * diff, axis=0, keepdims=True)
        dy = diff * (1.0 / D)
        dpre = _ln_bwd(dy, xhat, rstd, g2_ref[...])
        dp_ref[...] = dpre
        dg_ref[...] += jnp.sum(dy * xhat, axis=0, keepdims=True)
        db_ref[...] += jnp.sum(dy, axis=0, keepdims=True)
        dpb = dpre.astype(BF16)
        for j in range(2):
            cols = slice(j * half, (j + 1) * half)
            g, u = g_s[:, cols], u_s[:, cols]
            sg = _sigmoid(g)
            silu = g * sg
            da = _dot_nt(dpb, wd_ref[cols, :])
            a_ref[:, cols] = (silu * u).astype(BF16)
            dgate_ref[:, cols] = (da * u * (sg * (1.0 + g * (1.0 - sg)))).astype(BF16)
            dup_ref[:, cols] = (da * silu).astype(BF16)

        @pl.when(i == steps - 1)
        def _():
            loss_ref[...] = jnp.zeros_like(loss_ref) + (0.5 / D) * jnp.sum(acc[...], axis=1, keepdims=True)

    t_spec = lambda k: pl.BlockSpec((BLK, D), lambda i: (jnp.minimum(i * n_t + k, last_blk), 0))
    return pl.pallas_call(
        body, name="ffn_fwd_loss_bwd", grid=(steps,),
        in_specs=[_rows(tm, D), _const((D_FF, D)), _const((D_FF, D)), _const((D_FF, D))]
        + [t_spec(k) for k in range(n_t)] + [_const((1, D)), _const((1, D))],
        out_specs=[_rows(tm, D_FF), _rows(tm, D_FF), _rows(tm, D_FF), _rows(tm, D), _acc((1, LANE)), _acc((1, D)),
                   _acc((1, D))],
        out_shape=[pltpu.HBM((lp, D_FF), BF16)] * 3 + [pltpu.HBM((lp, D), F32), pltpu.HBM((1, LANE), F32),
                                                         pltpu.HBM((1, D), F32), pltpu.HBM((1, D), F32)],
        scratch_shapes=[pltpu.VMEM((tm, D_FF), F32), pltpu.VMEM((tm, D_FF), F32), pltpu.VMEM((1, D), F32)],
        compiler_params=_params(58, dimension_semantics=_seq()),
    )(*_hbm(h1, wg_t, wu_t, wd, *[target] * n_t, g2, b2))


def _ffn_out_bwd(dpre2, dgate, dup, pre1, wg_t, wu_t, g1, w_out, o_gla, r_g, gn4):
    tm = _row_tile(384)
    lp = _lp()

    def body(dp_ref, dg_ref, du_ref, p1_ref, wg_ref, wu_ref, g1_ref, w_ref, og_ref, r_ref, gn_ref,
             dp1_ref, dg1_ref, db1_ref, dos_ref, dog_ref, dr_ref, dgn_ref):
        @pl.when(pl.program_id(0) == 0)
        def _():
            for acc_ref in (dg1_ref, db1_ref, dgn_ref):
                acc_ref[...] = jnp.zeros_like(acc_ref)

        dh1 = ALPHA * dp_ref[...] + _dot(dg_ref[...], wg_ref[...]) + _dot(du_ref[...], wu_ref[...])
        xhat, rstd1 = _ln_stats(p1_ref[...])
        dpre1 = _ln_bwd(dh1, xhat, rstd1, g1_ref[...])
        dp1_ref[...] = dpre1
        dg1_ref[...] += jnp.sum(dh1 * xhat, axis=0, keepdims=True)
        db1_ref[...] += jnp.sum(dh1, axis=0, keepdims=True)

        do = _dot_nt(dpre1, w_ref[...])
        for pos, h in enumerate(HEAD_POS):
            dos_ref[:, pos * DH:(pos + 1) * DH] = do[:, h * DH:(h + 1) * DH]
        gn = gn_ref[...]
        for h in range(GLA_HEADS):
            hs = slice(h * DV, (h + 1) * DV)
            xg = og_ref[:, hs]
            rstd = lax.rsqrt(jnp.mean(xg * xg, axis=-1, keepdims=True) + RMS_EPS)
            nx = xg * rstd
            r = r_ref[:, hs]
            sr = _sigmoid(r)
            d_o = do[:, 512 + h * DV:512 + (h + 1) * DV]
            dr_ref[:, hs] = d_o * (nx * gn) * (sr * (1.0 + r * (1.0 - sr)))
            dn = d_o * (r * sr)
            dgn_ref[...] += jnp.sum(dn * nx, axis=0, keepdims=True)
            dnx = dn * gn
            dog_ref[:, hs] = rstd * (dnx - nx * jnp.mean(dnx * nx, axis=-1, keepdims=True))

    return pl.pallas_call(
        body, name="ffn_out_bwd", grid=(lp // tm,),
        in_specs=[_rows(tm, D), _rows(tm, D_FF), _rows(tm, D_FF), _rows(tm, D), _const((D_FF, D)), _const((D_FF, D)),
                  _const((1, D)), _const((D, D)), _rows(tm, 512), _rows(tm, 512), _const((1, DV))],
        out_specs=[_rows(tm, D), _acc((1, D)), _acc((1, D)), _rows(tm, 512), _rows(tm, 512), _rows(tm, 512),
                   _acc((1, DV))],
        out_shape=[pltpu.HBM((lp, D), F32), pltpu.HBM((1, D), F32), pltpu.HBM((1, D), F32)]
        + [pltpu.HBM((lp, 512), F32)] * 3 + [pltpu.HBM((1, DV), F32)],
        compiler_params=_params(48, dimension_semantics=_seq()),
    )(*_hbm(dpre2, dgate, dup, pre1, wg_t, wu_t, g1, w_out, o_gla, r_g, gn4))


def _atb(a, b, name, token=None, windows=None):
    lp = _lp()
    tm = _row_tile(1408)
    n, w = a.shape[1], b.shape[1]
    bw = 512 if n * w * 4 > (4 << 20) else w
    tokens = [] if token is None else [token]
    steps = lp // tm

    def body(a_ref, b_ref, *rest):
        o_ref, acc_ref = rest[len(tokens):] if windows else (rest[-1], rest[-1])

        @pl.when(pl.program_id(1) == 0)
        def _():
            acc_ref[...] = jnp.zeros_like(acc_ref)

        acc_ref[...] += _dot_tn(a_ref[...], b_ref[...])

        if windows:
            @pl.when(pl.program_id(1) == steps - 1)
            def _():
                for s, start in enumerate(windows[0]):
                    o_ref[s] = acc_ref[start:start + windows[1], :]

    if windows:
        count, height = len(windows[0]), windows[1]
        out_spec, out_shape = pl.BlockSpec((count, height, bw), lambda j, k: (0, 0, j)), (count, height, w)
    else:
        out_spec, out_shape = pl.BlockSpec((n, bw), lambda j, k: (0, j)), (n, w)
    return pl.pallas_call(
        body, name=name, grid=(w // bw, steps),
        in_specs=[pl.BlockSpec((tm, n), lambda j, k: (k, 0)), pl.BlockSpec((tm, bw), lambda j, k: (k, j))]
        + [_const(TOKEN)] * len(tokens),
        out_specs=out_spec, out_shape=pltpu.HBM(out_shape, F32),
        scratch_shapes=[pltpu.VMEM((n, bw), F32)] if windows else [],
        compiler_params=_params(48, dimension_semantics=_seq(2)),
    )(*_hbm(a, b), *tokens)


def _gla_bwd(qg, kg, vg, z, do_gla, st_all, token):
    steps = SEQ // BLK + 1
    kw, vw = GLA_HEADS * DK, GLA_HEADS * DV
    pairs = [(c, h) for c in range(GLA_PER_STEP) for h in range(GLA_HEADS)]
    heads = range(GLA_HEADS)

    def body(q_ref, k_ref, v_ref, z_ref, do_ref, st_ref, token_ref, dq_ref, dk_ref, dv_ref, dz_ref, dst):
        @pl.when(pl.program_id(0) == 0)
        def _():
            dst[...] = jnp.zeros_like(dst)

        rmask = _gla_rowmask(steps - 1 - pl.program_id(0))
        zz = z_ref[...]
        b, b_last = _gla_decay(zz, rmask)
        e_b, e_nb, e_kd, e_last = jnp.exp(b), jnp.exp(-b), jnp.exp(b_last - b), jnp.exp(b_last)
        q = q_ref[...] * (rmask * DK ** -0.5)
        k = k_ref[...] * rmask
        v = v_ref[...] * rmask
        qe, ke, kd = q * e_b, k * e_nb, k * e_kd
        d_o = do_ref[...]
        causal = _iota((CH, CH), 0) >= _iota((CH, CH), 1)
        a, da, dqe, dke, dv_intra, carry = {}, {}, {}, {}, {}, {}
        for c, h in pairs:
            rows, ks, vs_ = _gla_slices(c, h)
            a[c, h] = jnp.where(causal, _dot_nt(qe[rows, ks], ke[rows, ks]), 0.0)
            da[c, h] = jnp.where(causal, _dot_nt(d_o[rows, vs_], v[rows, vs_]), 0.0)
            carry[c, h] = _dot_tn(d_o[rows, vs_], qe[rows, ks])
        for c, h in pairs:
            rows, ks, vs_ = _gla_slices(c, h)
            dqe[c, h] = _dot(d_o[rows, vs_], st_ref[0, c][:, ks]) + _dot(da[c, h], ke[rows, ks])
            dke[c, h] = _dot_tn(da[c, h], qe[rows, ks])
            dv_intra[c, h] = _dot_tn(a[c, h], d_o[rows, vs_])
        dstate = dst[...]
        dkd, db_decay = {}, {}
        for c in reversed(range(GLA_PER_STEP)):
            for h in heads:
                rows, ks, vs_ = _gla_slices(c, h)
                dkd[c, h] = _dot(v[rows, vs_], dstate[:, ks])
                dv_ref[rows, vs_] = dv_intra[c, h] + _dot_nt(kd[rows, ks], dstate[:, ks])
            chunk_last = e_last[c * CH:c * CH + 1]
            db_decay[c] = jnp.sum(dstate * st_ref[0, c], axis=0, keepdims=True) * chunk_last
            dstate = dstate * chunk_last + jnp.concatenate([carry[c, h] for h in heads], axis=1)
        dst[...] = dstate
        rows_of = lambda parts: jnp.concatenate(
            [jnp.concatenate([parts[c, h] for h in heads], axis=1) for c in range(GLA_PER_STEP)], axis=0)
        dqe_all, dke_all, dkd_all = rows_of(dqe), rows_of(dke), rows_of(dkd)
        dq_ref[...] = dqe_all * e_b * (rmask * DK ** -0.5)
        dk_ref[...] = (dke_all * e_nb + dkd_all * e_kd) * rmask
        dkd_kd = dkd_all * kd
        db = dqe_all * qe - dke_all * ke - dkd_kd
        _, upper, same = _gla_chunk_masks()
        decay_rows = jnp.concatenate([jnp.broadcast_to(db_decay[c], (CH, kw)) for c in range(GLA_PER_STEP)], axis=0)
        dlog_g = _dot_exact(upper.astype(F32), db) + _dot_exact(same.astype(F32), dkd_kd) + decay_rows
        dz_ref[...] = dlog_g * (rmask / GLA_TAU) * _sigmoid(-zz)

    blk = lambda w: pl.BlockSpec((BLK, w), lambda s: (_gla_block(steps - 1 - s), 0))
    return pl.pallas_call(
        body, name="gla_bwd", grid=(steps,),
        in_specs=[blk(kw), blk(kw), blk(vw), blk(kw), blk(vw),
                  pl.BlockSpec((1, GLA_PER_STEP, DV, kw), lambda s: (steps - 1 - s, 0, 0, 0)), _const(TOKEN)],
        out_specs=[blk(kw), blk(kw), blk(vw), blk(kw)],
        out_shape=[pltpu.HBM((_lp(), kw), F32), pltpu.HBM((_lp(), kw), F32),
                   pltpu.HBM((_lp(), vw), F32), pltpu.HBM((_lp(), kw), F32)],
        scratch_shapes=[pltpu.VMEM((DV, kw), F32)],
        compiler_params=_params(16, dimension_semantics=_seq()),
    )(*_hbm(qg, kg, vg, z, do_gla, st_all), token)


def _swa_bwd(sinks, qs, ks, vs, do_s, token):
    nb = SEQ // BLK
    kvw = SWA_KV_HEADS * DH
    scale = DH ** -0.5
    heads = range(SWA_HEADS)

    def body(sink_ref, q_ref, km_ref, kp_ref, kc_ref, vm_ref, vp_ref, vc_ref, do_ref, token_ref,
             dq_ref, dk_ref, dv_ref, dsink_ref, carry_k, carry_v, meta_k, meta_v):
        n = pl.program_id(0)

        @pl.when(n == 0)
        def _():
            for r in (carry_k, carry_v, meta_k, meta_v):
                r[...] = jnp.zeros_like(r)
            dsink_ref[...] = jnp.zeros_like(dsink_ref)

        @pl.when(n <= nb)
        def _():
            negdist, maskbias = _swa_bias(n)
            lane = _iota((1, LANE), 1)
            k_all = jnp.concatenate([km_ref[...], kp_ref[...], kc_ref[...]], axis=0).astype(BF16)
            v_all = jnp.concatenate([vm_ref[...], vp_ref[...], vc_ref[...]], axis=0).astype(BF16)
            q = [_swa_half(q_ref, pos, scale) for pos in heads]
            d_o = [_swa_half(do_ref, pos) for pos in heads]
            t = [_dot_nt(q[pos], k_all) + (2.0 ** -(HEAD_POS[pos] + 1) * negdist + maskbias) for pos in heads]
            dp = [_dot_nt(d_o[pos], v_all) for pos in heads]
            soft = [_swa_softmax(t[pos], sink_ref[HEAD_POS[pos]]) for pos in heads]
            p = [s[0] for s in soft]
            delta = [jnp.sum(p[pos] * dp[pos], axis=-1, keepdims=True) for pos in heads]
            ds = [(p[pos] * (dp[pos] - delta[pos])).astype(BF16) for pos in heads]
            dq = [_dot(ds[pos], k_all) for pos in heads]
            for col in range(SWA_HEADS // 2):
                dq_ref[:, col * LANE:(col + 1) * LANE] = scale * _swa_merge(dq[2 * col], dq[2 * col + 1])
            dsink = jnp.zeros((1, LANE), F32)
            for pos in heads:
                dsink = dsink + jnp.where(lane == HEAD_POS[pos],
                                          -jnp.sum(soft[pos][1] * delta[pos], axis=0, keepdims=True), 0.0)
            dsink_ref[...] += dsink
            dk3 = _dot_tn(jnp.concatenate(q, axis=0), jnp.concatenate(ds, axis=0)).T
            dv3 = _dot_tn(jnp.concatenate(d_o, axis=0), jnp.concatenate([x.astype(BF16) for x in p], axis=0)).T
            meta_k[...] += dk3[0:BLK]
            meta_v[...] += dv3[0:BLK]
            dk_ref[...] = carry_k[...] + dk3[BLK:2 * BLK]
            dv_ref[...] = carry_v[...] + dv3[BLK:2 * BLK]
            carry_k[...] = dk3[2 * BLK:3 * BLK]
            carry_v[...] = dv3[2 * BLK:3 * BLK]

        @pl.when(n == nb + 1)
        def _():
            dk_ref[...] = meta_k[...]
            dv_ref[...] = meta_v[...]

    kv_out = pl.BlockSpec((BLK, kvw), lambda n: (jnp.where(n == nb + 1, nb, jnp.clip(n - 1, 0, nb - 1)), 0))
    qblk = pl.BlockSpec((BLK, SWA_HEADS * DH), lambda n: (jnp.minimum(n, nb), 0))
    return pl.pallas_call(
        body, name="swa_bwd", grid=(nb + 2,),
        in_specs=[pl.BlockSpec(memory_space=pltpu.SMEM), qblk] + _swa_kv_specs(kvw) + _swa_kv_specs(kvw)
        + [qblk, _const(TOKEN)],
        out_specs=[qblk, kv_out, kv_out, _acc((1, LANE))],
        out_shape=[pltpu.HBM((_lp(), SWA_HEADS * DH), F32), pltpu.HBM((_lp(), kvw), F32),
                   pltpu.HBM((_lp(), kvw), F32), pltpu.HBM((1, LANE), F32)],
        scratch_shapes=[pltpu.VMEM((BLK, kvw), F32)] * 4,
        compiler_params=_params(16, dimension_semantics=_seq()),
    )(sinks, *_hbm(qs, ks, ks, ks, vs, vs, vs, do_s), token)


def _in_bwd(dqs, dks, dvs, dqg, dkg, dvg, drg, dz, dpre1, w_in_t, wg2_p):
    tm = _row_tile(384)
    lp = _lp()
    widths = (512, 128, 128, 256, 256, 512, 512)
    offs = (O_QS, O_KS, O_VS, O_QG, O_KG, O_VG, O_RG)

    def body(*refs):
        parts, (dz_ref, dp1_ref, w_ref, wg2_ref, dproj_ref, dh0_ref, dbin_ref, dbg_ref) = refs[:7], refs[7:]

        @pl.when(pl.program_id(0) == 0)
        def _():
            dbin_ref[...] = jnp.zeros_like(dbin_ref)
            dbg_ref[...] = jnp.zeros_like(dbg_ref)

        for pos, h in enumerate(HEAD_POS):
            val = parts[0][:, pos * DH:(pos + 1) * DH]
            dproj_ref[:, O_QS + h * DH:O_QS + (h + 1) * DH] = val.astype(BF16)
            dbin_ref[:, O_QS + h * DH:O_QS + (h + 1) * DH] += jnp.sum(val, axis=0, keepdims=True)
        for p_ref, off, wd in zip(parts[1:], offs[1:], widths[1:]):
            val = p_ref[...]
            dproj_ref[:, off:off + wd] = val.astype(BF16)
            dbin_ref[:, off:off + wd] += jnp.sum(val, axis=0, keepdims=True)
        dz = dz_ref[...]
        dlr = _dot_nt(dz, wg2_ref[...])
        dproj_ref[:, O_LR:O_LR + LANE] = dlr.astype(BF16)
        dbin_ref[:, O_LR:O_LR + LANE] += jnp.sum(dlr, axis=0, keepdims=True)
        dbg_ref[...] += jnp.sum(dz, axis=0, keepdims=True)
        dh0_ref[...] = ALPHA * dp1_ref[...] + _dot(dproj_ref[...], w_ref[...])

    return pl.pallas_call(
        body, name="in_bwd", grid=(lp // tm,),
        in_specs=[_rows(tm, w) for w in widths] + [_rows(tm, 256), _rows(tm, D), _const((D_IN_P, D)), _const((LANE, 256))],
        out_specs=[_rows(tm, D_IN_P), _rows(tm, D), _acc((1, D_IN_P)), _acc((1, 256))],
        out_shape=[pltpu.HBM((lp, D_IN_P), BF16), pltpu.HBM((lp, D), F32),
                   pltpu.HBM((1, D_IN_P), F32), pltpu.HBM((1, 256), F32)],
        compiler_params=_params(40, dimension_semantics=_seq()),
    )(*_hbm(dqs, dks, dvs, dqg, dkg, dvg, drg, dz, dpre1, w_in_t, wg2_p))


def _ln_in_bwd(x, meta_ext, dh0, g, token):
    tr = min(LN_ROWS, SEQ)

    def ln_bwd(x_ref, dh_ref, g_ref, dx_ref, dg_ref, db_ref, so_far=None):
        @pl.when(pl.program_id(0) == 0)
        def _():
            dg_ref[...] = jnp.zeros_like(dg_ref) if so_far is None else so_far[0][...]
            db_ref[...] = jnp.zeros_like(db_ref) if so_far is None else so_far[1][...]

        xhat, rstd = _ln_stats(x_ref[...])
        dh = dh_ref[...]
        dx_ref[...] = _ln_bwd(dh, xhat, rstd, g_ref[...])
        dg_ref[...] += jnp.sum(dh * xhat, axis=0, keepdims=True)
        db_ref[...] += jnp.sum(dh, axis=0, keepdims=True)

    def body(x_ref, dh_ref, g_ref, token_ref, dx_ref, dg_ref, db_ref):
        ln_bwd(x_ref, dh_ref, g_ref, dx_ref, dg_ref, db_ref)

    def meta_body(m_ref, dh_ref, g_ref, dg_real_ref, db_real_ref, dm_ref, dg_ref, db_ref):
        ln_bwd(m_ref, dh_ref, g_ref, dm_ref, dg_ref, db_ref, (dg_real_ref, db_real_ref))

    sums = [pltpu.HBM((1, D), F32), pltpu.HBM((1, D), F32)]
    dx, dg, db = pl.pallas_call(
        body, name="ln_in_bwd", grid=(SEQ // tr,),
        in_specs=[_rows(tr, D), _rows(tr, D), _const((1, D)), _const(TOKEN)],
        out_specs=[_rows(tr, D), _acc((1, D)), _acc((1, D))],
        out_shape=[pltpu.HBM((SEQ, D), F32)] + sums,
        compiler_params=_params(32, dimension_semantics=_seq()),
    )(*_hbm(x, dh0, g), token)
    dm, dg, db = pl.pallas_call(
        meta_body, name="ln_in_bwd_meta", grid=(1,),
        in_specs=[_const((BLK, D)), pl.BlockSpec((BLK, D), lambda i: (SEQ // BLK, 0))] + [_const((1, D))] * 3,
        out_specs=[_acc((BLK, D)), _acc((1, D)), _acc((1, D))],
        out_shape=[pltpu.HBM((BLK, D), F32)] + sums,
        compiler_params=_params(16, dimension_semantics=_seq()),
    )(*_hbm(meta_ext, dh0, g, dg, db))
    return dx, dm, dg, db


def _local_step(x, target, ln_in_g, ln_in_b, b_in, bg2, sinks, gn, g1, b1, g2, b2,
                token, fetch_first, fetch_rest, fetch_ffn, exchange_ffn, ship_ffn, exchange_w_in, ship_w_in):
    row = lambda v: v.reshape(1, -1).astype(F32)
    b_in_p = jnp.pad(row(b_in), ((0, 0), (0, D_IN_P - D_IN)))
    gn4 = row(gn)
    sinks = sinks.reshape(-1).astype(F32)

    h_real = _ln_in_fwd_real(x, row(ln_in_g), row(ln_in_b), token)
    w_in_windows, meta_full, wg2 = fetch_first([h_real])
    meta_ext = jnp.pad(meta_full, ((META_OFF, BLK - CH), (0, 0)))
    wg2_p = jnp.pad(wg2, ((0, LANE - wg2.shape[0]), (0, 0))).astype(BF16)
    h0 = _ln_in_fwd_meta(h_real, meta_ext, row(ln_in_g), row(ln_in_b))
    qs, ks, vs, qg, kg, vg, rg, glr, z, w_in_t = _in_proj(h0, w_in_windows, b_in_p, wg2_p, row(bg2))
    o_s = _swa_fwd(sinks, qs, ks, vs)
    o_gla, st_all = _gla_fwd(qg, kg, vg, z)
    w_out, token = fetch_rest([o_s, o_gla])
    o, pre1, h1 = _post_mix(o_s, o_gla, rg, h0, gn4, w_out, row(g1), row(b1), token)
    wg_t, wu_t, wd = fetch_ffn([pre1])
    a, dgate, dup, dpre2, loss, dg2, db2 = _ffn_fwd_loss_bwd(h1, wg_t, wu_t, wd, target, row(g2), row(b2))
    dpre1, dg1, db1, do_s, do_gla, drg, dgn = _ffn_out_bwd(dpre2, dgate, dup, pre1, wg_t, wu_t, row(g1), w_out, o_gla,
                                                           rg, gn4)
    dwd = _atb(a, dpre2, "dw_down")
    dwg_t = _atb(dgate, h1, "dw_gate")
    dwu_t = _atb(dup, h1, "dw_up")
    token = exchange_ffn(dict(w_out=_atb(o, dpre1, "dw_out"), w_g=dwg_t, w_u=dwu_t, w_d=dwd))
    dqg, dkg, dvg, dz = _gla_bwd(qg, kg, vg, z, do_gla, st_all, token)
    token = ship_ffn([dqg])
    dqs, dks, dvs, dsinks = _swa_bwd(sinks, qs, ks, vs, do_s, token)
    dproj, dh0, db_in_p, dbg2 = _in_bwd(dqs, dks, dvs, dqg, dkg, dvg, drg, dz, dpre1, w_in_t, wg2_p)
    token = exchange_w_in(_atb(dproj, h0, "dw_in", windows=(W_IN_STARTS, W_IN_WIN)))
    dwg2_p = _atb(glr, dz, "dw_gate_lr2", token)
    token = ship_w_in([dwg2_p])
    dx, dmeta_blk, dg_in, db_in_ln = _ln_in_bwd(x, meta_ext, dh0, row(ln_in_g), token)

    small = dict(meta_blk=dmeta_blk, ln_in_g=dg_in, ln_in_b=db_in_ln, ln1_g=dg1, ln1_b=db1, ln2_g=dg2, ln2_b=db2,
                 b_in_p=db_in_p, wg2_p=dwg2_p, bg2=dbg2, sinks=dsinks, gn=dgn, loss=loss)
    return dx, small


HBM = pl.BlockSpec(memory_space=pltpu.HBM)


def _place():
    return lax.axis_index("x"), lax.axis_index("y"), lax.axis_index("c")


def _other_chips(x, y):
    return [(1 - x, y), (x, 1 - y), (1 - x, 1 - y)]


def _dma_sems(n):
    return pltpu.SemaphoreType.DMA((n,))


def _comm_params():
    return pltpu.CompilerParams(has_side_effects=True)


SEM = pl.BlockSpec(memory_space=pltpu.SEMAPHORE)


PER_ARRAY = dict(gather=3, scatter=3, sibling=N_CHIPS)


def _ici_copies(kind, landing, srcs, lands, send_sems, recv_sems):
    x, y, c = _place()
    mine = 2 * x + y
    copies = []
    for a in range(len(srcs)):
        if kind == "sibling":
            for s in range(N_CHIPS):
                copies.append(pltpu.make_async_remote_copy(
                    srcs[a].at[s, 1 - c], lands[a].at[s], send_sems.at[N_CHIPS * a + s], recv_sems.at[N_CHIPS * a + s],
                    device_id=(x, y, 1 - c), device_id_type=MESH))
            continue
        for j, (px, py) in enumerate(_other_chips(x, y)):
            slab = 2 * px + py if landing else mine
            if kind == "gather":
                src, dst = srcs[a].at[c], lands[a].at[slab, c]
            else:
                src, dst = srcs[a].at[2 * px + py], lands[a].at[slab]
            copies.append(pltpu.make_async_remote_copy(src, dst, send_sems.at[3 * a + j], recv_sems.at[3 * a + j],
                                                       device_id=(px, py, c), device_id_type=MESH))
    return copies


def _split_params():
    return pltpu.CompilerParams(has_side_effects=pltpu.SideEffectType.DATAFLOW_SIDE_EFFECTING)


def _ici_start(kind, srcs, land_shapes, after, name):
    n = len(srcs)
    lands = [pltpu.with_memory_space_constraint(lax.empty(s, a.dtype), pltpu.HBM) for s, a in zip(land_shapes, srcs)]

    def body(*refs):
        outs = refs[2 * n + len(after):]
        for cp in _ici_copies(kind, False, refs[:n], refs[n:2 * n], outs[0], outs[1]):
            cp.start()
        outs[-1][...] = jnp.zeros(TOKEN, F32)

    outs = pl.pallas_call(
        body, name=name, in_specs=[HBM] * (2 * n) + [pl.BlockSpec(memory_space=pl.ANY)] * len(after),
        out_specs=[SEM, SEM] + [HBM] * (2 * n) + [pl.BlockSpec(memory_space=pltpu.VMEM)],
        out_shape=[_dma_sems(PER_ARRAY[kind] * n)] * 2 + [pltpu.HBM(a.shape, a.dtype) for a in list(srcs) + lands]
        + [jax.ShapeDtypeStruct(TOKEN, F32)],
        input_output_aliases={i: 2 + i for i in range(2 * n)},
        compiler_params=_split_params(),
    )(*_hbm(*srcs), *lands, *after)
    return outs[:-1], outs[-1]


def _ici_wait(kind, handle, after, name):
    n = (len(handle) - 2) // 2

    def body(*refs):
        for cp in _ici_copies(kind, True, refs[:n], refs[n:2 * n], refs[2 * n], refs[2 * n + 1]):
            cp.wait_send()
            cp.wait_recv()

    outs = pl.pallas_call(
        body, name=name, in_specs=[HBM] * (2 * n) + [SEM, SEM] + [pl.BlockSpec(memory_space=pl.ANY)] * len(after),
        out_specs=[HBM] * (2 * n), out_shape=[pltpu.HBM(a.shape, a.dtype) for a in handle[2:]],
        input_output_aliases={i: i for i in range(2 * n)},
        compiler_params=_split_params(),
    )(*handle[2:], handle[0], handle[1], *after)
    return list(outs[:n]), list(outs[n:])


def _forward_copies(landing, arrs, send_sems, recv_sems):
    x, y, c = _place()
    copies = []
    for a in range(len(arrs)):
        for j, (px, py) in enumerate(_other_chips(x, y)):
            half = 1 - c if landing else c
            copies.append(pltpu.make_async_remote_copy(
                arrs[a].at[2 * px + py, c], arrs[a].at[2 * px + py, half], send_sems.at[3 * a + j],
                recv_sems.at[3 * a + j], device_id=(x, y, 1 - c), device_id_type=MESH))
    return copies


def _gather_wait_forward(handle, groups, after, name):
    n = (len(handle) - 2) // 2
    assert sum(groups) == n

    def body(*refs):
        outs = refs[2 * n + 2 + len(after):]
        lands, sems = outs[n:2 * n], outs[2 * n:-1]
        arrivals = _ici_copies("gather", True, refs[:n], refs[n:2 * n], refs[2 * n], refs[2 * n + 1])
        sends, first = [], 0
        for g, count in enumerate(groups):
            sends += _forward_copies(False, lands[first:first + count], sems[2 * g], sems[2 * g + 1])
            first += count
        for cp, send in zip(arrivals, sends):
            cp.wait_recv()
            send.start()
        for cp in arrivals:
            cp.wait_send()
        outs[-1][...] = jnp.zeros(TOKEN, F32)

    outs = pl.pallas_call(
        body, name=name, in_specs=[HBM] * (2 * n) + [SEM, SEM] + [pl.BlockSpec(memory_space=pl.ANY)] * len(after),
        out_specs=[HBM] * (2 * n) + [SEM] * (2 * len(groups)) + [pl.BlockSpec(memory_space=pltpu.VMEM)],
        out_shape=[pltpu.HBM(a.shape, a.dtype) for a in handle[2:]]
        + [_dma_sems(3 * count) for count in groups for _ in range(2)] + [jax.ShapeDtypeStruct(TOKEN, F32)],
        input_output_aliases={i: i for i in range(2 * n)},
        compiler_params=_split_params(),
    )(*handle[2:], handle[0], handle[1], *after)
    lands, sems, handles, first = outs[n:2 * n], outs[2 * n:-1], [], 0
    for g, count in enumerate(groups):
        handles.append([sems[2 * g], sems[2 * g + 1], *lands[first:first + count]])
        first += count
    return list(outs[:n]), handles, outs[-1]


def _forward_wait(handle, after, name):
    n = len(handle) - 2

    def body(*refs):
        for cp in _forward_copies(True, refs[:n], refs[n], refs[n + 1]):
            cp.wait_send()
            cp.wait_recv()

    return list(pl.pallas_call(
        body, name=name, in_specs=[HBM] * n + [SEM, SEM] + [pl.BlockSpec(memory_space=pl.ANY)] * len(after),
        out_specs=[HBM] * n, out_shape=[pltpu.HBM(a.shape, a.dtype) for a in handle[2:]],
        input_output_aliases={i: i for i in range(n)},
        compiler_params=_split_params(),
    )(*handle[2:], handle[0], handle[1], *after))


def _add_halves(core, grads, recvs, dtypes, name):
    n = len(grads)
    heights = [g.shape[2] for g in grads]

    def body(c_ref, *refs):
        for a in range(n):
            refs[2 * n + a][...] = (refs[2 * a][0] + refs[2 * a + 1][...]).astype(dtypes[a])

    slab = lambda h: pl.BlockSpec((1, h, D), lambda s, c: (s, 0, 0))
    mine = lambda h: pl.BlockSpec((1, 1, h, D), lambda s, c: (s, c[0], 0, 0))
    return pl.pallas_call(
        body, name=name,
        grid_spec=pltpu.PrefetchScalarGridSpec(
            num_scalar_prefetch=1, grid=(N_CHIPS,),
            in_specs=[spec(h) for h in heights for spec in (mine, slab)], out_specs=[slab(h) for h in heights]),
        out_shape=[pltpu.HBM((N_CHIPS, h, D), dt) for h, dt in zip(heights, dtypes)],
        compiler_params=_params(32, dimension_semantics=_seq()),
    )(core, *_hbm(*[a for pair in zip(grads, recvs) for a in pair]))


N_DEVICES = 2 * N_CHIPS
PEER_FLIPS = [(dx, dy, dc) for dx in (0, 1) for dy in (0, 1) for dc in (0, 1)][1:]


def _small_copies(landing, p_ref, out_ref, send_sems, recv_sems):
    x, y, c = _place()
    flip = lambda v, d: 1 - v if d else v
    copies = []
    for k, flips in enumerate(PEER_FLIPS):
        px, py, pc = (flip(v, d) for v, d in zip((x, y, c), flips))
        slab = 4 * px + 2 * py + pc if landing else 4 * x + 2 * y + c
        copies.append(pltpu.make_async_remote_copy(p_ref, out_ref.at[slab], send_sems.at[k], recv_sems.at[k],
                                                   device_id=(px, py, pc), device_id_type=MESH))
    return copies


def _small_wait(handle, after):
    def body(p_ref, land_ref, send_sems, recv_sems, *rest):
        for cp in _small_copies(True, p_ref, land_ref, send_sems, recv_sems):
            cp.wait_send()
            cp.wait_recv()

    return pl.pallas_call(
        body, name="small_exchange_wait", in_specs=[HBM, HBM, SEM, SEM] + [pl.BlockSpec(memory_space=pl.ANY)] * len(after),
        out_specs=[HBM, HBM], out_shape=[pltpu.HBM(a.shape, F32) for a in handle[2:]],
        input_output_aliases={0: 0, 1: 1},
        compiler_params=_split_params(),
    )(handle[2], handle[3], handle[0], handle[1], *after)


def _sum_chips(slots, firsts, rests, after, name):
    n = len(firsts)

    def body(i_ref, *refs):
        outs = refs[4 * n + len(after):]
        for a in range(n):
            first, r1, r2, r3 = refs[4 * a:4 * a + 4]
            outs[a][...] = ((first[...].astype(F32) + r1[...].astype(F32)) + r2[...].astype(F32)) + r3[...].astype(F32)

    slab = lambda h, k: pl.BlockSpec((1, h, D), lambda i, ix: (ix[k], 0, 0))
    heights = [f.shape[1] for f in firsts]
    return pl.pallas_call(
        body, name=name,
        grid_spec=pltpu.PrefetchScalarGridSpec(
            num_scalar_prefetch=1, grid=(1,),
            in_specs=[slab(h, k) for h in heights for k in range(4)] + [pl.BlockSpec(memory_space=pl.ANY)] * len(after),
            out_specs=[slab(h, 4) for h in heights]),
        out_shape=[pltpu.HBM((2, h, D), F32) for h in heights],
        compiler_params=_params(48, dimension_semantics=_seq()),
    )(slots, *_hbm(*[a for f, r in zip(firsts, rests) for a in (f, r, r, r)]), *after)


def _join_copies(landing, arrs, send_sems, recv_sems):
    x, y, c = _place()
    slab = 1 - c if landing else c
    return [pltpu.make_async_remote_copy(arr.at[slab], arr.at[slab], send_sems.at[a], recv_sems.at[a],
                                         device_id=(x, y, 1 - c), device_id_type=MESH) for a, arr in enumerate(arrs)]


def _join_halves(halves, name):
    n = len(halves)

    def body(*refs):
        outs = refs[n:2 * n]
        send_sems, recv_sems = refs[2 * n:]
        sends = _join_copies(False, outs, send_sems, recv_sems)
        for cp in sends:
            cp.start()
        for cp in _join_copies(True, outs, send_sems, recv_sems):
            cp.wait_recv()
        for cp in sends:
            cp.wait_send()

    return list(pl.pallas_call(
        body, name=name, in_specs=[HBM] * n, out_specs=[HBM] * n,
        out_shape=[pltpu.HBM(h.shape, F32) for h in halves],
        input_output_aliases={a: a for a in range(n)},
        scratch_shapes=[_dma_sems(n)] * 2,
        compiler_params=_comm_params(),
    )(*_hbm(*halves)))


def _join_small_start(halves, pack, name):
    n, peers = len(halves), len(PEER_FLIPS)
    land = pltpu.with_memory_space_constraint(lax.empty((N_DEVICES,) + pack.shape, F32), pltpu.HBM)

    def body(*refs):
        outs = refs[n + 2:]
        for cp in _join_copies(False, refs[:n], outs[0], outs[1]):
            cp.start()
        for cp in _small_copies(False, refs[n], refs[n + 1], outs[2], outs[3]):
            cp.start()
        outs[-1][...] = jnp.zeros(TOKEN, F32)

    outs = pl.pallas_call(
        body, name=name, in_specs=[HBM] * (n + 2),
        out_specs=[SEM] * 4 + [HBM] * (n + 2) + [pl.BlockSpec(memory_space=pltpu.VMEM)],
        out_shape=[_dma_sems(n)] * 2 + [_dma_sems(peers)] * 2
        + [pltpu.HBM(a.shape, a.dtype) for a in list(halves) + [pack, land]] + [jax.ShapeDtypeStruct(TOKEN, F32)],
        input_output_aliases={i: 4 + i for i in range(n + 2)},
        compiler_params=_split_params(),
    )(*_hbm(*halves, pack), land)
    return [outs[0], outs[1], *outs[4:4 + n]], [outs[2], outs[3], outs[4 + n], outs[5 + n]], outs[-1]


def _join_wait(handle, after, name):
    n = len(handle) - 2

    def body(*refs):
        for cp in _join_copies(True, refs[:n], refs[n], refs[n + 1]):
            cp.wait_send()
            cp.wait_recv()

    return list(pl.pallas_call(
        body, name=name, in_specs=[HBM] * n + [SEM, SEM] + [pl.BlockSpec(memory_space=pl.ANY)] * len(after),
        out_specs=[HBM] * n, out_shape=[pltpu.HBM(a.shape, a.dtype) for a in handle[2:]],
        input_output_aliases={i: i for i in range(n)},
        compiler_params=_split_params(),
    )(*handle[2:], handle[0], handle[1], *after))


def _chip_partials(grads, fetched, wire_dtypes, name):
    core = lax.axis_index("c").astype(jnp.int32).reshape(1)
    return list(_add_halves(core, grads, fetched, wire_dtypes, name))


def _chip_sums(parts, got, after, name):
    x, y, c = _place()
    others = [2 * px + py for px, py in _other_chips(x, y)]
    own_first = jnp.stack([2 * x + y] + others + [c]).astype(jnp.int32)
    return list(_sum_chips(own_first, parts, got, after, name))


ADAMW_STEPS = 8


def _adamw(params, by_row, chip, window_step):
    n = len(params)
    rows, _, cols = by_row[0].shape
    block = lambda shape: pl.BlockSpec((shape[0] // ADAMW_STEPS, shape[1]), lambda i, c: (i, 0))
    assert all(a.shape[0] % (8 * ADAMW_STEPS) == 0 for p in params for a in p)

    def body(c_ref, *refs):
        w_hbm, g_ref, m_hbm, v_hbm = refs[4 * n:4 * n + 4]
        results, (ins_ref, outs_ref, sems) = refs[8 * n + 4:8 * n + 8], refs[8 * n + 8:]
        loads = [pltpu.make_async_copy(src.at[:, 0, :], ins_ref.at[k], sems.at[k])
                 for k, src in enumerate((w_hbm, m_hbm, v_hbm))]
        stores = [pltpu.make_async_copy(outs_ref.at[k], dst.at[:, 0, :], sems.at[3 + k]) for k, dst in enumerate(results)]
        first = pl.program_id(0) == 0

        @pl.when(first)
        def _():
            for cp in loads:
                cp.start()

        for a in range(n):
            w_ref, a_g_ref, m_ref, v_ref = refs[4 * a:4 * a + 4]
            outs = refs[4 * n + 4 + 4 * a:4 * n + 8 + 4 * a]
            g = a_g_ref[...]
            outs[0][...] = g
            outs[1][...], outs[2][...], outs[3][...] = _adamw_math(w_ref[...], g, m_ref[...], v_ref[...])

        @pl.when(first)
        def _():
            for cp in loads:
                cp.wait()
            for lo in range(0, cols, LANE):
                lanes = slice(lo, lo + LANE)
                g = g_ref[0:rows, lanes]
                for s in range(1, N_CHIPS):
                    g = jnp.where(c_ref[0] == s, g_ref[s * window_step:s * window_step + rows, lanes], g)
                outs_ref[0, :, lanes] = g
                outs_ref[1, :, lanes], outs_ref[2, :, lanes], outs_ref[3, :, lanes] = _adamw_math(
                    ins_ref[0, :, lanes], g, ins_ref[1, :, lanes], ins_ref[2, :, lanes])
            for cp in stores:
                cp.start()

        @pl.when(pl.program_id(0) == ADAMW_STEPS - 1)
        def _():
            for cp in stores:
                cp.wait()

    outs = pl.pallas_call(
        body, name="adamw_matrices",
        grid_spec=pltpu.PrefetchScalarGridSpec(
            num_scalar_prefetch=1, grid=(ADAMW_STEPS,),
            in_specs=[block(a.shape) for p in params for a in p] + [HBM, _const(by_row[1].shape), HBM, HBM],
            out_specs=[block(p[0].shape) for p in params for _ in range(4)] + [HBM] * 4,
            scratch_shapes=[pltpu.VMEM((3, rows, cols), F32), pltpu.VMEM((4, rows, cols), F32), _dma_sems(7)]),
        out_shape=[pltpu.HBM(p[0].shape, F32) for p in params for _ in range(4)] + [pltpu.HBM((rows, 1, cols), F32)] * 4,
        compiler_params=_params(48, dimension_semantics=_seq()),
    )(chip, *_hbm(*[a for p in params for a in p], *by_row))
    return [outs[4 * a:4 * a + 4] for a in range(n)], outs[4 * n:]


def _adamw_math(w, g, m, v):
    nm = ADAM_B1 * m + (1.0 - ADAM_B1) * g
    nv = ADAM_B2 * v + (1.0 - ADAM_B2) * (g * g)
    m_hat = nm / (1.0 - ADAM_B1 ** ADAM_STEP)
    v_hat = nv / (1.0 - ADAM_B2 ** ADAM_STEP)
    return -ADAM_LR * (m_hat / (jnp.sqrt(v_hat) + ADAM_EPS) + ADAM_WD * w), nm, nv


SMALL = (("meta_tokens", (N_META, D // N_CHIPS)), ("ln_in_g", (1, D)), ("ln_in_b", (1, D)), ("b_in", (1, D_IN)),
         ("w_gate_lr2", (GATE_RANK, GLA_HEADS * DK // N_CHIPS)), ("b_gate_lr2", (1, GLA_HEADS * DK)),
         ("attn_sinks", (1, SWA_HEADS)),
         ("gla_norm_g", (1, DV)), ("ln1_g", (1, D)), ("ln1_b", (1, D)), ("ln2_g", (1, D)), ("ln2_b", (1, D)))
ROW_META, ROW_B_IN, ROW_TAIL, ROW_WG2 = 0, 22, 25, 32
ROW_LN = dict(ln_in_g=16, ln_in_b=17, ln1_g=18, ln1_b=19, ln2_g=20, ln2_b=21)
TAIL_BG2, TAIL_SINKS, TAIL_GN, TAIL_LOSS = 0, 256, 256 + SWA_HEADS, 256 + SWA_HEADS + DV


def _adamw_small(place, packs, own, params):
    n = len(SMALL)

    def body(place_ref, packs_ref, own_ref, *refs):
        ins, outs, p_ref = refs[:3 * n], refs[3 * n:-1], refs[-1]
        me, c = place_ref[0], place_ref[1]
        total = jnp.where(me == 0, own_ref[...], packs_ref[0])
        for i in range(1, N_DEVICES):
            total = total + jnp.where(me == i, own_ref[...], packs_ref[i])
        p_ref[...] = total
        outs[4 * n][...] = total[ROW_TAIL:ROW_TAIL + 1, TAIL_LOSS:TAIL_LOSS + 1]

        def mine(width, rows):
            part = lambda s: p_ref[rows, s * width:(s + 1) * width]
            return jnp.where(c == 0, part(0), jnp.where(c == 1, part(1), jnp.where(c == 2, part(2), part(3))))

        tail = lambda lo, width: p_ref[ROW_TAIL:ROW_TAIL + 1, lo:lo + width]
        grads = dict(
            meta_tokens=mine(D // N_CHIPS, slice(ROW_META, ROW_META + N_META)),
            b_in=jnp.concatenate([p_ref[ROW_B_IN:ROW_B_IN + 1, :], p_ref[ROW_B_IN + 1:ROW_B_IN + 2, :],
                                  p_ref[ROW_B_IN + 2:ROW_B_IN + 3, 0:D_IN - 2 * D]], axis=1),
            w_gate_lr2=mine(256 // N_CHIPS, slice(ROW_WG2, ROW_WG2 + 16)),
            b_gate_lr2=tail(TAIL_BG2, 256), attn_sinks=tail(TAIL_SINKS, SWA_HEADS), gla_norm_g=tail(TAIL_GN, DV),
            **{k: p_ref[r:r + 1, :] for k, r in ROW_LN.items()})
        for i, (name, _) in enumerate(SMALL):
            g = grads[name]
            outs[4 * i][...] = g
            outs[4 * i + 1][...], outs[4 * i + 2][...], outs[4 * i + 3][...] = _adamw_math(
                ins[3 * i][...], g, ins[3 * i + 1][...], ins[3 * i + 2][...])

    whole = lambda shape: pl.BlockSpec(shape, lambda i, c: (0,) * len(shape))
    outs = pl.pallas_call(
        body, name="adamw_small",
        grid_spec=pltpu.PrefetchScalarGridSpec(
            num_scalar_prefetch=1, grid=(1,),
            in_specs=[whole(packs.shape), whole(own.shape)] + [whole(s) for _, s in SMALL for _ in range(3)],
            out_specs=[whole(s) for _, s in SMALL for _ in range(4)] + [whole((1, 1))],
            scratch_shapes=[pltpu.VMEM(own.shape, F32)]),
        out_shape=[pltpu.HBM(s, F32) for _, s in SMALL for _ in range(4)] + [pltpu.HBM((1, 1), F32)],
        compiler_params=_params(16, dimension_semantics=_seq()),
    )(place, *_hbm(packs, own, *[a for p in params for a in p]))
    return [outs[4 * i:4 * i + 4] for i in range(n)], outs[4 * n]


def _small_pack(gr):
    names = ["meta_blk"] + list(ROW_LN) + ["b_in_p", "wg2_p", "bg2", "sinks", "gn", "loss"]
    gate_w = GLA_HEADS * DK

    def body(*refs):
        src, out = dict(zip(names, refs)), refs[-1]
        out[...] = jnp.zeros_like(out)
        out[ROW_META:ROW_META + N_META, :] = src["meta_blk"][META_OFF:CH, :]
        for k, r in ROW_LN.items():
            out[r:r + 1, :] = src[k][...]
        for j in range(-(-D_IN // D)):
            width = min(D, D_IN - j * D)
            out[ROW_B_IN + j:ROW_B_IN + j + 1, 0:width] = src["b_in_p"][:, j * D:j * D + width]
        tail = slice(ROW_TAIL, ROW_TAIL + 1)
        out[tail, TAIL_BG2:TAIL_BG2 + gate_w] = src["bg2"][...]
        out[tail, TAIL_SINKS:TAIL_SINKS + SWA_HEADS] = src["sinks"][:, 0:SWA_HEADS]
        out[tail, TAIL_GN:TAIL_GN + DV] = src["gn"][...]
        out[tail, TAIL_LOSS:TAIL_LOSS + 1] = src["loss"][:, 0:1]
        out[ROW_WG2:ROW_WG2 + GATE_RANK, 0:gate_w] = src["wg2_p"][0:GATE_RANK, :]

    arrays = [gr[k] for k in names]
    return pl.pallas_call(
        body, name="small_pack", grid=(1,),
        in_specs=[_acc(a.shape) for a in arrays], out_specs=_acc((SMALL_ROWS, D)),
        out_shape=pltpu.HBM((SMALL_ROWS, D), F32),
        compiler_params=_params(16, dimension_semantics=_seq()),
    )(*_hbm(*arrays))


BIG = ("w_in", "w_out", "w_g", "w_u", "w_d")


def kernel(x, meta_tokens, ln_in_g, ln_in_b, w_in, b_in, w_gate_lr2, b_gate_lr2, attn_sinks, gla_norm_g, w_out, ln1_g, ln1_b, w_ffn_gate, w_ffn_up, w_ffn_down, ln2_g, ln2_b, loss_target, m_meta_tokens, m_ln_in_g, m_ln_in_b, m_w_in, m_b_in, m_w_gate_lr2, m_b_gate_lr2, m_attn_sinks, m_gla_norm_g, m_w_out, m_ln1_g, m_ln1_b, m_w_ffn_gate, m_w_ffn_up, m_w_ffn_down, m_ln2_g, m_ln2_b, v_meta_tokens, v_ln_in_g, v_ln_in_b, v_w_in, v_b_in, v_w_gate_lr2, v_b_gate_lr2, v_attn_sinks, v_gla_norm_g, v_w_out, v_ln1_g, v_ln1_b, v_w_ffn_gate, v_w_ffn_up, v_w_ffn_down, v_ln2_g, v_ln2_b):
    chip = 2 * lax.axis_index("x") + lax.axis_index("y")

    halves = lambda a: a.reshape(2, a.shape[0] // 2, a.shape[1])
    r_in = SHARD_ROWS["w_in"]
    first = [halves(a) for a in (jnp.pad(w_in[0].T.astype(BF16), ((0, W_IN_WIN - r_in), (0, 0))), meta_tokens,
                                 w_gate_lr2[0])]
    rest = [halves(a) for a in (w_out[0].astype(BF16), w_ffn_gate[0].T.astype(BF16), w_ffn_up[0].T.astype(BF16),
                                w_ffn_down[0].astype(BF16))]
    lands = lambda arrs: [(N_CHIPS,) + a.shape for a in arrs]
    first_handle, first_token = _ici_start("gather", first, lands(first), [], "gather_first_start")
    rest_handle, token = _ici_start("gather", rest, lands(rest), [first_token], "gather_rest_start")
    own_slab = lambda got, shards: [lax.dynamic_update_index_in_dim(g, s, chip, axis=0) for g, s in zip(got, shards)]
    fetching = {}

    def fetch_first(after):
        shards, (forwarding,), _ = _gather_wait_forward(first_handle, [len(first)], after, "gather_first_wait")
        g_in, g_meta, g_wg2 = own_slab(_forward_wait(forwarding, [], "gather_first_forward_wait"), shards)
        w_in_windows = g_in.reshape(N_CHIPS, W_IN_WIN, D)
        meta_full = jnp.concatenate([g_meta[s].reshape(N_META, -1) for s in range(N_CHIPS)], axis=1)
        wg2_full = jnp.concatenate([g_wg2[s].reshape(w_gate_lr2.shape[1], -1) for s in range(N_CHIPS)], axis=1)
        return w_in_windows, meta_full, wg2_full

    def fetch_rest(after):
        shards, (w_out_forwarding, fetching["handle"]), forward_token = _gather_wait_forward(
            rest_handle, [1, len(rest) - 1], after, "gather_rest_wait")
        g_out, = own_slab(_forward_wait(w_out_forwarding, [], "gather_w_out_forward_wait"), shards[:1])
        fetching["shards"] = shards[1:]
        return g_out.reshape(-1, D), forward_token

    def fetch_ffn(after):
        got = _forward_wait(fetching["handle"], after, "gather_ffn_forward_wait")
        return [g.reshape(-1, D) for g in own_slab(got, fetching["shards"])]

    sent = {}
    split = lambda grads: [g.reshape(N_CHIPS, 2, -1, D) for g in grads]

    def exchange(key, grads):
        grads = split(grads)
        sent[key + "_halves"], exchange_token = _ici_start(
            "sibling", grads, [(N_CHIPS,) + a.shape[2:] for a in grads], [], "sibling_" + key + "_start")
        return exchange_token

    def ship(key, after):
        grads, fetched = _ici_wait("sibling", sent[key + "_halves"], after, "sibling_" + key + "_wait")
        parts = _chip_partials(grads, fetched, [BF16] * len(grads), "add_halves_" + key)
        sent[key], ship_token = _ici_start("scatter", parts, [p.shape for p in parts], [], "scatter_" + key + "_start")
        return ship_token

    dx, gr = _local_step(
        x[0], loss_target[0], ln_in_g, ln_in_b, b_in[0], b_gate_lr2[0], attn_sinks[0], gla_norm_g[0], ln1_g[0],
        ln1_b[0], ln2_g[0], ln2_b[0], token, fetch_first, fetch_rest, fetch_ffn,
        lambda g: exchange("ffn", [g[k] for k in BIG[1:]]), lambda after: ship("ffn", after),
        lambda g: exchange("w_in", [g]), lambda after: ship("w_in", after))
    ffn_parts, ffn_got = _ici_wait("scatter", sent["ffn"], [dx], "scatter_ffn_wait")
    join_handle, small_handle, token = _join_small_start(
        _chip_sums(ffn_parts, ffn_got, [], "sum_chips_ffn"), _small_pack(gr), "join_ffn_small_start")
    w_in_parts, w_in_got = _ici_wait("scatter", sent["w_in"], [token], "scatter_w_in_wait")
    w_in_joined = _join_halves(_chip_sums(w_in_parts, w_in_got, [], "sum_chips_w_in"), "join_w_in")
    red = [f.reshape(2 * f.shape[1], D) for f in w_in_joined + _join_wait(join_handle, w_in_joined, "join_ffn_wait")]

    big_g = dict(zip(BIG, red))
    weights = dict(meta_tokens=meta_tokens, ln_in_g=ln_in_g, ln_in_b=ln_in_b, w_in=w_in, b_in=b_in,
                   w_gate_lr2=w_gate_lr2, b_gate_lr2=b_gate_lr2, attn_sinks=attn_sinks, gla_norm_g=gla_norm_g,
                   w_out=w_out, ln1_g=ln1_g, ln1_b=ln1_b, w_ffn_gate=w_ffn_gate, w_ffn_up=w_ffn_up,
                   w_ffn_down=w_ffn_down, ln2_g=ln2_g, ln2_b=ln2_b)
    m_in = dict(meta_tokens=m_meta_tokens, ln_in_g=m_ln_in_g, ln_in_b=m_ln_in_b, w_in=m_w_in, b_in=m_b_in,
                w_gate_lr2=m_w_gate_lr2, b_gate_lr2=m_b_gate_lr2, attn_sinks=m_attn_sinks, gla_norm_g=m_gla_norm_g,
                w_out=m_w_out, ln1_g=m_ln1_g, ln1_b=m_ln1_b, w_ffn_gate=m_w_ffn_gate, w_ffn_up=m_w_ffn_up,
                w_ffn_down=m_w_ffn_down, ln2_g=m_ln2_g, ln2_b=m_ln2_b)
    v_in = dict(meta_tokens=v_meta_tokens, ln_in_g=v_ln_in_g, ln_in_b=v_ln_in_b, w_in=v_w_in, b_in=v_b_in,
                w_gate_lr2=v_w_gate_lr2, b_gate_lr2=v_b_gate_lr2, attn_sinks=v_attn_sinks, gla_norm_g=v_gla_norm_g,
                w_out=v_w_out, ln1_g=v_ln1_g, ln1_b=v_ln1_b, w_ffn_gate=v_w_ffn_gate, w_ffn_up=v_w_ffn_up,
                w_ffn_down=v_w_ffn_down, ln2_g=v_ln2_g, ln2_b=v_ln2_b)
    names = list(weights)
    big_names = ("w_in", "w_out", "w_ffn_gate", "w_ffn_up", "w_ffn_down")

    grads, delta, new_m, new_v = {}, {}, {}, {}
    flips = [(lambda a: a.T) if kk in ("w_g", "w_u") else (lambda a: a) for kk in BIG[1:]]
    by_row = lambda a: jnp.transpose(a, (2, 0, 1))
    updated, updated_w_in = _adamw(
        [(flip(weights[k][0]), big_g[kk], flip(m_in[k][0]), flip(v_in[k][0]))
         for k, kk, flip in zip(big_names[1:], BIG[1:], flips)],
        (by_row(w_in), big_g["w_in"], by_row(m_w_in), by_row(v_w_in)), chip.astype(jnp.int32).reshape(1), r_in % BF16_ROWS)
    for k, flip, results in zip(big_names[1:], flips, updated):
        grads[k], delta[k], new_m[k], new_v[k] = (flip(t)[None] for t in results)
    grads["w_in"], delta["w_in"], new_m["w_in"], new_v["w_in"] = (jnp.transpose(t, (1, 2, 0)) for t in updated_w_in)
    small_in = [tuple(src[k].reshape(shape) for src in (weights, m_in, v_in)) for k, shape in SMALL]
    place = jnp.stack([2 * chip + lax.axis_index("c"), chip]).astype(jnp.int32)
    small_own, small_all = _small_wait(small_handle, [updated[0][0]])
    small_out, loss = _adamw_small(place, small_all, small_own, small_in)
    for (k, _), results in zip(SMALL, small_out):
        grads[k], delta[k], new_m[k], new_v[k] = (r.reshape(weights[k].shape) for r in results)

    return (loss.reshape(()), dx[None], *[grads[k] for k in names], *[delta[k] for k in names], *[new_m[k] for k in names],
            *[new_v[k] for k in names])
```

```python
import jax
import jax.numpy as jnp
from jax import lax
from jax.experimental import pallas as pl
from jax.experimental.pallas import tpu as pltpu

F32 = jnp.float32
BF16 = jnp.bfloat16
MESH = pl.DeviceIdType.MESH

D = 1024
SEQ = 4096
N_META = 16
SWA_HEADS, SWA_KV_HEADS, DH = 8, 2, 64
WINDOW = 128
GLA_HEADS, DK, DV = 4, 64, 128
GLA_TAU = 16.0
CH = 64
D_FF = 2816
D_IN = 2320
LN_EPS = 1e-5
RMS_EPS = 1e-6
ALPHA = 2.0 ** 0.25
NEG = -1e30
ADAM_LR, ADAM_B1, ADAM_B2, ADAM_EPS, ADAM_WD, ADAM_STEP = 0.001, 0.9, 0.999, 1e-8, 0.01, 10
O_QS, O_KS, O_VS, O_QG, O_KG, O_VG, O_RG, O_LR = 0, 512, 640, 768, 1024, 1280, 1792, 2304

LANE = 128
BLK = WINDOW
GATE_RANK = 16
D_IN_P = D_IN + LANE - GATE_RANK
META_OFF = CH - N_META
HEAD_POS = (0, 4, 1, 5, 2, 6, 3, 7)
LN_ROWS = 512
TOKEN = (8, LANE)
N_CHIPS = 4
SHARD_ROWS = dict(w_in=D_IN // N_CHIPS, w_out=D // N_CHIPS, w_g=D_FF // N_CHIPS, w_u=D_FF // N_CHIPS,
                  w_d=D_FF // N_CHIPS)
SMALL_ROWS = 48
BF16_ROWS = 16
W_IN_WIN = -(-SHARD_ROWS["w_in"] // (2 * BF16_ROWS)) * 2 * BF16_ROWS
W_IN_STARTS = tuple(s * SHARD_ROWS["w_in"] // BF16_ROWS * BF16_ROWS for s in range(N_CHIPS))
VMEM_CAP_MB = 64
VMEM_SPARE_MB = 6


def _lp():
    return SEQ + BLK


def _row_tile(cap):
    lp = _lp()
    return max(t for t in range(16, cap + 1, 16) if lp % t == 0)


def _params(vmem_mb, **kw):
    assert vmem_mb <= VMEM_CAP_MB - VMEM_SPARE_MB
    return pltpu.CompilerParams(vmem_limit_bytes=vmem_mb << 20, **kw)


def _seq(n=1):
    return ("arbitrary",) * n


def _const(shape):
    return pl.BlockSpec(shape, lambda *_: (0,) * len(shape), pipeline_mode=pl.Buffered(1))


def _acc(shape):
    return pl.BlockSpec(shape, lambda *_: (0,) * len(shape))


def _rows(tm, width):
    return pl.BlockSpec((tm, width), lambda i: (i, 0))


def _dot(a, b):
    return jnp.dot(a.astype(BF16), b.astype(BF16), preferred_element_type=F32)


def _dot_nt(a, b):
    return lax.dot_general(a.astype(BF16), b.astype(BF16), (((1,), (1,)), ((), ())), preferred_element_type=F32)


def _dot_tn(a, b):
    return lax.dot_general(a.astype(BF16), b.astype(BF16), (((0,), (0,)), ((), ())), preferred_element_type=F32)


def _dot_exact(a, b):
    return jnp.dot(a, b, precision=lax.Precision.HIGHEST, preferred_element_type=F32)


def _ln_stats(x):
    mu = jnp.mean(x, axis=-1, keepdims=True)
    xc = x - mu
    rstd = lax.rsqrt(jnp.mean(xc * xc, axis=-1, keepdims=True) + LN_EPS)
    return xc * rstd, rstd


def _ln_bwd(dy, xhat, rstd, g):
    dxh = dy * g
    return rstd * (dxh - jnp.mean(dxh, axis=-1, keepdims=True) - xhat * jnp.mean(dxh * xhat, axis=-1, keepdims=True))


def _sigmoid(x):
    return 1.0 / (1.0 + jnp.exp(-x))


def _iota(shape, dim):
    return lax.broadcasted_iota(jnp.int32, shape, dim)


def _hbm(*arrays):
    return tuple(pltpu.with_memory_space_constraint(a, pltpu.HBM) for a in arrays)


RING = 3


def _ring_fetch(steps, tile, sources, rings, sems):
    assert steps >= RING - 1
    step = pl.program_id(0)

    def copy(t, k):
        rows = pl.ds(pl.multiple_of(t * tile, tile), tile)
        return pltpu.make_async_copy(sources[k].at[rows, :], rings[k].at[t % RING], sems.at[k, t % RING])

    @pl.when(step == 0)
    def _():
        for t in range(RING - 1):
            for k in range(len(sources)):
                copy(t, k).start()

    @pl.when(step + (RING - 1) < steps)
    def _():
        for k in range(len(sources)):
            copy(step + (RING - 1), k).start()

    for k in range(len(sources)):
        copy(step, k).wait()
    return step % RING


def _ring_scratch(count, tile, width):
    return [pltpu.VMEM((RING, tile, width), F32)] * count + [pltpu.SemaphoreType.DMA((count, RING))]


def _ln_in_fwd_real(x, g, b, token):
    tr = min(LN_ROWS, SEQ)
    steps = SEQ // tr

    def body(x_hbm, g_ref, b_ref, token_ref, h_ref, x_ring, sems):
        slot = _ring_fetch(steps, tr, [x_hbm], [x_ring], sems)
        xhat, _ = _ln_stats(x_ring[slot])
        h_ref[...] = xhat * g_ref[...] + b_ref[...]

    return pl.pallas_call(
        body, name="ln_in_fwd", grid=(steps,),
        in_specs=[pl.BlockSpec(memory_space=pl.ANY), _const((1, D)), _const((1, D)), _const(TOKEN)],
        out_specs=_rows(tr, D),
        out_shape=pltpu.HBM((_lp(), D), F32),
        scratch_shapes=_ring_scratch(1, tr, D),
        compiler_params=_params(32, dimension_semantics=_seq()),
    )(*_hbm(x, g, b), token)


def _ln_in_fwd_meta(h_real, meta_ext, g, b):
    def meta_body(m_ref, g_ref, b_ref, real_ref, h_ref):
        xhat, _ = _ln_stats(m_ref[...])
        h_ref[...] = xhat * g_ref[...] + b_ref[...]

    return pl.pallas_call(
        meta_body, name="ln_in_fwd_meta", grid=(1,),
        in_specs=[_const((BLK, D)), _const((1, D)), _const((1, D)), pl.BlockSpec(memory_space=pl.ANY)],
        out_specs=pl.BlockSpec((BLK, D), lambda i: (SEQ // BLK, 0)),
        out_shape=pltpu.HBM((_lp(), D), F32),
        input_output_aliases={3: 0},
        compiler_params=_params(16, dimension_semantics=_seq()),
    )(*_hbm(meta_ext, g, b, h_real))


def _in_proj(h0, w_in_windows, b_in_p, wg2_p, bg2):
    tm = _row_tile(384)
    lp = _lp()
    widths = (512, 128, 128, 256, 256, 512, 512, 128)
    offs = (O_QS, O_KS, O_VS, O_QG, O_KG, O_VG, O_RG, O_LR)
    shard = SHARD_ROWS["w_in"]

    def body(h_ref, win_ref, b_ref, wg2_ref, bg2_ref, *outs):
        w_ref = outs[9]

        @pl.when(pl.program_id(0) == 0)
        def _():
            for s in range(N_CHIPS):
                w_ref[shard * s:shard * (s + 1), :] = win_ref[s, 0:shard, :]
            w_ref[D_IN:D_IN_P, :] = jnp.zeros((D_IN_P - D_IN, D), BF16)

        proj = _dot_nt(h_ref[...], w_ref[...]) + b_ref[...]
        for pos, h in enumerate(HEAD_POS):
            outs[0][:, pos * DH:(pos + 1) * DH] = proj[:, O_QS + h * DH:O_QS + (h + 1) * DH]
        for o_ref, off, wd in zip(outs[1:8], offs[1:], widths[1:]):
            o_ref[...] = proj[:, off:off + wd]
        outs[8][...] = _dot(proj[:, O_LR:O_LR + LANE], wg2_ref[...]) + bg2_ref[...]

    return pl.pallas_call(
        body, name="in_proj", grid=(lp // tm,),
        in_specs=[_rows(tm, D), _const(w_in_windows.shape), _const((1, D_IN_P)), _const((LANE, 256)), _const((1, 256))],
        out_specs=[_rows(tm, w) for w in widths] + [_rows(tm, 256), _acc((D_IN_P, D))],
        out_shape=[pltpu.HBM((lp, w), F32) for w in widths] + [pltpu.HBM((lp, 256), F32), pltpu.HBM((D_IN_P, D), BF16)],
        compiler_params=_params(48, dimension_semantics=_seq()),
    )(*_hbm(h0, w_in_windows, b_in_p, wg2_p, bg2))


def _swa_masks(n):
    nb = SEQ // BLK
    is_meta = n == nb
    ri = _iota((BLK, BLK), 0)
    cj = _iota((BLK, BLK), 1)
    meta_col = ((cj >= META_OFF) & (cj < CH)).astype(jnp.int32)
    meta_q = meta_col * ((cj <= ri) & (ri < CH)).astype(jnp.int32)
    valid_m = jnp.where(is_meta, meta_q, meta_col) > 0
    dist_m = jnp.where(is_meta, ri - cj, n * BLK + ri + CH - cj).astype(F32)
    valid_p = jnp.where((n >= 1) & (n < nb), (cj > ri).astype(jnp.int32), 0) > 0
    dist_p = (ri + BLK - cj).astype(F32)
    valid_c = jnp.where(n < nb, (cj <= ri).astype(jnp.int32), 0) > 0
    dist_c = (ri - cj).astype(F32)
    return (dist_m, dist_p, dist_c), (valid_m, valid_p, valid_c)


def _swa_bias(n):
    dists, valids = _swa_masks(n)
    return (jnp.concatenate([-d for d in dists], axis=1),
            jnp.concatenate([jnp.where(v, 0.0, NEG) for v in valids], axis=1))


def _swa_half(ref, pos, scale=1.0):
    col = ref[:, (pos // 2) * LANE:(pos // 2 + 1) * LANE]
    lane = _iota((BLK, LANE), 1)
    mine = lane < DH if pos % 2 == 0 else lane >= DH
    return jnp.where(mine, col * scale, 0.0).astype(BF16)


def _swa_merge(even, odd):
    return jnp.where(_iota((BLK, LANE), 1) < DH, even, odd)


def _swa_softmax(t, sink):
    m = jnp.maximum(jnp.max(t, axis=-1, keepdims=True), sink)
    e = jnp.exp(t - m)
    e_sink = jnp.exp(sink - m)
    inv = 1.0 / (jnp.sum(e, axis=-1, keepdims=True) + e_sink)
    return e * inv, e_sink * inv


def _swa_kv_specs(width):
    nb = SEQ // BLK
    return [pl.BlockSpec((BLK, width), lambda n: (nb, 0)),
            pl.BlockSpec((BLK, width), lambda n: (jnp.clip(n - 1, 0, nb - 1), 0)),
            pl.BlockSpec((BLK, width), lambda n: (jnp.minimum(n, nb), 0))]


def _swa_fwd(sinks, qs, ks, vs):
    nb = SEQ // BLK
    heads = range(SWA_HEADS)

    def body(sink_ref, q_ref, km_ref, kp_ref, kc_ref, vm_ref, vp_ref, vc_ref, o_ref):
        negdist, maskbias = _swa_bias(pl.program_id(0))
        k_all = jnp.concatenate([km_ref[...], kp_ref[...], kc_ref[...]], axis=0).astype(BF16)
        v_all = jnp.concatenate([vm_ref[...], vp_ref[...], vc_ref[...]], axis=0).astype(BF16)
        q = [_swa_half(q_ref, pos, DH ** -0.5) for pos in heads]
        t = [_dot_nt(q[pos], k_all) + (2.0 ** -(HEAD_POS[pos] + 1) * negdist + maskbias) for pos in heads]
        p = [_swa_softmax(t[pos], sink_ref[HEAD_POS[pos]])[0].astype(BF16) for pos in heads]
        o = [_dot(p[pos], v_all) for pos in heads]
        for col in range(SWA_HEADS // 2):
            o_ref[:, col * LANE:(col + 1) * LANE] = _swa_merge(o[2 * col], o[2 * col + 1])

    kvw = SWA_KV_HEADS * DH
    return pl.pallas_call(
        body, name="swa_fwd", grid=(nb + 1,),
        in_specs=[pl.BlockSpec(memory_space=pltpu.SMEM), _rows(BLK, SWA_HEADS * DH)] + _swa_kv_specs(kvw) + _swa_kv_specs(kvw),
        out_specs=_rows(BLK, SWA_HEADS * DH),
        out_shape=pltpu.HBM((_lp(), SWA_HEADS * DH), F32),
        compiler_params=_params(16, dimension_semantics=_seq()),
    )(sinks, *_hbm(qs, ks, ks, ks, vs, vs, vs))


GLA_PER_STEP = BLK // CH


def _gla_block(s):
    nb = SEQ // BLK
    return jnp.where(s == 0, nb, s - 1)


def _gla_rowmask(s):
    ri = _iota((BLK, 1), 0)
    m = jnp.where(s == 0, ((ri >= META_OFF) & (ri < CH)).astype(jnp.int32), 1)
    return (m > 0).astype(F32) + jnp.zeros((BLK, 1), F32)


def _gla_chunk_masks():
    r, c = _iota((BLK, BLK), 0), _iota((BLK, BLK), 1)
    same = ((r < CH) & (c < CH)) | ((r >= CH) & (c >= CH))
    return same & (r >= c), same & (r <= c), same


def _gla_decay(z, rmask):
    log_g = (jnp.minimum(z, 0.0) - jnp.log1p(jnp.exp(-jnp.abs(z)))) * (rmask / GLA_TAU)
    lower, _, same = _gla_chunk_masks()
    return _dot_exact(lower.astype(F32), log_g), _dot_exact(same.astype(F32), log_g)


def _gla_slices(c, h):
    return slice(c * CH, (c + 1) * CH), slice(h * DK, (h + 1) * DK), slice(h * DV, (h + 1) * DV)


def _gla_fwd(qg, kg, vg, z):
    steps = SEQ // BLK + 1
    kw, vw = GLA_HEADS * DK, GLA_HEADS * DV
    pairs = [(c, h) for c in range(GLA_PER_STEP) for h in range(GLA_HEADS)]

    def body(q_ref, k_ref, v_ref, z_ref, o_ref, st_ref, st):
        s = pl.program_id(0)

        @pl.when(s == 0)
        def _():
            st[...] = jnp.zeros_like(st)

        rmask = _gla_rowmask(s)
        b, b_last = _gla_decay(z_ref[...], rmask)
        q = q_ref[...] * (rmask * DK ** -0.5)
        k = k_ref[...] * rmask
        v = v_ref[...] * rmask
        qe = q * jnp.exp(b)
        ke = k * jnp.exp(-b)
        kd = k * jnp.exp(b_last - b)
        e_last = jnp.exp(b_last)
        causal = _iota((CH, CH), 0) >= _iota((CH, CH), 1)
        a, upd, intra = {}, {}, {}
        for c, h in pairs:
            rows, ks, vs_ = _gla_slices(c, h)
            a[c, h] = jnp.where(causal, _dot_nt(qe[rows, ks], ke[rows, ks]), 0.0)
            upd[c, h] = _dot_tn(v[rows, vs_], kd[rows, ks])
        for c, h in pairs:
            rows, ks, vs_ = _gla_slices(c, h)
            intra[c, h] = _dot(a[c, h], v[rows, vs_])
        state = st[...]
        for c in range(GLA_PER_STEP):
            st_ref[0, c] = state
            for h in range(GLA_HEADS):
                rows, ks, vs_ = _gla_slices(c, h)
                o_ref[rows, vs_] = intra[c, h] + _dot_nt(qe[rows, ks], state[:, ks])
            state = state * e_last[c * CH:c * CH + 1] + jnp.concatenate([upd[c, h] for h in range(GLA_HEADS)], axis=1)
        st[...] = state

    blk = lambda w: pl.BlockSpec((BLK, w), lambda s: (_gla_block(s), 0))
    return pl.pallas_call(
        body, name="gla_fwd", grid=(steps,),
        in_specs=[blk(kw), blk(kw), blk(vw), blk(kw)],
        out_specs=[blk(vw), pl.BlockSpec((1, GLA_PER_STEP, DV, kw), lambda s: (s, 0, 0, 0))],
        out_shape=[pltpu.HBM((_lp(), vw), F32), pltpu.HBM((steps, GLA_PER_STEP, DV, kw), F32)],
        scratch_shapes=[pltpu.VMEM((DV, kw), F32)],
        compiler_params=_params(16, dimension_semantics=_seq()),
    )(*_hbm(qg, kg, vg, z))


def _post_mix(o_s, o_gla, r_g, h0, gn4, w_out, g1, b1, token):
    tm = _row_tile(384)
    lp = _lp()

    def body(os_ref, og_ref, r_ref, h0_ref, gn_ref, w_ref, g_ref, b_ref, token_ref, o_ref, pre_ref, h1_ref):
        for pos, h in enumerate(HEAD_POS):
            o_ref[:, h * DH:(h + 1) * DH] = os_ref[:, pos * DH:(pos + 1) * DH].astype(BF16)
        for h in range(GLA_HEADS):
            hs = slice(h * DV, (h + 1) * DV)
            xg = og_ref[:, hs]
            n = xg * lax.rsqrt(jnp.mean(xg * xg, axis=-1, keepdims=True) + RMS_EPS) * gn_ref[...]
            r = r_ref[:, hs]
            o_ref[:, 512 + h * DV:512 + (h + 1) * DV] = (n * (r * _sigmoid(r))).astype(BF16)
        pre = ALPHA * h0_ref[...] + _dot(o_ref[...], w_ref[...])
        pre_ref[...] = pre
        xhat, _ = _ln_stats(pre)
        h1_ref[...] = xhat * g_ref[...] + b_ref[...]

    return pl.pallas_call(
        body, name="post_mix", grid=(lp // tm,),
        in_specs=[_rows(tm, 512), _rows(tm, 512), _rows(tm, 512), _rows(tm, D), _const((1, DV)), _const((D, D)),
                  _const((1, D)), _const((1, D)), _const(TOKEN)],
        out_specs=[_rows(tm, D), _rows(tm, D), _rows(tm, D)],
        out_shape=[pltpu.HBM((lp, D), BF16), pltpu.HBM((lp, D), F32),
                   pltpu.HBM((lp, D), F32)],
        compiler_params=_params(32, dimension_semantics=_seq()),
    )(*_hbm(o_s, o_gla, r_g, h0, gn4, w_out, g1, b1), token)


def _ffn_fwd_loss_bwd(h1, wg_t, wu_t, wd, target, g2, b2):
    lp = _lp()
    tm = max(t for t in range(BLK, 384 + 1, BLK) if lp % t == 0)
    steps = lp // tm
    last_blk = SEQ // BLK - 1
    half = D_FF // 2
    n_t = tm // BLK

    def body(*refs):
        h_ref, wg_ref, wu_ref, wd_ref = refs[:4]
        t_refs = refs[4:4 + n_t]
        g2_ref, b2_ref, a_ref, dgate_ref, dup_ref, dp_ref, loss_ref, dg_ref, db_ref, g_s, u_s, acc = refs[4 + n_t:]
        i = pl.program_id(0)

        @pl.when(i == 0)
        def _():
            acc[...] = jnp.zeros_like(acc)
            dg_ref[...] = jnp.zeros_like(dg_ref)
            db_ref[...] = jnp.zeros_like(db_ref)

        h = h_ref[...]
        hb = h.astype(BF16)
        pre = ALPHA * h
        for j in range(2):
            cols = slice(j * half, (j + 1) * half)
            g = _dot_nt(hb, wg_ref[cols, :])
            u = _dot_nt(hb, wu_ref[cols, :])
            g_s[:, cols] = g
            u_s[:, cols] = u
            pre = pre + _dot(g * _sigmoid(g) * u, wd_ref[cols, :])
        xhat, rstd = _ln_stats(pre)
        real = i * tm + _iota((tm, 1), 0) < SEQ
        target_rows = jnp.concatenate([t[...] for t in t_refs], axis=0)
        diff = jnp.where(real, xhat * g2_ref[...] + b2_ref[...] - target_rows, 0.0)
        acc[...] += jnp.sum(diff * diff, axis=0, keepdims=True)
        dy = diff * (1.0 / D)
        dpre = _ln_bwd(dy, xhat, rstd, g2_ref[...])
        dp_ref[...] = dpre
        dg_ref[...] += jnp.sum(dy * xhat, axis=0, keepdims=True)
        db_ref[...] += jnp.sum(dy, axis=0, keepdims=True)
        dpb = dpre.astype(BF16)
        for j in range(2):
            cols = slice(j * half, (j + 1) * half)
            g, u = g_s[:, cols], u_s[:, cols]
            sg = _sigmoid(g)
            silu = g * sg
            da = _dot_nt(dpb, wd_ref[cols, :])
            a_ref[:, cols] = (silu * u).astype(BF16)
            dgate_ref[:, cols] = (da * u * (sg * (1.0 + g * (1.0 - sg)))).astype(BF16)
            dup_ref[:, cols] = (da * silu).astype(BF16)

        @pl.when(i == steps - 1)
        def _():
            loss_ref[...] = jnp.zeros_like(loss_ref) + (0.5 / D) * jnp.sum(acc[...], axis=1, keepdims=True)

    t_spec = lambda k: pl.BlockSpec((BLK, D), lambda i: (jnp.minimum(i * n_t + k, last_blk), 0))
    return pl.pallas_call(
        body, name="ffn_fwd_loss_bwd", grid=(steps,),
        in_specs=[_rows(tm, D), _const((D_FF, D)), _const((D_FF, D)), _const((D_FF, D))]
        + [t_spec(k) for k in range(n_t)] + [_const((1, D)), _const((1, D))],
        out_specs=[_rows(tm, D_FF), _rows(tm, D_FF), _rows(tm, D_FF), _rows(tm, D), _acc((1, LANE)), _acc((1, D)),
                   _acc((1, D))],
        out_shape=[pltpu.HBM((lp, D_FF), BF16)] * 3 + [pltpu.HBM((lp, D), F32), pltpu.HBM((1, LANE), F32),
                                                         pltpu.HBM((1, D), F32), pltpu.HBM((1, D), F32)],
        scratch_shapes=[pltpu.VMEM((tm, D_FF), F32), pltpu.VMEM((tm, D_FF), F32), pltpu.VMEM((1, D), F32)],
        compiler_params=_params(58, dimension_semantics=_seq()),
    )(*_hbm(h1, wg_t, wu_t, wd, *[target] * n_t, g2, b2))


def _ffn_out_bwd(dpre2, dgate, dup, pre1, wg_t, wu_t, g1, w_out, o_gla, r_g, gn4):
    tm = _row_tile(384)
    lp = _lp()

    def body(dp_ref, dg_ref, du_ref, p1_ref, wg_ref, wu_ref, g1_ref, w_ref, og_ref, r_ref, gn_ref,
             dp1_ref, dg1_ref, db1_ref, dos_ref, dog_ref, dr_ref, dgn_ref):
        @pl.when(pl.program_id(0) == 0)
        def _():
            for acc_ref in (dg1_ref, db1_ref, dgn_ref):
                acc_ref[...] = jnp.zeros_like(acc_ref)

        dh1 = ALPHA * dp_ref[...] + _dot(dg_ref[...], wg_ref[...]) + _dot(du_ref[...], wu_ref[...])
        xhat, rstd1 = _ln_stats(p1_ref[...])
        dpre1 = _ln_bwd(dh1, xhat, rstd1, g1_ref[...])
        dp1_ref[...] = dpre1
        dg1_ref[...] += jnp.sum(dh1 * xhat, axis=0, keepdims=True)
        db1_ref[...] += jnp.sum(dh1, axis=0, keepdims=True)

        do = _dot_nt(dpre1, w_ref[...])
        for pos, h in enumerate(HEAD_POS):
            dos_ref[:, pos * DH:(pos + 1) * DH] = do[:, h * DH:(h + 1) * DH]
        gn = gn_ref[...]
        for h in range(GLA_HEADS):
            hs = slice(h * DV, (h + 1) * DV)
            xg = og_ref[:, hs]
            rstd = lax.rsqrt(jnp.mean(xg * xg, axis=-1, keepdims=True) + RMS_EPS)
            nx = xg * rstd
            r = r_ref[:, hs]
            sr = _sigmoid(r)
            d_o = do[:, 512 + h * DV:512 + (h + 1) * DV]
            dr_ref[:, hs] = d_o * (nx * gn) * (sr * (1.0 + r * (1.0 - sr)))
            dn = d_o * (r * sr)
            dgn_ref[...] += jnp.sum(dn * nx, axis=0, keepdims=True)
            dnx = dn * gn
            dog_ref[:, hs] = rstd * (dnx - nx * jnp.mean(dnx * nx, axis=-1, keepdims=True))

    return pl.pallas_call(
        body, name="ffn_out_bwd", grid=(lp // tm,),
        in_specs=[_rows(tm, D), _rows(tm, D_FF), _rows(tm, D_FF), _rows(tm, D), _const((D_FF, D)), _const((D_FF, D)),
                  _const((1, D)), _const((D, D)), _rows(tm, 512), _rows(tm, 512), _const((1, DV))],
        out_specs=[_rows(tm, D), _acc((1, D)), _acc((1, D)), _rows(tm, 512), _rows(tm, 512), _rows(tm, 512),
                   _acc((1, DV))],
        out_shape=[pltpu.HBM((lp, D), F32), pltpu.HBM((1, D), F32), pltpu.HBM((1, D), F32)]
        + [pltpu.HBM((lp, 512), F32)] * 3 + [pltpu.HBM((1, DV), F32)],
        compiler_params=_params(48, dimension_semantics=_seq()),
    )(*_hbm(dpre2, dgate, dup, pre1, wg_t, wu_t, g1, w_out, o_gla, r_g, gn4))


def _atb(a, b, name, token=None, windows=None):
    lp = _lp()
    tm = _row_tile(1408)
    n, w = a.shape[1], b.shape[1]
    bw = 512 if n * w * 4 > (4 << 20) else w
    tokens = [] if token is None else [token]
    steps = lp // tm

    def body(a_ref, b_ref, *rest):
        o_ref, acc_ref = rest[len(tokens):] if windows else (rest[-1], rest[-1])

        @pl.when(pl.program_id(1) == 0)
        def _():
            acc_ref[...] = jnp.zeros_like(acc_ref)

        acc_ref[...] += _dot_tn(a_ref[...], b_ref[...])

        if windows:
            @pl.when(pl.program_id(1) == steps - 1)
            def _():
                for s, start in enumerate(windows[0]):
                    o_ref[s] = acc_ref[start:start + windows[1], :]

    if windows:
        count, height = len(windows[0]), windows[1]
        out_spec, out_shape = pl.BlockSpec((count, height, bw), lambda j, k: (0, 0, j)), (count, height, w)
    else:
        out_spec, out_shape = pl.BlockSpec((n, bw), lambda j, k: (0, j)), (n, w)
    return pl.pallas_call(
        body, name=name, grid=(w // bw, steps),
        in_specs=[pl.BlockSpec((tm, n), lambda j, k: (k, 0)), pl.BlockSpec((tm, bw), lambda j, k: (k, j))]
        + [_const(TOKEN)] * len(tokens),
        out_specs=out_spec, out_shape=pltpu.HBM(out_shape, F32),
        scratch_shapes=[pltpu.VMEM((n, bw), F32)] if windows else [],
        compiler_params=_params(48, dimension_semantics=_seq(2)),
    )(*_hbm(a, b), *tokens)


def _gla_bwd(qg, kg, vg, z, do_gla, st_all, token):
    steps = SEQ // BLK + 1
    kw, vw = GLA_HEADS * DK, GLA_HEADS * DV
    pairs = [(c, h) for c in range(GLA_PER_STEP) for h in range(GLA_HEADS)]
    heads = range(GLA_HEADS)

    def body(q_ref, k_ref, v_ref, z_ref, do_ref, st_ref, token_ref, dq_ref, dk_ref, dv_ref, dz_ref, dst):
        @pl.when(pl.program_id(0) == 0)
        def _():
            dst[...] = jnp.zeros_like(dst)

        rmask = _gla_rowmask(steps - 1 - pl.program_id(0))
        zz = z_ref[...]
        b, b_last = _gla_decay(zz, rmask)
        e_b, e_nb, e_kd, e_last = jnp.exp(b), jnp.exp(-b), jnp.exp(b_last - b), jnp.exp(b_last)
        q = q_ref[...] * (rmask * DK ** -0.5)
        k = k_ref[...] * rmask
        v = v_ref[...] * rmask
        qe, ke, kd = q * e_b, k * e_nb, k * e_kd
        d_o = do_ref[...]
        causal = _iota((CH, CH), 0) >= _iota((CH, CH), 1)
        a, da, dqe, dke, dv_intra, carry = {}, {}, {}, {}, {}, {}
        for c, h in pairs:
            rows, ks, vs_ = _gla_slices(c, h)
            a[c, h] = jnp.where(causal, _dot_nt(qe[rows, ks], ke[rows, ks]), 0.0)
            da[c, h] = jnp.where(causal, _dot_nt(d_o[rows, vs_], v[rows, vs_]), 0.0)
            carry[c, h] = _dot_tn(d_o[rows, vs_], qe[rows, ks])
        for c, h in pairs:
            rows, ks, vs_ = _gla_slices(c, h)
            dqe[c, h] = _dot(d_o[rows, vs_], st_ref[0, c][:, ks]) + _dot(da[c, h], ke[rows, ks])
            dke[c, h] = _dot_tn(da[c, h], qe[rows, ks])
            dv_intra[c, h] = _dot_tn(a[c, h], d_o[rows, vs_])
        dstate = dst[...]
        dkd, db_decay = {}, {}
        for c in reversed(range(GLA_PER_STEP)):
            for h in heads:
                rows, ks, vs_ = _gla_slices(c, h)
                dkd[c, h] = _dot(v[rows, vs_], dstate[:, ks])
                dv_ref[rows, vs_] = dv_intra[c, h] + _dot_nt(kd[rows, ks], dstate[:, ks])
            chunk_last = e_last[c * CH:c * CH + 1]
            db_decay[c] = jnp.sum(dstate * st_ref[0, c], axis=0, keepdims=True) * chunk_last
            dstate = dstate * chunk_last + jnp.concatenate([carry[c, h] for h in heads], axis=1)
        dst[...] = dstate
        rows_of = lambda parts: jnp.concatenate(
            [jnp.concatenate([parts[c, h] for h in heads], axis=1) for c in range(GLA_PER_STEP)], axis=0)
        dqe_all, dke_all, dkd_all = rows_of(dqe), rows_of(dke), rows_of(dkd)
        dq_ref[...] = dqe_all * e_b * (rmask * DK ** -0.5)
        dk_ref[...] = (dke_all * e_nb + dkd_all * e_kd) * rmask
        dkd_kd = dkd_all * kd
        db = dqe_all * qe - dke_all * ke - dkd_kd
        _, upper, same = _gla_chunk_masks()
        decay_rows = jnp.concatenate([jnp.broadcast_to(db_decay[c], (CH, kw)) for c in range(GLA_PER_STEP)], axis=0)
        dlog_g = _dot_exact(upper.astype(F32), db) + _dot_exact(same.astype(F32), dkd_kd) + decay_rows
        dz_ref[...] = dlog_g * (rmask / GLA_TAU) * _sigmoid(-zz)

    blk = lambda w: pl.BlockSpec((BLK, w), lambda s: (_gla_block(steps - 1 - s), 0))
    return pl.pallas_call(
        body, name="gla_bwd", grid=(steps,),
        in_specs=[blk(kw), blk(kw), blk(vw), blk(kw), blk(vw),
                  pl.BlockSpec((1, GLA_PER_STEP, DV, kw), lambda s: (steps - 1 - s, 0, 0, 0)), _const(TOKEN)],
        out_specs=[blk(kw), blk(kw), blk(vw), blk(kw)],
        out_shape=[pltpu.HBM((_lp(), kw), F32), pltpu.HBM((_lp(), kw), F32),
                   pltpu.HBM((_lp(), vw), F32), pltpu.HBM((_lp(), kw), F32)],
        scratch_shapes=[pltpu.VMEM((DV, kw), F32)],
        compiler_params=_params(16, dimension_semantics=_seq()),
    )(*_hbm(qg, kg, vg, z, do_gla, st_all), token)


def _swa_bwd(sinks, qs, ks, vs, do_s, token):
    nb = SEQ // BLK
    kvw = SWA_KV_HEADS * DH
    scale = DH ** -0.5
    heads = range(SWA_HEADS)

    def body(sink_ref, q_ref, km_ref, kp_ref, kc_ref, vm_ref, vp_ref, vc_ref, do_ref, token_ref,
             dq_ref, dk_ref, dv_ref, dsink_ref, carry_k, carry_v, meta_k, meta_v):
        n = pl.program_id(0)

        @pl.when(n == 0)
        def _():
            for r in (carry_k, carry_v, meta_k, meta_v):
                r[...] = jnp.zeros_like(r)
            dsink_ref[...] = jnp.zeros_like(dsink_ref)

        @pl.when(n <= nb)
        def _():
            negdist, maskbias = _swa_bias(n)
            lane = _iota((1, LANE), 1)
            k_all = jnp.concatenate([km_ref[...], kp_ref[...], kc_ref[...]], axis=0).astype(BF16)
            v_all = jnp.concatenate([vm_ref[...], vp_ref[...], vc_ref[...]], axis=0).astype(BF16)
            q = [_swa_half(q_ref, pos, scale) for pos in heads]
            d_o = [_swa_half(do_ref, pos) for pos in heads]
            t = [_dot_nt(q[pos], k_all) + (2.0 ** -(HEAD_POS[pos] + 1) * negdist + maskbias) for pos in heads]
            dp = [_dot_nt(d_o[pos], v_all) for pos in heads]
            soft = [_swa_softmax(t[pos], sink_ref[HEAD_POS[pos]]) for pos in heads]
            p = [s[0] for s in soft]
            delta = [jnp.sum(p[pos] * dp[pos], axis=-1, keepdims=True) for pos in heads]
            ds = [(p[pos] * (dp[pos] - delta[pos])).astype(BF16) for pos in heads]
            dq = [_dot(ds[pos], k_all) for pos in heads]
            for col in range(SWA_HEADS // 2):
                dq_ref[:, col * LANE:(col + 1) * LANE] = scale * _swa_merge(dq[2 * col], dq[2 * col + 1])
            dsink = jnp.zeros((1, LANE), F32)
            for pos in heads:
                dsink = dsink + jnp.where(lane == HEAD_POS[pos],
                                          -jnp.sum(soft[pos][1] * delta[pos], axis=0, keepdims=True), 0.0)
            dsink_ref[...] += dsink
            dk3 = _dot_tn(jnp.concatenate(q, axis=0), jnp.concatenate(ds, axis=0)).T
            dv3 = _dot_tn(jnp.concatenate(d_o, axis=0), jnp.concatenate([x.astype(BF16) for x in p], axis=0)).T
            meta_k[...] += dk3[0:BLK]
            meta_v[...] += dv3[0:BLK]
            dk_ref[...] = carry_k[...] + dk3[BLK:2 * BLK]
            dv_ref[...] = carry_v[...] + dv3[BLK:2 * BLK]
            carry_k[...] = dk3[2 * BLK:3 * BLK]
            carry_v[...] = dv3[2 * BLK:3 * BLK]

        @pl.when(n == nb + 1)
        def _():
            dk_ref[...] = meta_k[...]
            dv_ref[...] = meta_v[...]

    kv_out = pl.BlockSpec((BLK, kvw), lambda n: (jnp.where(n == nb + 1, nb, jnp.clip(n - 1, 0, nb - 1)), 0))
    qblk = pl.BlockSpec((BLK, SWA_HEADS * DH), lambda n: (jnp.minimum(n, nb), 0))
    return pl.pallas_call(
        body, name="swa_bwd", grid=(nb + 2,),
        in_specs=[pl.BlockSpec(memory_space=pltpu.SMEM), qblk] + _swa_kv_specs(kvw) + _swa_kv_specs(kvw)
        + [qblk, _const(TOKEN)],
        out_specs=[qblk, kv_out, kv_out, _acc((1, LANE))],
        out_shape=[pltpu.HBM((_lp(), SWA_HEADS * DH), F32), pltpu.HBM((_lp(), kvw), F32),
                   pltpu.HBM((_lp(), kvw), F32), pltpu.HBM((1, LANE), F32)],
        scratch_shapes=[pltpu.VMEM((BLK, kvw), F32)] * 4,
        compiler_params=_params(16, dimension_semantics=_seq()),
    )(sinks, *_hbm(qs, ks, ks, ks, vs, vs, vs, do_s), token)


def _in_bwd(dqs, dks, dvs, dqg, dkg, dvg, drg, dz, dpre1, w_in_t, wg2_p):
    tm = _row_tile(384)
    lp = _lp()
    widths = (512, 128, 128, 256, 256, 512, 512)
    offs = (O_QS, O_KS, O_VS, O_QG, O_KG, O_VG, O_RG)

    def body(*refs):
        parts, (dz_ref, dp1_ref, w_ref, wg2_ref, dproj_ref, dh0_ref, dbin_ref, dbg_ref) = refs[:7], refs[7:]

        @pl.when(pl.program_id(0) == 0)
        def _():
            dbin_ref[...] = jnp.zeros_like(dbin_ref)
            dbg_ref[...] = jnp.zeros_like(dbg_ref)

        for pos, h in enumerate(HEAD_POS):
            val = parts[0][:, pos * DH:(pos + 1) * DH]
            dproj_ref[:, O_QS + h * DH:O_QS + (h + 1) * DH] = val.astype(BF16)
            dbin_ref[:, O_QS + h * DH:O_QS + (h + 1) * DH] += jnp.sum(val, axis=0, keepdims=True)
        for p_ref, off, wd in zip(parts[1:], offs[1:], widths[1:]):
            val = p_ref[...]
            dproj_ref[:, off:off + wd] = val.astype(BF16)
            dbin_ref[:, off:off + wd] += jnp.sum(val, axis=0, keepdims=True)
        dz = dz_ref[...]
        dlr = _dot_nt(dz, wg2_ref[...])
        dproj_ref[:, O_LR:O_LR + LANE] = dlr.astype(BF16)
        dbin_ref[:, O_LR:O_LR + LANE] += jnp.sum(dlr, axis=0, keepdims=True)
        dbg_ref[...] += jnp.sum(dz, axis=0, keepdims=True)
        dh0_ref[...] = ALPHA * dp1_ref[...] + _dot(dproj_ref[...], w_ref[...])

    return pl.pallas_call(
        body, name="in_bwd", grid=(lp // tm,),
        in_specs=[_rows(tm, w) for w in widths] + [_rows(tm, 256), _rows(tm, D), _const((D_IN_P, D)), _const((LANE, 256))],
        out_specs=[_rows(tm, D_IN_P), _rows(tm, D), _acc((1, D_IN_P)), _acc((1, 256))],
        out_shape=[pltpu.HBM((lp, D_IN_P), BF16), pltpu.HBM((lp, D), F32),
                   pltpu.HBM((1, D_IN_P), F32), pltpu.HBM((1, 256), F32)],
        compiler_params=_params(40, dimension_semantics=_seq()),
    )(*_hbm(dqs, dks, dvs, dqg, dkg, dvg, drg, dz, dpre1, w_in_t, wg2_p))


def _ln_in_bwd(x, meta_ext, dh0, g, token):
    tr = min(LN_ROWS, SEQ)
    steps = SEQ // tr

    def ln_bwd(x_ref, dh_ref, g_ref, dx_ref, dg_ref, db_ref, so_far=None):
        @pl.when(pl.program_id(0) == 0)
        def _():
            dg_ref[...] = jnp.zeros_like(dg_ref) if so_far is None else so_far[0][...]
            db_ref[...] = jnp.zeros_like(db_ref) if so_far is None else so_far[1][...]

        xhat, rstd = _ln_stats(x_ref[...])
        dh = dh_ref[...]
        dx_ref[...] = _ln_bwd(dh, xhat, rstd, g_ref[...])
        dg_ref[...] += jnp.sum(dh * xhat, axis=0, keepdims=True)
        db_ref[...] += jnp.sum(dh, axis=0, keepdims=True)

    def body(x_hbm, dh_hbm, g_ref, token_ref, dx_ref, dg_ref, db_ref, x_ring, dh_ring, sems):
        slot = _ring_fetch(steps, tr, [x_hbm, dh_hbm], [x_ring, dh_ring], sems)
        ln_bwd(x_ring.at[slot], dh_ring.at[slot], g_ref, dx_ref, dg_ref, db_ref)

    def meta_body(m_ref, dh_ref, g_ref, dg_real_ref, db_real_ref, dm_ref, dg_ref, db_ref):
        ln_bwd(m_ref, dh_ref, g_ref, dm_ref, dg_ref, db_ref, (dg_real_ref, db_real_ref))

    sums = [pltpu.HBM((1, D), F32), pltpu.HBM((1, D), F32)]
    dx, dg, db = pl.pallas_call(
        body, name="ln_in_bwd", grid=(steps,),
        in_specs=[pl.BlockSpec(memory_space=pl.ANY)] * 2 + [_const((1, D)), _const(TOKEN)],
        out_specs=[_rows(tr, D), _acc((1, D)), _acc((1, D))],
        out_shape=[pltpu.HBM((SEQ, D), F32)] + sums,
        scratch_shapes=_ring_scratch(2, tr, D),
        compiler_params=_params(32, dimension_semantics=_seq()),
    )(*_hbm(x, dh0, g), token)
    dm, dg, db = pl.pallas_call(
        meta_body, name="ln_in_bwd_meta", grid=(1,),
        in_specs=[_const((BLK, D)), pl.BlockSpec((BLK, D), lambda i: (SEQ // BLK, 0))] + [_const((1, D))] * 3,
        out_specs=[_acc((BLK, D)), _acc((1, D)), _acc((1, D))],
        out_shape=[pltpu.HBM((BLK, D), F32)] + sums,
        compiler_params=_params(16, dimension_semantics=_seq()),
    )(*_hbm(meta_ext, dh0, g, dg, db))
    return dx, dm, dg, db


def _local_step(x, target, ln_in_g, ln_in_b, b_in, bg2, sinks, gn, g1, b1, g2, b2,
                token, fetch_first, fetch_rest, fetch_ffn, exchange_ffn, ship_ffn, exchange_w_in, ship_w_in):
    row = lambda v: v.reshape(1, -1).astype(F32)
    b_in_p = jnp.pad(row(b_in), ((0, 0), (0, D_IN_P - D_IN)))
    gn4 = row(gn)
    sinks = sinks.reshape(-1).astype(F32)

    h_real = _ln_in_fwd_real(x, row(ln_in_g), row(ln_in_b), token)
    w_in_windows, meta_full, wg2 = fetch_first([h_real])
    meta_ext = jnp.pad(meta_full, ((META_OFF, BLK - CH), (0, 0)))
    wg2_p = jnp.pad(wg2, ((0, LANE - wg2.shape[0]), (0, 0))).astype(BF16)
    h0 = _ln_in_fwd_meta(h_real, meta_ext, row(ln_in_g), row(ln_in_b))
    qs, ks, vs, qg, kg, vg, rg, glr, z, w_in_t = _in_proj(h0, w_in_windows, b_in_p, wg2_p, row(bg2))
    o_s = _swa_fwd(sinks, qs, ks, vs)
    o_gla, st_all = _gla_fwd(qg, kg, vg, z)
    w_out, token = fetch_rest([o_s, o_gla])
    o, pre1, h1 = _post_mix(o_s, o_gla, rg, h0, gn4, w_out, row(g1), row(b1), token)
    wg_t, wu_t, wd = fetch_ffn([pre1])
    a, dgate, dup, dpre2, loss, dg2, db2 = _ffn_fwd_loss_bwd(h1, wg_t, wu_t, wd, target, row(g2), row(b2))
    dpre1, dg1, db1, do_s, do_gla, drg, dgn = _ffn_out_bwd(dpre2, dgate, dup, pre1, wg_t, wu_t, row(g1), w_out, o_gla,
                                                           rg, gn4)
    dwd = _atb(a, dpre2, "dw_down")
    dwg_t = _atb(dgate, h1, "dw_gate")
    dwu_t = _atb(dup, h1, "dw_up")
    token = exchange_ffn(dict(w_out=_atb(o, dpre1, "dw_out"), w_g=dwg_t, w_u=dwu_t, w_d=dwd))
    dqg, dkg, dvg, dz = _gla_bwd(qg, kg, vg, z, do_gla, st_all, token)
    token = ship_ffn([dqg])
    dqs, dks, dvs, dsinks = _swa_bwd(sinks, qs, ks, vs, do_s, token)
    dproj, dh0, db_in_p, dbg2 = _in_bwd(dqs, dks, dvs, dqg, dkg, dvg, drg, dz, dpre1, w_in_t, wg2_p)
    token = exchange_w_in(_atb(dproj, h0, "dw_in", windows=(W_IN_STARTS, W_IN_WIN)))
    dwg2_p = _atb(glr, dz, "dw_gate_lr2", token)
    token = ship_w_in([dwg2_p])
    dx, dmeta_blk, dg_in, db_in_ln = _ln_in_bwd(x, meta_ext, dh0, row(ln_in_g), token)

    small = dict(meta_blk=dmeta_blk, ln_in_g=dg_in, ln_in_b=db_in_ln, ln1_g=dg1, ln1_b=db1, ln2_g=dg2, ln2_b=db2,
                 b_in_p=db_in_p, wg2_p=dwg2_p, bg2=dbg2, sinks=dsinks, gn=dgn, loss=loss)
    return dx, small


HBM = pl.BlockSpec(memory_space=pltpu.HBM)


def _place():
    return lax.axis_index("x"), lax.axis_index("y"), lax.axis_index("c")


def _other_chips(x, y):
    return [(1 - x, y), (x, 1 - y), (1 - x, 1 - y)]


def _dma_sems(n):
    return pltpu.SemaphoreType.DMA((n,))


def _comm_params():
    return pltpu.CompilerParams(has_side_effects=True)


SEM = pl.BlockSpec(memory_space=pltpu.SEMAPHORE)


PER_ARRAY = dict(gather=3, scatter=3, sibling=N_CHIPS)


def _ici_copies(kind, landing, srcs, lands, send_sems, recv_sems):
    x, y, c = _place()
    mine = 2 * x + y
    copies = []
    for a in range(len(srcs)):
        if kind == "sibling":
            for s in range(N_CHIPS):
                copies.append(pltpu.make_async_remote_copy(
                    srcs[a].at[s, 1 - c], lands[a].at[s], send_sems.at[N_CHIPS * a + s], recv_sems.at[N_CHIPS * a + s],
                    device_id=(x, y, 1 - c), device_id_type=MESH))
            continue
        for j, (px, py) in enumerate(_other_chips(x, y)):
            slab = 2 * px + py if landing else mine
            if kind == "gather":
                src, dst = srcs[a].at[c], lands[a].at[slab, c]
            else:
                src, dst = srcs[a].at[2 * px + py], lands[a].at[slab]
            copies.append(pltpu.make_async_remote_copy(src, dst, send_sems.at[3 * a + j], recv_sems.at[3 * a + j],
                                                       device_id=(px, py, c), device_id_type=MESH))
    return copies


def _split_params():
    return pltpu.CompilerParams(has_side_effects=pltpu.SideEffectType.DATAFLOW_SIDE_EFFECTING)


def _ici_start(kind, srcs, land_shapes, after, name):
    n = len(srcs)
    lands = [pltpu.with_memory_space_constraint(lax.empty(s, a.dtype), pltpu.HBM) for s, a in zip(land_shapes, srcs)]

    def body(*refs):
        outs = refs[2 * n + len(after):]
        for cp in _ici_copies(kind, False, refs[:n], refs[n:2 * n], outs[0], outs[1]):
            cp.start()
        outs[-1][...] = jnp.zeros(TOKEN, F32)

    outs = pl.pallas_call(
        body, name=name, in_specs=[HBM] * (2 * n) + [pl.BlockSpec(memory_space=pl.ANY)] * len(after),
        out_specs=[SEM, SEM] + [HBM] * (2 * n) + [pl.BlockSpec(memory_space=pltpu.VMEM)],
        out_shape=[_dma_sems(PER_ARRAY[kind] * n)] * 2 + [pltpu.HBM(a.shape, a.dtype) for a in list(srcs) + lands]
        + [jax.ShapeDtypeStruct(TOKEN, F32)],
        input_output_aliases={i: 2 + i for i in range(2 * n)},
        compiler_params=_split_params(),
    )(*_hbm(*srcs), *lands, *after)
    return outs[:-1], outs[-1]


def _ici_wait(kind, handle, after, name):
    n = (len(handle) - 2) // 2

    def body(*refs):
        for cp in _ici_copies(kind, True, refs[:n], refs[n:2 * n], refs[2 * n], refs[2 * n + 1]):
            cp.wait_send()
            cp.wait_recv()

    outs = pl.pallas_call(
        body, name=name, in_specs=[HBM] * (2 * n) + [SEM, SEM] + [pl.BlockSpec(memory_space=pl.ANY)] * len(after),
        out_specs=[HBM] * (2 * n), out_shape=[pltpu.HBM(a.shape, a.dtype) for a in handle[2:]],
        input_output_aliases={i: i for i in range(2 * n)},
        compiler_params=_split_params(),
    )(*handle[2:], handle[0], handle[1], *after)
    return list(outs[:n]), list(outs[n:])


def _forward_copies(landing, arrs, send_sems, recv_sems):
    x, y, c = _place()
    copies = []
    for a in range(len(arrs)):
        for j, (px, py) in enumerate(_other_chips(x, y)):
            half = 1 - c if landing else c
            copies.append(pltpu.make_async_remote_copy(
                arrs[a].at[2 * px + py, c], arrs[a].at[2 * px + py, half], send_sems.at[3 * a + j],
                recv_sems.at[3 * a + j], device_id=(x, y, 1 - c), device_id_type=MESH))
    return copies


def _gather_wait_forward(handle, groups, after, name):
    n = (len(handle) - 2) // 2
    assert sum(groups) == n

    def body(*refs):
        outs = refs[2 * n + 2 + len(after):]
        lands, sems = outs[n:2 * n], outs[2 * n:-1]
        arrivals = _ici_copies("gather", True, refs[:n], refs[n:2 * n], refs[2 * n], refs[2 * n + 1])
        sends, first = [], 0
        for g, count in enumerate(groups):
            sends += _forward_copies(False, lands[first:first + count], sems[2 * g], sems[2 * g + 1])
            first += count
        for cp, send in zip(arrivals, sends):
            cp.wait_recv()
            send.start()
        for cp in arrivals:
            cp.wait_send()
        outs[-1][...] = jnp.zeros(TOKEN, F32)

    outs = pl.pallas_call(
        body, name=name, in_specs=[HBM] * (2 * n) + [SEM, SEM] + [pl.BlockSpec(memory_space=pl.ANY)] * len(after),
        out_specs=[HBM] * (2 * n) + [SEM] * (2 * len(groups)) + [pl.BlockSpec(memory_space=pltpu.VMEM)],
        out_shape=[pltpu.HBM(a.shape, a.dtype) for a in handle[2:]]
        + [_dma_sems(3 * count) for count in groups for _ in range(2)] + [jax.ShapeDtypeStruct(TOKEN, F32)],
        input_output_aliases={i: i for i in range(2 * n)},
        compiler_params=_split_params(),
    )(*handle[2:], handle[0], handle[1], *after)
    lands, sems, handles, first = outs[n:2 * n], outs[2 * n:-1], [], 0
    for g, count in enumerate(groups):
        handles.append([sems[2 * g], sems[2 * g + 1], *lands[first:first + count]])
        first += count
    return list(outs[:n]), handles, outs[-1]


def _forward_wait(handle, after, name):
    n = len(handle) - 2

    def body(*refs):
        for cp in _forward_copies(True, refs[:n], refs[n], refs[n + 1]):
            cp.wait_send()
            cp.wait_recv()

    return list(pl.pallas_call(
        body, name=name, in_specs=[HBM] * n + [SEM, SEM] + [pl.BlockSpec(memory_space=pl.ANY)] * len(after),
        out_specs=[HBM] * n, out_shape=[pltpu.HBM(a.shape, a.dtype) for a in handle[2:]],
        input_output_aliases={i: i for i in range(n)},
        compiler_params=_split_params(),
    )(*handle[2:], handle[0], handle[1], *after))


def _add_halves(core, grads, recvs, dtypes, name):
    n = len(grads)
    heights = [g.shape[2] for g in grads]

    def body(c_ref, *refs):
        for a in range(n):
            refs[2 * n + a][...] = (refs[2 * a][0] + refs[2 * a + 1][...]).astype(dtypes[a])

    slab = lambda h: pl.BlockSpec((1, h, D), lambda s, c: (s, 0, 0))
    mine = lambda h: pl.BlockSpec((1, 1, h, D), lambda s, c: (s, c[0], 0, 0))
    return pl.pallas_call(
        body, name=name,
        grid_spec=pltpu.PrefetchScalarGridSpec(
            num_scalar_prefetch=1, grid=(N_CHIPS,),
            in_specs=[spec(h) for h in heights for spec in (mine, slab)], out_specs=[slab(h) for h in heights]),
        out_shape=[pltpu.HBM((N_CHIPS, h, D), dt) for h, dt in zip(heights, dtypes)],
        compiler_params=_params(32, dimension_semantics=_seq()),
    )(core, *_hbm(*[a for pair in zip(grads, recvs) for a in pair]))


N_DEVICES = 2 * N_CHIPS
PEER_FLIPS = [(dx, dy, dc) for dx in (0, 1) for dy in (0, 1) for dc in (0, 1)][1:]


def _small_copies(landing, p_ref, out_ref, send_sems, recv_sems):
    x, y, c = _place()
    flip = lambda v, d: 1 - v if d else v
    copies = []
    for k, flips in enumerate(PEER_FLIPS):
        px, py, pc = (flip(v, d) for v, d in zip((x, y, c), flips))
        slab = 4 * px + 2 * py + pc if landing else 4 * x + 2 * y + c
        copies.append(pltpu.make_async_remote_copy(p_ref, out_ref.at[slab], send_sems.at[k], recv_sems.at[k],
                                                   device_id=(px, py, pc), device_id_type=MESH))
    return copies


def _small_wait(handle, after):
    def body(p_ref, land_ref, send_sems, recv_sems, *rest):
        for cp in _small_copies(True, p_ref, land_ref, send_sems, recv_sems):
            cp.wait_send()
            cp.wait_recv()

    return pl.pallas_call(
        body, name="small_exchange_wait", in_specs=[HBM, HBM, SEM, SEM] + [pl.BlockSpec(memory_space=pl.ANY)] * len(after),
        out_specs=[HBM, HBM], out_shape=[pltpu.HBM(a.shape, F32) for a in handle[2:]],
        input_output_aliases={0: 0, 1: 1},
        compiler_params=_split_params(),
    )(handle[2], handle[3], handle[0], handle[1], *after)


def _sum_chips(slots, firsts, rests, after, name):
    n = len(firsts)

    def body(i_ref, *refs):
        outs = refs[4 * n + len(after):]
        for a in range(n):
            first, r1, r2, r3 = refs[4 * a:4 * a + 4]
            outs[a][...] = ((first[...].astype(F32) + r1[...].astype(F32)) + r2[...].astype(F32)) + r3[...].astype(F32)

    slab = lambda h, k: pl.BlockSpec((1, h, D), lambda i, ix: (ix[k], 0, 0))
    heights = [f.shape[1] for f in firsts]
    return pl.pallas_call(
        body, name=name,
        grid_spec=pltpu.PrefetchScalarGridSpec(
            num_scalar_prefetch=1, grid=(1,),
            in_specs=[slab(h, k) for h in heights for k in range(4)] + [pl.BlockSpec(memory_space=pl.ANY)] * len(after),
            out_specs=[slab(h, 4) for h in heights]),
        out_shape=[pltpu.HBM((2, h, D), F32) for h in heights],
        compiler_params=_params(48, dimension_semantics=_seq()),
    )(slots, *_hbm(*[a for f, r in zip(firsts, rests) for a in (f, r, r, r)]), *after)


def _join_copies(landing, arrs, send_sems, recv_sems):
    x, y, c = _place()
    slab = 1 - c if landing else c
    return [pltpu.make_async_remote_copy(arr.at[slab], arr.at[slab], send_sems.at[a], recv_sems.at[a],
                                         device_id=(x, y, 1 - c), device_id_type=MESH) for a, arr in enumerate(arrs)]


def _join_halves(halves, name):
    n = len(halves)

    def body(*refs):
        outs = refs[n:2 * n]
        send_sems, recv_sems = refs[2 * n:]
        sends = _join_copies(False, outs, send_sems, recv_sems)
        for cp in sends:
            cp.start()
        for cp in _join_copies(True, outs, send_sems, recv_sems):
            cp.wait_recv()
        for cp in sends:
            cp.wait_send()

    return list(pl.pallas_call(
        body, name=name, in_specs=[HBM] * n, out_specs=[HBM] * n,
        out_shape=[pltpu.HBM(h.shape, F32) for h in halves],
        input_output_aliases={a: a for a in range(n)},
        scratch_shapes=[_dma_sems(n)] * 2,
        compiler_params=_comm_params(),
    )(*_hbm(*halves)))


def _join_small_start(halves, pack, name):
    n, peers = len(halves), len(PEER_FLIPS)
    land = pltpu.with_memory_space_constraint(lax.empty((N_DEVICES,) + pack.shape, F32), pltpu.HBM)

    def body(*refs):
        outs = refs[n + 2:]
        for cp in _join_copies(False, refs[:n], outs[0], outs[1]):
            cp.start()
        for cp in _small_copies(False, refs[n], refs[n + 1], outs[2], outs[3]):
            cp.start()
        outs[-1][...] = jnp.zeros(TOKEN, F32)

    outs = pl.pallas_call(
        body, name=name, in_specs=[HBM] * (n + 2),
        out_specs=[SEM] * 4 + [HBM] * (n + 2) + [pl.BlockSpec(memory_space=pltpu.VMEM)],
        out_shape=[_dma_sems(n)] * 2 + [_dma_sems(peers)] * 2
        + [pltpu.HBM(a.shape, a.dtype) for a in list(halves) + [pack, land]] + [jax.ShapeDtypeStruct(TOKEN, F32)],
        input_output_aliases={i: 4 + i for i in range(n + 2)},
        compiler_params=_split_params(),
    )(*_hbm(*halves, pack), land)
    return [outs[0], outs[1], *outs[4:4 + n]], [outs[2], outs[3], outs[4 + n], outs[5 + n]], outs[-1]


def _join_wait(handle, after, name):
    n = len(handle) - 2

    def body(*refs):
        for cp in _join_copies(True, refs[:n], refs[n], refs[n + 1]):
            cp.wait_send()
            cp.wait_recv()

    return list(pl.pallas_call(
        body, name=name, in_specs=[HBM] * n + [SEM, SEM] + [pl.BlockSpec(memory_space=pl.ANY)] * len(after),
        out_specs=[HBM] * n, out_shape=[pltpu.HBM(a.shape, a.dtype) for a in handle[2:]],
        input_output_aliases={i: i for i in range(n)},
        compiler_params=_split_params(),
    )(*handle[2:], handle[0], handle[1], *after))


def _chip_partials(grads, fetched, wire_dtypes, name):
    core = lax.axis_index("c").astype(jnp.int32).reshape(1)
    return list(_add_halves(core, grads, fetched, wire_dtypes, name))


def _chip_sums(parts, got, after, name):
    x, y, c = _place()
    others = [2 * px + py for px, py in _other_chips(x, y)]
    own_first = jnp.stack([2 * x + y] + others + [c]).astype(jnp.int32)
    return list(_sum_chips(own_first, parts, got, after, name))


ADAMW_STEPS = 8


def _adamw(params, by_row, chip, window_step):
    n = len(params)
    rows, _, cols = by_row[0].shape
    block = lambda shape: pl.BlockSpec((shape[0] // ADAMW_STEPS, shape[1]), lambda i, c: (i, 0))
    assert all(a.shape[0] % (8 * ADAMW_STEPS) == 0 for p in params for a in p)

    def body(c_ref, *refs):
        w_hbm, g_ref, m_hbm, v_hbm = refs[4 * n:4 * n + 4]
        results, (ins_ref, outs_ref, sems) = refs[8 * n + 4:8 * n + 8], refs[8 * n + 8:]
        loads = [pltpu.make_async_copy(src.at[:, 0, :], ins_ref.at[k], sems.at[k])
                 for k, src in enumerate((w_hbm, m_hbm, v_hbm))]
        stores = [pltpu.make_async_copy(outs_ref.at[k], dst.at[:, 0, :], sems.at[3 + k]) for k, dst in enumerate(results)]
        first = pl.program_id(0) == 0

        @pl.when(first)
        def _():
            for cp in loads:
                cp.start()

        for a in range(n):
            w_ref, a_g_ref, m_ref, v_ref = refs[4 * a:4 * a + 4]
            outs = refs[4 * n + 4 + 4 * a:4 * n + 8 + 4 * a]
            g = a_g_ref[...]
            outs[0][...] = g
            outs[1][...], outs[2][...], outs[3][...] = _adamw_math(w_ref[...], g, m_ref[...], v_ref[...])

        @pl.when(first)
        def _():
            for cp in loads:
                cp.wait()
            for lo in range(0, cols, LANE):
                lanes = slice(lo, lo + LANE)
                g = g_ref[0:rows, lanes]
                for s in range(1, N_CHIPS):
                    g = jnp.where(c_ref[0] == s, g_ref[s * window_step:s * window_step + rows, lanes], g)
                outs_ref[0, :, lanes] = g
                outs_ref[1, :, lanes], outs_ref[2, :, lanes], outs_ref[3, :, lanes] = _adamw_math(
                    ins_ref[0, :, lanes], g, ins_ref[1, :, lanes], ins_ref[2, :, lanes])
            for cp in stores:
                cp.start()

        @pl.when(pl.program_id(0) == ADAMW_STEPS - 1)
        def _():
            for cp in stores:
                cp.wait()

    outs = pl.pallas_call(
        body, name="adamw_matrices",
        grid_spec=pltpu.PrefetchScalarGridSpec(
            num_scalar_prefetch=1, grid=(ADAMW_STEPS,),
            in_specs=[block(a.shape) for p in params for a in p] + [HBM, _const(by_row[1].shape), HBM, HBM],
            out_specs=[block(p[0].shape) for p in params for _ in range(4)] + [HBM] * 4,
            scratch_shapes=[pltpu.VMEM((3, rows, cols), F32), pltpu.VMEM((4, rows, cols), F32), _dma_sems(7)]),
        out_shape=[pltpu.HBM(p[0].shape, F32) for p in params for _ in range(4)] + [pltpu.HBM((rows, 1, cols), F32)] * 4,
        compiler_params=_params(48, dimension_semantics=_seq()),
    )(chip, *_hbm(*[a for p in params for a in p], *by_row))
    return [outs[4 * a:4 * a + 4] for a in range(n)], outs[4 * n:]


def _adamw_math(w, g, m, v):
    nm = ADAM_B1 * m + (1.0 - ADAM_B1) * g
    nv = ADAM_B2 * v + (1.0 - ADAM_B2) * (g * g)
    m_hat = nm / (1.0 - ADAM_B1 ** ADAM_STEP)
    v_hat = nv / (1.0 - ADAM_B2 ** ADAM_STEP)
    return -ADAM_LR * (m_hat / (jnp.sqrt(v_hat) + ADAM_EPS) + ADAM_WD * w), nm, nv


SMALL = (("meta_tokens", (N_META, D // N_CHIPS)), ("ln_in_g", (1, D)), ("ln_in_b", (1, D)), ("b_in", (1, D_IN)),
         ("w_gate_lr2", (GATE_RANK, GLA_HEADS * DK // N_CHIPS)), ("b_gate_lr2", (1, GLA_HEADS * DK)),
         ("attn_sinks", (1, SWA_HEADS)),
         ("gla_norm_g", (1, DV)), ("ln1_g", (1, D)), ("ln1_b", (1, D)), ("ln2_g", (1, D)), ("ln2_b", (1, D)))
ROW_META, ROW_B_IN, ROW_TAIL, ROW_WG2 = 0, 22, 25, 32
ROW_LN = dict(ln_in_g=16, ln_in_b=17, ln1_g=18, ln1_b=19, ln2_g=20, ln2_b=21)
TAIL_BG2, TAIL_SINKS, TAIL_GN, TAIL_LOSS = 0, 256, 256 + SWA_HEADS, 256 + SWA_HEADS + DV


def _adamw_small(place, packs, own, params):
    n = len(SMALL)

    def body(place_ref, packs_ref, own_ref, *refs):
        ins, outs, p_ref = refs[:3 * n], refs[3 * n:-1], refs[-1]
        me, c = place_ref[0], place_ref[1]
        total = jnp.where(me == 0, own_ref[...], packs_ref[0])
        for i in range(1, N_DEVICES):
            total = total + jnp.where(me == i, own_ref[...], packs_ref[i])
        p_ref[...] = total
        outs[4 * n][...] = total[ROW_TAIL:ROW_TAIL + 1, TAIL_LOSS:TAIL_LOSS + 1]

        def mine(width, rows):
            part = lambda s: p_ref[rows, s * width:(s + 1) * width]
            return jnp.where(c == 0, part(0), jnp.where(c == 1, part(1), jnp.where(c == 2, part(2), part(3))))

        tail = lambda lo, width: p_ref[ROW_TAIL:ROW_TAIL + 1, lo:lo + width]
        grads = dict(
            meta_tokens=mine(D // N_CHIPS, slice(ROW_META, ROW_META + N_META)),
            b_in=jnp.concatenate([p_ref[ROW_B_IN:ROW_B_IN + 1, :], p_ref[ROW_B_IN + 1:ROW_B_IN + 2, :],
                                  p_ref[ROW_B_IN + 2:ROW_B_IN + 3, 0:D_IN - 2 * D]], axis=1),
            w_gate_lr2=mine(256 // N_CHIPS, slice(ROW_WG2, ROW_WG2 + 16)),
            b_gate_lr2=tail(TAIL_BG2, 256), attn_sinks=tail(TAIL_SINKS, SWA_HEADS), gla_norm_g=tail(TAIL_GN, DV),
            **{k: p_ref[r:r + 1, :] for k, r in ROW_LN.items()})
        for i, (name, _) in enumerate(SMALL):
            g = grads[name]
            outs[4 * i][...] = g
            outs[4 * i + 1][...], outs[4 * i + 2][...], outs[4 * i + 3][...] = _adamw_math(
                ins[3 * i][...], g, ins[3 * i + 1][...], ins[3 * i + 2][...])

    whole = lambda shape: pl.BlockSpec(shape, lambda i, c: (0,) * len(shape))
    outs = pl.pallas_call(
        body, name="adamw_small",
        grid_spec=pltpu.PrefetchScalarGridSpec(
            num_scalar_prefetch=1, grid=(1,),
            in_specs=[whole(packs.shape), whole(own.shape)] + [whole(s) for _, s in SMALL for _ in range(3)],
            out_specs=[whole(s) for _, s in SMALL for _ in range(4)] + [whole((1, 1))],
            scratch_shapes=[pltpu.VMEM(own.shape, F32)]),
        out_shape=[pltpu.HBM(s, F32) for _, s in SMALL for _ in range(4)] + [pltpu.HBM((1, 1), F32)],
        compiler_params=_params(16, dimension_semantics=_seq()),
    )(place, *_hbm(packs, own, *[a for p in params for a in p]))
    return [outs[4 * i:4 * i + 4] for i in range(n)], outs[4 * n]


def _small_pack(gr):
    names = ["meta_blk"] + list(ROW_LN) + ["b_in_p", "wg2_p", "bg2", "sinks", "gn", "loss"]
    gate_w = GLA_HEADS * DK

    def body(*refs):
        src, out = dict(zip(names, refs)), refs[-1]
        out[...] = jnp.zeros_like(out)
        out[ROW_META:ROW_META + N_META, :] = src["meta_blk"][META_OFF:CH, :]
        for k, r in ROW_LN.items():
            out[r:r + 1, :] = src[k][...]
        for j in range(-(-D_IN // D)):
            width = min(D, D_IN - j * D)
            out[ROW_B_IN + j:ROW_B_IN + j + 1, 0:width] = src["b_in_p"][:, j * D:j * D + width]
        tail = slice(ROW_TAIL, ROW_TAIL + 1)
        out[tail, TAIL_BG2:TAIL_BG2 + gate_w] = src["bg2"][...]
        out[tail, TAIL_SINKS:TAIL_SINKS + SWA_HEADS] = src["sinks"][:, 0:SWA_HEADS]
        out[tail, TAIL_GN:TAIL_GN + DV] = src["gn"][...]
        out[tail, TAIL_LOSS:TAIL_LOSS + 1] = src["loss"][:, 0:1]
        out[ROW_WG2:ROW_WG2 + GATE_RANK, 0:gate_w] = src["wg2_p"][0:GATE_RANK, :]

    arrays = [gr[k] for k in names]
    return pl.pallas_call(
        body, name="small_pack", grid=(1,),
        in_specs=[_acc(a.shape) for a in arrays], out_specs=_acc((SMALL_ROWS, D)),
        out_shape=pltpu.HBM((SMALL_ROWS, D), F32),
        compiler_params=_params(16, dimension_semantics=_seq()),
    )(*_hbm(*arrays))


BIG = ("w_in", "w_out", "w_g", "w_u", "w_d")


def kernel(x, meta_tokens, ln_in_g, ln_in_b, w_in, b_in, w_gate_lr2, b_gate_lr2, attn_sinks, gla_norm_g, w_out, ln1_g, ln1_b, w_ffn_gate, w_ffn_up, w_ffn_down, ln2_g, ln2_b, loss_target, m_meta_tokens, m_ln_in_g, m_ln_in_b, m_w_in, m_b_in, m_w_gate_lr2, m_b_gate_lr2, m_attn_sinks, m_gla_norm_g, m_w_out, m_ln1_g, m_ln1_b, m_w_ffn_gate, m_w_ffn_up, m_w_ffn_down, m_ln2_g, m_ln2_b, v_meta_tokens, v_ln_in_g, v_ln_in_b, v_w_in, v_b_in, v_w_gate_lr2, v_b_gate_lr2, v_attn_sinks, v_gla_norm_g, v_w_out, v_ln1_g, v_ln1_b, v_w_ffn_gate, v_w_ffn_up, v_w_ffn_down, v_ln2_g, v_ln2_b):
    chip = 2 * lax.axis_index("x") + lax.axis_index("y")

    halves = lambda a: a.reshape(2, a.shape[0] // 2, a.shape[1])
    r_in = SHARD_ROWS["w_in"]
    first = [halves(a) for a in (jnp.pad(w_in[0].T.astype(BF16), ((0, W_IN_WIN - r_in), (0, 0))), meta_tokens,
                                 w_gate_lr2[0])]
    rest = [halves(a) for a in (w_out[0].astype(BF16), w_ffn_gate[0].T.astype(BF16), w_ffn_up[0].T.astype(BF16),
                                w_ffn_down[0].astype(BF16))]
    lands = lambda arrs: [(N_CHIPS,) + a.shape for a in arrs]
    first_handle, first_token = _ici_start("gather", first, lands(first), [], "gather_first_start")
    rest_handle, token = _ici_start("gather", rest, lands(rest), [first_token], "gather_rest_start")
    own_slab = lambda got, shards: [lax.dynamic_update_index_in_dim(g, s, chip, axis=0) for g, s in zip(got, shards)]
    fetching = {}

    def fetch_first(after):
        shards, (forwarding,), _ = _gather_wait_forward(first_handle, [len(first)], after, "gather_first_wait")
        g_in, g_meta, g_wg2 = own_slab(_forward_wait(forwarding, [], "gather_first_forward_wait"), shards)
        w_in_windows = g_in.reshape(N_CHIPS, W_IN_WIN, D)
        meta_full = jnp.concatenate([g_meta[s].reshape(N_META, -1) for s in range(N_CHIPS)], axis=1)
        wg2_full = jnp.concatenate([g_wg2[s].reshape(w_gate_lr2.shape[1], -1) for s in range(N_CHIPS)], axis=1)
        return w_in_windows, meta_full, wg2_full

    def fetch_rest(after):
        shards, (w_out_forwarding, fetching["handle"]), forward_token = _gather_wait_forward(
            rest_handle, [1, len(rest) - 1], after, "gather_rest_wait")
        g_out, = own_slab(_forward_wait(w_out_forwarding, [], "gather_w_out_forward_wait"), shards[:1])
        fetching["shards"] = shards[1:]
        return g_out.reshape(-1, D), forward_token

    def fetch_ffn(after):
        got = _forward_wait(fetching["handle"], after, "gather_ffn_forward_wait")
        return [g.reshape(-1, D) for g in own_slab(got, fetching["shards"])]

    sent = {}
    split = lambda grads: [g.reshape(N_CHIPS, 2, -1, D) for g in grads]

    def exchange(key, grads):
        grads = split(grads)
        sent[key + "_halves"], exchange_token = _ici_start(
            "sibling", grads, [(N_CHIPS,) + a.shape[2:] for a in grads], [], "sibling_" + key + "_start")
        return exchange_token

    def ship(key, after):
        grads, fetched = _ici_wait("sibling", sent[key + "_halves"], after, "sibling_" + key + "_wait")
        parts = _chip_partials(grads, fetched, [BF16] * len(grads), "add_halves_" + key)
        sent[key], ship_token = _ici_start("scatter", parts, [p.shape for p in parts], [], "scatter_" + key + "_start")
        return ship_token

    dx, gr = _local_step(
        x[0], loss_target[0], ln_in_g, ln_in_b, b_in[0], b_gate_lr2[0], attn_sinks[0], gla_norm_g[0], ln1_g[0],
        ln1_b[0], ln2_g[0], ln2_b[0], token, fetch_first, fetch_rest, fetch_ffn,
        lambda g: exchange("ffn", [g[k] for k in BIG[1:]]), lambda after: ship("ffn", after),
        lambda g: exchange("w_in", [g]), lambda after: ship("w_in", after))
    ffn_parts, ffn_got = _ici_wait("scatter", sent["ffn"], [dx], "scatter_ffn_wait")
    join_handle, small_handle, token = _join_small_start(
        _chip_sums(ffn_parts, ffn_got, [], "sum_chips_ffn"), _small_pack(gr), "join_ffn_small_start")
    w_in_parts, w_in_got = _ici_wait("scatter", sent["w_in"], [token], "scatter_w_in_wait")
    w_in_joined = _join_halves(_chip_sums(w_in_parts, w_in_got, [], "sum_chips_w_in"), "join_w_in")
    red = [f.reshape(2 * f.shape[1], D) for f in w_in_joined + _join_wait(join_handle, w_in_joined, "join_ffn_wait")]

    big_g = dict(zip(BIG, red))
    weights = dict(meta_tokens=meta_tokens, ln_in_g=ln_in_g, ln_in_b=ln_in_b, w_in=w_in, b_in=b_in,
                   w_gate_lr2=w_gate_lr2, b_gate_lr2=b_gate_lr2, attn_sinks=attn_sinks, gla_norm_g=gla_norm_g,
                   w_out=w_out, ln1_g=ln1_g, ln1_b=ln1_b, w_ffn_gate=w_ffn_gate, w_ffn_up=w_ffn_up,
                   w_ffn_down=w_ffn_down, ln2_g=ln2_g, ln2_b=ln2_b)
    m_in = dict(meta_tokens=m_meta_tokens, ln_in_g=m_ln_in_g, ln_in_b=m_ln_in_b, w_in=m_w_in, b_in=m_b_in,
                w_gate_lr2=m_w_gate_lr2, b_gate_lr2=m_b_gate_lr2, attn_sinks=m_attn_sinks, gla_norm_g=m_gla_norm_g,
                w_out=m_w_out, ln1_g=m_ln1_g, ln1_b=m_ln1_b, w_ffn_gate=m_w_ffn_gate, w_ffn_up=m_w_ffn_up,
                w_ffn_down=m_w_ffn_down, ln2_g=m_ln2_g, ln2_b=m_ln2_b)
    v_in = dict(meta_tokens=v_meta_tokens, ln_in_g=v_ln_in_g, ln_in_b=v_ln_in_b, w_in=v_w_in, b_in=v_b_in,
                w_gate_lr2=v_w_gate_lr2, b_gate_lr2=v_b_gate_lr2, attn_sinks=v_attn_sinks, gla_norm_g=v_gla_norm_g,
                w_out=v_w_out, ln1_g=v_ln1_g, ln1_b=v_ln1_b, w_ffn_gate=v_w_ffn_gate, w_ffn_up=v_w_ffn_up,
                w_ffn_down=v_w_ffn_down, ln2_g=v_ln2_g, ln2_b=v_ln2_b)
    names = list(weights)
    big_names = ("w_in", "w_out", "w_ffn_gate", "w_ffn_up", "w_ffn_down")

    grads, delta, new_m, new_v = {}, {}, {}, {}
    flips = [(lambda a: a.T) if kk in ("w_g", "w_u") else (lambda a: a) for kk in BIG[1:]]
    by_row = lambda a: jnp.transpose(a, (2, 0, 1))
    updated, updated_w_in = _adamw(
        [(flip(weights[k][0]), big_g[kk], flip(m_in[k][0]), flip(v_in[k][0]))
         for k, kk, flip in zip(big_names[1:], BIG[1:], flips)],
        (by_row(w_in), big_g["w_in"], by_row(m_w_in), by_row(v_w_in)), chip.astype(jnp.int32).reshape(1), r_in % BF16_ROWS)
    for k, flip, results in zip(big_names[1:], flips, updated):
        grads[k], delta[k], new_m[k], new_v[k] = (flip(t)[None] for t in results)
    grads["w_in"], delta["w_in"], new_m["w_in"], new_v["w_in"] = (jnp.transpose(t, (1, 2, 0)) for t in updated_w_in)
    small_in = [tuple(src[k].reshape(shape) for src in (weights, m_in, v_in)) for k, shape in SMALL]
    place = jnp.stack([2 * chip + lax.axis_index("c"), chip]).astype(jnp.int32)
    small_own, small_all = _small_wait(small_handle, [updated[0][0]])
    small_out, loss = _adamw_small(place, small_all, small_own, small_in)
    for (k, _), results in zip(SMALL, small_out):
        grads[k], delta[k], new_m[k], new_v[k] = (r.reshape(weights[k].shape) for r in results)

    return (loss.reshape(()), dx[None], *[grads[k] for k in names], *[delta[k] for k in names], *[new_m[k] for k in names],
            *[new_v[k] for k in names])
```

```python
import jax
import jax.numpy as jnp
from jax import lax
from jax.experimental import pallas as pl
from jax.experimental.pallas import tpu as pltpu

F32 = jnp.float32
BF16 = jnp.bfloat16
MESH = pl.DeviceIdType.MESH

D = 1024
SEQ = 4096
N_META = 16
SWA_HEADS, SWA_KV_HEADS, DH = 8, 2, 64
WINDOW = 128
GLA_HEADS, DK, DV = 4, 64, 128
GLA_TAU = 16.0
CH = 64
D_FF = 2816
D_IN = 2320
LN_EPS = 1e-5
RMS_EPS = 1e-6
ALPHA = 2.0 ** 0.25
NEG = -1e30
ADAM_LR, ADAM_B1, ADAM_B2, ADAM_EPS, ADAM_WD, ADAM_STEP = 0.001, 0.9, 0.999, 1e-8, 0.01, 10
O_QS, O_KS, O_VS, O_QG, O_KG, O_VG, O_RG, O_LR = 0, 512, 640, 768, 1024, 1280, 1792, 2304

LANE = 128
BLK = WINDOW
GATE_RANK = 16
D_IN_P = D_IN + LANE - GATE_RANK
META_OFF = CH - N_META
HEAD_POS = (0, 4, 1, 5, 2, 6, 3, 7)
LN_ROWS = 512
TOKEN = (8, LANE)
N_CHIPS = 4
SHARD_ROWS = dict(w_in=D_IN // N_CHIPS, w_out=D // N_CHIPS, w_g=D_FF // N_CHIPS, w_u=D_FF // N_CHIPS,
                  w_d=D_FF // N_CHIPS)
SMALL_ROWS = 48
BF16_ROWS = 16
W_IN_WIN = -(-SHARD_ROWS["w_in"] // (2 * BF16_ROWS)) * 2 * BF16_ROWS
W_IN_STARTS = tuple(s * SHARD_ROWS["w_in"] // BF16_ROWS * BF16_ROWS for s in range(N_CHIPS))
VMEM_CAP_MB = 64
VMEM_SPARE_MB = 6


def _lp():
    return SEQ + BLK


def _row_tile(cap):
    lp = _lp()
    return max(t for t in range(16, cap + 1, 16) if lp % t == 0)


def _params(vmem_mb, **kw):
    assert vmem_mb <= VMEM_CAP_MB - VMEM_SPARE_MB
    return pltpu.CompilerParams(vmem_limit_bytes=vmem_mb << 20, **kw)


def _seq(n=1):
    return ("arbitrary",) * n


def _const(shape):
    return pl.BlockSpec(shape, lambda *_: (0,) * len(shape), pipeline_mode=pl.Buffered(1))


def _acc(shape):
    return pl.BlockSpec(shape, lambda *_: (0,) * len(shape))


def _rows(tm, width):
    return pl.BlockSpec((tm, width), lambda i: (i, 0))


def _dot(a, b):
    return jnp.dot(a.astype(BF16), b.astype(BF16), preferred_element_type=F32)


def _dot_nt(a, b):
    return lax.dot_general(a.astype(BF16), b.astype(BF16), (((1,), (1,)), ((), ())), preferred_element_type=F32)


def _dot_tn(a, b):
    return lax.dot_general(a.astype(BF16), b.astype(BF16), (((0,), (0,)), ((), ())), preferred_element_type=F32)


def _dot_exact(a, b):
    return jnp.dot(a, b, precision=lax.Precision.HIGHEST, preferred_element_type=F32)


def _ln_stats(x):
    mu = jnp.mean(x, axis=-1, keepdims=True)
    xc = x - mu
    rstd = lax.rsqrt(jnp.mean(xc * xc, axis=-1, keepdims=True) + LN_EPS)
    return xc * rstd, rstd


def _ln_bwd(dy, xhat, rstd, g):
    dxh = dy * g
    return rstd * (dxh - jnp.mean(dxh, axis=-1, keepdims=True) - xhat * jnp.mean(dxh * xhat, axis=-1, keepdims=True))


def _sigmoid(x):
    return 1.0 / (1.0 + jnp.exp(-x))


def _iota(shape, dim):
    return lax.broadcasted_iota(jnp.int32, shape, dim)


def _hbm(*arrays):
    return tuple(pltpu.with_memory_space_constraint(a, pltpu.HBM) for a in arrays)


RING = 3


def _ring_fetch(steps, tile, sources, rings, sems):
    assert steps >= RING - 1
    step = pl.program_id(0)

    def copy(t, k):
        rows = pl.ds(pl.multiple_of(t * tile, tile), tile)
        return pltpu.make_async_copy(sources[k].at[rows, :], rings[k].at[t % RING], sems.at[k, t % RING])

    @pl.when(step == 0)
    def _():
        for t in range(RING - 1):
            for k in range(len(sources)):
                copy(t, k).start()

    @pl.when(step + (RING - 1) < steps)
    def _():
        for k in range(len(sources)):
            copy(step + (RING - 1), k).start()

    for k in range(len(sources)):
        copy(step, k).wait()
    return step % RING


def _ring_scratch(tile, widths):
    return [pltpu.VMEM((RING, tile, w), F32) for w in widths] + [pltpu.SemaphoreType.DMA((len(widths), RING))]


def _ln_in_fwd_real(x, g, b, token):
    tr = min(LN_ROWS, SEQ)
    steps = SEQ // tr

    def body(x_hbm, g_ref, b_ref, token_ref, h_ref, x_ring, sems):
        slot = _ring_fetch(steps, tr, [x_hbm], [x_ring], sems)
        xhat, _ = _ln_stats(x_ring[slot])
        h_ref[...] = xhat * g_ref[...] + b_ref[...]

    return pl.pallas_call(
        body, name="ln_in_fwd", grid=(steps,),
        in_specs=[pl.BlockSpec(memory_space=pl.ANY), _const((1, D)), _const((1, D)), _const(TOKEN)],
        out_specs=_rows(tr, D),
        out_shape=pltpu.HBM((_lp(), D), F32),
        scratch_shapes=_ring_scratch(tr, [D]),
        compiler_params=_params(32, dimension_semantics=_seq()),
    )(*_hbm(x, g, b), token)


def _ln_in_fwd_meta(h_real, meta_ext, g, b):
    def meta_body(m_ref, g_ref, b_ref, real_ref, h_ref):
        xhat, _ = _ln_stats(m_ref[...])
        h_ref[...] = xhat * g_ref[...] + b_ref[...]

    return pl.pallas_call(
        meta_body, name="ln_in_fwd_meta", grid=(1,),
        in_specs=[_const((BLK, D)), _const((1, D)), _const((1, D)), pl.BlockSpec(memory_space=pl.ANY)],
        out_specs=pl.BlockSpec((BLK, D), lambda i: (SEQ // BLK, 0)),
        out_shape=pltpu.HBM((_lp(), D), F32),
        input_output_aliases={3: 0},
        compiler_params=_params(16, dimension_semantics=_seq()),
    )(*_hbm(meta_ext, g, b, h_real))


def _in_proj(h0, w_in_windows, b_in_p, wg2_p, bg2):
    tm = _row_tile(384)
    lp = _lp()
    widths = (512, 128, 128, 256, 256, 512, 512, 128)
    offs = (O_QS, O_KS, O_VS, O_QG, O_KG, O_VG, O_RG, O_LR)
    shard = SHARD_ROWS["w_in"]

    def body(h_ref, win_ref, b_ref, wg2_ref, bg2_ref, *outs):
        w_ref = outs[9]

        @pl.when(pl.program_id(0) == 0)
        def _():
            for s in range(N_CHIPS):
                w_ref[shard * s:shard * (s + 1), :] = win_ref[s, 0:shard, :]
            w_ref[D_IN:D_IN_P, :] = jnp.zeros((D_IN_P - D_IN, D), BF16)

        proj = _dot_nt(h_ref[...], w_ref[...]) + b_ref[...]
        for pos, h in enumerate(HEAD_POS):
            outs[0][:, pos * DH:(pos + 1) * DH] = proj[:, O_QS + h * DH:O_QS + (h + 1) * DH]
        for o_ref, off, wd in zip(outs[1:8], offs[1:], widths[1:]):
            o_ref[...] = proj[:, off:off + wd]
        outs[8][...] = _dot(proj[:, O_LR:O_LR + LANE], wg2_ref[...]) + bg2_ref[...]

    return pl.pallas_call(
        body, name="in_proj", grid=(lp // tm,),
        in_specs=[_rows(tm, D), _const(w_in_windows.shape), _const((1, D_IN_P)), _const((LANE, 256)), _const((1, 256))],
        out_specs=[_rows(tm, w) for w in widths] + [_rows(tm, 256), _acc((D_IN_P, D))],
        out_shape=[pltpu.HBM((lp, w), F32) for w in widths] + [pltpu.HBM((lp, 256), F32), pltpu.HBM((D_IN_P, D), BF16)],
        compiler_params=_params(48, dimension_semantics=_seq()),
    )(*_hbm(h0, w_in_windows, b_in_p, wg2_p, bg2))


def _swa_masks(n):
    nb = SEQ // BLK
    is_meta = n == nb
    ri = _iota((BLK, BLK), 0)
    cj = _iota((BLK, BLK), 1)
    meta_col = ((cj >= META_OFF) & (cj < CH)).astype(jnp.int32)
    meta_q = meta_col * ((cj <= ri) & (ri < CH)).astype(jnp.int32)
    valid_m = jnp.where(is_meta, meta_q, meta_col) > 0
    dist_m = jnp.where(is_meta, ri - cj, n * BLK + ri + CH - cj).astype(F32)
    valid_p = jnp.where((n >= 1) & (n < nb), (cj > ri).astype(jnp.int32), 0) > 0
    dist_p = (ri + BLK - cj).astype(F32)
    valid_c = jnp.where(n < nb, (cj <= ri).astype(jnp.int32), 0) > 0
    dist_c = (ri - cj).astype(F32)
    return (dist_m, dist_p, dist_c), (valid_m, valid_p, valid_c)


def _swa_bias(n):
    dists, valids = _swa_masks(n)
    return (jnp.concatenate([-d for d in dists], axis=1),
            jnp.concatenate([jnp.where(v, 0.0, NEG) for v in valids], axis=1))


def _swa_half(ref, pos, scale=1.0):
    col = ref[:, (pos // 2) * LANE:(pos // 2 + 1) * LANE]
    lane = _iota((BLK, LANE), 1)
    mine = lane < DH if pos % 2 == 0 else lane >= DH
    return jnp.where(mine, col * scale, 0.0).astype(BF16)


def _swa_merge(even, odd):
    return jnp.where(_iota((BLK, LANE), 1) < DH, even, odd)


def _swa_softmax(t, sink):
    m = jnp.maximum(jnp.max(t, axis=-1, keepdims=True), sink)
    e = jnp.exp(t - m)
    e_sink = jnp.exp(sink - m)
    inv = 1.0 / (jnp.sum(e, axis=-1, keepdims=True) + e_sink)
    return e * inv, e_sink * inv


def _swa_kv_specs(width):
    nb = SEQ // BLK
    return [pl.BlockSpec((BLK, width), lambda n: (nb, 0)),
            pl.BlockSpec((BLK, width), lambda n: (jnp.clip(n - 1, 0, nb - 1), 0)),
            pl.BlockSpec((BLK, width), lambda n: (jnp.minimum(n, nb), 0))]


def _swa_fwd(sinks, qs, ks, vs):
    nb = SEQ // BLK
    heads = range(SWA_HEADS)

    def body(sink_ref, q_ref, km_ref, kp_ref, kc_ref, vm_ref, vp_ref, vc_ref, o_ref):
        negdist, maskbias = _swa_bias(pl.program_id(0))
        k_all = jnp.concatenate([km_ref[...], kp_ref[...], kc_ref[...]], axis=0).astype(BF16)
        v_all = jnp.concatenate([vm_ref[...], vp_ref[...], vc_ref[...]], axis=0).astype(BF16)
        q = [_swa_half(q_ref, pos, DH ** -0.5) for pos in heads]
        t = [_dot_nt(q[pos], k_all) + (2.0 ** -(HEAD_POS[pos] + 1) * negdist + maskbias) for pos in heads]
        p = [_swa_softmax(t[pos], sink_ref[HEAD_POS[pos]])[0].astype(BF16) for pos in heads]
        o = [_dot(p[pos], v_all) for pos in heads]
        for col in range(SWA_HEADS // 2):
            o_ref[:, col * LANE:(col + 1) * LANE] = _swa_merge(o[2 * col], o[2 * col + 1])

    kvw = SWA_KV_HEADS * DH
    return pl.pallas_call(
        body, name="swa_fwd", grid=(nb + 1,),
        in_specs=[pl.BlockSpec(memory_space=pltpu.SMEM), _rows(BLK, SWA_HEADS * DH)] + _swa_kv_specs(kvw) + _swa_kv_specs(kvw),
        out_specs=_rows(BLK, SWA_HEADS * DH),
        out_shape=pltpu.HBM((_lp(), SWA_HEADS * DH), F32),
        compiler_params=_params(16, dimension_semantics=_seq()),
    )(sinks, *_hbm(qs, ks, ks, ks, vs, vs, vs))


GLA_PER_STEP = BLK // CH


def _gla_block(s):
    nb = SEQ // BLK
    return jnp.where(s == 0, nb, s - 1)


def _gla_rowmask(s):
    ri = _iota((BLK, 1), 0)
    m = jnp.where(s == 0, ((ri >= META_OFF) & (ri < CH)).astype(jnp.int32), 1)
    return (m > 0).astype(F32) + jnp.zeros((BLK, 1), F32)


def _gla_chunk_masks():
    r, c = _iota((BLK, BLK), 0), _iota((BLK, BLK), 1)
    same = ((r < CH) & (c < CH)) | ((r >= CH) & (c >= CH))
    return same & (r >= c), same & (r <= c), same


def _gla_decay(z, rmask):
    log_g = (jnp.minimum(z, 0.0) - jnp.log1p(jnp.exp(-jnp.abs(z)))) * (rmask / GLA_TAU)
    lower, _, same = _gla_chunk_masks()
    return _dot_exact(lower.astype(F32), log_g), _dot_exact(same.astype(F32), log_g)


def _gla_slices(c, h):
    return slice(c * CH, (c + 1) * CH), slice(h * DK, (h + 1) * DK), slice(h * DV, (h + 1) * DV)


def _gla_fwd(qg, kg, vg, z):
    steps = SEQ // BLK + 1
    kw, vw = GLA_HEADS * DK, GLA_HEADS * DV
    pairs = [(c, h) for c in range(GLA_PER_STEP) for h in range(GLA_HEADS)]

    def body(q_ref, k_ref, v_ref, z_ref, o_ref, st_ref, st):
        s = pl.program_id(0)

        @pl.when(s == 0)
        def _():
            st[...] = jnp.zeros_like(st)

        rmask = _gla_rowmask(s)
        b, b_last = _gla_decay(z_ref[...], rmask)
        q = q_ref[...] * (rmask * DK ** -0.5)
        k = k_ref[...] * rmask
        v = v_ref[...] * rmask
        qe = q * jnp.exp(b)
        ke = k * jnp.exp(-b)
        kd = k * jnp.exp(b_last - b)
        e_last = jnp.exp(b_last)
        causal = _iota((CH, CH), 0) >= _iota((CH, CH), 1)
        a, upd, intra = {}, {}, {}
        for c, h in pairs:
            rows, ks, vs_ = _gla_slices(c, h)
            a[c, h] = jnp.where(causal, _dot_nt(qe[rows, ks], ke[rows, ks]), 0.0)
            upd[c, h] = _dot_tn(v[rows, vs_], kd[rows, ks])
        for c, h in pairs:
            rows, ks, vs_ = _gla_slices(c, h)
            intra[c, h] = _dot(a[c, h], v[rows, vs_])
        state = st[...]
        for c in range(GLA_PER_STEP):
            st_ref[0, c] = state
            for h in range(GLA_HEADS):
                rows, ks, vs_ = _gla_slices(c, h)
                o_ref[rows, vs_] = intra[c, h] + _dot_nt(qe[rows, ks], state[:, ks])
            state = state * e_last[c * CH:c * CH + 1] + jnp.concatenate([upd[c, h] for h in range(GLA_HEADS)], axis=1)
        st[...] = state

    blk = lambda w: pl.BlockSpec((BLK, w), lambda s: (_gla_block(s), 0))
    return pl.pallas_call(
        body, name="gla_fwd", grid=(steps,),
        in_specs=[blk(kw), blk(kw), blk(vw), blk(kw)],
        out_specs=[blk(vw), pl.BlockSpec((1, GLA_PER_STEP, DV, kw), lambda s: (s, 0, 0, 0))],
        out_shape=[pltpu.HBM((_lp(), vw), F32), pltpu.HBM((steps, GLA_PER_STEP, DV, kw), F32)],
        scratch_shapes=[pltpu.VMEM((DV, kw), F32)],
        compiler_params=_params(16, dimension_semantics=_seq()),
    )(*_hbm(qg, kg, vg, z))


def _post_mix(o_s, o_gla, r_g, h0, gn4, w_out, g1, b1, token):
    tm = _row_tile(384)
    lp = _lp()

    def body(os_hbm, og_hbm, r_hbm, h0_hbm, gn_ref, w_ref, g_ref, b_ref, token_ref, o_ref, pre_ref, h1_ref, *scratch):
        slot = _ring_fetch(lp // tm, tm, [os_hbm, og_hbm, r_hbm, h0_hbm], scratch[:4], scratch[4])
        os_ref, og_ref, r_ref, h0_ref = (ring.at[slot] for ring in scratch[:4])
        for pos, h in enumerate(HEAD_POS):
            o_ref[:, h * DH:(h + 1) * DH] = os_ref[:, pos * DH:(pos + 1) * DH].astype(BF16)
        for h in range(GLA_HEADS):
            hs = slice(h * DV, (h + 1) * DV)
            xg = og_ref[:, hs]
            n = xg * lax.rsqrt(jnp.mean(xg * xg, axis=-1, keepdims=True) + RMS_EPS) * gn_ref[...]
            r = r_ref[:, hs]
            o_ref[:, 512 + h * DV:512 + (h + 1) * DV] = (n * (r * _sigmoid(r))).astype(BF16)
        pre = ALPHA * h0_ref[...] + _dot(o_ref[...], w_ref[...])
        pre_ref[...] = pre
        xhat, _ = _ln_stats(pre)
        h1_ref[...] = xhat * g_ref[...] + b_ref[...]

    return pl.pallas_call(
        body, name="post_mix", grid=(lp // tm,),
        in_specs=[pl.BlockSpec(memory_space=pl.ANY)] * 4 + [_const((1, DV)), _const((D, D)),
                                                              _const((1, D)), _const((1, D)), _const(TOKEN)],
        out_specs=[_rows(tm, D), _rows(tm, D), _rows(tm, D)],
        out_shape=[pltpu.HBM((lp, D), BF16), pltpu.HBM((lp, D), F32),
                   pltpu.HBM((lp, D), F32)],
        scratch_shapes=_ring_scratch(tm, [512, 512, 512, D]),
        compiler_params=_params(40, dimension_semantics=_seq()),
    )(*_hbm(o_s, o_gla, r_g, h0, gn4, w_out, g1, b1), token)


def _ffn_fwd_loss_bwd(h1, wg_t, wu_t, wd, target, g2, b2):
    lp = _lp()
    tm = max(t for t in range(BLK, 384 + 1, BLK) if lp % t == 0)
    steps = lp // tm
    last_blk = SEQ // BLK - 1
    half = D_FF // 2
    n_t = tm // BLK

    def body(*refs):
        h_ref, wg_ref, wu_ref, wd_ref = refs[:4]
        t_refs = refs[4:4 + n_t]
        g2_ref, b2_ref, a_ref, dgate_ref, dup_ref, dp_ref, loss_ref, dg_ref, db_ref, g_s, u_s, acc = refs[4 + n_t:]
        i = pl.program_id(0)

        @pl.when(i == 0)
        def _():
            acc[...] = jnp.zeros_like(acc)
            dg_ref[...] = jnp.zeros_like(dg_ref)
            db_ref[...] = jnp.zeros_like(db_ref)

        h = h_ref[...]
        hb = h.astype(BF16)
        pre = ALPHA * h
        for j in range(2):
            cols = slice(j * half, (j + 1) * half)
            g = _dot_nt(hb, wg_ref[cols, :])
            u = _dot_nt(hb, wu_ref[cols, :])
            g_s[:, cols] = g
            u_s[:, cols] = u
            pre = pre + _dot(g * _sigmoid(g) * u, wd_ref[cols, :])
        xhat, rstd = _ln_stats(pre)
        real = i * tm + _iota((tm, 1), 0) < SEQ
        target_rows = jnp.concatenate([t[...] for t in t_refs], axis=0)
        diff = jnp.where(real, xhat * g2_ref[...] + b2_ref[...] - target_rows, 0.0)
        acc[...] += jnp.sum(diff * diff, axis=0, keepdims=True)
        dy = diff * (1.0 / D)
        dpre = _ln_bwd(dy, xhat, rstd, g2_ref[...])
        dp_ref[...] = dpre
        dg_ref[...] += jnp.sum(dy * xhat, axis=0, keepdims=True)
        db_ref[...] += jnp.sum(dy, axis=0, keepdims=True)
        dpb = dpre.astype(BF16)
        for j in range(2):
            cols = slice(j * half, (j + 1) * half)
            g, u = g_s[:, cols], u_s[:, cols]
            sg = _sigmoid(g)
            silu = g * sg
            da = _dot_nt(dpb, wd_ref[cols, :])
            a_ref[:, cols] = (silu * u).astype(BF16)
            dgate_ref[:, cols] = (da * u * (sg * (1.0 + g * (1.0 - sg)))).astype(BF16)
            dup_ref[:, cols] = (da * silu).astype(BF16)

        @pl.when(i == steps - 1)
        def _():
            loss_ref[...] = jnp.zeros_like(loss_ref) + (0.5 / D) * jnp.sum(acc[...], axis=1, keepdims=True)

    t_spec = lambda k: pl.BlockSpec((BLK, D), lambda i: (jnp.minimum(i * n_t + k, last_blk), 0))
    return pl.pallas_call(
        body, name="ffn_fwd_loss_bwd", grid=(steps,),
        in_specs=[_rows(tm, D), _const((D_FF, D)), _const((D_FF, D)), _const((D_FF, D))]
        + [t_spec(k) for k in range(n_t)] + [_const((1, D)), _const((1, D))],
        out_specs=[_rows(tm, D_FF), _rows(tm, D_FF), _rows(tm, D_FF), _rows(tm, D), _acc((1, LANE)), _acc((1, D)),
                   _acc((1, D))],
        out_shape=[pltpu.HBM((lp, D_FF), BF16)] * 3 + [pltpu.HBM((lp, D), F32), pltpu.HBM((1, LANE), F32),
                                                         pltpu.HBM((1, D), F32), pltpu.HBM((1, D), F32)],
        scratch_shapes=[pltpu.VMEM((tm, D_FF), F32), pltpu.VMEM((tm, D_FF), F32), pltpu.VMEM((1, D), F32)],
        compiler_params=_params(58, dimension_semantics=_seq()),
    )(*_hbm(h1, wg_t, wu_t, wd, *[target] * n_t, g2, b2))


def _ffn_out_bwd(dpre2, dgate, dup, pre1, wg_t, wu_t, g1, w_out, o_gla, r_g, gn4):
    tm = _row_tile(384)
    lp = _lp()

    def body(dp_ref, dg_ref, du_ref, p1_ref, wg_ref, wu_ref, g1_ref, w_ref, og_ref, r_ref, gn_ref,
             dp1_ref, dg1_ref, db1_ref, dos_ref, dog_ref, dr_ref, dgn_ref):
        @pl.when(pl.program_id(0) == 0)
        def _():
            for acc_ref in (dg1_ref, db1_ref, dgn_ref):
                acc_ref[...] = jnp.zeros_like(acc_ref)

        dh1 = ALPHA * dp_ref[...] + _dot(dg_ref[...], wg_ref[...]) + _dot(du_ref[...], wu_ref[...])
        xhat, rstd1 = _ln_stats(p1_ref[...])
        dpre1 = _ln_bwd(dh1, xhat, rstd1, g1_ref[...])
        dp1_ref[...] = dpre1
        dg1_ref[...] += jnp.sum(dh1 * xhat, axis=0, keepdims=True)
        db1_ref[...] += jnp.sum(dh1, axis=0, keepdims=True)

        do = _dot_nt(dpre1, w_ref[...])
        for pos, h in enumerate(HEAD_POS):
            dos_ref[:, pos * DH:(pos + 1) * DH] = do[:, h * DH:(h + 1) * DH]
        gn = gn_ref[...]
        for h in range(GLA_HEADS):
            hs = slice(h * DV, (h + 1) * DV)
            xg = og_ref[:, hs]
            rstd = lax.rsqrt(jnp.mean(xg * xg, axis=-1, keepdims=True) + RMS_EPS)
            nx = xg * rstd
            r = r_ref[:, hs]
            sr = _sigmoid(r)
            d_o = do[:, 512 + h * DV:512 + (h + 1) * DV]
            dr_ref[:, hs] = d_o * (nx * gn) * (sr * (1.0 + r * (1.0 - sr)))
            dn = d_o * (r * sr)
            dgn_ref[...] += jnp.sum(dn * nx, axis=0, keepdims=True)
            dnx = dn * gn
            dog_ref[:, hs] = rstd * (dnx - nx * jnp.mean(dnx * nx, axis=-1, keepdims=True))

    return pl.pallas_call(
        body, name="ffn_out_bwd", grid=(lp // tm,),
        in_specs=[_rows(tm, D), _rows(tm, D_FF), _rows(tm, D_FF), _rows(tm, D), _const((D_FF, D)), _const((D_FF, D)),
                  _const((1, D)), _const((D, D)), _rows(tm, 512), _rows(tm, 512), _const((1, DV))],
        out_specs=[_rows(tm, D), _acc((1, D)), _acc((1, D)), _rows(tm, 512), _rows(tm, 512), _rows(tm, 512),
                   _acc((1, DV))],
        out_shape=[pltpu.HBM((lp, D), F32), pltpu.HBM((1, D), F32), pltpu.HBM((1, D), F32)]
        + [pltpu.HBM((lp, 512), F32)] * 3 + [pltpu.HBM((1, DV), F32)],
        compiler_params=_params(48, dimension_semantics=_seq()),
    )(*_hbm(dpre2, dgate, dup, pre1, wg_t, wu_t, g1, w_out, o_gla, r_g, gn4))


def _atb(a, b, name, token=None, windows=None):
    lp = _lp()
    tm = _row_tile(1408)
    n, w = a.shape[1], b.shape[1]
    bw = 512 if n * w * 4 > (4 << 20) else w
    tokens = [] if token is None else [token]
    steps = lp // tm

    def body(a_ref, b_ref, *rest):
        o_ref, acc_ref = rest[len(tokens):] if windows else (rest[-1], rest[-1])

        @pl.when(pl.program_id(1) == 0)
        def _():
            acc_ref[...] = jnp.zeros_like(acc_ref)

        acc_ref[...] += _dot_tn(a_ref[...], b_ref[...])

        if windows:
            @pl.when(pl.program_id(1) == steps - 1)
            def _():
                for s, start in enumerate(windows[0]):
                    o_ref[s] = acc_ref[start:start + windows[1], :]

    if windows:
        count, height = len(windows[0]), windows[1]
        out_spec, out_shape = pl.BlockSpec((count, height, bw), lambda j, k: (0, 0, j)), (count, height, w)
    else:
        out_spec, out_shape = pl.BlockSpec((n, bw), lambda j, k: (0, j)), (n, w)
    return pl.pallas_call(
        body, name=name, grid=(w // bw, steps),
        in_specs=[pl.BlockSpec((tm, n), lambda j, k: (k, 0)), pl.BlockSpec((tm, bw), lambda j, k: (k, j))]
        + [_const(TOKEN)] * len(tokens),
        out_specs=out_spec, out_shape=pltpu.HBM(out_shape, F32),
        scratch_shapes=[pltpu.VMEM((n, bw), F32)] if windows else [],
        compiler_params=_params(48, dimension_semantics=_seq(2)),
    )(*_hbm(a, b), *tokens)


def _gla_bwd(qg, kg, vg, z, do_gla, st_all, token):
    steps = SEQ // BLK + 1
    kw, vw = GLA_HEADS * DK, GLA_HEADS * DV
    pairs = [(c, h) for c in range(GLA_PER_STEP) for h in range(GLA_HEADS)]
    heads = range(GLA_HEADS)

    def body(q_ref, k_ref, v_ref, z_ref, do_ref, st_ref, token_ref, dq_ref, dk_ref, dv_ref, dz_ref, dst):
        @pl.when(pl.program_id(0) == 0)
        def _():
            dst[...] = jnp.zeros_like(dst)

        rmask = _gla_rowmask(steps - 1 - pl.program_id(0))
        zz = z_ref[...]
        b, b_last = _gla_decay(zz, rmask)
        e_b, e_nb, e_kd, e_last = jnp.exp(b), jnp.exp(-b), jnp.exp(b_last - b), jnp.exp(b_last)
        q = q_ref[...] * (rmask * DK ** -0.5)
        k = k_ref[...] * rmask
        v = v_ref[...] * rmask
        qe, ke, kd = q * e_b, k * e_nb, k * e_kd
        d_o = do_ref[...]
        causal = _iota((CH, CH), 0) >= _iota((CH, CH), 1)
        a, da, dqe, dke, dv_intra, carry = {}, {}, {}, {}, {}, {}
        for c, h in pairs:
            rows, ks, vs_ = _gla_slices(c, h)
            a[c, h] = jnp.where(causal, _dot_nt(qe[rows, ks], ke[rows, ks]), 0.0)
            da[c, h] = jnp.where(causal, _dot_nt(d_o[rows, vs_], v[rows, vs_]), 0.0)
            carry[c, h] = _dot_tn(d_o[rows, vs_], qe[rows, ks])
        for c, h in pairs:
            rows, ks, vs_ = _gla_slices(c, h)
            dqe[c, h] = _dot(d_o[rows, vs_], st_ref[0, c][:, ks]) + _dot(da[c, h], ke[rows, ks])
            dke[c, h] = _dot_tn(da[c, h], qe[rows, ks])
            dv_intra[c, h] = _dot_tn(a[c, h], d_o[rows, vs_])
        dstate = dst[...]
        dkd, db_decay = {}, {}
        for c in reversed(range(GLA_PER_STEP)):
            for h in heads:
                rows, ks, vs_ = _gla_slices(c, h)
                dkd[c, h] = _dot(v[rows, vs_], dstate[:, ks])
                dv_ref[rows, vs_] = dv_intra[c, h] + _dot_nt(kd[rows, ks], dstate[:, ks])
            chunk_last = e_last[c * CH:c * CH + 1]
            db_decay[c] = jnp.sum(dstate * st_ref[0, c], axis=0, keepdims=True) * chunk_last
            dstate = dstate * chunk_last + jnp.concatenate([carry[c, h] for h in heads], axis=1)
        dst[...] = dstate
        rows_of = lambda parts: jnp.concatenate(
            [jnp.concatenate([parts[c, h] for h in heads], axis=1) for c in range(GLA_PER_STEP)], axis=0)
        dqe_all, dke_all, dkd_all = rows_of(dqe), rows_of(dke), rows_of(dkd)
        dq_ref[...] = dqe_all * e_b * (rmask * DK ** -0.5)
        dk_ref[...] = (dke_all * e_nb + dkd_all * e_kd) * rmask
        dkd_kd = dkd_all * kd
        db = dqe_all * qe - dke_all * ke - dkd_kd
        _, upper, same = _gla_chunk_masks()
        decay_rows = jnp.concatenate([jnp.broadcast_to(db_decay[c], (CH, kw)) for c in range(GLA_PER_STEP)], axis=0)
        dlog_g = _dot_exact(upper.astype(F32), db) + _dot_exact(same.astype(F32), dkd_kd) + decay_rows
        dz_ref[...] = dlog_g * (rmask / GLA_TAU) * _sigmoid(-zz)

    blk = lambda w: pl.BlockSpec((BLK, w), lambda s: (_gla_block(steps - 1 - s), 0))
    return pl.pallas_call(
        body, name="gla_bwd", grid=(steps,),
        in_specs=[blk(kw), blk(kw), blk(vw), blk(kw), blk(vw),
                  pl.BlockSpec((1, GLA_PER_STEP, DV, kw), lambda s: (steps - 1 - s, 0, 0, 0)), _const(TOKEN)],
        out_specs=[blk(kw), blk(kw), blk(vw), blk(kw)],
        out_shape=[pltpu.HBM((_lp(), kw), F32), pltpu.HBM((_lp(), kw), F32),
                   pltpu.HBM((_lp(), vw), F32), pltpu.HBM((_lp(), kw), F32)],
        scratch_shapes=[pltpu.VMEM((DV, kw), F32)],
        compiler_params=_params(16, dimension_semantics=_seq()),
    )(*_hbm(qg, kg, vg, z, do_gla, st_all), token)


def _swa_bwd(sinks, qs, ks, vs, do_s, token):
    nb = SEQ // BLK
    kvw = SWA_KV_HEADS * DH
    scale = DH ** -0.5
    heads = range(SWA_HEADS)

    def body(sink_ref, q_ref, km_ref, kp_ref, kc_ref, vm_ref, vp_ref, vc_ref, do_ref, token_ref,
             dq_ref, dk_ref, dv_ref, dsink_ref, carry_k, carry_v, meta_k, meta_v):
        n = pl.program_id(0)

        @pl.when(n == 0)
        def _():
            for r in (carry_k, carry_v, meta_k, meta_v):
                r[...] = jnp.zeros_like(r)
            dsink_ref[...] = jnp.zeros_like(dsink_ref)

        @pl.when(n <= nb)
        def _():
            negdist, maskbias = _swa_bias(n)
            lane = _iota((1, LANE), 1)
            k_all = jnp.concatenate([km_ref[...], kp_ref[...], kc_ref[...]], axis=0).astype(BF16)
            v_all = jnp.concatenate([vm_ref[...], vp_ref[...], vc_ref[...]], axis=0).astype(BF16)
            q = [_swa_half(q_ref, pos, scale) for pos in heads]
            d_o = [_swa_half(do_ref, pos) for pos in heads]
            t = [_dot_nt(q[pos], k_all) + (2.0 ** -(HEAD_POS[pos] + 1) * negdist + maskbias) for pos in heads]
            dp = [_dot_nt(d_o[pos], v_all) for pos in heads]
            soft = [_swa_softmax(t[pos], sink_ref[HEAD_POS[pos]]) for pos in heads]
            p = [s[0] for s in soft]
            delta = [jnp.sum(p[pos] * dp[pos], axis=-1, keepdims=True) for pos in heads]
            ds = [(p[pos] * (dp[pos] - delta[pos])).astype(BF16) for pos in heads]
            dq = [_dot(ds[pos], k_all) for pos in heads]
            for col in range(SWA_HEADS // 2):
                dq_ref[:, col * LANE:(col + 1) * LANE] = scale * _swa_merge(dq[2 * col], dq[2 * col + 1])
            dsink = jnp.zeros((1, LANE), F32)
            for pos in heads:
                dsink = dsink + jnp.where(lane == HEAD_POS[pos],
                                          -jnp.sum(soft[pos][1] * delta[pos], axis=0, keepdims=True), 0.0)
            dsink_ref[...] += dsink
            dk3 = _dot_tn(jnp.concatenate(q, axis=0), jnp.concatenate(ds, axis=0)).T
            dv3 = _dot_tn(jnp.concatenate(d_o, axis=0), jnp.concatenate([x.astype(BF16) for x in p], axis=0)).T
            meta_k[...] += dk3[0:BLK]
            meta_v[...] += dv3[0:BLK]
            dk_ref[...] = carry_k[...] + dk3[BLK:2 * BLK]
            dv_ref[...] = carry_v[...] + dv3[BLK:2 * BLK]
            carry_k[...] = dk3[2 * BLK:3 * BLK]
            carry_v[...] = dv3[2 * BLK:3 * BLK]

        @pl.when(n == nb + 1)
        def _():
            dk_ref[...] = meta_k[...]
            dv_ref[...] = meta_v[...]

    kv_out = pl.BlockSpec((BLK, kvw), lambda n: (jnp.where(n == nb + 1, nb, jnp.clip(n - 1, 0, nb - 1)), 0))
    qblk = pl.BlockSpec((BLK, SWA_HEADS * DH), lambda n: (jnp.minimum(n, nb), 0))
    return pl.pallas_call(
        body, name="swa_bwd", grid=(nb + 2,),
        in_specs=[pl.BlockSpec(memory_space=pltpu.SMEM), qblk] + _swa_kv_specs(kvw) + _swa_kv_specs(kvw)
        + [qblk, _const(TOKEN)],
        out_specs=[qblk, kv_out, kv_out, _acc((1, LANE))],
        out_shape=[pltpu.HBM((_lp(), SWA_HEADS * DH), F32), pltpu.HBM((_lp(), kvw), F32),
                   pltpu.HBM((_lp(), kvw), F32), pltpu.HBM((1, LANE), F32)],
        scratch_shapes=[pltpu.VMEM((BLK, kvw), F32)] * 4,
        compiler_params=_params(16, dimension_semantics=_seq()),
    )(sinks, *_hbm(qs, ks, ks, ks, vs, vs, vs, do_s), token)


def _in_bwd(dqs, dks, dvs, dqg, dkg, dvg, drg, dz, dpre1, w_in_t, wg2_p):
    tm = _row_tile(384)
    lp = _lp()
    widths = (512, 128, 128, 256, 256, 512, 512)
    offs = (O_QS, O_KS, O_VS, O_QG, O_KG, O_VG, O_RG)

    def body(*refs):
        parts, (dz_ref, dp1_ref, w_ref, wg2_ref, dproj_ref, dh0_ref, dbin_ref, dbg_ref) = refs[:7], refs[7:]

        @pl.when(pl.program_id(0) == 0)
        def _():
            dbin_ref[...] = jnp.zeros_like(dbin_ref)
            dbg_ref[...] = jnp.zeros_like(dbg_ref)

        for pos, h in enumerate(HEAD_POS):
            val = parts[0][:, pos * DH:(pos + 1) * DH]
            dproj_ref[:, O_QS + h * DH:O_QS + (h + 1) * DH] = val.astype(BF16)
            dbin_ref[:, O_QS + h * DH:O_QS + (h + 1) * DH] += jnp.sum(val, axis=0, keepdims=True)
        for p_ref, off, wd in zip(parts[1:], offs[1:], widths[1:]):
            val = p_ref[...]
            dproj_ref[:, off:off + wd] = val.astype(BF16)
            dbin_ref[:, off:off + wd] += jnp.sum(val, axis=0, keepdims=True)
        dz = dz_ref[...]
        dlr = _dot_nt(dz, wg2_ref[...])
        dproj_ref[:, O_LR:O_LR + LANE] = dlr.astype(BF16)
        dbin_ref[:, O_LR:O_LR + LANE] += jnp.sum(dlr, axis=0, keepdims=True)
        dbg_ref[...] += jnp.sum(dz, axis=0, keepdims=True)
        dh0_ref[...] = ALPHA * dp1_ref[...] + _dot(dproj_ref[...], w_ref[...])

    return pl.pallas_call(
        body, name="in_bwd", grid=(lp // tm,),
        in_specs=[_rows(tm, w) for w in widths] + [_rows(tm, 256), _rows(tm, D), _const((D_IN_P, D)), _const((LANE, 256))],
        out_specs=[_rows(tm, D_IN_P), _rows(tm, D), _acc((1, D_IN_P)), _acc((1, 256))],
        out_shape=[pltpu.HBM((lp, D_IN_P), BF16), pltpu.HBM((lp, D), F32),
                   pltpu.HBM((1, D_IN_P), F32), pltpu.HBM((1, 256), F32)],
        compiler_params=_params(40, dimension_semantics=_seq()),
    )(*_hbm(dqs, dks, dvs, dqg, dkg, dvg, drg, dz, dpre1, w_in_t, wg2_p))


def _ln_in_bwd(x, meta_ext, dh0, g, token):
    tr = min(LN_ROWS, SEQ)
    steps = SEQ // tr

    def ln_bwd(x_ref, dh_ref, g_ref, dx_ref, dg_ref, db_ref, so_far=None):
        @pl.when(pl.program_id(0) == 0)
        def _():
            dg_ref[...] = jnp.zeros_like(dg_ref) if so_far is None else so_far[0][...]
            db_ref[...] = jnp.zeros_like(db_ref) if so_far is None else so_far[1][...]

        xhat, rstd = _ln_stats(x_ref[...])
        dh = dh_ref[...]
        dx_ref[...] = _ln_bwd(dh, xhat, rstd, g_ref[...])
        dg_ref[...] += jnp.sum(dh * xhat, axis=0, keepdims=True)
        db_ref[...] += jnp.sum(dh, axis=0, keepdims=True)

    def body(x_hbm, dh_hbm, g_ref, token_ref, dx_ref, dg_ref, db_ref, x_ring, dh_ring, sems):
        slot = _ring_fetch(steps, tr, [x_hbm, dh_hbm], [x_ring, dh_ring], sems)
        ln_bwd(x_ring.at[slot], dh_ring.at[slot], g_ref, dx_ref, dg_ref, db_ref)

    def meta_body(m_ref, dh_ref, g_ref, dg_real_ref, db_real_ref, dm_ref, dg_ref, db_ref):
        ln_bwd(m_ref, dh_ref, g_ref, dm_ref, dg_ref, db_ref, (dg_real_ref, db_real_ref))

    sums = [pltpu.HBM((1, D), F32), pltpu.HBM((1, D), F32)]
    dx, dg, db = pl.pallas_call(
        body, name="ln_in_bwd", grid=(steps,),
        in_specs=[pl.BlockSpec(memory_space=pl.ANY)] * 2 + [_const((1, D)), _const(TOKEN)],
        out_specs=[_rows(tr, D), _acc((1, D)), _acc((1, D))],
        out_shape=[pltpu.HBM((SEQ, D), F32)] + sums,
        scratch_shapes=_ring_scratch(tr, [D, D]),
        compiler_params=_params(32, dimension_semantics=_seq()),
    )(*_hbm(x, dh0, g), token)
    dm, dg, db = pl.pallas_call(
        meta_body, name="ln_in_bwd_meta", grid=(1,),
        in_specs=[_const((BLK, D)), pl.BlockSpec((BLK, D), lambda i: (SEQ // BLK, 0))] + [_const((1, D))] * 3,
        out_specs=[_acc((BLK, D)), _acc((1, D)), _acc((1, D))],
        out_shape=[pltpu.HBM((BLK, D), F32)] + sums,
        compiler_params=_params(16, dimension_semantics=_seq()),
    )(*_hbm(meta_ext, dh0, g, dg, db))
    return dx, dm, dg, db


def _local_step(x, target, ln_in_g, ln_in_b, b_in, bg2, sinks, gn, g1, b1, g2, b2,
                token, fetch_first, fetch_rest, fetch_ffn, exchange_ffn, ship_ffn, exchange_w_in, ship_w_in):
    row = lambda v: v.reshape(1, -1).astype(F32)
    b_in_p = jnp.pad(row(b_in), ((0, 0), (0, D_IN_P - D_IN)))
    gn4 = row(gn)
    sinks = sinks.reshape(-1).astype(F32)

    h_real = _ln_in_fwd_real(x, row(ln_in_g), row(ln_in_b), token)
    w_in_windows, meta_full, wg2 = fetch_first([h_real])
    meta_ext = jnp.pad(meta_full, ((META_OFF, BLK - CH), (0, 0)))
    wg2_p = jnp.pad(wg2, ((0, LANE - wg2.shape[0]), (0, 0))).astype(BF16)
    h0 = _ln_in_fwd_meta(h_real, meta_ext, row(ln_in_g), row(ln_in_b))
    qs, ks, vs, qg, kg, vg, rg, glr, z, w_in_t = _in_proj(h0, w_in_windows, b_in_p, wg2_p, row(bg2))
    o_s = _swa_fwd(sinks, qs, ks, vs)
    o_gla, st_all = _gla_fwd(qg, kg, vg, z)
    w_out, token = fetch_rest([o_s, o_gla])
    o, pre1, h1 = _post_mix(o_s, o_gla, rg, h0, gn4, w_out, row(g1), row(b1), token)
    wg_t, wu_t, wd = fetch_ffn([pre1])
    a, dgate, dup, dpre2, loss, dg2, db2 = _ffn_fwd_loss_bwd(h1, wg_t, wu_t, wd, target, row(g2), row(b2))
    dpre1, dg1, db1, do_s, do_gla, drg, dgn = _ffn_out_bwd(dpre2, dgate, dup, pre1, wg_t, wu_t, row(g1), w_out, o_gla,
                                                           rg, gn4)
    dwd = _atb(a, dpre2, "dw_down")
    dwg_t = _atb(dgate, h1, "dw_gate")
    dwu_t = _atb(dup, h1, "dw_up")
    token = exchange_ffn(dict(w_out=_atb(o, dpre1, "dw_out"), w_g=dwg_t, w_u=dwu_t, w_d=dwd))
    dqg, dkg, dvg, dz = _gla_bwd(qg, kg, vg, z, do_gla, st_all, token)
    token = ship_ffn([dqg])
    dqs, dks, dvs, dsinks = _swa_bwd(sinks, qs, ks, vs, do_s, token)
    dproj, dh0, db_in_p, dbg2 = _in_bwd(dqs, dks, dvs, dqg, dkg, dvg, drg, dz, dpre1, w_in_t, wg2_p)
    token = exchange_w_in(_atb(dproj, h0, "dw_in", windows=(W_IN_STARTS, W_IN_WIN)))
    dwg2_p = _atb(glr, dz, "dw_gate_lr2", token)
    token = ship_w_in([dwg2_p])
    dx, dmeta_blk, dg_in, db_in_ln = _ln_in_bwd(x, meta_ext, dh0, row(ln_in_g), token)

    small = dict(meta_blk=dmeta_blk, ln_in_g=dg_in, ln_in_b=db_in_ln, ln1_g=dg1, ln1_b=db1, ln2_g=dg2, ln2_b=db2,
                 b_in_p=db_in_p, wg2_p=dwg2_p, bg2=dbg2, sinks=dsinks, gn=dgn, loss=loss)
    return dx, small


HBM = pl.BlockSpec(memory_space=pltpu.HBM)


def _place():
    return lax.axis_index("x"), lax.axis_index("y"), lax.axis_index("c")


def _other_chips(x, y):
    return [(1 - x, y), (x, 1 - y), (1 - x, 1 - y)]


def _dma_sems(n):
    return pltpu.SemaphoreType.DMA((n,))


def _comm_params():
    return pltpu.CompilerParams(has_side_effects=True)


SEM = pl.BlockSpec(memory_space=pltpu.SEMAPHORE)


PER_ARRAY = dict(gather=3, scatter=3, sibling=N_CHIPS)


def _ici_copies(kind, landing, srcs, lands, send_sems, recv_sems):
    x, y, c = _place()
    mine = 2 * x + y
    copies = []
    for a in range(len(srcs)):
        if kind == "sibling":
            for s in range(N_CHIPS):
                copies.append(pltpu.make_async_remote_copy(
                    srcs[a].at[s, 1 - c], lands[a].at[s], send_sems.at[N_CHIPS * a + s], recv_sems.at[N_CHIPS * a + s],
                    device_id=(x, y, 1 - c), device_id_type=MESH))
            continue
        for j, (px, py) in enumerate(_other_chips(x, y)):
            slab = 2 * px + py if landing else mine
            if kind == "gather":
                src, dst = srcs[a].at[c], lands[a].at[slab, c]
            else:
                src, dst = srcs[a].at[2 * px + py], lands[a].at[slab]
            copies.append(pltpu.make_async_remote_copy(src, dst, send_sems.at[3 * a + j], recv_sems.at[3 * a + j],
                                                       device_id=(px, py, c), device_id_type=MESH))
    return copies


def _split_params():
    return pltpu.CompilerParams(has_side_effects=pltpu.SideEffectType.DATAFLOW_SIDE_EFFECTING)


def _ici_start(kind, srcs, land_shapes, after, name):
    n = len(srcs)
    lands = [pltpu.with_memory_space_constraint(lax.empty(s, a.dtype), pltpu.HBM) for s, a in zip(land_shapes, srcs)]

    def body(*refs):
        outs = refs[2 * n + len(after):]
        for cp in _ici_copies(kind, False, refs[:n], refs[n:2 * n], outs[0], outs[1]):
            cp.start()
        outs[-1][...] = jnp.zeros(TOKEN, F32)

    outs = pl.pallas_call(
        body, name=name, in_specs=[HBM] * (2 * n) + [pl.BlockSpec(memory_space=pl.ANY)] * len(after),
        out_specs=[SEM, SEM] + [HBM] * (2 * n) + [pl.BlockSpec(memory_space=pltpu.VMEM)],
        out_shape=[_dma_sems(PER_ARRAY[kind] * n)] * 2 + [pltpu.HBM(a.shape, a.dtype) for a in list(srcs) + lands]
        + [jax.ShapeDtypeStruct(TOKEN, F32)],
        input_output_aliases={i: 2 + i for i in range(2 * n)},
        compiler_params=_split_params(),
    )(*_hbm(*srcs), *lands, *after)
    return outs[:-1], outs[-1]


def _ici_wait(kind, handle, after, name):
    n = (len(handle) - 2) // 2

    def body(*refs):
        for cp in _ici_copies(kind, True, refs[:n], refs[n:2 * n], refs[2 * n], refs[2 * n + 1]):
            cp.wait_send()
            cp.wait_recv()

    outs = pl.pallas_call(
        body, name=name, in_specs=[HBM] * (2 * n) + [SEM, SEM] + [pl.BlockSpec(memory_space=pl.ANY)] * len(after),
        out_specs=[HBM] * (2 * n), out_shape=[pltpu.HBM(a.shape, a.dtype) for a in handle[2:]],
        input_output_aliases={i: i for i in range(2 * n)},
        compiler_params=_split_params(),
    )(*handle[2:], handle[0], handle[1], *after)
    return list(outs[:n]), list(outs[n:])


def _forward_copies(landing, arrs, send_sems, recv_sems):
    x, y, c = _place()
    copies = []
    for a in range(len(arrs)):
        for j, (px, py) in enumerate(_other_chips(x, y)):
            half = 1 - c if landing else c
            copies.append(pltpu.make_async_remote_copy(
                arrs[a].at[2 * px + py, c], arrs[a].at[2 * px + py, half], send_sems.at[3 * a + j],
                recv_sems.at[3 * a + j], device_id=(x, y, 1 - c), device_id_type=MESH))
    return copies


def _gather_wait_forward(handle, groups, after, name):
    n = (len(handle) - 2) // 2
    assert sum(groups) == n

    def body(*refs):
        outs = refs[2 * n + 2 + len(after):]
        lands, sems = outs[n:2 * n], outs[2 * n:-1]
        arrivals = _ici_copies("gather", True, refs[:n], refs[n:2 * n], refs[2 * n], refs[2 * n + 1])
        sends, first = [], 0
        for g, count in enumerate(groups):
            sends += _forward_copies(False, lands[first:first + count], sems[2 * g], sems[2 * g + 1])
            first += count
        for cp, send in zip(arrivals, sends):
            cp.wait_recv()
            send.start()
        for cp in arrivals:
            cp.wait_send()
        outs[-1][...] = jnp.zeros(TOKEN, F32)

    outs = pl.pallas_call(
        body, name=name, in_specs=[HBM] * (2 * n) + [SEM, SEM] + [pl.BlockSpec(memory_space=pl.ANY)] * len(after),
        out_specs=[HBM] * (2 * n) + [SEM] * (2 * len(groups)) + [pl.BlockSpec(memory_space=pltpu.VMEM)],
        out_shape=[pltpu.HBM(a.shape, a.dtype) for a in handle[2:]]
        + [_dma_sems(3 * count) for count in groups for _ in range(2)] + [jax.ShapeDtypeStruct(TOKEN, F32)],
        input_output_aliases={i: i for i in range(2 * n)},
        compiler_params=_split_params(),
    )(*handle[2:], handle[0], handle[1], *after)
    lands, sems, handles, first = outs[n:2 * n], outs[2 * n:-1], [], 0
    for g, count in enumerate(groups):
        handles.append([sems[2 * g], sems[2 * g + 1], *lands[first:first + count]])
        first += count
    return list(outs[:n]), handles, outs[-1]


def _forward_wait(handle, after, name):
    n = len(handle) - 2

    def body(*refs):
        for cp in _forward_copies(True, refs[:n], refs[n], refs[n + 1]):
            cp.wait_send()
            cp.wait_recv()

    return list(pl.pallas_call(
        body, name=name, in_specs=[HBM] * n + [SEM, SEM] + [pl.BlockSpec(memory_space=pl.ANY)] * len(after),
        out_specs=[HBM] * n, out_shape=[pltpu.HBM(a.shape, a.dtype) for a in handle[2:]],
        input_output_aliases={i: i for i in range(n)},
        compiler_params=_split_params(),
    )(*handle[2:], handle[0], handle[1], *after))


def _add_halves(core, grads, recvs, dtypes, name):
    n = len(grads)
    heights = [g.shape[2] for g in grads]

    def body(c_ref, *refs):
        for a in range(n):
            refs[2 * n + a][...] = (refs[2 * a][0] + refs[2 * a + 1][...]).astype(dtypes[a])

    slab = lambda h: pl.BlockSpec((1, h, D), lambda s, c: (s, 0, 0))
    mine = lambda h: pl.BlockSpec((1, 1, h, D), lambda s, c: (s, c[0], 0, 0))
    return pl.pallas_call(
        body, name=name,
        grid_spec=pltpu.PrefetchScalarGridSpec(
            num_scalar_prefetch=1, grid=(N_CHIPS,),
            in_specs=[spec(h) for h in heights for spec in (mine, slab)], out_specs=[slab(h) for h in heights]),
        out_shape=[pltpu.HBM((N_CHIPS, h, D), dt) for h, dt in zip(heights, dtypes)],
        compiler_params=_params(32, dimension_semantics=_seq()),
    )(core, *_hbm(*[a for pair in zip(grads, recvs) for a in pair]))


N_DEVICES = 2 * N_CHIPS
PEER_FLIPS = [(dx, dy, dc) for dx in (0, 1) for dy in (0, 1) for dc in (0, 1)][1:]


def _small_copies(landing, p_ref, out_ref, send_sems, recv_sems):
    x, y, c = _place()
    flip = lambda v, d: 1 - v if d else v
    copies = []
    for k, flips in enumerate(PEER_FLIPS):
        px, py, pc = (flip(v, d) for v, d in zip((x, y, c), flips))
        slab = 4 * px + 2 * py + pc if landing else 4 * x + 2 * y + c
        copies.append(pltpu.make_async_remote_copy(p_ref, out_ref.at[slab], send_sems.at[k], recv_sems.at[k],
                                                   device_id=(px, py, pc), device_id_type=MESH))
    return copies


def _small_wait(handle, after):
    def body(p_ref, land_ref, send_sems, recv_sems, *rest):
        for cp in _small_copies(True, p_ref, land_ref, send_sems, recv_sems):
            cp.wait_send()
            cp.wait_recv()

    return pl.pallas_call(
        body, name="small_exchange_wait", in_specs=[HBM, HBM, SEM, SEM] + [pl.BlockSpec(memory_space=pl.ANY)] * len(after),
        out_specs=[HBM, HBM], out_shape=[pltpu.HBM(a.shape, F32) for a in handle[2:]],
        input_output_aliases={0: 0, 1: 1},
        compiler_params=_split_params(),
    )(handle[2], handle[3], handle[0], handle[1], *after)


def _sum_chips(slots, firsts, rests, after, name):
    n = len(firsts)

    def body(i_ref, *refs):
        outs = refs[4 * n + len(after):]
        for a in range(n):
            first, r1, r2, r3 = refs[4 * a:4 * a + 4]
            outs[a][...] = ((first[...].astype(F32) + r1[...].astype(F32)) + r2[...].astype(F32)) + r3[...].astype(F32)

    slab = lambda h, k: pl.BlockSpec((1, h, D), lambda i, ix: (ix[k], 0, 0))
    heights = [f.shape[1] for f in firsts]
    return pl.pallas_call(
        body, name=name,
        grid_spec=pltpu.PrefetchScalarGridSpec(
            num_scalar_prefetch=1, grid=(1,),
            in_specs=[slab(h, k) for h in heights for k in range(4)] + [pl.BlockSpec(memory_space=pl.ANY)] * len(after),
            out_specs=[slab(h, 4) for h in heights]),
        out_shape=[pltpu.HBM((2, h, D), F32) for h in heights],
        compiler_params=_params(48, dimension_semantics=_seq()),
    )(slots, *_hbm(*[a for f, r in zip(firsts, rests) for a in (f, r, r, r)]), *after)


def _join_copies(landing, arrs, send_sems, recv_sems):
    x, y, c = _place()
    slab = 1 - c if landing else c
    return [pltpu.make_async_remote_copy(arr.at[slab], arr.at[slab], send_sems.at[a], recv_sems.at[a],
                                         device_id=(x, y, 1 - c), device_id_type=MESH) for a, arr in enumerate(arrs)]


def _join_halves(halves, name):
    n = len(halves)

    def body(*refs):
        outs = refs[n:2 * n]
        send_sems, recv_sems = refs[2 * n:]
        sends = _join_copies(False, outs, send_sems, recv_sems)
        for cp in sends:
            cp.start()
        for cp in _join_copies(True, outs, send_sems, recv_sems):
            cp.wait_recv()
        for cp in sends:
            cp.wait_send()

    return list(pl.pallas_call(
        body, name=name, in_specs=[HBM] * n, out_specs=[HBM] * n,
        out_shape=[pltpu.HBM(h.shape, F32) for h in halves],
        input_output_aliases={a: a for a in range(n)},
        scratch_shapes=[_dma_sems(n)] * 2,
        compiler_params=_comm_params(),
    )(*_hbm(*halves)))


def _join_small_start(halves, pack, name):
    n, peers = len(halves), len(PEER_FLIPS)
    land = pltpu.with_memory_space_constraint(lax.empty((N_DEVICES,) + pack.shape, F32), pltpu.HBM)

    def body(*refs):
        outs = refs[n + 2:]
        for cp in _join_copies(False, refs[:n], outs[0], outs[1]):
            cp.start()
        for cp in _small_copies(False, refs[n], refs[n + 1], outs[2], outs[3]):
            cp.start()
        outs[-1][...] = jnp.zeros(TOKEN, F32)

    outs = pl.pallas_call(
        body, name=name, in_specs=[HBM] * (n + 2),
        out_specs=[SEM] * 4 + [HBM] * (n + 2) + [pl.BlockSpec(memory_space=pltpu.VMEM)],
        out_shape=[_dma_sems(n)] * 2 + [_dma_sems(peers)] * 2
        + [pltpu.HBM(a.shape, a.dtype) for a in list(halves) + [pack, land]] + [jax.ShapeDtypeStruct(TOKEN, F32)],
        input_output_aliases={i: 4 + i for i in range(n + 2)},
        compiler_params=_split_params(),
    )(*_hbm(*halves, pack), land)
    return [outs[0], outs[1], *outs[4:4 + n]], [outs[2], outs[3], outs[4 + n], outs[5 + n]], outs[-1]


def _join_wait(handle, after, name):
    n = len(handle) - 2

    def body(*refs):
        for cp in _join_copies(True, refs[:n], refs[n], refs[n + 1]):
            cp.wait_send()
            cp.wait_recv()

    return list(pl.pallas_call(
        body, name=name, in_specs=[HBM] * n + [SEM, SEM] + [pl.BlockSpec(memory_space=pl.ANY)] * len(after),
        out_specs=[HBM] * n, out_shape=[pltpu.HBM(a.shape, a.dtype) for a in handle[2:]],
        input_output_aliases={i: i for i in range(n)},
        compiler_params=_split_params(),
    )(*handle[2:], handle[0], handle[1], *after))


def _chip_partials(grads, fetched, wire_dtypes, name):
    core = lax.axis_index("c").astype(jnp.int32).reshape(1)
    return list(_add_halves(core, grads, fetched, wire_dtypes, name))


def _chip_sums(parts, got, after, name):
    x, y, c = _place()
    others = [2 * px + py for px, py in _other_chips(x, y)]
    own_first = jnp.stack([2 * x + y] + others + [c]).astype(jnp.int32)
    return list(_sum_chips(own_first, parts, got, after, name))


ADAMW_STEPS = 8


def _adamw(params, by_row, chip, window_step):
    n = len(params)
    rows, _, cols = by_row[0].shape
    block = lambda shape: pl.BlockSpec((shape[0] // ADAMW_STEPS, shape[1]), lambda i, c: (i, 0))
    assert all(a.shape[0] % (8 * ADAMW_STEPS) == 0 for p in params for a in p)

    def body(c_ref, *refs):
        w_hbm, g_ref, m_hbm, v_hbm = refs[4 * n:4 * n + 4]
        results, (ins_ref, outs_ref, sems) = refs[8 * n + 4:8 * n + 8], refs[8 * n + 8:]
        loads = [pltpu.make_async_copy(src.at[:, 0, :], ins_ref.at[k], sems.at[k])
                 for k, src in enumerate((w_hbm, m_hbm, v_hbm))]
        stores = [pltpu.make_async_copy(outs_ref.at[k], dst.at[:, 0, :], sems.at[3 + k]) for k, dst in enumerate(results)]
        first = pl.program_id(0) == 0

        @pl.when(first)
        def _():
            for cp in loads:
                cp.start()

        for a in range(n):
            w_ref, a_g_ref, m_ref, v_ref = refs[4 * a:4 * a + 4]
            outs = refs[4 * n + 4 + 4 * a:4 * n + 8 + 4 * a]
            g = a_g_ref[...]
            outs[0][...] = g
            outs[1][...], outs[2][...], outs[3][...] = _adamw_math(w_ref[...], g, m_ref[...], v_ref[...])

        @pl.when(first)
        def _():
            for cp in loads:
                cp.wait()
            for lo in range(0, cols, LANE):
                lanes = slice(lo, lo + LANE)
                g = g_ref[0:rows, lanes]
                for s in range(1, N_CHIPS):
                    g = jnp.where(c_ref[0] == s, g_ref[s * window_step:s * window_step + rows, lanes], g)
                outs_ref[0, :, lanes] = g
                outs_ref[1, :, lanes], outs_ref[2, :, lanes], outs_ref[3, :, lanes] = _adamw_math(
                    ins_ref[0, :, lanes], g, ins_ref[1, :, lanes], ins_ref[2, :, lanes])
            for cp in stores:
                cp.start()

        @pl.when(pl.program_id(0) == ADAMW_STEPS - 1)
        def _():
            for cp in stores:
                cp.wait()

    outs = pl.pallas_call(
        body, name="adamw_matrices",
        grid_spec=pltpu.PrefetchScalarGridSpec(
            num_scalar_prefetch=1, grid=(ADAMW_STEPS,),
            in_specs=[block(a.shape) for p in params for a in p] + [HBM, _const(by_row[1].shape), HBM, HBM],
            out_specs=[block(p[0].shape) for p in params for _ in range(4)] + [HBM] * 4,
            scratch_shapes=[pltpu.VMEM((3, rows, cols), F32), pltpu.VMEM((4, rows, cols), F32), _dma_sems(7)]),
        out_shape=[pltpu.HBM(p[0].shape, F32) for p in params for _ in range(4)] + [pltpu.HBM((rows, 1, cols), F32)] * 4,
        compiler_params=_params(48, dimension_semantics=_seq()),
    )(chip, *_hbm(*[a for p in params for a in p], *by_row))
    return [outs[4 * a:4 * a + 4] for a in range(n)], outs[4 * n:]


def _adamw_math(w, g, m, v):
    nm = ADAM_B1 * m + (1.0 - ADAM_B1) * g
    nv = ADAM_B2 * v + (1.0 - ADAM_B2) * (g * g)
    m_hat = nm / (1.0 - ADAM_B1 ** ADAM_STEP)
    v_hat = nv / (1.0 - ADAM_B2 ** ADAM_STEP)
    return -ADAM_LR * (m_hat / (jnp.sqrt(v_hat) + ADAM_EPS) + ADAM_WD * w), nm, nv


SMALL = (("meta_tokens", (N_META, D // N_CHIPS)), ("ln_in_g", (1, D)), ("ln_in_b", (1, D)), ("b_in", (1, D_IN)),
         ("w_gate_lr2", (GATE_RANK, GLA_HEADS * DK // N_CHIPS)), ("b_gate_lr2", (1, GLA_HEADS * DK)),
         ("attn_sinks", (1, SWA_HEADS)),
         ("gla_norm_g", (1, DV)), ("ln1_g", (1, D)), ("ln1_b", (1, D)), ("ln2_g", (1, D)), ("ln2_b", (1, D)))
ROW_META, ROW_B_IN, ROW_TAIL, ROW_WG2 = 0, 22, 25, 32
ROW_LN = dict(ln_in_g=16, ln_in_b=17, ln1_g=18, ln1_b=19, ln2_g=20, ln2_b=21)
TAIL_BG2, TAIL_SINKS, TAIL_GN, TAIL_LOSS = 0, 256, 256 + SWA_HEADS, 256 + SWA_HEADS + DV


def _adamw_small(place, packs, own, params):
    n = len(SMALL)

    def body(place_ref, packs_ref, own_ref, *refs):
        ins, outs, p_ref = refs[:3 * n], refs[3 * n:-1], refs[-1]
        me, c = place_ref[0], place_ref[1]
        total = jnp.where(me == 0, own_ref[...], packs_ref[0])
        for i in range(1, N_DEVICES):
            total = total + jnp.where(me == i, own_ref[...], packs_ref[i])
        p_ref[...] = total
        outs[4 * n][...] = total[ROW_TAIL:ROW_TAIL + 1, TAIL_LOSS:TAIL_LOSS + 1]

        def mine(width, rows):
            part = lambda s: p_ref[rows, s * width:(s + 1) * width]
            return jnp.where(c == 0, part(0), jnp.where(c == 1, part(1), jnp.where(c == 2, part(2), part(3))))

        tail = lambda lo, width: p_ref[ROW_TAIL:ROW_TAIL + 1, lo:lo + width]
        grads = dict(
            meta_tokens=mine(D // N_CHIPS, slice(ROW_META, ROW_META + N_META)),
            b_in=jnp.concatenate([p_ref[ROW_B_IN:ROW_B_IN + 1, :], p_ref[ROW_B_IN + 1:ROW_B_IN + 2, :],
                                  p_ref[ROW_B_IN + 2:ROW_B_IN + 3, 0:D_IN - 2 * D]], axis=1),
            w_gate_lr2=mine(256 // N_CHIPS, slice(ROW_WG2, ROW_WG2 + 16)),
            b_gate_lr2=tail(TAIL_BG2, 256), attn_sinks=tail(TAIL_SINKS, SWA_HEADS), gla_norm_g=tail(TAIL_GN, DV),
            **{k: p_ref[r:r + 1, :] for k, r in ROW_LN.items()})
        for i, (name, _) in enumerate(SMALL):
            g = grads[name]
            outs[4 * i][...] = g
            outs[4 * i + 1][...], outs[4 * i + 2][...], outs[4 * i + 3][...] = _adamw_math(
                ins[3 * i][...], g, ins[3 * i + 1][...], ins[3 * i + 2][...])

    whole = lambda shape: pl.BlockSpec(shape, lambda i, c: (0,) * len(shape))
    outs = pl.pallas_call(
        body, name="adamw_small",
        grid_spec=pltpu.PrefetchScalarGridSpec(
            num_scalar_prefetch=1, grid=(1,),
            in_specs=[whole(packs.shape), whole(own.shape)] + [whole(s) for _, s in SMALL for _ in range(3)],
            out_specs=[whole(s) for _, s in SMALL for _ in range(4)] + [whole((1, 1))],
            scratch_shapes=[pltpu.VMEM(own.shape, F32)]),
        out_shape=[pltpu.HBM(s, F32) for _, s in SMALL for _ in range(4)] + [pltpu.HBM((1, 1), F32)],
        compiler_params=_params(16, dimension_semantics=_seq()),
    )(place, *_hbm(packs, own, *[a for p in params for a in p]))
    return [outs[4 * i:4 * i + 4] for i in range(n)], outs[4 * n]


def _small_pack(gr):
    names = ["meta_blk"] + list(ROW_LN) + ["b_in_p", "wg2_p", "bg2", "sinks", "gn", "loss"]
    gate_w = GLA_HEADS * DK

    def body(*refs):
        src, out = dict(zip(names, refs)), refs[-1]
        out[...] = jnp.zeros_like(out)
        out[ROW_META:ROW_META + N_META, :] = src["meta_blk"][META_OFF:CH, :]
        for k, r in ROW_LN.items():
            out[r:r + 1, :] = src[k][...]
        for j in range(-(-D_IN // D)):
            width = min(D, D_IN - j * D)
            out[ROW_B_IN + j:ROW_B_IN + j + 1, 0:width] = src["b_in_p"][:, j * D:j * D + width]
        tail = slice(ROW_TAIL, ROW_TAIL + 1)
        out[tail, TAIL_BG2:TAIL_BG2 + gate_w] = src["bg2"][...]
        out[tail, TAIL_SINKS:TAIL_SINKS + SWA_HEADS] = src["sinks"][:, 0:SWA_HEADS]
        out[tail, TAIL_GN:TAIL_GN + DV] = src["gn"][...]
        out[tail, TAIL_LOSS:TAIL_LOSS + 1] = src["loss"][:, 0:1]
        out[ROW_WG2:ROW_WG2 + GATE_RANK, 0:gate_w] = src["wg2_p"][0:GATE_RANK, :]

    arrays = [gr[k] for k in names]
    return pl.pallas_call(
        body, name="small_pack", grid=(1,),
        in_specs=[_acc(a.shape) for a in arrays], out_specs=_acc((SMALL_ROWS, D)),
        out_shape=pltpu.HBM((SMALL_ROWS, D), F32),
        compiler_params=_params(16, dimension_semantics=_seq()),
    )(*_hbm(*arrays))


BIG = ("w_in", "w_out", "w_g", "w_u", "w_d")


def kernel(x, meta_tokens, ln_in_g, ln_in_b, w_in, b_in, w_gate_lr2, b_gate_lr2, attn_sinks, gla_norm_g, w_out, ln1_g, ln1_b, w_ffn_gate, w_ffn_up, w_ffn_down, ln2_g, ln2_b, loss_target, m_meta_tokens, m_ln_in_g, m_ln_in_b, m_w_in, m_b_in, m_w_gate_lr2, m_b_gate_lr2, m_attn_sinks, m_gla_norm_g, m_w_out, m_ln1_g, m_ln1_b, m_w_ffn_gate, m_w_ffn_up, m_w_ffn_down, m_ln2_g, m_ln2_b, v_meta_tokens, v_ln_in_g, v_ln_in_b, v_w_in, v_b_in, v_w_gate_lr2, v_b_gate_lr2, v_attn_sinks, v_gla_norm_g, v_w_out, v_ln1_g, v_ln1_b, v_w_ffn_gate, v_w_ffn_up, v_w_ffn_down, v_ln2_g, v_ln2_b):
    chip = 2 * lax.axis_index("x") + lax.axis_index("y")

    halves = lambda a: a.reshape(2, a.shape[0] // 2, a.shape[1])
    r_in = SHARD_ROWS["w_in"]
    first = [halves(a) for a in (jnp.pad(w_in[0].T.astype(BF16), ((0, W_IN_WIN - r_in), (0, 0))), meta_tokens,
                                 w_gate_lr2[0])]
    rest = [halves(a) for a in (w_out[0].astype(BF16), w_ffn_gate[0].T.astype(BF16), w_ffn_up[0].T.astype(BF16),
                                w_ffn_down[0].astype(BF16))]
    lands = lambda arrs: [(N_CHIPS,) + a.shape for a in arrs]
    first_handle, first_token = _ici_start("gather", first, lands(first), [], "gather_first_start")
    rest_handle, token = _ici_start("gather", rest, lands(rest), [first_token], "gather_rest_start")
    own_slab = lambda got, shards: [lax.dynamic_update_index_in_dim(g, s, chip, axis=0) for g, s in zip(got, shards)]
    fetching = {}

    def fetch_first(after):
        shards, (forwarding,), _ = _gather_wait_forward(first_handle, [len(first)], after, "gather_first_wait")
        g_in, g_meta, g_wg2 = own_slab(_forward_wait(forwarding, [], "gather_first_forward_wait"), shards)
        w_in_windows = g_in.reshape(N_CHIPS, W_IN_WIN, D)
        meta_full = jnp.concatenate([g_meta[s].reshape(N_META, -1) for s in range(N_CHIPS)], axis=1)
        wg2_full = jnp.concatenate([g_wg2[s].reshape(w_gate_lr2.shape[1], -1) for s in range(N_CHIPS)], axis=1)
        return w_in_windows, meta_full, wg2_full

    def fetch_rest(after):
        shards, (w_out_forwarding, fetching["handle"]), forward_token = _gather_wait_forward(
            rest_handle, [1, len(rest) - 1], after, "gather_rest_wait")
        g_out, = own_slab(_forward_wait(w_out_forwarding, [], "gather_w_out_forward_wait"), shards[:1])
        fetching["shards"] = shards[1:]
        return g_out.reshape(-1, D), forward_token

    def fetch_ffn(after):
        got = _forward_wait(fetching["handle"], after, "gather_ffn_forward_wait")
        return [g.reshape(-1, D) for g in own_slab(got, fetching["shards"])]

    sent = {}
    split = lambda grads: [g.reshape(N_CHIPS, 2, -1, D) for g in grads]

    def exchange(key, grads):
        grads = split(grads)
        sent[key + "_halves"], exchange_token = _ici_start(
            "sibling", grads, [(N_CHIPS,) + a.shape[2:] for a in grads], [], "sibling_" + key + "_start")
        return exchange_token

    def ship(key, after):
        grads, fetched = _ici_wait("sibling", sent[key + "_halves"], after, "sibling_" + key + "_wait")
        parts = _chip_partials(grads, fetched, [BF16] * len(grads), "add_halves_" + key)
        sent[key], ship_token = _ici_start("scatter", parts, [p.shape for p in parts], [], "scatter_" + key + "_start")
        return ship_token

    dx, gr = _local_step(
        x[0], loss_target[0], ln_in_g, ln_in_b, b_in[0], b_gate_lr2[0], attn_sinks[0], gla_norm_g[0], ln1_g[0],
        ln1_b[0], ln2_g[0], ln2_b[0], token, fetch_first, fetch_rest, fetch_ffn,
        lambda g: exchange("ffn", [g[k] for k in BIG[1:]]), lambda after: ship("ffn", after),
        lambda g: exchange("w_in", [g]), lambda after: ship("w_in", after))
    ffn_parts, ffn_got = _ici_wait("scatter", sent["ffn"], [dx], "scatter_ffn_wait")
    join_handle, small_handle, token = _join_small_start(
        _chip_sums(ffn_parts, ffn_got, [], "sum_chips_ffn"), _small_pack(gr), "join_ffn_small_start")
    w_in_parts, w_in_got = _ici_wait("scatter", sent["w_in"], [token], "scatter_w_in_wait")
    w_in_joined = _join_halves(_chip_sums(w_in_parts, w_in_got, [], "sum_chips_w_in"), "join_w_in")
    red = [f.reshape(2 * f.shape[1], D) for f in w_in_joined + _join_wait(join_handle, w_in_joined, "join_ffn_wait")]

    big_g = dict(zip(BIG, red))
    weights = dict(meta_tokens=meta_tokens, ln_in_g=ln_in_g, ln_in_b=ln_in_b, w_in=w_in, b_in=b_in,
                   w_gate_lr2=w_gate_lr2, b_gate_lr2=b_gate_lr2, attn_sinks=attn_sinks, gla_norm_g=gla_norm_g,
                   w_out=w_out, ln1_g=ln1_g, ln1_b=ln1_b, w_ffn_gate=w_ffn_gate, w_ffn_up=w_ffn_up,
                   w_ffn_down=w_ffn_down, ln2_g=ln2_g, ln2_b=ln2_b)
    m_in = dict(meta_tokens=m_meta_tokens, ln_in_g=m_ln_in_g, ln_in_b=m_ln_in_b, w_in=m_w_in, b_in=m_b_in,
                w_gate_lr2=m_w_gate_lr2, b_gate_lr2=m_b_gate_lr2, attn_sinks=m_attn_sinks, gla_norm_g=m_gla_norm_g,
                w_out=m_w_out, ln1_g=m_ln1_g, ln1_b=m_ln1_b, w_ffn_gate=m_w_ffn_gate, w_ffn_up=m_w_ffn_up,
                w_ffn_down=m_w_ffn_down, ln2_g=m_ln2_g, ln2_b=m_ln2_b)
    v_in = dict(meta_tokens=v_meta_tokens, ln_in_g=v_ln_in_g, ln_in_b=v_ln_in_b, w_in=v_w_in, b_in=v_b_in,
                w_gate_lr2=v_w_gate_lr2, b_gate_lr2=v_b_gate_lr2, attn_sinks=v_attn_sinks, gla_norm_g=v_gla_norm_g,
                w_out=v_w_out, ln1_g=v_ln1_g, ln1_b=v_ln1_b, w_ffn_gate=v_w_ffn_gate, w_ffn_up=v_w_ffn_up,
                w_ffn_down=v_w_ffn_down, ln2_g=v_ln2_g, ln2_b=v_ln2_b)
    names = list(weights)
    big_names = ("w_in", "w_out", "w_ffn_gate", "w_ffn_up", "w_ffn_down")

    grads, delta, new_m, new_v = {}, {}, {}, {}
    flips = [(lambda a: a.T) if kk in ("w_g", "w_u") else (lambda a: a) for kk in BIG[1:]]
    by_row = lambda a: jnp.transpose(a, (2, 0, 1))
    updated, updated_w_in = _adamw(
        [(flip(weights[k][0]), big_g[kk], flip(m_in[k][0]), flip(v_in[k][0]))
         for k, kk, flip in zip(big_names[1:], BIG[1:], flips)],
        (by_row(w_in), big_g["w_in"], by_row(m_w_in), by_row(v_w_in)), chip.astype(jnp.int32).reshape(1), r_in % BF16_ROWS)
    for k, flip, results in zip(big_names[1:], flips, updated):
        grads[k], delta[k], new_m[k], new_v[k] = (flip(t)[None] for t in results)
    grads["w_in"], delta["w_in"], new_m["w_in"], new_v["w_in"] = (jnp.transpose(t, (1, 2, 0)) for t in updated_w_in)
    small_in = [tuple(src[k].reshape(shape) for src in (weights, m_in, v_in)) for k, shape in SMALL]
    place = jnp.stack([2 * chip + lax.axis_index("c"), chip]).astype(jnp.int32)
    small_own, small_all = _small_wait(small_handle, [updated[0][0]])
    small_out, loss = _adamw_small(place, small_all, small_own, small_in)
    for (k, _), results in zip(SMALL, small_out):
        grads[k], delta[k], new_m[k], new_v[k] = (r.reshape(weights[k].shape) for r in results)

    return (loss.reshape(()), dx[None], *[grads[k] for k in names], *[delta[k] for k in names], *[new_m[k] for k in names],
            *[new_v[k] for k in names])
```

```python
import jax
import jax.numpy as jnp
from jax import lax
from jax.experimental import pallas as pl
from jax.experimental.pallas import tpu as pltpu

F32 = jnp.float32
BF16 = jnp.bfloat16
MESH = pl.DeviceIdType.MESH

D = 1024
SEQ = 4096
N_META = 16
SWA_HEADS, SWA_KV_HEADS, DH = 8, 2, 64
WINDOW = 128
GLA_HEADS, DK, DV = 4, 64, 128
GLA_TAU = 16.0
CH = 64
D_FF = 2816
D_IN = 2320
LN_EPS = 1e-5
RMS_EPS = 1e-6
ALPHA = 2.0 ** 0.25
NEG = -1e30
ADAM_LR, ADAM_B1, ADAM_B2, ADAM_EPS, ADAM_WD, ADAM_STEP = 0.001, 0.9, 0.999, 1e-8, 0.01, 10
O_QS, O_KS, O_VS, O_QG, O_KG, O_VG, O_RG, O_LR = 0, 512, 640, 768, 1024, 1280, 1792, 2304

LANE = 128
BLK = WINDOW
GATE_RANK = 16
D_IN_P = D_IN + LANE - GATE_RANK
META_OFF = CH - N_META
HEAD_POS = (0, 4, 1, 5, 2, 6, 3, 7)
LN_ROWS = 512
TOKEN = (8, LANE)
N_CHIPS = 4
SHARD_ROWS = dict(w_in=D_IN // N_CHIPS, w_out=D // N_CHIPS, w_g=D_FF // N_CHIPS, w_u=D_FF // N_CHIPS,
                  w_d=D_FF // N_CHIPS)
SMALL_ROWS = 48
BF16_ROWS = 16
W_IN_WIN = -(-SHARD_ROWS["w_in"] // (2 * BF16_ROWS)) * 2 * BF16_ROWS
W_IN_STARTS = tuple(s * SHARD_ROWS["w_in"] // BF16_ROWS * BF16_ROWS for s in range(N_CHIPS))
VMEM_CAP_MB = 64
VMEM_SPARE_MB = 6


def _lp():
    return SEQ + BLK


def _row_tile(cap):
    lp = _lp()
    return max(t for t in range(16, cap + 1, 16) if lp % t == 0)


def _params(vmem_mb, **kw):
    assert vmem_mb <= VMEM_CAP_MB - VMEM_SPARE_MB
    return pltpu.CompilerParams(vmem_limit_bytes=vmem_mb << 20, **kw)


def _seq(n=1):
    return ("arbitrary",) * n


def _const(shape):
    return pl.BlockSpec(shape, lambda *_: (0,) * len(shape), pipeline_mode=pl.Buffered(1))


def _acc(shape):
    return pl.BlockSpec(shape, lambda *_: (0,) * len(shape))


def _rows(tm, width):
    return pl.BlockSpec((tm, width), lambda i: (i, 0))


def _dot(a, b):
    return jnp.dot(a.astype(BF16), b.astype(BF16), preferred_element_type=F32)


def _dot_nt(a, b):
    return lax.dot_general(a.astype(BF16), b.astype(BF16), (((1,), (1,)), ((), ())), preferred_element_type=F32)


def _dot_tn(a, b):
    return lax.dot_general(a.astype(BF16), b.astype(BF16), (((0,), (0,)), ((), ())), preferred_element_type=F32)


def _dot_exact(a, b):
    return jnp.dot(a, b, precision=lax.Precision.HIGHEST, preferred_element_type=F32)


def _ln_stats(x):
    mu = jnp.mean(x, axis=-1, keepdims=True)
    xc = x - mu
    rstd = lax.rsqrt(jnp.mean(xc * xc, axis=-1, keepdims=True) + LN_EPS)
    return xc * rstd, rstd


def _ln_bwd(dy, xhat, rstd, g):
    dxh = dy * g
    return rstd * (dxh - jnp.mean(dxh, axis=-1, keepdims=True) - xhat * jnp.mean(dxh * xhat, axis=-1, keepdims=True))


def _sigmoid(x):
    return 1.0 / (1.0 + jnp.exp(-x))


def _iota(shape, dim):
    return lax.broadcasted_iota(jnp.int32, shape, dim)


def _hbm(*arrays):
    return tuple(pltpu.with_memory_space_constraint(a, pltpu.HBM) for a in arrays)


RING = 3


def _ring_fetch(steps, tile, sources, rings, sems, step=None, view=None):
    assert steps >= RING - 1
    step = pl.program_id(0) if step is None else step

    def copy(t, k):
        rows = pl.ds(pl.multiple_of(t * tile, tile), tile)
        src = sources[k].at[rows, :] if view is None else view(k, t)
        return pltpu.make_async_copy(src, rings[k].at[t % RING], sems.at[k, t % RING])

    @pl.when(step == 0)
    def _():
        for t in range(RING - 1):
            for k in range(len(sources)):
                copy(t, k).start()

    @pl.when(step + (RING - 1) < steps)
    def _():
        for k in range(len(sources)):
            copy(step + (RING - 1), k).start()

    for k in range(len(sources)):
        copy(step, k).wait()
    return step % RING


def _ring_scratch(tile, widths, dtypes=None):
    dtypes = dtypes or [F32] * len(widths)
    return ([pltpu.VMEM((RING, tile, w), dt) for w, dt in zip(widths, dtypes)]
            + [pltpu.SemaphoreType.DMA((len(widths), RING))])


def _ln_in_fwd_real(x, g, b, token):
    tr = min(LN_ROWS, SEQ)
    steps = SEQ // tr

    def body(x_hbm, g_ref, b_ref, token_ref, h_ref, x_ring, sems):
        slot = _ring_fetch(steps, tr, [x_hbm], [x_ring], sems)
        xhat, _ = _ln_stats(x_ring[slot])
        h_ref[...] = xhat * g_ref[...] + b_ref[...]

    return pl.pallas_call(
        body, name="ln_in_fwd", grid=(steps,),
        in_specs=[pl.BlockSpec(memory_space=pl.ANY), _const((1, D)), _const((1, D)), _const(TOKEN)],
        out_specs=_rows(tr, D),
        out_shape=pltpu.HBM((_lp(), D), F32),
        scratch_shapes=_ring_scratch(tr, [D]),
        compiler_params=_params(32, dimension_semantics=_seq()),
    )(*_hbm(x, g, b), token)


def _ln_in_fwd_meta(h_real, meta_ext, g, b):
    def meta_body(m_ref, g_ref, b_ref, real_ref, h_ref):
        xhat, _ = _ln_stats(m_ref[...])
        h_ref[...] = xhat * g_ref[...] + b_ref[...]

    return pl.pallas_call(
        meta_body, name="ln_in_fwd_meta", grid=(1,),
        in_specs=[_const((BLK, D)), _const((1, D)), _const((1, D)), pl.BlockSpec(memory_space=pl.ANY)],
        out_specs=pl.BlockSpec((BLK, D), lambda i: (SEQ // BLK, 0)),
        out_shape=pltpu.HBM((_lp(), D), F32),
        input_output_aliases={3: 0},
        compiler_params=_params(16, dimension_semantics=_seq()),
    )(*_hbm(meta_ext, g, b, h_real))


def _in_proj(h0, w_in_windows, b_in_p, wg2_p, bg2):
    tm = _row_tile(384)
    lp = _lp()
    widths = (512, 128, 128, 256, 256, 512, 512, 128)
    offs = (O_QS, O_KS, O_VS, O_QG, O_KG, O_VG, O_RG, O_LR)
    shard = SHARD_ROWS["w_in"]

    def body(h_ref, win_ref, b_ref, wg2_ref, bg2_ref, *outs):
        w_ref = outs[9]

        @pl.when(pl.program_id(0) == 0)
        def _():
            for s in range(N_CHIPS):
                w_ref[shard * s:shard * (s + 1), :] = win_ref[s, 0:shard, :]
            w_ref[D_IN:D_IN_P, :] = jnp.zeros((D_IN_P - D_IN, D), BF16)

        proj = _dot_nt(h_ref[...], w_ref[...]) + b_ref[...]
        for pos, h in enumerate(HEAD_POS):
            outs[0][:, pos * DH:(pos + 1) * DH] = proj[:, O_QS + h * DH:O_QS + (h + 1) * DH]
        for o_ref, off, wd in zip(outs[1:8], offs[1:], widths[1:]):
            o_ref[...] = proj[:, off:off + wd]
        outs[8][...] = _dot(proj[:, O_LR:O_LR + LANE], wg2_ref[...]) + bg2_ref[...]

    return pl.pallas_call(
        body, name="in_proj", grid=(lp // tm,),
        in_specs=[_rows(tm, D), _const(w_in_windows.shape), _const((1, D_IN_P)), _const((LANE, 256)), _const((1, 256))],
        out_specs=[_rows(tm, w) for w in widths] + [_rows(tm, 256), _acc((D_IN_P, D))],
        out_shape=[pltpu.HBM((lp, w), F32) for w in widths] + [pltpu.HBM((lp, 256), F32), pltpu.HBM((D_IN_P, D), BF16)],
        compiler_params=_params(48, dimension_semantics=_seq()),
    )(*_hbm(h0, w_in_windows, b_in_p, wg2_p, bg2))


def _swa_masks(n):
    nb = SEQ // BLK
    is_meta = n == nb
    ri = _iota((BLK, BLK), 0)
    cj = _iota((BLK, BLK), 1)
    meta_col = ((cj >= META_OFF) & (cj < CH)).astype(jnp.int32)
    meta_q = meta_col * ((cj <= ri) & (ri < CH)).astype(jnp.int32)
    valid_m = jnp.where(is_meta, meta_q, meta_col) > 0
    dist_m = jnp.where(is_meta, ri - cj, n * BLK + ri + CH - cj).astype(F32)
    valid_p = jnp.where((n >= 1) & (n < nb), (cj > ri).astype(jnp.int32), 0) > 0
    dist_p = (ri + BLK - cj).astype(F32)
    valid_c = jnp.where(n < nb, (cj <= ri).astype(jnp.int32), 0) > 0
    dist_c = (ri - cj).astype(F32)
    return (dist_m, dist_p, dist_c), (valid_m, valid_p, valid_c)


def _swa_bias(n):
    dists, valids = _swa_masks(n)
    return (jnp.concatenate([-d for d in dists], axis=1),
            jnp.concatenate([jnp.where(v, 0.0, NEG) for v in valids], axis=1))


def _swa_half(ref, pos, scale=1.0):
    col = ref[:, (pos // 2) * LANE:(pos // 2 + 1) * LANE]
    lane = _iota((BLK, LANE), 1)
    mine = lane < DH if pos % 2 == 0 else lane >= DH
    return jnp.where(mine, col * scale, 0.0).astype(BF16)


def _swa_merge(even, odd):
    return jnp.where(_iota((BLK, LANE), 1) < DH, even, odd)


def _swa_softmax(t, sink):
    m = jnp.maximum(jnp.max(t, axis=-1, keepdims=True), sink)
    e = jnp.exp(t - m)
    e_sink = jnp.exp(sink - m)
    inv = 1.0 / (jnp.sum(e, axis=-1, keepdims=True) + e_sink)
    return e * inv, e_sink * inv


def _swa_kv_specs(width):
    nb = SEQ // BLK
    return [pl.BlockSpec((BLK, width), lambda n: (nb, 0)),
            pl.BlockSpec((BLK, width), lambda n: (jnp.clip(n - 1, 0, nb - 1), 0)),
            pl.BlockSpec((BLK, width), lambda n: (jnp.minimum(n, nb), 0))]


def _swa_fwd(sinks, qs, ks, vs):
    nb = SEQ // BLK
    heads = range(SWA_HEADS)

    def body(sink_ref, q_ref, km_ref, kp_ref, kc_ref, vm_ref, vp_ref, vc_ref, o_ref):
        negdist, maskbias = _swa_bias(pl.program_id(0))
        k_all = jnp.concatenate([km_ref[...], kp_ref[...], kc_ref[...]], axis=0).astype(BF16)
        v_all = jnp.concatenate([vm_ref[...], vp_ref[...], vc_ref[...]], axis=0).astype(BF16)
        q = [_swa_half(q_ref, pos, DH ** -0.5) for pos in heads]
        t = [_dot_nt(q[pos], k_all) + (2.0 ** -(HEAD_POS[pos] + 1) * negdist + maskbias) for pos in heads]
        p = [_swa_softmax(t[pos], sink_ref[HEAD_POS[pos]])[0].astype(BF16) for pos in heads]
        o = [_dot(p[pos], v_all) for pos in heads]
        for col in range(SWA_HEADS // 2):
            o_ref[:, col * LANE:(col + 1) * LANE] = _swa_merge(o[2 * col], o[2 * col + 1])

    kvw = SWA_KV_HEADS * DH
    return pl.pallas_call(
        body, name="swa_fwd", grid=(nb + 1,),
        in_specs=[pl.BlockSpec(memory_space=pltpu.SMEM), _rows(BLK, SWA_HEADS * DH)] + _swa_kv_specs(kvw) + _swa_kv_specs(kvw),
        out_specs=_rows(BLK, SWA_HEADS * DH),
        out_shape=pltpu.HBM((_lp(), SWA_HEADS * DH), F32),
        compiler_params=_params(16, dimension_semantics=_seq()),
    )(sinks, *_hbm(qs, ks, ks, ks, vs, vs, vs))


GLA_PER_STEP = BLK // CH


def _gla_block(s):
    nb = SEQ // BLK
    return jnp.where(s == 0, nb, s - 1)


def _gla_rowmask(s):
    ri = _iota((BLK, 1), 0)
    m = jnp.where(s == 0, ((ri >= META_OFF) & (ri < CH)).astype(jnp.int32), 1)
    return (m > 0).astype(F32) + jnp.zeros((BLK, 1), F32)


def _gla_chunk_masks():
    r, c = _iota((BLK, BLK), 0), _iota((BLK, BLK), 1)
    same = ((r < CH) & (c < CH)) | ((r >= CH) & (c >= CH))
    return same & (r >= c), same & (r <= c), same


def _gla_decay(z, rmask):
    log_g = (jnp.minimum(z, 0.0) - jnp.log1p(jnp.exp(-jnp.abs(z)))) * (rmask / GLA_TAU)
    lower, _, same = _gla_chunk_masks()
    return _dot_exact(lower.astype(F32), log_g), _dot_exact(same.astype(F32), log_g)


def _gla_slices(c, h):
    return slice(c * CH, (c + 1) * CH), slice(h * DK, (h + 1) * DK), slice(h * DV, (h + 1) * DV)


def _gla_fwd(qg, kg, vg, z):
    steps = SEQ // BLK + 1
    kw, vw = GLA_HEADS * DK, GLA_HEADS * DV
    pairs = [(c, h) for c in range(GLA_PER_STEP) for h in range(GLA_HEADS)]

    def body(q_ref, k_ref, v_ref, z_ref, o_ref, st_ref, st):
        s = pl.program_id(0)

        @pl.when(s == 0)
        def _():
            st[...] = jnp.zeros_like(st)

        rmask = _gla_rowmask(s)
        b, b_last = _gla_decay(z_ref[...], rmask)
        q = q_ref[...] * (rmask * DK ** -0.5)
        k = k_ref[...] * rmask
        v = v_ref[...] * rmask
        qe = q * jnp.exp(b)
        ke = k * jnp.exp(-b)
        kd = k * jnp.exp(b_last - b)
        e_last = jnp.exp(b_last)
        causal = _iota((CH, CH), 0) >= _iota((CH, CH), 1)
        a, upd, intra = {}, {}, {}
        for c, h in pairs:
            rows, ks, vs_ = _gla_slices(c, h)
            a[c, h] = jnp.where(causal, _dot_nt(qe[rows, ks], ke[rows, ks]), 0.0)
            upd[c, h] = _dot_tn(v[rows, vs_], kd[rows, ks])
        for c, h in pairs:
            rows, ks, vs_ = _gla_slices(c, h)
            intra[c, h] = _dot(a[c, h], v[rows, vs_])
        state = st[...]
        for c in range(GLA_PER_STEP):
            st_ref[0, c] = state
            for h in range(GLA_HEADS):
                rows, ks, vs_ = _gla_slices(c, h)
                o_ref[rows, vs_] = intra[c, h] + _dot_nt(qe[rows, ks], state[:, ks])
            state = state * e_last[c * CH:c * CH + 1] + jnp.concatenate([upd[c, h] for h in range(GLA_HEADS)], axis=1)
        st[...] = state

    blk = lambda w: pl.BlockSpec((BLK, w), lambda s: (_gla_block(s), 0))
    return pl.pallas_call(
        body, name="gla_fwd", grid=(steps,),
        in_specs=[blk(kw), blk(kw), blk(vw), blk(kw)],
        out_specs=[blk(vw), pl.BlockSpec((1, GLA_PER_STEP, DV, kw), lambda s: (s, 0, 0, 0))],
        out_shape=[pltpu.HBM((_lp(), vw), F32), pltpu.HBM((steps, GLA_PER_STEP, DV, kw), F32)],
        scratch_shapes=[pltpu.VMEM((DV, kw), F32)],
        compiler_params=_params(16, dimension_semantics=_seq()),
    )(*_hbm(qg, kg, vg, z))


def _post_mix(o_s, o_gla, r_g, h0, gn4, w_out, g1, b1, token):
    tm = _row_tile(384)
    lp = _lp()

    def body(os_hbm, og_hbm, r_hbm, h0_hbm, gn_ref, w_ref, g_ref, b_ref, token_ref, o_ref, pre_ref, h1_ref, *scratch):
        slot = _ring_fetch(lp // tm, tm, [os_hbm, og_hbm, r_hbm, h0_hbm], scratch[:4], scratch[4])
        os_ref, og_ref, r_ref, h0_ref = (ring.at[slot] for ring in scratch[:4])
        for pos, h in enumerate(HEAD_POS):
            o_ref[:, h * DH:(h + 1) * DH] = os_ref[:, pos * DH:(pos + 1) * DH].astype(BF16)
        for h in range(GLA_HEADS):
            hs = slice(h * DV, (h + 1) * DV)
            xg = og_ref[:, hs]
            n = xg * lax.rsqrt(jnp.mean(xg * xg, axis=-1, keepdims=True) + RMS_EPS) * gn_ref[...]
            r = r_ref[:, hs]
            o_ref[:, 512 + h * DV:512 + (h + 1) * DV] = (n * (r * _sigmoid(r))).astype(BF16)
        pre = ALPHA * h0_ref[...] + _dot(o_ref[...], w_ref[...])
        pre_ref[...] = pre
        xhat, _ = _ln_stats(pre)
        h1_ref[...] = xhat * g_ref[...] + b_ref[...]

    return pl.pallas_call(
        body, name="post_mix", grid=(lp // tm,),
        in_specs=[pl.BlockSpec(memory_space=pl.ANY)] * 4 + [_const((1, DV)), _const((D, D)),
                                                              _const((1, D)), _const((1, D)), _const(TOKEN)],
        out_specs=[_rows(tm, D), _rows(tm, D), _rows(tm, D)],
        out_shape=[pltpu.HBM((lp, D), BF16), pltpu.HBM((lp, D), F32),
                   pltpu.HBM((lp, D), F32)],
        scratch_shapes=_ring_scratch(tm, [512, 512, 512, D]),
        compiler_params=_params(40, dimension_semantics=_seq()),
    )(*_hbm(o_s, o_gla, r_g, h0, gn4, w_out, g1, b1), token)


def _ffn_fwd_loss_bwd(h1, wg_t, wu_t, wd, target, g2, b2):
    lp = _lp()
    tm = max(t for t in range(BLK, 384 + 1, BLK) if lp % t == 0)
    steps = lp // tm
    last_blk = SEQ // BLK - 1
    half = D_FF // 2
    n_t = tm // BLK

    def body(*refs):
        h_ref, wg_ref, wu_ref, wd_ref = refs[:4]
        t_refs = refs[4:4 + n_t]
        g2_ref, b2_ref, a_ref, dgate_ref, dup_ref, dp_ref, loss_ref, dg_ref, db_ref, g_s, u_s, acc = refs[4 + n_t:]
        i = pl.program_id(0)

        @pl.when(i == 0)
        def _():
            acc[...] = jnp.zeros_like(acc)
            dg_ref[...] = jnp.zeros_like(dg_ref)
            db_ref[...] = jnp.zeros_like(db_ref)

        h = h_ref[...]
        hb = h.astype(BF16)
        pre = ALPHA * h
        for j in range(2):
            cols = slice(j * half, (j + 1) * half)
            g = _dot_nt(hb, wg_ref[cols, :])
            u = _dot_nt(hb, wu_ref[cols, :])
            g_s[:, cols] = g
            u_s[:, cols] = u
            pre = pre + _dot(g * _sigmoid(g) * u, wd_ref[cols, :])
        xhat, rstd = _ln_stats(pre)
        real = i * tm + _iota((tm, 1), 0) < SEQ
        target_rows = jnp.concatenate([t[...] for t in t_refs], axis=0)
        diff = jnp.where(real, xhat * g2_ref[...] + b2_ref[...] - target_rows, 0.0)
        acc[...] += jnp.sum(diff * diff, axis=0, keepdims=True)
        dy = diff * (1.0 / D)
        dpre = _ln_bwd(dy, xhat, rstd, g2_ref[...])
        dp_ref[...] = dpre
        dg_ref[...] += jnp.sum(dy * xhat, axis=0, keepdims=True)
        db_ref[...] += jnp.sum(dy, axis=0, keepdims=True)
        dpb = dpre.astype(BF16)
        for j in range(2):
            cols = slice(j * half, (j + 1) * half)
            g, u = g_s[:, cols], u_s[:, cols]
            sg = _sigmoid(g)
            silu = g * sg
            da = _dot_nt(dpb, wd_ref[cols, :])
            a_ref[:, cols] = (silu * u).astype(BF16)
            dgate_ref[:, cols] = (da * u * (sg * (1.0 + g * (1.0 - sg)))).astype(BF16)
            dup_ref[:, cols] = (da * silu).astype(BF16)

        @pl.when(i == steps - 1)
        def _():
            loss_ref[...] = jnp.zeros_like(loss_ref) + (0.5 / D) * jnp.sum(acc[...], axis=1, keepdims=True)

    t_spec = lambda k: pl.BlockSpec((BLK, D), lambda i: (jnp.minimum(i * n_t + k, last_blk), 0))
    return pl.pallas_call(
        body, name="ffn_fwd_loss_bwd", grid=(steps,),
        in_specs=[_rows(tm, D), _const((D_FF, D)), _const((D_FF, D)), _const((D_FF, D))]
        + [t_spec(k) for k in range(n_t)] + [_const((1, D)), _const((1, D))],
        out_specs=[_rows(tm, D_FF), _rows(tm, D_FF), _rows(tm, D_FF), _rows(tm, D), _acc((1, LANE)), _acc((1, D)),
                   _acc((1, D))],
        out_shape=[pltpu.HBM((lp, D_FF), BF16)] * 3 + [pltpu.HBM((lp, D), F32), pltpu.HBM((1, LANE), F32),
                                                         pltpu.HBM((1, D), F32), pltpu.HBM((1, D), F32)],
        scratch_shapes=[pltpu.VMEM((tm, D_FF), F32), pltpu.VMEM((tm, D_FF), F32), pltpu.VMEM((1, D), F32)],
        compiler_params=_params(58, dimension_semantics=_seq()),
    )(*_hbm(h1, wg_t, wu_t, wd, *[target] * n_t, g2, b2))


def _ffn_out_bwd(dpre2, dgate, dup, pre1, wg_t, wu_t, g1, w_out, o_gla, r_g, gn4):
    tm = _row_tile(384)
    lp = _lp()

    def body(dp_ref, dg_ref, du_ref, p1_ref, wg_ref, wu_ref, g1_ref, w_ref, og_ref, r_ref, gn_ref,
             dp1_ref, dg1_ref, db1_ref, dos_ref, dog_ref, dr_ref, dgn_ref):
        @pl.when(pl.program_id(0) == 0)
        def _():
            for acc_ref in (dg1_ref, db1_ref, dgn_ref):
                acc_ref[...] = jnp.zeros_like(acc_ref)

        dh1 = ALPHA * dp_ref[...] + _dot(dg_ref[...], wg_ref[...]) + _dot(du_ref[...], wu_ref[...])
        xhat, rstd1 = _ln_stats(p1_ref[...])
        dpre1 = _ln_bwd(dh1, xhat, rstd1, g1_ref[...])
        dp1_ref[...] = dpre1
        dg1_ref[...] += jnp.sum(dh1 * xhat, axis=0, keepdims=True)
        db1_ref[...] += jnp.sum(dh1, axis=0, keepdims=True)

        do = _dot_nt(dpre1, w_ref[...])
        for pos, h in enumerate(HEAD_POS):
            dos_ref[:, pos * DH:(pos + 1) * DH] = do[:, h * DH:(h + 1) * DH]
        gn = gn_ref[...]
        for h in range(GLA_HEADS):
            hs = slice(h * DV, (h + 1) * DV)
            xg = og_ref[:, hs]
            rstd = lax.rsqrt(jnp.mean(xg * xg, axis=-1, keepdims=True) + RMS_EPS)
            nx = xg * rstd
            r = r_ref[:, hs]
            sr = _sigmoid(r)
            d_o = do[:, 512 + h * DV:512 + (h + 1) * DV]
            dr_ref[:, hs] = d_o * (nx * gn) * (sr * (1.0 + r * (1.0 - sr)))
            dn = d_o * (r * sr)
            dgn_ref[...] += jnp.sum(dn * nx, axis=0, keepdims=True)
            dnx = dn * gn
            dog_ref[:, hs] = rstd * (dnx - nx * jnp.mean(dnx * nx, axis=-1, keepdims=True))

    return pl.pallas_call(
        body, name="ffn_out_bwd", grid=(lp // tm,),
        in_specs=[_rows(tm, D), _rows(tm, D_FF), _rows(tm, D_FF), _rows(tm, D), _const((D_FF, D)), _const((D_FF, D)),
                  _const((1, D)), _const((D, D)), _rows(tm, 512), _rows(tm, 512), _const((1, DV))],
        out_specs=[_rows(tm, D), _acc((1, D)), _acc((1, D)), _rows(tm, 512), _rows(tm, 512), _rows(tm, 512),
                   _acc((1, DV))],
        out_shape=[pltpu.HBM((lp, D), F32), pltpu.HBM((1, D), F32), pltpu.HBM((1, D), F32)]
        + [pltpu.HBM((lp, 512), F32)] * 3 + [pltpu.HBM((1, DV), F32)],
        compiler_params=_params(48, dimension_semantics=_seq()),
    )(*_hbm(dpre2, dgate, dup, pre1, wg_t, wu_t, g1, w_out, o_gla, r_g, gn4))


def _atb(a, b, name, token=None, windows=None):
    lp = _lp()
    tm = _row_tile(1056)
    n, w = a.shape[1], b.shape[1]
    bw = 512 if n * w * 4 > (4 << 20) else w
    tokens = [] if token is None else [token]
    steps = lp // tm

    def body(a_hbm, b_hbm, *rest):
        o_ref, scratch = rest[len(tokens)], rest[len(tokens) + 1:]
        acc_ref = scratch[0] if windows else o_ref
        a_ring, b_ring, sems = scratch[-3:]

        def view(source, t):
            rows = pl.ds(pl.multiple_of((t % steps) * tm, tm), tm)
            if source == 0:
                return a_hbm.at[rows, :]
            return b_hbm.at[rows, pl.ds(pl.multiple_of((t // steps) * bw, bw), bw)]

        slot = _ring_fetch((w // bw) * steps, tm, [a_hbm, b_hbm], [a_ring, b_ring], sems,
                           step=pl.program_id(0) * steps + pl.program_id(1), view=view)

        @pl.when(pl.program_id(1) == 0)
        def _():
            acc_ref[...] = jnp.zeros_like(acc_ref)

        acc_ref[...] += _dot_tn(a_ring[slot], b_ring[slot])

        if windows:
            @pl.when(pl.program_id(1) == steps - 1)
            def _():
                for s, start in enumerate(windows[0]):
                    o_ref[s] = acc_ref[start:start + windows[1], :]

    if windows:
        count, height = len(windows[0]), windows[1]
        out_spec, out_shape = pl.BlockSpec((count, height, bw), lambda j, k: (0, 0, j)), (count, height, w)
    else:
        out_spec, out_shape = pl.BlockSpec((n, bw), lambda j, k: (0, j)), (n, w)
    return pl.pallas_call(
        body, name=name, grid=(w // bw, steps),
        in_specs=[pl.BlockSpec(memory_space=pl.ANY)] * 2 + [_const(TOKEN)] * len(tokens),
        out_specs=out_spec, out_shape=pltpu.HBM(out_shape, F32),
        scratch_shapes=([pltpu.VMEM((n, bw), F32)] if windows else []) + _ring_scratch(tm, [n, bw], [a.dtype, b.dtype]),
        compiler_params=_params(48, dimension_semantics=_seq(2)),
    )(*_hbm(a, b), *tokens)


def _gla_bwd(qg, kg, vg, z, do_gla, st_all, token):
    steps = SEQ // BLK + 1
    kw, vw = GLA_HEADS * DK, GLA_HEADS * DV
    pairs = [(c, h) for c in range(GLA_PER_STEP) for h in range(GLA_HEADS)]
    heads = range(GLA_HEADS)

    def body(q_ref, k_ref, v_ref, z_ref, do_ref, st_ref, token_ref, dq_ref, dk_ref, dv_ref, dz_ref, dst):
        @pl.when(pl.program_id(0) == 0)
        def _():
            dst[...] = jnp.zeros_like(dst)

        rmask = _gla_rowmask(steps - 1 - pl.program_id(0))
        zz = z_ref[...]
        b, b_last = _gla_decay(zz, rmask)
        e_b, e_nb, e_kd, e_last = jnp.exp(b), jnp.exp(-b), jnp.exp(b_last - b), jnp.exp(b_last)
        q = q_ref[...] * (rmask * DK ** -0.5)
        k = k_ref[...] * rmask
        v = v_ref[...] * rmask
        qe, ke, kd = q * e_b, k * e_nb, k * e_kd
        d_o = do_ref[...]
        causal = _iota((CH, CH), 0) >= _iota((CH, CH), 1)
        a, da, dqe, dke, dv_intra, carry = {}, {}, {}, {}, {}, {}
        for c, h in pairs:
            rows, ks, vs_ = _gla_slices(c, h)
            a[c, h] = jnp.where(causal, _dot_nt(qe[rows, ks], ke[rows, ks]), 0.0)
            da[c, h] = jnp.where(causal, _dot_nt(d_o[rows, vs_], v[rows, vs_]), 0.0)
            carry[c, h] = _dot_tn(d_o[rows, vs_], qe[rows, ks])
        for c, h in pairs:
            rows, ks, vs_ = _gla_slices(c, h)
            dqe[c, h] = _dot(d_o[rows, vs_], st_ref[0, c][:, ks]) + _dot(da[c, h], ke[rows, ks])
            dke[c, h] = _dot_tn(da[c, h], qe[rows, ks])
            dv_intra[c, h] = _dot_tn(a[c, h], d_o[rows, vs_])
        dstate = dst[...]
        dkd, db_decay = {}, {}
        for c in reversed(range(GLA_PER_STEP)):
            for h in heads:
                rows, ks, vs_ = _gla_slices(c, h)
                dkd[c, h] = _dot(v[rows, vs_], dstate[:, ks])
                dv_ref[rows, vs_] = dv_intra[c, h] + _dot_nt(kd[rows, ks], dstate[:, ks])
            chunk_last = e_last[c * CH:c * CH + 1]
            db_decay[c] = jnp.sum(dstate * st_ref[0, c], axis=0, keepdims=True) * chunk_last
            dstate = dstate * chunk_last + jnp.concatenate([carry[c, h] for h in heads], axis=1)
        dst[...] = dstate
        rows_of = lambda parts: jnp.concatenate(
            [jnp.concatenate([parts[c, h] for h in heads], axis=1) for c in range(GLA_PER_STEP)], axis=0)
        dqe_all, dke_all, dkd_all = rows_of(dqe), rows_of(dke), rows_of(dkd)
        dq_ref[...] = dqe_all * e_b * (rmask * DK ** -0.5)
        dk_ref[...] = (dke_all * e_nb + dkd_all * e_kd) * rmask
        dkd_kd = dkd_all * kd
        db = dqe_all * qe - dke_all * ke - dkd_kd
        _, upper, same = _gla_chunk_masks()
        decay_rows = jnp.concatenate([jnp.broadcast_to(db_decay[c], (CH, kw)) for c in range(GLA_PER_STEP)], axis=0)
        dlog_g = _dot_exact(upper.astype(F32), db) + _dot_exact(same.astype(F32), dkd_kd) + decay_rows
        dz_ref[...] = dlog_g * (rmask / GLA_TAU) * _sigmoid(-zz)

    blk = lambda w: pl.BlockSpec((BLK, w), lambda s: (_gla_block(steps - 1 - s), 0))
    return pl.pallas_call(
        body, name="gla_bwd", grid=(steps,),
        in_specs=[blk(kw), blk(kw), blk(vw), blk(kw), blk(vw),
                  pl.BlockSpec((1, GLA_PER_STEP, DV, kw), lambda s: (steps - 1 - s, 0, 0, 0)), _const(TOKEN)],
        out_specs=[blk(kw), blk(kw), blk(vw), blk(kw)],
        out_shape=[pltpu.HBM((_lp(), kw), F32), pltpu.HBM((_lp(), kw), F32),
                   pltpu.HBM((_lp(), vw), F32), pltpu.HBM((_lp(), kw), F32)],
        scratch_shapes=[pltpu.VMEM((DV, kw), F32)],
        compiler_params=_params(16, dimension_semantics=_seq()),
    )(*_hbm(qg, kg, vg, z, do_gla, st_all), token)


def _swa_bwd(sinks, qs, ks, vs, do_s, token):
    nb = SEQ // BLK
    kvw = SWA_KV_HEADS * DH
    scale = DH ** -0.5
    heads = range(SWA_HEADS)

    def body(sink_ref, q_ref, km_ref, kp_ref, kc_ref, vm_ref, vp_ref, vc_ref, do_ref, token_ref,
             dq_ref, dk_ref, dv_ref, dsink_ref, carry_k, carry_v, meta_k, meta_v):
        n = pl.program_id(0)

        @pl.when(n == 0)
        def _():
            for r in (carry_k, carry_v, meta_k, meta_v):
                r[...] = jnp.zeros_like(r)
            dsink_ref[...] = jnp.zeros_like(dsink_ref)

        @pl.when(n <= nb)
        def _():
            negdist, maskbias = _swa_bias(n)
            lane = _iota((1, LANE), 1)
            k_all = jnp.concatenate([km_ref[...], kp_ref[...], kc_ref[...]], axis=0).astype(BF16)
            v_all = jnp.concatenate([vm_ref[...], vp_ref[...], vc_ref[...]], axis=0).astype(BF16)
            q = [_swa_half(q_ref, pos, scale) for pos in heads]
            d_o = [_swa_half(do_ref, pos) for pos in heads]
            t = [_dot_nt(q[pos], k_all) + (2.0 ** -(HEAD_POS[pos] + 1) * negdist + maskbias) for pos in heads]
            dp = [_dot_nt(d_o[pos], v_all) for pos in heads]
            soft = [_swa_softmax(t[pos], sink_ref[HEAD_POS[pos]]) for pos in heads]
            p = [s[0] for s in soft]
            delta = [jnp.sum(p[pos] * dp[pos], axis=-1, keepdims=True) for pos in heads]
            ds = [(p[pos] * (dp[pos] - delta[pos])).astype(BF16) for pos in heads]
            dq = [_dot(ds[pos], k_all) for pos in heads]
            for col in range(SWA_HEADS // 2):
                dq_ref[:, col * LANE:(col + 1) * LANE] = scale * _swa_merge(dq[2 * col], dq[2 * col + 1])
            dsink = jnp.zeros((1, LANE), F32)
            for pos in heads:
                dsink = dsink + jnp.where(lane == HEAD_POS[pos],
                                          -jnp.sum(soft[pos][1] * delta[pos], axis=0, keepdims=True), 0.0)
            dsink_ref[...] += dsink
            dk3 = _dot_tn(jnp.concatenate(q, axis=0), jnp.concatenate(ds, axis=0)).T
            dv3 = _dot_tn(jnp.concatenate(d_o, axis=0), jnp.concatenate([x.astype(BF16) for x in p], axis=0)).T
            meta_k[...] += dk3[0:BLK]
            meta_v[...] += dv3[0:BLK]
            dk_ref[...] = carry_k[...] + dk3[BLK:2 * BLK]
            dv_ref[...] = carry_v[...] + dv3[BLK:2 * BLK]
            carry_k[...] = dk3[2 * BLK:3 * BLK]
            carry_v[...] = dv3[2 * BLK:3 * BLK]

        @pl.when(n == nb + 1)
        def _():
            dk_ref[...] = meta_k[...]
            dv_ref[...] = meta_v[...]

    kv_out = pl.BlockSpec((BLK, kvw), lambda n: (jnp.where(n == nb + 1, nb, jnp.clip(n - 1, 0, nb - 1)), 0))
    qblk = pl.BlockSpec((BLK, SWA_HEADS * DH), lambda n: (jnp.minimum(n, nb), 0))
    return pl.pallas_call(
        body, name="swa_bwd", grid=(nb + 2,),
        in_specs=[pl.BlockSpec(memory_space=pltpu.SMEM), qblk] + _swa_kv_specs(kvw) + _swa_kv_specs(kvw)
        + [qblk, _const(TOKEN)],
        out_specs=[qblk, kv_out, kv_out, _acc((1, LANE))],
        out_shape=[pltpu.HBM((_lp(), SWA_HEADS * DH), F32), pltpu.HBM((_lp(), kvw), F32),
                   pltpu.HBM((_lp(), kvw), F32), pltpu.HBM((1, LANE), F32)],
        scratch_shapes=[pltpu.VMEM((BLK, kvw), F32)] * 4,
        compiler_params=_params(16, dimension_semantics=_seq()),
    )(sinks, *_hbm(qs, ks, ks, ks, vs, vs, vs, do_s), token)


def _in_bwd(dqs, dks, dvs, dqg, dkg, dvg, drg, dz, dpre1, w_in_t, wg2_p):
    tm = _row_tile(384)
    lp = _lp()
    widths = (512, 128, 128, 256, 256, 512, 512)
    offs = (O_QS, O_KS, O_VS, O_QG, O_KG, O_VG, O_RG)

    def body(*refs):
        parts, (dz_ref, dp1_ref, w_ref, wg2_ref, dproj_ref, dh0_ref, dbin_ref, dbg_ref) = refs[:7], refs[7:]

        @pl.when(pl.program_id(0) == 0)
        def _():
            dbin_ref[...] = jnp.zeros_like(dbin_ref)
            dbg_ref[...] = jnp.zeros_like(dbg_ref)

        for pos, h in enumerate(HEAD_POS):
            val = parts[0][:, pos * DH:(pos + 1) * DH]
            dproj_ref[:, O_QS + h * DH:O_QS + (h + 1) * DH] = val.astype(BF16)
            dbin_ref[:, O_QS + h * DH:O_QS + (h + 1) * DH] += jnp.sum(val, axis=0, keepdims=True)
        for p_ref, off, wd in zip(parts[1:], offs[1:], widths[1:]):
            val = p_ref[...]
            dproj_ref[:, off:off + wd] = val.astype(BF16)
            dbin_ref[:, off:off + wd] += jnp.sum(val, axis=0, keepdims=True)
        dz = dz_ref[...]
        dlr = _dot_nt(dz, wg2_ref[...])
        dproj_ref[:, O_LR:O_LR + LANE] = dlr.astype(BF16)
        dbin_ref[:, O_LR:O_LR + LANE] += jnp.sum(dlr, axis=0, keepdims=True)
        dbg_ref[...] += jnp.sum(dz, axis=0, keepdims=True)
        dh0_ref[...] = ALPHA * dp1_ref[...] + _dot(dproj_ref[...], w_ref[...])

    return pl.pallas_call(
        body, name="in_bwd", grid=(lp // tm,),
        in_specs=[_rows(tm, w) for w in widths] + [_rows(tm, 256), _rows(tm, D), _const((D_IN_P, D)), _const((LANE, 256))],
        out_specs=[_rows(tm, D_IN_P), _rows(tm, D), _acc((1, D_IN_P)), _acc((1, 256))],
        out_shape=[pltpu.HBM((lp, D_IN_P), BF16), pltpu.HBM((lp, D), F32),
                   pltpu.HBM((1, D_IN_P), F32), pltpu.HBM((1, 256), F32)],
        compiler_params=_params(40, dimension_semantics=_seq()),
    )(*_hbm(dqs, dks, dvs, dqg, dkg, dvg, drg, dz, dpre1, w_in_t, wg2_p))


def _ln_in_bwd(x, meta_ext, dh0, g, token):
    tr = min(LN_ROWS, SEQ)
    steps = SEQ // tr

    def ln_bwd(x_ref, dh_ref, g_ref, dx_ref, dg_ref, db_ref, so_far=None):
        @pl.when(pl.program_id(0) == 0)
        def _():
            dg_ref[...] = jnp.zeros_like(dg_ref) if so_far is None else so_far[0][...]
            db_ref[...] = jnp.zeros_like(db_ref) if so_far is None else so_far[1][...]

        xhat, rstd = _ln_stats(x_ref[...])
        dh = dh_ref[...]
        dx_ref[...] = _ln_bwd(dh, xhat, rstd, g_ref[...])
        dg_ref[...] += jnp.sum(dh * xhat, axis=0, keepdims=True)
        db_ref[...] += jnp.sum(dh, axis=0, keepdims=True)

    def body(x_hbm, dh_hbm, g_ref, token_ref, dx_ref, dg_ref, db_ref, x_ring, dh_ring, sems):
        slot = _ring_fetch(steps, tr, [x_hbm, dh_hbm], [x_ring, dh_ring], sems)
        ln_bwd(x_ring.at[slot], dh_ring.at[slot], g_ref, dx_ref, dg_ref, db_ref)

    def meta_body(m_ref, dh_ref, g_ref, dg_real_ref, db_real_ref, dm_ref, dg_ref, db_ref):
        ln_bwd(m_ref, dh_ref, g_ref, dm_ref, dg_ref, db_ref, (dg_real_ref, db_real_ref))

    sums = [pltpu.HBM((1, D), F32), pltpu.HBM((1, D), F32)]
    dx, dg, db = pl.pallas_call(
        body, name="ln_in_bwd", grid=(steps,),
        in_specs=[pl.BlockSpec(memory_space=pl.ANY)] * 2 + [_const((1, D)), _const(TOKEN)],
        out_specs=[_rows(tr, D), _acc((1, D)), _acc((1, D))],
        out_shape=[pltpu.HBM((SEQ, D), F32)] + sums,
        scratch_shapes=_ring_scratch(tr, [D, D]),
        compiler_params=_params(32, dimension_semantics=_seq()),
    )(*_hbm(x, dh0, g), token)
    dm, dg, db = pl.pallas_call(
        meta_body, name="ln_in_bwd_meta", grid=(1,),
        in_specs=[_const((BLK, D)), pl.BlockSpec((BLK, D), lambda i: (SEQ // BLK, 0))] + [_const((1, D))] * 3,
        out_specs=[_acc((BLK, D)), _acc((1, D)), _acc((1, D))],
        out_shape=[pltpu.HBM((BLK, D), F32)] + sums,
        compiler_params=_params(16, dimension_semantics=_seq()),
    )(*_hbm(meta_ext, dh0, g, dg, db))
    return dx, dm, dg, db


def _local_step(x, target, ln_in_g, ln_in_b, b_in, bg2, sinks, gn, g1, b1, g2, b2,
                token, fetch_first, fetch_rest, fetch_ffn, exchange_ffn, ship_ffn, exchange_w_in, ship_w_in):
    row = lambda v: v.reshape(1, -1).astype(F32)
    b_in_p = jnp.pad(row(b_in), ((0, 0), (0, D_IN_P - D_IN)))
    gn4 = row(gn)
    sinks = sinks.reshape(-1).astype(F32)

    h_real = _ln_in_fwd_real(x, row(ln_in_g), row(ln_in_b), token)
    w_in_windows, meta_full, wg2 = fetch_first([h_real])
    meta_ext = jnp.pad(meta_full, ((META_OFF, BLK - CH), (0, 0)))
    wg2_p = jnp.pad(wg2, ((0, LANE - wg2.shape[0]), (0, 0))).astype(BF16)
    h0 = _ln_in_fwd_meta(h_real, meta_ext, row(ln_in_g), row(ln_in_b))
    qs, ks, vs, qg, kg, vg, rg, glr, z, w_in_t = _in_proj(h0, w_in_windows, b_in_p, wg2_p, row(bg2))
    o_s = _swa_fwd(sinks, qs, ks, vs)
    o_gla, st_all = _gla_fwd(qg, kg, vg, z)
    w_out, token = fetch_rest([o_s, o_gla])
    o, pre1, h1 = _post_mix(o_s, o_gla, rg, h0, gn4, w_out, row(g1), row(b1), token)
    wg_t, wu_t, wd = fetch_ffn([pre1])
    a, dgate, dup, dpre2, loss, dg2, db2 = _ffn_fwd_loss_bwd(h1, wg_t, wu_t, wd, target, row(g2), row(b2))
    dpre1, dg1, db1, do_s, do_gla, drg, dgn = _ffn_out_bwd(dpre2, dgate, dup, pre1, wg_t, wu_t, row(g1), w_out, o_gla,
                                                           rg, gn4)
    dwd = _atb(a, dpre2, "dw_down")
    dwg_t = _atb(dgate, h1, "dw_gate")
    dwu_t = _atb(dup, h1, "dw_up")
    token = exchange_ffn(dict(w_out=_atb(o, dpre1, "dw_out"), w_g=dwg_t, w_u=dwu_t, w_d=dwd))
    dqg, dkg, dvg, dz = _gla_bwd(qg, kg, vg, z, do_gla, st_all, token)
    token = ship_ffn([dqg])
    dqs, dks, dvs, dsinks = _swa_bwd(sinks, qs, ks, vs, do_s, token)
    dproj, dh0, db_in_p, dbg2 = _in_bwd(dqs, dks, dvs, dqg, dkg, dvg, drg, dz, dpre1, w_in_t, wg2_p)
    token = exchange_w_in(_atb(dproj, h0, "dw_in", windows=(W_IN_STARTS, W_IN_WIN)))
    dwg2_p = _atb(glr, dz, "dw_gate_lr2", token)
    token = ship_w_in([dwg2_p])
    dx, dmeta_blk, dg_in, db_in_ln = _ln_in_bwd(x, meta_ext, dh0, row(ln_in_g), token)

    small = dict(meta_blk=dmeta_blk, ln_in_g=dg_in, ln_in_b=db_in_ln, ln1_g=dg1, ln1_b=db1, ln2_g=dg2, ln2_b=db2,
                 b_in_p=db_in_p, wg2_p=dwg2_p, bg2=dbg2, sinks=dsinks, gn=dgn, loss=loss)
    return dx, small


HBM = pl.BlockSpec(memory_space=pltpu.HBM)


def _place():
    return lax.axis_index("x"), lax.axis_index("y"), lax.axis_index("c")


def _other_chips(x, y):
    return [(1 - x, y), (x, 1 - y), (1 - x, 1 - y)]


def _dma_sems(n):
    return pltpu.SemaphoreType.DMA((n,))


def _comm_params():
    return pltpu.CompilerParams(has_side_effects=True)


SEM = pl.BlockSpec(memory_space=pltpu.SEMAPHORE)


PER_ARRAY = dict(gather=3, scatter=3, sibling=N_CHIPS)


def _ici_copies(kind, landing, srcs, lands, send_sems, recv_sems):
    x, y, c = _place()
    mine = 2 * x + y
    copies = []
    for a in range(len(srcs)):
        if kind == "sibling":
            for s in range(N_CHIPS):
                copies.append(pltpu.make_async_remote_copy(
                    srcs[a].at[s, 1 - c], lands[a].at[s], send_sems.at[N_CHIPS * a + s], recv_sems.at[N_CHIPS * a + s],
                    device_id=(x, y, 1 - c), device_id_type=MESH))
            continue
        for j, (px, py) in enumerate(_other_chips(x, y)):
            slab = 2 * px + py if landing else mine
            if kind == "gather":
                src, dst = srcs[a].at[c], lands[a].at[slab, c]
            else:
                src, dst = srcs[a].at[2 * px + py], lands[a].at[slab]
            copies.append(pltpu.make_async_remote_copy(src, dst, send_sems.at[3 * a + j], recv_sems.at[3 * a + j],
                                                       device_id=(px, py, c), device_id_type=MESH))
    return copies


def _split_params():
    return pltpu.CompilerParams(has_side_effects=pltpu.SideEffectType.DATAFLOW_SIDE_EFFECTING)


def _ici_start(kind, srcs, land_shapes, after, name):
    n = len(srcs)
    lands = [pltpu.with_memory_space_constraint(lax.empty(s, a.dtype), pltpu.HBM) for s, a in zip(land_shapes, srcs)]

    def body(*refs):
        outs = refs[2 * n + len(after):]
        for cp in _ici_copies(kind, False, refs[:n], refs[n:2 * n], outs[0], outs[1]):
            cp.start()
        outs[-1][...] = jnp.zeros(TOKEN, F32)

    outs = pl.pallas_call(
        body, name=name, in_specs=[HBM] * (2 * n) + [pl.BlockSpec(memory_space=pl.ANY)] * len(after),
        out_specs=[SEM, SEM] + [HBM] * (2 * n) + [pl.BlockSpec(memory_space=pltpu.VMEM)],
        out_shape=[_dma_sems(PER_ARRAY[kind] * n)] * 2 + [pltpu.HBM(a.shape, a.dtype) for a in list(srcs) + lands]
        + [jax.ShapeDtypeStruct(TOKEN, F32)],
        input_output_aliases={i: 2 + i for i in range(2 * n)},
        compiler_params=_split_params(),
    )(*_hbm(*srcs), *lands, *after)
    return outs[:-1], outs[-1]


def _ici_wait(kind, handle, after, name):
    n = (len(handle) - 2) // 2

    def body(*refs):
        for cp in _ici_copies(kind, True, refs[:n], refs[n:2 * n], refs[2 * n], refs[2 * n + 1]):
            cp.wait_send()
            cp.wait_recv()

    outs = pl.pallas_call(
        body, name=name, in_specs=[HBM] * (2 * n) + [SEM, SEM] + [pl.BlockSpec(memory_space=pl.ANY)] * len(after),
        out_specs=[HBM] * (2 * n), out_shape=[pltpu.HBM(a.shape, a.dtype) for a in handle[2:]],
        input_output_aliases={i: i for i in range(2 * n)},
        compiler_params=_split_params(),
    )(*handle[2:], handle[0], handle[1], *after)
    return list(outs[:n]), list(outs[n:])


def _forward_copies(landing, arrs, send_sems, recv_sems):
    x, y, c = _place()
    copies = []
    for a in range(len(arrs)):
        for j, (px, py) in enumerate(_other_chips(x, y)):
            half = 1 - c if landing else c
            copies.append(pltpu.make_async_remote_copy(
                arrs[a].at[2 * px + py, c], arrs[a].at[2 * px + py, half], send_sems.at[3 * a + j],
                recv_sems.at[3 * a + j], device_id=(x, y, 1 - c), device_id_type=MESH))
    return copies


def _gather_wait_forward(handle, groups, after, name):
    n = (len(handle) - 2) // 2
    assert sum(groups) == n

    def body(*refs):
        outs = refs[2 * n + 2 + len(after):]
        lands, sems = outs[n:2 * n], outs[2 * n:-1]
        arrivals = _ici_copies("gather", True, refs[:n], refs[n:2 * n], refs[2 * n], refs[2 * n + 1])
        sends, first = [], 0
        for g, count in enumerate(groups):
            sends += _forward_copies(False, lands[first:first + count], sems[2 * g], sems[2 * g + 1])
            first += count
        for cp, send in zip(arrivals, sends):
            cp.wait_recv()
            send.start()
        for cp in arrivals:
            cp.wait_send()
        outs[-1][...] = jnp.zeros(TOKEN, F32)

    outs = pl.pallas_call(
        body, name=name, in_specs=[HBM] * (2 * n) + [SEM, SEM] + [pl.BlockSpec(memory_space=pl.ANY)] * len(after),
        out_specs=[HBM] * (2 * n) + [SEM] * (2 * len(groups)) + [pl.BlockSpec(memory_space=pltpu.VMEM)],
        out_shape=[pltpu.HBM(a.shape, a.dtype) for a in handle[2:]]
        + [_dma_sems(3 * count) for count in groups for _ in range(2)] + [jax.ShapeDtypeStruct(TOKEN, F32)],
        input_output_aliases={i: i for i in range(2 * n)},
        compiler_params=_split_params(),
    )(*handle[2:], handle[0], handle[1], *after)
    lands, sems, handles, first = outs[n:2 * n], outs[2 * n:-1], [], 0
    for g, count in enumerate(groups):
        handles.append([sems[2 * g], sems[2 * g + 1], *lands[first:first + count]])
        first += count
    return list(outs[:n]), handles, outs[-1]


def _forward_wait(handle, after, name):
    n = len(handle) - 2

    def body(*refs):
        for cp in _forward_copies(True, refs[:n], refs[n], refs[n + 1]):
            cp.wait_send()
            cp.wait_recv()

    return list(pl.pallas_call(
        body, name=name, in_specs=[HBM] * n + [SEM, SEM] + [pl.BlockSpec(memory_space=pl.ANY)] * len(after),
        out_specs=[HBM] * n, out_shape=[pltpu.HBM(a.shape, a.dtype) for a in handle[2:]],
        input_output_aliases={i: i for i in range(n)},
        compiler_params=_split_params(),
    )(*handle[2:], handle[0], handle[1], *after))


def _add_halves(core, grads, recvs, dtypes, name):
    n = len(grads)
    heights = [g.shape[2] for g in grads]

    def body(c_ref, *refs):
        for a in range(n):
            refs[2 * n + a][...] = (refs[2 * a][0] + refs[2 * a + 1][...]).astype(dtypes[a])

    slab = lambda h: pl.BlockSpec((1, h, D), lambda s, c: (s, 0, 0))
    mine = lambda h: pl.BlockSpec((1, 1, h, D), lambda s, c: (s, c[0], 0, 0))
    return pl.pallas_call(
        body, name=name,
        grid_spec=pltpu.PrefetchScalarGridSpec(
            num_scalar_prefetch=1, grid=(N_CHIPS,),
            in_specs=[spec(h) for h in heights for spec in (mine, slab)], out_specs=[slab(h) for h in heights]),
        out_shape=[pltpu.HBM((N_CHIPS, h, D), dt) for h, dt in zip(heights, dtypes)],
        compiler_params=_params(32, dimension_semantics=_seq()),
    )(core, *_hbm(*[a for pair in zip(grads, recvs) for a in pair]))


N_DEVICES = 2 * N_CHIPS
PEER_FLIPS = [(dx, dy, dc) for dx in (0, 1) for dy in (0, 1) for dc in (0, 1)][1:]


def _small_copies(landing, p_ref, out_ref, send_sems, recv_sems):
    x, y, c = _place()
    flip = lambda v, d: 1 - v if d else v
    copies = []
    for k, flips in enumerate(PEER_FLIPS):
        px, py, pc = (flip(v, d) for v, d in zip((x, y, c), flips))
        slab = 4 * px + 2 * py + pc if landing else 4 * x + 2 * y + c
        copies.append(pltpu.make_async_remote_copy(p_ref, out_ref.at[slab], send_sems.at[k], recv_sems.at[k],
                                                   device_id=(px, py, pc), device_id_type=MESH))
    return copies


def _small_wait(handle, after):
    def body(p_ref, land_ref, send_sems, recv_sems, *rest):
        for cp in _small_copies(True, p_ref, land_ref, send_sems, recv_sems):
            cp.wait_send()
            cp.wait_recv()

    return pl.pallas_call(
        body, name="small_exchange_wait", in_specs=[HBM, HBM, SEM, SEM] + [pl.BlockSpec(memory_space=pl.ANY)] * len(after),
        out_specs=[HBM, HBM], out_shape=[pltpu.HBM(a.shape, F32) for a in handle[2:]],
        input_output_aliases={0: 0, 1: 1},
        compiler_params=_split_params(),
    )(handle[2], handle[3], handle[0], handle[1], *after)


def _sum_chips(slots, firsts, rests, after, name):
    n = len(firsts)

    def body(i_ref, *refs):
        outs = refs[4 * n + len(after):]
        for a in range(n):
            first, r1, r2, r3 = refs[4 * a:4 * a + 4]
            outs[a][...] = ((first[...].astype(F32) + r1[...].astype(F32)) + r2[...].astype(F32)) + r3[...].astype(F32)

    slab = lambda h, k: pl.BlockSpec((1, h, D), lambda i, ix: (ix[k], 0, 0))
    heights = [f.shape[1] for f in firsts]
    return pl.pallas_call(
        body, name=name,
        grid_spec=pltpu.PrefetchScalarGridSpec(
            num_scalar_prefetch=1, grid=(1,),
            in_specs=[slab(h, k) for h in heights for k in range(4)] + [pl.BlockSpec(memory_space=pl.ANY)] * len(after),
            out_specs=[slab(h, 4) for h in heights]),
        out_shape=[pltpu.HBM((2, h, D), F32) for h in heights],
        compiler_params=_params(48, dimension_semantics=_seq()),
    )(slots, *_hbm(*[a for f, r in zip(firsts, rests) for a in (f, r, r, r)]), *after)


def _join_copies(landing, arrs, send_sems, recv_sems):
    x, y, c = _place()
    slab = 1 - c if landing else c
    return [pltpu.make_async_remote_copy(arr.at[slab], arr.at[slab], send_sems.at[a], recv_sems.at[a],
                                         device_id=(x, y, 1 - c), device_id_type=MESH) for a, arr in enumerate(arrs)]


def _join_halves(halves, name):
    n = len(halves)

    def body(*refs):
        outs = refs[n:2 * n]
        send_sems, recv_sems = refs[2 * n:]
        sends = _join_copies(False, outs, send_sems, recv_sems)
        for cp in sends:
            cp.start()
        for cp in _join_copies(True, outs, send_sems, recv_sems):
            cp.wait_recv()
        for cp in sends:
            cp.wait_send()

    return list(pl.pallas_call(
        body, name=name, in_specs=[HBM] * n, out_specs=[HBM] * n,
        out_shape=[pltpu.HBM(h.shape, F32) for h in halves],
        input_output_aliases={a: a for a in range(n)},
        scratch_shapes=[_dma_sems(n)] * 2,
        compiler_params=_comm_params(),
    )(*_hbm(*halves)))


def _join_small_start(halves, pack, name):
    n, peers = len(halves), len(PEER_FLIPS)
    land = pltpu.with_memory_space_constraint(lax.empty((N_DEVICES,) + pack.shape, F32), pltpu.HBM)

    def body(*refs):
        outs = refs[n + 2:]
        for cp in _join_copies(False, refs[:n], outs[0], outs[1]):
            cp.start()
        for cp in _small_copies(False, refs[n], refs[n + 1], outs[2], outs[3]):
            cp.start()
        outs[-1][...] = jnp.zeros(TOKEN, F32)

    outs = pl.pallas_call(
        body, name=name, in_specs=[HBM] * (n + 2),
        out_specs=[SEM] * 4 + [HBM] * (n + 2) + [pl.BlockSpec(memory_space=pltpu.VMEM)],
        out_shape=[_dma_sems(n)] * 2 + [_dma_sems(peers)] * 2
        + [pltpu.HBM(a.shape, a.dtype) for a in list(halves) + [pack, land]] + [jax.ShapeDtypeStruct(TOKEN, F32)],
        input_output_aliases={i: 4 + i for i in range(n + 2)},
        compiler_params=_split_params(),
    )(*_hbm(*halves, pack), land)
    return [outs[0], outs[1], *outs[4:4 + n]], [outs[2], outs[3], outs[4 + n], outs[5 + n]], outs[-1]


def _join_wait(handle, after, name):
    n = len(handle) - 2

    def body(*refs):
        for cp in _join_copies(True, refs[:n], refs[n], refs[n + 1]):
            cp.wait_send()
            cp.wait_recv()

    return list(pl.pallas_call(
        body, name=name, in_specs=[HBM] * n + [SEM, SEM] + [pl.BlockSpec(memory_space=pl.ANY)] * len(after),
        out_specs=[HBM] * n, out_shape=[pltpu.HBM(a.shape, a.dtype) for a in handle[2:]],
        input_output_aliases={i: i for i in range(n)},
        compiler_params=_split_params(),
    )(*handle[2:], handle[0], handle[1], *after))


def _chip_partials(grads, fetched, wire_dtypes, name):
    core = lax.axis_index("c").astype(jnp.int32).reshape(1)
    return list(_add_halves(core, grads, fetched, wire_dtypes, name))


def _chip_sums(parts, got, after, name):
    x, y, c = _place()
    others = [2 * px + py for px, py in _other_chips(x, y)]
    own_first = jnp.stack([2 * x + y] + others + [c]).astype(jnp.int32)
    return list(_sum_chips(own_first, parts, got, after, name))


ADAMW_STEPS = 8


def _adamw(params, by_row, chip, window_step):
    n = len(params)
    rows, _, cols = by_row[0].shape
    block = lambda shape: pl.BlockSpec((shape[0] // ADAMW_STEPS, shape[1]), lambda i, c: (i, 0))
    assert all(a.shape[0] % (8 * ADAMW_STEPS) == 0 for p in params for a in p)

    def body(c_ref, *refs):
        w_hbm, g_ref, m_hbm, v_hbm = refs[4 * n:4 * n + 4]
        results, (ins_ref, outs_ref, sems) = refs[8 * n + 4:8 * n + 8], refs[8 * n + 8:]
        loads = [pltpu.make_async_copy(src.at[:, 0, :], ins_ref.at[k], sems.at[k])
                 for k, src in enumerate((w_hbm, m_hbm, v_hbm))]
        stores = [pltpu.make_async_copy(outs_ref.at[k], dst.at[:, 0, :], sems.at[3 + k]) for k, dst in enumerate(results)]
        first = pl.program_id(0) == 0

        @pl.when(first)
        def _():
            for cp in loads:
                cp.start()

        for a in range(n):
            w_ref, a_g_ref, m_ref, v_ref = refs[4 * a:4 * a + 4]
            outs = refs[4 * n + 4 + 4 * a:4 * n + 8 + 4 * a]
            g = a_g_ref[...]
            outs[0][...] = g
            outs[1][...], outs[2][...], outs[3][...] = _adamw_math(w_ref[...], g, m_ref[...], v_ref[...])

        @pl.when(first)
        def _():
            for cp in loads:
                cp.wait()
            for lo in range(0, cols, LANE):
                lanes = slice(lo, lo + LANE)
                g = g_ref[0:rows, lanes]
                for s in range(1, N_CHIPS):
                    g = jnp.where(c_ref[0] == s, g_ref[s * window_step:s * window_step + rows, lanes], g)
                outs_ref[0, :, lanes] = g
                outs_ref[1, :, lanes], outs_ref[2, :, lanes], outs_ref[3, :, lanes] = _adamw_math(
                    ins_ref[0, :, lanes], g, ins_ref[1, :, lanes], ins_ref[2, :, lanes])
            for cp in stores:
                cp.start()

        @pl.when(pl.program_id(0) == ADAMW_STEPS - 1)
        def _():
            for cp in stores:
                cp.wait()

    outs = pl.pallas_call(
        body, name="adamw_matrices",
        grid_spec=pltpu.PrefetchScalarGridSpec(
            num_scalar_prefetch=1, grid=(ADAMW_STEPS,),
            in_specs=[block(a.shape) for p in params for a in p] + [HBM, _const(by_row[1].shape), HBM, HBM],
            out_specs=[block(p[0].shape) for p in params for _ in range(4)] + [HBM] * 4,
            scratch_shapes=[pltpu.VMEM((3, rows, cols), F32), pltpu.VMEM((4, rows, cols), F32), _dma_sems(7)]),
        out_shape=[pltpu.HBM(p[0].shape, F32) for p in params for _ in range(4)] + [pltpu.HBM((rows, 1, cols), F32)] * 4,
        compiler_params=_params(48, dimension_semantics=_seq()),
    )(chip, *_hbm(*[a for p in params for a in p], *by_row))
    return [outs[4 * a:4 * a + 4] for a in range(n)], outs[4 * n:]


def _adamw_math(w, g, m, v):
    nm = ADAM_B1 * m + (1.0 - ADAM_B1) * g
    nv = ADAM_B2 * v + (1.0 - ADAM_B2) * (g * g)
    m_hat = nm / (1.0 - ADAM_B1 ** ADAM_STEP)
    v_hat = nv / (1.0 - ADAM_B2 ** ADAM_STEP)
    return -ADAM_LR * (m_hat / (jnp.sqrt(v_hat) + ADAM_EPS) + ADAM_WD * w), nm, nv


SMALL = (("meta_tokens", (N_META, D // N_CHIPS)), ("ln_in_g", (1, D)), ("ln_in_b", (1, D)), ("b_in", (1, D_IN)),
         ("w_gate_lr2", (GATE_RANK, GLA_HEADS * DK // N_CHIPS)), ("b_gate_lr2", (1, GLA_HEADS * DK)),
         ("attn_sinks", (1, SWA_HEADS)),
         ("gla_norm_g", (1, DV)), ("ln1_g", (1, D)), ("ln1_b", (1, D)), ("ln2_g", (1, D)), ("ln2_b", (1, D)))
ROW_META, ROW_B_IN, ROW_TAIL, ROW_WG2 = 0, 22, 25, 32
ROW_LN = dict(ln_in_g=16, ln_in_b=17, ln1_g=18, ln1_b=19, ln2_g=20, ln2_b=21)
TAIL_BG2, TAIL_SINKS, TAIL_GN, TAIL_LOSS = 0, 256, 256 + SWA_HEADS, 256 + SWA_HEADS + DV


def _adamw_small(place, packs, own, params):
    n = len(SMALL)

    def body(place_ref, packs_ref, own_ref, *refs):
        ins, outs, p_ref = refs[:3 * n], refs[3 * n:-1], refs[-1]
        me, c = place_ref[0], place_ref[1]
        total = jnp.where(me == 0, own_ref[...], packs_ref[0])
        for i in range(1, N_DEVICES):
            total = total + jnp.where(me == i, own_ref[...], packs_ref[i])
        p_ref[...] = total
        outs[4 * n][...] = total[ROW_TAIL:ROW_TAIL + 1, TAIL_LOSS:TAIL_LOSS + 1]

        def mine(width, rows):
            part = lambda s: p_ref[rows, s * width:(s + 1) * width]
            return jnp.where(c == 0, part(0), jnp.where(c == 1, part(1), jnp.where(c == 2, part(2), part(3))))

        tail = lambda lo, width: p_ref[ROW_TAIL:ROW_TAIL + 1, lo:lo + width]
        grads = dict(
            meta_tokens=mine(D // N_CHIPS, slice(ROW_META, ROW_META + N_META)),
            b_in=jnp.concatenate([p_ref[ROW_B_IN:ROW_B_IN + 1, :], p_ref[ROW_B_IN + 1:ROW_B_IN + 2, :],
                                  p_ref[ROW_B_IN + 2:ROW_B_IN + 3, 0:D_IN - 2 * D]], axis=1),
            w_gate_lr2=mine(256 // N_CHIPS, slice(ROW_WG2, ROW_WG2 + 16)),
            b_gate_lr2=tail(TAIL_BG2, 256), attn_sinks=tail(TAIL_SINKS, SWA_HEADS), gla_norm_g=tail(TAIL_GN, DV),
            **{k: p_ref[r:r + 1, :] for k, r in ROW_LN.items()})
        for i, (name, _) in enumerate(SMALL):
            g = grads[name]
            outs[4 * i][...] = g
            outs[4 * i + 1][...], outs[4 * i + 2][...], outs[4 * i + 3][...] = _adamw_math(
                ins[3 * i][...], g, ins[3 * i + 1][...], ins[3 * i + 2][...])

    whole = lambda shape: pl.BlockSpec(shape, lambda i, c: (0,) * len(shape))
    outs = pl.pallas_call(
        body, name="adamw_small",
        grid_spec=pltpu.PrefetchScalarGridSpec(
            num_scalar_prefetch=1, grid=(1,),
            in_specs=[whole(packs.shape), whole(own.shape)] + [whole(s) for _, s in SMALL for _ in range(3)],
            out_specs=[whole(s) for _, s in SMALL for _ in range(4)] + [whole((1, 1))],
            scratch_shapes=[pltpu.VMEM(own.shape, F32)]),
        out_shape=[pltpu.HBM(s, F32) for _, s in SMALL for _ in range(4)] + [pltpu.HBM((1, 1), F32)],
        compiler_params=_params(16, dimension_semantics=_seq()),
    )(place, *_hbm(packs, own, *[a for p in params for a in p]))
    return [outs[4 * i:4 * i + 4] for i in range(n)], outs[4 * n]


def _small_pack(gr):
    names = ["meta_blk"] + list(ROW_LN) + ["b_in_p", "wg2_p", "bg2", "sinks", "gn", "loss"]
    gate_w = GLA_HEADS * DK

    def body(*refs):
        src, out = dict(zip(names, refs)), refs[-1]
        out[...] = jnp.zeros_like(out)
        out[ROW_META:ROW_META + N_META, :] = src["meta_blk"][META_OFF:CH, :]
        for k, r in ROW_LN.items():
            out[r:r + 1, :] = src[k][...]
        for j in range(-(-D_IN // D)):
            width = min(D, D_IN - j * D)
            out[ROW_B_IN + j:ROW_B_IN + j + 1, 0:width] = src["b_in_p"][:, j * D:j * D + width]
        tail = slice(ROW_TAIL, ROW_TAIL + 1)
        out[tail, TAIL_BG2:TAIL_BG2 + gate_w] = src["bg2"][...]
        out[tail, TAIL_SINKS:TAIL_SINKS + SWA_HEADS] = src["sinks"][:, 0:SWA_HEADS]
        out[tail, TAIL_GN:TAIL_GN + DV] = src["gn"][...]
        out[tail, TAIL_LOSS:TAIL_LOSS + 1] = src["loss"][:, 0:1]
        out[ROW_WG2:ROW_WG2 + GATE_RANK, 0:gate_w] = src["wg2_p"][0:GATE_RANK, :]

    arrays = [gr[k] for k in names]
    return pl.pallas_call(
        body, name="small_pack", grid=(1,),
        in_specs=[_acc(a.shape) for a in arrays], out_specs=_acc((SMALL_ROWS, D)),
        out_shape=pltpu.HBM((SMALL_ROWS, D), F32),
        compiler_params=_params(16, dimension_semantics=_seq()),
    )(*_hbm(*arrays))


BIG = ("w_in", "w_out", "w_g", "w_u", "w_d")


def kernel(x, meta_tokens, ln_in_g, ln_in_b, w_in, b_in, w_gate_lr2, b_gate_lr2, attn_sinks, gla_norm_g, w_out, ln1_g, ln1_b, w_ffn_gate, w_ffn_up, w_ffn_down, ln2_g, ln2_b, loss_target, m_meta_tokens, m_ln_in_g, m_ln_in_b, m_w_in, m_b_in, m_w_gate_lr2, m_b_gate_lr2, m_attn_sinks, m_gla_norm_g, m_w_out, m_ln1_g, m_ln1_b, m_w_ffn_gate, m_w_ffn_up, m_w_ffn_down, m_ln2_g, m_ln2_b, v_meta_tokens, v_ln_in_g, v_ln_in_b, v_w_in, v_b_in, v_w_gate_lr2, v_b_gate_lr2, v_attn_sinks, v_gla_norm_g, v_w_out, v_ln1_g, v_ln1_b, v_w_ffn_gate, v_w_ffn_up, v_w_ffn_down, v_ln2_g, v_ln2_b):
    chip = 2 * lax.axis_index("x") + lax.axis_index("y")

    halves = lambda a: a.reshape(2, a.shape[0] // 2, a.shape[1])
    r_in = SHARD_ROWS["w_in"]
    first = [halves(a) for a in (jnp.pad(w_in[0].T.astype(BF16), ((0, W_IN_WIN - r_in), (0, 0))), meta_tokens,
                                 w_gate_lr2[0])]
    rest = [halves(a) for a in (w_out[0].astype(BF16), w_ffn_gate[0].T.astype(BF16), w_ffn_up[0].T.astype(BF16),
                                w_ffn_down[0].astype(BF16))]
    lands = lambda arrs: [(N_CHIPS,) + a.shape for a in arrs]
    first_handle, first_token = _ici_start("gather", first, lands(first), [], "gather_first_start")
    rest_handle, token = _ici_start("gather", rest, lands(rest), [first_token], "gather_rest_start")
    own_slab = lambda got, shards: [lax.dynamic_update_index_in_dim(g, s, chip, axis=0) for g, s in zip(got, shards)]
    fetching = {}

    def fetch_first(after):
        shards, (forwarding,), _ = _gather_wait_forward(first_handle, [len(first)], after, "gather_first_wait")
        g_in, g_meta, g_wg2 = own_slab(_forward_wait(forwarding, [], "gather_first_forward_wait"), shards)
        w_in_windows = g_in.reshape(N_CHIPS, W_IN_WIN, D)
        meta_full = jnp.concatenate([g_meta[s].reshape(N_META, -1) for s in range(N_CHIPS)], axis=1)
        wg2_full = jnp.concatenate([g_wg2[s].reshape(w_gate_lr2.shape[1], -1) for s in range(N_CHIPS)], axis=1)
        return w_in_windows, meta_full, wg2_full

    def fetch_rest(after):
        shards, (w_out_forwarding, fetching["handle"]), forward_token = _gather_wait_forward(
            rest_handle, [1, len(rest) - 1], after, "gather_rest_wait")
        g_out, = own_slab(_forward_wait(w_out_forwarding, [], "gather_w_out_forward_wait"), shards[:1])
        fetching["shards"] = shards[1:]
        return g_out.reshape(-1, D), forward_token

    def fetch_ffn(after):
        got = _forward_wait(fetching["handle"], after, "gather_ffn_forward_wait")
        return [g.reshape(-1, D) for g in own_slab(got, fetching["shards"])]

    sent = {}
    split = lambda grads: [g.reshape(N_CHIPS, 2, -1, D) for g in grads]

    def exchange(key, grads):
        grads = split(grads)
        sent[key + "_halves"], exchange_token = _ici_start(
            "sibling", grads, [(N_CHIPS,) + a.shape[2:] for a in grads], [], "sibling_" + key + "_start")
        return exchange_token

    def ship(key, after):
        grads, fetched = _ici_wait("sibling", sent[key + "_halves"], after, "sibling_" + key + "_wait")
        parts = _chip_partials(grads, fetched, [BF16] * len(grads), "add_halves_" + key)
        sent[key], ship_token = _ici_start("scatter", parts, [p.shape for p in parts], [], "scatter_" + key + "_start")
        return ship_token

    dx, gr = _local_step(
        x[0], loss_target[0], ln_in_g, ln_in_b, b_in[0], b_gate_lr2[0], attn_sinks[0], gla_norm_g[0], ln1_g[0],
        ln1_b[0], ln2_g[0], ln2_b[0], token, fetch_first, fetch_rest, fetch_ffn,
        lambda g: exchange("ffn", [g[k] for k in BIG[1:]]), lambda after: ship("ffn", after),
        lambda g: exchange("w_in", [g]), lambda after: ship("w_in", after))
    ffn_parts, ffn_got = _ici_wait("scatter", sent["ffn"], [dx], "scatter_ffn_wait")
    join_handle, small_handle, token = _join_small_start(
        _chip_sums(ffn_parts, ffn_got, [], "sum_chips_ffn"), _small_pack(gr), "join_ffn_small_start")
    w_in_parts, w_in_got = _ici_wait("scatter", sent["w_in"], [token], "scatter_w_in_wait")
    w_in_joined = _join_halves(_chip_sums(w_in_parts, w_in_got, [], "sum_chips_w_in"), "join_w_in")
    red = [f.reshape(2 * f.shape[1], D) for f in w_in_joined + _join_wait(join_handle, w_in_joined, "join_ffn_wait")]

    big_g = dict(zip(BIG, red))
    weights = dict(meta_tokens=meta_tokens, ln_in_g=ln_in_g, ln_in_b=ln_in_b, w_in=w_in, b_in=b_in,
                   w_gate_lr2=w_gate_lr2, b_gate_lr2=b_gate_lr2, attn_sinks=attn_sinks, gla_norm_g=gla_norm_g,
                   w_out=w_out, ln1_g=ln1_g, ln1_b=ln1_b, w_ffn_gate=w_ffn_gate, w_ffn_up=w_ffn_up,
                   w_ffn_down=w_ffn_down, ln2_g=ln2_g, ln2_b=ln2_b)
    m_in = dict(meta_tokens=m_meta_tokens, ln_in_g=m_ln_in_g, ln_in_b=m_ln_in_b, w_in=m_w_in, b_in=m_b_in,
                w_gate_lr2=m_w_gate_lr2, b_gate_lr2=m_b_gate_lr2, attn_sinks=m_attn_sinks, gla_norm_g=m_gla_norm_g,
                w_out=m_w_out, ln1_g=m_ln1_g, ln1_b=m_ln1_b, w_ffn_gate=m_w_ffn_gate, w_ffn_up=m_w_ffn_up,
                w_ffn_down=m_w_ffn_down, ln2_g=m_ln2_g, ln2_b=m_ln2_b)
    v_in = dict(meta_tokens=v_meta_tokens, ln_in_g=v_ln_in_g, ln_in_b=v_ln_in_b, w_in=v_w_in, b_in=v_b_in,
                w_gate_lr2=v_w_gate_lr2, b_gate_lr2=v_b_gate_lr2, attn_sinks=v_attn_sinks, gla_norm_g=v_gla_norm_g,
                w_out=v_w_out, ln1_g=v_ln1_g, ln1_b=v_ln1_b, w_ffn_gate=v_w_ffn_gate, w_ffn_up=v_w_ffn_up,
                w_ffn_down=v_w_ffn_down, ln2_g=v_ln2_g, ln2_b=v_ln2_b)
    names = list(weights)
    big_names = ("w_in", "w_out", "w_ffn_gate", "w_ffn_up", "w_ffn_down")

    grads, delta, new_m, new_v = {}, {}, {}, {}
    flips = [(lambda a: a.T) if kk in ("w_g", "w_u") else (lambda a: a) for kk in BIG[1:]]
    by_row = lambda a: jnp.transpose(a, (2, 0, 1))
    updated, updated_w_in = _adamw(
        [(flip(weights[k][0]), big_g[kk], flip(m_in[k][0]), flip(v_in[k][0]))
         for k, kk, flip in zip(big_names[1:], BIG[1:], flips)],
        (by_row(w_in), big_g["w_in"], by_row(m_w_in), by_row(v_w_in)), chip.astype(jnp.int32).reshape(1), r_in % BF16_ROWS)
    for k, flip, results in zip(big_names[1:], flips, updated):
        grads[k], delta[k], new_m[k], new_v[k] = (flip(t)[None] for t in results)
    grads["w_in"], delta["w_in"], new_m["w_in"], new_v["w_in"] = (jnp.transpose(t, (1, 2, 0)) for t in updated_w_in)
    small_in = [tuple(src[k].reshape(shape) for src in (weights, m_in, v_in)) for k, shape in SMALL]
    place = jnp.stack([2 * chip + lax.axis_index("c"), chip]).astype(jnp.int32)
    small_own, small_all = _small_wait(small_handle, [updated[0][0]])
    small_out, loss = _adamw_small(place, small_all, small_own, small_in)
    for (k, _), results in zip(SMALL, small_out):
        grads[k], delta[k], new_m[k], new_v[k] = (r.reshape(weights[k].shape) for r in results)

    return (loss.reshape(()), dx[None], *[grads[k] for k in names], *[delta[k] for k in names], *[new_m[k] for k in names],
            *[new_v[k] for k in names])
```

```python
import jax
import jax.numpy as jnp
from jax import lax
from jax.experimental import pallas as pl
from jax.experimental.pallas import tpu as pltpu

F32 = jnp.float32
BF16 = jnp.bfloat16
MESH = pl.DeviceIdType.MESH

D = 1024
SEQ = 4096
N_META = 16
SWA_HEADS, SWA_KV_HEADS, DH = 8, 2, 64
WINDOW = 128
GLA_HEADS, DK, DV = 4, 64, 128
GLA_TAU = 16.0
CH = 64
D_FF = 2816
D_IN = 2320
LN_EPS = 1e-5
RMS_EPS = 1e-6
ALPHA = 2.0 ** 0.25
NEG = -1e30
ADAM_LR, ADAM_B1, ADAM_B2, ADAM_EPS, ADAM_WD, ADAM_STEP = 0.001, 0.9, 0.999, 1e-8, 0.01, 10
O_QS, O_KS, O_VS, O_QG, O_KG, O_VG, O_RG, O_LR = 0, 512, 640, 768, 1024, 1280, 1792, 2304

LANE = 128
BLK = WINDOW
GATE_RANK = 16
D_IN_P = D_IN + LANE - GATE_RANK
META_OFF = CH - N_META
HEAD_POS = (0, 4, 1, 5, 2, 6, 3, 7)
LN_ROWS = 512
TOKEN = (8, LANE)
N_CHIPS = 4
SHARD_ROWS = dict(w_in=D_IN // N_CHIPS, w_out=D // N_CHIPS, w_g=D_FF // N_CHIPS, w_u=D_FF // N_CHIPS,
                  w_d=D_FF // N_CHIPS)
SMALL_ROWS = 48
BF16_ROWS = 16
W_IN_WIN = -(-SHARD_ROWS["w_in"] // (2 * BF16_ROWS)) * 2 * BF16_ROWS
W_IN_STARTS = tuple(s * SHARD_ROWS["w_in"] // BF16_ROWS * BF16_ROWS for s in range(N_CHIPS))
VMEM_CAP_MB = 64
VMEM_SPARE_MB = 6


def _lp():
    return SEQ + BLK


def _row_tile(cap):
    lp = _lp()
    return max(t for t in range(16, cap + 1, 16) if lp % t == 0)


def _params(vmem_mb, **kw):
    assert vmem_mb <= VMEM_CAP_MB - VMEM_SPARE_MB
    return pltpu.CompilerParams(vmem_limit_bytes=vmem_mb << 20, **kw)


def _seq(n=1):
    return ("arbitrary",) * n


def _const(shape):
    return pl.BlockSpec(shape, lambda *_: (0,) * len(shape), pipeline_mode=pl.Buffered(1))


def _acc(shape):
    return pl.BlockSpec(shape, lambda *_: (0,) * len(shape))


def _rows(tm, width):
    return pl.BlockSpec((tm, width), lambda i: (i, 0))


def _dot(a, b):
    return jnp.dot(a.astype(BF16), b.astype(BF16), preferred_element_type=F32)


def _dot_nt(a, b):
    return lax.dot_general(a.astype(BF16), b.astype(BF16), (((1,), (1,)), ((), ())), preferred_element_type=F32)


def _dot_tn(a, b):
    return lax.dot_general(a.astype(BF16), b.astype(BF16), (((0,), (0,)), ((), ())), preferred_element_type=F32)


def _dot_exact(a, b):
    return jnp.dot(a, b, precision=lax.Precision.HIGHEST, preferred_element_type=F32)


def _ln_stats(x):
    mu = jnp.mean(x, axis=-1, keepdims=True)
    xc = x - mu
    rstd = lax.rsqrt(jnp.mean(xc * xc, axis=-1, keepdims=True) + LN_EPS)
    return xc * rstd, rstd


def _ln_bwd(dy, xhat, rstd, g):
    dxh = dy * g
    return rstd * (dxh - jnp.mean(dxh, axis=-1, keepdims=True) - xhat * jnp.mean(dxh * xhat, axis=-1, keepdims=True))


def _sigmoid(x):
    return 1.0 / (1.0 + jnp.exp(-x))


def _iota(shape, dim):
    return lax.broadcasted_iota(jnp.int32, shape, dim)


def _hbm(*arrays):
    return tuple(pltpu.with_memory_space_constraint(a, pltpu.HBM) for a in arrays)


RING = 3


def _ring_fetch(steps, tile, sources, rings, sems):
    assert steps >= RING - 1
    step = pl.program_id(0)

    def copy(t, k):
        rows = pl.ds(pl.multiple_of(t * tile, tile), tile)
        return pltpu.make_async_copy(sources[k].at[rows, :], rings[k].at[t % RING], sems.at[k, t % RING])

    @pl.when(step == 0)
    def _():
        for t in range(RING - 1):
            for k in range(len(sources)):
                copy(t, k).start()

    @pl.when(step + (RING - 1) < steps)
    def _():
        for k in range(len(sources)):
            copy(step + (RING - 1), k).start()

    for k in range(len(sources)):
        copy(step, k).wait()
    return step % RING


def _ring_scratch(tile, widths):
    return [pltpu.VMEM((RING, tile, w), F32) for w in widths] + [pltpu.SemaphoreType.DMA((len(widths), RING))]


def _ln_in_fwd_real(x, g, b, token):
    tr = min(LN_ROWS, SEQ)
    steps = SEQ // tr

    def body(x_hbm, g_ref, b_ref, token_ref, h_ref, x_ring, sems):
        slot = _ring_fetch(steps, tr, [x_hbm], [x_ring], sems)
        xhat, _ = _ln_stats(x_ring[slot])
        h_ref[...] = xhat * g_ref[...] + b_ref[...]

    return pl.pallas_call(
        body, name="ln_in_fwd", grid=(steps,),
        in_specs=[pl.BlockSpec(memory_space=pl.ANY), _const((1, D)), _const((1, D)), _const(TOKEN)],
        out_specs=_rows(tr, D),
        out_shape=pltpu.HBM((_lp(), D), F32),
        scratch_shapes=_ring_scratch(tr, [D]),
        compiler_params=_params(32, dimension_semantics=_seq()),
    )(*_hbm(x, g, b), token)


def _ln_in_fwd_meta(h_real, meta_ext, g, b):
    def meta_body(m_ref, g_ref, b_ref, real_ref, h_ref):
        xhat, _ = _ln_stats(m_ref[...])
        h_ref[...] = xhat * g_ref[...] + b_ref[...]

    return pl.pallas_call(
        meta_body, name="ln_in_fwd_meta", grid=(1,),
        in_specs=[_const((BLK, D)), _const((1, D)), _const((1, D)), pl.BlockSpec(memory_space=pl.ANY)],
        out_specs=pl.BlockSpec((BLK, D), lambda i: (SEQ // BLK, 0)),
        out_shape=pltpu.HBM((_lp(), D), F32),
        input_output_aliases={3: 0},
        compiler_params=_params(16, dimension_semantics=_seq()),
    )(*_hbm(meta_ext, g, b, h_real))


def _in_proj(h0, w_in_windows, b_in_p, wg2_p, bg2):
    tm = _row_tile(384)
    lp = _lp()
    widths = (512, 128, 128, 256, 256, 512, 512, 128)
    offs = (O_QS, O_KS, O_VS, O_QG, O_KG, O_VG, O_RG, O_LR)
    shard = SHARD_ROWS["w_in"]

    def body(h_ref, win_ref, b_ref, wg2_ref, bg2_ref, *outs):
        w_ref = outs[9]

        @pl.when(pl.program_id(0) == 0)
        def _():
            for s in range(N_CHIPS):
                w_ref[shard * s:shard * (s + 1), :] = win_ref[s, 0:shard, :]
            w_ref[D_IN:D_IN_P, :] = jnp.zeros((D_IN_P - D_IN, D), BF16)

        proj = _dot_nt(h_ref[...], w_ref[...]) + b_ref[...]
        for pos, h in enumerate(HEAD_POS):
            outs[0][:, pos * DH:(pos + 1) * DH] = proj[:, O_QS + h * DH:O_QS + (h + 1) * DH]
        for o_ref, off, wd in zip(outs[1:8], offs[1:], widths[1:]):
            o_ref[...] = proj[:, off:off + wd]
        outs[8][...] = _dot(proj[:, O_LR:O_LR + LANE], wg2_ref[...]) + bg2_ref[...]

    return pl.pallas_call(
        body, name="in_proj", grid=(lp // tm,),
        in_specs=[_rows(tm, D), _const(w_in_windows.shape), _const((1, D_IN_P)), _const((LANE, 256)), _const((1, 256))],
        out_specs=[_rows(tm, w) for w in widths] + [_rows(tm, 256), _acc((D_IN_P, D))],
        out_shape=[pltpu.HBM((lp, w), F32) for w in widths] + [pltpu.HBM((lp, 256), F32), pltpu.HBM((D_IN_P, D), BF16)],
        compiler_params=_params(48, dimension_semantics=_seq()),
    )(*_hbm(h0, w_in_windows, b_in_p, wg2_p, bg2))


def _swa_masks(n):
    nb = SEQ // BLK
    is_meta = n == nb
    ri = _iota((BLK, BLK), 0)
    cj = _iota((BLK, BLK), 1)
    meta_col = ((cj >= META_OFF) & (cj < CH)).astype(jnp.int32)
    meta_q = meta_col * ((cj <= ri) & (ri < CH)).astype(jnp.int32)
    valid_m = jnp.where(is_meta, meta_q, meta_col) > 0
    dist_m = jnp.where(is_meta, ri - cj, n * BLK + ri + CH - cj).astype(F32)
    valid_p = jnp.where((n >= 1) & (n < nb), (cj > ri).astype(jnp.int32), 0) > 0
    dist_p = (ri + BLK - cj).astype(F32)
    valid_c = jnp.where(n < nb, (cj <= ri).astype(jnp.int32), 0) > 0
    dist_c = (ri - cj).astype(F32)
    return (dist_m, dist_p, dist_c), (valid_m, valid_p, valid_c)


def _swa_bias(n):
    dists, valids = _swa_masks(n)
    return (jnp.concatenate([-d for d in dists], axis=1),
            jnp.concatenate([jnp.where(v, 0.0, NEG) for v in valids], axis=1))


def _swa_half(ref, pos, scale=1.0):
    col = ref[:, (pos // 2) * LANE:(pos // 2 + 1) * LANE]
    lane = _iota((BLK, LANE), 1)
    mine = lane < DH if pos % 2 == 0 else lane >= DH
    return jnp.where(mine, col * scale, 0.0).astype(BF16)


def _swa_merge(even, odd):
    return jnp.where(_iota((BLK, LANE), 1) < DH, even, odd)


def _swa_softmax(t, sink):
    m = jnp.maximum(jnp.max(t, axis=-1, keepdims=True), sink)
    e = jnp.exp(t - m)
    e_sink = jnp.exp(sink - m)
    inv = 1.0 / (jnp.sum(e, axis=-1, keepdims=True) + e_sink)
    return e * inv, e_sink * inv


def _swa_kv_specs(width):
    nb = SEQ // BLK
    return [pl.BlockSpec((BLK, width), lambda n: (nb, 0)),
            pl.BlockSpec((BLK, width), lambda n: (jnp.clip(n - 1, 0, nb - 1), 0)),
            pl.BlockSpec((BLK, width), lambda n: (jnp.minimum(n, nb), 0))]


def _swa_fwd(sinks, qs, ks, vs):
    nb = SEQ // BLK
    heads = range(SWA_HEADS)

    def body(sink_ref, q_ref, km_ref, kp_ref, kc_ref, vm_ref, vp_ref, vc_ref, o_ref):
        negdist, maskbias = _swa_bias(pl.program_id(0))
        k_all = jnp.concatenate([km_ref[...], kp_ref[...], kc_ref[...]], axis=0).astype(BF16)
        v_all = jnp.concatenate([vm_ref[...], vp_ref[...], vc_ref[...]], axis=0).astype(BF16)
        q = [_swa_half(q_ref, pos, DH ** -0.5) for pos in heads]
        t = [_dot_nt(q[pos], k_all) + (2.0 ** -(HEAD_POS[pos] + 1) * negdist + maskbias) for pos in heads]
        p = [_swa_softmax(t[pos], sink_ref[HEAD_POS[pos]])[0].astype(BF16) for pos in heads]
        o = [_dot(p[pos], v_all) for pos in heads]
        for col in range(SWA_HEADS // 2):
            o_ref[:, col * LANE:(col + 1) * LANE] = _swa_merge(o[2 * col], o[2 * col + 1])

    kvw = SWA_KV_HEADS * DH
    return pl.pallas_call(
        body, name="swa_fwd", grid=(nb + 1,),
        in_specs=[pl.BlockSpec(memory_space=pltpu.SMEM), _rows(BLK, SWA_HEADS * DH)] + _swa_kv_specs(kvw) + _swa_kv_specs(kvw),
        out_specs=_rows(BLK, SWA_HEADS * DH),
        out_shape=pltpu.HBM((_lp(), SWA_HEADS * DH), F32),
        compiler_params=_params(16, dimension_semantics=_seq()),
    )(sinks, *_hbm(qs, ks, ks, ks, vs, vs, vs))


GLA_PER_STEP = BLK // CH


def _gla_block(s):
    nb = SEQ // BLK
    return jnp.where(s == 0, nb, s - 1)


def _gla_rowmask(s):
    ri = _iota((BLK, 1), 0)
    m = jnp.where(s == 0, ((ri >= META_OFF) & (ri < CH)).astype(jnp.int32), 1)
    return (m > 0).astype(F32) + jnp.zeros((BLK, 1), F32)


def _gla_chunk_masks():
    r, c = _iota((BLK, BLK), 0), _iota((BLK, BLK), 1)
    same = ((r < CH) & (c < CH)) | ((r >= CH) & (c >= CH))
    return same & (r >= c), same & (r <= c), same


def _gla_decay(z, rmask):
    log_g = (jnp.minimum(z, 0.0) - jnp.log1p(jnp.exp(-jnp.abs(z)))) * (rmask / GLA_TAU)
    lower, _, same = _gla_chunk_masks()
    return _dot_exact(lower.astype(F32), log_g), _dot_exact(same.astype(F32), log_g)


def _gla_slices(c, h):
    return slice(c * CH, (c + 1) * CH), slice(h * DK, (h + 1) * DK), slice(h * DV, (h + 1) * DV)


def _gla_fwd(qg, kg, vg, z):
    steps = SEQ // BLK + 1
    kw, vw = GLA_HEADS * DK, GLA_HEADS * DV
    pairs = [(c, h) for c in range(GLA_PER_STEP) for h in range(GLA_HEADS)]

    def body(q_ref, k_ref, v_ref, z_ref, o_ref, st_ref, st):
        s = pl.program_id(0)

        @pl.when(s == 0)
        def _():
            st[...] = jnp.zeros_like(st)

        rmask = _gla_rowmask(s)
        b, b_last = _gla_decay(z_ref[...], rmask)
        q = q_ref[...] * (rmask * DK ** -0.5)
        k = k_ref[...] * rmask
        v = v_ref[...] * rmask
        qe = q * jnp.exp(b)
        ke = k * jnp.exp(-b)
        kd = k * jnp.exp(b_last - b)
        e_last = jnp.exp(b_last)
        causal = _iota((CH, CH), 0) >= _iota((CH, CH), 1)
        a, upd, intra = {}, {}, {}
        for c, h in pairs:
            rows, ks, vs_ = _gla_slices(c, h)
            a[c, h] = jnp.where(causal, _dot_nt(qe[rows, ks], ke[rows, ks]), 0.0)
            upd[c, h] = _dot_tn(v[rows, vs_], kd[rows, ks])
        for c, h in pairs:
            rows, ks, vs_ = _gla_slices(c, h)
            intra[c, h] = _dot(a[c, h], v[rows, vs_])
        state = st[...]
        for c in range(GLA_PER_STEP):
            st_ref[0, c] = state
            for h in range(GLA_HEADS):
                rows, ks, vs_ = _gla_slices(c, h)
                o_ref[rows, vs_] = intra[c, h] + _dot_nt(qe[rows, ks], state[:, ks])
            state = state * e_last[c * CH:c * CH + 1] + jnp.concatenate([upd[c, h] for h in range(GLA_HEADS)], axis=1)
        st[...] = state

    blk = lambda w: pl.BlockSpec((BLK, w), lambda s: (_gla_block(s), 0))
    return pl.pallas_call(
        body, name="gla_fwd", grid=(steps,),
        in_specs=[blk(kw), blk(kw), blk(vw), blk(kw)],
        out_specs=[blk(vw), pl.BlockSpec((1, GLA_PER_STEP, DV, kw), lambda s: (s, 0, 0, 0))],
        out_shape=[pltpu.HBM((_lp(), vw), F32), pltpu.HBM((steps, GLA_PER_STEP, DV, kw), F32)],
        scratch_shapes=[pltpu.VMEM((DV, kw), F32)],
        compiler_params=_params(16, dimension_semantics=_seq()),
    )(*_hbm(qg, kg, vg, z))


def _post_mix(o_s, o_gla, r_g, h0, gn4, w_out, g1, b1, token):
    tm = _row_tile(384)
    lp = _lp()

    def body(os_hbm, og_hbm, r_hbm, h0_hbm, gn_ref, w_ref, g_ref, b_ref, token_ref, o_ref, pre_ref, h1_ref, *scratch):
        slot = _ring_fetch(lp // tm, tm, [os_hbm, og_hbm, r_hbm, h0_hbm], scratch[:4], scratch[4])
        os_ref, og_ref, r_ref, h0_ref = (ring.at[slot] for ring in scratch[:4])
        for pos, h in enumerate(HEAD_POS):
            o_ref[:, h * DH:(h + 1) * DH] = os_ref[:, pos * DH:(pos + 1) * DH].astype(BF16)
        for h in range(GLA_HEADS):
            hs = slice(h * DV, (h + 1) * DV)
            xg = og_ref[:, hs]
            n = xg * lax.rsqrt(jnp.mean(xg * xg, axis=-1, keepdims=True) + RMS_EPS) * gn_ref[...]
            r = r_ref[:, hs]
            o_ref[:, 512 + h * DV:512 + (h + 1) * DV] = (n * (r * _sigmoid(r))).astype(BF16)
        pre = ALPHA * h0_ref[...] + _dot(o_ref[...], w_ref[...])
        pre_ref[...] = pre
        xhat, _ = _ln_stats(pre)
        h1_ref[...] = xhat * g_ref[...] + b_ref[...]

    return pl.pallas_call(
        body, name="post_mix", grid=(lp // tm,),
        in_specs=[pl.BlockSpec(memory_space=pl.ANY)] * 4 + [_const((1, DV)), _const((D, D)),
                                                              _const((1, D)), _const((1, D)), _const(TOKEN)],
        out_specs=[_rows(tm, D), _rows(tm, D), _rows(tm, D)],
        out_shape=[pltpu.HBM((lp, D), BF16), pltpu.HBM((lp, D), F32),
                   pltpu.HBM((lp, D), F32)],
        scratch_shapes=_ring_scratch(tm, [512, 512, 512, D]),
        compiler_params=_params(40, dimension_semantics=_seq()),
    )(*_hbm(o_s, o_gla, r_g, h0, gn4, w_out, g1, b1), token)


def _ffn_fwd_loss_bwd(h1, wg_t, wu_t, wd, target, g2, b2):
    lp = _lp()
    tm = max(t for t in range(BLK, 384 + 1, BLK) if lp % t == 0)
    steps = lp // tm
    last_blk = SEQ // BLK - 1
    half = D_FF // 2
    n_t = tm // BLK

    def body(*refs):
        h_ref, wg_ref, wu_ref, wd_ref = refs[:4]
        t_refs = refs[4:4 + n_t]
        g2_ref, b2_ref, a_ref, dgate_ref, dup_ref, dp_ref, loss_ref, dg_ref, db_ref, g_s, u_s, acc = refs[4 + n_t:]
        i = pl.program_id(0)

        @pl.when(i == 0)
        def _():
            acc[...] = jnp.zeros_like(acc)
            dg_ref[...] = jnp.zeros_like(dg_ref)
            db_ref[...] = jnp.zeros_like(db_ref)

        h = h_ref[...]
        hb = h.astype(BF16)
        pre = ALPHA * h
        for j in range(2):
            cols = slice(j * half, (j + 1) * half)
            g = _dot_nt(hb, wg_ref[cols, :])
            u = _dot_nt(hb, wu_ref[cols, :])
            g_s[:, cols] = g
            u_s[:, cols] = u
            pre = pre + _dot(g * _sigmoid(g) * u, wd_ref[cols, :])
        xhat, rstd = _ln_stats(pre)
        real = i * tm + _iota((tm, 1), 0) < SEQ
        target_rows = jnp.concatenate([t[...] for t in t_refs], axis=0)
        diff = jnp.where(real, xhat * g2_ref[...] + b2_ref[...] - target_rows, 0.0)
        acc[...] += jnp.sum(diff * diff, axis=0, keepdims=True)
        dy = diff * (1.0 / D)
        dpre = _ln_bwd(dy, xhat, rstd, g2_ref[...])
        dp_ref[...] = dpre
        dg_ref[...] += jnp.sum(dy * xhat, axis=0, keepdims=True)
        db_ref[...] += jnp.sum(dy, axis=0, keepdims=True)
        dpb = dpre.astype(BF16)
        for j in range(2):
            cols = slice(j * half, (j + 1) * half)
            g, u = g_s[:, cols], u_s[:, cols]
            sg = _sigmoid(g)
            silu = g * sg
            da = _dot_nt(dpb, wd_ref[cols, :])
            a_ref[:, cols] = (silu * u).astype(BF16)
            dgate_ref[:, cols] = (da * u * (sg * (1.0 + g * (1.0 - sg)))).astype(BF16)
            dup_ref[:, cols] = (da * silu).astype(BF16)

        @pl.when(i == steps - 1)
        def _():
            loss_ref[...] = jnp.zeros_like(loss_ref) + (0.5 / D) * jnp.sum(acc[...], axis=1, keepdims=True)

    t_spec = lambda k: pl.BlockSpec((BLK, D), lambda i: (jnp.minimum(i * n_t + k, last_blk), 0))
    return pl.pallas_call(
        body, name="ffn_fwd_loss_bwd", grid=(steps,),
        in_specs=[_rows(tm, D), _const((D_FF, D)), _const((D_FF, D)), _const((D_FF, D))]
        + [t_spec(k) for k in range(n_t)] + [_const((1, D)), _const((1, D))],
        out_specs=[_rows(tm, D_FF), _rows(tm, D_FF), _rows(tm, D_FF), _rows(tm, D), _acc((1, LANE)), _acc((1, D)),
                   _acc((1, D))],
        out_shape=[pltpu.HBM((lp, D_FF), BF16)] * 3 + [pltpu.HBM((lp, D), F32), pltpu.HBM((1, LANE), F32),
                                                         pltpu.HBM((1, D), F32), pltpu.HBM((1, D), F32)],
        scratch_shapes=[pltpu.VMEM((tm, D_FF), F32), pltpu.VMEM((tm, D_FF), F32), pltpu.VMEM((1, D), F32)],
        compiler_params=_params(58, dimension_semantics=_seq()),
    )(*_hbm(h1, wg_t, wu_t, wd, *[target] * n_t, g2, b2))


def _ffn_out_bwd(dpre2, dgate, dup, pre1, wg_t, wu_t, g1, w_out, o_gla, r_g, gn4):
    tm = _row_tile(384)
    lp = _lp()

    def body(dp_ref, dg_ref, du_ref, p1_ref, wg_ref, wu_ref, g1_ref, w_ref, og_ref, r_ref, gn_ref,
             dp1_ref, dg1_ref, db1_ref, dos_ref, dog_ref, dr_ref, dgn_ref):
        @pl.when(pl.program_id(0) == 0)
        def _():
            for acc_ref in (dg1_ref, db1_ref, dgn_ref):
                acc_ref[...] = jnp.zeros_like(acc_ref)

        dh1 = ALPHA * dp_ref[...] + _dot(dg_ref[...], wg_ref[...]) + _dot(du_ref[...], wu_ref[...])
        xhat, rstd1 = _ln_stats(p1_ref[...])
        dpre1 = _ln_bwd(dh1, xhat, rstd1, g1_ref[...])
        dp1_ref[...] = dpre1
        dg1_ref[...] += jnp.sum(dh1 * xhat, axis=0, keepdims=True)
        db1_ref[...] += jnp.sum(dh1, axis=0, keepdims=True)

        do = _dot_nt(dpre1, w_ref[...])
        for pos, h in enumerate(HEAD_POS):
            dos_ref[:, pos * DH:(pos + 1) * DH] = do[:, h * DH:(h + 1) * DH]
        gn = gn_ref[...]
        for h in range(GLA_HEADS):
            hs = slice(h * DV, (h + 1) * DV)
            xg = og_ref[:, hs]
            rstd = lax.rsqrt(jnp.mean(xg * xg, axis=-1, keepdims=True) + RMS_EPS)
            nx = xg * rstd
            r = r_ref[:, hs]
            sr = _sigmoid(r)
            d_o = do[:, 512 + h * DV:512 + (h + 1) * DV]
            dr_ref[:, hs] = d_o * (nx * gn) * (sr * (1.0 + r * (1.0 - sr)))
            dn = d_o * (r * sr)
            dgn_ref[...] += jnp.sum(dn * nx, axis=0, keepdims=True)
            dnx = dn * gn
            dog_ref[:, hs] = rstd * (dnx - nx * jnp.mean(dnx * nx, axis=-1, keepdims=True))

    return pl.pallas_call(
        body, name="ffn_out_bwd", grid=(lp // tm,),
        in_specs=[_rows(tm, D), _rows(tm, D_FF), _rows(tm, D_FF), _rows(tm, D), _const((D_FF, D)), _const((D_FF, D)),
                  _const((1, D)), _const((D, D)), _rows(tm, 512), _rows(tm, 512), _const((1, DV))],
        out_specs=[_rows(tm, D), _acc((1, D)), _acc((1, D)), _rows(tm, 512), _rows(tm, 512), _rows(tm, 512),
                   _acc((1, DV))],
        out_shape=[pltpu.HBM((lp, D), F32), pltpu.HBM((1, D), F32), pltpu.HBM((1, D), F32)]
        + [pltpu.HBM((lp, 512), F32)] * 3 + [pltpu.HBM((1, DV), F32)],
        compiler_params=_params(48, dimension_semantics=_seq()),
    )(*_hbm(dpre2, dgate, dup, pre1, wg_t, wu_t, g1, w_out, o_gla, r_g, gn4))


def _atb(a, b, name, token=None, windows=None):
    lp = _lp()
    tm = _row_tile(1408)
    n, w = a.shape[1], b.shape[1]
    bw = 512 if n * w * 4 > (4 << 20) else w
    tokens = [] if token is None else [token]
    steps = lp // tm

    def body(a_ref, b_ref, *rest):
        o_ref, acc_ref = rest[len(tokens):] if windows else (rest[-1], rest[-1])

        @pl.when(pl.program_id(1) == 0)
        def _():
            acc_ref[...] = jnp.zeros_like(acc_ref)

        acc_ref[...] += _dot_tn(a_ref[...], b_ref[...])

        if windows:
            @pl.when(pl.program_id(1) == steps - 1)
            def _():
                for s, start in enumerate(windows[0]):
                    o_ref[s] = acc_ref[start:start + windows[1], :]

    if windows:
        count, height = len(windows[0]), windows[1]
        out_spec, out_shape = pl.BlockSpec((count, height, bw), lambda j, k: (0, 0, j)), (count, height, w)
    else:
        out_spec, out_shape = pl.BlockSpec((n, bw), lambda j, k: (0, j)), (n, w)
    return pl.pallas_call(
        body, name=name, grid=(w // bw, steps),
        in_specs=[pl.BlockSpec((tm, n), lambda j, k: (k, 0)), pl.BlockSpec((tm, bw), lambda j, k: (k, j))]
        + [_const(TOKEN)] * len(tokens),
        out_specs=out_spec, out_shape=pltpu.HBM(out_shape, F32),
        scratch_shapes=[pltpu.VMEM((n, bw), F32)] if windows else [],
        compiler_params=_params(48, dimension_semantics=_seq(2)),
    )(*_hbm(a, b), *tokens)


def _gla_bwd(qg, kg, vg, z, do_gla, st_all, token):
    steps = SEQ // BLK + 1
    kw, vw = GLA_HEADS * DK, GLA_HEADS * DV
    pairs = [(c, h) for c in range(GLA_PER_STEP) for h in range(GLA_HEADS)]
    heads = range(GLA_HEADS)

    def body(q_ref, k_ref, v_ref, z_ref, do_ref, st_ref, token_ref, dq_ref, dk_ref, dv_ref, dz_ref, dst):
        @pl.when(pl.program_id(0) == 0)
        def _():
            dst[...] = jnp.zeros_like(dst)

        rmask = _gla_rowmask(steps - 1 - pl.program_id(0))
        zz = z_ref[...]
        b, b_last = _gla_decay(zz, rmask)
        e_b, e_nb, e_kd, e_last = jnp.exp(b), jnp.exp(-b), jnp.exp(b_last - b), jnp.exp(b_last)
        q = q_ref[...] * (rmask * DK ** -0.5)
        k = k_ref[...] * rmask
        v = v_ref[...] * rmask
        qe, ke, kd = q * e_b, k * e_nb, k * e_kd
        d_o = do_ref[...]
        causal = _iota((CH, CH), 0) >= _iota((CH, CH), 1)
        a, da, dqe, dke, dv_intra, carry = {}, {}, {}, {}, {}, {}
        for c, h in pairs:
            rows, ks, vs_ = _gla_slices(c, h)
            a[c, h] = jnp.where(causal, _dot_nt(qe[rows, ks], ke[rows, ks]), 0.0)
            da[c, h] = jnp.where(causal, _dot_nt(d_o[rows, vs_], v[rows, vs_]), 0.0)
            carry[c, h] = _dot_tn(d_o[rows, vs_], qe[rows, ks])
        for c, h in pairs:
            rows, ks, vs_ = _gla_slices(c, h)
            dqe[c, h] = _dot(d_o[rows, vs_], st_ref[0, c][:, ks]) + _dot(da[c, h], ke[rows, ks])
            dke[c, h] = _dot_tn(da[c, h], qe[rows, ks])
            dv_intra[c, h] = _dot_tn(a[c, h], d_o[rows, vs_])
        dstate = dst[...]
        dkd, db_decay = {}, {}
        for c in reversed(range(GLA_PER_STEP)):
            for h in heads:
                rows, ks, vs_ = _gla_slices(c, h)
                dkd[c, h] = _dot(v[rows, vs_], dstate[:, ks])
                dv_ref[rows, vs_] = dv_intra[c, h] + _dot_nt(kd[rows, ks], dstate[:, ks])
            chunk_last = e_last[c * CH:c * CH + 1]
            db_decay[c] = jnp.sum(dstate * st_ref[0, c], axis=0, keepdims=True) * chunk_last
            dstate = dstate * chunk_last + jnp.concatenate([carry[c, h] for h in heads], axis=1)
        dst[...] = dstate
        rows_of = lambda parts: jnp.concatenate(
            [jnp.concatenate([parts[c, h] for h in heads], axis=1) for c in range(GLA_PER_STEP)], axis=0)
        dqe_all, dke_all, dkd_all = rows_of(dqe), rows_of(dke), rows_of(dkd)
        dq_ref[...] = dqe_all * e_b * (rmask * DK ** -0.5)
        dk_ref[...] = (dke_all * e_nb + dkd_all * e_kd) * rmask
        dkd_kd = dkd_all * kd
        db = dqe_all * qe - dke_all * ke - dkd_kd
        _, upper, same = _gla_chunk_masks()
        decay_rows = jnp.concatenate([jnp.broadcast_to(db_decay[c], (CH, kw)) for c in range(GLA_PER_STEP)], axis=0)
        dlog_g = _dot_exact(upper.astype(F32), db) + _dot_exact(same.astype(F32), dkd_kd) + decay_rows
        dz_ref[...] = dlog_g * (rmask / GLA_TAU) * _sigmoid(-zz)

    blk = lambda w: pl.BlockSpec((BLK, w), lambda s: (_gla_block(steps - 1 - s), 0))
    return pl.pallas_call(
        body, name="gla_bwd", grid=(steps,),
        in_specs=[blk(kw), blk(kw), blk(vw), blk(kw), blk(vw),
                  pl.BlockSpec((1, GLA_PER_STEP, DV, kw), lambda s: (steps - 1 - s, 0, 0, 0)), _const(TOKEN)],
        out_specs=[blk(kw), blk(kw), blk(vw), blk(kw)],
        out_shape=[pltpu.HBM((_lp(), kw), F32), pltpu.HBM((_lp(), kw), F32),
                   pltpu.HBM((_lp(), vw), F32), pltpu.HBM((_lp(), kw), F32)],
        scratch_shapes=[pltpu.VMEM((DV, kw), F32)],
        compiler_params=_params(16, dimension_semantics=_seq()),
    )(*_hbm(qg, kg, vg, z, do_gla, st_all), token)


def _swa_bwd(sinks, qs, ks, vs, do_s, token):
    nb = SEQ // BLK
    kvw = SWA_KV_HEADS * DH
    scale = DH ** -0.5
    heads = range(SWA_HEADS)

    def body(sink_ref, q_ref, km_ref, kp_ref, kc_ref, vm_ref, vp_ref, vc_ref, do_ref, token_ref,
             dq_ref, dk_ref, dv_ref, dsink_ref, carry_k, carry_v, meta_k, meta_v):
        n = pl.program_id(0)

        @pl.when(n == 0)
        def _():
            for r in (carry_k, carry_v, meta_k, meta_v):
                r[...] = jnp.zeros_like(r)
            dsink_ref[...] = jnp.zeros_like(dsink_ref)

        @pl.when(n <= nb)
        def _():
            negdist, maskbias = _swa_bias(n)
            lane = _iota((1, LANE), 1)
            k_all = jnp.concatenate([km_ref[...], kp_ref[...], kc_ref[...]], axis=0).astype(BF16)
            v_all = jnp.concatenate([vm_ref[...], vp_ref[...], vc_ref[...]], axis=0).astype(BF16)
            q = [_swa_half(q_ref, pos, scale) for pos in heads]
            d_o = [_swa_half(do_ref, pos) for pos in heads]
            t = [_dot_nt(q[pos], k_all) + (2.0 ** -(HEAD_POS[pos] + 1) * negdist + maskbias) for pos in heads]
            dp = [_dot_nt(d_o[pos], v_all) for pos in heads]
            soft = [_swa_softmax(t[pos], sink_ref[HEAD_POS[pos]]) for pos in heads]
            p = [s[0] for s in soft]
            delta = [jnp.sum(p[pos] * dp[pos], axis=-1, keepdims=True) for pos in heads]
            ds = [(p[pos] * (dp[pos] - delta[pos])).astype(BF16) for pos in heads]
            dq = [_dot(ds[pos], k_all) for pos in heads]
            for col in range(SWA_HEADS // 2):
                dq_ref[:, col * LANE:(col + 1) * LANE] = scale * _swa_merge(dq[2 * col], dq[2 * col + 1])
            dsink = jnp.zeros((1, LANE), F32)
            for pos in heads:
                dsink = dsink + jnp.where(lane == HEAD_POS[pos],
                                          -jnp.sum(soft[pos][1] * delta[pos], axis=0, keepdims=True), 0.0)
            dsink_ref[...] += dsink
            dk3 = _dot_tn(jnp.concatenate(q, axis=0), jnp.concatenate(ds, axis=0)).T
            dv3 = _dot_tn(jnp.concatenate(d_o, axis=0), jnp.concatenate([x.astype(BF16) for x in p], axis=0)).T
            meta_k[...] += dk3[0:BLK]
            meta_v[...] += dv3[0:BLK]
            dk_ref[...] = carry_k[...] + dk3[BLK:2 * BLK]
            dv_ref[...] = carry_v[...] + dv3[BLK:2 * BLK]
            carry_k[...] = dk3[2 * BLK:3 * BLK]
            carry_v[...] = dv3[2 * BLK:3 * BLK]

        @pl.when(n == nb + 1)
        def _():
            dk_ref[...] = meta_k[...]
            dv_ref[...] = meta_v[...]

    kv_out = pl.BlockSpec((BLK, kvw), lambda n: (jnp.where(n == nb + 1, nb, jnp.clip(n - 1, 0, nb - 1)), 0))
    qblk = pl.BlockSpec((BLK, SWA_HEADS * DH), lambda n: (jnp.minimum(n, nb), 0))
    return pl.pallas_call(
        body, name="swa_bwd", grid=(nb + 2,),
        in_specs=[pl.BlockSpec(memory_space=pltpu.SMEM), qblk] + _swa_kv_specs(kvw) + _swa_kv_specs(kvw)
        + [qblk, _const(TOKEN)],
        out_specs=[qblk, kv_out, kv_out, _acc((1, LANE))],
        out_shape=[pltpu.HBM((_lp(), SWA_HEADS * DH), F32), pltpu.HBM((_lp(), kvw), F32),
                   pltpu.HBM((_lp(), kvw), F32), pltpu.HBM((1, LANE), F32)],
        scratch_shapes=[pltpu.VMEM((BLK, kvw), F32)] * 4,
        compiler_params=_params(16, dimension_semantics=_seq()),
    )(sinks, *_hbm(qs, ks, ks, ks, vs, vs, vs, do_s), token)


def _in_bwd(dqs, dks, dvs, dqg, dkg, dvg, drg, dz, dpre1, w_in_t, wg2_p):
    tm = _row_tile(384)
    lp = _lp()
    widths = (512, 128, 128, 256, 256, 512, 512)
    offs = (O_QS, O_KS, O_VS, O_QG, O_KG, O_VG, O_RG)

    tiled = widths + (256, D)

    def body(*refs):
        w_ref, wg2_ref, dproj_ref, dh0_ref, dbin_ref, dbg_ref = refs[9:15]
        rings, sems = refs[15:24], refs[24]
        slot = _ring_fetch(lp // tm, tm, refs[:9], rings, sems)
        parts, dz_ref, dp1_ref = [r.at[slot] for r in rings[:7]], rings[7].at[slot], rings[8].at[slot]

        @pl.when(pl.program_id(0) == 0)
        def _():
            dbin_ref[...] = jnp.zeros_like(dbin_ref)
            dbg_ref[...] = jnp.zeros_like(dbg_ref)

        for pos, h in enumerate(HEAD_POS):
            val = parts[0][:, pos * DH:(pos + 1) * DH]
            dproj_ref[:, O_QS + h * DH:O_QS + (h + 1) * DH] = val.astype(BF16)
            dbin_ref[:, O_QS + h * DH:O_QS + (h + 1) * DH] += jnp.sum(val, axis=0, keepdims=True)
        for p_ref, off, wd in zip(parts[1:], offs[1:], widths[1:]):
            val = p_ref[...]
            dproj_ref[:, off:off + wd] = val.astype(BF16)
            dbin_ref[:, off:off + wd] += jnp.sum(val, axis=0, keepdims=True)
        dz = dz_ref[...]
        dlr = _dot_nt(dz, wg2_ref[...])
        dproj_ref[:, O_LR:O_LR + LANE] = dlr.astype(BF16)
        dbin_ref[:, O_LR:O_LR + LANE] += jnp.sum(dlr, axis=0, keepdims=True)
        dbg_ref[...] += jnp.sum(dz, axis=0, keepdims=True)
        dh0_ref[...] = ALPHA * dp1_ref[...] + _dot(dproj_ref[...], w_ref[...])

    return pl.pallas_call(
        body, name="in_bwd", grid=(lp // tm,),
        in_specs=[pl.BlockSpec(memory_space=pl.ANY)] * len(tiled) + [_const((D_IN_P, D)), _const((LANE, 256))],
        out_specs=[_rows(tm, D_IN_P), _rows(tm, D), _acc((1, D_IN_P)), _acc((1, 256))],
        out_shape=[pltpu.HBM((lp, D_IN_P), BF16), pltpu.HBM((lp, D), F32),
                   pltpu.HBM((1, D_IN_P), F32), pltpu.HBM((1, 256), F32)],
        scratch_shapes=_ring_scratch(tm, tiled),
        compiler_params=_params(48, dimension_semantics=_seq()),
    )(*_hbm(dqs, dks, dvs, dqg, dkg, dvg, drg, dz, dpre1, w_in_t, wg2_p))


def _ln_in_bwd(x, meta_ext, dh0, g, token):
    tr = min(LN_ROWS, SEQ)
    steps = SEQ // tr

    def ln_bwd(x_ref, dh_ref, g_ref, dx_ref, dg_ref, db_ref, so_far=None):
        @pl.when(pl.program_id(0) == 0)
        def _():
            dg_ref[...] = jnp.zeros_like(dg_ref) if so_far is None else so_far[0][...]
            db_ref[...] = jnp.zeros_like(db_ref) if so_far is None else so_far[1][...]

        xhat, rstd = _ln_stats(x_ref[...])
        dh = dh_ref[...]
        dx_ref[...] = _ln_bwd(dh, xhat, rstd, g_ref[...])
        dg_ref[...] += jnp.sum(dh * xhat, axis=0, keepdims=True)
        db_ref[...] += jnp.sum(dh, axis=0, keepdims=True)

    def body(x_hbm, dh_hbm, g_ref, token_ref, dx_ref, dg_ref, db_ref, x_ring, dh_ring, sems):
        slot = _ring_fetch(steps, tr, [x_hbm, dh_hbm], [x_ring, dh_ring], sems)
        ln_bwd(x_ring.at[slot], dh_ring.at[slot], g_ref, dx_ref, dg_ref, db_ref)

    def meta_body(m_ref, dh_ref, g_ref, dg_real_ref, db_real_ref, dm_ref, dg_ref, db_ref):
        ln_bwd(m_ref, dh_ref, g_ref, dm_ref, dg_ref, db_ref, (dg_real_ref, db_real_ref))

    sums = [pltpu.HBM((1, D), F32), pltpu.HBM((1, D), F32)]
    dx, dg, db = pl.pallas_call(
        body, name="ln_in_bwd", grid=(steps,),
        in_specs=[pl.BlockSpec(memory_space=pl.ANY)] * 2 + [_const((1, D)), _const(TOKEN)],
        out_specs=[_rows(tr, D), _acc((1, D)), _acc((1, D))],
        out_shape=[pltpu.HBM((SEQ, D), F32)] + sums,
        scratch_shapes=_ring_scratch(tr, [D, D]),
        compiler_params=_params(32, dimension_semantics=_seq()),
    )(*_hbm(x, dh0, g), token)
    dm, dg, db = pl.pallas_call(
        meta_body, name="ln_in_bwd_meta", grid=(1,),
        in_specs=[_const((BLK, D)), pl.BlockSpec((BLK, D), lambda i: (SEQ // BLK, 0))] + [_const((1, D))] * 3,
        out_specs=[_acc((BLK, D)), _acc((1, D)), _acc((1, D))],
        out_shape=[pltpu.HBM((BLK, D), F32)] + sums,
        compiler_params=_params(16, dimension_semantics=_seq()),
    )(*_hbm(meta_ext, dh0, g, dg, db))
    return dx, dm, dg, db


def _local_step(x, target, ln_in_g, ln_in_b, b_in, bg2, sinks, gn, g1, b1, g2, b2,
                token, fetch_first, fetch_rest, fetch_ffn, exchange_ffn, ship_ffn, exchange_w_in, ship_w_in):
    row = lambda v: v.reshape(1, -1).astype(F32)
    b_in_p = jnp.pad(row(b_in), ((0, 0), (0, D_IN_P - D_IN)))
    gn4 = row(gn)
    sinks = sinks.reshape(-1).astype(F32)

    h_real = _ln_in_fwd_real(x, row(ln_in_g), row(ln_in_b), token)
    w_in_windows, meta_full, wg2 = fetch_first([h_real])
    meta_ext = jnp.pad(meta_full, ((META_OFF, BLK - CH), (0, 0)))
    wg2_p = jnp.pad(wg2, ((0, LANE - wg2.shape[0]), (0, 0))).astype(BF16)
    h0 = _ln_in_fwd_meta(h_real, meta_ext, row(ln_in_g), row(ln_in_b))
    qs, ks, vs, qg, kg, vg, rg, glr, z, w_in_t = _in_proj(h0, w_in_windows, b_in_p, wg2_p, row(bg2))
    o_s = _swa_fwd(sinks, qs, ks, vs)
    o_gla, st_all = _gla_fwd(qg, kg, vg, z)
    w_out, token = fetch_rest([o_s, o_gla])
    o, pre1, h1 = _post_mix(o_s, o_gla, rg, h0, gn4, w_out, row(g1), row(b1), token)
    wg_t, wu_t, wd = fetch_ffn([pre1])
    a, dgate, dup, dpre2, loss, dg2, db2 = _ffn_fwd_loss_bwd(h1, wg_t, wu_t, wd, target, row(g2), row(b2))
    dpre1, dg1, db1, do_s, do_gla, drg, dgn = _ffn_out_bwd(dpre2, dgate, dup, pre1, wg_t, wu_t, row(g1), w_out, o_gla,
                                                           rg, gn4)
    dwd = _atb(a, dpre2, "dw_down")
    dwg_t = _atb(dgate, h1, "dw_gate")
    dwu_t = _atb(dup, h1, "dw_up")
    token = exchange_ffn(dict(w_out=_atb(o, dpre1, "dw_out"), w_g=dwg_t, w_u=dwu_t, w_d=dwd))
    dqg, dkg, dvg, dz = _gla_bwd(qg, kg, vg, z, do_gla, st_all, token)
    token = ship_ffn([dqg])
    dqs, dks, dvs, dsinks = _swa_bwd(sinks, qs, ks, vs, do_s, token)
    dproj, dh0, db_in_p, dbg2 = _in_bwd(dqs, dks, dvs, dqg, dkg, dvg, drg, dz, dpre1, w_in_t, wg2_p)
    token = exchange_w_in(_atb(dproj, h0, "dw_in", windows=(W_IN_STARTS, W_IN_WIN)))
    dwg2_p = _atb(glr, dz, "dw_gate_lr2", token)
    token = ship_w_in([dwg2_p])
    dx, dmeta_blk, dg_in, db_in_ln = _ln_in_bwd(x, meta_ext, dh0, row(ln_in_g), token)

    small = dict(meta_blk=dmeta_blk, ln_in_g=dg_in, ln_in_b=db_in_ln, ln1_g=dg1, ln1_b=db1, ln2_g=dg2, ln2_b=db2,
                 b_in_p=db_in_p, wg2_p=dwg2_p, bg2=dbg2, sinks=dsinks, gn=dgn, loss=loss)
    return dx, small


HBM = pl.BlockSpec(memory_space=pltpu.HBM)


def _place():
    return lax.axis_index("x"), lax.axis_index("y"), lax.axis_index("c")


def _other_chips(x, y):
    return [(1 - x, y), (x, 1 - y), (1 - x, 1 - y)]


def _dma_sems(n):
    return pltpu.SemaphoreType.DMA((n,))


def _comm_params():
    return pltpu.CompilerParams(has_side_effects=True)


SEM = pl.BlockSpec(memory_space=pltpu.SEMAPHORE)


PER_ARRAY = dict(gather=3, scatter=3, sibling=N_CHIPS)


def _ici_copies(kind, landing, srcs, lands, send_sems, recv_sems):
    x, y, c = _place()
    mine = 2 * x + y
    copies = []
    for a in range(len(srcs)):
        if kind == "sibling":
            for s in range(N_CHIPS):
                copies.append(pltpu.make_async_remote_copy(
                    srcs[a].at[s, 1 - c], lands[a].at[s], send_sems.at[N_CHIPS * a + s], recv_sems.at[N_CHIPS * a + s],
                    device_id=(x, y, 1 - c), device_id_type=MESH))
            continue
        for j, (px, py) in enumerate(_other_chips(x, y)):
            slab = 2 * px + py if landing else mine
            if kind == "gather":
                src, dst = srcs[a].at[c], lands[a].at[slab, c]
            else:
                src, dst = srcs[a].at[2 * px + py], lands[a].at[slab]
            copies.append(pltpu.make_async_remote_copy(src, dst, send_sems.at[3 * a + j], recv_sems.at[3 * a + j],
                                                       device_id=(px, py, c), device_id_type=MESH))
    return copies


def _split_params():
    return pltpu.CompilerParams(has_side_effects=pltpu.SideEffectType.DATAFLOW_SIDE_EFFECTING)


def _ici_start(kind, srcs, land_shapes, after, name):
    n = len(srcs)
    lands = [pltpu.with_memory_space_constraint(lax.empty(s, a.dtype), pltpu.HBM) for s, a in zip(land_shapes, srcs)]

    def body(*refs):
        outs = refs[2 * n + len(after):]
        for cp in _ici_copies(kind, False, refs[:n], refs[n:2 * n], outs[0], outs[1]):
            cp.start()
        outs[-1][...] = jnp.zeros(TOKEN, F32)

    outs = pl.pallas_call(
        body, name=name, in_specs=[HBM] * (2 * n) + [pl.BlockSpec(memory_space=pl.ANY)] * len(after),
        out_specs=[SEM, SEM] + [HBM] * (2 * n) + [pl.BlockSpec(memory_space=pltpu.VMEM)],
        out_shape=[_dma_sems(PER_ARRAY[kind] * n)] * 2 + [pltpu.HBM(a.shape, a.dtype) for a in list(srcs) + lands]
        + [jax.ShapeDtypeStruct(TOKEN, F32)],
        input_output_aliases={i: 2 + i for i in range(2 * n)},
        compiler_params=_split_params(),
    )(*_hbm(*srcs), *lands, *after)
    return outs[:-1], outs[-1]


def _ici_wait(kind, handle, after, name):
    n = (len(handle) - 2) // 2

    def body(*refs):
        for cp in _ici_copies(kind, True, refs[:n], refs[n:2 * n], refs[2 * n], refs[2 * n + 1]):
            cp.wait_send()
            cp.wait_recv()

    outs = pl.pallas_call(
        body, name=name, in_specs=[HBM] * (2 * n) + [SEM, SEM] + [pl.BlockSpec(memory_space=pl.ANY)] * len(after),
        out_specs=[HBM] * (2 * n), out_shape=[pltpu.HBM(a.shape, a.dtype) for a in handle[2:]],
        input_output_aliases={i: i for i in range(2 * n)},
        compiler_params=_split_params(),
    )(*handle[2:], handle[0], handle[1], *after)
    return list(outs[:n]), list(outs[n:])


def _forward_copies(landing, arrs, send_sems, recv_sems):
    x, y, c = _place()
    copies = []
    for a in range(len(arrs)):
        for j, (px, py) in enumerate(_other_chips(x, y)):
            half = 1 - c if landing else c
            copies.append(pltpu.make_async_remote_copy(
                arrs[a].at[2 * px + py, c], arrs[a].at[2 * px + py, half], send_sems.at[3 * a + j],
                recv_sems.at[3 * a + j], device_id=(x, y, 1 - c), device_id_type=MESH))
    return copies


def _gather_wait_forward(handle, groups, after, name):
    n = (len(handle) - 2) // 2
    assert sum(groups) == n

    def body(*refs):
        outs = refs[2 * n + 2 + len(after):]
        lands, sems = outs[n:2 * n], outs[2 * n:-1]
        arrivals = _ici_copies("gather", True, refs[:n], refs[n:2 * n], refs[2 * n], refs[2 * n + 1])
        sends, first = [], 0
        for g, count in enumerate(groups):
            sends += _forward_copies(False, lands[first:first + count], sems[2 * g], sems[2 * g + 1])
            first += count
        for cp, send in zip(arrivals, sends):
            cp.wait_recv()
            send.start()
        for cp in arrivals:
            cp.wait_send()
        outs[-1][...] = jnp.zeros(TOKEN, F32)

    outs = pl.pallas_call(
        body, name=name, in_specs=[HBM] * (2 * n) + [SEM, SEM] + [pl.BlockSpec(memory_space=pl.ANY)] * len(after),
        out_specs=[HBM] * (2 * n) + [SEM] * (2 * len(groups)) + [pl.BlockSpec(memory_space=pltpu.VMEM)],
        out_shape=[pltpu.HBM(a.shape, a.dtype) for a in handle[2:]]
        + [_dma_sems(3 * count) for count in groups for _ in range(2)] + [jax.ShapeDtypeStruct(TOKEN, F32)],
        input_output_aliases={i: i for i in range(2 * n)},
        compiler_params=_split_params(),
    )(*handle[2:], handle[0], handle[1], *after)
    lands, sems, handles, first = outs[n:2 * n], outs[2 * n:-1], [], 0
    for g, count in enumerate(groups):
        handles.append([sems[2 * g], sems[2 * g + 1], *lands[first:first + count]])
        first += count
    return list(outs[:n]), handles, outs[-1]


def _forward_wait(handle, after, name):
    n = len(handle) - 2

    def body(*refs):
        for cp in _forward_copies(True, refs[:n], refs[n], refs[n + 1]):
            cp.wait_send()
            cp.wait_recv()

    return list(pl.pallas_call(
        body, name=name, in_specs=[HBM] * n + [SEM, SEM] + [pl.BlockSpec(memory_space=pl.ANY)] * len(after),
        out_specs=[HBM] * n, out_shape=[pltpu.HBM(a.shape, a.dtype) for a in handle[2:]],
        input_output_aliases={i: i for i in range(n)},
        compiler_params=_split_params(),
    )(*handle[2:], handle[0], handle[1], *after))


def _add_halves(core, grads, recvs, dtypes, name):
    n = len(grads)
    heights = [g.shape[2] for g in grads]

    def body(c_ref, *refs):
        for a in range(n):
            refs[2 * n + a][...] = (refs[2 * a][0] + refs[2 * a + 1][...]).astype(dtypes[a])

    slab = lambda h: pl.BlockSpec((1, h, D), lambda s, c: (s, 0, 0))
    mine = lambda h: pl.BlockSpec((1, 1, h, D), lambda s, c: (s, c[0], 0, 0))
    return pl.pallas_call(
        body, name=name,
        grid_spec=pltpu.PrefetchScalarGridSpec(
            num_scalar_prefetch=1, grid=(N_CHIPS,),
            in_specs=[spec(h) for h in heights for spec in (mine, slab)], out_specs=[slab(h) for h in heights]),
        out_shape=[pltpu.HBM((N_CHIPS, h, D), dt) for h, dt in zip(heights, dtypes)],
        compiler_params=_params(32, dimension_semantics=_seq()),
    )(core, *_hbm(*[a for pair in zip(grads, recvs) for a in pair]))


N_DEVICES = 2 * N_CHIPS
PEER_FLIPS = [(dx, dy, dc) for dx in (0, 1) for dy in (0, 1) for dc in (0, 1)][1:]


def _small_copies(landing, p_ref, out_ref, send_sems, recv_sems):
    x, y, c = _place()
    flip = lambda v, d: 1 - v if d else v
    copies = []
    for k, flips in enumerate(PEER_FLIPS):
        px, py, pc = (flip(v, d) for v, d in zip((x, y, c), flips))
        slab = 4 * px + 2 * py + pc if landing else 4 * x + 2 * y + c
        copies.append(pltpu.make_async_remote_copy(p_ref, out_ref.at[slab], send_sems.at[k], recv_sems.at[k],
                                                   device_id=(px, py, pc), device_id_type=MESH))
    return copies


def _small_wait(handle, after):
    def body(p_ref, land_ref, send_sems, recv_sems, *rest):
        for cp in _small_copies(True, p_ref, land_ref, send_sems, recv_sems):
            cp.wait_send()
            cp.wait_recv()

    return pl.pallas_call(
        body, name="small_exchange_wait", in_specs=[HBM, HBM, SEM, SEM] + [pl.BlockSpec(memory_space=pl.ANY)] * len(after),
        out_specs=[HBM, HBM], out_shape=[pltpu.HBM(a.shape, F32) for a in handle[2:]],
        input_output_aliases={0: 0, 1: 1},
        compiler_params=_split_params(),
    )(handle[2], handle[3], handle[0], handle[1], *after)


def _sum_chips(slots, firsts, rests, after, name):
    n = len(firsts)

    def body(i_ref, *refs):
        outs = refs[4 * n + len(after):]
        for a in range(n):
            first, r1, r2, r3 = refs[4 * a:4 * a + 4]
            outs[a][...] = ((first[...].astype(F32) + r1[...].astype(F32)) + r2[...].astype(F32)) + r3[...].astype(F32)

    slab = lambda h, k: pl.BlockSpec((1, h, D), lambda i, ix: (ix[k], 0, 0))
    heights = [f.shape[1] for f in firsts]
    return pl.pallas_call(
        body, name=name,
        grid_spec=pltpu.PrefetchScalarGridSpec(
            num_scalar_prefetch=1, grid=(1,),
            in_specs=[slab(h, k) for h in heights for k in range(4)] + [pl.BlockSpec(memory_space=pl.ANY)] * len(after),
            out_specs=[slab(h, 4) for h in heights]),
        out_shape=[pltpu.HBM((2, h, D), F32) for h in heights],
        compiler_params=_params(48, dimension_semantics=_seq()),
    )(slots, *_hbm(*[a for f, r in zip(firsts, rests) for a in (f, r, r, r)]), *after)


def _join_copies(landing, arrs, send_sems, recv_sems):
    x, y, c = _place()
    slab = 1 - c if landing else c
    return [pltpu.make_async_remote_copy(arr.at[slab], arr.at[slab], send_sems.at[a], recv_sems.at[a],
                                         device_id=(x, y, 1 - c), device_id_type=MESH) for a, arr in enumerate(arrs)]


def _join_halves(halves, name):
    n = len(halves)

    def body(*refs):
        outs = refs[n:2 * n]
        send_sems, recv_sems = refs[2 * n:]
        sends = _join_copies(False, outs, send_sems, recv_sems)
        for cp in sends:
            cp.start()
        for cp in _join_copies(True, outs, send_sems, recv_sems):
            cp.wait_recv()
        for cp in sends:
            cp.wait_send()

    return list(pl.pallas_call(
        body, name=name, in_specs=[HBM] * n, out_specs=[HBM] * n,
        out_shape=[pltpu.HBM(h.shape, F32) for h in halves],
        input_output_aliases={a: a for a in range(n)},
        scratch_shapes=[_dma_sems(n)] * 2,
        compiler_params=_comm_params(),
    )(*_hbm(*halves)))


def _join_small_start(halves, pack, name):
    n, peers = len(halves), len(PEER_FLIPS)
    land = pltpu.with_memory_space_constraint(lax.empty((N_DEVICES,) + pack.shape, F32), pltpu.HBM)

    def body(*refs):
        outs = refs[n + 2:]
        for cp in _join_copies(False, refs[:n], outs[0], outs[1]):
            cp.start()
        for cp in _small_copies(False, refs[n], refs[n + 1], outs[2], outs[3]):
            cp.start()
        outs[-1][...] = jnp.zeros(TOKEN, F32)

    outs = pl.pallas_call(
        body, name=name, in_specs=[HBM] * (n + 2),
        out_specs=[SEM] * 4 + [HBM] * (n + 2) + [pl.BlockSpec(memory_space=pltpu.VMEM)],
        out_shape=[_dma_sems(n)] * 2 + [_dma_sems(peers)] * 2
        + [pltpu.HBM(a.shape, a.dtype) for a in list(halves) + [pack, land]] + [jax.ShapeDtypeStruct(TOKEN, F32)],
        input_output_aliases={i: 4 + i for i in range(n + 2)},
        compiler_params=_split_params(),
    )(*_hbm(*halves, pack), land)
    return [outs[0], outs[1], *outs[4:4 + n]], [outs[2], outs[3], outs[4 + n], outs[5 + n]], outs[-1]


def _join_wait(handle, after, name):
    n = len(handle) - 2

    def body(*refs):
        for cp in _join_copies(True, refs[:n], refs[n], refs[n + 1]):
            cp.wait_send()
            cp.wait_recv()

    return list(pl.pallas_call(
        body, name=name, in_specs=[HBM] * n + [SEM, SEM] + [pl.BlockSpec(memory_space=pl.ANY)] * len(after),
        out_specs=[HBM] * n, out_shape=[pltpu.HBM(a.shape, a.dtype) for a in handle[2:]],
        input_output_aliases={i: i for i in range(n)},
        compiler_params=_split_params(),
    )(*handle[2:], handle[0], handle[1], *after))


def _chip_partials(grads, fetched, wire_dtypes, name):
    core = lax.axis_index("c").astype(jnp.int32).reshape(1)
    return list(_add_halves(core, grads, fetched, wire_dtypes, name))


def _chip_sums(parts, got, after, name):
    x, y, c = _place()
    others = [2 * px + py for px, py in _other_chips(x, y)]
    own_first = jnp.stack([2 * x + y] + others + [c]).astype(jnp.int32)
    return list(_sum_chips(own_first, parts, got, after, name))


ADAMW_STEPS = 8


def _adamw(params, by_row, chip, window_step):
    n = len(params)
    rows, _, cols = by_row[0].shape
    block = lambda shape: pl.BlockSpec((shape[0] // ADAMW_STEPS, shape[1]), lambda i, c: (i, 0))
    assert all(a.shape[0] % (8 * ADAMW_STEPS) == 0 for p in params for a in p)

    def body(c_ref, *refs):
        w_hbm, g_ref, m_hbm, v_hbm = refs[4 * n:4 * n + 4]
        results, (ins_ref, outs_ref, sems) = refs[8 * n + 4:8 * n + 8], refs[8 * n + 8:]
        loads = [pltpu.make_async_copy(src.at[:, 0, :], ins_ref.at[k], sems.at[k])
                 for k, src in enumerate((w_hbm, m_hbm, v_hbm))]
        stores = [pltpu.make_async_copy(outs_ref.at[k], dst.at[:, 0, :], sems.at[3 + k]) for k, dst in enumerate(results)]
        first = pl.program_id(0) == 0

        @pl.when(first)
        def _():
            for cp in loads:
                cp.start()

        for a in range(n):
            w_ref, a_g_ref, m_ref, v_ref = refs[4 * a:4 * a + 4]
            outs = refs[4 * n + 4 + 4 * a:4 * n + 8 + 4 * a]
            g = a_g_ref[...]
            outs[0][...] = g
            outs[1][...], outs[2][...], outs[3][...] = _adamw_math(w_ref[...], g, m_ref[...], v_ref[...])

        @pl.when(first)
        def _():
            for cp in loads:
                cp.wait()
            for lo in range(0, cols, LANE):
                lanes = slice(lo, lo + LANE)
                g = g_ref[0:rows, lanes]
                for s in range(1, N_CHIPS):
                    g = jnp.where(c_ref[0] == s, g_ref[s * window_step:s * window_step + rows, lanes], g)
                outs_ref[0, :, lanes] = g
                outs_ref[1, :, lanes], outs_ref[2, :, lanes], outs_ref[3, :, lanes] = _adamw_math(
                    ins_ref[0, :, lanes], g, ins_ref[1, :, lanes], ins_ref[2, :, lanes])
            for cp in stores:
                cp.start()

        @pl.when(pl.program_id(0) == ADAMW_STEPS - 1)
        def _():
            for cp in stores:
                cp.wait()

    outs = pl.pallas_call(
        body, name="adamw_matrices",
        grid_spec=pltpu.PrefetchScalarGridSpec(
            num_scalar_prefetch=1, grid=(ADAMW_STEPS,),
            in_specs=[block(a.shape) for p in params for a in p] + [HBM, _const(by_row[1].shape), HBM, HBM],
            out_specs=[block(p[0].shape) for p in params for _ in range(4)] + [HBM] * 4,
            scratch_shapes=[pltpu.VMEM((3, rows, cols), F32), pltpu.VMEM((4, rows, cols), F32), _dma_sems(7)]),
        out_shape=[pltpu.HBM(p[0].shape, F32) for p in params for _ in range(4)] + [pltpu.HBM((rows, 1, cols), F32)] * 4,
        compiler_params=_params(48, dimension_semantics=_seq()),
    )(chip, *_hbm(*[a for p in params for a in p], *by_row))
    return [outs[4 * a:4 * a + 4] for a in range(n)], outs[4 * n:]


def _adamw_math(w, g, m, v):
    nm = ADAM_B1 * m + (1.0 - ADAM_B1) * g
    nv = ADAM_B2 * v + (1.0 - ADAM_B2) * (g * g)
    m_hat = nm / (1.0 - ADAM_B1 ** ADAM_STEP)
    v_hat = nv / (1.0 - ADAM_B2 ** ADAM_STEP)
    return -ADAM_LR * (m_hat / (jnp.sqrt(v_hat) + ADAM_EPS) + ADAM_WD * w), nm, nv


SMALL = (("meta_tokens", (N_META, D // N_CHIPS)), ("ln_in_g", (1, D)), ("ln_in_b", (1, D)), ("b_in", (1, D_IN)),
         ("w_gate_lr2", (GATE_RANK, GLA_HEADS * DK // N_CHIPS)), ("b_gate_lr2", (1, GLA_HEADS * DK)),
         ("attn_sinks", (1, SWA_HEADS)),
         ("gla_norm_g", (1, DV)), ("ln1_g", (1, D)), ("ln1_b", (1, D)), ("ln2_g", (1, D)), ("ln2_b", (1, D)))
ROW_META, ROW_B_IN, ROW_TAIL, ROW_WG2 = 0, 22, 25, 32
ROW_LN = dict(ln_in_g=16, ln_in_b=17, ln1_g=18, ln1_b=19, ln2_g=20, ln2_b=21)
TAIL_BG2, TAIL_SINKS, TAIL_GN, TAIL_LOSS = 0, 256, 256 + SWA_HEADS, 256 + SWA_HEADS + DV


def _adamw_small(place, packs, own, params):
    n = len(SMALL)

    def body(place_ref, packs_ref, own_ref, *refs):
        ins, outs, p_ref = refs[:3 * n], refs[3 * n:-1], refs[-1]
        me, c = place_ref[0], place_ref[1]
        total = jnp.where(me == 0, own_ref[...], packs_ref[0])
        for i in range(1, N_DEVICES):
            total = total + jnp.where(me == i, own_ref[...], packs_ref[i])
        p_ref[...] = total
        outs[4 * n][...] = total[ROW_TAIL:ROW_TAIL + 1, TAIL_LOSS:TAIL_LOSS + 1]

        def mine(width, rows):
            part = lambda s: p_ref[rows, s * width:(s + 1) * width]
            return jnp.where(c == 0, part(0), jnp.where(c == 1, part(1), jnp.where(c == 2, part(2), part(3))))

        tail = lambda lo, width: p_ref[ROW_TAIL:ROW_TAIL + 1, lo:lo + width]
        grads = dict(
            meta_tokens=mine(D // N_CHIPS, slice(ROW_META, ROW_META + N_META)),
            b_in=jnp.concatenate([p_ref[ROW_B_IN:ROW_B_IN + 1, :], p_ref[ROW_B_IN + 1:ROW_B_IN + 2, :],
                                  p_ref[ROW_B_IN + 2:ROW_B_IN + 3, 0:D_IN - 2 * D]], axis=1),
            w_gate_lr2=mine(256 // N_CHIPS, slice(ROW_WG2, ROW_WG2 + 16)),
            b_gate_lr2=tail(TAIL_BG2, 256), attn_sinks=tail(TAIL_SINKS, SWA_HEADS), gla_norm_g=tail(TAIL_GN, DV),
            **{k: p_ref[r:r + 1, :] for k, r in ROW_LN.items()})
        for i, (name, _) in enumerate(SMALL):
            g = grads[name]
            outs[4 * i][...] = g
            outs[4 * i + 1][...], outs[4 * i + 2][...], outs[4 * i + 3][...] = _adamw_math(
                ins[3 * i][...], g, ins[3 * i + 1][...], ins[3 * i + 2][...])

    whole = lambda shape: pl.BlockSpec(shape, lambda i, c: (0,) * len(shape))
    outs = pl.pallas_call(
        body, name="adamw_small",
        grid_spec=pltpu.PrefetchScalarGridSpec(
            num_scalar_prefetch=1, grid=(1,),
            in_specs=[whole(packs.shape), whole(own.shape)] + [whole(s) for _, s in SMALL for _ in range(3)],
            out_specs=[whole(s) for _, s in SMALL for _ in range(4)] + [whole((1, 1))],
            scratch_shapes=[pltpu.VMEM(own.shape, F32)]),
        out_shape=[pltpu.HBM(s, F32) for _, s in SMALL for _ in range(4)] + [pltpu.HBM((1, 1), F32)],
        compiler_params=_params(16, dimension_semantics=_seq()),
    )(place, *_hbm(packs, own, *[a for p in params for a in p]))
    return [outs[4 * i:4 * i + 4] for i in range(n)], outs[4 * n]


def _small_pack(gr):
    names = ["meta_blk"] + list(ROW_LN) + ["b_in_p", "wg2_p", "bg2", "sinks", "gn", "loss"]
    gate_w = GLA_HEADS * DK

    def body(*refs):
        src, out = dict(zip(names, refs)), refs[-1]
        out[...] = jnp.zeros_like(out)
        out[ROW_META:ROW_META + N_META, :] = src["meta_blk"][META_OFF:CH, :]
        for k, r in ROW_LN.items():
            out[r:r + 1, :] = src[k][...]
        for j in range(-(-D_IN // D)):
            width = min(D, D_IN - j * D)
            out[ROW_B_IN + j:ROW_B_IN + j + 1, 0:width] = src["b_in_p"][:, j * D:j * D + width]
        tail = slice(ROW_TAIL, ROW_TAIL + 1)
        out[tail, TAIL_BG2:TAIL_BG2 + gate_w] = src["bg2"][...]
        out[tail, TAIL_SINKS:TAIL_SINKS + SWA_HEADS] = src["sinks"][:, 0:SWA_HEADS]
        out[tail, TAIL_GN:TAIL_GN + DV] = src["gn"][...]
        out[tail, TAIL_LOSS:TAIL_LOSS + 1] = src["loss"][:, 0:1]
        out[ROW_WG2:ROW_WG2 + GATE_RANK, 0:gate_w] = src["wg2_p"][0:GATE_RANK, :]

    arrays = [gr[k] for k in names]
    return pl.pallas_call(
        body, name="small_pack", grid=(1,),
        in_specs=[_acc(a.shape) for a in arrays], out_specs=_acc((SMALL_ROWS, D)),
        out_shape=pltpu.HBM((SMALL_ROWS, D), F32),
        compiler_params=_params(16, dimension_semantics=_seq()),
    )(*_hbm(*arrays))


BIG = ("w_in", "w_out", "w_g", "w_u", "w_d")


def kernel(x, meta_tokens, ln_in_g, ln_in_b, w_in, b_in, w_gate_lr2, b_gate_lr2, attn_sinks, gla_norm_g, w_out, ln1_g, ln1_b, w_ffn_gate, w_ffn_up, w_ffn_down, ln2_g, ln2_b, loss_target, m_meta_tokens, m_ln_in_g, m_ln_in_b, m_w_in, m_b_in, m_w_gate_lr2, m_b_gate_lr2, m_attn_sinks, m_gla_norm_g, m_w_out, m_ln1_g, m_ln1_b, m_w_ffn_gate, m_w_ffn_up, m_w_ffn_down, m_ln2_g, m_ln2_b, v_meta_tokens, v_ln_in_g, v_ln_in_b, v_w_in, v_b_in, v_w_gate_lr2, v_b_gate_lr2, v_attn_sinks, v_gla_norm_g, v_w_out, v_ln1_g, v_ln1_b, v_w_ffn_gate, v_w_ffn_up, v_w_ffn_down, v_ln2_g, v_ln2_b):
    chip = 2 * lax.axis_index("x") + lax.axis_index("y")

    halves = lambda a: a.reshape(2, a.shape[0] // 2, a.shape[1])
    r_in = SHARD_ROWS["w_in"]
    first = [halves(a) for a in (jnp.pad(w_in[0].T.astype(BF16), ((0, W_IN_WIN - r_in), (0, 0))), meta_tokens,
                                 w_gate_lr2[0])]
    rest = [halves(a) for a in (w_out[0].astype(BF16), w_ffn_gate[0].T.astype(BF16), w_ffn_up[0].T.astype(BF16),
                                w_ffn_down[0].astype(BF16))]
    lands = lambda arrs: [(N_CHIPS,) + a.shape for a in arrs]
    first_handle, first_token = _ici_start("gather", first, lands(first), [], "gather_first_start")
    rest_handle, token = _ici_start("gather", rest, lands(rest), [first_token], "gather_rest_start")
    own_slab = lambda got, shards: [lax.dynamic_update_index_in_dim(g, s, chip, axis=0) for g, s in zip(got, shards)]
    fetching = {}

    def fetch_first(after):
        shards, (forwarding,), _ = _gather_wait_forward(first_handle, [len(first)], after, "gather_first_wait")
        g_in, g_meta, g_wg2 = own_slab(_forward_wait(forwarding, [], "gather_first_forward_wait"), shards)
        w_in_windows = g_in.reshape(N_CHIPS, W_IN_WIN, D)
        meta_full = jnp.concatenate([g_meta[s].reshape(N_META, -1) for s in range(N_CHIPS)], axis=1)
        wg2_full = jnp.concatenate([g_wg2[s].reshape(w_gate_lr2.shape[1], -1) for s in range(N_CHIPS)], axis=1)
        return w_in_windows, meta_full, wg2_full

    def fetch_rest(after):
        shards, (w_out_forwarding, fetching["handle"]), forward_token = _gather_wait_forward(
            rest_handle, [1, len(rest) - 1], after, "gather_rest_wait")
        g_out, = own_slab(_forward_wait(w_out_forwarding, [], "gather_w_out_forward_wait"), shards[:1])
        fetching["shards"] = shards[1:]
        return g_out.reshape(-1, D), forward_token

    def fetch_ffn(after):
        got = _forward_wait(fetching["handle"], after, "gather_ffn_forward_wait")
        return [g.reshape(-1, D) for g in own_slab(got, fetching["shards"])]

    sent = {}
    split = lambda grads: [g.reshape(N_CHIPS, 2, -1, D) for g in grads]

    def exchange(key, grads):
        grads = split(grads)
        sent[key + "_halves"], exchange_token = _ici_start(
            "sibling", grads, [(N_CHIPS,) + a.shape[2:] for a in grads], [], "sibling_" + key + "_start")
        return exchange_token

    def ship(key, after):
        grads, fetched = _ici_wait("sibling", sent[key + "_halves"], after, "sibling_" + key + "_wait")
        parts = _chip_partials(grads, fetched, [BF16] * len(grads), "add_halves_" + key)
        sent[key], ship_token = _ici_start("scatter", parts, [p.shape for p in parts], [], "scatter_" + key + "_start")
        return ship_token

    dx, gr = _local_step(
        x[0], loss_target[0], ln_in_g, ln_in_b, b_in[0], b_gate_lr2[0], attn_sinks[0], gla_norm_g[0], ln1_g[0],
        ln1_b[0], ln2_g[0], ln2_b[0], token, fetch_first, fetch_rest, fetch_ffn,
        lambda g: exchange("ffn", [g[k] for k in BIG[1:]]), lambda after: ship("ffn", after),
        lambda g: exchange("w_in", [g]), lambda after: ship("w_in", after))
    ffn_parts, ffn_got = _ici_wait("scatter", sent["ffn"], [dx], "scatter_ffn_wait")
    join_handle, small_handle, token = _join_small_start(
        _chip_sums(ffn_parts, ffn_got, [], "sum_chips_ffn"), _small_pack(gr), "join_ffn_small_start")
    w_in_parts, w_in_got = _ici_wait("scatter", sent["w_in"], [token], "scatter_w_in_wait")
    w_in_joined = _join_halves(_chip_sums(w_in_parts, w_in_got, [], "sum_chips_w_in"), "join_w_in")
    red = [f.reshape(2 * f.shape[1], D) for f in w_in_joined + _join_wait(join_handle, w_in_joined, "join_ffn_wait")]

    big_g = dict(zip(BIG, red))
    weights = dict(meta_tokens=meta_tokens, ln_in_g=ln_in_g, ln_in_b=ln_in_b, w_in=w_in, b_in=b_in,
                   w_gate_lr2=w_gate_lr2, b_gate_lr2=b_gate_lr2, attn_sinks=attn_sinks, gla_norm_g=gla_norm_g,
                   w_out=w_out, ln1_g=ln1_g, ln1_b=ln1_b, w_ffn_gate=w_ffn_gate, w_ffn_up=w_ffn_up,
                   w_ffn_down=w_ffn_down, ln2_g=ln2_g, ln2_b=ln2_b)
    m_in = dict(meta_tokens=m_meta_tokens, ln_in_g=m_ln_in_g, ln_in_b=m_ln_in_b, w_in=m_w_in, b_in=m_b_in,
                w_gate_lr2=m_w_gate_lr2, b_gate_lr2=m_b_gate_lr2, attn_sinks=m_attn_sinks, gla_norm_g=m_gla_norm_g,
                w_out=m_w_out, ln1_g=m_ln1_g, ln1_b=m_ln1_b, w_ffn_gate=m_w_ffn_gate, w_ffn_up=m_w_ffn_up,
                w_ffn_down=m_w_ffn_down, ln2_g=m_ln2_g, ln2_b=m_ln2_b)
    v_in = dict(meta_tokens=v_meta_tokens, ln_in_g=v_ln_in_g, ln_in_b=v_ln_in_b, w_in=v_w_in, b_in=v_b_in,
                w_gate_lr2=v_w_gate_lr2, b_gate_lr2=v_b_gate_lr2, attn_sinks=v_attn_sinks, gla_norm_g=v_gla_norm_g,
                w_out=v_w_out, ln1_g=v_ln1_g, ln1_b=v_ln1_b, w_ffn_gate=v_w_ffn_gate, w_ffn_up=v_w_ffn_up,
                w_ffn_down=v_w_ffn_down, ln2_g=v_ln2_g, ln2_b=v_ln2_b)
    names = list(weights)
    big_names = ("w_in", "w_out", "w_ffn_gate", "w_ffn_up", "w_ffn_down")

    grads, delta, new_m, new_v = {}, {}, {}, {}
    flips = [(lambda a: a.T) if kk in ("w_g", "w_u") else (lambda a: a) for kk in BIG[1:]]
    by_row = lambda a: jnp.transpose(a, (2, 0, 1))
    updated, updated_w_in = _adamw(
        [(flip(weights[k][0]), big_g[kk], flip(m_in[k][0]), flip(v_in[k][0]))
         for k, kk, flip in zip(big_names[1:], BIG[1:], flips)],
        (by_row(w_in), big_g["w_in"], by_row(m_w_in), by_row(v_w_in)), chip.astype(jnp.int32).reshape(1), r_in % BF16_ROWS)
    for k, flip, results in zip(big_names[1:], flips, updated):
        grads[k], delta[k], new_m[k], new_v[k] = (flip(t)[None] for t in results)
    grads["w_in"], delta["w_in"], new_m["w_in"], new_v["w_in"] = (jnp.transpose(t, (1, 2, 0)) for t in updated_w_in)
    small_in = [tuple(src[k].reshape(shape) for src in (weights, m_in, v_in)) for k, shape in SMALL]
    place = jnp.stack([2 * chip + lax.axis_index("c"), chip]).astype(jnp.int32)
    small_own, small_all = _small_wait(small_handle, [updated[0][0]])
    small_out, loss = _adamw_small(place, small_all, small_own, small_in)
    for (k, _), results in zip(SMALL, small_out):
        grads[k], delta[k], new_m[k], new_v[k] = (r.reshape(weights[k].shape) for r in results)

    return (loss.reshape(()), dx[None], *[grads[k] for k in names], *[delta[k] for k in names], *[new_m[k] for k in names],
            *[new_v[k] for k in names])
```

```python
import jax
import jax.numpy as jnp
from jax import lax
from jax.experimental import pallas as pl
from jax.experimental.pallas import tpu as pltpu

F32 = jnp.float32
BF16 = jnp.bfloat16
MESH = pl.DeviceIdType.MESH

D = 1024
SEQ = 4096
N_META = 16
SWA_HEADS, SWA_KV_HEADS, DH = 8, 2, 64
WINDOW = 128
GLA_HEADS, DK, DV = 4, 64, 128
GLA_TAU = 16.0
CH = 64
D_FF = 2816
D_IN = 2320
LN_EPS = 1e-5
RMS_EPS = 1e-6
ALPHA = 2.0 ** 0.25
NEG = -1e30
ADAM_LR, ADAM_B1, ADAM_B2, ADAM_EPS, ADAM_WD, ADAM_STEP = 0.001, 0.9, 0.999, 1e-8, 0.01, 10
O_QS, O_KS, O_VS, O_QG, O_KG, O_VG, O_RG, O_LR = 0, 512, 640, 768, 1024, 1280, 1792, 2304

LANE = 128
BLK = WINDOW
GATE_RANK = 16
D_IN_P = D_IN + LANE - GATE_RANK
META_OFF = CH - N_META
HEAD_POS = (0, 4, 1, 5, 2, 6, 3, 7)
LN_ROWS = 512
TOKEN = (8, LANE)
N_CHIPS = 4
SHARD_ROWS = dict(w_in=D_IN // N_CHIPS, w_out=D // N_CHIPS, w_g=D_FF // N_CHIPS, w_u=D_FF // N_CHIPS,
                  w_d=D_FF // N_CHIPS)
SMALL_ROWS = 48
BF16_ROWS = 16
W_IN_WIN = -(-SHARD_ROWS["w_in"] // (2 * BF16_ROWS)) * 2 * BF16_ROWS
W_IN_STARTS = tuple(s * SHARD_ROWS["w_in"] // BF16_ROWS * BF16_ROWS for s in range(N_CHIPS))
VMEM_CAP_MB = 64
VMEM_SPARE_MB = 6


def _lp():
    return SEQ + BLK


def _row_tile(cap):
    lp = _lp()
    return max(t for t in range(16, cap + 1, 16) if lp % t == 0)


def _params(vmem_mb, **kw):
    assert vmem_mb <= VMEM_CAP_MB - VMEM_SPARE_MB
    return pltpu.CompilerParams(vmem_limit_bytes=vmem_mb << 20, **kw)


def _seq(n=1):
    return ("arbitrary",) * n


def _const(shape):
    return pl.BlockSpec(shape, lambda *_: (0,) * len(shape), pipeline_mode=pl.Buffered(1))


def _acc(shape):
    return pl.BlockSpec(shape, lambda *_: (0,) * len(shape))


def _rows(tm, width):
    return pl.BlockSpec((tm, width), lambda i: (i, 0))


def _dot(a, b):
    return jnp.dot(a.astype(BF16), b.astype(BF16), preferred_element_type=F32)


def _dot_nt(a, b):
    return lax.dot_general(a.astype(BF16), b.astype(BF16), (((1,), (1,)), ((), ())), preferred_element_type=F32)


def _dot_tn(a, b):
    return lax.dot_general(a.astype(BF16), b.astype(BF16), (((0,), (0,)), ((), ())), preferred_element_type=F32)


def _dot_exact(a, b):
    return jnp.dot(a, b, precision=lax.Precision.HIGHEST, preferred_element_type=F32)


def _ln_stats(x):
    mu = jnp.mean(x, axis=-1, keepdims=True)
    xc = x - mu
    rstd = lax.rsqrt(jnp.mean(xc * xc, axis=-1, keepdims=True) + LN_EPS)
    return xc * rstd, rstd


def _ln_bwd(dy, xhat, rstd, g):
    dxh = dy * g
    return rstd * (dxh - jnp.mean(dxh, axis=-1, keepdims=True) - xhat * jnp.mean(dxh * xhat, axis=-1, keepdims=True))


def _sigmoid(x):
    return 1.0 / (1.0 + jnp.exp(-x))


def _iota(shape, dim):
    return lax.broadcasted_iota(jnp.int32, shape, dim)


def _hbm(*arrays):
    return tuple(pltpu.with_memory_space_constraint(a, pltpu.HBM) for a in arrays)


RING = 3


def _ring_fetch(steps, tile, sources, rings, sems):
    assert steps >= RING - 1
    step = pl.program_id(0)

    def copy(t, k):
        rows = pl.ds(pl.multiple_of(t * tile, tile), tile)
        return pltpu.make_async_copy(sources[k].at[rows, :], rings[k].at[t % RING], sems.at[k, t % RING])

    @pl.when(step == 0)
    def _():
        for t in range(RING - 1):
            for k in range(len(sources)):
                copy(t, k).start()

    @pl.when(step + (RING - 1) < steps)
    def _():
        for k in range(len(sources)):
            copy(step + (RING - 1), k).start()

    for k in range(len(sources)):
        copy(step, k).wait()
    return step % RING


def _ring_scratch(tile, widths, dtypes=None):
    dtypes = dtypes or [F32] * len(widths)
    return ([pltpu.VMEM((RING, tile, w), dt) for w, dt in zip(widths, dtypes)]
            + [pltpu.SemaphoreType.DMA((len(widths), RING))])


def _ln_in_fwd_real(x, g, b, token):
    tr = min(LN_ROWS, SEQ)
    steps = SEQ // tr

    def body(x_hbm, g_ref, b_ref, token_ref, h_ref, x_ring, sems):
        slot = _ring_fetch(steps, tr, [x_hbm], [x_ring], sems)
        xhat, _ = _ln_stats(x_ring[slot])
        h_ref[...] = xhat * g_ref[...] + b_ref[...]

    return pl.pallas_call(
        body, name="ln_in_fwd", grid=(steps,),
        in_specs=[pl.BlockSpec(memory_space=pl.ANY), _const((1, D)), _const((1, D)), _const(TOKEN)],
        out_specs=_rows(tr, D),
        out_shape=pltpu.HBM((_lp(), D), F32),
        scratch_shapes=_ring_scratch(tr, [D]),
        compiler_params=_params(32, dimension_semantics=_seq()),
    )(*_hbm(x, g, b), token)


def _ln_in_fwd_meta(h_real, meta_ext, g, b):
    def meta_body(m_ref, g_ref, b_ref, real_ref, h_ref):
        xhat, _ = _ln_stats(m_ref[...])
        h_ref[...] = xhat * g_ref[...] + b_ref[...]

    return pl.pallas_call(
        meta_body, name="ln_in_fwd_meta", grid=(1,),
        in_specs=[_const((BLK, D)), _const((1, D)), _const((1, D)), pl.BlockSpec(memory_space=pl.ANY)],
        out_specs=pl.BlockSpec((BLK, D), lambda i: (SEQ // BLK, 0)),
        out_shape=pltpu.HBM((_lp(), D), F32),
        input_output_aliases={3: 0},
        compiler_params=_params(16, dimension_semantics=_seq()),
    )(*_hbm(meta_ext, g, b, h_real))


def _in_proj(h0, w_in_windows, b_in_p, wg2_p, bg2):
    tm = _row_tile(384)
    lp = _lp()
    widths = (512, 128, 128, 256, 256, 512, 512, 128)
    offs = (O_QS, O_KS, O_VS, O_QG, O_KG, O_VG, O_RG, O_LR)
    shard = SHARD_ROWS["w_in"]

    def body(h_ref, win_ref, b_ref, wg2_ref, bg2_ref, *outs):
        w_ref = outs[9]

        @pl.when(pl.program_id(0) == 0)
        def _():
            for s in range(N_CHIPS):
                w_ref[shard * s:shard * (s + 1), :] = win_ref[s, 0:shard, :]
            w_ref[D_IN:D_IN_P, :] = jnp.zeros((D_IN_P - D_IN, D), BF16)

        proj = _dot_nt(h_ref[...], w_ref[...]) + b_ref[...]
        for pos, h in enumerate(HEAD_POS):
            outs[0][:, pos * DH:(pos + 1) * DH] = proj[:, O_QS + h * DH:O_QS + (h + 1) * DH]
        for o_ref, off, wd in zip(outs[1:8], offs[1:], widths[1:]):
            o_ref[...] = proj[:, off:off + wd]
        outs[8][...] = _dot(proj[:, O_LR:O_LR + LANE], wg2_ref[...]) + bg2_ref[...]

    return pl.pallas_call(
        body, name="in_proj", grid=(lp // tm,),
        in_specs=[_rows(tm, D), _const(w_in_windows.shape), _const((1, D_IN_P)), _const((LANE, 256)), _const((1, 256))],
        out_specs=[_rows(tm, w) for w in widths] + [_rows(tm, 256), _acc((D_IN_P, D))],
        out_shape=[pltpu.HBM((lp, w), F32) for w in widths] + [pltpu.HBM((lp, 256), F32), pltpu.HBM((D_IN_P, D), BF16)],
        compiler_params=_params(48, dimension_semantics=_seq()),
    )(*_hbm(h0, w_in_windows, b_in_p, wg2_p, bg2))


def _swa_masks(n):
    nb = SEQ // BLK
    is_meta = n == nb
    ri = _iota((BLK, BLK), 0)
    cj = _iota((BLK, BLK), 1)
    meta_col = ((cj >= META_OFF) & (cj < CH)).astype(jnp.int32)
    meta_q = meta_col * ((cj <= ri) & (ri < CH)).astype(jnp.int32)
    valid_m = jnp.where(is_meta, meta_q, meta_col) > 0
    dist_m = jnp.where(is_meta, ri - cj, n * BLK + ri + CH - cj).astype(F32)
    valid_p = jnp.where((n >= 1) & (n < nb), (cj > ri).astype(jnp.int32), 0) > 0
    dist_p = (ri + BLK - cj).astype(F32)
    valid_c = jnp.where(n < nb, (cj <= ri).astype(jnp.int32), 0) > 0
    dist_c = (ri - cj).astype(F32)
    return (dist_m, dist_p, dist_c), (valid_m, valid_p, valid_c)


def _swa_bias(n):
    dists, valids = _swa_masks(n)
    return (jnp.concatenate([-d for d in dists], axis=1),
            jnp.concatenate([jnp.where(v, 0.0, NEG) for v in valids], axis=1))


def _swa_half(ref, pos, scale=1.0):
    col = ref[:, (pos // 2) * LANE:(pos // 2 + 1) * LANE]
    lane = _iota((BLK, LANE), 1)
    mine = lane < DH if pos % 2 == 0 else lane >= DH
    return jnp.where(mine, col * scale, 0.0).astype(BF16)


def _swa_merge(even, odd):
    return jnp.where(_iota((BLK, LANE), 1) < DH, even, odd)


def _swa_softmax(t, sink):
    m = jnp.maximum(jnp.max(t, axis=-1, keepdims=True), sink)
    e = jnp.exp(t - m)
    e_sink = jnp.exp(sink - m)
    inv = 1.0 / (jnp.sum(e, axis=-1, keepdims=True) + e_sink)
    return e * inv, e_sink * inv


def _swa_kv_specs(width):
    nb = SEQ // BLK
    return [pl.BlockSpec((BLK, width), lambda n: (nb, 0)),
            pl.BlockSpec((BLK, width), lambda n: (jnp.clip(n - 1, 0, nb - 1), 0)),
            pl.BlockSpec((BLK, width), lambda n: (jnp.minimum(n, nb), 0))]


def _swa_fwd(sinks, qs, ks, vs):
    nb = SEQ // BLK
    heads = range(SWA_HEADS)

    def body(sink_ref, q_ref, km_ref, kp_ref, kc_ref, vm_ref, vp_ref, vc_ref, o_ref):
        negdist, maskbias = _swa_bias(pl.program_id(0))
        k_all = jnp.concatenate([km_ref[...], kp_ref[...], kc_ref[...]], axis=0).astype(BF16)
        v_all = jnp.concatenate([vm_ref[...], vp_ref[...], vc_ref[...]], axis=0).astype(BF16)
        q = [_swa_half(q_ref, pos, DH ** -0.5) for pos in heads]
        t = [_dot_nt(q[pos], k_all) + (2.0 ** -(HEAD_POS[pos] + 1) * negdist + maskbias) for pos in heads]
        p = [_swa_softmax(t[pos], sink_ref[HEAD_POS[pos]])[0].astype(BF16) for pos in heads]
        o = [_dot(p[pos], v_all) for pos in heads]
        for col in range(SWA_HEADS // 2):
            o_ref[:, col * LANE:(col + 1) * LANE] = _swa_merge(o[2 * col], o[2 * col + 1])

    kvw = SWA_KV_HEADS * DH
    return pl.pallas_call(
        body, name="swa_fwd", grid=(nb + 1,),
        in_specs=[pl.BlockSpec(memory_space=pltpu.SMEM), _rows(BLK, SWA_HEADS * DH)] + _swa_kv_specs(kvw) + _swa_kv_specs(kvw),
        out_specs=_rows(BLK, SWA_HEADS * DH),
        out_shape=pltpu.HBM((_lp(), SWA_HEADS * DH), F32),
        compiler_params=_params(16, dimension_semantics=_seq()),
    )(sinks, *_hbm(qs, ks, ks, ks, vs, vs, vs))


GLA_PER_STEP = BLK // CH


def _gla_block(s):
    nb = SEQ // BLK
    return jnp.where(s == 0, nb, s - 1)


def _gla_rowmask(s):
    ri = _iota((BLK, 1), 0)
    m = jnp.where(s == 0, ((ri >= META_OFF) & (ri < CH)).astype(jnp.int32), 1)
    return (m > 0).astype(F32) + jnp.zeros((BLK, 1), F32)


def _gla_chunk_masks():
    r, c = _iota((BLK, BLK), 0), _iota((BLK, BLK), 1)
    same = ((r < CH) & (c < CH)) | ((r >= CH) & (c >= CH))
    return same & (r >= c), same & (r <= c), same


def _gla_decay(z, rmask):
    log_g = (jnp.minimum(z, 0.0) - jnp.log1p(jnp.exp(-jnp.abs(z)))) * (rmask / GLA_TAU)
    lower, _, same = _gla_chunk_masks()
    return _dot_exact(lower.astype(F32), log_g), _dot_exact(same.astype(F32), log_g)


def _gla_slices(c, h):
    return slice(c * CH, (c + 1) * CH), slice(h * DK, (h + 1) * DK), slice(h * DV, (h + 1) * DV)


def _gla_fwd(qg, kg, vg, z):
    steps = SEQ // BLK + 1
    kw, vw = GLA_HEADS * DK, GLA_HEADS * DV
    pairs = [(c, h) for c in range(GLA_PER_STEP) for h in range(GLA_HEADS)]

    def body(q_ref, k_ref, v_ref, z_ref, o_ref, st_ref, st):
        s = pl.program_id(0)

        @pl.when(s == 0)
        def _():
            st[...] = jnp.zeros_like(st)

        rmask = _gla_rowmask(s)
        b, b_last = _gla_decay(z_ref[...], rmask)
        q = q_ref[...] * (rmask * DK ** -0.5)
        k = k_ref[...] * rmask
        v = v_ref[...] * rmask
        qe = q * jnp.exp(b)
        ke = k * jnp.exp(-b)
        kd = k * jnp.exp(b_last - b)
        e_last = jnp.exp(b_last)
        causal = _iota((CH, CH), 0) >= _iota((CH, CH), 1)
        a, upd, intra = {}, {}, {}
        for c, h in pairs:
            rows, ks, vs_ = _gla_slices(c, h)
            a[c, h] = jnp.where(causal, _dot_nt(qe[rows, ks], ke[rows, ks]), 0.0)
            upd[c, h] = _dot_tn(v[rows, vs_], kd[rows, ks])
        for c, h in pairs:
            rows, ks, vs_ = _gla_slices(c, h)
            intra[c, h] = _dot(a[c, h], v[rows, vs_])
        state = st[...]
        for c in range(GLA_PER_STEP):
            st_ref[0, c] = state
            for h in range(GLA_HEADS):
                rows, ks, vs_ = _gla_slices(c, h)
                o_ref[rows, vs_] = intra[c, h] + _dot_nt(qe[rows, ks], state[:, ks])
            state = state * e_last[c * CH:c * CH + 1] + jnp.concatenate([upd[c, h] for h in range(GLA_HEADS)], axis=1)
        st[...] = state

    blk = lambda w: pl.BlockSpec((BLK, w), lambda s: (_gla_block(s), 0))
    return pl.pallas_call(
        body, name="gla_fwd", grid=(steps,),
        in_specs=[blk(kw), blk(kw), blk(vw), blk(kw)],
        out_specs=[blk(vw), pl.BlockSpec((1, GLA_PER_STEP, DV, kw), lambda s: (s, 0, 0, 0))],
        out_shape=[pltpu.HBM((_lp(), vw), F32), pltpu.HBM((steps, GLA_PER_STEP, DV, kw), F32)],
        scratch_shapes=[pltpu.VMEM((DV, kw), F32)],
        compiler_params=_params(16, dimension_semantics=_seq()),
    )(*_hbm(qg, kg, vg, z))


def _post_mix(o_s, o_gla, r_g, h0, gn4, w_out, g1, b1, token):
    tm = _row_tile(384)
    lp = _lp()

    def body(os_hbm, og_hbm, r_hbm, h0_hbm, gn_ref, w_ref, g_ref, b_ref, token_ref, o_ref, pre_ref, h1_ref, *scratch):
        slot = _ring_fetch(lp // tm, tm, [os_hbm, og_hbm, r_hbm, h0_hbm], scratch[:4], scratch[4])
        os_ref, og_ref, r_ref, h0_ref = (ring.at[slot] for ring in scratch[:4])
        for pos, h in enumerate(HEAD_POS):
            o_ref[:, h * DH:(h + 1) * DH] = os_ref[:, pos * DH:(pos + 1) * DH].astype(BF16)
        for h in range(GLA_HEADS):
            hs = slice(h * DV, (h + 1) * DV)
            xg = og_ref[:, hs]
            n = xg * lax.rsqrt(jnp.mean(xg * xg, axis=-1, keepdims=True) + RMS_EPS) * gn_ref[...]
            r = r_ref[:, hs]
            o_ref[:, 512 + h * DV:512 + (h + 1) * DV] = (n * (r * _sigmoid(r))).astype(BF16)
        pre = ALPHA * h0_ref[...] + _dot(o_ref[...], w_ref[...])
        pre_ref[...] = pre
        xhat, _ = _ln_stats(pre)
        h1_ref[...] = xhat * g_ref[...] + b_ref[...]

    return pl.pallas_call(
        body, name="post_mix", grid=(lp // tm,),
        in_specs=[pl.BlockSpec(memory_space=pl.ANY)] * 4 + [_const((1, DV)), _const((D, D)),
                                                              _const((1, D)), _const((1, D)), _const(TOKEN)],
        out_specs=[_rows(tm, D), _rows(tm, D), _rows(tm, D)],
        out_shape=[pltpu.HBM((lp, D), BF16), pltpu.HBM((lp, D), F32),
                   pltpu.HBM((lp, D), F32)],
        scratch_shapes=_ring_scratch(tm, [512, 512, 512, D]),
        compiler_params=_params(40, dimension_semantics=_seq()),
    )(*_hbm(o_s, o_gla, r_g, h0, gn4, w_out, g1, b1), token)


def _ffn_fwd_loss_bwd(h1, wg_t, wu_t, wd, target, g2, b2):
    lp = _lp()
    tm = max(t for t in range(BLK, 384 + 1, BLK) if lp % t == 0)
    steps = lp // tm
    last_blk = SEQ // BLK - 1
    half = D_FF // 2
    n_t = tm // BLK

    def body(*refs):
        h_ref, wg_ref, wu_ref, wd_ref = refs[:4]
        t_refs = refs[4:4 + n_t]
        g2_ref, b2_ref, a_ref, dgate_ref, dup_ref, dp_ref, loss_ref, dg_ref, db_ref, g_s, u_s, acc = refs[4 + n_t:]
        i = pl.program_id(0)

        @pl.when(i == 0)
        def _():
            acc[...] = jnp.zeros_like(acc)
            dg_ref[...] = jnp.zeros_like(dg_ref)
            db_ref[...] = jnp.zeros_like(db_ref)

        h = h_ref[...]
        hb = h.astype(BF16)
        pre = ALPHA * h
        for j in range(2):
            cols = slice(j * half, (j + 1) * half)
            g = _dot_nt(hb, wg_ref[cols, :])
            u = _dot_nt(hb, wu_ref[cols, :])
            g_s[:, cols] = g
            u_s[:, cols] = u
            pre = pre + _dot(g * _sigmoid(g) * u, wd_ref[cols, :])
        xhat, rstd = _ln_stats(pre)
        real = i * tm + _iota((tm, 1), 0) < SEQ
        target_rows = jnp.concatenate([t[...] for t in t_refs], axis=0)
        diff = jnp.where(real, xhat * g2_ref[...] + b2_ref[...] - target_rows, 0.0)
        acc[...] += jnp.sum(diff * diff, axis=0, keepdims=True)
        dy = diff * (1.0 / D)
        dpre = _ln_bwd(dy, xhat, rstd, g2_ref[...])
        dp_ref[...] = dpre
        dg_ref[...] += jnp.sum(dy * xhat, axis=0, keepdims=True)
        db_ref[...] += jnp.sum(dy, axis=0, keepdims=True)
        dpb = dpre.astype(BF16)
        for j in range(2):
            cols = slice(j * half, (j + 1) * half)
            g, u = g_s[:, cols], u_s[:, cols]
            sg = _sigmoid(g)
            silu = g * sg
            da = _dot_nt(dpb, wd_ref[cols, :])
            a_ref[:, cols] = (silu * u).astype(BF16)
            dgate_ref[:, cols] = (da * u * (sg * (1.0 + g * (1.0 - sg)))).astype(BF16)
            dup_ref[:, cols] = (da * silu).astype(BF16)

        @pl.when(i == steps - 1)
        def _():
            loss_ref[...] = jnp.zeros_like(loss_ref) + (0.5 / D) * jnp.sum(acc[...], axis=1, keepdims=True)

    t_spec = lambda k: pl.BlockSpec((BLK, D), lambda i: (jnp.minimum(i * n_t + k, last_blk), 0))
    return pl.pallas_call(
        body, name="ffn_fwd_loss_bwd", grid=(steps,),
        in_specs=[_rows(tm, D), _const((D_FF, D)), _const((D_FF, D)), _const((D_FF, D))]
        + [t_spec(k) for k in range(n_t)] + [_const((1, D)), _const((1, D))],
        out_specs=[_rows(tm, D_FF), _rows(tm, D_FF), _rows(tm, D_FF), _rows(tm, D), _acc((1, LANE)), _acc((1, D)),
                   _acc((1, D))],
        out_shape=[pltpu.HBM((lp, D_FF), BF16)] * 3 + [pltpu.HBM((lp, D), F32), pltpu.HBM((1, LANE), F32),
                                                         pltpu.HBM((1, D), F32), pltpu.HBM((1, D), F32)],
        scratch_shapes=[pltpu.VMEM((tm, D_FF), F32), pltpu.VMEM((tm, D_FF), F32), pltpu.VMEM((1, D), F32)],
        compiler_params=_params(58, dimension_semantics=_seq()),
    )(*_hbm(h1, wg_t, wu_t, wd, *[target] * n_t, g2, b2))


def _ffn_out_bwd(dpre2, dgate, dup, pre1, wg_t, wu_t, g1, w_out, o_gla, r_g, gn4):
    tm = _row_tile(384)
    lp = _lp()
    tiled = pl.BlockSpec(memory_space=pl.ANY)

    def body(dp_hbm, dg_hbm, du_hbm, p1_hbm, wg_ref, wu_ref, g1_ref, w_ref, og_hbm, r_hbm, gn_ref,
             dp1_ref, dg1_ref, db1_ref, dos_ref, dog_ref, dr_ref, dgn_ref, *scratch):
        slot = _ring_fetch(lp // tm, tm, [dp_hbm, dg_hbm, du_hbm, p1_hbm, og_hbm, r_hbm], scratch[:6], scratch[6])
        dp_ref, dg_ref, du_ref, p1_ref, og_ref, r_ref = (ring.at[slot] for ring in scratch[:6])

        @pl.when(pl.program_id(0) == 0)
        def _():
            for acc_ref in (dg1_ref, db1_ref, dgn_ref):
                acc_ref[...] = jnp.zeros_like(acc_ref)

        dh1 = ALPHA * dp_ref[...] + _dot(dg_ref[...], wg_ref[...]) + _dot(du_ref[...], wu_ref[...])
        xhat, rstd1 = _ln_stats(p1_ref[...])
        dpre1 = _ln_bwd(dh1, xhat, rstd1, g1_ref[...])
        dp1_ref[...] = dpre1
        dg1_ref[...] += jnp.sum(dh1 * xhat, axis=0, keepdims=True)
        db1_ref[...] += jnp.sum(dh1, axis=0, keepdims=True)

        do = _dot_nt(dpre1, w_ref[...])
        for pos, h in enumerate(HEAD_POS):
            dos_ref[:, pos * DH:(pos + 1) * DH] = do[:, h * DH:(h + 1) * DH]
        gn = gn_ref[...]
        for h in range(GLA_HEADS):
            hs = slice(h * DV, (h + 1) * DV)
            xg = og_ref[:, hs]
            rstd = lax.rsqrt(jnp.mean(xg * xg, axis=-1, keepdims=True) + RMS_EPS)
            nx = xg * rstd
            r = r_ref[:, hs]
            sr = _sigmoid(r)
            d_o = do[:, 512 + h * DV:512 + (h + 1) * DV]
            dr_ref[:, hs] = d_o * (nx * gn) * (sr * (1.0 + r * (1.0 - sr)))
            dn = d_o * (r * sr)
            dgn_ref[...] += jnp.sum(dn * nx, axis=0, keepdims=True)
            dnx = dn * gn
            dog_ref[:, hs] = rstd * (dnx - nx * jnp.mean(dnx * nx, axis=-1, keepdims=True))

    return pl.pallas_call(
        body, name="ffn_out_bwd", grid=(lp // tm,),
        in_specs=[tiled] * 4 + [_const((D_FF, D)), _const((D_FF, D)), _const((1, D)), _const((D, D)), tiled, tiled,
                                _const((1, DV))],
        out_specs=[_rows(tm, D), _acc((1, D)), _acc((1, D)), _rows(tm, 512), _rows(tm, 512), _rows(tm, 512),
                   _acc((1, DV))],
        out_shape=[pltpu.HBM((lp, D), F32), pltpu.HBM((1, D), F32), pltpu.HBM((1, D), F32)]
        + [pltpu.HBM((lp, 512), F32)] * 3 + [pltpu.HBM((1, DV), F32)],
        scratch_shapes=_ring_scratch(tm, [D, D_FF, D_FF, D, 512, 512], [F32, BF16, BF16, F32, F32, F32]),
        compiler_params=_params(56, dimension_semantics=_seq()),
    )(*_hbm(dpre2, dgate, dup, pre1, wg_t, wu_t, g1, w_out, o_gla, r_g, gn4))


def _atb(a, b, name, token=None, windows=None):
    lp = _lp()
    tm = _row_tile(1408)
    n, w = a.shape[1], b.shape[1]
    bw = 512 if n * w * 4 > (4 << 20) else w
    tokens = [] if token is None else [token]
    steps = lp // tm

    def body(a_ref, b_ref, *rest):
        o_ref, acc_ref = rest[len(tokens):] if windows else (rest[-1], rest[-1])

        @pl.when(pl.program_id(1) == 0)
        def _():
            acc_ref[...] = jnp.zeros_like(acc_ref)

        acc_ref[...] += _dot_tn(a_ref[...], b_ref[...])

        if windows:
            @pl.when(pl.program_id(1) == steps - 1)
            def _():
                for s, start in enumerate(windows[0]):
                    o_ref[s] = acc_ref[start:start + windows[1], :]

    if windows:
        count, height = len(windows[0]), windows[1]
        out_spec, out_shape = pl.BlockSpec((count, height, bw), lambda j, k: (0, 0, j)), (count, height, w)
    else:
        out_spec, out_shape = pl.BlockSpec((n, bw), lambda j, k: (0, j)), (n, w)
    return pl.pallas_call(
        body, name=name, grid=(w // bw, steps),
        in_specs=[pl.BlockSpec((tm, n), lambda j, k: (k, 0)), pl.BlockSpec((tm, bw), lambda j, k: (k, j))]
        + [_const(TOKEN)] * len(tokens),
        out_specs=out_spec, out_shape=pltpu.HBM(out_shape, F32),
        scratch_shapes=[pltpu.VMEM((n, bw), F32)] if windows else [],
        compiler_params=_params(48, dimension_semantics=_seq(2)),
    )(*_hbm(a, b), *tokens)


def _gla_bwd(qg, kg, vg, z, do_gla, st_all, token):
    steps = SEQ // BLK + 1
    kw, vw = GLA_HEADS * DK, GLA_HEADS * DV
    pairs = [(c, h) for c in range(GLA_PER_STEP) for h in range(GLA_HEADS)]
    heads = range(GLA_HEADS)

    def body(q_ref, k_ref, v_ref, z_ref, do_ref, st_ref, token_ref, dq_ref, dk_ref, dv_ref, dz_ref, dst):
        @pl.when(pl.program_id(0) == 0)
        def _():
            dst[...] = jnp.zeros_like(dst)

        rmask = _gla_rowmask(steps - 1 - pl.program_id(0))
        zz = z_ref[...]
        b, b_last = _gla_decay(zz, rmask)
        e_b, e_nb, e_kd, e_last = jnp.exp(b), jnp.exp(-b), jnp.exp(b_last - b), jnp.exp(b_last)
        q = q_ref[...] * (rmask * DK ** -0.5)
        k = k_ref[...] * rmask
        v = v_ref[...] * rmask
        qe, ke, kd = q * e_b, k * e_nb, k * e_kd
        d_o = do_ref[...]
        causal = _iota((CH, CH), 0) >= _iota((CH, CH), 1)
        a, da, dqe, dke, dv_intra, carry = {}, {}, {}, {}, {}, {}
        for c, h in pairs:
            rows, ks, vs_ = _gla_slices(c, h)
            a[c, h] = jnp.where(causal, _dot_nt(qe[rows, ks], ke[rows, ks]), 0.0)
            da[c, h] = jnp.where(causal, _dot_nt(d_o[rows, vs_], v[rows, vs_]), 0.0)
            carry[c, h] = _dot_tn(d_o[rows, vs_], qe[rows, ks])
        for c, h in pairs:
            rows, ks, vs_ = _gla_slices(c, h)
            dqe[c, h] = _dot(d_o[rows, vs_], st_ref[0, c][:, ks]) + _dot(da[c, h], ke[rows, ks])
            dke[c, h] = _dot_tn(da[c, h], qe[rows, ks])
            dv_intra[c, h] = _dot_tn(a[c, h], d_o[rows, vs_])
        dstate = dst[...]
        dkd, db_decay = {}, {}
        for c in reversed(range(GLA_PER_STEP)):
            for h in heads:
                rows, ks, vs_ = _gla_slices(c, h)
                dkd[c, h] = _dot(v[rows, vs_], dstate[:, ks])
                dv_ref[rows, vs_] = dv_intra[c, h] + _dot_nt(kd[rows, ks], dstate[:, ks])
            chunk_last = e_last[c * CH:c * CH + 1]
            db_decay[c] = jnp.sum(dstate * st_ref[0, c], axis=0, keepdims=True) * chunk_last
            dstate = dstate * chunk_last + jnp.concatenate([carry[c, h] for h in heads], axis=1)
        dst[...] = dstate
        rows_of = lambda parts: jnp.concatenate(
            [jnp.concatenate([parts[c, h] for h in heads], axis=1) for c in range(GLA_PER_STEP)], axis=0)
        dqe_all, dke_all, dkd_all = rows_of(dqe), rows_of(dke), rows_of(dkd)
        dq_ref[...] = dqe_all * e_b * (rmask * DK ** -0.5)
        dk_ref[...] = (dke_all * e_nb + dkd_all * e_kd) * rmask
        dkd_kd = dkd_all * kd
        db = dqe_all * qe - dke_all * ke - dkd_kd
        _, upper, same = _gla_chunk_masks()
        decay_rows = jnp.concatenate([jnp.broadcast_to(db_decay[c], (CH, kw)) for c in range(GLA_PER_STEP)], axis=0)
        dlog_g = _dot_exact(upper.astype(F32), db) + _dot_exact(same.astype(F32), dkd_kd) + decay_rows
        dz_ref[...] = dlog_g * (rmask / GLA_TAU) * _sigmoid(-zz)

    blk = lambda w: pl.BlockSpec((BLK, w), lambda s: (_gla_block(steps - 1 - s), 0))
    return pl.pallas_call(
        body, name="gla_bwd", grid=(steps,),
        in_specs=[blk(kw), blk(kw), blk(vw), blk(kw), blk(vw),
                  pl.BlockSpec((1, GLA_PER_STEP, DV, kw), lambda s: (steps - 1 - s, 0, 0, 0)), _const(TOKEN)],
        out_specs=[blk(kw), blk(kw), blk(vw), blk(kw)],
        out_shape=[pltpu.HBM((_lp(), kw), F32), pltpu.HBM((_lp(), kw), F32),
                   pltpu.HBM((_lp(), vw), F32), pltpu.HBM((_lp(), kw), F32)],
        scratch_shapes=[pltpu.VMEM((DV, kw), F32)],
        compiler_params=_params(16, dimension_semantics=_seq()),
    )(*_hbm(qg, kg, vg, z, do_gla, st_all), token)


def _swa_bwd(sinks, qs, ks, vs, do_s, token):
    nb = SEQ // BLK
    kvw = SWA_KV_HEADS * DH
    scale = DH ** -0.5
    heads = range(SWA_HEADS)

    def body(sink_ref, q_ref, km_ref, kp_ref, kc_ref, vm_ref, vp_ref, vc_ref, do_ref, token_ref,
             dq_ref, dk_ref, dv_ref, dsink_ref, carry_k, carry_v, meta_k, meta_v):
        n = pl.program_id(0)

        @pl.when(n == 0)
        def _():
            for r in (carry_k, carry_v, meta_k, meta_v):
                r[...] = jnp.zeros_like(r)
            dsink_ref[...] = jnp.zeros_like(dsink_ref)

        @pl.when(n <= nb)
        def _():
            negdist, maskbias = _swa_bias(n)
            lane = _iota((1, LANE), 1)
            k_all = jnp.concatenate([km_ref[...], kp_ref[...], kc_ref[...]], axis=0).astype(BF16)
            v_all = jnp.concatenate([vm_ref[...], vp_ref[...], vc_ref[...]], axis=0).astype(BF16)
            q = [_swa_half(q_ref, pos, scale) for pos in heads]
            d_o = [_swa_half(do_ref, pos) for pos in heads]
            t = [_dot_nt(q[pos], k_all) + (2.0 ** -(HEAD_POS[pos] + 1) * negdist + maskbias) for pos in heads]
            dp = [_dot_nt(d_o[pos], v_all) for pos in heads]
            soft = [_swa_softmax(t[pos], sink_ref[HEAD_POS[pos]]) for pos in heads]
            p = [s[0] for s in soft]
            delta = [jnp.sum(p[pos] * dp[pos], axis=-1, keepdims=True) for pos in heads]
            ds = [(p[pos] * (dp[pos] - delta[pos])).astype(BF16) for pos in heads]
            dq = [_dot(ds[pos], k_all) for pos in heads]
            for col in range(SWA_HEADS // 2):
                dq_ref[:, col * LANE:(col + 1) * LANE] = scale * _swa_merge(dq[2 * col], dq[2 * col + 1])
            dsink = jnp.zeros((1, LANE), F32)
            for pos in heads:
                dsink = dsink + jnp.where(lane == HEAD_POS[pos],
                                          -jnp.sum(soft[pos][1] * delta[pos], axis=0, keepdims=True), 0.0)
            dsink_ref[...] += dsink
            dk3 = _dot_tn(jnp.concatenate(q, axis=0), jnp.concatenate(ds, axis=0)).T
            dv3 = _dot_tn(jnp.concatenate(d_o, axis=0), jnp.concatenate([x.astype(BF16) for x in p], axis=0)).T
            meta_k[...] += dk3[0:BLK]
            meta_v[...] += dv3[0:BLK]
            dk_ref[...] = carry_k[...] + dk3[BLK:2 * BLK]
            dv_ref[...] = carry_v[...] + dv3[BLK:2 * BLK]
            carry_k[...] = dk3[2 * BLK:3 * BLK]
            carry_v[...] = dv3[2 * BLK:3 * BLK]

        @pl.when(n == nb + 1)
        def _():
            dk_ref[...] = meta_k[...]
            dv_ref[...] = meta_v[...]

    kv_out = pl.BlockSpec((BLK, kvw), lambda n: (jnp.where(n == nb + 1, nb, jnp.clip(n - 1, 0, nb - 1)), 0))
    qblk = pl.BlockSpec((BLK, SWA_HEADS * DH), lambda n: (jnp.minimum(n, nb), 0))
    return pl.pallas_call(
        body, name="swa_bwd", grid=(nb + 2,),
        in_specs=[pl.BlockSpec(memory_space=pltpu.SMEM), qblk] + _swa_kv_specs(kvw) + _swa_kv_specs(kvw)
        + [qblk, _const(TOKEN)],
        out_specs=[qblk, kv_out, kv_out, _acc((1, LANE))],
        out_shape=[pltpu.HBM((_lp(), SWA_HEADS * DH), F32), pltpu.HBM((_lp(), kvw), F32),
                   pltpu.HBM((_lp(), kvw), F32), pltpu.HBM((1, LANE), F32)],
        scratch_shapes=[pltpu.VMEM((BLK, kvw), F32)] * 4,
        compiler_params=_params(16, dimension_semantics=_seq()),
    )(sinks, *_hbm(qs, ks, ks, ks, vs, vs, vs, do_s), token)


def _in_bwd(dqs, dks, dvs, dqg, dkg, dvg, drg, dz, dpre1, w_in_t, wg2_p):
    tm = _row_tile(384)
    lp = _lp()
    widths = (512, 128, 128, 256, 256, 512, 512)
    offs = (O_QS, O_KS, O_VS, O_QG, O_KG, O_VG, O_RG)

    tiled = widths + (256, D)

    def body(*refs):
        w_ref, wg2_ref, dproj_ref, dh0_ref, dbin_ref, dbg_ref = refs[9:15]
        rings, sems = refs[15:24], refs[24]
        slot = _ring_fetch(lp // tm, tm, refs[:9], rings, sems)
        parts, dz_ref, dp1_ref = [r.at[slot] for r in rings[:7]], rings[7].at[slot], rings[8].at[slot]

        @pl.when(pl.program_id(0) == 0)
        def _():
            dbin_ref[...] = jnp.zeros_like(dbin_ref)
            dbg_ref[...] = jnp.zeros_like(dbg_ref)

        for pos, h in enumerate(HEAD_POS):
            val = parts[0][:, pos * DH:(pos + 1) * DH]
            dproj_ref[:, O_QS + h * DH:O_QS + (h + 1) * DH] = val.astype(BF16)
            dbin_ref[:, O_QS + h * DH:O_QS + (h + 1) * DH] += jnp.sum(val, axis=0, keepdims=True)
        for p_ref, off, wd in zip(parts[1:], offs[1:], widths[1:]):
            val = p_ref[...]
            dproj_ref[:, off:off + wd] = val.astype(BF16)
            dbin_ref[:, off:off + wd] += jnp.sum(val, axis=0, keepdims=True)
        dz = dz_ref[...]
        dlr = _dot_nt(dz, wg2_ref[...])
        dproj_ref[:, O_LR:O_LR + LANE] = dlr.astype(BF16)
        dbin_ref[:, O_LR:O_LR + LANE] += jnp.sum(dlr, axis=0, keepdims=True)
        dbg_ref[...] += jnp.sum(dz, axis=0, keepdims=True)
        dh0_ref[...] = ALPHA * dp1_ref[...] + _dot(dproj_ref[...], w_ref[...])

    return pl.pallas_call(
        body, name="in_bwd", grid=(lp // tm,),
        in_specs=[pl.BlockSpec(memory_space=pl.ANY)] * len(tiled) + [_const((D_IN_P, D)), _const((LANE, 256))],
        out_specs=[_rows(tm, D_IN_P), _rows(tm, D), _acc((1, D_IN_P)), _acc((1, 256))],
        out_shape=[pltpu.HBM((lp, D_IN_P), BF16), pltpu.HBM((lp, D), F32),
                   pltpu.HBM((1, D_IN_P), F32), pltpu.HBM((1, 256), F32)],
        scratch_shapes=_ring_scratch(tm, tiled),
        compiler_params=_params(48, dimension_semantics=_seq()),
    )(*_hbm(dqs, dks, dvs, dqg, dkg, dvg, drg, dz, dpre1, w_in_t, wg2_p))


def _ln_in_bwd(x, meta_ext, dh0, g, token):
    tr = min(LN_ROWS, SEQ)
    steps = SEQ // tr

    def ln_bwd(x_ref, dh_ref, g_ref, dx_ref, dg_ref, db_ref, so_far=None):
        @pl.when(pl.program_id(0) == 0)
        def _():
            dg_ref[...] = jnp.zeros_like(dg_ref) if so_far is None else so_far[0][...]
            db_ref[...] = jnp.zeros_like(db_ref) if so_far is None else so_far[1][...]

        xhat, rstd = _ln_stats(x_ref[...])
        dh = dh_ref[...]
        dx_ref[...] = _ln_bwd(dh, xhat, rstd, g_ref[...])
        dg_ref[...] += jnp.sum(dh * xhat, axis=0, keepdims=True)
        db_ref[...] += jnp.sum(dh, axis=0, keepdims=True)

    def body(x_hbm, dh_hbm, g_ref, token_ref, dx_ref, dg_ref, db_ref, x_ring, dh_ring, sems):
        slot = _ring_fetch(steps, tr, [x_hbm, dh_hbm], [x_ring, dh_ring], sems)
        ln_bwd(x_ring.at[slot], dh_ring.at[slot], g_ref, dx_ref, dg_ref, db_ref)

    def meta_body(m_ref, dh_ref, g_ref, dg_real_ref, db_real_ref, dm_ref, dg_ref, db_ref):
        ln_bwd(m_ref, dh_ref, g_ref, dm_ref, dg_ref, db_ref, (dg_real_ref, db_real_ref))

    sums = [pltpu.HBM((1, D), F32), pltpu.HBM((1, D), F32)]
    dx, dg, db = pl.pallas_call(
        body, name="ln_in_bwd", grid=(steps,),
        in_specs=[pl.BlockSpec(memory_space=pl.ANY)] * 2 + [_const((1, D)), _const(TOKEN)],
        out_specs=[_rows(tr, D), _acc((1, D)), _acc((1, D))],
        out_shape=[pltpu.HBM((SEQ, D), F32)] + sums,
        scratch_shapes=_ring_scratch(tr, [D, D]),
        compiler_params=_params(32, dimension_semantics=_seq()),
    )(*_hbm(x, dh0, g), token)
    dm, dg, db = pl.pallas_call(
        meta_body, name="ln_in_bwd_meta", grid=(1,),
        in_specs=[_const((BLK, D)), pl.BlockSpec((BLK, D), lambda i: (SEQ // BLK, 0))] + [_const((1, D))] * 3,
        out_specs=[_acc((BLK, D)), _acc((1, D)), _acc((1, D))],
        out_shape=[pltpu.HBM((BLK, D), F32)] + sums,
        compiler_params=_params(16, dimension_semantics=_seq()),
    )(*_hbm(meta_ext, dh0, g, dg, db))
    return dx, dm, dg, db


def _local_step(x, target, ln_in_g, ln_in_b, b_in, bg2, sinks, gn, g1, b1, g2, b2,
                token, fetch_first, fetch_rest, fetch_ffn, exchange_ffn, ship_ffn, exchange_w_in, ship_w_in):
    row = lambda v: v.reshape(1, -1).astype(F32)
    b_in_p = jnp.pad(row(b_in), ((0, 0), (0, D_IN_P - D_IN)))
    gn4 = row(gn)
    sinks = sinks.reshape(-1).astype(F32)

    h_real = _ln_in_fwd_real(x, row(ln_in_g), row(ln_in_b), token)
    w_in_windows, meta_full, wg2 = fetch_first([h_real])
    meta_ext = jnp.pad(meta_full, ((META_OFF, BLK - CH), (0, 0)))
    wg2_p = jnp.pad(wg2, ((0, LANE - wg2.shape[0]), (0, 0))).astype(BF16)
    h0 = _ln_in_fwd_meta(h_real, meta_ext, row(ln_in_g), row(ln_in_b))
    qs, ks, vs, qg, kg, vg, rg, glr, z, w_in_t = _in_proj(h0, w_in_windows, b_in_p, wg2_p, row(bg2))
    o_s = _swa_fwd(sinks, qs, ks, vs)
    o_gla, st_all = _gla_fwd(qg, kg, vg, z)
    w_out, token = fetch_rest([o_s, o_gla])
    o, pre1, h1 = _post_mix(o_s, o_gla, rg, h0, gn4, w_out, row(g1), row(b1), token)
    wg_t, wu_t, wd = fetch_ffn([pre1])
    a, dgate, dup, dpre2, loss, dg2, db2 = _ffn_fwd_loss_bwd(h1, wg_t, wu_t, wd, target, row(g2), row(b2))
    dpre1, dg1, db1, do_s, do_gla, drg, dgn = _ffn_out_bwd(dpre2, dgate, dup, pre1, wg_t, wu_t, row(g1), w_out, o_gla,
                                                           rg, gn4)
    dwd = _atb(a, dpre2, "dw_down")
    dwg_t = _atb(dgate, h1, "dw_gate")
    dwu_t = _atb(dup, h1, "dw_up")
    token = exchange_ffn(dict(w_out=_atb(o, dpre1, "dw_out"), w_g=dwg_t, w_u=dwu_t, w_d=dwd))
    dqg, dkg, dvg, dz = _gla_bwd(qg, kg, vg, z, do_gla, st_all, token)
    token = ship_ffn([dqg])
    dqs, dks, dvs, dsinks = _swa_bwd(sinks, qs, ks, vs, do_s, token)
    dproj, dh0, db_in_p, dbg2 = _in_bwd(dqs, dks, dvs, dqg, dkg, dvg, drg, dz, dpre1, w_in_t, wg2_p)
    token = exchange_w_in(_atb(dproj, h0, "dw_in", windows=(W_IN_STARTS, W_IN_WIN)))
    dwg2_p = _atb(glr, dz, "dw_gate_lr2", token)
    token = ship_w_in([dwg2_p])
    dx, dmeta_blk, dg_in, db_in_ln = _ln_in_bwd(x, meta_ext, dh0, row(ln_in_g), token)

    small = dict(meta_blk=dmeta_blk, ln_in_g=dg_in, ln_in_b=db_in_ln, ln1_g=dg1, ln1_b=db1, ln2_g=dg2, ln2_b=db2,
                 b_in_p=db_in_p, wg2_p=dwg2_p, bg2=dbg2, sinks=dsinks, gn=dgn, loss=loss)
    return dx, small


HBM = pl.BlockSpec(memory_space=pltpu.HBM)


def _place():
    return lax.axis_index("x"), lax.axis_index("y"), lax.axis_index("c")


def _other_chips(x, y):
    return [(1 - x, y), (x, 1 - y), (1 - x, 1 - y)]


def _dma_sems(n):
    return pltpu.SemaphoreType.DMA((n,))


def _comm_params():
    return pltpu.CompilerParams(has_side_effects=True)


SEM = pl.BlockSpec(memory_space=pltpu.SEMAPHORE)


PER_ARRAY = dict(gather=3, scatter=3, sibling=N_CHIPS)


def _ici_copies(kind, landing, srcs, lands, send_sems, recv_sems):
    x, y, c = _place()
    mine = 2 * x + y
    copies = []
    for a in range(len(srcs)):
        if kind == "sibling":
            for s in range(N_CHIPS):
                copies.append(pltpu.make_async_remote_copy(
                    srcs[a].at[s, 1 - c], lands[a].at[s], send_sems.at[N_CHIPS * a + s], recv_sems.at[N_CHIPS * a + s],
                    device_id=(x, y, 1 - c), device_id_type=MESH))
            continue
        for j, (px, py) in enumerate(_other_chips(x, y)):
            slab = 2 * px + py if landing else mine
            if kind == "gather":
                src, dst = srcs[a].at[c], lands[a].at[slab, c]
            else:
                src, dst = srcs[a].at[2 * px + py], lands[a].at[slab]
            copies.append(pltpu.make_async_remote_copy(src, dst, send_sems.at[3 * a + j], recv_sems.at[3 * a + j],
                                                       device_id=(px, py, c), device_id_type=MESH))
    return copies


def _split_params():
    return pltpu.CompilerParams(has_side_effects=pltpu.SideEffectType.DATAFLOW_SIDE_EFFECTING)


def _ici_start(kind, srcs, land_shapes, after, name):
    n = len(srcs)
    lands = [pltpu.with_memory_space_constraint(lax.empty(s, a.dtype), pltpu.HBM) for s, a in zip(land_shapes, srcs)]

    def body(*refs):
        outs = refs[2 * n + len(after):]
        for cp in _ici_copies(kind, False, refs[:n], refs[n:2 * n], outs[0], outs[1]):
            cp.start()
        outs[-1][...] = jnp.zeros(TOKEN, F32)

    outs = pl.pallas_call(
        body, name=name, in_specs=[HBM] * (2 * n) + [pl.BlockSpec(memory_space=pl.ANY)] * len(after),
        out_specs=[SEM, SEM] + [HBM] * (2 * n) + [pl.BlockSpec(memory_space=pltpu.VMEM)],
        out_shape=[_dma_sems(PER_ARRAY[kind] * n)] * 2 + [pltpu.HBM(a.shape, a.dtype) for a in list(srcs) + lands]
        + [jax.ShapeDtypeStruct(TOKEN, F32)],
        input_output_aliases={i: 2 + i for i in range(2 * n)},
        compiler_params=_split_params(),
    )(*_hbm(*srcs), *lands, *after)
    return outs[:-1], outs[-1]


def _ici_wait(kind, handle, after, name):
    n = (len(handle) - 2) // 2

    def body(*refs):
        for cp in _ici_copies(kind, True, refs[:n], refs[n:2 * n], refs[2 * n], refs[2 * n + 1]):
            cp.wait_send()
            cp.wait_recv()

    outs = pl.pallas_call(
        body, name=name, in_specs=[HBM] * (2 * n) + [SEM, SEM] + [pl.BlockSpec(memory_space=pl.ANY)] * len(after),
        out_specs=[HBM] * (2 * n), out_shape=[pltpu.HBM(a.shape, a.dtype) for a in handle[2:]],
        input_output_aliases={i: i for i in range(2 * n)},
        compiler_params=_split_params(),
    )(*handle[2:], handle[0], handle[1], *after)
    return list(outs[:n]), list(outs[n:])


def _forward_copies(landing, arrs, send_sems, recv_sems):
    x, y, c = _place()
    copies = []
    for a in range(len(arrs)):
        for j, (px, py) in enumerate(_other_chips(x, y)):
            half = 1 - c if landing else c
            copies.append(pltpu.make_async_remote_copy(
                arrs[a].at[2 * px + py, c], arrs[a].at[2 * px + py, half], send_sems.at[3 * a + j],
                recv_sems.at[3 * a + j], device_id=(x, y, 1 - c), device_id_type=MESH))
    return copies


def _gather_wait_forward(handle, groups, after, name):
    n = (len(handle) - 2) // 2
    assert sum(groups) == n

    def body(*refs):
        outs = refs[2 * n + 2 + len(after):]
        lands, sems = outs[n:2 * n], outs[2 * n:-1]
        arrivals = _ici_copies("gather", True, refs[:n], refs[n:2 * n], refs[2 * n], refs[2 * n + 1])
        sends, first = [], 0
        for g, count in enumerate(groups):
            sends += _forward_copies(False, lands[first:first + count], sems[2 * g], sems[2 * g + 1])
            first += count
        for cp, send in zip(arrivals, sends):
            cp.wait_recv()
            send.start()
        for cp in arrivals:
            cp.wait_send()
        outs[-1][...] = jnp.zeros(TOKEN, F32)

    outs = pl.pallas_call(
        body, name=name, in_specs=[HBM] * (2 * n) + [SEM, SEM] + [pl.BlockSpec(memory_space=pl.ANY)] * len(after),
        out_specs=[HBM] * (2 * n) + [SEM] * (2 * len(groups)) + [pl.BlockSpec(memory_space=pltpu.VMEM)],
        out_shape=[pltpu.HBM(a.shape, a.dtype) for a in handle[2:]]
        + [_dma_sems(3 * count) for count in groups for _ in range(2)] + [jax.ShapeDtypeStruct(TOKEN, F32)],
        input_output_aliases={i: i for i in range(2 * n)},
        compiler_params=_split_params(),
    )(*handle[2:], handle[0], handle[1], *after)
    lands, sems, handles, first = outs[n:2 * n], outs[2 * n:-1], [], 0
    for g, count in enumerate(groups):
        handles.append([sems[2 * g], sems[2 * g + 1], *lands[first:first + count]])
        first += count
    return list(outs[:n]), handles, outs[-1]


def _forward_wait(handle, after, name):
    n = len(handle) - 2

    def body(*refs):
        for cp in _forward_copies(True, refs[:n], refs[n], refs[n + 1]):
            cp.wait_send()
            cp.wait_recv()

    return list(pl.pallas_call(
        body, name=name, in_specs=[HBM] * n + [SEM, SEM] + [pl.BlockSpec(memory_space=pl.ANY)] * len(after),
        out_specs=[HBM] * n, out_shape=[pltpu.HBM(a.shape, a.dtype) for a in handle[2:]],
        input_output_aliases={i: i for i in range(n)},
        compiler_params=_split_params(),
    )(*handle[2:], handle[0], handle[1], *after))


def _add_halves(core, grads, recvs, dtypes, name):
    n = len(grads)
    heights = [g.shape[2] for g in grads]

    def body(c_ref, *refs):
        for a in range(n):
            refs[2 * n + a][...] = (refs[2 * a][0] + refs[2 * a + 1][...]).astype(dtypes[a])

    slab = lambda h: pl.BlockSpec((1, h, D), lambda s, c: (s, 0, 0))
    mine = lambda h: pl.BlockSpec((1, 1, h, D), lambda s, c: (s, c[0], 0, 0))
    return pl.pallas_call(
        body, name=name,
        grid_spec=pltpu.PrefetchScalarGridSpec(
            num_scalar_prefetch=1, grid=(N_CHIPS,),
            in_specs=[spec(h) for h in heights for spec in (mine, slab)], out_specs=[slab(h) for h in heights]),
        out_shape=[pltpu.HBM((N_CHIPS, h, D), dt) for h, dt in zip(heights, dtypes)],
        compiler_params=_params(32, dimension_semantics=_seq()),
    )(core, *_hbm(*[a for pair in zip(grads, recvs) for a in pair]))


N_DEVICES = 2 * N_CHIPS
PEER_FLIPS = [(dx, dy, dc) for dx in (0, 1) for dy in (0, 1) for dc in (0, 1)][1:]


def _small_copies(landing, p_ref, out_ref, send_sems, recv_sems):
    x, y, c = _place()
    flip = lambda v, d: 1 - v if d else v
    copies = []
    for k, flips in enumerate(PEER_FLIPS):
        px, py, pc = (flip(v, d) for v, d in zip((x, y, c), flips))
        slab = 4 * px + 2 * py + pc if landing else 4 * x + 2 * y + c
        copies.append(pltpu.make_async_remote_copy(p_ref, out_ref.at[slab], send_sems.at[k], recv_sems.at[k],
                                                   device_id=(px, py, pc), device_id_type=MESH))
    return copies


def _small_wait(handle, after):
    def body(p_ref, land_ref, send_sems, recv_sems, *rest):
        for cp in _small_copies(True, p_ref, land_ref, send_sems, recv_sems):
            cp.wait_send()
            cp.wait_recv()

    return pl.pallas_call(
        body, name="small_exchange_wait", in_specs=[HBM, HBM, SEM, SEM] + [pl.BlockSpec(memory_space=pl.ANY)] * len(after),
        out_specs=[HBM, HBM], out_shape=[pltpu.HBM(a.shape, F32) for a in handle[2:]],
        input_output_aliases={0: 0, 1: 1},
        compiler_params=_split_params(),
    )(handle[2], handle[3], handle[0], handle[1], *after)


def _sum_chips(slots, firsts, rests, after, name):
    n = len(firsts)

    def body(i_ref, *refs):
        outs = refs[4 * n + len(after):]
        for a in range(n):
            first, r1, r2, r3 = refs[4 * a:4 * a + 4]
            outs[a][...] = ((first[...].astype(F32) + r1[...].astype(F32)) + r2[...].astype(F32)) + r3[...].astype(F32)

    slab = lambda h, k: pl.BlockSpec((1, h, D), lambda i, ix: (ix[k], 0, 0))
    heights = [f.shape[1] for f in firsts]
    return pl.pallas_call(
        body, name=name,
        grid_spec=pltpu.PrefetchScalarGridSpec(
            num_scalar_prefetch=1, grid=(1,),
            in_specs=[slab(h, k) for h in heights for k in range(4)] + [pl.BlockSpec(memory_space=pl.ANY)] * len(after),
            out_specs=[slab(h, 4) for h in heights]),
        out_shape=[pltpu.HBM((2, h, D), F32) for h in heights],
        compiler_params=_params(48, dimension_semantics=_seq()),
    )(slots, *_hbm(*[a for f, r in zip(firsts, rests) for a in (f, r, r, r)]), *after)


def _join_copies(landing, arrs, send_sems, recv_sems):
    x, y, c = _place()
    slab = 1 - c if landing else c
    return [pltpu.make_async_remote_copy(arr.at[slab], arr.at[slab], send_sems.at[a], recv_sems.at[a],
                                         device_id=(x, y, 1 - c), device_id_type=MESH) for a, arr in enumerate(arrs)]


def _join_halves(halves, name):
    n = len(halves)

    def body(*refs):
        outs = refs[n:2 * n]
        send_sems, recv_sems = refs[2 * n:]
        sends = _join_copies(False, outs, send_sems, recv_sems)
        for cp in sends:
            cp.start()
        for cp in _join_copies(True, outs, send_sems, recv_sems):
            cp.wait_recv()
        for cp in sends:
            cp.wait_send()

    return list(pl.pallas_call(
        body, name=name, in_specs=[HBM] * n, out_specs=[HBM] * n,
        out_shape=[pltpu.HBM(h.shape, F32) for h in halves],
        input_output_aliases={a: a for a in range(n)},
        scratch_shapes=[_dma_sems(n)] * 2,
        compiler_params=_comm_params(),
    )(*_hbm(*halves)))


def _join_small_start(halves, pack, name):
    n, peers = len(halves), len(PEER_FLIPS)
    land = pltpu.with_memory_space_constraint(lax.empty((N_DEVICES,) + pack.shape, F32), pltpu.HBM)

    def body(*refs):
        outs = refs[n + 2:]
        for cp in _join_copies(False, refs[:n], outs[0], outs[1]):
            cp.start()
        for cp in _small_copies(False, refs[n], refs[n + 1], outs[2], outs[3]):
            cp.start()
        outs[-1][...] = jnp.zeros(TOKEN, F32)

    outs = pl.pallas_call(
        body, name=name, in_specs=[HBM] * (n + 2),
        out_specs=[SEM] * 4 + [HBM] * (n + 2) + [pl.BlockSpec(memory_space=pltpu.VMEM)],
        out_shape=[_dma_sems(n)] * 2 + [_dma_sems(peers)] * 2
        + [pltpu.HBM(a.shape, a.dtype) for a in list(halves) + [pack, land]] + [jax.ShapeDtypeStruct(TOKEN, F32)],
        input_output_aliases={i: 4 + i for i in range(n + 2)},
        compiler_params=_split_params(),
    )(*_hbm(*halves, pack), land)
    return [outs[0], outs[1], *outs[4:4 + n]], [outs[2], outs[3], outs[4 + n], outs[5 + n]], outs[-1]


def _join_wait(handle, after, name):
    n = len(handle) - 2

    def body(*refs):
        for cp in _join_copies(True, refs[:n], refs[n], refs[n + 1]):
            cp.wait_send()
            cp.wait_recv()

    return list(pl.pallas_call(
        body, name=name, in_specs=[HBM] * n + [SEM, SEM] + [pl.BlockSpec(memory_space=pl.ANY)] * len(after),
        out_specs=[HBM] * n, out_shape=[pltpu.HBM(a.shape, a.dtype) for a in handle[2:]],
        input_output_aliases={i: i for i in range(n)},
        compiler_params=_split_params(),
    )(*handle[2:], handle[0], handle[1], *after))


def _chip_partials(grads, fetched, wire_dtypes, name):
    core = lax.axis_index("c").astype(jnp.int32).reshape(1)
    return list(_add_halves(core, grads, fetched, wire_dtypes, name))


def _chip_sums(parts, got, after, name):
    x, y, c = _place()
    others = [2 * px + py for px, py in _other_chips(x, y)]
    own_first = jnp.stack([2 * x + y] + others + [c]).astype(jnp.int32)
    return list(_sum_chips(own_first, parts, got, after, name))


ADAMW_STEPS = 8


def _adamw(params, by_row, chip, window_step):
    n = len(params)
    rows, _, cols = by_row[0].shape
    block = lambda shape: pl.BlockSpec((shape[0] // ADAMW_STEPS, shape[1]), lambda i, c: (i, 0))
    assert all(a.shape[0] % (8 * ADAMW_STEPS) == 0 for p in params for a in p)

    def body(c_ref, *refs):
        w_hbm, g_ref, m_hbm, v_hbm = refs[4 * n:4 * n + 4]
        results, (ins_ref, outs_ref, sems) = refs[8 * n + 4:8 * n + 8], refs[8 * n + 8:]
        loads = [pltpu.make_async_copy(src.at[:, 0, :], ins_ref.at[k], sems.at[k])
                 for k, src in enumerate((w_hbm, m_hbm, v_hbm))]
        stores = [pltpu.make_async_copy(outs_ref.at[k], dst.at[:, 0, :], sems.at[3 + k]) for k, dst in enumerate(results)]
        first = pl.program_id(0) == 0

        @pl.when(first)
        def _():
            for cp in loads:
                cp.start()

        for a in range(n):
            w_ref, a_g_ref, m_ref, v_ref = refs[4 * a:4 * a + 4]
            outs = refs[4 * n + 4 + 4 * a:4 * n + 8 + 4 * a]
            g = a_g_ref[...]
            outs[0][...] = g
            outs[1][...], outs[2][...], outs[3][...] = _adamw_math(w_ref[...], g, m_ref[...], v_ref[...])

        @pl.when(first)
        def _():
            for cp in loads:
                cp.wait()
            for lo in range(0, cols, LANE):
                lanes = slice(lo, lo + LANE)
                g = g_ref[0:rows, lanes]
                for s in range(1, N_CHIPS):
                    g = jnp.where(c_ref[0] == s, g_ref[s * window_step:s * window_step + rows, lanes], g)
                outs_ref[0, :, lanes] = g
                outs_ref[1, :, lanes], outs_ref[2, :, lanes], outs_ref[3, :, lanes] = _adamw_math(
                    ins_ref[0, :, lanes], g, ins_ref[1, :, lanes], ins_ref[2, :, lanes])
            for cp in stores:
                cp.start()

        @pl.when(pl.program_id(0) == ADAMW_STEPS - 1)
        def _():
            for cp in stores:
                cp.wait()

    outs = pl.pallas_call(
        body, name="adamw_matrices",
        grid_spec=pltpu.PrefetchScalarGridSpec(
            num_scalar_prefetch=1, grid=(ADAMW_STEPS,),
            in_specs=[block(a.shape) for p in params for a in p] + [HBM, _const(by_row[1].shape), HBM, HBM],
            out_specs=[block(p[0].shape) for p in params for _ in range(4)] + [HBM] * 4,
            scratch_shapes=[pltpu.VMEM((3, rows, cols), F32), pltpu.VMEM((4, rows, cols), F32), _dma_sems(7)]),
        out_shape=[pltpu.HBM(p[0].shape, F32) for p in params for _ in range(4)] + [pltpu.HBM((rows, 1, cols), F32)] * 4,
        compiler_params=_params(48, dimension_semantics=_seq()),
    )(chip, *_hbm(*[a for p in params for a in p], *by_row))
    return [outs[4 * a:4 * a + 4] for a in range(n)], outs[4 * n:]


def _adamw_math(w, g, m, v):
    nm = ADAM_B1 * m + (1.0 - ADAM_B1) * g
    nv = ADAM_B2 * v + (1.0 - ADAM_B2) * (g * g)
    m_hat = nm / (1.0 - ADAM_B1 ** ADAM_STEP)
    v_hat = nv / (1.0 - ADAM_B2 ** ADAM_STEP)
    return -ADAM_LR * (m_hat / (jnp.sqrt(v_hat) + ADAM_EPS) + ADAM_WD * w), nm, nv


SMALL = (("meta_tokens", (N_META, D // N_CHIPS)), ("ln_in_g", (1, D)), ("ln_in_b", (1, D)), ("b_in", (1, D_IN)),
         ("w_gate_lr2", (GATE_RANK, GLA_HEADS * DK // N_CHIPS)), ("b_gate_lr2", (1, GLA_HEADS * DK)),
         ("attn_sinks", (1, SWA_HEADS)),
         ("gla_norm_g", (1, DV)), ("ln1_g", (1, D)), ("ln1_b", (1, D)), ("ln2_g", (1, D)), ("ln2_b", (1, D)))
ROW_META, ROW_B_IN, ROW_TAIL, ROW_WG2 = 0, 22, 25, 32
ROW_LN = dict(ln_in_g=16, ln_in_b=17, ln1_g=18, ln1_b=19, ln2_g=20, ln2_b=21)
TAIL_BG2, TAIL_SINKS, TAIL_GN, TAIL_LOSS = 0, 256, 256 + SWA_HEADS, 256 + SWA_HEADS + DV


def _adamw_small(place, packs, own, params):
    n = len(SMALL)

    def body(place_ref, packs_ref, own_ref, *refs):
        ins, outs, p_ref = refs[:3 * n], refs[3 * n:-1], refs[-1]
        me, c = place_ref[0], place_ref[1]
        total = jnp.where(me == 0, own_ref[...], packs_ref[0])
        for i in range(1, N_DEVICES):
            total = total + jnp.where(me == i, own_ref[...], packs_ref[i])
        p_ref[...] = total
        outs[4 * n][...] = total[ROW_TAIL:ROW_TAIL + 1, TAIL_LOSS:TAIL_LOSS + 1]

        def mine(width, rows):
            part = lambda s: p_ref[rows, s * width:(s + 1) * width]
            return jnp.where(c == 0, part(0), jnp.where(c == 1, part(1), jnp.where(c == 2, part(2), part(3))))

        tail = lambda lo, width: p_ref[ROW_TAIL:ROW_TAIL + 1, lo:lo + width]
        grads = dict(
            meta_tokens=mine(D // N_CHIPS, slice(ROW_META, ROW_META + N_META)),
            b_in=jnp.concatenate([p_ref[ROW_B_IN:ROW_B_IN + 1, :], p_ref[ROW_B_IN + 1:ROW_B_IN + 2, :],
                                  p_ref[ROW_B_IN + 2:ROW_B_IN + 3, 0:D_IN - 2 * D]], axis=1),
            w_gate_lr2=mine(256 // N_CHIPS, slice(ROW_WG2, ROW_WG2 + 16)),
            b_gate_lr2=tail(TAIL_BG2, 256), attn_sinks=tail(TAIL_SINKS, SWA_HEADS), gla_norm_g=tail(TAIL_GN, DV),
            **{k: p_ref[r:r + 1, :] for k, r in ROW_LN.items()})
        for i, (name, _) in enumerate(SMALL):
            g = grads[name]
            outs[4 * i][...] = g
            outs[4 * i + 1][...], outs[4 * i + 2][...], outs[4 * i + 3][...] = _adamw_math(
                ins[3 * i][...], g, ins[3 * i + 1][...], ins[3 * i + 2][...])

    whole = lambda shape: pl.BlockSpec(shape, lambda i, c: (0,) * len(shape))
    outs = pl.pallas_call(
        body, name="adamw_small",
        grid_spec=pltpu.PrefetchScalarGridSpec(
            num_scalar_prefetch=1, grid=(1,),
            in_specs=[whole(packs.shape), whole(own.shape)] + [whole(s) for _, s in SMALL for _ in range(3)],
            out_specs=[whole(s) for _, s in SMALL for _ in range(4)] + [whole((1, 1))],
            scratch_shapes=[pltpu.VMEM(own.shape, F32)]),
        out_shape=[pltpu.HBM(s, F32) for _, s in SMALL for _ in range(4)] + [pltpu.HBM((1, 1), F32)],
        compiler_params=_params(16, dimension_semantics=_seq()),
    )(place, *_hbm(packs, own, *[a for p in params for a in p]))
    return [outs[4 * i:4 * i + 4] for i in range(n)], outs[4 * n]


def _small_pack(gr):
    names = ["meta_blk"] + list(ROW_LN) + ["b_in_p", "wg2_p", "bg2", "sinks", "gn", "loss"]
    gate_w = GLA_HEADS * DK

    def body(*refs):
        src, out = dict(zip(names, refs)), refs[-1]
        out[...] = jnp.zeros_like(out)
        out[ROW_META:ROW_META + N_META, :] = src["meta_blk"][META_OFF:CH, :]
        for k, r in ROW_LN.items():
            out[r:r + 1, :] = src[k][...]
        for j in range(-(-D_IN // D)):
            width = min(D, D_IN - j * D)
            out[ROW_B_IN + j:ROW_B_IN + j + 1, 0:width] = src["b_in_p"][:, j * D:j * D + width]
        tail = slice(ROW_TAIL, ROW_TAIL + 1)
        out[tail, TAIL_BG2:TAIL_BG2 + gate_w] = src["bg2"][...]
        out[tail, TAIL_SINKS:TAIL_SINKS + SWA_HEADS] = src["sinks"][:, 0:SWA_HEADS]
        out[tail, TAIL_GN:TAIL_GN + DV] = src["gn"][...]
        out[tail, TAIL_LOSS:TAIL_LOSS + 1] = src["loss"][:, 0:1]
        out[ROW_WG2:ROW_WG2 + GATE_RANK, 0:gate_w] = src["wg2_p"][0:GATE_RANK, :]

    arrays = [gr[k] for k in names]
    return pl.pallas_call(
        body, name="small_pack", grid=(1,),
        in_specs=[_acc(a.shape) for a in arrays], out_specs=_acc((SMALL_ROWS, D)),
        out_shape=pltpu.HBM((SMALL_ROWS, D), F32),
        compiler_params=_params(16, dimension_semantics=_seq()),
    )(*_hbm(*arrays))


BIG = ("w_in", "w_out", "w_g", "w_u", "w_d")


def kernel(x, meta_tokens, ln_in_g, ln_in_b, w_in, b_in, w_gate_lr2, b_gate_lr2, attn_sinks, gla_norm_g, w_out, ln1_g, ln1_b, w_ffn_gate, w_ffn_up, w_ffn_down, ln2_g, ln2_b, loss_target, m_meta_tokens, m_ln_in_g, m_ln_in_b, m_w_in, m_b_in, m_w_gate_lr2, m_b_gate_lr2, m_attn_sinks, m_gla_norm_g, m_w_out, m_ln1_g, m_ln1_b, m_w_ffn_gate, m_w_ffn_up, m_w_ffn_down, m_ln2_g, m_ln2_b, v_meta_tokens, v_ln_in_g, v_ln_in_b, v_w_in, v_b_in, v_w_gate_lr2, v_b_gate_lr2, v_attn_sinks, v_gla_norm_g, v_w_out, v_ln1_g, v_ln1_b, v_w_ffn_gate, v_w_ffn_up, v_w_ffn_down, v_ln2_g, v_ln2_b):
    chip = 2 * lax.axis_index("x") + lax.axis_index("y")

    halves = lambda a: a.reshape(2, a.shape[0] // 2, a.shape[1])
    r_in = SHARD_ROWS["w_in"]
    first = [halves(a) for a in (jnp.pad(w_in[0].T.astype(BF16), ((0, W_IN_WIN - r_in), (0, 0))), meta_tokens,
                                 w_gate_lr2[0])]
    rest = [halves(a) for a in (w_out[0].astype(BF16), w_ffn_gate[0].T.astype(BF16), w_ffn_up[0].T.astype(BF16),
                                w_ffn_down[0].astype(BF16))]
    lands = lambda arrs: [(N_CHIPS,) + a.shape for a in arrs]
    first_handle, first_token = _ici_start("gather", first, lands(first), [], "gather_first_start")
    rest_handle, token = _ici_start("gather", rest, lands(rest), [first_token], "gather_rest_start")
    own_slab = lambda got, shards: [lax.dynamic_update_index_in_dim(g, s, chip, axis=0) for g, s in zip(got, shards)]
    fetching = {}

    def fetch_first(after):
        shards, (forwarding,), _ = _gather_wait_forward(first_handle, [len(first)], after, "gather_first_wait")
        g_in, g_meta, g_wg2 = own_slab(_forward_wait(forwarding, [], "gather_first_forward_wait"), shards)
        w_in_windows = g_in.reshape(N_CHIPS, W_IN_WIN, D)
        meta_full = jnp.concatenate([g_meta[s].reshape(N_META, -1) for s in range(N_CHIPS)], axis=1)
        wg2_full = jnp.concatenate([g_wg2[s].reshape(w_gate_lr2.shape[1], -1) for s in range(N_CHIPS)], axis=1)
        return w_in_windows, meta_full, wg2_full

    def fetch_rest(after):
        shards, (w_out_forwarding, fetching["handle"]), forward_token = _gather_wait_forward(
            rest_handle, [1, len(rest) - 1], after, "gather_rest_wait")
        g_out, = own_slab(_forward_wait(w_out_forwarding, [], "gather_w_out_forward_wait"), shards[:1])
        fetching["shards"] = shards[1:]
        return g_out.reshape(-1, D), forward_token

    def fetch_ffn(after):
        got = _forward_wait(fetching["handle"], after, "gather_ffn_forward_wait")
        return [g.reshape(-1, D) for g in own_slab(got, fetching["shards"])]

    sent = {}
    split = lambda grads: [g.reshape(N_CHIPS, 2, -1, D) for g in grads]

    def exchange(key, grads):
        grads = split(grads)
        sent[key + "_halves"], exchange_token = _ici_start(
            "sibling", grads, [(N_CHIPS,) + a.shape[2:] for a in grads], [], "sibling_" + key + "_start")
        return exchange_token

    def ship(key, after):
        grads, fetched = _ici_wait("sibling", sent[key + "_halves"], after, "sibling_" + key + "_wait")
        parts = _chip_partials(grads, fetched, [BF16] * len(grads), "add_halves_" + key)
        sent[key], ship_token = _ici_start("scatter", parts, [p.shape for p in parts], [], "scatter_" + key + "_start")
        return ship_token

    dx, gr = _local_step(
        x[0], loss_target[0], ln_in_g, ln_in_b, b_in[0], b_gate_lr2[0], attn_sinks[0], gla_norm_g[0], ln1_g[0],
        ln1_b[0], ln2_g[0], ln2_b[0], token, fetch_first, fetch_rest, fetch_ffn,
        lambda g: exchange("ffn", [g[k] for k in BIG[1:]]), lambda after: ship("ffn", after),
        lambda g: exchange("w_in", [g]), lambda after: ship("w_in", after))
    ffn_parts, ffn_got = _ici_wait("scatter", sent["ffn"], [dx], "scatter_ffn_wait")
    join_handle, small_handle, token = _join_small_start(
        _chip_sums(ffn_parts, ffn_got, [], "sum_chips_ffn"), _small_pack(gr), "join_ffn_small_start")
    w_in_parts, w_in_got = _ici_wait("scatter", sent["w_in"], [token], "scatter_w_in_wait")
    w_in_joined = _join_halves(_chip_sums(w_in_parts, w_in_got, [], "sum_chips_w_in"), "join_w_in")
    red = [f.reshape(2 * f.shape[1], D) for f in w_in_joined + _join_wait(join_handle, w_in_joined, "join_ffn_wait")]

    big_g = dict(zip(BIG, red))
    weights = dict(meta_tokens=meta_tokens, ln_in_g=ln_in_g, ln_in_b=ln_in_b, w_in=w_in, b_in=b_in,
                   w_gate_lr2=w_gate_lr2, b_gate_lr2=b_gate_lr2, attn_sinks=attn_sinks, gla_norm_g=gla_norm_g,
                   w_out=w_out, ln1_g=ln1_g, ln1_b=ln1_b, w_ffn_gate=w_ffn_gate, w_ffn_up=w_ffn_up,
                   w_ffn_down=w_ffn_down, ln2_g=ln2_g, ln2_b=ln2_b)
    m_in = dict(meta_tokens=m_meta_tokens, ln_in_g=m_ln_in_g, ln_in_b=m_ln_in_b, w_in=m_w_in, b_in=m_b_in,
                w_gate_lr2=m_w_gate_lr2, b_gate_lr2=m_b_gate_lr2, attn_sinks=m_attn_sinks, gla_norm_g=m_gla_norm_g,
                w_out=m_w_out, ln1_g=m_ln1_g, ln1_b=m_ln1_b, w_ffn_gate=m_w_ffn_gate, w_ffn_up=m_w_ffn_up,
                w_ffn_down=m_w_ffn_down, ln2_g=m_ln2_g, ln2_b=m_ln2_b)
    v_in = dict(meta_tokens=v_meta_tokens, ln_in_g=v_ln_in_g, ln_in_b=v_ln_in_b, w_in=v_w_in, b_in=v_b_in,
                w_gate_lr2=v_w_gate_lr2, b_gate_lr2=v_b_gate_lr2, attn_sinks=v_attn_sinks, gla_norm_g=v_gla_norm_g,
                w_out=v_w_out, ln1_g=v_ln1_g, ln1_b=v_ln1_b, w_ffn_gate=v_w_ffn_gate, w_ffn_up=v_w_ffn_up,
                w_ffn_down=v_w_ffn_down, ln2_g=v_ln2_g, ln2_b=v_ln2_b)
    names = list(weights)
    big_names = ("w_in", "w_out", "w_ffn_gate", "w_ffn_up", "w_ffn_down")

    grads, delta, new_m, new_v = {}, {}, {}, {}
    flips = [(lambda a: a.T) if kk in ("w_g", "w_u") else (lambda a: a) for kk in BIG[1:]]
    by_row = lambda a: jnp.transpose(a, (2, 0, 1))
    updated, updated_w_in = _adamw(
        [(flip(weights[k][0]), big_g[kk], flip(m_in[k][0]), flip(v_in[k][0]))
         for k, kk, flip in zip(big_names[1:], BIG[1:], flips)],
        (by_row(w_in), big_g["w_in"], by_row(m_w_in), by_row(v_w_in)), chip.astype(jnp.int32).reshape(1), r_in % BF16_ROWS)
    for k, flip, results in zip(big_names[1:], flips, updated):
        grads[k], delta[k], new_m[k], new_v[k] = (flip(t)[None] for t in results)
    grads["w_in"], delta["w_in"], new_m["w_in"], new_v["w_in"] = (jnp.transpose(t, (1, 2, 0)) for t in updated_w_in)
    small_in = [tuple(src[k].reshape(shape) for src in (weights, m_in, v_in)) for k, shape in SMALL]
    place = jnp.stack([2 * chip + lax.axis_index("c"), chip]).astype(jnp.int32)
    small_own, small_all = _small_wait(small_handle, [updated[0][0]])
    small_out, loss = _adamw_small(place, small_all, small_own, small_in)
    for (k, _), results in zip(SMALL, small_out):
        grads[k], delta[k], new_m[k], new_v[k] = (r.reshape(weights[k].shape) for r in results)

    return (loss.reshape(()), dx[None], *[grads[k] for k in names], *[delta[k] for k in names], *[new_m[k] for k in names],
            *[new_v[k] for k in names])
```
